```python
import math, functools
import jax, jax.numpy as jnp
from jax import lax
import numpy as np

D_MODEL = 1024
BATCH = 8
SEQ = 4096
DEPTH = 1

CTX_LEN = 256
GRID_W = 64
D_MIX = 2 * D_MODEL
HG_WIDTH = D_MODEL // 1
HG_HEADS = 8
HG_F = 128
HG_V = HG_WIDTH // HG_HEADS
HG_F_WIDTH = HG_HEADS * HG_F
HG_CHUNK = 64
SSD_WIDTH = D_MIX - HG_WIDTH
SSD_HEADDIM = 64
SSD_HEADS = SSD_WIDTH // SSD_HEADDIM
SSD_GROUPS = 4
SSD_HPG = SSD_HEADS // SSD_GROUPS
SSD_STATE = 128
SSD_CONV = 5
SSD_CHUNK = 128
SSD_CONV_CH = SSD_WIDTH + 2 * SSD_GROUPS * SSD_STATE
D_FF = ((8 * D_MODEL // 3 + 255) // 256) * 256
N_MOD = 6
EPS = 1e-6
IN_SPLITS = (HG_F_WIDTH, 2 * HG_F_WIDTH, HG_WIDTH, HG_WIDTH,
             SSD_WIDTH, SSD_CONV_CH, 2 * SSD_HEADS)
D_IN = sum(IN_SPLITS)

kernel_name = "hgrn2_ssd_parallel_heads_dit_layer"


def split_cols(p, sizes):
    idx = np.cumsum(sizes)[:-1].tolist()
    return jnp.split(p, idx, axis=-1)


def rms_norm(x, w):
    xf = x.astype(jnp.float32)
    y = xf * lax.rsqrt(jnp.mean(xf * xf, axis=-1, keepdims=True) + EPS)
    return (y * w.astype(jnp.float32)).astype(x.dtype)


def modulate(u, shift, scale):
    return u * (1.0 + scale) + shift


def swiglu(u, w_gate, w_up, w_down):
    return (jax.nn.silu(u @ w_gate) * (u @ w_up)) @ w_down


def centred_dwconv(u, w, b):
    pad = SSD_CONV // 2
    length = u.shape[-2]
    up = jnp.pad(u, [(0, 0)] * (u.ndim - 2) + [(pad, pad), (0, 0)])
    return sum(w[k] * up[..., k:k + length, :] for k in range(SSD_CONV)) + b


def lower_tri(n):
    return jnp.tril(jnp.ones((n, n), dtype=bool))


def gla_chunk_scan(q, k, v, log_f, s0):
    bsz, length, heads = q.shape[:3]
    nc = length // HG_CHUNK
    q, k, v, log_f = (t.reshape(bsz, nc, HG_CHUNK, heads, t.shape[-1]) for t in (q, k, v, log_f))
    cum = jnp.cumsum(log_f, axis=2)
    q_dec = q * jnp.exp(cum)
    k_inv = k * jnp.exp(-cum)
    scores = jnp.where(lower_tri(HG_CHUNK), jnp.einsum('bcihf,bcjhf->bchij', q_dec, k_inv), 0.0)
    o_intra = jnp.einsum('bchij,bcjhv->bcihv', scores, v)
    k_end = k * jnp.exp(cum[:, :, -1:] - cum)
    chunk_states = jnp.einsum('bcjhf,bcjhv->bchfv', k_end, v)
    chunk_decay = jnp.exp(cum[:, :, -1]).astype(chunk_states.dtype)

    def step(s, inp):
        dec, ds = inp
        return (dec[..., None] * s + ds).astype(ds.dtype), s

    s_final, s_prev = lax.scan(step, s0.astype(chunk_states.dtype),
                               (jnp.moveaxis(chunk_decay, 1, 0), jnp.moveaxis(chunk_states, 1, 0)))
    s_prev = jnp.moveaxis(s_prev, 0, 1)
    o = o_intra + jnp.einsum('bcihf,bchfv->bcihv', q_dec, s_prev)
    return o.reshape(bsz, length, heads, -1), s_final


def ssd_chunk_scan(x, dt, bm, cm, s0, a):
    bsz, length = x.shape[:2]
    nc = length // SSD_CHUNK
    x, dt, bm, cm = (t.reshape((bsz, nc, SSD_CHUNK) + t.shape[2:]) for t in (x, dt, bm, cm))
    cum = jnp.cumsum(jnp.moveaxis(dt * a, 2, -1), axis=-1)
    seg = cum[..., :, None] - cum[..., None, :]
    decay = jnp.exp(jnp.where(lower_tri(SSD_CHUNK), seg, -jnp.inf))
    dtx = dt[..., None] * x
    scores = jnp.einsum('bcign,bcjgn->bcgij', cm, bm)
    y = jnp.einsum('bcgij,bcghij,bcjghp->bcighp', scores, decay, dtx)
    decay_to_end = jnp.exp(cum[..., -1:] - cum)
    chunk_states = jnp.einsum('bcjgn,bcghj,bcjghp->bcghpn', bm, decay_to_end, dtx)
    chunk_decay = jnp.exp(cum[..., -1]).astype(chunk_states.dtype)

    def step(s, inp):
        dec, ds = inp
        return (dec[..., None, None] * s + ds).astype(ds.dtype), s

    s_final, s_prev = lax.scan(step, s0.astype(chunk_states.dtype),
                               (jnp.moveaxis(chunk_decay, 1, 0), jnp.moveaxis(chunk_states, 1, 0)))
    s_prev = jnp.moveaxis(s_prev, 0, 1)
    y = y + jnp.einsum('bcign,bcghi,bcghpn->bcighp', cm, jnp.exp(cum), s_prev)
    return y.reshape((bsz, length) + y.shape[3:]), s_final


def prefix_scan(scan_fn, ctx_args, lat_args, s0, reverse):
    flip = (lambda t: jnp.flip(t, axis=1)) if reverse else (lambda t: t)
    y_ctx, s_ctx = scan_fn(*[flip(t) for t in ctx_args], s0)
    y_lat, _ = scan_fn(*[flip(t) for t in lat_args], s_ctx)
    return flip(y_ctx), flip(y_lat)


def hgrn2_inputs(q, f_raw, i, lb):
    bsz, length = q.shape[:2]
    q = jax.nn.silu(q).reshape(bsz, length, HG_HEADS, HG_F) * HG_F ** -0.5
    lb = lb.reshape(2, HG_HEADS, HG_F)
    f = lb + (1.0 - lb) * jax.nn.sigmoid(
        f_raw.astype(jnp.float32).reshape(bsz, length, 2, HG_HEADS, HG_F))
    v = i.reshape(bsz, length, HG_HEADS, HG_V)
    return q, 1.0 - f, jnp.log(f), v


def hgrn2_readout(o, g, norm_w):
    bsz, length = o.shape[:2]
    return rms_norm(o, norm_w).reshape(bsz, length, HG_WIDTH) * jax.nn.silu(g)


def hgrn2_mixer(ctx_cols, lat_cols, lb, norm_w, with_ctx_out):
    qc, kc, lfc, vc = hgrn2_inputs(*ctx_cols[:3], lb)
    ql, kl, lfl, vl = hgrn2_inputs(*lat_cols[:3], lb)
    s0 = jnp.zeros((qc.shape[0], HG_HEADS, HG_F, HG_V), jnp.float32)
    outs = [prefix_scan(gla_chunk_scan,
                        (qc, kc[:, :, d], vc, lfc[:, :, d]),
                        (ql, kl[:, :, d], vl, lfl[:, :, d]), s0, d == 1) for d in range(2)]
    y_lat = hgrn2_readout(outs[0][1] + outs[1][1], lat_cols[3], norm_w)
    y_ctx = hgrn2_readout(outs[0][0] + outs[1][0], ctx_cols[3], norm_w) if with_ctx_out else None
    return y_ctx, y_lat


def ssd_inputs(xbc, dt_raw, dt_bias):
    bsz, length = xbc.shape[:2]
    xs, bm, cm = split_cols(xbc, (SSD_WIDTH, SSD_GROUPS * SSD_STATE, SSD_GROUPS * SSD_STATE))
    xs = xs.reshape(bsz, length, SSD_GROUPS, SSD_HPG, SSD_HEADDIM)
    bm = bm.reshape(bsz, length, SSD_GROUPS, SSD_STATE)
    cm = cm.reshape(bsz, length, SSD_GROUPS, SSD_STATE)
    dt = jax.nn.softplus(dt_raw.astype(jnp.float32).reshape(bsz, length, 2, SSD_GROUPS, SSD_HPG)
                         + dt_bias.astype(jnp.float32).reshape(2, SSD_GROUPS, SSD_HPG))
    return xs, dt, bm, cm


def ssd_mixer(ctx_cols, lat_cols, rows, conv_w, conv_b, a_log, dt_bias, d_skip, norm_w, with_ctx_out):
    z_c, xbc_c, dt_c = ctx_cols
    z_l, xbc_l, dt_l = lat_cols
    bsz = xbc_l.shape[0]
    xbc_c = jax.nn.silu(centred_dwconv(xbc_c, conv_w, conv_b))
    xbc_l = jax.nn.silu(centred_dwconv(xbc_l.reshape(bsz, rows, GRID_W, SSD_CONV_CH), conv_w, conv_b)
                        ).reshape(bsz, rows * GRID_W, SSD_CONV_CH)
    xc, dtc, bc, cc = ssd_inputs(xbc_c, dt_c, dt_bias)
    xl, dtl, bl, cl = ssd_inputs(xbc_l, dt_l, dt_bias)
    a = -jnp.exp(a_log.astype(jnp.float32)).reshape(2, SSD_GROUPS, SSD_HPG)
    s0 = jnp.zeros((bsz, SSD_GROUPS, SSD_HPG, SSD_HEADDIM, SSD_STATE), jnp.float32)
    outs = [prefix_scan(functools.partial(ssd_chunk_scan, a=a[d]),
                        (xc, dtc[:, :, d], bc, cc),
                        (xl, dtl[:, :, d], bl, cl), s0, d == 1) for d in range(2)]
    d_h = d_skip.reshape(SSD_GROUPS, SSD_HPG, 1)

    def readout(y, xs, z):
        b, length = z.shape[:2]
        u = (y + d_h * xs).reshape(b, length, SSD_WIDTH) * jax.nn.silu(z)
        u = rms_norm(u.reshape(b, length, SSD_GROUPS, SSD_WIDTH // SSD_GROUPS),
                     norm_w.reshape(SSD_GROUPS, -1))
        return u.reshape(b, length, SSD_WIDTH)

    y_lat = readout(outs[0][1] + outs[1][1], xl, z_l)
    y_ctx = readout(outs[0][0] + outs[1][0], xc, z_c) if with_ctx_out else None
    return y_ctx, y_lat


def token_mixing(u_ctx, u_lat, rows, w_in, conv_w, conv_b, a_log, dt_bias, d_skip, ssd_norm_w,
                 lb, hgrn_norm_w, w_out, with_ctx_out):
    pc = split_cols(u_ctx @ w_in, IN_SPLITS)
    pl = split_cols(u_lat @ w_in, IN_SPLITS)
    hg_c, hg_l = hgrn2_mixer(tuple(pc[:4]), tuple(pl[:4]), lb, hgrn_norm_w, with_ctx_out)
    ss_c, ss_l = ssd_mixer(tuple(pc[4:]), tuple(pl[4:]), rows, conv_w, conv_b, a_log, dt_bias,
                           d_skip, ssd_norm_w, with_ctx_out)
    y_lat = jnp.concatenate([hg_l, ss_l], axis=-1) @ w_out
    y_ctx = jnp.concatenate([hg_c, ss_c], axis=-1) @ w_out if with_ctx_out else None
    return y_ctx, y_lat


def _fwd_setup_inputs(seed: int = 0) -> dict:
    key = jax.random.key(seed)
    ks = jax.random.split(key, 24)
    nrm = lambda k, shape, s: jax.random.normal(k, shape, jnp.float32) * s
    gain = lambda k, shape: 1.0 + 0.01 * jax.random.normal(k, shape, jnp.float32)
    dt0 = jnp.exp(jax.random.uniform(ks[10], (DEPTH, 2, SSD_HEADS), jnp.float32)
                  * (math.log(0.1) - math.log(0.001)) + math.log(0.001))
    return {
        "x": nrm(ks[0], (BATCH, SEQ, D_MODEL), 1.0),
        "c": nrm(ks[1], (BATCH, D_MODEL), 1.0),
        "ctx": nrm(ks[2], (BATCH, CTX_LEN, D_MODEL), 1.0),
        "c_ctx": nrm(ks[3], (D_MODEL,), 1.0),
        "w_ada": nrm(ks[4], (DEPTH, D_MODEL, N_MOD * D_MODEL), D_MODEL ** -0.5),
        "b_ada": nrm(ks[5], (DEPTH, N_MOD * D_MODEL), 0.01),
        "norm_mix": gain(ks[6], (DEPTH, D_MODEL)),
        "w_in": nrm(ks[7], (DEPTH, D_MODEL, D_IN), D_MODEL ** -0.5),
        "conv_w": nrm(ks[8], (DEPTH, SSD_CONV, SSD_CONV_CH), SSD_CONV ** -0.5),
        "conv_b": nrm(ks[9], (DEPTH, SSD_CONV_CH), 0.01),
        "ssd_a_log": jnp.log(jax.random.uniform(ks[11], (DEPTH, 2, SSD_HEADS), jnp.float32, 1.0, 16.0)),
        "ssd_dt_bias": dt0 + jnp.log(-jnp.expm1(-dt0)),
        "ssd_d": gain(ks[12], (DEPTH, SSD_HEADS)),
        "ssd_norm": gain(ks[13], (DEPTH, SSD_WIDTH)),
        "hgrn_lb_raw": nrm(ks[14], (DEPTH + 1, 2, HG_F_WIDTH), 0.1),
        "hgrn_norm": gain(ks[15], (DEPTH, HG_V)),
        "w_out": nrm(ks[16], (DEPTH, D_MIX, D_MODEL), D_MIX ** -0.5),
        "norm_ffn": gain(ks[17], (DEPTH, D_MODEL)),
        "w_gate": nrm(ks[18], (DEPTH, D_MODEL, D_FF), D_MODEL ** -0.5),
        "w_up": nrm(ks[19], (DEPTH, D_MODEL, D_FF), D_MODEL ** -0.5),
        "w_down": nrm(ks[20], (DEPTH, D_FF, D_MODEL), D_FF ** -0.5),
        "final_norm": gain(ks[21], (D_MODEL,)),
    }


def _fwd_reference(x, c, ctx, c_ctx, w_ada, b_ada, norm_mix, w_in, conv_w, conv_b, ssd_a_log,
              ssd_dt_bias, ssd_d, ssd_norm, hgrn_lb_raw, hgrn_norm, w_out, norm_ffn,
              w_gate, w_up, w_down, final_norm):
    rows = x.shape[1] // GRID_W
    lbs = jnp.cumsum(jax.nn.softmax(hgrn_lb_raw.astype(jnp.float32), axis=0), axis=0)
    c_act = jax.nn.silu(c)
    cc_act = jax.nn.silu(c_ctx)
    h, hc = x, ctx
    for layer in range(DEPTH):
        with_ctx_out = layer < DEPTH - 1
        mod_lat = (c_act @ w_ada[layer] + b_ada[layer])[:, None, :]
        mod_ctx = cc_act @ w_ada[layer] + b_ada[layer]
        sh1, sc1, g1, sh2, sc2, g2 = jnp.split(mod_lat, N_MOD, axis=-1)
        csh1, csc1, cg1, csh2, csc2, cg2 = jnp.split(mod_ctx, N_MOD, axis=-1)
        u_lat = modulate(rms_norm(h, norm_mix[layer]), sh1, sc1)
        u_ctx = modulate(rms_norm(hc, norm_mix[layer]), csh1, csc1)
        y_ctx, y_lat = token_mixing(u_ctx, u_lat, rows, w_in[layer], conv_w[layer], conv_b[layer],
                                    ssd_a_log[layer], ssd_dt_bias[layer], ssd_d[layer],
                                    ssd_norm[layer], lbs[layer], hgrn_norm[layer], w_out[layer],
                                    with_ctx_out)
        h = (h + g1 * y_lat).astype(x.dtype)
        u = modulate(rms_norm(h, norm_ffn[layer]), sh2, sc2)
        h = (h + g2 * swiglu(u, w_gate[layer], w_up[layer], w_down[layer])).astype(x.dtype)
        if with_ctx_out:
            hc = (hc + cg1 * y_ctx).astype(ctx.dtype)
            uc = modulate(rms_norm(hc, norm_ffn[layer]), csh2, csc2)
            hc = (hc + cg2 * swiglu(uc, w_gate[layer], w_up[layer], w_down[layer])).astype(ctx.dtype)
    return rms_norm(h, final_norm)


import jax as _jax
import jax.numpy as _jnp

TWIN_FORMAT = 'train_step'
FWD_PARAMS = ['x', 'c', 'ctx', 'c_ctx', 'w_ada', 'b_ada', 'norm_mix', 'w_in', 'conv_w', 'conv_b', 'ssd_a_log', 'ssd_dt_bias', 'ssd_d', 'ssd_norm', 'hgrn_lb_raw', 'hgrn_norm', 'w_out', 'norm_ffn', 'w_gate', 'w_up', 'w_down', 'final_norm']
TWIN_WEIGHTS = ['c_ctx', 'w_ada', 'b_ada', 'norm_mix', 'w_in', 'conv_w', 'conv_b', 'ssd_a_log', 'ssd_dt_bias', 'ssd_d', 'ssd_norm', 'hgrn_lb_raw', 'hgrn_norm', 'w_out', 'norm_ffn', 'w_gate', 'w_up', 'w_down', 'final_norm']
TWIN_DIFF_INPUT = 'x'
TWIN_INPUTS = ['x', 'c', 'ctx', 'c_ctx', 'w_ada', 'b_ada', 'norm_mix', 'w_in', 'conv_w', 'conv_b', 'ssd_a_log', 'ssd_dt_bias', 'ssd_d', 'ssd_norm', 'hgrn_lb_raw', 'hgrn_norm', 'w_out', 'norm_ffn', 'w_gate', 'w_up', 'w_down', 'final_norm', 'loss_target', 'm_c_ctx', 'm_w_ada', 'm_b_ada', 'm_norm_mix', 'm_w_in', 'm_conv_w', 'm_conv_b', 'm_ssd_a_log', 'm_ssd_dt_bias', 'm_ssd_d', 'm_ssd_norm', 'm_hgrn_lb_raw', 'm_hgrn_norm', 'm_w_out', 'm_norm_ffn', 'm_w_gate', 'm_w_up', 'm_w_down', 'm_final_norm', 'v_c_ctx', 'v_w_ada', 'v_b_ada', 'v_norm_mix', 'v_w_in', 'v_conv_w', 'v_conv_b', 'v_ssd_a_log', 'v_ssd_dt_bias', 'v_ssd_d', 'v_ssd_norm', 'v_hgrn_lb_raw', 'v_hgrn_norm', 'v_w_out', 'v_norm_ffn', 'v_w_gate', 'v_w_up', 'v_w_down', 'v_final_norm']
TWIN_OUTPUTS = ['loss', 'grad_x', 'grad_c_ctx', 'grad_w_ada', 'grad_b_ada', 'grad_norm_mix', 'grad_w_in', 'grad_conv_w', 'grad_conv_b', 'grad_ssd_a_log', 'grad_ssd_dt_bias', 'grad_ssd_d', 'grad_ssd_norm', 'grad_hgrn_lb_raw', 'grad_hgrn_norm', 'grad_w_out', 'grad_norm_ffn', 'grad_w_gate', 'grad_w_up', 'grad_w_down', 'grad_final_norm', 'delta_c_ctx', 'delta_w_ada', 'delta_b_ada', 'delta_norm_mix', 'delta_w_in', 'delta_conv_w', 'delta_conv_b', 'delta_ssd_a_log', 'delta_ssd_dt_bias', 'delta_ssd_d', 'delta_ssd_norm', 'delta_hgrn_lb_raw', 'delta_hgrn_norm', 'delta_w_out', 'delta_norm_ffn', 'delta_w_gate', 'delta_w_up', 'delta_w_down', 'delta_final_norm', 'new_m_c_ctx', 'new_m_w_ada', 'new_m_b_ada', 'new_m_norm_mix', 'new_m_w_in', 'new_m_conv_w', 'new_m_conv_b', 'new_m_ssd_a_log', 'new_m_ssd_dt_bias', 'new_m_ssd_d', 'new_m_ssd_norm', 'new_m_hgrn_lb_raw', 'new_m_hgrn_norm', 'new_m_w_out', 'new_m_norm_ffn', 'new_m_w_gate', 'new_m_w_up', 'new_m_w_down', 'new_m_final_norm', 'new_v_c_ctx', 'new_v_w_ada', 'new_v_b_ada', 'new_v_norm_mix', 'new_v_w_in', 'new_v_conv_w', 'new_v_conv_b', 'new_v_ssd_a_log', 'new_v_ssd_dt_bias', 'new_v_ssd_d', 'new_v_ssd_norm', 'new_v_hgrn_lb_raw', 'new_v_hgrn_norm', 'new_v_w_out', 'new_v_norm_ffn', 'new_v_w_gate', 'new_v_w_up', 'new_v_w_down', 'new_v_final_norm']
TWIN_LEAF_KINDS = {'loss': 'loss', 'grad_x': 'grad_x', 'grad_c_ctx': 'grad_w', 'grad_w_ada': 'grad_w', 'grad_b_ada': 'grad_w', 'grad_norm_mix': 'grad_w', 'grad_w_in': 'grad_w', 'grad_conv_w': 'grad_w', 'grad_conv_b': 'grad_w', 'grad_ssd_a_log': 'grad_w', 'grad_ssd_dt_bias': 'grad_w', 'grad_ssd_d': 'grad_w', 'grad_ssd_norm': 'grad_w', 'grad_hgrn_lb_raw': 'grad_w', 'grad_hgrn_norm': 'grad_w', 'grad_w_out': 'grad_w', 'grad_norm_ffn': 'grad_w', 'grad_w_gate': 'grad_w', 'grad_w_up': 'grad_w', 'grad_w_down': 'grad_w', 'grad_final_norm': 'grad_w', 'delta_c_ctx': 'delta_w', 'delta_w_ada': 'delta_w', 'delta_b_ada': 'delta_w', 'delta_norm_mix': 'delta_w', 'delta_w_in': 'delta_w', 'delta_conv_w': 'delta_w', 'delta_conv_b': 'delta_w', 'delta_ssd_a_log': 'delta_w', 'delta_ssd_dt_bias': 'delta_w', 'delta_ssd_d': 'delta_w', 'delta_ssd_norm': 'delta_w', 'delta_hgrn_lb_raw': 'delta_w', 'delta_hgrn_norm': 'delta_w', 'delta_w_out': 'delta_w', 'delta_norm_ffn': 'delta_w', 'delta_w_gate': 'delta_w', 'delta_w_up': 'delta_w', 'delta_w_down': 'delta_w', 'delta_final_norm': 'delta_w', 'new_m_c_ctx': 'new_m', 'new_m_w_ada': 'new_m', 'new_m_b_ada': 'new_m', 'new_m_norm_mix': 'new_m', 'new_m_w_in': 'new_m', 'new_m_conv_w': 'new_m', 'new_m_conv_b': 'new_m', 'new_m_ssd_a_log': 'new_m', 'new_m_ssd_dt_bias': 'new_m', 'new_m_ssd_d': 'new_m', 'new_m_ssd_norm': 'new_m', 'new_m_hgrn_lb_raw': 'new_m', 'new_m_hgrn_norm': 'new_m', 'new_m_w_out': 'new_m', 'new_m_norm_ffn': 'new_m', 'new_m_w_gate': 'new_m', 'new_m_w_up': 'new_m', 'new_m_w_down': 'new_m', 'new_m_final_norm': 'new_m', 'new_v_c_ctx': 'new_v', 'new_v_w_ada': 'new_v', 'new_v_b_ada': 'new_v', 'new_v_norm_mix': 'new_v', 'new_v_w_in': 'new_v', 'new_v_conv_w': 'new_v', 'new_v_conv_b': 'new_v', 'new_v_ssd_a_log': 'new_v', 'new_v_ssd_dt_bias': 'new_v', 'new_v_ssd_d': 'new_v', 'new_v_ssd_norm': 'new_v', 'new_v_hgrn_lb_raw': 'new_v', 'new_v_hgrn_norm': 'new_v', 'new_v_w_out': 'new_v', 'new_v_norm_ffn': 'new_v', 'new_v_w_gate': 'new_v', 'new_v_w_up': 'new_v', 'new_v_w_down': 'new_v', 'new_v_final_norm': 'new_v'}


def _forward(args):
    return _fwd_reference(*[args[k] for k in FWD_PARAMS])


def _output_shape():
    out = _jax.eval_shape(lambda: _forward(_fwd_setup_inputs(0)))
    return out.shape, out.dtype

N_MICROBATCH = 1
ADAM_LR = 0.001
ADAM_B1 = 0.9
ADAM_B2 = 0.999
ADAM_EPS = 1e-08
ADAM_WD = 0.01
ADAM_STEP = 10
PER_EXAMPLE_BATCH_AXIS = {'x': 0, 'c': 0, 'ctx': 0, 'loss_target': 0}
SHARED_INPUTS = []
_WEIGHT_DTYPES = {'c_ctx': _jnp.float32, 'w_ada': _jnp.float32, 'b_ada': _jnp.float32, 'norm_mix': _jnp.float32, 'w_in': _jnp.float32, 'conv_w': _jnp.float32, 'conv_b': _jnp.float32, 'ssd_a_log': _jnp.float32, 'ssd_dt_bias': _jnp.float32, 'ssd_d': _jnp.float32, 'ssd_norm': _jnp.float32, 'hgrn_lb_raw': _jnp.float32, 'hgrn_norm': _jnp.float32, 'w_out': _jnp.float32, 'norm_ffn': _jnp.float32, 'w_gate': _jnp.float32, 'w_up': _jnp.float32, 'w_down': _jnp.float32, 'final_norm': _jnp.float32}
MOMENT_SCALE = {'c_ctx': 6.588187e-03, 'w_ada': 7.751187e-02, 'b_ada': 1.264685e-01, 'norm_mix': 1.182915e-01, 'w_in': 4.973944e-02, 'conv_w': 5.985594e-02, 'conv_b': 5.883965e-02, 'ssd_a_log': 2.732873e-01, 'ssd_dt_bias': 1.263380e-01, 'ssd_d': 1.812722e-01, 'ssd_norm': 7.748544e-02, 'hgrn_lb_raw': 1.900684e-03, 'hgrn_norm': 1.825556e-01, 'w_out': 9.350231e-02, 'norm_ffn': 1.264564e-01, 'w_gate': 5.731682e-02, 'w_up': 5.532591e-02, 'w_down': 9.170510e-02, 'final_norm': 3.299476e+01}


def _to_microbatches(a, axis):
    t = _jnp.moveaxis(a, axis, 0)
    t = t.reshape((N_MICROBATCH, t.shape[0] // N_MICROBATCH) + t.shape[1:])
    return _jnp.moveaxis(t, 1, axis + 1)


def setup_inputs(seed: int = 0) -> dict:
    inp = _fwd_setup_inputs(seed)
    key = _jax.random.fold_in(_jax.random.key(seed), 7919)
    shape, _ = _output_shape()
    out = dict(inp)
    out["loss_target"] = _jax.random.normal(_jax.random.fold_in(key, 0), shape, _jnp.float32)
    for i, name in enumerate(TWIN_WEIGHTS):
        w = inp[name].astype(_jnp.float32)
        if MOMENT_SCALE is None:
            s = _jnp.sqrt(_jnp.mean(_jnp.square(w)) + 1e-30)
        else:
            s = MOMENT_SCALE[name]
        km, kv = _jax.random.split(_jax.random.fold_in(key, i + 1))
        out[name] = w
        out["m_" + name] = s * _jax.random.normal(km, w.shape, _jnp.float32)
        out["v_" + name] = (s * s) * _jax.random.uniform(kv, w.shape, _jnp.float32, 0.5, 1.5)
    if N_MICROBATCH > 1:
        for name, axis in PER_EXAMPLE_BATCH_AXIS.items():
            out[name] = _to_microbatches(out[name], axis)
    return {'x': out['x'], 'c': out['c'], 'ctx': out['ctx'], 'c_ctx': out['c_ctx'], 'w_ada': out['w_ada'], 'b_ada': out['b_ada'], 'norm_mix': out['norm_mix'], 'w_in': out['w_in'], 'conv_w': out['conv_w'], 'conv_b': out['conv_b'], 'ssd_a_log': out['ssd_a_log'], 'ssd_dt_bias': out['ssd_dt_bias'], 'ssd_d': out['ssd_d'], 'ssd_norm': out['ssd_norm'], 'hgrn_lb_raw': out['hgrn_lb_raw'], 'hgrn_norm': out['hgrn_norm'], 'w_out': out['w_out'], 'norm_ffn': out['norm_ffn'], 'w_gate': out['w_gate'], 'w_up': out['w_up'], 'w_down': out['w_down'], 'final_norm': out['final_norm'], 'loss_target': out['loss_target'], 'm_c_ctx': out['m_c_ctx'], 'm_w_ada': out['m_w_ada'], 'm_b_ada': out['m_b_ada'], 'm_norm_mix': out['m_norm_mix'], 'm_w_in': out['m_w_in'], 'm_conv_w': out['m_conv_w'], 'm_conv_b': out['m_conv_b'], 'm_ssd_a_log': out['m_ssd_a_log'], 'm_ssd_dt_bias': out['m_ssd_dt_bias'], 'm_ssd_d': out['m_ssd_d'], 'm_ssd_norm': out['m_ssd_norm'], 'm_hgrn_lb_raw': out['m_hgrn_lb_raw'], 'm_hgrn_norm': out['m_hgrn_norm'], 'm_w_out': out['m_w_out'], 'm_norm_ffn': out['m_norm_ffn'], 'm_w_gate': out['m_w_gate'], 'm_w_up': out['m_w_up'], 'm_w_down': out['m_w_down'], 'm_final_norm': out['m_final_norm'], 'v_c_ctx': out['v_c_ctx'], 'v_w_ada': out['v_w_ada'], 'v_b_ada': out['v_b_ada'], 'v_norm_mix': out['v_norm_mix'], 'v_w_in': out['v_w_in'], 'v_conv_w': out['v_conv_w'], 'v_conv_b': out['v_conv_b'], 'v_ssd_a_log': out['v_ssd_a_log'], 'v_ssd_dt_bias': out['v_ssd_dt_bias'], 'v_ssd_d': out['v_ssd_d'], 'v_ssd_norm': out['v_ssd_norm'], 'v_hgrn_lb_raw': out['v_hgrn_lb_raw'], 'v_hgrn_norm': out['v_hgrn_norm'], 'v_w_out': out['v_w_out'], 'v_norm_ffn': out['v_norm_ffn'], 'v_w_gate': out['v_w_gate'], 'v_w_up': out['v_w_up'], 'v_w_down': out['v_w_down'], 'v_final_norm': out['v_final_norm']}


def _loss(weights, diff, rest, loss_target):
    with _jax.named_scope("forward"):
        args = {**rest, TWIN_DIFF_INPUT: diff, **{k: w.astype(_WEIGHT_DTYPES[k]) for k, w in weights.items()}}
        y = _forward(args)
    with _jax.named_scope("loss_head"):
        err = _jnp.square(y.astype(_jnp.float32) - loss_target)
        return 0.5 * _jnp.sum(_jnp.mean(err, axis=-1)) if err.ndim else 0.5 * err


def _adamw(w, g, m, v):
    m = ADAM_B1 * m + (1.0 - ADAM_B1) * g
    v = ADAM_B2 * v + (1.0 - ADAM_B2) * _jnp.square(g)
    m_hat = m / (1.0 - ADAM_B1 ** ADAM_STEP)
    v_hat = v / (1.0 - ADAM_B2 ** ADAM_STEP)
    delta = -ADAM_LR * (m_hat / (_jnp.sqrt(v_hat) + ADAM_EPS) + ADAM_WD * w)
    return delta, m, v


def reference(x, c, ctx, c_ctx, w_ada, b_ada, norm_mix, w_in, conv_w, conv_b, ssd_a_log, ssd_dt_bias, ssd_d, ssd_norm, hgrn_lb_raw, hgrn_norm, w_out, norm_ffn, w_gate, w_up, w_down, final_norm, loss_target, m_c_ctx, m_w_ada, m_b_ada, m_norm_mix, m_w_in, m_conv_w, m_conv_b, m_ssd_a_log, m_ssd_dt_bias, m_ssd_d, m_ssd_norm, m_hgrn_lb_raw, m_hgrn_norm, m_w_out, m_norm_ffn, m_w_gate, m_w_up, m_w_down, m_final_norm, v_c_ctx, v_w_ada, v_b_ada, v_norm_mix, v_w_in, v_conv_w, v_conv_b, v_ssd_a_log, v_ssd_dt_bias, v_ssd_d, v_ssd_norm, v_hgrn_lb_raw, v_hgrn_norm, v_w_out, v_norm_ffn, v_w_gate, v_w_up, v_w_down, v_final_norm):
    given = dict(x=x, c=c, ctx=ctx, c_ctx=c_ctx, w_ada=w_ada, b_ada=b_ada, norm_mix=norm_mix, w_in=w_in, conv_w=conv_w, conv_b=conv_b, ssd_a_log=ssd_a_log, ssd_dt_bias=ssd_dt_bias, ssd_d=ssd_d, ssd_norm=ssd_norm, hgrn_lb_raw=hgrn_lb_raw, hgrn_norm=hgrn_norm, w_out=w_out, norm_ffn=norm_ffn, w_gate=w_gate, w_up=w_up, w_down=w_down, final_norm=final_norm, loss_target=loss_target, m_c_ctx=m_c_ctx, m_w_ada=m_w_ada, m_b_ada=m_b_ada, m_norm_mix=m_norm_mix, m_w_in=m_w_in, m_conv_w=m_conv_w, m_conv_b=m_conv_b, m_ssd_a_log=m_ssd_a_log, m_ssd_dt_bias=m_ssd_dt_bias, m_ssd_d=m_ssd_d, m_ssd_norm=m_ssd_norm, m_hgrn_lb_raw=m_hgrn_lb_raw, m_hgrn_norm=m_hgrn_norm, m_w_out=m_w_out, m_norm_ffn=m_norm_ffn, m_w_gate=m_w_gate, m_w_up=m_w_up, m_w_down=m_w_down, m_final_norm=m_final_norm, v_c_ctx=v_c_ctx, v_w_ada=v_w_ada, v_b_ada=v_b_ada, v_norm_mix=v_norm_mix, v_w_in=v_w_in, v_conv_w=v_conv_w, v_conv_b=v_conv_b, v_ssd_a_log=v_ssd_a_log, v_ssd_dt_bias=v_ssd_dt_bias, v_ssd_d=v_ssd_d, v_ssd_norm=v_ssd_norm, v_hgrn_lb_raw=v_hgrn_lb_raw, v_hgrn_norm=v_hgrn_norm, v_w_out=v_w_out, v_norm_ffn=v_norm_ffn, v_w_gate=v_w_gate, v_w_up=v_w_up, v_w_down=v_w_down, v_final_norm=v_final_norm)
    weights = {n: given[n] for n in TWIN_WEIGHTS}
    shared = {n: given[n] for n in SHARED_INPUTS}
    per_example = {n: given[n] for n in ['x', 'c', 'ctx']}
    grad_fn = _jax.value_and_grad(_loss, argnums=(0, 1))

    def one_microbatch(ex, loss_target):
        ex = dict(ex)
        diff = ex.pop(TWIN_DIFF_INPUT)
        return grad_fn(weights, diff, {**shared, **ex}, loss_target)

    if N_MICROBATCH == 1:
        loss, (grad_w, grad_x) = one_microbatch(per_example, given["loss_target"])
    else:
        def body(carry, xs):
            loss_sum, grad_sum = carry
            l_k, (gw_k, gx_k) = one_microbatch(xs[0], xs[1])
            with _jax.named_scope("update"):
                return (loss_sum + l_k, _jax.tree.map(_jnp.add, grad_sum, gw_k)), gx_k

        init = (_jnp.zeros((), _jnp.float32), _jax.tree.map(_jnp.zeros_like, weights))
        (loss, grad_w), grad_x = _jax.lax.scan(body, init, (per_example, given["loss_target"]))
    with _jax.named_scope("update"):
        delta_w, new_m, new_v = {}, {}, {}
        for n in TWIN_WEIGHTS:
            delta_w[n], new_m[n], new_v[n] = _adamw(weights[n], grad_w[n], given["m_" + n], given["v_" + n])
    return (loss, grad_x, *[grad_w[n] for n in TWIN_WEIGHTS], *[delta_w[n] for n in TWIN_WEIGHTS],
            *[new_m[n] for n in TWIN_WEIGHTS], *[new_v[n] for n in TWIN_WEIGHTS])
```

```python
import functools
import math

import jax
import jax.numpy as jnp
from jax import lax
from jax.experimental import pallas as pl
from jax.experimental.pallas import tpu as pltpu

F32 = jnp.float32
BF16 = jnp.bfloat16
MXU_DTYPE = jnp.bfloat16
_INTERPRET = False

D = 1024
NH, HF = 8, 128
HC = 64
SC = 128
SN = 128
SHEADS, SP = 16, 64
GRID_W = 64
KCONV = 5
DFF = 2816
DFF_HALF = DFF // 2
NIN = 8224
TB = 256
EPS = 1e-6
LR, B1, B2, AEPS, WD, STEP = 0.001, 0.9, 0.999, 1e-08, 0.01, 10
MESH_ID = pl.DeviceIdType.MESH
R_IN, R_OUT, R_FF = 2056, 512, 704
BLOB_ROWS = 4704
HALF_ROWS = BLOB_ROWS // 2


def _pcall(body, *, name, out_shape, grid=(), in_specs=None, out_specs=None, scratch=(), sem=None,
           vmem_mb=None, aliases=None):
    params = {}
    if sem is not None:
        params["dimension_semantics"] = sem
    if vmem_mb is not None:
        params["vmem_limit_bytes"] = vmem_mb << 20
    kw = dict(name=name, out_shape=out_shape, scratch_shapes=list(scratch),
              input_output_aliases=aliases or {}, compiler_params=pltpu.CompilerParams(**params),
              interpret=_INTERPRET)
    if grid:
        kw["grid"] = grid
    if in_specs is not None:
        kw["in_specs"] = in_specs
    if out_specs is not None:
        kw["out_specs"] = out_specs
    return pl.pallas_call(body, **kw)


def _mx(a):
    return a.astype(MXU_DTYPE)


def _dg(a, b, ca, cb):
    return lax.dot_general(_mx(a), _mx(b), (((ca,), (cb,)), ((), ())), preferred_element_type=F32)


def _nn(a, b):
    return _dg(a, b, 1, 0)


def _nt(a, b):
    return _dg(a, b, 1, 1)


def _tn(a, b):
    return _dg(a, b, 0, 0)


def _dot01(m, x):
    hi = x.astype(BF16)
    r1 = x - hi.astype(F32)
    mid = r1.astype(BF16)
    lo = (r1 - mid.astype(F32)).astype(BF16)
    f = lambda t: lax.dot_general(m, t, (((1,), (0,)), ((), ())), preferred_element_type=F32)
    return f(hi) + f(mid) + f(lo)


def _tri(n, upper):
    r = lax.broadcasted_iota(jnp.int32, (n, n), 0)
    c = lax.broadcasted_iota(jnp.int32, (n, n), 1)
    return (c >= r) if upper else (c <= r)


def _b01(mask):
    return jnp.where(mask, 1.0, 0.0).astype(BF16)


def _sig(x):
    return jax.nn.sigmoid(x)


def _silu(x):
    return x * _sig(x)


def _dsilu(x):
    s = _sig(x)
    return s * (1.0 + x * (1.0 - s))


def _softplus(x):
    return jnp.maximum(x, 0.0) + jnp.log(1.0 + jnp.exp(-jnp.abs(x)))


def _rowsum(x):
    return jnp.sum(x, axis=1, keepdims=True)


def _colsum(x):
    return jnp.sum(x, axis=0, keepdims=True)


def _full(shape):
    return pl.BlockSpec(shape, lambda *_: (0,) * len(shape))


def _allgather8(v, name):
    m_per, n = v.shape

    def body(x_ref, out_ref, send_sems, recv_sems, local_sem):
        x, y, c = lax.axis_index("x"), lax.axis_index("y"), lax.axis_index("c")
        me, sibling = (x, y, c), (x, y, 1 - c)
        chips = [(1 - x, y), (x, 1 - y), (1 - x, 1 - y)]

        def rows(px, py, pc):
            return out_ref.at[pl.ds((4 * px + 2 * py + pc) * m_per, m_per), :]

        def copy(k, block, to, src=None):
            return pltpu.make_async_remote_copy(
                src_ref=rows(*block) if src is None else src, dst_ref=rows(*block),
                send_sem=send_sems.at[k], recv_sem=recv_sems.at[k], device_id=to, device_id_type=MESH_ID)

        mine = pltpu.make_async_copy(x_ref, rows(*me), local_sem)
        mine.start()
        first = [copy(0, me, sibling, src=x_ref)]
        first += [copy(1 + j, me, (*chip, c), src=x_ref) for j, chip in enumerate(chips)]
        for cp in first:
            cp.start()
        passed = [copy(4 + j, (*chip, c), sibling) for j, chip in enumerate(chips)]
        for j, chip in enumerate(chips):
            copy(1 + j, (*chip, c), me).wait_recv()
            passed[j].start()
        copy(0, sibling, me).wait_recv()
        for j, chip in enumerate(chips):
            copy(4 + j, (*chip, 1 - c), me).wait_recv()
        for cp in first + passed:
            cp.wait_send()
        mine.wait()

    return _pcall(
        body, name=name, out_shape=jax.ShapeDtypeStruct((8 * m_per, n), v.dtype),
        in_specs=[pl.BlockSpec(memory_space=pltpu.VMEM)], out_specs=pl.BlockSpec(memory_space=pltpu.VMEM),
        scratch=[pltpu.SemaphoreType.DMA((7,)), pltpu.SemaphoreType.DMA((7,)), pltpu.SemaphoreType.DMA],
    )(v)


def _weights_allgather(blob):
    rows, n = blob.shape
    half = rows // 2

    def body(b_ref, out_ref, send_sems, recv_sems, local_sem):
        x, y, c = lax.axis_index("x"), lax.axis_index("y"), lax.axis_index("c")
        sibling = (x, y, 1 - c)
        chips = [(1 - x, y), (x, 1 - y), (1 - x, 1 - y)]

        def part(px, py, pc):
            return out_ref.at[2 * px + py, pl.ds(pc * half, half), :]

        def copy(k, block, to, src=None):
            return pltpu.make_async_remote_copy(
                src_ref=part(*block) if src is None else src, dst_ref=part(*block),
                send_sem=send_sems.at[k], recv_sem=recv_sems.at[k], device_id=to, device_id_type=MESH_ID)

        mine = pltpu.make_async_copy(b_ref, out_ref.at[2 * x + y], local_sem)
        mine.start()
        my_half = b_ref.at[pl.ds(c * half, half), :]
        first = [copy(j, (x, y, c), (*chip, c), src=my_half) for j, chip in enumerate(chips)]
        for cp in first:
            cp.start()
        passed = [copy(3 + j, (*chip, c), sibling) for j, chip in enumerate(chips)]
        for j, chip in enumerate(chips):
            copy(j, (*chip, c), (x, y, c)).wait_recv()
            passed[j].start()
        for j, chip in enumerate(chips):
            copy(3 + j, (*chip, 1 - c), (x, y, c)).wait_recv()
        for cp in first + passed:
            cp.wait_send()
        mine.wait()

    return _pcall(
        body, name="weights_allgather", out_shape=jax.ShapeDtypeStruct((4, rows, n), blob.dtype),
        in_specs=[pl.BlockSpec(memory_space=pl.ANY)], out_specs=pl.BlockSpec(memory_space=pl.ANY),
        scratch=[pltpu.SemaphoreType.DMA((6,)), pltpu.SemaphoreType.DMA((6,)), pltpu.SemaphoreType.DMA],
    )(blob)


def _pair_exchange(g):
    _, rows, n = g.shape
    half = rows // 2

    def body(g_ref, out_ref, send_sem, recv_sem):
        x, y, c = lax.axis_index("x"), lax.axis_index("y"), lax.axis_index("c")
        cp = pltpu.make_async_remote_copy(
            src_ref=g_ref.at[:, pl.ds((1 - c) * half, half), :], dst_ref=out_ref,
            send_sem=send_sem, recv_sem=recv_sem, device_id=(x, y, 1 - c), device_id_type=MESH_ID)
        cp.start()
        cp.wait()

    return _pcall(
        body, name="grads_pair_exchange", out_shape=jax.ShapeDtypeStruct((4, half, n), g.dtype),
        in_specs=[pl.BlockSpec(memory_space=pl.ANY)], out_specs=pl.BlockSpec(memory_space=pl.ANY),
        scratch=[pltpu.SemaphoreType.DMA, pltpu.SemaphoreType.DMA],
    )(g)


def _chip_exchange(h):
    _, half, n = h.shape

    def body(h_ref, out_ref, send_sems, recv_sems):
        x, y, c = lax.axis_index("x"), lax.axis_index("y"), lax.axis_index("c")
        chips = [(1 - x, y), (x, 1 - y), (1 - x, 1 - y)]
        cps = [pltpu.make_async_remote_copy(
            src_ref=h_ref.at[2 * px + py], dst_ref=out_ref.at[j], send_sem=send_sems.at[j],
            recv_sem=recv_sems.at[j], device_id=(px, py, c), device_id_type=MESH_ID)
            for j, (px, py) in enumerate(chips)]
        for cp in cps:
            cp.start()
        for cp in cps:
            cp.wait()

    return _pcall(
        body, name="grads_chip_exchange", out_shape=jax.ShapeDtypeStruct((3, half, n), h.dtype),
        in_specs=[pl.BlockSpec(memory_space=pl.ANY)], out_specs=pl.BlockSpec(memory_space=pl.ANY),
        scratch=[pltpu.SemaphoreType.DMA((3,)), pltpu.SemaphoreType.DMA((3,))],
    )(h)


def _pair_allgather(r):
    half, n = r.shape

    def body(r_ref, out_ref, send_sem, recv_sem, local_sem):
        x, y, c = lax.axis_index("x"), lax.axis_index("y"), lax.axis_index("c")
        mine = pltpu.make_async_copy(r_ref, out_ref.at[pl.ds(c * half, half), :], local_sem)
        mine.start()
        cp = pltpu.make_async_remote_copy(
            src_ref=r_ref, dst_ref=out_ref.at[pl.ds(c * half, half), :], send_sem=send_sem,
            recv_sem=recv_sem, device_id=(x, y, 1 - c), device_id_type=MESH_ID)
        cp.start()
        pltpu.make_async_remote_copy(
            src_ref=r_ref, dst_ref=out_ref.at[pl.ds((1 - c) * half, half), :], send_sem=send_sem,
            recv_sem=recv_sem, device_id=(x, y, 1 - c), device_id_type=MESH_ID).wait_recv()
        cp.wait_send()
        mine.wait()

    return _pcall(
        body, name="grads_pair_allgather", out_shape=jax.ShapeDtypeStruct((2 * half, n), r.dtype),
        in_specs=[pl.BlockSpec(memory_space=pl.ANY)], out_specs=pl.BlockSpec(memory_space=pl.ANY),
        scratch=[pltpu.SemaphoreType.DMA, pltpu.SemaphoreType.DMA, pltpu.SemaphoreType.DMA],
    )(r)


def _pair_sum(g, recv, core):
    _, rows, n = g.shape
    half = rows // 2
    rb = half // 7

    def body(c_ref, g_ref, r_ref, o_ref):
        o_ref[...] = (g_ref[...].astype(F32) + r_ref[...].astype(F32)).astype(o_ref.dtype)

    return pl.pallas_call(
        body, name="grads_pair_sum", out_shape=jax.ShapeDtypeStruct((4, half, n), g.dtype),
        grid_spec=pltpu.PrefetchScalarGridSpec(
            num_scalar_prefetch=1, grid=(4, 7),
            in_specs=[pl.BlockSpec((1, rb, n), lambda k, i, cr: (k, cr[0] * 7 + i, 0)),
                      pl.BlockSpec((1, rb, n), lambda k, i, cr: (k, i, 0))],
            out_specs=pl.BlockSpec((1, rb, n), lambda k, i, cr: (k, i, 0))),
        interpret=_INTERPRET,
    )(core, g, recv)


def _chip_sum(h, recv, chip):
    _, half, n = h.shape
    rb = half // 7

    def body(k_ref, h_ref, r_ref, o_ref):
        acc = h_ref[0].astype(F32)
        for j in range(3):
            acc = acc + r_ref[j].astype(F32)
        o_ref[...] = acc

    return pl.pallas_call(
        body, name="grads_chip_sum", out_shape=jax.ShapeDtypeStruct((half, n), F32),
        grid_spec=pltpu.PrefetchScalarGridSpec(
            num_scalar_prefetch=1, grid=(7,),
            in_specs=[pl.BlockSpec((1, rb, n), lambda i, kr: (kr[0], i, 0)),
                      pl.BlockSpec((3, rb, n), lambda i, kr: (0, i, 0))],
            out_specs=pl.BlockSpec((rb, n), lambda i, kr: (i, 0))),
        interpret=_INTERPRET,
    )(chip, h, recv)


def _ada_fwd(araw, w, b):
    nblk = w.shape[1] // 512

    def body(a_ref, w_ref, b_ref, o_ref):
        o_ref[...] = _nn(_silu(a_ref[...]), w_ref[...]) + b_ref[...]

    return _pcall(
        body, name="ada_fwd", out_shape=jax.ShapeDtypeStruct((16, w.shape[1]), F32), grid=(nblk,),
        in_specs=[_full((16, D)), pl.BlockSpec((D, 512), lambda j: (0, j)), pl.BlockSpec((1, 512), lambda j: (0, j))],
        out_specs=pl.BlockSpec((16, 512), lambda j: (0, j)), sem=("parallel",),
    )(araw, w, b)


def _ada_bwd(araw, dmod, w):
    nblk = w.shape[1] // 512

    def body(a_ref, d_ref, w_ref, gw_ref, da_ref):
        j = pl.program_id(0)
        gw_ref[...] = _tn(_silu(a_ref[...]), d_ref[...])
        part = _nt(d_ref[...], w_ref[...])

        @pl.when(j == 0)
        def _():
            da_ref[...] = part

        @pl.when(j > 0)
        def _():
            da_ref[...] += part

    return _pcall(
        body, name="ada_bwd",
        out_shape=(jax.ShapeDtypeStruct(w.shape, F32), jax.ShapeDtypeStruct((16, D), F32)), grid=(nblk,),
        in_specs=[_full((16, D)), pl.BlockSpec((16, 512), lambda j: (0, j)), pl.BlockSpec((D, 512), lambda j: (0, j))],
        out_specs=(pl.BlockSpec((D, 512), lambda j: (0, j)), _full((16, D))), sem=("arbitrary",),
    )(araw, dmod, w)


def _inproj(xin, mods, w_main, w_dt, t_total, tb, blk_off, prev, name):
    n = xin.shape[0]
    nt = n // tb
    ncol = w_main.shape[1] // D

    def body(x_ref, mod_ref, w_ref, wdt_ref, *rest):
        p_ref, pdt_ref, u_ref, uscr = rest[-4:]
        j = pl.program_id(1)

        @pl.when(j == 0)
        def _():
            xv = x_ref[...]
            r = lax.rsqrt(jnp.mean(xv * xv, axis=1, keepdims=True) + EPS)
            u = (xv * r * mod_ref[2:3, :]) * mod_ref[0:1, :] + mod_ref[1:2, :]
            ub = u.astype(MXU_DTYPE)
            uscr[...] = ub
            u_ref[...] = ub
            pdt_ref[...] = _nn(ub, wdt_ref[...])

        p_ref[...] = _nn(uscr[...], w_ref[...])

    in_specs = [pl.BlockSpec((tb, D), lambda i, j: (i, 0)), _full((8, D)),
                pl.BlockSpec((D, D), lambda i, j: (0, j)), _full((D, 128))]
    args = [xin, mods, w_main, w_dt]
    aliases = None
    if prev is not None:
        in_specs += [pl.BlockSpec(memory_space=pl.ANY)] * 3
        args += list(prev)
        aliases = {4: 0, 5: 1, 6: 2}
    return _pcall(
        body, name=name,
        out_shape=(jax.ShapeDtypeStruct((t_total, ncol * D), F32), jax.ShapeDtypeStruct((t_total, 128), F32),
                   jax.ShapeDtypeStruct((t_total, D), MXU_DTYPE)),
        grid=(nt, ncol), in_specs=in_specs,
        out_specs=(pl.BlockSpec((tb, D), lambda i, j: (i + blk_off, j)),
                   pl.BlockSpec((tb, 128), lambda i, j: (i + blk_off, 0)),
                   pl.BlockSpec((tb, D), lambda i, j: (i + blk_off, 0))),
        scratch=[pltpu.VMEM((tb, D), MXU_DTYPE)], sem=("parallel", "arbitrary"), vmem_mb=48, aliases=aliases,
    )(*args)


def _blk(s, nb, rev):
    return jnp.where(s == 0, nb - 1, (nb - 1 - s) if rev else (s - 1))


def _hgrn_gate(fr, lbraw_ref, d):
    lb = _sig(lbraw_ref[d:d + 1, :] - lbraw_ref[2 + d:3 + d, :])
    sg = _sig(fr)
    return lb, sg, lb + (1.0 - lb) * sg


def _hgrn_fwd(p_main, lbraw, d, nb):
    t_total = p_main.shape[0]
    rev = d == 1
    nch = TB // HC
    scale = HF ** -0.5

    def body(q_ref, f_ref, v_ref, lb_ref, o_ref, sp_ref, st):
        s = pl.program_id(0)

        @pl.when(s == 0)
        def _():
            st[...] = jnp.zeros_like(st)

        mb = _tri(HC, rev)
        m01 = _b01(mb)
        for c in (reversed(range(nch)) if rev else range(nch)):
            rows = slice(c * HC, (c + 1) * HC)
            _, _, f = _hgrn_gate(f_ref[rows, :], lb_ref, d)
            lf = jnp.log(f)
            k = 1.0 - f
            cum = _dot01(m01, lf)
            tot = cum[0:1, :] if rev else cum[HC - 1:HC, :]
            qd = _silu(q_ref[rows, :]) * scale * jnp.exp(cum)
            ki = k * jnp.exp(-cum)
            ke = k * jnp.exp(tot - cum)
            etot = jnp.exp(tot)
            v = v_ref[rows, :]
            for h in range(NH):
                cs = slice(h * HF, (h + 1) * HF)
                sth = st[h]
                sp_ref[c, h] = sth.astype(sp_ref.dtype)
                sc = jnp.where(mb, _nt(qd[:, cs], ki[:, cs]), 0.0)
                o_ref[rows, cs] = _nn(sc, v[:, cs]) + _nt(qd[:, cs], sth)
                st[h] = sth * etot[:, cs] + _tn(v[:, cs], ke[:, cs])

    col = lambda j: (lambda s: (_blk(s, nb, rev), j))
    return _pcall(
        body, name=f"hgrn_fwd_{d}",
        out_shape=(jax.ShapeDtypeStruct((t_total, D), F32),
                   jax.ShapeDtypeStruct((nch * nb, NH, HF, HF), MXU_DTYPE)),
        grid=(nb,),
        in_specs=[pl.BlockSpec((TB, D), col(0)), pl.BlockSpec((TB, D), col(1 + d)), pl.BlockSpec((TB, D), col(3)),
                  _full((8, D))],
        out_specs=(pl.BlockSpec((TB, D), col(0)),
                   pl.BlockSpec((nch, NH, HF, HF), lambda s: (_blk(s, nb, rev), 0, 0, 0))),
        scratch=[pltpu.VMEM((NH, HF, HF), F32)], sem=("arbitrary",), vmem_mb=40,
    )(p_main, p_main, p_main, lbraw)


def _hgrn_bwd(p_main, lbraw, sprev, do, d, nb, prev):
    t_total = p_main.shape[0]
    rev = d == 1
    nch = TB // HC
    scale = HF ** -0.5
    last = prev is not None
    odt = MXU_DTYPE if last else F32

    def body(q_ref, f_ref, v_ref, lb_ref, sp_ref, do_ref, *rest):
        if last:
            dqp_ref, dvp_ref = rest[:2]
            rest = rest[2:]
        dq_ref, df_ref, dv_ref, dlb_ref, dst, s_dqd, s_dki, s_dke, s_dv, s_dtot = rest
        sp_id = pl.program_id(0)
        is_ctx = sp_id == nb - 1

        @pl.when(sp_id == 0)
        def _():
            dst[...] = jnp.zeros_like(dst)
            dlb_ref[...] = jnp.zeros_like(dlb_ref)

        mb = _tri(HC, rev)
        mbt = _tri(HC, not rev)
        m01 = _b01(mb)
        mt01 = _b01(mbt)
        for c in (range(nch) if rev else reversed(range(nch))):
            rows = slice(c * HC, (c + 1) * HC)
            fr = f_ref[rows, :]
            lb, sg, f = _hgrn_gate(fr, lb_ref, d)
            lf = jnp.log(f)
            k = 1.0 - f
            cum = _dot01(m01, lf)
            tot = cum[0:1, :] if rev else cum[HC - 1:HC, :]
            e = jnp.exp(cum)
            ei = jnp.exp(-cum)
            ee = jnp.exp(tot - cum)
            qraw = q_ref[rows, :]
            qd = _silu(qraw) * scale * e
            ki = k * ei
            ke = k * ee
            etot = jnp.exp(tot)
            v = v_ref[rows, :]
            dov = jnp.where(is_ctx, 0.0, do_ref[rows, :])
            for h in range(NH):
                cs = slice(h * HF, (h + 1) * HF)
                stin = sp_ref[c, h]
                dso = dst[h]
                qdh, kih, keh, vh, doh = qd[:, cs], ki[:, cs], ke[:, cs], v[:, cs], dov[:, cs]
                pt = jnp.where(mbt, _nt(kih, qdh), 0.0)
                dp = jnp.where(mb, _nt(doh, vh), 0.0)
                dpt = jnp.where(mbt, _nt(vh, doh), 0.0)
                s_dv[:, cs] = _nn(pt, doh) + _nt(keh, dso)
                s_dqd[:, cs] = _nn(dp, kih) + _nn(doh, stin)
                s_dki[:, cs] = _nn(dpt, qdh)
                s_dke[:, cs] = _nn(vh, dso)
                s_dtot[0:1, cs] = _colsum(dso * stin.astype(F32)) * etot[:, cs]
                dst[h] = dso * etot[:, cs] + _tn(doh, qdh)
            dqd, dki, dke = s_dqd[...], s_dki[...], s_dke[...]
            dcum = dqd * qd - dki * ki - dke * ke
            dtot = s_dtot[0:1, :] + _colsum(dke * ke)
            dk = dki * ei + dke * ee
            dlf = _dot01(mt01, dcum) + dtot
            df = dlf / f - dk
            dlb_ref[0:1, :] += _colsum(df * (1.0 - sg))
            dfr = df * (1.0 - lb) * sg * (1.0 - sg)
            dq = dqd * e * scale * _dsilu(qraw)
            dv = s_dv[...]
            if last:
                dq = dq + dqp_ref[rows, :]
                dv = dv + dvp_ref[rows, :]
            dq_ref[rows, :] = dq.astype(odt)
            dv_ref[rows, :] = dv.astype(odt)
            df_ref[rows, :] = dfr.astype(MXU_DTYPE)

    blk = lambda s: _blk(nb - 1 - s, nb, rev)
    col = lambda j: (lambda s: (blk(s), j))
    in_specs = [pl.BlockSpec((TB, D), col(0)), pl.BlockSpec((TB, D), col(1 + d)), pl.BlockSpec((TB, D), col(3)),
                _full((8, D)), pl.BlockSpec((nch, NH, HF, HF), lambda s: (blk(s), 0, 0, 0)),
                pl.BlockSpec((TB, D), lambda s: (jnp.minimum(blk(s), nb - 2), 0))]
    args = [p_main, p_main, p_main, lbraw, sprev, do]
    if last:
        in_specs += [pl.BlockSpec((TB, D), col(0))] * 2
        args += list(prev)
    return _pcall(
        body, name=f"hgrn_bwd_{d}",
        out_shape=(jax.ShapeDtypeStruct((t_total, D), odt), jax.ShapeDtypeStruct((t_total, D), MXU_DTYPE),
                   jax.ShapeDtypeStruct((t_total, D), odt), jax.ShapeDtypeStruct((8, D), F32)),
        grid=(nb,), in_specs=in_specs,
        out_specs=(pl.BlockSpec((TB, D), col(0)), pl.BlockSpec((TB, D), col(0)), pl.BlockSpec((TB, D), col(0)),
                   _full((8, D))),
        scratch=[pltpu.VMEM((NH, HF, HF), F32)] + [pltpu.VMEM((HC, D), F32)] * 4 + [pltpu.VMEM((8, D), F32)],
        sem=("arbitrary",), vmem_mb=48,
    )(*args)


def _conv_masks(tb, is_ctx):
    seg = jnp.where(is_ctx, tb, GRID_W)
    pos = lax.broadcasted_iota(jnp.int32, (tb, 1), 0) & (seg - 1)
    return pos, seg


def _shift_rows(x, dshift, pos, seg):
    if dshift == 0:
        return x
    n = x.shape[0]
    rolled = pltpu.roll(x, (-dshift) % n, 0)
    ok = (pos + dshift >= 0) & (pos + dshift < seg)
    return jnp.where(ok, rolled, 0.0)


def _ssd_prep(p_main, p_dt, convp, dtb, nb):
    t_total = p_main.shape[0]

    def body(x_ref, dt_ref, cw_ref, dtb_ref, xa_ref, dts_ref):
        is_ctx = pl.program_id(0) == nb - 1
        pos, seg = _conv_masks(TB, is_ctx)
        xv = x_ref[...]
        acc = cw_ref[5:6, :] + cw_ref[2:3, :] * xv
        for kk in (0, 1, 3, 4):
            acc = acc + cw_ref[kk:kk + 1, :] * _shift_rows(xv, kk - 2, pos, seg)
        xa_ref[...] = _silu(acc)
        dts_ref[...] = _softplus(dt_ref[...] + dtb_ref[0:1, :])

    return _pcall(
        body, name="ssd_prep",
        out_shape=(jax.ShapeDtypeStruct((t_total, 2048), F32), jax.ShapeDtypeStruct((t_total, 128), F32)),
        grid=(nb,),
        in_specs=[pl.BlockSpec((TB, 2048), lambda i: (i, 3)), pl.BlockSpec((TB, 128), lambda i: (i, 0)),
                  _full((8, 2048)), _full((8, 128))],
        out_specs=(pl.BlockSpec((TB, 2048), lambda i: (i, 0)), pl.BlockSpec((TB, 128), lambda i: (i, 0))),
        sem=("parallel",), vmem_mb=32,
    )(p_main, p_dt, convp, dtb)


def _ssd_prep_bwd(p_main, p_dt, convp, dtb, dxa, dxs_skip, ddts, nb):
    t_total = p_main.shape[0]

    def body(x_ref, dt_ref, cw_ref, dtb_ref, dxa_ref, dsk_ref, ddts_ref, dx_ref, ddt_ref, dcw_ref, ddtb_ref):
        i = pl.program_id(0)
        is_ctx = i == nb - 1

        @pl.when(i == 0)
        def _():
            dcw_ref[...] = jnp.zeros_like(dcw_ref)
            ddtb_ref[...] = jnp.zeros_like(ddtb_ref)

        pos, seg = _conv_masks(TB, is_ctx)
        xv = x_ref[...]
        sh = {kk: _shift_rows(xv, kk - 2, pos, seg) for kk in range(KCONV)}
        acc = cw_ref[5:6, :]
        for kk in range(KCONV):
            acc = acc + cw_ref[kk:kk + 1, :] * sh[kk]
        dact = dxa_ref[...]
        dact = jnp.concatenate([dact[:, :D] + jnp.where(is_ctx, 0.0, dsk_ref[...]), dact[:, D:]], axis=1)
        dpre = dact * _dsilu(acc)
        dxv = cw_ref[2:3, :] * dpre
        for kk in (0, 1, 3, 4):
            dxv = dxv + cw_ref[kk:kk + 1, :] * _shift_rows(dpre, 2 - kk, pos, seg)
        dx_ref[...] = dxv.astype(dx_ref.dtype)
        for kk in range(KCONV):
            dcw_ref[kk:kk + 1, :] += _colsum(dpre * sh[kk])
        dcw_ref[5:6, :] += _colsum(dpre)
        draw = ddts_ref[...] * _sig(dt_ref[...] + dtb_ref[0:1, :])
        ddt_ref[...] = draw.astype(ddt_ref.dtype)
        ddtb_ref[0:1, :] += _colsum(draw)

    return _pcall(
        body, name="ssd_prep_bwd",
        out_shape=(jax.ShapeDtypeStruct((t_total, 2048), MXU_DTYPE), jax.ShapeDtypeStruct((t_total, 128), MXU_DTYPE),
                   jax.ShapeDtypeStruct((8, 2048), F32), jax.ShapeDtypeStruct((8, 128), F32)),
        grid=(nb,),
        in_specs=[pl.BlockSpec((TB, 2048), lambda i: (i, 3)), pl.BlockSpec((TB, 128), lambda i: (i, 0)),
                  _full((8, 2048)), _full((8, 128)), pl.BlockSpec((TB, 2048), lambda i: (i, 0)),
                  pl.BlockSpec((TB, D), lambda i: (jnp.minimum(i, nb - 2), 0)),
                  pl.BlockSpec((TB, 128), lambda i: (i, 0))],
        out_specs=(pl.BlockSpec((TB, 2048), lambda i: (i, 0)), pl.BlockSpec((TB, 128), lambda i: (i, 0)),
                   _full((8, 2048)), _full((8, 128))),
        sem=("arbitrary",), vmem_mb=40,
    )(p_main, p_dt, convp, dtb, dxa, dxs_skip, ddts)


def _lane_pick(x, lane, col):
    return _rowsum(jnp.where(lane == col, x, 0.0))


def _ssd_chunk_common(xa, dts, alog_ref, m01, d, rev):
    lane = lax.broadcasted_iota(jnp.int32, (1, 128), 1)
    arow = -jnp.exp(alog_ref[0:1, :])
    la = dts * arow
    cum = _dot01(m01, la)
    cumt = cum.T
    tot = cum[0:1, :] if rev else cum[SC - 1:SC, :]
    return lane, arow, la, cum, cumt, tot


def _ssd_fwd(xa, dts, alog, d, nb):
    t_total = xa.shape[0]
    rev = d == 1
    nch = TB // SC
    npair = SHEADS // 2

    def body(xa_ref, dts_ref, alog_ref, y_ref, sp_ref, st):
        s = pl.program_id(0)

        @pl.when(s == 0)
        def _():
            st[...] = jnp.zeros_like(st)

        mb = _tri(SC, rev)
        m01 = _b01(mb)
        lo = lax.broadcasted_iota(jnp.int32, (1, 128), 1) < SP
        rlo = lax.broadcasted_iota(jnp.int32, (128, 1), 0) < SP
        for c in (reversed(range(nch)) if rev else range(nch)):
            rows = slice(c * SC, (c + 1) * SC)
            dts_c = dts_ref[rows, :]
            lane, arow, la, cum, cumt, tot = _ssd_chunk_common(None, dts_c, alog_ref, m01, d, rev)
            for g in range(4):
                bg = xa_ref[rows, D + g * SN:D + (g + 1) * SN]
                cg = xa_ref[rows, D + 512 + g * SN:D + 512 + (g + 1) * SN]
                gmat = _nt(cg, bg)
                for pp in range(2):
                    pr = g * 2 + pp
                    xs = xa_ref[rows, pr * 128:(pr + 1) * 128]
                    cols = [16 * d + 2 * pr, 16 * d + 2 * pr + 1]
                    cum_c = [_lane_pick(cum, lane, q) for q in cols]
                    dt_c = [_lane_pick(dts_c, lane, q) for q in cols]
                    tot_c = [_lane_pick(tot, lane, q) for q in cols]
                    dt_pair = jnp.where(lo, dt_c[0], dt_c[1])
                    e1_pair = jnp.where(lo, jnp.exp(cum_c[0]), jnp.exp(cum_c[1]))
                    e2_pair = jnp.where(lo, jnp.exp(tot_c[0] - cum_c[0]), jnp.exp(tot_c[1] - cum_c[1]))
                    etot_col = jnp.where(rlo, jnp.exp(tot_c[0]), jnp.exp(tot_c[1]))
                    dtx = xs * dt_pair
                    stp = st[pr]
                    sp_ref[c, pr] = stp.astype(sp_ref.dtype)
                    y = e1_pair * _nt(cg, stp)
                    for q in range(2):
                        dec = jnp.where(mb, jnp.exp(cum_c[q] - cumt[cols[q]:cols[q] + 1, :]), 0.0)
                        y = y + _nn(gmat * dec, jnp.where(lo if q == 0 else ~lo, dtx, 0.0))
                    y_ref[rows, pr * 128:(pr + 1) * 128] = y
                    st[pr] = stp * etot_col + _tn(dtx * e2_pair, bg)

    blk = lambda s: _blk(s, nb, rev)
    return _pcall(
        body, name=f"ssd_fwd_{d}",
        out_shape=(jax.ShapeDtypeStruct((t_total, D), F32),
                   jax.ShapeDtypeStruct((nch * nb, npair, 128, SN), MXU_DTYPE)),
        grid=(nb,),
        in_specs=[pl.BlockSpec((TB, 2048), lambda s: (blk(s), 0)), pl.BlockSpec((TB, 128), lambda s: (blk(s), 0)),
                  _full((8, 128))],
        out_specs=(pl.BlockSpec((TB, D), lambda s: (blk(s), 0)),
                   pl.BlockSpec((nch, npair, 128, SN), lambda s: (blk(s), 0, 0, 0))),
        scratch=[pltpu.VMEM((npair, 128, SN), F32)], sem=("arbitrary",), vmem_mb=40,
    )(xa, dts, alog)


def _ssd_bwd(xa, dts, alog, sprev, dy, d, nb, prev):
    t_total = xa.shape[0]
    rev = d == 1
    nch = TB // SC
    npair = SHEADS // 2
    last = prev is not None

    def body(xa_ref, dts_ref, alog_ref, sp_ref, dy_ref, *rest):
        if last:
            dxp_ref, ddp_ref = rest[:2]
            rest = rest[2:]
        dxa_ref, ddts_ref, da_ref, dst = rest
        sp_id = pl.program_id(0)
        is_ctx = sp_id == nb - 1

        @pl.when(sp_id == 0)
        def _():
            dst[...] = jnp.zeros_like(dst)
            da_ref[...] = jnp.zeros_like(da_ref)

        mb = _tri(SC, rev)
        m01 = _b01(mb)
        mt01 = _b01(_tri(SC, not rev))
        lo = lax.broadcasted_iota(jnp.int32, (1, 128), 1) < SP
        rlo = lax.broadcasted_iota(jnp.int32, (128, 1), 0) < SP
        for c in (range(nch) if rev else reversed(range(nch))):
            rows = slice(c * SC, (c + 1) * SC)
            dts_c = dts_ref[rows, :]
            lane, arow, la, cum, cumt, tot = _ssd_chunk_common(None, dts_c, alog_ref, m01, d, rev)
            dcum = jnp.zeros((SC, 128), F32)
            ddt = jnp.zeros((SC, 128), F32)
            dtot = jnp.zeros((1, 128), F32)
            for g in range(4):
                bg = xa_ref[rows, D + g * SN:D + (g + 1) * SN]
                cg = xa_ref[rows, D + 512 + g * SN:D + 512 + (g + 1) * SN]
                gmat = _nt(cg, bg)
                dgm = jnp.zeros((SC, SC), F32)
                dbg = jnp.zeros((SC, SN), F32)
                dcg = jnp.zeros((SC, SN), F32)
                for pp in range(2):
                    pr = g * 2 + pp
                    xs = xa_ref[rows, pr * 128:(pr + 1) * 128]
                    dyp = jnp.where(is_ctx, 0.0, dy_ref[rows, pr * 128:(pr + 1) * 128])
                    cols = [16 * d + 2 * pr, 16 * d + 2 * pr + 1]
                    cum_c = [_lane_pick(cum, lane, q) for q in cols]
                    dt_c = [_lane_pick(dts_c, lane, q) for q in cols]
                    tot_c = [_lane_pick(tot, lane, q) for q in cols]
                    e1_c = [jnp.exp(cum_c[q]) for q in range(2)]
                    e2_c = [jnp.exp(tot_c[q] - cum_c[q]) for q in range(2)]
                    etot_c = [jnp.exp(tot_c[q]) for q in range(2)]
                    dt_pair = jnp.where(lo, dt_c[0], dt_c[1])
                    e1_pair = jnp.where(lo, e1_c[0], e1_c[1])
                    e2_pair = jnp.where(lo, e2_c[0], e2_c[1])
                    etot_col = jnp.where(rlo, etot_c[0], etot_c[1])
                    dtx = xs * dt_pair
                    stin = sp_ref[c, pr]
                    dso = dst[pr]
                    xe = dtx * e2_pair
                    dxe = _nt(bg, dso)
                    dbg = dbg + _nn(xe, dso)
                    ddtx = dxe * e2_pair
                    de2 = dxe * dtx
                    y0 = _nt(cg, stin)
                    dy0 = dyp * e1_pair
                    dcg = dcg + _nn(dy0, stin)
                    de1 = dyp * y0
                    rsum = _rowsum(dso * stin.astype(F32))
                    dst[pr] = dso * etot_col + _tn(dy0, cg)
                    for q in range(2):
                        hm = lo if q == 0 else ~lo
                        col = cols[q]
                        dec = jnp.where(mb, jnp.exp(cum_c[q] - cumt[col:col + 1, :]), 0.0)
                        w = gmat * dec
                        dyq = jnp.where(hm, dyp, 0.0)
                        dw = jnp.where(mb, _nt(dyq, dtx), 0.0)
                        ddtx = ddtx + jnp.where(hm, _tn(w, dyq), 0.0)
                        dgm = dgm + dw * dec
                        z = dw * w
                        de1q = _rowsum(jnp.where(hm, de1, 0.0))
                        de2q = _rowsum(jnp.where(hm, de2, 0.0))
                        dcum_q = _rowsum(z) - _rowsum(z.T) + de1q * e1_c[q] - de2q * e2_c[q]
                        rs = rsum[0:SP, :] if q == 0 else rsum[SP:2 * SP, :]
                        dtot_q = _colsum(de2q * e2_c[q]) + _colsum(rs) * etot_c[q]
                        dcum = jnp.where(lane == col, dcum_q, dcum)
                        dtot = jnp.where(lane == col, dtot_q, dtot)
                    dxs = ddtx * dt_pair
                    ddt_pair = ddtx * xs
                    for q in range(2):
                        hm = lo if q == 0 else ~lo
                        ddt = jnp.where(lane == cols[q], _rowsum(jnp.where(hm, ddt_pair, 0.0)), ddt)
                    if last:
                        dxs = dxs + dxp_ref[rows, pr * 128:(pr + 1) * 128]
                    dxa_ref[rows, pr * 128:(pr + 1) * 128] = dxs
                dbg = dbg + _tn(dgm, cg)
                dcg = dcg + _nn(dgm, bg)
                if last:
                    dbg = dbg + dxp_ref[rows, D + g * SN:D + (g + 1) * SN]
                    dcg = dcg + dxp_ref[rows, D + 512 + g * SN:D + 512 + (g + 1) * SN]
                dxa_ref[rows, D + g * SN:D + (g + 1) * SN] = dbg
                dxa_ref[rows, D + 512 + g * SN:D + 512 + (g + 1) * SN] = dcg
            dla = _dot01(mt01, dcum) + dtot
            ddt = ddt + dla * arow
            da_ref[0:1, :] += _colsum(dla * dts_c)
            if last:
                ddt = ddt + ddp_ref[rows, :]
            ddts_ref[rows, :] = ddt

    blk = lambda s: _blk(nb - 1 - s, nb, rev)
    in_specs = [pl.BlockSpec((TB, 2048), lambda s: (blk(s), 0)), pl.BlockSpec((TB, 128), lambda s: (blk(s), 0)),
                _full((8, 128)), pl.BlockSpec((nch, npair, 128, SN), lambda s: (blk(s), 0, 0, 0)),
                pl.BlockSpec((TB, D), lambda s: (jnp.minimum(blk(s), nb - 2), 0))]
    args = [xa, dts, alog, sprev, dy]
    if last:
        in_specs += [pl.BlockSpec((TB, 2048), lambda s: (blk(s), 0)), pl.BlockSpec((TB, 128), lambda s: (blk(s), 0))]
        args += list(prev)
    return _pcall(
        body, name=f"ssd_bwd_{d}",
        out_shape=(jax.ShapeDtypeStruct((t_total, 2048), F32), jax.ShapeDtypeStruct((t_total, 128), F32),
                   jax.ShapeDtypeStruct((8, 128), F32)),
        grid=(nb,), in_specs=in_specs,
        out_specs=(pl.BlockSpec((TB, 2048), lambda s: (blk(s), 0)), pl.BlockSpec((TB, 128), lambda s: (blk(s), 0)),
                   _full((8, 128))),
        scratch=[pltpu.VMEM((npair, 128, SN), F32)], sem=("arbitrary",), vmem_mb=48,
    )(*args)


def _readout(o, g, yy, z, vec_ref):
    hg, ss, keep = [], [], []
    for h in range(NH):
        cs = slice(h * HF, (h + 1) * HF)
        oh = o[:, cs]
        r = lax.rsqrt(jnp.mean(oh * oh, axis=1, keepdims=True) + EPS)
        hg.append(oh * r * vec_ref[0:1, cs] * _silu(g[:, cs]))
        keep.append(r)
    u = yy * _silu(z)
    for gi in range(4):
        cs = slice(gi * 256, (gi + 1) * 256)
        ug = u[:, cs]
        r = lax.rsqrt(jnp.mean(ug * ug, axis=1, keepdims=True) + EPS)
        ss.append(ug * r * vec_ref[2:3, cs])
        keep.append(r)
    return jnp.concatenate(hg, axis=1), jnp.concatenate(ss, axis=1), keep, u


def _mix_out(o_f, o_b, p_main, y_f, y_b, xa, x, vecs, w_out):
    n = x.shape[0]

    def body(of_ref, ob_ref, g_ref, z_ref, yf_ref, yb_ref, xs_ref, x_ref, vec_ref, w_ref,
             ymix_ref, ylat_ref, h1_ref, u2_ref):
        o = of_ref[...] + ob_ref[...]
        yy = yf_ref[...] + yb_ref[...] + vec_ref[1:2, :] * xs_ref[...]
        hg, ss, _, _ = _readout(o, g_ref[...], yy, z_ref[...], vec_ref)
        ymix = jnp.concatenate([hg, ss], axis=1).astype(MXU_DTYPE)
        ymix_ref[...] = ymix
        ylat = _nn(ymix, w_ref[...])
        ylat_ref[...] = ylat
        h1 = x_ref[...] + vec_ref[3:4, :] * ylat
        h1_ref[...] = h1
        r = lax.rsqrt(jnp.mean(h1 * h1, axis=1, keepdims=True) + EPS)
        u2_ref[...] = ((h1 * r * vec_ref[6:7, :]) * vec_ref[4:5, :] + vec_ref[5:6, :]).astype(MXU_DTYPE)

    row = lambda j: (lambda i: (i, j))
    return _pcall(
        body, name="mix_out",
        out_shape=(jax.ShapeDtypeStruct((n, 2 * D), MXU_DTYPE), jax.ShapeDtypeStruct((n, D), F32),
                   jax.ShapeDtypeStruct((n, D), F32), jax.ShapeDtypeStruct((n, D), MXU_DTYPE)),
        grid=(n // TB,),
        in_specs=[pl.BlockSpec((TB, D), row(0)), pl.BlockSpec((TB, D), row(0)), pl.BlockSpec((TB, D), row(4)),
                  pl.BlockSpec((TB, D), row(5)), pl.BlockSpec((TB, D), row(0)), pl.BlockSpec((TB, D), row(0)),
                  pl.BlockSpec((TB, D), row(0)), pl.BlockSpec((TB, D), row(0)), _full((8, D)), _full((2 * D, D))],
        out_specs=(pl.BlockSpec((TB, 2 * D), row(0)), pl.BlockSpec((TB, D), row(0)), pl.BlockSpec((TB, D), row(0)),
                   pl.BlockSpec((TB, D), row(0))),
        sem=("parallel",), vmem_mb=48,
    )(o_f, o_b, p_main, p_main, y_f, y_b, xa, x, vecs, w_out)


def _mix_bwd(dylat, o_f, o_b, p_main, y_f, y_b, xa, vecs, w_out):
    n = dylat.shape[0]

    def body(dyl_ref, of_ref, ob_ref, g_ref, z_ref, yf_ref, yb_ref, xs_ref, vec_ref, w_ref,
             do_ref, dg_ref, dys_ref, dz_ref, dxs_ref, acc_ref):
        i = pl.program_id(0)

        @pl.when(i == 0)
        def _():
            acc_ref[...] = jnp.zeros_like(acc_ref)

        dymix = _nt(dyl_ref[...], w_ref[...])
        o = of_ref[...] + ob_ref[...]
        g = g_ref[...]
        z = z_ref[...]
        xs = xs_ref[...]
        yy = yf_ref[...] + yb_ref[...] + vec_ref[1:2, :] * xs
        _, _, keep, u = _readout(o, g, yy, z, vec_ref)
        do_l, dg_l = [], []
        for h in range(NH):
            cs = slice(h * HF, (h + 1) * HF)
            oh, gh, r, wv = o[:, cs], g[:, cs], keep[h], vec_ref[0:1, cs]
            dhg = dymix[:, cs]
            xh = oh * r
            dn = dhg * _silu(gh)
            dg_l.append(dhg * xh * wv * _dsilu(gh))
            acc_ref[0:1, cs] += _colsum(dn * xh)
            dxh = dn * wv
            do_l.append(r * (dxh - xh * jnp.mean(dxh * xh, axis=1, keepdims=True)))
        du_l = []
        for gi in range(4):
            cs = slice(gi * 256, (gi + 1) * 256)
            ug, r, wv = u[:, cs], keep[NH + gi], vec_ref[2:3, cs]
            dss = dymix[:, D + gi * 256:D + (gi + 1) * 256]
            xh = ug * r
            acc_ref[2:3, cs] += _colsum(dss * xh)
            dxh = dss * wv
            du_l.append(r * (dxh - xh * jnp.mean(dxh * xh, axis=1, keepdims=True)))
        du = jnp.concatenate(du_l, axis=1)
        dyy = du * _silu(z)
        do_ref[...] = jnp.concatenate(do_l, axis=1)
        dg_ref[...] = jnp.concatenate(dg_l, axis=1).astype(dg_ref.dtype)
        dys_ref[...] = dyy
        dz_ref[...] = (du * yy * _dsilu(z)).astype(dz_ref.dtype)
        dxs_ref[...] = dyy * vec_ref[1:2, :]
        acc_ref[1:2, :] += _colsum(dyy * xs)

    row = lambda j: (lambda i: (i, j))
    return _pcall(
        body, name="mix_bwd",
        out_shape=(jax.ShapeDtypeStruct((n, D), F32), jax.ShapeDtypeStruct((n, D), MXU_DTYPE),
                   jax.ShapeDtypeStruct((n, D), F32), jax.ShapeDtypeStruct((n, D), MXU_DTYPE),
                   jax.ShapeDtypeStruct((n, D), F32), jax.ShapeDtypeStruct((8, D), F32)),
        grid=(n // TB,),
        in_specs=[pl.BlockSpec((TB, D), row(0)), pl.BlockSpec((TB, D), row(0)), pl.BlockSpec((TB, D), row(0)),
                  pl.BlockSpec((TB, D), row(4)), pl.BlockSpec((TB, D), row(5)), pl.BlockSpec((TB, D), row(0)),
                  pl.BlockSpec((TB, D), row(0)), pl.BlockSpec((TB, D), row(0)), _full((8, D)), _full((2 * D, D))],
        out_specs=(pl.BlockSpec((TB, D), row(0)),) * 5 + (_full((8, D)),),
        sem=("arbitrary",), vmem_mb=48,
    )(dylat, o_f, o_b, p_main, p_main, y_f, y_b, xa, vecs, w_out)


def _ffn_up(u2, w_gate, w_up):
    n = u2.shape[0]
    tb = 512

    def body(u_ref, wg_ref, wu_ref, g_ref, up_ref, a_ref):
        uv = u_ref[...]
        gt = _nn(uv, wg_ref[...])
        upv = _nn(uv, wu_ref[...])
        g_ref[...] = gt
        up_ref[...] = upv
        a_ref[...] = (_silu(gt) * upv).astype(a_ref.dtype)

    blk = pl.BlockSpec((tb, DFF_HALF), lambda i, j: (i, j))
    wblk = pl.BlockSpec((D, DFF_HALF), lambda i, j: (0, j))
    return _pcall(
        body, name="ffn_up",
        out_shape=(jax.ShapeDtypeStruct((n, DFF), F32), jax.ShapeDtypeStruct((n, DFF), F32),
                   jax.ShapeDtypeStruct((n, DFF), MXU_DTYPE)),
        grid=(n // tb, 2), in_specs=[pl.BlockSpec((tb, D), lambda i, j: (i, 0)), wblk, wblk],
        out_specs=(blk, blk, blk), sem=("parallel", "parallel"), vmem_mb=48,
    )(u2, w_gate, w_up)


def _ffn_down_loss(act, w_down, h1, tgt, vecs):
    n = act.shape[0]
    tb = 512

    def body(a_ref, w_ref, h1_ref, t_ref, vec_ref, dh2_ref, dffn_ref, acc_ref):
        i = pl.program_id(0)

        @pl.when(i == 0)
        def _():
            acc_ref[...] = jnp.zeros_like(acc_ref)

        ffn = _nn(a_ref[...], w_ref[...])
        g2 = vec_ref[0:1, :]
        fw = vec_ref[1:2, :]
        h2 = h1_ref[...] + g2 * ffn
        r = lax.rsqrt(jnp.mean(h2 * h2, axis=1, keepdims=True) + EPS)
        xh = h2 * r
        err = xh * fw - t_ref[...]
        dy = err * (1.0 / D)
        acc_ref[2:3, :] += _colsum(err * err) * (0.5 / D)
        acc_ref[1:2, :] += _colsum(dy * xh)
        dxh = dy * fw
        dh2 = r * (dxh - xh * jnp.mean(dxh * xh, axis=1, keepdims=True))
        dh2_ref[...] = dh2
        dffn_ref[...] = (g2 * dh2).astype(dffn_ref.dtype)
        acc_ref[0:1, :] += _colsum(dh2 * ffn)

    return _pcall(
        body, name="ffn_down_loss",
        out_shape=(jax.ShapeDtypeStruct((n, D), F32), jax.ShapeDtypeStruct((n, D), MXU_DTYPE),
                   jax.ShapeDtypeStruct((8, D), F32)),
        grid=(n // tb,),
        in_specs=[pl.BlockSpec((tb, DFF), lambda i: (i, 0)), _full((DFF, D)), pl.BlockSpec((tb, D), lambda i: (i, 0)),
                  pl.BlockSpec((tb, D), lambda i: (i, 0)), _full((8, D))],
        out_specs=(pl.BlockSpec((tb, D), lambda i: (i, 0)), pl.BlockSpec((tb, D), lambda i: (i, 0)), _full((8, D))),
        sem=("arbitrary",), vmem_mb=48,
    )(act, w_down, h1, tgt, vecs)


def _ffn_bwd(dffn, w_down, gate, up, w_gate, w_up, h1, ylat, dh2, vecs):
    n = dffn.shape[0]
    tb = 256

    def body(df_ref, wd_ref, g_ref, up_ref, wg_ref, wu_ref, h1_ref, yl_ref, dh2_ref, vec_ref,
             dg_ref, dup_ref, dh1_ref, dyl_ref, acc_ref, du_scr):
        i, j = pl.program_id(0), pl.program_id(1)

        @pl.when((i == 0) & (j == 0))
        def _():
            acc_ref[...] = jnp.zeros_like(acc_ref)

        dact = _nt(df_ref[...], wd_ref[...])
        gt = g_ref[...]
        upv = up_ref[...]
        dgt = (dact * upv * _dsilu(gt)).astype(MXU_DTYPE)
        dupv = (dact * _silu(gt)).astype(MXU_DTYPE)
        dg_ref[...] = dgt
        dup_ref[...] = dupv
        part = _nt(dgt, wg_ref[...]) + _nt(dupv, wu_ref[...])

        @pl.when(j == 0)
        def _():
            du_scr[...] = part

        @pl.when(j == 1)
        def _():
            du = du_scr[...] + part
            h1 = h1_ref[...]
            r = lax.rsqrt(jnp.mean(h1 * h1, axis=1, keepdims=True) + EPS)
            xh = h1 * r
            nw = vec_ref[2:3, :]
            acc_ref[0:1, :] += _colsum(du)
            acc_ref[1:2, :] += _colsum(du * xh * nw)
            dn = du * vec_ref[1:2, :]
            acc_ref[2:3, :] += _colsum(dn * xh)
            dxh = dn * nw
            dh1 = dh2_ref[...] + r * (dxh - xh * jnp.mean(dxh * xh, axis=1, keepdims=True))
            dh1_ref[...] = dh1
            dyl_ref[...] = (vec_ref[0:1, :] * dh1).astype(dyl_ref.dtype)
            acc_ref[3:4, :] += _colsum(dh1 * yl_ref[...])

    tok = pl.BlockSpec((tb, D), lambda i, j: (i, 0))
    ffb = pl.BlockSpec((tb, DFF_HALF), lambda i, j: (i, j))
    return _pcall(
        body, name="ffn_bwd",
        out_shape=(jax.ShapeDtypeStruct((n, DFF), MXU_DTYPE), jax.ShapeDtypeStruct((n, DFF), MXU_DTYPE),
                   jax.ShapeDtypeStruct((n, D), F32), jax.ShapeDtypeStruct((n, D), MXU_DTYPE),
                   jax.ShapeDtypeStruct((8, D), F32)),
        grid=(n // tb, 2),
        in_specs=[tok, pl.BlockSpec((DFF_HALF, D), lambda i, j: (j, 0)), ffb, ffb,
                  pl.BlockSpec((D, DFF_HALF), lambda i, j: (0, j)), pl.BlockSpec((D, DFF_HALF), lambda i, j: (0, j)),
                  tok, tok, tok, _full((8, D))],
        out_specs=(ffb, ffb, tok, tok, _full((8, D))),
        scratch=[pltpu.VMEM((tb, D), F32)], sem=("arbitrary", "arbitrary"), vmem_mb=48,
    )(dffn, w_down, gate, up, w_gate, w_up, h1, ylat, dh2, vecs)


def _dw(a, b, name, col_blk=None, a_col=0, b_col=0):
    tn_rows = a.shape[0]
    bt = 512 if tn_rows % 512 == 0 else 256
    kk, nn_ = col_blk if col_blk is not None else (a.shape[1], b.shape[1])
    bk = 1024 if kk % 1024 == 0 else (1408 if kk % 1408 == 0 else kk)
    bn = 1024 if nn_ % 1024 == 0 else (1408 if nn_ % 1408 == 0 else nn_)
    nt = tn_rows // bt
    ka, nb_ = kk // bk, nn_ // bn

    def body(a_ref, b_ref, o_ref):
        t = pl.program_id(2)
        part = _tn(a_ref[...], b_ref[...])

        @pl.when(t == 0)
        def _():
            o_ref[...] = part

        @pl.when(t > 0)
        def _():
            o_ref[...] += part

    return _pcall(
        body, name=name, out_shape=jax.ShapeDtypeStruct((kk, nn_), F32), grid=(ka, nb_, nt),
        in_specs=[pl.BlockSpec((bt, bk), lambda i, j, t: (t, a_col * ka + i)),
                  pl.BlockSpec((bt, bn), lambda i, j, t: (t, b_col * nb_ + j))],
        out_specs=pl.BlockSpec((bk, bn), lambda i, j, t: (i, j)),
        sem=("parallel", "parallel", "arbitrary"), vmem_mb=40,
    )(a, b)


def _du_prenorm_bwd(segs, ddt, w_main, w_dt, xin, mods, dres, row_off, tb, name):
    n = xin.shape[0]
    nt = n // tb
    off = row_off // tb
    has_dx = dres is not None

    def body(*refs):
        seg_refs = refs[:7]
        ddt_ref, w_ref, wdt_ref, x_ref, mod_ref = refs[7:12]
        rest = refs[12:]
        if has_dx:
            dres_ref, dx_ref, acc_ref, du_scr = rest
        else:
            acc_ref, du_scr = rest
        i, j = pl.program_id(0), pl.program_id(1)

        @pl.when((i == 0) & (j == 0))
        def _():
            acc_ref[...] = jnp.zeros_like(acc_ref)

        @pl.when(j == 0)
        def _():
            du_scr[...] = _nt(ddt_ref[...], wdt_ref[...])

        for k in range(8):
            @pl.when(j == k)
            def _(k=k):
                if k < 6:
                    sv = seg_refs[k][...]
                else:
                    sv = seg_refs[6][:, (k - 6) * D:(k - 5) * D]
                du_scr[...] += _nt(sv, w_ref[...])

        @pl.when(j == 7)
        def _():
            du = du_scr[...]
            xv = x_ref[...]
            r = lax.rsqrt(jnp.mean(xv * xv, axis=1, keepdims=True) + EPS)
            xh = xv * r
            nw = mod_ref[1:2, :]
            acc_ref[0:1, :] += _colsum(du)
            acc_ref[1:2, :] += _colsum(du * xh * nw)
            dn = du * mod_ref[0:1, :]
            acc_ref[2:3, :] += _colsum(dn * xh)
            if has_dx:
                dxh = dn * nw
                dx_ref[...] = dres_ref[...] + r * (dxh - xh * jnp.mean(dxh * xh, axis=1, keepdims=True))

    tokT = lambda w: pl.BlockSpec((tb, w), lambda i, j: (i + off, 0))
    tok = pl.BlockSpec((tb, D), lambda i, j: (i, 0))
    in_specs = [tokT(D)] * 6 + [tokT(2 * D), tokT(128), pl.BlockSpec((D, D), lambda i, j: (0, j)), _full((D, 128)),
                                tok, _full((8, D))]
    args = list(segs) + [ddt, w_main, w_dt, xin, mods]
    out_shape = [jax.ShapeDtypeStruct((8, D), F32)]
    out_specs = [_full((8, D))]
    if has_dx:
        in_specs.append(tok)
        args.append(dres)
        out_shape.insert(0, jax.ShapeDtypeStruct((n, D), F32))
        out_specs.insert(0, tok)
    return _pcall(
        body, name=name, out_shape=tuple(out_shape), grid=(nt, 8), in_specs=in_specs, out_specs=tuple(out_specs),
        scratch=[pltpu.VMEM((tb, D), F32)], sem=("arbitrary", "arbitrary"), vmem_mb=48,
    )(*args)


def _sum8(v):
    def body(v_ref, o_ref):
        acc = v_ref[0]
        for k in range(1, 8):
            acc = acc + v_ref[k]
        o_ref[...] = acc

    return _pcall(body, name="small_sum", out_shape=jax.ShapeDtypeStruct(v.shape[1:], F32),
                  in_specs=[pl.BlockSpec(memory_space=pltpu.VMEM)], out_specs=pl.BlockSpec(memory_space=pltpu.VMEM))(v)


def _adamw(w, m, v, g, name):
    rows, cols = w.shape
    rb = 256 if rows % 256 == 0 else (352 if rows % 352 == 0 else rows)
    c1 = 1.0 - B1 ** STEP
    c2 = 1.0 - B2 ** STEP

    def body(w_ref, m_ref, v_ref, g_ref, d_ref, nm_ref, nv_ref):
        gv = g_ref[...]
        mn = B1 * m_ref[...] + (1.0 - B1) * gv
        vn = B2 * v_ref[...] + (1.0 - B2) * (gv * gv)
        nm_ref[...] = mn
        nv_ref[...] = vn
        d_ref[...] = -LR * ((mn / c1) / (jnp.sqrt(vn / c2) + AEPS) + WD * w_ref[...])

    spec = pl.BlockSpec((rb, cols), lambda i: (i, 0))
    return _pcall(
        body, name=name, out_shape=(jax.ShapeDtypeStruct(w.shape, F32),) * 3, grid=(rows // rb,),
        in_specs=[spec] * 4, out_specs=(spec,) * 3, sem=("parallel",), vmem_mb=40,
    )(w, m, v, g)


def _rows(v, n):
    f = v.reshape(-1)
    return jnp.pad(f, (0, n * D - f.shape[0])).reshape(n, D)


def kernel(x, c, ctx, c_ctx, w_ada, b_ada, norm_mix, w_in, conv_w, conv_b, ssd_a_log, ssd_dt_bias, ssd_d, ssd_norm, hgrn_lb_raw, hgrn_norm, w_out, norm_ffn, w_gate, w_up, w_down, final_norm, loss_target, m_c_ctx, m_w_ada, m_b_ada, m_norm_mix, m_w_in, m_conv_w, m_conv_b, m_ssd_a_log, m_ssd_dt_bias, m_ssd_d, m_ssd_norm, m_hgrn_lb_raw, m_hgrn_norm, m_w_out, m_norm_ffn, m_w_gate, m_w_up, m_w_down, m_final_norm, v_c_ctx, v_w_ada, v_b_ada, v_norm_mix, v_w_in, v_conv_w, v_conv_b, v_ssd_a_log, v_ssd_dt_bias, v_ssd_d, v_ssd_norm, v_hgrn_lb_raw, v_hgrn_norm, v_w_out, v_norm_ffn, v_w_gate, v_w_up, v_w_down, v_final_norm):
    ix, iy, ic = lax.axis_index("x"), lax.axis_index("y"), lax.axis_index("c")
    chip = 2 * ix + iy
    me = 2 * chip + ic
    xl, xc, tgt = x[0], ctx[0], loss_target[0]
    n_lat, n_ctx = xl.shape[0], xc.shape[0]
    assert n_ctx == TB and n_lat % 1024 == 0
    t_total = n_lat + n_ctx
    nb = t_total // TB

    pack = jnp.concatenate([c, hgrn_lb_raw.reshape(1, D), _rows(conv_w[0], 3), jnp.zeros((3, D), F32)], axis=0)
    gath = _allgather8(pack, "small_gather").reshape(8, 8, D)
    c_all = gath[:, 0]
    lbraw_full = gath[0::2, 1].reshape(4, 2, 2, 256).transpose(1, 2, 0, 3).reshape(4, D)
    convw_full = gath[0::2, 2:5].reshape(4, 3 * D)[:, :KCONV * 512].reshape(4, KCONV, 512).transpose(1, 0, 2)
    convw_full = convw_full.reshape(KCONV, 2048)
    lbraw8 = jnp.pad(lbraw_full, ((0, 4), (0, 0)))
    convp = jnp.concatenate([convw_full, conv_b, jnp.zeros((2, 2048), F32)], axis=0)
    dtb = jnp.pad(ssd_dt_bias.reshape(1, 32), ((0, 7), (0, 96)))
    alog = jnp.pad(ssd_a_log.reshape(1, 32), ((0, 7), (0, 96)))

    araw = jnp.concatenate([c_all, c_ctx.reshape(1, D), jnp.zeros((7, D), F32)], axis=0)
    ncol_ada = w_ada.shape[2]
    b_shard = lax.dynamic_slice(b_ada, (0, chip * ncol_ada), (1, ncol_ada))
    mod_shard = _ada_fwd(araw, w_ada[0], b_shard)
    mod_all = _allgather8(mod_shard, "mod_gather").reshape(8, 16, ncol_ada)[0::2]
    mod_full = mod_all.transpose(1, 0, 2).reshape(16, 4 * ncol_ada)
    my_mod = lax.dynamic_slice(mod_full, (me, 0), (1, 6 * D)).reshape(6, D)
    sh1, sc1, g1, sh2, sc2, g2 = (my_mod[k:k + 1] for k in range(6))
    csh1, csc1 = mod_full[8:9, 0:D], mod_full[8:9, D:2 * D]

    blob = jnp.concatenate([w_in[0].reshape(R_IN, D), w_out[0], w_gate[0].reshape(R_FF, D), w_up[0].reshape(R_FF, D),
                            w_down[0], jnp.zeros((BLOB_ROWS - R_IN - R_OUT - 3 * R_FF, D), F32)], axis=0)
    wall = _weights_allgather(blob.astype(MXU_DTYPE))
    o0, o1, o2, o3, o4 = 0, R_IN, R_IN + R_OUT, R_IN + R_OUT + R_FF, R_IN + R_OUT + 2 * R_FF
    colcat = lambda a, r: a.reshape(4, D, r).transpose(1, 0, 2).reshape(D, 4 * r)
    w_in_f = colcat(wall[:, o0:o1], NIN // 4)
    w_main = w_in_f[:, :8 * D]
    w_dtp = jnp.pad(w_in_f[:, 8 * D:], ((0, 0), (0, 128 - (NIN - 8 * D))))
    w_out_f = wall[:, o1:o2].reshape(2 * D, D)
    w_gate_f = colcat(wall[:, o2:o3], DFF // 4)
    w_up_f = colcat(wall[:, o3:o4], DFF // 4)
    w_down_f = wall[:, o4:o4 + R_FF].reshape(DFF, D)

    zrow = jnp.zeros((1, D), F32)
    mods_lat = jnp.concatenate([1.0 + sc1, sh1, norm_mix, zrow, zrow, zrow, zrow, zrow], axis=0)
    mods_ctx = jnp.concatenate([1.0 + csc1, csh1, norm_mix, zrow, zrow, zrow, zrow, zrow], axis=0)
    outs = _inproj(xl, mods_lat, w_main, w_dtp, t_total, 1024, 0, None, "inproj_lat")
    p_main, p_dt, u_all = _inproj(xc, mods_ctx, w_main, w_dtp, t_total, TB, nb - 1, outs, "inproj_ctx")

    o_f, hs_f = _hgrn_fwd(p_main, lbraw8, 0, nb)
    o_b, hs_b = _hgrn_fwd(p_main, lbraw8, 1, nb)
    xa, dts = _ssd_prep(p_main, p_dt, convp, dtb, nb)
    y_f, ss_f = _ssd_fwd(xa, dts, alog, 0, nb)
    y_b, ss_b = _ssd_fwd(xa, dts, alog, 1, nb)

    vec_mix = jnp.concatenate([jnp.tile(hgrn_norm, (1, NH)), jnp.repeat(ssd_d, SP, axis=1), ssd_norm, g1, 1.0 + sc2,
                               sh2, norm_ffn, zrow], axis=0)
    ymix, ylat, h1, u2 = _mix_out(o_f, o_b, p_main, y_f, y_b, xa, xl, vec_mix, w_out_f)
    gate, up, act = _ffn_up(u2, w_gate_f, w_up_f)
    vec_loss = jnp.concatenate([g2, final_norm.reshape(1, D)] + [zrow] * 6, axis=0)
    dh2, dffn, acc_loss = _ffn_down_loss(act, w_down_f, h1, tgt, vec_loss)

    vec_ffn = jnp.concatenate([g1, 1.0 + sc2, norm_ffn] + [zrow] * 5, axis=0)
    dgate, dup, dh1, dylat, acc_ffn = _ffn_bwd(dffn, w_down_f, gate, up, w_gate_f, w_up_f, h1, ylat, dh2, vec_ffn)
    gw_down = _dw(act, dffn, "dw_down")
    gw_gate = _dw(u2, dgate, "dw_gate")
    gw_up = _dw(u2, dup, "dw_up")
    do, dgr, dys, dzr, dxs_skip, acc_mix = _mix_bwd(dylat, o_f, o_b, p_main, y_f, y_b, xa, vec_mix, w_out_f)
    gw_out = _dw(ymix, dylat, "dw_out")

    dq0, dff, dv0, dlb_f = _hgrn_bwd(p_main, lbraw8, hs_f, do, 0, nb, None)
    dq, dfb, dv, dlb_b = _hgrn_bwd(p_main, lbraw8, hs_b, do, 1, nb, (dq0, dv0))
    dxa0, ddts0, da_f = _ssd_bwd(xa, dts, alog, ss_f, dys, 0, nb, None)
    dxa, ddts, da_b = _ssd_bwd(xa, dts, alog, ss_b, dys, 1, nb, (dxa0, ddts0))
    dxbc, ddt, acc_conv, acc_dtb = _ssd_prep_bwd(p_main, p_dt, convp, dtb, dxa, dxs_skip, ddts, nb)

    zpad = lambda a: jnp.concatenate([a, jnp.zeros((n_ctx, D), a.dtype)], axis=0)
    segs = [dq, dff, dfb, dv, zpad(dgr), zpad(dzr), dxbc]
    bmods_lat = jnp.concatenate([1.0 + sc1, norm_mix] + [zrow] * 6, axis=0)
    bmods_ctx = jnp.concatenate([1.0 + csc1, norm_mix] + [zrow] * 6, axis=0)
    grad_x, acc_lat = _du_prenorm_bwd(segs, ddt, w_main, w_dtp, xl, bmods_lat, dh1, 0, 512, "du_lat")
    (acc_ctx,) = _du_prenorm_bwd(segs, ddt, w_main, w_dtp, xc, bmods_ctx, None, n_lat, TB, "du_ctx")

    gw_in_parts = [_dw(u_all, s_, f"dw_in_{k}") for k, s_ in enumerate(segs[:6])]
    gw_in_parts += [_dw(u_all, dxbc, f"dw_in_{6 + k}", col_blk=(D, D), b_col=k) for k in range(2)]
    gw_in_parts.append(_dw(u_all, ddt, "dw_in_dt")[:, :NIN - 8 * D])
    gw_in = jnp.concatenate(gw_in_parts, axis=1)

    colsplit = lambda a, r: a.reshape(D, 4, r).transpose(1, 0, 2).reshape(4, -1, D)
    gblob = jnp.concatenate([colsplit(gw_in, NIN // 4), gw_out.reshape(4, R_OUT, D), colsplit(gw_gate, DFF // 4),
                             colsplit(gw_up, DFF // 4), gw_down.reshape(4, R_FF, D),
                             jnp.zeros((4, BLOB_ROWS - R_IN - R_OUT - 3 * R_FF, D), F32)], axis=1).astype(MXU_DTYPE)
    core_arr = jnp.reshape(ic, (1,)).astype(jnp.int32)
    chip_arr = jnp.reshape(chip, (1,)).astype(jnp.int32)
    pair = _pair_sum(gblob, _pair_exchange(gblob), core_arr)
    red_half = _chip_sum(pair, _chip_exchange(pair), chip_arr)
    red = _pair_allgather(red_half)
    g_w_in = red[o0:o1].reshape(D, NIN // 4)
    g_w_out = red[o1:o2]
    g_w_gate = red[o2:o3].reshape(D, DFF // 4)
    g_w_up = red[o3:o4].reshape(D, DFF // 4)
    g_w_down = red[o4:o4 + R_FF]

    dmod_lat = jnp.concatenate([acc_lat[0:2], acc_ffn[3:4], acc_ffn[0:2], acc_loss[0:1]], axis=0)
    misc = jnp.concatenate([(da_f + da_b)[0, :32], jnp.zeros((96,), F32), acc_dtb[0, :32], jnp.zeros((96,), F32),
                            jnp.sum(acc_loss[2]).reshape(1), jnp.zeros((D - 257,), F32)]).reshape(1, D)
    sv = jnp.concatenate([
        dmod_lat, acc_ctx[0:2], (acc_lat[2:3] + acc_ctx[2:3]), acc_ffn[2:3], acc_loss[1:2], acc_mix[2:3],
        acc_mix[0:1], acc_mix[1:2], dlb_f[0:1], dlb_b[0:1], acc_conv[0:6].reshape(12, D), misc,
        jnp.zeros((3, D), F32)], axis=0)
    sv_all = _allgather8(sv, "small_grads_gather").reshape(8, 32, D)
    ssum = _sum8(sv_all)
    dmod_rows = sv_all[:, 0:6].reshape(8, 6 * D)
    dmod_ctx_row = jnp.concatenate([ssum[6:8].reshape(1, 2 * D), jnp.zeros((1, 4 * D), F32)], axis=1)
    dmod_full = jnp.concatenate([dmod_rows, dmod_ctx_row, jnp.zeros((7, 6 * D), F32)], axis=0)
    grad_b_ada = jnp.sum(dmod_full, axis=0, keepdims=True)
    dmod_shard = lax.dynamic_slice(dmod_full, (0, chip * ncol_ada), (16, ncol_ada))
    g_w_ada, da_part = _ada_bwd(araw, dmod_shard, w_ada[0])
    da_all = _allgather8(da_part, "ada_ctx_gather").reshape(8, 16, D)[0::2, 8]
    cc = c_ctx.reshape(1, D)
    grad_c_ctx = (jnp.sum(da_all, axis=0, keepdims=True) * _dsilu(cc)).reshape(D)

    grad_norm_mix, grad_norm_ffn, grad_final_norm = ssum[8:9], ssum[9:10], ssum[10].reshape(D)
    grad_ssd_norm = ssum[11:12]
    grad_hgrn_norm = jnp.sum(ssum[12].reshape(NH, HF), axis=0, keepdims=True)
    grad_ssd_d = jnp.sum(ssum[13].reshape(SHEADS, SP), axis=1).reshape(1, SHEADS)
    lb_full = _sig(lbraw_full[0:2] - lbraw_full[2:4])
    dr0 = ssum[14:16] * lb_full * (1.0 - lb_full)
    grad_lb_full = jnp.stack([dr0, -dr0], axis=0)
    grad_lb = lax.dynamic_slice(grad_lb_full, (0, 0, chip * 256), (2, 2, 256))
    grad_conv_w = lax.dynamic_slice(ssum[16:26].reshape(KCONV, 2048), (0, chip * 512), (KCONV, 512)).reshape(1, KCONV, 512)
    grad_conv_b = ssum[26:28].reshape(1, 2048)
    a_val = -jnp.exp(ssd_a_log)
    grad_a_log = ssum[28, 0:32].reshape(1, 2, SHEADS) * a_val
    grad_dt_bias = ssum[28, 128:160].reshape(1, 2, SHEADS)
    loss = ssum[28, 256]

    small_w = [c_ctx, b_ada, norm_mix, conv_w, conv_b, ssd_a_log, ssd_dt_bias, ssd_d, ssd_norm, hgrn_lb_raw,
               hgrn_norm, norm_ffn, final_norm]
    small_m = [m_c_ctx, m_b_ada, m_norm_mix, m_conv_w, m_conv_b, m_ssd_a_log, m_ssd_dt_bias, m_ssd_d, m_ssd_norm,
               m_hgrn_lb_raw, m_hgrn_norm, m_norm_ffn, m_final_norm]
    small_v = [v_c_ctx, v_b_ada, v_norm_mix, v_conv_w, v_conv_b, v_ssd_a_log, v_ssd_dt_bias, v_ssd_d, v_ssd_norm,
               v_hgrn_lb_raw, v_hgrn_norm, v_norm_ffn, v_final_norm]
    small_g = [grad_c_ctx, grad_b_ada, grad_norm_mix, grad_conv_w, grad_conv_b, grad_a_log, grad_dt_bias, grad_ssd_d,
               grad_ssd_norm, grad_lb, grad_hgrn_norm, grad_norm_ffn, grad_final_norm]
    nrows = [-(-a.size // D) for a in small_w]
    packs = lambda lst: jnp.concatenate([_rows(a, r) for a, r in zip(lst, nrows)]
                                        + [jnp.zeros((24 - sum(nrows), D), F32)], axis=0)
    sd, sm, svv = _adamw(packs(small_w), packs(small_m), packs(small_v), packs(small_g), "adamw_small")

    def unpack(p):
        out, r0 = [], 0
        for a, r in zip(small_w, nrows):
            out.append(p[r0:r0 + r].reshape(-1)[:a.size].reshape(a.shape))
            r0 += r
        return out

    sd, sm, svv = unpack(sd), unpack(sm), unpack(svv)
    big = {}
    for nm, w_, m_, v_, g_ in (("w_ada", w_ada, m_w_ada, v_w_ada, g_w_ada), ("w_in", w_in, m_w_in, v_w_in, g_w_in),
                               ("w_out", w_out, m_w_out, v_w_out, g_w_out),
                               ("w_gate", w_gate, m_w_gate, v_w_gate, g_w_gate),
                               ("w_up", w_up, m_w_up, v_w_up, g_w_up),
                               ("w_down", w_down, m_w_down, v_w_down, g_w_down)):
        dl, nm_, nv_ = _adamw(w_[0], m_[0], v_[0], g_, "adamw_" + nm)
        big[nm] = (g_[None], dl[None], nm_[None], nv_[None])

    order = ["c_ctx", "w_ada", "b_ada", "norm_mix", "w_in", "conv_w", "conv_b", "ssd_a_log", "ssd_dt_bias", "ssd_d",
             "ssd_norm", "hgrn_lb_raw", "hgrn_norm", "w_out", "norm_ffn", "w_gate", "w_up", "w_down", "final_norm"]
    small_names = ["c_ctx", "b_ada", "norm_mix", "conv_w", "conv_b", "ssd_a_log", "ssd_dt_bias", "ssd_d", "ssd_norm",
                   "hgrn_lb_raw", "hgrn_norm", "norm_ffn", "final_norm"]
    table = dict(big)
    for k, nm in enumerate(small_names):
        table[nm] = (small_g[k].reshape(small_w[k].shape), sd[k], sm[k], svv[k])
    grads = [table[nm][0] for nm in order]
    deltas = [table[nm][1] for nm in order]
    new_m = [table[nm][2] for nm in order]
    new_v = [table[nm][3] for nm in order]
    return (loss, grad_x[None], *grads, *deltas, *new_m, *new_v)
```

```python
import functools
import math

import jax
import jax.numpy as jnp
from jax import lax
from jax.experimental import pallas as pl
from jax.experimental.pallas import tpu as pltpu

F32 = jnp.float32
BF16 = jnp.bfloat16
MXU_DTYPE = jnp.bfloat16
_INTERPRET = False

D = 1024
NH, HF = 8, 128
HC = 64
SC = 128
SN = 128
SHEADS, SP = 16, 64
GRID_W = 64
KCONV = 5
DFF = 2816
FSL = 768
DFFP = 4 * FSL
NIN = 8224
TB = 256
EPS = 1e-6
LR, B1, B2, AEPS, WD, STEP = 0.001, 0.9, 0.999, 1e-08, 0.01, 10
MESH_ID = pl.DeviceIdType.MESH
NSH = NIN // 4
WSL = 2048
WTAIL = 128


def _pcall(body, *, name, out_shape, grid=(), in_specs=None, out_specs=None, scratch=(), sem=None,
           vmem_mb=None, aliases=None):
    params = {}
    if sem is not None:
        params["dimension_semantics"] = sem
    if vmem_mb is not None:
        params["vmem_limit_bytes"] = vmem_mb << 20
    kw = dict(name=name, out_shape=out_shape, scratch_shapes=list(scratch),
              input_output_aliases=aliases or {}, compiler_params=pltpu.CompilerParams(**params),
              interpret=_INTERPRET)
    if grid:
        kw["grid"] = grid
    if in_specs is not None:
        kw["in_specs"] = in_specs
    if out_specs is not None:
        kw["out_specs"] = out_specs
    return pl.pallas_call(body, **kw)


def _mx(a):
    return a.astype(MXU_DTYPE)


def _dg(a, b, ca, cb):
    return lax.dot_general(_mx(a), _mx(b), (((ca,), (cb,)), ((), ())), preferred_element_type=F32)


def _nn(a, b):
    return _dg(a, b, 1, 0)


def _nt(a, b):
    return _dg(a, b, 1, 1)


def _tn(a, b):
    return _dg(a, b, 0, 0)


def _dot01(m, x):
    hi = x.astype(BF16)
    r1 = x - hi.astype(F32)
    mid = r1.astype(BF16)
    lo = (r1 - mid.astype(F32)).astype(BF16)
    f = lambda t: lax.dot_general(m, t, (((1,), (0,)), ((), ())), preferred_element_type=F32)
    return f(hi) + f(mid) + f(lo)


def _tri(n, upper):
    r = lax.broadcasted_iota(jnp.int32, (n, n), 0)
    c = lax.broadcasted_iota(jnp.int32, (n, n), 1)
    return (c >= r) if upper else (c <= r)


def _b01(mask):
    return jnp.where(mask, 1.0, 0.0).astype(BF16)


def _sig(x):
    return jax.nn.sigmoid(x)


def _silu(x):
    return x * _sig(x)


def _dsilu(x):
    s = _sig(x)
    return s * (1.0 + x * (1.0 - s))


def _softplus(x):
    return jnp.maximum(x, 0.0) + jnp.log(1.0 + jnp.exp(-jnp.abs(x)))


def _rowsum(x):
    return jnp.sum(x, axis=1, keepdims=True)


def _colsum(x):
    return jnp.sum(x, axis=0, keepdims=True)


def _full(shape):
    return pl.BlockSpec(shape, lambda *_: (0,) * len(shape))


def _allgather8(v, name):
    m_per, n = v.shape

    def body(x_ref, out_ref, send_sems, recv_sems, local_sem):
        x, y, c = lax.axis_index("x"), lax.axis_index("y"), lax.axis_index("c")
        me, sibling = (x, y, c), (x, y, 1 - c)
        chips = [(1 - x, y), (x, 1 - y), (1 - x, 1 - y)]

        def rows(px, py, pc):
            return out_ref.at[pl.ds((4 * px + 2 * py + pc) * m_per, m_per), :]

        def copy(k, block, to, src=None):
            return pltpu.make_async_remote_copy(
                src_ref=rows(*block) if src is None else src, dst_ref=rows(*block),
                send_sem=send_sems.at[k], recv_sem=recv_sems.at[k], device_id=to, device_id_type=MESH_ID)

        mine = pltpu.make_async_copy(x_ref, rows(*me), local_sem)
        mine.start()
        first = [copy(0, me, sibling, src=x_ref)]
        first += [copy(1 + j, me, (*chip, c), src=x_ref) for j, chip in enumerate(chips)]
        for cp in first:
            cp.start()
        passed = [copy(4 + j, (*chip, c), sibling) for j, chip in enumerate(chips)]
        for j, chip in enumerate(chips):
            copy(1 + j, (*chip, c), me).wait_recv()
            passed[j].start()
        copy(0, sibling, me).wait_recv()
        for j, chip in enumerate(chips):
            copy(4 + j, (*chip, 1 - c), me).wait_recv()
        for cp in first + passed:
            cp.wait_send()
        mine.wait()

    return _pcall(
        body, name=name, out_shape=jax.ShapeDtypeStruct((8 * m_per, n), v.dtype),
        in_specs=[pl.BlockSpec(memory_space=pltpu.VMEM)], out_specs=pl.BlockSpec(memory_space=pltpu.VMEM),
        scratch=[pltpu.SemaphoreType.DMA((7,)), pltpu.SemaphoreType.DMA((7,)), pltpu.SemaphoreType.DMA],
    )(v)


def _weights_allgather(shards):
    n = len(shards)

    def body(*refs):
        ins, outs = refs[:n], refs[n:2 * n]
        send_sems, recv_sems = refs[2 * n:]
        x, y, c = lax.axis_index("x"), lax.axis_index("y"), lax.axis_index("c")
        sibling = (x, y, 1 - c)
        chips = [(1 - x, y), (x, 1 - y), (1 - x, 1 - y)]

        def part(a, px, py, pc):
            half = ins[a].shape[0] // 2
            return outs[a].at[2 * px + py, pl.ds(pc * half, half), :]

        def copy(a, k, block, to, src=None):
            return pltpu.make_async_remote_copy(
                src_ref=part(a, *block) if src is None else src, dst_ref=part(a, *block),
                send_sem=send_sems.at[6 * a + k], recv_sem=recv_sems.at[6 * a + k], device_id=to,
                device_id_type=MESH_ID)

        started = []
        for a in range(n):
            half = ins[a].shape[0] // 2
            for j, chip in enumerate(chips):
                cp = copy(a, j, (x, y, c), (*chip, c), src=ins[a].at[pl.ds(c * half, half), :])
                cp.start()
                started.append(cp)
        for a in range(n):
            for j, chip in enumerate(chips):
                copy(a, j, (*chip, c), (x, y, c)).wait_recv()
                cp = copy(a, 3 + j, (*chip, c), sibling)
                cp.start()
                started.append(cp)
        for a in range(n):
            for j, chip in enumerate(chips):
                copy(a, 3 + j, (*chip, 1 - c), (x, y, c)).wait_recv()
        for cp in started:
            cp.wait_send()

    return _pcall(
        body, name="weights_allgather",
        out_shape=tuple(jax.ShapeDtypeStruct((4,) + s_.shape, s_.dtype) for s_ in shards),
        in_specs=[pl.BlockSpec(memory_space=pl.ANY)] * n, out_specs=(pl.BlockSpec(memory_space=pl.ANY),) * n,
        scratch=[pltpu.SemaphoreType.DMA((6 * n,)), pltpu.SemaphoreType.DMA((6 * n,))],
    )(*shards)


def _pair_exchange(gs):
    n = len(gs)

    def body(*refs):
        ins, outs = refs[:n], refs[n:2 * n]
        send_sems, recv_sems = refs[2 * n:]
        x, y, c = lax.axis_index("x"), lax.axis_index("y"), lax.axis_index("c")
        cps = []
        for a in range(n):
            half = ins[a].shape[1] // 2
            cps.append(pltpu.make_async_remote_copy(
                src_ref=ins[a].at[:, pl.ds((1 - c) * half, half), :], dst_ref=outs[a], send_sem=send_sems.at[a],
                recv_sem=recv_sems.at[a], device_id=(x, y, 1 - c), device_id_type=MESH_ID))
        for cp in cps:
            cp.start()
        for cp in cps:
            cp.wait()

    return _pcall(
        body, name="grads_pair_exchange",
        out_shape=tuple(jax.ShapeDtypeStruct((4, g.shape[1] // 2, g.shape[2]), g.dtype) for g in gs),
        in_specs=[pl.BlockSpec(memory_space=pl.ANY)] * n, out_specs=(pl.BlockSpec(memory_space=pl.ANY),) * n,
        scratch=[pltpu.SemaphoreType.DMA((n,)), pltpu.SemaphoreType.DMA((n,))],
    )(*gs)


def _chip_exchange(hs):
    n = len(hs)

    def body(*refs):
        ins, outs = refs[:n], refs[n:2 * n]
        send_sems, recv_sems = refs[2 * n:]
        x, y, c = lax.axis_index("x"), lax.axis_index("y"), lax.axis_index("c")
        chips = [(1 - x, y), (x, 1 - y), (1 - x, 1 - y)]
        cps = []
        for a in range(n):
            for j, (px, py) in enumerate(chips):
                cps.append(pltpu.make_async_remote_copy(
                    src_ref=ins[a].at[2 * px + py], dst_ref=outs[a].at[j], send_sem=send_sems.at[3 * a + j],
                    recv_sem=recv_sems.at[3 * a + j], device_id=(px, py, c), device_id_type=MESH_ID))
        for cp in cps:
            cp.start()
        for cp in cps:
            cp.wait()

    return _pcall(
        body, name="grads_chip_exchange",
        out_shape=tuple(jax.ShapeDtypeStruct((3,) + h.shape[1:], h.dtype) for h in hs),
        in_specs=[pl.BlockSpec(memory_space=pl.ANY)] * n, out_specs=(pl.BlockSpec(memory_space=pl.ANY),) * n,
        scratch=[pltpu.SemaphoreType.DMA((3 * n,)), pltpu.SemaphoreType.DMA((3 * n,))],
    )(*hs)


def _pair_swap(rs):
    n = len(rs)

    def body(*refs):
        ins, outs = refs[:n], refs[n:2 * n]
        send_sems, recv_sems = refs[2 * n:]
        x, y, c = lax.axis_index("x"), lax.axis_index("y"), lax.axis_index("c")
        cps = [pltpu.make_async_remote_copy(
            src_ref=ins[a], dst_ref=outs[a], send_sem=send_sems.at[a], recv_sem=recv_sems.at[a],
            device_id=(x, y, 1 - c), device_id_type=MESH_ID) for a in range(n)]
        for cp in cps:
            cp.start()
        for cp in cps:
            cp.wait()

    return _pcall(
        body, name="grads_pair_swap", out_shape=tuple(jax.ShapeDtypeStruct(r.shape, r.dtype) for r in rs),
        in_specs=[pl.BlockSpec(memory_space=pl.ANY)] * n, out_specs=(pl.BlockSpec(memory_space=pl.ANY),) * n,
        scratch=[pltpu.SemaphoreType.DMA((n,)), pltpu.SemaphoreType.DMA((n,))],
    )(*rs)


SUM_STEPS = 8


def _pair_sum(gs, recvs, core):
    n = len(gs)

    def body(c_ref, *refs):
        for a in range(n):
            refs[2 * n + a][...] = (refs[a][...].astype(F32) + refs[n + a][...].astype(F32)).astype(refs[2 * n + a].dtype)

    blk = lambda g: (4, g.shape[1] // (2 * SUM_STEPS), g.shape[2])
    return pl.pallas_call(
        body, name="grads_pair_sum",
        out_shape=tuple(jax.ShapeDtypeStruct((4, g.shape[1] // 2, g.shape[2]), g.dtype) for g in gs),
        grid_spec=pltpu.PrefetchScalarGridSpec(
            num_scalar_prefetch=1, grid=(SUM_STEPS,),
            in_specs=[pl.BlockSpec(blk(g), lambda i, cr: (0, cr[0] * SUM_STEPS + i, 0)) for g in gs]
            + [pl.BlockSpec(blk(g), lambda i, cr: (0, i, 0)) for g in gs],
            out_specs=tuple(pl.BlockSpec(blk(g), lambda i, cr: (0, i, 0)) for g in gs)),
        compiler_params=pltpu.CompilerParams(vmem_limit_bytes=40 << 20), interpret=_INTERPRET,
    )(core, *gs, *recvs)


def _chip_sum(hs, recvs, chip):
    n = len(hs)

    def body(k_ref, *refs):
        for a in range(n):
            acc = refs[a][0].astype(F32)
            for j in range(3):
                acc = acc + refs[n + a][j].astype(F32)
            refs[2 * n + a][...] = acc

    rb = lambda h: h.shape[1] // SUM_STEPS
    return pl.pallas_call(
        body, name="grads_chip_sum",
        out_shape=tuple(jax.ShapeDtypeStruct(h.shape[1:], F32) for h in hs),
        grid_spec=pltpu.PrefetchScalarGridSpec(
            num_scalar_prefetch=1, grid=(SUM_STEPS,),
            in_specs=[pl.BlockSpec((1, rb(h), h.shape[2]), lambda i, kr: (kr[0], i, 0)) for h in hs]
            + [pl.BlockSpec((3, rb(h), h.shape[2]), lambda i, kr: (0, i, 0)) for h in hs],
            out_specs=tuple(pl.BlockSpec((rb(h), h.shape[2]), lambda i, kr: (i, 0)) for h in hs)),
        compiler_params=pltpu.CompilerParams(vmem_limit_bytes=40 << 20), interpret=_INTERPRET,
    )(chip, *hs, *recvs)


def _ada_fwd(araw, w, b):
    nblk = w.shape[1] // 512

    def body(a_ref, w_ref, b_ref, o_ref):
        o_ref[...] = _nn(_silu(a_ref[...]), w_ref[...]) + b_ref[...]

    return _pcall(
        body, name="ada_fwd", out_shape=jax.ShapeDtypeStruct((16, w.shape[1]), F32), grid=(nblk,),
        in_specs=[_full((16, D)), pl.BlockSpec((D, 512), lambda j: (0, j)), pl.BlockSpec((1, 512), lambda j: (0, j))],
        out_specs=pl.BlockSpec((16, 512), lambda j: (0, j)), sem=("parallel",),
    )(araw, w, b)


def _ada_bwd(araw, dmod, w):
    nblk = w.shape[1] // 512

    def body(a_ref, d_ref, w_ref, gw_ref, da_ref):
        j = pl.program_id(0)
        gw_ref[...] = _tn(_silu(a_ref[...]), d_ref[...])
        part = _nt(d_ref[...], w_ref[...])

        @pl.when(j == 0)
        def _():
            da_ref[...] = part

        @pl.when(j > 0)
        def _():
            da_ref[...] += part

    return _pcall(
        body, name="ada_bwd",
        out_shape=(jax.ShapeDtypeStruct(w.shape, F32), jax.ShapeDtypeStruct((16, D), F32)), grid=(nblk,),
        in_specs=[_full((16, D)), pl.BlockSpec((16, 512), lambda j: (0, j)), pl.BlockSpec((D, 512), lambda j: (0, j))],
        out_specs=(pl.BlockSpec((D, 512), lambda j: (0, j)), _full((16, D))), sem=("arbitrary",),
    )(araw, dmod, w)


def _w_specs():
    return [pl.BlockSpec((None, D, D), lambda i, j: (j // 2, 0, j % 2)),
            pl.BlockSpec((None, D, WTAIL), lambda i, j: (jnp.maximum(j // 2 - 1, 0), 0, 0)),
            pl.BlockSpec((None, D, WTAIL), lambda i, j: (3, 0, 0))]


def _inproj(xin, mods, wi_main, wi_tail, t_total, tb, blk_off, prev, name):
    n = xin.shape[0]
    nt = n // tb
    ncol = 8

    def body(x_ref, mod_ref, w_ref, wb_ref, wdt_ref, *rest):
        p_ref, pdt_ref, u_ref, uscr = rest[-4:]
        j = pl.program_id(1)

        @pl.when(j == 0)
        def _():
            xv = x_ref[...]
            r = lax.rsqrt(jnp.mean(xv * xv, axis=1, keepdims=True) + EPS)
            u = (xv * r * mod_ref[2:3, :]) * mod_ref[0:1, :] + mod_ref[1:2, :]
            ub = u.astype(MXU_DTYPE)
            uscr[...] = ub
            u_ref[...] = ub
            pdt_ref[...] = _nn(ub, wdt_ref[...])

        p_ref[...] = _nn(uscr[...], w_ref[...])

        @pl.when((j % 2 == 0) & (j > 0))
        def _():
            p_ref[:, 0:WTAIL] += _nn(uscr[...], wb_ref[...])

    in_specs = [pl.BlockSpec((tb, D), lambda i, j: (i, 0)), _full((8, D))] + _w_specs()
    args = [xin, mods, wi_main, wi_tail, wi_tail]
    aliases = None
    if prev is not None:
        in_specs += [pl.BlockSpec(memory_space=pl.ANY)] * 3
        args += list(prev)
        aliases = {5: 0, 6: 1, 7: 2}
    return _pcall(
        body, name=name,
        out_shape=(jax.ShapeDtypeStruct((t_total, ncol * D), F32), jax.ShapeDtypeStruct((t_total, 128), F32),
                   jax.ShapeDtypeStruct((t_total, D), MXU_DTYPE)),
        grid=(nt, ncol), in_specs=in_specs,
        out_specs=(pl.BlockSpec((tb, D), lambda i, j: (i + blk_off, j)),
                   pl.BlockSpec((tb, 128), lambda i, j: (i + blk_off, 0)),
                   pl.BlockSpec((tb, D), lambda i, j: (i + blk_off, 0))),
        scratch=[pltpu.VMEM((tb, D), MXU_DTYPE)], sem=("parallel", "arbitrary"), vmem_mb=48, aliases=aliases,
    )(*args)


def _blk(s, nb, rev):
    return jnp.where(s == 0, nb - 1, (nb - 1 - s) if rev else (s - 1))


def _hgrn_gate(fr, lbraw_ref, d):
    lb = _sig(lbraw_ref[d:d + 1, :] - lbraw_ref[2 + d:3 + d, :])
    sg = _sig(fr)
    return lb, sg, lb + (1.0 - lb) * sg


def _hgrn_fwd(p_main, lbraw, d, nb):
    t_total = p_main.shape[0]
    rev = d == 1
    nch = TB // HC
    scale = HF ** -0.5

    def body(q_ref, f_ref, v_ref, lb_ref, o_ref, sp_ref, st):
        s = pl.program_id(0)

        @pl.when(s == 0)
        def _():
            st[...] = jnp.zeros_like(st)

        mb = _tri(HC, rev)
        m01 = _b01(mb)
        for c in (reversed(range(nch)) if rev else range(nch)):
            rows = slice(c * HC, (c + 1) * HC)
            _, _, f = _hgrn_gate(f_ref[rows, :], lb_ref, d)
            lf = jnp.log(f)
            k = 1.0 - f
            cum = _dot01(m01, lf)
            tot = cum[0:1, :] if rev else cum[HC - 1:HC, :]
            qd = _silu(q_ref[rows, :]) * scale * jnp.exp(cum)
            ki = k * jnp.exp(-cum)
            etot = jnp.exp(tot)
            ke = ki * etot
            v = v_ref[rows, :]
            for h in range(NH):
                cs = slice(h * HF, (h + 1) * HF)
                sth = st[h]
                sp_ref[c, h] = sth.astype(sp_ref.dtype)
                sc = jnp.where(mb, _nt(qd[:, cs], ki[:, cs]), 0.0)
                o_ref[rows, cs] = _nn(sc, v[:, cs]) + _nt(qd[:, cs], sth)
                st[h] = sth * etot[:, cs] + _tn(v[:, cs], ke[:, cs])

    col = lambda j: (lambda s: (_blk(s, nb, rev), j))
    return _pcall(
        body, name=f"hgrn_fwd_{d}",
        out_shape=(jax.ShapeDtypeStruct((t_total, D), F32),
                   jax.ShapeDtypeStruct((nch * nb, NH, HF, HF), MXU_DTYPE)),
        grid=(nb,),
        in_specs=[pl.BlockSpec((TB, D), col(0)), pl.BlockSpec((TB, D), col(1 + d)), pl.BlockSpec((TB, D), col(3)),
                  _full((8, D))],
        out_specs=(pl.BlockSpec((TB, D), col(0)),
                   pl.BlockSpec((nch, NH, HF, HF), lambda s: (_blk(s, nb, rev), 0, 0, 0))),
        scratch=[pltpu.VMEM((NH, HF, HF), F32)], sem=("arbitrary",), vmem_mb=40,
    )(p_main, p_main, p_main, lbraw)


def _hgrn_bwd(p_main, lbraw, sprev, do, d, nb, prev):
    t_total = p_main.shape[0]
    rev = d == 1
    nch = TB // HC
    scale = HF ** -0.5
    last = prev is not None
    odt = MXU_DTYPE if last else F32

    def body(q_ref, f_ref, v_ref, lb_ref, sp_ref, do_ref, *rest):
        if last:
            dqp_ref, dvp_ref = rest[:2]
            rest = rest[2:]
        dq_ref, df_ref, dv_ref, dlb_ref, dst, s_dqd, s_dki, s_dke, s_dv, s_dtot = rest
        sp_id = pl.program_id(0)
        is_ctx = sp_id == nb - 1

        @pl.when(sp_id == 0)
        def _():
            dst[...] = jnp.zeros_like(dst)
            dlb_ref[...] = jnp.zeros_like(dlb_ref)

        mb = _tri(HC, rev)
        mbt = _tri(HC, not rev)
        m01 = _b01(mb)
        mt01 = _b01(mbt)
        for c in (range(nch) if rev else reversed(range(nch))):
            rows = slice(c * HC, (c + 1) * HC)
            fr = f_ref[rows, :]
            lb, sg, f = _hgrn_gate(fr, lb_ref, d)
            lf = jnp.log(f)
            k = 1.0 - f
            cum = _dot01(m01, lf)
            tot = cum[0:1, :] if rev else cum[HC - 1:HC, :]
            e = jnp.exp(cum)
            ei = jnp.exp(-cum)
            ee = jnp.exp(tot - cum)
            qraw = q_ref[rows, :]
            qd = _silu(qraw) * scale * e
            ki = k * ei
            ke = k * ee
            etot = jnp.exp(tot)
            v = v_ref[rows, :]
            dov = jnp.where(is_ctx, 0.0, do_ref[rows, :])
            for h in range(NH):
                cs = slice(h * HF, (h + 1) * HF)
                stin = sp_ref[c, h]
                dso = dst[h]
                qdh, kih, keh, vh, doh = qd[:, cs], ki[:, cs], ke[:, cs], v[:, cs], dov[:, cs]
                pt = jnp.where(mbt, _nt(kih, qdh), 0.0)
                dp = jnp.where(mb, _nt(doh, vh), 0.0)
                dpt = jnp.where(mbt, _nt(vh, doh), 0.0)
                s_dv[:, cs] = _nn(pt, doh) + _nt(keh, dso)
                s_dqd[:, cs] = _nn(dp, kih) + _nn(doh, stin)
                s_dki[:, cs] = _nn(dpt, qdh)
                s_dke[:, cs] = _nn(vh, dso)
                s_dtot[0:1, cs] = _colsum(dso * stin.astype(F32)) * etot[:, cs]
                dst[h] = dso * etot[:, cs] + _tn(doh, qdh)
            dqd, dki, dke = s_dqd[...], s_dki[...], s_dke[...]
            dcum = dqd * qd - dki * ki - dke * ke
            dtot = s_dtot[0:1, :] + _colsum(dke * ke)
            dk = dki * ei + dke * ee
            dlf = _dot01(mt01, dcum) + dtot
            df = dlf / f - dk
            dlb_ref[0:1, :] += _colsum(df * (1.0 - sg))
            dfr = df * (1.0 - lb) * sg * (1.0 - sg)
            dq = dqd * e * scale * _dsilu(qraw)
            dv = s_dv[...]
            if last:
                dq = dq + dqp_ref[rows, :]
                dv = dv + dvp_ref[rows, :]
            dq_ref[rows, :] = dq.astype(odt)
            dv_ref[rows, :] = dv.astype(odt)
            df_ref[rows, :] = dfr.astype(MXU_DTYPE)

    blk = lambda s: _blk(nb - 1 - s, nb, rev)
    col = lambda j: (lambda s: (blk(s), j))
    in_specs = [pl.BlockSpec((TB, D), col(0)), pl.BlockSpec((TB, D), col(1 + d)), pl.BlockSpec((TB, D), col(3)),
                _full((8, D)), pl.BlockSpec((nch, NH, HF, HF), lambda s: (blk(s), 0, 0, 0)),
                pl.BlockSpec((TB, D), lambda s: (jnp.minimum(blk(s), nb - 2), 0))]
    args = [p_main, p_main, p_main, lbraw, sprev, do]
    if last:
        in_specs += [pl.BlockSpec((TB, D), col(0))] * 2
        args += list(prev)
    return _pcall(
        body, name=f"hgrn_bwd_{d}",
        out_shape=(jax.ShapeDtypeStruct((t_total, D), odt), jax.ShapeDtypeStruct((t_total, D), MXU_DTYPE),
                   jax.ShapeDtypeStruct((t_total, D), odt), jax.ShapeDtypeStruct((8, D), F32)),
        grid=(nb,), in_specs=in_specs,
        out_specs=(pl.BlockSpec((TB, D), col(0)), pl.BlockSpec((TB, D), col(0)), pl.BlockSpec((TB, D), col(0)),
                   _full((8, D))),
        scratch=[pltpu.VMEM((NH, HF, HF), F32)] + [pltpu.VMEM((HC, D), F32)] * 4 + [pltpu.VMEM((8, D), F32)],
        sem=("arbitrary",), vmem_mb=48,
    )(*args)


def _conv_masks(tb, is_ctx):
    seg = jnp.where(is_ctx, tb, GRID_W)
    pos = lax.broadcasted_iota(jnp.int32, (tb, 1), 0) & (seg - 1)
    return pos, seg


def _shift_rows(x, dshift, pos, seg):
    if dshift == 0:
        return x
    n = x.shape[0]
    rolled = pltpu.roll(x, (-dshift) % n, 0)
    ok = (pos + dshift >= 0) & (pos + dshift < seg)
    return jnp.where(ok, rolled, 0.0)


def _ssd_prep(p_main, p_dt, convp, dtb, nb):
    t_total = p_main.shape[0]

    def body(x_ref, dt_ref, cw_ref, dtb_ref, xa_ref, dts_ref):
        is_ctx = pl.program_id(0) == nb - 1
        pos, seg = _conv_masks(TB, is_ctx)
        xv = x_ref[...]
        acc = cw_ref[5:6, :] + cw_ref[2:3, :] * xv
        for kk in (0, 1, 3, 4):
            acc = acc + cw_ref[kk:kk + 1, :] * _shift_rows(xv, kk - 2, pos, seg)
        xa_ref[...] = _silu(acc)
        dts_ref[...] = _softplus(dt_ref[...] + dtb_ref[0:1, :])

    return _pcall(
        body, name="ssd_prep",
        out_shape=(jax.ShapeDtypeStruct((t_total, 2048), F32), jax.ShapeDtypeStruct((t_total, 128), F32)),
        grid=(nb,),
        in_specs=[pl.BlockSpec((TB, 2048), lambda i: (i, 3)), pl.BlockSpec((TB, 128), lambda i: (i, 0)),
                  _full((8, 2048)), _full((8, 128))],
        out_specs=(pl.BlockSpec((TB, 2048), lambda i: (i, 0)), pl.BlockSpec((TB, 128), lambda i: (i, 0))),
        sem=("parallel",), vmem_mb=32,
    )(p_main, p_dt, convp, dtb)


def _ssd_prep_bwd(p_main, p_dt, convp, dtb, dxa, dxs_skip, ddts, nb):
    t_total = p_main.shape[0]

    def body(x_ref, dt_ref, cw_ref, dtb_ref, dxa_ref, dsk_ref, ddts_ref, dx_ref, ddt_ref, dcw_ref, ddtb_ref):
        i = pl.program_id(0)
        is_ctx = i == nb - 1

        @pl.when(i == 0)
        def _():
            dcw_ref[...] = jnp.zeros_like(dcw_ref)
            ddtb_ref[...] = jnp.zeros_like(ddtb_ref)

        pos, seg = _conv_masks(TB, is_ctx)
        xv = x_ref[...]
        sh = {kk: _shift_rows(xv, kk - 2, pos, seg) for kk in range(KCONV)}
        acc = cw_ref[5:6, :]
        for kk in range(KCONV):
            acc = acc + cw_ref[kk:kk + 1, :] * sh[kk]
        dact = dxa_ref[...]
        dact = jnp.concatenate([dact[:, :D] + jnp.where(is_ctx, 0.0, dsk_ref[...]), dact[:, D:]], axis=1)
        dpre = dact * _dsilu(acc)
        dxv = cw_ref[2:3, :] * dpre
        for kk in (0, 1, 3, 4):
            dxv = dxv + cw_ref[kk:kk + 1, :] * _shift_rows(dpre, 2 - kk, pos, seg)
        dx_ref[...] = dxv.astype(dx_ref.dtype)
        for kk in range(KCONV):
            dcw_ref[kk:kk + 1, :] += _colsum(dpre * sh[kk])
        dcw_ref[5:6, :] += _colsum(dpre)
        draw = ddts_ref[...] * _sig(dt_ref[...] + dtb_ref[0:1, :])
        ddt_ref[...] = draw.astype(ddt_ref.dtype)
        ddtb_ref[0:1, :] += _colsum(draw)

    return _pcall(
        body, name="ssd_prep_bwd",
        out_shape=(jax.ShapeDtypeStruct((t_total, 2048), MXU_DTYPE), jax.ShapeDtypeStruct((t_total, 128), MXU_DTYPE),
                   jax.ShapeDtypeStruct((8, 2048), F32), jax.ShapeDtypeStruct((8, 128), F32)),
        grid=(nb,),
        in_specs=[pl.BlockSpec((TB, 2048), lambda i: (i, 3)), pl.BlockSpec((TB, 128), lambda i: (i, 0)),
                  _full((8, 2048)), _full((8, 128)), pl.BlockSpec((TB, 2048), lambda i: (i, 0)),
                  pl.BlockSpec((TB, D), lambda i: (jnp.minimum(i, nb - 2), 0)),
                  pl.BlockSpec((TB, 128), lambda i: (i, 0))],
        out_specs=(pl.BlockSpec((TB, 2048), lambda i: (i, 0)), pl.BlockSpec((TB, 128), lambda i: (i, 0)),
                   _full((8, 2048)), _full((8, 128))),
        sem=("arbitrary",), vmem_mb=40,
    )(p_main, p_dt, convp, dtb, dxa, dxs_skip, ddts)


def _lane_pick(x, lane, col):
    return _rowsum(jnp.where(lane == col, x, 0.0))


def _ssd_chunk_common(dts, alog_ref, m01, rev):
    lane = lax.broadcasted_iota(jnp.int32, (1, 128), 1)
    arow = -jnp.exp(alog_ref[0:1, :])
    cum = _dot01(m01, dts * arow)
    tot = cum[0:1, :] if rev else cum[SC - 1:SC, :]
    return lane, arow, cum, cum.T, tot


def _ssd_fwd(xa, dts, alog, d, nb):
    t_total = xa.shape[0]
    rev = d == 1
    nch = TB // SC
    npair = SHEADS // 2

    def body(xa_ref, dts_ref, alog_ref, y_ref, sp_ref, st):
        s = pl.program_id(0)

        @pl.when(s == 0)
        def _():
            st[...] = jnp.zeros_like(st)

        mb = _tri(SC, rev)
        m01 = _b01(mb)
        lo = lax.broadcasted_iota(jnp.int32, (1, 128), 1) < SP
        rlo = lax.broadcasted_iota(jnp.int32, (128, 1), 0) < SP
        for c in (reversed(range(nch)) if rev else range(nch)):
            rows = slice(c * SC, (c + 1) * SC)
            dts_c = dts_ref[rows, :]
            lane, arow, cum, cumt, tot = _ssd_chunk_common(dts_c, alog_ref, m01, rev)
            for g in range(4):
                bg = xa_ref[rows, D + g * SN:D + (g + 1) * SN]
                cg = xa_ref[rows, D + 512 + g * SN:D + 512 + (g + 1) * SN]
                gmat = _nt(cg, bg)
                for pp in range(2):
                    pr = g * 2 + pp
                    xs = xa_ref[rows, pr * 128:(pr + 1) * 128]
                    cols = [16 * d + 2 * pr, 16 * d + 2 * pr + 1]
                    cum_c = [_lane_pick(cum, lane, q) for q in cols]
                    dt_c = [_lane_pick(dts_c, lane, q) for q in cols]
                    tot_c = [_lane_pick(tot, lane, q) for q in cols]
                    dt_pair = jnp.where(lo, dt_c[0], dt_c[1])
                    e1_pair = jnp.where(lo, jnp.exp(cum_c[0]), jnp.exp(cum_c[1]))
                    e2_pair = jnp.where(lo, jnp.exp(tot_c[0] - cum_c[0]), jnp.exp(tot_c[1] - cum_c[1]))
                    etot_col = jnp.where(rlo, jnp.exp(tot_c[0]), jnp.exp(tot_c[1]))
                    dtx = xs * dt_pair
                    stp = st[pr]
                    sp_ref[c, pr] = stp.astype(sp_ref.dtype)
                    y = e1_pair * _nt(cg, stp)
                    for q in range(2):
                        dec = jnp.where(mb, jnp.exp(cum_c[q] - cumt[cols[q]:cols[q] + 1, :]), 0.0)
                        y = y + _nn(gmat * dec, jnp.where(lo if q == 0 else ~lo, dtx, 0.0))
                    y_ref[rows, pr * 128:(pr + 1) * 128] = y
                    st[pr] = stp * etot_col + _tn(dtx * e2_pair, bg)

    blk = lambda s: _blk(s, nb, rev)
    return _pcall(
        body, name=f"ssd_fwd_{d}",
        out_shape=(jax.ShapeDtypeStruct((t_total, D), F32),
                   jax.ShapeDtypeStruct((nch * nb, npair, 128, SN), MXU_DTYPE)),
        grid=(nb,),
        in_specs=[pl.BlockSpec((TB, 2048), lambda s: (blk(s), 0)), pl.BlockSpec((TB, 128), lambda s: (blk(s), 0)),
                  _full((8, 128))],
        out_specs=(pl.BlockSpec((TB, D), lambda s: (blk(s), 0)),
                   pl.BlockSpec((nch, npair, 128, SN), lambda s: (blk(s), 0, 0, 0))),
        scratch=[pltpu.VMEM((npair, 128, SN), F32)], sem=("arbitrary",), vmem_mb=40,
    )(xa, dts, alog)


def _ssd_bwd(xa, dts, alog, sprev, dy, d, nb, prev):
    t_total = xa.shape[0]
    rev = d == 1
    nch = TB // SC
    npair = SHEADS // 2
    last = prev is not None

    def body(xa_ref, dts_ref, alog_ref, sp_ref, dy_ref, *rest):
        if last:
            dxp_ref, ddp_ref = rest[:2]
            rest = rest[2:]
        dxa_ref, ddts_ref, da_ref, dst = rest
        sp_id = pl.program_id(0)
        is_ctx = sp_id == nb - 1

        @pl.when(sp_id == 0)
        def _():
            dst[...] = jnp.zeros_like(dst)
            da_ref[...] = jnp.zeros_like(da_ref)

        mb = _tri(SC, rev)
        m01 = _b01(mb)
        mt01 = _b01(_tri(SC, not rev))
        lo = lax.broadcasted_iota(jnp.int32, (1, 128), 1) < SP
        rlo = lax.broadcasted_iota(jnp.int32, (128, 1), 0) < SP
        for c in (range(nch) if rev else reversed(range(nch))):
            rows = slice(c * SC, (c + 1) * SC)
            dts_c = dts_ref[rows, :]
            lane, arow, cum, cumt, tot = _ssd_chunk_common(dts_c, alog_ref, m01, rev)
            dcum = jnp.zeros((SC, 128), F32)
            ddt = jnp.zeros((SC, 128), F32)
            dtot = jnp.zeros((1, 128), F32)
            for g in range(4):
                bg = xa_ref[rows, D + g * SN:D + (g + 1) * SN]
                cg = xa_ref[rows, D + 512 + g * SN:D + 512 + (g + 1) * SN]
                gmat = _nt(cg, bg)
                dgm = jnp.zeros((SC, SC), F32)
                dbg = jnp.zeros((SC, SN), F32)
                dcg = jnp.zeros((SC, SN), F32)
                for pp in range(2):
                    pr = g * 2 + pp
                    xs = xa_ref[rows, pr * 128:(pr + 1) * 128]
                    dyp = jnp.where(is_ctx, 0.0, dy_ref[rows, pr * 128:(pr + 1) * 128])
                    cols = [16 * d + 2 * pr, 16 * d + 2 * pr + 1]
                    cum_c = [_lane_pick(cum, lane, q) for q in cols]
                    dt_c = [_lane_pick(dts_c, lane, q) for q in cols]
                    tot_c = [_lane_pick(tot, lane, q) for q in cols]
                    e1_c = [jnp.exp(cum_c[q]) for q in range(2)]
                    e2_c = [jnp.exp(tot_c[q] - cum_c[q]) for q in range(2)]
                    etot_c = [jnp.exp(tot_c[q]) for q in range(2)]
                    dt_pair = jnp.where(lo, dt_c[0], dt_c[1])
                    e1_pair = jnp.where(lo, e1_c[0], e1_c[1])
                    e2_pair = jnp.where(lo, e2_c[0], e2_c[1])
                    etot_col = jnp.where(rlo, etot_c[0], etot_c[1])
                    dtx = xs * dt_pair
                    stin = sp_ref[c, pr]
                    dso = dst[pr]
                    xe = dtx * e2_pair
                    dxe = _nt(bg, dso)
                    dbg = dbg + _nn(xe, dso)
                    ddtx = dxe * e2_pair
                    de2 = dxe * dtx
                    y0 = _nt(cg, stin)
                    dy0 = dyp * e1_pair
                    dcg = dcg + _nn(dy0, stin)
                    de1 = dyp * y0
                    rsum = _rowsum(dso * stin.astype(F32))
                    dst[pr] = dso * etot_col + _tn(dy0, cg)
                    for q in range(2):
                        hm = lo if q == 0 else ~lo
                        col = cols[q]
                        dec = jnp.where(mb, jnp.exp(cum_c[q] - cumt[col:col + 1, :]), 0.0)
                        w = gmat * dec
                        dyq = jnp.where(hm, dyp, 0.0)
                        dw = jnp.where(mb, _nt(dyq, dtx), 0.0)
                        ddtx = ddtx + jnp.where(hm, _tn(w, dyq), 0.0)
                        dgm = dgm + dw * dec
                        z = dw * w
                        de1q = _rowsum(jnp.where(hm, de1, 0.0))
                        de2q = _rowsum(jnp.where(hm, de2, 0.0))
                        dcum_q = _rowsum(z) - _rowsum(z.T) + de1q * e1_c[q] - de2q * e2_c[q]
                        rs = rsum[0:SP, :] if q == 0 else rsum[SP:2 * SP, :]
                        dtot_q = _colsum(de2q * e2_c[q]) + _colsum(rs) * etot_c[q]
                        dcum = jnp.where(lane == col, dcum_q, dcum)
                        dtot = jnp.where(lane == col, dtot_q, dtot)
                    dxs = ddtx * dt_pair
                    ddt_pair = ddtx * xs
                    for q in range(2):
                        hm = lo if q == 0 else ~lo
                        ddt = jnp.where(lane == cols[q], _rowsum(jnp.where(hm, ddt_pair, 0.0)), ddt)
                    if last:
                        dxs = dxs + dxp_ref[rows, pr * 128:(pr + 1) * 128]
                    dxa_ref[rows, pr * 128:(pr + 1) * 128] = dxs
                dbg = dbg + _tn(dgm, cg)
                dcg = dcg + _nn(dgm, bg)
                if last:
                    dbg = dbg + dxp_ref[rows, D + g * SN:D + (g + 1) * SN]
                    dcg = dcg + dxp_ref[rows, D + 512 + g * SN:D + 512 + (g + 1) * SN]
                dxa_ref[rows, D + g * SN:D + (g + 1) * SN] = dbg
                dxa_ref[rows, D + 512 + g * SN:D + 512 + (g + 1) * SN] = dcg
            dla = _dot01(mt01, dcum) + dtot
            ddt = ddt + dla * arow
            da_ref[0:1, :] += _colsum(dla * dts_c)
            if last:
                ddt = ddt + ddp_ref[rows, :]
            ddts_ref[rows, :] = ddt

    blk = lambda s: _blk(nb - 1 - s, nb, rev)
    in_specs = [pl.BlockSpec((TB, 2048), lambda s: (blk(s), 0)), pl.BlockSpec((TB, 128), lambda s: (blk(s), 0)),
                _full((8, 128)), pl.BlockSpec((nch, npair, 128, SN), lambda s: (blk(s), 0, 0, 0)),
                pl.BlockSpec((TB, D), lambda s: (jnp.minimum(blk(s), nb - 2), 0))]
    args = [xa, dts, alog, sprev, dy]
    if last:
        in_specs += [pl.BlockSpec((TB, 2048), lambda s: (blk(s), 0)), pl.BlockSpec((TB, 128), lambda s: (blk(s), 0))]
        args += list(prev)
    return _pcall(
        body, name=f"ssd_bwd_{d}",
        out_shape=(jax.ShapeDtypeStruct((t_total, 2048), F32), jax.ShapeDtypeStruct((t_total, 128), F32),
                   jax.ShapeDtypeStruct((8, 128), F32)),
        grid=(nb,), in_specs=in_specs,
        out_specs=(pl.BlockSpec((TB, 2048), lambda s: (blk(s), 0)), pl.BlockSpec((TB, 128), lambda s: (blk(s), 0)),
                   _full((8, 128))),
        scratch=[pltpu.VMEM((npair, 128, SN), F32)], sem=("arbitrary",), vmem_mb=48,
    )(*args)


def _readout(o, g, yy, z, vec_ref):
    hg, ss, keep = [], [], []
    for h in range(NH):
        cs = slice(h * HF, (h + 1) * HF)
        oh = o[:, cs]
        r = lax.rsqrt(jnp.mean(oh * oh, axis=1, keepdims=True) + EPS)
        hg.append(oh * r * vec_ref[0:1, cs] * _silu(g[:, cs]))
        keep.append(r)
    u = yy * _silu(z)
    for gi in range(4):
        cs = slice(gi * 256, (gi + 1) * 256)
        ug = u[:, cs]
        r = lax.rsqrt(jnp.mean(ug * ug, axis=1, keepdims=True) + EPS)
        ss.append(ug * r * vec_ref[2:3, cs])
        keep.append(r)
    return jnp.concatenate(hg, axis=1), jnp.concatenate(ss, axis=1), keep, u


def _mix_out(o_f, o_b, p_main, y_f, y_b, xa, x, vecs, w_out):
    n = x.shape[0]

    def body(of_ref, ob_ref, g_ref, z_ref, yf_ref, yb_ref, xs_ref, x_ref, vec_ref, w_ref,
             ymix_ref, ylat_ref, h1_ref, u2_ref):
        o = of_ref[...] + ob_ref[...]
        yy = yf_ref[...] + yb_ref[...] + vec_ref[1:2, :] * xs_ref[...]
        hg, ss, _, _ = _readout(o, g_ref[...], yy, z_ref[...], vec_ref)
        ymix = jnp.concatenate([hg, ss], axis=1).astype(MXU_DTYPE)
        ymix_ref[...] = ymix
        ylat = _nn(ymix, w_ref[...])
        ylat_ref[...] = ylat
        h1 = x_ref[...] + vec_ref[3:4, :] * ylat
        h1_ref[...] = h1
        r = lax.rsqrt(jnp.mean(h1 * h1, axis=1, keepdims=True) + EPS)
        u2_ref[...] = ((h1 * r * vec_ref[6:7, :]) * vec_ref[4:5, :] + vec_ref[5:6, :]).astype(MXU_DTYPE)

    row = lambda j: (lambda i: (i, j))
    return _pcall(
        body, name="mix_out",
        out_shape=(jax.ShapeDtypeStruct((n, 2 * D), MXU_DTYPE), jax.ShapeDtypeStruct((n, D), F32),
                   jax.ShapeDtypeStruct((n, D), F32), jax.ShapeDtypeStruct((n, D), MXU_DTYPE)),
        grid=(n // TB,),
        in_specs=[pl.BlockSpec((TB, D), row(0)), pl.BlockSpec((TB, D), row(0)), pl.BlockSpec((TB, D), row(4)),
                  pl.BlockSpec((TB, D), row(5)), pl.BlockSpec((TB, D), row(0)), pl.BlockSpec((TB, D), row(0)),
                  pl.BlockSpec((TB, D), row(0)), pl.BlockSpec((TB, D), row(0)), _full((8, D)), _full((2 * D, D))],
        out_specs=(pl.BlockSpec((TB, 2 * D), row(0)), pl.BlockSpec((TB, D), row(0)), pl.BlockSpec((TB, D), row(0)),
                   pl.BlockSpec((TB, D), row(0))),
        sem=("parallel",), vmem_mb=48,
    )(o_f, o_b, p_main, p_main, y_f, y_b, xa, x, vecs, w_out)


def _mix_bwd(dylat, o_f, o_b, p_main, y_f, y_b, xa, vecs, w_out):
    n = dylat.shape[0]

    def body(dyl_ref, of_ref, ob_ref, g_ref, z_ref, yf_ref, yb_ref, xs_ref, vec_ref, w_ref,
             do_ref, dg_ref, dys_ref, dz_ref, dxs_ref, acc_ref):
        i = pl.program_id(0)

        @pl.when(i == 0)
        def _():
            acc_ref[...] = jnp.zeros_like(acc_ref)

        dymix = _nt(dyl_ref[...], w_ref[...])
        o = of_ref[...] + ob_ref[...]
        g = g_ref[...]
        z = z_ref[...]
        xs = xs_ref[...]
        yy = yf_ref[...] + yb_ref[...] + vec_ref[1:2, :] * xs
        _, _, keep, u = _readout(o, g, yy, z, vec_ref)
        do_l, dg_l = [], []
        for h in range(NH):
            cs = slice(h * HF, (h + 1) * HF)
            oh, gh, r, wv = o[:, cs], g[:, cs], keep[h], vec_ref[0:1, cs]
            dhg = dymix[:, cs]
            xh = oh * r
            dn = dhg * _silu(gh)
            dg_l.append(dhg * xh * wv * _dsilu(gh))
            acc_ref[0:1, cs] += _colsum(dn * xh)
            dxh = dn * wv
            do_l.append(r * (dxh - xh * jnp.mean(dxh * xh, axis=1, keepdims=True)))
        du_l = []
        for gi in range(4):
            cs = slice(gi * 256, (gi + 1) * 256)
            ug, r, wv = u[:, cs], keep[NH + gi], vec_ref[2:3, cs]
            dss = dymix[:, D + gi * 256:D + (gi + 1) * 256]
            xh = ug * r
            acc_ref[2:3, cs] += _colsum(dss * xh)
            dxh = dss * wv
            du_l.append(r * (dxh - xh * jnp.mean(dxh * xh, axis=1, keepdims=True)))
        du = jnp.concatenate(du_l, axis=1)
        dyy = du * _silu(z)
        do_ref[...] = jnp.concatenate(do_l, axis=1)
        dg_ref[...] = jnp.concatenate(dg_l, axis=1).astype(dg_ref.dtype)
        dys_ref[...] = dyy
        dz_ref[...] = (du * yy * _dsilu(z)).astype(dz_ref.dtype)
        dxs_ref[...] = dyy * vec_ref[1:2, :]
        acc_ref[1:2, :] += _colsum(dyy * xs)

    row = lambda j: (lambda i: (i, j))
    return _pcall(
        body, name="mix_bwd",
        out_shape=(jax.ShapeDtypeStruct((n, D), F32), jax.ShapeDtypeStruct((n, D), MXU_DTYPE),
                   jax.ShapeDtypeStruct((n, D), F32), jax.ShapeDtypeStruct((n, D), MXU_DTYPE),
                   jax.ShapeDtypeStruct((n, D), F32), jax.ShapeDtypeStruct((8, D), F32)),
        grid=(n // TB,),
        in_specs=[pl.BlockSpec((TB, D), row(0)), pl.BlockSpec((TB, D), row(0)), pl.BlockSpec((TB, D), row(0)),
                  pl.BlockSpec((TB, D), row(4)), pl.BlockSpec((TB, D), row(5)), pl.BlockSpec((TB, D), row(0)),
                  pl.BlockSpec((TB, D), row(0)), pl.BlockSpec((TB, D), row(0)), _full((8, D)), _full((2 * D, D))],
        out_specs=(pl.BlockSpec((TB, D), row(0)),) * 5 + (_full((8, D)),),
        sem=("arbitrary",), vmem_mb=48,
    )(dylat, o_f, o_b, p_main, p_main, y_f, y_b, xa, vecs, w_out)


def _ffn_up(u2, w_gate, w_up):
    n = u2.shape[0]
    tb = 512

    def body(u_ref, wg_ref, wu_ref, g_ref, up_ref, a_ref):
        uv = u_ref[...]
        gt = _nn(uv, wg_ref[...])
        upv = _nn(uv, wu_ref[...])
        g_ref[...] = gt
        up_ref[...] = upv
        a_ref[...] = (_silu(gt) * upv).astype(a_ref.dtype)

    blk = pl.BlockSpec((tb, FSL), lambda i, j: (i, j))
    wblk = pl.BlockSpec((None, D, FSL), lambda i, j: (j, 0, 0))
    return _pcall(
        body, name="ffn_up",
        out_shape=(jax.ShapeDtypeStruct((n, DFFP), F32), jax.ShapeDtypeStruct((n, DFFP), F32),
                   jax.ShapeDtypeStruct((n, DFFP), MXU_DTYPE)),
        grid=(n // tb, 4), in_specs=[pl.BlockSpec((tb, D), lambda i, j: (i, 0)), wblk, wblk],
        out_specs=(blk, blk, blk), sem=("parallel", "parallel"), vmem_mb=48,
    )(u2, w_gate, w_up)


def _ffn_down_loss(act, w_down, h1, tgt, vecs):
    n = act.shape[0]
    tb = 512

    def body(a_ref, w_ref, h1_ref, t_ref, vec_ref, dh2_ref, dffn_ref, acc_ref):
        i = pl.program_id(0)

        @pl.when(i == 0)
        def _():
            acc_ref[...] = jnp.zeros_like(acc_ref)

        ffn = _nn(a_ref[...], w_ref[...])
        g2 = vec_ref[0:1, :]
        fw = vec_ref[1:2, :]
        h2 = h1_ref[...] + g2 * ffn
        r = lax.rsqrt(jnp.mean(h2 * h2, axis=1, keepdims=True) + EPS)
        xh = h2 * r
        err = xh * fw - t_ref[...]
        dy = err * (1.0 / D)
        acc_ref[2:3, :] += _colsum(err * err) * (0.5 / D)
        acc_ref[1:2, :] += _colsum(dy * xh)
        dxh = dy * fw
        dh2 = r * (dxh - xh * jnp.mean(dxh * xh, axis=1, keepdims=True))
        dh2_ref[...] = dh2
        dffn_ref[...] = (g2 * dh2).astype(dffn_ref.dtype)
        acc_ref[0:1, :] += _colsum(dh2 * ffn)

    return _pcall(
        body, name="ffn_down_loss",
        out_shape=(jax.ShapeDtypeStruct((n, D), F32), jax.ShapeDtypeStruct((n, D), MXU_DTYPE),
                   jax.ShapeDtypeStruct((8, D), F32)),
        grid=(n // tb,),
        in_specs=[pl.BlockSpec((tb, DFFP), lambda i: (i, 0)), _full((DFFP, D)), pl.BlockSpec((tb, D), lambda i: (i, 0)),
                  pl.BlockSpec((tb, D), lambda i: (i, 0)), _full((8, D))],
        out_specs=(pl.BlockSpec((tb, D), lambda i: (i, 0)), pl.BlockSpec((tb, D), lambda i: (i, 0)), _full((8, D))),
        sem=("arbitrary",), vmem_mb=48,
    )(act, w_down, h1, tgt, vecs)


def _ffn_bwd(dffn, w_down, gate, up, w_gate, w_up, h1, ylat, dh2, vecs):
    n = dffn.shape[0]
    tb = 256

    def body(df_ref, wd_ref, g_ref, up_ref, wg_ref, wu_ref, h1_ref, yl_ref, dh2_ref, vec_ref,
             dg_ref, dup_ref, dh1_ref, dyl_ref, acc_ref, du_scr):
        i, j = pl.program_id(0), pl.program_id(1)

        @pl.when((i == 0) & (j == 0))
        def _():
            acc_ref[...] = jnp.zeros_like(acc_ref)

        dact = _nt(df_ref[...], wd_ref[...])
        gt = g_ref[...]
        upv = up_ref[...]
        dgt = (dact * upv * _dsilu(gt)).astype(MXU_DTYPE)
        dupv = (dact * _silu(gt)).astype(MXU_DTYPE)
        dg_ref[...] = dgt
        dup_ref[...] = dupv
        part = _nt(dgt, wg_ref[...]) + _nt(dupv, wu_ref[...])

        @pl.when(j == 0)
        def _():
            du_scr[...] = part

        @pl.when((j > 0) & (j < 3))
        def _():
            du_scr[...] += part

        @pl.when(j == 3)
        def _():
            du = du_scr[...] + part
            h1 = h1_ref[...]
            r = lax.rsqrt(jnp.mean(h1 * h1, axis=1, keepdims=True) + EPS)
            xh = h1 * r
            nw = vec_ref[2:3, :]
            acc_ref[0:1, :] += _colsum(du)
            acc_ref[1:2, :] += _colsum(du * xh * nw)
            dn = du * vec_ref[1:2, :]
            acc_ref[2:3, :] += _colsum(dn * xh)
            dxh = dn * nw
            dh1 = dh2_ref[...] + r * (dxh - xh * jnp.mean(dxh * xh, axis=1, keepdims=True))
            dh1_ref[...] = dh1
            dyl_ref[...] = (vec_ref[0:1, :] * dh1).astype(dyl_ref.dtype)
            acc_ref[3:4, :] += _colsum(dh1 * yl_ref[...])

    tok = pl.BlockSpec((tb, D), lambda i, j: (i, 0))
    ffb = pl.BlockSpec((tb, FSL), lambda i, j: (i, j))
    wsl = pl.BlockSpec((None, D, FSL), lambda i, j: (j, 0, 0))
    return _pcall(
        body, name="ffn_bwd",
        out_shape=(jax.ShapeDtypeStruct((n, DFFP), MXU_DTYPE), jax.ShapeDtypeStruct((n, DFFP), MXU_DTYPE),
                   jax.ShapeDtypeStruct((n, D), F32), jax.ShapeDtypeStruct((n, D), MXU_DTYPE),
                   jax.ShapeDtypeStruct((8, D), F32)),
        grid=(n // tb, 4),
        in_specs=[tok, pl.BlockSpec((FSL, D), lambda i, j: (j, 0)), ffb, ffb, wsl, wsl,
                  tok, tok, tok, _full((8, D))],
        out_specs=(ffb, ffb, tok, tok, _full((8, D))),
        scratch=[pltpu.VMEM((tb, D), F32)], sem=("arbitrary", "arbitrary"), vmem_mb=48,
    )(dffn, w_down, gate, up, w_gate, w_up, h1, ylat, dh2, vecs)


def _dw(a, b, name, slabs=None):
    tn_rows = a.shape[0]
    bt = 512 if tn_rows % 512 == 0 else 256
    kk, nn_ = a.shape[1], b.shape[1]
    bk = 1024 if kk % 1024 == 0 else kk
    bn = slabs if slabs is not None else (1024 if nn_ % 1024 == 0 else nn_)
    nt = tn_rows // bt
    ka, nb_ = kk // bk, nn_ // bn

    def body(a_ref, b_ref, o_ref, acc):
        t = pl.program_id(2)
        part = _tn(a_ref[...], b_ref[...])

        @pl.when(t == 0)
        def _():
            acc[...] = part

        @pl.when(t > 0)
        def _():
            acc[...] += part

        @pl.when(t == nt - 1)
        def _():
            o_ref[...] = acc[...].astype(o_ref.dtype)

    if slabs is None:
        out_shape = jax.ShapeDtypeStruct((kk, nn_), MXU_DTYPE)
        out_spec = pl.BlockSpec((bk, bn), lambda i, j, t: (i, j))
    else:
        out_shape = jax.ShapeDtypeStruct((nb_, kk, bn), MXU_DTYPE)
        out_spec = pl.BlockSpec((None, bk, bn), lambda i, j, t: (j, i, 0))
    return _pcall(
        body, name=name, out_shape=out_shape, grid=(ka, nb_, nt),
        in_specs=[pl.BlockSpec((bt, bk), lambda i, j, t: (t, i)), pl.BlockSpec((bt, bn), lambda i, j, t: (t, j))],
        out_specs=out_spec, scratch=[pltpu.VMEM((bk, bn), F32)],
        sem=("parallel", "parallel", "arbitrary"), vmem_mb=40,
    )(a, b)


def _dw_in(u_all, segs, n_lat):
    t_total = u_all.shape[0]
    nt = t_total // TB
    nlat = n_lat // TB

    def body(u_ref, *refs):
        seg_refs, o_ref, acc = refs[:7], refs[7], refs[8]
        n, t = pl.program_id(0), pl.program_id(1)

        @pl.when(t == 0)
        def _():
            acc[...] = jnp.zeros_like(acc)

        for m in range(8):
            lat_only = m in (4, 5)

            @pl.when((n == m) & ((t < nlat) if lat_only else (t >= 0)))
            def _(m=m):
                acc[...] += _tn(u_ref[...], seg_refs[min(m, 6)][...])

        @pl.when(t == nt - 1)
        def _():
            o_ref[...] = acc[...].astype(o_ref.dtype)

    def seg_spec(m):
        lim = nlat - 1 if m in (4, 5) else nt - 1
        return pl.BlockSpec((TB, D), lambda n, t: (jnp.where(n == m, jnp.minimum(t, lim), 0), 0))

    in_specs = [pl.BlockSpec((TB, D), lambda n, t: (t, 0))] + [seg_spec(m) for m in range(6)]
    in_specs.append(pl.BlockSpec((TB, D), lambda n, t: (jnp.where(n >= 6, t, 0), jnp.where(n >= 6, n - 6, 0))))
    return _pcall(
        body, name="dw_in", out_shape=jax.ShapeDtypeStruct((4, D, 2 * D), MXU_DTYPE), grid=(8, nt),
        in_specs=in_specs, out_specs=pl.BlockSpec((None, D, D), lambda n, t: (n // 2, 0, n % 2)),
        scratch=[pltpu.VMEM((D, D), F32)], sem=("parallel", "arbitrary"), vmem_mb=40,
    )(u_all, *segs)


def _du_prenorm_bwd(segs, ddt, wi_main, wi_tail, xin, mods, dres, row_off, tb, name):
    n = xin.shape[0]
    nt = n // tb
    off = row_off // tb
    has_dx = dres is not None

    def body(*refs):
        seg_refs = refs[:7]
        ddt_ref, w_ref, wb_ref, wdt_ref, x_ref, mod_ref = refs[7:13]
        rest = refs[13:]
        if has_dx:
            dres_ref, dx_ref, acc_ref, du_scr = rest
        else:
            acc_ref, du_scr = rest
        i, j = pl.program_id(0), pl.program_id(1)

        @pl.when((i == 0) & (j == 0))
        def _():
            acc_ref[...] = jnp.zeros_like(acc_ref)

        @pl.when(j == 0)
        def _():
            du_scr[...] = _nt(ddt_ref[...], wdt_ref[...])

        for k in range(8):
            if not has_dx and k in (4, 5):
                continue

            @pl.when(j == k)
            def _(k=k):
                if k < 6:
                    sv = seg_refs[k][...]
                else:
                    sv = seg_refs[6][:, (k - 6) * D:(k - 5) * D]
                du_scr[...] += _nt(sv, w_ref[...])
                if k in (2, 4, 6):
                    du_scr[...] += _nt(sv[:, 0:WTAIL], wb_ref[...])

        @pl.when(j == 7)
        def _():
            du = du_scr[...]
            xv = x_ref[...]
            r = lax.rsqrt(jnp.mean(xv * xv, axis=1, keepdims=True) + EPS)
            xh = xv * r
            nw = mod_ref[1:2, :]
            acc_ref[0:1, :] += _colsum(du)
            acc_ref[1:2, :] += _colsum(du * xh * nw)
            dn = du * mod_ref[0:1, :]
            acc_ref[2:3, :] += _colsum(dn * xh)
            if has_dx:
                dxh = dn * nw
                dx_ref[...] = dres_ref[...] + r * (dxh - xh * jnp.mean(dxh * xh, axis=1, keepdims=True))

    tokT = lambda w: pl.BlockSpec((tb, w), lambda i, j: (i + off, 0))
    tokL = pl.BlockSpec((tb, D), lambda i, j: ((i + off) if has_dx else 0, 0))
    tok = pl.BlockSpec((tb, D), lambda i, j: (i, 0))
    in_specs = [tokT(D)] * 4 + [tokL] * 2 + [tokT(2 * D), tokT(128)] + _w_specs() + [tok, _full((8, D))]
    args = list(segs) + [ddt, wi_main, wi_tail, wi_tail, xin, mods]
    out_shape = [jax.ShapeDtypeStruct((8, D), F32)]
    out_specs = [_full((8, D))]
    if has_dx:
        in_specs.append(tok)
        args.append(dres)
        out_shape.insert(0, jax.ShapeDtypeStruct((n, D), F32))
        out_specs.insert(0, tok)
    return _pcall(
        body, name=name, out_shape=tuple(out_shape), grid=(nt, 8), in_specs=in_specs, out_specs=tuple(out_specs),
        scratch=[pltpu.VMEM((tb, D), F32)], sem=("arbitrary", "arbitrary"), vmem_mb=48,
    )(*args)


def _sum8(v):
    def body(v_ref, o_ref):
        acc = v_ref[0]
        for k in range(1, 8):
            acc = acc + v_ref[k]
        o_ref[...] = acc

    return _pcall(body, name="small_sum", out_shape=jax.ShapeDtypeStruct(v.shape[1:], F32),
                  in_specs=[pl.BlockSpec(memory_space=pltpu.VMEM)], out_specs=pl.BlockSpec(memory_space=pltpu.VMEM))(v)


def _adamw(w, m, v, g, name):
    rows, cols = w.shape
    rb = 256 if rows % 256 == 0 else (352 if rows % 352 == 0 else rows)
    c1 = 1.0 - B1 ** STEP
    c2 = 1.0 - B2 ** STEP

    def body(w_ref, m_ref, v_ref, g_ref, d_ref, nm_ref, nv_ref):
        gv = g_ref[...]
        mn = B1 * m_ref[...] + (1.0 - B1) * gv
        vn = B2 * v_ref[...] + (1.0 - B2) * (gv * gv)
        nm_ref[...] = mn
        nv_ref[...] = vn
        d_ref[...] = -LR * ((mn / c1) / (jnp.sqrt(vn / c2) + AEPS) + WD * w_ref[...])

    spec = pl.BlockSpec((rb, cols), lambda i: (i, 0))
    return _pcall(
        body, name=name, out_shape=(jax.ShapeDtypeStruct(w.shape, F32),) * 3, grid=(rows // rb,),
        in_specs=[spec] * 4, out_specs=(spec,) * 3, sem=("parallel",), vmem_mb=40,
    )(w, m, v, g)


def _rows(v, n):
    f = v.reshape(-1)
    return jnp.pad(f, (0, n * D - f.shape[0])).reshape(n, D)


def kernel(x, c, ctx, c_ctx, w_ada, b_ada, norm_mix, w_in, conv_w, conv_b, ssd_a_log, ssd_dt_bias, ssd_d, ssd_norm, hgrn_lb_raw, hgrn_norm, w_out, norm_ffn, w_gate, w_up, w_down, final_norm, loss_target, m_c_ctx, m_w_ada, m_b_ada, m_norm_mix, m_w_in, m_conv_w, m_conv_b, m_ssd_a_log, m_ssd_dt_bias, m_ssd_d, m_ssd_norm, m_hgrn_lb_raw, m_hgrn_norm, m_w_out, m_norm_ffn, m_w_gate, m_w_up, m_w_down, m_final_norm, v_c_ctx, v_w_ada, v_b_ada, v_norm_mix, v_w_in, v_conv_w, v_conv_b, v_ssd_a_log, v_ssd_dt_bias, v_ssd_d, v_ssd_norm, v_hgrn_lb_raw, v_hgrn_norm, v_w_out, v_norm_ffn, v_w_gate, v_w_up, v_w_down, v_final_norm):
    ix, iy, ic = lax.axis_index("x"), lax.axis_index("y"), lax.axis_index("c")
    chip = 2 * ix + iy
    me = 2 * chip + ic
    xl, xc, tgt = x[0], ctx[0], loss_target[0]
    n_lat, n_ctx = xl.shape[0], xc.shape[0]
    assert n_ctx == TB and n_lat % 1024 == 0
    t_total = n_lat + n_ctx
    nb = t_total // TB

    pack = jnp.concatenate([c, hgrn_lb_raw.reshape(1, D), _rows(conv_w[0], 3), jnp.zeros((3, D), F32)], axis=0)
    gath = _allgather8(pack, "small_gather").reshape(8, 8, D)
    c_all = gath[:, 0]
    lbraw_full = gath[0::2, 1].reshape(4, 2, 2, 256).transpose(1, 2, 0, 3).reshape(4, D)
    convw_full = gath[0::2, 2:5].reshape(4, 3 * D)[:, :KCONV * 512].reshape(4, KCONV, 512).transpose(1, 0, 2)
    convw_full = convw_full.reshape(KCONV, 2048)
    lbraw8 = jnp.pad(lbraw_full, ((0, 4), (0, 0)))
    convp = jnp.concatenate([convw_full, conv_b, jnp.zeros((2, 2048), F32)], axis=0)
    dtb = jnp.pad(ssd_dt_bias.reshape(1, 32), ((0, 7), (0, 96)))
    alog = jnp.pad(ssd_a_log.reshape(1, 32), ((0, 7), (0, 96)))

    araw = jnp.concatenate([c_all, c_ctx.reshape(1, D), jnp.zeros((7, D), F32)], axis=0)
    ncol_ada = w_ada.shape[2]
    b_shard = lax.dynamic_slice(b_ada, (0, chip * ncol_ada), (1, ncol_ada))
    mod_shard = _ada_fwd(araw, w_ada[0], b_shard)
    mod_all = _allgather8(mod_shard, "mod_gather").reshape(8, 16, ncol_ada)[0::2]
    mod_full = mod_all.transpose(1, 0, 2).reshape(16, 4 * ncol_ada)
    my_mod = lax.dynamic_slice(mod_full, (me, 0), (1, 6 * D)).reshape(6, D)
    sh1, sc1, g1, sh2, sc2, g2 = (my_mod[k:k + 1] for k in range(6))
    csh1, csc1 = mod_full[8:9, 0:D], mod_full[8:9, D:2 * D]

    slab = lax.dynamic_update_slice(jnp.zeros((D, WSL + WTAIL), F32), w_in[0], (0, 8 * chip)).astype(MXU_DTYPE)
    shards = [slab[:, :WSL], slab[:, WSL:], w_out[0].astype(MXU_DTYPE),
              jnp.pad(w_gate[0], ((0, 0), (0, FSL - DFF // 4))).astype(MXU_DTYPE),
              jnp.pad(w_up[0], ((0, 0), (0, FSL - DFF // 4))).astype(MXU_DTYPE),
              jnp.pad(w_down[0], ((0, FSL - DFF // 4), (0, 0))).astype(MXU_DTYPE)]
    gathered = _weights_allgather(shards)
    wi_main, wi_tail, wo_g, wg_g, wu_g, wd_g = (
        lax.dynamic_update_slice(g_, s_[None], (chip, 0, 0)) for g_, s_ in zip(gathered, shards))
    w_out_f = wo_g.reshape(2 * D, D)
    w_down_f = wd_g.reshape(DFFP, D)

    zrow = jnp.zeros((1, D), F32)
    mods_lat = jnp.concatenate([1.0 + sc1, sh1, norm_mix, zrow, zrow, zrow, zrow, zrow], axis=0)
    mods_ctx = jnp.concatenate([1.0 + csc1, csh1, norm_mix, zrow, zrow, zrow, zrow, zrow], axis=0)
    outs = _inproj(xl, mods_lat, wi_main, wi_tail, t_total, 1024, 0, None, "inproj_lat")
    p_main, p_dt, u_all = _inproj(xc, mods_ctx, wi_main, wi_tail, t_total, TB, nb - 1, outs, "inproj_ctx")

    o_f, hs_f = _hgrn_fwd(p_main, lbraw8, 0, nb)
    o_b, hs_b = _hgrn_fwd(p_main, lbraw8, 1, nb)
    xa, dts = _ssd_prep(p_main, p_dt, convp, dtb, nb)
    y_f, ss_f = _ssd_fwd(xa, dts, alog, 0, nb)
    y_b, ss_b = _ssd_fwd(xa, dts, alog, 1, nb)

    vec_mix = jnp.concatenate([jnp.tile(hgrn_norm, (1, NH)), jnp.repeat(ssd_d, SP, axis=1), ssd_norm, g1, 1.0 + sc2,
                               sh2, norm_ffn, zrow], axis=0)
    ymix, ylat, h1, u2 = _mix_out(o_f, o_b, p_main, y_f, y_b, xa, xl, vec_mix, w_out_f)
    gate, up, act = _ffn_up(u2, wg_g, wu_g)
    vec_loss = jnp.concatenate([g2, final_norm.reshape(1, D)] + [zrow] * 6, axis=0)
    dh2, dffn, acc_loss = _ffn_down_loss(act, w_down_f, h1, tgt, vec_loss)

    vec_ffn = jnp.concatenate([g1, 1.0 + sc2, norm_ffn] + [zrow] * 5, axis=0)
    dgate, dup, dh1, dylat, acc_ffn = _ffn_bwd(dffn, w_down_f, gate, up, wg_g, wu_g, h1, ylat, dh2, vec_ffn)
    gw_down = _dw(act, dffn, "dw_down").reshape(4, FSL, D)
    gw_gate = _dw(u2, dgate, "dw_gate", slabs=FSL)
    gw_up = _dw(u2, dup, "dw_up", slabs=FSL)
    do, dgr, dys, dzr, dxs_skip, acc_mix = _mix_bwd(dylat, o_f, o_b, p_main, y_f, y_b, xa, vec_mix, w_out_f)
    gw_out = _dw(ymix, dylat, "dw_out").reshape(4, D // 2, D)

    dq0, dff, dv0, dlb_f = _hgrn_bwd(p_main, lbraw8, hs_f, do, 0, nb, None)
    dq, dfb, dv, dlb_b = _hgrn_bwd(p_main, lbraw8, hs_b, do, 1, nb, (dq0, dv0))
    dxa0, ddts0, da_f = _ssd_bwd(xa, dts, alog, ss_f, dys, 0, nb, None)
    dxa, ddts, da_b = _ssd_bwd(xa, dts, alog, ss_b, dys, 1, nb, (dxa0, ddts0))
    dxbc, ddt, acc_conv, acc_dtb = _ssd_prep_bwd(p_main, p_dt, convp, dtb, dxa, dxs_skip, ddts, nb)

    segs = [dq, dff, dfb, dv, dgr, dzr, dxbc]
    bmods_lat = jnp.concatenate([1.0 + sc1, norm_mix] + [zrow] * 6, axis=0)
    bmods_ctx = jnp.concatenate([1.0 + csc1, norm_mix] + [zrow] * 6, axis=0)
    grad_x, acc_lat = _du_prenorm_bwd(segs, ddt, wi_main, wi_tail, xl, bmods_lat, dh1, 0, 512, "du_lat")
    (acc_ctx,) = _du_prenorm_bwd(segs, ddt, wi_main, wi_tail, xc, bmods_ctx, None, n_lat, TB, "du_ctx")
    gw_in_main = _dw_in(u_all, segs, n_lat)
    gw_in_dt = _dw(u_all, ddt, "dw_in_dt")
    gw_in_tail = jnp.concatenate([gw_in_main[1:, :, 0:WTAIL], gw_in_dt[None]], axis=0)

    core_arr = jnp.reshape(ic, (1,)).astype(jnp.int32)
    chip_arr = jnp.reshape(chip, (1,)).astype(jnp.int32)
    gs = [gw_in_main, gw_in_tail, gw_out, gw_gate, gw_up, gw_down]
    pair = _pair_sum(gs, _pair_exchange(gs), core_arr)
    mine = _chip_sum(pair, _chip_exchange(pair), chip_arr)
    theirs = _pair_swap(mine)
    whole = [jnp.concatenate([jnp.where(ic == 0, m_, t_), jnp.where(ic == 0, t_, m_)], axis=0)
             for m_, t_ in zip(mine, theirs)]
    g_w_in = lax.dynamic_slice(jnp.concatenate(whole[0:2], axis=1), (0, 8 * chip), (D, NSH))
    g_w_out = whole[2]
    g_w_gate = whole[3][:, :DFF // 4]
    g_w_up = whole[4][:, :DFF // 4]
    g_w_down = whole[5][:DFF // 4]

    dmod_lat = jnp.concatenate([acc_lat[0:2], acc_ffn[3:4], acc_ffn[0:2], acc_loss[0:1]], axis=0)
    misc = jnp.concatenate([(da_f + da_b)[0, :32], jnp.zeros((96,), F32), acc_dtb[0, :32], jnp.zeros((96,), F32),
                            jnp.sum(acc_loss[2]).reshape(1), jnp.zeros((D - 257,), F32)]).reshape(1, D)
    sv = jnp.concatenate([
        dmod_lat, acc_ctx[0:2], (acc_lat[2:3] + acc_ctx[2:3]), acc_ffn[2:3], acc_loss[1:2], acc_mix[2:3],
        acc_mix[0:1], acc_mix[1:2], dlb_f[0:1], dlb_b[0:1], acc_conv[0:6].reshape(12, D), misc,
        jnp.zeros((3, D), F32)], axis=0)
    sv_all = _allgather8(sv, "small_grads_gather").reshape(8, 32, D)
    ssum = _sum8(sv_all)
    dmod_rows = sv_all[:, 0:6].reshape(8, 6 * D)
    dmod_ctx_row = jnp.concatenate([ssum[6:8].reshape(1, 2 * D), jnp.zeros((1, 4 * D), F32)], axis=1)
    dmod_full = jnp.concatenate([dmod_rows, dmod_ctx_row, jnp.zeros((7, 6 * D), F32)], axis=0)
    grad_b_ada = jnp.sum(dmod_full, axis=0, keepdims=True)
    dmod_shard = lax.dynamic_slice(dmod_full, (0, chip * ncol_ada), (16, ncol_ada))
    g_w_ada, da_part = _ada_bwd(araw, dmod_shard, w_ada[0])
    da_all = _allgather8(da_part, "ada_ctx_gather").reshape(8, 16, D)[0::2, 8]
    cc = c_ctx.reshape(1, D)
    grad_c_ctx = (jnp.sum(da_all, axis=0, keepdims=True) * _dsilu(cc)).reshape(D)

    grad_norm_mix, grad_norm_ffn, grad_final_norm = ssum[8:9], ssum[9:10], ssum[10].reshape(D)
    grad_ssd_norm = ssum[11:12]
    grad_hgrn_norm = jnp.sum(ssum[12].reshape(NH, HF), axis=0, keepdims=True)
    grad_ssd_d = jnp.sum(ssum[13].reshape(SHEADS, SP), axis=1).reshape(1, SHEADS)
    lb_full = _sig(lbraw_full[0:2] - lbraw_full[2:4])
    dr0 = ssum[14:16] * lb_full * (1.0 - lb_full)
    grad_lb_full = jnp.stack([dr0, -dr0], axis=0)
    grad_lb = lax.dynamic_slice(grad_lb_full, (0, 0, chip * 256), (2, 2, 256))
    grad_conv_w = lax.dynamic_slice(ssum[16:26].reshape(KCONV, 2048), (0, chip * 512), (KCONV, 512)).reshape(1, KCONV, 512)
    grad_conv_b = ssum[26:28].reshape(1, 2048)
    a_val = -jnp.exp(ssd_a_log)
    grad_a_log = ssum[28, 0:32].reshape(1, 2, SHEADS) * a_val
    grad_dt_bias = ssum[28, 128:160].reshape(1, 2, SHEADS)
    loss = ssum[28, 256]

    small_w = [c_ctx, b_ada, norm_mix, conv_w, conv_b, ssd_a_log, ssd_dt_bias, ssd_d, ssd_norm, hgrn_lb_raw,
               hgrn_norm, norm_ffn, final_norm]
    small_m = [m_c_ctx, m_b_ada, m_norm_mix, m_conv_w, m_conv_b, m_ssd_a_log, m_ssd_dt_bias, m_ssd_d, m_ssd_norm,
               m_hgrn_lb_raw, m_hgrn_norm, m_norm_ffn, m_final_norm]
    small_v = [v_c_ctx, v_b_ada, v_norm_mix, v_conv_w, v_conv_b, v_ssd_a_log, v_ssd_dt_bias, v_ssd_d, v_ssd_norm,
               v_hgrn_lb_raw, v_hgrn_norm, v_norm_ffn, v_final_norm]
    small_g = [grad_c_ctx, grad_b_ada, grad_norm_mix, grad_conv_w, grad_conv_b, grad_a_log, grad_dt_bias, grad_ssd_d,
               grad_ssd_norm, grad_lb, grad_hgrn_norm, grad_norm_ffn, grad_final_norm]
    nrows = [-(-a.size // D) for a in small_w]
    packs = lambda lst: jnp.concatenate([_rows(a, r) for a, r in zip(lst, nrows)]
                                        + [jnp.zeros((24 - sum(nrows), D), F32)], axis=0)
    sd, sm, svv = _adamw(packs(small_w), packs(small_m), packs(small_v), packs(small_g), "adamw_small")

    def unpack(p):
        out, r0 = [], 0
        for a, r in zip(small_w, nrows):
            out.append(p[r0:r0 + r].reshape(-1)[:a.size].reshape(a.shape))
            r0 += r
        return out

    sd, sm, svv = unpack(sd), unpack(sm), unpack(svv)
    big = {}
    for nm, w_, m_, v_, g_ in (("w_ada", w_ada, m_w_ada, v_w_ada, g_w_ada), ("w_in", w_in, m_w_in, v_w_in, g_w_in),
                               ("w_out", w_out, m_w_out, v_w_out, g_w_out),
                               ("w_gate", w_gate, m_w_gate, v_w_gate, g_w_gate),
                               ("w_up", w_up, m_w_up, v_w_up, g_w_up),
                               ("w_down", w_down, m_w_down, v_w_down, g_w_down)):
        dl, nm_, nv_ = _adamw(w_[0], m_[0], v_[0], g_, "adamw_" + nm)
        big[nm] = (g_[None], dl[None], nm_[None], nv_[None])

    order = ["c_ctx", "w_ada", "b_ada", "norm_mix", "w_in", "conv_w", "conv_b", "ssd_a_log", "ssd_dt_bias", "ssd_d",
             "ssd_norm", "hgrn_lb_raw", "hgrn_norm", "w_out", "norm_ffn", "w_gate", "w_up", "w_down", "final_norm"]
    small_names = ["c_ctx", "b_ada", "norm_mix", "conv_w", "conv_b", "ssd_a_log", "ssd_dt_bias", "ssd_d", "ssd_norm",
                   "hgrn_lb_raw", "hgrn_norm", "norm_ffn", "final_norm"]
    table = dict(big)
    for k, nm in enumerate(small_names):
        table[nm] = (small_g[k].reshape(small_w[k].shape), sd[k], sm[k], svv[k])
    grads = [table[nm][0] for nm in order]
    deltas = [table[nm][1] for nm in order]
    new_m = [table[nm][2] for nm in order]
    new_v = [table[nm][3] for nm in order]
    return (loss, grad_x[None], *grads, *deltas, *new_m, *new_v)
```

```python
import functools
import math

import jax
import jax.numpy as jnp
from jax import lax
from jax.experimental import pallas as pl
from jax.experimental.pallas import tpu as pltpu

F32 = jnp.float32
BF16 = jnp.bfloat16
MXU_DTYPE = jnp.bfloat16
_INTERPRET = False

D = 1024
NH, HF = 8, 128
HC = 64
SC = 128
SN = 128
SHEADS, SP = 16, 64
GRID_W = 64
KCONV = 5
DFF = 2816
FSL = 768
DFFP = 4 * FSL
NIN = 8224
TB = 256
EPS = 1e-6
LR, B1, B2, AEPS, WD, STEP = 0.001, 0.9, 0.999, 1e-08, 0.01, 10
MESH_ID = pl.DeviceIdType.MESH
NSH = NIN // 4
WSL = 2048
WTAIL = 128


def _pcall(body, *, name, out_shape, grid=(), in_specs=None, out_specs=None, scratch=(), sem=None,
           vmem_mb=None, aliases=None):
    params = {}
    if sem is not None:
        params["dimension_semantics"] = sem
    if vmem_mb is not None:
        params["vmem_limit_bytes"] = vmem_mb << 20
    kw = dict(name=name, out_shape=out_shape, scratch_shapes=list(scratch),
              input_output_aliases=aliases or {}, compiler_params=pltpu.CompilerParams(**params),
              interpret=_INTERPRET)
    if grid:
        kw["grid"] = grid
    if in_specs is not None:
        kw["in_specs"] = in_specs
    if out_specs is not None:
        kw["out_specs"] = out_specs
    return pl.pallas_call(body, **kw)


def _mx(a):
    return a.astype(MXU_DTYPE)


def _dg(a, b, ca, cb):
    return lax.dot_general(_mx(a), _mx(b), (((ca,), (cb,)), ((), ())), preferred_element_type=F32)


def _nn(a, b):
    return _dg(a, b, 1, 0)


def _nt(a, b):
    return _dg(a, b, 1, 1)


def _tn(a, b):
    return _dg(a, b, 0, 0)


def _dot01(m, x):
    hi = x.astype(BF16)
    r1 = x - hi.astype(F32)
    mid = r1.astype(BF16)
    lo = (r1 - mid.astype(F32)).astype(BF16)
    f = lambda t: lax.dot_general(m, t, (((1,), (0,)), ((), ())), preferred_element_type=F32)
    return f(hi) + f(mid) + f(lo)


def _tri(n, upper):
    r = lax.broadcasted_iota(jnp.int32, (n, n), 0)
    c = lax.broadcasted_iota(jnp.int32, (n, n), 1)
    return (c >= r) if upper else (c <= r)


def _b01(mask):
    return jnp.where(mask, 1.0, 0.0).astype(BF16)


def _sig(x):
    return jax.nn.sigmoid(x)


def _silu(x):
    return x * _sig(x)


def _dsilu(x):
    s = _sig(x)
    return s * (1.0 + x * (1.0 - s))


def _softplus(x):
    return jnp.maximum(x, 0.0) + jnp.log(1.0 + jnp.exp(-jnp.abs(x)))


def _rowsum(x):
    return jnp.sum(x, axis=1, keepdims=True)


def _colsum(x):
    return jnp.sum(x, axis=0, keepdims=True)


def _full(shape):
    return pl.BlockSpec(shape, lambda *_: (0,) * len(shape))


def _allgather8(v, name):
    m_per, n = v.shape

    def body(x_ref, out_ref, send_sems, recv_sems, local_sem):
        x, y, c = lax.axis_index("x"), lax.axis_index("y"), lax.axis_index("c")
        me, sibling = (x, y, c), (x, y, 1 - c)
        chips = [(1 - x, y), (x, 1 - y), (1 - x, 1 - y)]

        def rows(px, py, pc):
            return out_ref.at[pl.ds((4 * px + 2 * py + pc) * m_per, m_per), :]

        def copy(k, block, to, src=None):
            return pltpu.make_async_remote_copy(
                src_ref=rows(*block) if src is None else src, dst_ref=rows(*block),
                send_sem=send_sems.at[k], recv_sem=recv_sems.at[k], device_id=to, device_id_type=MESH_ID)

        mine = pltpu.make_async_copy(x_ref, rows(*me), local_sem)
        mine.start()
        first = [copy(0, me, sibling, src=x_ref)]
        first += [copy(1 + j, me, (*chip, c), src=x_ref) for j, chip in enumerate(chips)]
        for cp in first:
            cp.start()
        passed = [copy(4 + j, (*chip, c), sibling) for j, chip in enumerate(chips)]
        for j, chip in enumerate(chips):
            copy(1 + j, (*chip, c), me).wait_recv()
            passed[j].start()
        copy(0, sibling, me).wait_recv()
        for j, chip in enumerate(chips):
            copy(4 + j, (*chip, 1 - c), me).wait_recv()
        for cp in first + passed:
            cp.wait_send()
        mine.wait()

    return _pcall(
        body, name=name, out_shape=jax.ShapeDtypeStruct((8 * m_per, n), v.dtype),
        in_specs=[pl.BlockSpec(memory_space=pltpu.VMEM)], out_specs=pl.BlockSpec(memory_space=pltpu.VMEM),
        scratch=[pltpu.SemaphoreType.DMA((7,)), pltpu.SemaphoreType.DMA((7,)), pltpu.SemaphoreType.DMA],
    )(v)


def _gather_ops(ins, outs, send_sems, recv_sems):
    n = len(ins)
    x, y, c = lax.axis_index("x"), lax.axis_index("y"), lax.axis_index("c")
    sibling = (x, y, 1 - c)
    chips = [(1 - x, y), (x, 1 - y), (1 - x, 1 - y)]

    def part(a, px, py, pc):
        half = ins[a].shape[0] // 2
        return outs[a].at[2 * px + py, pl.ds(pc * half, half), :]

    def copy(a, k, block, to, src=None):
        return pltpu.make_async_remote_copy(
            src_ref=part(a, *block) if src is None else src, dst_ref=part(a, *block),
            send_sem=send_sems.at[6 * a + k], recv_sem=recv_sems.at[6 * a + k], device_id=to,
            device_id_type=MESH_ID)

    def first(a, j):
        half = ins[a].shape[0] // 2
        return copy(a, j, (x, y, c), (*chips[j], c), src=ins[a].at[pl.ds(c * half, half), :])

    def start():
        for a in range(n):
            for j in range(3):
                first(a, j).start()

    def finish():
        for a in range(n):
            for j, chip in enumerate(chips):
                copy(a, j, (*chip, c), (x, y, c)).wait_recv()
                copy(a, 3 + j, (*chip, c), sibling).start()
        for a in range(n):
            for j, chip in enumerate(chips):
                copy(a, 3 + j, (*chip, 1 - c), (x, y, c)).wait_recv()
        for a in range(n):
            for j, chip in enumerate(chips):
                first(a, j).wait_send()
                copy(a, 3 + j, (*chip, c), sibling).wait_send()

    return start, finish


def _gather_out(shards):
    return tuple(jax.ShapeDtypeStruct((4,) + s_.shape, s_.dtype) for s_ in shards)


def _gather_sems(n):
    return [pltpu.SemaphoreType.DMA((6 * n,)), pltpu.SemaphoreType.DMA((6 * n,))]


def _weights_allgather(shards):
    n = len(shards)

    def body(*refs):
        start, finish = _gather_ops(refs[:n], refs[n:2 * n], *refs[2 * n:])
        start()
        finish()

    return _pcall(
        body, name="weights_allgather", out_shape=_gather_out(shards),
        in_specs=[pl.BlockSpec(memory_space=pl.ANY)] * n, out_specs=(pl.BlockSpec(memory_space=pl.ANY),) * n,
        scratch=_gather_sems(n),
    )(*shards)


def _pair_exchange(gs, name):
    n = len(gs)

    def body(*refs):
        ins, outs = refs[:n], refs[n:2 * n]
        send_sems, recv_sems = refs[2 * n:]
        x, y, c = lax.axis_index("x"), lax.axis_index("y"), lax.axis_index("c")
        cps = []
        for a in range(n):
            half = ins[a].shape[1] // 2
            cps.append(pltpu.make_async_remote_copy(
                src_ref=ins[a].at[:, pl.ds((1 - c) * half, half), :], dst_ref=outs[a], send_sem=send_sems.at[a],
                recv_sem=recv_sems.at[a], device_id=(x, y, 1 - c), device_id_type=MESH_ID))
        for cp in cps:
            cp.start()
        for cp in cps:
            cp.wait()

    return _pcall(
        body, name=name,
        out_shape=tuple(jax.ShapeDtypeStruct((g.shape[0], g.shape[1] // 2, g.shape[2]), g.dtype) for g in gs),
        in_specs=[pl.BlockSpec(memory_space=pl.ANY)] * n, out_specs=(pl.BlockSpec(memory_space=pl.ANY),) * n,
        scratch=[pltpu.SemaphoreType.DMA((n,)), pltpu.SemaphoreType.DMA((n,))],
    )(*gs)


def _exchange_ops(ins, outs, send_sems, recv_sems, dests):
    x, y, c = lax.axis_index("x"), lax.axis_index("y"), lax.axis_index("c")
    mine = 2 * x + y
    chips = [(1 - x, y), (x, 1 - y), (1 - x, 1 - y)]

    def each(fn):
        for a in range(len(ins)):
            lo, hi = dests[a]
            for j, (px, py) in enumerate(chips):
                q = 2 * px + py
                cp = pltpu.make_async_remote_copy(
                    src_ref=ins[a].at[jnp.clip(q - lo, 0, hi - lo - 1)], dst_ref=outs[a].at[j],
                    send_sem=send_sems.at[3 * a + j], recv_sem=recv_sems.at[3 * a + j], device_id=(px, py, c),
                    device_id_type=MESH_ID)
                fn(cp, (q >= lo) & (q < hi), (mine >= lo) & (mine < hi), (lo, hi) == (0, 4))

    def start():
        def go(cp, send_ok, recv_ok, always):
            if always:
                cp.start()
            else:
                pl.when(send_ok)(cp.start)
        each(go)

    def finish():
        def go(cp, send_ok, recv_ok, always):
            if always:
                cp.wait()
            else:
                pl.when(send_ok)(cp.wait_send)
                pl.when(recv_ok)(cp.wait_recv)
        each(go)

    return start, finish


def _comm_exchange(hs, dests):
    n = len(hs)
    return (list(hs), tuple(jax.ShapeDtypeStruct((3,) + h.shape[1:], h.dtype) for h in hs),
            [pltpu.SemaphoreType.DMA((3 * n,)), pltpu.SemaphoreType.DMA((3 * n,))],
            lambda i, o, s, r: _exchange_ops(i, o, s, r, dests))


def _comm_gather(shards):
    return (list(shards), _gather_out(shards), _gather_sems(len(shards)), _gather_ops)


def _carry(call, comm, steps):
    if comm is None:
        return call
    arrays, out_shape, sems, make = comm
    n, n_in, n_out = len(arrays), len(call["args"]), len(call["out_shape"])
    body = call["body"]

    def wrapped(*refs):
        base_in, cin = refs[:n_in], refs[n_in:n_in + n]
        rest = refs[n_in + n:]
        base_out, cout, scr = rest[:n_out], rest[n_out:n_out + n], rest[n_out + n:]
        start, finish = make(cin, cout, scr[-2], scr[-1])
        first, last = steps()
        pl.when(first)(start)
        body(*base_in, *base_out, *scr[:-2])
        pl.when(last)(finish)

    anyspec = pl.BlockSpec(memory_space=pl.ANY)
    return dict(call, body=wrapped, args=list(call["args"]) + arrays,
                in_specs=list(call["in_specs"]) + [anyspec] * n,
                out_shape=tuple(call["out_shape"]) + tuple(out_shape),
                out_specs=tuple(call["out_specs"]) + (anyspec,) * n,
                scratch=list(call["scratch"]) + sems)


def _run(call):
    args = call.pop("args")
    body = call.pop("body")
    return _pcall(body, **call)(*args)


def _pair_swap(rs):
    n = len(rs)

    def body(*refs):
        ins, outs = refs[:n], refs[n:2 * n]
        send_sems, recv_sems = refs[2 * n:]
        x, y, c = lax.axis_index("x"), lax.axis_index("y"), lax.axis_index("c")
        cps = [pltpu.make_async_remote_copy(
            src_ref=ins[a], dst_ref=outs[a], send_sem=send_sems.at[a], recv_sem=recv_sems.at[a],
            device_id=(x, y, 1 - c), device_id_type=MESH_ID) for a in range(n)]
        for cp in cps:
            cp.start()
        for cp in cps:
            cp.wait()

    return _pcall(
        body, name="grads_pair_swap", out_shape=tuple(jax.ShapeDtypeStruct(r.shape, r.dtype) for r in rs),
        in_specs=[pl.BlockSpec(memory_space=pl.ANY)] * n, out_specs=(pl.BlockSpec(memory_space=pl.ANY),) * n,
        scratch=[pltpu.SemaphoreType.DMA((n,)), pltpu.SemaphoreType.DMA((n,))],
    )(*rs)


SUM_STEPS = 8


def _pair_sum(gs, recvs, core, name):
    n = len(gs)

    def body(c_ref, *refs):
        for a in range(n):
            refs[2 * n + a][...] = (refs[a][...].astype(F32) + refs[n + a][...].astype(F32)).astype(refs[2 * n + a].dtype)

    blk = lambda g: (g.shape[0], g.shape[1] // (2 * SUM_STEPS), g.shape[2])
    return pl.pallas_call(
        body, name=name,
        out_shape=tuple(jax.ShapeDtypeStruct((g.shape[0], g.shape[1] // 2, g.shape[2]), g.dtype) for g in gs),
        grid_spec=pltpu.PrefetchScalarGridSpec(
            num_scalar_prefetch=1, grid=(SUM_STEPS,),
            in_specs=[pl.BlockSpec(blk(g), lambda i, cr: (0, cr[0] * SUM_STEPS + i, 0)) for g in gs]
            + [pl.BlockSpec(blk(g), lambda i, cr: (0, i, 0)) for g in gs],
            out_specs=tuple(pl.BlockSpec(blk(g), lambda i, cr: (0, i, 0)) for g in gs)),
        compiler_params=pltpu.CompilerParams(vmem_limit_bytes=40 << 20), interpret=_INTERPRET,
    )(core, *gs, *recvs)


def _chip_sum(hs, recvs, chip, dests, slots):
    n = len(hs)
    nout = max(slots) + 1
    first = [slots.index(o) for o in range(nout)]
    every = lambda d_: d_ == (0, 4)

    def own(d_):
        if every(d_):
            return lambda i, kr: (kr[0], i, 0)
        return lambda i, kr: (0, jnp.where(kr[0] == d_[0], i, 0), 0)

    def got(d_):
        if every(d_):
            return lambda i, kr: (0, i, 0)
        return lambda i, kr: (0, jnp.where(kr[0] == d_[0], i, 0), 0)

    def body(k_ref, *refs):
        for a in range(n):
            def emit(a=a):
                acc = refs[a][0].astype(F32)
                for j in range(3):
                    acc = acc + refs[n + a][j].astype(F32)
                refs[2 * n + slots[a]][...] = acc
            if every(dests[a]):
                emit()
            else:
                pl.when(k_ref[0] == dests[a][0])(emit)

    rb = lambda h: h.shape[1] // SUM_STEPS
    return pl.pallas_call(
        body, name="grads_chip_sum",
        out_shape=tuple(jax.ShapeDtypeStruct(hs[a].shape[1:], F32) for a in first),
        grid_spec=pltpu.PrefetchScalarGridSpec(
            num_scalar_prefetch=1, grid=(SUM_STEPS,),
            in_specs=[pl.BlockSpec((1, rb(h), h.shape[2]), own(d_)) for h, d_ in zip(hs, dests)]
            + [pl.BlockSpec((3, rb(h), h.shape[2]), got(d_)) for h, d_ in zip(hs, dests)],
            out_specs=tuple(pl.BlockSpec((rb(hs[a]), hs[a].shape[2]), lambda i, kr: (i, 0)) for a in first)),
        compiler_params=pltpu.CompilerParams(vmem_limit_bytes=40 << 20), interpret=_INTERPRET,
    )(chip, *hs, *recvs)


def _ada_fwd(araw, w, b):
    nblk = w.shape[1] // 512

    def body(a_ref, w_ref, b_ref, o_ref):
        o_ref[...] = _nn(_silu(a_ref[...]), w_ref[...]) + b_ref[...]

    return _pcall(
        body, name="ada_fwd", out_shape=jax.ShapeDtypeStruct((16, w.shape[1]), F32), grid=(nblk,),
        in_specs=[_full((16, D)), pl.BlockSpec((D, 512), lambda j: (0, j)), pl.BlockSpec((1, 512), lambda j: (0, j))],
        out_specs=pl.BlockSpec((16, 512), lambda j: (0, j)), sem=("parallel",),
    )(araw, w, b)


def _ada_bwd(araw, dmod, w):
    nblk = w.shape[1] // 512

    def body(a_ref, d_ref, w_ref, gw_ref, da_ref):
        j = pl.program_id(0)
        gw_ref[...] = _tn(_silu(a_ref[...]), d_ref[...])
        part = _nt(d_ref[...], w_ref[...])

        @pl.when(j == 0)
        def _():
            da_ref[...] = part

        @pl.when(j > 0)
        def _():
            da_ref[...] += part

    return _pcall(
        body, name="ada_bwd",
        out_shape=(jax.ShapeDtypeStruct(w.shape, F32), jax.ShapeDtypeStruct((16, D), F32)), grid=(nblk,),
        in_specs=[_full((16, D)), pl.BlockSpec((16, 512), lambda j: (0, j)), pl.BlockSpec((D, 512), lambda j: (0, j))],
        out_specs=(pl.BlockSpec((D, 512), lambda j: (0, j)), _full((16, D))), sem=("arbitrary",),
    )(araw, dmod, w)


def _w_specs():
    return [pl.BlockSpec((None, D, D), lambda i, j: (j // 2, 0, j % 2)),
            pl.BlockSpec((None, D, WTAIL), lambda i, j: (jnp.maximum(j // 2 - 1, 0), 0, 0)),
            pl.BlockSpec((None, D, WTAIL), lambda i, j: (3, 0, 0))]


def _inproj(xin, mods, wi_main, wi_tail, t_total, tb, blk_off, prev, name, comm=None):
    n = xin.shape[0]
    nt = n // tb
    ncol = 8

    def body(x_ref, mod_ref, w_ref, wb_ref, wdt_ref, *rest):
        p_ref, pdt_ref, u_ref, uscr = rest[-4:]
        j = pl.program_id(1)

        @pl.when(j == 0)
        def _():
            xv = x_ref[...]
            r = lax.rsqrt(jnp.mean(xv * xv, axis=1, keepdims=True) + EPS)
            u = (xv * r * mod_ref[2:3, :]) * mod_ref[0:1, :] + mod_ref[1:2, :]
            ub = u.astype(MXU_DTYPE)
            uscr[...] = ub
            u_ref[...] = ub
            pdt_ref[...] = _nn(ub, wdt_ref[...])

        p_ref[...] = _nn(uscr[...], w_ref[...])

        @pl.when((j % 2 == 0) & (j > 0))
        def _():
            p_ref[:, 0:WTAIL] += _nn(uscr[...], wb_ref[...])

    in_specs = [pl.BlockSpec((tb, D), lambda i, j: (i, 0)), _full((8, D))] + _w_specs()
    args = [xin, mods, wi_main, wi_tail, wi_tail]
    aliases = None
    if prev is not None:
        in_specs += [pl.BlockSpec(memory_space=pl.ANY)] * 3
        args += list(prev)
        aliases = {5: 0, 6: 1, 7: 2}
    call = dict(
        body=body, args=args, name=name,
        out_shape=(jax.ShapeDtypeStruct((t_total, ncol * D), F32), jax.ShapeDtypeStruct((t_total, 128), F32),
                   jax.ShapeDtypeStruct((t_total, D), MXU_DTYPE)),
        grid=(nt, ncol), in_specs=in_specs,
        out_specs=(pl.BlockSpec((tb, D), lambda i, j: (i + blk_off, j)),
                   pl.BlockSpec((tb, 128), lambda i, j: (i + blk_off, 0)),
                   pl.BlockSpec((tb, D), lambda i, j: (i + blk_off, 0))),
        scratch=[pltpu.VMEM((tb, D), MXU_DTYPE)], sem=("arbitrary", "arbitrary"), vmem_mb=48, aliases=aliases)
    steps = lambda: ((pl.program_id(0) == 0) & (pl.program_id(1) == 0),
                     (pl.program_id(0) == nt - 1) & (pl.program_id(1) == ncol - 1))
    return _run(_carry(call, comm, steps))


def _blk(s, nb, rev):
    return jnp.where(s == 0, nb - 1, (nb - 1 - s) if rev else (s - 1))


def _hgrn_gate(fr, lbraw_ref, d):
    lb = _sig(lbraw_ref[d:d + 1, :] - lbraw_ref[2 + d:3 + d, :])
    sg = _sig(fr)
    return lb, sg, lb + (1.0 - lb) * sg


def _hgrn_fwd(p_main, lbraw, d, nb):
    t_total = p_main.shape[0]
    rev = d == 1
    nch = TB // HC
    scale = HF ** -0.5

    def body(q_ref, f_ref, v_ref, lb_ref, o_ref, sp_ref, st):
        s = pl.program_id(0)

        @pl.when(s == 0)
        def _():
            st[...] = jnp.zeros_like(st)

        mb = _tri(HC, rev)
        m01 = _b01(mb)
        for c in (reversed(range(nch)) if rev else range(nch)):
            rows = slice(c * HC, (c + 1) * HC)
            _, _, f = _hgrn_gate(f_ref[rows, :], lb_ref, d)
            lf = jnp.log(f)
            k = 1.0 - f
            cum = _dot01(m01, lf)
            tot = cum[0:1, :] if rev else cum[HC - 1:HC, :]
            qd = _silu(q_ref[rows, :]) * scale * jnp.exp(cum)
            ki = k * jnp.exp(-cum)
            etot = jnp.exp(tot)
            ke = ki * etot
            v = v_ref[rows, :]
            for h in range(NH):
                cs = slice(h * HF, (h + 1) * HF)
                sth = st[h]
                sp_ref[c, h] = sth.astype(sp_ref.dtype)
                sc = jnp.where(mb, _nt(qd[:, cs], ki[:, cs]), 0.0)
                o_ref[rows, cs] = _nn(sc, v[:, cs]) + _nt(qd[:, cs], sth)
                st[h] = sth * etot[:, cs] + _tn(v[:, cs], ke[:, cs])

    col = lambda j: (lambda s: (_blk(s, nb, rev), j))
    return _pcall(
        body, name=f"hgrn_fwd_{d}",
        out_shape=(jax.ShapeDtypeStruct((t_total, D), F32),
                   jax.ShapeDtypeStruct((nch * nb, NH, HF, HF), MXU_DTYPE)),
        grid=(nb,),
        in_specs=[pl.BlockSpec((TB, D), col(0)), pl.BlockSpec((TB, D), col(1 + d)), pl.BlockSpec((TB, D), col(3)),
                  _full((8, D))],
        out_specs=(pl.BlockSpec((TB, D), col(0)),
                   pl.BlockSpec((nch, NH, HF, HF), lambda s: (_blk(s, nb, rev), 0, 0, 0))),
        scratch=[pltpu.VMEM((NH, HF, HF), F32)], sem=("arbitrary",), vmem_mb=40,
    )(p_main, p_main, p_main, lbraw)


def _hgrn_bwd(p_main, lbraw, sprev, do, d, nb, prev, comm=None):
    t_total = p_main.shape[0]
    rev = d == 1
    nch = TB // HC
    scale = HF ** -0.5
    last = prev is not None
    odt = MXU_DTYPE if last else F32

    def body(q_ref, f_ref, v_ref, lb_ref, sp_ref, do_ref, *rest):
        if last:
            dqp_ref, dvp_ref = rest[:2]
            rest = rest[2:]
        dq_ref, df_ref, dv_ref, dlb_ref, dst, s_dqd, s_dki, s_dke, s_dv, s_dtot = rest
        sp_id = pl.program_id(0)
        is_ctx = sp_id == nb - 1

        @pl.when(sp_id == 0)
        def _():
            dst[...] = jnp.zeros_like(dst)
            dlb_ref[...] = jnp.zeros_like(dlb_ref)

        mb = _tri(HC, rev)
        mbt = _tri(HC, not rev)
        m01 = _b01(mb)
        mt01 = _b01(mbt)
        for c in (range(nch) if rev else reversed(range(nch))):
            rows = slice(c * HC, (c + 1) * HC)
            fr = f_ref[rows, :]
            lb, sg, f = _hgrn_gate(fr, lb_ref, d)
            lf = jnp.log(f)
            k = 1.0 - f
            cum = _dot01(m01, lf)
            tot = cum[0:1, :] if rev else cum[HC - 1:HC, :]
            e = jnp.exp(cum)
            ei = jnp.exp(-cum)
            ee = jnp.exp(tot - cum)
            qraw = q_ref[rows, :]
            qd = _silu(qraw) * scale * e
            ki = k * ei
            ke = k * ee
            etot = jnp.exp(tot)
            v = v_ref[rows, :]
            dov = jnp.where(is_ctx, 0.0, do_ref[rows, :])
            for h in range(NH):
                cs = slice(h * HF, (h + 1) * HF)
                stin = sp_ref[c, h]
                dso = dst[h]
                qdh, kih, keh, vh, doh = qd[:, cs], ki[:, cs], ke[:, cs], v[:, cs], dov[:, cs]
                pt = jnp.where(mbt, _nt(kih, qdh), 0.0)
                dp = jnp.where(mb, _nt(doh, vh), 0.0)
                dpt = jnp.where(mbt, _nt(vh, doh), 0.0)
                s_dv[:, cs] = _nn(pt, doh) + _nt(keh, dso)
                s_dqd[:, cs] = _nn(dp, kih) + _nn(doh, stin)
                s_dki[:, cs] = _nn(dpt, qdh)
                s_dke[:, cs] = _nn(vh, dso)
                s_dtot[0:1, cs] = _colsum(dso * stin.astype(F32)) * etot[:, cs]
                dst[h] = dso * etot[:, cs] + _tn(doh, qdh)
            dqd, dki, dke = s_dqd[...], s_dki[...], s_dke[...]
            dcum = dqd * qd - dki * ki - dke * ke
            dtot = s_dtot[0:1, :] + _colsum(dke * ke)
            dk = dki * ei + dke * ee
            dlf = _dot01(mt01, dcum) + dtot
            df = dlf / f - dk
            dlb_ref[0:1, :] += _colsum(df * (1.0 - sg))
            dfr = df * (1.0 - lb) * sg * (1.0 - sg)
            dq = dqd * e * scale * _dsilu(qraw)
            dv = s_dv[...]
            if last:
                dq = dq + dqp_ref[rows, :]
                dv = dv + dvp_ref[rows, :]
            dq_ref[rows, :] = dq.astype(odt)
            dv_ref[rows, :] = dv.astype(odt)
            df_ref[rows, :] = dfr.astype(MXU_DTYPE)

    blk = lambda s: _blk(nb - 1 - s, nb, rev)
    col = lambda j: (lambda s: (blk(s), j))
    in_specs = [pl.BlockSpec((TB, D), col(0)), pl.BlockSpec((TB, D), col(1 + d)), pl.BlockSpec((TB, D), col(3)),
                _full((8, D)), pl.BlockSpec((nch, NH, HF, HF), lambda s: (blk(s), 0, 0, 0)),
                pl.BlockSpec((TB, D), lambda s: (jnp.minimum(blk(s), nb - 2), 0))]
    args = [p_main, p_main, p_main, lbraw, sprev, do]
    if last:
        in_specs += [pl.BlockSpec((TB, D), col(0))] * 2
        args += list(prev)
    call = dict(
        body=body, args=args, name=f"hgrn_bwd_{d}",
        out_shape=(jax.ShapeDtypeStruct((t_total, D), odt), jax.ShapeDtypeStruct((t_total, D), MXU_DTYPE),
                   jax.ShapeDtypeStruct((t_total, D), odt), jax.ShapeDtypeStruct((8, D), F32)),
        grid=(nb,), in_specs=in_specs,
        out_specs=(pl.BlockSpec((TB, D), col(0)), pl.BlockSpec((TB, D), col(0)), pl.BlockSpec((TB, D), col(0)),
                   _full((8, D))),
        scratch=[pltpu.VMEM((NH, HF, HF), F32)] + [pltpu.VMEM((HC, D), F32)] * 4 + [pltpu.VMEM((8, D), F32)],
        sem=("arbitrary",), vmem_mb=48)
    return _run(_carry(call, comm, lambda: (pl.program_id(0) == 0, pl.program_id(0) == nb - 1)))


def _conv_masks(tb, is_ctx):
    seg = jnp.where(is_ctx, tb, GRID_W)
    pos = lax.broadcasted_iota(jnp.int32, (tb, 1), 0) & (seg - 1)
    return pos, seg


def _shift_rows(x, dshift, pos, seg):
    if dshift == 0:
        return x
    n = x.shape[0]
    rolled = pltpu.roll(x, (-dshift) % n, 0)
    ok = (pos + dshift >= 0) & (pos + dshift < seg)
    return jnp.where(ok, rolled, 0.0)


def _ssd_prep(p_main, p_dt, convp, dtb, nb):
    t_total = p_main.shape[0]

    def body(x_ref, dt_ref, cw_ref, dtb_ref, xa_ref, dts_ref):
        is_ctx = pl.program_id(0) == nb - 1
        pos, seg = _conv_masks(TB, is_ctx)
        xv = x_ref[...]
        acc = cw_ref[5:6, :] + cw_ref[2:3, :] * xv
        for kk in (0, 1, 3, 4):
            acc = acc + cw_ref[kk:kk + 1, :] * _shift_rows(xv, kk - 2, pos, seg)
        xa_ref[...] = _silu(acc)
        dts_ref[...] = _softplus(dt_ref[...] + dtb_ref[0:1, :])

    return _pcall(
        body, name="ssd_prep",
        out_shape=(jax.ShapeDtypeStruct((t_total, 2048), F32), jax.ShapeDtypeStruct((t_total, 128), F32)),
        grid=(nb,),
        in_specs=[pl.BlockSpec((TB, 2048), lambda i: (i, 3)), pl.BlockSpec((TB, 128), lambda i: (i, 0)),
                  _full((8, 2048)), _full((8, 128))],
        out_specs=(pl.BlockSpec((TB, 2048), lambda i: (i, 0)), pl.BlockSpec((TB, 128), lambda i: (i, 0))),
        sem=("parallel",), vmem_mb=32,
    )(p_main, p_dt, convp, dtb)


def _ssd_prep_bwd(p_main, p_dt, convp, dtb, dxa, dxs_skip, ddts, nb):
    t_total = p_main.shape[0]

    def body(x_ref, dt_ref, cw_ref, dtb_ref, dxa_ref, dsk_ref, ddts_ref, dx_ref, ddt_ref, dcw_ref, ddtb_ref):
        i = pl.program_id(0)
        is_ctx = i == nb - 1

        @pl.when(i == 0)
        def _():
            dcw_ref[...] = jnp.zeros_like(dcw_ref)
            ddtb_ref[...] = jnp.zeros_like(ddtb_ref)

        pos, seg = _conv_masks(TB, is_ctx)
        xv = x_ref[...]
        sh = {kk: _shift_rows(xv, kk - 2, pos, seg) for kk in range(KCONV)}
        acc = cw_ref[5:6, :]
        for kk in range(KCONV):
            acc = acc + cw_ref[kk:kk + 1, :] * sh[kk]
        dact = dxa_ref[...]
        dact = jnp.concatenate([dact[:, :D] + jnp.where(is_ctx, 0.0, dsk_ref[...]), dact[:, D:]], axis=1)
        dpre = dact * _dsilu(acc)
        dxv = cw_ref[2:3, :] * dpre
        for kk in (0, 1, 3, 4):
            dxv = dxv + cw_ref[kk:kk + 1, :] * _shift_rows(dpre, 2 - kk, pos, seg)
        dx_ref[...] = dxv.astype(dx_ref.dtype)
        for kk in range(KCONV):
            dcw_ref[kk:kk + 1, :] += _colsum(dpre * sh[kk])
        dcw_ref[5:6, :] += _colsum(dpre)
        draw = ddts_ref[...] * _sig(dt_ref[...] + dtb_ref[0:1, :])
        ddt_ref[...] = draw.astype(ddt_ref.dtype)
        ddtb_ref[0:1, :] += _colsum(draw)

    return _pcall(
        body, name="ssd_prep_bwd",
        out_shape=(jax.ShapeDtypeStruct((t_total, 2048), MXU_DTYPE), jax.ShapeDtypeStruct((t_total, 128), MXU_DTYPE),
                   jax.ShapeDtypeStruct((8, 2048), F32), jax.ShapeDtypeStruct((8, 128), F32)),
        grid=(nb,),
        in_specs=[pl.BlockSpec((TB, 2048), lambda i: (i, 3)), pl.BlockSpec((TB, 128), lambda i: (i, 0)),
                  _full((8, 2048)), _full((8, 128)), pl.BlockSpec((TB, 2048), lambda i: (i, 0)),
                  pl.BlockSpec((TB, D), lambda i: (jnp.minimum(i, nb - 2), 0)),
                  pl.BlockSpec((TB, 128), lambda i: (i, 0))],
        out_specs=(pl.BlockSpec((TB, 2048), lambda i: (i, 0)), pl.BlockSpec((TB, 128), lambda i: (i, 0)),
                   _full((8, 2048)), _full((8, 128))),
        sem=("arbitrary",), vmem_mb=40,
    )(p_main, p_dt, convp, dtb, dxa, dxs_skip, ddts)


def _lane_pick(x, lane, col):
    return _rowsum(jnp.where(lane == col, x, 0.0))


def _ssd_chunk_common(dts, alog_ref, m01, rev):
    lane = lax.broadcasted_iota(jnp.int32, (1, 128), 1)
    arow = -jnp.exp(alog_ref[0:1, :])
    cum = _dot01(m01, dts * arow)
    tot = cum[0:1, :] if rev else cum[SC - 1:SC, :]
    return lane, arow, cum, cum.T, tot


def _ssd_fwd(xa, dts, alog, d, nb):
    t_total = xa.shape[0]
    rev = d == 1
    nch = TB // SC
    npair = SHEADS // 2

    def body(xa_ref, dts_ref, alog_ref, y_ref, sp_ref, st):
        s = pl.program_id(0)

        @pl.when(s == 0)
        def _():
            st[...] = jnp.zeros_like(st)

        mb = _tri(SC, rev)
        m01 = _b01(mb)
        lo = lax.broadcasted_iota(jnp.int32, (1, 128), 1) < SP
        rlo = lax.broadcasted_iota(jnp.int32, (128, 1), 0) < SP
        for c in (reversed(range(nch)) if rev else range(nch)):
            rows = slice(c * SC, (c + 1) * SC)
            dts_c = dts_ref[rows, :]
            lane, arow, cum, cumt, tot = _ssd_chunk_common(dts_c, alog_ref, m01, rev)
            for g in range(4):
                bg = xa_ref[rows, D + g * SN:D + (g + 1) * SN]
                cg = xa_ref[rows, D + 512 + g * SN:D + 512 + (g + 1) * SN]
                gmat = _nt(cg, bg)
                for pp in range(2):
                    pr = g * 2 + pp
                    xs = xa_ref[rows, pr * 128:(pr + 1) * 128]
                    cols = [16 * d + 2 * pr, 16 * d + 2 * pr + 1]
                    cum_c = [_lane_pick(cum, lane, q) for q in cols]
                    dt_c = [_lane_pick(dts_c, lane, q) for q in cols]
                    tot_c = [_lane_pick(tot, lane, q) for q in cols]
                    dt_pair = jnp.where(lo, dt_c[0], dt_c[1])
                    e1_pair = jnp.where(lo, jnp.exp(cum_c[0]), jnp.exp(cum_c[1]))
                    e2_pair = jnp.where(lo, jnp.exp(tot_c[0] - cum_c[0]), jnp.exp(tot_c[1] - cum_c[1]))
                    etot_col = jnp.where(rlo, jnp.exp(tot_c[0]), jnp.exp(tot_c[1]))
                    dtx = xs * dt_pair
                    stp = st[pr]
                    sp_ref[c, pr] = stp.astype(sp_ref.dtype)
                    y = e1_pair * _nt(cg, stp)
                    for q in range(2):
                        dec = jnp.where(mb, jnp.exp(cum_c[q] - cumt[cols[q]:cols[q] + 1, :]), 0.0)
                        y = y + _nn(gmat * dec, jnp.where(lo if q == 0 else ~lo, dtx, 0.0))
                    y_ref[rows, pr * 128:(pr + 1) * 128] = y
                    st[pr] = stp * etot_col + _tn(dtx * e2_pair, bg)

    blk = lambda s: _blk(s, nb, rev)
    return _pcall(
        body, name=f"ssd_fwd_{d}",
        out_shape=(jax.ShapeDtypeStruct((t_total, D), F32),
                   jax.ShapeDtypeStruct((nch * nb, npair, 128, SN), MXU_DTYPE)),
        grid=(nb,),
        in_specs=[pl.BlockSpec((TB, 2048), lambda s: (blk(s), 0)), pl.BlockSpec((TB, 128), lambda s: (blk(s), 0)),
                  _full((8, 128))],
        out_specs=(pl.BlockSpec((TB, D), lambda s: (blk(s), 0)),
                   pl.BlockSpec((nch, npair, 128, SN), lambda s: (blk(s), 0, 0, 0))),
        scratch=[pltpu.VMEM((npair, 128, SN), F32)], sem=("arbitrary",), vmem_mb=40,
    )(xa, dts, alog)


def _ssd_bwd(xa, dts, alog, sprev, dy, d, nb, prev, comm=None):
    t_total = xa.shape[0]
    rev = d == 1
    nch = TB // SC
    npair = SHEADS // 2
    last = prev is not None

    def body(xa_ref, dts_ref, alog_ref, sp_ref, dy_ref, *rest):
        if last:
            dxp_ref, ddp_ref = rest[:2]
            rest = rest[2:]
        dxa_ref, ddts_ref, da_ref, dst = rest
        sp_id = pl.program_id(0)
        is_ctx = sp_id == nb - 1

        @pl.when(sp_id == 0)
        def _():
            dst[...] = jnp.zeros_like(dst)
            da_ref[...] = jnp.zeros_like(da_ref)

        mb = _tri(SC, rev)
        m01 = _b01(mb)
        mt01 = _b01(_tri(SC, not rev))
        lo = lax.broadcasted_iota(jnp.int32, (1, 128), 1) < SP
        rlo = lax.broadcasted_iota(jnp.int32, (128, 1), 0) < SP
        for c in (range(nch) if rev else reversed(range(nch))):
            rows = slice(c * SC, (c + 1) * SC)
            dts_c = dts_ref[rows, :]
            lane, arow, cum, cumt, tot = _ssd_chunk_common(dts_c, alog_ref, m01, rev)
            dcum = jnp.zeros((SC, 128), F32)
            ddt = jnp.zeros((SC, 128), F32)
            dtot = jnp.zeros((1, 128), F32)
            for g in range(4):
                bg = xa_ref[rows, D + g * SN:D + (g + 1) * SN]
                cg = xa_ref[rows, D + 512 + g * SN:D + 512 + (g + 1) * SN]
                gmat = _nt(cg, bg)
                dgm = jnp.zeros((SC, SC), F32)
                dbg = jnp.zeros((SC, SN), F32)
                dcg = jnp.zeros((SC, SN), F32)
                for pp in range(2):
                    pr = g * 2 + pp
                    xs = xa_ref[rows, pr * 128:(pr + 1) * 128]
                    dyp = jnp.where(is_ctx, 0.0, dy_ref[rows, pr * 128:(pr + 1) * 128])
                    cols = [16 * d + 2 * pr, 16 * d + 2 * pr + 1]
                    cum_c = [_lane_pick(cum, lane, q) for q in cols]
                    dt_c = [_lane_pick(dts_c, lane, q) for q in cols]
                    tot_c = [_lane_pick(tot, lane, q) for q in cols]
                    e1_c = [jnp.exp(cum_c[q]) for q in range(2)]
                    e2_c = [jnp.exp(tot_c[q] - cum_c[q]) for q in range(2)]
                    etot_c = [jnp.exp(tot_c[q]) for q in range(2)]
                    dt_pair = jnp.where(lo, dt_c[0], dt_c[1])
                    e1_pair = jnp.where(lo, e1_c[0], e1_c[1])
                    e2_pair = jnp.where(lo, e2_c[0], e2_c[1])
                    etot_col = jnp.where(rlo, etot_c[0], etot_c[1])
                    dtx = xs * dt_pair
                    stin = sp_ref[c, pr]
                    dso = dst[pr]
                    xe = dtx * e2_pair
                    dxe = _nt(bg, dso)
                    dbg = dbg + _nn(xe, dso)
                    ddtx = dxe * e2_pair
                    de2 = dxe * dtx
                    y0 = _nt(cg, stin)
                    dy0 = dyp * e1_pair
                    dcg = dcg + _nn(dy0, stin)
                    de1 = dyp * y0
                    rsum = _rowsum(dso * stin.astype(F32))
                    dst[pr] = dso * etot_col + _tn(dy0, cg)
                    for q in range(2):
                        hm = lo if q == 0 else ~lo
                        col = cols[q]
                        dec = jnp.where(mb, jnp.exp(cum_c[q] - cumt[col:col + 1, :]), 0.0)
                        w = gmat * dec
                        dyq = jnp.where(hm, dyp, 0.0)
                        dw = jnp.where(mb, _nt(dyq, dtx), 0.0)
                        ddtx = ddtx + jnp.where(hm, _tn(w, dyq), 0.0)
                        dgm = dgm + dw * dec
                        z = dw * w
                        de1q = _rowsum(jnp.where(hm, de1, 0.0))
                        de2q = _rowsum(jnp.where(hm, de2, 0.0))
                        dcum_q = _rowsum(z) - _rowsum(z.T) + de1q * e1_c[q] - de2q * e2_c[q]
                        rs = rsum[0:SP, :] if q == 0 else rsum[SP:2 * SP, :]
                        dtot_q = _colsum(de2q * e2_c[q]) + _colsum(rs) * etot_c[q]
                        dcum = jnp.where(lane == col, dcum_q, dcum)
                        dtot = jnp.where(lane == col, dtot_q, dtot)
                    dxs = ddtx * dt_pair
                    ddt_pair = ddtx * xs
                    for q in range(2):
                        hm = lo if q == 0 else ~lo
                        ddt = jnp.where(lane == cols[q], _rowsum(jnp.where(hm, ddt_pair, 0.0)), ddt)
                    if last:
                        dxs = dxs + dxp_ref[rows, pr * 128:(pr + 1) * 128]
                    dxa_ref[rows, pr * 128:(pr + 1) * 128] = dxs
                dbg = dbg + _tn(dgm, cg)
                dcg = dcg + _nn(dgm, bg)
                if last:
                    dbg = dbg + dxp_ref[rows, D + g * SN:D + (g + 1) * SN]
                    dcg = dcg + dxp_ref[rows, D + 512 + g * SN:D + 512 + (g + 1) * SN]
                dxa_ref[rows, D + g * SN:D + (g + 1) * SN] = dbg
                dxa_ref[rows, D + 512 + g * SN:D + 512 + (g + 1) * SN] = dcg
            dla = _dot01(mt01, dcum) + dtot
            ddt = ddt + dla * arow
            da_ref[0:1, :] += _colsum(dla * dts_c)
            if last:
                ddt = ddt + ddp_ref[rows, :]
            ddts_ref[rows, :] = ddt

    blk = lambda s: _blk(nb - 1 - s, nb, rev)
    in_specs = [pl.BlockSpec((TB, 2048), lambda s: (blk(s), 0)), pl.BlockSpec((TB, 128), lambda s: (blk(s), 0)),
                _full((8, 128)), pl.BlockSpec((nch, npair, 128, SN), lambda s: (blk(s), 0, 0, 0)),
                pl.BlockSpec((TB, D), lambda s: (jnp.minimum(blk(s), nb - 2), 0))]
    args = [xa, dts, alog, sprev, dy]
    if last:
        in_specs += [pl.BlockSpec((TB, 2048), lambda s: (blk(s), 0)), pl.BlockSpec((TB, 128), lambda s: (blk(s), 0))]
        args += list(prev)
    call = dict(
        body=body, args=args, name=f"ssd_bwd_{d}",
        out_shape=(jax.ShapeDtypeStruct((t_total, 2048), F32), jax.ShapeDtypeStruct((t_total, 128), F32),
                   jax.ShapeDtypeStruct((8, 128), F32)),
        grid=(nb,), in_specs=in_specs,
        out_specs=(pl.BlockSpec((TB, 2048), lambda s: (blk(s), 0)), pl.BlockSpec((TB, 128), lambda s: (blk(s), 0)),
                   _full((8, 128))),
        scratch=[pltpu.VMEM((npair, 128, SN), F32)], sem=("arbitrary",), vmem_mb=48)
    return _run(_carry(call, comm, lambda: (pl.program_id(0) == 0, pl.program_id(0) == nb - 1)))


def _readout(o, g, yy, z, vec_ref):
    hg, ss, keep = [], [], []
    for h in range(NH):
        cs = slice(h * HF, (h + 1) * HF)
        oh = o[:, cs]
        r = lax.rsqrt(jnp.mean(oh * oh, axis=1, keepdims=True) + EPS)
        hg.append(oh * r * vec_ref[0:1, cs] * _silu(g[:, cs]))
        keep.append(r)
    u = yy * _silu(z)
    for gi in range(4):
        cs = slice(gi * 256, (gi + 1) * 256)
        ug = u[:, cs]
        r = lax.rsqrt(jnp.mean(ug * ug, axis=1, keepdims=True) + EPS)
        ss.append(ug * r * vec_ref[2:3, cs])
        keep.append(r)
    return jnp.concatenate(hg, axis=1), jnp.concatenate(ss, axis=1), keep, u


def _mix_out(o_f, o_b, p_main, y_f, y_b, xa, x, vecs, w_out):
    n = x.shape[0]

    def body(of_ref, ob_ref, g_ref, z_ref, yf_ref, yb_ref, xs_ref, x_ref, vec_ref, w_ref,
             ymix_ref, ylat_ref, h1_ref, u2_ref):
        o = of_ref[...] + ob_ref[...]
        yy = yf_ref[...] + yb_ref[...] + vec_ref[1:2, :] * xs_ref[...]
        hg, ss, _, _ = _readout(o, g_ref[...], yy, z_ref[...], vec_ref)
        ymix = jnp.concatenate([hg, ss], axis=1).astype(MXU_DTYPE)
        ymix_ref[...] = ymix
        ylat = _nn(ymix, w_ref[...])
        ylat_ref[...] = ylat
        h1 = x_ref[...] + vec_ref[3:4, :] * ylat
        h1_ref[...] = h1
        r = lax.rsqrt(jnp.mean(h1 * h1, axis=1, keepdims=True) + EPS)
        u2_ref[...] = ((h1 * r * vec_ref[6:7, :]) * vec_ref[4:5, :] + vec_ref[5:6, :]).astype(MXU_DTYPE)

    row = lambda j: (lambda i: (i, j))
    return _pcall(
        body, name="mix_out",
        out_shape=(jax.ShapeDtypeStruct((n, 2 * D), MXU_DTYPE), jax.ShapeDtypeStruct((n, D), F32),
                   jax.ShapeDtypeStruct((n, D), F32), jax.ShapeDtypeStruct((n, D), MXU_DTYPE)),
        grid=(n // TB,),
        in_specs=[pl.BlockSpec((TB, D), row(0)), pl.BlockSpec((TB, D), row(0)), pl.BlockSpec((TB, D), row(4)),
                  pl.BlockSpec((TB, D), row(5)), pl.BlockSpec((TB, D), row(0)), pl.BlockSpec((TB, D), row(0)),
                  pl.BlockSpec((TB, D), row(0)), pl.BlockSpec((TB, D), row(0)), _full((8, D)), _full((2 * D, D))],
        out_specs=(pl.BlockSpec((TB, 2 * D), row(0)), pl.BlockSpec((TB, D), row(0)), pl.BlockSpec((TB, D), row(0)),
                   pl.BlockSpec((TB, D), row(0))),
        sem=("parallel",), vmem_mb=48,
    )(o_f, o_b, p_main, p_main, y_f, y_b, xa, x, vecs, w_out)


def _mix_bwd(dylat, o_f, o_b, p_main, y_f, y_b, xa, vecs, w_out):
    n = dylat.shape[0]
    t_total = p_main.shape[0]
    nlat = n // TB

    def body(*refs):
        dg_ref, dz_ref, acc_ref = refs[11], refs[13], refs[15]
        i = pl.program_id(0)

        @pl.when(i == 0)
        def _():
            acc_ref[...] = jnp.zeros_like(acc_ref)

        @pl.when(i < nlat)
        def _():
            compute(*refs)

        @pl.when(i == nlat)
        def _():
            dg_ref[...] = jnp.zeros_like(dg_ref)
            dz_ref[...] = jnp.zeros_like(dz_ref)

    def compute(dyl_ref, of_ref, ob_ref, g_ref, z_ref, yf_ref, yb_ref, xs_ref, vec_ref, w_ref,
                do_ref, dg_ref, dys_ref, dz_ref, dxs_ref, acc_ref):
        dymix = _nt(dyl_ref[...], w_ref[...])
        o = of_ref[...] + ob_ref[...]
        g = g_ref[...]
        z = z_ref[...]
        xs = xs_ref[...]
        yy = yf_ref[...] + yb_ref[...] + vec_ref[1:2, :] * xs
        _, _, keep, u = _readout(o, g, yy, z, vec_ref)
        do_l, dg_l = [], []
        for h in range(NH):
            cs = slice(h * HF, (h + 1) * HF)
            oh, gh, r, wv = o[:, cs], g[:, cs], keep[h], vec_ref[0:1, cs]
            dhg = dymix[:, cs]
            xh = oh * r
            dn = dhg * _silu(gh)
            dg_l.append(dhg * xh * wv * _dsilu(gh))
            acc_ref[0:1, cs] += _colsum(dn * xh)
            dxh = dn * wv
            do_l.append(r * (dxh - xh * jnp.mean(dxh * xh, axis=1, keepdims=True)))
        du_l = []
        for gi in range(4):
            cs = slice(gi * 256, (gi + 1) * 256)
            ug, r, wv = u[:, cs], keep[NH + gi], vec_ref[2:3, cs]
            dss = dymix[:, D + gi * 256:D + (gi + 1) * 256]
            xh = ug * r
            acc_ref[2:3, cs] += _colsum(dss * xh)
            dxh = dss * wv
            du_l.append(r * (dxh - xh * jnp.mean(dxh * xh, axis=1, keepdims=True)))
        du = jnp.concatenate(du_l, axis=1)
        dyy = du * _silu(z)
        do_ref[...] = jnp.concatenate(do_l, axis=1)
        dg_ref[...] = jnp.concatenate(dg_l, axis=1).astype(dg_ref.dtype)
        dys_ref[...] = dyy
        dz_ref[...] = (du * yy * _dsilu(z)).astype(dz_ref.dtype)
        dxs_ref[...] = dyy * vec_ref[1:2, :]
        acc_ref[1:2, :] += _colsum(dyy * xs)

    row = lambda j: (lambda i: (jnp.minimum(i, nlat - 1), j))
    lat = pl.BlockSpec((TB, D), row(0))
    tok = pl.BlockSpec((TB, D), lambda i: (i, 0))
    return _pcall(
        body, name="mix_bwd",
        out_shape=(jax.ShapeDtypeStruct((n, D), F32), jax.ShapeDtypeStruct((t_total, D), MXU_DTYPE),
                   jax.ShapeDtypeStruct((n, D), F32), jax.ShapeDtypeStruct((t_total, D), MXU_DTYPE),
                   jax.ShapeDtypeStruct((n, D), F32), jax.ShapeDtypeStruct((8, D), F32)),
        grid=(t_total // TB,),
        in_specs=[lat, lat, lat, pl.BlockSpec((TB, D), row(4)), pl.BlockSpec((TB, D), row(5)), lat, lat, lat,
                  _full((8, D)), _full((2 * D, D))],
        out_specs=(lat, tok, lat, tok, lat, _full((8, D))),
        sem=("arbitrary",), vmem_mb=48,
    )(dylat, o_f, o_b, p_main, p_main, y_f, y_b, xa, vecs, w_out)


def _ffn_up(u2, w_gate, w_up):
    n = u2.shape[0]
    tb = 512

    def body(u_ref, wg_ref, wu_ref, g_ref, up_ref, a_ref):
        uv = u_ref[...]
        gt = _nn(uv, wg_ref[...])
        upv = _nn(uv, wu_ref[...])
        g_ref[...] = gt
        up_ref[...] = upv
        a_ref[...] = (_silu(gt) * upv).astype(a_ref.dtype)

    blk = pl.BlockSpec((tb, FSL), lambda i, j: (i, j))
    wblk = pl.BlockSpec((None, D, FSL), lambda i, j: (j, 0, 0))
    return _pcall(
        body, name="ffn_up",
        out_shape=(jax.ShapeDtypeStruct((n, DFFP), F32), jax.ShapeDtypeStruct((n, DFFP), F32),
                   jax.ShapeDtypeStruct((n, DFFP), MXU_DTYPE)),
        grid=(n // tb, 4), in_specs=[pl.BlockSpec((tb, D), lambda i, j: (i, 0)), wblk, wblk],
        out_specs=(blk, blk, blk), sem=("parallel", "parallel"), vmem_mb=48,
    )(u2, w_gate, w_up)


def _ffn_down_loss(act, w_down, h1, tgt, vecs):
    n = act.shape[0]
    tb = 512

    def body(a_ref, w_ref, h1_ref, t_ref, vec_ref, dh2_ref, dffn_ref, acc_ref):
        i = pl.program_id(0)

        @pl.when(i == 0)
        def _():
            acc_ref[...] = jnp.zeros_like(acc_ref)

        ffn = _nn(a_ref[...], w_ref[...])
        g2 = vec_ref[0:1, :]
        fw = vec_ref[1:2, :]
        h2 = h1_ref[...] + g2 * ffn
        r = lax.rsqrt(jnp.mean(h2 * h2, axis=1, keepdims=True) + EPS)
        xh = h2 * r
        err = xh * fw - t_ref[...]
        dy = err * (1.0 / D)
        acc_ref[2:3, :] += _colsum(err * err) * (0.5 / D)
        acc_ref[1:2, :] += _colsum(dy * xh)
        dxh = dy * fw
        dh2 = r * (dxh - xh * jnp.mean(dxh * xh, axis=1, keepdims=True))
        dh2_ref[...] = dh2
        dffn_ref[...] = (g2 * dh2).astype(dffn_ref.dtype)
        acc_ref[0:1, :] += _colsum(dh2 * ffn)

    return _pcall(
        body, name="ffn_down_loss",
        out_shape=(jax.ShapeDtypeStruct((n, D), F32), jax.ShapeDtypeStruct((n, D), MXU_DTYPE),
                   jax.ShapeDtypeStruct((8, D), F32)),
        grid=(n // tb,),
        in_specs=[pl.BlockSpec((tb, DFFP), lambda i: (i, 0)), _full((DFFP, D)), pl.BlockSpec((tb, D), lambda i: (i, 0)),
                  pl.BlockSpec((tb, D), lambda i: (i, 0)), _full((8, D))],
        out_specs=(pl.BlockSpec((tb, D), lambda i: (i, 0)), pl.BlockSpec((tb, D), lambda i: (i, 0)), _full((8, D))),
        sem=("arbitrary",), vmem_mb=48,
    )(act, w_down, h1, tgt, vecs)


def _ffn_bwd(dffn, w_down, gate, up, w_gate, w_up, h1, ylat, dh2, vecs):
    n = dffn.shape[0]
    tb = 256

    def body(df_ref, wd_ref, g_ref, up_ref, wg_ref, wu_ref, h1_ref, yl_ref, dh2_ref, vec_ref,
             dg_ref, dup_ref, dh1_ref, dyl_ref, acc_ref, du_scr):
        i, j = pl.program_id(0), pl.program_id(1)

        @pl.when((i == 0) & (j == 0))
        def _():
            acc_ref[...] = jnp.zeros_like(acc_ref)

        dact = _nt(df_ref[...], wd_ref[...])
        gt = g_ref[...]
        upv = up_ref[...]
        dgt = (dact * upv * _dsilu(gt)).astype(MXU_DTYPE)
        dupv = (dact * _silu(gt)).astype(MXU_DTYPE)
        dg_ref[...] = dgt
        dup_ref[...] = dupv
        part = _nt(dgt, wg_ref[...]) + _nt(dupv, wu_ref[...])

        @pl.when(j == 0)
        def _():
            du_scr[...] = part

        @pl.when((j > 0) & (j < 3))
        def _():
            du_scr[...] += part

        @pl.when(j == 3)
        def _():
            du = du_scr[...] + part
            h1 = h1_ref[...]
            r = lax.rsqrt(jnp.mean(h1 * h1, axis=1, keepdims=True) + EPS)
            xh = h1 * r
            nw = vec_ref[2:3, :]
            acc_ref[0:1, :] += _colsum(du)
            acc_ref[1:2, :] += _colsum(du * xh * nw)
            dn = du * vec_ref[1:2, :]
            acc_ref[2:3, :] += _colsum(dn * xh)
            dxh = dn * nw
            dh1 = dh2_ref[...] + r * (dxh - xh * jnp.mean(dxh * xh, axis=1, keepdims=True))
            dh1_ref[...] = dh1
            dyl_ref[...] = (vec_ref[0:1, :] * dh1).astype(dyl_ref.dtype)
            acc_ref[3:4, :] += _colsum(dh1 * yl_ref[...])

    tok = pl.BlockSpec((tb, D), lambda i, j: (i, 0))
    ffb = pl.BlockSpec((tb, FSL), lambda i, j: (i, j))
    wsl = pl.BlockSpec((None, D, FSL), lambda i, j: (j, 0, 0))
    return _pcall(
        body, name="ffn_bwd",
        out_shape=(jax.ShapeDtypeStruct((n, DFFP), MXU_DTYPE), jax.ShapeDtypeStruct((n, DFFP), MXU_DTYPE),
                   jax.ShapeDtypeStruct((n, D), F32), jax.ShapeDtypeStruct((n, D), MXU_DTYPE),
                   jax.ShapeDtypeStruct((8, D), F32)),
        grid=(n // tb, 4),
        in_specs=[tok, pl.BlockSpec((FSL, D), lambda i, j: (j, 0)), ffb, ffb, wsl, wsl,
                  tok, tok, tok, _full((8, D))],
        out_specs=(ffb, ffb, tok, tok, _full((8, D))),
        scratch=[pltpu.VMEM((tb, D), F32)], sem=("arbitrary", "arbitrary"), vmem_mb=48,
    )(dffn, w_down, gate, up, w_gate, w_up, h1, ylat, dh2, vecs)


def _deep_rows(rows):
    return max(r for r in range(128, 2305, 128) if rows % r == 0)


def _dw(a, b, name, slabs=None):
    tn_rows = a.shape[0]
    bt = _deep_rows(tn_rows)
    kk, nn_ = a.shape[1], b.shape[1]
    bk = 1024 if kk % 1024 == 0 else kk
    bn = slabs if slabs is not None else (1024 if nn_ % 1024 == 0 else nn_)
    nt = tn_rows // bt
    ka, nb_ = kk // bk, nn_ // bn

    def body(a_ref, b_ref, o_ref, acc):
        t = pl.program_id(2)
        part = _tn(a_ref[...], b_ref[...])

        @pl.when(t == 0)
        def _():
            acc[...] = part

        @pl.when(t > 0)
        def _():
            acc[...] += part

        @pl.when(t == nt - 1)
        def _():
            o_ref[...] = acc[...].astype(o_ref.dtype)

    if slabs is None:
        out_shape = jax.ShapeDtypeStruct((kk, nn_), MXU_DTYPE)
        out_spec = pl.BlockSpec((bk, bn), lambda i, j, t: (i, j))
    else:
        out_shape = jax.ShapeDtypeStruct((nb_, kk, bn), MXU_DTYPE)
        out_spec = pl.BlockSpec((None, bk, bn), lambda i, j, t: (j, i, 0))
    return _pcall(
        body, name=name, out_shape=out_shape, grid=(ka, nb_, nt),
        in_specs=[pl.BlockSpec((bt, bk), lambda i, j, t: (t, i)), pl.BlockSpec((bt, bn), lambda i, j, t: (t, j))],
        out_specs=out_spec, scratch=[pltpu.VMEM((bk, bn), F32)],
        sem=("parallel", "parallel", "arbitrary"), vmem_mb=40,
    )(a, b)


def _dw_in(u_all, segs, name):
    tiles = []
    for m, s_ in enumerate(segs):
        tiles += [(m, h) for h in range(s_.shape[1] // D)]
    ntile = len(tiles)
    t_total = u_all.shape[0]
    bt = _deep_rows(t_total)
    nt = t_total // bt

    def body(u_ref, *refs):
        seg_refs, o_ref, acc = refs[:len(segs)], refs[len(segs)], refs[len(segs) + 1]
        n, t = pl.program_id(0), pl.program_id(1)
        for k, (m, _) in enumerate(tiles):
            @pl.when(n == k)
            def _(m=m):
                part = _tn(u_ref[...], seg_refs[m][...])

                @pl.when(t == 0)
                def _():
                    acc[...] = part

                @pl.when(t > 0)
                def _():
                    acc[...] += part

        @pl.when(t == nt - 1)
        def _():
            o_ref[...] = acc[...].astype(o_ref.dtype)

    def seg_spec(m):
        ks = [k for k, (mm, _) in enumerate(tiles) if mm == m]
        lo, hi = ks[0], ks[-1]
        on = lambda n: (n >= lo) & (n <= hi)
        return pl.BlockSpec((bt, D), lambda n, t: (jnp.where(on(n), t, 0), jnp.where(on(n), n - lo, 0)))

    return _pcall(
        body, name=name, out_shape=jax.ShapeDtypeStruct((ntile // 2, D, 2 * D), MXU_DTYPE), grid=(ntile, nt),
        in_specs=[pl.BlockSpec((bt, D), lambda n, t: (t, 0))] + [seg_spec(m) for m in range(len(segs))],
        out_specs=pl.BlockSpec((None, D, D), lambda n, t: (n // 2, 0, n % 2)),
        scratch=[pltpu.VMEM((D, D), F32)], sem=("parallel", "arbitrary"), vmem_mb=56,
    )(u_all, *segs)


def _du_prenorm_bwd(segs, ddt, wi_main, wi_tail, xin, mods, dres, row_off, tb, name, comm=None):
    n = xin.shape[0]
    nt = n // tb
    off = row_off // tb
    has_dx = dres is not None

    def body(*refs):
        seg_refs = refs[:7]
        ddt_ref, w_ref, wb_ref, wdt_ref, x_ref, mod_ref = refs[7:13]
        rest = refs[13:]
        if has_dx:
            dres_ref, dx_ref, acc_ref, du_scr = rest
        else:
            acc_ref, du_scr = rest
        i, j = pl.program_id(0), pl.program_id(1)

        @pl.when((i == 0) & (j == 0))
        def _():
            acc_ref[...] = jnp.zeros_like(acc_ref)

        @pl.when(j == 0)
        def _():
            du_scr[...] = _nt(ddt_ref[...], wdt_ref[...])

        for k in range(8):
            if not has_dx and k in (4, 5):
                continue

            @pl.when(j == k)
            def _(k=k):
                if k < 6:
                    sv = seg_refs[k][...]
                else:
                    sv = seg_refs[6][:, (k - 6) * D:(k - 5) * D]
                du_scr[...] += _nt(sv, w_ref[...])
                if k in (2, 4, 6):
                    du_scr[...] += _nt(sv[:, 0:WTAIL], wb_ref[...])

        @pl.when(j == 7)
        def _():
            du = du_scr[...]
            xv = x_ref[...]
            r = lax.rsqrt(jnp.mean(xv * xv, axis=1, keepdims=True) + EPS)
            xh = xv * r
            nw = mod_ref[1:2, :]
            acc_ref[0:1, :] += _colsum(du)
            acc_ref[1:2, :] += _colsum(du * xh * nw)
            dn = du * mod_ref[0:1, :]
            acc_ref[2:3, :] += _colsum(dn * xh)
            if has_dx:
                dxh = dn * nw
                dx_ref[...] = dres_ref[...] + r * (dxh - xh * jnp.mean(dxh * xh, axis=1, keepdims=True))

    tokT = lambda w: pl.BlockSpec((tb, w), lambda i, j: (i + off, 0))
    tok = pl.BlockSpec((tb, D), lambda i, j: (i, 0))
    in_specs = [tokT(D)] * 6 + [tokT(2 * D), tokT(128)] + _w_specs() + [tok, _full((8, D))]
    args = list(segs) + [ddt, wi_main, wi_tail, wi_tail, xin, mods]
    out_shape = [jax.ShapeDtypeStruct((8, D), F32)]
    out_specs = [_full((8, D))]
    if has_dx:
        in_specs.append(tok)
        args.append(dres)
        out_shape.insert(0, jax.ShapeDtypeStruct((n, D), F32))
        out_specs.insert(0, tok)
    call = dict(body=body, args=args, name=name, out_shape=tuple(out_shape), grid=(nt, 8), in_specs=in_specs,
                out_specs=tuple(out_specs), scratch=[pltpu.VMEM((tb, D), F32)], sem=("arbitrary", "arbitrary"),
                vmem_mb=56)
    steps = lambda: ((pl.program_id(0) == 0) & (pl.program_id(1) == 0),
                     (pl.program_id(0) == nt - 1) & (pl.program_id(1) == 7))
    return _run(_carry(call, comm, steps))


def _sum8(v):
    def body(v_ref, o_ref):
        acc = v_ref[0]
        for k in range(1, 8):
            acc = acc + v_ref[k]
        o_ref[...] = acc

    return _pcall(body, name="small_sum", out_shape=jax.ShapeDtypeStruct(v.shape[1:], F32),
                  in_specs=[pl.BlockSpec(memory_space=pltpu.VMEM)], out_specs=pl.BlockSpec(memory_space=pltpu.VMEM))(v)


def _adamw(w, m, v, g, name):
    rows, cols = w.shape
    rb = 256 if rows % 256 == 0 else (352 if rows % 352 == 0 else rows)
    c1 = 1.0 - B1 ** STEP
    c2 = 1.0 - B2 ** STEP

    def body(w_ref, m_ref, v_ref, g_ref, d_ref, nm_ref, nv_ref):
        gv = g_ref[...]
        mn = B1 * m_ref[...] + (1.0 - B1) * gv
        vn = B2 * v_ref[...] + (1.0 - B2) * (gv * gv)
        nm_ref[...] = mn
        nv_ref[...] = vn
        d_ref[...] = -LR * ((mn / c1) / (jnp.sqrt(vn / c2) + AEPS) + WD * w_ref[...])

    spec = pl.BlockSpec((rb, cols), lambda i: (i, 0))
    return _pcall(
        body, name=name, out_shape=(jax.ShapeDtypeStruct(w.shape, F32),) * 3, grid=(rows // rb,),
        in_specs=[spec] * 4, out_specs=(spec,) * 3, sem=("parallel",), vmem_mb=40,
    )(w, m, v, g)


def _rows(v, n):
    f = v.reshape(-1)
    return jnp.pad(f, (0, n * D - f.shape[0])).reshape(n, D)


def kernel(x, c, ctx, c_ctx, w_ada, b_ada, norm_mix, w_in, conv_w, conv_b, ssd_a_log, ssd_dt_bias, ssd_d, ssd_norm, hgrn_lb_raw, hgrn_norm, w_out, norm_ffn, w_gate, w_up, w_down, final_norm, loss_target, m_c_ctx, m_w_ada, m_b_ada, m_norm_mix, m_w_in, m_conv_w, m_conv_b, m_ssd_a_log, m_ssd_dt_bias, m_ssd_d, m_ssd_norm, m_hgrn_lb_raw, m_hgrn_norm, m_w_out, m_norm_ffn, m_w_gate, m_w_up, m_w_down, m_final_norm, v_c_ctx, v_w_ada, v_b_ada, v_norm_mix, v_w_in, v_conv_w, v_conv_b, v_ssd_a_log, v_ssd_dt_bias, v_ssd_d, v_ssd_norm, v_hgrn_lb_raw, v_hgrn_norm, v_w_out, v_norm_ffn, v_w_gate, v_w_up, v_w_down, v_final_norm):
    ix, iy, ic = lax.axis_index("x"), lax.axis_index("y"), lax.axis_index("c")
    chip = 2 * ix + iy
    me = 2 * chip + ic
    xl, xc, tgt = x[0], ctx[0], loss_target[0]
    n_lat, n_ctx = xl.shape[0], xc.shape[0]
    assert n_ctx == TB and n_lat % 1024 == 0
    t_total = n_lat + n_ctx
    nb = t_total // TB

    pack = jnp.concatenate([c, hgrn_lb_raw.reshape(1, D), _rows(conv_w[0], 3), jnp.zeros((3, D), F32)], axis=0)
    gath = _allgather8(pack, "small_gather").reshape(8, 8, D)
    c_all = gath[:, 0]
    lbraw_full = gath[0::2, 1].reshape(4, 2, 2, 256).transpose(1, 2, 0, 3).reshape(4, D)
    convw_full = gath[0::2, 2:5].reshape(4, 3 * D)[:, :KCONV * 512].reshape(4, KCONV, 512).transpose(1, 0, 2)
    convw_full = convw_full.reshape(KCONV, 2048)
    lbraw8 = jnp.pad(lbraw_full, ((0, 4), (0, 0)))
    convp = jnp.concatenate([convw_full, conv_b, jnp.zeros((2, 2048), F32)], axis=0)
    dtb = jnp.pad(ssd_dt_bias.reshape(1, 32), ((0, 7), (0, 96)))
    alog = jnp.pad(ssd_a_log.reshape(1, 32), ((0, 7), (0, 96)))

    araw = jnp.concatenate([c_all, c_ctx.reshape(1, D), jnp.zeros((7, D), F32)], axis=0)
    ncol_ada = w_ada.shape[2]
    b_shard = lax.dynamic_slice(b_ada, (0, chip * ncol_ada), (1, ncol_ada))
    mod_shard = _ada_fwd(araw, w_ada[0], b_shard)
    mod_all = _allgather8(mod_shard, "mod_gather").reshape(8, 16, ncol_ada)[0::2]
    mod_full = mod_all.transpose(1, 0, 2).reshape(16, 4 * ncol_ada)
    my_mod = lax.dynamic_slice(mod_full, (me, 0), (1, 6 * D)).reshape(6, D)
    sh1, sc1, g1, sh2, sc2, g2 = (my_mod[k:k + 1] for k in range(6))
    csh1, csc1 = mod_full[8:9, 0:D], mod_full[8:9, D:2 * D]

    slab = lax.dynamic_update_slice(jnp.zeros((D, WSL + WTAIL), F32), w_in[0], (0, 8 * chip)).astype(MXU_DTYPE)
    shards = [slab[:, :WSL], slab[:, WSL:], w_out[0].astype(MXU_DTYPE),
              jnp.pad(w_gate[0], ((0, 0), (0, FSL - DFF // 4))).astype(MXU_DTYPE),
              jnp.pad(w_up[0], ((0, 0), (0, FSL - DFF // 4))).astype(MXU_DTYPE),
              jnp.pad(w_down[0], ((0, FSL - DFF // 4), (0, 0))).astype(MXU_DTYPE)]
    own = lambda g_, s_: lax.dynamic_update_slice(g_, s_[None], (chip, 0, 0))
    wi_main, wi_tail = (own(g_, s_) for g_, s_ in zip(_weights_allgather(shards[:2]), shards[:2]))

    zrow = jnp.zeros((1, D), F32)
    mods_lat = jnp.concatenate([1.0 + sc1, sh1, norm_mix, zrow, zrow, zrow, zrow, zrow], axis=0)
    mods_ctx = jnp.concatenate([1.0 + csc1, csh1, norm_mix, zrow, zrow, zrow, zrow, zrow], axis=0)
    outs = _inproj(xl, mods_lat, wi_main, wi_tail, t_total, 1024, 0, None, "inproj_lat",
                   comm=_comm_gather(shards[2:]))
    wo_g, wg_g, wu_g, wd_g = (own(g_, s_) for g_, s_ in zip(outs[3:], shards[2:]))
    w_out_f = wo_g.reshape(2 * D, D)
    w_down_f = wd_g.reshape(DFFP, D)
    p_main, p_dt, u_all = _inproj(xc, mods_ctx, wi_main, wi_tail, t_total, TB, nb - 1, outs[:3], "inproj_ctx")

    o_f, hs_f = _hgrn_fwd(p_main, lbraw8, 0, nb)
    o_b, hs_b = _hgrn_fwd(p_main, lbraw8, 1, nb)
    xa, dts = _ssd_prep(p_main, p_dt, convp, dtb, nb)
    y_f, ss_f = _ssd_fwd(xa, dts, alog, 0, nb)
    y_b, ss_b = _ssd_fwd(xa, dts, alog, 1, nb)

    vec_mix = jnp.concatenate([jnp.tile(hgrn_norm, (1, NH)), jnp.repeat(ssd_d, SP, axis=1), ssd_norm, g1, 1.0 + sc2,
                               sh2, norm_ffn, zrow], axis=0)
    ymix, ylat, h1, u2 = _mix_out(o_f, o_b, p_main, y_f, y_b, xa, xl, vec_mix, w_out_f)
    gate, up, act = _ffn_up(u2, wg_g, wu_g)
    vec_loss = jnp.concatenate([g2, final_norm.reshape(1, D)] + [zrow] * 6, axis=0)
    dh2, dffn, acc_loss = _ffn_down_loss(act, w_down_f, h1, tgt, vec_loss)

    core_arr = jnp.reshape(ic, (1,)).astype(jnp.int32)
    chip_arr = jnp.reshape(chip, (1,)).astype(jnp.int32)
    every = (0, 4)

    def pair_stage(gs, tag):
        return list(_pair_sum(gs, _pair_exchange(gs, "grads_pair_exchange_" + tag), core_arr, "grads_pair_sum_" + tag))

    vec_ffn = jnp.concatenate([g1, 1.0 + sc2, norm_ffn] + [zrow] * 5, axis=0)
    dgate, dup, dh1, dylat, acc_ffn = _ffn_bwd(dffn, w_down_f, gate, up, wg_g, wu_g, h1, ylat, dh2, vec_ffn)
    gw_down = _dw(act, dffn, "dw_down").reshape(4, FSL, D)
    gw_gate = _dw(u2, dgate, "dw_gate", slabs=FSL)
    gw_up = _dw(u2, dup, "dw_up", slabs=FSL)
    do, dgr, dys, dzr, dxs_skip, acc_mix = _mix_bwd(dylat, o_f, o_b, p_main, y_f, y_b, xa, vec_mix, w_out_f)
    gw_out = _dw(ymix, dylat, "dw_out").reshape(4, D // 2, D)
    pair_a, dests_a = pair_stage([gw_out, gw_gate, gw_up, gw_down], "a"), [every] * 4

    res = _hgrn_bwd(p_main, lbraw8, hs_f, do, 0, nb, None, comm=_comm_exchange(pair_a, dests_a))
    (dq0, dff, dv0, dlb_f), recv_a = res[:4], list(res[4:])
    dq, dfb, dv, dlb_b = _hgrn_bwd(p_main, lbraw8, hs_b, do, 1, nb, (dq0, dv0))
    gw_in = [_dw_in(u_all, [dq, dff], "dw_in_0"), _dw_in(u_all, [dfb, dv], "dw_in_1"),
             _dw_in(u_all, [dgr, dzr], "dw_in_2")]
    pair_b, dests_b = pair_stage(gw_in, "b"), [(0, 1), (1, 2), (2, 3)]

    res = _ssd_bwd(xa, dts, alog, ss_f, dys, 0, nb, None, comm=_comm_exchange(pair_b, dests_b))
    (dxa0, ddts0, da_f), recv_b = res[:3], list(res[3:])
    dxa, ddts, da_b = _ssd_bwd(xa, dts, alog, ss_b, dys, 1, nb, (dxa0, ddts0))
    dxbc, ddt, acc_conv, acc_dtb = _ssd_prep_bwd(p_main, p_dt, convp, dtb, dxa, dxs_skip, ddts, nb)
    gw_in.append(_dw_in(u_all, [dxbc], "dw_in_3"))
    gw_in_dt = _dw(u_all, ddt, "dw_in_dt")
    gw_in_tail = jnp.concatenate([g_[:, :, 0:WTAIL] for g_ in gw_in[1:]] + [gw_in_dt[None]], axis=0)
    pair_c, dests_c = pair_stage([gw_in[3], gw_in_tail], "c"), [(3, 4), every]

    segs = [dq, dff, dfb, dv, dgr, dzr, dxbc]
    bmods_lat = jnp.concatenate([1.0 + sc1, norm_mix] + [zrow] * 6, axis=0)
    bmods_ctx = jnp.concatenate([1.0 + csc1, norm_mix] + [zrow] * 6, axis=0)
    res = _du_prenorm_bwd(segs, ddt, wi_main, wi_tail, xl, bmods_lat, dh1, 0, 512, "du_lat",
                          comm=_comm_exchange(pair_c, dests_c))
    (grad_x, acc_lat), recv_c = res[:2], list(res[2:])
    (acc_ctx,) = _du_prenorm_bwd(segs, ddt, wi_main, wi_tail, xc, bmods_ctx, None, n_lat, TB, "du_ctx")

    mine = _chip_sum(pair_b + pair_c + pair_a, recv_b + recv_c + recv_a, chip_arr, dests_b + dests_c + dests_a,
                     [0, 0, 0, 0, 1, 2, 3, 4, 5])
    theirs = _pair_swap(mine)
    whole = [jnp.concatenate([jnp.where(ic == 0, m_, t_), jnp.where(ic == 0, t_, m_)], axis=0)
             for m_, t_ in zip(mine, theirs)]
    g_w_in = lax.dynamic_slice(jnp.concatenate(whole[0:2], axis=1), (0, 8 * chip), (D, NSH))
    g_w_out = whole[2]
    g_w_gate = whole[3][:, :DFF // 4]
    g_w_up = whole[4][:, :DFF // 4]
    g_w_down = whole[5][:DFF // 4]

    dmod_lat = jnp.concatenate([acc_lat[0:2], acc_ffn[3:4], acc_ffn[0:2], acc_loss[0:1]], axis=0)
    misc = jnp.concatenate([(da_f + da_b)[0, :32], jnp.zeros((96,), F32), acc_dtb[0, :32], jnp.zeros((96,), F32),
                            jnp.sum(acc_loss[2]).reshape(1), jnp.zeros((D - 257,), F32)]).reshape(1, D)
    sv = jnp.concatenate([
        dmod_lat, acc_ctx[0:2], (acc_lat[2:3] + acc_ctx[2:3]), acc_ffn[2:3], acc_loss[1:2], acc_mix[2:3],
        acc_mix[0:1], acc_mix[1:2], dlb_f[0:1], dlb_b[0:1], acc_conv[0:6].reshape(12, D), misc,
        jnp.zeros((3, D), F32)], axis=0)
    sv_all = _allgather8(sv, "small_grads_gather").reshape(8, 32, D)
    ssum = _sum8(sv_all)
    dmod_rows = sv_all[:, 0:6].reshape(8, 6 * D)
    dmod_ctx_row = jnp.concatenate([ssum[6:8].reshape(1, 2 * D), jnp.zeros((1, 4 * D), F32)], axis=1)
    dmod_full = jnp.concatenate([dmod_rows, dmod_ctx_row, jnp.zeros((7, 6 * D), F32)], axis=0)
    grad_b_ada = jnp.sum(dmod_full, axis=0, keepdims=True)
    dmod_shard = lax.dynamic_slice(dmod_full, (0, chip * ncol_ada), (16, ncol_ada))
    g_w_ada, da_part = _ada_bwd(araw, dmod_shard, w_ada[0])
    da_all = _allgather8(da_part, "ada_ctx_gather").reshape(8, 16, D)[0::2, 8]
    cc = c_ctx.reshape(1, D)
    grad_c_ctx = (jnp.sum(da_all, axis=0, keepdims=True) * _dsilu(cc)).reshape(D)

    grad_norm_mix, grad_norm_ffn, grad_final_norm = ssum[8:9], ssum[9:10], ssum[10].reshape(D)
    grad_ssd_norm = ssum[11:12]
    grad_hgrn_norm = jnp.sum(ssum[12].reshape(NH, HF), axis=0, keepdims=True)
    grad_ssd_d = jnp.sum(ssum[13].reshape(SHEADS, SP), axis=1).reshape(1, SHEADS)
    lb_full = _sig(lbraw_full[0:2] - lbraw_full[2:4])
    dr0 = ssum[14:16] * lb_full * (1.0 - lb_full)
    grad_lb_full = jnp.stack([dr0, -dr0], axis=0)
    grad_lb = lax.dynamic_slice(grad_lb_full, (0, 0, chip * 256), (2, 2, 256))
    grad_conv_w = lax.dynamic_slice(ssum[16:26].reshape(KCONV, 2048), (0, chip * 512), (KCONV, 512)).reshape(1, KCONV, 512)
    grad_conv_b = ssum[26:28].reshape(1, 2048)
    a_val = -jnp.exp(ssd_a_log)
    grad_a_log = ssum[28, 0:32].reshape(1, 2, SHEADS) * a_val
    grad_dt_bias = ssum[28, 128:160].reshape(1, 2, SHEADS)
    loss = ssum[28, 256]

    small_w = [c_ctx, b_ada, norm_mix, conv_w, conv_b, ssd_a_log, ssd_dt_bias, ssd_d, ssd_norm, hgrn_lb_raw,
               hgrn_norm, norm_ffn, final_norm]
    small_m = [m_c_ctx, m_b_ada, m_norm_mix, m_conv_w, m_conv_b, m_ssd_a_log, m_ssd_dt_bias, m_ssd_d, m_ssd_norm,
               m_hgrn_lb_raw, m_hgrn_norm, m_norm_ffn, m_final_norm]
    small_v = [v_c_ctx, v_b_ada, v_norm_mix, v_conv_w, v_conv_b, v_ssd_a_log, v_ssd_dt_bias, v_ssd_d, v_ssd_norm,
               v_hgrn_lb_raw, v_hgrn_norm, v_norm_ffn, v_final_norm]
    small_g = [grad_c_ctx, grad_b_ada, grad_norm_mix, grad_conv_w, grad_conv_b, grad_a_log, grad_dt_bias, grad_ssd_d,
               grad_ssd_norm, grad_lb, grad_hgrn_norm, grad_norm_ffn, grad_final_norm]
    nrows = [-(-a.size // D) for a in small_w]
    packs = lambda lst: jnp.concatenate([_rows(a, r) for a, r in zip(lst, nrows)]
                                        + [jnp.zeros((24 - sum(nrows), D), F32)], axis=0)
    sd, sm, svv = _adamw(packs(small_w), packs(small_m), packs(small_v), packs(small_g), "adamw_small")

    def unpack(p):
        out, r0 = [], 0
        for a, r in zip(small_w, nrows):
            out.append(p[r0:r0 + r].reshape(-1)[:a.size].reshape(a.shape))
            r0 += r
        return out

    sd, sm, svv = unpack(sd), unpack(sm), unpack(svv)
    big = {}
    for nm, w_, m_, v_, g_ in (("w_ada", w_ada, m_w_ada, v_w_ada, g_w_ada), ("w_in", w_in, m_w_in, v_w_in, g_w_in),
                               ("w_out", w_out, m_w_out, v_w_out, g_w_out),
                               ("w_gate", w_gate, m_w_gate, v_w_gate, g_w_gate),
                               ("w_up", w_up, m_w_up, v_w_up, g_w_up),
                               ("w_down", w_down, m_w_down, v_w_down, g_w_down)):
        dl, nm_, nv_ = _adamw(w_[0], m_[0], v_[0], g_, "adamw_" + nm)
        big[nm] = (g_[None], dl[None], nm_[None], nv_[None])

    order = ["c_ctx", "w_ada", "b_ada", "norm_mix", "w_in", "conv_w", "conv_b", "ssd_a_log", "ssd_dt_bias", "ssd_d",
             "ssd_norm", "hgrn_lb_raw", "hgrn_norm", "w_out", "norm_ffn", "w_gate", "w_up", "w_down", "final_norm"]
    small_names = ["c_ctx", "b_ada", "norm_mix", "conv_w", "conv_b", "ssd_a_log", "ssd_dt_bias", "ssd_d", "ssd_norm",
                   "hgrn_lb_raw", "hgrn_norm", "norm_ffn", "final_norm"]
    table = dict(big)
    for k, nm in enumerate(small_names):
        table[nm] = (small_g[k].reshape(small_w[k].shape), sd[k], sm[k], svv[k])
    grads = [table[nm][0] for nm in order]
    deltas = [table[nm][1] for nm in order]
    new_m = [table[nm][2] for nm in order]
    new_v = [table[nm][3] for nm in order]
    return (loss, grad_x[None], *grads, *deltas, *new_m, *new_v)
```

```python
import functools
import math

import jax
import jax.numpy as jnp
from jax import lax
from jax.experimental import pallas as pl
from jax.experimental.pallas import tpu as pltpu

F32 = jnp.float32
BF16 = jnp.bfloat16
MXU_DTYPE = jnp.bfloat16
_INTERPRET = False

D = 1024
NH, HF = 8, 128
HC = 64
SC = 128
SN = 128
SHEADS, SP = 16, 64
GRID_W = 64
KCONV = 5
DFF = 2816
FSL = 768
DFFP = 4 * FSL
NIN = 8224
TB = 256
EPS = 1e-6
LR, B1, B2, AEPS, WD, STEP = 0.001, 0.9, 0.999, 1e-08, 0.01, 10
MESH_ID = pl.DeviceIdType.MESH
NSH = NIN // 4
WSL = 2048
WTAIL = 128


def _pcall(body, *, name, out_shape, grid=(), in_specs=None, out_specs=None, scratch=(), sem=None,
           vmem_mb=None, aliases=None):
    params = {}
    if sem is not None:
        params["dimension_semantics"] = sem
    if vmem_mb is not None:
        params["vmem_limit_bytes"] = vmem_mb << 20
    kw = dict(name=name, out_shape=out_shape, scratch_shapes=list(scratch),
              input_output_aliases=aliases or {}, compiler_params=pltpu.CompilerParams(**params),
              interpret=_INTERPRET)
    if grid:
        kw["grid"] = grid
    if in_specs is not None:
        kw["in_specs"] = in_specs
    if out_specs is not None:
        kw["out_specs"] = out_specs
    return pl.pallas_call(body, **kw)


def _mx(a):
    return a.astype(MXU_DTYPE)


def _dg(a, b, ca, cb):
    return lax.dot_general(_mx(a), _mx(b), (((ca,), (cb,)), ((), ())), preferred_element_type=F32)


def _nn(a, b):
    return _dg(a, b, 1, 0)


def _nt(a, b):
    return _dg(a, b, 1, 1)


def _tn(a, b):
    return _dg(a, b, 0, 0)


def _dot01(m, x):
    hi = x.astype(BF16)
    r1 = x - hi.astype(F32)
    mid = r1.astype(BF16)
    lo = (r1 - mid.astype(F32)).astype(BF16)
    f = lambda t: lax.dot_general(m, t, (((1,), (0,)), ((), ())), preferred_element_type=F32)
    return f(hi) + f(mid) + f(lo)


def _tri(n, upper):
    r = lax.broadcasted_iota(jnp.int32, (n, n), 0)
    c = lax.broadcasted_iota(jnp.int32, (n, n), 1)
    return (c >= r) if upper else (c <= r)


def _b01(mask):
    return jnp.where(mask, 1.0, 0.0).astype(BF16)


def _sig(x):
    return jax.nn.sigmoid(x)


def _silu(x):
    return x * _sig(x)


def _dsilu(x):
    s = _sig(x)
    return s * (1.0 + x * (1.0 - s))


def _softplus(x):
    return jnp.maximum(x, 0.0) + jnp.log(1.0 + jnp.exp(-jnp.abs(x)))


def _rowsum(x):
    return jnp.sum(x, axis=1, keepdims=True)


def _colsum(x):
    return jnp.sum(x, axis=0, keepdims=True)


def _full(shape):
    return pl.BlockSpec(shape, lambda *_: (0,) * len(shape))


def _allgather8(v, name):
    m_per, n = v.shape

    def body(x_ref, out_ref, send_sems, recv_sems, local_sem):
        x, y, c = lax.axis_index("x"), lax.axis_index("y"), lax.axis_index("c")
        me, sibling = (x, y, c), (x, y, 1 - c)
        chips = [(1 - x, y), (x, 1 - y), (1 - x, 1 - y)]

        def rows(px, py, pc):
            return out_ref.at[pl.ds((4 * px + 2 * py + pc) * m_per, m_per), :]

        def copy(k, block, to, src=None):
            return pltpu.make_async_remote_copy(
                src_ref=rows(*block) if src is None else src, dst_ref=rows(*block),
                send_sem=send_sems.at[k], recv_sem=recv_sems.at[k], device_id=to, device_id_type=MESH_ID)

        mine = pltpu.make_async_copy(x_ref, rows(*me), local_sem)
        mine.start()
        first = [copy(0, me, sibling, src=x_ref)]
        first += [copy(1 + j, me, (*chip, c), src=x_ref) for j, chip in enumerate(chips)]
        for cp in first:
            cp.start()
        passed = [copy(4 + j, (*chip, c), sibling) for j, chip in enumerate(chips)]
        for j, chip in enumerate(chips):
            copy(1 + j, (*chip, c), me).wait_recv()
            passed[j].start()
        copy(0, sibling, me).wait_recv()
        for j, chip in enumerate(chips):
            copy(4 + j, (*chip, 1 - c), me).wait_recv()
        for cp in first + passed:
            cp.wait_send()
        mine.wait()

    return _pcall(
        body, name=name, out_shape=jax.ShapeDtypeStruct((8 * m_per, n), v.dtype),
        in_specs=[pl.BlockSpec(memory_space=pltpu.VMEM)], out_specs=pl.BlockSpec(memory_space=pltpu.VMEM),
        scratch=[pltpu.SemaphoreType.DMA((7,)), pltpu.SemaphoreType.DMA((7,)), pltpu.SemaphoreType.DMA],
    )(v)


def _gather_ops(ins, outs, send_sems, recv_sems):
    n = len(ins)
    x, y, c = lax.axis_index("x"), lax.axis_index("y"), lax.axis_index("c")
    sibling = (x, y, 1 - c)
    chips = [(1 - x, y), (x, 1 - y), (1 - x, 1 - y)]

    def part(a, px, py, pc):
        half = ins[a].shape[0] // 2
        return outs[a].at[2 * px + py, pl.ds(pc * half, half), :]

    def copy(a, k, block, to, src=None):
        return pltpu.make_async_remote_copy(
            src_ref=part(a, *block) if src is None else src, dst_ref=part(a, *block),
            send_sem=send_sems.at[6 * a + k], recv_sem=recv_sems.at[6 * a + k], device_id=to,
            device_id_type=MESH_ID)

    def first(a, j):
        half = ins[a].shape[0] // 2
        return copy(a, j, (x, y, c), (*chips[j], c), src=ins[a].at[pl.ds(c * half, half), :])

    def start():
        for a in range(n):
            for j in range(3):
                first(a, j).start()

    def finish():
        for a in range(n):
            for j, chip in enumerate(chips):
                copy(a, j, (*chip, c), (x, y, c)).wait_recv()
                copy(a, 3 + j, (*chip, c), sibling).start()
        for a in range(n):
            for j, chip in enumerate(chips):
                copy(a, 3 + j, (*chip, 1 - c), (x, y, c)).wait_recv()
        for a in range(n):
            for j, chip in enumerate(chips):
                first(a, j).wait_send()
                copy(a, 3 + j, (*chip, c), sibling).wait_send()

    return start, finish


def _gather_out(shards):
    return tuple(jax.ShapeDtypeStruct((4,) + s_.shape, s_.dtype) for s_ in shards)


def _gather_sems(n):
    return [pltpu.SemaphoreType.DMA((6 * n,)), pltpu.SemaphoreType.DMA((6 * n,))]


def _weights_allgather(shards):
    n = len(shards)

    def body(*refs):
        start, finish = _gather_ops(refs[:n], refs[n:2 * n], *refs[2 * n:])
        start()
        finish()

    return _pcall(
        body, name="weights_allgather", out_shape=_gather_out(shards),
        in_specs=[pl.BlockSpec(memory_space=pl.ANY)] * n, out_specs=(pl.BlockSpec(memory_space=pl.ANY),) * n,
        scratch=_gather_sems(n),
    )(*shards)


def _pair_exchange(gs, name):
    n = len(gs)

    def body(*refs):
        ins, outs = refs[:n], refs[n:2 * n]
        send_sems, recv_sems = refs[2 * n:]
        x, y, c = lax.axis_index("x"), lax.axis_index("y"), lax.axis_index("c")
        cps = []
        for a in range(n):
            half = ins[a].shape[1] // 2
            cps.append(pltpu.make_async_remote_copy(
                src_ref=ins[a].at[:, pl.ds((1 - c) * half, half), :], dst_ref=outs[a], send_sem=send_sems.at[a],
                recv_sem=recv_sems.at[a], device_id=(x, y, 1 - c), device_id_type=MESH_ID))
        for cp in cps:
            cp.start()
        for cp in cps:
            cp.wait()

    return _pcall(
        body, name=name,
        out_shape=tuple(jax.ShapeDtypeStruct((g.shape[0], g.shape[1] // 2, g.shape[2]), g.dtype) for g in gs),
        in_specs=[pl.BlockSpec(memory_space=pl.ANY)] * n, out_specs=(pl.BlockSpec(memory_space=pl.ANY),) * n,
        scratch=[pltpu.SemaphoreType.DMA((n,)), pltpu.SemaphoreType.DMA((n,))],
    )(*gs)


def _exchange_ops(ins, outs, send_sems, recv_sems, dests):
    x, y, c = lax.axis_index("x"), lax.axis_index("y"), lax.axis_index("c")
    mine = 2 * x + y
    chips = [(1 - x, y), (x, 1 - y), (1 - x, 1 - y)]

    def each(fn):
        for a in range(len(ins)):
            lo, hi = dests[a]
            for j, (px, py) in enumerate(chips):
                q = 2 * px + py
                cp = pltpu.make_async_remote_copy(
                    src_ref=ins[a].at[jnp.clip(q - lo, 0, hi - lo - 1)], dst_ref=outs[a].at[j],
                    send_sem=send_sems.at[3 * a + j], recv_sem=recv_sems.at[3 * a + j], device_id=(px, py, c),
                    device_id_type=MESH_ID)
                fn(cp, (q >= lo) & (q < hi), (mine >= lo) & (mine < hi), (lo, hi) == (0, 4))

    def start():
        def go(cp, send_ok, recv_ok, always):
            if always:
                cp.start()
            else:
                pl.when(send_ok)(cp.start)
        each(go)

    def finish():
        def go(cp, send_ok, recv_ok, always):
            if always:
                cp.wait()
            else:
                pl.when(send_ok)(cp.wait_send)
                pl.when(recv_ok)(cp.wait_recv)
        each(go)

    return start, finish


def _comm_exchange(hs, dests):
    n = len(hs)
    return (list(hs), tuple(jax.ShapeDtypeStruct((3,) + h.shape[1:], h.dtype) for h in hs),
            [pltpu.SemaphoreType.DMA((3 * n,)), pltpu.SemaphoreType.DMA((3 * n,))],
            lambda i, o, s, r: _exchange_ops(i, o, s, r, dests))


def _comm_gather(shards):
    return (list(shards), _gather_out(shards), _gather_sems(len(shards)), _gather_ops)


def _carry(call, comm, steps):
    if comm is None:
        return call
    arrays, out_shape, sems, make = comm
    n, n_in, n_out = len(arrays), len(call["args"]), len(call["out_shape"])
    body = call["body"]

    def wrapped(*refs):
        base_in, cin = refs[:n_in], refs[n_in:n_in + n]
        rest = refs[n_in + n:]
        base_out, cout, scr = rest[:n_out], rest[n_out:n_out + n], rest[n_out + n:]
        start, finish = make(cin, cout, scr[-2], scr[-1])
        first, last = steps()
        pl.when(first)(start)
        body(*base_in, *base_out, *scr[:-2])
        pl.when(last)(finish)

    anyspec = pl.BlockSpec(memory_space=pl.ANY)
    return dict(call, body=wrapped, args=list(call["args"]) + arrays,
                in_specs=list(call["in_specs"]) + [anyspec] * n,
                out_shape=tuple(call["out_shape"]) + tuple(out_shape),
                out_specs=tuple(call["out_specs"]) + (anyspec,) * n,
                scratch=list(call["scratch"]) + sems)


def _run(call):
    args = call.pop("args")
    body = call.pop("body")
    return _pcall(body, **call)(*args)


def _pair_swap(rs):
    n = len(rs)

    def body(*refs):
        ins, outs = refs[:n], refs[n:2 * n]
        send_sems, recv_sems = refs[2 * n:]
        x, y, c = lax.axis_index("x"), lax.axis_index("y"), lax.axis_index("c")
        cps = [pltpu.make_async_remote_copy(
            src_ref=ins[a], dst_ref=outs[a], send_sem=send_sems.at[a], recv_sem=recv_sems.at[a],
            device_id=(x, y, 1 - c), device_id_type=MESH_ID) for a in range(n)]
        for cp in cps:
            cp.start()
        for cp in cps:
            cp.wait()

    return _pcall(
        body, name="grads_pair_swap", out_shape=tuple(jax.ShapeDtypeStruct(r.shape, r.dtype) for r in rs),
        in_specs=[pl.BlockSpec(memory_space=pl.ANY)] * n, out_specs=(pl.BlockSpec(memory_space=pl.ANY),) * n,
        scratch=[pltpu.SemaphoreType.DMA((n,)), pltpu.SemaphoreType.DMA((n,))],
    )(*rs)


SUM_STEPS = 8


def _pair_sum(gs, recvs, core, name):
    n = len(gs)

    def body(c_ref, *refs):
        for a in range(n):
            refs[2 * n + a][...] = (refs[a][...].astype(F32) + refs[n + a][...].astype(F32)).astype(refs[2 * n + a].dtype)

    blk = lambda g: (g.shape[0], g.shape[1] // (2 * SUM_STEPS), g.shape[2])
    return pl.pallas_call(
        body, name=name,
        out_shape=tuple(jax.ShapeDtypeStruct((g.shape[0], g.shape[1] // 2, g.shape[2]), g.dtype) for g in gs),
        grid_spec=pltpu.PrefetchScalarGridSpec(
            num_scalar_prefetch=1, grid=(SUM_STEPS,),
            in_specs=[pl.BlockSpec(blk(g), lambda i, cr: (0, cr[0] * SUM_STEPS + i, 0)) for g in gs]
            + [pl.BlockSpec(blk(g), lambda i, cr: (0, i, 0)) for g in gs],
            out_specs=tuple(pl.BlockSpec(blk(g), lambda i, cr: (0, i, 0)) for g in gs)),
        compiler_params=pltpu.CompilerParams(vmem_limit_bytes=40 << 20), interpret=_INTERPRET,
    )(core, *gs, *recvs)


def _chip_sum(hs, recvs, chip, dests, slots):
    n = len(hs)
    nout = max(slots) + 1
    first = [slots.index(o) for o in range(nout)]
    every = lambda d_: d_ == (0, 4)

    def own(d_):
        if every(d_):
            return lambda i, kr: (kr[0], i, 0)
        return lambda i, kr: (0, jnp.where(kr[0] == d_[0], i, 0), 0)

    def got(d_):
        if every(d_):
            return lambda i, kr: (0, i, 0)
        return lambda i, kr: (0, jnp.where(kr[0] == d_[0], i, 0), 0)

    def body(k_ref, *refs):
        for a in range(n):
            def emit(a=a):
                acc = refs[a][0].astype(F32)
                for j in range(3):
                    acc = acc + refs[n + a][j].astype(F32)
                refs[2 * n + slots[a]][...] = acc
            if every(dests[a]):
                emit()
            else:
                pl.when(k_ref[0] == dests[a][0])(emit)

    rb = lambda h: h.shape[1] // SUM_STEPS
    return pl.pallas_call(
        body, name="grads_chip_sum",
        out_shape=tuple(jax.ShapeDtypeStruct(hs[a].shape[1:], F32) for a in first),
        grid_spec=pltpu.PrefetchScalarGridSpec(
            num_scalar_prefetch=1, grid=(SUM_STEPS,),
            in_specs=[pl.BlockSpec((1, rb(h), h.shape[2]), own(d_)) for h, d_ in zip(hs, dests)]
            + [pl.BlockSpec((3, rb(h), h.shape[2]), got(d_)) for h, d_ in zip(hs, dests)],
            out_specs=tuple(pl.BlockSpec((rb(hs[a]), hs[a].shape[2]), lambda i, kr: (i, 0)) for a in first)),
        compiler_params=pltpu.CompilerParams(vmem_limit_bytes=40 << 20), interpret=_INTERPRET,
    )(chip, *hs, *recvs)


def _ada_fwd(araw, w, b):
    nblk = w.shape[1] // 512

    def body(a_ref, w_ref, b_ref, o_ref):
        o_ref[...] = _nn(_silu(a_ref[...]), w_ref[...]) + b_ref[...]

    return _pcall(
        body, name="ada_fwd", out_shape=jax.ShapeDtypeStruct((16, w.shape[1]), F32), grid=(nblk,),
        in_specs=[_full((16, D)), pl.BlockSpec((D, 512), lambda j: (0, j)), pl.BlockSpec((1, 512), lambda j: (0, j))],
        out_specs=pl.BlockSpec((16, 512), lambda j: (0, j)), sem=("parallel",),
    )(araw, w, b)


def _ada_bwd(araw, dmod, w):
    nblk = w.shape[1] // 512

    def body(a_ref, d_ref, w_ref, gw_ref, da_ref):
        j = pl.program_id(0)
        gw_ref[...] = _tn(_silu(a_ref[...]), d_ref[...])
        part = _nt(d_ref[...], w_ref[...])

        @pl.when(j == 0)
        def _():
            da_ref[...] = part

        @pl.when(j > 0)
        def _():
            da_ref[...] += part

    return _pcall(
        body, name="ada_bwd",
        out_shape=(jax.ShapeDtypeStruct(w.shape, F32), jax.ShapeDtypeStruct((16, D), F32)), grid=(nblk,),
        in_specs=[_full((16, D)), pl.BlockSpec((16, 512), lambda j: (0, j)), pl.BlockSpec((D, 512), lambda j: (0, j))],
        out_specs=(pl.BlockSpec((D, 512), lambda j: (0, j)), _full((16, D))), sem=("arbitrary",),
    )(araw, dmod, w)


def _w_specs():
    return [pl.BlockSpec((None, D, D), lambda i, j: (j // 2, 0, j % 2)),
            pl.BlockSpec((None, D, WTAIL), lambda i, j: (jnp.maximum(j // 2 - 1, 0), 0, 0)),
            pl.BlockSpec((None, D, WTAIL), lambda i, j: (3, 0, 0))]


def _inproj(xin, mods, wi_main, wi_tail, t_total, tb, blk_off, prev, name, comm=None):
    n = xin.shape[0]
    nt = n // tb
    ncol = 8

    def body(x_ref, mod_ref, w_ref, wb_ref, wdt_ref, *rest):
        p_ref, pdt_ref, u_ref, uscr = rest[-4:]
        j = pl.program_id(1)

        @pl.when(j == 0)
        def _():
            xv = x_ref[...]
            r = lax.rsqrt(jnp.mean(xv * xv, axis=1, keepdims=True) + EPS)
            u = (xv * r * mod_ref[2:3, :]) * mod_ref[0:1, :] + mod_ref[1:2, :]
            ub = u.astype(MXU_DTYPE)
            uscr[...] = ub
            u_ref[...] = ub
            pdt_ref[...] = _nn(ub, wdt_ref[...])

        p_ref[...] = _nn(uscr[...], w_ref[...])

        @pl.when((j % 2 == 0) & (j > 0))
        def _():
            p_ref[:, 0:WTAIL] += _nn(uscr[...], wb_ref[...])

    in_specs = [pl.BlockSpec((tb, D), lambda i, j: (i, 0)), _full((8, D))] + _w_specs()
    args = [xin, mods, wi_main, wi_tail, wi_tail]
    aliases = None
    if prev is not None:
        in_specs += [pl.BlockSpec(memory_space=pl.ANY)] * 3
        args += list(prev)
        aliases = {5: 0, 6: 1, 7: 2}
    call = dict(
        body=body, args=args, name=name,
        out_shape=(jax.ShapeDtypeStruct((t_total, ncol * D), F32), jax.ShapeDtypeStruct((t_total, 128), F32),
                   jax.ShapeDtypeStruct((t_total, D), MXU_DTYPE)),
        grid=(nt, ncol), in_specs=in_specs,
        out_specs=(pl.BlockSpec((tb, D), lambda i, j: (i + blk_off, j)),
                   pl.BlockSpec((tb, 128), lambda i, j: (i + blk_off, 0)),
                   pl.BlockSpec((tb, D), lambda i, j: (i + blk_off, 0))),
        scratch=[pltpu.VMEM((tb, D), MXU_DTYPE)], sem=("arbitrary", "arbitrary"), vmem_mb=48, aliases=aliases)
    steps = lambda: ((pl.program_id(0) == 0) & (pl.program_id(1) == 0),
                     (pl.program_id(0) == nt - 1) & (pl.program_id(1) == ncol - 1))
    return _run(_carry(call, comm, steps))


def _blk(s, nb, rev):
    return jnp.where(s == 0, nb - 1, (nb - 1 - s) if rev else (s - 1))


def _hgrn_gate(fr, lbraw_ref, d):
    lb = _sig(lbraw_ref[d:d + 1, :] - lbraw_ref[2 + d:3 + d, :])
    sg = _sig(fr)
    return lb, sg, lb + (1.0 - lb) * sg


def _hgrn_fwd(p_main, lbraw, d, nb):
    t_total = p_main.shape[0]
    rev = d == 1
    nch = TB // HC
    scale = HF ** -0.5

    def body(q_ref, f_ref, v_ref, lb_ref, o_ref, sp_ref, st):
        s = pl.program_id(0)

        @pl.when(s == 0)
        def _():
            st[...] = jnp.zeros_like(st)

        mb = _tri(HC, rev)
        m01 = _b01(mb)
        order = list(reversed(range(nch)) if rev else range(nch))
        hs_ = [slice(h * HF, (h + 1) * HF) for h in range(NH)]
        pre = {}
        for c in order:
            rows = slice(c * HC, (c + 1) * HC)
            _, _, f = _hgrn_gate(f_ref[rows, :], lb_ref, d)
            k = 1.0 - f
            cum = _dot01(m01, jnp.log(f))
            tot = cum[0:1, :] if rev else cum[HC - 1:HC, :]
            qd = _silu(q_ref[rows, :]) * scale * jnp.exp(cum)
            ki = k * jnp.exp(-cum)
            etot = jnp.exp(tot)
            pre[c] = (_mx(qd), _mx(ki), _mx(ki * etot), _mx(v_ref[rows, :]), etot)
        scs = {c: [_nt(pre[c][0][:, cs], pre[c][1][:, cs]) for cs in hs_] for c in order}
        upd = {c: [_tn(pre[c][3][:, cs], pre[c][2][:, cs]) for cs in hs_] for c in order}
        intra = {c: [_nn(jnp.where(mb, scs[c][h], 0.0), pre[c][3][:, cs]) for h, cs in enumerate(hs_)] for c in order}
        for c in order:
            rows = slice(c * HC, (c + 1) * HC)
            qdb, etot = pre[c][0], pre[c][4]
            for h, cs in enumerate(hs_):
                sth = st[h]
                stb = sth.astype(sp_ref.dtype)
                sp_ref[c, h] = stb
                o_ref[rows, cs] = intra[c][h] + _nt(qdb[:, cs], stb)
                st[h] = sth * etot[:, cs] + upd[c][h]

    col = lambda j: (lambda s: (_blk(s, nb, rev), j))
    return _pcall(
        body, name=f"hgrn_fwd_{d}",
        out_shape=(jax.ShapeDtypeStruct((t_total, D), F32),
                   jax.ShapeDtypeStruct((nch * nb, NH, HF, HF), MXU_DTYPE)),
        grid=(nb,),
        in_specs=[pl.BlockSpec((TB, D), col(0)), pl.BlockSpec((TB, D), col(1 + d)), pl.BlockSpec((TB, D), col(3)),
                  _full((8, D))],
        out_specs=(pl.BlockSpec((TB, D), col(0)),
                   pl.BlockSpec((nch, NH, HF, HF), lambda s: (_blk(s, nb, rev), 0, 0, 0))),
        scratch=[pltpu.VMEM((NH, HF, HF), F32)], sem=("arbitrary",), vmem_mb=40,
    )(p_main, p_main, p_main, lbraw)


def _hgrn_bwd(p_main, lbraw, sprev, do, d, nb, prev, comm=None):
    t_total = p_main.shape[0]
    rev = d == 1
    nch = TB // HC
    scale = HF ** -0.5
    last = prev is not None
    odt = MXU_DTYPE if last else F32

    def body(q_ref, f_ref, v_ref, lb_ref, sp_ref, do_ref, *rest):
        if last:
            dqp_ref, dvp_ref = rest[:2]
            rest = rest[2:]
        dq_ref, df_ref, dv_ref, dlb_ref, dst = rest
        sp_id = pl.program_id(0)
        is_ctx = sp_id == nb - 1

        @pl.when(sp_id == 0)
        def _():
            dst[...] = jnp.zeros_like(dst)
            dlb_ref[...] = jnp.zeros_like(dlb_ref)

        mb = _tri(HC, rev)
        mbt = _tri(HC, not rev)
        m01 = _b01(mb)
        mt01 = _b01(mbt)
        order = list(range(nch) if rev else reversed(range(nch)))
        hs_ = [slice(h * HF, (h + 1) * HF) for h in range(NH)]
        pre = {}
        for c in order:
            rows = slice(c * HC, (c + 1) * HC)
            lb, sg, f = _hgrn_gate(f_ref[rows, :], lb_ref, d)
            k = 1.0 - f
            cum = _dot01(m01, jnp.log(f))
            tot = cum[0:1, :] if rev else cum[HC - 1:HC, :]
            e = jnp.exp(cum)
            ei = jnp.exp(-cum)
            etot = jnp.exp(tot)
            ee = ei * etot
            qraw = q_ref[rows, :]
            qd = _silu(qraw) * scale * e
            ki = k * ei
            ke = k * ee
            dov = jnp.where(is_ctx, 0.0, do_ref[rows, :])
            pre[c] = dict(lb=lb, sg=sg, f=f, e=e, ei=ei, ee=ee, etot=etot, qraw=qraw, qd=qd, ki=ki, ke=ke,
                          qdb=_mx(qd), kib=_mx(ki), keb=_mx(ke), vb=_mx(v_ref[rows, :]), dob=_mx(dov))
        units = [(c, h) for c in order for h in range(NH)]
        col = lambda u, key: pre[u[0]][key][:, hs_[u[1]]]
        pt = {u: jnp.where(mbt, _nt(col(u, "kib"), col(u, "qdb")), 0.0) for u in units}
        dp = {u: jnp.where(mb, _nt(col(u, "dob"), col(u, "vb")), 0.0) for u in units}
        dpt = {u: jnp.where(mbt, _nt(col(u, "vb"), col(u, "dob")), 0.0) for u in units}
        dv_i = {u: _nn(pt[u], col(u, "dob")) for u in units}
        dqd_ = {u: _nn(dp[u], col(u, "kib")) + _nn(col(u, "dob"), sp_ref[u[0], u[1]]) for u in units}
        dki_ = {u: _nn(dpt[u], col(u, "qdb")) for u in units}
        dsl = {u: _tn(col(u, "dob"), col(u, "qdb")) for u in units}
        for c in order:
            rows = slice(c * HC, (c + 1) * HC)
            p = pre[c]
            dv_l, dke_l, dtot_l = [], [], []
            for h, cs in enumerate(hs_):
                dso = dst[h]
                dsob = _mx(dso)
                dv_l.append(dv_i[(c, h)] + _nt(p["keb"][:, cs], dsob))
                dke_l.append(_nn(p["vb"][:, cs], dsob))
                dtot_l.append(_colsum(dso * sp_ref[c, h].astype(F32)) * p["etot"][:, cs])
                dst[h] = dso * p["etot"][:, cs] + dsl[(c, h)]
            lb, sg, f, e, ei, ee, qraw, qd, ki, ke = (p[n_] for n_ in ("lb", "sg", "f", "e", "ei", "ee", "qraw", "qd",
                                                                     "ki", "ke"))
            dqd = jnp.concatenate([dqd_[(c, h)] for h in range(NH)], axis=1)
            dki = jnp.concatenate([dki_[(c, h)] for h in range(NH)], axis=1)
            dke = jnp.concatenate(dke_l, axis=1)
            dcum = dqd * qd - dki * ki - dke * ke
            dtot = jnp.concatenate(dtot_l, axis=1) + _colsum(dke * ke)
            dk = dki * ei + dke * ee
            dlf = _dot01(mt01, dcum) + dtot
            df = dlf / f - dk
            dlb_ref[0:1, :] += _colsum(df * (1.0 - sg))
            dfr = df * (1.0 - lb) * sg * (1.0 - sg)
            dq = dqd * e * scale * _dsilu(qraw)
            dv = jnp.concatenate(dv_l, axis=1)
            if last:
                dq = dq + dqp_ref[rows, :]
                dv = dv + dvp_ref[rows, :]
            dq_ref[rows, :] = dq.astype(odt)
            dv_ref[rows, :] = dv.astype(odt)
            df_ref[rows, :] = dfr.astype(MXU_DTYPE)

    blk = lambda s: _blk(nb - 1 - s, nb, rev)
    col = lambda j: (lambda s: (blk(s), j))
    in_specs = [pl.BlockSpec((TB, D), col(0)), pl.BlockSpec((TB, D), col(1 + d)), pl.BlockSpec((TB, D), col(3)),
                _full((8, D)), pl.BlockSpec((nch, NH, HF, HF), lambda s: (blk(s), 0, 0, 0)),
                pl.BlockSpec((TB, D), lambda s: (jnp.minimum(blk(s), nb - 2), 0))]
    args = [p_main, p_main, p_main, lbraw, sprev, do]
    if last:
        in_specs += [pl.BlockSpec((TB, D), col(0))] * 2
        args += list(prev)
    call = dict(
        body=body, args=args, name=f"hgrn_bwd_{d}",
        out_shape=(jax.ShapeDtypeStruct((t_total, D), odt), jax.ShapeDtypeStruct((t_total, D), MXU_DTYPE),
                   jax.ShapeDtypeStruct((t_total, D), odt), jax.ShapeDtypeStruct((8, D), F32)),
        grid=(nb,), in_specs=in_specs,
        out_specs=(pl.BlockSpec((TB, D), col(0)), pl.BlockSpec((TB, D), col(0)), pl.BlockSpec((TB, D), col(0)),
                   _full((8, D))),
        scratch=[pltpu.VMEM((NH, HF, HF), F32)], sem=("arbitrary",), vmem_mb=48)
    return _run(_carry(call, comm, lambda: (pl.program_id(0) == 0, pl.program_id(0) == nb - 1)))


def _conv_masks(tb, is_ctx):
    seg = jnp.where(is_ctx, tb, GRID_W)
    pos = lax.broadcasted_iota(jnp.int32, (tb, 1), 0) & (seg - 1)
    return pos, seg


def _shift_rows(x, dshift, pos, seg):
    if dshift == 0:
        return x
    n = x.shape[0]
    rolled = pltpu.roll(x, (-dshift) % n, 0)
    ok = (pos + dshift >= 0) & (pos + dshift < seg)
    return jnp.where(ok, rolled, 0.0)


def _ssd_prep(p_main, p_dt, convp, dtb, nb):
    t_total = p_main.shape[0]

    def body(x_ref, dt_ref, cw_ref, dtb_ref, xa_ref, dts_ref):
        is_ctx = pl.program_id(0) == nb - 1
        pos, seg = _conv_masks(TB, is_ctx)
        xv = x_ref[...]
        acc = cw_ref[5:6, :] + cw_ref[2:3, :] * xv
        for kk in (0, 1, 3, 4):
            acc = acc + cw_ref[kk:kk + 1, :] * _shift_rows(xv, kk - 2, pos, seg)
        xa_ref[...] = _silu(acc)
        dts_ref[...] = _softplus(dt_ref[...] + dtb_ref[0:1, :])

    return _pcall(
        body, name="ssd_prep",
        out_shape=(jax.ShapeDtypeStruct((t_total, 2048), F32), jax.ShapeDtypeStruct((t_total, 128), F32)),
        grid=(nb,),
        in_specs=[pl.BlockSpec((TB, 2048), lambda i: (i, 3)), pl.BlockSpec((TB, 128), lambda i: (i, 0)),
                  _full((8, 2048)), _full((8, 128))],
        out_specs=(pl.BlockSpec((TB, 2048), lambda i: (i, 0)), pl.BlockSpec((TB, 128), lambda i: (i, 0))),
        sem=("parallel",), vmem_mb=32,
    )(p_main, p_dt, convp, dtb)


def _ssd_prep_bwd(p_main, p_dt, convp, dtb, dxa, dxs_skip, ddts, nb):
    t_total = p_main.shape[0]

    def body(x_ref, dt_ref, cw_ref, dtb_ref, dxa_ref, dsk_ref, ddts_ref, dx_ref, ddt_ref, dcw_ref, ddtb_ref):
        i = pl.program_id(0)
        is_ctx = i == nb - 1

        @pl.when(i == 0)
        def _():
            dcw_ref[...] = jnp.zeros_like(dcw_ref)
            ddtb_ref[...] = jnp.zeros_like(ddtb_ref)

        pos, seg = _conv_masks(TB, is_ctx)
        xv = x_ref[...]
        sh = {kk: _shift_rows(xv, kk - 2, pos, seg) for kk in range(KCONV)}
        acc = cw_ref[5:6, :]
        for kk in range(KCONV):
            acc = acc + cw_ref[kk:kk + 1, :] * sh[kk]
        dact = dxa_ref[...]
        dact = jnp.concatenate([dact[:, :D] + jnp.where(is_ctx, 0.0, dsk_ref[...]), dact[:, D:]], axis=1)
        dpre = dact * _dsilu(acc)
        dxv = cw_ref[2:3, :] * dpre
        for kk in (0, 1, 3, 4):
            dxv = dxv + cw_ref[kk:kk + 1, :] * _shift_rows(dpre, 2 - kk, pos, seg)
        dx_ref[...] = dxv.astype(dx_ref.dtype)
        for kk in range(KCONV):
            dcw_ref[kk:kk + 1, :] += _colsum(dpre * sh[kk])
        dcw_ref[5:6, :] += _colsum(dpre)
        draw = ddts_ref[...] * _sig(dt_ref[...] + dtb_ref[0:1, :])
        ddt_ref[...] = draw.astype(ddt_ref.dtype)
        ddtb_ref[0:1, :] += _colsum(draw)

    return _pcall(
        body, name="ssd_prep_bwd",
        out_shape=(jax.ShapeDtypeStruct((t_total, 2048), MXU_DTYPE), jax.ShapeDtypeStruct((t_total, 128), MXU_DTYPE),
                   jax.ShapeDtypeStruct((8, 2048), F32), jax.ShapeDtypeStruct((8, 128), F32)),
        grid=(nb,),
        in_specs=[pl.BlockSpec((TB, 2048), lambda i: (i, 3)), pl.BlockSpec((TB, 128), lambda i: (i, 0)),
                  _full((8, 2048)), _full((8, 128)), pl.BlockSpec((TB, 2048), lambda i: (i, 0)),
                  pl.BlockSpec((TB, D), lambda i: (jnp.minimum(i, nb - 2), 0)),
                  pl.BlockSpec((TB, 128), lambda i: (i, 0))],
        out_specs=(pl.BlockSpec((TB, 2048), lambda i: (i, 0)), pl.BlockSpec((TB, 128), lambda i: (i, 0)),
                   _full((8, 2048)), _full((8, 128))),
        sem=("arbitrary",), vmem_mb=40,
    )(p_main, p_dt, convp, dtb, dxa, dxs_skip, ddts)


def _lane_pick(x, lane, col):
    return _rowsum(jnp.where(lane == col, x, 0.0))


def _ssd_chunk_common(dts, alog_ref, m01, rev):
    lane = lax.broadcasted_iota(jnp.int32, (1, 128), 1)
    arow = -jnp.exp(alog_ref[0:1, :])
    cum = _dot01(m01, dts * arow)
    tot = cum[0:1, :] if rev else cum[SC - 1:SC, :]
    return lane, arow, cum, cum.T, tot


def _ssd_fwd(xa, dts, alog, d, nb):
    t_total = xa.shape[0]
    rev = d == 1
    nch = TB // SC
    npair = SHEADS // 2

    def body(xa_ref, dts_ref, alog_ref, y_ref, sp_ref, st):
        s = pl.program_id(0)

        @pl.when(s == 0)
        def _():
            st[...] = jnp.zeros_like(st)

        mb = _tri(SC, rev)
        m01 = _b01(mb)
        lo = lax.broadcasted_iota(jnp.int32, (1, 128), 1) < SP
        rlo = lax.broadcasted_iota(jnp.int32, (128, 1), 0) < SP
        order = list(reversed(range(nch)) if rev else range(nch))
        pre = {}
        for c in order:
            rows = slice(c * SC, (c + 1) * SC)
            dts_c = dts_ref[rows, :]
            lane, arow, cum, cumt, tot = _ssd_chunk_common(dts_c, alog_ref, m01, rev)
            bgs = [_mx(xa_ref[rows, D + g * SN:D + (g + 1) * SN]) for g in range(4)]
            cgs = [_mx(xa_ref[rows, D + 512 + g * SN:D + 512 + (g + 1) * SN]) for g in range(4)]
            pairs = []
            for pr in range(npair):
                xs = xa_ref[rows, pr * 128:(pr + 1) * 128]
                cols = [16 * d + 2 * pr, 16 * d + 2 * pr + 1]
                cum_c = [_lane_pick(cum, lane, q) for q in cols]
                dt_c = [_lane_pick(dts_c, lane, q) for q in cols]
                tot_c = [_lane_pick(tot, lane, q) for q in cols]
                dtx = xs * jnp.where(lo, dt_c[0], dt_c[1])
                e1_pair = jnp.where(lo, jnp.exp(cum_c[0]), jnp.exp(cum_c[1]))
                e2_pair = jnp.where(lo, jnp.exp(tot_c[0] - cum_c[0]), jnp.exp(tot_c[1] - cum_c[1]))
                etot_col = jnp.where(rlo, jnp.exp(tot_c[0]), jnp.exp(tot_c[1]))
                decs = [jnp.where(mb, jnp.exp(cum_c[q] - cumt[cols[q]:cols[q] + 1, :]), 0.0) for q in range(2)]
                dtxq = [_mx(jnp.where(lo if q == 0 else ~lo, dtx, 0.0)) for q in range(2)]
                pairs.append(dict(e1=e1_pair, etot=etot_col, decs=decs, dtxq=dtxq, xe=_mx(dtx * e2_pair)))
            pre[c] = (bgs, cgs, pairs)
        gm = {(c, g): _nt(pre[c][1][g], pre[c][0][g]) for c in order for g in range(4)}
        upd = {(c, pr): _tn(pre[c][2][pr]["xe"], pre[c][0][pr // 2]) for c in order for pr in range(npair)}
        intra = {(c, pr): sum(_nn(gm[(c, pr // 2)] * pre[c][2][pr]["decs"][q], pre[c][2][pr]["dtxq"][q]) for q in range(2))
                 for c in order for pr in range(npair)}
        for c in order:
            rows = slice(c * SC, (c + 1) * SC)
            bgs, cgs, pairs = pre[c]
            for pr in range(npair):
                stp = st[pr]
                stb = stp.astype(sp_ref.dtype)
                sp_ref[c, pr] = stb
                y_ref[rows, pr * 128:(pr + 1) * 128] = intra[(c, pr)] + pairs[pr]["e1"] * _nt(cgs[pr // 2], stb)
                st[pr] = stp * pairs[pr]["etot"] + upd[(c, pr)]

    blk = lambda s: _blk(s, nb, rev)
    return _pcall(
        body, name=f"ssd_fwd_{d}",
        out_shape=(jax.ShapeDtypeStruct((t_total, D), F32),
                   jax.ShapeDtypeStruct((nch * nb, npair, 128, SN), MXU_DTYPE)),
        grid=(nb,),
        in_specs=[pl.BlockSpec((TB, 2048), lambda s: (blk(s), 0)), pl.BlockSpec((TB, 128), lambda s: (blk(s), 0)),
                  _full((8, 128))],
        out_specs=(pl.BlockSpec((TB, D), lambda s: (blk(s), 0)),
                   pl.BlockSpec((nch, npair, 128, SN), lambda s: (blk(s), 0, 0, 0))),
        scratch=[pltpu.VMEM((npair, 128, SN), F32)], sem=("arbitrary",), vmem_mb=40,
    )(xa, dts, alog)


def _ssd_bwd(xa, dts, alog, sprev, dy, d, nb, prev, comm=None):
    t_total = xa.shape[0]
    rev = d == 1
    nch = TB // SC
    npair = SHEADS // 2
    last = prev is not None

    def body(xa_ref, dts_ref, alog_ref, sp_ref, dy_ref, *rest):
        if last:
            dxp_ref, ddp_ref = rest[:2]
            rest = rest[2:]
        dxa_ref, ddts_ref, da_ref, dst = rest
        sp_id = pl.program_id(0)
        is_ctx = sp_id == nb - 1

        @pl.when(sp_id == 0)
        def _():
            dst[...] = jnp.zeros_like(dst)
            da_ref[...] = jnp.zeros_like(da_ref)

        mb = _tri(SC, rev)
        m01 = _b01(mb)
        mt01 = _b01(_tri(SC, not rev))
        lo = lax.broadcasted_iota(jnp.int32, (1, 128), 1) < SP
        rlo = lax.broadcasted_iota(jnp.int32, (128, 1), 0) < SP
        order = list(range(nch) if rev else reversed(range(nch)))
        pre = {}
        for c in order:
            rows = slice(c * SC, (c + 1) * SC)
            dts_c = dts_ref[rows, :]
            lane, arow, cum, cumt, tot = _ssd_chunk_common(dts_c, alog_ref, m01, rev)
            pairs = []
            for pr in range(npair):
                xs = xa_ref[rows, pr * 128:(pr + 1) * 128]
                dyp = jnp.where(is_ctx, 0.0, dy_ref[rows, pr * 128:(pr + 1) * 128])
                cols = [16 * d + 2 * pr, 16 * d + 2 * pr + 1]
                cum_c = [_lane_pick(cum, lane, q) for q in cols]
                dt_c = [_lane_pick(dts_c, lane, q) for q in cols]
                tot_c = [_lane_pick(tot, lane, q) for q in cols]
                e1_c = [jnp.exp(cum_c[q]) for q in range(2)]
                e2_c = [jnp.exp(tot_c[q] - cum_c[q]) for q in range(2)]
                etot_c = [jnp.exp(tot_c[q]) for q in range(2)]
                dt_pair = jnp.where(lo, dt_c[0], dt_c[1])
                e1_pair = jnp.where(lo, e1_c[0], e1_c[1])
                e2_pair = jnp.where(lo, e2_c[0], e2_c[1])
                dtx = xs * dt_pair
                decs = [jnp.where(mb, jnp.exp(cum_c[q] - cumt[cols[q]:cols[q] + 1, :]), 0.0) for q in range(2)]
                dyq = [_mx(jnp.where(lo if q == 0 else ~lo, dyp, 0.0)) for q in range(2)]
                pairs.append(dict(xs=xs, dyp=dyp, cols=cols, e1_c=e1_c, e2_c=e2_c, etot_c=etot_c, dt_pair=dt_pair,
                                  e2_pair=e2_pair, etot_col=jnp.where(rlo, etot_c[0], etot_c[1]), dtx=dtx,
                                  dtxb=_mx(dtx), xeb=_mx(dtx * e2_pair), dy0b=_mx(dyp * e1_pair), decs=decs, dyq=dyq))
            pre[c] = dict(lane=lane, arow=arow, dts=dts_c, pairs=pairs,
                          bgb=[_mx(xa_ref[rows, D + g * SN:D + (g + 1) * SN]) for g in range(4)],
                          cgb=[_mx(xa_ref[rows, D + 512 + g * SN:D + 512 + (g + 1) * SN]) for g in range(4)])
        units = [(c, pr) for c in order for pr in range(npair)]
        P = lambda u: pre[u[0]]["pairs"][u[1]]
        cgu = lambda u: pre[u[0]]["cgb"][u[1] // 2]
        gm = {(c, g): _nt(pre[c]["cgb"][g], pre[c]["bgb"][g]) for c in order for g in range(4)}
        y0 = {u: _nt(cgu(u), sp_ref[u[0], u[1]]) for u in units}
        dcg_i = {u: _nn(P(u)["dy0b"], sp_ref[u[0], u[1]]) for u in units}
        dsl = {u: _tn(P(u)["dy0b"], cgu(u)) for u in units}
        w_ = {(u, q): gm[(u[0], u[1] // 2)] * P(u)["decs"][q] for u in units for q in range(2)}
        dw_ = {(u, q): jnp.where(mb, _nt(P(u)["dyq"][q], P(u)["dtxb"]), 0.0) for u in units for q in range(2)}
        ddtx_i = {(u, q): _tn(w_[(u, q)], P(u)["dyq"][q]) for u in units for q in range(2)}
        for c in order:
            rows = slice(c * SC, (c + 1) * SC)
            pc = pre[c]
            lane, arow, dts_c = pc["lane"], pc["arow"], pc["dts"]
            dcum = jnp.zeros((SC, 128), F32)
            ddt = jnp.zeros((SC, 128), F32)
            dtot = jnp.zeros((1, 128), F32)
            dgm = [jnp.zeros((SC, SC), F32) for _ in range(4)]
            dbg = [jnp.zeros((SC, SN), F32) for _ in range(4)]
            dcg = [jnp.zeros((SC, SN), F32) for _ in range(4)]
            for pr in range(npair):
                u, g, p = (c, pr), pr // 2, pc["pairs"][pr]
                dso = dst[pr]
                dsob = _mx(dso)
                dxe = _nt(pc["bgb"][g], dsob)
                dbg[g] = dbg[g] + _nn(p["xeb"], dsob)
                ddtx = dxe * p["e2_pair"]
                de2 = dxe * p["dtx"]
                dcg[g] = dcg[g] + dcg_i[u]
                de1 = p["dyp"] * y0[u]
                rsum = _rowsum(dso * sp_ref[c, pr].astype(F32))
                dst[pr] = dso * p["etot_col"] + dsl[u]
                for q in range(2):
                    hm = lo if q == 0 else ~lo
                    col = p["cols"][q]
                    dw = dw_[(u, q)]
                    ddtx = ddtx + jnp.where(hm, ddtx_i[(u, q)], 0.0)
                    dgm[g] = dgm[g] + dw * p["decs"][q]
                    z = dw * w_[(u, q)]
                    de1q = _rowsum(jnp.where(hm, de1, 0.0))
                    de2q = _rowsum(jnp.where(hm, de2, 0.0))
                    dcum_q = _rowsum(z) - _rowsum(z.T) + de1q * p["e1_c"][q] - de2q * p["e2_c"][q]
                    rs = rsum[0:SP, :] if q == 0 else rsum[SP:2 * SP, :]
                    dtot_q = _colsum(de2q * p["e2_c"][q]) + _colsum(rs) * p["etot_c"][q]
                    dcum = jnp.where(lane == col, dcum_q, dcum)
                    dtot = jnp.where(lane == col, dtot_q, dtot)
                dxs = ddtx * p["dt_pair"]
                ddt_pair = ddtx * p["xs"]
                for q in range(2):
                    hm = lo if q == 0 else ~lo
                    ddt = jnp.where(lane == p["cols"][q], _rowsum(jnp.where(hm, ddt_pair, 0.0)), ddt)
                if last:
                    dxs = dxs + dxp_ref[rows, pr * 128:(pr + 1) * 128]
                dxa_ref[rows, pr * 128:(pr + 1) * 128] = dxs
            for g in range(4):
                db = dbg[g] + _tn(dgm[g], pc["cgb"][g])
                dc = dcg[g] + _nn(dgm[g], pc["bgb"][g])
                if last:
                    db = db + dxp_ref[rows, D + g * SN:D + (g + 1) * SN]
                    dc = dc + dxp_ref[rows, D + 512 + g * SN:D + 512 + (g + 1) * SN]
                dxa_ref[rows, D + g * SN:D + (g + 1) * SN] = db
                dxa_ref[rows, D + 512 + g * SN:D + 512 + (g + 1) * SN] = dc
            dla = _dot01(mt01, dcum) + dtot
            ddt = ddt + dla * arow
            da_ref[0:1, :] += _colsum(dla * dts_c)
            if last:
                ddt = ddt + ddp_ref[rows, :]
            ddts_ref[rows, :] = ddt

    blk = lambda s: _blk(nb - 1 - s, nb, rev)
    in_specs = [pl.BlockSpec((TB, 2048), lambda s: (blk(s), 0)), pl.BlockSpec((TB, 128), lambda s: (blk(s), 0)),
                _full((8, 128)), pl.BlockSpec((nch, npair, 128, SN), lambda s: (blk(s), 0, 0, 0)),
                pl.BlockSpec((TB, D), lambda s: (jnp.minimum(blk(s), nb - 2), 0))]
    args = [xa, dts, alog, sprev, dy]
    if last:
        in_specs += [pl.BlockSpec((TB, 2048), lambda s: (blk(s), 0)), pl.BlockSpec((TB, 128), lambda s: (blk(s), 0))]
        args += list(prev)
    call = dict(
        body=body, args=args, name=f"ssd_bwd_{d}",
        out_shape=(jax.ShapeDtypeStruct((t_total, 2048), F32), jax.ShapeDtypeStruct((t_total, 128), F32),
                   jax.ShapeDtypeStruct((8, 128), F32)),
        grid=(nb,), in_specs=in_specs,
        out_specs=(pl.BlockSpec((TB, 2048), lambda s: (blk(s), 0)), pl.BlockSpec((TB, 128), lambda s: (blk(s), 0)),
                   _full((8, 128))),
        scratch=[pltpu.VMEM((npair, 128, SN), F32)], sem=("arbitrary",), vmem_mb=48)
    return _run(_carry(call, comm, lambda: (pl.program_id(0) == 0, pl.program_id(0) == nb - 1)))


def _readout(o, g, yy, z, vec_ref):
    hg, ss, keep = [], [], []
    for h in range(NH):
        cs = slice(h * HF, (h + 1) * HF)
        oh = o[:, cs]
        r = lax.rsqrt(jnp.mean(oh * oh, axis=1, keepdims=True) + EPS)
        hg.append(oh * r * vec_ref[0:1, cs] * _silu(g[:, cs]))
        keep.append(r)
    u = yy * _silu(z)
    for gi in range(4):
        cs = slice(gi * 256, (gi + 1) * 256)
        ug = u[:, cs]
        r = lax.rsqrt(jnp.mean(ug * ug, axis=1, keepdims=True) + EPS)
        ss.append(ug * r * vec_ref[2:3, cs])
        keep.append(r)
    return jnp.concatenate(hg, axis=1), jnp.concatenate(ss, axis=1), keep, u


def _mix_out(o_f, o_b, p_main, y_f, y_b, xa, x, vecs, w_out):
    n = x.shape[0]

    def body(of_ref, ob_ref, g_ref, z_ref, yf_ref, yb_ref, xs_ref, x_ref, vec_ref, w_ref,
             ymix_ref, ylat_ref, h1_ref, u2_ref):
        o = of_ref[...] + ob_ref[...]
        yy = yf_ref[...] + yb_ref[...] + vec_ref[1:2, :] * xs_ref[...]
        hg, ss, _, _ = _readout(o, g_ref[...], yy, z_ref[...], vec_ref)
        ymix = jnp.concatenate([hg, ss], axis=1).astype(MXU_DTYPE)
        ymix_ref[...] = ymix
        ylat = _nn(ymix, w_ref[...])
        ylat_ref[...] = ylat
        h1 = x_ref[...] + vec_ref[3:4, :] * ylat
        h1_ref[...] = h1
        r = lax.rsqrt(jnp.mean(h1 * h1, axis=1, keepdims=True) + EPS)
        u2_ref[...] = ((h1 * r * vec_ref[6:7, :]) * vec_ref[4:5, :] + vec_ref[5:6, :]).astype(MXU_DTYPE)

    row = lambda j: (lambda i: (i, j))
    return _pcall(
        body, name="mix_out",
        out_shape=(jax.ShapeDtypeStruct((n, 2 * D), MXU_DTYPE), jax.ShapeDtypeStruct((n, D), F32),
                   jax.ShapeDtypeStruct((n, D), F32), jax.ShapeDtypeStruct((n, D), MXU_DTYPE)),
        grid=(n // TB,),
        in_specs=[pl.BlockSpec((TB, D), row(0)), pl.BlockSpec((TB, D), row(0)), pl.BlockSpec((TB, D), row(4)),
                  pl.BlockSpec((TB, D), row(5)), pl.BlockSpec((TB, D), row(0)), pl.BlockSpec((TB, D), row(0)),
                  pl.BlockSpec((TB, D), row(0)), pl.BlockSpec((TB, D), row(0)), _full((8, D)), _full((2 * D, D))],
        out_specs=(pl.BlockSpec((TB, 2 * D), row(0)), pl.BlockSpec((TB, D), row(0)), pl.BlockSpec((TB, D), row(0)),
                   pl.BlockSpec((TB, D), row(0))),
        sem=("parallel",), vmem_mb=48,
    )(o_f, o_b, p_main, p_main, y_f, y_b, xa, x, vecs, w_out)


def _mix_bwd(dylat, o_f, o_b, p_main, y_f, y_b, xa, vecs, w_out):
    n = dylat.shape[0]
    t_total = p_main.shape[0]
    nlat = n // TB

    def body(*refs):
        dg_ref, dz_ref, acc_ref = refs[11], refs[13], refs[15]
        i = pl.program_id(0)

        @pl.when(i == 0)
        def _():
            acc_ref[...] = jnp.zeros_like(acc_ref)

        @pl.when(i < nlat)
        def _():
            compute(*refs)

        @pl.when(i == nlat)
        def _():
            dg_ref[...] = jnp.zeros_like(dg_ref)
            dz_ref[...] = jnp.zeros_like(dz_ref)

    def compute(dyl_ref, of_ref, ob_ref, g_ref, z_ref, yf_ref, yb_ref, xs_ref, vec_ref, w_ref,
                do_ref, dg_ref, dys_ref, dz_ref, dxs_ref, acc_ref):
        dymix = _nt(dyl_ref[...], w_ref[...])
        o = of_ref[...] + ob_ref[...]
        g = g_ref[...]
        z = z_ref[...]
        xs = xs_ref[...]
        yy = yf_ref[...] + yb_ref[...] + vec_ref[1:2, :] * xs
        _, _, keep, u = _readout(o, g, yy, z, vec_ref)
        do_l, dg_l = [], []
        for h in range(NH):
            cs = slice(h * HF, (h + 1) * HF)
            oh, gh, r, wv = o[:, cs], g[:, cs], keep[h], vec_ref[0:1, cs]
            dhg = dymix[:, cs]
            xh = oh * r
            dn = dhg * _silu(gh)
            dg_l.append(dhg * xh * wv * _dsilu(gh))
            acc_ref[0:1, cs] += _colsum(dn * xh)
            dxh = dn * wv
            do_l.append(r * (dxh - xh * jnp.mean(dxh * xh, axis=1, keepdims=True)))
        du_l = []
        for gi in range(4):
            cs = slice(gi * 256, (gi + 1) * 256)
            ug, r, wv = u[:, cs], keep[NH + gi], vec_ref[2:3, cs]
            dss = dymix[:, D + gi * 256:D + (gi + 1) * 256]
            xh = ug * r
            acc_ref[2:3, cs] += _colsum(dss * xh)
            dxh = dss * wv
            du_l.append(r * (dxh - xh * jnp.mean(dxh * xh, axis=1, keepdims=True)))
        du = jnp.concatenate(du_l, axis=1)
        dyy = du * _silu(z)
        do_ref[...] = jnp.concatenate(do_l, axis=1)
        dg_ref[...] = jnp.concatenate(dg_l, axis=1).astype(dg_ref.dtype)
        dys_ref[...] = dyy
        dz_ref[...] = (du * yy * _dsilu(z)).astype(dz_ref.dtype)
        dxs_ref[...] = dyy * vec_ref[1:2, :]
        acc_ref[1:2, :] += _colsum(dyy * xs)

    row = lambda j: (lambda i: (jnp.minimum(i, nlat - 1), j))
    lat = pl.BlockSpec((TB, D), row(0))
    tok = pl.BlockSpec((TB, D), lambda i: (i, 0))
    return _pcall(
        body, name="mix_bwd",
        out_shape=(jax.ShapeDtypeStruct((n, D), F32), jax.ShapeDtypeStruct((t_total, D), MXU_DTYPE),
                   jax.ShapeDtypeStruct((n, D), F32), jax.ShapeDtypeStruct((t_total, D), MXU_DTYPE),
                   jax.ShapeDtypeStruct((n, D), F32), jax.ShapeDtypeStruct((8, D), F32)),
        grid=(t_total // TB,),
        in_specs=[lat, lat, lat, pl.BlockSpec((TB, D), row(4)), pl.BlockSpec((TB, D), row(5)), lat, lat, lat,
                  _full((8, D)), _full((2 * D, D))],
        out_specs=(lat, tok, lat, tok, lat, _full((8, D))),
        sem=("arbitrary",), vmem_mb=48,
    )(dylat, o_f, o_b, p_main, p_main, y_f, y_b, xa, vecs, w_out)


def _ffn_up(u2, w_gate, w_up):
    n = u2.shape[0]
    tb = 1024

    def body(u_ref, wg_ref, wu_ref, g_ref, up_ref, a_ref):
        uv = u_ref[...]
        gt = _nn(uv, wg_ref[...])
        upv = _nn(uv, wu_ref[...])
        g_ref[...] = gt.astype(g_ref.dtype)
        up_ref[...] = upv.astype(up_ref.dtype)
        a_ref[...] = (_silu(gt) * upv).astype(a_ref.dtype)

    blk = pl.BlockSpec((tb, FSL), lambda i, j: (i, j))
    wblk = pl.BlockSpec((None, D, FSL), lambda i, j: (j, 0, 0))
    return _pcall(
        body, name="ffn_up",
        out_shape=(jax.ShapeDtypeStruct((n, DFFP), MXU_DTYPE),) * 3,
        grid=(n // tb, 4), in_specs=[pl.BlockSpec((tb, D), lambda i, j: (i, 0)), wblk, wblk],
        out_specs=(blk, blk, blk), sem=("parallel", "parallel"), vmem_mb=48,
    )(u2, w_gate, w_up)


def _ffn_down_loss(act, w_down, h1, tgt, vecs):
    n = act.shape[0]
    tb = 512

    def body(a_ref, w_ref, h1_ref, t_ref, vec_ref, dh2_ref, dffn_ref, acc_ref):
        i = pl.program_id(0)

        @pl.when(i == 0)
        def _():
            acc_ref[...] = jnp.zeros_like(acc_ref)

        ffn = _nn(a_ref[...], w_ref[...])
        g2 = vec_ref[0:1, :]
        fw = vec_ref[1:2, :]
        h2 = h1_ref[...] + g2 * ffn
        r = lax.rsqrt(jnp.mean(h2 * h2, axis=1, keepdims=True) + EPS)
        xh = h2 * r
        err = xh * fw - t_ref[...]
        dy = err * (1.0 / D)
        acc_ref[2:3, :] += _colsum(err * err) * (0.5 / D)
        acc_ref[1:2, :] += _colsum(dy * xh)
        dxh = dy * fw
        dh2 = r * (dxh - xh * jnp.mean(dxh * xh, axis=1, keepdims=True))
        dh2_ref[...] = dh2
        dffn_ref[...] = (g2 * dh2).astype(dffn_ref.dtype)
        acc_ref[0:1, :] += _colsum(dh2 * ffn)

    return _pcall(
        body, name="ffn_down_loss",
        out_shape=(jax.ShapeDtypeStruct((n, D), F32), jax.ShapeDtypeStruct((n, D), MXU_DTYPE),
                   jax.ShapeDtypeStruct((8, D), F32)),
        grid=(n // tb,),
        in_specs=[pl.BlockSpec((tb, DFFP), lambda i: (i, 0)), _full((DFFP, D)), pl.BlockSpec((tb, D), lambda i: (i, 0)),
                  pl.BlockSpec((tb, D), lambda i: (i, 0)), _full((8, D))],
        out_specs=(pl.BlockSpec((tb, D), lambda i: (i, 0)), pl.BlockSpec((tb, D), lambda i: (i, 0)), _full((8, D))),
        sem=("arbitrary",), vmem_mb=48,
    )(act, w_down, h1, tgt, vecs)


def _ffn_bwd(dffn, w_down, gate, up, w_gate, w_up, h1, ylat, dh2, vecs):
    n = dffn.shape[0]
    tb = 512

    def body(df_ref, wd_ref, g_ref, up_ref, wg_ref, wu_ref, h1_ref, yl_ref, dh2_ref, vec_ref,
             dg_ref, dup_ref, dh1_ref, dyl_ref, acc_ref, du_scr):
        i, j = pl.program_id(0), pl.program_id(1)

        @pl.when((i == 0) & (j == 0))
        def _():
            acc_ref[...] = jnp.zeros_like(acc_ref)

        dact = _nt(df_ref[...], wd_ref[...])
        gt = g_ref[...].astype(F32)
        upv = up_ref[...].astype(F32)
        dgt = (dact * upv * _dsilu(gt)).astype(MXU_DTYPE)
        dupv = (dact * _silu(gt)).astype(MXU_DTYPE)
        dg_ref[...] = dgt
        dup_ref[...] = dupv
        part = _nt(dgt, wg_ref[...]) + _nt(dupv, wu_ref[...])

        @pl.when(j == 0)
        def _():
            du_scr[...] = part

        @pl.when((j > 0) & (j < 3))
        def _():
            du_scr[...] += part

        @pl.when(j == 3)
        def _():
            du = du_scr[...] + part
            h1 = h1_ref[...]
            r = lax.rsqrt(jnp.mean(h1 * h1, axis=1, keepdims=True) + EPS)
            xh = h1 * r
            nw = vec_ref[2:3, :]
            acc_ref[0:1, :] += _colsum(du)
            acc_ref[1:2, :] += _colsum(du * xh * nw)
            dn = du * vec_ref[1:2, :]
            acc_ref[2:3, :] += _colsum(dn * xh)
            dxh = dn * nw
            dh1 = dh2_ref[...] + r * (dxh - xh * jnp.mean(dxh * xh, axis=1, keepdims=True))
            dh1_ref[...] = dh1
            dyl_ref[...] = (vec_ref[0:1, :] * dh1).astype(dyl_ref.dtype)
            acc_ref[3:4, :] += _colsum(dh1 * yl_ref[...])

    tok = pl.BlockSpec((tb, D), lambda i, j: (i, 0))
    ffb = pl.BlockSpec((tb, FSL), lambda i, j: (i, j))
    wsl = pl.BlockSpec((None, D, FSL), lambda i, j: (j, 0, 0))
    return _pcall(
        body, name="ffn_bwd",
        out_shape=(jax.ShapeDtypeStruct((n, DFFP), MXU_DTYPE), jax.ShapeDtypeStruct((n, DFFP), MXU_DTYPE),
                   jax.ShapeDtypeStruct((n, D), F32), jax.ShapeDtypeStruct((n, D), MXU_DTYPE),
                   jax.ShapeDtypeStruct((8, D), F32)),
        grid=(n // tb, 4),
        in_specs=[tok, pl.BlockSpec((FSL, D), lambda i, j: (j, 0)), ffb, ffb, wsl, wsl,
                  tok, tok, tok, _full((8, D))],
        out_specs=(ffb, ffb, tok, tok, _full((8, D))),
        scratch=[pltpu.VMEM((tb, D), F32)], sem=("arbitrary", "arbitrary"), vmem_mb=48,
    )(dffn, w_down, gate, up, w_gate, w_up, h1, ylat, dh2, vecs)


def _deep_rows(rows):
    return max(r for r in range(128, 2305, 128) if rows % r == 0)


def _dw(a, b, name, slabs=None):
    tn_rows = a.shape[0]
    bt = _deep_rows(tn_rows)
    kk, nn_ = a.shape[1], b.shape[1]
    bk = 1024 if kk % 1024 == 0 else kk
    bn = slabs if slabs is not None else (1024 if nn_ % 1024 == 0 else nn_)
    nt = tn_rows // bt
    ka, nb_ = kk // bk, nn_ // bn

    def body(a_ref, b_ref, o_ref, acc):
        t = pl.program_id(2)
        part = _tn(a_ref[...], b_ref[...])

        @pl.when(t == 0)
        def _():
            acc[...] = part

        @pl.when(t > 0)
        def _():
            acc[...] += part

        @pl.when(t == nt - 1)
        def _():
            o_ref[...] = acc[...].astype(o_ref.dtype)

    if slabs is None:
        out_shape = jax.ShapeDtypeStruct((kk, nn_), MXU_DTYPE)
        out_spec = pl.BlockSpec((bk, bn), lambda i, j, t: (i, j))
    else:
        out_shape = jax.ShapeDtypeStruct((nb_, kk, bn), MXU_DTYPE)
        out_spec = pl.BlockSpec((None, bk, bn), lambda i, j, t: (j, i, 0))
    return _pcall(
        body, name=name, out_shape=out_shape, grid=(ka, nb_, nt),
        in_specs=[pl.BlockSpec((bt, bk), lambda i, j, t: (t, i)), pl.BlockSpec((bt, bn), lambda i, j, t: (t, j))],
        out_specs=out_spec, scratch=[pltpu.VMEM((bk, bn), F32)],
        sem=("parallel", "parallel", "arbitrary"), vmem_mb=40,
    )(a, b)


def _dw_in(u_all, segs, name):
    tiles = []
    for m, s_ in enumerate(segs):
        tiles += [(m, h) for h in range(s_.shape[1] // D)]
    ntile = len(tiles)
    t_total = u_all.shape[0]
    bt = _deep_rows(t_total)
    nt = t_total // bt

    def body(u_ref, *refs):
        seg_refs, o_ref, acc = refs[:len(segs)], refs[len(segs)], refs[len(segs) + 1]
        n, t = pl.program_id(0), pl.program_id(1)
        for k, (m, _) in enumerate(tiles):
            @pl.when(n == k)
            def _(m=m):
                part = _tn(u_ref[...], seg_refs[m][...])

                @pl.when(t == 0)
                def _():
                    acc[...] = part

                @pl.when(t > 0)
                def _():
                    acc[...] += part

        @pl.when(t == nt - 1)
        def _():
            o_ref[...] = acc[...].astype(o_ref.dtype)

    def seg_spec(m):
        ks = [k for k, (mm, _) in enumerate(tiles) if mm == m]
        lo, hi = ks[0], ks[-1]
        on = lambda n: (n >= lo) & (n <= hi)
        return pl.BlockSpec((bt, D), lambda n, t: (jnp.where(on(n), t, 0), jnp.where(on(n), n - lo, 0)))

    return _pcall(
        body, name=name, out_shape=jax.ShapeDtypeStruct((ntile // 2, D, 2 * D), MXU_DTYPE), grid=(ntile, nt),
        in_specs=[pl.BlockSpec((bt, D), lambda n, t: (t, 0))] + [seg_spec(m) for m in range(len(segs))],
        out_specs=pl.BlockSpec((None, D, D), lambda n, t: (n // 2, 0, n % 2)),
        scratch=[pltpu.VMEM((D, D), F32)], sem=("parallel", "arbitrary"), vmem_mb=56,
    )(u_all, *segs)


def _du_prenorm_bwd(segs, ddt, wi_main, wi_tail, xin, mods, dres, row_off, tb, name, comm=None):
    n = xin.shape[0]
    nt = n // tb
    off = row_off // tb
    has_dx = dres is not None

    def body(*refs):
        seg_refs = refs[:7]
        ddt_ref, w_ref, wb_ref, wdt_ref, x_ref, mod_ref = refs[7:13]
        rest = refs[13:]
        if has_dx:
            dres_ref, dx_ref, acc_ref, du_scr = rest
        else:
            acc_ref, du_scr = rest
        i, j = pl.program_id(0), pl.program_id(1)

        @pl.when((i == 0) & (j == 0))
        def _():
            acc_ref[...] = jnp.zeros_like(acc_ref)

        @pl.when(j == 0)
        def _():
            du_scr[...] = _nt(ddt_ref[...], wdt_ref[...])

        for k in range(8):
            if not has_dx and k in (4, 5):
                continue

            @pl.when(j == k)
            def _(k=k):
                if k < 6:
                    sv = seg_refs[k][...]
                else:
                    sv = seg_refs[6][:, (k - 6) * D:(k - 5) * D]
                du_scr[...] += _nt(sv, w_ref[...])
                if k in (2, 4, 6):
                    du_scr[...] += _nt(sv[:, 0:WTAIL], wb_ref[...])

        @pl.when(j == 7)
        def _():
            du = du_scr[...]
            xv = x_ref[...]
            r = lax.rsqrt(jnp.mean(xv * xv, axis=1, keepdims=True) + EPS)
            xh = xv * r
            nw = mod_ref[1:2, :]
            acc_ref[0:1, :] += _colsum(du)
            acc_ref[1:2, :] += _colsum(du * xh * nw)
            dn = du * mod_ref[0:1, :]
            acc_ref[2:3, :] += _colsum(dn * xh)
            if has_dx:
                dxh = dn * nw
                dx_ref[...] = dres_ref[...] + r * (dxh - xh * jnp.mean(dxh * xh, axis=1, keepdims=True))

    tokT = lambda w: pl.BlockSpec((tb, w), lambda i, j: (i + off, 0))
    tok = pl.BlockSpec((tb, D), lambda i, j: (i, 0))
    in_specs = [tokT(D)] * 6 + [tokT(2 * D), tokT(128)] + _w_specs() + [tok, _full((8, D))]
    args = list(segs) + [ddt, wi_main, wi_tail, wi_tail, xin, mods]
    out_shape = [jax.ShapeDtypeStruct((8, D), F32)]
    out_specs = [_full((8, D))]
    if has_dx:
        in_specs.append(tok)
        args.append(dres)
        out_shape.insert(0, jax.ShapeDtypeStruct((n, D), F32))
        out_specs.insert(0, tok)
    call = dict(body=body, args=args, name=name, out_shape=tuple(out_shape), grid=(nt, 8), in_specs=in_specs,
                out_specs=tuple(out_specs), scratch=[pltpu.VMEM((tb, D), F32)], sem=("arbitrary", "arbitrary"),
                vmem_mb=56)
    steps = lambda: ((pl.program_id(0) == 0) & (pl.program_id(1) == 0),
                     (pl.program_id(0) == nt - 1) & (pl.program_id(1) == 7))
    return _run(_carry(call, comm, steps))


def _sum8(v):
    def body(v_ref, o_ref):
        acc = v_ref[0]
        for k in range(1, 8):
            acc = acc + v_ref[k]
        o_ref[...] = acc

    return _pcall(body, name="small_sum", out_shape=jax.ShapeDtypeStruct(v.shape[1:], F32),
                  in_specs=[pl.BlockSpec(memory_space=pltpu.VMEM)], out_specs=pl.BlockSpec(memory_space=pltpu.VMEM))(v)


def _adamw(w, m, v, g, name):
    rows, cols = w.shape
    rb = 256 if rows % 256 == 0 else (352 if rows % 352 == 0 else rows)
    c1 = 1.0 - B1 ** STEP
    c2 = 1.0 - B2 ** STEP

    def body(w_ref, m_ref, v_ref, g_ref, d_ref, nm_ref, nv_ref):
        gv = g_ref[...]
        mn = B1 * m_ref[...] + (1.0 - B1) * gv
        vn = B2 * v_ref[...] + (1.0 - B2) * (gv * gv)
        nm_ref[...] = mn
        nv_ref[...] = vn
        d_ref[...] = -LR * ((mn / c1) / (jnp.sqrt(vn / c2) + AEPS) + WD * w_ref[...])

    spec = pl.BlockSpec((rb, cols), lambda i: (i, 0))
    return _pcall(
        body, name=name, out_shape=(jax.ShapeDtypeStruct(w.shape, F32),) * 3, grid=(rows // rb,),
        in_specs=[spec] * 4, out_specs=(spec,) * 3, sem=("parallel",), vmem_mb=40,
    )(w, m, v, g)


def _rows(v, n):
    f = v.reshape(-1)
    return jnp.pad(f, (0, n * D - f.shape[0])).reshape(n, D)


def kernel(x, c, ctx, c_ctx, w_ada, b_ada, norm_mix, w_in, conv_w, conv_b, ssd_a_log, ssd_dt_bias, ssd_d, ssd_norm, hgrn_lb_raw, hgrn_norm, w_out, norm_ffn, w_gate, w_up, w_down, final_norm, loss_target, m_c_ctx, m_w_ada, m_b_ada, m_norm_mix, m_w_in, m_conv_w, m_conv_b, m_ssd_a_log, m_ssd_dt_bias, m_ssd_d, m_ssd_norm, m_hgrn_lb_raw, m_hgrn_norm, m_w_out, m_norm_ffn, m_w_gate, m_w_up, m_w_down, m_final_norm, v_c_ctx, v_w_ada, v_b_ada, v_norm_mix, v_w_in, v_conv_w, v_conv_b, v_ssd_a_log, v_ssd_dt_bias, v_ssd_d, v_ssd_norm, v_hgrn_lb_raw, v_hgrn_norm, v_w_out, v_norm_ffn, v_w_gate, v_w_up, v_w_down, v_final_norm):
    ix, iy, ic = lax.axis_index("x"), lax.axis_index("y"), lax.axis_index("c")
    chip = 2 * ix + iy
    me = 2 * chip + ic
    xl, xc, tgt = x[0], ctx[0], loss_target[0]
    n_lat, n_ctx = xl.shape[0], xc.shape[0]
    assert n_ctx == TB and n_lat % 1024 == 0
    t_total = n_lat + n_ctx
    nb = t_total // TB

    pack = jnp.concatenate([c, hgrn_lb_raw.reshape(1, D), _rows(conv_w[0], 3), jnp.zeros((3, D), F32)], axis=0)
    gath = _allgather8(pack, "small_gather").reshape(8, 8, D)
    c_all = gath[:, 0]
    lbraw_full = gath[0::2, 1].reshape(4, 2, 2, 256).transpose(1, 2, 0, 3).reshape(4, D)
    convw_full = gath[0::2, 2:5].reshape(4, 3 * D)[:, :KCONV * 512].reshape(4, KCONV, 512).transpose(1, 0, 2)
    convw_full = convw_full.reshape(KCONV, 2048)
    lbraw8 = jnp.pad(lbraw_full, ((0, 4), (0, 0)))
    convp = jnp.concatenate([convw_full, conv_b, jnp.zeros((2, 2048), F32)], axis=0)
    dtb = jnp.pad(ssd_dt_bias.reshape(1, 32), ((0, 7), (0, 96)))
    alog = jnp.pad(ssd_a_log.reshape(1, 32), ((0, 7), (0, 96)))

    araw = jnp.concatenate([c_all, c_ctx.reshape(1, D), jnp.zeros((7, D), F32)], axis=0)
    ncol_ada = w_ada.shape[2]
    b_shard = lax.dynamic_slice(b_ada, (0, chip * ncol_ada), (1, ncol_ada))
    mod_shard = _ada_fwd(araw, w_ada[0], b_shard)
    mod_all = _allgather8(mod_shard, "mod_gather").reshape(8, 16, ncol_ada)[0::2]
    mod_full = mod_all.transpose(1, 0, 2).reshape(16, 4 * ncol_ada)
    my_mod = lax.dynamic_slice(mod_full, (me, 0), (1, 6 * D)).reshape(6, D)
    sh1, sc1, g1, sh2, sc2, g2 = (my_mod[k:k + 1] for k in range(6))
    csh1, csc1 = mod_full[8:9, 0:D], mod_full[8:9, D:2 * D]

    shift = [functools.partial(jnp.pad, pad_width=((0, 0), (8 * k, WSL + WTAIL - NSH - 8 * k))) for k in range(4)]
    slab = lax.switch(chip, shift, w_in[0].astype(MXU_DTYPE))
    shards = [slab[:, :WSL], slab[:, WSL:], w_out[0].astype(MXU_DTYPE),
              jnp.pad(w_gate[0], ((0, 0), (0, FSL - DFF // 4))).astype(MXU_DTYPE),
              jnp.pad(w_up[0], ((0, 0), (0, FSL - DFF // 4))).astype(MXU_DTYPE),
              jnp.pad(w_down[0], ((0, FSL - DFF // 4), (0, 0))).astype(MXU_DTYPE)]
    own = lambda g_, s_: lax.dynamic_update_slice(g_, s_[None], (chip, 0, 0))
    wi_main, wi_tail = (own(g_, s_) for g_, s_ in zip(_weights_allgather(shards[:2]), shards[:2]))

    zrow = jnp.zeros((1, D), F32)
    mods_lat = jnp.concatenate([1.0 + sc1, sh1, norm_mix, zrow, zrow, zrow, zrow, zrow], axis=0)
    mods_ctx = jnp.concatenate([1.0 + csc1, csh1, norm_mix, zrow, zrow, zrow, zrow, zrow], axis=0)
    outs = _inproj(xl, mods_lat, wi_main, wi_tail, t_total, 1024, 0, None, "inproj_lat",
                   comm=_comm_gather(shards[2:]))
    wo_g, wg_g, wu_g, wd_g = (own(g_, s_) for g_, s_ in zip(outs[3:], shards[2:]))
    w_out_f = wo_g.reshape(2 * D, D)
    w_down_f = wd_g.reshape(DFFP, D)
    p_main, p_dt, u_all = _inproj(xc, mods_ctx, wi_main, wi_tail, t_total, TB, nb - 1, outs[:3], "inproj_ctx")

    o_f, hs_f = _hgrn_fwd(p_main, lbraw8, 0, nb)
    o_b, hs_b = _hgrn_fwd(p_main, lbraw8, 1, nb)
    xa, dts = _ssd_prep(p_main, p_dt, convp, dtb, nb)
    y_f, ss_f = _ssd_fwd(xa, dts, alog, 0, nb)
    y_b, ss_b = _ssd_fwd(xa, dts, alog, 1, nb)

    vec_mix = jnp.concatenate([jnp.tile(hgrn_norm, (1, NH)), jnp.repeat(ssd_d, SP, axis=1), ssd_norm, g1, 1.0 + sc2,
                               sh2, norm_ffn, zrow], axis=0)
    ymix, ylat, h1, u2 = _mix_out(o_f, o_b, p_main, y_f, y_b, xa, xl, vec_mix, w_out_f)
    gate, up, act = _ffn_up(u2, wg_g, wu_g)
    vec_loss = jnp.concatenate([g2, final_norm.reshape(1, D)] + [zrow] * 6, axis=0)
    dh2, dffn, acc_loss = _ffn_down_loss(act, w_down_f, h1, tgt, vec_loss)

    core_arr = jnp.reshape(ic, (1,)).astype(jnp.int32)
    chip_arr = jnp.reshape(chip, (1,)).astype(jnp.int32)
    every = (0, 4)

    def pair_stage(gs, tag):
        return list(_pair_sum(gs, _pair_exchange(gs, "grads_pair_exchange_" + tag), core_arr, "grads_pair_sum_" + tag))

    vec_ffn = jnp.concatenate([g1, 1.0 + sc2, norm_ffn] + [zrow] * 5, axis=0)
    dgate, dup, dh1, dylat, acc_ffn = _ffn_bwd(dffn, w_down_f, gate, up, wg_g, wu_g, h1, ylat, dh2, vec_ffn)
    gw_down = _dw(act, dffn, "dw_down").reshape(4, FSL, D)
    gw_gate = _dw(u2, dgate, "dw_gate", slabs=FSL)
    gw_up = _dw(u2, dup, "dw_up", slabs=FSL)
    do, dgr, dys, dzr, dxs_skip, acc_mix = _mix_bwd(dylat, o_f, o_b, p_main, y_f, y_b, xa, vec_mix, w_out_f)
    gw_out = _dw(ymix, dylat, "dw_out").reshape(4, D // 2, D)
    pair_a, dests_a = pair_stage([gw_out, gw_gate, gw_up, gw_down], "a"), [every] * 4

    res = _hgrn_bwd(p_main, lbraw8, hs_f, do, 0, nb, None, comm=_comm_exchange(pair_a, dests_a))
    (dq0, dff, dv0, dlb_f), recv_a = res[:4], list(res[4:])
    dq, dfb, dv, dlb_b = _hgrn_bwd(p_main, lbraw8, hs_b, do, 1, nb, (dq0, dv0))
    gw_in = [_dw_in(u_all, [dq, dff], "dw_in_0"), _dw_in(u_all, [dfb, dv], "dw_in_1"),
             _dw_in(u_all, [dgr, dzr], "dw_in_2")]
    pair_b, dests_b = pair_stage(gw_in, "b"), [(0, 1), (1, 2), (2, 3)]

    res = _ssd_bwd(xa, dts, alog, ss_f, dys, 0, nb, None, comm=_comm_exchange(pair_b, dests_b))
    (dxa0, ddts0, da_f), recv_b = res[:3], list(res[3:])
    dxa, ddts, da_b = _ssd_bwd(xa, dts, alog, ss_b, dys, 1, nb, (dxa0, ddts0))
    dxbc, ddt, acc_conv, acc_dtb = _ssd_prep_bwd(p_main, p_dt, convp, dtb, dxa, dxs_skip, ddts, nb)
    gw_in.append(_dw_in(u_all, [dxbc], "dw_in_3"))
    gw_in_dt = _dw(u_all, ddt, "dw_in_dt")
    gw_in_tail = jnp.concatenate([g_[:, :, 0:WTAIL] for g_ in gw_in[1:]] + [gw_in_dt[None]], axis=0)
    pair_c, dests_c = pair_stage([gw_in[3], gw_in_tail], "c"), [(3, 4), every]

    segs = [dq, dff, dfb, dv, dgr, dzr, dxbc]
    bmods_lat = jnp.concatenate([1.0 + sc1, norm_mix] + [zrow] * 6, axis=0)
    bmods_ctx = jnp.concatenate([1.0 + csc1, norm_mix] + [zrow] * 6, axis=0)
    res = _du_prenorm_bwd(segs, ddt, wi_main, wi_tail, xl, bmods_lat, dh1, 0, 512, "du_lat",
                          comm=_comm_exchange(pair_c, dests_c))
    (grad_x, acc_lat), recv_c = res[:2], list(res[2:])
    (acc_ctx,) = _du_prenorm_bwd(segs, ddt, wi_main, wi_tail, xc, bmods_ctx, None, n_lat, TB, "du_ctx")

    mine = _chip_sum(pair_b + pair_c + pair_a, recv_b + recv_c + recv_a, chip_arr, dests_b + dests_c + dests_a,
                     [0, 0, 0, 0, 1, 2, 3, 4, 5])
    theirs = _pair_swap(mine)
    whole = [jnp.concatenate([jnp.where(ic == 0, m_, t_), jnp.where(ic == 0, t_, m_)], axis=0)
             for m_, t_ in zip(mine, theirs)]
    g_w_in = lax.dynamic_slice(jnp.concatenate(whole[0:2], axis=1), (0, 8 * chip), (D, NSH))
    g_w_out = whole[2]
    g_w_gate = whole[3][:, :DFF // 4]
    g_w_up = whole[4][:, :DFF // 4]
    g_w_down = whole[5][:DFF // 4]

    dmod_lat = jnp.concatenate([acc_lat[0:2], acc_ffn[3:4], acc_ffn[0:2], acc_loss[0:1]], axis=0)
    misc = jnp.concatenate([(da_f + da_b)[0, :32], jnp.zeros((96,), F32), acc_dtb[0, :32], jnp.zeros((96,), F32),
                            jnp.sum(acc_loss[2]).reshape(1), jnp.zeros((D - 257,), F32)]).reshape(1, D)
    sv = jnp.concatenate([
        dmod_lat, acc_ctx[0:2], (acc_lat[2:3] + acc_ctx[2:3]), acc_ffn[2:3], acc_loss[1:2], acc_mix[2:3],
        acc_mix[0:1], acc_mix[1:2], dlb_f[0:1], dlb_b[0:1], acc_conv[0:6].reshape(12, D), misc,
        jnp.zeros((3, D), F32)], axis=0)
    sv_all = _allgather8(sv, "small_grads_gather").reshape(8, 32, D)
    ssum = _sum8(sv_all)
    dmod_rows = sv_all[:, 0:6].reshape(8, 6 * D)
    dmod_ctx_row = jnp.concatenate([ssum[6:8].reshape(1, 2 * D), jnp.zeros((1, 4 * D), F32)], axis=1)
    dmod_full = jnp.concatenate([dmod_rows, dmod_ctx_row, jnp.zeros((7, 6 * D), F32)], axis=0)
    grad_b_ada = jnp.sum(dmod_full, axis=0, keepdims=True)
    dmod_shard = lax.dynamic_slice(dmod_full, (0, chip * ncol_ada), (16, ncol_ada))
    g_w_ada, da_part = _ada_bwd(araw, dmod_shard, w_ada[0])
    da_all = _allgather8(da_part, "ada_ctx_gather").reshape(8, 16, D)[0::2, 8]
    cc = c_ctx.reshape(1, D)
    grad_c_ctx = (jnp.sum(da_all, axis=0, keepdims=True) * _dsilu(cc)).reshape(D)

    grad_norm_mix, grad_norm_ffn, grad_final_norm = ssum[8:9], ssum[9:10], ssum[10].reshape(D)
    grad_ssd_norm = ssum[11:12]
    grad_hgrn_norm = jnp.sum(ssum[12].reshape(NH, HF), axis=0, keepdims=True)
    grad_ssd_d = jnp.sum(ssum[13].reshape(SHEADS, SP), axis=1).reshape(1, SHEADS)
    lb_full = _sig(lbraw_full[0:2] - lbraw_full[2:4])
    dr0 = ssum[14:16] * lb_full * (1.0 - lb_full)
    grad_lb_full = jnp.stack([dr0, -dr0], axis=0)
    grad_lb = lax.dynamic_slice(grad_lb_full, (0, 0, chip * 256), (2, 2, 256))
    grad_conv_w = lax.dynamic_slice(ssum[16:26].reshape(KCONV, 2048), (0, chip * 512), (KCONV, 512)).reshape(1, KCONV, 512)
    grad_conv_b = ssum[26:28].reshape(1, 2048)
    a_val = -jnp.exp(ssd_a_log)
    grad_a_log = ssum[28, 0:32].reshape(1, 2, SHEADS) * a_val
    grad_dt_bias = ssum[28, 128:160].reshape(1, 2, SHEADS)
    loss = ssum[28, 256]

    small_w = [c_ctx, b_ada, norm_mix, conv_w, conv_b, ssd_a_log, ssd_dt_bias, ssd_d, ssd_norm, hgrn_lb_raw,
               hgrn_norm, norm_ffn, final_norm]
    small_m = [m_c_ctx, m_b_ada, m_norm_mix, m_conv_w, m_conv_b, m_ssd_a_log, m_ssd_dt_bias, m_ssd_d, m_ssd_norm,
               m_hgrn_lb_raw, m_hgrn_norm, m_norm_ffn, m_final_norm]
    small_v = [v_c_ctx, v_b_ada, v_norm_mix, v_conv_w, v_conv_b, v_ssd_a_log, v_ssd_dt_bias, v_ssd_d, v_ssd_norm,
               v_hgrn_lb_raw, v_hgrn_norm, v_norm_ffn, v_final_norm]
    small_g = [grad_c_ctx, grad_b_ada, grad_norm_mix, grad_conv_w, grad_conv_b, grad_a_log, grad_dt_bias, grad_ssd_d,
               grad_ssd_norm, grad_lb, grad_hgrn_norm, grad_norm_ffn, grad_final_norm]
    nrows = [-(-a.size // D) for a in small_w]
    packs = lambda lst: jnp.concatenate([_rows(a, r) for a, r in zip(lst, nrows)]
                                        + [jnp.zeros((24 - sum(nrows), D), F32)], axis=0)
    sd, sm, svv = _adamw(packs(small_w), packs(small_m), packs(small_v), packs(small_g), "adamw_small")

    def unpack(p):
        out, r0 = [], 0
        for a, r in zip(small_w, nrows):
            out.append(p[r0:r0 + r].reshape(-1)[:a.size].reshape(a.shape))
            r0 += r
        return out

    sd, sm, svv = unpack(sd), unpack(sm), unpack(svv)
    big = {}
    for nm, w_, m_, v_, g_ in (("w_ada", w_ada, m_w_ada, v_w_ada, g_w_ada), ("w_in", w_in, m_w_in, v_w_in, g_w_in),
                               ("w_out", w_out, m_w_out, v_w_out, g_w_out),
                               ("w_gate", w_gate, m_w_gate, v_w_gate, g_w_gate),
                               ("w_up", w_up, m_w_up, v_w_up, g_w_up),
                               ("w_down", w_down, m_w_down, v_w_down, g_w_down)):
        dl, nm_, nv_ = _adamw(w_[0], m_[0], v_[0], g_, "adamw_" + nm)
        big[nm] = (g_[None], dl[None], nm_[None], nv_[None])

    order = ["c_ctx", "w_ada", "b_ada", "norm_mix", "w_in", "conv_w", "conv_b", "ssd_a_log", "ssd_dt_bias", "ssd_d",
             "ssd_norm", "hgrn_lb_raw", "hgrn_norm", "w_out", "norm_ffn", "w_gate", "w_up", "w_down", "final_norm"]
    small_names = ["c_ctx", "b_ada", "norm_mix", "conv_w", "conv_b", "ssd_a_log", "ssd_dt_bias", "ssd_d", "ssd_norm",
                   "hgrn_lb_raw", "hgrn_norm", "norm_ffn", "final_norm"]
    table = dict(big)
    for k, nm in enumerate(small_names):
        table[nm] = (small_g[k].reshape(small_w[k].shape), sd[k], sm[k], svv[k])
    grads = [table[nm][0] for nm in order]
    deltas = [table[nm][1] for nm in order]
    new_m = [table[nm][2] for nm in order]
    new_v = [table[nm][3] for nm in order]
    return (loss, grad_x[None], *grads, *deltas, *new_m, *new_v)
```

```python
import functools
import math

import jax
import jax.numpy as jnp
from jax import lax
from jax.experimental import pallas as pl
from jax.experimental.pallas import tpu as pltpu

F32 = jnp.float32
BF16 = jnp.bfloat16
MXU_DTYPE = jnp.bfloat16
_INTERPRET = False

D = 1024
NH, HF = 8, 128
HC = 64
SC = 128
SN = 128
SHEADS, SP = 16, 64
GRID_W = 64
KCONV = 5
DFF = 2816
FSL = 768
DFFP = 4 * FSL
NIN = 8224
TB = 256
EPS = 1e-6
LR, B1, B2, AEPS, WD, STEP = 0.001, 0.9, 0.999, 1e-08, 0.01, 10
MESH_ID = pl.DeviceIdType.MESH
NSH = NIN // 4
WSL = 2048
WTAIL = 128


def _pcall(body, *, name, out_shape, grid=(), in_specs=None, out_specs=None, scratch=(), sem=None,
           vmem_mb=None, aliases=None):
    params = {}
    if sem is not None:
        params["dimension_semantics"] = sem
    if vmem_mb is not None:
        params["vmem_limit_bytes"] = vmem_mb << 20
    kw = dict(name=name, out_shape=out_shape, scratch_shapes=list(scratch),
              input_output_aliases=aliases or {}, compiler_params=pltpu.CompilerParams(**params),
              interpret=_INTERPRET)
    if grid:
        kw["grid"] = grid
    if in_specs is not None:
        kw["in_specs"] = in_specs
    if out_specs is not None:
        kw["out_specs"] = out_specs
    return pl.pallas_call(body, **kw)


def _mx(a):
    return a.astype(MXU_DTYPE)


def _dg(a, b, ca, cb):
    return lax.dot_general(_mx(a), _mx(b), (((ca,), (cb,)), ((), ())), preferred_element_type=F32)


def _nn(a, b):
    return _dg(a, b, 1, 0)


def _nt(a, b):
    return _dg(a, b, 1, 1)


def _tn(a, b):
    return _dg(a, b, 0, 0)


def _dot01(m, x):
    hi = x.astype(BF16)
    r1 = x - hi.astype(F32)
    mid = r1.astype(BF16)
    lo = (r1 - mid.astype(F32)).astype(BF16)
    f = lambda t: lax.dot_general(m, t, (((1,), (0,)), ((), ())), preferred_element_type=F32)
    return f(hi) + f(mid) + f(lo)


def _tri(n, upper):
    r = lax.broadcasted_iota(jnp.int32, (n, n), 0)
    c = lax.broadcasted_iota(jnp.int32, (n, n), 1)
    return (c >= r) if upper else (c <= r)


def _b01(mask):
    return jnp.where(mask, 1.0, 0.0).astype(BF16)


def _sig(x):
    return jax.nn.sigmoid(x)


def _silu(x):
    return x * _sig(x)


def _dsilu(x):
    s = _sig(x)
    return s * (1.0 + x * (1.0 - s))


def _softplus(x):
    return jnp.maximum(x, 0.0) + jnp.log(1.0 + jnp.exp(-jnp.abs(x)))


def _rowsum(x):
    return jnp.sum(x, axis=1, keepdims=True)


def _colsum(x):
    return jnp.sum(x, axis=0, keepdims=True)


def _full(shape):
    return pl.BlockSpec(shape, lambda *_: (0,) * len(shape))


def _allgather8(v, name):
    m_per, n = v.shape

    def body(x_ref, out_ref, send_sems, recv_sems, local_sem):
        x, y, c = lax.axis_index("x"), lax.axis_index("y"), lax.axis_index("c")
        me, sibling = (x, y, c), (x, y, 1 - c)
        chips = [(1 - x, y), (x, 1 - y), (1 - x, 1 - y)]

        def rows(px, py, pc):
            return out_ref.at[pl.ds((4 * px + 2 * py + pc) * m_per, m_per), :]

        def copy(k, block, to, src=None):
            return pltpu.make_async_remote_copy(
                src_ref=rows(*block) if src is None else src, dst_ref=rows(*block),
                send_sem=send_sems.at[k], recv_sem=recv_sems.at[k], device_id=to, device_id_type=MESH_ID)

        mine = pltpu.make_async_copy(x_ref, rows(*me), local_sem)
        mine.start()
        first = [copy(0, me, sibling, src=x_ref)]
        first += [copy(1 + j, me, (*chip, c), src=x_ref) for j, chip in enumerate(chips)]
        for cp in first:
            cp.start()
        passed = [copy(4 + j, (*chip, c), sibling) for j, chip in enumerate(chips)]
        for j, chip in enumerate(chips):
            copy(1 + j, (*chip, c), me).wait_recv()
            passed[j].start()
        copy(0, sibling, me).wait_recv()
        for j, chip in enumerate(chips):
            copy(4 + j, (*chip, 1 - c), me).wait_recv()
        for cp in first + passed:
            cp.wait_send()
        mine.wait()

    return _pcall(
        body, name=name, out_shape=jax.ShapeDtypeStruct((8 * m_per, n), v.dtype),
        in_specs=[pl.BlockSpec(memory_space=pltpu.VMEM)], out_specs=pl.BlockSpec(memory_space=pltpu.VMEM),
        scratch=[pltpu.SemaphoreType.DMA((7,)), pltpu.SemaphoreType.DMA((7,)), pltpu.SemaphoreType.DMA],
    )(v)


def _gather_ops(ins, outs, send_sems, recv_sems):
    n = len(ins)
    x, y, c = lax.axis_index("x"), lax.axis_index("y"), lax.axis_index("c")
    sibling = (x, y, 1 - c)
    chips = [(1 - x, y), (x, 1 - y), (1 - x, 1 - y)]

    def part(a, px, py, pc):
        half = ins[a].shape[0] // 2
        return outs[a].at[2 * px + py, pl.ds(pc * half, half), :]

    def copy(a, k, block, to, src=None):
        return pltpu.make_async_remote_copy(
            src_ref=part(a, *block) if src is None else src, dst_ref=part(a, *block),
            send_sem=send_sems.at[6 * a + k], recv_sem=recv_sems.at[6 * a + k], device_id=to,
            device_id_type=MESH_ID)

    def first(a, j):
        half = ins[a].shape[0] // 2
        return copy(a, j, (x, y, c), (*chips[j], c), src=ins[a].at[pl.ds(c * half, half), :])

    def start():
        for a in range(n):
            for j in range(3):
                first(a, j).start()

    def finish():
        for a in range(n):
            for j, chip in enumerate(chips):
                copy(a, j, (*chip, c), (x, y, c)).wait_recv()
                copy(a, 3 + j, (*chip, c), sibling).start()
        for a in range(n):
            for j, chip in enumerate(chips):
                copy(a, 3 + j, (*chip, 1 - c), (x, y, c)).wait_recv()
        for a in range(n):
            for j, chip in enumerate(chips):
                first(a, j).wait_send()
                copy(a, 3 + j, (*chip, c), sibling).wait_send()

    return start, finish


def _gather_out(shards):
    return tuple(jax.ShapeDtypeStruct((4,) + s_.shape, s_.dtype) for s_ in shards)


def _gather_sems(n):
    return [pltpu.SemaphoreType.DMA((6 * n,)), pltpu.SemaphoreType.DMA((6 * n,))]


def _weights_allgather(shards):
    n = len(shards)

    def body(*refs):
        start, finish = _gather_ops(refs[:n], refs[n:2 * n], *refs[2 * n:])
        start()
        finish()

    return _pcall(
        body, name="weights_allgather", out_shape=_gather_out(shards),
        in_specs=[pl.BlockSpec(memory_space=pl.ANY)] * n, out_specs=(pl.BlockSpec(memory_space=pl.ANY),) * n,
        scratch=_gather_sems(n),
    )(*shards)


def _pair_exchange(gs, name):
    n = len(gs)

    def body(*refs):
        ins, outs = refs[:n], refs[n:2 * n]
        send_sems, recv_sems = refs[2 * n:]
        x, y, c = lax.axis_index("x"), lax.axis_index("y"), lax.axis_index("c")
        cps = []
        for a in range(n):
            half = ins[a].shape[1] // 2
            cps.append(pltpu.make_async_remote_copy(
                src_ref=ins[a].at[:, pl.ds((1 - c) * half, half), :], dst_ref=outs[a], send_sem=send_sems.at[a],
                recv_sem=recv_sems.at[a], device_id=(x, y, 1 - c), device_id_type=MESH_ID))
        for cp in cps:
            cp.start()
        for cp in cps:
            cp.wait()

    return _pcall(
        body, name=name,
        out_shape=tuple(jax.ShapeDtypeStruct((g.shape[0], g.shape[1] // 2, g.shape[2]), g.dtype) for g in gs),
        in_specs=[pl.BlockSpec(memory_space=pl.ANY)] * n, out_specs=(pl.BlockSpec(memory_space=pl.ANY),) * n,
        scratch=[pltpu.SemaphoreType.DMA((n,)), pltpu.SemaphoreType.DMA((n,))],
    )(*gs)


def _exchange_ops(ins, outs, send_sems, recv_sems, dests):
    x, y, c = lax.axis_index("x"), lax.axis_index("y"), lax.axis_index("c")
    mine = 2 * x + y
    chips = [(1 - x, y), (x, 1 - y), (1 - x, 1 - y)]

    def each(fn):
        for a in range(len(ins)):
            lo, hi = dests[a]
            for j, (px, py) in enumerate(chips):
                q = 2 * px + py
                cp = pltpu.make_async_remote_copy(
                    src_ref=ins[a].at[jnp.clip(q - lo, 0, hi - lo - 1)], dst_ref=outs[a].at[j],
                    send_sem=send_sems.at[3 * a + j], recv_sem=recv_sems.at[3 * a + j], device_id=(px, py, c),
                    device_id_type=MESH_ID)
                fn(cp, (q >= lo) & (q < hi), (mine >= lo) & (mine < hi), (lo, hi) == (0, 4))

    def start():
        def go(cp, send_ok, recv_ok, always):
            if always:
                cp.start()
            else:
                pl.when(send_ok)(cp.start)
        each(go)

    def finish():
        def go(cp, send_ok, recv_ok, always):
            if always:
                cp.wait()
            else:
                pl.when(send_ok)(cp.wait_send)
                pl.when(recv_ok)(cp.wait_recv)
        each(go)

    return start, finish


def _comm_exchange(hs, dests):
    n = len(hs)
    return (list(hs), tuple(jax.ShapeDtypeStruct((3,) + h.shape[1:], h.dtype) for h in hs),
            [pltpu.SemaphoreType.DMA((3 * n,)), pltpu.SemaphoreType.DMA((3 * n,))],
            lambda i, o, s, r: _exchange_ops(i, o, s, r, dests))


def _comm_gather(shards):
    return (list(shards), _gather_out(shards), _gather_sems(len(shards)), _gather_ops)


def _carry(call, comm, steps):
    if comm is None:
        return call
    arrays, out_shape, sems, make = comm
    n, n_in, n_out = len(arrays), len(call["args"]), len(call["out_shape"])
    body = call["body"]

    def wrapped(*refs):
        base_in, cin = refs[:n_in], refs[n_in:n_in + n]
        rest = refs[n_in + n:]
        base_out, cout, scr = rest[:n_out], rest[n_out:n_out + n], rest[n_out + n:]
        start, finish = make(cin, cout, scr[-2], scr[-1])
        first, last = steps()
        pl.when(first)(start)
        body(*base_in, *base_out, *scr[:-2])
        pl.when(last)(finish)

    anyspec = pl.BlockSpec(memory_space=pl.ANY)
    return dict(call, body=wrapped, args=list(call["args"]) + arrays,
                in_specs=list(call["in_specs"]) + [anyspec] * n,
                out_shape=tuple(call["out_shape"]) + tuple(out_shape),
                out_specs=tuple(call["out_specs"]) + (anyspec,) * n,
                scratch=list(call["scratch"]) + sems)


def _run(call):
    args = call.pop("args")
    body = call.pop("body")
    return _pcall(body, **call)(*args)


def _pair_swap(rs):
    n = len(rs)

    def body(*refs):
        ins, outs = refs[:n], refs[n:2 * n]
        send_sems, recv_sems = refs[2 * n:]
        x, y, c = lax.axis_index("x"), lax.axis_index("y"), lax.axis_index("c")
        cps = [pltpu.make_async_remote_copy(
            src_ref=ins[a], dst_ref=outs[a], send_sem=send_sems.at[a], recv_sem=recv_sems.at[a],
            device_id=(x, y, 1 - c), device_id_type=MESH_ID) for a in range(n)]
        for cp in cps:
            cp.start()
        for cp in cps:
            cp.wait()

    return _pcall(
        body, name="grads_pair_swap", out_shape=tuple(jax.ShapeDtypeStruct(r.shape, r.dtype) for r in rs),
        in_specs=[pl.BlockSpec(memory_space=pl.ANY)] * n, out_specs=(pl.BlockSpec(memory_space=pl.ANY),) * n,
        scratch=[pltpu.SemaphoreType.DMA((n,)), pltpu.SemaphoreType.DMA((n,))],
    )(*rs)


SUM_STEPS = 8


def _pair_sum(gs, recvs, core, name):
    n = len(gs)

    def body(c_ref, *refs):
        for a in range(n):
            refs[2 * n + a][...] = (refs[a][...].astype(F32) + refs[n + a][...].astype(F32)).astype(refs[2 * n + a].dtype)

    blk = lambda g: (g.shape[0], g.shape[1] // (2 * SUM_STEPS), g.shape[2])
    return pl.pallas_call(
        body, name=name,
        out_shape=tuple(jax.ShapeDtypeStruct((g.shape[0], g.shape[1] // 2, g.shape[2]), g.dtype) for g in gs),
        grid_spec=pltpu.PrefetchScalarGridSpec(
            num_scalar_prefetch=1, grid=(SUM_STEPS,),
            in_specs=[pl.BlockSpec(blk(g), lambda i, cr: (0, cr[0] * SUM_STEPS + i, 0)) for g in gs]
            + [pl.BlockSpec(blk(g), lambda i, cr: (0, i, 0)) for g in gs],
            out_specs=tuple(pl.BlockSpec(blk(g), lambda i, cr: (0, i, 0)) for g in gs)),
        compiler_params=pltpu.CompilerParams(vmem_limit_bytes=40 << 20), interpret=_INTERPRET,
    )(core, *gs, *recvs)


def _chip_sum(hs, recvs, chip, dests, slots):
    n = len(hs)
    nout = max(slots) + 1
    first = [slots.index(o) for o in range(nout)]
    every = lambda d_: d_ == (0, 4)

    def own(d_):
        if every(d_):
            return lambda i, kr: (kr[0], i, 0)
        return lambda i, kr: (0, jnp.where(kr[0] == d_[0], i, 0), 0)

    def got(d_):
        if every(d_):
            return lambda i, kr: (0, i, 0)
        return lambda i, kr: (0, jnp.where(kr[0] == d_[0], i, 0), 0)

    def body(k_ref, *refs):
        for a in range(n):
            def emit(a=a):
                acc = refs[a][0].astype(F32)
                for j in range(3):
                    acc = acc + refs[n + a][j].astype(F32)
                refs[2 * n + slots[a]][...] = acc
            if every(dests[a]):
                emit()
            else:
                pl.when(k_ref[0] == dests[a][0])(emit)

    rb = lambda h: h.shape[1] // SUM_STEPS
    return pl.pallas_call(
        body, name="grads_chip_sum",
        out_shape=tuple(jax.ShapeDtypeStruct(hs[a].shape[1:], F32) for a in first),
        grid_spec=pltpu.PrefetchScalarGridSpec(
            num_scalar_prefetch=1, grid=(SUM_STEPS,),
            in_specs=[pl.BlockSpec((1, rb(h), h.shape[2]), own(d_)) for h, d_ in zip(hs, dests)]
            + [pl.BlockSpec((3, rb(h), h.shape[2]), got(d_)) for h, d_ in zip(hs, dests)],
            out_specs=tuple(pl.BlockSpec((rb(hs[a]), hs[a].shape[2]), lambda i, kr: (i, 0)) for a in first)),
        compiler_params=pltpu.CompilerParams(vmem_limit_bytes=40 << 20), interpret=_INTERPRET,
    )(chip, *hs, *recvs)


def _ada_fwd(araw, w, b):
    nblk = w.shape[1] // 512

    def body(a_ref, w_ref, b_ref, o_ref):
        o_ref[...] = _nn(_silu(a_ref[...]), w_ref[...]) + b_ref[...]

    return _pcall(
        body, name="ada_fwd", out_shape=jax.ShapeDtypeStruct((16, w.shape[1]), F32), grid=(nblk,),
        in_specs=[_full((16, D)), pl.BlockSpec((D, 512), lambda j: (0, j)), pl.BlockSpec((1, 512), lambda j: (0, j))],
        out_specs=pl.BlockSpec((16, 512), lambda j: (0, j)), sem=("parallel",),
    )(araw, w, b)


def _ada_bwd(araw, dmod, w):
    nblk = w.shape[1] // 512

    def body(a_ref, d_ref, w_ref, gw_ref, da_ref):
        j = pl.program_id(0)
        gw_ref[...] = _tn(_silu(a_ref[...]), d_ref[...])
        part = _nt(d_ref[...], w_ref[...])

        @pl.when(j == 0)
        def _():
            da_ref[...] = part

        @pl.when(j > 0)
        def _():
            da_ref[...] += part

    return _pcall(
        body, name="ada_bwd",
        out_shape=(jax.ShapeDtypeStruct(w.shape, F32), jax.ShapeDtypeStruct((16, D), F32)), grid=(nblk,),
        in_specs=[_full((16, D)), pl.BlockSpec((16, 512), lambda j: (0, j)), pl.BlockSpec((D, 512), lambda j: (0, j))],
        out_specs=(pl.BlockSpec((D, 512), lambda j: (0, j)), _full((16, D))), sem=("arbitrary",),
    )(araw, dmod, w)


def _w_specs():
    return [pl.BlockSpec((None, D, D), lambda j, i: (j // 2, 0, j % 2)),
            pl.BlockSpec((None, D, WTAIL), lambda j, i: (jnp.maximum(j // 2 - 1, 0), 0, 0)),
            pl.BlockSpec((None, D, WTAIL), lambda j, i: (3, 0, 0))]


def _inproj(xin, mods, wi_main, wi_tail, t_total, tb, blk_off, prev, name, comm=None):
    n = xin.shape[0]
    nt = n // tb
    ncol = 8

    def body(x_ref, mod_ref, w_ref, wb_ref, wdt_ref, *rest):
        p_ref, pdt_ref, u_ref, uscr = rest[-4:]
        j, i = pl.program_id(0), pl.program_id(1)
        rows = pl.ds(pl.multiple_of(i * tb, tb), tb)

        @pl.when(j == 0)
        def _():
            xv = x_ref[...]
            r = lax.rsqrt(jnp.mean(xv * xv, axis=1, keepdims=True) + EPS)
            u = (xv * r * mod_ref[2:3, :]) * mod_ref[0:1, :] + mod_ref[1:2, :]
            ub = u.astype(MXU_DTYPE)
            uscr[rows, :] = ub
            u_ref[...] = ub
            pdt_ref[...] = _nn(ub, wdt_ref[...])

        ub = uscr[rows, :]
        pv = _nn(ub, w_ref[...])

        @pl.when((j % 2 == 1) | (j == 0))
        def _():
            p_ref[...] = pv.astype(p_ref.dtype)

        @pl.when((j % 2 == 0) & (j > 0))
        def _():
            head = pv[:, 0:WTAIL] + _nn(ub, wb_ref[...])
            p_ref[...] = jnp.concatenate([head, pv[:, WTAIL:]], axis=1).astype(p_ref.dtype)

    once = lambda j, i: (jnp.where(j == 0, i, nt - 1) + blk_off, 0)
    in_specs = [pl.BlockSpec((tb, D), lambda j, i: (jnp.where(j == 0, i, nt - 1), 0)), _full((8, D))] + _w_specs()
    args = [xin, mods, wi_main, wi_tail, wi_tail]
    aliases = None
    if prev is not None:
        in_specs += [pl.BlockSpec(memory_space=pl.ANY)] * 3
        args += list(prev)
        aliases = {5: 0, 6: 1, 7: 2}
    call = dict(
        body=body, args=args, name=name,
        out_shape=(jax.ShapeDtypeStruct((t_total, ncol * D), MXU_DTYPE), jax.ShapeDtypeStruct((t_total, 128), F32),
                   jax.ShapeDtypeStruct((t_total, D), MXU_DTYPE)),
        grid=(ncol, nt), in_specs=in_specs,
        out_specs=(pl.BlockSpec((tb, D), lambda j, i: (i + blk_off, j)), pl.BlockSpec((tb, 128), once),
                   pl.BlockSpec((tb, D), once)),
        scratch=[pltpu.VMEM((n, D), MXU_DTYPE)], sem=("arbitrary", "arbitrary"), vmem_mb=48, aliases=aliases)
    steps = lambda: ((pl.program_id(0) == 0) & (pl.program_id(1) == 0),
                     (pl.program_id(0) == ncol - 1) & (pl.program_id(1) == nt - 1))
    return _run(_carry(call, comm, steps))


def _blk(s, nb, rev):
    return jnp.where(s == 0, nb - 1, (nb - 1 - s) if rev else (s - 1))


def _hgrn_gate(fr, lbraw_ref, d):
    lb = _sig(lbraw_ref[d:d + 1, :] - lbraw_ref[2 + d:3 + d, :])
    sg = _sig(fr)
    return lb, sg, lb + (1.0 - lb) * sg


def _hgrn_fwd(p_main, lbraw, d, nb):
    t_total = p_main.shape[0]
    rev = d == 1
    nch = TB // HC
    scale = HF ** -0.5

    def body(q_ref, f_ref, v_ref, lb_ref, o_ref, sp_ref, st):
        s = pl.program_id(0)

        @pl.when(s == 0)
        def _():
            st[...] = jnp.zeros_like(st)

        mb = _tri(HC, rev)
        m01 = _b01(mb)
        order = list(reversed(range(nch)) if rev else range(nch))
        hs_ = [slice(h * HF, (h + 1) * HF) for h in range(NH)]
        pre = {}
        for c in order:
            rows = slice(c * HC, (c + 1) * HC)
            _, _, f = _hgrn_gate(f_ref[rows, :].astype(F32), lb_ref, d)
            k = 1.0 - f
            cum = _dot01(m01, jnp.log(f))
            tot = cum[0:1, :] if rev else cum[HC - 1:HC, :]
            qd = _silu(q_ref[rows, :].astype(F32)) * scale * jnp.exp(cum)
            ki = k * jnp.exp(-cum)
            etot = jnp.exp(tot)
            pre[c] = (_mx(qd), _mx(ki), _mx(ki * etot), _mx(v_ref[rows, :]), etot)
        scs = {c: [_nt(pre[c][0][:, cs], pre[c][1][:, cs]) for cs in hs_] for c in order}
        upd = {c: [_tn(pre[c][3][:, cs], pre[c][2][:, cs]) for cs in hs_] for c in order}
        intra = {c: [_nn(jnp.where(mb, scs[c][h], 0.0), pre[c][3][:, cs]) for h, cs in enumerate(hs_)] for c in order}
        for c in order:
            rows = slice(c * HC, (c + 1) * HC)
            qdb, etot = pre[c][0], pre[c][4]
            for h, cs in enumerate(hs_):
                sth = st[h]
                stb = sth.astype(sp_ref.dtype)
                sp_ref[c, h] = stb
                o_ref[rows, cs] = intra[c][h] + _nt(qdb[:, cs], stb)
                st[h] = sth * etot[:, cs] + upd[c][h]

    col = lambda j: (lambda s: (_blk(s, nb, rev), j))
    return _pcall(
        body, name=f"hgrn_fwd_{d}",
        out_shape=(jax.ShapeDtypeStruct((t_total, D), F32),
                   jax.ShapeDtypeStruct((nch * nb, NH, HF, HF), MXU_DTYPE)),
        grid=(nb,),
        in_specs=[pl.BlockSpec((TB, D), col(0)), pl.BlockSpec((TB, D), col(1 + d)), pl.BlockSpec((TB, D), col(3)),
                  _full((8, D))],
        out_specs=(pl.BlockSpec((TB, D), col(0)),
                   pl.BlockSpec((nch, NH, HF, HF), lambda s: (_blk(s, nb, rev), 0, 0, 0))),
        scratch=[pltpu.VMEM((NH, HF, HF), F32)], sem=("arbitrary",), vmem_mb=40,
    )(p_main, p_main, p_main, lbraw)


def _hgrn_bwd(p_main, lbraw, sprev, do, d, nb, prev, comm=None):
    t_total = p_main.shape[0]
    rev = d == 1
    nch = TB // HC
    scale = HF ** -0.5
    last = prev is not None
    odt = MXU_DTYPE if last else F32

    def body(q_ref, f_ref, v_ref, lb_ref, sp_ref, do_ref, *rest):
        if last:
            dqp_ref, dvp_ref = rest[:2]
            rest = rest[2:]
        dq_ref, df_ref, dv_ref, dlb_ref, dst = rest
        sp_id = pl.program_id(0)
        is_ctx = sp_id == nb - 1

        @pl.when(sp_id == 0)
        def _():
            dst[...] = jnp.zeros_like(dst)
            dlb_ref[...] = jnp.zeros_like(dlb_ref)

        mb = _tri(HC, rev)
        mbt = _tri(HC, not rev)
        m01 = _b01(mb)
        mt01 = _b01(mbt)
        order = list(range(nch) if rev else reversed(range(nch)))
        hs_ = [slice(h * HF, (h + 1) * HF) for h in range(NH)]
        pre = {}
        for c in order:
            rows = slice(c * HC, (c + 1) * HC)
            lb, sg, f = _hgrn_gate(f_ref[rows, :].astype(F32), lb_ref, d)
            k = 1.0 - f
            cum = _dot01(m01, jnp.log(f))
            tot = cum[0:1, :] if rev else cum[HC - 1:HC, :]
            e = jnp.exp(cum)
            ei = jnp.exp(-cum)
            etot = jnp.exp(tot)
            ee = ei * etot
            qraw = q_ref[rows, :].astype(F32)
            qd = _silu(qraw) * scale * e
            ki = k * ei
            ke = k * ee
            dov = jnp.where(is_ctx, 0.0, do_ref[rows, :])
            pre[c] = dict(lb=lb, sg=sg, f=f, e=e, ei=ei, ee=ee, etot=etot, qraw=qraw, qd=qd, ki=ki, ke=ke,
                          qdb=_mx(qd), kib=_mx(ki), keb=_mx(ke), vb=_mx(v_ref[rows, :]), dob=_mx(dov))
        units = [(c, h) for c in order for h in range(NH)]
        col = lambda u, key: pre[u[0]][key][:, hs_[u[1]]]
        pt = {u: jnp.where(mbt, _nt(col(u, "kib"), col(u, "qdb")), 0.0) for u in units}
        dp = {u: jnp.where(mb, _nt(col(u, "dob"), col(u, "vb")), 0.0) for u in units}
        dpt = {u: jnp.where(mbt, _nt(col(u, "vb"), col(u, "dob")), 0.0) for u in units}
        dv_i = {u: _nn(pt[u], col(u, "dob")) for u in units}
        dqd_ = {u: _nn(dp[u], col(u, "kib")) + _nn(col(u, "dob"), sp_ref[u[0], u[1]]) for u in units}
        dki_ = {u: _nn(dpt[u], col(u, "qdb")) for u in units}
        dsl = {u: _tn(col(u, "dob"), col(u, "qdb")) for u in units}
        for c in order:
            rows = slice(c * HC, (c + 1) * HC)
            p = pre[c]
            dv_l, dke_l, dtot_l = [], [], []
            for h, cs in enumerate(hs_):
                dso = dst[h]
                dsob = _mx(dso)
                dv_l.append(dv_i[(c, h)] + _nt(p["keb"][:, cs], dsob))
                dke_l.append(_nn(p["vb"][:, cs], dsob))
                dtot_l.append(_colsum(dso * sp_ref[c, h].astype(F32)) * p["etot"][:, cs])
                dst[h] = dso * p["etot"][:, cs] + dsl[(c, h)]
            lb, sg, f, e, ei, ee, qraw, qd, ki, ke = (p[n_] for n_ in ("lb", "sg", "f", "e", "ei", "ee", "qraw", "qd",
                                                                     "ki", "ke"))
            dqd = jnp.concatenate([dqd_[(c, h)] for h in range(NH)], axis=1)
            dki = jnp.concatenate([dki_[(c, h)] for h in range(NH)], axis=1)
            dke = jnp.concatenate(dke_l, axis=1)
            dcum = dqd * qd - dki * ki - dke * ke
            dtot = jnp.concatenate(dtot_l, axis=1) + _colsum(dke * ke)
            dk = dki * ei + dke * ee
            dlf = _dot01(mt01, dcum) + dtot
            df = dlf / f - dk
            dlb_ref[0:1, :] += _colsum(df * (1.0 - sg))
            dfr = df * (1.0 - lb) * sg * (1.0 - sg)
            dq = dqd * e * scale * _dsilu(qraw)
            dv = jnp.concatenate(dv_l, axis=1)
            if last:
                dq = dq + dqp_ref[rows, :]
                dv = dv + dvp_ref[rows, :]
            dq_ref[rows, :] = dq.astype(odt)
            dv_ref[rows, :] = dv.astype(odt)
            df_ref[rows, :] = dfr.astype(MXU_DTYPE)

    blk = lambda s: _blk(nb - 1 - s, nb, rev)
    col = lambda j: (lambda s: (blk(s), j))
    in_specs = [pl.BlockSpec((TB, D), col(0)), pl.BlockSpec((TB, D), col(1 + d)), pl.BlockSpec((TB, D), col(3)),
                _full((8, D)), pl.BlockSpec((nch, NH, HF, HF), lambda s: (blk(s), 0, 0, 0)),
                pl.BlockSpec((TB, D), lambda s: (jnp.minimum(blk(s), nb - 2), 0))]
    args = [p_main, p_main, p_main, lbraw, sprev, do]
    if last:
        in_specs += [pl.BlockSpec((TB, D), col(0))] * 2
        args += list(prev)
    call = dict(
        body=body, args=args, name=f"hgrn_bwd_{d}",
        out_shape=(jax.ShapeDtypeStruct((t_total, D), odt), jax.ShapeDtypeStruct((t_total, D), MXU_DTYPE),
                   jax.ShapeDtypeStruct((t_total, D), odt), jax.ShapeDtypeStruct((8, D), F32)),
        grid=(nb,), in_specs=in_specs,
        out_specs=(pl.BlockSpec((TB, D), col(0)), pl.BlockSpec((TB, D), col(0)), pl.BlockSpec((TB, D), col(0)),
                   _full((8, D))),
        scratch=[pltpu.VMEM((NH, HF, HF), F32)], sem=("arbitrary",), vmem_mb=48)
    return _run(_carry(call, comm, lambda: (pl.program_id(0) == 0, pl.program_id(0) == nb - 1)))


def _conv_masks(tb, is_ctx):
    seg = jnp.where(is_ctx, tb, GRID_W)
    pos = lax.broadcasted_iota(jnp.int32, (tb, 1), 0) & (seg - 1)
    return pos, seg


def _shift_rows(x, dshift, pos, seg):
    if dshift == 0:
        return x
    n = x.shape[0]
    rolled = pltpu.roll(x, (-dshift) % n, 0)
    ok = (pos + dshift >= 0) & (pos + dshift < seg)
    return jnp.where(ok, rolled, 0.0)


def _ssd_prep(p_main, p_dt, convp, dtb, nb):
    t_total = p_main.shape[0]

    def body(x_ref, dt_ref, cw_ref, dtb_ref, xa_ref, dts_ref):
        is_ctx = pl.program_id(0) == nb - 1
        pos, seg = _conv_masks(TB, is_ctx)
        xv = x_ref[...].astype(F32)
        acc = cw_ref[5:6, :] + cw_ref[2:3, :] * xv
        for kk in (0, 1, 3, 4):
            acc = acc + cw_ref[kk:kk + 1, :] * _shift_rows(xv, kk - 2, pos, seg)
        xa_ref[...] = _silu(acc)
        dts_ref[...] = _softplus(dt_ref[...] + dtb_ref[0:1, :])

    return _pcall(
        body, name="ssd_prep",
        out_shape=(jax.ShapeDtypeStruct((t_total, 2048), F32), jax.ShapeDtypeStruct((t_total, 128), F32)),
        grid=(nb,),
        in_specs=[pl.BlockSpec((TB, 2048), lambda i: (i, 3)), pl.BlockSpec((TB, 128), lambda i: (i, 0)),
                  _full((8, 2048)), _full((8, 128))],
        out_specs=(pl.BlockSpec((TB, 2048), lambda i: (i, 0)), pl.BlockSpec((TB, 128), lambda i: (i, 0))),
        sem=("parallel",), vmem_mb=32,
    )(p_main, p_dt, convp, dtb)


def _ssd_prep_bwd(p_main, p_dt, convp, dtb, dxa, dxs_skip, ddts, nb):
    t_total = p_main.shape[0]

    def body(x_ref, dt_ref, cw_ref, dtb_ref, dxa_ref, dsk_ref, ddts_ref, dx_ref, ddt_ref, dcw_ref, ddtb_ref):
        i = pl.program_id(0)
        is_ctx = i == nb - 1

        @pl.when(i == 0)
        def _():
            dcw_ref[...] = jnp.zeros_like(dcw_ref)
            ddtb_ref[...] = jnp.zeros_like(ddtb_ref)

        pos, seg = _conv_masks(TB, is_ctx)
        xv = x_ref[...].astype(F32)
        sh = {kk: _shift_rows(xv, kk - 2, pos, seg) for kk in range(KCONV)}
        acc = cw_ref[5:6, :]
        for kk in range(KCONV):
            acc = acc + cw_ref[kk:kk + 1, :] * sh[kk]
        dact = dxa_ref[...]
        dact = jnp.concatenate([dact[:, :D] + jnp.where(is_ctx, 0.0, dsk_ref[...]), dact[:, D:]], axis=1)
        dpre = dact * _dsilu(acc)
        dxv = cw_ref[2:3, :] * dpre
        for kk in (0, 1, 3, 4):
            dxv = dxv + cw_ref[kk:kk + 1, :] * _shift_rows(dpre, 2 - kk, pos, seg)
        dx_ref[...] = dxv.astype(dx_ref.dtype)
        for kk in range(KCONV):
            dcw_ref[kk:kk + 1, :] += _colsum(dpre * sh[kk])
        dcw_ref[5:6, :] += _colsum(dpre)
        draw = ddts_ref[...] * _sig(dt_ref[...] + dtb_ref[0:1, :])
        ddt_ref[...] = draw.astype(ddt_ref.dtype)
        ddtb_ref[0:1, :] += _colsum(draw)

    return _pcall(
        body, name="ssd_prep_bwd",
        out_shape=(jax.ShapeDtypeStruct((t_total, 2048), MXU_DTYPE), jax.ShapeDtypeStruct((t_total, 128), MXU_DTYPE),
                   jax.ShapeDtypeStruct((8, 2048), F32), jax.ShapeDtypeStruct((8, 128), F32)),
        grid=(nb,),
        in_specs=[pl.BlockSpec((TB, 2048), lambda i: (i, 3)), pl.BlockSpec((TB, 128), lambda i: (i, 0)),
                  _full((8, 2048)), _full((8, 128)), pl.BlockSpec((TB, 2048), lambda i: (i, 0)),
                  pl.BlockSpec((TB, D), lambda i: (jnp.minimum(i, nb - 2), 0)),
                  pl.BlockSpec((TB, 128), lambda i: (i, 0))],
        out_specs=(pl.BlockSpec((TB, 2048), lambda i: (i, 0)), pl.BlockSpec((TB, 128), lambda i: (i, 0)),
                   _full((8, 2048)), _full((8, 128))),
        sem=("arbitrary",), vmem_mb=40,
    )(p_main, p_dt, convp, dtb, dxa, dxs_skip, ddts)


def _lane_pick(x, lane, col):
    return _rowsum(jnp.where(lane == col, x, 0.0))


def _ssd_chunk_common(dts, alog_ref, m01, rev):
    lane = lax.broadcasted_iota(jnp.int32, (1, 128), 1)
    arow = -jnp.exp(alog_ref[0:1, :])
    cum = _dot01(m01, dts * arow)
    tot = cum[0:1, :] if rev else cum[SC - 1:SC, :]
    return lane, arow, cum, cum.T, tot


def _ssd_fwd(xa, dts, alog, d, nb):
    t_total = xa.shape[0]
    rev = d == 1
    nch = TB // SC
    npair = SHEADS // 2

    def body(xa_ref, dts_ref, alog_ref, y_ref, sp_ref, st):
        s = pl.program_id(0)

        @pl.when(s == 0)
        def _():
            st[...] = jnp.zeros_like(st)

        mb = _tri(SC, rev)
        m01 = _b01(mb)
        lo = lax.broadcasted_iota(jnp.int32, (1, 128), 1) < SP
        rlo = lax.broadcasted_iota(jnp.int32, (128, 1), 0) < SP
        order = list(reversed(range(nch)) if rev else range(nch))
        pre = {}
        for c in order:
            rows = slice(c * SC, (c + 1) * SC)
            dts_c = dts_ref[rows, :]
            lane, arow, cum, cumt, tot = _ssd_chunk_common(dts_c, alog_ref, m01, rev)
            bgs = [_mx(xa_ref[rows, D + g * SN:D + (g + 1) * SN]) for g in range(4)]
            cgs = [_mx(xa_ref[rows, D + 512 + g * SN:D + 512 + (g + 1) * SN]) for g in range(4)]
            pairs = []
            for pr in range(npair):
                xs = xa_ref[rows, pr * 128:(pr + 1) * 128]
                cols = [16 * d + 2 * pr, 16 * d + 2 * pr + 1]
                cum_c = [_lane_pick(cum, lane, q) for q in cols]
                dt_c = [_lane_pick(dts_c, lane, q) for q in cols]
                tot_c = [_lane_pick(tot, lane, q) for q in cols]
                dtx = xs * jnp.where(lo, dt_c[0], dt_c[1])
                e1_pair = jnp.where(lo, jnp.exp(cum_c[0]), jnp.exp(cum_c[1]))
                e2_pair = jnp.where(lo, jnp.exp(tot_c[0] - cum_c[0]), jnp.exp(tot_c[1] - cum_c[1]))
                etot_col = jnp.where(rlo, jnp.exp(tot_c[0]), jnp.exp(tot_c[1]))
                decs = [jnp.where(mb, jnp.exp(cum_c[q] - cumt[cols[q]:cols[q] + 1, :]), 0.0) for q in range(2)]
                dtxq = [_mx(jnp.where(lo if q == 0 else ~lo, dtx, 0.0)) for q in range(2)]
                pairs.append(dict(e1=e1_pair, etot=etot_col, decs=decs, dtxq=dtxq, xe=_mx(dtx * e2_pair)))
            pre[c] = (bgs, cgs, pairs)
        gm = {(c, g): _nt(pre[c][1][g], pre[c][0][g]) for c in order for g in range(4)}
        upd = {(c, pr): _tn(pre[c][2][pr]["xe"], pre[c][0][pr // 2]) for c in order for pr in range(npair)}
        intra = {(c, pr): sum(_nn(gm[(c, pr // 2)] * pre[c][2][pr]["decs"][q], pre[c][2][pr]["dtxq"][q]) for q in range(2))
                 for c in order for pr in range(npair)}
        for c in order:
            rows = slice(c * SC, (c + 1) * SC)
            bgs, cgs, pairs = pre[c]
            for pr in range(npair):
                stp = st[pr]
                stb = stp.astype(sp_ref.dtype)
                sp_ref[c, pr] = stb
                y_ref[rows, pr * 128:(pr + 1) * 128] = intra[(c, pr)] + pairs[pr]["e1"] * _nt(cgs[pr // 2], stb)
                st[pr] = stp * pairs[pr]["etot"] + upd[(c, pr)]

    blk = lambda s: _blk(s, nb, rev)
    return _pcall(
        body, name=f"ssd_fwd_{d}",
        out_shape=(jax.ShapeDtypeStruct((t_total, D), F32),
                   jax.ShapeDtypeStruct((nch * nb, npair, 128, SN), MXU_DTYPE)),
        grid=(nb,),
        in_specs=[pl.BlockSpec((TB, 2048), lambda s: (blk(s), 0)), pl.BlockSpec((TB, 128), lambda s: (blk(s), 0)),
                  _full((8, 128))],
        out_specs=(pl.BlockSpec((TB, D), lambda s: (blk(s), 0)),
                   pl.BlockSpec((nch, npair, 128, SN), lambda s: (blk(s), 0, 0, 0))),
        scratch=[pltpu.VMEM((npair, 128, SN), F32)], sem=("arbitrary",), vmem_mb=40,
    )(xa, dts, alog)


def _ssd_bwd(xa, dts, alog, sprev, dy, d, nb, prev, comm=None):
    t_total = xa.shape[0]
    rev = d == 1
    nch = TB // SC
    npair = SHEADS // 2
    last = prev is not None

    def body(xa_ref, dts_ref, alog_ref, sp_ref, dy_ref, *rest):
        if last:
            dxp_ref, ddp_ref = rest[:2]
            rest = rest[2:]
        dxa_ref, ddts_ref, da_ref, dst = rest
        sp_id = pl.program_id(0)
        is_ctx = sp_id == nb - 1

        @pl.when(sp_id == 0)
        def _():
            dst[...] = jnp.zeros_like(dst)
            da_ref[...] = jnp.zeros_like(da_ref)

        mb = _tri(SC, rev)
        m01 = _b01(mb)
        mt01 = _b01(_tri(SC, not rev))
        lo = lax.broadcasted_iota(jnp.int32, (1, 128), 1) < SP
        rlo = lax.broadcasted_iota(jnp.int32, (128, 1), 0) < SP
        order = list(range(nch) if rev else reversed(range(nch)))
        pre = {}
        for c in order:
            rows = slice(c * SC, (c + 1) * SC)
            dts_c = dts_ref[rows, :]
            lane, arow, cum, cumt, tot = _ssd_chunk_common(dts_c, alog_ref, m01, rev)
            pairs = []
            for pr in range(npair):
                xs = xa_ref[rows, pr * 128:(pr + 1) * 128]
                dyp = jnp.where(is_ctx, 0.0, dy_ref[rows, pr * 128:(pr + 1) * 128])
                cols = [16 * d + 2 * pr, 16 * d + 2 * pr + 1]
                cum_c = [_lane_pick(cum, lane, q) for q in cols]
                dt_c = [_lane_pick(dts_c, lane, q) for q in cols]
                tot_c = [_lane_pick(tot, lane, q) for q in cols]
                e1_c = [jnp.exp(cum_c[q]) for q in range(2)]
                e2_c = [jnp.exp(tot_c[q] - cum_c[q]) for q in range(2)]
                etot_c = [jnp.exp(tot_c[q]) for q in range(2)]
                dt_pair = jnp.where(lo, dt_c[0], dt_c[1])
                e1_pair = jnp.where(lo, e1_c[0], e1_c[1])
                e2_pair = jnp.where(lo, e2_c[0], e2_c[1])
                dtx = xs * dt_pair
                decs = [jnp.where(mb, jnp.exp(cum_c[q] - cumt[cols[q]:cols[q] + 1, :]), 0.0) for q in range(2)]
                dyq = [_mx(jnp.where(lo if q == 0 else ~lo, dyp, 0.0)) for q in range(2)]
                pairs.append(dict(xs=xs, dyp=dyp, cols=cols, e1_c=e1_c, e2_c=e2_c, etot_c=etot_c, dt_pair=dt_pair,
                                  e2_pair=e2_pair, etot_col=jnp.where(rlo, etot_c[0], etot_c[1]), dtx=dtx,
                                  dtxb=_mx(dtx), xeb=_mx(dtx * e2_pair), dy0b=_mx(dyp * e1_pair), decs=decs, dyq=dyq))
            pre[c] = dict(lane=lane, arow=arow, dts=dts_c, pairs=pairs,
                          bgb=[_mx(xa_ref[rows, D + g * SN:D + (g + 1) * SN]) for g in range(4)],
                          cgb=[_mx(xa_ref[rows, D + 512 + g * SN:D + 512 + (g + 1) * SN]) for g in range(4)])
        units = [(c, pr) for c in order for pr in range(npair)]
        P = lambda u: pre[u[0]]["pairs"][u[1]]
        cgu = lambda u: pre[u[0]]["cgb"][u[1] // 2]
        gm = {(c, g): _nt(pre[c]["cgb"][g], pre[c]["bgb"][g]) for c in order for g in range(4)}
        y0 = {u: _nt(cgu(u), sp_ref[u[0], u[1]]) for u in units}
        dcg_i = {u: _nn(P(u)["dy0b"], sp_ref[u[0], u[1]]) for u in units}
        dsl = {u: _tn(P(u)["dy0b"], cgu(u)) for u in units}
        w_ = {(u, q): gm[(u[0], u[1] // 2)] * P(u)["decs"][q] for u in units for q in range(2)}
        dw_ = {(u, q): jnp.where(mb, _nt(P(u)["dyq"][q], P(u)["dtxb"]), 0.0) for u in units for q in range(2)}
        ddtx_i = {(u, q): _tn(w_[(u, q)], P(u)["dyq"][q]) for u in units for q in range(2)}
        for c in order:
            rows = slice(c * SC, (c + 1) * SC)
            pc = pre[c]
            lane, arow, dts_c = pc["lane"], pc["arow"], pc["dts"]
            dcum = jnp.zeros((SC, 128), F32)
            ddt = jnp.zeros((SC, 128), F32)
            dtot = jnp.zeros((1, 128), F32)
            dgm = [jnp.zeros((SC, SC), F32) for _ in range(4)]
            dbg = [jnp.zeros((SC, SN), F32) for _ in range(4)]
            dcg = [jnp.zeros((SC, SN), F32) for _ in range(4)]
            for pr in range(npair):
                u, g, p = (c, pr), pr // 2, pc["pairs"][pr]
                dso = dst[pr]
                dsob = _mx(dso)
                dxe = _nt(pc["bgb"][g], dsob)
                dbg[g] = dbg[g] + _nn(p["xeb"], dsob)
                ddtx = dxe * p["e2_pair"]
                de2 = dxe * p["dtx"]
                dcg[g] = dcg[g] + dcg_i[u]
                de1 = p["dyp"] * y0[u]
                rsum = _rowsum(dso * sp_ref[c, pr].astype(F32))
                dst[pr] = dso * p["etot_col"] + dsl[u]
                for q in range(2):
                    hm = lo if q == 0 else ~lo
                    col = p["cols"][q]
                    dw = dw_[(u, q)]
                    ddtx = ddtx + jnp.where(hm, ddtx_i[(u, q)], 0.0)
                    dgm[g] = dgm[g] + dw * p["decs"][q]
                    z = dw * w_[(u, q)]
                    de1q = _rowsum(jnp.where(hm, de1, 0.0))
                    de2q = _rowsum(jnp.where(hm, de2, 0.0))
                    dcum_q = _rowsum(z) - _rowsum(z.T) + de1q * p["e1_c"][q] - de2q * p["e2_c"][q]
                    rs = rsum[0:SP, :] if q == 0 else rsum[SP:2 * SP, :]
                    dtot_q = _colsum(de2q * p["e2_c"][q]) + _colsum(rs) * p["etot_c"][q]
                    dcum = jnp.where(lane == col, dcum_q, dcum)
                    dtot = jnp.where(lane == col, dtot_q, dtot)
                dxs = ddtx * p["dt_pair"]
                ddt_pair = ddtx * p["xs"]
                for q in range(2):
                    hm = lo if q == 0 else ~lo
                    ddt = jnp.where(lane == p["cols"][q], _rowsum(jnp.where(hm, ddt_pair, 0.0)), ddt)
                if last:
                    dxs = dxs + dxp_ref[rows, pr * 128:(pr + 1) * 128]
                dxa_ref[rows, pr * 128:(pr + 1) * 128] = dxs
            for g in range(4):
                db = dbg[g] + _tn(dgm[g], pc["cgb"][g])
                dc = dcg[g] + _nn(dgm[g], pc["bgb"][g])
                if last:
                    db = db + dxp_ref[rows, D + g * SN:D + (g + 1) * SN]
                    dc = dc + dxp_ref[rows, D + 512 + g * SN:D + 512 + (g + 1) * SN]
                dxa_ref[rows, D + g * SN:D + (g + 1) * SN] = db
                dxa_ref[rows, D + 512 + g * SN:D + 512 + (g + 1) * SN] = dc
            dla = _dot01(mt01, dcum) + dtot
            ddt = ddt + dla * arow
            da_ref[0:1, :] += _colsum(dla * dts_c)
            if last:
                ddt = ddt + ddp_ref[rows, :]
            ddts_ref[rows, :] = ddt

    blk = lambda s: _blk(nb - 1 - s, nb, rev)
    in_specs = [pl.BlockSpec((TB, 2048), lambda s: (blk(s), 0)), pl.BlockSpec((TB, 128), lambda s: (blk(s), 0)),
                _full((8, 128)), pl.BlockSpec((nch, npair, 128, SN), lambda s: (blk(s), 0, 0, 0)),
                pl.BlockSpec((TB, D), lambda s: (jnp.minimum(blk(s), nb - 2), 0))]
    args = [xa, dts, alog, sprev, dy]
    if last:
        in_specs += [pl.BlockSpec((TB, 2048), lambda s: (blk(s), 0)), pl.BlockSpec((TB, 128), lambda s: (blk(s), 0))]
        args += list(prev)
    call = dict(
        body=body, args=args, name=f"ssd_bwd_{d}",
        out_shape=(jax.ShapeDtypeStruct((t_total, 2048), F32), jax.ShapeDtypeStruct((t_total, 128), F32),
                   jax.ShapeDtypeStruct((8, 128), F32)),
        grid=(nb,), in_specs=in_specs,
        out_specs=(pl.BlockSpec((TB, 2048), lambda s: (blk(s), 0)), pl.BlockSpec((TB, 128), lambda s: (blk(s), 0)),
                   _full((8, 128))),
        scratch=[pltpu.VMEM((npair, 128, SN), F32)], sem=("arbitrary",), vmem_mb=48)
    return _run(_carry(call, comm, lambda: (pl.program_id(0) == 0, pl.program_id(0) == nb - 1)))


def _readout(o, g, yy, z, vec_ref):
    hg, ss, keep = [], [], []
    for h in range(NH):
        cs = slice(h * HF, (h + 1) * HF)
        oh = o[:, cs]
        r = lax.rsqrt(jnp.mean(oh * oh, axis=1, keepdims=True) + EPS)
        hg.append(oh * r * vec_ref[0:1, cs] * _silu(g[:, cs]))
        keep.append(r)
    u = yy * _silu(z)
    for gi in range(4):
        cs = slice(gi * 256, (gi + 1) * 256)
        ug = u[:, cs]
        r = lax.rsqrt(jnp.mean(ug * ug, axis=1, keepdims=True) + EPS)
        ss.append(ug * r * vec_ref[2:3, cs])
        keep.append(r)
    return jnp.concatenate(hg, axis=1), jnp.concatenate(ss, axis=1), keep, u


def _mix_out(o_f, o_b, p_main, y_f, y_b, xa, x, vecs, w_out):
    n = x.shape[0]

    def body(of_ref, ob_ref, g_ref, z_ref, yf_ref, yb_ref, xs_ref, x_ref, vec_ref, w_ref,
             ymix_ref, ylat_ref, h1_ref, u2_ref):
        o = of_ref[...] + ob_ref[...]
        yy = yf_ref[...] + yb_ref[...] + vec_ref[1:2, :] * xs_ref[...]
        hg, ss, _, _ = _readout(o, g_ref[...].astype(F32), yy, z_ref[...].astype(F32), vec_ref)
        ymix = jnp.concatenate([hg, ss], axis=1).astype(MXU_DTYPE)
        ymix_ref[...] = ymix
        ylat = _nn(ymix, w_ref[...])
        ylat_ref[...] = ylat
        h1 = x_ref[...] + vec_ref[3:4, :] * ylat
        h1_ref[...] = h1
        r = lax.rsqrt(jnp.mean(h1 * h1, axis=1, keepdims=True) + EPS)
        u2_ref[...] = ((h1 * r * vec_ref[6:7, :]) * vec_ref[4:5, :] + vec_ref[5:6, :]).astype(MXU_DTYPE)

    row = lambda j: (lambda i: (i, j))
    return _pcall(
        body, name="mix_out",
        out_shape=(jax.ShapeDtypeStruct((n, 2 * D), MXU_DTYPE), jax.ShapeDtypeStruct((n, D), F32),
                   jax.ShapeDtypeStruct((n, D), F32), jax.ShapeDtypeStruct((n, D), MXU_DTYPE)),
        grid=(n // TB,),
        in_specs=[pl.BlockSpec((TB, D), row(0)), pl.BlockSpec((TB, D), row(0)), pl.BlockSpec((TB, D), row(4)),
                  pl.BlockSpec((TB, D), row(5)), pl.BlockSpec((TB, D), row(0)), pl.BlockSpec((TB, D), row(0)),
                  pl.BlockSpec((TB, D), row(0)), pl.BlockSpec((TB, D), row(0)), _full((8, D)), _full((2 * D, D))],
        out_specs=(pl.BlockSpec((TB, 2 * D), row(0)), pl.BlockSpec((TB, D), row(0)), pl.BlockSpec((TB, D), row(0)),
                   pl.BlockSpec((TB, D), row(0))),
        sem=("parallel",), vmem_mb=48,
    )(o_f, o_b, p_main, p_main, y_f, y_b, xa, x, vecs, w_out)


def _mix_bwd(dylat, o_f, o_b, p_main, y_f, y_b, xa, vecs, w_out):
    n = dylat.shape[0]
    t_total = p_main.shape[0]
    nlat = n // TB

    def body(*refs):
        dg_ref, dz_ref, acc_ref = refs[11], refs[13], refs[15]
        i = pl.program_id(0)

        @pl.when(i == 0)
        def _():
            acc_ref[...] = jnp.zeros_like(acc_ref)

        @pl.when(i < nlat)
        def _():
            compute(*refs)

        @pl.when(i == nlat)
        def _():
            dg_ref[...] = jnp.zeros_like(dg_ref)
            dz_ref[...] = jnp.zeros_like(dz_ref)

    def compute(dyl_ref, of_ref, ob_ref, g_ref, z_ref, yf_ref, yb_ref, xs_ref, vec_ref, w_ref,
                do_ref, dg_ref, dys_ref, dz_ref, dxs_ref, acc_ref):
        dymix = _nt(dyl_ref[...], w_ref[...])
        o = of_ref[...] + ob_ref[...]
        g = g_ref[...].astype(F32)
        z = z_ref[...].astype(F32)
        xs = xs_ref[...]
        yy = yf_ref[...] + yb_ref[...] + vec_ref[1:2, :] * xs
        _, _, keep, u = _readout(o, g, yy, z, vec_ref)
        do_l, dg_l = [], []
        for h in range(NH):
            cs = slice(h * HF, (h + 1) * HF)
            oh, gh, r, wv = o[:, cs], g[:, cs], keep[h], vec_ref[0:1, cs]
            dhg = dymix[:, cs]
            xh = oh * r
            dn = dhg * _silu(gh)
            dg_l.append(dhg * xh * wv * _dsilu(gh))
            acc_ref[0:1, cs] += _colsum(dn * xh)
            dxh = dn * wv
            do_l.append(r * (dxh - xh * jnp.mean(dxh * xh, axis=1, keepdims=True)))
        du_l = []
        for gi in range(4):
            cs = slice(gi * 256, (gi + 1) * 256)
            ug, r, wv = u[:, cs], keep[NH + gi], vec_ref[2:3, cs]
            dss = dymix[:, D + gi * 256:D + (gi + 1) * 256]
            xh = ug * r
            acc_ref[2:3, cs] += _colsum(dss * xh)
            dxh = dss * wv
            du_l.append(r * (dxh - xh * jnp.mean(dxh * xh, axis=1, keepdims=True)))
        du = jnp.concatenate(du_l, axis=1)
        dyy = du * _silu(z)
        do_ref[...] = jnp.concatenate(do_l, axis=1)
        dg_ref[...] = jnp.concatenate(dg_l, axis=1).astype(dg_ref.dtype)
        dys_ref[...] = dyy
        dz_ref[...] = (du * yy * _dsilu(z)).astype(dz_ref.dtype)
        dxs_ref[...] = dyy * vec_ref[1:2, :]
        acc_ref[1:2, :] += _colsum(dyy * xs)

    row = lambda j: (lambda i: (jnp.minimum(i, nlat - 1), j))
    lat = pl.BlockSpec((TB, D), row(0))
    tok = pl.BlockSpec((TB, D), lambda i: (i, 0))
    return _pcall(
        body, name="mix_bwd",
        out_shape=(jax.ShapeDtypeStruct((n, D), F32), jax.ShapeDtypeStruct((t_total, D), MXU_DTYPE),
                   jax.ShapeDtypeStruct((n, D), F32), jax.ShapeDtypeStruct((t_total, D), MXU_DTYPE),
                   jax.ShapeDtypeStruct((n, D), F32), jax.ShapeDtypeStruct((8, D), F32)),
        grid=(t_total // TB,),
        in_specs=[lat, lat, lat, pl.BlockSpec((TB, D), row(4)), pl.BlockSpec((TB, D), row(5)), lat, lat, lat,
                  _full((8, D)), _full((2 * D, D))],
        out_specs=(lat, tok, lat, tok, lat, _full((8, D))),
        sem=("arbitrary",), vmem_mb=48,
    )(dylat, o_f, o_b, p_main, p_main, y_f, y_b, xa, vecs, w_out)


def _ffn_up(u2, w_gate, w_up):
    n = u2.shape[0]
    tb = 1024

    def body(u_ref, wg_ref, wu_ref, g_ref, up_ref, a_ref):
        uv = u_ref[...]
        gt = _nn(uv, wg_ref[...])
        upv = _nn(uv, wu_ref[...])
        g_ref[...] = gt.astype(g_ref.dtype)
        up_ref[...] = upv.astype(up_ref.dtype)
        a_ref[...] = (_silu(gt) * upv).astype(a_ref.dtype)

    blk = pl.BlockSpec((tb, FSL), lambda j, i: (i, j))
    wblk = pl.BlockSpec((None, D, FSL), lambda j, i: (j, 0, 0))
    return _pcall(
        body, name="ffn_up",
        out_shape=(jax.ShapeDtypeStruct((n, DFFP), MXU_DTYPE),) * 3,
        grid=(4, n // tb), in_specs=[pl.BlockSpec((tb, D), lambda j, i: (i, 0)), wblk, wblk],
        out_specs=(blk, blk, blk), sem=("parallel", "parallel"), vmem_mb=48,
    )(u2, w_gate, w_up)


def _ffn_down_loss(act, w_down, h1, tgt, vecs):
    n = act.shape[0]
    tb = 512

    def body(a_ref, w_ref, h1_ref, t_ref, vec_ref, dh2_ref, dffn_ref, acc_ref):
        i = pl.program_id(0)

        @pl.when(i == 0)
        def _():
            acc_ref[...] = jnp.zeros_like(acc_ref)

        ffn = _nn(a_ref[...], w_ref[...])
        g2 = vec_ref[0:1, :]
        fw = vec_ref[1:2, :]
        h2 = h1_ref[...] + g2 * ffn
        r = lax.rsqrt(jnp.mean(h2 * h2, axis=1, keepdims=True) + EPS)
        xh = h2 * r
        err = xh * fw - t_ref[...]
        dy = err * (1.0 / D)
        acc_ref[2:3, :] += _colsum(err * err) * (0.5 / D)
        acc_ref[1:2, :] += _colsum(dy * xh)
        dxh = dy * fw
        dh2 = r * (dxh - xh * jnp.mean(dxh * xh, axis=1, keepdims=True))
        dh2_ref[...] = dh2
        dffn_ref[...] = (g2 * dh2).astype(dffn_ref.dtype)
        acc_ref[0:1, :] += _colsum(dh2 * ffn)

    return _pcall(
        body, name="ffn_down_loss",
        out_shape=(jax.ShapeDtypeStruct((n, D), F32), jax.ShapeDtypeStruct((n, D), MXU_DTYPE),
                   jax.ShapeDtypeStruct((8, D), F32)),
        grid=(n // tb,),
        in_specs=[pl.BlockSpec((tb, DFFP), lambda i: (i, 0)), _full((DFFP, D)), pl.BlockSpec((tb, D), lambda i: (i, 0)),
                  pl.BlockSpec((tb, D), lambda i: (i, 0)), _full((8, D))],
        out_specs=(pl.BlockSpec((tb, D), lambda i: (i, 0)), pl.BlockSpec((tb, D), lambda i: (i, 0)), _full((8, D))),
        sem=("arbitrary",), vmem_mb=48,
    )(act, w_down, h1, tgt, vecs)


def _ffn_bwd(dffn, w_down, gate, up, w_gate, w_up, h1, ylat, dh2, vecs):
    n = dffn.shape[0]
    tb = 256

    def body(df_ref, wd_ref, g_ref, up_ref, wg_ref, wu_ref, h1_ref, yl_ref, dh2_ref, vec_ref,
             dg_ref, dup_ref, dh1_ref, dyl_ref, acc_ref, du_scr):
        j, i = pl.program_id(0), pl.program_id(1)
        rows = pl.ds(pl.multiple_of(i * tb, tb), tb)

        @pl.when((i == 0) & (j == 0))
        def _():
            acc_ref[...] = jnp.zeros_like(acc_ref)

        dact = _nt(df_ref[...], wd_ref[...])
        gt = g_ref[...].astype(F32)
        upv = up_ref[...].astype(F32)
        dgt = (dact * upv * _dsilu(gt)).astype(MXU_DTYPE)
        dupv = (dact * _silu(gt)).astype(MXU_DTYPE)
        dg_ref[...] = dgt
        dup_ref[...] = dupv
        part = _nt(dgt, wg_ref[...]) + _nt(dupv, wu_ref[...])

        @pl.when(j == 0)
        def _():
            du_scr[rows, :] = part

        @pl.when((j > 0) & (j < 3))
        def _():
            du_scr[rows, :] += part

        @pl.when(j == 3)
        def _():
            du = du_scr[rows, :] + part
            h1 = h1_ref[...]
            r = lax.rsqrt(jnp.mean(h1 * h1, axis=1, keepdims=True) + EPS)
            xh = h1 * r
            nw = vec_ref[2:3, :]
            acc_ref[0:1, :] += _colsum(du)
            acc_ref[1:2, :] += _colsum(du * xh * nw)
            dn = du * vec_ref[1:2, :]
            acc_ref[2:3, :] += _colsum(dn * xh)
            dxh = dn * nw
            dh1 = dh2_ref[...] + r * (dxh - xh * jnp.mean(dxh * xh, axis=1, keepdims=True))
            dh1_ref[...] = dh1
            dyl_ref[...] = (vec_ref[0:1, :] * dh1).astype(dyl_ref.dtype)
            acc_ref[3:4, :] += _colsum(dh1 * yl_ref[...])

    tok = pl.BlockSpec((tb, D), lambda j, i: (i, 0))
    last = pl.BlockSpec((tb, D), lambda j, i: (jnp.where(j == 3, i, 0), 0))
    ffb = pl.BlockSpec((tb, FSL), lambda j, i: (i, j))
    wsl = pl.BlockSpec((None, D, FSL), lambda j, i: (j, 0, 0))
    return _pcall(
        body, name="ffn_bwd",
        out_shape=(jax.ShapeDtypeStruct((n, DFFP), MXU_DTYPE), jax.ShapeDtypeStruct((n, DFFP), MXU_DTYPE),
                   jax.ShapeDtypeStruct((n, D), F32), jax.ShapeDtypeStruct((n, D), MXU_DTYPE),
                   jax.ShapeDtypeStruct((8, D), F32)),
        grid=(4, n // tb),
        in_specs=[tok, pl.BlockSpec((FSL, D), lambda j, i: (j, 0)), ffb, ffb, wsl, wsl,
                  last, last, last, _full((8, D))],
        out_specs=(ffb, ffb, last, last, _full((8, D))),
        scratch=[pltpu.VMEM((n, D), F32)], sem=("arbitrary", "arbitrary"), vmem_mb=58,
    )(dffn, w_down, gate, up, w_gate, w_up, h1, ylat, dh2, vecs)


def _deep_rows(rows):
    return max(r for r in range(128, 2305, 128) if rows % r == 0)


def _dw(a, b, name, slabs=None):
    tn_rows = a.shape[0]
    bt = _deep_rows(tn_rows)
    kk, nn_ = a.shape[1], b.shape[1]
    bk = 1024 if kk % 1024 == 0 else kk
    bn = slabs if slabs is not None else (1024 if nn_ % 1024 == 0 else nn_)
    nt = tn_rows // bt
    ka, nb_ = kk // bk, nn_ // bn

    def body(a_ref, b_ref, o_ref, acc):
        t = pl.program_id(2)
        part = _tn(a_ref[...], b_ref[...])

        @pl.when(t == 0)
        def _():
            acc[...] = part

        @pl.when(t > 0)
        def _():
            acc[...] += part

        @pl.when(t == nt - 1)
        def _():
            o_ref[...] = acc[...].astype(o_ref.dtype)

    if slabs is None:
        out_shape = jax.ShapeDtypeStruct((kk, nn_), MXU_DTYPE)
        out_spec = pl.BlockSpec((bk, bn), lambda i, j, t: (i, j))
    else:
        out_shape = jax.ShapeDtypeStruct((nb_, kk, bn), MXU_DTYPE)
        out_spec = pl.BlockSpec((None, bk, bn), lambda i, j, t: (j, i, 0))
    return _pcall(
        body, name=name, out_shape=out_shape, grid=(ka, nb_, nt),
        in_specs=[pl.BlockSpec((bt, bk), lambda i, j, t: (t, i)), pl.BlockSpec((bt, bn), lambda i, j, t: (t, j))],
        out_specs=out_spec, scratch=[pltpu.VMEM((bk, bn), F32)],
        sem=("parallel", "parallel", "arbitrary"), vmem_mb=40,
    )(a, b)


def _dw_in(u_all, segs, name):
    tiles = []
    for m, s_ in enumerate(segs):
        tiles += [(m, h) for h in range(s_.shape[1] // D)]
    ntile = len(tiles)
    t_total = u_all.shape[0]
    bt = _deep_rows(t_total)
    nt = t_total // bt

    def body(u_ref, *refs):
        seg_refs, o_ref, acc = refs[:len(segs)], refs[len(segs)], refs[len(segs) + 1]
        n, t = pl.program_id(0), pl.program_id(1)
        for k, (m, _) in enumerate(tiles):
            @pl.when(n == k)
            def _(m=m):
                part = _tn(u_ref[...], seg_refs[m][...])

                @pl.when(t == 0)
                def _():
                    acc[...] = part

                @pl.when(t > 0)
                def _():
                    acc[...] += part

        @pl.when(t == nt - 1)
        def _():
            o_ref[...] = acc[...].astype(o_ref.dtype)

    def seg_spec(m):
        ks = [k for k, (mm, _) in enumerate(tiles) if mm == m]
        lo, hi = ks[0], ks[-1]
        on = lambda n: (n >= lo) & (n <= hi)
        return pl.BlockSpec((bt, D), lambda n, t: (jnp.where(on(n), t, 0), jnp.where(on(n), n - lo, 0)))

    return _pcall(
        body, name=name, out_shape=jax.ShapeDtypeStruct((ntile // 2, D, 2 * D), MXU_DTYPE), grid=(ntile, nt),
        in_specs=[pl.BlockSpec((bt, D), lambda n, t: (t, 0))] + [seg_spec(m) for m in range(len(segs))],
        out_specs=pl.BlockSpec((None, D, D), lambda n, t: (n // 2, 0, n % 2)),
        scratch=[pltpu.VMEM((D, D), F32)], sem=("parallel", "arbitrary"), vmem_mb=56,
    )(u_all, *segs)


def _du_prenorm_bwd(segs, ddt, wi_main, wi_tail, xin, mods, dres, row_off, tb, name, comm=None):
    n = xin.shape[0]
    nt = n // tb
    off = row_off // tb
    has_dx = dres is not None

    def body(*refs):
        seg_refs = refs[:7]
        ddt_ref, w_ref, wb_ref, wdt_ref, x_ref, mod_ref = refs[7:13]
        rest = refs[13:]
        if has_dx:
            dres_ref, dx_ref, acc_ref, du_scr = rest
        else:
            acc_ref, du_scr = rest
        j, i = pl.program_id(0), pl.program_id(1)
        rows = pl.ds(pl.multiple_of(i * tb, tb), tb)

        @pl.when((i == 0) & (j == 0))
        def _():
            acc_ref[...] = jnp.zeros_like(acc_ref)

        @pl.when(j == 0)
        def _():
            du_scr[rows, :] = _nt(ddt_ref[...], wdt_ref[...])

        for k in range(8):
            if not has_dx and k in (4, 5):
                continue

            @pl.when(j == k)
            def _(k=k):
                sv = seg_refs[min(k, 6)][...]
                part = _nt(sv, w_ref[...])
                if k in (2, 4, 6):
                    part = part + _nt(sv[:, 0:WTAIL], wb_ref[...])
                du_scr[rows, :] += part

        @pl.when(j == 7)
        def _():
            du = du_scr[rows, :]
            xv = x_ref[...]
            r = lax.rsqrt(jnp.mean(xv * xv, axis=1, keepdims=True) + EPS)
            xh = xv * r
            nw = mod_ref[1:2, :]
            acc_ref[0:1, :] += _colsum(du)
            acc_ref[1:2, :] += _colsum(du * xh * nw)
            dn = du * mod_ref[0:1, :]
            acc_ref[2:3, :] += _colsum(dn * xh)
            if has_dx:
                dxh = dn * nw
                dx_ref[...] = dres_ref[...] + r * (dxh - xh * jnp.mean(dxh * xh, axis=1, keepdims=True))

    def seg_spec(k):
        if k < 6:
            return pl.BlockSpec((tb, D), lambda j, i: (jnp.where(j == k, i + off, 0), 0))
        return pl.BlockSpec((tb, D), lambda j, i: (jnp.where(j >= 6, i + off, 0), jnp.where(j >= 6, j - 6, 0)))

    last = pl.BlockSpec((tb, D), lambda j, i: (jnp.where(j == 7, i, 0), 0))
    in_specs = [seg_spec(k) for k in range(7)]
    in_specs += [pl.BlockSpec((tb, 128), lambda j, i: (jnp.where(j == 0, i + off, 0), 0))] + _w_specs()
    in_specs += [last, _full((8, D))]
    args = list(segs) + [ddt, wi_main, wi_tail, wi_tail, xin, mods]
    out_shape = [jax.ShapeDtypeStruct((8, D), F32)]
    out_specs = [_full((8, D))]
    if has_dx:
        in_specs.append(last)
        args.append(dres)
        out_shape.insert(0, jax.ShapeDtypeStruct((n, D), F32))
        out_specs.insert(0, last)
    call = dict(body=body, args=args, name=name, out_shape=tuple(out_shape), grid=(8, nt), in_specs=in_specs,
                out_specs=tuple(out_specs), scratch=[pltpu.VMEM((n, D), F32)], sem=("arbitrary", "arbitrary"),
                vmem_mb=56)
    steps = lambda: ((pl.program_id(0) == 0) & (pl.program_id(1) == 0),
                     (pl.program_id(0) == 7) & (pl.program_id(1) == nt - 1))
    return _run(_carry(call, comm, steps))


def _sum8(v):
    def body(v_ref, o_ref):
        acc = v_ref[0]
        for k in range(1, 8):
            acc = acc + v_ref[k]
        o_ref[...] = acc

    return _pcall(body, name="small_sum", out_shape=jax.ShapeDtypeStruct(v.shape[1:], F32),
                  in_specs=[pl.BlockSpec(memory_space=pltpu.VMEM)], out_specs=pl.BlockSpec(memory_space=pltpu.VMEM))(v)


def _adamw(w, m, v, g, name):
    lead = w.ndim == 3
    rows, cols = w.shape[-2:]
    rb = 256 if rows % 256 == 0 else (352 if rows % 352 == 0 else rows)
    c1 = 1.0 - B1 ** STEP
    c2 = 1.0 - B2 ** STEP

    def body(w_ref, m_ref, v_ref, g_ref, d_ref, nm_ref, nv_ref):
        gv = g_ref[...]
        mn = B1 * m_ref[...] + (1.0 - B1) * gv
        vn = B2 * v_ref[...] + (1.0 - B2) * (gv * gv)
        nm_ref[...] = mn
        nv_ref[...] = vn
        d_ref[...] = -LR * ((mn / c1) / (jnp.sqrt(vn / c2) + AEPS) + WD * w_ref[...])

    gspec = pl.BlockSpec((rb, cols), lambda i: (i, 0))
    spec = pl.BlockSpec((None, rb, cols), lambda i: (0, i, 0)) if lead else gspec
    return _pcall(
        body, name=name, out_shape=(jax.ShapeDtypeStruct(w.shape, F32),) * 3, grid=(rows // rb,),
        in_specs=[spec] * 3 + [gspec], out_specs=(spec,) * 3, sem=("parallel",), vmem_mb=40,
    )(w, m, v, g)


def _rows(v, n):
    f = v.reshape(-1)
    return jnp.pad(f, (0, n * D - f.shape[0])).reshape(n, D)


def kernel(x, c, ctx, c_ctx, w_ada, b_ada, norm_mix, w_in, conv_w, conv_b, ssd_a_log, ssd_dt_bias, ssd_d, ssd_norm, hgrn_lb_raw, hgrn_norm, w_out, norm_ffn, w_gate, w_up, w_down, final_norm, loss_target, m_c_ctx, m_w_ada, m_b_ada, m_norm_mix, m_w_in, m_conv_w, m_conv_b, m_ssd_a_log, m_ssd_dt_bias, m_ssd_d, m_ssd_norm, m_hgrn_lb_raw, m_hgrn_norm, m_w_out, m_norm_ffn, m_w_gate, m_w_up, m_w_down, m_final_norm, v_c_ctx, v_w_ada, v_b_ada, v_norm_mix, v_w_in, v_conv_w, v_conv_b, v_ssd_a_log, v_ssd_dt_bias, v_ssd_d, v_ssd_norm, v_hgrn_lb_raw, v_hgrn_norm, v_w_out, v_norm_ffn, v_w_gate, v_w_up, v_w_down, v_final_norm):
    ix, iy, ic = lax.axis_index("x"), lax.axis_index("y"), lax.axis_index("c")
    chip = 2 * ix + iy
    me = 2 * chip + ic
    xl, xc, tgt = x[0], ctx[0], loss_target[0]
    n_lat, n_ctx = xl.shape[0], xc.shape[0]
    assert n_ctx == TB and n_lat % 1024 == 0
    t_total = n_lat + n_ctx
    nb = t_total // TB

    pack = jnp.concatenate([c, hgrn_lb_raw.reshape(1, D), _rows(conv_w[0], 3), jnp.zeros((3, D), F32)], axis=0)
    gath = _allgather8(pack, "small_gather").reshape(8, 8, D)
    c_all = gath[:, 0]
    lbraw_full = gath[0::2, 1].reshape(4, 2, 2, 256).transpose(1, 2, 0, 3).reshape(4, D)
    convw_full = gath[0::2, 2:5].reshape(4, 3 * D)[:, :KCONV * 512].reshape(4, KCONV, 512).transpose(1, 0, 2)
    convw_full = convw_full.reshape(KCONV, 2048)
    lbraw8 = jnp.pad(lbraw_full, ((0, 4), (0, 0)))
    convp = jnp.concatenate([convw_full, conv_b, jnp.zeros((2, 2048), F32)], axis=0)
    dtb = jnp.pad(ssd_dt_bias.reshape(1, 32), ((0, 7), (0, 96)))
    alog = jnp.pad(ssd_a_log.reshape(1, 32), ((0, 7), (0, 96)))

    araw = jnp.concatenate([c_all, c_ctx.reshape(1, D), jnp.zeros((7, D), F32)], axis=0)
    ncol_ada = w_ada.shape[2]
    b_shard = lax.dynamic_slice(b_ada, (0, chip * ncol_ada), (1, ncol_ada))
    mod_shard = _ada_fwd(araw, w_ada[0], b_shard)
    mod_all = _allgather8(mod_shard, "mod_gather").reshape(8, 16, ncol_ada)[0::2]
    mod_full = mod_all.transpose(1, 0, 2).reshape(16, 4 * ncol_ada)
    my_mod = lax.dynamic_slice(mod_full, (me, 0), (1, 6 * D)).reshape(6, D)
    sh1, sc1, g1, sh2, sc2, g2 = (my_mod[k:k + 1] for k in range(6))
    csh1, csc1 = mod_full[8:9, 0:D], mod_full[8:9, D:2 * D]

    shift = [functools.partial(jnp.pad, pad_width=((0, 0), (8 * k, WSL + WTAIL - NSH - 8 * k))) for k in range(4)]
    slab = lax.switch(chip, shift, w_in[0].astype(MXU_DTYPE))
    shards = [slab[:, :WSL], slab[:, WSL:], w_out[0].astype(MXU_DTYPE),
              jnp.pad(w_gate[0], ((0, 0), (0, FSL - DFF // 4))).astype(MXU_DTYPE),
              jnp.pad(w_up[0], ((0, 0), (0, FSL - DFF // 4))).astype(MXU_DTYPE),
              jnp.pad(w_down[0], ((0, FSL - DFF // 4), (0, 0))).astype(MXU_DTYPE)]
    own = lambda g_, s_: lax.dynamic_update_slice(g_, s_[None], (chip, 0, 0))
    wi_main, wi_tail = (own(g_, s_) for g_, s_ in zip(_weights_allgather(shards[:2]), shards[:2]))

    zrow = jnp.zeros((1, D), F32)
    mods_lat = jnp.concatenate([1.0 + sc1, sh1, norm_mix, zrow, zrow, zrow, zrow, zrow], axis=0)
    mods_ctx = jnp.concatenate([1.0 + csc1, csh1, norm_mix, zrow, zrow, zrow, zrow, zrow], axis=0)
    outs = _inproj(xl, mods_lat, wi_main, wi_tail, t_total, 1024, 0, None, "inproj_lat",
                   comm=_comm_gather(shards[2:]))
    wo_g, wg_g, wu_g, wd_g = (own(g_, s_) for g_, s_ in zip(outs[3:], shards[2:]))
    w_out_f = wo_g.reshape(2 * D, D)
    w_down_f = wd_g.reshape(DFFP, D)
    p_main, p_dt, u_all = _inproj(xc, mods_ctx, wi_main, wi_tail, t_total, TB, nb - 1, outs[:3], "inproj_ctx")

    o_f, hs_f = _hgrn_fwd(p_main, lbraw8, 0, nb)
    o_b, hs_b = _hgrn_fwd(p_main, lbraw8, 1, nb)
    xa, dts = _ssd_prep(p_main, p_dt, convp, dtb, nb)
    y_f, ss_f = _ssd_fwd(xa, dts, alog, 0, nb)
    y_b, ss_b = _ssd_fwd(xa, dts, alog, 1, nb)

    vec_mix = jnp.concatenate([jnp.tile(hgrn_norm, (1, NH)), jnp.repeat(ssd_d, SP, axis=1), ssd_norm, g1, 1.0 + sc2,
                               sh2, norm_ffn, zrow], axis=0)
    ymix, ylat, h1, u2 = _mix_out(o_f, o_b, p_main, y_f, y_b, xa, xl, vec_mix, w_out_f)
    gate, up, act = _ffn_up(u2, wg_g, wu_g)
    vec_loss = jnp.concatenate([g2, final_norm.reshape(1, D)] + [zrow] * 6, axis=0)
    dh2, dffn, acc_loss = _ffn_down_loss(act, w_down_f, h1, tgt, vec_loss)

    core_arr = jnp.reshape(ic, (1,)).astype(jnp.int32)
    chip_arr = jnp.reshape(chip, (1,)).astype(jnp.int32)
    every = (0, 4)

    def pair_stage(gs, tag):
        return list(_pair_sum(gs, _pair_exchange(gs, "grads_pair_exchange_" + tag), core_arr, "grads_pair_sum_" + tag))

    vec_ffn = jnp.concatenate([g1, 1.0 + sc2, norm_ffn] + [zrow] * 5, axis=0)
    dgate, dup, dh1, dylat, acc_ffn = _ffn_bwd(dffn, w_down_f, gate, up, wg_g, wu_g, h1, ylat, dh2, vec_ffn)
    gw_down = _dw(act, dffn, "dw_down").reshape(4, FSL, D)
    gw_gate = _dw(u2, dgate, "dw_gate", slabs=FSL)
    gw_up = _dw(u2, dup, "dw_up", slabs=FSL)
    do, dgr, dys, dzr, dxs_skip, acc_mix = _mix_bwd(dylat, o_f, o_b, p_main, y_f, y_b, xa, vec_mix, w_out_f)
    gw_out = _dw(ymix, dylat, "dw_out").reshape(4, D // 2, D)
    pair_a, dests_a = pair_stage([gw_gate, gw_up], "a1") + pair_stage([gw_down, gw_out], "a2"), [every] * 4

    res = _hgrn_bwd(p_main, lbraw8, hs_f, do, 0, nb, None, comm=_comm_exchange(pair_a[:2], dests_a[:2]))
    (dq0, dff, dv0, dlb_f), recv_a = res[:4], list(res[4:])
    res = _hgrn_bwd(p_main, lbraw8, hs_b, do, 1, nb, (dq0, dv0), comm=_comm_exchange(pair_a[2:], dests_a[2:]))
    (dq, dfb, dv, dlb_b), recv_a = res[:4], recv_a + list(res[4:])
    gw_in = [_dw_in(u_all, [dq, dff], "dw_in_0"), _dw_in(u_all, [dfb, dv], "dw_in_1"),
             _dw_in(u_all, [dgr, dzr], "dw_in_2")]
    pair_b, dests_b = pair_stage(gw_in, "b"), [(0, 1), (1, 2), (2, 3)]

    res = _ssd_bwd(xa, dts, alog, ss_f, dys, 0, nb, None, comm=_comm_exchange(pair_b, dests_b))
    (dxa0, ddts0, da_f), recv_b = res[:3], list(res[3:])
    dxa, ddts, da_b = _ssd_bwd(xa, dts, alog, ss_b, dys, 1, nb, (dxa0, ddts0))
    dxbc, ddt, acc_conv, acc_dtb = _ssd_prep_bwd(p_main, p_dt, convp, dtb, dxa, dxs_skip, ddts, nb)
    gw_in.append(_dw_in(u_all, [dxbc], "dw_in_3"))
    gw_in_dt = _dw(u_all, ddt, "dw_in_dt")
    gw_in_tail = jnp.concatenate([g_[:, :, 0:WTAIL] for g_ in gw_in[1:]] + [gw_in_dt[None]], axis=0)
    pair_c, dests_c = pair_stage([gw_in[3], gw_in_tail], "c"), [(3, 4), every]

    segs = [dq, dff, dfb, dv, dgr, dzr, dxbc]
    bmods_lat = jnp.concatenate([1.0 + sc1, norm_mix] + [zrow] * 6, axis=0)
    bmods_ctx = jnp.concatenate([1.0 + csc1, norm_mix] + [zrow] * 6, axis=0)
    res = _du_prenorm_bwd(segs, ddt, wi_main, wi_tail, xl, bmods_lat, dh1, 0, 512, "du_lat",
                          comm=_comm_exchange(pair_c, dests_c))
    (grad_x, acc_lat), recv_c = res[:2], list(res[2:])
    (acc_ctx,) = _du_prenorm_bwd(segs, ddt, wi_main, wi_tail, xc, bmods_ctx, None, n_lat, TB, "du_ctx")

    mine = _chip_sum(pair_b + pair_c + pair_a, recv_b + recv_c + recv_a, chip_arr, dests_b + dests_c + dests_a,
                     [0, 0, 0, 0, 1, 3, 4, 5, 2])
    theirs = _pair_swap(mine)
    whole = [jnp.concatenate([jnp.where(ic == 0, m_, t_), jnp.where(ic == 0, t_, m_)], axis=0)
             for m_, t_ in zip(mine, theirs)]
    g_w_in = lax.dynamic_slice(jnp.concatenate(whole[0:2], axis=1), (0, 8 * chip), (D, NSH))
    g_w_out = whole[2]
    g_w_gate = whole[3][:, :DFF // 4]
    g_w_up = whole[4][:, :DFF // 4]
    g_w_down = whole[5][:DFF // 4]

    dmod_lat = jnp.concatenate([acc_lat[0:2], acc_ffn[3:4], acc_ffn[0:2], acc_loss[0:1]], axis=0)
    misc = jnp.concatenate([(da_f + da_b)[0, :32], jnp.zeros((96,), F32), acc_dtb[0, :32], jnp.zeros((96,), F32),
                            jnp.sum(acc_loss[2]).reshape(1), jnp.zeros((D - 257,), F32)]).reshape(1, D)
    sv = jnp.concatenate([
        dmod_lat, acc_ctx[0:2], (acc_lat[2:3] + acc_ctx[2:3]), acc_ffn[2:3], acc_loss[1:2], acc_mix[2:3],
        acc_mix[0:1], acc_mix[1:2], dlb_f[0:1], dlb_b[0:1], acc_conv[0:6].reshape(12, D), misc,
        jnp.zeros((3, D), F32)], axis=0)
    sv_all = _allgather8(sv, "small_grads_gather").reshape(8, 32, D)
    ssum = _sum8(sv_all)
    dmod_rows = sv_all[:, 0:6].reshape(8, 6 * D)
    dmod_ctx_row = jnp.concatenate([ssum[6:8].reshape(1, 2 * D), jnp.zeros((1, 4 * D), F32)], axis=1)
    dmod_full = jnp.concatenate([dmod_rows, dmod_ctx_row, jnp.zeros((7, 6 * D), F32)], axis=0)
    grad_b_ada = jnp.sum(dmod_full, axis=0, keepdims=True)
    dmod_shard = lax.dynamic_slice(dmod_full, (0, chip * ncol_ada), (16, ncol_ada))
    g_w_ada, da_part = _ada_bwd(araw, dmod_shard, w_ada[0])
    da_all = _allgather8(da_part, "ada_ctx_gather").reshape(8, 16, D)[0::2, 8]
    cc = c_ctx.reshape(1, D)
    grad_c_ctx = (jnp.sum(da_all, axis=0, keepdims=True) * _dsilu(cc)).reshape(D)

    grad_norm_mix, grad_norm_ffn, grad_final_norm = ssum[8:9], ssum[9:10], ssum[10].reshape(D)
    grad_ssd_norm = ssum[11:12]
    grad_hgrn_norm = jnp.sum(ssum[12].reshape(NH, HF), axis=0, keepdims=True)
    grad_ssd_d = jnp.sum(ssum[13].reshape(SHEADS, SP), axis=1).reshape(1, SHEADS)
    lb_full = _sig(lbraw_full[0:2] - lbraw_full[2:4])
    dr0 = ssum[14:16] * lb_full * (1.0 - lb_full)
    grad_lb_full = jnp.stack([dr0, -dr0], axis=0)
    grad_lb = lax.dynamic_slice(grad_lb_full, (0, 0, chip * 256), (2, 2, 256))
    grad_conv_w = lax.dynamic_slice(ssum[16:26].reshape(KCONV, 2048), (0, chip * 512), (KCONV, 512)).reshape(1, KCONV, 512)
    grad_conv_b = ssum[26:28].reshape(1, 2048)
    a_val = -jnp.exp(ssd_a_log)
    grad_a_log = ssum[28, 0:32].reshape(1, 2, SHEADS) * a_val
    grad_dt_bias = ssum[28, 128:160].reshape(1, 2, SHEADS)
    loss = ssum[28, 256]

    small_w = [c_ctx, b_ada, norm_mix, conv_w, conv_b, ssd_a_log, ssd_dt_bias, ssd_d, ssd_norm, hgrn_lb_raw,
               hgrn_norm, norm_ffn, final_norm]
    small_m = [m_c_ctx, m_b_ada, m_norm_mix, m_conv_w, m_conv_b, m_ssd_a_log, m_ssd_dt_bias, m_ssd_d, m_ssd_norm,
               m_hgrn_lb_raw, m_hgrn_norm, m_norm_ffn, m_final_norm]
    small_v = [v_c_ctx, v_b_ada, v_norm_mix, v_conv_w, v_conv_b, v_ssd_a_log, v_ssd_dt_bias, v_ssd_d, v_ssd_norm,
               v_hgrn_lb_raw, v_hgrn_norm, v_norm_ffn, v_final_norm]
    small_g = [grad_c_ctx, grad_b_ada, grad_norm_mix, grad_conv_w, grad_conv_b, grad_a_log, grad_dt_bias, grad_ssd_d,
               grad_ssd_norm, grad_lb, grad_hgrn_norm, grad_norm_ffn, grad_final_norm]
    nrows = [-(-a.size // D) for a in small_w]
    packs = lambda lst: jnp.concatenate([_rows(a, r) for a, r in zip(lst, nrows)]
                                        + [jnp.zeros((24 - sum(nrows), D), F32)], axis=0)
    sd, sm, svv = _adamw(packs(small_w), packs(small_m), packs(small_v), packs(small_g), "adamw_small")

    def unpack(p):
        out, r0 = [], 0
        for a, r in zip(small_w, nrows):
            out.append(p[r0:r0 + r].reshape(-1)[:a.size].reshape(a.shape))
            r0 += r
        return out

    sd, sm, svv = unpack(sd), unpack(sm), unpack(svv)
    big = {}
    for nm, w_, m_, v_, g_ in (("w_ada", w_ada, m_w_ada, v_w_ada, g_w_ada), ("w_in", w_in, m_w_in, v_w_in, g_w_in),
                               ("w_out", w_out, m_w_out, v_w_out, g_w_out),
                               ("w_gate", w_gate, m_w_gate, v_w_gate, g_w_gate),
                               ("w_up", w_up, m_w_up, v_w_up, g_w_up),
                               ("w_down", w_down, m_w_down, v_w_down, g_w_down)):
        big[nm] = (g_[None],) + tuple(_adamw(w_, m_, v_, g_, "adamw_" + nm))

    order = ["c_ctx", "w_ada", "b_ada", "norm_mix", "w_in", "conv_w", "conv_b", "ssd_a_log", "ssd_dt_bias", "ssd_d",
             "ssd_norm", "hgrn_lb_raw", "hgrn_norm", "w_out", "norm_ffn", "w_gate", "w_up", "w_down", "final_norm"]
    small_names = ["c_ctx", "b_ada", "norm_mix", "conv_w", "conv_b", "ssd_a_log", "ssd_dt_bias", "ssd_d", "ssd_norm",
                   "hgrn_lb_raw", "hgrn_norm", "norm_ffn", "final_norm"]
    table = dict(big)
    for k, nm in enumerate(small_names):
        table[nm] = (small_g[k].reshape(small_w[k].shape), sd[k], sm[k], svv[k])
    grads = [table[nm][0] for nm in order]
    deltas = [table[nm][1] for nm in order]
    new_m = [table[nm][2] for nm in order]
    new_v = [table[nm][3] for nm in order]
    return (loss, grad_x[None], *grads, *deltas, *new_m, *new_v)
```

```python
import functools
import math

import jax
import jax.numpy as jnp
from jax import lax
from jax.experimental import pallas as pl
from jax.experimental.pallas import tpu as pltpu

F32 = jnp.float32
BF16 = jnp.bfloat16
MXU_DTYPE = jnp.bfloat16
_INTERPRET = False

D = 1024
NH, HF = 8, 128
HC = 64
SC = 128
SN = 128
SHEADS, SP = 16, 64
GRID_W = 64
KCONV = 5
DFF = 2816
FSL = 768
DFFP = 4 * FSL
NIN = 8224
TB = 256
EPS = 1e-6
LR, B1, B2, AEPS, WD, STEP = 0.001, 0.9, 0.999, 1e-08, 0.01, 10
MESH_ID = pl.DeviceIdType.MESH
NSH = NIN // 4
WSL = 2048
WTAIL = 128


def _pcall(body, *, name, out_shape, grid=(), in_specs=None, out_specs=None, scratch=(), sem=None,
           vmem_mb=None, aliases=None):
    params = {}
    if sem is not None:
        params["dimension_semantics"] = sem
    if vmem_mb is not None:
        params["vmem_limit_bytes"] = vmem_mb << 20
    kw = dict(name=name, out_shape=out_shape, scratch_shapes=list(scratch),
              input_output_aliases=aliases or {}, compiler_params=pltpu.CompilerParams(**params),
              interpret=_INTERPRET)
    if grid:
        kw["grid"] = grid
    if in_specs is not None:
        kw["in_specs"] = in_specs
    if out_specs is not None:
        kw["out_specs"] = out_specs
    return pl.pallas_call(body, **kw)


def _mx(a):
    return a.astype(MXU_DTYPE)


def _dg(a, b, ca, cb):
    return lax.dot_general(_mx(a), _mx(b), (((ca,), (cb,)), ((), ())), preferred_element_type=F32)


def _nn(a, b):
    return _dg(a, b, 1, 0)


def _nt(a, b):
    return _dg(a, b, 1, 1)


def _tn(a, b):
    return _dg(a, b, 0, 0)


def _dot01(m, x):
    hi = x.astype(BF16)
    r1 = x - hi.astype(F32)
    mid = r1.astype(BF16)
    lo = (r1 - mid.astype(F32)).astype(BF16)
    f = lambda t: lax.dot_general(m, t, (((1,), (0,)), ((), ())), preferred_element_type=F32)
    return f(hi) + f(mid) + f(lo)


def _tri(n, upper):
    r = lax.broadcasted_iota(jnp.int32, (n, n), 0)
    c = lax.broadcasted_iota(jnp.int32, (n, n), 1)
    return (c >= r) if upper else (c <= r)


def _b01(mask):
    return jnp.where(mask, 1.0, 0.0).astype(BF16)


def _sig(x):
    return jax.nn.sigmoid(x)


def _silu(x):
    return x * _sig(x)


def _dsilu(x):
    s = _sig(x)
    return s * (1.0 + x * (1.0 - s))


def _softplus(x):
    return jnp.maximum(x, 0.0) + jnp.log(1.0 + jnp.exp(-jnp.abs(x)))


def _rowsum(x):
    return jnp.sum(x, axis=1, keepdims=True)


def _colsum(x):
    return jnp.sum(x, axis=0, keepdims=True)


def _full(shape):
    return pl.BlockSpec(shape, lambda *_: (0,) * len(shape))


def _allgather8(v, name):
    m_per, n = v.shape

    def body(x_ref, out_ref, send_sems, recv_sems, local_sem):
        x, y, c = lax.axis_index("x"), lax.axis_index("y"), lax.axis_index("c")
        me, sibling = (x, y, c), (x, y, 1 - c)
        chips = [(1 - x, y), (x, 1 - y), (1 - x, 1 - y)]

        def rows(px, py, pc):
            return out_ref.at[pl.ds((4 * px + 2 * py + pc) * m_per, m_per), :]

        def copy(k, block, to, src=None):
            return pltpu.make_async_remote_copy(
                src_ref=rows(*block) if src is None else src, dst_ref=rows(*block),
                send_sem=send_sems.at[k], recv_sem=recv_sems.at[k], device_id=to, device_id_type=MESH_ID)

        mine = pltpu.make_async_copy(x_ref, rows(*me), local_sem)
        mine.start()
        first = [copy(0, me, sibling, src=x_ref)]
        first += [copy(1 + j, me, (*chip, c), src=x_ref) for j, chip in enumerate(chips)]
        for cp in first:
            cp.start()
        passed = [copy(4 + j, (*chip, c), sibling) for j, chip in enumerate(chips)]
        for j, chip in enumerate(chips):
            copy(1 + j, (*chip, c), me).wait_recv()
            passed[j].start()
        copy(0, sibling, me).wait_recv()
        for j, chip in enumerate(chips):
            copy(4 + j, (*chip, 1 - c), me).wait_recv()
        for cp in first + passed:
            cp.wait_send()
        mine.wait()

    return _pcall(
        body, name=name, out_shape=jax.ShapeDtypeStruct((8 * m_per, n), v.dtype),
        in_specs=[pl.BlockSpec(memory_space=pltpu.VMEM)], out_specs=pl.BlockSpec(memory_space=pltpu.VMEM),
        scratch=[pltpu.SemaphoreType.DMA((7,)), pltpu.SemaphoreType.DMA((7,)), pltpu.SemaphoreType.DMA],
    )(v)


def _gather_ops(ins, outs, send_sems, recv_sems):
    n = len(ins)
    x, y, c = lax.axis_index("x"), lax.axis_index("y"), lax.axis_index("c")
    sibling = (x, y, 1 - c)
    chips = [(1 - x, y), (x, 1 - y), (1 - x, 1 - y)]

    def part(a, px, py, pc):
        half = ins[a].shape[0] // 2
        return outs[a].at[2 * px + py, pl.ds(pc * half, half), :]

    def copy(a, k, block, to, src=None):
        return pltpu.make_async_remote_copy(
            src_ref=part(a, *block) if src is None else src, dst_ref=part(a, *block),
            send_sem=send_sems.at[6 * a + k], recv_sem=recv_sems.at[6 * a + k], device_id=to,
            device_id_type=MESH_ID)

    def first(a, j):
        half = ins[a].shape[0] // 2
        return copy(a, j, (x, y, c), (*chips[j], c), src=ins[a].at[pl.ds(c * half, half), :])

    def start():
        for a in range(n):
            for j in range(3):
                first(a, j).start()

    def finish():
        for a in range(n):
            for j, chip in enumerate(chips):
                copy(a, j, (*chip, c), (x, y, c)).wait_recv()
                copy(a, 3 + j, (*chip, c), sibling).start()
        for a in range(n):
            for j, chip in enumerate(chips):
                copy(a, 3 + j, (*chip, 1 - c), (x, y, c)).wait_recv()
        for a in range(n):
            for j, chip in enumerate(chips):
                first(a, j).wait_send()
                copy(a, 3 + j, (*chip, c), sibling).wait_send()

    return start, finish


def _gather_out(shards):
    return tuple(jax.ShapeDtypeStruct((4,) + s_.shape, s_.dtype) for s_ in shards)


def _gather_sems(n):
    return [pltpu.SemaphoreType.DMA((6 * n,)), pltpu.SemaphoreType.DMA((6 * n,))]


def _weights_allgather(shards):
    n = len(shards)

    def body(*refs):
        start, finish = _gather_ops(refs[:n], refs[n:2 * n], *refs[2 * n:])
        start()
        finish()

    return _pcall(
        body, name="weights_allgather", out_shape=_gather_out(shards),
        in_specs=[pl.BlockSpec(memory_space=pl.ANY)] * n, out_specs=(pl.BlockSpec(memory_space=pl.ANY),) * n,
        scratch=_gather_sems(n),
    )(*shards)


def _pair_exchange(gs, name):
    n = len(gs)

    def body(*refs):
        ins, outs = refs[:n], refs[n:2 * n]
        send_sems, recv_sems = refs[2 * n:]
        x, y, c = lax.axis_index("x"), lax.axis_index("y"), lax.axis_index("c")
        cps = []
        for a in range(n):
            half = ins[a].shape[1] // 2
            cps.append(pltpu.make_async_remote_copy(
                src_ref=ins[a].at[:, pl.ds((1 - c) * half, half), :], dst_ref=outs[a], send_sem=send_sems.at[a],
                recv_sem=recv_sems.at[a], device_id=(x, y, 1 - c), device_id_type=MESH_ID))
        for cp in cps:
            cp.start()
        for cp in cps:
            cp.wait()

    return _pcall(
        body, name=name,
        out_shape=tuple(jax.ShapeDtypeStruct((g.shape[0], g.shape[1] // 2, g.shape[2]), g.dtype) for g in gs),
        in_specs=[pl.BlockSpec(memory_space=pl.ANY)] * n, out_specs=(pl.BlockSpec(memory_space=pl.ANY),) * n,
        scratch=[pltpu.SemaphoreType.DMA((n,)), pltpu.SemaphoreType.DMA((n,))],
    )(*gs)


def _exchange_ops(ins, outs, send_sems, recv_sems, dests):
    x, y, c = lax.axis_index("x"), lax.axis_index("y"), lax.axis_index("c")
    mine = 2 * x + y
    chips = [(1 - x, y), (x, 1 - y), (1 - x, 1 - y)]

    def each(fn):
        for a in range(len(ins)):
            lo, hi = dests[a]
            for j, (px, py) in enumerate(chips):
                q = 2 * px + py
                cp = pltpu.make_async_remote_copy(
                    src_ref=ins[a].at[jnp.clip(q - lo, 0, hi - lo - 1)], dst_ref=outs[a].at[j],
                    send_sem=send_sems.at[3 * a + j], recv_sem=recv_sems.at[3 * a + j], device_id=(px, py, c),
                    device_id_type=MESH_ID)
                fn(cp, (q >= lo) & (q < hi), (mine >= lo) & (mine < hi), (lo, hi) == (0, 4))

    def start():
        def go(cp, send_ok, recv_ok, always):
            if always:
                cp.start()
            else:
                pl.when(send_ok)(cp.start)
        each(go)

    def finish():
        def go(cp, send_ok, recv_ok, always):
            if always:
                cp.wait()
            else:
                pl.when(send_ok)(cp.wait_send)
                pl.when(recv_ok)(cp.wait_recv)
        each(go)

    return start, finish


def _comm_exchange(hs, dests):
    n = len(hs)
    return (list(hs), tuple(jax.ShapeDtypeStruct((3,) + h.shape[1:], h.dtype) for h in hs),
            [pltpu.SemaphoreType.DMA((3 * n,)), pltpu.SemaphoreType.DMA((3 * n,))],
            lambda i, o, s, r: _exchange_ops(i, o, s, r, dests))


def _comm_gather(shards):
    return (list(shards), _gather_out(shards), _gather_sems(len(shards)), _gather_ops)


def _carry(call, comm, steps):
    if comm is None:
        return call
    arrays, out_shape, sems, make = comm
    n, n_in, n_out = len(arrays), len(call["args"]), len(call["out_shape"])
    body = call["body"]

    def wrapped(*refs):
        base_in, cin = refs[:n_in], refs[n_in:n_in + n]
        rest = refs[n_in + n:]
        base_out, cout, scr = rest[:n_out], rest[n_out:n_out + n], rest[n_out + n:]
        start, finish = make(cin, cout, scr[-2], scr[-1])
        first, last = steps()
        pl.when(first)(start)
        body(*base_in, *base_out, *scr[:-2])
        pl.when(last)(finish)

    anyspec = pl.BlockSpec(memory_space=pl.ANY)
    return dict(call, body=wrapped, args=list(call["args"]) + arrays,
                in_specs=list(call["in_specs"]) + [anyspec] * n,
                out_shape=tuple(call["out_shape"]) + tuple(out_shape),
                out_specs=tuple(call["out_specs"]) + (anyspec,) * n,
                scratch=list(call["scratch"]) + sems)


def _run(call):
    args = call.pop("args")
    body = call.pop("body")
    return _pcall(body, **call)(*args)


def _pair_swap(rs):
    n = len(rs)

    def body(*refs):
        ins, outs = refs[:n], refs[n:2 * n]
        send_sems, recv_sems = refs[2 * n:]
        x, y, c = lax.axis_index("x"), lax.axis_index("y"), lax.axis_index("c")
        cps = [pltpu.make_async_remote_copy(
            src_ref=ins[a], dst_ref=outs[a], send_sem=send_sems.at[a], recv_sem=recv_sems.at[a],
            device_id=(x, y, 1 - c), device_id_type=MESH_ID) for a in range(n)]
        for cp in cps:
            cp.start()
        for cp in cps:
            cp.wait()

    return _pcall(
        body, name="grads_pair_swap", out_shape=tuple(jax.ShapeDtypeStruct(r.shape, r.dtype) for r in rs),
        in_specs=[pl.BlockSpec(memory_space=pl.ANY)] * n, out_specs=(pl.BlockSpec(memory_space=pl.ANY),) * n,
        scratch=[pltpu.SemaphoreType.DMA((n,)), pltpu.SemaphoreType.DMA((n,))],
    )(*rs)


SUM_STEPS = 4


def _pair_sum(gs, recvs, core, name):
    n = len(gs)

    def body(c_ref, *refs):
        for a in range(n):
            refs[2 * n + a][...] = (refs[a][...].astype(F32) + refs[n + a][...].astype(F32)).astype(refs[2 * n + a].dtype)

    blk = lambda g: (g.shape[0], g.shape[1] // (2 * SUM_STEPS), g.shape[2])
    return pl.pallas_call(
        body, name=name,
        out_shape=tuple(jax.ShapeDtypeStruct((g.shape[0], g.shape[1] // 2, g.shape[2]), g.dtype) for g in gs),
        grid_spec=pltpu.PrefetchScalarGridSpec(
            num_scalar_prefetch=1, grid=(SUM_STEPS,),
            in_specs=[pl.BlockSpec(blk(g), lambda i, cr: (0, cr[0] * SUM_STEPS + i, 0)) for g in gs]
            + [pl.BlockSpec(blk(g), lambda i, cr: (0, i, 0)) for g in gs],
            out_specs=tuple(pl.BlockSpec(blk(g), lambda i, cr: (0, i, 0)) for g in gs)),
        compiler_params=pltpu.CompilerParams(vmem_limit_bytes=40 << 20), interpret=_INTERPRET,
    )(core, *gs, *recvs)


def _chip_sum(hs, recvs, chip, dests, slots):
    n = len(hs)
    nout = max(slots) + 1
    first = [slots.index(o) for o in range(nout)]
    every = lambda d_: d_ == (0, 4)

    def own(d_):
        if every(d_):
            return lambda i, kr: (kr[0], i, 0)
        return lambda i, kr: (0, jnp.where(kr[0] == d_[0], i, 0), 0)

    def got(d_):
        if every(d_):
            return lambda i, kr: (0, i, 0)
        return lambda i, kr: (0, jnp.where(kr[0] == d_[0], i, 0), 0)

    def body(k_ref, *refs):
        for a in range(n):
            def emit(a=a):
                acc = refs[a][0].astype(F32)
                for j in range(3):
                    acc = acc + refs[n + a][j].astype(F32)
                refs[2 * n + slots[a]][...] = acc
            if every(dests[a]):
                emit()
            else:
                pl.when(k_ref[0] == dests[a][0])(emit)

    rb = lambda h: h.shape[1] // SUM_STEPS
    return pl.pallas_call(
        body, name="grads_chip_sum",
        out_shape=tuple(jax.ShapeDtypeStruct(hs[a].shape[1:], F32) for a in first),
        grid_spec=pltpu.PrefetchScalarGridSpec(
            num_scalar_prefetch=1, grid=(SUM_STEPS,),
            in_specs=[pl.BlockSpec((1, rb(h), h.shape[2]), own(d_)) for h, d_ in zip(hs, dests)]
            + [pl.BlockSpec((3, rb(h), h.shape[2]), got(d_)) for h, d_ in zip(hs, dests)],
            out_specs=tuple(pl.BlockSpec((rb(hs[a]), hs[a].shape[2]), lambda i, kr: (i, 0)) for a in first)),
        compiler_params=pltpu.CompilerParams(vmem_limit_bytes=40 << 20), interpret=_INTERPRET,
    )(chip, *hs, *recvs)


def _ada_fwd(araw, w, b):
    nblk = w.shape[1] // 512

    def body(a_ref, w_ref, b_ref, o_ref):
        o_ref[...] = _nn(_silu(a_ref[...]), w_ref[...]) + b_ref[...]

    return _pcall(
        body, name="ada_fwd", out_shape=jax.ShapeDtypeStruct((16, w.shape[1]), F32), grid=(nblk,),
        in_specs=[_full((16, D)), pl.BlockSpec((D, 512), lambda j: (0, j)), pl.BlockSpec((1, 512), lambda j: (0, j))],
        out_specs=pl.BlockSpec((16, 512), lambda j: (0, j)), sem=("parallel",),
    )(araw, w, b)


def _ada_bwd(araw, dmod, w):
    nblk = w.shape[1] // 512

    def body(a_ref, d_ref, w_ref, gw_ref, da_ref):
        j = pl.program_id(0)
        gw_ref[...] = _tn(_silu(a_ref[...]), d_ref[...])
        part = _nt(d_ref[...], w_ref[...])

        @pl.when(j == 0)
        def _():
            da_ref[...] = part

        @pl.when(j > 0)
        def _():
            da_ref[...] += part

    return _pcall(
        body, name="ada_bwd",
        out_shape=(jax.ShapeDtypeStruct(w.shape, F32), jax.ShapeDtypeStruct((16, D), F32)), grid=(nblk,),
        in_specs=[_full((16, D)), pl.BlockSpec((16, 512), lambda j: (0, j)), pl.BlockSpec((D, 512), lambda j: (0, j))],
        out_specs=(pl.BlockSpec((D, 512), lambda j: (0, j)), _full((16, D))), sem=("arbitrary",),
    )(araw, dmod, w)


def _w_specs():
    return [pl.BlockSpec((None, D, D), lambda j, i: (j // 2, j % 2, 0)),
            pl.BlockSpec((None, WTAIL, D), lambda j, i: (jnp.maximum(j // 2 - 1, 0), 0, 0)),
            pl.BlockSpec((None, WTAIL, D), lambda j, i: (3, 0, 0))]


def _inproj(xin, mods, wi_main, wi_tail, t_total, tb, blk_off, prev, name, comm=None):
    n = xin.shape[0]
    nt = n // tb
    ncol = 8

    def body(x_ref, mod_ref, w_ref, wb_ref, wdt_ref, *rest):
        p_ref, pdt_ref, u_ref, uscr = rest[-4:]
        j, i = pl.program_id(0), pl.program_id(1)
        rows = pl.ds(pl.multiple_of(i * tb, tb), tb)

        @pl.when(j == 0)
        def _():
            xv = x_ref[...]
            r = lax.rsqrt(jnp.mean(xv * xv, axis=1, keepdims=True) + EPS)
            u = (xv * r * mod_ref[2:3, :]) * mod_ref[0:1, :] + mod_ref[1:2, :]
            ub = u.astype(MXU_DTYPE)
            uscr[rows, :] = ub
            u_ref[...] = ub
            pdt_ref[...] = _nt(ub, wdt_ref[...])

        ub = uscr[rows, :]
        pv = _nt(ub, w_ref[...])

        @pl.when((j % 2 == 1) | (j == 0))
        def _():
            p_ref[...] = pv.astype(p_ref.dtype)

        @pl.when((j % 2 == 0) & (j > 0))
        def _():
            head = pv[:, 0:WTAIL] + _nt(ub, wb_ref[...])
            p_ref[...] = jnp.concatenate([head, pv[:, WTAIL:]], axis=1).astype(p_ref.dtype)

    once = lambda j, i: (jnp.where(j == 0, i, nt - 1) + blk_off, 0)
    in_specs = [pl.BlockSpec((tb, D), lambda j, i: (jnp.where(j == 0, i, nt - 1), 0)), _full((8, D))] + _w_specs()
    args = [xin, mods, wi_main, wi_tail, wi_tail]
    aliases = None
    if prev is not None:
        in_specs += [pl.BlockSpec(memory_space=pl.ANY)] * 3
        args += list(prev)
        aliases = {5: 0, 6: 1, 7: 2}
    call = dict(
        body=body, args=args, name=name,
        out_shape=(jax.ShapeDtypeStruct((t_total, ncol * D), MXU_DTYPE), jax.ShapeDtypeStruct((t_total, 128), F32),
                   jax.ShapeDtypeStruct((t_total, D), MXU_DTYPE)),
        grid=(ncol, nt), in_specs=in_specs,
        out_specs=(pl.BlockSpec((tb, D), lambda j, i: (i + blk_off, j)), pl.BlockSpec((tb, 128), once),
                   pl.BlockSpec((tb, D), once)),
        scratch=[pltpu.VMEM((n, D), MXU_DTYPE)], sem=("arbitrary", "arbitrary"), vmem_mb=48, aliases=aliases)
    steps = lambda: ((pl.program_id(0) == 0) & (pl.program_id(1) == 0),
                     (pl.program_id(0) == ncol - 1) & (pl.program_id(1) == nt - 1))
    return _run(_carry(call, comm, steps))


def _blk(s, nb, rev):
    return jnp.where(s == 0, nb - 1, (nb - 1 - s) if rev else (s - 1))


def _hgrn_gate(fr, lbraw_ref, d):
    lb = _sig(lbraw_ref[d:d + 1, :] - lbraw_ref[2 + d:3 + d, :])
    sg = _sig(fr)
    return lb, sg, lb + (1.0 - lb) * sg


def _hgrn_fwd(p_main, lbraw, d, nb, comm=None):
    t_total = p_main.shape[0]
    rev = d == 1
    nch = TB // HC
    scale = HF ** -0.5

    def body(q_ref, f_ref, v_ref, lb_ref, o_ref, sp_ref, st):
        s = pl.program_id(0)

        @pl.when(s == 0)
        def _():
            st[...] = jnp.zeros_like(st)

        mb = _tri(HC, rev)
        m01 = _b01(mb)
        order = list(reversed(range(nch)) if rev else range(nch))
        hs_ = [slice(h * HF, (h + 1) * HF) for h in range(NH)]
        pre = {}
        for c in order:
            rows = slice(c * HC, (c + 1) * HC)
            _, _, f = _hgrn_gate(f_ref[rows, :].astype(F32), lb_ref, d)
            k = 1.0 - f
            cum = _dot01(m01, jnp.log(f))
            tot = cum[0:1, :] if rev else cum[HC - 1:HC, :]
            qd = _silu(q_ref[rows, :].astype(F32)) * scale * jnp.exp(cum)
            ki = k * jnp.exp(-cum)
            etot = jnp.exp(tot)
            pre[c] = (_mx(qd), _mx(ki), _mx(ki * etot), _mx(v_ref[rows, :]), etot)
        scs = {c: [_nt(pre[c][0][:, cs], pre[c][1][:, cs]) for cs in hs_] for c in order}
        upd = {c: [_tn(pre[c][3][:, cs], pre[c][2][:, cs]) for cs in hs_] for c in order}
        intra = {c: [_nn(jnp.where(mb, scs[c][h], 0.0), pre[c][3][:, cs]) for h, cs in enumerate(hs_)] for c in order}
        for c in order:
            rows = slice(c * HC, (c + 1) * HC)
            qdb, etot = pre[c][0], pre[c][4]
            for h, cs in enumerate(hs_):
                sth = st[h]
                stb = sth.astype(sp_ref.dtype)
                sp_ref[c, h] = stb
                o_ref[rows, cs] = intra[c][h] + _nt(qdb[:, cs], stb)
                st[h] = sth * etot[:, cs] + upd[c][h]

    col = lambda j: (lambda s: (_blk(s, nb, rev), j))
    call = dict(
        body=body, args=[p_main, p_main, p_main, lbraw], name=f"hgrn_fwd_{d}",
        out_shape=(jax.ShapeDtypeStruct((t_total, D), F32),
                   jax.ShapeDtypeStruct((nch * nb, NH, HF, HF), MXU_DTYPE)),
        grid=(nb,),
        in_specs=[pl.BlockSpec((TB, D), col(0)), pl.BlockSpec((TB, D), col(1 + d)), pl.BlockSpec((TB, D), col(3)),
                  _full((8, D))],
        out_specs=(pl.BlockSpec((TB, D), col(0)),
                   pl.BlockSpec((nch, NH, HF, HF), lambda s: (_blk(s, nb, rev), 0, 0, 0))),
        scratch=[pltpu.VMEM((NH, HF, HF), F32)], sem=("arbitrary",), vmem_mb=40)
    return _run(_carry(call, comm, lambda: (pl.program_id(0) == 0, pl.program_id(0) == nb - 1)))


def _hgrn_bwd(p_main, lbraw, sprev, do, d, nb, prev, comm=None):
    t_total = p_main.shape[0]
    rev = d == 1
    nch = TB // HC
    scale = HF ** -0.5
    last = prev is not None
    odt = MXU_DTYPE if last else F32

    def body(q_ref, f_ref, v_ref, lb_ref, sp_ref, do_ref, *rest):
        if last:
            dqp_ref, dvp_ref = rest[:2]
            rest = rest[2:]
        dq_ref, df_ref, dv_ref, dlb_ref, dst = rest
        sp_id = pl.program_id(0)
        is_ctx = sp_id == nb - 1

        @pl.when(sp_id == 0)
        def _():
            dst[...] = jnp.zeros_like(dst)
            dlb_ref[...] = jnp.zeros_like(dlb_ref)

        mb = _tri(HC, rev)
        mbt = _tri(HC, not rev)
        m01 = _b01(mb)
        mt01 = _b01(mbt)
        order = list(range(nch) if rev else reversed(range(nch)))
        hs_ = [slice(h * HF, (h + 1) * HF) for h in range(NH)]
        pre = {}
        for c in order:
            rows = slice(c * HC, (c + 1) * HC)
            lb, sg, f = _hgrn_gate(f_ref[rows, :].astype(F32), lb_ref, d)
            k = 1.0 - f
            cum = _dot01(m01, jnp.log(f))
            tot = cum[0:1, :] if rev else cum[HC - 1:HC, :]
            e = jnp.exp(cum)
            ei = jnp.exp(-cum)
            etot = jnp.exp(tot)
            ee = ei * etot
            qraw = q_ref[rows, :].astype(F32)
            qd = _silu(qraw) * scale * e
            ki = k * ei
            ke = k * ee
            dov = jnp.where(is_ctx, 0.0, do_ref[rows, :])
            pre[c] = dict(lb=lb, sg=sg, f=f, e=e, ei=ei, ee=ee, etot=etot, qraw=qraw, qd=qd, ki=ki, ke=ke,
                          qdb=_mx(qd), kib=_mx(ki), keb=_mx(ke), vb=_mx(v_ref[rows, :]), dob=_mx(dov))
        units = [(c, h) for c in order for h in range(NH)]
        col = lambda u, key: pre[u[0]][key][:, hs_[u[1]]]
        pt = {u: jnp.where(mbt, _nt(col(u, "kib"), col(u, "qdb")), 0.0) for u in units}
        dp = {u: jnp.where(mb, _nt(col(u, "dob"), col(u, "vb")), 0.0) for u in units}
        dpt = {u: jnp.where(mbt, _nt(col(u, "vb"), col(u, "dob")), 0.0) for u in units}
        dv_i = {u: _nn(pt[u], col(u, "dob")) for u in units}
        dqd_ = {u: _nn(dp[u], col(u, "kib")) + _nn(col(u, "dob"), sp_ref[u[0], u[1]]) for u in units}
        dki_ = {u: _nn(dpt[u], col(u, "qdb")) for u in units}
        dsl = {u: _tn(col(u, "dob"), col(u, "qdb")) for u in units}
        for c in order:
            rows = slice(c * HC, (c + 1) * HC)
            p = pre[c]
            dv_l, dke_l, dtot_l = [], [], []
            for h, cs in enumerate(hs_):
                dso = dst[h]
                dsob = _mx(dso)
                dv_l.append(dv_i[(c, h)] + _nt(p["keb"][:, cs], dsob))
                dke_l.append(_nn(p["vb"][:, cs], dsob))
                dtot_l.append(_colsum(dso * sp_ref[c, h].astype(F32)) * p["etot"][:, cs])
                dst[h] = dso * p["etot"][:, cs] + dsl[(c, h)]
            lb, sg, f, e, ei, ee, qraw, qd, ki, ke = (p[n_] for n_ in ("lb", "sg", "f", "e", "ei", "ee", "qraw", "qd",
                                                                     "ki", "ke"))
            dqd = jnp.concatenate([dqd_[(c, h)] for h in range(NH)], axis=1)
            dki = jnp.concatenate([dki_[(c, h)] for h in range(NH)], axis=1)
            dke = jnp.concatenate(dke_l, axis=1)
            dcum = dqd * qd - dki * ki - dke * ke
            dtot = jnp.concatenate(dtot_l, axis=1) + _colsum(dke * ke)
            dk = dki * ei + dke * ee
            dlf = _dot01(mt01, dcum) + dtot
            df = dlf / f - dk
            dlb_ref[0:1, :] += _colsum(df * (1.0 - sg))
            dfr = df * (1.0 - lb) * sg * (1.0 - sg)
            dq = dqd * e * scale * _dsilu(qraw)
            dv = jnp.concatenate(dv_l, axis=1)
            if last:
                dq = dq + dqp_ref[rows, :]
                dv = dv + dvp_ref[rows, :]
            dq_ref[rows, :] = dq.astype(odt)
            dv_ref[rows, :] = dv.astype(odt)
            df_ref[rows, :] = dfr.astype(MXU_DTYPE)

    blk = lambda s: _blk(nb - 1 - s, nb, rev)
    col = lambda j: (lambda s: (blk(s), j))
    in_specs = [pl.BlockSpec((TB, D), col(0)), pl.BlockSpec((TB, D), col(1 + d)), pl.BlockSpec((TB, D), col(3)),
                _full((8, D)), pl.BlockSpec((nch, NH, HF, HF), lambda s: (blk(s), 0, 0, 0)),
                pl.BlockSpec((TB, D), lambda s: (jnp.minimum(blk(s), nb - 2), 0))]
    args = [p_main, p_main, p_main, lbraw, sprev, do]
    if last:
        in_specs += [pl.BlockSpec((TB, D), col(0))] * 2
        args += list(prev)
    call = dict(
        body=body, args=args, name=f"hgrn_bwd_{d}",
        out_shape=(jax.ShapeDtypeStruct((t_total, D), odt), jax.ShapeDtypeStruct((t_total, D), MXU_DTYPE),
                   jax.ShapeDtypeStruct((t_total, D), odt), jax.ShapeDtypeStruct((8, D), F32)),
        grid=(nb,), in_specs=in_specs,
        out_specs=(pl.BlockSpec((TB, D), col(0)), pl.BlockSpec((TB, D), col(0)), pl.BlockSpec((TB, D), col(0)),
                   _full((8, D))),
        scratch=[pltpu.VMEM((NH, HF, HF), F32)], sem=("arbitrary",), vmem_mb=48)
    return _run(_carry(call, comm, lambda: (pl.program_id(0) == 0, pl.program_id(0) == nb - 1)))


def _conv_masks(tb, is_ctx):
    seg = jnp.where(is_ctx, tb, GRID_W)
    pos = lax.broadcasted_iota(jnp.int32, (tb, 1), 0) & (seg - 1)
    return pos, seg


def _shift_rows(x, dshift, pos, seg):
    if dshift == 0:
        return x
    n = x.shape[0]
    rolled = pltpu.roll(x, (-dshift) % n, 0)
    ok = (pos + dshift >= 0) & (pos + dshift < seg)
    return jnp.where(ok, rolled, 0.0)


def _ssd_prep(p_main, p_dt, convp, dtb, nb):
    t_total = p_main.shape[0]

    def body(x_ref, dt_ref, cw_ref, dtb_ref, xa_ref, dts_ref):
        is_ctx = pl.program_id(0) == nb - 1
        pos, seg = _conv_masks(TB, is_ctx)
        xv = x_ref[...].astype(F32)
        acc = cw_ref[5:6, :] + cw_ref[2:3, :] * xv
        for kk in (0, 1, 3, 4):
            acc = acc + cw_ref[kk:kk + 1, :] * _shift_rows(xv, kk - 2, pos, seg)
        xa_ref[...] = _silu(acc)
        dts_ref[...] = _softplus(dt_ref[...] + dtb_ref[0:1, :])

    return _pcall(
        body, name="ssd_prep",
        out_shape=(jax.ShapeDtypeStruct((t_total, 2048), F32), jax.ShapeDtypeStruct((t_total, 128), F32)),
        grid=(nb,),
        in_specs=[pl.BlockSpec((TB, 2048), lambda i: (i, 3)), pl.BlockSpec((TB, 128), lambda i: (i, 0)),
                  _full((8, 2048)), _full((8, 128))],
        out_specs=(pl.BlockSpec((TB, 2048), lambda i: (i, 0)), pl.BlockSpec((TB, 128), lambda i: (i, 0))),
        sem=("parallel",), vmem_mb=32,
    )(p_main, p_dt, convp, dtb)


def _ssd_prep_bwd(p_main, p_dt, convp, dtb, dxa, dxs_skip, ddts, nb):
    t_total = p_main.shape[0]

    def body(x_ref, dt_ref, cw_ref, dtb_ref, dxa_ref, dsk_ref, ddts_ref, dx_ref, ddt_ref, dcw_ref, ddtb_ref):
        i = pl.program_id(0)
        is_ctx = i == nb - 1

        @pl.when(i == 0)
        def _():
            dcw_ref[...] = jnp.zeros_like(dcw_ref)
            ddtb_ref[...] = jnp.zeros_like(ddtb_ref)

        pos, seg = _conv_masks(TB, is_ctx)
        xv = x_ref[...].astype(F32)
        sh = {kk: _shift_rows(xv, kk - 2, pos, seg) for kk in range(KCONV)}
        acc = cw_ref[5:6, :]
        for kk in range(KCONV):
            acc = acc + cw_ref[kk:kk + 1, :] * sh[kk]
        dact = dxa_ref[...]
        dact = jnp.concatenate([dact[:, :D] + jnp.where(is_ctx, 0.0, dsk_ref[...]), dact[:, D:]], axis=1)
        dpre = dact * _dsilu(acc)
        dxv = cw_ref[2:3, :] * dpre
        for kk in (0, 1, 3, 4):
            dxv = dxv + cw_ref[kk:kk + 1, :] * _shift_rows(dpre, 2 - kk, pos, seg)
        dx_ref[...] = dxv.astype(dx_ref.dtype)
        for kk in range(KCONV):
            dcw_ref[kk:kk + 1, :] += _colsum(dpre * sh[kk])
        dcw_ref[5:6, :] += _colsum(dpre)
        draw = ddts_ref[...] * _sig(dt_ref[...] + dtb_ref[0:1, :])
        ddt_ref[...] = draw.astype(ddt_ref.dtype)
        ddtb_ref[0:1, :] += _colsum(draw)

    return _pcall(
        body, name="ssd_prep_bwd",
        out_shape=(jax.ShapeDtypeStruct((t_total, 2048), MXU_DTYPE), jax.ShapeDtypeStruct((t_total, 128), MXU_DTYPE),
                   jax.ShapeDtypeStruct((8, 2048), F32), jax.ShapeDtypeStruct((8, 128), F32)),
        grid=(nb,),
        in_specs=[pl.BlockSpec((TB, 2048), lambda i: (i, 3)), pl.BlockSpec((TB, 128), lambda i: (i, 0)),
                  _full((8, 2048)), _full((8, 128)), pl.BlockSpec((TB, 2048), lambda i: (i, 0)),
                  pl.BlockSpec((TB, D), lambda i: (jnp.minimum(i, nb - 2), 0)),
                  pl.BlockSpec((TB, 128), lambda i: (i, 0))],
        out_specs=(pl.BlockSpec((TB, 2048), lambda i: (i, 0)), pl.BlockSpec((TB, 128), lambda i: (i, 0)),
                   _full((8, 2048)), _full((8, 128))),
        sem=("arbitrary",), vmem_mb=40,
    )(p_main, p_dt, convp, dtb, dxa, dxs_skip, ddts)


def _lane_pick(x, lane, col):
    return _rowsum(jnp.where(lane == col, x, 0.0))


def _ssd_chunk_common(dts, alog_ref, m01, rev):
    lane = lax.broadcasted_iota(jnp.int32, (1, 128), 1)
    arow = -jnp.exp(alog_ref[0:1, :])
    cum = _dot01(m01, dts * arow)
    tot = cum[0:1, :] if rev else cum[SC - 1:SC, :]
    return lane, arow, cum, cum.T, tot


def _ssd_fwd(xa, dts, alog, d, nb):
    t_total = xa.shape[0]
    rev = d == 1
    nch = TB // SC
    npair = SHEADS // 2

    def body(xa_ref, dts_ref, alog_ref, y_ref, sp_ref, st):
        s = pl.program_id(0)

        @pl.when(s == 0)
        def _():
            st[...] = jnp.zeros_like(st)

        mb = _tri(SC, rev)
        m01 = _b01(mb)
        lo = lax.broadcasted_iota(jnp.int32, (1, 128), 1) < SP
        rlo = lax.broadcasted_iota(jnp.int32, (128, 1), 0) < SP
        order = list(reversed(range(nch)) if rev else range(nch))
        pre = {}
        for c in order:
            rows = slice(c * SC, (c + 1) * SC)
            dts_c = dts_ref[rows, :]
            lane, arow, cum, cumt, tot = _ssd_chunk_common(dts_c, alog_ref, m01, rev)
            bgs = [_mx(xa_ref[rows, D + g * SN:D + (g + 1) * SN]) for g in range(4)]
            cgs = [_mx(xa_ref[rows, D + 512 + g * SN:D + 512 + (g + 1) * SN]) for g in range(4)]
            pairs = []
            for pr in range(npair):
                xs = xa_ref[rows, pr * 128:(pr + 1) * 128]
                cols = [16 * d + 2 * pr, 16 * d + 2 * pr + 1]
                cum_c = [_lane_pick(cum, lane, q) for q in cols]
                dt_c = [_lane_pick(dts_c, lane, q) for q in cols]
                tot_c = [_lane_pick(tot, lane, q) for q in cols]
                dtx = xs * jnp.where(lo, dt_c[0], dt_c[1])
                e1_pair = jnp.where(lo, jnp.exp(cum_c[0]), jnp.exp(cum_c[1]))
                e2_pair = jnp.where(lo, jnp.exp(tot_c[0] - cum_c[0]), jnp.exp(tot_c[1] - cum_c[1]))
                etot_col = jnp.where(rlo, jnp.exp(tot_c[0]), jnp.exp(tot_c[1]))
                decs = [jnp.where(mb, jnp.exp(cum_c[q] - cumt[cols[q]:cols[q] + 1, :]), 0.0) for q in range(2)]
                dtxq = [_mx(jnp.where(lo if q == 0 else ~lo, dtx, 0.0)) for q in range(2)]
                pairs.append(dict(e1=e1_pair, etot=etot_col, decs=decs, dtxq=dtxq, xe=_mx(dtx * e2_pair)))
            pre[c] = (bgs, cgs, pairs)
        gm = {(c, g): _nt(pre[c][1][g], pre[c][0][g]) for c in order for g in range(4)}
        upd = {(c, pr): _tn(pre[c][2][pr]["xe"], pre[c][0][pr // 2]) for c in order for pr in range(npair)}
        intra = {(c, pr): sum(_nn(gm[(c, pr // 2)] * pre[c][2][pr]["decs"][q], pre[c][2][pr]["dtxq"][q]) for q in range(2))
                 for c in order for pr in range(npair)}
        for c in order:
            rows = slice(c * SC, (c + 1) * SC)
            bgs, cgs, pairs = pre[c]
            for pr in range(npair):
                stp = st[pr]
                stb = stp.astype(sp_ref.dtype)
                sp_ref[c, pr] = stb
                y_ref[rows, pr * 128:(pr + 1) * 128] = intra[(c, pr)] + pairs[pr]["e1"] * _nt(cgs[pr // 2], stb)
                st[pr] = stp * pairs[pr]["etot"] + upd[(c, pr)]

    blk = lambda s: _blk(s, nb, rev)
    return _pcall(
        body, name=f"ssd_fwd_{d}",
        out_shape=(jax.ShapeDtypeStruct((t_total, D), F32),
                   jax.ShapeDtypeStruct((nch * nb, npair, 128, SN), MXU_DTYPE)),
        grid=(nb,),
        in_specs=[pl.BlockSpec((TB, 2048), lambda s: (blk(s), 0)), pl.BlockSpec((TB, 128), lambda s: (blk(s), 0)),
                  _full((8, 128))],
        out_specs=(pl.BlockSpec((TB, D), lambda s: (blk(s), 0)),
                   pl.BlockSpec((nch, npair, 128, SN), lambda s: (blk(s), 0, 0, 0))),
        scratch=[pltpu.VMEM((npair, 128, SN), F32)], sem=("arbitrary",), vmem_mb=40,
    )(xa, dts, alog)


def _ssd_bwd(xa, dts, alog, sprev, dy, d, nb, prev, comm=None):
    t_total = xa.shape[0]
    rev = d == 1
    nch = TB // SC
    npair = SHEADS // 2
    last = prev is not None

    def body(xa_ref, dts_ref, alog_ref, sp_ref, dy_ref, *rest):
        if last:
            dxp_ref, ddp_ref = rest[:2]
            rest = rest[2:]
        dxa_ref, ddts_ref, da_ref, dst = rest
        sp_id = pl.program_id(0)
        is_ctx = sp_id == nb - 1

        @pl.when(sp_id == 0)
        def _():
            dst[...] = jnp.zeros_like(dst)
            da_ref[...] = jnp.zeros_like(da_ref)

        mb = _tri(SC, rev)
        m01 = _b01(mb)
        mt01 = _b01(_tri(SC, not rev))
        lo = lax.broadcasted_iota(jnp.int32, (1, 128), 1) < SP
        rlo = lax.broadcasted_iota(jnp.int32, (128, 1), 0) < SP
        order = list(range(nch) if rev else reversed(range(nch)))
        pre = {}
        for c in order:
            rows = slice(c * SC, (c + 1) * SC)
            dts_c = dts_ref[rows, :]
            lane, arow, cum, cumt, tot = _ssd_chunk_common(dts_c, alog_ref, m01, rev)
            pairs = []
            for pr in range(npair):
                xs = xa_ref[rows, pr * 128:(pr + 1) * 128]
                dyp = jnp.where(is_ctx, 0.0, dy_ref[rows, pr * 128:(pr + 1) * 128])
                cols = [16 * d + 2 * pr, 16 * d + 2 * pr + 1]
                cum_c = [_lane_pick(cum, lane, q) for q in cols]
                dt_c = [_lane_pick(dts_c, lane, q) for q in cols]
                tot_c = [_lane_pick(tot, lane, q) for q in cols]
                e1_c = [jnp.exp(cum_c[q]) for q in range(2)]
                e2_c = [jnp.exp(tot_c[q] - cum_c[q]) for q in range(2)]
                etot_c = [jnp.exp(tot_c[q]) for q in range(2)]
                dt_pair = jnp.where(lo, dt_c[0], dt_c[1])
                e1_pair = jnp.where(lo, e1_c[0], e1_c[1])
                e2_pair = jnp.where(lo, e2_c[0], e2_c[1])
                dtx = xs * dt_pair
                decs = [jnp.where(mb, jnp.exp(cum_c[q] - cumt[cols[q]:cols[q] + 1, :]), 0.0) for q in range(2)]
                dyq = [_mx(jnp.where(lo if q == 0 else ~lo, dyp, 0.0)) for q in range(2)]
                pairs.append(dict(xs=xs, dyp=dyp, cols=cols, e1_c=e1_c, e2_c=e2_c, etot_c=etot_c, dt_pair=dt_pair,
                                  e2_pair=e2_pair, etot_col=jnp.where(rlo, etot_c[0], etot_c[1]), dtx=dtx,
                                  dtxb=_mx(dtx), xeb=_mx(dtx * e2_pair), dy0b=_mx(dyp * e1_pair), decs=decs, dyq=dyq))
            pre[c] = dict(lane=lane, arow=arow, dts=dts_c, pairs=pairs,
                          bgb=[_mx(xa_ref[rows, D + g * SN:D + (g + 1) * SN]) for g in range(4)],
                          cgb=[_mx(xa_ref[rows, D + 512 + g * SN:D + 512 + (g + 1) * SN]) for g in range(4)])
        units = [(c, pr) for c in order for pr in range(npair)]
        P = lambda u: pre[u[0]]["pairs"][u[1]]
        cgu = lambda u: pre[u[0]]["cgb"][u[1] // 2]
        gm = {(c, g): _nt(pre[c]["cgb"][g], pre[c]["bgb"][g]) for c in order for g in range(4)}
        y0 = {u: _nt(cgu(u), sp_ref[u[0], u[1]]) for u in units}
        dcg_i = {u: _nn(P(u)["dy0b"], sp_ref[u[0], u[1]]) for u in units}
        dsl = {u: _tn(P(u)["dy0b"], cgu(u)) for u in units}
        w_ = {(u, q): gm[(u[0], u[1] // 2)] * P(u)["decs"][q] for u in units for q in range(2)}
        dw_ = {(u, q): jnp.where(mb, _nt(P(u)["dyq"][q], P(u)["dtxb"]), 0.0) for u in units for q in range(2)}
        ddtx_i = {(u, q): _tn(w_[(u, q)], P(u)["dyq"][q]) for u in units for q in range(2)}
        for c in order:
            rows = slice(c * SC, (c + 1) * SC)
            pc = pre[c]
            lane, arow, dts_c = pc["lane"], pc["arow"], pc["dts"]
            dcum = jnp.zeros((SC, 128), F32)
            ddt = jnp.zeros((SC, 128), F32)
            dtot = jnp.zeros((1, 128), F32)
            dgm = [jnp.zeros((SC, SC), F32) for _ in range(4)]
            dbg = [jnp.zeros((SC, SN), F32) for _ in range(4)]
            dcg = [jnp.zeros((SC, SN), F32) for _ in range(4)]
            for pr in range(npair):
                u, g, p = (c, pr), pr // 2, pc["pairs"][pr]
                dso = dst[pr]
                dsob = _mx(dso)
                dxe = _nt(pc["bgb"][g], dsob)
                dbg[g] = dbg[g] + _nn(p["xeb"], dsob)
                ddtx = dxe * p["e2_pair"]
                de2 = dxe * p["dtx"]
                dcg[g] = dcg[g] + dcg_i[u]
                de1 = p["dyp"] * y0[u]
                rsum = _rowsum(dso * sp_ref[c, pr].astype(F32))
                dst[pr] = dso * p["etot_col"] + dsl[u]
                for q in range(2):
                    hm = lo if q == 0 else ~lo
                    col = p["cols"][q]
                    dw = dw_[(u, q)]
                    ddtx = ddtx + jnp.where(hm, ddtx_i[(u, q)], 0.0)
                    dgm[g] = dgm[g] + dw * p["decs"][q]
                    z = dw * w_[(u, q)]
                    de1q = _rowsum(jnp.where(hm, de1, 0.0))
                    de2q = _rowsum(jnp.where(hm, de2, 0.0))
                    dcum_q = _rowsum(z) - _rowsum(z.T) + de1q * p["e1_c"][q] - de2q * p["e2_c"][q]
                    rs = rsum[0:SP, :] if q == 0 else rsum[SP:2 * SP, :]
                    dtot_q = _colsum(de2q * p["e2_c"][q]) + _colsum(rs) * p["etot_c"][q]
                    dcum = jnp.where(lane == col, dcum_q, dcum)
                    dtot = jnp.where(lane == col, dtot_q, dtot)
                dxs = ddtx * p["dt_pair"]
                ddt_pair = ddtx * p["xs"]
                for q in range(2):
                    hm = lo if q == 0 else ~lo
                    ddt = jnp.where(lane == p["cols"][q], _rowsum(jnp.where(hm, ddt_pair, 0.0)), ddt)
                if last:
                    dxs = dxs + dxp_ref[rows, pr * 128:(pr + 1) * 128]
                dxa_ref[rows, pr * 128:(pr + 1) * 128] = dxs
            for g in range(4):
                db = dbg[g] + _tn(dgm[g], pc["cgb"][g])
                dc = dcg[g] + _nn(dgm[g], pc["bgb"][g])
                if last:
                    db = db + dxp_ref[rows, D + g * SN:D + (g + 1) * SN]
                    dc = dc + dxp_ref[rows, D + 512 + g * SN:D + 512 + (g + 1) * SN]
                dxa_ref[rows, D + g * SN:D + (g + 1) * SN] = db
                dxa_ref[rows, D + 512 + g * SN:D + 512 + (g + 1) * SN] = dc
            dla = _dot01(mt01, dcum) + dtot
            ddt = ddt + dla * arow
            da_ref[0:1, :] += _colsum(dla * dts_c)
            if last:
                ddt = ddt + ddp_ref[rows, :]
            ddts_ref[rows, :] = ddt

    blk = lambda s: _blk(nb - 1 - s, nb, rev)
    in_specs = [pl.BlockSpec((TB, 2048), lambda s: (blk(s), 0)), pl.BlockSpec((TB, 128), lambda s: (blk(s), 0)),
                _full((8, 128)), pl.BlockSpec((nch, npair, 128, SN), lambda s: (blk(s), 0, 0, 0)),
                pl.BlockSpec((TB, D), lambda s: (jnp.minimum(blk(s), nb - 2), 0))]
    args = [xa, dts, alog, sprev, dy]
    if last:
        in_specs += [pl.BlockSpec((TB, 2048), lambda s: (blk(s), 0)), pl.BlockSpec((TB, 128), lambda s: (blk(s), 0))]
        args += list(prev)
    call = dict(
        body=body, args=args, name=f"ssd_bwd_{d}",
        out_shape=(jax.ShapeDtypeStruct((t_total, 2048), F32), jax.ShapeDtypeStruct((t_total, 128), F32),
                   jax.ShapeDtypeStruct((8, 128), F32)),
        grid=(nb,), in_specs=in_specs,
        out_specs=(pl.BlockSpec((TB, 2048), lambda s: (blk(s), 0)), pl.BlockSpec((TB, 128), lambda s: (blk(s), 0)),
                   _full((8, 128))),
        scratch=[pltpu.VMEM((npair, 128, SN), F32)], sem=("arbitrary",), vmem_mb=48)
    return _run(_carry(call, comm, lambda: (pl.program_id(0) == 0, pl.program_id(0) == nb - 1)))


def _readout(o, g, yy, z, vec_ref):
    hg, ss, keep = [], [], []
    for h in range(NH):
        cs = slice(h * HF, (h + 1) * HF)
        oh = o[:, cs]
        r = lax.rsqrt(jnp.mean(oh * oh, axis=1, keepdims=True) + EPS)
        hg.append(oh * r * vec_ref[0:1, cs] * _silu(g[:, cs]))
        keep.append(r)
    u = yy * _silu(z)
    for gi in range(4):
        cs = slice(gi * 256, (gi + 1) * 256)
        ug = u[:, cs]
        r = lax.rsqrt(jnp.mean(ug * ug, axis=1, keepdims=True) + EPS)
        ss.append(ug * r * vec_ref[2:3, cs])
        keep.append(r)
    return jnp.concatenate(hg, axis=1), jnp.concatenate(ss, axis=1), keep, u


def _mix_out(o_f, o_b, p_main, y_f, y_b, xa, x, vecs, w_out):
    n = x.shape[0]

    def body(of_ref, ob_ref, g_ref, z_ref, yf_ref, yb_ref, xs_ref, x_ref, vec_ref, w_ref,
             ymix_ref, ylat_ref, h1_ref, u2_ref):
        o = of_ref[...] + ob_ref[...]
        yy = yf_ref[...] + yb_ref[...] + vec_ref[1:2, :] * xs_ref[...]
        hg, ss, _, _ = _readout(o, g_ref[...].astype(F32), yy, z_ref[...].astype(F32), vec_ref)
        ymix = jnp.concatenate([hg, ss], axis=1).astype(MXU_DTYPE)
        ymix_ref[...] = ymix
        ylat = _nn(ymix, w_ref[...])
        ylat_ref[...] = ylat
        h1 = x_ref[...] + vec_ref[3:4, :] * ylat
        h1_ref[...] = h1
        r = lax.rsqrt(jnp.mean(h1 * h1, axis=1, keepdims=True) + EPS)
        u2_ref[...] = ((h1 * r * vec_ref[6:7, :]) * vec_ref[4:5, :] + vec_ref[5:6, :]).astype(MXU_DTYPE)

    row = lambda j: (lambda i: (i, j))
    return _pcall(
        body, name="mix_out",
        out_shape=(jax.ShapeDtypeStruct((n, 2 * D), MXU_DTYPE), jax.ShapeDtypeStruct((n, D), F32),
                   jax.ShapeDtypeStruct((n, D), F32), jax.ShapeDtypeStruct((n, D), MXU_DTYPE)),
        grid=(n // TB,),
        in_specs=[pl.BlockSpec((TB, D), row(0)), pl.BlockSpec((TB, D), row(0)), pl.BlockSpec((TB, D), row(4)),
                  pl.BlockSpec((TB, D), row(5)), pl.BlockSpec((TB, D), row(0)), pl.BlockSpec((TB, D), row(0)),
                  pl.BlockSpec((TB, D), row(0)), pl.BlockSpec((TB, D), row(0)), _full((8, D)), _full((2 * D, D))],
        out_specs=(pl.BlockSpec((TB, 2 * D), row(0)), pl.BlockSpec((TB, D), row(0)), pl.BlockSpec((TB, D), row(0)),
                   pl.BlockSpec((TB, D), row(0))),
        sem=("parallel",), vmem_mb=48,
    )(o_f, o_b, p_main, p_main, y_f, y_b, xa, x, vecs, w_out)


def _mix_bwd(dylat, o_f, o_b, p_main, y_f, y_b, xa, vecs, w_out):
    n = dylat.shape[0]
    t_total = p_main.shape[0]
    nlat = n // TB

    def body(*refs):
        dg_ref, dz_ref, acc_ref = refs[11], refs[13], refs[15]
        i = pl.program_id(0)

        @pl.when(i == 0)
        def _():
            acc_ref[...] = jnp.zeros_like(acc_ref)

        @pl.when(i < nlat)
        def _():
            compute(*refs)

        @pl.when(i == nlat)
        def _():
            dg_ref[...] = jnp.zeros_like(dg_ref)
            dz_ref[...] = jnp.zeros_like(dz_ref)

    def compute(dyl_ref, of_ref, ob_ref, g_ref, z_ref, yf_ref, yb_ref, xs_ref, vec_ref, w_ref,
                do_ref, dg_ref, dys_ref, dz_ref, dxs_ref, acc_ref):
        dymix = _nt(dyl_ref[...], w_ref[...])
        o = of_ref[...] + ob_ref[...]
        g = g_ref[...].astype(F32)
        z = z_ref[...].astype(F32)
        xs = xs_ref[...]
        yy = yf_ref[...] + yb_ref[...] + vec_ref[1:2, :] * xs
        _, _, keep, u = _readout(o, g, yy, z, vec_ref)
        do_l, dg_l = [], []
        for h in range(NH):
            cs = slice(h * HF, (h + 1) * HF)
            oh, gh, r, wv = o[:, cs], g[:, cs], keep[h], vec_ref[0:1, cs]
            dhg = dymix[:, cs]
            xh = oh * r
            dn = dhg * _silu(gh)
            dg_l.append(dhg * xh * wv * _dsilu(gh))
            acc_ref[0:1, cs] += _colsum(dn * xh)
            dxh = dn * wv
            do_l.append(r * (dxh - xh * jnp.mean(dxh * xh, axis=1, keepdims=True)))
        du_l = []
        for gi in range(4):
            cs = slice(gi * 256, (gi + 1) * 256)
            ug, r, wv = u[:, cs], keep[NH + gi], vec_ref[2:3, cs]
            dss = dymix[:, D + gi * 256:D + (gi + 1) * 256]
            xh = ug * r
            acc_ref[2:3, cs] += _colsum(dss * xh)
            dxh = dss * wv
            du_l.append(r * (dxh - xh * jnp.mean(dxh * xh, axis=1, keepdims=True)))
        du = jnp.concatenate(du_l, axis=1)
        dyy = du * _silu(z)
        do_ref[...] = jnp.concatenate(do_l, axis=1)
        dg_ref[...] = jnp.concatenate(dg_l, axis=1).astype(dg_ref.dtype)
        dys_ref[...] = dyy
        dz_ref[...] = (du * yy * _dsilu(z)).astype(dz_ref.dtype)
        dxs_ref[...] = dyy * vec_ref[1:2, :]
        acc_ref[1:2, :] += _colsum(dyy * xs)

    row = lambda j: (lambda i: (jnp.minimum(i, nlat - 1), j))
    lat = pl.BlockSpec((TB, D), row(0))
    tok = pl.BlockSpec((TB, D), lambda i: (i, 0))
    return _pcall(
        body, name="mix_bwd",
        out_shape=(jax.ShapeDtypeStruct((n, D), F32), jax.ShapeDtypeStruct((t_total, D), MXU_DTYPE),
                   jax.ShapeDtypeStruct((n, D), F32), jax.ShapeDtypeStruct((t_total, D), MXU_DTYPE),
                   jax.ShapeDtypeStruct((n, D), F32), jax.ShapeDtypeStruct((8, D), F32)),
        grid=(t_total // TB,),
        in_specs=[lat, lat, lat, pl.BlockSpec((TB, D), row(4)), pl.BlockSpec((TB, D), row(5)), lat, lat, lat,
                  _full((8, D)), _full((2 * D, D))],
        out_specs=(lat, tok, lat, tok, lat, _full((8, D))),
        sem=("arbitrary",), vmem_mb=48,
    )(dylat, o_f, o_b, p_main, p_main, y_f, y_b, xa, vecs, w_out)


def _ffn_up(u2, w_gate, w_up):
    n = u2.shape[0]
    tb = 1024

    def body(u_ref, wg_ref, wu_ref, g_ref, up_ref, a_ref):
        uv = u_ref[...]
        gt = _nt(uv, wg_ref[...])
        upv = _nt(uv, wu_ref[...])
        g_ref[...] = gt.astype(g_ref.dtype)
        up_ref[...] = upv.astype(up_ref.dtype)
        a_ref[...] = (_silu(gt) * upv).astype(a_ref.dtype)

    blk = pl.BlockSpec((tb, FSL), lambda j, i: (i, j))
    wblk = pl.BlockSpec((None, FSL, D), lambda j, i: (j, 0, 0))
    return _pcall(
        body, name="ffn_up",
        out_shape=(jax.ShapeDtypeStruct((n, DFFP), MXU_DTYPE),) * 3,
        grid=(4, n // tb), in_specs=[pl.BlockSpec((tb, D), lambda j, i: (i, 0)), wblk, wblk],
        out_specs=(blk, blk, blk), sem=("parallel", "parallel"), vmem_mb=48,
    )(u2, w_gate, w_up)


def _ffn_down_loss(act, w_down, h1, tgt, vecs):
    n = act.shape[0]
    tb = 512

    def body(a_ref, w_ref, h1_ref, t_ref, vec_ref, dh2_ref, dffn_ref, acc_ref):
        i = pl.program_id(0)

        @pl.when(i == 0)
        def _():
            acc_ref[...] = jnp.zeros_like(acc_ref)

        ffn = _nn(a_ref[...], w_ref[...])
        g2 = vec_ref[0:1, :]
        fw = vec_ref[1:2, :]
        h2 = h1_ref[...] + g2 * ffn
        r = lax.rsqrt(jnp.mean(h2 * h2, axis=1, keepdims=True) + EPS)
        xh = h2 * r
        err = xh * fw - t_ref[...]
        dy = err * (1.0 / D)
        acc_ref[2:3, :] += _colsum(err * err) * (0.5 / D)
        acc_ref[1:2, :] += _colsum(dy * xh)
        dxh = dy * fw
        dh2 = r * (dxh - xh * jnp.mean(dxh * xh, axis=1, keepdims=True))
        dh2_ref[...] = dh2
        dffn_ref[...] = (g2 * dh2).astype(dffn_ref.dtype)
        acc_ref[0:1, :] += _colsum(dh2 * ffn)

    return _pcall(
        body, name="ffn_down_loss",
        out_shape=(jax.ShapeDtypeStruct((n, D), F32), jax.ShapeDtypeStruct((n, D), MXU_DTYPE),
                   jax.ShapeDtypeStruct((8, D), F32)),
        grid=(n // tb,),
        in_specs=[pl.BlockSpec((tb, DFFP), lambda i: (i, 0)), _full((DFFP, D)), pl.BlockSpec((tb, D), lambda i: (i, 0)),
                  pl.BlockSpec((tb, D), lambda i: (i, 0)), _full((8, D))],
        out_specs=(pl.BlockSpec((tb, D), lambda i: (i, 0)), pl.BlockSpec((tb, D), lambda i: (i, 0)), _full((8, D))),
        sem=("arbitrary",), vmem_mb=48,
    )(act, w_down, h1, tgt, vecs)


def _ffn_bwd(dffn, w_down, gate, up, w_gate_t, w_up_t):
    n = dffn.shape[0]
    tb = 1024

    def body(df_ref, wd_ref, g_ref, up_ref, wg_ref, wu_ref, dg_ref, dup_ref, du_ref):
        j = pl.program_id(1)
        dact = _nt(df_ref[...], wd_ref[...])
        gt = g_ref[...].astype(F32)
        upv = up_ref[...].astype(F32)
        sg = _sig(gt)
        dgt = (dact * upv * (sg * (1.0 + gt * (1.0 - sg)))).astype(MXU_DTYPE)
        dupv = (dact * (gt * sg)).astype(MXU_DTYPE)
        dg_ref[...] = dgt
        dup_ref[...] = dupv
        part = _nn(dgt, wg_ref[...]) + _nn(dupv, wu_ref[...])

        @pl.when(j == 0)
        def _():
            du_ref[...] = part

        @pl.when(j > 0)
        def _():
            du_ref[...] += part

    tok = pl.BlockSpec((tb, D), lambda i, j: (i, 0))
    ffb = pl.BlockSpec((tb, FSL), lambda i, j: (i, j))
    wsl = pl.BlockSpec((None, FSL, D), lambda i, j: (j, 0, 0))
    return _pcall(
        body, name="ffn_bwd",
        out_shape=(jax.ShapeDtypeStruct((n, DFFP), MXU_DTYPE), jax.ShapeDtypeStruct((n, DFFP), MXU_DTYPE),
                   jax.ShapeDtypeStruct((n, D), F32)),
        grid=(n // tb, 4),
        in_specs=[tok, pl.BlockSpec((FSL, D), lambda i, j: (j, 0)), ffb, ffb, wsl, wsl],
        out_specs=(ffb, ffb, tok), sem=("parallel", "arbitrary"), vmem_mb=48,
    )(dffn, w_down, gate, up, w_gate_t, w_up_t)


def _ffn_norm_bwd(du, h1, ylat, dh2, vecs):
    n = du.shape[0]
    tb = 512

    def body(du_ref, h1_ref, yl_ref, dh2_ref, vec_ref, dh1_ref, dyl_ref, acc_ref):
        @pl.when(pl.program_id(0) == 0)
        def _():
            acc_ref[...] = jnp.zeros_like(acc_ref)

        duv = du_ref[...]
        h1 = h1_ref[...]
        r = lax.rsqrt(jnp.mean(h1 * h1, axis=1, keepdims=True) + EPS)
        xh = h1 * r
        nw = vec_ref[2:3, :]
        acc_ref[0:1, :] += _colsum(duv)
        acc_ref[1:2, :] += _colsum(duv * xh * nw)
        dn = duv * vec_ref[1:2, :]
        acc_ref[2:3, :] += _colsum(dn * xh)
        dxh = dn * nw
        dh1 = dh2_ref[...] + r * (dxh - xh * jnp.mean(dxh * xh, axis=1, keepdims=True))
        dh1_ref[...] = dh1
        dyl_ref[...] = (vec_ref[0:1, :] * dh1).astype(dyl_ref.dtype)
        acc_ref[3:4, :] += _colsum(dh1 * yl_ref[...])

    tok = pl.BlockSpec((tb, D), lambda i: (i, 0))
    return _pcall(
        body, name="ffn_norm_bwd",
        out_shape=(jax.ShapeDtypeStruct((n, D), F32), jax.ShapeDtypeStruct((n, D), MXU_DTYPE),
                   jax.ShapeDtypeStruct((8, D), F32)),
        grid=(n // tb,), in_specs=[tok, tok, tok, tok, _full((8, D))], out_specs=(tok, tok, _full((8, D))),
        sem=("arbitrary",), vmem_mb=40,
    )(du, h1, ylat, dh2, vecs)


def _deep_rows(rows):
    return max(r for r in range(128, 2305, 128) if rows % r == 0)


def _dw(a, b, name):
    tn_rows = a.shape[0]
    bt = _deep_rows(tn_rows)
    kk, nn_ = a.shape[1], b.shape[1]
    bk = 1024 if kk % 1024 == 0 else kk
    bn = 1024 if nn_ % 1024 == 0 else nn_
    nt = tn_rows // bt

    def body(a_ref, b_ref, o_ref, acc):
        t = pl.program_id(2)
        part = _tn(a_ref[...], b_ref[...])

        @pl.when(t == 0)
        def _():
            acc[...] = part

        @pl.when(t > 0)
        def _():
            acc[...] += part

        @pl.when(t == nt - 1)
        def _():
            o_ref[...] = acc[...].astype(o_ref.dtype)

    return _pcall(
        body, name=name, out_shape=jax.ShapeDtypeStruct((kk, nn_), MXU_DTYPE), grid=(kk // bk, nn_ // bn, nt),
        in_specs=[pl.BlockSpec((bt, bk), lambda i, j, t: (t, i)), pl.BlockSpec((bt, bn), lambda i, j, t: (t, j))],
        out_specs=pl.BlockSpec((bk, bn), lambda i, j, t: (i, j)), scratch=[pltpu.VMEM((bk, bn), F32)],
        sem=("parallel", "parallel", "arbitrary"), vmem_mb=40,
    )(a, b)


def _dw_in(segs, u_all, name):
    tiles = []
    for m, s_ in enumerate(segs):
        tiles += [(m, h) for h in range(s_.shape[1] // D)]
    ntile = len(tiles)
    t_total = u_all.shape[0]
    bt = _deep_rows(t_total)
    nt = t_total // bt

    def body(u_ref, *refs):
        seg_refs, o_ref, acc = refs[:len(segs)], refs[len(segs)], refs[len(segs) + 1]
        n, t = pl.program_id(0), pl.program_id(1)
        for k, (m, _) in enumerate(tiles):
            @pl.when(n == k)
            def _(m=m):
                part = _tn(seg_refs[m][...], u_ref[...])

                @pl.when(t == 0)
                def _():
                    acc[...] = part

                @pl.when(t > 0)
                def _():
                    acc[...] += part

        @pl.when(t == nt - 1)
        def _():
            o_ref[...] = acc[...].astype(o_ref.dtype)

    def seg_spec(m):
        ks = [k for k, (mm, _) in enumerate(tiles) if mm == m]
        lo, hi = ks[0], ks[-1]
        on = lambda n: (n >= lo) & (n <= hi)
        return pl.BlockSpec((bt, D), lambda n, t: (jnp.where(on(n), t, 0), jnp.where(on(n), n - lo, 0)))

    return _pcall(
        body, name=name, out_shape=jax.ShapeDtypeStruct((1, ntile * D, D), MXU_DTYPE), grid=(ntile, nt),
        in_specs=[pl.BlockSpec((bt, D), lambda n, t: (t, 0))] + [seg_spec(m) for m in range(len(segs))],
        out_specs=pl.BlockSpec((None, D, D), lambda n, t: (0, n, 0)),
        scratch=[pltpu.VMEM((D, D), F32)], sem=("parallel", "arbitrary"), vmem_mb=56,
    )(u_all, *segs)


def _du_prenorm_bwd(segs, ddt, wi_main, wi_tail, xin, mods, dres, row_off, tb, name, comm=None):
    n = xin.shape[0]
    nt = n // tb
    off = row_off // tb
    has_dx = dres is not None

    def body(*refs):
        seg_refs = refs[:7]
        ddt_ref, w_ref, wb_ref, wdt_ref, x_ref, mod_ref = refs[7:13]
        rest = refs[13:]
        if has_dx:
            dres_ref, dx_ref, acc_ref, du_scr = rest
        else:
            acc_ref, du_scr = rest
        j, i = pl.program_id(0), pl.program_id(1)
        rows = pl.ds(pl.multiple_of(i * tb, tb), tb)

        @pl.when((i == 0) & (j == 0))
        def _():
            acc_ref[...] = jnp.zeros_like(acc_ref)

        @pl.when(j == 0)
        def _():
            du_scr[rows, :] = _nn(ddt_ref[...], wdt_ref[...])

        for k in range(8):
            if not has_dx and k in (4, 5):
                continue

            @pl.when(j == k)
            def _(k=k):
                sv = seg_refs[min(k, 6)][...]
                part = _nn(sv, w_ref[...])
                if k in (2, 4, 6):
                    part = part + _nn(sv[:, 0:WTAIL], wb_ref[...])
                du_scr[rows, :] += part

        @pl.when(j == 7)
        def _():
            du = du_scr[rows, :]
            xv = x_ref[...]
            r = lax.rsqrt(jnp.mean(xv * xv, axis=1, keepdims=True) + EPS)
            xh = xv * r
            nw = mod_ref[1:2, :]
            acc_ref[0:1, :] += _colsum(du)
            acc_ref[1:2, :] += _colsum(du * xh * nw)
            dn = du * mod_ref[0:1, :]
            acc_ref[2:3, :] += _colsum(dn * xh)
            if has_dx:
                dxh = dn * nw
                dx_ref[...] = dres_ref[...] + r * (dxh - xh * jnp.mean(dxh * xh, axis=1, keepdims=True))

    def seg_spec(k):
        if k < 6:
            return pl.BlockSpec((tb, D), lambda j, i: (jnp.where(j == k, i + off, 0), 0))
        return pl.BlockSpec((tb, D), lambda j, i: (jnp.where(j >= 6, i + off, 0), jnp.where(j >= 6, j - 6, 0)))

    last = pl.BlockSpec((tb, D), lambda j, i: (jnp.where(j == 7, i, 0), 0))
    in_specs = [seg_spec(k) for k in range(7)]
    in_specs += [pl.BlockSpec((tb, 128), lambda j, i: (jnp.where(j == 0, i + off, 0), 0))] + _w_specs()
    in_specs += [last, _full((8, D))]
    args = list(segs) + [ddt, wi_main, wi_tail, wi_tail, xin, mods]
    out_shape = [jax.ShapeDtypeStruct((8, D), F32)]
    out_specs = [_full((8, D))]
    if has_dx:
        in_specs.append(last)
        args.append(dres)
        out_shape.insert(0, jax.ShapeDtypeStruct((n, D), F32))
        out_specs.insert(0, last)
    call = dict(body=body, args=args, name=name, out_shape=tuple(out_shape), grid=(8, nt), in_specs=in_specs,
                out_specs=tuple(out_specs), scratch=[pltpu.VMEM((n, D), F32)], sem=("arbitrary", "arbitrary"),
                vmem_mb=56)
    steps = lambda: ((pl.program_id(0) == 0) & (pl.program_id(1) == 0),
                     (pl.program_id(0) == 7) & (pl.program_id(1) == nt - 1))
    return _run(_carry(call, comm, steps))


def _sum8(v):
    def body(v_ref, o_ref):
        acc = v_ref[0]
        for k in range(1, 8):
            acc = acc + v_ref[k]
        o_ref[...] = acc

    return _pcall(body, name="small_sum", out_shape=jax.ShapeDtypeStruct(v.shape[1:], F32),
                  in_specs=[pl.BlockSpec(memory_space=pltpu.VMEM)], out_specs=pl.BlockSpec(memory_space=pltpu.VMEM))(v)


def _adamw(w, m, v, g, name):
    lead = w.ndim == 3
    rows, cols = w.shape[-2:]
    rb = 256 if rows % 256 == 0 else (352 if rows % 352 == 0 else rows)
    c1 = 1.0 - B1 ** STEP
    c2 = 1.0 - B2 ** STEP

    def body(w_ref, m_ref, v_ref, g_ref, d_ref, nm_ref, nv_ref):
        gv = g_ref[...]
        mn = B1 * m_ref[...] + (1.0 - B1) * gv
        vn = B2 * v_ref[...] + (1.0 - B2) * (gv * gv)
        nm_ref[...] = mn
        nv_ref[...] = vn
        d_ref[...] = -LR * ((mn / c1) / (jnp.sqrt(vn / c2) + AEPS) + WD * w_ref[...])

    if rb == rows and rows > 1024:
        cb, steps = 256, cols // 256
        gspec = pl.BlockSpec((rows, cb), lambda i: (0, i))
        spec = pl.BlockSpec((None, rows, cb), lambda i: (0, 0, i)) if lead else gspec
    else:
        steps = rows // rb
        gspec = pl.BlockSpec((rb, cols), lambda i: (i, 0))
        spec = pl.BlockSpec((None, rb, cols), lambda i: (0, i, 0)) if lead else gspec
    return _pcall(
        body, name=name, out_shape=(jax.ShapeDtypeStruct(w.shape, F32),) * 3, grid=(steps,),
        in_specs=[spec] * 3 + [gspec], out_specs=(spec,) * 3, sem=("parallel",), vmem_mb=40,
    )(w, m, v, g)


def _rows(v, n):
    f = v.reshape(-1)
    return jnp.pad(f, (0, n * D - f.shape[0])).reshape(n, D)


def kernel(x, c, ctx, c_ctx, w_ada, b_ada, norm_mix, w_in, conv_w, conv_b, ssd_a_log, ssd_dt_bias, ssd_d, ssd_norm, hgrn_lb_raw, hgrn_norm, w_out, norm_ffn, w_gate, w_up, w_down, final_norm, loss_target, m_c_ctx, m_w_ada, m_b_ada, m_norm_mix, m_w_in, m_conv_w, m_conv_b, m_ssd_a_log, m_ssd_dt_bias, m_ssd_d, m_ssd_norm, m_hgrn_lb_raw, m_hgrn_norm, m_w_out, m_norm_ffn, m_w_gate, m_w_up, m_w_down, m_final_norm, v_c_ctx, v_w_ada, v_b_ada, v_norm_mix, v_w_in, v_conv_w, v_conv_b, v_ssd_a_log, v_ssd_dt_bias, v_ssd_d, v_ssd_norm, v_hgrn_lb_raw, v_hgrn_norm, v_w_out, v_norm_ffn, v_w_gate, v_w_up, v_w_down, v_final_norm):
    ix, iy, ic = lax.axis_index("x"), lax.axis_index("y"), lax.axis_index("c")
    chip = 2 * ix + iy
    me = 2 * chip + ic
    xl, xc, tgt = x[0], ctx[0], loss_target[0]
    n_lat, n_ctx = xl.shape[0], xc.shape[0]
    assert n_ctx == TB and n_lat % 1024 == 0
    t_total = n_lat + n_ctx
    nb = t_total // TB

    pack = jnp.concatenate([c, hgrn_lb_raw.reshape(1, D), _rows(conv_w[0], 3), jnp.zeros((3, D), F32)], axis=0)
    gath = _allgather8(pack, "small_gather").reshape(8, 8, D)
    c_all = gath[:, 0]
    lbraw_full = gath[0::2, 1].reshape(4, 2, 2, 256).transpose(1, 2, 0, 3).reshape(4, D)
    convw_full = gath[0::2, 2:5].reshape(4, 3 * D)[:, :KCONV * 512].reshape(4, KCONV, 512).transpose(1, 0, 2)
    convw_full = convw_full.reshape(KCONV, 2048)
    lbraw8 = jnp.pad(lbraw_full, ((0, 4), (0, 0)))
    convp = jnp.concatenate([convw_full, conv_b, jnp.zeros((2, 2048), F32)], axis=0)
    dtb = jnp.pad(ssd_dt_bias.reshape(1, 32), ((0, 7), (0, 96)))
    alog = jnp.pad(ssd_a_log.reshape(1, 32), ((0, 7), (0, 96)))

    araw = jnp.concatenate([c_all, c_ctx.reshape(1, D), jnp.zeros((7, D), F32)], axis=0)
    ncol_ada = w_ada.shape[2]
    b_shard = lax.dynamic_slice(b_ada, (0, chip * ncol_ada), (1, ncol_ada))
    mod_shard = _ada_fwd(araw, w_ada[0], b_shard)
    mod_all = _allgather8(mod_shard, "mod_gather").reshape(8, 16, ncol_ada)[0::2]
    mod_full = mod_all.transpose(1, 0, 2).reshape(16, 4 * ncol_ada)
    my_mod = lax.dynamic_slice(mod_full, (me, 0), (1, 6 * D)).reshape(6, D)
    sh1, sc1, g1, sh2, sc2, g2 = (my_mod[k:k + 1] for k in range(6))
    csh1, csc1 = mod_full[8:9, 0:D], mod_full[8:9, D:2 * D]

    tr = lambda a: jnp.swapaxes(a, -1, -2)
    shift = [functools.partial(jnp.pad, pad_width=((8 * k, WSL + WTAIL - NSH - 8 * k), (0, 0))) for k in range(4)]
    slab = lax.switch(chip, shift, tr(w_in[0]).astype(MXU_DTYPE))
    padrows = lambda a: jnp.pad(a, ((0, FSL - DFF // 4), (0, 0))).astype(MXU_DTYPE)
    shards = [slab[:WSL], slab[WSL:], w_out[0].astype(MXU_DTYPE), padrows(tr(w_gate[0])), padrows(tr(w_up[0])),
              padrows(w_down[0])]
    own = lambda g_, s_: lax.dynamic_update_slice(g_, s_[None], (chip, 0, 0))
    wi_main, wi_tail = (own(g_, s_) for g_, s_ in zip(_weights_allgather(shards[:2]), shards[:2]))

    zrow = jnp.zeros((1, D), F32)
    mods_lat = jnp.concatenate([1.0 + sc1, sh1, norm_mix, zrow, zrow, zrow, zrow, zrow], axis=0)
    mods_ctx = jnp.concatenate([1.0 + csc1, csh1, norm_mix, zrow, zrow, zrow, zrow, zrow], axis=0)
    outs = _inproj(xl, mods_lat, wi_main, wi_tail, t_total, 1024, 0, None, "inproj_lat",
                   comm=_comm_gather(shards[2:5]))
    wo_g, wg_g, wu_g = (own(g_, s_) for g_, s_ in zip(outs[3:], shards[2:5]))
    w_out_f = wo_g.reshape(2 * D, D)
    p_main, p_dt, u_all = _inproj(xc, mods_ctx, wi_main, wi_tail, t_total, TB, nb - 1, outs[:3], "inproj_ctx")

    o_f, hs_f, wd_g = _hgrn_fwd(p_main, lbraw8, 0, nb, comm=_comm_gather(shards[5:]))
    w_down_f = own(wd_g, shards[5]).reshape(DFFP, D)
    o_b, hs_b = _hgrn_fwd(p_main, lbraw8, 1, nb)
    xa, dts = _ssd_prep(p_main, p_dt, convp, dtb, nb)
    y_f, ss_f = _ssd_fwd(xa, dts, alog, 0, nb)
    y_b, ss_b = _ssd_fwd(xa, dts, alog, 1, nb)

    vec_mix = jnp.concatenate([jnp.tile(hgrn_norm, (1, NH)), jnp.repeat(ssd_d, SP, axis=1), ssd_norm, g1, 1.0 + sc2,
                               sh2, norm_ffn, zrow], axis=0)
    ymix, ylat, h1, u2 = _mix_out(o_f, o_b, p_main, y_f, y_b, xa, xl, vec_mix, w_out_f)
    gate, up, act = _ffn_up(u2, wg_g, wu_g)
    vec_loss = jnp.concatenate([g2, final_norm.reshape(1, D)] + [zrow] * 6, axis=0)
    dh2, dffn, acc_loss = _ffn_down_loss(act, w_down_f, h1, tgt, vec_loss)

    core_arr = jnp.reshape(ic, (1,)).astype(jnp.int32)
    chip_arr = jnp.reshape(chip, (1,)).astype(jnp.int32)
    every = (0, 4)

    def pair_stage(gs, tag):
        return list(_pair_sum(gs, _pair_exchange(gs, "grads_pair_exchange_" + tag), core_arr, "grads_pair_sum_" + tag))

    vec_ffn = jnp.concatenate([g1, 1.0 + sc2, norm_ffn] + [zrow] * 5, axis=0)
    dgate, dup, du2 = _ffn_bwd(dffn, w_down_f, gate, up, wg_g, wu_g)
    dh1, dylat, acc_ffn = _ffn_norm_bwd(du2, h1, ylat, dh2, vec_ffn)
    gw_down = _dw(act, dffn, "dw_down").reshape(4, FSL, D)
    gw_gate = _dw(dgate, u2, "dw_gate").reshape(4, FSL, D)
    gw_up = _dw(dup, u2, "dw_up").reshape(4, FSL, D)
    do, dgr, dys, dzr, dxs_skip, acc_mix = _mix_bwd(dylat, o_f, o_b, p_main, y_f, y_b, xa, vec_mix, w_out_f)
    gw_out = _dw(ymix, dylat, "dw_out").reshape(4, D // 2, D)
    pair_a, dests_a = pair_stage([gw_gate, gw_up], "a1") + pair_stage([gw_down, gw_out], "a2"), [every] * 4

    res = _hgrn_bwd(p_main, lbraw8, hs_f, do, 0, nb, None, comm=_comm_exchange(pair_a[:2], dests_a[:2]))
    (dq0, dff, dv0, dlb_f), recv_a = res[:4], list(res[4:])
    res = _hgrn_bwd(p_main, lbraw8, hs_b, do, 1, nb, (dq0, dv0), comm=_comm_exchange(pair_a[2:], dests_a[2:]))
    (dq, dfb, dv, dlb_b), recv_a = res[:4], recv_a + list(res[4:])
    gw_in = [_dw_in([dq, dff], u_all, "dw_in_0"), _dw_in([dfb, dv], u_all, "dw_in_1"),
             _dw_in([dgr, dzr], u_all, "dw_in_2")]
    pair_b, dests_b = pair_stage(gw_in, "b"), [(0, 1), (1, 2), (2, 3)]

    res = _ssd_bwd(xa, dts, alog, ss_f, dys, 0, nb, None, comm=_comm_exchange(pair_b, dests_b))
    (dxa0, ddts0, da_f), recv_b = res[:3], list(res[3:])
    dxa, ddts, da_b = _ssd_bwd(xa, dts, alog, ss_b, dys, 1, nb, (dxa0, ddts0))
    dxbc, ddt, acc_conv, acc_dtb = _ssd_prep_bwd(p_main, p_dt, convp, dtb, dxa, dxs_skip, ddts, nb)
    gw_in.append(_dw_in([dxbc], u_all, "dw_in_3"))
    gw_in_dt = _dw(ddt, u_all, "dw_in_dt")
    gw_in_tail = jnp.concatenate([g_[:, 0:WTAIL, :] for g_ in gw_in[1:]] + [gw_in_dt[None]], axis=0)
    pair_c, dests_c = pair_stage([gw_in[3], gw_in_tail], "c"), [(3, 4), every]

    segs = [dq, dff, dfb, dv, dgr, dzr, dxbc]
    bmods_lat = jnp.concatenate([1.0 + sc1, norm_mix] + [zrow] * 6, axis=0)
    bmods_ctx = jnp.concatenate([1.0 + csc1, norm_mix] + [zrow] * 6, axis=0)
    res = _du_prenorm_bwd(segs, ddt, wi_main, wi_tail, xl, bmods_lat, dh1, 0, 512, "du_lat",
                          comm=_comm_exchange(pair_c, dests_c))
    (grad_x, acc_lat), recv_c = res[:2], list(res[2:])
    (acc_ctx,) = _du_prenorm_bwd(segs, ddt, wi_main, wi_tail, xc, bmods_ctx, None, n_lat, TB, "du_ctx")

    mine = _chip_sum(pair_b + pair_c + pair_a, recv_b + recv_c + recv_a, chip_arr, dests_b + dests_c + dests_a,
                     [0, 0, 0, 0, 1, 3, 4, 5, 2])
    theirs = _pair_swap(mine)
    whole = [jnp.concatenate([jnp.where(ic == 0, m_, t_), jnp.where(ic == 0, t_, m_)], axis=0)
             for m_, t_ in zip(mine, theirs)]
    g_w_in = lax.dynamic_slice(jnp.concatenate(whole[0:2], axis=0), (8 * chip, 0), (NSH, D))
    g_w_out = whole[2]
    g_w_gate = whole[3][:DFF // 4]
    g_w_up = whole[4][:DFF // 4]
    g_w_down = whole[5][:DFF // 4]

    dmod_lat = jnp.concatenate([acc_lat[0:2], acc_ffn[3:4], acc_ffn[0:2], acc_loss[0:1]], axis=0)
    misc = jnp.concatenate([(da_f + da_b)[0, :32], jnp.zeros((96,), F32), acc_dtb[0, :32], jnp.zeros((96,), F32),
                            jnp.sum(acc_loss[2]).reshape(1), jnp.zeros((D - 257,), F32)]).reshape(1, D)
    sv = jnp.concatenate([
        dmod_lat, acc_ctx[0:2], (acc_lat[2:3] + acc_ctx[2:3]), acc_ffn[2:3], acc_loss[1:2], acc_mix[2:3],
        acc_mix[0:1], acc_mix[1:2], dlb_f[0:1], dlb_b[0:1], acc_conv[0:6].reshape(12, D), misc,
        jnp.zeros((3, D), F32)], axis=0)
    sv_all = _allgather8(sv, "small_grads_gather").reshape(8, 32, D)
    ssum = _sum8(sv_all)
    dmod_rows = sv_all[:, 0:6].reshape(8, 6 * D)
    dmod_ctx_row = jnp.concatenate([ssum[6:8].reshape(1, 2 * D), jnp.zeros((1, 4 * D), F32)], axis=1)
    dmod_full = jnp.concatenate([dmod_rows, dmod_ctx_row, jnp.zeros((7, 6 * D), F32)], axis=0)
    grad_b_ada = jnp.sum(dmod_full, axis=0, keepdims=True)
    dmod_shard = lax.dynamic_slice(dmod_full, (0, chip * ncol_ada), (16, ncol_ada))
    g_w_ada, da_part = _ada_bwd(araw, dmod_shard, w_ada[0])
    da_all = _allgather8(da_part, "ada_ctx_gather").reshape(8, 16, D)[0::2, 8]
    cc = c_ctx.reshape(1, D)
    grad_c_ctx = (jnp.sum(da_all, axis=0, keepdims=True) * _dsilu(cc)).reshape(D)

    grad_norm_mix, grad_norm_ffn, grad_final_norm = ssum[8:9], ssum[9:10], ssum[10].reshape(D)
    grad_ssd_norm = ssum[11:12]
    grad_hgrn_norm = jnp.sum(ssum[12].reshape(NH, HF), axis=0, keepdims=True)
    grad_ssd_d = jnp.sum(ssum[13].reshape(SHEADS, SP), axis=1).reshape(1, SHEADS)
    lb_full = _sig(lbraw_full[0:2] - lbraw_full[2:4])
    dr0 = ssum[14:16] * lb_full * (1.0 - lb_full)
    grad_lb_full = jnp.stack([dr0, -dr0], axis=0)
    grad_lb = lax.dynamic_slice(grad_lb_full, (0, 0, chip * 256), (2, 2, 256))
    grad_conv_w = lax.dynamic_slice(ssum[16:26].reshape(KCONV, 2048), (0, chip * 512), (KCONV, 512)).reshape(1, KCONV, 512)
    grad_conv_b = ssum[26:28].reshape(1, 2048)
    a_val = -jnp.exp(ssd_a_log)
    grad_a_log = ssum[28, 0:32].reshape(1, 2, SHEADS) * a_val
    grad_dt_bias = ssum[28, 128:160].reshape(1, 2, SHEADS)
    loss = ssum[28, 256]

    small_w = [c_ctx, b_ada, norm_mix, conv_w, conv_b, ssd_a_log, ssd_dt_bias, ssd_d, ssd_norm, hgrn_lb_raw,
               hgrn_norm, norm_ffn, final_norm]
    small_m = [m_c_ctx, m_b_ada, m_norm_mix, m_conv_w, m_conv_b, m_ssd_a_log, m_ssd_dt_bias, m_ssd_d, m_ssd_norm,
               m_hgrn_lb_raw, m_hgrn_norm, m_norm_ffn, m_final_norm]
    small_v = [v_c_ctx, v_b_ada, v_norm_mix, v_conv_w, v_conv_b, v_ssd_a_log, v_ssd_dt_bias, v_ssd_d, v_ssd_norm,
               v_hgrn_lb_raw, v_hgrn_norm, v_norm_ffn, v_final_norm]
    small_g = [grad_c_ctx, grad_b_ada, grad_norm_mix, grad_conv_w, grad_conv_b, grad_a_log, grad_dt_bias, grad_ssd_d,
               grad_ssd_norm, grad_lb, grad_hgrn_norm, grad_norm_ffn, grad_final_norm]
    nrows = [-(-a.size // D) for a in small_w]
    packs = lambda lst: jnp.concatenate([_rows(a, r) for a, r in zip(lst, nrows)]
                                        + [jnp.zeros((24 - sum(nrows), D), F32)], axis=0)
    sd, sm, svv = _adamw(packs(small_w), packs(small_m), packs(small_v), packs(small_g), "adamw_small")

    def unpack(p):
        out, r0 = [], 0
        for a, r in zip(small_w, nrows):
            out.append(p[r0:r0 + r].reshape(-1)[:a.size].reshape(a.shape))
            r0 += r
        return out

    sd, sm, svv = unpack(sd), unpack(sm), unpack(svv)
    big = {}
    for nm, w_, m_, v_, g_ in (("w_ada", w_ada, m_w_ada, v_w_ada, g_w_ada), ("w_in", w_in, m_w_in, v_w_in, g_w_in),
                               ("w_out", w_out, m_w_out, v_w_out, g_w_out),
                               ("w_gate", w_gate, m_w_gate, v_w_gate, g_w_gate),
                               ("w_up", w_up, m_w_up, v_w_up, g_w_up),
                               ("w_down", w_down, m_w_down, v_w_down, g_w_down)):
        if nm in ("w_in", "w_gate", "w_up"):
            big[nm] = tuple(tr(t) for t in (g_[None],) + tuple(_adamw(tr(w_), tr(m_), tr(v_), g_, "adamw_" + nm)))
        else:
            big[nm] = (g_[None],) + tuple(_adamw(w_, m_, v_, g_, "adamw_" + nm))

    order = ["c_ctx", "w_ada", "b_ada", "norm_mix", "w_in", "conv_w", "conv_b", "ssd_a_log", "ssd_dt_bias", "ssd_d",
             "ssd_norm", "hgrn_lb_raw", "hgrn_norm", "w_out", "norm_ffn", "w_gate", "w_up", "w_down", "final_norm"]
    small_names = ["c_ctx", "b_ada", "norm_mix", "conv_w", "conv_b", "ssd_a_log", "ssd_dt_bias", "ssd_d", "ssd_norm",
                   "hgrn_lb_raw", "hgrn_norm", "norm_ffn", "final_norm"]
    table = dict(big)
    for k, nm in enumerate(small_names):
        table[nm] = (small_g[k].reshape(small_w[k].shape), sd[k], sm[k], svv[k])
    grads = [table[nm][0] for nm in order]
    deltas = [table[nm][1] for nm in order]
    new_m = [table[nm][2] for nm in order]
    new_v = [table[nm][3] for nm in order]
    return (loss, grad_x[None], *grads, *deltas, *new_m, *new_v)
```

```python
import functools
import math

import jax
import jax.numpy as jnp
from jax import lax
from jax.experimental import pallas as pl
from jax.experimental.pallas import tpu as pltpu

F32 = jnp.float32
BF16 = jnp.bfloat16
MXU_DTYPE = jnp.bfloat16
_INTERPRET = False

D = 1024
NH, HF = 8, 128
HC = 64
SC = 128
SN = 128
SHEADS, SP = 16, 64
GRID_W = 64
KCONV = 5
DFF = 2816
FSL = 768
DFFP = 4 * FSL
NIN = 8224
TB = 256
EPS = 1e-6
LR, B1, B2, AEPS, WD, STEP = 0.001, 0.9, 0.999, 1e-08, 0.01, 10
MESH_ID = pl.DeviceIdType.MESH
NSH = NIN // 4
WSL = 2048
WTAIL = 128


def _pcall(body, *, name, out_shape, grid=(), in_specs=None, out_specs=None, scratch=(), sem=None,
           vmem_mb=None, aliases=None):
    params = {}
    if sem is not None:
        params["dimension_semantics"] = sem
    if vmem_mb is not None:
        params["vmem_limit_bytes"] = vmem_mb << 20
    kw = dict(name=name, out_shape=out_shape, scratch_shapes=list(scratch),
              input_output_aliases=aliases or {}, compiler_params=pltpu.CompilerParams(**params),
              interpret=_INTERPRET)
    if grid:
        kw["grid"] = grid
    if in_specs is not None:
        kw["in_specs"] = in_specs
    if out_specs is not None:
        kw["out_specs"] = out_specs
    return pl.pallas_call(body, **kw)


def _mx(a):
    return a.astype(MXU_DTYPE)


def _dg(a, b, ca, cb):
    return lax.dot_general(_mx(a), _mx(b), (((ca,), (cb,)), ((), ())), preferred_element_type=F32)


def _nn(a, b):
    return _dg(a, b, 1, 0)


def _nt(a, b):
    return _dg(a, b, 1, 1)


def _tn(a, b):
    return _dg(a, b, 0, 0)


def _dot01(m, x):
    hi = x.astype(BF16)
    r1 = x - hi.astype(F32)
    mid = r1.astype(BF16)
    lo = (r1 - mid.astype(F32)).astype(BF16)
    f = lambda t: lax.dot_general(m, t, (((1,), (0,)), ((), ())), preferred_element_type=F32)
    return f(hi) + f(mid) + f(lo)


def _tri(n, upper):
    r = lax.broadcasted_iota(jnp.int32, (n, n), 0)
    c = lax.broadcasted_iota(jnp.int32, (n, n), 1)
    return (c >= r) if upper else (c <= r)


def _b01(mask):
    return jnp.where(mask, 1.0, 0.0).astype(BF16)


def _sig(x):
    return jax.nn.sigmoid(x)


def _silu(x):
    return x * _sig(x)


def _dsilu(x):
    s = _sig(x)
    return s * (1.0 + x * (1.0 - s))


def _softplus(x):
    return jnp.maximum(x, 0.0) + jnp.log(1.0 + jnp.exp(-jnp.abs(x)))


def _rowsum(x):
    return jnp.sum(x, axis=1, keepdims=True)


def _colsum(x):
    return jnp.sum(x, axis=0, keepdims=True)


def _full(shape):
    return pl.BlockSpec(shape, lambda *_: (0,) * len(shape))


def _allgather8(v, name):
    m_per, n = v.shape

    def body(x_ref, out_ref, send_sems, recv_sems, local_sem):
        x, y, c = lax.axis_index("x"), lax.axis_index("y"), lax.axis_index("c")
        me, sibling = (x, y, c), (x, y, 1 - c)
        chips = [(1 - x, y), (x, 1 - y), (1 - x, 1 - y)]

        def rows(px, py, pc):
            return out_ref.at[pl.ds((4 * px + 2 * py + pc) * m_per, m_per), :]

        def copy(k, block, to, src=None):
            return pltpu.make_async_remote_copy(
                src_ref=rows(*block) if src is None else src, dst_ref=rows(*block),
                send_sem=send_sems.at[k], recv_sem=recv_sems.at[k], device_id=to, device_id_type=MESH_ID)

        mine = pltpu.make_async_copy(x_ref, rows(*me), local_sem)
        mine.start()
        first = [copy(0, me, sibling, src=x_ref)]
        first += [copy(1 + j, me, (*chip, c), src=x_ref) for j, chip in enumerate(chips)]
        for cp in first:
            cp.start()
        passed = [copy(4 + j, (*chip, c), sibling) for j, chip in enumerate(chips)]
        for j, chip in enumerate(chips):
            copy(1 + j, (*chip, c), me).wait_recv()
            passed[j].start()
        copy(0, sibling, me).wait_recv()
        for j, chip in enumerate(chips):
            copy(4 + j, (*chip, 1 - c), me).wait_recv()
        for cp in first + passed:
            cp.wait_send()
        mine.wait()

    return _pcall(
        body, name=name, out_shape=jax.ShapeDtypeStruct((8 * m_per, n), v.dtype),
        in_specs=[pl.BlockSpec(memory_space=pltpu.VMEM)], out_specs=pl.BlockSpec(memory_space=pltpu.VMEM),
        scratch=[pltpu.SemaphoreType.DMA((7,)), pltpu.SemaphoreType.DMA((7,)), pltpu.SemaphoreType.DMA],
    )(v)


def _gather_ops(ins, outs, send_sems, recv_sems):
    n = len(ins)
    x, y, c = lax.axis_index("x"), lax.axis_index("y"), lax.axis_index("c")
    sibling = (x, y, 1 - c)
    chips = [(1 - x, y), (x, 1 - y), (1 - x, 1 - y)]

    def part(a, px, py, pc):
        half = ins[a].shape[0] // 2
        return outs[a].at[2 * px + py, pl.ds(pc * half, half), :]

    def copy(a, k, block, to, src=None):
        return pltpu.make_async_remote_copy(
            src_ref=part(a, *block) if src is None else src, dst_ref=part(a, *block),
            send_sem=send_sems.at[6 * a + k], recv_sem=recv_sems.at[6 * a + k], device_id=to,
            device_id_type=MESH_ID)

    def first(a, j):
        half = ins[a].shape[0] // 2
        return copy(a, j, (x, y, c), (*chips[j], c), src=ins[a].at[pl.ds(c * half, half), :])

    def start():
        for a in range(n):
            for j in range(3):
                first(a, j).start()

    def finish():
        for a in range(n):
            for j, chip in enumerate(chips):
                copy(a, j, (*chip, c), (x, y, c)).wait_recv()
                copy(a, 3 + j, (*chip, c), sibling).start()
        for a in range(n):
            for j, chip in enumerate(chips):
                copy(a, 3 + j, (*chip, 1 - c), (x, y, c)).wait_recv()
        for a in range(n):
            for j, chip in enumerate(chips):
                first(a, j).wait_send()
                copy(a, 3 + j, (*chip, c), sibling).wait_send()

    return start, finish


def _gather_out(shards):
    return tuple(jax.ShapeDtypeStruct((4,) + s_.shape, s_.dtype) for s_ in shards)


def _gather_sems(n):
    return [pltpu.SemaphoreType.DMA((6 * n,)), pltpu.SemaphoreType.DMA((6 * n,))]


def _weights_allgather(shards):
    n = len(shards)

    def body(*refs):
        start, finish = _gather_ops(refs[:n], refs[n:2 * n], *refs[2 * n:])
        start()
        finish()

    return _pcall(
        body, name="weights_allgather", out_shape=_gather_out(shards),
        in_specs=[pl.BlockSpec(memory_space=pl.ANY)] * n, out_specs=(pl.BlockSpec(memory_space=pl.ANY),) * n,
        scratch=_gather_sems(n),
    )(*shards)


def _pair_exchange(gs, name):
    n = len(gs)

    def body(*refs):
        ins, outs = refs[:n], refs[n:2 * n]
        send_sems, recv_sems = refs[2 * n:]
        x, y, c = lax.axis_index("x"), lax.axis_index("y"), lax.axis_index("c")
        cps = []
        for a in range(n):
            half = ins[a].shape[1] // 2
            cps.append(pltpu.make_async_remote_copy(
                src_ref=ins[a].at[:, pl.ds((1 - c) * half, half), :], dst_ref=outs[a], send_sem=send_sems.at[a],
                recv_sem=recv_sems.at[a], device_id=(x, y, 1 - c), device_id_type=MESH_ID))
        for cp in cps:
            cp.start()
        for cp in cps:
            cp.wait()

    return _pcall(
        body, name=name,
        out_shape=tuple(jax.ShapeDtypeStruct((g.shape[0], g.shape[1] // 2, g.shape[2]), g.dtype) for g in gs),
        in_specs=[pl.BlockSpec(memory_space=pl.ANY)] * n, out_specs=(pl.BlockSpec(memory_space=pl.ANY),) * n,
        scratch=[pltpu.SemaphoreType.DMA((n,)), pltpu.SemaphoreType.DMA((n,))],
    )(*gs)


def _exchange_ops(ins, outs, send_sems, recv_sems, dests):
    x, y, c = lax.axis_index("x"), lax.axis_index("y"), lax.axis_index("c")
    mine = 2 * x + y
    chips = [(1 - x, y), (x, 1 - y), (1 - x, 1 - y)]

    def each(fn):
        for a in range(len(ins)):
            lo, hi = dests[a]
            for j, (px, py) in enumerate(chips):
                q = 2 * px + py
                cp = pltpu.make_async_remote_copy(
                    src_ref=ins[a].at[jnp.clip(q - lo, 0, hi - lo - 1)], dst_ref=outs[a].at[j],
                    send_sem=send_sems.at[3 * a + j], recv_sem=recv_sems.at[3 * a + j], device_id=(px, py, c),
                    device_id_type=MESH_ID)
                fn(cp, (q >= lo) & (q < hi), (mine >= lo) & (mine < hi), (lo, hi) == (0, 4))

    def start():
        def go(cp, send_ok, recv_ok, always):
            if always:
                cp.start()
            else:
                pl.when(send_ok)(cp.start)
        each(go)

    def finish():
        def go(cp, send_ok, recv_ok, always):
            if always:
                cp.wait()
            else:
                pl.when(send_ok)(cp.wait_send)
                pl.when(recv_ok)(cp.wait_recv)
        each(go)

    return start, finish


def _comm_exchange(hs, dests):
    n = len(hs)
    return (list(hs), tuple(jax.ShapeDtypeStruct((3,) + h.shape[1:], h.dtype) for h in hs),
            [pltpu.SemaphoreType.DMA((3 * n,)), pltpu.SemaphoreType.DMA((3 * n,))],
            lambda i, o, s, r: _exchange_ops(i, o, s, r, dests))


def _comm_gather(shards):
    return (list(shards), _gather_out(shards), _gather_sems(len(shards)), _gather_ops)


def _carry(call, comm, steps):
    if comm is None:
        return call
    arrays, out_shape, sems, make = comm
    n, n_in, n_out = len(arrays), len(call["args"]), len(call["out_shape"])
    body = call["body"]

    def wrapped(*refs):
        base_in, cin = refs[:n_in], refs[n_in:n_in + n]
        rest = refs[n_in + n:]
        base_out, cout, scr = rest[:n_out], rest[n_out:n_out + n], rest[n_out + n:]
        start, finish = make(cin, cout, scr[-2], scr[-1])
        first, last = steps()
        pl.when(first)(start)
        body(*base_in, *base_out, *scr[:-2])
        pl.when(last)(finish)

    anyspec = pl.BlockSpec(memory_space=pl.ANY)
    return dict(call, body=wrapped, args=list(call["args"]) + arrays,
                in_specs=list(call["in_specs"]) + [anyspec] * n,
                out_shape=tuple(call["out_shape"]) + tuple(out_shape),
                out_specs=tuple(call["out_specs"]) + (anyspec,) * n,
                scratch=list(call["scratch"]) + sems)


def _run(call):
    args = call.pop("args")
    body = call.pop("body")
    return _pcall(body, **call)(*args)


def _pair_swap(rs):
    n = len(rs)

    def body(*refs):
        ins, outs = refs[:n], refs[n:2 * n]
        send_sems, recv_sems = refs[2 * n:]
        x, y, c = lax.axis_index("x"), lax.axis_index("y"), lax.axis_index("c")
        cps = [pltpu.make_async_remote_copy(
            src_ref=ins[a], dst_ref=outs[a], send_sem=send_sems.at[a], recv_sem=recv_sems.at[a],
            device_id=(x, y, 1 - c), device_id_type=MESH_ID) for a in range(n)]
        for cp in cps:
            cp.start()
        for cp in cps:
            cp.wait()

    return _pcall(
        body, name="grads_pair_swap", out_shape=tuple(jax.ShapeDtypeStruct(r.shape, r.dtype) for r in rs),
        in_specs=[pl.BlockSpec(memory_space=pl.ANY)] * n, out_specs=(pl.BlockSpec(memory_space=pl.ANY),) * n,
        scratch=[pltpu.SemaphoreType.DMA((n,)), pltpu.SemaphoreType.DMA((n,))],
    )(*rs)


SUM_STEPS = 4


def _pair_sum(gs, recvs, core, name):
    n = len(gs)

    def body(c_ref, *refs):
        for a in range(n):
            refs[2 * n + a][...] = (refs[a][...].astype(F32) + refs[n + a][...].astype(F32)).astype(refs[2 * n + a].dtype)

    blk = lambda g: (g.shape[0], g.shape[1] // (2 * SUM_STEPS), g.shape[2])
    return pl.pallas_call(
        body, name=name,
        out_shape=tuple(jax.ShapeDtypeStruct((g.shape[0], g.shape[1] // 2, g.shape[2]), g.dtype) for g in gs),
        grid_spec=pltpu.PrefetchScalarGridSpec(
            num_scalar_prefetch=1, grid=(SUM_STEPS,),
            in_specs=[pl.BlockSpec(blk(g), lambda i, cr: (0, cr[0] * SUM_STEPS + i, 0)) for g in gs]
            + [pl.BlockSpec(blk(g), lambda i, cr: (0, i, 0)) for g in gs],
            out_specs=tuple(pl.BlockSpec(blk(g), lambda i, cr: (0, i, 0)) for g in gs)),
        compiler_params=pltpu.CompilerParams(vmem_limit_bytes=40 << 20), interpret=_INTERPRET,
    )(core, *gs, *recvs)


def _chip_sum(hs, recvs, chip, dests, slots):
    n = len(hs)
    nout = max(slots) + 1
    first = [slots.index(o) for o in range(nout)]
    every = lambda d_: d_ == (0, 4)

    def own(d_):
        if every(d_):
            return lambda i, kr: (kr[0], i, 0)
        return lambda i, kr: (0, jnp.where(kr[0] == d_[0], i, 0), 0)

    def got(d_):
        if every(d_):
            return lambda i, kr: (0, i, 0)
        return lambda i, kr: (0, jnp.where(kr[0] == d_[0], i, 0), 0)

    def body(k_ref, *refs):
        for a in range(n):
            def emit(a=a):
                acc = refs[a][0].astype(F32)
                for j in range(3):
                    acc = acc + refs[n + a][j].astype(F32)
                refs[2 * n + slots[a]][...] = acc
            if every(dests[a]):
                emit()
            else:
                pl.when(k_ref[0] == dests[a][0])(emit)

    rb = lambda h: h.shape[1] // SUM_STEPS
    return pl.pallas_call(
        body, name="grads_chip_sum",
        out_shape=tuple(jax.ShapeDtypeStruct(hs[a].shape[1:], F32) for a in first),
        grid_spec=pltpu.PrefetchScalarGridSpec(
            num_scalar_prefetch=1, grid=(SUM_STEPS,),
            in_specs=[pl.BlockSpec((1, rb(h), h.shape[2]), own(d_)) for h, d_ in zip(hs, dests)]
            + [pl.BlockSpec((3, rb(h), h.shape[2]), got(d_)) for h, d_ in zip(hs, dests)],
            out_specs=tuple(pl.BlockSpec((rb(hs[a]), hs[a].shape[2]), lambda i, kr: (i, 0)) for a in first)),
        compiler_params=pltpu.CompilerParams(vmem_limit_bytes=40 << 20), interpret=_INTERPRET,
    )(chip, *hs, *recvs)


def _ada_fwd(araw, w, b):
    nblk = w.shape[1] // 512

    def body(a_ref, w_ref, b_ref, o_ref):
        o_ref[...] = _nn(_silu(a_ref[...]), w_ref[...]) + b_ref[...]

    return _pcall(
        body, name="ada_fwd", out_shape=jax.ShapeDtypeStruct((16, w.shape[1]), F32), grid=(nblk,),
        in_specs=[_full((16, D)), pl.BlockSpec((D, 512), lambda j: (0, j)), pl.BlockSpec((1, 512), lambda j: (0, j))],
        out_specs=pl.BlockSpec((16, 512), lambda j: (0, j)), sem=("parallel",),
    )(araw, w, b)


def _ada_bwd(araw, dmod, w):
    nblk = w.shape[1] // 512

    def body(a_ref, d_ref, w_ref, gw_ref, da_ref):
        j = pl.program_id(0)
        gw_ref[...] = _tn(_silu(a_ref[...]), d_ref[...])
        part = _nt(d_ref[...], w_ref[...])

        @pl.when(j == 0)
        def _():
            da_ref[...] = part

        @pl.when(j > 0)
        def _():
            da_ref[...] += part

    return _pcall(
        body, name="ada_bwd",
        out_shape=(jax.ShapeDtypeStruct(w.shape, F32), jax.ShapeDtypeStruct((16, D), F32)), grid=(nblk,),
        in_specs=[_full((16, D)), pl.BlockSpec((16, 512), lambda j: (0, j)), pl.BlockSpec((D, 512), lambda j: (0, j))],
        out_specs=(pl.BlockSpec((D, 512), lambda j: (0, j)), _full((16, D))), sem=("arbitrary",),
    )(araw, dmod, w)


def _w_specs():
    return [pl.BlockSpec((None, D, D), lambda j, i: (j // 2, j % 2, 0)),
            pl.BlockSpec((None, WTAIL, D), lambda j, i: (jnp.maximum(j // 2 - 1, 0), 0, 0)),
            pl.BlockSpec((None, WTAIL, D), lambda j, i: (3, 0, 0))]


def _inproj(xin, mods, wi_main, wi_tail, t_total, tb, blk_off, prev, name, comm=None):
    n = xin.shape[0]
    nt = n // tb
    ncol = 8

    def body(x_ref, mod_ref, w_ref, wb_ref, wdt_ref, *rest):
        p_ref, pdt_ref, u_ref, uscr = rest[-4:]
        j, i = pl.program_id(0), pl.program_id(1)
        rows = pl.ds(pl.multiple_of(i * tb, tb), tb)

        @pl.when(j == 0)
        def _():
            xv = x_ref[...]
            r = lax.rsqrt(jnp.mean(xv * xv, axis=1, keepdims=True) + EPS)
            u = (xv * r * mod_ref[2:3, :]) * mod_ref[0:1, :] + mod_ref[1:2, :]
            ub = u.astype(MXU_DTYPE)
            uscr[rows, :] = ub
            u_ref[...] = ub
            pdt_ref[...] = _nt(ub, wdt_ref[...])

        ub = uscr[rows, :]
        pv = _nt(ub, w_ref[...])

        @pl.when((j % 2 == 1) | (j == 0))
        def _():
            p_ref[...] = pv.astype(p_ref.dtype)

        @pl.when((j % 2 == 0) & (j > 0))
        def _():
            head = pv[:, 0:WTAIL] + _nt(ub, wb_ref[...])
            p_ref[...] = jnp.concatenate([head, pv[:, WTAIL:]], axis=1).astype(p_ref.dtype)

    once = lambda j, i: (jnp.where(j == 0, i, nt - 1) + blk_off, 0)
    in_specs = [pl.BlockSpec((tb, D), lambda j, i: (jnp.where(j == 0, i, nt - 1), 0)), _full((8, D))] + _w_specs()
    args = [xin, mods, wi_main, wi_tail, wi_tail]
    aliases = None
    if prev is not None:
        in_specs += [pl.BlockSpec(memory_space=pl.ANY)] * 3
        args += list(prev)
        aliases = {5: 0, 6: 1, 7: 2}
    call = dict(
        body=body, args=args, name=name,
        out_shape=(jax.ShapeDtypeStruct((t_total, ncol * D), MXU_DTYPE), jax.ShapeDtypeStruct((t_total, 128), F32),
                   jax.ShapeDtypeStruct((t_total, D), MXU_DTYPE)),
        grid=(ncol, nt), in_specs=in_specs,
        out_specs=(pl.BlockSpec((tb, D), lambda j, i: (i + blk_off, j)), pl.BlockSpec((tb, 128), once),
                   pl.BlockSpec((tb, D), once)),
        scratch=[pltpu.VMEM((n, D), MXU_DTYPE)], sem=("arbitrary", "arbitrary"), vmem_mb=48, aliases=aliases)
    steps = lambda: ((pl.program_id(0) == 0) & (pl.program_id(1) == 0),
                     (pl.program_id(0) == ncol - 1) & (pl.program_id(1) == nt - 1))
    return _run(_carry(call, comm, steps))


def _blk(s, nb, rev):
    return jnp.where(s == 0, nb - 1, (nb - 1 - s) if rev else (s - 1))


def _hgrn_gate(fr, lbraw_ref, d):
    lb = _sig(lbraw_ref[d:d + 1, :] - lbraw_ref[2 + d:3 + d, :])
    sg = _sig(fr)
    return lb, sg, lb + (1.0 - lb) * sg


def _hgrn_fwd(p_main, lbraw, d, nb, comm=None):
    t_total = p_main.shape[0]
    rev = d == 1
    nch = TB // HC
    scale = HF ** -0.5

    def body(q_ref, f_ref, v_ref, lb_ref, o_ref, sp_ref, st):
        s = pl.program_id(0)

        @pl.when(s == 0)
        def _():
            st[...] = jnp.zeros_like(st)

        mb = _tri(HC, rev)
        m01 = _b01(mb)
        order = list(reversed(range(nch)) if rev else range(nch))
        hs_ = [slice(h * HF, (h + 1) * HF) for h in range(NH)]
        pre = {}
        for c in order:
            rows = slice(c * HC, (c + 1) * HC)
            _, _, f = _hgrn_gate(f_ref[rows, :].astype(F32), lb_ref, d)
            k = 1.0 - f
            cum = _dot01(m01, jnp.log(f))
            tot = cum[0:1, :] if rev else cum[HC - 1:HC, :]
            qd = _silu(q_ref[rows, :].astype(F32)) * scale * jnp.exp(cum)
            ki = k * jnp.exp(-cum)
            etot = jnp.exp(tot)
            pre[c] = (_mx(qd), _mx(ki), _mx(ki * etot), _mx(v_ref[rows, :]), etot)
        scs = {c: [_nt(pre[c][0][:, cs], pre[c][1][:, cs]) for cs in hs_] for c in order}
        upd = {c: [_tn(pre[c][3][:, cs], pre[c][2][:, cs]) for cs in hs_] for c in order}
        intra = {c: [_nn(jnp.where(mb, scs[c][h], 0.0), pre[c][3][:, cs]) for h, cs in enumerate(hs_)] for c in order}
        for c in order:
            rows = slice(c * HC, (c + 1) * HC)
            qdb, etot = pre[c][0], pre[c][4]
            for h, cs in enumerate(hs_):
                sth = st[h]
                stb = sth.astype(sp_ref.dtype)
                sp_ref[c, h] = stb
                o_ref[rows, cs] = intra[c][h] + _nt(qdb[:, cs], stb)
                st[h] = sth * etot[:, cs] + upd[c][h]

    col = lambda j: (lambda s: (_blk(s, nb, rev), j))
    call = dict(
        body=body, args=[p_main, p_main, p_main, lbraw], name=f"hgrn_fwd_{d}",
        out_shape=(jax.ShapeDtypeStruct((t_total, D), F32),
                   jax.ShapeDtypeStruct((nch * nb, NH, HF, HF), MXU_DTYPE)),
        grid=(nb,),
        in_specs=[pl.BlockSpec((TB, D), col(0)), pl.BlockSpec((TB, D), col(1 + d)), pl.BlockSpec((TB, D), col(3)),
                  _full((8, D))],
        out_specs=(pl.BlockSpec((TB, D), col(0)),
                   pl.BlockSpec((nch, NH, HF, HF), lambda s: (_blk(s, nb, rev), 0, 0, 0))),
        scratch=[pltpu.VMEM((NH, HF, HF), F32)], sem=("arbitrary",), vmem_mb=40)
    return _run(_carry(call, comm, lambda: (pl.program_id(0) == 0, pl.program_id(0) == nb - 1)))


def _hgrn_bwd(p_main, lbraw, sprev, do, d, nb, prev, comm=None):
    t_total = p_main.shape[0]
    rev = d == 1
    nch = TB // HC
    scale = HF ** -0.5
    last = prev is not None
    odt = MXU_DTYPE if last else F32

    def body(q_ref, f_ref, v_ref, lb_ref, sp_ref, do_ref, *rest):
        if last:
            dqp_ref, dvp_ref = rest[:2]
            rest = rest[2:]
        dq_ref, df_ref, dv_ref, dlb_ref, dst = rest
        sp_id = pl.program_id(0)
        is_ctx = sp_id == nb - 1

        @pl.when(sp_id == 0)
        def _():
            dst[...] = jnp.zeros_like(dst)
            dlb_ref[...] = jnp.zeros_like(dlb_ref)

        mb = _tri(HC, rev)
        mbt = _tri(HC, not rev)
        m01 = _b01(mb)
        mt01 = _b01(mbt)
        order = list(range(nch) if rev else reversed(range(nch)))
        hs_ = [slice(h * HF, (h + 1) * HF) for h in range(NH)]
        pre = {}
        for c in order:
            rows = slice(c * HC, (c + 1) * HC)
            lb, sg, f = _hgrn_gate(f_ref[rows, :].astype(F32), lb_ref, d)
            k = 1.0 - f
            cum = _dot01(m01, jnp.log(f))
            tot = cum[0:1, :] if rev else cum[HC - 1:HC, :]
            e = jnp.exp(cum)
            ei = jnp.exp(-cum)
            etot = jnp.exp(tot)
            ee = ei * etot
            qraw = q_ref[rows, :].astype(F32)
            qd = _silu(qraw) * scale * e
            ki = k * ei
            ke = k * ee
            dov = jnp.where(is_ctx, 0.0, do_ref[rows, :])
            pre[c] = dict(lb=lb, sg=sg, f=f, e=e, ei=ei, ee=ee, etot=etot, qraw=qraw, qd=qd, ki=ki, ke=ke,
                          qdb=_mx(qd), kib=_mx(ki), keb=_mx(ke), vb=_mx(v_ref[rows, :]), dob=_mx(dov))
        units = [(c, h) for c in order for h in range(NH)]
        col = lambda u, key: pre[u[0]][key][:, hs_[u[1]]]
        pt = {u: jnp.where(mbt, _nt(col(u, "kib"), col(u, "qdb")), 0.0) for u in units}
        dp = {u: jnp.where(mb, _nt(col(u, "dob"), col(u, "vb")), 0.0) for u in units}
        dpt = {u: jnp.where(mbt, _nt(col(u, "vb"), col(u, "dob")), 0.0) for u in units}
        dv_i = {u: _nn(pt[u], col(u, "dob")) for u in units}
        dqd_ = {u: _nn(dp[u], col(u, "kib")) + _nn(col(u, "dob"), sp_ref[u[0], u[1]]) for u in units}
        dki_ = {u: _nn(dpt[u], col(u, "qdb")) for u in units}
        dsl = {u: _tn(col(u, "dob"), col(u, "qdb")) for u in units}
        for c in order:
            rows = slice(c * HC, (c + 1) * HC)
            p = pre[c]
            dv_l, dke_l, dtot_l = [], [], []
            for h, cs in enumerate(hs_):
                dso = dst[h]
                dsob = _mx(dso)
                dv_l.append(dv_i[(c, h)] + _nt(p["keb"][:, cs], dsob))
                dke_l.append(_nn(p["vb"][:, cs], dsob))
                dtot_l.append(_colsum(dso * sp_ref[c, h].astype(F32)) * p["etot"][:, cs])
                dst[h] = dso * p["etot"][:, cs] + dsl[(c, h)]
            lb, sg, f, e, ei, ee, qraw, qd, ki, ke = (p[n_] for n_ in ("lb", "sg", "f", "e", "ei", "ee", "qraw", "qd",
                                                                     "ki", "ke"))
            dqd = jnp.concatenate([dqd_[(c, h)] for h in range(NH)], axis=1)
            dki = jnp.concatenate([dki_[(c, h)] for h in range(NH)], axis=1)
            dke = jnp.concatenate(dke_l, axis=1)
            dcum = dqd * qd - dki * ki - dke * ke
            dtot = jnp.concatenate(dtot_l, axis=1) + _colsum(dke * ke)
            dk = dki * ei + dke * ee
            dlf = _dot01(mt01, dcum) + dtot
            df = dlf / f - dk
            dlb_ref[0:1, :] += _colsum(df * (1.0 - sg))
            dfr = df * (1.0 - lb) * sg * (1.0 - sg)
            dq = dqd * e * scale * _dsilu(qraw)
            dv = jnp.concatenate(dv_l, axis=1)
            if last:
                dq = dq + dqp_ref[rows, :]
                dv = dv + dvp_ref[rows, :]
            dq_ref[rows, :] = dq.astype(odt)
            dv_ref[rows, :] = dv.astype(odt)
            df_ref[rows, :] = dfr.astype(MXU_DTYPE)

    blk = lambda s: _blk(nb - 1 - s, nb, rev)
    col = lambda j: (lambda s: (blk(s), j))
    in_specs = [pl.BlockSpec((TB, D), col(0)), pl.BlockSpec((TB, D), col(1 + d)), pl.BlockSpec((TB, D), col(3)),
                _full((8, D)), pl.BlockSpec((nch, NH, HF, HF), lambda s: (blk(s), 0, 0, 0)),
                pl.BlockSpec((TB, D), lambda s: (jnp.minimum(blk(s), nb - 2), 0))]
    args = [p_main, p_main, p_main, lbraw, sprev, do]
    if last:
        in_specs += [pl.BlockSpec((TB, D), col(0))] * 2
        args += list(prev)
    call = dict(
        body=body, args=args, name=f"hgrn_bwd_{d}",
        out_shape=(jax.ShapeDtypeStruct((t_total, D), odt), jax.ShapeDtypeStruct((t_total, D), MXU_DTYPE),
                   jax.ShapeDtypeStruct((t_total, D), odt), jax.ShapeDtypeStruct((8, D), F32)),
        grid=(nb,), in_specs=in_specs,
        out_specs=(pl.BlockSpec((TB, D), col(0)), pl.BlockSpec((TB, D), col(0)), pl.BlockSpec((TB, D), col(0)),
                   _full((8, D))),
        scratch=[pltpu.VMEM((NH, HF, HF), F32)], sem=("arbitrary",), vmem_mb=48)
    return _run(_carry(call, comm, lambda: (pl.program_id(0) == 0, pl.program_id(0) == nb - 1)))


def _conv_masks(tb, is_ctx):
    seg = jnp.where(is_ctx, tb, GRID_W)
    pos = lax.broadcasted_iota(jnp.int32, (tb, 1), 0) & (seg - 1)
    return pos, seg


def _shift_rows(x, dshift, pos, seg):
    if dshift == 0:
        return x
    n = x.shape[0]
    rolled = pltpu.roll(x, (-dshift) % n, 0)
    ok = (pos + dshift >= 0) & (pos + dshift < seg)
    return jnp.where(ok, rolled, 0.0)


def _ssd_prep(p_main, p_dt, convp, dtb, nb):
    t_total = p_main.shape[0]

    def body(x_ref, dt_ref, cw_ref, dtb_ref, xa_ref, dts_ref):
        is_ctx = pl.program_id(0) == nb - 1
        pos, seg = _conv_masks(TB, is_ctx)
        xv = x_ref[...].astype(F32)
        acc = cw_ref[5:6, :] + cw_ref[2:3, :] * xv
        for kk in (0, 1, 3, 4):
            acc = acc + cw_ref[kk:kk + 1, :] * _shift_rows(xv, kk - 2, pos, seg)
        xa_ref[...] = _silu(acc)
        dts_ref[...] = _softplus(dt_ref[...] + dtb_ref[0:1, :])

    return _pcall(
        body, name="ssd_prep",
        out_shape=(jax.ShapeDtypeStruct((t_total, 2048), F32), jax.ShapeDtypeStruct((t_total, 128), F32)),
        grid=(nb,),
        in_specs=[pl.BlockSpec((TB, 2048), lambda i: (i, 3)), pl.BlockSpec((TB, 128), lambda i: (i, 0)),
                  _full((8, 2048)), _full((8, 128))],
        out_specs=(pl.BlockSpec((TB, 2048), lambda i: (i, 0)), pl.BlockSpec((TB, 128), lambda i: (i, 0))),
        sem=("parallel",), vmem_mb=32,
    )(p_main, p_dt, convp, dtb)


def _ssd_prep_bwd(p_main, p_dt, convp, dtb, dxa, dxs_skip, ddts, nb):
    t_total = p_main.shape[0]

    def body(x_ref, dt_ref, cw_ref, dtb_ref, dxa_ref, dsk_ref, ddts_ref, dx_ref, ddt_ref, dcw_ref, ddtb_ref):
        i = pl.program_id(0)
        is_ctx = i == nb - 1

        @pl.when(i == 0)
        def _():
            dcw_ref[...] = jnp.zeros_like(dcw_ref)
            ddtb_ref[...] = jnp.zeros_like(ddtb_ref)

        pos, seg = _conv_masks(TB, is_ctx)
        xv = x_ref[...].astype(F32)
        sh = {kk: _shift_rows(xv, kk - 2, pos, seg) for kk in range(KCONV)}
        acc = cw_ref[5:6, :]
        for kk in range(KCONV):
            acc = acc + cw_ref[kk:kk + 1, :] * sh[kk]
        dact = dxa_ref[...]
        dact = jnp.concatenate([dact[:, :D] + jnp.where(is_ctx, 0.0, dsk_ref[...]), dact[:, D:]], axis=1)
        dpre = dact * _dsilu(acc)
        dxv = cw_ref[2:3, :] * dpre
        for kk in (0, 1, 3, 4):
            dxv = dxv + cw_ref[kk:kk + 1, :] * _shift_rows(dpre, 2 - kk, pos, seg)
        dx_ref[...] = dxv.astype(dx_ref.dtype)
        for kk in range(KCONV):
            dcw_ref[kk:kk + 1, :] += _colsum(dpre * sh[kk])
        dcw_ref[5:6, :] += _colsum(dpre)
        draw = ddts_ref[...] * _sig(dt_ref[...] + dtb_ref[0:1, :])
        ddt_ref[...] = draw.astype(ddt_ref.dtype)
        ddtb_ref[0:1, :] += _colsum(draw)

    return _pcall(
        body, name="ssd_prep_bwd",
        out_shape=(jax.ShapeDtypeStruct((t_total, 2048), MXU_DTYPE), jax.ShapeDtypeStruct((t_total, 128), MXU_DTYPE),
                   jax.ShapeDtypeStruct((8, 2048), F32), jax.ShapeDtypeStruct((8, 128), F32)),
        grid=(nb,),
        in_specs=[pl.BlockSpec((TB, 2048), lambda i: (i, 3)), pl.BlockSpec((TB, 128), lambda i: (i, 0)),
                  _full((8, 2048)), _full((8, 128)), pl.BlockSpec((TB, 2048), lambda i: (i, 0)),
                  pl.BlockSpec((TB, D), lambda i: (jnp.minimum(i, nb - 2), 0)),
                  pl.BlockSpec((TB, 128), lambda i: (i, 0))],
        out_specs=(pl.BlockSpec((TB, 2048), lambda i: (i, 0)), pl.BlockSpec((TB, 128), lambda i: (i, 0)),
                   _full((8, 2048)), _full((8, 128))),
        sem=("arbitrary",), vmem_mb=40,
    )(p_main, p_dt, convp, dtb, dxa, dxs_skip, ddts)


def _dot2(x, m01):
    hi = x.astype(BF16)
    lo = (x - hi.astype(F32)).astype(BF16)
    f = lambda t: lax.dot_general(t, m01, (((1,), (0,)), ((), ())), preferred_element_type=F32)
    return f(hi) + f(lo)


def _head_lanes(c0, c1):
    p = lax.broadcasted_iota(jnp.int32, (128, 128), 0)
    l = lax.broadcasted_iota(jnp.int32, (128, 128), 1)
    return _b01(((l == c0) & (p < SP)) | ((l == c1) & (p >= SP)))


def _one_lane(col):
    return _b01(lax.broadcasted_iota(jnp.int32, (128, 128), 1) == col)


def _lane_pick(x, lane, col):
    return _rowsum(jnp.where(lane == col, x, 0.0))


def _ssd_chunk_common(dts, alog_ref, m01, rev):
    lane = lax.broadcasted_iota(jnp.int32, (1, 128), 1)
    arow = -jnp.exp(alog_ref[0:1, :])
    cum = _dot01(m01, dts * arow)
    tot = cum[0:1, :] if rev else cum[SC - 1:SC, :]
    return lane, arow, cum, cum.T, tot


def _ssd_fwd(xa, dts, alog, d, nb):
    t_total = xa.shape[0]
    rev = d == 1
    nch = TB // SC
    npair = SHEADS // 2

    def body(xa_ref, dts_ref, alog_ref, y_ref, sp_ref, st):
        s = pl.program_id(0)

        @pl.when(s == 0)
        def _():
            st[...] = jnp.zeros_like(st)

        mb = _tri(SC, rev)
        m01 = _b01(mb)
        lo = lax.broadcasted_iota(jnp.int32, (1, 128), 1) < SP
        rlo = lax.broadcasted_iota(jnp.int32, (128, 1), 0) < SP
        order = list(reversed(range(nch)) if rev else range(nch))
        pre = {}
        for c in order:
            rows = slice(c * SC, (c + 1) * SC)
            dts_c = dts_ref[rows, :]
            lane, arow, cum, cumt, tot = _ssd_chunk_common(dts_c, alog_ref, m01, rev)
            bgs = [_mx(xa_ref[rows, D + g * SN:D + (g + 1) * SN]) for g in range(4)]
            cgs = [_mx(xa_ref[rows, D + 512 + g * SN:D + 512 + (g + 1) * SN]) for g in range(4)]
            pairs = []
            for pr in range(npair):
                xs = xa_ref[rows, pr * 128:(pr + 1) * 128]
                cols = [16 * d + 2 * pr, 16 * d + 2 * pr + 1]
                cum_c = [_lane_pick(cum, lane, q) for q in cols]
                dt_c = [_lane_pick(dts_c, lane, q) for q in cols]
                tot_c = [_lane_pick(tot, lane, q) for q in cols]
                dtx = xs * jnp.where(lo, dt_c[0], dt_c[1])
                e1_pair = jnp.where(lo, jnp.exp(cum_c[0]), jnp.exp(cum_c[1]))
                e2_pair = jnp.where(lo, jnp.exp(tot_c[0] - cum_c[0]), jnp.exp(tot_c[1] - cum_c[1]))
                etot_col = jnp.where(rlo, jnp.exp(tot_c[0]), jnp.exp(tot_c[1]))
                decs = [jnp.where(mb, jnp.exp(cum_c[q] - cumt[cols[q]:cols[q] + 1, :]), 0.0) for q in range(2)]
                dtxq = [_mx(jnp.where(lo if q == 0 else ~lo, dtx, 0.0)) for q in range(2)]
                pairs.append(dict(e1=e1_pair, etot=etot_col, decs=decs, dtxq=dtxq, xe=_mx(dtx * e2_pair)))
            pre[c] = (bgs, cgs, pairs)
        gm = {(c, g): _nt(pre[c][1][g], pre[c][0][g]) for c in order for g in range(4)}
        upd = {(c, pr): _tn(pre[c][2][pr]["xe"], pre[c][0][pr // 2]) for c in order for pr in range(npair)}
        intra = {(c, pr): sum(_nn(gm[(c, pr // 2)] * pre[c][2][pr]["decs"][q], pre[c][2][pr]["dtxq"][q]) for q in range(2))
                 for c in order for pr in range(npair)}
        for c in order:
            rows = slice(c * SC, (c + 1) * SC)
            bgs, cgs, pairs = pre[c]
            for pr in range(npair):
                stp = st[pr]
                stb = stp.astype(sp_ref.dtype)
                sp_ref[c, pr] = stb
                y_ref[rows, pr * 128:(pr + 1) * 128] = intra[(c, pr)] + pairs[pr]["e1"] * _nt(cgs[pr // 2], stb)
                st[pr] = stp * pairs[pr]["etot"] + upd[(c, pr)]

    blk = lambda s: _blk(s, nb, rev)
    return _pcall(
        body, name=f"ssd_fwd_{d}",
        out_shape=(jax.ShapeDtypeStruct((t_total, D), F32),
                   jax.ShapeDtypeStruct((nch * nb, npair, 128, SN), MXU_DTYPE)),
        grid=(nb,),
        in_specs=[pl.BlockSpec((TB, 2048), lambda s: (blk(s), 0)), pl.BlockSpec((TB, 128), lambda s: (blk(s), 0)),
                  _full((8, 128))],
        out_specs=(pl.BlockSpec((TB, D), lambda s: (blk(s), 0)),
                   pl.BlockSpec((nch, npair, 128, SN), lambda s: (blk(s), 0, 0, 0))),
        scratch=[pltpu.VMEM((npair, 128, SN), F32)], sem=("arbitrary",), vmem_mb=40,
    )(xa, dts, alog)


def _ssd_bwd(xa, dts, alog, sprev, dy, d, nb, prev, comm=None):
    t_total = xa.shape[0]
    rev = d == 1
    nch = TB // SC
    npair = SHEADS // 2
    last = prev is not None

    def body(xa_ref, dts_ref, alog_ref, sp_ref, dy_ref, *rest):
        if last:
            dxp_ref, ddp_ref = rest[:2]
            rest = rest[2:]
        dxa_ref, ddts_ref, da_ref, dst, zc_scr = rest
        sp_id = pl.program_id(0)
        is_ctx = sp_id == nb - 1

        @pl.when(sp_id == 0)
        def _():
            dst[...] = jnp.zeros_like(dst)
            da_ref[...] = jnp.zeros_like(da_ref)
            zc_scr[...] = jnp.zeros_like(zc_scr)

        mb = _tri(SC, rev)
        m01 = _b01(mb)
        mt01 = _b01(_tri(SC, not rev))
        lo = lax.broadcasted_iota(jnp.int32, (1, 128), 1) < SP
        rlo = lax.broadcasted_iota(jnp.int32, (128, 1), 0) < SP
        order = list(range(nch) if rev else reversed(range(nch)))
        pre = {}
        for c in order:
            rows = slice(c * SC, (c + 1) * SC)
            dts_c = dts_ref[rows, :]
            lane, arow, cum, cumt, tot = _ssd_chunk_common(dts_c, alog_ref, m01, rev)
            pairs = []
            for pr in range(npair):
                xs = xa_ref[rows, pr * 128:(pr + 1) * 128]
                dyp = jnp.where(is_ctx, 0.0, dy_ref[rows, pr * 128:(pr + 1) * 128])
                cols = [16 * d + 2 * pr, 16 * d + 2 * pr + 1]
                cum_c = [_lane_pick(cum, lane, q) for q in cols]
                dt_c = [_lane_pick(dts_c, lane, q) for q in cols]
                tot_c = [_lane_pick(tot, lane, q) for q in cols]
                e1_c = [jnp.exp(cum_c[q]) for q in range(2)]
                e2_c = [jnp.exp(tot_c[q] - cum_c[q]) for q in range(2)]
                etot_c = [jnp.exp(tot_c[q]) for q in range(2)]
                dt_pair = jnp.where(lo, dt_c[0], dt_c[1])
                e1_pair = jnp.where(lo, e1_c[0], e1_c[1])
                e2_pair = jnp.where(lo, e2_c[0], e2_c[1])
                dtx = xs * dt_pair
                decs = [jnp.where(mb, jnp.exp(cum_c[q] - cumt[cols[q]:cols[q] + 1, :]), 0.0) for q in range(2)]
                dyq = [_mx(jnp.where(lo if q == 0 else ~lo, dyp, 0.0)) for q in range(2)]
                pairs.append(dict(xs=xs, dyp=dyp, cols=cols, e1_c=e1_c, e2_c=e2_c, etot_c=etot_c, dt_pair=dt_pair,
                                  e2_pair=e2_pair, etot_col=jnp.where(rlo, etot_c[0], etot_c[1]), dtx=dtx,
                                  dtxb=_mx(dtx), xeb=_mx(dtx * e2_pair), dy0b=_mx(dyp * e1_pair), decs=decs, dyq=dyq))
            pre[c] = dict(lane=lane, arow=arow, dts=dts_c, pairs=pairs, cum=cum, tot=tot,
                          bgb=[_mx(xa_ref[rows, D + g * SN:D + (g + 1) * SN]) for g in range(4)],
                          cgb=[_mx(xa_ref[rows, D + 512 + g * SN:D + 512 + (g + 1) * SN]) for g in range(4)])
        units = [(c, pr) for c in order for pr in range(npair)]
        P = lambda u: pre[u[0]]["pairs"][u[1]]
        cgu = lambda u: pre[u[0]]["cgb"][u[1] // 2]
        gm = {(c, g): _nt(pre[c]["cgb"][g], pre[c]["bgb"][g]) for c in order for g in range(4)}
        y0 = {u: _nt(cgu(u), sp_ref[u[0], u[1]]) for u in units}
        dcg_i = {u: _nn(P(u)["dy0b"], sp_ref[u[0], u[1]]) for u in units}
        dsl = {u: _tn(P(u)["dy0b"], cgu(u)) for u in units}
        w_ = {(u, q): gm[(u[0], u[1] // 2)] * P(u)["decs"][q] for u in units for q in range(2)}
        dw_ = {(u, q): jnp.where(mb, _nt(P(u)["dyq"][q], P(u)["dtxb"]), 0.0) for u in units for q in range(2)}
        ddtx_i = {(u, q): _tn(w_[(u, q)], P(u)["dyq"][q]) for u in units for q in range(2)}
        for c in order:
            rows = slice(c * SC, (c + 1) * SC)
            pc = pre[c]
            lane, arow, dts_c = pc["lane"], pc["arow"], pc["dts"]
            d1 = jnp.zeros((SC, 128), F32)
            d2 = jnp.zeros((SC, 128), F32)
            dz = jnp.zeros((SC, 128), F32)
            ddt = jnp.zeros((SC, 128), F32)
            dtot = jnp.zeros((1, 128), F32)
            dgm = [jnp.zeros((SC, SC), F32) for _ in range(4)]
            dbg = [jnp.zeros((SC, SN), F32) for _ in range(4)]
            dcg = [jnp.zeros((SC, SN), F32) for _ in range(4)]
            for pr in range(npair):
                u, g, p = (c, pr), pr // 2, pc["pairs"][pr]
                hs = _head_lanes(*p["cols"])
                dso = dst[pr]
                dsob = _mx(dso)
                dxe = _nt(pc["bgb"][g], dsob)
                dbg[g] = dbg[g] + _nn(p["xeb"], dsob)
                ddtx = dxe * p["e2_pair"]
                d2 = d2 + _dot2(dxe * p["dtx"], hs)
                dcg[g] = dcg[g] + dcg_i[u]
                d1 = d1 + _dot2(p["dyp"] * y0[u], hs)
                sprod = dso * sp_ref[c, pr].astype(F32)
                dst[pr] = dso * p["etot_col"] + dsl[u]
                for q in range(2):
                    hm = lo if q == 0 else ~lo
                    col = p["cols"][q]
                    dw = dw_[(u, q)]
                    ddtx = ddtx + jnp.where(hm, ddtx_i[(u, q)], 0.0)
                    dgm[g] = dgm[g] + dw * p["decs"][q]
                    z = dw * w_[(u, q)]
                    dz = dz + _dot2(z, _one_lane(col))
                    zc_scr[col:col + 1, :] = _colsum(z)
                    tsum = _rowsum(_colsum(sprod[q * SP:(q + 1) * SP, :]))
                    dtot = jnp.where(lane == col, tsum * p["etot_c"][q], dtot)
                dxs = ddtx * p["dt_pair"]
                ddt = ddt + _dot2(ddtx * p["xs"], hs)
                if last:
                    dxs = dxs + dxp_ref[rows, pr * 128:(pr + 1) * 128]
                dxa_ref[rows, pr * 128:(pr + 1) * 128] = dxs
            e2_all = jnp.exp(pc["tot"] - pc["cum"])
            dcum = dz - zc_scr[...].T + d1 * jnp.exp(pc["cum"]) - d2 * e2_all
            dtot = dtot + _colsum(d2 * e2_all)
            for g in range(4):
                db = dbg[g] + _tn(dgm[g], pc["cgb"][g])
                dc = dcg[g] + _nn(dgm[g], pc["bgb"][g])
                if last:
                    db = db + dxp_ref[rows, D + g * SN:D + (g + 1) * SN]
                    dc = dc + dxp_ref[rows, D + 512 + g * SN:D + 512 + (g + 1) * SN]
                dxa_ref[rows, D + g * SN:D + (g + 1) * SN] = db
                dxa_ref[rows, D + 512 + g * SN:D + 512 + (g + 1) * SN] = dc
            dla = _dot01(mt01, dcum) + dtot
            ddt = ddt + dla * arow
            da_ref[0:1, :] += _colsum(dla * dts_c)
            if last:
                ddt = ddt + ddp_ref[rows, :]
            ddts_ref[rows, :] = ddt

    blk = lambda s: _blk(nb - 1 - s, nb, rev)
    in_specs = [pl.BlockSpec((TB, 2048), lambda s: (blk(s), 0)), pl.BlockSpec((TB, 128), lambda s: (blk(s), 0)),
                _full((8, 128)), pl.BlockSpec((nch, npair, 128, SN), lambda s: (blk(s), 0, 0, 0)),
                pl.BlockSpec((TB, D), lambda s: (jnp.minimum(blk(s), nb - 2), 0))]
    args = [xa, dts, alog, sprev, dy]
    if last:
        in_specs += [pl.BlockSpec((TB, 2048), lambda s: (blk(s), 0)), pl.BlockSpec((TB, 128), lambda s: (blk(s), 0))]
        args += list(prev)
    call = dict(
        body=body, args=args, name=f"ssd_bwd_{d}",
        out_shape=(jax.ShapeDtypeStruct((t_total, 2048), F32), jax.ShapeDtypeStruct((t_total, 128), F32),
                   jax.ShapeDtypeStruct((8, 128), F32)),
        grid=(nb,), in_specs=in_specs,
        out_specs=(pl.BlockSpec((TB, 2048), lambda s: (blk(s), 0)), pl.BlockSpec((TB, 128), lambda s: (blk(s), 0)),
                   _full((8, 128))),
        scratch=[pltpu.VMEM((npair, 128, SN), F32), pltpu.VMEM((128, 128), F32)], sem=("arbitrary",), vmem_mb=48)
    return _run(_carry(call, comm, lambda: (pl.program_id(0) == 0, pl.program_id(0) == nb - 1)))


def _readout(o, g, yy, z, vec_ref):
    hg, ss, keep = [], [], []
    for h in range(NH):
        cs = slice(h * HF, (h + 1) * HF)
        oh = o[:, cs]
        r = lax.rsqrt(jnp.mean(oh * oh, axis=1, keepdims=True) + EPS)
        hg.append(oh * r * vec_ref[0:1, cs] * _silu(g[:, cs]))
        keep.append(r)
    u = yy * _silu(z)
    for gi in range(4):
        cs = slice(gi * 256, (gi + 1) * 256)
        ug = u[:, cs]
        r = lax.rsqrt(jnp.mean(ug * ug, axis=1, keepdims=True) + EPS)
        ss.append(ug * r * vec_ref[2:3, cs])
        keep.append(r)
    return jnp.concatenate(hg, axis=1), jnp.concatenate(ss, axis=1), keep, u


def _mix_out(o_f, o_b, p_main, y_f, y_b, xa, x, vecs, w_out):
    n = x.shape[0]

    def body(of_ref, ob_ref, g_ref, z_ref, yf_ref, yb_ref, xs_ref, x_ref, vec_ref, w_ref,
             ymix_ref, ylat_ref, h1_ref, u2_ref):
        o = of_ref[...] + ob_ref[...]
        yy = yf_ref[...] + yb_ref[...] + vec_ref[1:2, :] * xs_ref[...]
        hg, ss, _, _ = _readout(o, g_ref[...].astype(F32), yy, z_ref[...].astype(F32), vec_ref)
        ymix = jnp.concatenate([hg, ss], axis=1).astype(MXU_DTYPE)
        ymix_ref[...] = ymix
        ylat = _nn(ymix, w_ref[...])
        ylat_ref[...] = ylat
        h1 = x_ref[...] + vec_ref[3:4, :] * ylat
        h1_ref[...] = h1
        r = lax.rsqrt(jnp.mean(h1 * h1, axis=1, keepdims=True) + EPS)
        u2_ref[...] = ((h1 * r * vec_ref[6:7, :]) * vec_ref[4:5, :] + vec_ref[5:6, :]).astype(MXU_DTYPE)

    row = lambda j: (lambda i: (i, j))
    return _pcall(
        body, name="mix_out",
        out_shape=(jax.ShapeDtypeStruct((n, 2 * D), MXU_DTYPE), jax.ShapeDtypeStruct((n, D), F32),
                   jax.ShapeDtypeStruct((n, D), F32), jax.ShapeDtypeStruct((n, D), MXU_DTYPE)),
        grid=(n // TB,),
        in_specs=[pl.BlockSpec((TB, D), row(0)), pl.BlockSpec((TB, D), row(0)), pl.BlockSpec((TB, D), row(4)),
                  pl.BlockSpec((TB, D), row(5)), pl.BlockSpec((TB, D), row(0)), pl.BlockSpec((TB, D), row(0)),
                  pl.BlockSpec((TB, D), row(0)), pl.BlockSpec((TB, D), row(0)), _full((8, D)), _full((2 * D, D))],
        out_specs=(pl.BlockSpec((TB, 2 * D), row(0)), pl.BlockSpec((TB, D), row(0)), pl.BlockSpec((TB, D), row(0)),
                   pl.BlockSpec((TB, D), row(0))),
        sem=("parallel",), vmem_mb=48,
    )(o_f, o_b, p_main, p_main, y_f, y_b, xa, x, vecs, w_out)


def _mix_bwd(dylat, o_f, o_b, p_main, y_f, y_b, xa, vecs, w_out):
    n = dylat.shape[0]
    t_total = p_main.shape[0]
    nlat = n // TB

    def body(*refs):
        dg_ref, dz_ref, acc_ref = refs[11], refs[13], refs[15]
        i = pl.program_id(0)

        @pl.when(i == 0)
        def _():
            acc_ref[...] = jnp.zeros_like(acc_ref)

        @pl.when(i < nlat)
        def _():
            compute(*refs)

        @pl.when(i == nlat)
        def _():
            dg_ref[...] = jnp.zeros_like(dg_ref)
            dz_ref[...] = jnp.zeros_like(dz_ref)

    def compute(dyl_ref, of_ref, ob_ref, g_ref, z_ref, yf_ref, yb_ref, xs_ref, vec_ref, w_ref,
                do_ref, dg_ref, dys_ref, dz_ref, dxs_ref, acc_ref):
        dymix = _nt(dyl_ref[...], w_ref[...])
        o = of_ref[...] + ob_ref[...]
        g = g_ref[...].astype(F32)
        z = z_ref[...].astype(F32)
        xs = xs_ref[...]
        yy = yf_ref[...] + yb_ref[...] + vec_ref[1:2, :] * xs
        _, _, keep, u = _readout(o, g, yy, z, vec_ref)
        do_l, dg_l = [], []
        for h in range(NH):
            cs = slice(h * HF, (h + 1) * HF)
            oh, gh, r, wv = o[:, cs], g[:, cs], keep[h], vec_ref[0:1, cs]
            dhg = dymix[:, cs]
            xh = oh * r
            dn = dhg * _silu(gh)
            dg_l.append(dhg * xh * wv * _dsilu(gh))
            acc_ref[0:1, cs] += _colsum(dn * xh)
            dxh = dn * wv
            do_l.append(r * (dxh - xh * jnp.mean(dxh * xh, axis=1, keepdims=True)))
        du_l = []
        for gi in range(4):
            cs = slice(gi * 256, (gi + 1) * 256)
            ug, r, wv = u[:, cs], keep[NH + gi], vec_ref[2:3, cs]
            dss = dymix[:, D + gi * 256:D + (gi + 1) * 256]
            xh = ug * r
            acc_ref[2:3, cs] += _colsum(dss * xh)
            dxh = dss * wv
            du_l.append(r * (dxh - xh * jnp.mean(dxh * xh, axis=1, keepdims=True)))
        du = jnp.concatenate(du_l, axis=1)
        dyy = du * _silu(z)
        do_ref[...] = jnp.concatenate(do_l, axis=1)
        dg_ref[...] = jnp.concatenate(dg_l, axis=1).astype(dg_ref.dtype)
        dys_ref[...] = dyy
        dz_ref[...] = (du * yy * _dsilu(z)).astype(dz_ref.dtype)
        dxs_ref[...] = dyy * vec_ref[1:2, :]
        acc_ref[1:2, :] += _colsum(dyy * xs)

    row = lambda j: (lambda i: (jnp.minimum(i, nlat - 1), j))
    lat = pl.BlockSpec((TB, D), row(0))
    tok = pl.BlockSpec((TB, D), lambda i: (i, 0))
    return _pcall(
        body, name="mix_bwd",
        out_shape=(jax.ShapeDtypeStruct((n, D), F32), jax.ShapeDtypeStruct((t_total, D), MXU_DTYPE),
                   jax.ShapeDtypeStruct((n, D), F32), jax.ShapeDtypeStruct((t_total, D), MXU_DTYPE),
                   jax.ShapeDtypeStruct((n, D), F32), jax.ShapeDtypeStruct((8, D), F32)),
        grid=(t_total // TB,),
        in_specs=[lat, lat, lat, pl.BlockSpec((TB, D), row(4)), pl.BlockSpec((TB, D), row(5)), lat, lat, lat,
                  _full((8, D)), _full((2 * D, D))],
        out_specs=(lat, tok, lat, tok, lat, _full((8, D))),
        sem=("arbitrary",), vmem_mb=48,
    )(dylat, o_f, o_b, p_main, p_main, y_f, y_b, xa, vecs, w_out)


def _ffn_up(u2, w_gate, w_up):
    n = u2.shape[0]
    tb = 1024

    def body(u_ref, wg_ref, wu_ref, g_ref, up_ref, a_ref):
        uv = u_ref[...]
        gt = _nt(uv, wg_ref[...])
        upv = _nt(uv, wu_ref[...])
        g_ref[...] = gt.astype(g_ref.dtype)
        up_ref[...] = upv.astype(up_ref.dtype)
        a_ref[...] = (_silu(gt) * upv).astype(a_ref.dtype)

    blk = pl.BlockSpec((tb, FSL), lambda j, i: (i, j))
    wblk = pl.BlockSpec((None, FSL, D), lambda j, i: (j, 0, 0))
    return _pcall(
        body, name="ffn_up",
        out_shape=(jax.ShapeDtypeStruct((n, DFFP), MXU_DTYPE),) * 3,
        grid=(4, n // tb), in_specs=[pl.BlockSpec((tb, D), lambda j, i: (i, 0)), wblk, wblk],
        out_specs=(blk, blk, blk), sem=("parallel", "parallel"), vmem_mb=48,
    )(u2, w_gate, w_up)


def _ffn_down_loss(act, w_down, h1, tgt, vecs):
    n = act.shape[0]
    tb = 512

    def body(a_ref, w_ref, h1_ref, t_ref, vec_ref, dh2_ref, dffn_ref, acc_ref):
        i = pl.program_id(0)

        @pl.when(i == 0)
        def _():
            acc_ref[...] = jnp.zeros_like(acc_ref)

        g2 = vec_ref[0:1, :]
        fw = vec_ref[1:2, :]
        nsub = 4
        sb = tb // nsub
        wv = w_ref[...]
        ffns = [_nn(a_ref[r_ * sb:(r_ + 1) * sb, :], wv) for r_ in range(nsub)]
        for r_ in range(nsub):
            rows = slice(r_ * sb, (r_ + 1) * sb)
            ffn = ffns[r_]
            h2 = h1_ref[rows, :] + g2 * ffn
            r = lax.rsqrt(jnp.mean(h2 * h2, axis=1, keepdims=True) + EPS)
            xh = h2 * r
            err = xh * fw - t_ref[rows, :]
            dy = err * (1.0 / D)
            acc_ref[2:3, :] += _colsum(err * err) * (0.5 / D)
            acc_ref[1:2, :] += _colsum(dy * xh)
            dxh = dy * fw
            dh2 = r * (dxh - xh * jnp.mean(dxh * xh, axis=1, keepdims=True))
            dh2_ref[rows, :] = dh2
            dffn_ref[rows, :] = (g2 * dh2).astype(dffn_ref.dtype)
            acc_ref[0:1, :] += _colsum(dh2 * ffn)

    return _pcall(
        body, name="ffn_down_loss",
        out_shape=(jax.ShapeDtypeStruct((n, D), F32), jax.ShapeDtypeStruct((n, D), MXU_DTYPE),
                   jax.ShapeDtypeStruct((8, D), F32)),
        grid=(n // tb,),
        in_specs=[pl.BlockSpec((tb, DFFP), lambda i: (i, 0)), _full((DFFP, D)), pl.BlockSpec((tb, D), lambda i: (i, 0)),
                  pl.BlockSpec((tb, D), lambda i: (i, 0)), _full((8, D))],
        out_specs=(pl.BlockSpec((tb, D), lambda i: (i, 0)), pl.BlockSpec((tb, D), lambda i: (i, 0)), _full((8, D))),
        sem=("arbitrary",), vmem_mb=48,
    )(act, w_down, h1, tgt, vecs)


def _ffn_bwd(dffn, w_down, gate, up, w_gate_t, w_up_t):
    n = dffn.shape[0]
    tb = 1024

    def body(df_ref, wd_ref, g_ref, up_ref, wg_ref, wu_ref, dg_ref, dup_ref, du_ref):
        j = pl.program_id(1)
        nsub = 4
        sb = tb // nsub
        wd, wg, wu = wd_ref[...], wg_ref[...], wu_ref[...]
        dacts = [_nt(df_ref[r * sb:(r + 1) * sb, :], wd) for r in range(nsub)]
        parts = []
        for r in range(nsub):
            rows = slice(r * sb, (r + 1) * sb)
            gt = g_ref[rows, :].astype(F32)
            upv = up_ref[rows, :].astype(F32)
            sg = _sig(gt)
            dgt = (dacts[r] * upv * (sg * (1.0 + gt * (1.0 - sg)))).astype(MXU_DTYPE)
            dupv = (dacts[r] * (gt * sg)).astype(MXU_DTYPE)
            dg_ref[rows, :] = dgt
            dup_ref[rows, :] = dupv
            parts.append(_nn(dgt, wg) + _nn(dupv, wu))
        part = jnp.concatenate(parts, axis=0)

        @pl.when(j == 0)
        def _():
            du_ref[...] = part

        @pl.when(j > 0)
        def _():
            du_ref[...] += part

    tok = pl.BlockSpec((tb, D), lambda i, j: (i, 0))
    ffb = pl.BlockSpec((tb, FSL), lambda i, j: (i, j))
    wsl = pl.BlockSpec((None, FSL, D), lambda i, j: (j, 0, 0))
    return _pcall(
        body, name="ffn_bwd",
        out_shape=(jax.ShapeDtypeStruct((n, DFFP), MXU_DTYPE), jax.ShapeDtypeStruct((n, DFFP), MXU_DTYPE),
                   jax.ShapeDtypeStruct((n, D), F32)),
        grid=(n // tb, 4),
        in_specs=[tok, pl.BlockSpec((FSL, D), lambda i, j: (j, 0)), ffb, ffb, wsl, wsl],
        out_specs=(ffb, ffb, tok), sem=("parallel", "arbitrary"), vmem_mb=48,
    )(dffn, w_down, gate, up, w_gate_t, w_up_t)


def _ffn_norm_bwd(du, h1, ylat, dh2, vecs):
    n = du.shape[0]
    tb = 512

    def body(du_ref, h1_ref, yl_ref, dh2_ref, vec_ref, dh1_ref, dyl_ref, acc_ref):
        @pl.when(pl.program_id(0) == 0)
        def _():
            acc_ref[...] = jnp.zeros_like(acc_ref)

        duv = du_ref[...]
        h1 = h1_ref[...]
        r = lax.rsqrt(jnp.mean(h1 * h1, axis=1, keepdims=True) + EPS)
        xh = h1 * r
        nw = vec_ref[2:3, :]
        acc_ref[0:1, :] += _colsum(duv)
        acc_ref[1:2, :] += _colsum(duv * xh * nw)
        dn = duv * vec_ref[1:2, :]
        acc_ref[2:3, :] += _colsum(dn * xh)
        dxh = dn * nw
        dh1 = dh2_ref[...] + r * (dxh - xh * jnp.mean(dxh * xh, axis=1, keepdims=True))
        dh1_ref[...] = dh1
        dyl_ref[...] = (vec_ref[0:1, :] * dh1).astype(dyl_ref.dtype)
        acc_ref[3:4, :] += _colsum(dh1 * yl_ref[...])

    tok = pl.BlockSpec((tb, D), lambda i: (i, 0))
    return _pcall(
        body, name="ffn_norm_bwd",
        out_shape=(jax.ShapeDtypeStruct((n, D), F32), jax.ShapeDtypeStruct((n, D), MXU_DTYPE),
                   jax.ShapeDtypeStruct((8, D), F32)),
        grid=(n // tb,), in_specs=[tok, tok, tok, tok, _full((8, D))], out_specs=(tok, tok, _full((8, D))),
        sem=("arbitrary",), vmem_mb=40,
    )(du, h1, ylat, dh2, vecs)


def _deep_rows(rows):
    return max(r for r in range(128, 2305, 128) if rows % r == 0)


def _dw(a, b, name):
    tn_rows = a.shape[0]
    bt = _deep_rows(tn_rows)
    kk, nn_ = a.shape[1], b.shape[1]
    bk = 1024 if kk % 1024 == 0 else kk
    bn = 1024 if nn_ % 1024 == 0 else nn_
    nt = tn_rows // bt

    def body(a_ref, b_ref, o_ref, acc):
        t = pl.program_id(2)
        part = _tn(a_ref[...], b_ref[...])

        @pl.when(t == 0)
        def _():
            acc[...] = part

        @pl.when(t > 0)
        def _():
            acc[...] += part

        @pl.when(t == nt - 1)
        def _():
            o_ref[...] = acc[...].astype(o_ref.dtype)

    return _pcall(
        body, name=name, out_shape=jax.ShapeDtypeStruct((kk, nn_), MXU_DTYPE), grid=(kk // bk, nn_ // bn, nt),
        in_specs=[pl.BlockSpec((bt, bk), lambda i, j, t: (t, i)), pl.BlockSpec((bt, bn), lambda i, j, t: (t, j))],
        out_specs=pl.BlockSpec((bk, bn), lambda i, j, t: (i, j)), scratch=[pltpu.VMEM((bk, bn), F32)],
        sem=("parallel", "parallel", "arbitrary"), vmem_mb=40,
    )(a, b)


def _dw_in(segs, u_all, name):
    tiles = []
    for m, s_ in enumerate(segs):
        tiles += [(m, h) for h in range(s_.shape[1] // D)]
    ntile = len(tiles)
    t_total = u_all.shape[0]
    bt = _deep_rows(t_total)
    nt = t_total // bt

    def body(u_ref, *refs):
        seg_refs, o_ref, acc = refs[:len(segs)], refs[len(segs)], refs[len(segs) + 1]
        n, t = pl.program_id(0), pl.program_id(1)
        for k, (m, _) in enumerate(tiles):
            @pl.when(n == k)
            def _(m=m):
                part = _tn(seg_refs[m][...], u_ref[...])

                @pl.when(t == 0)
                def _():
                    acc[...] = part

                @pl.when(t > 0)
                def _():
                    acc[...] += part

        @pl.when(t == nt - 1)
        def _():
            o_ref[...] = acc[...].astype(o_ref.dtype)

    def seg_spec(m):
        ks = [k for k, (mm, _) in enumerate(tiles) if mm == m]
        lo, hi = ks[0], ks[-1]
        on = lambda n: (n >= lo) & (n <= hi)
        return pl.BlockSpec((bt, D), lambda n, t: (jnp.where(on(n), t, 0), jnp.where(on(n), n - lo, 0)))

    return _pcall(
        body, name=name, out_shape=jax.ShapeDtypeStruct((1, ntile * D, D), MXU_DTYPE), grid=(ntile, nt),
        in_specs=[pl.BlockSpec((bt, D), lambda n, t: (t, 0))] + [seg_spec(m) for m in range(len(segs))],
        out_specs=pl.BlockSpec((None, D, D), lambda n, t: (0, n, 0)),
        scratch=[pltpu.VMEM((D, D), F32)], sem=("parallel", "arbitrary"), vmem_mb=56,
    )(u_all, *segs)


def _du_prenorm_bwd(segs, ddt, wi_main, wi_tail, xin, mods, dres, row_off, tb, name, comm=None):
    n = xin.shape[0]
    nt = n // tb
    off = row_off // tb
    has_dx = dres is not None

    def body(*refs):
        seg_refs = refs[:7]
        ddt_ref, w_ref, wb_ref, wdt_ref, x_ref, mod_ref = refs[7:13]
        rest = refs[13:]
        if has_dx:
            dres_ref, dx_ref, acc_ref, du_scr = rest
        else:
            acc_ref, du_scr = rest
        j, i = pl.program_id(0), pl.program_id(1)
        rows = pl.ds(pl.multiple_of(i * tb, tb), tb)

        @pl.when((i == 0) & (j == 0))
        def _():
            acc_ref[...] = jnp.zeros_like(acc_ref)

        @pl.when(j == 0)
        def _():
            du_scr[rows, :] = _nn(ddt_ref[...], wdt_ref[...])

        for k in range(8):
            if not has_dx and k in (4, 5):
                continue

            @pl.when(j == k)
            def _(k=k):
                sv = seg_refs[min(k, 6)][...]
                part = _nn(sv, w_ref[...])
                if k in (2, 4, 6):
                    part = part + _nn(sv[:, 0:WTAIL], wb_ref[...])
                du_scr[rows, :] += part

        @pl.when(j == 7)
        def _():
            du = du_scr[rows, :]
            xv = x_ref[...]
            r = lax.rsqrt(jnp.mean(xv * xv, axis=1, keepdims=True) + EPS)
            xh = xv * r
            nw = mod_ref[1:2, :]
            acc_ref[0:1, :] += _colsum(du)
            acc_ref[1:2, :] += _colsum(du * xh * nw)
            dn = du * mod_ref[0:1, :]
            acc_ref[2:3, :] += _colsum(dn * xh)
            if has_dx:
                dxh = dn * nw
                dx_ref[...] = dres_ref[...] + r * (dxh - xh * jnp.mean(dxh * xh, axis=1, keepdims=True))

    def seg_spec(k):
        if k < 6:
            return pl.BlockSpec((tb, D), lambda j, i: (jnp.where(j == k, i + off, 0), 0))
        return pl.BlockSpec((tb, D), lambda j, i: (jnp.where(j >= 6, i + off, 0), jnp.where(j >= 6, j - 6, 0)))

    last = pl.BlockSpec((tb, D), lambda j, i: (jnp.where(j == 7, i, 0), 0))
    in_specs = [seg_spec(k) for k in range(7)]
    in_specs += [pl.BlockSpec((tb, 128), lambda j, i: (jnp.where(j == 0, i + off, 0), 0))] + _w_specs()
    in_specs += [last, _full((8, D))]
    args = list(segs) + [ddt, wi_main, wi_tail, wi_tail, xin, mods]
    out_shape = [jax.ShapeDtypeStruct((8, D), F32)]
    out_specs = [_full((8, D))]
    if has_dx:
        in_specs.append(last)
        args.append(dres)
        out_shape.insert(0, jax.ShapeDtypeStruct((n, D), F32))
        out_specs.insert(0, last)
    call = dict(body=body, args=args, name=name, out_shape=tuple(out_shape), grid=(8, nt), in_specs=in_specs,
                out_specs=tuple(out_specs), scratch=[pltpu.VMEM((n, D), F32)], sem=("arbitrary", "arbitrary"),
                vmem_mb=56)
    steps = lambda: ((pl.program_id(0) == 0) & (pl.program_id(1) == 0),
                     (pl.program_id(0) == 7) & (pl.program_id(1) == nt - 1))
    return _run(_carry(call, comm, steps))


def _sum8(v):
    def body(v_ref, o_ref):
        acc = v_ref[0]
        for k in range(1, 8):
            acc = acc + v_ref[k]
        o_ref[...] = acc

    return _pcall(body, name="small_sum", out_shape=jax.ShapeDtypeStruct(v.shape[1:], F32),
                  in_specs=[pl.BlockSpec(memory_space=pltpu.VMEM)], out_specs=pl.BlockSpec(memory_space=pltpu.VMEM))(v)


def _adamw(w, m, v, g, name):
    lead = w.ndim == 3
    rows, cols = w.shape[-2:]
    rb = 256 if rows % 256 == 0 else (352 if rows % 352 == 0 else rows)
    c1 = 1.0 - B1 ** STEP
    c2 = 1.0 - B2 ** STEP

    def body(w_ref, m_ref, v_ref, g_ref, d_ref, nm_ref, nv_ref):
        gv = g_ref[...]
        mn = B1 * m_ref[...] + (1.0 - B1) * gv
        vn = B2 * v_ref[...] + (1.0 - B2) * (gv * gv)
        nm_ref[...] = mn
        nv_ref[...] = vn
        d_ref[...] = -LR * ((mn / c1) / (jnp.sqrt(vn / c2) + AEPS) + WD * w_ref[...])

    if rb == rows and rows > 1024:
        cb, steps = 256, cols // 256
        gspec = pl.BlockSpec((rows, cb), lambda i: (0, i))
        spec = pl.BlockSpec((None, rows, cb), lambda i: (0, 0, i)) if lead else gspec
    else:
        steps = rows // rb
        gspec = pl.BlockSpec((rb, cols), lambda i: (i, 0))
        spec = pl.BlockSpec((None, rb, cols), lambda i: (0, i, 0)) if lead else gspec
    return _pcall(
        body, name=name, out_shape=(jax.ShapeDtypeStruct(w.shape, F32),) * 3, grid=(steps,),
        in_specs=[spec] * 3 + [gspec], out_specs=(spec,) * 3, sem=("parallel",), vmem_mb=40,
    )(w, m, v, g)


def _rows(v, n):
    f = v.reshape(-1)
    return jnp.pad(f, (0, n * D - f.shape[0])).reshape(n, D)


def kernel(x, c, ctx, c_ctx, w_ada, b_ada, norm_mix, w_in, conv_w, conv_b, ssd_a_log, ssd_dt_bias, ssd_d, ssd_norm, hgrn_lb_raw, hgrn_norm, w_out, norm_ffn, w_gate, w_up, w_down, final_norm, loss_target, m_c_ctx, m_w_ada, m_b_ada, m_norm_mix, m_w_in, m_conv_w, m_conv_b, m_ssd_a_log, m_ssd_dt_bias, m_ssd_d, m_ssd_norm, m_hgrn_lb_raw, m_hgrn_norm, m_w_out, m_norm_ffn, m_w_gate, m_w_up, m_w_down, m_final_norm, v_c_ctx, v_w_ada, v_b_ada, v_norm_mix, v_w_in, v_conv_w, v_conv_b, v_ssd_a_log, v_ssd_dt_bias, v_ssd_d, v_ssd_norm, v_hgrn_lb_raw, v_hgrn_norm, v_w_out, v_norm_ffn, v_w_gate, v_w_up, v_w_down, v_final_norm):
    ix, iy, ic = lax.axis_index("x"), lax.axis_index("y"), lax.axis_index("c")
    chip = 2 * ix + iy
    me = 2 * chip + ic
    xl, xc, tgt = x[0], ctx[0], loss_target[0]
    n_lat, n_ctx = xl.shape[0], xc.shape[0]
    assert n_ctx == TB and n_lat % 1024 == 0
    t_total = n_lat + n_ctx
    nb = t_total // TB

    pack = jnp.concatenate([c, hgrn_lb_raw.reshape(1, D), _rows(conv_w[0], 3), jnp.zeros((3, D), F32)], axis=0)
    gath = _allgather8(pack, "small_gather").reshape(8, 8, D)
    c_all = gath[:, 0]
    lbraw_full = gath[0::2, 1].reshape(4, 2, 2, 256).transpose(1, 2, 0, 3).reshape(4, D)
    convw_full = gath[0::2, 2:5].reshape(4, 3 * D)[:, :KCONV * 512].reshape(4, KCONV, 512).transpose(1, 0, 2)
    convw_full = convw_full.reshape(KCONV, 2048)
    lbraw8 = jnp.pad(lbraw_full, ((0, 4), (0, 0)))
    convp = jnp.concatenate([convw_full, conv_b, jnp.zeros((2, 2048), F32)], axis=0)
    dtb = jnp.pad(ssd_dt_bias.reshape(1, 32), ((0, 7), (0, 96)))
    alog = jnp.pad(ssd_a_log.reshape(1, 32), ((0, 7), (0, 96)))

    araw = jnp.concatenate([c_all, c_ctx.reshape(1, D), jnp.zeros((7, D), F32)], axis=0)
    ncol_ada = w_ada.shape[2]
    b_shard = lax.dynamic_slice(b_ada, (0, chip * ncol_ada), (1, ncol_ada))
    mod_shard = _ada_fwd(araw, w_ada[0], b_shard)
    mod_all = _allgather8(mod_shard, "mod_gather").reshape(8, 16, ncol_ada)[0::2]
    mod_full = mod_all.transpose(1, 0, 2).reshape(16, 4 * ncol_ada)
    my_mod = lax.dynamic_slice(mod_full, (me, 0), (1, 6 * D)).reshape(6, D)
    sh1, sc1, g1, sh2, sc2, g2 = (my_mod[k:k + 1] for k in range(6))
    csh1, csc1 = mod_full[8:9, 0:D], mod_full[8:9, D:2 * D]

    tr = lambda a: jnp.swapaxes(a, -1, -2)
    shift = [functools.partial(jnp.pad, pad_width=((8 * k, WSL + WTAIL - NSH - 8 * k), (0, 0))) for k in range(4)]
    slab = lax.switch(chip, shift, tr(w_in[0]).astype(MXU_DTYPE))
    padrows = lambda a: jnp.pad(a, ((0, FSL - DFF // 4), (0, 0))).astype(MXU_DTYPE)
    shards = [slab[:WSL], slab[WSL:], w_out[0].astype(MXU_DTYPE), padrows(tr(w_gate[0])), padrows(tr(w_up[0])),
              padrows(w_down[0])]
    own = lambda g_, s_: lax.dynamic_update_slice(g_, s_[None], (chip, 0, 0))
    wi_main, wi_tail = (own(g_, s_) for g_, s_ in zip(_weights_allgather(shards[:2]), shards[:2]))

    zrow = jnp.zeros((1, D), F32)
    mods_lat = jnp.concatenate([1.0 + sc1, sh1, norm_mix, zrow, zrow, zrow, zrow, zrow], axis=0)
    mods_ctx = jnp.concatenate([1.0 + csc1, csh1, norm_mix, zrow, zrow, zrow, zrow, zrow], axis=0)
    outs = _inproj(xl, mods_lat, wi_main, wi_tail, t_total, 1024, 0, None, "inproj_lat",
                   comm=_comm_gather(shards[2:5]))
    wo_g, wg_g, wu_g = (own(g_, s_) for g_, s_ in zip(outs[3:], shards[2:5]))
    w_out_f = wo_g.reshape(2 * D, D)
    p_main, p_dt, u_all = _inproj(xc, mods_ctx, wi_main, wi_tail, t_total, TB, nb - 1, outs[:3], "inproj_ctx")

    o_f, hs_f, wd_g = _hgrn_fwd(p_main, lbraw8, 0, nb, comm=_comm_gather(shards[5:]))
    w_down_f = own(wd_g, shards[5]).reshape(DFFP, D)
    o_b, hs_b = _hgrn_fwd(p_main, lbraw8, 1, nb)
    xa, dts = _ssd_prep(p_main, p_dt, convp, dtb, nb)
    y_f, ss_f = _ssd_fwd(xa, dts, alog, 0, nb)
    y_b, ss_b = _ssd_fwd(xa, dts, alog, 1, nb)

    vec_mix = jnp.concatenate([jnp.tile(hgrn_norm, (1, NH)), jnp.repeat(ssd_d, SP, axis=1), ssd_norm, g1, 1.0 + sc2,
                               sh2, norm_ffn, zrow], axis=0)
    ymix, ylat, h1, u2 = _mix_out(o_f, o_b, p_main, y_f, y_b, xa, xl, vec_mix, w_out_f)
    gate, up, act = _ffn_up(u2, wg_g, wu_g)
    vec_loss = jnp.concatenate([g2, final_norm.reshape(1, D)] + [zrow] * 6, axis=0)
    dh2, dffn, acc_loss = _ffn_down_loss(act, w_down_f, h1, tgt, vec_loss)

    core_arr = jnp.reshape(ic, (1,)).astype(jnp.int32)
    chip_arr = jnp.reshape(chip, (1,)).astype(jnp.int32)
    every = (0, 4)

    def pair_stage(gs, tag):
        return list(_pair_sum(gs, _pair_exchange(gs, "grads_pair_exchange_" + tag), core_arr, "grads_pair_sum_" + tag))

    vec_ffn = jnp.concatenate([g1, 1.0 + sc2, norm_ffn] + [zrow] * 5, axis=0)
    dgate, dup, du2 = _ffn_bwd(dffn, w_down_f, gate, up, wg_g, wu_g)
    dh1, dylat, acc_ffn = _ffn_norm_bwd(du2, h1, ylat, dh2, vec_ffn)
    gw_down = _dw(act, dffn, "dw_down").reshape(4, FSL, D)
    gw_gate = _dw(dgate, u2, "dw_gate").reshape(4, FSL, D)
    gw_up = _dw(dup, u2, "dw_up").reshape(4, FSL, D)
    do, dgr, dys, dzr, dxs_skip, acc_mix = _mix_bwd(dylat, o_f, o_b, p_main, y_f, y_b, xa, vec_mix, w_out_f)
    gw_out = _dw(ymix, dylat, "dw_out").reshape(4, D // 2, D)
    pair_a, dests_a = pair_stage([gw_gate, gw_up], "a1") + pair_stage([gw_down, gw_out], "a2"), [every] * 4

    res = _hgrn_bwd(p_main, lbraw8, hs_f, do, 0, nb, None, comm=_comm_exchange(pair_a[:2], dests_a[:2]))
    (dq0, dff, dv0, dlb_f), recv_a = res[:4], list(res[4:])
    res = _hgrn_bwd(p_main, lbraw8, hs_b, do, 1, nb, (dq0, dv0), comm=_comm_exchange(pair_a[2:], dests_a[2:]))
    (dq, dfb, dv, dlb_b), recv_a = res[:4], recv_a + list(res[4:])
    gw_in = [_dw_in([dq, dff], u_all, "dw_in_0"), _dw_in([dfb, dv], u_all, "dw_in_1"),
             _dw_in([dgr, dzr], u_all, "dw_in_2")]
    pair_b, dests_b = pair_stage(gw_in, "b"), [(0, 1), (1, 2), (2, 3)]

    res = _ssd_bwd(xa, dts, alog, ss_f, dys, 0, nb, None, comm=_comm_exchange(pair_b, dests_b))
    (dxa0, ddts0, da_f), recv_b = res[:3], list(res[3:])
    dxa, ddts, da_b = _ssd_bwd(xa, dts, alog, ss_b, dys, 1, nb, (dxa0, ddts0))
    dxbc, ddt, acc_conv, acc_dtb = _ssd_prep_bwd(p_main, p_dt, convp, dtb, dxa, dxs_skip, ddts, nb)
    gw_in.append(_dw_in([dxbc], u_all, "dw_in_3"))
    gw_in_dt = _dw(ddt, u_all, "dw_in_dt")
    gw_in_tail = jnp.concatenate([g_[:, 0:WTAIL, :] for g_ in gw_in[1:]] + [gw_in_dt[None]], axis=0)
    pair_c, dests_c = pair_stage([gw_in[3], gw_in_tail], "c"), [(3, 4), every]

    segs = [dq, dff, dfb, dv, dgr, dzr, dxbc]
    bmods_lat = jnp.concatenate([1.0 + sc1, norm_mix] + [zrow] * 6, axis=0)
    bmods_ctx = jnp.concatenate([1.0 + csc1, norm_mix] + [zrow] * 6, axis=0)
    res = _du_prenorm_bwd(segs, ddt, wi_main, wi_tail, xl, bmods_lat, dh1, 0, 512, "du_lat",
                          comm=_comm_exchange(pair_c, dests_c))
    (grad_x, acc_lat), recv_c = res[:2], list(res[2:])
    (acc_ctx,) = _du_prenorm_bwd(segs, ddt, wi_main, wi_tail, xc, bmods_ctx, None, n_lat, TB, "du_ctx")

    mine = _chip_sum(pair_b + pair_c + pair_a, recv_b + recv_c + recv_a, chip_arr, dests_b + dests_c + dests_a,
                     [0, 0, 0, 0, 1, 3, 4, 5, 2])
    theirs = _pair_swap(mine)
    whole = [jnp.concatenate([jnp.where(ic == 0, m_, t_), jnp.where(ic == 0, t_, m_)], axis=0)
             for m_, t_ in zip(mine, theirs)]
    g_w_in = lax.dynamic_slice(jnp.concatenate(whole[0:2], axis=0), (8 * chip, 0), (NSH, D))
    g_w_out = whole[2]
    g_w_gate = whole[3][:DFF // 4]
    g_w_up = whole[4][:DFF // 4]
    g_w_down = whole[5][:DFF // 4]

    dmod_lat = jnp.concatenate([acc_lat[0:2], acc_ffn[3:4], acc_ffn[0:2], acc_loss[0:1]], axis=0)
    misc = jnp.concatenate([(da_f + da_b)[0, :32], jnp.zeros((96,), F32), acc_dtb[0, :32], jnp.zeros((96,), F32),
                            jnp.sum(acc_loss[2]).reshape(1), jnp.zeros((D - 257,), F32)]).reshape(1, D)
    sv = jnp.concatenate([
        dmod_lat, acc_ctx[0:2], (acc_lat[2:3] + acc_ctx[2:3]), acc_ffn[2:3], acc_loss[1:2], acc_mix[2:3],
        acc_mix[0:1], acc_mix[1:2], dlb_f[0:1], dlb_b[0:1], acc_conv[0:6].reshape(12, D), misc,
        jnp.zeros((3, D), F32)], axis=0)
    sv_all = _allgather8(sv, "small_grads_gather").reshape(8, 32, D)
    ssum = _sum8(sv_all)
    dmod_rows = sv_all[:, 0:6].reshape(8, 6 * D)
    dmod_ctx_row = jnp.concatenate([ssum[6:8].reshape(1, 2 * D), jnp.zeros((1, 4 * D), F32)], axis=1)
    dmod_full = jnp.concatenate([dmod_rows, dmod_ctx_row, jnp.zeros((7, 6 * D), F32)], axis=0)
    grad_b_ada = jnp.sum(dmod_full, axis=0, keepdims=True)
    dmod_shard = lax.dynamic_slice(dmod_full, (0, chip * ncol_ada), (16, ncol_ada))
    g_w_ada, da_part = _ada_bwd(araw, dmod_shard, w_ada[0])
    da_all = _allgather8(da_part, "ada_ctx_gather").reshape(8, 16, D)[0::2, 8]
    cc = c_ctx.reshape(1, D)
    grad_c_ctx = (jnp.sum(da_all, axis=0, keepdims=True) * _dsilu(cc)).reshape(D)

    grad_norm_mix, grad_norm_ffn, grad_final_norm = ssum[8:9], ssum[9:10], ssum[10].reshape(D)
    grad_ssd_norm = ssum[11:12]
    grad_hgrn_norm = jnp.sum(ssum[12].reshape(NH, HF), axis=0, keepdims=True)
    grad_ssd_d = jnp.sum(ssum[13].reshape(SHEADS, SP), axis=1).reshape(1, SHEADS)
    lb_full = _sig(lbraw_full[0:2] - lbraw_full[2:4])
    dr0 = ssum[14:16] * lb_full * (1.0 - lb_full)
    grad_lb_full = jnp.stack([dr0, -dr0], axis=0)
    grad_lb = lax.dynamic_slice(grad_lb_full, (0, 0, chip * 256), (2, 2, 256))
    grad_conv_w = lax.dynamic_slice(ssum[16:26].reshape(KCONV, 2048), (0, chip * 512), (KCONV, 512)).reshape(1, KCONV, 512)
    grad_conv_b = ssum[26:28].reshape(1, 2048)
    a_val = -jnp.exp(ssd_a_log)
    grad_a_log = ssum[28, 0:32].reshape(1, 2, SHEADS) * a_val
    grad_dt_bias = ssum[28, 128:160].reshape(1, 2, SHEADS)
    loss = ssum[28, 256]

    small_w = [c_ctx, b_ada, norm_mix, conv_w, conv_b, ssd_a_log, ssd_dt_bias, ssd_d, ssd_norm, hgrn_lb_raw,
               hgrn_norm, norm_ffn, final_norm]
    small_m = [m_c_ctx, m_b_ada, m_norm_mix, m_conv_w, m_conv_b, m_ssd_a_log, m_ssd_dt_bias, m_ssd_d, m_ssd_norm,
               m_hgrn_lb_raw, m_hgrn_norm, m_norm_ffn, m_final_norm]
    small_v = [v_c_ctx, v_b_ada, v_norm_mix, v_conv_w, v_conv_b, v_ssd_a_log, v_ssd_dt_bias, v_ssd_d, v_ssd_norm,
               v_hgrn_lb_raw, v_hgrn_norm, v_norm_ffn, v_final_norm]
    small_g = [grad_c_ctx, grad_b_ada, grad_norm_mix, grad_conv_w, grad_conv_b, grad_a_log, grad_dt_bias, grad_ssd_d,
               grad_ssd_norm, grad_lb, grad_hgrn_norm, grad_norm_ffn, grad_final_norm]
    nrows = [-(-a.size // D) for a in small_w]
    packs = lambda lst: jnp.concatenate([_rows(a, r) for a, r in zip(lst, nrows)]
                                        + [jnp.zeros((24 - sum(nrows), D), F32)], axis=0)
    sd, sm, svv = _adamw(packs(small_w), packs(small_m), packs(small_v), packs(small_g), "adamw_small")

    def unpack(p):
        out, r0 = [], 0
        for a, r in zip(small_w, nrows):
            out.append(p[r0:r0 + r].reshape(-1)[:a.size].reshape(a.shape))
            r0 += r
        return out

    sd, sm, svv = unpack(sd), unpack(sm), unpack(svv)
    big = {}
    for nm, w_, m_, v_, g_ in (("w_ada", w_ada, m_w_ada, v_w_ada, g_w_ada), ("w_in", w_in, m_w_in, v_w_in, g_w_in),
                               ("w_out", w_out, m_w_out, v_w_out, g_w_out),
                               ("w_gate", w_gate, m_w_gate, v_w_gate, g_w_gate),
                               ("w_up", w_up, m_w_up, v_w_up, g_w_up),
                               ("w_down", w_down, m_w_down, v_w_down, g_w_down)):
        if nm in ("w_in", "w_gate", "w_up"):
            big[nm] = tuple(tr(t) for t in (g_[None],) + tuple(_adamw(tr(w_), tr(m_), tr(v_), g_, "adamw_" + nm)))
        else:
            big[nm] = (g_[None],) + tuple(_adamw(w_, m_, v_, g_, "adamw_" + nm))

    order = ["c_ctx", "w_ada", "b_ada", "norm_mix", "w_in", "conv_w", "conv_b", "ssd_a_log", "ssd_dt_bias", "ssd_d",
             "ssd_norm", "hgrn_lb_raw", "hgrn_norm", "w_out", "norm_ffn", "w_gate", "w_up", "w_down", "final_norm"]
    small_names = ["c_ctx", "b_ada", "norm_mix", "conv_w", "conv_b", "ssd_a_log", "ssd_dt_bias", "ssd_d", "ssd_norm",
                   "hgrn_lb_raw", "hgrn_norm", "norm_ffn", "final_norm"]
    table = dict(big)
    for k, nm in enumerate(small_names):
        table[nm] = (small_g[k].reshape(small_w[k].shape), sd[k], sm[k], svv[k])
    grads = [table[nm][0] for nm in order]
    deltas = [table[nm][1] for nm in order]
    new_m = [table[nm][2] for nm in order]
    new_v = [table[nm][3] for nm in order]
    return (loss, grad_x[None], *grads, *deltas, *new_m, *new_v)
```

```python
import functools
import math

import jax
import jax.numpy as jnp
from jax import lax
from jax.experimental import pallas as pl
from jax.experimental.pallas import tpu as pltpu

F32 = jnp.float32
BF16 = jnp.bfloat16
MXU_DTYPE = jnp.bfloat16
_INTERPRET = False

D = 1024
NH, HF = 8, 128
HC = 64
SC = 128
SN = 128
SHEADS, SP = 16, 64
GRID_W = 64
KCONV = 5
DFF = 2816
FSL = 768
DFFP = 4 * FSL
NIN = 8224
TB = 256
EPS = 1e-6
LR, B1, B2, AEPS, WD, STEP = 0.001, 0.9, 0.999, 1e-08, 0.01, 10
MESH_ID = pl.DeviceIdType.MESH
NSH = NIN // 4
WSL = 2048
WTAIL = 128


def _pcall(body, *, name, out_shape, grid=(), in_specs=None, out_specs=None, scratch=(), sem=None,
           vmem_mb=None, aliases=None):
    params = {}
    if sem is not None:
        params["dimension_semantics"] = sem
    if vmem_mb is not None:
        params["vmem_limit_bytes"] = vmem_mb << 20
    kw = dict(name=name, out_shape=out_shape, scratch_shapes=list(scratch),
              input_output_aliases=aliases or {}, compiler_params=pltpu.CompilerParams(**params),
              interpret=_INTERPRET)
    if grid:
        kw["grid"] = grid
    if in_specs is not None:
        kw["in_specs"] = in_specs
    if out_specs is not None:
        kw["out_specs"] = out_specs
    return pl.pallas_call(body, **kw)


def _mx(a):
    return a.astype(MXU_DTYPE)


def _dg(a, b, ca, cb):
    return lax.dot_general(_mx(a), _mx(b), (((ca,), (cb,)), ((), ())), preferred_element_type=F32)


def _nn(a, b):
    return _dg(a, b, 1, 0)


def _nt(a, b):
    return _dg(a, b, 1, 1)


def _tn(a, b):
    return _dg(a, b, 0, 0)


def _dot01(m, x):
    hi = x.astype(BF16)
    r1 = x - hi.astype(F32)
    mid = r1.astype(BF16)
    lo = (r1 - mid.astype(F32)).astype(BF16)
    f = lambda t: lax.dot_general(m, t, (((1,), (0,)), ((), ())), preferred_element_type=F32)
    return f(hi) + f(mid) + f(lo)


def _tri(n, upper):
    r = lax.broadcasted_iota(jnp.int32, (n, n), 0)
    c = lax.broadcasted_iota(jnp.int32, (n, n), 1)
    return (c >= r) if upper else (c <= r)


def _b01(mask):
    return jnp.where(mask, 1.0, 0.0).astype(BF16)


def _sig(x):
    return jax.nn.sigmoid(x)


def _silu(x):
    return x * _sig(x)


def _dsilu(x):
    s = _sig(x)
    return s * (1.0 + x * (1.0 - s))


def _softplus(x):
    return jnp.maximum(x, 0.0) + jnp.log(1.0 + jnp.exp(-jnp.abs(x)))


def _rowsum(x):
    return jnp.sum(x, axis=1, keepdims=True)


def _colsum(x):
    return jnp.sum(x, axis=0, keepdims=True)


def _full(shape):
    return pl.BlockSpec(shape, lambda *_: (0,) * len(shape))


def _allgather8_ops(x_ref, out_ref, send_sems, recv_sems, local_sem):
    m_per = x_ref.shape[0]
    x, y, c = lax.axis_index("x"), lax.axis_index("y"), lax.axis_index("c")
    me, sibling = (x, y, c), (x, y, 1 - c)
    chips = [(1 - x, y), (x, 1 - y), (1 - x, 1 - y)]

    def rows(px, py, pc):
        return out_ref.at[pl.ds((4 * px + 2 * py + pc) * m_per, m_per), :]

    def copy(k, block, to, src=None):
        return pltpu.make_async_remote_copy(
            src_ref=rows(*block) if src is None else src, dst_ref=rows(*block),
            send_sem=send_sems.at[k], recv_sem=recv_sems.at[k], device_id=to, device_id_type=MESH_ID)

    mine = pltpu.make_async_copy(x_ref, rows(*me), local_sem)
    mine.start()
    first = [copy(0, me, sibling, src=x_ref)]
    first += [copy(1 + j, me, (*chip, c), src=x_ref) for j, chip in enumerate(chips)]
    for cp in first:
        cp.start()
    passed = [copy(4 + j, (*chip, c), sibling) for j, chip in enumerate(chips)]
    for j, chip in enumerate(chips):
        copy(1 + j, (*chip, c), me).wait_recv()
        passed[j].start()
    copy(0, sibling, me).wait_recv()
    for j, chip in enumerate(chips):
        copy(4 + j, (*chip, 1 - c), me).wait_recv()
    for cp in first + passed:
        cp.wait_send()
    mine.wait()


_AG8_SEMS = [pltpu.SemaphoreType.DMA((7,)), pltpu.SemaphoreType.DMA((7,)), pltpu.SemaphoreType.DMA]


def _allgather8(v, name):
    m_per, n = v.shape
    return _pcall(
        functools.partial(_allgather8_ops), name=name, out_shape=jax.ShapeDtypeStruct((8 * m_per, n), v.dtype),
        in_specs=[pl.BlockSpec(memory_space=pltpu.VMEM)], out_specs=pl.BlockSpec(memory_space=pltpu.VMEM),
        scratch=list(_AG8_SEMS),
    )(v)


def _prologue(pack, cc_row, w_ada, b_shard, shards):
    n = len(shards)
    ncol = w_ada.shape[1]

    def body(pack_ref, cc_ref, w_ref, b_ref, *refs):
        ins = refs[:n]
        gath_ref, araw_ref, mod_ref = refs[n:n + 3]
        outs = refs[n + 3:2 * n + 3]
        modsh, s1, r1, l1, s2, r2, l2, gs, gr = refs[2 * n + 3:]
        start, finish = _gather_ops(ins, outs, gs, gr)
        start()
        _allgather8_ops(pack_ref, gath_ref, s1, r1, l1)
        a = jnp.concatenate([gath_ref[8 * i:8 * i + 1, :] for i in range(8)] + [cc_ref[...], jnp.zeros((7, D), F32)],
                            axis=0)
        araw_ref[...] = a
        modsh[...] = _nn(_silu(a), w_ref[...]) + b_ref[...]
        _allgather8_ops(modsh, mod_ref, s2, r2, l2)
        finish()

    vm = pl.BlockSpec(memory_space=pltpu.VMEM)
    anyspec = pl.BlockSpec(memory_space=pl.ANY)
    return _pcall(
        body, name="prologue",
        out_shape=(jax.ShapeDtypeStruct((64, D), F32), jax.ShapeDtypeStruct((16, D), F32),
                   jax.ShapeDtypeStruct((128, ncol), F32)) + _gather_out(shards),
        in_specs=[vm, vm, vm, vm] + [anyspec] * n, out_specs=(vm, vm, vm) + (anyspec,) * n,
        scratch=[pltpu.VMEM((16, ncol), F32)] + list(_AG8_SEMS) + list(_AG8_SEMS) + _gather_sems(n), vmem_mb=40,
    )(pack, cc_row, w_ada, b_shard, *shards)


def _gather_ops(ins, outs, send_sems, recv_sems):
    n = len(ins)
    x, y, c = lax.axis_index("x"), lax.axis_index("y"), lax.axis_index("c")
    sibling = (x, y, 1 - c)
    chips = [(1 - x, y), (x, 1 - y), (1 - x, 1 - y)]

    def part(a, px, py, pc):
        half = ins[a].shape[0] // 2
        return outs[a].at[2 * px + py, pl.ds(pc * half, half), :]

    def copy(a, k, block, to, src=None):
        return pltpu.make_async_remote_copy(
            src_ref=part(a, *block) if src is None else src, dst_ref=part(a, *block),
            send_sem=send_sems.at[6 * a + k], recv_sem=recv_sems.at[6 * a + k], device_id=to,
            device_id_type=MESH_ID)

    def first(a, j):
        half = ins[a].shape[0] // 2
        return copy(a, j, (x, y, c), (*chips[j], c), src=ins[a].at[pl.ds(c * half, half), :])

    def start():
        for a in range(n):
            for j in range(3):
                first(a, j).start()

    def finish():
        for a in range(n):
            for j, chip in enumerate(chips):
                copy(a, j, (*chip, c), (x, y, c)).wait_recv()
                copy(a, 3 + j, (*chip, c), sibling).start()
        for a in range(n):
            for j, chip in enumerate(chips):
                copy(a, 3 + j, (*chip, 1 - c), (x, y, c)).wait_recv()
        for a in range(n):
            for j, chip in enumerate(chips):
                first(a, j).wait_send()
                copy(a, 3 + j, (*chip, c), sibling).wait_send()

    return start, finish


def _gather_out(shards):
    return tuple(jax.ShapeDtypeStruct((4,) + s_.shape, s_.dtype) for s_ in shards)


def _gather_sems(n):
    return [pltpu.SemaphoreType.DMA((6 * n,)), pltpu.SemaphoreType.DMA((6 * n,))]


def _weights_allgather(shards):
    n = len(shards)

    def body(*refs):
        start, finish = _gather_ops(refs[:n], refs[n:2 * n], *refs[2 * n:])
        start()
        finish()

    return _pcall(
        body, name="weights_allgather", out_shape=_gather_out(shards),
        in_specs=[pl.BlockSpec(memory_space=pl.ANY)] * n, out_specs=(pl.BlockSpec(memory_space=pl.ANY),) * n,
        scratch=_gather_sems(n),
    )(*shards)


def _pair_exchange(gs, name):
    n = len(gs)

    def body(*refs):
        ins, outs = refs[:n], refs[n:2 * n]
        send_sems, recv_sems = refs[2 * n:]
        x, y, c = lax.axis_index("x"), lax.axis_index("y"), lax.axis_index("c")
        cps = []
        for a in range(n):
            half = ins[a].shape[1] // 2
            cps.append(pltpu.make_async_remote_copy(
                src_ref=ins[a].at[:, pl.ds((1 - c) * half, half), :], dst_ref=outs[a], send_sem=send_sems.at[a],
                recv_sem=recv_sems.at[a], device_id=(x, y, 1 - c), device_id_type=MESH_ID))
        for cp in cps:
            cp.start()
        for cp in cps:
            cp.wait()

    return _pcall(
        body, name=name,
        out_shape=tuple(jax.ShapeDtypeStruct((g.shape[0], g.shape[1] // 2, g.shape[2]), g.dtype) for g in gs),
        in_specs=[pl.BlockSpec(memory_space=pl.ANY)] * n, out_specs=(pl.BlockSpec(memory_space=pl.ANY),) * n,
        scratch=[pltpu.SemaphoreType.DMA((n,)), pltpu.SemaphoreType.DMA((n,))],
    )(*gs)


def _exchange_ops(ins, outs, send_sems, recv_sems, dests):
    x, y, c = lax.axis_index("x"), lax.axis_index("y"), lax.axis_index("c")
    mine = 2 * x + y
    chips = [(1 - x, y), (x, 1 - y), (1 - x, 1 - y)]

    def each(fn):
        for a in range(len(ins)):
            lo, hi = dests[a]
            for j, (px, py) in enumerate(chips):
                q = 2 * px + py
                cp = pltpu.make_async_remote_copy(
                    src_ref=ins[a].at[jnp.clip(q - lo, 0, hi - lo - 1)], dst_ref=outs[a].at[j],
                    send_sem=send_sems.at[3 * a + j], recv_sem=recv_sems.at[3 * a + j], device_id=(px, py, c),
                    device_id_type=MESH_ID)
                fn(cp, (q >= lo) & (q < hi), (mine >= lo) & (mine < hi), (lo, hi) == (0, 4))

    def start():
        def go(cp, send_ok, recv_ok, always):
            if always:
                cp.start()
            else:
                pl.when(send_ok)(cp.start)
        each(go)

    def finish():
        def go(cp, send_ok, recv_ok, always):
            if always:
                cp.wait()
            else:
                pl.when(send_ok)(cp.wait_send)
                pl.when(recv_ok)(cp.wait_recv)
        each(go)

    return start, finish


def _comm_exchange(hs, dests):
    n = len(hs)
    return (list(hs), tuple(jax.ShapeDtypeStruct((3,) + h.shape[1:], h.dtype) for h in hs),
            [pltpu.SemaphoreType.DMA((3 * n,)), pltpu.SemaphoreType.DMA((3 * n,))],
            lambda i, o, s, r: _exchange_ops(i, o, s, r, dests))


def _comm_gather(shards):
    return (list(shards), _gather_out(shards), _gather_sems(len(shards)), _gather_ops)


def _carry(call, comm, steps):
    if comm is None:
        return call
    arrays, out_shape, sems, make = comm
    n, n_in, n_out = len(arrays), len(call["args"]), len(call["out_shape"])
    body = call["body"]

    def wrapped(*refs):
        base_in, cin = refs[:n_in], refs[n_in:n_in + n]
        rest = refs[n_in + n:]
        base_out, cout, scr = rest[:n_out], rest[n_out:n_out + n], rest[n_out + n:]
        start, finish = make(cin, cout, scr[-2], scr[-1])
        first, last = steps()
        pl.when(first)(start)
        body(*base_in, *base_out, *scr[:-2])
        pl.when(last)(finish)

    anyspec = pl.BlockSpec(memory_space=pl.ANY)
    return dict(call, body=wrapped, args=list(call["args"]) + arrays,
                in_specs=list(call["in_specs"]) + [anyspec] * n,
                out_shape=tuple(call["out_shape"]) + tuple(out_shape),
                out_specs=tuple(call["out_specs"]) + (anyspec,) * n,
                scratch=list(call["scratch"]) + sems)


def _run(call):
    args = call.pop("args")
    body = call.pop("body")
    return _pcall(body, **call)(*args)


def _pair_swap(rs):
    n = len(rs)

    def body(*refs):
        ins, outs = refs[:n], refs[n:2 * n]
        send_sems, recv_sems = refs[2 * n:]
        x, y, c = lax.axis_index("x"), lax.axis_index("y"), lax.axis_index("c")
        cps = [pltpu.make_async_remote_copy(
            src_ref=ins[a], dst_ref=outs[a], send_sem=send_sems.at[a], recv_sem=recv_sems.at[a],
            device_id=(x, y, 1 - c), device_id_type=MESH_ID) for a in range(n)]
        for cp in cps:
            cp.start()
        for cp in cps:
            cp.wait()

    return _pcall(
        body, name="grads_pair_swap", out_shape=tuple(jax.ShapeDtypeStruct(r.shape, r.dtype) for r in rs),
        in_specs=[pl.BlockSpec(memory_space=pl.ANY)] * n, out_specs=(pl.BlockSpec(memory_space=pl.ANY),) * n,
        scratch=[pltpu.SemaphoreType.DMA((n,)), pltpu.SemaphoreType.DMA((n,))],
    )(*rs)


SUM_STEPS = 4


def _pair_sum(gs, recvs, core, name):
    n = len(gs)

    def body(c_ref, *refs):
        for a in range(n):
            refs[2 * n + a][...] = (refs[a][...].astype(F32) + refs[n + a][...].astype(F32)).astype(refs[2 * n + a].dtype)

    blk = lambda g: (g.shape[0], g.shape[1] // (2 * SUM_STEPS), g.shape[2])
    return pl.pallas_call(
        body, name=name,
        out_shape=tuple(jax.ShapeDtypeStruct((g.shape[0], g.shape[1] // 2, g.shape[2]), g.dtype) for g in gs),
        grid_spec=pltpu.PrefetchScalarGridSpec(
            num_scalar_prefetch=1, grid=(SUM_STEPS,),
            in_specs=[pl.BlockSpec(blk(g), lambda i, cr: (0, cr[0] * SUM_STEPS + i, 0)) for g in gs]
            + [pl.BlockSpec(blk(g), lambda i, cr: (0, i, 0)) for g in gs],
            out_specs=tuple(pl.BlockSpec(blk(g), lambda i, cr: (0, i, 0)) for g in gs)),
        compiler_params=pltpu.CompilerParams(vmem_limit_bytes=40 << 20), interpret=_INTERPRET,
    )(core, *gs, *recvs)


def _chip_sum(hs, recvs, chip, dests, slots):
    n = len(hs)
    nout = max(slots) + 1
    first = [slots.index(o) for o in range(nout)]
    every = lambda d_: d_ == (0, 4)

    def own(d_):
        if every(d_):
            return lambda i, kr: (kr[0], i, 0)
        return lambda i, kr: (0, jnp.where(kr[0] == d_[0], i, 0), 0)

    def got(d_):
        if every(d_):
            return lambda i, kr: (0, i, 0)
        return lambda i, kr: (0, jnp.where(kr[0] == d_[0], i, 0), 0)

    def body(k_ref, *refs):
        for a in range(n):
            def emit(a=a):
                acc = refs[a][0].astype(F32)
                for j in range(3):
                    acc = acc + refs[n + a][j].astype(F32)
                refs[2 * n + slots[a]][...] = acc
            if every(dests[a]):
                emit()
            else:
                pl.when(k_ref[0] == dests[a][0])(emit)

    rb = lambda h: h.shape[1] // SUM_STEPS
    return pl.pallas_call(
        body, name="grads_chip_sum",
        out_shape=tuple(jax.ShapeDtypeStruct(hs[a].shape[1:], F32) for a in first),
        grid_spec=pltpu.PrefetchScalarGridSpec(
            num_scalar_prefetch=1, grid=(SUM_STEPS,),
            in_specs=[pl.BlockSpec((1, rb(h), h.shape[2]), own(d_)) for h, d_ in zip(hs, dests)]
            + [pl.BlockSpec((3, rb(h), h.shape[2]), got(d_)) for h, d_ in zip(hs, dests)],
            out_specs=tuple(pl.BlockSpec((rb(hs[a]), hs[a].shape[2]), lambda i, kr: (i, 0)) for a in first)),
        compiler_params=pltpu.CompilerParams(vmem_limit_bytes=40 << 20), interpret=_INTERPRET,
    )(chip, *hs, *recvs)


def _ada_fwd(araw, w, b):
    nblk = w.shape[1] // 512

    def body(a_ref, w_ref, b_ref, o_ref):
        o_ref[...] = _nn(_silu(a_ref[...]), w_ref[...]) + b_ref[...]

    return _pcall(
        body, name="ada_fwd", out_shape=jax.ShapeDtypeStruct((16, w.shape[1]), F32), grid=(nblk,),
        in_specs=[_full((16, D)), pl.BlockSpec((D, 512), lambda j: (0, j)), pl.BlockSpec((1, 512), lambda j: (0, j))],
        out_specs=pl.BlockSpec((16, 512), lambda j: (0, j)), sem=("parallel",),
    )(araw, w, b)


def _ada_bwd(araw, dmod, w):
    nblk = w.shape[1] // 512

    def body(a_ref, d_ref, w_ref, gw_ref, da_ref):
        j = pl.program_id(0)
        gw_ref[...] = _tn(_silu(a_ref[...]), d_ref[...])
        part = _nt(d_ref[...], w_ref[...])

        @pl.when(j == 0)
        def _():
            da_ref[...] = part

        @pl.when(j > 0)
        def _():
            da_ref[...] += part

    return _pcall(
        body, name="ada_bwd",
        out_shape=(jax.ShapeDtypeStruct(w.shape, F32), jax.ShapeDtypeStruct((16, D), F32)), grid=(nblk,),
        in_specs=[_full((16, D)), pl.BlockSpec((16, 512), lambda j: (0, j)), pl.BlockSpec((D, 512), lambda j: (0, j))],
        out_specs=(pl.BlockSpec((D, 512), lambda j: (0, j)), _full((16, D))), sem=("arbitrary",),
    )(araw, dmod, w)


def _w_specs():
    return [pl.BlockSpec((None, D, D), lambda j, i: (j // 2, j % 2, 0)),
            pl.BlockSpec((None, WTAIL, D), lambda j, i: (jnp.maximum(j // 2 - 1, 0), 0, 0)),
            pl.BlockSpec((None, WTAIL, D), lambda j, i: (3, 0, 0))]


def _inproj(xin, mods, wi_main, wi_tail, t_total, tb, blk_off, prev, name, comm=None):
    n = xin.shape[0]
    nt = n // tb
    ncol = 8

    def body(x_ref, mod_ref, w_ref, wb_ref, wdt_ref, *rest):
        p_ref, pdt_ref, u_ref, uscr = rest[-4:]
        j, i = pl.program_id(0), pl.program_id(1)
        rows = pl.ds(pl.multiple_of(i * tb, tb), tb)

        @pl.when(j == 0)
        def _():
            xv = x_ref[...]
            r = lax.rsqrt(jnp.mean(xv * xv, axis=1, keepdims=True) + EPS)
            u = (xv * r * mod_ref[2:3, :]) * mod_ref[0:1, :] + mod_ref[1:2, :]
            ub = u.astype(MXU_DTYPE)
            uscr[rows, :] = ub
            u_ref[...] = ub
            pdt_ref[...] = _nt(ub, wdt_ref[...])

        ub = uscr[rows, :]
        pv = _nt(ub, w_ref[...])

        @pl.when((j % 2 == 1) | (j == 0))
        def _():
            p_ref[...] = pv.astype(p_ref.dtype)

        @pl.when((j % 2 == 0) & (j > 0))
        def _():
            head = pv[:, 0:WTAIL] + _nt(ub, wb_ref[...])
            p_ref[...] = jnp.concatenate([head, pv[:, WTAIL:]], axis=1).astype(p_ref.dtype)

    once = lambda j, i: (jnp.where(j == 0, i, nt - 1) + blk_off, 0)
    in_specs = [pl.BlockSpec((tb, D), lambda j, i: (jnp.where(j == 0, i, nt - 1), 0)), _full((8, D))] + _w_specs()
    args = [xin, mods, wi_main, wi_tail, wi_tail]
    aliases = None
    if prev is not None:
        in_specs += [pl.BlockSpec(memory_space=pl.ANY)] * 3
        args += list(prev)
        aliases = {5: 0, 6: 1, 7: 2}
    call = dict(
        body=body, args=args, name=name,
        out_shape=(jax.ShapeDtypeStruct((t_total, ncol * D), MXU_DTYPE), jax.ShapeDtypeStruct((t_total, 128), F32),
                   jax.ShapeDtypeStruct((t_total, D), MXU_DTYPE)),
        grid=(ncol, nt), in_specs=in_specs,
        out_specs=(pl.BlockSpec((tb, D), lambda j, i: (i + blk_off, j)), pl.BlockSpec((tb, 128), once),
                   pl.BlockSpec((tb, D), once)),
        scratch=[pltpu.VMEM((n, D), MXU_DTYPE)], sem=("arbitrary", "arbitrary"), vmem_mb=48, aliases=aliases)
    steps = lambda: ((pl.program_id(0) == 0) & (pl.program_id(1) == 0),
                     (pl.program_id(0) == ncol - 1) & (pl.program_id(1) == nt - 1))
    return _run(_carry(call, comm, steps))


def _blk(s, nb, rev):
    return jnp.where(s == 0, nb - 1, (nb - 1 - s) if rev else (s - 1))


def _hgrn_gate(fr, lbraw_ref, d):
    lb = _sig(lbraw_ref[d:d + 1, :] - lbraw_ref[2 + d:3 + d, :])
    sg = _sig(fr)
    return lb, sg, lb + (1.0 - lb) * sg


def _hgrn_fwd(p_main, lbraw, d, nb, comm=None):
    t_total = p_main.shape[0]
    rev = d == 1
    nch = TB // HC
    scale = HF ** -0.5

    def body(q_ref, f_ref, v_ref, lb_ref, o_ref, sp_ref, st):
        s = pl.program_id(0)

        @pl.when(s == 0)
        def _():
            st[...] = jnp.zeros_like(st)

        mb = _tri(HC, rev)
        m01 = _b01(mb)
        order = list(reversed(range(nch)) if rev else range(nch))
        hs_ = [slice(h * HF, (h + 1) * HF) for h in range(NH)]
        pre = {}
        for c in order:
            rows = slice(c * HC, (c + 1) * HC)
            _, _, f = _hgrn_gate(f_ref[rows, :].astype(F32), lb_ref, d)
            k = 1.0 - f
            cum = _dot01(m01, jnp.log(f))
            tot = cum[0:1, :] if rev else cum[HC - 1:HC, :]
            qd = _silu(q_ref[rows, :].astype(F32)) * scale * jnp.exp(cum)
            ki = k * jnp.exp(-cum)
            etot = jnp.exp(tot)
            pre[c] = (_mx(qd), _mx(ki), _mx(ki * etot), _mx(v_ref[rows, :]), etot)
        scs = {c: [_nt(pre[c][0][:, cs], pre[c][1][:, cs]) for cs in hs_] for c in order}
        upd = {c: [_tn(pre[c][3][:, cs], pre[c][2][:, cs]) for cs in hs_] for c in order}
        intra = {c: [_nn(jnp.where(mb, scs[c][h], 0.0), pre[c][3][:, cs]) for h, cs in enumerate(hs_)] for c in order}
        for c in order:
            rows = slice(c * HC, (c + 1) * HC)
            qdb, etot = pre[c][0], pre[c][4]
            for h, cs in enumerate(hs_):
                sth = st[h]
                stb = sth.astype(sp_ref.dtype)
                sp_ref[c, h] = stb
                o_ref[rows, cs] = (intra[c][h] + _nt(qdb[:, cs], stb)).astype(o_ref.dtype)
                st[h] = sth * etot[:, cs] + upd[c][h]

    col = lambda j: (lambda s: (_blk(s, nb, rev), j))
    call = dict(
        body=body, args=[p_main, p_main, p_main, lbraw], name=f"hgrn_fwd_{d}",
        out_shape=(jax.ShapeDtypeStruct((t_total, D), MXU_DTYPE),
                   jax.ShapeDtypeStruct((nch * nb, NH, HF, HF), MXU_DTYPE)),
        grid=(nb,),
        in_specs=[pl.BlockSpec((TB, D), col(0)), pl.BlockSpec((TB, D), col(1 + d)), pl.BlockSpec((TB, D), col(3)),
                  _full((8, D))],
        out_specs=(pl.BlockSpec((TB, D), col(0)),
                   pl.BlockSpec((nch, NH, HF, HF), lambda s: (_blk(s, nb, rev), 0, 0, 0))),
        scratch=[pltpu.VMEM((NH, HF, HF), F32)], sem=("arbitrary",), vmem_mb=40)
    return _run(_carry(call, comm, lambda: (pl.program_id(0) == 0, pl.program_id(0) == nb - 1)))


def _hgrn_bwd(p_main, lbraw, sprev, do, d, nb, prev, comm=None):
    t_total = p_main.shape[0]
    rev = d == 1
    nch = TB // HC
    scale = HF ** -0.5
    last = prev is not None
    odt = MXU_DTYPE if last else F32

    def body(q_ref, f_ref, v_ref, lb_ref, sp_ref, do_ref, *rest):
        if last:
            dqp_ref, dvp_ref = rest[:2]
            rest = rest[2:]
        dq_ref, df_ref, dv_ref, dlb_ref, dst = rest
        sp_id = pl.program_id(0)
        is_ctx = sp_id == nb - 1

        @pl.when(sp_id == 0)
        def _():
            dst[...] = jnp.zeros_like(dst)
            dlb_ref[...] = jnp.zeros_like(dlb_ref)

        mb = _tri(HC, rev)
        mbt = _tri(HC, not rev)
        m01 = _b01(mb)
        mt01 = _b01(mbt)
        order = list(range(nch) if rev else reversed(range(nch)))
        hs_ = [slice(h * HF, (h + 1) * HF) for h in range(NH)]
        pre = {}
        for c in order:
            rows = slice(c * HC, (c + 1) * HC)
            lb, sg, f = _hgrn_gate(f_ref[rows, :].astype(F32), lb_ref, d)
            k = 1.0 - f
            cum = _dot01(m01, jnp.log(f))
            tot = cum[0:1, :] if rev else cum[HC - 1:HC, :]
            e = jnp.exp(cum)
            ei = jnp.exp(-cum)
            etot = jnp.exp(tot)
            ee = ei * etot
            qraw = q_ref[rows, :].astype(F32)
            qd = _silu(qraw) * scale * e
            ki = k * ei
            ke = k * ee
            dov = jnp.where(is_ctx, 0.0, do_ref[rows, :].astype(F32))
            pre[c] = dict(lb=lb, sg=sg, f=f, e=e, ei=ei, ee=ee, etot=etot, qraw=qraw, qd=qd, ki=ki, ke=ke,
                          qdb=_mx(qd), kib=_mx(ki), keb=_mx(ke), vb=_mx(v_ref[rows, :]), dob=_mx(dov))
        units = [(c, h) for c in order for h in range(NH)]
        col = lambda u, key: pre[u[0]][key][:, hs_[u[1]]]
        pt = {u: jnp.where(mbt, _nt(col(u, "kib"), col(u, "qdb")), 0.0) for u in units}
        dp = {u: jnp.where(mb, _nt(col(u, "dob"), col(u, "vb")), 0.0) for u in units}
        dpt = {u: jnp.where(mbt, _nt(col(u, "vb"), col(u, "dob")), 0.0) for u in units}
        dv_i = {u: _nn(pt[u], col(u, "dob")) for u in units}
        dqd_ = {u: _nn(dp[u], col(u, "kib")) + _nn(col(u, "dob"), sp_ref[u[0], u[1]]) for u in units}
        dki_ = {u: _nn(dpt[u], col(u, "qdb")) for u in units}
        dsl = {u: _tn(col(u, "dob"), col(u, "qdb")) for u in units}
        for c in order:
            rows = slice(c * HC, (c + 1) * HC)
            p = pre[c]
            dv_l, dke_l, dtot_l = [], [], []
            for h, cs in enumerate(hs_):
                dso = dst[h]
                dsob = _mx(dso)
                dv_l.append(dv_i[(c, h)] + _nt(p["keb"][:, cs], dsob))
                dke_l.append(_nn(p["vb"][:, cs], dsob))
                dtot_l.append(_colsum(dso * sp_ref[c, h].astype(F32)) * p["etot"][:, cs])
                dst[h] = dso * p["etot"][:, cs] + dsl[(c, h)]
            lb, sg, f, e, ei, ee, qraw, qd, ki, ke = (p[n_] for n_ in ("lb", "sg", "f", "e", "ei", "ee", "qraw", "qd",
                                                                     "ki", "ke"))
            dqd = jnp.concatenate([dqd_[(c, h)] for h in range(NH)], axis=1)
            dki = jnp.concatenate([dki_[(c, h)] for h in range(NH)], axis=1)
            dke = jnp.concatenate(dke_l, axis=1)
            dcum = dqd * qd - dki * ki - dke * ke
            dtot = jnp.concatenate(dtot_l, axis=1) + _colsum(dke * ke)
            dk = dki * ei + dke * ee
            dlf = _dot01(mt01, dcum) + dtot
            df = dlf / f - dk
            dlb_ref[0:1, :] += _colsum(df * (1.0 - sg))
            dfr = df * (1.0 - lb) * sg * (1.0 - sg)
            dq = dqd * e * scale * _dsilu(qraw)
            dv = jnp.concatenate(dv_l, axis=1)
            if last:
                dq = dq + dqp_ref[rows, :]
                dv = dv + dvp_ref[rows, :]
            dq_ref[rows, :] = dq.astype(odt)
            dv_ref[rows, :] = dv.astype(odt)
            df_ref[rows, :] = dfr.astype(MXU_DTYPE)

    blk = lambda s: _blk(nb - 1 - s, nb, rev)
    col = lambda j: (lambda s: (blk(s), j))
    in_specs = [pl.BlockSpec((TB, D), col(0)), pl.BlockSpec((TB, D), col(1 + d)), pl.BlockSpec((TB, D), col(3)),
                _full((8, D)), pl.BlockSpec((nch, NH, HF, HF), lambda s: (blk(s), 0, 0, 0)),
                pl.BlockSpec((TB, D), lambda s: (jnp.minimum(blk(s), nb - 2), 0))]
    args = [p_main, p_main, p_main, lbraw, sprev, do]
    if last:
        in_specs += [pl.BlockSpec((TB, D), col(0))] * 2
        args += list(prev)
    call = dict(
        body=body, args=args, name=f"hgrn_bwd_{d}",
        out_shape=(jax.ShapeDtypeStruct((t_total, D), odt), jax.ShapeDtypeStruct((t_total, D), MXU_DTYPE),
                   jax.ShapeDtypeStruct((t_total, D), odt), jax.ShapeDtypeStruct((8, D), F32)),
        grid=(nb,), in_specs=in_specs,
        out_specs=(pl.BlockSpec((TB, D), col(0)), pl.BlockSpec((TB, D), col(0)), pl.BlockSpec((TB, D), col(0)),
                   _full((8, D))),
        scratch=[pltpu.VMEM((NH, HF, HF), F32)], sem=("arbitrary",), vmem_mb=48)
    return _run(_carry(call, comm, lambda: (pl.program_id(0) == 0, pl.program_id(0) == nb - 1)))


def _conv_masks(tb, is_ctx):
    seg = jnp.where(is_ctx, tb, GRID_W)
    pos = lax.broadcasted_iota(jnp.int32, (tb, 1), 0) & (seg - 1)
    return pos, seg


def _shift_rows(x, dshift, pos, seg):
    if dshift == 0:
        return x
    n = x.shape[0]
    rolled = pltpu.roll(x, (-dshift) % n, 0)
    ok = (pos + dshift >= 0) & (pos + dshift < seg)
    return jnp.where(ok, rolled, 0.0)


def _ssd_prep(p_main, p_dt, convp, dtb, nb):
    t_total = p_main.shape[0]

    def body(x_ref, dt_ref, cw_ref, dtb_ref, xa_ref, dts_ref):
        is_ctx = pl.program_id(0) == nb - 1
        pos, seg = _conv_masks(TB, is_ctx)
        xv = x_ref[...].astype(F32)
        acc = cw_ref[5:6, :] + cw_ref[2:3, :] * xv
        for kk in (0, 1, 3, 4):
            acc = acc + cw_ref[kk:kk + 1, :] * _shift_rows(xv, kk - 2, pos, seg)
        xa_ref[...] = _silu(acc).astype(xa_ref.dtype)
        dts_ref[...] = _softplus(dt_ref[...] + dtb_ref[0:1, :])

    return _pcall(
        body, name="ssd_prep",
        out_shape=(jax.ShapeDtypeStruct((t_total, 2048), MXU_DTYPE), jax.ShapeDtypeStruct((t_total, 128), F32)),
        grid=(nb,),
        in_specs=[pl.BlockSpec((TB, 2048), lambda i: (i, 3)), pl.BlockSpec((TB, 128), lambda i: (i, 0)),
                  _full((8, 2048)), _full((8, 128))],
        out_specs=(pl.BlockSpec((TB, 2048), lambda i: (i, 0)), pl.BlockSpec((TB, 128), lambda i: (i, 0))),
        sem=("parallel",), vmem_mb=32,
    )(p_main, p_dt, convp, dtb)


def _ssd_prep_bwd(p_main, p_dt, convp, dtb, dxa, dxs_skip, ddts, nb):
    t_total = p_main.shape[0]

    def body(x_ref, dt_ref, cw_ref, dtb_ref, dxa_ref, dsk_ref, ddts_ref, dx_ref, ddt_ref, dcw_ref, ddtb_ref):
        i = pl.program_id(0)
        is_ctx = i == nb - 1

        @pl.when(i == 0)
        def _():
            dcw_ref[...] = jnp.zeros_like(dcw_ref)
            ddtb_ref[...] = jnp.zeros_like(ddtb_ref)

        pos, seg = _conv_masks(TB, is_ctx)
        xv = x_ref[...].astype(F32)
        sh = {kk: _shift_rows(xv, kk - 2, pos, seg) for kk in range(KCONV)}
        acc = cw_ref[5:6, :]
        for kk in range(KCONV):
            acc = acc + cw_ref[kk:kk + 1, :] * sh[kk]
        dact = dxa_ref[...]
        dact = jnp.concatenate([dact[:, :D] + jnp.where(is_ctx, 0.0, dsk_ref[...].astype(F32)), dact[:, D:]], axis=1)
        dpre = dact * _dsilu(acc)
        dxv = cw_ref[2:3, :] * dpre
        for kk in (0, 1, 3, 4):
            dxv = dxv + cw_ref[kk:kk + 1, :] * _shift_rows(dpre, 2 - kk, pos, seg)
        dx_ref[...] = dxv.astype(dx_ref.dtype)
        for kk in range(KCONV):
            dcw_ref[kk:kk + 1, :] += _colsum(dpre * sh[kk])
        dcw_ref[5:6, :] += _colsum(dpre)
        draw = ddts_ref[...] * _sig(dt_ref[...] + dtb_ref[0:1, :])
        ddt_ref[...] = draw.astype(ddt_ref.dtype)
        ddtb_ref[0:1, :] += _colsum(draw)

    return _pcall(
        body, name="ssd_prep_bwd",
        out_shape=(jax.ShapeDtypeStruct((t_total, 2048), MXU_DTYPE), jax.ShapeDtypeStruct((t_total, 128), MXU_DTYPE),
                   jax.ShapeDtypeStruct((8, 2048), F32), jax.ShapeDtypeStruct((8, 128), F32)),
        grid=(nb,),
        in_specs=[pl.BlockSpec((TB, 2048), lambda i: (i, 3)), pl.BlockSpec((TB, 128), lambda i: (i, 0)),
                  _full((8, 2048)), _full((8, 128)), pl.BlockSpec((TB, 2048), lambda i: (i, 0)),
                  pl.BlockSpec((TB, D), lambda i: (jnp.minimum(i, nb - 2), 0)),
                  pl.BlockSpec((TB, 128), lambda i: (i, 0))],
        out_specs=(pl.BlockSpec((TB, 2048), lambda i: (i, 0)), pl.BlockSpec((TB, 128), lambda i: (i, 0)),
                   _full((8, 2048)), _full((8, 128))),
        sem=("arbitrary",), vmem_mb=40,
    )(p_main, p_dt, convp, dtb, dxa, dxs_skip, ddts)


def _dot2(x, m01):
    hi = x.astype(BF16)
    lo = (x - hi.astype(F32)).astype(BF16)
    f = lambda t: lax.dot_general(t, m01, (((1,), (0,)), ((), ())), preferred_element_type=F32)
    return f(hi) + f(lo)


def _head_lanes(c0, c1):
    p = lax.broadcasted_iota(jnp.int32, (128, 128), 0)
    l = lax.broadcasted_iota(jnp.int32, (128, 128), 1)
    return _b01(((l == c0) & (p < SP)) | ((l == c1) & (p >= SP)))


def _one_lane(col):
    return _b01(lax.broadcasted_iota(jnp.int32, (128, 128), 1) == col)


def _lane_pick(x, lane, col):
    return _rowsum(jnp.where(lane == col, x, 0.0))


def _ssd_chunk_common(dts, alog_ref, m01, rev):
    lane = lax.broadcasted_iota(jnp.int32, (1, 128), 1)
    arow = -jnp.exp(alog_ref[0:1, :])
    cum = _dot01(m01, dts * arow)
    tot = cum[0:1, :] if rev else cum[SC - 1:SC, :]
    return lane, arow, cum, cum.T, tot


def _ssd_fwd(xa, dts, alog, d, nb):
    t_total = xa.shape[0]
    rev = d == 1
    nch = TB // SC
    npair = SHEADS // 2

    def body(xa_ref, dts_ref, alog_ref, y_ref, sp_ref, st):
        s = pl.program_id(0)

        @pl.when(s == 0)
        def _():
            st[...] = jnp.zeros_like(st)

        mb = _tri(SC, rev)
        m01 = _b01(mb)
        lo = lax.broadcasted_iota(jnp.int32, (1, 128), 1) < SP
        rlo = lax.broadcasted_iota(jnp.int32, (128, 1), 0) < SP
        order = list(reversed(range(nch)) if rev else range(nch))
        pre = {}
        for c in order:
            rows = slice(c * SC, (c + 1) * SC)
            dts_c = dts_ref[rows, :]
            lane, arow, cum, cumt, tot = _ssd_chunk_common(dts_c, alog_ref, m01, rev)
            bgs = [_mx(xa_ref[rows, D + g * SN:D + (g + 1) * SN]) for g in range(4)]
            cgs = [_mx(xa_ref[rows, D + 512 + g * SN:D + 512 + (g + 1) * SN]) for g in range(4)]
            pairs = []
            for pr in range(npair):
                xs = xa_ref[rows, pr * 128:(pr + 1) * 128].astype(F32)
                cols = [16 * d + 2 * pr, 16 * d + 2 * pr + 1]
                cum_c = [_lane_pick(cum, lane, q) for q in cols]
                dt_c = [_lane_pick(dts_c, lane, q) for q in cols]
                tot_c = [_lane_pick(tot, lane, q) for q in cols]
                dtx = xs * jnp.where(lo, dt_c[0], dt_c[1])
                e1_pair = jnp.where(lo, jnp.exp(cum_c[0]), jnp.exp(cum_c[1]))
                e2_pair = jnp.where(lo, jnp.exp(tot_c[0] - cum_c[0]), jnp.exp(tot_c[1] - cum_c[1]))
                etot_col = jnp.where(rlo, jnp.exp(tot_c[0]), jnp.exp(tot_c[1]))
                decs = [jnp.where(mb, jnp.exp(cum_c[q] - cumt[cols[q]:cols[q] + 1, :]), 0.0) for q in range(2)]
                dtxq = [_mx(jnp.where(lo if q == 0 else ~lo, dtx, 0.0)) for q in range(2)]
                pairs.append(dict(e1=e1_pair, etot=etot_col, decs=decs, dtxq=dtxq, xe=_mx(dtx * e2_pair)))
            pre[c] = (bgs, cgs, pairs)
        gm = {(c, g): _nt(pre[c][1][g], pre[c][0][g]) for c in order for g in range(4)}
        upd = {(c, pr): _tn(pre[c][2][pr]["xe"], pre[c][0][pr // 2]) for c in order for pr in range(npair)}
        intra = {(c, pr): sum(_nn(gm[(c, pr // 2)] * pre[c][2][pr]["decs"][q], pre[c][2][pr]["dtxq"][q]) for q in range(2))
                 for c in order for pr in range(npair)}
        for c in order:
            rows = slice(c * SC, (c + 1) * SC)
            bgs, cgs, pairs = pre[c]
            for pr in range(npair):
                stp = st[pr]
                stb = stp.astype(sp_ref.dtype)
                sp_ref[c, pr] = stb
                y_ref[rows, pr * 128:(pr + 1) * 128] = (
                    intra[(c, pr)] + pairs[pr]["e1"] * _nt(cgs[pr // 2], stb)).astype(y_ref.dtype)
                st[pr] = stp * pairs[pr]["etot"] + upd[(c, pr)]

    blk = lambda s: _blk(s, nb, rev)
    return _pcall(
        body, name=f"ssd_fwd_{d}",
        out_shape=(jax.ShapeDtypeStruct((t_total, D), MXU_DTYPE),
                   jax.ShapeDtypeStruct((nch * nb, npair, 128, SN), MXU_DTYPE)),
        grid=(nb,),
        in_specs=[pl.BlockSpec((TB, 2048), lambda s: (blk(s), 0)), pl.BlockSpec((TB, 128), lambda s: (blk(s), 0)),
                  _full((8, 128))],
        out_specs=(pl.BlockSpec((TB, D), lambda s: (blk(s), 0)),
                   pl.BlockSpec((nch, npair, 128, SN), lambda s: (blk(s), 0, 0, 0))),
        scratch=[pltpu.VMEM((npair, 128, SN), F32)], sem=("arbitrary",), vmem_mb=40,
    )(xa, dts, alog)


def _ssd_bwd(xa, dts, alog, sprev, dy, d, nb, prev, comm=None):
    t_total = xa.shape[0]
    rev = d == 1
    nch = TB // SC
    npair = SHEADS // 2
    last = prev is not None

    def body(xa_ref, dts_ref, alog_ref, sp_ref, dy_ref, *rest):
        if last:
            dxp_ref, ddp_ref = rest[:2]
            rest = rest[2:]
        dxa_ref, ddts_ref, da_ref, dst, zc_scr = rest
        sp_id = pl.program_id(0)
        is_ctx = sp_id == nb - 1

        @pl.when(sp_id == 0)
        def _():
            dst[...] = jnp.zeros_like(dst)
            da_ref[...] = jnp.zeros_like(da_ref)
            zc_scr[...] = jnp.zeros_like(zc_scr)

        mb = _tri(SC, rev)
        m01 = _b01(mb)
        mt01 = _b01(_tri(SC, not rev))
        lo = lax.broadcasted_iota(jnp.int32, (1, 128), 1) < SP
        rlo = lax.broadcasted_iota(jnp.int32, (128, 1), 0) < SP
        order = list(range(nch) if rev else reversed(range(nch)))
        pre = {}
        for c in order:
            rows = slice(c * SC, (c + 1) * SC)
            dts_c = dts_ref[rows, :]
            lane, arow, cum, cumt, tot = _ssd_chunk_common(dts_c, alog_ref, m01, rev)
            pairs = []
            for pr in range(npair):
                xs = xa_ref[rows, pr * 128:(pr + 1) * 128].astype(F32)
                dyp = jnp.where(is_ctx, 0.0, dy_ref[rows, pr * 128:(pr + 1) * 128].astype(F32))
                cols = [16 * d + 2 * pr, 16 * d + 2 * pr + 1]
                cum_c = [_lane_pick(cum, lane, q) for q in cols]
                dt_c = [_lane_pick(dts_c, lane, q) for q in cols]
                tot_c = [_lane_pick(tot, lane, q) for q in cols]
                e1_c = [jnp.exp(cum_c[q]) for q in range(2)]
                e2_c = [jnp.exp(tot_c[q] - cum_c[q]) for q in range(2)]
                etot_c = [jnp.exp(tot_c[q]) for q in range(2)]
                dt_pair = jnp.where(lo, dt_c[0], dt_c[1])
                e1_pair = jnp.where(lo, e1_c[0], e1_c[1])
                e2_pair = jnp.where(lo, e2_c[0], e2_c[1])
                dtx = xs * dt_pair
                decs = [jnp.where(mb, jnp.exp(cum_c[q] - cumt[cols[q]:cols[q] + 1, :]), 0.0) for q in range(2)]
                dyq = [_mx(jnp.where(lo if q == 0 else ~lo, dyp, 0.0)) for q in range(2)]
                pairs.append(dict(xs=xs, dyp=dyp, cols=cols, e1_c=e1_c, e2_c=e2_c, etot_c=etot_c, dt_pair=dt_pair,
                                  e2_pair=e2_pair, etot_col=jnp.where(rlo, etot_c[0], etot_c[1]), dtx=dtx,
                                  dtxb=_mx(dtx), xeb=_mx(dtx * e2_pair), dy0b=_mx(dyp * e1_pair), decs=decs, dyq=dyq))
            pre[c] = dict(lane=lane, arow=arow, dts=dts_c, pairs=pairs, cum=cum, tot=tot,
                          bgb=[_mx(xa_ref[rows, D + g * SN:D + (g + 1) * SN]) for g in range(4)],
                          cgb=[_mx(xa_ref[rows, D + 512 + g * SN:D + 512 + (g + 1) * SN]) for g in range(4)])
        units = [(c, pr) for c in order for pr in range(npair)]
        P = lambda u: pre[u[0]]["pairs"][u[1]]
        cgu = lambda u: pre[u[0]]["cgb"][u[1] // 2]
        gm = {(c, g): _nt(pre[c]["cgb"][g], pre[c]["bgb"][g]) for c in order for g in range(4)}
        y0 = {u: _nt(cgu(u), sp_ref[u[0], u[1]]) for u in units}
        dcg_i = {u: _nn(P(u)["dy0b"], sp_ref[u[0], u[1]]) for u in units}
        dsl = {u: _tn(P(u)["dy0b"], cgu(u)) for u in units}
        w_ = {(u, q): gm[(u[0], u[1] // 2)] * P(u)["decs"][q] for u in units for q in range(2)}
        dw_ = {(u, q): jnp.where(mb, _nt(P(u)["dyq"][q], P(u)["dtxb"]), 0.0) for u in units for q in range(2)}
        ddtx_i = {(u, q): _tn(w_[(u, q)], P(u)["dyq"][q]) for u in units for q in range(2)}
        for c in order:
            rows = slice(c * SC, (c + 1) * SC)
            pc = pre[c]
            lane, arow, dts_c = pc["lane"], pc["arow"], pc["dts"]
            d1 = jnp.zeros((SC, 128), F32)
            d2 = jnp.zeros((SC, 128), F32)
            dz = jnp.zeros((SC, 128), F32)
            ddt = jnp.zeros((SC, 128), F32)
            dtot = jnp.zeros((1, 128), F32)
            dgm = [jnp.zeros((SC, SC), F32) for _ in range(4)]
            dbg = [jnp.zeros((SC, SN), F32) for _ in range(4)]
            dcg = [jnp.zeros((SC, SN), F32) for _ in range(4)]
            for pr in range(npair):
                u, g, p = (c, pr), pr // 2, pc["pairs"][pr]
                hs = _head_lanes(*p["cols"])
                dso = dst[pr]
                dsob = _mx(dso)
                dxe = _nt(pc["bgb"][g], dsob)
                dbg[g] = dbg[g] + _nn(p["xeb"], dsob)
                ddtx = dxe * p["e2_pair"]
                d2 = d2 + _dot2(dxe * p["dtx"], hs)
                dcg[g] = dcg[g] + dcg_i[u]
                d1 = d1 + _dot2(p["dyp"] * y0[u], hs)
                sprod = dso * sp_ref[c, pr].astype(F32)
                dst[pr] = dso * p["etot_col"] + dsl[u]
                for q in range(2):
                    hm = lo if q == 0 else ~lo
                    col = p["cols"][q]
                    dw = dw_[(u, q)]
                    ddtx = ddtx + jnp.where(hm, ddtx_i[(u, q)], 0.0)
                    dgm[g] = dgm[g] + dw * p["decs"][q]
                    z = dw * w_[(u, q)]
                    dz = dz + _dot2(z, _one_lane(col))
                    zc_scr[col:col + 1, :] = _colsum(z)
                    tsum = _rowsum(_colsum(sprod[q * SP:(q + 1) * SP, :]))
                    dtot = jnp.where(lane == col, tsum * p["etot_c"][q], dtot)
                dxs = ddtx * p["dt_pair"]
                ddt = ddt + _dot2(ddtx * p["xs"], hs)
                if last:
                    dxs = dxs + dxp_ref[rows, pr * 128:(pr + 1) * 128]
                dxa_ref[rows, pr * 128:(pr + 1) * 128] = dxs
            e2_all = jnp.exp(pc["tot"] - pc["cum"])
            dcum = dz - zc_scr[...].T + d1 * jnp.exp(pc["cum"]) - d2 * e2_all
            dtot = dtot + _colsum(d2 * e2_all)
            for g in range(4):
                db = dbg[g] + _tn(dgm[g], pc["cgb"][g])
                dc = dcg[g] + _nn(dgm[g], pc["bgb"][g])
                if last:
                    db = db + dxp_ref[rows, D + g * SN:D + (g + 1) * SN]
                    dc = dc + dxp_ref[rows, D + 512 + g * SN:D + 512 + (g + 1) * SN]
                dxa_ref[rows, D + g * SN:D + (g + 1) * SN] = db
                dxa_ref[rows, D + 512 + g * SN:D + 512 + (g + 1) * SN] = dc
            dla = _dot01(mt01, dcum) + dtot
            ddt = ddt + dla * arow
            da_ref[0:1, :] += _colsum(dla * dts_c)
            if last:
                ddt = ddt + ddp_ref[rows, :]
            ddts_ref[rows, :] = ddt

    blk = lambda s: _blk(nb - 1 - s, nb, rev)
    in_specs = [pl.BlockSpec((TB, 2048), lambda s: (blk(s), 0)), pl.BlockSpec((TB, 128), lambda s: (blk(s), 0)),
                _full((8, 128)), pl.BlockSpec((nch, npair, 128, SN), lambda s: (blk(s), 0, 0, 0)),
                pl.BlockSpec((TB, D), lambda s: (jnp.minimum(blk(s), nb - 2), 0))]
    args = [xa, dts, alog, sprev, dy]
    if last:
        in_specs += [pl.BlockSpec((TB, 2048), lambda s: (blk(s), 0)), pl.BlockSpec((TB, 128), lambda s: (blk(s), 0))]
        args += list(prev)
    call = dict(
        body=body, args=args, name=f"ssd_bwd_{d}",
        out_shape=(jax.ShapeDtypeStruct((t_total, 2048), F32), jax.ShapeDtypeStruct((t_total, 128), F32),
                   jax.ShapeDtypeStruct((8, 128), F32)),
        grid=(nb,), in_specs=in_specs,
        out_specs=(pl.BlockSpec((TB, 2048), lambda s: (blk(s), 0)), pl.BlockSpec((TB, 128), lambda s: (blk(s), 0)),
                   _full((8, 128))),
        scratch=[pltpu.VMEM((npair, 128, SN), F32), pltpu.VMEM((128, 128), F32)], sem=("arbitrary",), vmem_mb=48)
    return _run(_carry(call, comm, lambda: (pl.program_id(0) == 0, pl.program_id(0) == nb - 1)))


def _readout(o, g, yy, z, vec_ref):
    hg, ss, keep = [], [], []
    for h in range(NH):
        cs = slice(h * HF, (h + 1) * HF)
        oh = o[:, cs]
        r = lax.rsqrt(jnp.mean(oh * oh, axis=1, keepdims=True) + EPS)
        hg.append(oh * r * vec_ref[0:1, cs] * _silu(g[:, cs]))
        keep.append(r)
    u = yy * _silu(z)
    for gi in range(4):
        cs = slice(gi * 256, (gi + 1) * 256)
        ug = u[:, cs]
        r = lax.rsqrt(jnp.mean(ug * ug, axis=1, keepdims=True) + EPS)
        ss.append(ug * r * vec_ref[2:3, cs])
        keep.append(r)
    return jnp.concatenate(hg, axis=1), jnp.concatenate(ss, axis=1), keep, u


def _mix_out(o_f, o_b, p_main, y_f, y_b, xa, x, vecs, w_out):
    n = x.shape[0]

    def body(of_ref, ob_ref, g_ref, z_ref, yf_ref, yb_ref, xs_ref, x_ref, vec_ref, w_ref,
             ymix_ref, ylat_ref, h1_ref, u2_ref):
        o = of_ref[...].astype(F32) + ob_ref[...].astype(F32)
        yy = yf_ref[...].astype(F32) + yb_ref[...].astype(F32) + vec_ref[1:2, :] * xs_ref[...].astype(F32)
        hg, ss, _, _ = _readout(o, g_ref[...].astype(F32), yy, z_ref[...].astype(F32), vec_ref)
        ymix = jnp.concatenate([hg, ss], axis=1).astype(MXU_DTYPE)
        ymix_ref[...] = ymix
        ylat = _nn(ymix, w_ref[...])
        ylat_ref[...] = ylat
        h1 = x_ref[...] + vec_ref[3:4, :] * ylat
        h1_ref[...] = h1
        r = lax.rsqrt(jnp.mean(h1 * h1, axis=1, keepdims=True) + EPS)
        u2_ref[...] = ((h1 * r * vec_ref[6:7, :]) * vec_ref[4:5, :] + vec_ref[5:6, :]).astype(MXU_DTYPE)

    row = lambda j: (lambda i: (i, j))
    return _pcall(
        body, name="mix_out",
        out_shape=(jax.ShapeDtypeStruct((n, 2 * D), MXU_DTYPE), jax.ShapeDtypeStruct((n, D), F32),
                   jax.ShapeDtypeStruct((n, D), F32), jax.ShapeDtypeStruct((n, D), MXU_DTYPE)),
        grid=(n // TB,),
        in_specs=[pl.BlockSpec((TB, D), row(0)), pl.BlockSpec((TB, D), row(0)), pl.BlockSpec((TB, D), row(4)),
                  pl.BlockSpec((TB, D), row(5)), pl.BlockSpec((TB, D), row(0)), pl.BlockSpec((TB, D), row(0)),
                  pl.BlockSpec((TB, D), row(0)), pl.BlockSpec((TB, D), row(0)), _full((8, D)), _full((2 * D, D))],
        out_specs=(pl.BlockSpec((TB, 2 * D), row(0)), pl.BlockSpec((TB, D), row(0)), pl.BlockSpec((TB, D), row(0)),
                   pl.BlockSpec((TB, D), row(0))),
        sem=("parallel",), vmem_mb=48,
    )(o_f, o_b, p_main, p_main, y_f, y_b, xa, x, vecs, w_out)


def _mix_bwd(dylat, o_f, o_b, p_main, y_f, y_b, xa, vecs, w_out):
    n = dylat.shape[0]
    t_total = p_main.shape[0]
    nlat = n // TB

    def body(*refs):
        dg_ref, dz_ref, acc_ref = refs[11], refs[13], refs[15]
        i = pl.program_id(0)

        @pl.when(i == 0)
        def _():
            acc_ref[...] = jnp.zeros_like(acc_ref)

        @pl.when(i < nlat)
        def _():
            compute(*refs)

        @pl.when(i == nlat)
        def _():
            dg_ref[...] = jnp.zeros_like(dg_ref)
            dz_ref[...] = jnp.zeros_like(dz_ref)

    def compute(dyl_ref, of_ref, ob_ref, g_ref, z_ref, yf_ref, yb_ref, xs_ref, vec_ref, w_ref,
                do_ref, dg_ref, dys_ref, dz_ref, dxs_ref, acc_ref):
        dymix = _nt(dyl_ref[...], w_ref[...])
        o = of_ref[...].astype(F32) + ob_ref[...].astype(F32)
        g = g_ref[...].astype(F32)
        z = z_ref[...].astype(F32)
        xs = xs_ref[...].astype(F32)
        yy = yf_ref[...].astype(F32) + yb_ref[...].astype(F32) + vec_ref[1:2, :] * xs
        _, _, keep, u = _readout(o, g, yy, z, vec_ref)
        do_l, dg_l = [], []
        for h in range(NH):
            cs = slice(h * HF, (h + 1) * HF)
            oh, gh, r, wv = o[:, cs], g[:, cs], keep[h], vec_ref[0:1, cs]
            dhg = dymix[:, cs]
            xh = oh * r
            dn = dhg * _silu(gh)
            dg_l.append(dhg * xh * wv * _dsilu(gh))
            acc_ref[0:1, cs] += _colsum(dn * xh)
            dxh = dn * wv
            do_l.append(r * (dxh - xh * jnp.mean(dxh * xh, axis=1, keepdims=True)))
        du_l = []
        for gi in range(4):
            cs = slice(gi * 256, (gi + 1) * 256)
            ug, r, wv = u[:, cs], keep[NH + gi], vec_ref[2:3, cs]
            dss = dymix[:, D + gi * 256:D + (gi + 1) * 256]
            xh = ug * r
            acc_ref[2:3, cs] += _colsum(dss * xh)
            dxh = dss * wv
            du_l.append(r * (dxh - xh * jnp.mean(dxh * xh, axis=1, keepdims=True)))
        du = jnp.concatenate(du_l, axis=1)
        dyy = du * _silu(z)
        do_ref[...] = jnp.concatenate(do_l, axis=1).astype(do_ref.dtype)
        dg_ref[...] = jnp.concatenate(dg_l, axis=1).astype(dg_ref.dtype)
        dys_ref[...] = dyy.astype(dys_ref.dtype)
        dz_ref[...] = (du * yy * _dsilu(z)).astype(dz_ref.dtype)
        dxs_ref[...] = (dyy * vec_ref[1:2, :]).astype(dxs_ref.dtype)
        acc_ref[1:2, :] += _colsum(dyy * xs)

    row = lambda j: (lambda i: (jnp.minimum(i, nlat - 1), j))
    lat = pl.BlockSpec((TB, D), row(0))
    tok = pl.BlockSpec((TB, D), lambda i: (i, 0))
    return _pcall(
        body, name="mix_bwd",
        out_shape=(jax.ShapeDtypeStruct((n, D), MXU_DTYPE), jax.ShapeDtypeStruct((t_total, D), MXU_DTYPE),
                   jax.ShapeDtypeStruct((n, D), MXU_DTYPE), jax.ShapeDtypeStruct((t_total, D), MXU_DTYPE),
                   jax.ShapeDtypeStruct((n, D), MXU_DTYPE), jax.ShapeDtypeStruct((8, D), F32)),
        grid=(t_total // TB,),
        in_specs=[lat, lat, lat, pl.BlockSpec((TB, D), row(4)), pl.BlockSpec((TB, D), row(5)), lat, lat, lat,
                  _full((8, D)), _full((2 * D, D))],
        out_specs=(lat, tok, lat, tok, lat, _full((8, D))),
        sem=("arbitrary",), vmem_mb=48,
    )(dylat, o_f, o_b, p_main, p_main, y_f, y_b, xa, vecs, w_out)


def _ffn_up(u2, w_gate, w_up):
    n = u2.shape[0]
    tb = 1024

    def body(u_ref, wg_ref, wu_ref, g_ref, up_ref, a_ref):
        uv = u_ref[...]
        gt = _nt(uv, wg_ref[...])
        upv = _nt(uv, wu_ref[...])
        g_ref[...] = gt.astype(g_ref.dtype)
        up_ref[...] = upv.astype(up_ref.dtype)
        a_ref[...] = (_silu(gt) * upv).astype(a_ref.dtype)

    blk = pl.BlockSpec((tb, FSL), lambda j, i: (i, j))
    wblk = pl.BlockSpec((None, FSL, D), lambda j, i: (j, 0, 0))
    return _pcall(
        body, name="ffn_up",
        out_shape=(jax.ShapeDtypeStruct((n, DFFP), MXU_DTYPE),) * 3,
        grid=(4, n // tb), in_specs=[pl.BlockSpec((tb, D), lambda j, i: (i, 0)), wblk, wblk],
        out_specs=(blk, blk, blk), sem=("parallel", "parallel"), vmem_mb=48,
    )(u2, w_gate, w_up)


def _ffn_down_loss(act, w_down, h1, tgt, vecs):
    n = act.shape[0]
    tb = 512

    def body(a_ref, w_ref, h1_ref, t_ref, vec_ref, dh2_ref, dffn_ref, acc_ref):
        i = pl.program_id(0)

        @pl.when(i == 0)
        def _():
            acc_ref[...] = jnp.zeros_like(acc_ref)

        g2 = vec_ref[0:1, :]
        fw = vec_ref[1:2, :]
        nsub = 4
        sb = tb // nsub
        wv = w_ref[...]
        ffns = [_nn(a_ref[r_ * sb:(r_ + 1) * sb, :], wv) for r_ in range(nsub)]
        for r_ in range(nsub):
            rows = slice(r_ * sb, (r_ + 1) * sb)
            ffn = ffns[r_]
            h2 = h1_ref[rows, :] + g2 * ffn
            r = lax.rsqrt(jnp.mean(h2 * h2, axis=1, keepdims=True) + EPS)
            xh = h2 * r
            err = xh * fw - t_ref[rows, :]
            dy = err * (1.0 / D)
            acc_ref[2:3, :] += _colsum(err * err) * (0.5 / D)
            acc_ref[1:2, :] += _colsum(dy * xh)
            dxh = dy * fw
            dh2 = r * (dxh - xh * jnp.mean(dxh * xh, axis=1, keepdims=True))
            dh2_ref[rows, :] = dh2
            dffn_ref[rows, :] = (g2 * dh2).astype(dffn_ref.dtype)
            acc_ref[0:1, :] += _colsum(dh2 * ffn)

    return _pcall(
        body, name="ffn_down_loss",
        out_shape=(jax.ShapeDtypeStruct((n, D), F32), jax.ShapeDtypeStruct((n, D), MXU_DTYPE),
                   jax.ShapeDtypeStruct((8, D), F32)),
        grid=(n // tb,),
        in_specs=[pl.BlockSpec((tb, DFFP), lambda i: (i, 0)), _full((DFFP, D)), pl.BlockSpec((tb, D), lambda i: (i, 0)),
                  pl.BlockSpec((tb, D), lambda i: (i, 0)), _full((8, D))],
        out_specs=(pl.BlockSpec((tb, D), lambda i: (i, 0)), pl.BlockSpec((tb, D), lambda i: (i, 0)), _full((8, D))),
        sem=("arbitrary",), vmem_mb=48,
    )(act, w_down, h1, tgt, vecs)


def _ffn_bwd(dffn, w_down, gate, up, w_gate_t, w_up_t):
    n = dffn.shape[0]
    tb = 1024

    def body(df_ref, wd_ref, g_ref, up_ref, wg_ref, wu_ref, dg_ref, dup_ref, du_ref):
        j = pl.program_id(1)
        nsub = 4
        sb = tb // nsub
        wd, wg, wu = wd_ref[...], wg_ref[...], wu_ref[...]
        dacts = [_nt(df_ref[r * sb:(r + 1) * sb, :], wd) for r in range(nsub)]
        parts = []
        for r in range(nsub):
            rows = slice(r * sb, (r + 1) * sb)
            gt = g_ref[rows, :].astype(F32)
            upv = up_ref[rows, :].astype(F32)
            sg = _sig(gt)
            dgt = (dacts[r] * upv * (sg * (1.0 + gt * (1.0 - sg)))).astype(MXU_DTYPE)
            dupv = (dacts[r] * (gt * sg)).astype(MXU_DTYPE)
            dg_ref[rows, :] = dgt
            dup_ref[rows, :] = dupv
            parts.append(_nn(dgt, wg) + _nn(dupv, wu))
        part = jnp.concatenate(parts, axis=0)

        @pl.when(j == 0)
        def _():
            du_ref[...] = part

        @pl.when(j > 0)
        def _():
            du_ref[...] += part

    tok = pl.BlockSpec((tb, D), lambda i, j: (i, 0))
    ffb = pl.BlockSpec((tb, FSL), lambda i, j: (i, j))
    wsl = pl.BlockSpec((None, FSL, D), lambda i, j: (j, 0, 0))
    return _pcall(
        body, name="ffn_bwd",
        out_shape=(jax.ShapeDtypeStruct((n, DFFP), MXU_DTYPE), jax.ShapeDtypeStruct((n, DFFP), MXU_DTYPE),
                   jax.ShapeDtypeStruct((n, D), F32)),
        grid=(n // tb, 4),
        in_specs=[tok, pl.BlockSpec((FSL, D), lambda i, j: (j, 0)), ffb, ffb, wsl, wsl],
        out_specs=(ffb, ffb, tok), sem=("parallel", "arbitrary"), vmem_mb=48,
    )(dffn, w_down, gate, up, w_gate_t, w_up_t)


def _ffn_norm_bwd(du, h1, ylat, dh2, vecs):
    n = du.shape[0]
    tb = 512

    def body(du_ref, h1_ref, yl_ref, dh2_ref, vec_ref, dh1_ref, dyl_ref, acc_ref):
        @pl.when(pl.program_id(0) == 0)
        def _():
            acc_ref[...] = jnp.zeros_like(acc_ref)

        duv = du_ref[...]
        h1 = h1_ref[...]
        r = lax.rsqrt(jnp.mean(h1 * h1, axis=1, keepdims=True) + EPS)
        xh = h1 * r
        nw = vec_ref[2:3, :]
        acc_ref[0:1, :] += _colsum(duv)
        acc_ref[1:2, :] += _colsum(duv * xh * nw)
        dn = duv * vec_ref[1:2, :]
        acc_ref[2:3, :] += _colsum(dn * xh)
        dxh = dn * nw
        dh1 = dh2_ref[...] + r * (dxh - xh * jnp.mean(dxh * xh, axis=1, keepdims=True))
        dh1_ref[...] = dh1
        dyl_ref[...] = (vec_ref[0:1, :] * dh1).astype(dyl_ref.dtype)
        acc_ref[3:4, :] += _colsum(dh1 * yl_ref[...])

    tok = pl.BlockSpec((tb, D), lambda i: (i, 0))
    return _pcall(
        body, name="ffn_norm_bwd",
        out_shape=(jax.ShapeDtypeStruct((n, D), F32), jax.ShapeDtypeStruct((n, D), MXU_DTYPE),
                   jax.ShapeDtypeStruct((8, D), F32)),
        grid=(n // tb,), in_specs=[tok, tok, tok, tok, _full((8, D))], out_specs=(tok, tok, _full((8, D))),
        sem=("arbitrary",), vmem_mb=40,
    )(du, h1, ylat, dh2, vecs)


def _deep_rows(rows):
    return max(r for r in range(128, 2305, 128) if rows % r == 0)


def _dw(a, b, name):
    tn_rows = a.shape[0]
    bt = _deep_rows(tn_rows)
    kk, nn_ = a.shape[1], b.shape[1]
    bk = 1024 if kk % 1024 == 0 else kk
    bn = 1024 if nn_ % 1024 == 0 else nn_
    nt = tn_rows // bt

    def body(a_ref, b_ref, o_ref, acc):
        t = pl.program_id(2)
        part = _tn(a_ref[...], b_ref[...])

        @pl.when(t == 0)
        def _():
            acc[...] = part

        @pl.when(t > 0)
        def _():
            acc[...] += part

        @pl.when(t == nt - 1)
        def _():
            o_ref[...] = acc[...].astype(o_ref.dtype)

    return _pcall(
        body, name=name, out_shape=jax.ShapeDtypeStruct((kk, nn_), MXU_DTYPE), grid=(kk // bk, nn_ // bn, nt),
        in_specs=[pl.BlockSpec((bt, bk), lambda i, j, t: (t, i)), pl.BlockSpec((bt, bn), lambda i, j, t: (t, j))],
        out_specs=pl.BlockSpec((bk, bn), lambda i, j, t: (i, j)), scratch=[pltpu.VMEM((bk, bn), F32)],
        sem=("parallel", "parallel", "arbitrary"), vmem_mb=40,
    )(a, b)


def _dw_in(segs, u_all, name):
    tiles = []
    for m, s_ in enumerate(segs):
        tiles += [(m, h) for h in range(s_.shape[1] // D)]
    ntile = len(tiles)
    t_total = u_all.shape[0]
    bt = _deep_rows(t_total)
    nt = t_total // bt

    def body(u_ref, *refs):
        seg_refs, o_ref, acc = refs[:len(segs)], refs[len(segs)], refs[len(segs) + 1]
        n, t = pl.program_id(0), pl.program_id(1)
        for k, (m, _) in enumerate(tiles):
            @pl.when(n == k)
            def _(m=m):
                part = _tn(seg_refs[m][...], u_ref[...])

                @pl.when(t == 0)
                def _():
                    acc[...] = part

                @pl.when(t > 0)
                def _():
                    acc[...] += part

        @pl.when(t == nt - 1)
        def _():
            o_ref[...] = acc[...].astype(o_ref.dtype)

    def seg_spec(m):
        ks = [k for k, (mm, _) in enumerate(tiles) if mm == m]
        lo, hi = ks[0], ks[-1]
        on = lambda n: (n >= lo) & (n <= hi)
        return pl.BlockSpec((bt, D), lambda n, t: (jnp.where(on(n), t, 0), jnp.where(on(n), n - lo, 0)))

    return _pcall(
        body, name=name, out_shape=jax.ShapeDtypeStruct((1, ntile * D, D), MXU_DTYPE), grid=(ntile, nt),
        in_specs=[pl.BlockSpec((bt, D), lambda n, t: (t, 0))] + [seg_spec(m) for m in range(len(segs))],
        out_specs=pl.BlockSpec((None, D, D), lambda n, t: (0, n, 0)),
        scratch=[pltpu.VMEM((D, D), F32)], sem=("parallel", "arbitrary"), vmem_mb=56,
    )(u_all, *segs)


def _du_prenorm_bwd(segs, ddt, wi_main, wi_tail, xin, mods, dres, row_off, tb, name, comm=None):
    n = xin.shape[0]
    nt = n // tb
    off = row_off // tb
    has_dx = dres is not None

    def body(*refs):
        seg_refs = refs[:7]
        ddt_ref, w_ref, wb_ref, wdt_ref, x_ref, mod_ref = refs[7:13]
        rest = refs[13:]
        if has_dx:
            dres_ref, dx_ref, acc_ref, du_scr = rest
        else:
            acc_ref, du_scr = rest
        j, i = pl.program_id(0), pl.program_id(1)
        rows = pl.ds(pl.multiple_of(i * tb, tb), tb)

        @pl.when((i == 0) & (j == 0))
        def _():
            acc_ref[...] = jnp.zeros_like(acc_ref)

        @pl.when(j == 0)
        def _():
            du_scr[rows, :] = _nn(ddt_ref[...], wdt_ref[...])

        for k in range(8):
            if not has_dx and k in (4, 5):
                continue

            @pl.when(j == k)
            def _(k=k):
                sv = seg_refs[min(k, 6)][...]
                part = _nn(sv, w_ref[...])
                if k in (2, 4, 6):
                    part = part + _nn(sv[:, 0:WTAIL], wb_ref[...])
                du_scr[rows, :] += part

        @pl.when(j == 7)
        def _():
            du = du_scr[rows, :]
            xv = x_ref[...]
            r = lax.rsqrt(jnp.mean(xv * xv, axis=1, keepdims=True) + EPS)
            xh = xv * r
            nw = mod_ref[1:2, :]
            acc_ref[0:1, :] += _colsum(du)
            acc_ref[1:2, :] += _colsum(du * xh * nw)
            dn = du * mod_ref[0:1, :]
            acc_ref[2:3, :] += _colsum(dn * xh)
            if has_dx:
                dxh = dn * nw
                dx_ref[...] = dres_ref[...] + r * (dxh - xh * jnp.mean(dxh * xh, axis=1, keepdims=True))

    def seg_spec(k):
        if k < 6:
            return pl.BlockSpec((tb, D), lambda j, i: (jnp.where(j == k, i + off, 0), 0))
        return pl.BlockSpec((tb, D), lambda j, i: (jnp.where(j >= 6, i + off, 0), jnp.where(j >= 6, j - 6, 0)))

    last = pl.BlockSpec((tb, D), lambda j, i: (jnp.where(j == 7, i, 0), 0))
    in_specs = [seg_spec(k) for k in range(7)]
    in_specs += [pl.BlockSpec((tb, 128), lambda j, i: (jnp.where(j == 0, i + off, 0), 0))] + _w_specs()
    in_specs += [last, _full((8, D))]
    args = list(segs) + [ddt, wi_main, wi_tail, wi_tail, xin, mods]
    out_shape = [jax.ShapeDtypeStruct((8, D), F32)]
    out_specs = [_full((8, D))]
    if has_dx:
        in_specs.append(last)
        args.append(dres)
        out_shape.insert(0, jax.ShapeDtypeStruct((n, D), F32))
        out_specs.insert(0, last)
    call = dict(body=body, args=args, name=name, out_shape=tuple(out_shape), grid=(8, nt), in_specs=in_specs,
                out_specs=tuple(out_specs), scratch=[pltpu.VMEM((n, D), F32)], sem=("arbitrary", "arbitrary"),
                vmem_mb=56)
    steps = lambda: ((pl.program_id(0) == 0) & (pl.program_id(1) == 0),
                     (pl.program_id(0) == 7) & (pl.program_id(1) == nt - 1))
    return _run(_carry(call, comm, steps))


def _sum8(v):
    def body(v_ref, o_ref):
        acc = v_ref[0]
        for k in range(1, 8):
            acc = acc + v_ref[k]
        o_ref[...] = acc

    return _pcall(body, name="small_sum", out_shape=jax.ShapeDtypeStruct(v.shape[1:], F32),
                  in_specs=[pl.BlockSpec(memory_space=pltpu.VMEM)], out_specs=pl.BlockSpec(memory_space=pltpu.VMEM))(v)


def _adamw(w, m, v, g, name):
    lead = w.ndim == 3
    rows, cols = w.shape[-2:]
    rb = 256 if rows % 256 == 0 else (352 if rows % 352 == 0 else rows)
    c1 = 1.0 - B1 ** STEP
    c2 = 1.0 - B2 ** STEP

    def body(w_ref, m_ref, v_ref, g_ref, d_ref, nm_ref, nv_ref):
        gv = g_ref[...]
        mn = B1 * m_ref[...] + (1.0 - B1) * gv
        vn = B2 * v_ref[...] + (1.0 - B2) * (gv * gv)
        nm_ref[...] = mn
        nv_ref[...] = vn
        d_ref[...] = -LR * ((mn / c1) / (jnp.sqrt(vn / c2) + AEPS) + WD * w_ref[...])

    if rb == rows and rows > 1024:
        cb, steps = 256, cols // 256
        gspec = pl.BlockSpec((rows, cb), lambda i: (0, i))
        spec = pl.BlockSpec((None, rows, cb), lambda i: (0, 0, i)) if lead else gspec
    else:
        steps = rows // rb
        gspec = pl.BlockSpec((rb, cols), lambda i: (i, 0))
        spec = pl.BlockSpec((None, rb, cols), lambda i: (0, i, 0)) if lead else gspec
    return _pcall(
        body, name=name, out_shape=(jax.ShapeDtypeStruct(w.shape, F32),) * 3, grid=(steps,),
        in_specs=[spec] * 3 + [gspec], out_specs=(spec,) * 3, sem=("parallel",), vmem_mb=40,
    )(w, m, v, g)


def _rows(v, n):
    f = v.reshape(-1)
    return jnp.pad(f, (0, n * D - f.shape[0])).reshape(n, D)


def kernel(x, c, ctx, c_ctx, w_ada, b_ada, norm_mix, w_in, conv_w, conv_b, ssd_a_log, ssd_dt_bias, ssd_d, ssd_norm, hgrn_lb_raw, hgrn_norm, w_out, norm_ffn, w_gate, w_up, w_down, final_norm, loss_target, m_c_ctx, m_w_ada, m_b_ada, m_norm_mix, m_w_in, m_conv_w, m_conv_b, m_ssd_a_log, m_ssd_dt_bias, m_ssd_d, m_ssd_norm, m_hgrn_lb_raw, m_hgrn_norm, m_w_out, m_norm_ffn, m_w_gate, m_w_up, m_w_down, m_final_norm, v_c_ctx, v_w_ada, v_b_ada, v_norm_mix, v_w_in, v_conv_w, v_conv_b, v_ssd_a_log, v_ssd_dt_bias, v_ssd_d, v_ssd_norm, v_hgrn_lb_raw, v_hgrn_norm, v_w_out, v_norm_ffn, v_w_gate, v_w_up, v_w_down, v_final_norm):
    ix, iy, ic = lax.axis_index("x"), lax.axis_index("y"), lax.axis_index("c")
    chip = 2 * ix + iy
    me = 2 * chip + ic
    xl, xc, tgt = x[0], ctx[0], loss_target[0]
    n_lat, n_ctx = xl.shape[0], xc.shape[0]
    assert n_ctx == TB and n_lat % 1024 == 0
    t_total = n_lat + n_ctx
    nb = t_total // TB

    tr = lambda a: jnp.swapaxes(a, -1, -2)
    shift = [functools.partial(jnp.pad, pad_width=((8 * k, WSL + WTAIL - NSH - 8 * k), (0, 0))) for k in range(4)]
    slab = lax.switch(chip, shift, tr(w_in[0]).astype(MXU_DTYPE))
    padrows = lambda a: jnp.pad(a, ((0, FSL - DFF // 4), (0, 0))).astype(MXU_DTYPE)
    shards = [slab[:WSL], slab[WSL:], w_out[0].astype(MXU_DTYPE), padrows(tr(w_gate[0])), padrows(tr(w_up[0])),
              padrows(w_down[0])]
    own = lambda g_, s_: lax.dynamic_update_slice(g_, s_[None], (chip, 0, 0))
    pack = jnp.concatenate([c, hgrn_lb_raw.reshape(1, D), _rows(conv_w[0], 3), jnp.zeros((3, D), F32)], axis=0)
    ncol_ada = w_ada.shape[2]
    b_shard = lax.dynamic_slice(b_ada, (0, chip * ncol_ada), (1, ncol_ada))
    gath, araw, mod_all, wi_main, wi_tail = _prologue(pack, c_ctx.reshape(1, D), w_ada[0], b_shard, shards[:2])
    wi_main, wi_tail = own(wi_main, shards[0]), own(wi_tail, shards[1])
    gath = gath.reshape(8, 8, D)
    lbraw_full = gath[0::2, 1].reshape(4, 2, 2, 256).transpose(1, 2, 0, 3).reshape(4, D)
    convw_full = gath[0::2, 2:5].reshape(4, 3 * D)[:, :KCONV * 512].reshape(4, KCONV, 512).transpose(1, 0, 2)
    convw_full = convw_full.reshape(KCONV, 2048)
    lbraw8 = jnp.pad(lbraw_full, ((0, 4), (0, 0)))
    convp = jnp.concatenate([convw_full, conv_b, jnp.zeros((2, 2048), F32)], axis=0)
    dtb = jnp.pad(ssd_dt_bias.reshape(1, 32), ((0, 7), (0, 96)))
    alog = jnp.pad(ssd_a_log.reshape(1, 32), ((0, 7), (0, 96)))
    mod_all = mod_all.reshape(8, 16, ncol_ada)[0::2]
    mod_full = mod_all.transpose(1, 0, 2).reshape(16, 4 * ncol_ada)
    my_mod = lax.dynamic_slice(mod_full, (me, 0), (1, 6 * D)).reshape(6, D)
    sh1, sc1, g1, sh2, sc2, g2 = (my_mod[k:k + 1] for k in range(6))
    csh1, csc1 = mod_full[8:9, 0:D], mod_full[8:9, D:2 * D]

    zrow = jnp.zeros((1, D), F32)
    mods_lat = jnp.concatenate([1.0 + sc1, sh1, norm_mix, zrow, zrow, zrow, zrow, zrow], axis=0)
    mods_ctx = jnp.concatenate([1.0 + csc1, csh1, norm_mix, zrow, zrow, zrow, zrow, zrow], axis=0)
    outs = _inproj(xl, mods_lat, wi_main, wi_tail, t_total, 1024, 0, None, "inproj_lat",
                   comm=_comm_gather(shards[2:5]))
    wo_g, wg_g, wu_g = (own(g_, s_) for g_, s_ in zip(outs[3:], shards[2:5]))
    w_out_f = wo_g.reshape(2 * D, D)
    p_main, p_dt, u_all = _inproj(xc, mods_ctx, wi_main, wi_tail, t_total, TB, nb - 1, outs[:3], "inproj_ctx")

    o_f, hs_f, wd_g = _hgrn_fwd(p_main, lbraw8, 0, nb, comm=_comm_gather(shards[5:]))
    w_down_f = own(wd_g, shards[5]).reshape(DFFP, D)
    o_b, hs_b = _hgrn_fwd(p_main, lbraw8, 1, nb)
    xa, dts = _ssd_prep(p_main, p_dt, convp, dtb, nb)
    y_f, ss_f = _ssd_fwd(xa, dts, alog, 0, nb)
    y_b, ss_b = _ssd_fwd(xa, dts, alog, 1, nb)

    vec_mix = jnp.concatenate([jnp.tile(hgrn_norm, (1, NH)), jnp.repeat(ssd_d, SP, axis=1), ssd_norm, g1, 1.0 + sc2,
                               sh2, norm_ffn, zrow], axis=0)
    ymix, ylat, h1, u2 = _mix_out(o_f, o_b, p_main, y_f, y_b, xa, xl, vec_mix, w_out_f)
    gate, up, act = _ffn_up(u2, wg_g, wu_g)
    vec_loss = jnp.concatenate([g2, final_norm.reshape(1, D)] + [zrow] * 6, axis=0)
    dh2, dffn, acc_loss = _ffn_down_loss(act, w_down_f, h1, tgt, vec_loss)

    core_arr = jnp.reshape(ic, (1,)).astype(jnp.int32)
    chip_arr = jnp.reshape(chip, (1,)).astype(jnp.int32)
    every = (0, 4)

    def pair_stage(gs, tag):
        return list(_pair_sum(gs, _pair_exchange(gs, "grads_pair_exchange_" + tag), core_arr, "grads_pair_sum_" + tag))

    vec_ffn = jnp.concatenate([g1, 1.0 + sc2, norm_ffn] + [zrow] * 5, axis=0)
    dgate, dup, du2 = _ffn_bwd(dffn, w_down_f, gate, up, wg_g, wu_g)
    dh1, dylat, acc_ffn = _ffn_norm_bwd(du2, h1, ylat, dh2, vec_ffn)
    gw_down = _dw(act, dffn, "dw_down").reshape(4, FSL, D)
    gw_gate = _dw(dgate, u2, "dw_gate").reshape(4, FSL, D)
    gw_up = _dw(dup, u2, "dw_up").reshape(4, FSL, D)
    do, dgr, dys, dzr, dxs_skip, acc_mix = _mix_bwd(dylat, o_f, o_b, p_main, y_f, y_b, xa, vec_mix, w_out_f)
    gw_out = _dw(ymix, dylat, "dw_out").reshape(4, D // 2, D)
    pair_a, dests_a = pair_stage([gw_gate, gw_up], "a1") + pair_stage([gw_down, gw_out], "a2"), [every] * 4

    res = _hgrn_bwd(p_main, lbraw8, hs_f, do, 0, nb, None, comm=_comm_exchange(pair_a[:2], dests_a[:2]))
    (dq0, dff, dv0, dlb_f), recv_a = res[:4], list(res[4:])
    res = _hgrn_bwd(p_main, lbraw8, hs_b, do, 1, nb, (dq0, dv0), comm=_comm_exchange(pair_a[2:], dests_a[2:]))
    (dq, dfb, dv, dlb_b), recv_a = res[:4], recv_a + list(res[4:])
    gw_in = [_dw_in([dq, dff], u_all, "dw_in_0"), _dw_in([dfb, dv], u_all, "dw_in_1"),
             _dw_in([dgr, dzr], u_all, "dw_in_2")]
    pair_b, dests_b = pair_stage(gw_in, "b"), [(0, 1), (1, 2), (2, 3)]

    res = _ssd_bwd(xa, dts, alog, ss_f, dys, 0, nb, None, comm=_comm_exchange(pair_b, dests_b))
    (dxa0, ddts0, da_f), recv_b = res[:3], list(res[3:])
    dxa, ddts, da_b = _ssd_bwd(xa, dts, alog, ss_b, dys, 1, nb, (dxa0, ddts0))
    dxbc, ddt, acc_conv, acc_dtb = _ssd_prep_bwd(p_main, p_dt, convp, dtb, dxa, dxs_skip, ddts, nb)
    gw_in.append(_dw_in([dxbc], u_all, "dw_in_3"))
    gw_in_dt = _dw(ddt, u_all, "dw_in_dt")
    gw_in_tail = jnp.concatenate([g_[:, 0:WTAIL, :] for g_ in gw_in[1:]] + [gw_in_dt[None]], axis=0)
    pair_c, dests_c = pair_stage([gw_in[3], gw_in_tail], "c"), [(3, 4), every]

    segs = [dq, dff, dfb, dv, dgr, dzr, dxbc]
    bmods_lat = jnp.concatenate([1.0 + sc1, norm_mix] + [zrow] * 6, axis=0)
    bmods_ctx = jnp.concatenate([1.0 + csc1, norm_mix] + [zrow] * 6, axis=0)
    res = _du_prenorm_bwd(segs, ddt, wi_main, wi_tail, xl, bmods_lat, dh1, 0, 512, "du_lat",
                          comm=_comm_exchange(pair_c, dests_c))
    (grad_x, acc_lat), recv_c = res[:2], list(res[2:])
    (acc_ctx,) = _du_prenorm_bwd(segs, ddt, wi_main, wi_tail, xc, bmods_ctx, None, n_lat, TB, "du_ctx")

    mine = _chip_sum(pair_b + pair_c + pair_a, recv_b + recv_c + recv_a, chip_arr, dests_b + dests_c + dests_a,
                     [0, 0, 0, 0, 1, 3, 4, 5, 2])
    theirs = _pair_swap(mine)
    whole = [jnp.concatenate([jnp.where(ic == 0, m_, t_), jnp.where(ic == 0, t_, m_)], axis=0)
             for m_, t_ in zip(mine, theirs)]
    g_w_in = lax.dynamic_slice(jnp.concatenate(whole[0:2], axis=0), (8 * chip, 0), (NSH, D))
    g_w_out = whole[2]
    g_w_gate = whole[3][:DFF // 4]
    g_w_up = whole[4][:DFF // 4]
    g_w_down = whole[5][:DFF // 4]

    dmod_lat = jnp.concatenate([acc_lat[0:2], acc_ffn[3:4], acc_ffn[0:2], acc_loss[0:1]], axis=0)
    misc = jnp.concatenate([(da_f + da_b)[0, :32], jnp.zeros((96,), F32), acc_dtb[0, :32], jnp.zeros((96,), F32),
                            jnp.sum(acc_loss[2]).reshape(1), jnp.zeros((D - 257,), F32)]).reshape(1, D)
    sv = jnp.concatenate([
        dmod_lat, acc_ctx[0:2], (acc_lat[2:3] + acc_ctx[2:3]), acc_ffn[2:3], acc_loss[1:2], acc_mix[2:3],
        acc_mix[0:1], acc_mix[1:2], dlb_f[0:1], dlb_b[0:1], acc_conv[0:6].reshape(12, D), misc,
        jnp.zeros((3, D), F32)], axis=0)
    sv_all = _allgather8(sv, "small_grads_gather").reshape(8, 32, D)
    ssum = _sum8(sv_all)
    dmod_rows = sv_all[:, 0:6].reshape(8, 6 * D)
    dmod_ctx_row = jnp.concatenate([ssum[6:8].reshape(1, 2 * D), jnp.zeros((1, 4 * D), F32)], axis=1)
    dmod_full = jnp.concatenate([dmod_rows, dmod_ctx_row, jnp.zeros((7, 6 * D), F32)], axis=0)
    grad_b_ada = jnp.sum(dmod_full, axis=0, keepdims=True)
    dmod_shard = lax.dynamic_slice(dmod_full, (0, chip * ncol_ada), (16, ncol_ada))
    g_w_ada, da_part = _ada_bwd(araw, dmod_shard, w_ada[0])
    da_all = _allgather8(da_part, "ada_ctx_gather").reshape(8, 16, D)[0::2, 8]
    cc = c_ctx.reshape(1, D)
    grad_c_ctx = (jnp.sum(da_all, axis=0, keepdims=True) * _dsilu(cc)).reshape(D)

    grad_norm_mix, grad_norm_ffn, grad_final_norm = ssum[8:9], ssum[9:10], ssum[10].reshape(D)
    grad_ssd_norm = ssum[11:12]
    grad_hgrn_norm = jnp.sum(ssum[12].reshape(NH, HF), axis=0, keepdims=True)
    grad_ssd_d = jnp.sum(ssum[13].reshape(SHEADS, SP), axis=1).reshape(1, SHEADS)
    lb_full = _sig(lbraw_full[0:2] - lbraw_full[2:4])
    dr0 = ssum[14:16] * lb_full * (1.0 - lb_full)
    grad_lb_full = jnp.stack([dr0, -dr0], axis=0)
    grad_lb = lax.dynamic_slice(grad_lb_full, (0, 0, chip * 256), (2, 2, 256))
    grad_conv_w = lax.dynamic_slice(ssum[16:26].reshape(KCONV, 2048), (0, chip * 512), (KCONV, 512)).reshape(1, KCONV, 512)
    grad_conv_b = ssum[26:28].reshape(1, 2048)
    a_val = -jnp.exp(ssd_a_log)
    grad_a_log = ssum[28, 0:32].reshape(1, 2, SHEADS) * a_val
    grad_dt_bias = ssum[28, 128:160].reshape(1, 2, SHEADS)
    loss = ssum[28, 256]

    small_w = [c_ctx, b_ada, norm_mix, conv_w, conv_b, ssd_a_log, ssd_dt_bias, ssd_d, ssd_norm, hgrn_lb_raw,
               hgrn_norm, norm_ffn, final_norm]
    small_m = [m_c_ctx, m_b_ada, m_norm_mix, m_conv_w, m_conv_b, m_ssd_a_log, m_ssd_dt_bias, m_ssd_d, m_ssd_norm,
               m_hgrn_lb_raw, m_hgrn_norm, m_norm_ffn, m_final_norm]
    small_v = [v_c_ctx, v_b_ada, v_norm_mix, v_conv_w, v_conv_b, v_ssd_a_log, v_ssd_dt_bias, v_ssd_d, v_ssd_norm,
               v_hgrn_lb_raw, v_hgrn_norm, v_norm_ffn, v_final_norm]
    small_g = [grad_c_ctx, grad_b_ada, grad_norm_mix, grad_conv_w, grad_conv_b, grad_a_log, grad_dt_bias, grad_ssd_d,
               grad_ssd_norm, grad_lb, grad_hgrn_norm, grad_norm_ffn, grad_final_norm]
    nrows = [-(-a.size // D) for a in small_w]
    packs = lambda lst: jnp.concatenate([_rows(a, r) for a, r in zip(lst, nrows)]
                                        + [jnp.zeros((24 - sum(nrows), D), F32)], axis=0)
    sd, sm, svv = _adamw(packs(small_w), packs(small_m), packs(small_v), packs(small_g), "adamw_small")

    def unpack(p):
        out, r0 = [], 0
        for a, r in zip(small_w, nrows):
            out.append(p[r0:r0 + r].reshape(-1)[:a.size].reshape(a.shape))
            r0 += r
        return out

    sd, sm, svv = unpack(sd), unpack(sm), unpack(svv)
    big = {}
    for nm, w_, m_, v_, g_ in (("w_ada", w_ada, m_w_ada, v_w_ada, g_w_ada), ("w_in", w_in, m_w_in, v_w_in, g_w_in),
                               ("w_out", w_out, m_w_out, v_w_out, g_w_out),
                               ("w_gate", w_gate, m_w_gate, v_w_gate, g_w_gate),
                               ("w_up", w_up, m_w_up, v_w_up, g_w_up),
                               ("w_down", w_down, m_w_down, v_w_down, g_w_down)):
        if nm in ("w_in", "w_gate", "w_up"):
            big[nm] = tuple(tr(t) for t in (g_[None],) + tuple(_adamw(tr(w_), tr(m_), tr(v_), g_, "adamw_" + nm)))
        else:
            big[nm] = (g_[None],) + tuple(_adamw(w_, m_, v_, g_, "adamw_" + nm))

    order = ["c_ctx", "w_ada", "b_ada", "norm_mix", "w_in", "conv_w", "conv_b", "ssd_a_log", "ssd_dt_bias", "ssd_d",
             "ssd_norm", "hgrn_lb_raw", "hgrn_norm", "w_out", "norm_ffn", "w_gate", "w_up", "w_down", "final_norm"]
    small_names = ["c_ctx", "b_ada", "norm_mix", "conv_w", "conv_b", "ssd_a_log", "ssd_dt_bias", "ssd_d", "ssd_norm",
                   "hgrn_lb_raw", "hgrn_norm", "norm_ffn", "final_norm"]
    table = dict(big)
    for k, nm in enumerate(small_names):
        table[nm] = (small_g[k].reshape(small_w[k].shape), sd[k], sm[k], svv[k])
    grads = [table[nm][0] for nm in order]
    deltas = [table[nm][1] for nm in order]
    new_m = [table[nm][2] for nm in order]
    new_v = [table[nm][3] for nm in order]
    return (loss, grad_x[None], *grads, *deltas, *new_m, *new_v)
```

```python
import functools
import math

import jax
import jax.numpy as jnp
from jax import lax
from jax.experimental import pallas as pl
from jax.experimental.pallas import tpu as pltpu

F32 = jnp.float32
BF16 = jnp.bfloat16
MXU_DTYPE = jnp.bfloat16
_INTERPRET = False

D = 1024
NH, HF = 8, 128
HC = 64
SC = 128
SN = 128
SHEADS, SP = 16, 64
GRID_W = 64
KCONV = 5
DFF = 2816
FSL = 768
DFFP = 4 * FSL
NIN = 8224
TB = 256
EPS = 1e-6
LR, B1, B2, AEPS, WD, STEP = 0.001, 0.9, 0.999, 1e-08, 0.01, 10
MESH_ID = pl.DeviceIdType.MESH
NSH = NIN // 4
WSL = 2048
WTAIL = 128


def _pcall(body, *, name, out_shape, grid=(), in_specs=None, out_specs=None, scratch=(), sem=None,
           vmem_mb=None, aliases=None):
    params = {}
    if sem is not None:
        params["dimension_semantics"] = sem
    if vmem_mb is not None:
        params["vmem_limit_bytes"] = vmem_mb << 20
    kw = dict(name=name, out_shape=out_shape, scratch_shapes=list(scratch),
              input_output_aliases=aliases or {}, compiler_params=pltpu.CompilerParams(**params),
              interpret=_INTERPRET)
    if grid:
        kw["grid"] = grid
    if in_specs is not None:
        kw["in_specs"] = in_specs
    if out_specs is not None:
        kw["out_specs"] = out_specs
    return pl.pallas_call(body, **kw)


def _mx(a):
    return a.astype(MXU_DTYPE)


def _dg(a, b, ca, cb):
    return lax.dot_general(_mx(a), _mx(b), (((ca,), (cb,)), ((), ())), preferred_element_type=F32)


def _nn(a, b):
    return _dg(a, b, 1, 0)


def _nt(a, b):
    return _dg(a, b, 1, 1)


def _tn(a, b):
    return _dg(a, b, 0, 0)


def _dot01(m, x):
    hi = x.astype(BF16)
    r1 = x - hi.astype(F32)
    mid = r1.astype(BF16)
    lo = (r1 - mid.astype(F32)).astype(BF16)
    f = lambda t: lax.dot_general(m, t, (((1,), (0,)), ((), ())), preferred_element_type=F32)
    return f(hi) + f(mid) + f(lo)


def _tri(n, upper):
    r = lax.broadcasted_iota(jnp.int32, (n, n), 0)
    c = lax.broadcasted_iota(jnp.int32, (n, n), 1)
    return (c >= r) if upper else (c <= r)


def _b01(mask):
    return jnp.where(mask, 1.0, 0.0).astype(BF16)


def _sig(x):
    return jax.nn.sigmoid(x)


def _silu(x):
    return x * _sig(x)


def _dsilu(x):
    s = _sig(x)
    return s * (1.0 + x * (1.0 - s))


def _softplus(x):
    return jnp.maximum(x, 0.0) + jnp.log(1.0 + jnp.exp(-jnp.abs(x)))


def _rowsum(x):
    return jnp.sum(x, axis=1, keepdims=True)


def _colsum(x):
    return jnp.sum(x, axis=0, keepdims=True)


def _full(shape):
    return pl.BlockSpec(shape, lambda *_: (0,) * len(shape))


def _allgather8_ops(x_ref, out_ref, send_sems, recv_sems, local_sem):
    m_per = x_ref.shape[0]
    x, y, c = lax.axis_index("x"), lax.axis_index("y"), lax.axis_index("c")
    me, sibling = (x, y, c), (x, y, 1 - c)
    chips = [(1 - x, y), (x, 1 - y), (1 - x, 1 - y)]

    def rows(px, py, pc):
        return out_ref.at[pl.ds((4 * px + 2 * py + pc) * m_per, m_per), :]

    def copy(k, block, to, src=None):
        return pltpu.make_async_remote_copy(
            src_ref=rows(*block) if src is None else src, dst_ref=rows(*block),
            send_sem=send_sems.at[k], recv_sem=recv_sems.at[k], device_id=to, device_id_type=MESH_ID)

    mine = pltpu.make_async_copy(x_ref, rows(*me), local_sem)
    mine.start()
    first = [copy(0, me, sibling, src=x_ref)]
    first += [copy(1 + j, me, (*chip, c), src=x_ref) for j, chip in enumerate(chips)]
    for cp in first:
        cp.start()
    passed = [copy(4 + j, (*chip, c), sibling) for j, chip in enumerate(chips)]
    for j, chip in enumerate(chips):
        copy(1 + j, (*chip, c), me).wait_recv()
        passed[j].start()
    copy(0, sibling, me).wait_recv()
    for j, chip in enumerate(chips):
        copy(4 + j, (*chip, 1 - c), me).wait_recv()
    for cp in first + passed:
        cp.wait_send()
    mine.wait()


_AG8_SEMS = [pltpu.SemaphoreType.DMA((7,)), pltpu.SemaphoreType.DMA((7,)), pltpu.SemaphoreType.DMA]


def _allgather8(v, name):
    m_per, n = v.shape
    return _pcall(
        functools.partial(_allgather8_ops), name=name, out_shape=jax.ShapeDtypeStruct((8 * m_per, n), v.dtype),
        in_specs=[pl.BlockSpec(memory_space=pltpu.VMEM)], out_specs=pl.BlockSpec(memory_space=pltpu.VMEM),
        scratch=list(_AG8_SEMS),
    )(v)


def _prologue(pack, cc_row, w_ada, b_shard, shards):
    n = len(shards)
    ncol = w_ada.shape[1]

    def body(pack_ref, cc_ref, w_ref, b_ref, *refs):
        ins = refs[:n]
        gath_ref, araw_ref, mod_ref = refs[n:n + 3]
        outs = refs[n + 3:2 * n + 3]
        modsh, s1, r1, l1, s2, r2, l2, gs, gr = refs[2 * n + 3:]
        start, finish = _gather_ops(ins, outs, gs, gr, relay=True)
        start()
        _allgather8_ops(pack_ref, gath_ref, s1, r1, l1)
        a = jnp.concatenate([gath_ref[8 * i:8 * i + 1, :] for i in range(8)] + [cc_ref[...], jnp.zeros((7, D), F32)],
                            axis=0)
        araw_ref[...] = a
        modsh[...] = _nn(_silu(a), w_ref[...]) + b_ref[...]
        _allgather8_ops(modsh, mod_ref, s2, r2, l2)
        finish()

    vm = pl.BlockSpec(memory_space=pltpu.VMEM)
    anyspec = pl.BlockSpec(memory_space=pl.ANY)
    return _pcall(
        body, name="prologue",
        out_shape=(jax.ShapeDtypeStruct((64, D), F32), jax.ShapeDtypeStruct((16, D), F32),
                   jax.ShapeDtypeStruct((128, ncol), F32)) + _gather_out(shards),
        in_specs=[vm, vm, vm, vm] + [anyspec] * n, out_specs=(vm, vm, vm) + (anyspec,) * n,
        scratch=[pltpu.VMEM((16, ncol), F32)] + list(_AG8_SEMS) + list(_AG8_SEMS) + _gather_sems(n), vmem_mb=40,
    )(pack, cc_row, w_ada, b_shard, *shards)


def _gather_ops(ins, outs, send_sems, recv_sems, relay=False):
    n = len(ins)
    x, y, c = lax.axis_index("x"), lax.axis_index("y"), lax.axis_index("c")
    me, sibling = (x, y, c), (x, y, 1 - c)
    chips = [(1 - x, y), (x, 1 - y), (1 - x, 1 - y)]
    direct = 2 if relay else 3

    def part(a, px, py, pc, quarter=None):
        half = ins[a].shape[0] // 2
        if quarter is None:
            return outs[a].at[2 * px + py, pl.ds(pc * half, half), :]
        return outs[a].at[2 * px + py, pl.ds(pc * half + quarter * (half // 2), half // 2), :]

    def copy(a, k, block, to, src=None, quarter=None):
        return pltpu.make_async_remote_copy(
            src_ref=part(a, *block, quarter) if src is None else src, dst_ref=part(a, *block, quarter),
            send_sem=send_sems.at[8 * a + k], recv_sem=recv_sems.at[8 * a + k], device_id=to,
            device_id_type=MESH_ID)

    def first(a, j):
        half = ins[a].shape[0] // 2
        return copy(a, j, me, (*chips[j], c), src=ins[a].at[pl.ds(c * half, half), :])

    relayed = lambda a, q: copy(a, 6 + q, (*chips[q], c), (*chips[1 - q], c), quarter=q)

    def start():
        for a in range(n):
            for j in range(direct):
                first(a, j).start()

    def finish():
        for a in range(n):
            for j in range(direct):
                copy(a, j, (*chips[j], c), me).wait_recv()
                copy(a, 3 + j, (*chips[j], c), sibling).start()
                if relay:
                    relayed(a, j).start()
            if relay:
                for q in range(2):
                    copy(a, 6 + q, (*chips[2], c), me, quarter=q).wait_recv()
                copy(a, 5, (*chips[2], c), sibling).start()
        for a in range(n):
            for j, chip in enumerate(chips):
                copy(a, 3 + j, (*chip, 1 - c), me).wait_recv()
        for a in range(n):
            for j, chip in enumerate(chips):
                if j < direct:
                    first(a, j).wait_send()
                    if relay:
                        relayed(a, j).wait_send()
                copy(a, 3 + j, (*chip, c), sibling).wait_send()

    return start, finish


def _gather_out(shards):
    return tuple(jax.ShapeDtypeStruct((4,) + s_.shape, s_.dtype) for s_ in shards)


def _gather_sems(n):
    return [pltpu.SemaphoreType.DMA((8 * n,)), pltpu.SemaphoreType.DMA((8 * n,))]


def _pair_exchange(gs, name):
    n = len(gs)

    def body(*refs):
        ins, outs = refs[:n], refs[n:2 * n]
        send_sems, recv_sems = refs[2 * n:]
        x, y, c = lax.axis_index("x"), lax.axis_index("y"), lax.axis_index("c")
        cps = []
        for a in range(n):
            half = ins[a].shape[1] // 2
            cps.append(pltpu.make_async_remote_copy(
                src_ref=ins[a].at[:, pl.ds((1 - c) * half, half), :], dst_ref=outs[a], send_sem=send_sems.at[a],
                recv_sem=recv_sems.at[a], device_id=(x, y, 1 - c), device_id_type=MESH_ID))
        for cp in cps:
            cp.start()
        for cp in cps:
            cp.wait()

    return _pcall(
        body, name=name,
        out_shape=tuple(jax.ShapeDtypeStruct((g.shape[0], g.shape[1] // 2, g.shape[2]), g.dtype) for g in gs),
        in_specs=[pl.BlockSpec(memory_space=pl.ANY)] * n, out_specs=(pl.BlockSpec(memory_space=pl.ANY),) * n,
        scratch=[pltpu.SemaphoreType.DMA((n,)), pltpu.SemaphoreType.DMA((n,))],
    )(*gs)


def _exchange_ops(ins, outs, send_sems, recv_sems, dests):
    x, y, c = lax.axis_index("x"), lax.axis_index("y"), lax.axis_index("c")
    mine = 2 * x + y
    chips = [(1 - x, y), (x, 1 - y), (1 - x, 1 - y)]

    def each(fn):
        for a in range(len(ins)):
            lo, hi = dests[a]
            for j, (px, py) in enumerate(chips):
                q = 2 * px + py
                cp = pltpu.make_async_remote_copy(
                    src_ref=ins[a].at[jnp.clip(q - lo, 0, hi - lo - 1)], dst_ref=outs[a].at[j],
                    send_sem=send_sems.at[3 * a + j], recv_sem=recv_sems.at[3 * a + j], device_id=(px, py, c),
                    device_id_type=MESH_ID)
                fn(cp, (q >= lo) & (q < hi), (mine >= lo) & (mine < hi), (lo, hi) == (0, 4))

    def start():
        def go(cp, send_ok, recv_ok, always):
            if always:
                cp.start()
            else:
                pl.when(send_ok)(cp.start)
        each(go)

    def finish():
        def go(cp, send_ok, recv_ok, always):
            if always:
                cp.wait()
            else:
                pl.when(send_ok)(cp.wait_send)
                pl.when(recv_ok)(cp.wait_recv)
        each(go)

    return start, finish


def _comm_exchange(hs, dests):
    n = len(hs)
    return (list(hs), tuple(jax.ShapeDtypeStruct((3,) + h.shape[1:], h.dtype) for h in hs),
            [pltpu.SemaphoreType.DMA((3 * n,)), pltpu.SemaphoreType.DMA((3 * n,))],
            lambda i, o, s, r: _exchange_ops(i, o, s, r, dests))


def _comm_gather(shards):
    return (list(shards), _gather_out(shards), _gather_sems(len(shards)), _gather_ops)


def _carry(call, comm, steps):
    if comm is None:
        return call
    arrays, out_shape, sems, make = comm
    n, n_in, n_out = len(arrays), len(call["args"]), len(call["out_shape"])
    body = call["body"]

    def wrapped(*refs):
        base_in, cin = refs[:n_in], refs[n_in:n_in + n]
        rest = refs[n_in + n:]
        base_out, cout, scr = rest[:n_out], rest[n_out:n_out + n], rest[n_out + n:]
        start, finish = make(cin, cout, scr[-2], scr[-1])
        first, last = steps()
        pl.when(first)(start)
        body(*base_in, *base_out, *scr[:-2])
        pl.when(last)(finish)

    anyspec = pl.BlockSpec(memory_space=pl.ANY)
    return dict(call, body=wrapped, args=list(call["args"]) + arrays,
                in_specs=list(call["in_specs"]) + [anyspec] * n,
                out_shape=tuple(call["out_shape"]) + tuple(out_shape),
                out_specs=tuple(call["out_specs"]) + (anyspec,) * n,
                scratch=list(call["scratch"]) + sems)


def _run(call):
    args = call.pop("args")
    body = call.pop("body")
    return _pcall(body, **call)(*args)


def _pair_swap(rs):
    n = len(rs)

    def body(*refs):
        ins, outs = refs[:n], refs[n:2 * n]
        send_sems, recv_sems = refs[2 * n:]
        x, y, c = lax.axis_index("x"), lax.axis_index("y"), lax.axis_index("c")
        cps = [pltpu.make_async_remote_copy(
            src_ref=ins[a], dst_ref=outs[a], send_sem=send_sems.at[a], recv_sem=recv_sems.at[a],
            device_id=(x, y, 1 - c), device_id_type=MESH_ID) for a in range(n)]
        for cp in cps:
            cp.start()
        for cp in cps:
            cp.wait()

    return _pcall(
        body, name="grads_pair_swap", out_shape=tuple(jax.ShapeDtypeStruct(r.shape, r.dtype) for r in rs),
        in_specs=[pl.BlockSpec(memory_space=pl.ANY)] * n, out_specs=(pl.BlockSpec(memory_space=pl.ANY),) * n,
        scratch=[pltpu.SemaphoreType.DMA((n,)), pltpu.SemaphoreType.DMA((n,))],
    )(*rs)


SUM_STEPS = 4


def _pair_sum(gs, recvs, core, name):
    n = len(gs)

    def body(c_ref, *refs):
        for a in range(n):
            refs[2 * n + a][...] = (refs[a][...].astype(F32) + refs[n + a][...].astype(F32)).astype(refs[2 * n + a].dtype)

    blk = lambda g: (g.shape[0], g.shape[1] // (2 * SUM_STEPS), g.shape[2])
    return pl.pallas_call(
        body, name=name,
        out_shape=tuple(jax.ShapeDtypeStruct((g.shape[0], g.shape[1] // 2, g.shape[2]), g.dtype) for g in gs),
        grid_spec=pltpu.PrefetchScalarGridSpec(
            num_scalar_prefetch=1, grid=(SUM_STEPS,),
            in_specs=[pl.BlockSpec(blk(g), lambda i, cr: (0, cr[0] * SUM_STEPS + i, 0)) for g in gs]
            + [pl.BlockSpec(blk(g), lambda i, cr: (0, i, 0)) for g in gs],
            out_specs=tuple(pl.BlockSpec(blk(g), lambda i, cr: (0, i, 0)) for g in gs)),
        compiler_params=pltpu.CompilerParams(vmem_limit_bytes=40 << 20), interpret=_INTERPRET,
    )(core, *gs, *recvs)


def _chip_sum(hs, recvs, chip, dests, slots):
    n = len(hs)
    nout = max(slots) + 1
    first = [slots.index(o) for o in range(nout)]
    every = lambda d_: d_ == (0, 4)

    def own(d_):
        if every(d_):
            return lambda i, kr: (kr[0], i, 0)
        return lambda i, kr: (0, jnp.where(kr[0] == d_[0], i, 0), 0)

    def got(d_):
        if every(d_):
            return lambda i, kr: (0, i, 0)
        return lambda i, kr: (0, jnp.where(kr[0] == d_[0], i, 0), 0)

    def body(k_ref, *refs):
        for a in range(n):
            def emit(a=a):
                acc = refs[a][0].astype(F32)
                for j in range(3):
                    acc = acc + refs[n + a][j].astype(F32)
                refs[2 * n + slots[a]][...] = acc
            if every(dests[a]):
                emit()
            else:
                pl.when(k_ref[0] == dests[a][0])(emit)

    rb = lambda h: h.shape[1] // SUM_STEPS
    return pl.pallas_call(
        body, name="grads_chip_sum",
        out_shape=tuple(jax.ShapeDtypeStruct(hs[a].shape[1:], F32) for a in first),
        grid_spec=pltpu.PrefetchScalarGridSpec(
            num_scalar_prefetch=1, grid=(SUM_STEPS,),
            in_specs=[pl.BlockSpec((1, rb(h), h.shape[2]), own(d_)) for h, d_ in zip(hs, dests)]
            + [pl.BlockSpec((3, rb(h), h.shape[2]), got(d_)) for h, d_ in zip(hs, dests)],
            out_specs=tuple(pl.BlockSpec((rb(hs[a]), hs[a].shape[2]), lambda i, kr: (i, 0)) for a in first)),
        compiler_params=pltpu.CompilerParams(vmem_limit_bytes=40 << 20), interpret=_INTERPRET,
    )(chip, *hs, *recvs)


def _ada_bwd(araw, dmod, w):
    nblk = w.shape[1] // 512

    def body(a_ref, d_ref, w_ref, gw_ref, da_ref):
        j = pl.program_id(0)
        gw_ref[...] = _tn(_silu(a_ref[...]), d_ref[...])
        part = _nt(d_ref[...], w_ref[...])

        @pl.when(j == 0)
        def _():
            da_ref[...] = part

        @pl.when(j > 0)
        def _():
            da_ref[...] += part

    return _pcall(
        body, name="ada_bwd",
        out_shape=(jax.ShapeDtypeStruct(w.shape, F32), jax.ShapeDtypeStruct((16, D), F32)), grid=(nblk,),
        in_specs=[_full((16, D)), pl.BlockSpec((16, 512), lambda j: (0, j)), pl.BlockSpec((D, 512), lambda j: (0, j))],
        out_specs=(pl.BlockSpec((D, 512), lambda j: (0, j)), _full((16, D))), sem=("arbitrary",),
    )(araw, dmod, w)


def _w_specs():
    return [pl.BlockSpec((None, D, D), lambda j, i: (j // 2, j % 2, 0)),
            pl.BlockSpec((None, WTAIL, D), lambda j, i: (jnp.maximum(j // 2 - 1, 0), 0, 0)),
            pl.BlockSpec((None, WTAIL, D), lambda j, i: (3, 0, 0))]


def _inproj(xin, mods, wi_main, wi_tail, t_total, tb, blk_off, prev, name, comm=None):
    n = xin.shape[0]
    nt = n // tb
    ncol = 8

    def body(x_ref, mod_ref, w_ref, wb_ref, wdt_ref, *rest):
        p_ref, pdt_ref, u_ref, uscr = rest[-4:]
        j, i = pl.program_id(0), pl.program_id(1)
        rows = pl.ds(pl.multiple_of(i * tb, tb), tb)

        @pl.when(j == 0)
        def _():
            xv = x_ref[...]
            r = lax.rsqrt(jnp.mean(xv * xv, axis=1, keepdims=True) + EPS)
            u = (xv * r * mod_ref[2:3, :]) * mod_ref[0:1, :] + mod_ref[1:2, :]
            ub = u.astype(MXU_DTYPE)
            uscr[rows, :] = ub
            u_ref[...] = ub
            pdt_ref[...] = _nt(ub, wdt_ref[...])

        ub = uscr[rows, :]
        pv = _nt(ub, w_ref[...])

        @pl.when((j % 2 == 1) | (j == 0))
        def _():
            p_ref[...] = pv.astype(p_ref.dtype)

        @pl.when((j % 2 == 0) & (j > 0))
        def _():
            head = pv[:, 0:WTAIL] + _nt(ub, wb_ref[...])
            p_ref[...] = jnp.concatenate([head, pv[:, WTAIL:]], axis=1).astype(p_ref.dtype)

    once = lambda j, i: (jnp.where(j == 0, i, nt - 1) + blk_off, 0)
    in_specs = [pl.BlockSpec((tb, D), lambda j, i: (jnp.where(j == 0, i, nt - 1), 0)), _full((8, D))] + _w_specs()
    args = [xin, mods, wi_main, wi_tail, wi_tail]
    aliases = None
    if prev is not None:
        in_specs += [pl.BlockSpec(memory_space=pl.ANY)] * 3
        args += list(prev)
        aliases = {5: 0, 6: 1, 7: 2}
    call = dict(
        body=body, args=args, name=name,
        out_shape=(jax.ShapeDtypeStruct((t_total, ncol * D), MXU_DTYPE), jax.ShapeDtypeStruct((t_total, 128), F32),
                   jax.ShapeDtypeStruct((t_total, D), MXU_DTYPE)),
        grid=(ncol, nt), in_specs=in_specs,
        out_specs=(pl.BlockSpec((tb, D), lambda j, i: (i + blk_off, j)), pl.BlockSpec((tb, 128), once),
                   pl.BlockSpec((tb, D), once)),
        scratch=[pltpu.VMEM((n, D), MXU_DTYPE)], sem=("arbitrary", "arbitrary"), vmem_mb=48, aliases=aliases)
    steps = lambda: ((pl.program_id(0) == 0) & (pl.program_id(1) == 0),
                     (pl.program_id(0) == ncol - 1) & (pl.program_id(1) == nt - 1))
    return _run(_carry(call, comm, steps))


def _blk(s, nb, rev):
    return jnp.where(s == 0, nb - 1, (nb - 1 - s) if rev else (s - 1))


def _hgrn_gate(fr, lbraw_ref, d):
    lb = _sig(lbraw_ref[d:d + 1, :] - lbraw_ref[2 + d:3 + d, :])
    sg = _sig(fr)
    return lb, sg, lb + (1.0 - lb) * sg


def _hgrn_fwd(p_main, lbraw, d, nb, comm=None):
    t_total = p_main.shape[0]
    rev = d == 1
    nch = TB // HC
    scale = HF ** -0.5

    def body(q_ref, f_ref, v_ref, lb_ref, o_ref, sp_ref, st):
        s = pl.program_id(0)

        @pl.when(s == 0)
        def _():
            st[...] = jnp.zeros_like(st)

        mb = _tri(HC, rev)
        m01 = _b01(mb)
        order = list(reversed(range(nch)) if rev else range(nch))
        hs_ = [slice(h * HF, (h + 1) * HF) for h in range(NH)]
        pre = {}
        for c in order:
            rows = slice(c * HC, (c + 1) * HC)
            _, _, f = _hgrn_gate(f_ref[rows, :].astype(F32), lb_ref, d)
            k = 1.0 - f
            cum = _dot01(m01, jnp.log(f))
            tot = cum[0:1, :] if rev else cum[HC - 1:HC, :]
            qd = _silu(q_ref[rows, :].astype(F32)) * scale * jnp.exp(cum)
            ki = k * jnp.exp(-cum)
            etot = jnp.exp(tot)
            pre[c] = (_mx(qd), _mx(ki), _mx(ki * etot), _mx(v_ref[rows, :]), etot)
        scs = {c: [_nt(pre[c][0][:, cs], pre[c][1][:, cs]) for cs in hs_] for c in order}
        upd = {c: [_tn(pre[c][3][:, cs], pre[c][2][:, cs]) for cs in hs_] for c in order}
        intra = {c: [_nn(jnp.where(mb, scs[c][h], 0.0), pre[c][3][:, cs]) for h, cs in enumerate(hs_)] for c in order}
        for c in order:
            rows = slice(c * HC, (c + 1) * HC)
            qdb, etot = pre[c][0], pre[c][4]
            for h, cs in enumerate(hs_):
                sth = st[h]
                stb = sth.astype(sp_ref.dtype)
                sp_ref[c, h] = stb
                o_ref[rows, cs] = (intra[c][h] + _nt(qdb[:, cs], stb)).astype(o_ref.dtype)
                st[h] = sth * etot[:, cs] + upd[c][h]

    col = lambda j: (lambda s: (_blk(s, nb, rev), j))
    call = dict(
        body=body, args=[p_main, p_main, p_main, lbraw], name=f"hgrn_fwd_{d}",
        out_shape=(jax.ShapeDtypeStruct((t_total, D), MXU_DTYPE),
                   jax.ShapeDtypeStruct((nch * nb, NH, HF, HF), MXU_DTYPE)),
        grid=(nb,),
        in_specs=[pl.BlockSpec((TB, D), col(0)), pl.BlockSpec((TB, D), col(1 + d)), pl.BlockSpec((TB, D), col(3)),
                  _full((8, D))],
        out_specs=(pl.BlockSpec((TB, D), col(0)),
                   pl.BlockSpec((nch, NH, HF, HF), lambda s: (_blk(s, nb, rev), 0, 0, 0))),
        scratch=[pltpu.VMEM((NH, HF, HF), F32)], sem=("arbitrary",), vmem_mb=40)
    return _run(_carry(call, comm, lambda: (pl.program_id(0) == 0, pl.program_id(0) == nb - 1)))


def _hgrn_bwd(p_main, lbraw, sprev, do, d, nb, prev, comm=None):
    t_total = p_main.shape[0]
    rev = d == 1
    nch = TB // HC
    scale = HF ** -0.5
    last = prev is not None
    odt = MXU_DTYPE if last else F32

    def body(q_ref, f_ref, v_ref, lb_ref, sp_ref, do_ref, *rest):
        if last:
            dqp_ref, dvp_ref = rest[:2]
            rest = rest[2:]
        dq_ref, df_ref, dv_ref, dlb_ref, dst = rest
        sp_id = pl.program_id(0)
        is_ctx = sp_id == nb - 1

        @pl.when(sp_id == 0)
        def _():
            dst[...] = jnp.zeros_like(dst)
            dlb_ref[...] = jnp.zeros_like(dlb_ref)

        mb = _tri(HC, rev)
        mbt = _tri(HC, not rev)
        m01 = _b01(mb)
        mt01 = _b01(mbt)
        order = list(range(nch) if rev else reversed(range(nch)))
        hs_ = [slice(h * HF, (h + 1) * HF) for h in range(NH)]
        pre = {}
        for c in order:
            rows = slice(c * HC, (c + 1) * HC)
            lb, sg, f = _hgrn_gate(f_ref[rows, :].astype(F32), lb_ref, d)
            k = 1.0 - f
            cum = _dot01(m01, jnp.log(f))
            tot = cum[0:1, :] if rev else cum[HC - 1:HC, :]
            e = jnp.exp(cum)
            ei = jnp.exp(-cum)
            etot = jnp.exp(tot)
            ee = ei * etot
            qraw = q_ref[rows, :].astype(F32)
            qd = _silu(qraw) * scale * e
            ki = k * ei
            ke = k * ee
            dov = jnp.where(is_ctx, 0.0, do_ref[rows, :].astype(F32))
            pre[c] = dict(lb=lb, sg=sg, f=f, e=e, ei=ei, ee=ee, etot=etot, qraw=qraw, qd=qd, ki=ki, ke=ke,
                          qdb=_mx(qd), kib=_mx(ki), keb=_mx(ke), vb=_mx(v_ref[rows, :]), dob=_mx(dov))
        units = [(c, h) for c in order for h in range(NH)]
        col = lambda u, key: pre[u[0]][key][:, hs_[u[1]]]
        pt = {u: jnp.where(mbt, _nt(col(u, "kib"), col(u, "qdb")), 0.0) for u in units}
        dp = {u: jnp.where(mb, _nt(col(u, "dob"), col(u, "vb")), 0.0) for u in units}
        dpt = {u: jnp.where(mbt, _nt(col(u, "vb"), col(u, "dob")), 0.0) for u in units}
        dv_i = {u: _nn(pt[u], col(u, "dob")) for u in units}
        dqd_ = {u: _nn(dp[u], col(u, "kib")) + _nn(col(u, "dob"), sp_ref[u[0], u[1]]) for u in units}
        dki_ = {u: _nn(dpt[u], col(u, "qdb")) for u in units}
        dsl = {u: _tn(col(u, "dob"), col(u, "qdb")) for u in units}
        for c in order:
            rows = slice(c * HC, (c + 1) * HC)
            p = pre[c]
            dv_l, dke_l, dtot_l = [], [], []
            for h, cs in enumerate(hs_):
                dso = dst[h]
                dsob = _mx(dso)
                dv_l.append(dv_i[(c, h)] + _nt(p["keb"][:, cs], dsob))
                dke_l.append(_nn(p["vb"][:, cs], dsob))
                dtot_l.append(_colsum(dso * sp_ref[c, h].astype(F32)) * p["etot"][:, cs])
                dst[h] = dso * p["etot"][:, cs] + dsl[(c, h)]
            lb, sg, f, e, ei, ee, qraw, qd, ki, ke = (p[n_] for n_ in ("lb", "sg", "f", "e", "ei", "ee", "qraw", "qd",
                                                                     "ki", "ke"))
            dqd = jnp.concatenate([dqd_[(c, h)] for h in range(NH)], axis=1)
            dki = jnp.concatenate([dki_[(c, h)] for h in range(NH)], axis=1)
            dke = jnp.concatenate(dke_l, axis=1)
            dcum = dqd * qd - dki * ki - dke * ke
            dtot = jnp.concatenate(dtot_l, axis=1) + _colsum(dke * ke)
            dk = dki * ei + dke * ee
            dlf = _dot01(mt01, dcum) + dtot
            df = dlf / f - dk
            dlb_ref[0:1, :] += _colsum(df * (1.0 - sg))
            dfr = df * (1.0 - lb) * sg * (1.0 - sg)
            dq = dqd * e * scale * _dsilu(qraw)
            dv = jnp.concatenate(dv_l, axis=1)
            if last:
                dq = dq + dqp_ref[rows, :]
                dv = dv + dvp_ref[rows, :]
            dq_ref[rows, :] = dq.astype(odt)
            dv_ref[rows, :] = dv.astype(odt)
            df_ref[rows, :] = dfr.astype(MXU_DTYPE)

    blk = lambda s: _blk(nb - 1 - s, nb, rev)
    col = lambda j: (lambda s: (blk(s), j))
    in_specs = [pl.BlockSpec((TB, D), col(0)), pl.BlockSpec((TB, D), col(1 + d)), pl.BlockSpec((TB, D), col(3)),
                _full((8, D)), pl.BlockSpec((nch, NH, HF, HF), lambda s: (blk(s), 0, 0, 0)),
                pl.BlockSpec((TB, D), lambda s: (jnp.minimum(blk(s), nb - 2), 0))]
    args = [p_main, p_main, p_main, lbraw, sprev, do]
    if last:
        in_specs += [pl.BlockSpec((TB, D), col(0))] * 2
        args += list(prev)
    call = dict(
        body=body, args=args, name=f"hgrn_bwd_{d}",
        out_shape=(jax.ShapeDtypeStruct((t_total, D), odt), jax.ShapeDtypeStruct((t_total, D), MXU_DTYPE),
                   jax.ShapeDtypeStruct((t_total, D), odt), jax.ShapeDtypeStruct((8, D), F32)),
        grid=(nb,), in_specs=in_specs,
        out_specs=(pl.BlockSpec((TB, D), col(0)), pl.BlockSpec((TB, D), col(0)), pl.BlockSpec((TB, D), col(0)),
                   _full((8, D))),
        scratch=[pltpu.VMEM((NH, HF, HF), F32)], sem=("arbitrary",), vmem_mb=48)
    return _run(_carry(call, comm, lambda: (pl.program_id(0) == 0, pl.program_id(0) == nb - 1)))


def _conv_masks(tb, is_ctx):
    seg = jnp.where(is_ctx, tb, GRID_W)
    pos = lax.broadcasted_iota(jnp.int32, (tb, 1), 0) & (seg - 1)
    return pos, seg


def _shift_rows(x, dshift, pos, seg):
    if dshift == 0:
        return x
    n = x.shape[0]
    rolled = pltpu.roll(x, (-dshift) % n, 0)
    ok = (pos + dshift >= 0) & (pos + dshift < seg)
    return jnp.where(ok, rolled, 0.0)


def _ssd_prep(p_main, p_dt, convp, dtb, nb):
    t_total = p_main.shape[0]

    def body(x_ref, dt_ref, cw_ref, dtb_ref, xa_ref, dts_ref):
        is_ctx = pl.program_id(0) == nb - 1
        pos, seg = _conv_masks(TB, is_ctx)
        xv = x_ref[...].astype(F32)
        acc = cw_ref[5:6, :] + cw_ref[2:3, :] * xv
        for kk in (0, 1, 3, 4):
            acc = acc + cw_ref[kk:kk + 1, :] * _shift_rows(xv, kk - 2, pos, seg)
        xa_ref[...] = _silu(acc).astype(xa_ref.dtype)
        dts_ref[...] = _softplus(dt_ref[...] + dtb_ref[0:1, :])

    return _pcall(
        body, name="ssd_prep",
        out_shape=(jax.ShapeDtypeStruct((t_total, 2048), MXU_DTYPE), jax.ShapeDtypeStruct((t_total, 128), F32)),
        grid=(nb,),
        in_specs=[pl.BlockSpec((TB, 2048), lambda i: (i, 3)), pl.BlockSpec((TB, 128), lambda i: (i, 0)),
                  _full((8, 2048)), _full((8, 128))],
        out_specs=(pl.BlockSpec((TB, 2048), lambda i: (i, 0)), pl.BlockSpec((TB, 128), lambda i: (i, 0))),
        sem=("parallel",), vmem_mb=32,
    )(p_main, p_dt, convp, dtb)


def _ssd_prep_bwd(p_main, p_dt, convp, dtb, dxa, dxs_skip, ddts, nb):
    t_total = p_main.shape[0]

    def body(x_ref, dt_ref, cw_ref, dtb_ref, dxa_ref, dsk_ref, ddts_ref, dx_ref, ddt_ref, dcw_ref, ddtb_ref):
        i = pl.program_id(0)
        is_ctx = i == nb - 1

        @pl.when(i == 0)
        def _():
            dcw_ref[...] = jnp.zeros_like(dcw_ref)
            ddtb_ref[...] = jnp.zeros_like(ddtb_ref)

        pos, seg = _conv_masks(TB, is_ctx)
        xv = x_ref[...].astype(F32)
        sh = {kk: _shift_rows(xv, kk - 2, pos, seg) for kk in range(KCONV)}
        acc = cw_ref[5:6, :]
        for kk in range(KCONV):
            acc = acc + cw_ref[kk:kk + 1, :] * sh[kk]
        dact = dxa_ref[...]
        dact = jnp.concatenate([dact[:, :D] + jnp.where(is_ctx, 0.0, dsk_ref[...].astype(F32)), dact[:, D:]], axis=1)
        dpre = dact * _dsilu(acc)
        dxv = cw_ref[2:3, :] * dpre
        for kk in (0, 1, 3, 4):
            dxv = dxv + cw_ref[kk:kk + 1, :] * _shift_rows(dpre, 2 - kk, pos, seg)
        dx_ref[...] = dxv.astype(dx_ref.dtype)
        for kk in range(KCONV):
            dcw_ref[kk:kk + 1, :] += _colsum(dpre * sh[kk])
        dcw_ref[5:6, :] += _colsum(dpre)
        draw = ddts_ref[...] * _sig(dt_ref[...] + dtb_ref[0:1, :])
        ddt_ref[...] = draw.astype(ddt_ref.dtype)
        ddtb_ref[0:1, :] += _colsum(draw)

    return _pcall(
        body, name="ssd_prep_bwd",
        out_shape=(jax.ShapeDtypeStruct((t_total, 2048), MXU_DTYPE), jax.ShapeDtypeStruct((t_total, 128), MXU_DTYPE),
                   jax.ShapeDtypeStruct((8, 2048), F32), jax.ShapeDtypeStruct((8, 128), F32)),
        grid=(nb,),
        in_specs=[pl.BlockSpec((TB, 2048), lambda i: (i, 3)), pl.BlockSpec((TB, 128), lambda i: (i, 0)),
                  _full((8, 2048)), _full((8, 128)), pl.BlockSpec((TB, 2048), lambda i: (i, 0)),
                  pl.BlockSpec((TB, D), lambda i: (jnp.minimum(i, nb - 2), 0)),
                  pl.BlockSpec((TB, 128), lambda i: (i, 0))],
        out_specs=(pl.BlockSpec((TB, 2048), lambda i: (i, 0)), pl.BlockSpec((TB, 128), lambda i: (i, 0)),
                   _full((8, 2048)), _full((8, 128))),
        sem=("arbitrary",), vmem_mb=40,
    )(p_main, p_dt, convp, dtb, dxa, dxs_skip, ddts)


def _dot2(x, m01):
    hi = x.astype(BF16)
    lo = (x - hi.astype(F32)).astype(BF16)
    f = lambda t: lax.dot_general(t, m01, (((1,), (0,)), ((), ())), preferred_element_type=F32)
    return f(hi) + f(lo)


def _head_lanes(c0, c1):
    p = lax.broadcasted_iota(jnp.int32, (128, 128), 0)
    l = lax.broadcasted_iota(jnp.int32, (128, 128), 1)
    return _b01(((l == c0) & (p < SP)) | ((l == c1) & (p >= SP)))


def _one_lane(col):
    return _b01(lax.broadcasted_iota(jnp.int32, (128, 128), 1) == col)


def _lane_pick(x, lane, col):
    return _rowsum(jnp.where(lane == col, x, 0.0))


def _ssd_chunk_common(dts, alog_ref, m01, rev):
    lane = lax.broadcasted_iota(jnp.int32, (1, 128), 1)
    arow = -jnp.exp(alog_ref[0:1, :])
    cum = _dot01(m01, dts * arow)
    tot = cum[0:1, :] if rev else cum[SC - 1:SC, :]
    return lane, arow, cum, cum.T, tot


def _ssd_fwd(xa, dts, alog, d, nb):
    t_total = xa.shape[0]
    rev = d == 1
    nch = TB // SC
    npair = SHEADS // 2

    def body(xa_ref, dts_ref, alog_ref, y_ref, sp_ref, st):
        s = pl.program_id(0)

        @pl.when(s == 0)
        def _():
            st[...] = jnp.zeros_like(st)

        mb = _tri(SC, rev)
        m01 = _b01(mb)
        lo = lax.broadcasted_iota(jnp.int32, (1, 128), 1) < SP
        rlo = lax.broadcasted_iota(jnp.int32, (128, 1), 0) < SP
        order = list(reversed(range(nch)) if rev else range(nch))
        pre = {}
        for c in order:
            rows = slice(c * SC, (c + 1) * SC)
            dts_c = dts_ref[rows, :]
            lane, arow, cum, cumt, tot = _ssd_chunk_common(dts_c, alog_ref, m01, rev)
            bgs = [_mx(xa_ref[rows, D + g * SN:D + (g + 1) * SN]) for g in range(4)]
            cgs = [_mx(xa_ref[rows, D + 512 + g * SN:D + 512 + (g + 1) * SN]) for g in range(4)]
            pairs = []
            for pr in range(npair):
                xs = xa_ref[rows, pr * 128:(pr + 1) * 128].astype(F32)
                cols = [16 * d + 2 * pr, 16 * d + 2 * pr + 1]
                cum_c = [_lane_pick(cum, lane, q) for q in cols]
                dt_c = [_lane_pick(dts_c, lane, q) for q in cols]
                tot_c = [_lane_pick(tot, lane, q) for q in cols]
                dtx = xs * jnp.where(lo, dt_c[0], dt_c[1])
                e1_pair = jnp.where(lo, jnp.exp(cum_c[0]), jnp.exp(cum_c[1]))
                e2_pair = jnp.where(lo, jnp.exp(tot_c[0] - cum_c[0]), jnp.exp(tot_c[1] - cum_c[1]))
                etot_col = jnp.where(rlo, jnp.exp(tot_c[0]), jnp.exp(tot_c[1]))
                decs = [jnp.where(mb, jnp.exp(cum_c[q] - cumt[cols[q]:cols[q] + 1, :]), 0.0) for q in range(2)]
                dtxq = [_mx(jnp.where(lo if q == 0 else ~lo, dtx, 0.0)) for q in range(2)]
                pairs.append(dict(e1=e1_pair, etot=etot_col, decs=decs, dtxq=dtxq, xe=_mx(dtx * e2_pair)))
            pre[c] = (bgs, cgs, pairs)
        gm = {(c, g): _nt(pre[c][1][g], pre[c][0][g]) for c in order for g in range(4)}
        upd = {(c, pr): _tn(pre[c][2][pr]["xe"], pre[c][0][pr // 2]) for c in order for pr in range(npair)}
        intra = {(c, pr): sum(_nn(gm[(c, pr // 2)] * pre[c][2][pr]["decs"][q], pre[c][2][pr]["dtxq"][q]) for q in range(2))
                 for c in order for pr in range(npair)}
        for c in order:
            rows = slice(c * SC, (c + 1) * SC)
            bgs, cgs, pairs = pre[c]
            for pr in range(npair):
                stp = st[pr]
                stb = stp.astype(sp_ref.dtype)
                sp_ref[c, pr] = stb
                y_ref[rows, pr * 128:(pr + 1) * 128] = (
                    intra[(c, pr)] + pairs[pr]["e1"] * _nt(cgs[pr // 2], stb)).astype(y_ref.dtype)
                st[pr] = stp * pairs[pr]["etot"] + upd[(c, pr)]

    blk = lambda s: _blk(s, nb, rev)
    return _pcall(
        body, name=f"ssd_fwd_{d}",
        out_shape=(jax.ShapeDtypeStruct((t_total, D), MXU_DTYPE),
                   jax.ShapeDtypeStruct((nch * nb, npair, 128, SN), MXU_DTYPE)),
        grid=(nb,),
        in_specs=[pl.BlockSpec((TB, 2048), lambda s: (blk(s), 0)), pl.BlockSpec((TB, 128), lambda s: (blk(s), 0)),
                  _full((8, 128))],
        out_specs=(pl.BlockSpec((TB, D), lambda s: (blk(s), 0)),
                   pl.BlockSpec((nch, npair, 128, SN), lambda s: (blk(s), 0, 0, 0))),
        scratch=[pltpu.VMEM((npair, 128, SN), F32)], sem=("arbitrary",), vmem_mb=40,
    )(xa, dts, alog)


def _ssd_bwd(xa, dts, alog, sprev, dy, d, nb, prev, comm=None):
    t_total = xa.shape[0]
    rev = d == 1
    nch = TB // SC
    npair = SHEADS // 2
    last = prev is not None

    def body(xa_ref, dts_ref, alog_ref, sp_ref, dy_ref, *rest):
        if last:
            dxp_ref, ddp_ref = rest[:2]
            rest = rest[2:]
        dxa_ref, ddts_ref, da_ref, dst, zc_scr = rest
        sp_id = pl.program_id(0)
        is_ctx = sp_id == nb - 1

        @pl.when(sp_id == 0)
        def _():
            dst[...] = jnp.zeros_like(dst)
            da_ref[...] = jnp.zeros_like(da_ref)
            zc_scr[...] = jnp.zeros_like(zc_scr)

        mb = _tri(SC, rev)
        m01 = _b01(mb)
        mt01 = _b01(_tri(SC, not rev))
        lo = lax.broadcasted_iota(jnp.int32, (1, 128), 1) < SP
        rlo = lax.broadcasted_iota(jnp.int32, (128, 1), 0) < SP
        order = list(range(nch) if rev else reversed(range(nch)))
        pre = {}
        for c in order:
            rows = slice(c * SC, (c + 1) * SC)
            dts_c = dts_ref[rows, :]
            lane, arow, cum, cumt, tot = _ssd_chunk_common(dts_c, alog_ref, m01, rev)
            pairs = []
            for pr in range(npair):
                xs = xa_ref[rows, pr * 128:(pr + 1) * 128].astype(F32)
                dyp = jnp.where(is_ctx, 0.0, dy_ref[rows, pr * 128:(pr + 1) * 128].astype(F32))
                cols = [16 * d + 2 * pr, 16 * d + 2 * pr + 1]
                cum_c = [_lane_pick(cum, lane, q) for q in cols]
                dt_c = [_lane_pick(dts_c, lane, q) for q in cols]
                tot_c = [_lane_pick(tot, lane, q) for q in cols]
                e1_c = [jnp.exp(cum_c[q]) for q in range(2)]
                e2_c = [jnp.exp(tot_c[q] - cum_c[q]) for q in range(2)]
                etot_c = [jnp.exp(tot_c[q]) for q in range(2)]
                dt_pair = jnp.where(lo, dt_c[0], dt_c[1])
                e1_pair = jnp.where(lo, e1_c[0], e1_c[1])
                e2_pair = jnp.where(lo, e2_c[0], e2_c[1])
                dtx = xs * dt_pair
                decs = [jnp.where(mb, jnp.exp(cum_c[q] - cumt[cols[q]:cols[q] + 1, :]), 0.0) for q in range(2)]
                dyq = [_mx(jnp.where(lo if q == 0 else ~lo, dyp, 0.0)) for q in range(2)]
                pairs.append(dict(xs=xs, dyp=dyp, cols=cols, e1_c=e1_c, e2_c=e2_c, etot_c=etot_c, dt_pair=dt_pair,
                                  e2_pair=e2_pair, etot_col=jnp.where(rlo, etot_c[0], etot_c[1]), dtx=dtx,
                                  dtxb=_mx(dtx), xeb=_mx(dtx * e2_pair), dy0b=_mx(dyp * e1_pair), decs=decs, dyq=dyq))
            pre[c] = dict(lane=lane, arow=arow, dts=dts_c, pairs=pairs, cum=cum, tot=tot,
                          bgb=[_mx(xa_ref[rows, D + g * SN:D + (g + 1) * SN]) for g in range(4)],
                          cgb=[_mx(xa_ref[rows, D + 512 + g * SN:D + 512 + (g + 1) * SN]) for g in range(4)])
        units = [(c, pr) for c in order for pr in range(npair)]
        P = lambda u: pre[u[0]]["pairs"][u[1]]
        cgu = lambda u: pre[u[0]]["cgb"][u[1] // 2]
        gm = {(c, g): _nt(pre[c]["cgb"][g], pre[c]["bgb"][g]) for c in order for g in range(4)}
        y0 = {u: _nt(cgu(u), sp_ref[u[0], u[1]]) for u in units}
        dcg_i = {u: _nn(P(u)["dy0b"], sp_ref[u[0], u[1]]) for u in units}
        dsl = {u: _tn(P(u)["dy0b"], cgu(u)) for u in units}
        w_ = {(u, q): gm[(u[0], u[1] // 2)] * P(u)["decs"][q] for u in units for q in range(2)}
        dw_ = {(u, q): jnp.where(mb, _nt(P(u)["dyq"][q], P(u)["dtxb"]), 0.0) for u in units for q in range(2)}
        ddtx_i = {(u, q): _tn(w_[(u, q)], P(u)["dyq"][q]) for u in units for q in range(2)}
        for c in order:
            rows = slice(c * SC, (c + 1) * SC)
            pc = pre[c]
            lane, arow, dts_c = pc["lane"], pc["arow"], pc["dts"]
            d1 = jnp.zeros((SC, 128), F32)
            d2 = jnp.zeros((SC, 128), F32)
            dz = jnp.zeros((SC, 128), F32)
            ddt = jnp.zeros((SC, 128), F32)
            dtot = jnp.zeros((1, 128), F32)
            dgm = [jnp.zeros((SC, SC), F32) for _ in range(4)]
            dbg = [jnp.zeros((SC, SN), F32) for _ in range(4)]
            dcg = [jnp.zeros((SC, SN), F32) for _ in range(4)]
            for pr in range(npair):
                u, g, p = (c, pr), pr // 2, pc["pairs"][pr]
                hs = _head_lanes(*p["cols"])
                dso = dst[pr]
                dsob = _mx(dso)
                dxe = _nt(pc["bgb"][g], dsob)
                dbg[g] = dbg[g] + _nn(p["xeb"], dsob)
                ddtx = dxe * p["e2_pair"]
                d2 = d2 + _dot2(dxe * p["dtx"], hs)
                dcg[g] = dcg[g] + dcg_i[u]
                d1 = d1 + _dot2(p["dyp"] * y0[u], hs)
                sprod = dso * sp_ref[c, pr].astype(F32)
                dst[pr] = dso * p["etot_col"] + dsl[u]
                for q in range(2):
                    hm = lo if q == 0 else ~lo
                    col = p["cols"][q]
                    dw = dw_[(u, q)]
                    ddtx = ddtx + jnp.where(hm, ddtx_i[(u, q)], 0.0)
                    dgm[g] = dgm[g] + dw * p["decs"][q]
                    z = dw * w_[(u, q)]
                    dz = dz + _dot2(z, _one_lane(col))
                    zc_scr[col:col + 1, :] = _colsum(z)
                    tsum = _rowsum(_colsum(sprod[q * SP:(q + 1) * SP, :]))
                    dtot = jnp.where(lane == col, tsum * p["etot_c"][q], dtot)
                dxs = ddtx * p["dt_pair"]
                ddt = ddt + _dot2(ddtx * p["xs"], hs)
                if last:
                    dxs = dxs + dxp_ref[rows, pr * 128:(pr + 1) * 128]
                dxa_ref[rows, pr * 128:(pr + 1) * 128] = dxs
            e2_all = jnp.exp(pc["tot"] - pc["cum"])
            dcum = dz - zc_scr[...].T + d1 * jnp.exp(pc["cum"]) - d2 * e2_all
            dtot = dtot + _colsum(d2 * e2_all)
            for g in range(4):
                db = dbg[g] + _tn(dgm[g], pc["cgb"][g])
                dc = dcg[g] + _nn(dgm[g], pc["bgb"][g])
                if last:
                    db = db + dxp_ref[rows, D + g * SN:D + (g + 1) * SN]
                    dc = dc + dxp_ref[rows, D + 512 + g * SN:D + 512 + (g + 1) * SN]
                dxa_ref[rows, D + g * SN:D + (g + 1) * SN] = db
                dxa_ref[rows, D + 512 + g * SN:D + 512 + (g + 1) * SN] = dc
            dla = _dot01(mt01, dcum) + dtot
            ddt = ddt + dla * arow
            da_ref[0:1, :] += _colsum(dla * dts_c)
            if last:
                ddt = ddt + ddp_ref[rows, :]
            ddts_ref[rows, :] = ddt

    blk = lambda s: _blk(nb - 1 - s, nb, rev)
    in_specs = [pl.BlockSpec((TB, 2048), lambda s: (blk(s), 0)), pl.BlockSpec((TB, 128), lambda s: (blk(s), 0)),
                _full((8, 128)), pl.BlockSpec((nch, npair, 128, SN), lambda s: (blk(s), 0, 0, 0)),
                pl.BlockSpec((TB, D), lambda s: (jnp.minimum(blk(s), nb - 2), 0))]
    args = [xa, dts, alog, sprev, dy]
    if last:
        in_specs += [pl.BlockSpec((TB, 2048), lambda s: (blk(s), 0)), pl.BlockSpec((TB, 128), lambda s: (blk(s), 0))]
        args += list(prev)
    call = dict(
        body=body, args=args, name=f"ssd_bwd_{d}",
        out_shape=(jax.ShapeDtypeStruct((t_total, 2048), F32), jax.ShapeDtypeStruct((t_total, 128), F32),
                   jax.ShapeDtypeStruct((8, 128), F32)),
        grid=(nb,), in_specs=in_specs,
        out_specs=(pl.BlockSpec((TB, 2048), lambda s: (blk(s), 0)), pl.BlockSpec((TB, 128), lambda s: (blk(s), 0)),
                   _full((8, 128))),
        scratch=[pltpu.VMEM((npair, 128, SN), F32), pltpu.VMEM((128, 128), F32)], sem=("arbitrary",), vmem_mb=48)
    return _run(_carry(call, comm, lambda: (pl.program_id(0) == 0, pl.program_id(0) == nb - 1)))


def _readout(o, g, yy, z, vec_ref):
    hg, ss, keep = [], [], []
    for h in range(NH):
        cs = slice(h * HF, (h + 1) * HF)
        oh = o[:, cs]
        r = lax.rsqrt(jnp.mean(oh * oh, axis=1, keepdims=True) + EPS)
        hg.append(oh * r * vec_ref[0:1, cs] * _silu(g[:, cs]))
        keep.append(r)
    u = yy * _silu(z)
    for gi in range(4):
        cs = slice(gi * 256, (gi + 1) * 256)
        ug = u[:, cs]
        r = lax.rsqrt(jnp.mean(ug * ug, axis=1, keepdims=True) + EPS)
        ss.append(ug * r * vec_ref[2:3, cs])
        keep.append(r)
    return jnp.concatenate(hg, axis=1), jnp.concatenate(ss, axis=1), keep, u


def _mix_out(o_f, o_b, p_main, y_f, y_b, xa, x, vecs, w_out):
    n = x.shape[0]

    def body(of_ref, ob_ref, g_ref, z_ref, yf_ref, yb_ref, xs_ref, x_ref, vec_ref, w_ref,
             ymix_ref, ylat_ref, h1_ref, u2_ref):
        o = of_ref[...].astype(F32) + ob_ref[...].astype(F32)
        yy = yf_ref[...].astype(F32) + yb_ref[...].astype(F32) + vec_ref[1:2, :] * xs_ref[...].astype(F32)
        hg, ss, _, _ = _readout(o, g_ref[...].astype(F32), yy, z_ref[...].astype(F32), vec_ref)
        ymix = jnp.concatenate([hg, ss], axis=1).astype(MXU_DTYPE)
        ymix_ref[...] = ymix
        ylat = _nn(ymix, w_ref[...])
        ylat_ref[...] = ylat
        h1 = x_ref[...] + vec_ref[3:4, :] * ylat
        h1_ref[...] = h1
        r = lax.rsqrt(jnp.mean(h1 * h1, axis=1, keepdims=True) + EPS)
        u2_ref[...] = ((h1 * r * vec_ref[6:7, :]) * vec_ref[4:5, :] + vec_ref[5:6, :]).astype(MXU_DTYPE)

    row = lambda j: (lambda i: (i, j))
    return _pcall(
        body, name="mix_out",
        out_shape=(jax.ShapeDtypeStruct((n, 2 * D), MXU_DTYPE), jax.ShapeDtypeStruct((n, D), F32),
                   jax.ShapeDtypeStruct((n, D), F32), jax.ShapeDtypeStruct((n, D), MXU_DTYPE)),
        grid=(n // TB,),
        in_specs=[pl.BlockSpec((TB, D), row(0)), pl.BlockSpec((TB, D), row(0)), pl.BlockSpec((TB, D), row(4)),
                  pl.BlockSpec((TB, D), row(5)), pl.BlockSpec((TB, D), row(0)), pl.BlockSpec((TB, D), row(0)),
                  pl.BlockSpec((TB, D), row(0)), pl.BlockSpec((TB, D), row(0)), _full((8, D)), _full((2 * D, D))],
        out_specs=(pl.BlockSpec((TB, 2 * D), row(0)), pl.BlockSpec((TB, D), row(0)), pl.BlockSpec((TB, D), row(0)),
                   pl.BlockSpec((TB, D), row(0))),
        sem=("parallel",), vmem_mb=48,
    )(o_f, o_b, p_main, p_main, y_f, y_b, xa, x, vecs, w_out)


def _mix_bwd(dylat, o_f, o_b, p_main, y_f, y_b, xa, vecs, w_out):
    n = dylat.shape[0]
    t_total = p_main.shape[0]
    nlat = n // TB

    def body(*refs):
        dg_ref, dz_ref, acc_ref = refs[11], refs[13], refs[15]
        i = pl.program_id(0)

        @pl.when(i == 0)
        def _():
            acc_ref[...] = jnp.zeros_like(acc_ref)

        @pl.when(i < nlat)
        def _():
            compute(*refs)

        @pl.when(i == nlat)
        def _():
            dg_ref[...] = jnp.zeros_like(dg_ref)
            dz_ref[...] = jnp.zeros_like(dz_ref)

    def compute(dyl_ref, of_ref, ob_ref, g_ref, z_ref, yf_ref, yb_ref, xs_ref, vec_ref, w_ref,
                do_ref, dg_ref, dys_ref, dz_ref, dxs_ref, acc_ref):
        dymix = _nt(dyl_ref[...], w_ref[...])
        o = of_ref[...].astype(F32) + ob_ref[...].astype(F32)
        g = g_ref[...].astype(F32)
        z = z_ref[...].astype(F32)
        xs = xs_ref[...].astype(F32)
        yy = yf_ref[...].astype(F32) + yb_ref[...].astype(F32) + vec_ref[1:2, :] * xs
        _, _, keep, u = _readout(o, g, yy, z, vec_ref)
        do_l, dg_l = [], []
        for h in range(NH):
            cs = slice(h * HF, (h + 1) * HF)
            oh, gh, r, wv = o[:, cs], g[:, cs], keep[h], vec_ref[0:1, cs]
            dhg = dymix[:, cs]
            xh = oh * r
            dn = dhg * _silu(gh)
            dg_l.append(dhg * xh * wv * _dsilu(gh))
            acc_ref[0:1, cs] += _colsum(dn * xh)
            dxh = dn * wv
            do_l.append(r * (dxh - xh * jnp.mean(dxh * xh, axis=1, keepdims=True)))
        du_l = []
        for gi in range(4):
            cs = slice(gi * 256, (gi + 1) * 256)
            ug, r, wv = u[:, cs], keep[NH + gi], vec_ref[2:3, cs]
            dss = dymix[:, D + gi * 256:D + (gi + 1) * 256]
            xh = ug * r
            acc_ref[2:3, cs] += _colsum(dss * xh)
            dxh = dss * wv
            du_l.append(r * (dxh - xh * jnp.mean(dxh * xh, axis=1, keepdims=True)))
        du = jnp.concatenate(du_l, axis=1)
        dyy = du * _silu(z)
        do_ref[...] = jnp.concatenate(do_l, axis=1).astype(do_ref.dtype)
        dg_ref[...] = jnp.concatenate(dg_l, axis=1).astype(dg_ref.dtype)
        dys_ref[...] = dyy.astype(dys_ref.dtype)
        dz_ref[...] = (du * yy * _dsilu(z)).astype(dz_ref.dtype)
        dxs_ref[...] = (dyy * vec_ref[1:2, :]).astype(dxs_ref.dtype)
        acc_ref[1:2, :] += _colsum(dyy * xs)

    row = lambda j: (lambda i: (jnp.minimum(i, nlat - 1), j))
    lat = pl.BlockSpec((TB, D), row(0))
    tok = pl.BlockSpec((TB, D), lambda i: (i, 0))
    return _pcall(
        body, name="mix_bwd",
        out_shape=(jax.ShapeDtypeStruct((n, D), MXU_DTYPE), jax.ShapeDtypeStruct((t_total, D), MXU_DTYPE),
                   jax.ShapeDtypeStruct((n, D), MXU_DTYPE), jax.ShapeDtypeStruct((t_total, D), MXU_DTYPE),
                   jax.ShapeDtypeStruct((n, D), MXU_DTYPE), jax.ShapeDtypeStruct((8, D), F32)),
        grid=(t_total // TB,),
        in_specs=[lat, lat, lat, pl.BlockSpec((TB, D), row(4)), pl.BlockSpec((TB, D), row(5)), lat, lat, lat,
                  _full((8, D)), _full((2 * D, D))],
        out_specs=(lat, tok, lat, tok, lat, _full((8, D))),
        sem=("arbitrary",), vmem_mb=48,
    )(dylat, o_f, o_b, p_main, p_main, y_f, y_b, xa, vecs, w_out)


def _ffn_up(u2, w_gate, w_up):
    n = u2.shape[0]
    tb = 1024

    def body(u_ref, wg_ref, wu_ref, g_ref, up_ref, a_ref):
        uv = u_ref[...]
        gt = _nt(uv, wg_ref[...])
        upv = _nt(uv, wu_ref[...])
        g_ref[...] = gt.astype(g_ref.dtype)
        up_ref[...] = upv.astype(up_ref.dtype)
        a_ref[...] = (_silu(gt) * upv).astype(a_ref.dtype)

    blk = pl.BlockSpec((tb, FSL), lambda j, i: (i, j))
    wblk = pl.BlockSpec((None, FSL, D), lambda j, i: (j, 0, 0))
    return _pcall(
        body, name="ffn_up",
        out_shape=(jax.ShapeDtypeStruct((n, DFFP), MXU_DTYPE),) * 3,
        grid=(4, n // tb), in_specs=[pl.BlockSpec((tb, D), lambda j, i: (i, 0)), wblk, wblk],
        out_specs=(blk, blk, blk), sem=("parallel", "parallel"), vmem_mb=48,
    )(u2, w_gate, w_up)


def _ffn_down_loss(act, w_down, h1, tgt, vecs):
    n = act.shape[0]
    tb = 512

    def body(a_ref, w_ref, h1_ref, t_ref, vec_ref, dh2_ref, dffn_ref, acc_ref):
        i = pl.program_id(0)

        @pl.when(i == 0)
        def _():
            acc_ref[...] = jnp.zeros_like(acc_ref)

        g2 = vec_ref[0:1, :]
        fw = vec_ref[1:2, :]
        nsub = 4
        sb = tb // nsub
        wv = w_ref[...]
        ffns = [_nn(a_ref[r_ * sb:(r_ + 1) * sb, :], wv) for r_ in range(nsub)]
        for r_ in range(nsub):
            rows = slice(r_ * sb, (r_ + 1) * sb)
            ffn = ffns[r_]
            h2 = h1_ref[rows, :] + g2 * ffn
            r = lax.rsqrt(jnp.mean(h2 * h2, axis=1, keepdims=True) + EPS)
            xh = h2 * r
            err = xh * fw - t_ref[rows, :]
            dy = err * (1.0 / D)
            acc_ref[2:3, :] += _colsum(err * err) * (0.5 / D)
            acc_ref[1:2, :] += _colsum(dy * xh)
            dxh = dy * fw
            dh2 = r * (dxh - xh * jnp.mean(dxh * xh, axis=1, keepdims=True))
            dh2_ref[rows, :] = dh2
            dffn_ref[rows, :] = (g2 * dh2).astype(dffn_ref.dtype)
            acc_ref[0:1, :] += _colsum(dh2 * ffn)

    return _pcall(
        body, name="ffn_down_loss",
        out_shape=(jax.ShapeDtypeStruct((n, D), F32), jax.ShapeDtypeStruct((n, D), MXU_DTYPE),
                   jax.ShapeDtypeStruct((8, D), F32)),
        grid=(n // tb,),
        in_specs=[pl.BlockSpec((tb, DFFP), lambda i: (i, 0)), _full((DFFP, D)), pl.BlockSpec((tb, D), lambda i: (i, 0)),
                  pl.BlockSpec((tb, D), lambda i: (i, 0)), _full((8, D))],
        out_specs=(pl.BlockSpec((tb, D), lambda i: (i, 0)), pl.BlockSpec((tb, D), lambda i: (i, 0)), _full((8, D))),
        sem=("arbitrary",), vmem_mb=48,
    )(act, w_down, h1, tgt, vecs)


def _ffn_bwd(dffn, w_down, gate, up, w_gate_t, w_up_t):
    n = dffn.shape[0]
    tb = 1024

    def body(df_ref, wd_ref, g_ref, up_ref, wg_ref, wu_ref, dg_ref, dup_ref, du_ref):
        j = pl.program_id(1)
        nsub = 4
        sb = tb // nsub
        wd, wg, wu = wd_ref[...], wg_ref[...], wu_ref[...]
        dacts = [_nt(df_ref[r * sb:(r + 1) * sb, :], wd) for r in range(nsub)]
        parts = []
        for r in range(nsub):
            rows = slice(r * sb, (r + 1) * sb)
            gt = g_ref[rows, :].astype(F32)
            upv = up_ref[rows, :].astype(F32)
            sg = _sig(gt)
            dgt = (dacts[r] * upv * (sg * (1.0 + gt * (1.0 - sg)))).astype(MXU_DTYPE)
            dupv = (dacts[r] * (gt * sg)).astype(MXU_DTYPE)
            dg_ref[rows, :] = dgt
            dup_ref[rows, :] = dupv
            parts.append(_nn(dgt, wg) + _nn(dupv, wu))
        part = jnp.concatenate(parts, axis=0)

        @pl.when(j == 0)
        def _():
            du_ref[...] = part

        @pl.when(j > 0)
        def _():
            du_ref[...] += part

    tok = pl.BlockSpec((tb, D), lambda i, j: (i, 0))
    ffb = pl.BlockSpec((tb, FSL), lambda i, j: (i, j))
    wsl = pl.BlockSpec((None, FSL, D), lambda i, j: (j, 0, 0))
    return _pcall(
        body, name="ffn_bwd",
        out_shape=(jax.ShapeDtypeStruct((n, DFFP), MXU_DTYPE), jax.ShapeDtypeStruct((n, DFFP), MXU_DTYPE),
                   jax.ShapeDtypeStruct((n, D), F32)),
        grid=(n // tb, 4),
        in_specs=[tok, pl.BlockSpec((FSL, D), lambda i, j: (j, 0)), ffb, ffb, wsl, wsl],
        out_specs=(ffb, ffb, tok), sem=("parallel", "arbitrary"), vmem_mb=48,
    )(dffn, w_down, gate, up, w_gate_t, w_up_t)


def _ffn_norm_bwd(du, h1, ylat, dh2, vecs):
    n = du.shape[0]
    tb = 512

    def body(du_ref, h1_ref, yl_ref, dh2_ref, vec_ref, dh1_ref, dyl_ref, acc_ref):
        @pl.when(pl.program_id(0) == 0)
        def _():
            acc_ref[...] = jnp.zeros_like(acc_ref)

        duv = du_ref[...]
        h1 = h1_ref[...]
        r = lax.rsqrt(jnp.mean(h1 * h1, axis=1, keepdims=True) + EPS)
        xh = h1 * r
        nw = vec_ref[2:3, :]
        acc_ref[0:1, :] += _colsum(duv)
        acc_ref[1:2, :] += _colsum(duv * xh * nw)
        dn = duv * vec_ref[1:2, :]
        acc_ref[2:3, :] += _colsum(dn * xh)
        dxh = dn * nw
        dh1 = dh2_ref[...] + r * (dxh - xh * jnp.mean(dxh * xh, axis=1, keepdims=True))
        dh1_ref[...] = dh1
        dyl_ref[...] = (vec_ref[0:1, :] * dh1).astype(dyl_ref.dtype)
        acc_ref[3:4, :] += _colsum(dh1 * yl_ref[...])

    tok = pl.BlockSpec((tb, D), lambda i: (i, 0))
    return _pcall(
        body, name="ffn_norm_bwd",
        out_shape=(jax.ShapeDtypeStruct((n, D), F32), jax.ShapeDtypeStruct((n, D), MXU_DTYPE),
                   jax.ShapeDtypeStruct((8, D), F32)),
        grid=(n // tb,), in_specs=[tok, tok, tok, tok, _full((8, D))], out_specs=(tok, tok, _full((8, D))),
        sem=("arbitrary",), vmem_mb=40,
    )(du, h1, ylat, dh2, vecs)


def _deep_rows(rows):
    return max(r for r in range(128, 2305, 128) if rows % r == 0)


def _dw(a, b, name):
    tn_rows = a.shape[0]
    bt = _deep_rows(tn_rows)
    kk, nn_ = a.shape[1], b.shape[1]
    bk = 1024 if kk % 1024 == 0 else kk
    bn = 1024 if nn_ % 1024 == 0 else nn_
    nt = tn_rows // bt

    def body(a_ref, b_ref, o_ref, acc):
        t = pl.program_id(2)
        part = _tn(a_ref[...], b_ref[...])

        @pl.when(t == 0)
        def _():
            acc[...] = part

        @pl.when(t > 0)
        def _():
            acc[...] += part

        @pl.when(t == nt - 1)
        def _():
            o_ref[...] = acc[...].astype(o_ref.dtype)

    return _pcall(
        body, name=name, out_shape=jax.ShapeDtypeStruct((kk, nn_), MXU_DTYPE), grid=(kk // bk, nn_ // bn, nt),
        in_specs=[pl.BlockSpec((bt, bk), lambda i, j, t: (t, i)), pl.BlockSpec((bt, bn), lambda i, j, t: (t, j))],
        out_specs=pl.BlockSpec((bk, bn), lambda i, j, t: (i, j)), scratch=[pltpu.VMEM((bk, bn), F32)],
        sem=("parallel", "parallel", "arbitrary"), vmem_mb=40,
    )(a, b)


def _dw_in(segs, u_all, name):
    tiles = []
    for m, s_ in enumerate(segs):
        tiles += [(m, h) for h in range(s_.shape[1] // D)]
    ntile = len(tiles)
    t_total = u_all.shape[0]
    bt = _deep_rows(t_total)
    nt = t_total // bt

    def body(u_ref, *refs):
        seg_refs, o_ref, acc = refs[:len(segs)], refs[len(segs)], refs[len(segs) + 1]
        n, t = pl.program_id(0), pl.program_id(1)
        for k, (m, _) in enumerate(tiles):
            @pl.when(n == k)
            def _(m=m):
                part = _tn(seg_refs[m][...], u_ref[...])

                @pl.when(t == 0)
                def _():
                    acc[...] = part

                @pl.when(t > 0)
                def _():
                    acc[...] += part

        @pl.when(t == nt - 1)
        def _():
            o_ref[...] = acc[...].astype(o_ref.dtype)

    def seg_spec(m):
        ks = [k for k, (mm, _) in enumerate(tiles) if mm == m]
        lo, hi = ks[0], ks[-1]
        on = lambda n: (n >= lo) & (n <= hi)
        return pl.BlockSpec((bt, D), lambda n, t: (jnp.where(on(n), t, 0), jnp.where(on(n), n - lo, 0)))

    return _pcall(
        body, name=name, out_shape=jax.ShapeDtypeStruct((1, ntile * D, D), MXU_DTYPE), grid=(ntile, nt),
        in_specs=[pl.BlockSpec((bt, D), lambda n, t: (t, 0))] + [seg_spec(m) for m in range(len(segs))],
        out_specs=pl.BlockSpec((None, D, D), lambda n, t: (0, n, 0)),
        scratch=[pltpu.VMEM((D, D), F32)], sem=("parallel", "arbitrary"), vmem_mb=56,
    )(u_all, *segs)


def _du_prenorm_bwd(segs, ddt, wi_main, wi_tail, xin, mods, dres, row_off, tb, name, comm=None):
    n = xin.shape[0]
    nt = n // tb
    off = row_off // tb
    has_dx = dres is not None

    def body(*refs):
        seg_refs = refs[:7]
        ddt_ref, w_ref, wb_ref, wdt_ref, x_ref, mod_ref = refs[7:13]
        rest = refs[13:]
        if has_dx:
            dres_ref, dx_ref, acc_ref, du_scr = rest
        else:
            acc_ref, du_scr = rest
        j, i = pl.program_id(0), pl.program_id(1)
        rows = pl.ds(pl.multiple_of(i * tb, tb), tb)

        @pl.when((i == 0) & (j == 0))
        def _():
            acc_ref[...] = jnp.zeros_like(acc_ref)

        @pl.when(j == 0)
        def _():
            du_scr[rows, :] = _nn(ddt_ref[...], wdt_ref[...])

        for k in range(8):
            if not has_dx and k in (4, 5):
                continue

            @pl.when(j == k)
            def _(k=k):
                sv = seg_refs[min(k, 6)][...]
                part = _nn(sv, w_ref[...])
                if k in (2, 4, 6):
                    part = part + _nn(sv[:, 0:WTAIL], wb_ref[...])
                du_scr[rows, :] += part

        @pl.when(j == 7)
        def _():
            du = du_scr[rows, :]
            xv = x_ref[...]
            r = lax.rsqrt(jnp.mean(xv * xv, axis=1, keepdims=True) + EPS)
            xh = xv * r
            nw = mod_ref[1:2, :]
            acc_ref[0:1, :] += _colsum(du)
            acc_ref[1:2, :] += _colsum(du * xh * nw)
            dn = du * mod_ref[0:1, :]
            acc_ref[2:3, :] += _colsum(dn * xh)
            if has_dx:
                dxh = dn * nw
                dx_ref[...] = dres_ref[...] + r * (dxh - xh * jnp.mean(dxh * xh, axis=1, keepdims=True))

    def seg_spec(k):
        if k < 6:
            return pl.BlockSpec((tb, D), lambda j, i: (jnp.where(j == k, i + off, 0), 0))
        return pl.BlockSpec((tb, D), lambda j, i: (jnp.where(j >= 6, i + off, 0), jnp.where(j >= 6, j - 6, 0)))

    last = pl.BlockSpec((tb, D), lambda j, i: (jnp.where(j == 7, i, 0), 0))
    in_specs = [seg_spec(k) for k in range(7)]
    in_specs += [pl.BlockSpec((tb, 128), lambda j, i: (jnp.where(j == 0, i + off, 0), 0))] + _w_specs()
    in_specs += [last, _full((8, D))]
    args = list(segs) + [ddt, wi_main, wi_tail, wi_tail, xin, mods]
    out_shape = [jax.ShapeDtypeStruct((8, D), F32)]
    out_specs = [_full((8, D))]
    if has_dx:
        in_specs.append(last)
        args.append(dres)
        out_shape.insert(0, jax.ShapeDtypeStruct((n, D), F32))
        out_specs.insert(0, last)
    call = dict(body=body, args=args, name=name, out_shape=tuple(out_shape), grid=(8, nt), in_specs=in_specs,
                out_specs=tuple(out_specs), scratch=[pltpu.VMEM((n, D), F32)], sem=("arbitrary", "arbitrary"),
                vmem_mb=56)
    steps = lambda: ((pl.program_id(0) == 0) & (pl.program_id(1) == 0),
                     (pl.program_id(0) == 7) & (pl.program_id(1) == nt - 1))
    return _run(_carry(call, comm, steps))


def _sum8(v):
    def body(v_ref, o_ref):
        acc = v_ref[0]
        for k in range(1, 8):
            acc = acc + v_ref[k]
        o_ref[...] = acc

    return _pcall(body, name="small_sum", out_shape=jax.ShapeDtypeStruct(v.shape[1:], F32),
                  in_specs=[pl.BlockSpec(memory_space=pltpu.VMEM)], out_specs=pl.BlockSpec(memory_space=pltpu.VMEM))(v)


def _adamw(w, m, v, g, name):
    lead = w.ndim == 3
    rows, cols = w.shape[-2:]
    rb = 256 if rows % 256 == 0 else (352 if rows % 352 == 0 else rows)
    c1 = 1.0 - B1 ** STEP
    c2 = 1.0 - B2 ** STEP

    def body(w_ref, m_ref, v_ref, g_ref, d_ref, nm_ref, nv_ref):
        gv = g_ref[...]
        mn = B1 * m_ref[...] + (1.0 - B1) * gv
        vn = B2 * v_ref[...] + (1.0 - B2) * (gv * gv)
        nm_ref[...] = mn
        nv_ref[...] = vn
        d_ref[...] = -LR * ((mn / c1) / (jnp.sqrt(vn / c2) + AEPS) + WD * w_ref[...])

    if rb == rows and rows > 1024:
        cb, steps = 256, cols // 256
        gspec = pl.BlockSpec((rows, cb), lambda i: (0, i))
        spec = pl.BlockSpec((None, rows, cb), lambda i: (0, 0, i)) if lead else gspec
    else:
        steps = rows // rb
        gspec = pl.BlockSpec((rb, cols), lambda i: (i, 0))
        spec = pl.BlockSpec((None, rb, cols), lambda i: (0, i, 0)) if lead else gspec
    return _pcall(
        body, name=name, out_shape=(jax.ShapeDtypeStruct(w.shape, F32),) * 3, grid=(steps,),
        in_specs=[spec] * 3 + [gspec], out_specs=(spec,) * 3, sem=("parallel",), vmem_mb=40,
    )(w, m, v, g)


def _rows(v, n):
    f = v.reshape(-1)
    return jnp.pad(f, (0, n * D - f.shape[0])).reshape(n, D)


def kernel(x, c, ctx, c_ctx, w_ada, b_ada, norm_mix, w_in, conv_w, conv_b, ssd_a_log, ssd_dt_bias, ssd_d, ssd_norm, hgrn_lb_raw, hgrn_norm, w_out, norm_ffn, w_gate, w_up, w_down, final_norm, loss_target, m_c_ctx, m_w_ada, m_b_ada, m_norm_mix, m_w_in, m_conv_w, m_conv_b, m_ssd_a_log, m_ssd_dt_bias, m_ssd_d, m_ssd_norm, m_hgrn_lb_raw, m_hgrn_norm, m_w_out, m_norm_ffn, m_w_gate, m_w_up, m_w_down, m_final_norm, v_c_ctx, v_w_ada, v_b_ada, v_norm_mix, v_w_in, v_conv_w, v_conv_b, v_ssd_a_log, v_ssd_dt_bias, v_ssd_d, v_ssd_norm, v_hgrn_lb_raw, v_hgrn_norm, v_w_out, v_norm_ffn, v_w_gate, v_w_up, v_w_down, v_final_norm):
    ix, iy, ic = lax.axis_index("x"), lax.axis_index("y"), lax.axis_index("c")
    chip = 2 * ix + iy
    me = 2 * chip + ic
    xl, xc, tgt = x[0], ctx[0], loss_target[0]
    n_lat, n_ctx = xl.shape[0], xc.shape[0]
    assert n_ctx == TB and n_lat % 1024 == 0
    t_total = n_lat + n_ctx
    nb = t_total // TB

    tr = lambda a: jnp.swapaxes(a, -1, -2)
    shift = [functools.partial(jnp.pad, pad_width=((8 * k, WSL + WTAIL - NSH - 8 * k), (0, 0))) for k in range(4)]
    slab = lax.switch(chip, shift, tr(w_in[0]).astype(MXU_DTYPE))
    padrows = lambda a: jnp.pad(a, ((0, FSL - DFF // 4), (0, 0))).astype(MXU_DTYPE)
    shards = [slab[:WSL], slab[WSL:], w_out[0].astype(MXU_DTYPE), padrows(tr(w_gate[0])), padrows(tr(w_up[0])),
              padrows(w_down[0])]
    own = lambda g_, s_: lax.dynamic_update_slice(g_, s_[None], (chip, 0, 0))
    pack = jnp.concatenate([c, hgrn_lb_raw.reshape(1, D), _rows(conv_w[0], 3), jnp.zeros((3, D), F32)], axis=0)
    ncol_ada = w_ada.shape[2]
    b_shard = lax.dynamic_slice(b_ada, (0, chip * ncol_ada), (1, ncol_ada))
    gath, araw, mod_all, wi_main, wi_tail = _prologue(pack, c_ctx.reshape(1, D), w_ada[0], b_shard, shards[:2])
    wi_main, wi_tail = own(wi_main, shards[0]), own(wi_tail, shards[1])
    gath = gath.reshape(8, 8, D)
    lbraw_full = gath[0::2, 1].reshape(4, 2, 2, 256).transpose(1, 2, 0, 3).reshape(4, D)
    convw_full = gath[0::2, 2:5].reshape(4, 3 * D)[:, :KCONV * 512].reshape(4, KCONV, 512).transpose(1, 0, 2)
    convw_full = convw_full.reshape(KCONV, 2048)
    lbraw8 = jnp.pad(lbraw_full, ((0, 4), (0, 0)))
    convp = jnp.concatenate([convw_full, conv_b, jnp.zeros((2, 2048), F32)], axis=0)
    dtb = jnp.pad(ssd_dt_bias.reshape(1, 32), ((0, 7), (0, 96)))
    alog = jnp.pad(ssd_a_log.reshape(1, 32), ((0, 7), (0, 96)))
    mod_all = mod_all.reshape(8, 16, ncol_ada)[0::2]
    mod_full = mod_all.transpose(1, 0, 2).reshape(16, 4 * ncol_ada)
    my_mod = lax.dynamic_slice(mod_full, (me, 0), (1, 6 * D)).reshape(6, D)
    sh1, sc1, g1, sh2, sc2, g2 = (my_mod[k:k + 1] for k in range(6))
    csh1, csc1 = mod_full[8:9, 0:D], mod_full[8:9, D:2 * D]

    zrow = jnp.zeros((1, D), F32)
    mods_lat = jnp.concatenate([1.0 + sc1, sh1, norm_mix, zrow, zrow, zrow, zrow, zrow], axis=0)
    mods_ctx = jnp.concatenate([1.0 + csc1, csh1, norm_mix, zrow, zrow, zrow, zrow, zrow], axis=0)
    outs = _inproj(xl, mods_lat, wi_main, wi_tail, t_total, 1024, 0, None, "inproj_lat",
                   comm=_comm_gather(shards[2:5]))
    wo_g, wg_g, wu_g = (own(g_, s_) for g_, s_ in zip(outs[3:], shards[2:5]))
    w_out_f = wo_g.reshape(2 * D, D)
    p_main, p_dt, u_all = _inproj(xc, mods_ctx, wi_main, wi_tail, t_total, TB, nb - 1, outs[:3], "inproj_ctx")

    o_f, hs_f, wd_g = _hgrn_fwd(p_main, lbraw8, 0, nb, comm=_comm_gather(shards[5:]))
    w_down_f = own(wd_g, shards[5]).reshape(DFFP, D)
    o_b, hs_b = _hgrn_fwd(p_main, lbraw8, 1, nb)
    xa, dts = _ssd_prep(p_main, p_dt, convp, dtb, nb)
    y_f, ss_f = _ssd_fwd(xa, dts, alog, 0, nb)
    y_b, ss_b = _ssd_fwd(xa, dts, alog, 1, nb)

    vec_mix = jnp.concatenate([jnp.tile(hgrn_norm, (1, NH)), jnp.repeat(ssd_d, SP, axis=1), ssd_norm, g1, 1.0 + sc2,
                               sh2, norm_ffn, zrow], axis=0)
    ymix, ylat, h1, u2 = _mix_out(o_f, o_b, p_main, y_f, y_b, xa, xl, vec_mix, w_out_f)
    gate, up, act = _ffn_up(u2, wg_g, wu_g)
    vec_loss = jnp.concatenate([g2, final_norm.reshape(1, D)] + [zrow] * 6, axis=0)
    dh2, dffn, acc_loss = _ffn_down_loss(act, w_down_f, h1, tgt, vec_loss)

    core_arr = jnp.reshape(ic, (1,)).astype(jnp.int32)
    chip_arr = jnp.reshape(chip, (1,)).astype(jnp.int32)
    every = (0, 4)

    def pair_stage(gs, tag):
        return list(_pair_sum(gs, _pair_exchange(gs, "grads_pair_exchange_" + tag), core_arr, "grads_pair_sum_" + tag))

    vec_ffn = jnp.concatenate([g1, 1.0 + sc2, norm_ffn] + [zrow] * 5, axis=0)
    dgate, dup, du2 = _ffn_bwd(dffn, w_down_f, gate, up, wg_g, wu_g)
    dh1, dylat, acc_ffn = _ffn_norm_bwd(du2, h1, ylat, dh2, vec_ffn)
    gw_down = _dw(act, dffn, "dw_down").reshape(4, FSL, D)
    gw_gate = _dw(dgate, u2, "dw_gate").reshape(4, FSL, D)
    gw_up = _dw(dup, u2, "dw_up").reshape(4, FSL, D)
    do, dgr, dys, dzr, dxs_skip, acc_mix = _mix_bwd(dylat, o_f, o_b, p_main, y_f, y_b, xa, vec_mix, w_out_f)
    gw_out = _dw(ymix, dylat, "dw_out").reshape(4, D // 2, D)
    pair_a, dests_a = pair_stage([gw_gate, gw_up], "a1") + pair_stage([gw_down, gw_out], "a2"), [every] * 4

    res = _hgrn_bwd(p_main, lbraw8, hs_f, do, 0, nb, None, comm=_comm_exchange(pair_a[:2], dests_a[:2]))
    (dq0, dff, dv0, dlb_f), recv_a = res[:4], list(res[4:])
    res = _hgrn_bwd(p_main, lbraw8, hs_b, do, 1, nb, (dq0, dv0), comm=_comm_exchange(pair_a[2:], dests_a[2:]))
    (dq, dfb, dv, dlb_b), recv_a = res[:4], recv_a + list(res[4:])
    gw_in = [_dw_in([dq, dff], u_all, "dw_in_0"), _dw_in([dfb, dv], u_all, "dw_in_1"),
             _dw_in([dgr, dzr], u_all, "dw_in_2")]
    pair_b, dests_b = pair_stage(gw_in, "b"), [(0, 1), (1, 2), (2, 3)]

    res = _ssd_bwd(xa, dts, alog, ss_f, dys, 0, nb, None, comm=_comm_exchange(pair_b, dests_b))
    (dxa0, ddts0, da_f), recv_b = res[:3], list(res[3:])
    dxa, ddts, da_b = _ssd_bwd(xa, dts, alog, ss_b, dys, 1, nb, (dxa0, ddts0))
    dxbc, ddt, acc_conv, acc_dtb = _ssd_prep_bwd(p_main, p_dt, convp, dtb, dxa, dxs_skip, ddts, nb)
    gw_in.append(_dw_in([dxbc], u_all, "dw_in_3"))
    gw_in_dt = _dw(ddt, u_all, "dw_in_dt")
    gw_in_tail = jnp.concatenate([g_[:, 0:WTAIL, :] for g_ in gw_in[1:]] + [gw_in_dt[None]], axis=0)
    pair_c, dests_c = pair_stage([gw_in[3], gw_in_tail], "c"), [(3, 4), every]

    segs = [dq, dff, dfb, dv, dgr, dzr, dxbc]
    bmods_lat = jnp.concatenate([1.0 + sc1, norm_mix] + [zrow] * 6, axis=0)
    bmods_ctx = jnp.concatenate([1.0 + csc1, norm_mix] + [zrow] * 6, axis=0)
    res = _du_prenorm_bwd(segs, ddt, wi_main, wi_tail, xl, bmods_lat, dh1, 0, 512, "du_lat",
                          comm=_comm_exchange(pair_c, dests_c))
    (grad_x, acc_lat), recv_c = res[:2], list(res[2:])
    (acc_ctx,) = _du_prenorm_bwd(segs, ddt, wi_main, wi_tail, xc, bmods_ctx, None, n_lat, TB, "du_ctx")

    mine = _chip_sum(pair_b + pair_c + pair_a, recv_b + recv_c + recv_a, chip_arr, dests_b + dests_c + dests_a,
                     [0, 0, 0, 0, 1, 3, 4, 5, 2])
    theirs = _pair_swap(mine)
    whole = [jnp.concatenate([jnp.where(ic == 0, m_, t_), jnp.where(ic == 0, t_, m_)], axis=0)
             for m_, t_ in zip(mine, theirs)]
    g_w_in = lax.dynamic_slice(jnp.concatenate(whole[0:2], axis=0), (8 * chip, 0), (NSH, D))
    g_w_out = whole[2]
    g_w_gate = whole[3][:DFF // 4]
    g_w_up = whole[4][:DFF // 4]
    g_w_down = whole[5][:DFF // 4]

    dmod_lat = jnp.concatenate([acc_lat[0:2], acc_ffn[3:4], acc_ffn[0:2], acc_loss[0:1]], axis=0)
    misc = jnp.concatenate([(da_f + da_b)[0, :32], jnp.zeros((96,), F32), acc_dtb[0, :32], jnp.zeros((96,), F32),
                            jnp.sum(acc_loss[2]).reshape(1), jnp.zeros((D - 257,), F32)]).reshape(1, D)
    sv = jnp.concatenate([
        dmod_lat, acc_ctx[0:2], (acc_lat[2:3] + acc_ctx[2:3]), acc_ffn[2:3], acc_loss[1:2], acc_mix[2:3],
        acc_mix[0:1], acc_mix[1:2], dlb_f[0:1], dlb_b[0:1], acc_conv[0:6].reshape(12, D), misc,
        jnp.zeros((3, D), F32)], axis=0)
    sv_all = _allgather8(sv, "small_grads_gather").reshape(8, 32, D)
    ssum = _sum8(sv_all)
    dmod_rows = sv_all[:, 0:6].reshape(8, 6 * D)
    dmod_ctx_row = jnp.concatenate([ssum[6:8].reshape(1, 2 * D), jnp.zeros((1, 4 * D), F32)], axis=1)
    dmod_full = jnp.concatenate([dmod_rows, dmod_ctx_row, jnp.zeros((7, 6 * D), F32)], axis=0)
    grad_b_ada = jnp.sum(dmod_full, axis=0, keepdims=True)
    dmod_shard = lax.dynamic_slice(dmod_full, (0, chip * ncol_ada), (16, ncol_ada))
    g_w_ada, da_part = _ada_bwd(araw, dmod_shard, w_ada[0])
    da_all = _allgather8(da_part, "ada_ctx_gather").reshape(8, 16, D)[0::2, 8]
    cc = c_ctx.reshape(1, D)
    grad_c_ctx = (jnp.sum(da_all, axis=0, keepdims=True) * _dsilu(cc)).reshape(D)

    grad_norm_mix, grad_norm_ffn, grad_final_norm = ssum[8:9], ssum[9:10], ssum[10].reshape(D)
    grad_ssd_norm = ssum[11:12]
    grad_hgrn_norm = jnp.sum(ssum[12].reshape(NH, HF), axis=0, keepdims=True)
    grad_ssd_d = jnp.sum(ssum[13].reshape(SHEADS, SP), axis=1).reshape(1, SHEADS)
    lb_full = _sig(lbraw_full[0:2] - lbraw_full[2:4])
    dr0 = ssum[14:16] * lb_full * (1.0 - lb_full)
    grad_lb_full = jnp.stack([dr0, -dr0], axis=0)
    grad_lb = lax.dynamic_slice(grad_lb_full, (0, 0, chip * 256), (2, 2, 256))
    grad_conv_w = lax.dynamic_slice(ssum[16:26].reshape(KCONV, 2048), (0, chip * 512), (KCONV, 512)).reshape(1, KCONV, 512)
    grad_conv_b = ssum[26:28].reshape(1, 2048)
    a_val = -jnp.exp(ssd_a_log)
    grad_a_log = ssum[28, 0:32].reshape(1, 2, SHEADS) * a_val
    grad_dt_bias = ssum[28, 128:160].reshape(1, 2, SHEADS)
    loss = ssum[28, 256]

    small_w = [c_ctx, b_ada, norm_mix, conv_w, conv_b, ssd_a_log, ssd_dt_bias, ssd_d, ssd_norm, hgrn_lb_raw,
               hgrn_norm, norm_ffn, final_norm]
    small_m = [m_c_ctx, m_b_ada, m_norm_mix, m_conv_w, m_conv_b, m_ssd_a_log, m_ssd_dt_bias, m_ssd_d, m_ssd_norm,
               m_hgrn_lb_raw, m_hgrn_norm, m_norm_ffn, m_final_norm]
    small_v = [v_c_ctx, v_b_ada, v_norm_mix, v_conv_w, v_conv_b, v_ssd_a_log, v_ssd_dt_bias, v_ssd_d, v_ssd_norm,
               v_hgrn_lb_raw, v_hgrn_norm, v_norm_ffn, v_final_norm]
    small_g = [grad_c_ctx, grad_b_ada, grad_norm_mix, grad_conv_w, grad_conv_b, grad_a_log, grad_dt_bias, grad_ssd_d,
               grad_ssd_norm, grad_lb, grad_hgrn_norm, grad_norm_ffn, grad_final_norm]
    nrows = [-(-a.size // D) for a in small_w]
    packs = lambda lst: jnp.concatenate([_rows(a, r) for a, r in zip(lst, nrows)]
                                        + [jnp.zeros((24 - sum(nrows), D), F32)], axis=0)
    sd, sm, svv = _adamw(packs(small_w), packs(small_m), packs(small_v), packs(small_g), "adamw_small")

    def unpack(p):
        out, r0 = [], 0
        for a, r in zip(small_w, nrows):
            out.append(p[r0:r0 + r].reshape(-1)[:a.size].reshape(a.shape))
            r0 += r
        return out

    sd, sm, svv = unpack(sd), unpack(sm), unpack(svv)
    big = {}
    for nm, w_, m_, v_, g_ in (("w_ada", w_ada, m_w_ada, v_w_ada, g_w_ada), ("w_in", w_in, m_w_in, v_w_in, g_w_in),
                               ("w_out", w_out, m_w_out, v_w_out, g_w_out),
                               ("w_gate", w_gate, m_w_gate, v_w_gate, g_w_gate),
                               ("w_up", w_up, m_w_up, v_w_up, g_w_up),
                               ("w_down", w_down, m_w_down, v_w_down, g_w_down)):
        if nm in ("w_in", "w_gate", "w_up"):
            big[nm] = tuple(tr(t) for t in (g_[None],) + tuple(_adamw(tr(w_), tr(m_), tr(v_), g_, "adamw_" + nm)))
        else:
            big[nm] = (g_[None],) + tuple(_adamw(w_, m_, v_, g_, "adamw_" + nm))

    order = ["c_ctx", "w_ada", "b_ada", "norm_mix", "w_in", "conv_w", "conv_b", "ssd_a_log", "ssd_dt_bias", "ssd_d",
             "ssd_norm", "hgrn_lb_raw", "hgrn_norm", "w_out", "norm_ffn", "w_gate", "w_up", "w_down", "final_norm"]
    small_names = ["c_ctx", "b_ada", "norm_mix", "conv_w", "conv_b", "ssd_a_log", "ssd_dt_bias", "ssd_d", "ssd_norm",
                   "hgrn_lb_raw", "hgrn_norm", "norm_ffn", "final_norm"]
    table = dict(big)
    for k, nm in enumerate(small_names):
        table[nm] = (small_g[k].reshape(small_w[k].shape), sd[k], sm[k], svv[k])
    grads = [table[nm][0] for nm in order]
    deltas = [table[nm][1] for nm in order]
    new_m = [table[nm][2] for nm in order]
    new_v = [table[nm][3] for nm in order]
    return (loss, grad_x[None], *grads, *deltas, *new_m, *new_v)
```

```python
import functools
import math

import jax
import jax.numpy as jnp
from jax import lax
from jax.experimental import pallas as pl
from jax.experimental.pallas import tpu as pltpu

F32 = jnp.float32
BF16 = jnp.bfloat16
MXU_DTYPE = jnp.bfloat16
_INTERPRET = False

D = 1024
NH, HF = 8, 128
HC = 64
SC = 128
SN = 128
SHEADS, SP = 16, 64
GRID_W = 64
KCONV = 5
DFF = 2816
FSL = 768
DFFP = 4 * FSL
NIN = 8224
TB = 256
EPS = 1e-6
LR, B1, B2, AEPS, WD, STEP = 0.001, 0.9, 0.999, 1e-08, 0.01, 10
MESH_ID = pl.DeviceIdType.MESH
NSH = NIN // 4
WSL = 2048
WTAIL = 128


def _pcall(body, *, name, out_shape, grid=(), in_specs=None, out_specs=None, scratch=(), sem=None,
           vmem_mb=None, aliases=None):
    params = {}
    if sem is not None:
        params["dimension_semantics"] = sem
    if vmem_mb is not None:
        params["vmem_limit_bytes"] = vmem_mb << 20
    kw = dict(name=name, out_shape=out_shape, scratch_shapes=list(scratch),
              input_output_aliases=aliases or {}, compiler_params=pltpu.CompilerParams(**params),
              interpret=_INTERPRET)
    if grid:
        kw["grid"] = grid
    if in_specs is not None:
        kw["in_specs"] = in_specs
    if out_specs is not None:
        kw["out_specs"] = out_specs
    return pl.pallas_call(body, **kw)


def _mx(a):
    return a.astype(MXU_DTYPE)


def _dg(a, b, ca, cb):
    return lax.dot_general(_mx(a), _mx(b), (((ca,), (cb,)), ((), ())), preferred_element_type=F32)


def _nn(a, b):
    return _dg(a, b, 1, 0)


def _nt(a, b):
    return _dg(a, b, 1, 1)


def _tn(a, b):
    return _dg(a, b, 0, 0)


def _dot01(m, x):
    hi = x.astype(BF16)
    r1 = x - hi.astype(F32)
    mid = r1.astype(BF16)
    lo = (r1 - mid.astype(F32)).astype(BF16)
    f = lambda t: lax.dot_general(m, t, (((1,), (0,)), ((), ())), preferred_element_type=F32)
    return f(hi) + f(mid) + f(lo)


def _tri(n, upper):
    r = lax.broadcasted_iota(jnp.int32, (n, n), 0)
    c = lax.broadcasted_iota(jnp.int32, (n, n), 1)
    return (c >= r) if upper else (c <= r)


def _b01(mask):
    return jnp.where(mask, 1.0, 0.0).astype(BF16)


def _sig(x):
    return jax.nn.sigmoid(x)


def _silu(x):
    return x * _sig(x)


def _dsilu(x):
    s = _sig(x)
    return s * (1.0 + x * (1.0 - s))


def _softplus(x):
    return jnp.maximum(x, 0.0) + jnp.log(1.0 + jnp.exp(-jnp.abs(x)))


def _rowsum(x):
    return jnp.sum(x, axis=1, keepdims=True)


def _colsum(x):
    return jnp.sum(x, axis=0, keepdims=True)


def _full(shape):
    return pl.BlockSpec(shape, lambda *_: (0,) * len(shape))


def _allgather8_ops(x_ref, out_ref, send_sems, recv_sems, local_sem):
    m_per = x_ref.shape[0]
    x, y, c = lax.axis_index("x"), lax.axis_index("y"), lax.axis_index("c")
    me, sibling = (x, y, c), (x, y, 1 - c)
    chips = [(1 - x, y), (x, 1 - y), (1 - x, 1 - y)]

    def rows(px, py, pc):
        return out_ref.at[pl.ds((4 * px + 2 * py + pc) * m_per, m_per), :]

    def copy(k, block, to, src=None):
        return pltpu.make_async_remote_copy(
            src_ref=rows(*block) if src is None else src, dst_ref=rows(*block),
            send_sem=send_sems.at[k], recv_sem=recv_sems.at[k], device_id=to, device_id_type=MESH_ID)

    mine = pltpu.make_async_copy(x_ref, rows(*me), local_sem)
    mine.start()
    first = [copy(0, me, sibling, src=x_ref)]
    first += [copy(1 + j, me, (*chip, c), src=x_ref) for j, chip in enumerate(chips)]
    for cp in first:
        cp.start()
    passed = [copy(4 + j, (*chip, c), sibling) for j, chip in enumerate(chips)]
    for j, chip in enumerate(chips):
        copy(1 + j, (*chip, c), me).wait_recv()
        passed[j].start()
    copy(0, sibling, me).wait_recv()
    for j, chip in enumerate(chips):
        copy(4 + j, (*chip, 1 - c), me).wait_recv()
    for cp in first + passed:
        cp.wait_send()
    mine.wait()


_AG8_SEMS = [pltpu.SemaphoreType.DMA((7,)), pltpu.SemaphoreType.DMA((7,)), pltpu.SemaphoreType.DMA]


def _allgather8(v, name):
    m_per, n = v.shape
    return _pcall(
        functools.partial(_allgather8_ops), name=name, out_shape=jax.ShapeDtypeStruct((8 * m_per, n), v.dtype),
        in_specs=[pl.BlockSpec(memory_space=pltpu.VMEM)], out_specs=pl.BlockSpec(memory_space=pltpu.VMEM),
        scratch=list(_AG8_SEMS),
    )(v)


def _prologue(pack, cc_row, w_ada, b_shard, shards):
    n = len(shards)
    ncol = w_ada.shape[1]

    def body(pack_ref, cc_ref, w_ref, b_ref, *refs):
        ins = refs[:n]
        gath_ref, araw_ref, mod_ref = refs[n:n + 3]
        outs = refs[n + 3:2 * n + 3]
        modsh, s1, r1, l1, s2, r2, l2, gs, gr = refs[2 * n + 3:]
        start, finish = _gather_ops(ins, outs, gs, gr, relay=True)
        start()
        _allgather8_ops(pack_ref, gath_ref, s1, r1, l1)
        a = jnp.concatenate([gath_ref[8 * i:8 * i + 1, :] for i in range(8)] + [cc_ref[...], jnp.zeros((7, D), F32)],
                            axis=0)
        araw_ref[...] = a
        modsh[...] = _nn(_silu(a), w_ref[...]) + b_ref[...]
        _allgather8_ops(modsh, mod_ref, s2, r2, l2)
        finish()

    vm = pl.BlockSpec(memory_space=pltpu.VMEM)
    anyspec = pl.BlockSpec(memory_space=pl.ANY)
    return _pcall(
        body, name="prologue",
        out_shape=(jax.ShapeDtypeStruct((64, D), F32), jax.ShapeDtypeStruct((16, D), F32),
                   jax.ShapeDtypeStruct((128, ncol), F32)) + _gather_out(shards),
        in_specs=[vm, vm, vm, vm] + [anyspec] * n, out_specs=(vm, vm, vm) + (anyspec,) * n,
        scratch=[pltpu.VMEM((16, ncol), F32)] + list(_AG8_SEMS) + list(_AG8_SEMS) + _gather_sems(n), vmem_mb=40,
    )(pack, cc_row, w_ada, b_shard, *shards)


def _gather_ops(ins, outs, send_sems, recv_sems, relay=False):
    n = len(ins)
    x, y, c = lax.axis_index("x"), lax.axis_index("y"), lax.axis_index("c")
    me, sibling = (x, y, c), (x, y, 1 - c)
    chips = [(1 - x, y), (x, 1 - y), (1 - x, 1 - y)]
    direct = 2 if relay else 3

    def part(a, px, py, pc, quarter=None):
        half = ins[a].shape[0] // 2
        if quarter is None:
            return outs[a].at[2 * px + py, pl.ds(pc * half, half), :]
        return outs[a].at[2 * px + py, pl.ds(pc * half + quarter * (half // 2), half // 2), :]

    def copy(a, k, block, to, src=None, quarter=None):
        return pltpu.make_async_remote_copy(
            src_ref=part(a, *block, quarter) if src is None else src, dst_ref=part(a, *block, quarter),
            send_sem=send_sems.at[8 * a + k], recv_sem=recv_sems.at[8 * a + k], device_id=to,
            device_id_type=MESH_ID)

    def first(a, j):
        half = ins[a].shape[0] // 2
        return copy(a, j, me, (*chips[j], c), src=ins[a].at[pl.ds(c * half, half), :])

    relayed = lambda a, q: copy(a, 6 + q, (*chips[q], c), (*chips[1 - q], c), quarter=q)

    def start():
        for a in range(n):
            for j in range(direct):
                first(a, j).start()

    def finish():
        for a in range(n):
            for j in range(direct):
                copy(a, j, (*chips[j], c), me).wait_recv()
                copy(a, 3 + j, (*chips[j], c), sibling).start()
                if relay:
                    relayed(a, j).start()
            if relay:
                for q in range(2):
                    copy(a, 6 + q, (*chips[2], c), me, quarter=q).wait_recv()
                copy(a, 5, (*chips[2], c), sibling).start()
        for a in range(n):
            for j, chip in enumerate(chips):
                copy(a, 3 + j, (*chip, 1 - c), me).wait_recv()
        for a in range(n):
            for j, chip in enumerate(chips):
                if j < direct:
                    first(a, j).wait_send()
                    if relay:
                        relayed(a, j).wait_send()
                copy(a, 3 + j, (*chip, c), sibling).wait_send()

    return start, finish


def _gather_out(shards):
    return tuple(jax.ShapeDtypeStruct((4,) + s_.shape, s_.dtype) for s_ in shards)


def _gather_sems(n):
    return [pltpu.SemaphoreType.DMA((8 * n,)), pltpu.SemaphoreType.DMA((8 * n,))]


def _pair_ops(ins, outs, send_sems, recv_sems):
    x, y, c = lax.axis_index("x"), lax.axis_index("y"), lax.axis_index("c")
    cps = []
    for a in range(len(ins)):
        half = ins[a].shape[1] // 2
        cps.append(pltpu.make_async_remote_copy(
            src_ref=ins[a].at[:, pl.ds((1 - c) * half, half), :], dst_ref=outs[a], send_sem=send_sems.at[a],
            recv_sem=recv_sems.at[a], device_id=(x, y, 1 - c), device_id_type=MESH_ID))

    def start():
        for cp in cps:
            cp.start()

    def finish():
        for cp in cps:
            cp.wait()

    return start, finish


def _comm_pair(gs):
    n = len(gs)
    return (list(gs), tuple(jax.ShapeDtypeStruct((g.shape[0], g.shape[1] // 2, g.shape[2]), g.dtype) for g in gs),
            [pltpu.SemaphoreType.DMA((n,)), pltpu.SemaphoreType.DMA((n,))], _pair_ops)


def _exchange_ops(ins, outs, send_sems, recv_sems, dests):
    x, y, c = lax.axis_index("x"), lax.axis_index("y"), lax.axis_index("c")
    mine = 2 * x + y
    chips = [(1 - x, y), (x, 1 - y), (1 - x, 1 - y)]

    def each(fn):
        for a in range(len(ins)):
            lo, hi = dests[a]
            for j, (px, py) in enumerate(chips):
                q = 2 * px + py
                cp = pltpu.make_async_remote_copy(
                    src_ref=ins[a].at[jnp.clip(q - lo, 0, hi - lo - 1)], dst_ref=outs[a].at[j],
                    send_sem=send_sems.at[3 * a + j], recv_sem=recv_sems.at[3 * a + j], device_id=(px, py, c),
                    device_id_type=MESH_ID)
                fn(cp, (q >= lo) & (q < hi), (mine >= lo) & (mine < hi), (lo, hi) == (0, 4))

    def start():
        def go(cp, send_ok, recv_ok, always):
            if always:
                cp.start()
            else:
                pl.when(send_ok)(cp.start)
        each(go)

    def finish():
        def go(cp, send_ok, recv_ok, always):
            if always:
                cp.wait()
            else:
                pl.when(send_ok)(cp.wait_send)
                pl.when(recv_ok)(cp.wait_recv)
        each(go)

    return start, finish


def _comm_exchange(hs, dests):
    n = len(hs)
    return (list(hs), tuple(jax.ShapeDtypeStruct((3,) + h.shape[1:], h.dtype) for h in hs),
            [pltpu.SemaphoreType.DMA((3 * n,)), pltpu.SemaphoreType.DMA((3 * n,))],
            lambda i, o, s, r: _exchange_ops(i, o, s, r, dests))


def _comm_gather(shards):
    return (list(shards), _gather_out(shards), _gather_sems(len(shards)), _gather_ops)


def _carry(call, comm, steps):
    if comm is None:
        return call
    if isinstance(comm, list):
        for one in comm:
            call = _carry(call, one, steps)
        return call
    arrays, out_shape, sems, make = comm
    n, n_in, n_out = len(arrays), len(call["args"]), len(call["out_shape"])
    body = call["body"]

    def wrapped(*refs):
        base_in, cin = refs[:n_in], refs[n_in:n_in + n]
        rest = refs[n_in + n:]
        base_out, cout, scr = rest[:n_out], rest[n_out:n_out + n], rest[n_out + n:]
        start, finish = make(cin, cout, scr[-2], scr[-1])
        first, last = steps()
        pl.when(first)(start)
        body(*base_in, *base_out, *scr[:-2])
        pl.when(last)(finish)

    anyspec = pl.BlockSpec(memory_space=pl.ANY)
    return dict(call, body=wrapped, args=list(call["args"]) + arrays,
                in_specs=list(call["in_specs"]) + [anyspec] * n,
                out_shape=tuple(call["out_shape"]) + tuple(out_shape),
                out_specs=tuple(call["out_specs"]) + (anyspec,) * n,
                scratch=list(call["scratch"]) + sems)


def _run(call):
    args = call.pop("args")
    body = call.pop("body")
    return _pcall(body, **call)(*args)


def _pair_swap(rs):
    n = len(rs)

    def body(*refs):
        ins, outs = refs[:n], refs[n:2 * n]
        send_sems, recv_sems = refs[2 * n:]
        x, y, c = lax.axis_index("x"), lax.axis_index("y"), lax.axis_index("c")
        cps = [pltpu.make_async_remote_copy(
            src_ref=ins[a], dst_ref=outs[a], send_sem=send_sems.at[a], recv_sem=recv_sems.at[a],
            device_id=(x, y, 1 - c), device_id_type=MESH_ID) for a in range(n)]
        for cp in cps:
            cp.start()
        for cp in cps:
            cp.wait()

    return _pcall(
        body, name="grads_pair_swap", out_shape=tuple(jax.ShapeDtypeStruct(r.shape, r.dtype) for r in rs),
        in_specs=[pl.BlockSpec(memory_space=pl.ANY)] * n, out_specs=(pl.BlockSpec(memory_space=pl.ANY),) * n,
        scratch=[pltpu.SemaphoreType.DMA((n,)), pltpu.SemaphoreType.DMA((n,))],
    )(*rs)


SUM_STEPS = 4


def _pair_sum(gs, recvs, core, name):
    n = len(gs)

    def body(c_ref, *refs):
        for a in range(n):
            refs[2 * n + a][...] = (refs[a][...].astype(F32) + refs[n + a][...].astype(F32)).astype(refs[2 * n + a].dtype)

    blk = lambda g: (g.shape[0], g.shape[1] // (2 * SUM_STEPS), g.shape[2])
    return pl.pallas_call(
        body, name=name,
        out_shape=tuple(jax.ShapeDtypeStruct((g.shape[0], g.shape[1] // 2, g.shape[2]), g.dtype) for g in gs),
        grid_spec=pltpu.PrefetchScalarGridSpec(
            num_scalar_prefetch=1, grid=(SUM_STEPS,),
            in_specs=[pl.BlockSpec(blk(g), lambda i, cr: (0, cr[0] * SUM_STEPS + i, 0)) for g in gs]
            + [pl.BlockSpec(blk(g), lambda i, cr: (0, i, 0)) for g in gs],
            out_specs=tuple(pl.BlockSpec(blk(g), lambda i, cr: (0, i, 0)) for g in gs)),
        compiler_params=pltpu.CompilerParams(vmem_limit_bytes=40 << 20), interpret=_INTERPRET,
    )(core, *gs, *recvs)


def _chip_sum(hs, recvs, chip, dests, slots):
    n = len(hs)
    nout = max(slots) + 1
    first = [slots.index(o) for o in range(nout)]
    every = lambda d_: d_ == (0, 4)

    def own(d_):
        if every(d_):
            return lambda i, kr: (kr[0], i, 0)
        return lambda i, kr: (0, jnp.where(kr[0] == d_[0], i, 0), 0)

    def got(d_):
        if every(d_):
            return lambda i, kr: (0, i, 0)
        return lambda i, kr: (0, jnp.where(kr[0] == d_[0], i, 0), 0)

    def body(k_ref, *refs):
        for a in range(n):
            def emit(a=a):
                acc = refs[a][0].astype(F32)
                for j in range(3):
                    acc = acc + refs[n + a][j].astype(F32)
                refs[2 * n + slots[a]][...] = acc
            if every(dests[a]):
                emit()
            else:
                pl.when(k_ref[0] == dests[a][0])(emit)

    rb = lambda h: h.shape[1] // SUM_STEPS
    return pl.pallas_call(
        body, name="grads_chip_sum",
        out_shape=tuple(jax.ShapeDtypeStruct(hs[a].shape[1:], F32) for a in first),
        grid_spec=pltpu.PrefetchScalarGridSpec(
            num_scalar_prefetch=1, grid=(SUM_STEPS,),
            in_specs=[pl.BlockSpec((1, rb(h), h.shape[2]), own(d_)) for h, d_ in zip(hs, dests)]
            + [pl.BlockSpec((3, rb(h), h.shape[2]), got(d_)) for h, d_ in zip(hs, dests)],
            out_specs=tuple(pl.BlockSpec((rb(hs[a]), hs[a].shape[2]), lambda i, kr: (i, 0)) for a in first)),
        compiler_params=pltpu.CompilerParams(vmem_limit_bytes=40 << 20), interpret=_INTERPRET,
    )(chip, *hs, *recvs)


def _ada_bwd(araw, dmod, w):
    nblk = w.shape[1] // 512

    def body(a_ref, d_ref, w_ref, gw_ref, da_ref):
        j = pl.program_id(0)
        gw_ref[...] = _tn(_silu(a_ref[...]), d_ref[...])
        part = _nt(d_ref[...], w_ref[...])

        @pl.when(j == 0)
        def _():
            da_ref[...] = part

        @pl.when(j > 0)
        def _():
            da_ref[...] += part

    return _pcall(
        body, name="ada_bwd",
        out_shape=(jax.ShapeDtypeStruct(w.shape, F32), jax.ShapeDtypeStruct((16, D), F32)), grid=(nblk,),
        in_specs=[_full((16, D)), pl.BlockSpec((16, 512), lambda j: (0, j)), pl.BlockSpec((D, 512), lambda j: (0, j))],
        out_specs=(pl.BlockSpec((D, 512), lambda j: (0, j)), _full((16, D))), sem=("arbitrary",),
    )(araw, dmod, w)


def _w_specs():
    return [pl.BlockSpec((None, D, D), lambda j, i: (j // 2, j % 2, 0)),
            pl.BlockSpec((None, WTAIL, D), lambda j, i: (jnp.maximum(j // 2 - 1, 0), 0, 0)),
            pl.BlockSpec((None, WTAIL, D), lambda j, i: (3, 0, 0))]


def _inproj(xin, mods, wi_main, wi_tail, t_total, tb, blk_off, prev, name, comm=None):
    n = xin.shape[0]
    nt = n // tb
    ncol = 8

    def body(x_ref, mod_ref, w_ref, wb_ref, wdt_ref, *rest):
        p_ref, pdt_ref, u_ref, uscr = rest[-4:]
        j, i = pl.program_id(0), pl.program_id(1)
        rows = pl.ds(pl.multiple_of(i * tb, tb), tb)

        @pl.when(j == 0)
        def _():
            xv = x_ref[...]
            r = lax.rsqrt(jnp.mean(xv * xv, axis=1, keepdims=True) + EPS)
            u = (xv * r * mod_ref[2:3, :]) * mod_ref[0:1, :] + mod_ref[1:2, :]
            ub = u.astype(MXU_DTYPE)
            uscr[rows, :] = ub
            u_ref[...] = ub
            pdt_ref[...] = _nt(ub, wdt_ref[...])

        ub = uscr[rows, :]
        pv = _nt(ub, w_ref[...])

        @pl.when((j % 2 == 1) | (j == 0))
        def _():
            p_ref[...] = pv.astype(p_ref.dtype)

        @pl.when((j % 2 == 0) & (j > 0))
        def _():
            head = pv[:, 0:WTAIL] + _nt(ub, wb_ref[...])
            p_ref[...] = jnp.concatenate([head, pv[:, WTAIL:]], axis=1).astype(p_ref.dtype)

    once = lambda j, i: (jnp.where(j == 0, i, nt - 1) + blk_off, 0)
    in_specs = [pl.BlockSpec((tb, D), lambda j, i: (jnp.where(j == 0, i, nt - 1), 0)), _full((8, D))] + _w_specs()
    args = [xin, mods, wi_main, wi_tail, wi_tail]
    aliases = None
    if prev is not None:
        in_specs += [pl.BlockSpec(memory_space=pl.ANY)] * 3
        args += list(prev)
        aliases = {5: 0, 6: 1, 7: 2}
    call = dict(
        body=body, args=args, name=name,
        out_shape=(jax.ShapeDtypeStruct((t_total, ncol * D), MXU_DTYPE), jax.ShapeDtypeStruct((t_total, 128), F32),
                   jax.ShapeDtypeStruct((t_total, D), MXU_DTYPE)),
        grid=(ncol, nt), in_specs=in_specs,
        out_specs=(pl.BlockSpec((tb, D), lambda j, i: (i + blk_off, j)), pl.BlockSpec((tb, 128), once),
                   pl.BlockSpec((tb, D), once)),
        scratch=[pltpu.VMEM((n, D), MXU_DTYPE)], sem=("arbitrary", "arbitrary"), vmem_mb=48, aliases=aliases)
    steps = lambda: ((pl.program_id(0) == 0) & (pl.program_id(1) == 0),
                     (pl.program_id(0) == ncol - 1) & (pl.program_id(1) == nt - 1))
    return _run(_carry(call, comm, steps))


def _blk(s, nb, rev):
    return jnp.where(s == 0, nb - 1, (nb - 1 - s) if rev else (s - 1))


def _hgrn_gate(fr, lbraw_ref, d):
    lb = _sig(lbraw_ref[d:d + 1, :] - lbraw_ref[2 + d:3 + d, :])
    sg = _sig(fr)
    return lb, sg, lb + (1.0 - lb) * sg


def _hgrn_fwd(p_main, lbraw, d, nb, comm=None):
    t_total = p_main.shape[0]
    rev = d == 1
    nch = TB // HC
    scale = HF ** -0.5

    def body(q_ref, f_ref, v_ref, lb_ref, o_ref, sp_ref, st):
        s = pl.program_id(0)

        @pl.when(s == 0)
        def _():
            st[...] = jnp.zeros_like(st)

        mb = _tri(HC, rev)
        m01 = _b01(mb)
        order = list(reversed(range(nch)) if rev else range(nch))
        hs_ = [slice(h * HF, (h + 1) * HF) for h in range(NH)]
        pre = {}
        for c in order:
            rows = slice(c * HC, (c + 1) * HC)
            _, _, f = _hgrn_gate(f_ref[rows, :].astype(F32), lb_ref, d)
            k = 1.0 - f
            cum = _dot01(m01, jnp.log(f))
            tot = cum[0:1, :] if rev else cum[HC - 1:HC, :]
            qd = _silu(q_ref[rows, :].astype(F32)) * scale * jnp.exp(cum)
            ki = k * jnp.exp(-cum)
            etot = jnp.exp(tot)
            pre[c] = (_mx(qd), _mx(ki), _mx(ki * etot), _mx(v_ref[rows, :]), etot)
        scs = {c: [_nt(pre[c][0][:, cs], pre[c][1][:, cs]) for cs in hs_] for c in order}
        upd = {c: [_tn(pre[c][3][:, cs], pre[c][2][:, cs]) for cs in hs_] for c in order}
        intra = {c: [_nn(jnp.where(mb, scs[c][h], 0.0), pre[c][3][:, cs]) for h, cs in enumerate(hs_)] for c in order}
        for c in order:
            rows = slice(c * HC, (c + 1) * HC)
            qdb, etot = pre[c][0], pre[c][4]
            for h, cs in enumerate(hs_):
                sth = st[h]
                stb = sth.astype(sp_ref.dtype)
                sp_ref[c, h] = stb
                o_ref[rows, cs] = (intra[c][h] + _nt(qdb[:, cs], stb)).astype(o_ref.dtype)
                st[h] = sth * etot[:, cs] + upd[c][h]

    col = lambda j: (lambda s: (_blk(s, nb, rev), j))
    call = dict(
        body=body, args=[p_main, p_main, p_main, lbraw], name=f"hgrn_fwd_{d}",
        out_shape=(jax.ShapeDtypeStruct((t_total, D), MXU_DTYPE),
                   jax.ShapeDtypeStruct((nch * nb, NH, HF, HF), MXU_DTYPE)),
        grid=(nb,),
        in_specs=[pl.BlockSpec((TB, D), col(0)), pl.BlockSpec((TB, D), col(1 + d)), pl.BlockSpec((TB, D), col(3)),
                  _full((8, D))],
        out_specs=(pl.BlockSpec((TB, D), col(0)),
                   pl.BlockSpec((nch, NH, HF, HF), lambda s: (_blk(s, nb, rev), 0, 0, 0))),
        scratch=[pltpu.VMEM((NH, HF, HF), F32)], sem=("arbitrary",), vmem_mb=40)
    return _run(_carry(call, comm, lambda: (pl.program_id(0) == 0, pl.program_id(0) == nb - 1)))


def _hgrn_bwd(p_main, lbraw, sprev, do, d, nb, prev, comm=None):
    t_total = p_main.shape[0]
    rev = d == 1
    nch = TB // HC
    scale = HF ** -0.5
    last = prev is not None
    odt = MXU_DTYPE if last else F32

    def body(q_ref, f_ref, v_ref, lb_ref, sp_ref, do_ref, *rest):
        if last:
            dqp_ref, dvp_ref = rest[:2]
            rest = rest[2:]
        dq_ref, df_ref, dv_ref, dlb_ref, dst = rest
        sp_id = pl.program_id(0)
        is_ctx = sp_id == nb - 1

        @pl.when(sp_id == 0)
        def _():
            dst[...] = jnp.zeros_like(dst)
            dlb_ref[...] = jnp.zeros_like(dlb_ref)

        mb = _tri(HC, rev)
        mbt = _tri(HC, not rev)
        m01 = _b01(mb)
        mt01 = _b01(mbt)
        order = list(range(nch) if rev else reversed(range(nch)))
        hs_ = [slice(h * HF, (h + 1) * HF) for h in range(NH)]
        pre = {}
        for c in order:
            rows = slice(c * HC, (c + 1) * HC)
            lb, sg, f = _hgrn_gate(f_ref[rows, :].astype(F32), lb_ref, d)
            k = 1.0 - f
            cum = _dot01(m01, jnp.log(f))
            tot = cum[0:1, :] if rev else cum[HC - 1:HC, :]
            e = jnp.exp(cum)
            ei = jnp.exp(-cum)
            etot = jnp.exp(tot)
            ee = ei * etot
            qraw = q_ref[rows, :].astype(F32)
            qd = _silu(qraw) * scale * e
            ki = k * ei
            ke = k * ee
            dov = jnp.where(is_ctx, 0.0, do_ref[rows, :].astype(F32))
            pre[c] = dict(lb=lb, sg=sg, f=f, e=e, ei=ei, ee=ee, etot=etot, qraw=qraw, qd=qd, ki=ki, ke=ke,
                          qdb=_mx(qd), kib=_mx(ki), keb=_mx(ke), vb=_mx(v_ref[rows, :]), dob=_mx(dov))
        units = [(c, h) for c in order for h in range(NH)]
        col = lambda u, key: pre[u[0]][key][:, hs_[u[1]]]
        pt = {u: jnp.where(mbt, _nt(col(u, "kib"), col(u, "qdb")), 0.0) for u in units}
        dp = {u: jnp.where(mb, _nt(col(u, "dob"), col(u, "vb")), 0.0) for u in units}
        dpt = {u: jnp.where(mbt, _nt(col(u, "vb"), col(u, "dob")), 0.0) for u in units}
        dv_i = {u: _nn(pt[u], col(u, "dob")) for u in units}
        dqd_ = {u: _nn(dp[u], col(u, "kib")) + _nn(col(u, "dob"), sp_ref[u[0], u[1]]) for u in units}
        dki_ = {u: _nn(dpt[u], col(u, "qdb")) for u in units}
        dsl = {u: _tn(col(u, "dob"), col(u, "qdb")) for u in units}
        for c in order:
            rows = slice(c * HC, (c + 1) * HC)
            p = pre[c]
            dv_l, dke_l, dtot_l = [], [], []
            for h, cs in enumerate(hs_):
                dso = dst[h]
                dsob = _mx(dso)
                dv_l.append(dv_i[(c, h)] + _nt(p["keb"][:, cs], dsob))
                dke_l.append(_nn(p["vb"][:, cs], dsob))
                dtot_l.append(_colsum(dso * sp_ref[c, h].astype(F32)) * p["etot"][:, cs])
                dst[h] = dso * p["etot"][:, cs] + dsl[(c, h)]
            lb, sg, f, e, ei, ee, qraw, qd, ki, ke = (p[n_] for n_ in ("lb", "sg", "f", "e", "ei", "ee", "qraw", "qd",
                                                                     "ki", "ke"))
            dqd = jnp.concatenate([dqd_[(c, h)] for h in range(NH)], axis=1)
            dki = jnp.concatenate([dki_[(c, h)] for h in range(NH)], axis=1)
            dke = jnp.concatenate(dke_l, axis=1)
            dcum = dqd * qd - dki * ki - dke * ke
            dtot = jnp.concatenate(dtot_l, axis=1) + _colsum(dke * ke)
            dk = dki * ei + dke * ee
            dlf = _dot01(mt01, dcum) + dtot
            df = dlf / f - dk
            dlb_ref[0:1, :] += _colsum(df * (1.0 - sg))
            dfr = df * (1.0 - lb) * sg * (1.0 - sg)
            dq = dqd * e * scale * _dsilu(qraw)
            dv = jnp.concatenate(dv_l, axis=1)
            if last:
                dq = dq + dqp_ref[rows, :]
                dv = dv + dvp_ref[rows, :]
            dq_ref[rows, :] = dq.astype(odt)
            dv_ref[rows, :] = dv.astype(odt)
            df_ref[rows, :] = dfr.astype(MXU_DTYPE)

    blk = lambda s: _blk(nb - 1 - s, nb, rev)
    col = lambda j: (lambda s: (blk(s), j))
    in_specs = [pl.BlockSpec((TB, D), col(0)), pl.BlockSpec((TB, D), col(1 + d)), pl.BlockSpec((TB, D), col(3)),
                _full((8, D)), pl.BlockSpec((nch, NH, HF, HF), lambda s: (blk(s), 0, 0, 0)),
                pl.BlockSpec((TB, D), lambda s: (jnp.minimum(blk(s), nb - 2), 0))]
    args = [p_main, p_main, p_main, lbraw, sprev, do]
    if last:
        in_specs += [pl.BlockSpec((TB, D), col(0))] * 2
        args += list(prev)
    call = dict(
        body=body, args=args, name=f"hgrn_bwd_{d}",
        out_shape=(jax.ShapeDtypeStruct((t_total, D), odt), jax.ShapeDtypeStruct((t_total, D), MXU_DTYPE),
                   jax.ShapeDtypeStruct((t_total, D), odt), jax.ShapeDtypeStruct((8, D), F32)),
        grid=(nb,), in_specs=in_specs,
        out_specs=(pl.BlockSpec((TB, D), col(0)), pl.BlockSpec((TB, D), col(0)), pl.BlockSpec((TB, D), col(0)),
                   _full((8, D))),
        scratch=[pltpu.VMEM((NH, HF, HF), F32)], sem=("arbitrary",), vmem_mb=48)
    return _run(_carry(call, comm, lambda: (pl.program_id(0) == 0, pl.program_id(0) == nb - 1)))


def _conv_masks(tb, is_ctx):
    seg = jnp.where(is_ctx, tb, GRID_W)
    pos = lax.broadcasted_iota(jnp.int32, (tb, 1), 0) & (seg - 1)
    return pos, seg


def _shift_rows(x, dshift, pos, seg):
    if dshift == 0:
        return x
    n = x.shape[0]
    rolled = pltpu.roll(x, (-dshift) % n, 0)
    ok = (pos + dshift >= 0) & (pos + dshift < seg)
    return jnp.where(ok, rolled, 0.0)


def _ssd_prep(p_main, p_dt, convp, dtb, nb):
    t_total = p_main.shape[0]

    def body(x_ref, dt_ref, cw_ref, dtb_ref, xa_ref, dts_ref):
        is_ctx = pl.program_id(0) == nb - 1
        pos, seg = _conv_masks(TB, is_ctx)
        xv = x_ref[...].astype(F32)
        acc = cw_ref[5:6, :] + cw_ref[2:3, :] * xv
        for kk in (0, 1, 3, 4):
            acc = acc + cw_ref[kk:kk + 1, :] * _shift_rows(xv, kk - 2, pos, seg)
        xa_ref[...] = _silu(acc).astype(xa_ref.dtype)
        dts_ref[...] = _softplus(dt_ref[...] + dtb_ref[0:1, :])

    return _pcall(
        body, name="ssd_prep",
        out_shape=(jax.ShapeDtypeStruct((t_total, 2048), MXU_DTYPE), jax.ShapeDtypeStruct((t_total, 128), F32)),
        grid=(nb,),
        in_specs=[pl.BlockSpec((TB, 2048), lambda i: (i, 3)), pl.BlockSpec((TB, 128), lambda i: (i, 0)),
                  _full((8, 2048)), _full((8, 128))],
        out_specs=(pl.BlockSpec((TB, 2048), lambda i: (i, 0)), pl.BlockSpec((TB, 128), lambda i: (i, 0))),
        sem=("parallel",), vmem_mb=32,
    )(p_main, p_dt, convp, dtb)


def _ssd_prep_bwd(p_main, p_dt, convp, dtb, dxa, dxs_skip, ddts, nb):
    t_total = p_main.shape[0]

    def body(x_ref, dt_ref, cw_ref, dtb_ref, dxa_ref, dsk_ref, ddts_ref, dx_ref, ddt_ref, dcw_ref, ddtb_ref):
        i = pl.program_id(0)
        is_ctx = i == nb - 1

        @pl.when(i == 0)
        def _():
            dcw_ref[...] = jnp.zeros_like(dcw_ref)
            ddtb_ref[...] = jnp.zeros_like(ddtb_ref)

        pos, seg = _conv_masks(TB, is_ctx)
        xv = x_ref[...].astype(F32)
        sh = {kk: _shift_rows(xv, kk - 2, pos, seg) for kk in range(KCONV)}
        acc = cw_ref[5:6, :]
        for kk in range(KCONV):
            acc = acc + cw_ref[kk:kk + 1, :] * sh[kk]
        dact = dxa_ref[...]
        dact = jnp.concatenate([dact[:, :D] + jnp.where(is_ctx, 0.0, dsk_ref[...].astype(F32)), dact[:, D:]], axis=1)
        dpre = dact * _dsilu(acc)
        dxv = cw_ref[2:3, :] * dpre
        for kk in (0, 1, 3, 4):
            dxv = dxv + cw_ref[kk:kk + 1, :] * _shift_rows(dpre, 2 - kk, pos, seg)
        dx_ref[...] = dxv.astype(dx_ref.dtype)
        for kk in range(KCONV):
            dcw_ref[kk:kk + 1, :] += _colsum(dpre * sh[kk])
        dcw_ref[5:6, :] += _colsum(dpre)
        draw = ddts_ref[...] * _sig(dt_ref[...] + dtb_ref[0:1, :])
        ddt_ref[...] = draw.astype(ddt_ref.dtype)
        ddtb_ref[0:1, :] += _colsum(draw)

    return _pcall(
        body, name="ssd_prep_bwd",
        out_shape=(jax.ShapeDtypeStruct((t_total, 2048), MXU_DTYPE), jax.ShapeDtypeStruct((t_total, 128), MXU_DTYPE),
                   jax.ShapeDtypeStruct((8, 2048), F32), jax.ShapeDtypeStruct((8, 128), F32)),
        grid=(nb,),
        in_specs=[pl.BlockSpec((TB, 2048), lambda i: (i, 3)), pl.BlockSpec((TB, 128), lambda i: (i, 0)),
                  _full((8, 2048)), _full((8, 128)), pl.BlockSpec((TB, 2048), lambda i: (i, 0)),
                  pl.BlockSpec((TB, D), lambda i: (jnp.minimum(i, nb - 2), 0)),
                  pl.BlockSpec((TB, 128), lambda i: (i, 0))],
        out_specs=(pl.BlockSpec((TB, 2048), lambda i: (i, 0)), pl.BlockSpec((TB, 128), lambda i: (i, 0)),
                   _full((8, 2048)), _full((8, 128))),
        sem=("arbitrary",), vmem_mb=40,
    )(p_main, p_dt, convp, dtb, dxa, dxs_skip, ddts)


def _dot2(x, m01):
    hi = x.astype(BF16)
    lo = (x - hi.astype(F32)).astype(BF16)
    f = lambda t: lax.dot_general(t, m01, (((1,), (0,)), ((), ())), preferred_element_type=F32)
    return f(hi) + f(lo)


def _head_lanes(c0, c1):
    p = lax.broadcasted_iota(jnp.int32, (128, 128), 0)
    l = lax.broadcasted_iota(jnp.int32, (128, 128), 1)
    return _b01(((l == c0) & (p < SP)) | ((l == c1) & (p >= SP)))


def _one_lane(col):
    return _b01(lax.broadcasted_iota(jnp.int32, (128, 128), 1) == col)


def _lane_pick(x, lane, col):
    return _rowsum(jnp.where(lane == col, x, 0.0))


def _ssd_chunk_common(dts, alog_ref, m01, rev):
    lane = lax.broadcasted_iota(jnp.int32, (1, 128), 1)
    arow = -jnp.exp(alog_ref[0:1, :])
    cum = _dot01(m01, dts * arow)
    tot = cum[0:1, :] if rev else cum[SC - 1:SC, :]
    return lane, arow, cum, cum.T, tot


def _ssd_fwd(xa, dts, alog, d, nb):
    t_total = xa.shape[0]
    rev = d == 1
    nch = TB // SC
    npair = SHEADS // 2

    def body(xa_ref, dts_ref, alog_ref, y_ref, sp_ref, st):
        s = pl.program_id(0)

        @pl.when(s == 0)
        def _():
            st[...] = jnp.zeros_like(st)

        mb = _tri(SC, rev)
        m01 = _b01(mb)
        lo = lax.broadcasted_iota(jnp.int32, (1, 128), 1) < SP
        rlo = lax.broadcasted_iota(jnp.int32, (128, 1), 0) < SP
        order = list(reversed(range(nch)) if rev else range(nch))
        pre = {}
        for c in order:
            rows = slice(c * SC, (c + 1) * SC)
            dts_c = dts_ref[rows, :]
            lane, arow, cum, cumt, tot = _ssd_chunk_common(dts_c, alog_ref, m01, rev)
            bgs = [_mx(xa_ref[rows, D + g * SN:D + (g + 1) * SN]) for g in range(4)]
            cgs = [_mx(xa_ref[rows, D + 512 + g * SN:D + 512 + (g + 1) * SN]) for g in range(4)]
            pairs = []
            for pr in range(npair):
                xs = xa_ref[rows, pr * 128:(pr + 1) * 128].astype(F32)
                cols = [16 * d + 2 * pr, 16 * d + 2 * pr + 1]
                cum_c = [_lane_pick(cum, lane, q) for q in cols]
                dt_c = [_lane_pick(dts_c, lane, q) for q in cols]
                tot_c = [_lane_pick(tot, lane, q) for q in cols]
                dtx = xs * jnp.where(lo, dt_c[0], dt_c[1])
                e1_pair = jnp.where(lo, jnp.exp(cum_c[0]), jnp.exp(cum_c[1]))
                e2_pair = jnp.where(lo, jnp.exp(tot_c[0] - cum_c[0]), jnp.exp(tot_c[1] - cum_c[1]))
                etot_col = jnp.where(rlo, jnp.exp(tot_c[0]), jnp.exp(tot_c[1]))
                decs = [jnp.where(mb, jnp.exp(cum_c[q] - cumt[cols[q]:cols[q] + 1, :]), 0.0) for q in range(2)]
                dtxq = [_mx(jnp.where(lo if q == 0 else ~lo, dtx, 0.0)) for q in range(2)]
                pairs.append(dict(e1=e1_pair, etot=etot_col, decs=decs, dtxq=dtxq, xe=_mx(dtx * e2_pair)))
            pre[c] = (bgs, cgs, pairs)
        gm = {(c, g): _nt(pre[c][1][g], pre[c][0][g]) for c in order for g in range(4)}
        upd = {(c, pr): _tn(pre[c][2][pr]["xe"], pre[c][0][pr // 2]) for c in order for pr in range(npair)}
        intra = {(c, pr): sum(_nn(gm[(c, pr // 2)] * pre[c][2][pr]["decs"][q], pre[c][2][pr]["dtxq"][q]) for q in range(2))
                 for c in order for pr in range(npair)}
        for c in order:
            rows = slice(c * SC, (c + 1) * SC)
            bgs, cgs, pairs = pre[c]
            for pr in range(npair):
                stp = st[pr]
                stb = stp.astype(sp_ref.dtype)
                sp_ref[c, pr] = stb
                y_ref[rows, pr * 128:(pr + 1) * 128] = (
                    intra[(c, pr)] + pairs[pr]["e1"] * _nt(cgs[pr // 2], stb)).astype(y_ref.dtype)
                st[pr] = stp * pairs[pr]["etot"] + upd[(c, pr)]

    blk = lambda s: _blk(s, nb, rev)
    return _pcall(
        body, name=f"ssd_fwd_{d}",
        out_shape=(jax.ShapeDtypeStruct((t_total, D), MXU_DTYPE),
                   jax.ShapeDtypeStruct((nch * nb, npair, 128, SN), MXU_DTYPE)),
        grid=(nb,),
        in_specs=[pl.BlockSpec((TB, 2048), lambda s: (blk(s), 0)), pl.BlockSpec((TB, 128), lambda s: (blk(s), 0)),
                  _full((8, 128))],
        out_specs=(pl.BlockSpec((TB, D), lambda s: (blk(s), 0)),
                   pl.BlockSpec((nch, npair, 128, SN), lambda s: (blk(s), 0, 0, 0))),
        scratch=[pltpu.VMEM((npair, 128, SN), F32)], sem=("arbitrary",), vmem_mb=40,
    )(xa, dts, alog)


def _ssd_bwd(xa, dts, alog, sprev, dy, d, nb, prev, comm=None):
    t_total = xa.shape[0]
    rev = d == 1
    nch = TB // SC
    npair = SHEADS // 2
    last = prev is not None

    def body(xa_ref, dts_ref, alog_ref, sp_ref, dy_ref, *rest):
        if last:
            dxp_ref, ddp_ref = rest[:2]
            rest = rest[2:]
        dxa_ref, ddts_ref, da_ref, dst, zc_scr = rest
        sp_id = pl.program_id(0)
        is_ctx = sp_id == nb - 1

        @pl.when(sp_id == 0)
        def _():
            dst[...] = jnp.zeros_like(dst)
            da_ref[...] = jnp.zeros_like(da_ref)
            zc_scr[...] = jnp.zeros_like(zc_scr)

        mb = _tri(SC, rev)
        m01 = _b01(mb)
        mt01 = _b01(_tri(SC, not rev))
        lo = lax.broadcasted_iota(jnp.int32, (1, 128), 1) < SP
        rlo = lax.broadcasted_iota(jnp.int32, (128, 1), 0) < SP
        order = list(range(nch) if rev else reversed(range(nch)))
        pre = {}
        for c in order:
            rows = slice(c * SC, (c + 1) * SC)
            dts_c = dts_ref[rows, :]
            lane, arow, cum, cumt, tot = _ssd_chunk_common(dts_c, alog_ref, m01, rev)
            pairs = []
            for pr in range(npair):
                xs = xa_ref[rows, pr * 128:(pr + 1) * 128].astype(F32)
                dyp = jnp.where(is_ctx, 0.0, dy_ref[rows, pr * 128:(pr + 1) * 128].astype(F32))
                cols = [16 * d + 2 * pr, 16 * d + 2 * pr + 1]
                cum_c = [_lane_pick(cum, lane, q) for q in cols]
                dt_c = [_lane_pick(dts_c, lane, q) for q in cols]
                tot_c = [_lane_pick(tot, lane, q) for q in cols]
                e1_c = [jnp.exp(cum_c[q]) for q in range(2)]
                e2_c = [jnp.exp(tot_c[q] - cum_c[q]) for q in range(2)]
                etot_c = [jnp.exp(tot_c[q]) for q in range(2)]
                dt_pair = jnp.where(lo, dt_c[0], dt_c[1])
                e1_pair = jnp.where(lo, e1_c[0], e1_c[1])
                e2_pair = jnp.where(lo, e2_c[0], e2_c[1])
                dtx = xs * dt_pair
                decs = [jnp.where(mb, jnp.exp(cum_c[q] - cumt[cols[q]:cols[q] + 1, :]), 0.0) for q in range(2)]
                dyq = [_mx(jnp.where(lo if q == 0 else ~lo, dyp, 0.0)) for q in range(2)]
                pairs.append(dict(xs=xs, dyp=dyp, cols=cols, e1_c=e1_c, e2_c=e2_c, etot_c=etot_c, dt_pair=dt_pair,
                                  e2_pair=e2_pair, etot_col=jnp.where(rlo, etot_c[0], etot_c[1]), dtx=dtx,
                                  dtxb=_mx(dtx), xeb=_mx(dtx * e2_pair), dy0b=_mx(dyp * e1_pair), decs=decs, dyq=dyq))
            pre[c] = dict(lane=lane, arow=arow, dts=dts_c, pairs=pairs, cum=cum, tot=tot,
                          bgb=[_mx(xa_ref[rows, D + g * SN:D + (g + 1) * SN]) for g in range(4)],
                          cgb=[_mx(xa_ref[rows, D + 512 + g * SN:D + 512 + (g + 1) * SN]) for g in range(4)])
        units = [(c, pr) for c in order for pr in range(npair)]
        P = lambda u: pre[u[0]]["pairs"][u[1]]
        cgu = lambda u: pre[u[0]]["cgb"][u[1] // 2]
        gm = {(c, g): _nt(pre[c]["cgb"][g], pre[c]["bgb"][g]) for c in order for g in range(4)}
        y0 = {u: _nt(cgu(u), sp_ref[u[0], u[1]]) for u in units}
        dcg_i = {u: _nn(P(u)["dy0b"], sp_ref[u[0], u[1]]) for u in units}
        dsl = {u: _tn(P(u)["dy0b"], cgu(u)) for u in units}
        w_ = {(u, q): gm[(u[0], u[1] // 2)] * P(u)["decs"][q] for u in units for q in range(2)}
        dw_ = {(u, q): jnp.where(mb, _nt(P(u)["dyq"][q], P(u)["dtxb"]), 0.0) for u in units for q in range(2)}
        ddtx_i = {(u, q): _tn(w_[(u, q)], P(u)["dyq"][q]) for u in units for q in range(2)}
        for c in order:
            rows = slice(c * SC, (c + 1) * SC)
            pc = pre[c]
            lane, arow, dts_c = pc["lane"], pc["arow"], pc["dts"]
            d1 = jnp.zeros((SC, 128), F32)
            d2 = jnp.zeros((SC, 128), F32)
            dz = jnp.zeros((SC, 128), F32)
            ddt = jnp.zeros((SC, 128), F32)
            dtot = jnp.zeros((1, 128), F32)
            dgm = [jnp.zeros((SC, SC), F32) for _ in range(4)]
            dbg = [jnp.zeros((SC, SN), F32) for _ in range(4)]
            dcg = [jnp.zeros((SC, SN), F32) for _ in range(4)]
            for pr in range(npair):
                u, g, p = (c, pr), pr // 2, pc["pairs"][pr]
                hs = _head_lanes(*p["cols"])
                dso = dst[pr]
                dsob = _mx(dso)
                dxe = _nt(pc["bgb"][g], dsob)
                dbg[g] = dbg[g] + _nn(p["xeb"], dsob)
                ddtx = dxe * p["e2_pair"]
                d2 = d2 + _dot2(dxe * p["dtx"], hs)
                dcg[g] = dcg[g] + dcg_i[u]
                d1 = d1 + _dot2(p["dyp"] * y0[u], hs)
                sprod = dso * sp_ref[c, pr].astype(F32)
                dst[pr] = dso * p["etot_col"] + dsl[u]
                for q in range(2):
                    hm = lo if q == 0 else ~lo
                    col = p["cols"][q]
                    dw = dw_[(u, q)]
                    ddtx = ddtx + jnp.where(hm, ddtx_i[(u, q)], 0.0)
                    dgm[g] = dgm[g] + dw * p["decs"][q]
                    z = dw * w_[(u, q)]
                    dz = dz + _dot2(z, _one_lane(col))
                    zc_scr[col:col + 1, :] = _colsum(z)
                    tsum = _rowsum(_colsum(sprod[q * SP:(q + 1) * SP, :]))
                    dtot = jnp.where(lane == col, tsum * p["etot_c"][q], dtot)
                dxs = ddtx * p["dt_pair"]
                ddt = ddt + _dot2(ddtx * p["xs"], hs)
                if last:
                    dxs = dxs + dxp_ref[rows, pr * 128:(pr + 1) * 128]
                dxa_ref[rows, pr * 128:(pr + 1) * 128] = dxs
            e2_all = jnp.exp(pc["tot"] - pc["cum"])
            dcum = dz - zc_scr[...].T + d1 * jnp.exp(pc["cum"]) - d2 * e2_all
            dtot = dtot + _colsum(d2 * e2_all)
            for g in range(4):
                db = dbg[g] + _tn(dgm[g], pc["cgb"][g])
                dc = dcg[g] + _nn(dgm[g], pc["bgb"][g])
                if last:
                    db = db + dxp_ref[rows, D + g * SN:D + (g + 1) * SN]
                    dc = dc + dxp_ref[rows, D + 512 + g * SN:D + 512 + (g + 1) * SN]
                dxa_ref[rows, D + g * SN:D + (g + 1) * SN] = db
                dxa_ref[rows, D + 512 + g * SN:D + 512 + (g + 1) * SN] = dc
            dla = _dot01(mt01, dcum) + dtot
            ddt = ddt + dla * arow
            da_ref[0:1, :] += _colsum(dla * dts_c)
            if last:
                ddt = ddt + ddp_ref[rows, :]
            ddts_ref[rows, :] = ddt

    blk = lambda s: _blk(nb - 1 - s, nb, rev)
    in_specs = [pl.BlockSpec((TB, 2048), lambda s: (blk(s), 0)), pl.BlockSpec((TB, 128), lambda s: (blk(s), 0)),
                _full((8, 128)), pl.BlockSpec((nch, npair, 128, SN), lambda s: (blk(s), 0, 0, 0)),
                pl.BlockSpec((TB, D), lambda s: (jnp.minimum(blk(s), nb - 2), 0))]
    args = [xa, dts, alog, sprev, dy]
    if last:
        in_specs += [pl.BlockSpec((TB, 2048), lambda s: (blk(s), 0)), pl.BlockSpec((TB, 128), lambda s: (blk(s), 0))]
        args += list(prev)
    call = dict(
        body=body, args=args, name=f"ssd_bwd_{d}",
        out_shape=(jax.ShapeDtypeStruct((t_total, 2048), F32), jax.ShapeDtypeStruct((t_total, 128), F32),
                   jax.ShapeDtypeStruct((8, 128), F32)),
        grid=(nb,), in_specs=in_specs,
        out_specs=(pl.BlockSpec((TB, 2048), lambda s: (blk(s), 0)), pl.BlockSpec((TB, 128), lambda s: (blk(s), 0)),
                   _full((8, 128))),
        scratch=[pltpu.VMEM((npair, 128, SN), F32), pltpu.VMEM((128, 128), F32)], sem=("arbitrary",), vmem_mb=48)
    return _run(_carry(call, comm, lambda: (pl.program_id(0) == 0, pl.program_id(0) == nb - 1)))


def _readout(o, g, yy, z, vec_ref):
    hg, ss, keep = [], [], []
    for h in range(NH):
        cs = slice(h * HF, (h + 1) * HF)
        oh = o[:, cs]
        r = lax.rsqrt(jnp.mean(oh * oh, axis=1, keepdims=True) + EPS)
        hg.append(oh * r * vec_ref[0:1, cs] * _silu(g[:, cs]))
        keep.append(r)
    u = yy * _silu(z)
    for gi in range(4):
        cs = slice(gi * 256, (gi + 1) * 256)
        ug = u[:, cs]
        r = lax.rsqrt(jnp.mean(ug * ug, axis=1, keepdims=True) + EPS)
        ss.append(ug * r * vec_ref[2:3, cs])
        keep.append(r)
    return jnp.concatenate(hg, axis=1), jnp.concatenate(ss, axis=1), keep, u


def _mix_out(o_f, o_b, p_main, y_f, y_b, xa, x, vecs, w_out):
    n = x.shape[0]

    def body(of_ref, ob_ref, g_ref, z_ref, yf_ref, yb_ref, xs_ref, x_ref, vec_ref, w_ref,
             ymix_ref, ylat_ref, h1_ref, u2_ref):
        o = of_ref[...].astype(F32) + ob_ref[...].astype(F32)
        yy = yf_ref[...].astype(F32) + yb_ref[...].astype(F32) + vec_ref[1:2, :] * xs_ref[...].astype(F32)
        hg, ss, _, _ = _readout(o, g_ref[...].astype(F32), yy, z_ref[...].astype(F32), vec_ref)
        ymix = jnp.concatenate([hg, ss], axis=1).astype(MXU_DTYPE)
        ymix_ref[...] = ymix
        ylat = _nn(ymix, w_ref[...])
        ylat_ref[...] = ylat
        h1 = x_ref[...] + vec_ref[3:4, :] * ylat
        h1_ref[...] = h1
        r = lax.rsqrt(jnp.mean(h1 * h1, axis=1, keepdims=True) + EPS)
        u2_ref[...] = ((h1 * r * vec_ref[6:7, :]) * vec_ref[4:5, :] + vec_ref[5:6, :]).astype(MXU_DTYPE)

    row = lambda j: (lambda i: (i, j))
    return _pcall(
        body, name="mix_out",
        out_shape=(jax.ShapeDtypeStruct((n, 2 * D), MXU_DTYPE), jax.ShapeDtypeStruct((n, D), F32),
                   jax.ShapeDtypeStruct((n, D), F32), jax.ShapeDtypeStruct((n, D), MXU_DTYPE)),
        grid=(n // TB,),
        in_specs=[pl.BlockSpec((TB, D), row(0)), pl.BlockSpec((TB, D), row(0)), pl.BlockSpec((TB, D), row(4)),
                  pl.BlockSpec((TB, D), row(5)), pl.BlockSpec((TB, D), row(0)), pl.BlockSpec((TB, D), row(0)),
                  pl.BlockSpec((TB, D), row(0)), pl.BlockSpec((TB, D), row(0)), _full((8, D)), _full((2 * D, D))],
        out_specs=(pl.BlockSpec((TB, 2 * D), row(0)), pl.BlockSpec((TB, D), row(0)), pl.BlockSpec((TB, D), row(0)),
                   pl.BlockSpec((TB, D), row(0))),
        sem=("parallel",), vmem_mb=48,
    )(o_f, o_b, p_main, p_main, y_f, y_b, xa, x, vecs, w_out)


def _mix_bwd(dylat, o_f, o_b, p_main, y_f, y_b, xa, vecs, w_out, comm=None):
    n = dylat.shape[0]
    t_total = p_main.shape[0]
    nlat = n // TB

    def body(*refs):
        dg_ref, dz_ref, acc_ref = refs[11], refs[13], refs[15]
        i = pl.program_id(0)

        @pl.when(i == 0)
        def _():
            acc_ref[...] = jnp.zeros_like(acc_ref)

        @pl.when(i < nlat)
        def _():
            compute(*refs)

        @pl.when(i == nlat)
        def _():
            dg_ref[...] = jnp.zeros_like(dg_ref)
            dz_ref[...] = jnp.zeros_like(dz_ref)

    def compute(dyl_ref, of_ref, ob_ref, g_ref, z_ref, yf_ref, yb_ref, xs_ref, vec_ref, w_ref,
                do_ref, dg_ref, dys_ref, dz_ref, dxs_ref, acc_ref):
        dymix = _nt(dyl_ref[...], w_ref[...])
        o = of_ref[...].astype(F32) + ob_ref[...].astype(F32)
        g = g_ref[...].astype(F32)
        z = z_ref[...].astype(F32)
        xs = xs_ref[...].astype(F32)
        yy = yf_ref[...].astype(F32) + yb_ref[...].astype(F32) + vec_ref[1:2, :] * xs
        _, _, keep, u = _readout(o, g, yy, z, vec_ref)
        do_l, dg_l = [], []
        for h in range(NH):
            cs = slice(h * HF, (h + 1) * HF)
            oh, gh, r, wv = o[:, cs], g[:, cs], keep[h], vec_ref[0:1, cs]
            dhg = dymix[:, cs]
            xh = oh * r
            dn = dhg * _silu(gh)
            dg_l.append(dhg * xh * wv * _dsilu(gh))
            acc_ref[0:1, cs] += _colsum(dn * xh)
            dxh = dn * wv
            do_l.append(r * (dxh - xh * jnp.mean(dxh * xh, axis=1, keepdims=True)))
        du_l = []
        for gi in range(4):
            cs = slice(gi * 256, (gi + 1) * 256)
            ug, r, wv = u[:, cs], keep[NH + gi], vec_ref[2:3, cs]
            dss = dymix[:, D + gi * 256:D + (gi + 1) * 256]
            xh = ug * r
            acc_ref[2:3, cs] += _colsum(dss * xh)
            dxh = dss * wv
            du_l.append(r * (dxh - xh * jnp.mean(dxh * xh, axis=1, keepdims=True)))
        du = jnp.concatenate(du_l, axis=1)
        dyy = du * _silu(z)
        do_ref[...] = jnp.concatenate(do_l, axis=1).astype(do_ref.dtype)
        dg_ref[...] = jnp.concatenate(dg_l, axis=1).astype(dg_ref.dtype)
        dys_ref[...] = dyy.astype(dys_ref.dtype)
        dz_ref[...] = (du * yy * _dsilu(z)).astype(dz_ref.dtype)
        dxs_ref[...] = (dyy * vec_ref[1:2, :]).astype(dxs_ref.dtype)
        acc_ref[1:2, :] += _colsum(dyy * xs)

    row = lambda j: (lambda i: (jnp.minimum(i, nlat - 1), j))
    lat = pl.BlockSpec((TB, D), row(0))
    tok = pl.BlockSpec((TB, D), lambda i: (i, 0))
    call = dict(
        body=body, args=[dylat, o_f, o_b, p_main, p_main, y_f, y_b, xa, vecs, w_out], name="mix_bwd",
        out_shape=(jax.ShapeDtypeStruct((n, D), MXU_DTYPE), jax.ShapeDtypeStruct((t_total, D), MXU_DTYPE),
                   jax.ShapeDtypeStruct((n, D), MXU_DTYPE), jax.ShapeDtypeStruct((t_total, D), MXU_DTYPE),
                   jax.ShapeDtypeStruct((n, D), MXU_DTYPE), jax.ShapeDtypeStruct((8, D), F32)),
        grid=(t_total // TB,),
        in_specs=[lat, lat, lat, pl.BlockSpec((TB, D), row(4)), pl.BlockSpec((TB, D), row(5)), lat, lat, lat,
                  _full((8, D)), _full((2 * D, D))],
        out_specs=(lat, tok, lat, tok, lat, _full((8, D))), scratch=[],
        sem=("arbitrary",), vmem_mb=48)
    return _run(_carry(call, comm, lambda: (pl.program_id(0) == 0, pl.program_id(0) == t_total // TB - 1)))


def _ffn_up(u2, w_gate, w_up):
    n = u2.shape[0]
    tb = 1024

    def body(u_ref, wg_ref, wu_ref, g_ref, up_ref, a_ref):
        uv = u_ref[...]
        gt = _nt(uv, wg_ref[...])
        upv = _nt(uv, wu_ref[...])
        g_ref[...] = gt.astype(g_ref.dtype)
        up_ref[...] = upv.astype(up_ref.dtype)
        a_ref[...] = (_silu(gt) * upv).astype(a_ref.dtype)

    blk = pl.BlockSpec((tb, FSL), lambda j, i: (i, j))
    wblk = pl.BlockSpec((None, FSL, D), lambda j, i: (j, 0, 0))
    return _pcall(
        body, name="ffn_up",
        out_shape=(jax.ShapeDtypeStruct((n, DFFP), MXU_DTYPE),) * 3,
        grid=(4, n // tb), in_specs=[pl.BlockSpec((tb, D), lambda j, i: (i, 0)), wblk, wblk],
        out_specs=(blk, blk, blk), sem=("parallel", "parallel"), vmem_mb=48,
    )(u2, w_gate, w_up)


def _ffn_down_loss(act, w_down, h1, tgt, vecs):
    n = act.shape[0]
    tb = 512

    def body(a_ref, w_ref, h1_ref, t_ref, vec_ref, dh2_ref, dffn_ref, acc_ref):
        i = pl.program_id(0)

        @pl.when(i == 0)
        def _():
            acc_ref[...] = jnp.zeros_like(acc_ref)

        g2 = vec_ref[0:1, :]
        fw = vec_ref[1:2, :]
        nsub = 4
        sb = tb // nsub
        wv = w_ref[...]
        ffns = [_nn(a_ref[r_ * sb:(r_ + 1) * sb, :], wv) for r_ in range(nsub)]
        for r_ in range(nsub):
            rows = slice(r_ * sb, (r_ + 1) * sb)
            ffn = ffns[r_]
            h2 = h1_ref[rows, :] + g2 * ffn
            r = lax.rsqrt(jnp.mean(h2 * h2, axis=1, keepdims=True) + EPS)
            xh = h2 * r
            err = xh * fw - t_ref[rows, :]
            dy = err * (1.0 / D)
            acc_ref[2:3, :] += _colsum(err * err) * (0.5 / D)
            acc_ref[1:2, :] += _colsum(dy * xh)
            dxh = dy * fw
            dh2 = r * (dxh - xh * jnp.mean(dxh * xh, axis=1, keepdims=True))
            dh2_ref[rows, :] = dh2
            dffn_ref[rows, :] = (g2 * dh2).astype(dffn_ref.dtype)
            acc_ref[0:1, :] += _colsum(dh2 * ffn)

    return _pcall(
        body, name="ffn_down_loss",
        out_shape=(jax.ShapeDtypeStruct((n, D), F32), jax.ShapeDtypeStruct((n, D), MXU_DTYPE),
                   jax.ShapeDtypeStruct((8, D), F32)),
        grid=(n // tb,),
        in_specs=[pl.BlockSpec((tb, DFFP), lambda i: (i, 0)), _full((DFFP, D)), pl.BlockSpec((tb, D), lambda i: (i, 0)),
                  pl.BlockSpec((tb, D), lambda i: (i, 0)), _full((8, D))],
        out_specs=(pl.BlockSpec((tb, D), lambda i: (i, 0)), pl.BlockSpec((tb, D), lambda i: (i, 0)), _full((8, D))),
        sem=("arbitrary",), vmem_mb=48,
    )(act, w_down, h1, tgt, vecs)


def _ffn_bwd(dffn, w_down, gate, up, w_gate_t, w_up_t):
    n = dffn.shape[0]
    tb = 1024

    def body(df_ref, wd_ref, g_ref, up_ref, wg_ref, wu_ref, dg_ref, dup_ref, du_ref):
        j = pl.program_id(1)
        nsub = 4
        sb = tb // nsub
        wd, wg, wu = wd_ref[...], wg_ref[...], wu_ref[...]
        dacts = [_nt(df_ref[r * sb:(r + 1) * sb, :], wd) for r in range(nsub)]
        parts = []
        for r in range(nsub):
            rows = slice(r * sb, (r + 1) * sb)
            gt = g_ref[rows, :].astype(F32)
            upv = up_ref[rows, :].astype(F32)
            sg = _sig(gt)
            dgt = (dacts[r] * upv * (sg * (1.0 + gt * (1.0 - sg)))).astype(MXU_DTYPE)
            dupv = (dacts[r] * (gt * sg)).astype(MXU_DTYPE)
            dg_ref[rows, :] = dgt
            dup_ref[rows, :] = dupv
            parts.append(_nn(dgt, wg) + _nn(dupv, wu))
        part = jnp.concatenate(parts, axis=0)

        @pl.when(j == 0)
        def _():
            du_ref[...] = part

        @pl.when(j > 0)
        def _():
            du_ref[...] += part

    tok = pl.BlockSpec((tb, D), lambda i, j: (i, 0))
    ffb = pl.BlockSpec((tb, FSL), lambda i, j: (i, j))
    wsl = pl.BlockSpec((None, FSL, D), lambda i, j: (j, 0, 0))
    return _pcall(
        body, name="ffn_bwd",
        out_shape=(jax.ShapeDtypeStruct((n, DFFP), MXU_DTYPE), jax.ShapeDtypeStruct((n, DFFP), MXU_DTYPE),
                   jax.ShapeDtypeStruct((n, D), F32)),
        grid=(n // tb, 4),
        in_specs=[tok, pl.BlockSpec((FSL, D), lambda i, j: (j, 0)), ffb, ffb, wsl, wsl],
        out_specs=(ffb, ffb, tok), sem=("parallel", "arbitrary"), vmem_mb=48,
    )(dffn, w_down, gate, up, w_gate_t, w_up_t)


def _ffn_norm_bwd(du, h1, ylat, dh2, vecs):
    n = du.shape[0]
    tb = 512

    def body(du_ref, h1_ref, yl_ref, dh2_ref, vec_ref, dh1_ref, dyl_ref, acc_ref):
        @pl.when(pl.program_id(0) == 0)
        def _():
            acc_ref[...] = jnp.zeros_like(acc_ref)

        duv = du_ref[...]
        h1 = h1_ref[...]
        r = lax.rsqrt(jnp.mean(h1 * h1, axis=1, keepdims=True) + EPS)
        xh = h1 * r
        nw = vec_ref[2:3, :]
        acc_ref[0:1, :] += _colsum(duv)
        acc_ref[1:2, :] += _colsum(duv * xh * nw)
        dn = duv * vec_ref[1:2, :]
        acc_ref[2:3, :] += _colsum(dn * xh)
        dxh = dn * nw
        dh1 = dh2_ref[...] + r * (dxh - xh * jnp.mean(dxh * xh, axis=1, keepdims=True))
        dh1_ref[...] = dh1
        dyl_ref[...] = (vec_ref[0:1, :] * dh1).astype(dyl_ref.dtype)
        acc_ref[3:4, :] += _colsum(dh1 * yl_ref[...])

    tok = pl.BlockSpec((tb, D), lambda i: (i, 0))
    return _pcall(
        body, name="ffn_norm_bwd",
        out_shape=(jax.ShapeDtypeStruct((n, D), F32), jax.ShapeDtypeStruct((n, D), MXU_DTYPE),
                   jax.ShapeDtypeStruct((8, D), F32)),
        grid=(n // tb,), in_specs=[tok, tok, tok, tok, _full((8, D))], out_specs=(tok, tok, _full((8, D))),
        sem=("arbitrary",), vmem_mb=40,
    )(du, h1, ylat, dh2, vecs)


def _deep_rows(rows):
    return max(r for r in range(128, 2305, 128) if rows % r == 0)


def _dw(a, b, name):
    tn_rows = a.shape[0]
    bt = _deep_rows(tn_rows)
    kk, nn_ = a.shape[1], b.shape[1]
    bk = 1024 if kk % 1024 == 0 else kk
    bn = 1024 if nn_ % 1024 == 0 else nn_
    nt = tn_rows // bt

    def body(a_ref, b_ref, o_ref, acc):
        t = pl.program_id(2)
        part = _tn(a_ref[...], b_ref[...])

        @pl.when(t == 0)
        def _():
            acc[...] = part

        @pl.when(t > 0)
        def _():
            acc[...] += part

        @pl.when(t == nt - 1)
        def _():
            o_ref[...] = acc[...].astype(o_ref.dtype)

    return _pcall(
        body, name=name, out_shape=jax.ShapeDtypeStruct((kk, nn_), MXU_DTYPE), grid=(kk // bk, nn_ // bn, nt),
        in_specs=[pl.BlockSpec((bt, bk), lambda i, j, t: (t, i)), pl.BlockSpec((bt, bn), lambda i, j, t: (t, j))],
        out_specs=pl.BlockSpec((bk, bn), lambda i, j, t: (i, j)), scratch=[pltpu.VMEM((bk, bn), F32)],
        sem=("parallel", "parallel", "arbitrary"), vmem_mb=40,
    )(a, b)


def _dw_in(segs, u_all, name):
    tiles = []
    for m, s_ in enumerate(segs):
        tiles += [(m, h) for h in range(s_.shape[1] // D)]
    ntile = len(tiles)
    t_total = u_all.shape[0]
    bt = _deep_rows(t_total)
    nt = t_total // bt

    def body(u_ref, *refs):
        seg_refs, o_ref, acc = refs[:len(segs)], refs[len(segs)], refs[len(segs) + 1]
        n, t = pl.program_id(0), pl.program_id(1)
        for k, (m, _) in enumerate(tiles):
            @pl.when(n == k)
            def _(m=m):
                part = _tn(seg_refs[m][...], u_ref[...])

                @pl.when(t == 0)
                def _():
                    acc[...] = part

                @pl.when(t > 0)
                def _():
                    acc[...] += part

        @pl.when(t == nt - 1)
        def _():
            o_ref[...] = acc[...].astype(o_ref.dtype)

    def seg_spec(m):
        ks = [k for k, (mm, _) in enumerate(tiles) if mm == m]
        lo, hi = ks[0], ks[-1]
        on = lambda n: (n >= lo) & (n <= hi)
        return pl.BlockSpec((bt, D), lambda n, t: (jnp.where(on(n), t, 0), jnp.where(on(n), n - lo, 0)))

    return _pcall(
        body, name=name, out_shape=jax.ShapeDtypeStruct((1, ntile * D, D), MXU_DTYPE), grid=(ntile, nt),
        in_specs=[pl.BlockSpec((bt, D), lambda n, t: (t, 0))] + [seg_spec(m) for m in range(len(segs))],
        out_specs=pl.BlockSpec((None, D, D), lambda n, t: (0, n, 0)),
        scratch=[pltpu.VMEM((D, D), F32)], sem=("parallel", "arbitrary"), vmem_mb=56,
    )(u_all, *segs)


def _du_prenorm_bwd(segs, ddt, wi_main, wi_tail, xin, mods, dres, row_off, tb, name, comm=None):
    n = xin.shape[0]
    nt = n // tb
    off = row_off // tb
    has_dx = dres is not None

    def body(*refs):
        seg_refs = refs[:7]
        ddt_ref, w_ref, wb_ref, wdt_ref, x_ref, mod_ref = refs[7:13]
        rest = refs[13:]
        if has_dx:
            dres_ref, dx_ref, acc_ref, du_scr = rest
        else:
            acc_ref, du_scr = rest
        j, i = pl.program_id(0), pl.program_id(1)
        rows = pl.ds(pl.multiple_of(i * tb, tb), tb)

        @pl.when((i == 0) & (j == 0))
        def _():
            acc_ref[...] = jnp.zeros_like(acc_ref)

        @pl.when(j == 0)
        def _():
            du_scr[rows, :] = _nn(ddt_ref[...], wdt_ref[...])

        for k in range(8):
            if not has_dx and k in (4, 5):
                continue

            @pl.when(j == k)
            def _(k=k):
                sv = seg_refs[min(k, 6)][...]
                part = _nn(sv, w_ref[...])
                if k in (2, 4, 6):
                    part = part + _nn(sv[:, 0:WTAIL], wb_ref[...])
                du_scr[rows, :] += part

        @pl.when(j == 7)
        def _():
            du = du_scr[rows, :]
            xv = x_ref[...]
            r = lax.rsqrt(jnp.mean(xv * xv, axis=1, keepdims=True) + EPS)
            xh = xv * r
            nw = mod_ref[1:2, :]
            acc_ref[0:1, :] += _colsum(du)
            acc_ref[1:2, :] += _colsum(du * xh * nw)
            dn = du * mod_ref[0:1, :]
            acc_ref[2:3, :] += _colsum(dn * xh)
            if has_dx:
                dxh = dn * nw
                dx_ref[...] = dres_ref[...] + r * (dxh - xh * jnp.mean(dxh * xh, axis=1, keepdims=True))

    def seg_spec(k):
        if k < 6:
            return pl.BlockSpec((tb, D), lambda j, i: (jnp.where(j == k, i + off, 0), 0))
        return pl.BlockSpec((tb, D), lambda j, i: (jnp.where(j >= 6, i + off, 0), jnp.where(j >= 6, j - 6, 0)))

    last = pl.BlockSpec((tb, D), lambda j, i: (jnp.where(j == 7, i, 0), 0))
    in_specs = [seg_spec(k) for k in range(7)]
    in_specs += [pl.BlockSpec((tb, 128), lambda j, i: (jnp.where(j == 0, i + off, 0), 0))] + _w_specs()
    in_specs += [last, _full((8, D))]
    args = list(segs) + [ddt, wi_main, wi_tail, wi_tail, xin, mods]
    out_shape = [jax.ShapeDtypeStruct((8, D), F32)]
    out_specs = [_full((8, D))]
    if has_dx:
        in_specs.append(last)
        args.append(dres)
        out_shape.insert(0, jax.ShapeDtypeStruct((n, D), F32))
        out_specs.insert(0, last)
    call = dict(body=body, args=args, name=name, out_shape=tuple(out_shape), grid=(8, nt), in_specs=in_specs,
                out_specs=tuple(out_specs), scratch=[pltpu.VMEM((n, D), F32)], sem=("arbitrary", "arbitrary"),
                vmem_mb=56)
    steps = lambda: ((pl.program_id(0) == 0) & (pl.program_id(1) == 0),
                     (pl.program_id(0) == 7) & (pl.program_id(1) == nt - 1))
    return _run(_carry(call, comm, steps))


def _sum8(v):
    def body(v_ref, o_ref):
        acc = v_ref[0]
        for k in range(1, 8):
            acc = acc + v_ref[k]
        o_ref[...] = acc

    return _pcall(body, name="small_sum", out_shape=jax.ShapeDtypeStruct(v.shape[1:], F32),
                  in_specs=[pl.BlockSpec(memory_space=pltpu.VMEM)], out_specs=pl.BlockSpec(memory_space=pltpu.VMEM))(v)


def _adamw(w, m, v, g, name):
    lead = w.ndim == 3
    rows, cols = w.shape[-2:]
    rb = 256 if rows % 256 == 0 else (352 if rows % 352 == 0 else rows)
    c1 = 1.0 - B1 ** STEP
    c2 = 1.0 - B2 ** STEP

    def body(w_ref, m_ref, v_ref, g_ref, d_ref, nm_ref, nv_ref):
        gv = g_ref[...]
        mn = B1 * m_ref[...] + (1.0 - B1) * gv
        vn = B2 * v_ref[...] + (1.0 - B2) * (gv * gv)
        nm_ref[...] = mn
        nv_ref[...] = vn
        d_ref[...] = -LR * ((mn / c1) / (jnp.sqrt(vn / c2) + AEPS) + WD * w_ref[...])

    if rb == rows and rows > 1024:
        cb, steps = 256, cols // 256
        gspec = pl.BlockSpec((rows, cb), lambda i: (0, i))
        spec = pl.BlockSpec((None, rows, cb), lambda i: (0, 0, i)) if lead else gspec
    else:
        steps = rows // rb
        gspec = pl.BlockSpec((rb, cols), lambda i: (i, 0))
        spec = pl.BlockSpec((None, rb, cols), lambda i: (0, i, 0)) if lead else gspec
    return _pcall(
        body, name=name, out_shape=(jax.ShapeDtypeStruct(w.shape, F32),) * 3, grid=(steps,),
        in_specs=[spec] * 3 + [gspec], out_specs=(spec,) * 3, sem=("parallel",), vmem_mb=40,
    )(w, m, v, g)


def _rows(v, n):
    f = v.reshape(-1)
    return jnp.pad(f, (0, n * D - f.shape[0])).reshape(n, D)


def kernel(x, c, ctx, c_ctx, w_ada, b_ada, norm_mix, w_in, conv_w, conv_b, ssd_a_log, ssd_dt_bias, ssd_d, ssd_norm, hgrn_lb_raw, hgrn_norm, w_out, norm_ffn, w_gate, w_up, w_down, final_norm, loss_target, m_c_ctx, m_w_ada, m_b_ada, m_norm_mix, m_w_in, m_conv_w, m_conv_b, m_ssd_a_log, m_ssd_dt_bias, m_ssd_d, m_ssd_norm, m_hgrn_lb_raw, m_hgrn_norm, m_w_out, m_norm_ffn, m_w_gate, m_w_up, m_w_down, m_final_norm, v_c_ctx, v_w_ada, v_b_ada, v_norm_mix, v_w_in, v_conv_w, v_conv_b, v_ssd_a_log, v_ssd_dt_bias, v_ssd_d, v_ssd_norm, v_hgrn_lb_raw, v_hgrn_norm, v_w_out, v_norm_ffn, v_w_gate, v_w_up, v_w_down, v_final_norm):
    ix, iy, ic = lax.axis_index("x"), lax.axis_index("y"), lax.axis_index("c")
    chip = 2 * ix + iy
    me = 2 * chip + ic
    xl, xc, tgt = x[0], ctx[0], loss_target[0]
    n_lat, n_ctx = xl.shape[0], xc.shape[0]
    assert n_ctx == TB and n_lat % 1024 == 0
    t_total = n_lat + n_ctx
    nb = t_total // TB

    tr = lambda a: jnp.swapaxes(a, -1, -2)
    shift = [functools.partial(jnp.pad, pad_width=((8 * k, WSL + WTAIL - NSH - 8 * k), (0, 0))) for k in range(4)]
    slab = lax.switch(chip, shift, tr(w_in[0]).astype(MXU_DTYPE))
    padrows = lambda a: jnp.pad(a, ((0, FSL - DFF // 4), (0, 0))).astype(MXU_DTYPE)
    shards = [slab[:WSL], slab[WSL:], w_out[0].astype(MXU_DTYPE), padrows(tr(w_gate[0])), padrows(tr(w_up[0])),
              padrows(w_down[0])]
    own = lambda g_, s_: lax.dynamic_update_slice(g_, s_[None], (chip, 0, 0))
    pack = jnp.concatenate([c, hgrn_lb_raw.reshape(1, D), _rows(conv_w[0], 3), jnp.zeros((3, D), F32)], axis=0)
    ncol_ada = w_ada.shape[2]
    b_shard = lax.dynamic_slice(b_ada, (0, chip * ncol_ada), (1, ncol_ada))
    gath, araw, mod_all, wi_main, wi_tail = _prologue(pack, c_ctx.reshape(1, D), w_ada[0], b_shard, shards[:2])
    wi_main, wi_tail = own(wi_main, shards[0]), own(wi_tail, shards[1])
    gath = gath.reshape(8, 8, D)
    lbraw_full = gath[0::2, 1].reshape(4, 2, 2, 256).transpose(1, 2, 0, 3).reshape(4, D)
    convw_full = gath[0::2, 2:5].reshape(4, 3 * D)[:, :KCONV * 512].reshape(4, KCONV, 512).transpose(1, 0, 2)
    convw_full = convw_full.reshape(KCONV, 2048)
    lbraw8 = jnp.pad(lbraw_full, ((0, 4), (0, 0)))
    convp = jnp.concatenate([convw_full, conv_b, jnp.zeros((2, 2048), F32)], axis=0)
    dtb = jnp.pad(ssd_dt_bias.reshape(1, 32), ((0, 7), (0, 96)))
    alog = jnp.pad(ssd_a_log.reshape(1, 32), ((0, 7), (0, 96)))
    mod_all = mod_all.reshape(8, 16, ncol_ada)[0::2]
    mod_full = mod_all.transpose(1, 0, 2).reshape(16, 4 * ncol_ada)
    my_mod = lax.dynamic_slice(mod_full, (me, 0), (1, 6 * D)).reshape(6, D)
    sh1, sc1, g1, sh2, sc2, g2 = (my_mod[k:k + 1] for k in range(6))
    csh1, csc1 = mod_full[8:9, 0:D], mod_full[8:9, D:2 * D]

    zrow = jnp.zeros((1, D), F32)
    mods_lat = jnp.concatenate([1.0 + sc1, sh1, norm_mix, zrow, zrow, zrow, zrow, zrow], axis=0)
    mods_ctx = jnp.concatenate([1.0 + csc1, csh1, norm_mix, zrow, zrow, zrow, zrow, zrow], axis=0)
    outs = _inproj(xl, mods_lat, wi_main, wi_tail, t_total, 1024, 0, None, "inproj_lat",
                   comm=_comm_gather(shards[2:5]))
    wo_g, wg_g, wu_g = (own(g_, s_) for g_, s_ in zip(outs[3:], shards[2:5]))
    w_out_f = wo_g.reshape(2 * D, D)
    p_main, p_dt, u_all = _inproj(xc, mods_ctx, wi_main, wi_tail, t_total, TB, nb - 1, outs[:3], "inproj_ctx")

    o_f, hs_f, wd_g = _hgrn_fwd(p_main, lbraw8, 0, nb, comm=_comm_gather(shards[5:]))
    w_down_f = own(wd_g, shards[5]).reshape(DFFP, D)
    o_b, hs_b = _hgrn_fwd(p_main, lbraw8, 1, nb)
    xa, dts = _ssd_prep(p_main, p_dt, convp, dtb, nb)
    y_f, ss_f = _ssd_fwd(xa, dts, alog, 0, nb)
    y_b, ss_b = _ssd_fwd(xa, dts, alog, 1, nb)

    vec_mix = jnp.concatenate([jnp.tile(hgrn_norm, (1, NH)), jnp.repeat(ssd_d, SP, axis=1), ssd_norm, g1, 1.0 + sc2,
                               sh2, norm_ffn, zrow], axis=0)
    ymix, ylat, h1, u2 = _mix_out(o_f, o_b, p_main, y_f, y_b, xa, xl, vec_mix, w_out_f)
    gate, up, act = _ffn_up(u2, wg_g, wu_g)
    vec_loss = jnp.concatenate([g2, final_norm.reshape(1, D)] + [zrow] * 6, axis=0)
    dh2, dffn, acc_loss = _ffn_down_loss(act, w_down_f, h1, tgt, vec_loss)

    core_arr = jnp.reshape(ic, (1,)).astype(jnp.int32)
    chip_arr = jnp.reshape(chip, (1,)).astype(jnp.int32)
    every = (0, 4)

    def pair_sum(gs, got, tag):
        return list(_pair_sum(gs, list(got), core_arr, "grads_pair_sum_" + tag))

    vec_ffn = jnp.concatenate([g1, 1.0 + sc2, norm_ffn] + [zrow] * 5, axis=0)
    dgate, dup, du2 = _ffn_bwd(dffn, w_down_f, gate, up, wg_g, wu_g)
    dh1, dylat, acc_ffn = _ffn_norm_bwd(du2, h1, ylat, dh2, vec_ffn)
    gw_down = _dw(act, dffn, "dw_down").reshape(4, FSL, D)
    ga1 = [_dw(dgate, u2, "dw_gate").reshape(4, FSL, D), _dw(dup, u2, "dw_up").reshape(4, FSL, D)]
    res = _mix_bwd(dylat, o_f, o_b, p_main, y_f, y_b, xa, vec_mix, w_out_f, comm=_comm_pair(ga1))
    (do, dgr, dys, dzr, dxs_skip, acc_mix), pair_a1 = res[:6], pair_sum(ga1, res[6:], "a1")
    ga2 = [gw_down, _dw(ymix, dylat, "dw_out").reshape(4, D // 2, D)]

    res = _hgrn_bwd(p_main, lbraw8, hs_f, do, 0, nb, None,
                    comm=[_comm_exchange(pair_a1, [every] * 2), _comm_pair(ga2)])
    (dq0, dff, dv0, dlb_f), recv_a, pair_a2 = res[:4], list(res[4:6]), pair_sum(ga2, res[6:], "a2")
    res = _hgrn_bwd(p_main, lbraw8, hs_b, do, 1, nb, (dq0, dv0), comm=_comm_exchange(pair_a2, [every] * 2))
    (dq, dfb, dv, dlb_b), recv_a = res[:4], recv_a + list(res[4:])
    pair_a, dests_a = pair_a1 + pair_a2, [every] * 4
    gw_in = [_dw_in([dq, dff], u_all, "dw_in_0"), _dw_in([dfb, dv], u_all, "dw_in_1"),
             _dw_in([dgr, dzr], u_all, "dw_in_2")]

    res = _ssd_bwd(xa, dts, alog, ss_f, dys, 0, nb, None, comm=_comm_pair(gw_in))
    (dxa0, ddts0, da_f), pair_b, dests_b = res[:3], pair_sum(gw_in, res[3:], "b"), [(0, 1), (1, 2), (2, 3)]
    res = _ssd_bwd(xa, dts, alog, ss_b, dys, 1, nb, (dxa0, ddts0), comm=_comm_exchange(pair_b, dests_b))
    (dxa, ddts, da_b), recv_b = res[:3], list(res[3:])
    dxbc, ddt, acc_conv, acc_dtb = _ssd_prep_bwd(p_main, p_dt, convp, dtb, dxa, dxs_skip, ddts, nb)
    gw_in.append(_dw_in([dxbc], u_all, "dw_in_3"))
    gw_in_dt = _dw(ddt, u_all, "dw_in_dt")
    gc = [gw_in[3], jnp.concatenate([g_[:, 0:WTAIL, :] for g_ in gw_in[1:]] + [gw_in_dt[None]], axis=0)]

    segs = [dq, dff, dfb, dv, dgr, dzr, dxbc]
    bmods_lat = jnp.concatenate([1.0 + sc1, norm_mix] + [zrow] * 6, axis=0)
    bmods_ctx = jnp.concatenate([1.0 + csc1, norm_mix] + [zrow] * 6, axis=0)
    res = _du_prenorm_bwd(segs, ddt, wi_main, wi_tail, xc, bmods_ctx, None, n_lat, TB, "du_ctx", comm=_comm_pair(gc))
    acc_ctx, pair_c, dests_c = res[0], pair_sum(gc, res[1:], "c"), [(3, 4), every]
    res = _du_prenorm_bwd(segs, ddt, wi_main, wi_tail, xl, bmods_lat, dh1, 0, 512, "du_lat",
                          comm=_comm_exchange(pair_c, dests_c))
    (grad_x, acc_lat), recv_c = res[:2], list(res[2:])

    mine = _chip_sum(pair_b + pair_c + pair_a, recv_b + recv_c + recv_a, chip_arr, dests_b + dests_c + dests_a,
                     [0, 0, 0, 0, 1, 3, 4, 5, 2])
    theirs = _pair_swap(mine)
    whole = [jnp.concatenate([jnp.where(ic == 0, m_, t_), jnp.where(ic == 0, t_, m_)], axis=0)
             for m_, t_ in zip(mine, theirs)]
    g_w_in = lax.dynamic_slice(jnp.concatenate(whole[0:2], axis=0), (8 * chip, 0), (NSH, D))
    g_w_out = whole[2]
    g_w_gate = whole[3][:DFF // 4]
    g_w_up = whole[4][:DFF // 4]
    g_w_down = whole[5][:DFF // 4]

    dmod_lat = jnp.concatenate([acc_lat[0:2], acc_ffn[3:4], acc_ffn[0:2], acc_loss[0:1]], axis=0)
    misc = jnp.concatenate([(da_f + da_b)[0, :32], jnp.zeros((96,), F32), acc_dtb[0, :32], jnp.zeros((96,), F32),
                            jnp.sum(acc_loss[2]).reshape(1), jnp.zeros((D - 257,), F32)]).reshape(1, D)
    sv = jnp.concatenate([
        dmod_lat, acc_ctx[0:2], (acc_lat[2:3] + acc_ctx[2:3]), acc_ffn[2:3], acc_loss[1:2], acc_mix[2:3],
        acc_mix[0:1], acc_mix[1:2], dlb_f[0:1], dlb_b[0:1], acc_conv[0:6].reshape(12, D), misc,
        jnp.zeros((3, D), F32)], axis=0)
    sv_all = _allgather8(sv, "small_grads_gather").reshape(8, 32, D)
    ssum = _sum8(sv_all)
    dmod_rows = sv_all[:, 0:6].reshape(8, 6 * D)
    dmod_ctx_row = jnp.concatenate([ssum[6:8].reshape(1, 2 * D), jnp.zeros((1, 4 * D), F32)], axis=1)
    dmod_full = jnp.concatenate([dmod_rows, dmod_ctx_row, jnp.zeros((7, 6 * D), F32)], axis=0)
    grad_b_ada = jnp.sum(dmod_full, axis=0, keepdims=True)
    dmod_shard = lax.dynamic_slice(dmod_full, (0, chip * ncol_ada), (16, ncol_ada))
    g_w_ada, da_part = _ada_bwd(araw, dmod_shard, w_ada[0])
    da_all = _allgather8(da_part, "ada_ctx_gather").reshape(8, 16, D)[0::2, 8]
    cc = c_ctx.reshape(1, D)
    grad_c_ctx = (jnp.sum(da_all, axis=0, keepdims=True) * _dsilu(cc)).reshape(D)

    grad_norm_mix, grad_norm_ffn, grad_final_norm = ssum[8:9], ssum[9:10], ssum[10].reshape(D)
    grad_ssd_norm = ssum[11:12]
    grad_hgrn_norm = jnp.sum(ssum[12].reshape(NH, HF), axis=0, keepdims=True)
    grad_ssd_d = jnp.sum(ssum[13].reshape(SHEADS, SP), axis=1).reshape(1, SHEADS)
    lb_full = _sig(lbraw_full[0:2] - lbraw_full[2:4])
    dr0 = ssum[14:16] * lb_full * (1.0 - lb_full)
    grad_lb_full = jnp.stack([dr0, -dr0], axis=0)
    grad_lb = lax.dynamic_slice(grad_lb_full, (0, 0, chip * 256), (2, 2, 256))
    grad_conv_w = lax.dynamic_slice(ssum[16:26].reshape(KCONV, 2048), (0, chip * 512), (KCONV, 512)).reshape(1, KCONV, 512)
    grad_conv_b = ssum[26:28].reshape(1, 2048)
    a_val = -jnp.exp(ssd_a_log)
    grad_a_log = ssum[28, 0:32].reshape(1, 2, SHEADS) * a_val
    grad_dt_bias = ssum[28, 128:160].reshape(1, 2, SHEADS)
    loss = ssum[28, 256]

    small_w = [c_ctx, b_ada, norm_mix, conv_w, conv_b, ssd_a_log, ssd_dt_bias, ssd_d, ssd_norm, hgrn_lb_raw,
               hgrn_norm, norm_ffn, final_norm]
    small_m = [m_c_ctx, m_b_ada, m_norm_mix, m_conv_w, m_conv_b, m_ssd_a_log, m_ssd_dt_bias, m_ssd_d, m_ssd_norm,
               m_hgrn_lb_raw, m_hgrn_norm, m_norm_ffn, m_final_norm]
    small_v = [v_c_ctx, v_b_ada, v_norm_mix, v_conv_w, v_conv_b, v_ssd_a_log, v_ssd_dt_bias, v_ssd_d, v_ssd_norm,
               v_hgrn_lb_raw, v_hgrn_norm, v_norm_ffn, v_final_norm]
    small_g = [grad_c_ctx, grad_b_ada, grad_norm_mix, grad_conv_w, grad_conv_b, grad_a_log, grad_dt_bias, grad_ssd_d,
               grad_ssd_norm, grad_lb, grad_hgrn_norm, grad_norm_ffn, grad_final_norm]
    nrows = [-(-a.size // D) for a in small_w]
    packs = lambda lst: jnp.concatenate([_rows(a, r) for a, r in zip(lst, nrows)]
                                        + [jnp.zeros((24 - sum(nrows), D), F32)], axis=0)
    sd, sm, svv = _adamw(packs(small_w), packs(small_m), packs(small_v), packs(small_g), "adamw_small")

    def unpack(p):
        out, r0 = [], 0
        for a, r in zip(small_w, nrows):
            out.append(p[r0:r0 + r].reshape(-1)[:a.size].reshape(a.shape))
            r0 += r
        return out

    sd, sm, svv = unpack(sd), unpack(sm), unpack(svv)
    big = {}
    for nm, w_, m_, v_, g_ in (("w_ada", w_ada, m_w_ada, v_w_ada, g_w_ada), ("w_in", w_in, m_w_in, v_w_in, g_w_in),
                               ("w_out", w_out, m_w_out, v_w_out, g_w_out),
                               ("w_gate", w_gate, m_w_gate, v_w_gate, g_w_gate),
                               ("w_up", w_up, m_w_up, v_w_up, g_w_up),
                               ("w_down", w_down, m_w_down, v_w_down, g_w_down)):
        if nm in ("w_in", "w_gate", "w_up"):
            big[nm] = tuple(tr(t) for t in (g_[None],) + tuple(_adamw(tr(w_), tr(m_), tr(v_), g_, "adamw_" + nm)))
        else:
            big[nm] = (g_[None],) + tuple(_adamw(w_, m_, v_, g_, "adamw_" + nm))

    order = ["c_ctx", "w_ada", "b_ada", "norm_mix", "w_in", "conv_w", "conv_b", "ssd_a_log", "ssd_dt_bias", "ssd_d",
             "ssd_norm", "hgrn_lb_raw", "hgrn_norm", "w_out", "norm_ffn", "w_gate", "w_up", "w_down", "final_norm"]
    small_names = ["c_ctx", "b_ada", "norm_mix", "conv_w", "conv_b", "ssd_a_log", "ssd_dt_bias", "ssd_d", "ssd_norm",
                   "hgrn_lb_raw", "hgrn_norm", "norm_ffn", "final_norm"]
    table = dict(big)
    for k, nm in enumerate(small_names):
        table[nm] = (small_g[k].reshape(small_w[k].shape), sd[k], sm[k], svv[k])
    grads = [table[nm][0] for nm in order]
    deltas = [table[nm][1] for nm in order]
    new_m = [table[nm][2] for nm in order]
    new_v = [table[nm][3] for nm in order]
    return (loss, grad_x[None], *grads, *deltas, *new_m, *new_v)
```

```python
import functools
import math

import jax
import jax.numpy as jnp
from jax import lax
from jax.experimental import pallas as pl
from jax.experimental.pallas import tpu as pltpu

F32 = jnp.float32
BF16 = jnp.bfloat16
MXU_DTYPE = jnp.bfloat16
_INTERPRET = False

D = 1024
NH, HF = 8, 128
HC = 64
SC = 128
SN = 128
SHEADS, SP = 16, 64
GRID_W = 64
KCONV = 5
DFF = 2816
FSL = 768
DFFP = 4 * FSL
NIN = 8224
TB = 256
EPS = 1e-6
LR, B1, B2, AEPS, WD, STEP = 0.001, 0.9, 0.999, 1e-08, 0.01, 10
MESH_ID = pl.DeviceIdType.MESH
NSH = NIN // 4
WSL = 2048
WTAIL = 128


def _pcall(body, *, name, out_shape, grid=(), in_specs=None, out_specs=None, scratch=(), sem=None,
           vmem_mb=None, aliases=None):
    params = {}
    if sem is not None:
        params["dimension_semantics"] = sem
    if vmem_mb is not None:
        params["vmem_limit_bytes"] = vmem_mb << 20
    kw = dict(name=name, out_shape=out_shape, scratch_shapes=list(scratch),
              input_output_aliases=aliases or {}, compiler_params=pltpu.CompilerParams(**params),
              interpret=_INTERPRET)
    if grid:
        kw["grid"] = grid
    if in_specs is not None:
        kw["in_specs"] = in_specs
    if out_specs is not None:
        kw["out_specs"] = out_specs
    return pl.pallas_call(body, **kw)


def _mx(a):
    return a.astype(MXU_DTYPE)


def _dg(a, b, ca, cb):
    return lax.dot_general(_mx(a), _mx(b), (((ca,), (cb,)), ((), ())), preferred_element_type=F32)


def _nn(a, b):
    return _dg(a, b, 1, 0)


def _nt(a, b):
    return _dg(a, b, 1, 1)


def _tn(a, b):
    return _dg(a, b, 0, 0)


def _dot01(m, x):
    hi = x.astype(BF16)
    r1 = x - hi.astype(F32)
    mid = r1.astype(BF16)
    lo = (r1 - mid.astype(F32)).astype(BF16)
    f = lambda t: lax.dot_general(m, t, (((1,), (0,)), ((), ())), preferred_element_type=F32)
    return f(hi) + f(mid) + f(lo)


def _tri(n, upper):
    r = lax.broadcasted_iota(jnp.int32, (n, n), 0)
    c = lax.broadcasted_iota(jnp.int32, (n, n), 1)
    return (c >= r) if upper else (c <= r)


def _b01(mask):
    return jnp.where(mask, 1.0, 0.0).astype(BF16)


def _sig(x):
    return jax.nn.sigmoid(x)


def _silu(x):
    return x * _sig(x)


def _dsilu(x):
    s = _sig(x)
    return s * (1.0 + x * (1.0 - s))


def _softplus(x):
    return jnp.maximum(x, 0.0) + jnp.log(1.0 + jnp.exp(-jnp.abs(x)))


def _rowsum(x):
    return jnp.sum(x, axis=1, keepdims=True)


def _colsum(x):
    return jnp.sum(x, axis=0, keepdims=True)


def _full(shape):
    return pl.BlockSpec(shape, lambda *_: (0,) * len(shape))


def _allgather8_ops(x_ref, out_ref, send_sems, recv_sems, local_sem):
    m_per = x_ref.shape[0]
    x, y, c = lax.axis_index("x"), lax.axis_index("y"), lax.axis_index("c")
    me, sibling = (x, y, c), (x, y, 1 - c)
    chips = [(1 - x, y), (x, 1 - y), (1 - x, 1 - y)]

    def rows(px, py, pc):
        return out_ref.at[pl.ds((4 * px + 2 * py + pc) * m_per, m_per), :]

    def copy(k, block, to, src=None):
        return pltpu.make_async_remote_copy(
            src_ref=rows(*block) if src is None else src, dst_ref=rows(*block),
            send_sem=send_sems.at[k], recv_sem=recv_sems.at[k], device_id=to, device_id_type=MESH_ID)

    mine = pltpu.make_async_copy(x_ref, rows(*me), local_sem)
    mine.start()
    first = [copy(0, me, sibling, src=x_ref)]
    first += [copy(1 + j, me, (*chip, c), src=x_ref) for j, chip in enumerate(chips)]
    for cp in first:
        cp.start()
    passed = [copy(4 + j, (*chip, c), sibling) for j, chip in enumerate(chips)]
    for j, chip in enumerate(chips):
        copy(1 + j, (*chip, c), me).wait_recv()
        passed[j].start()
    copy(0, sibling, me).wait_recv()
    for j, chip in enumerate(chips):
        copy(4 + j, (*chip, 1 - c), me).wait_recv()
    for cp in first + passed:
        cp.wait_send()
    mine.wait()


_AG8_SEMS = [pltpu.SemaphoreType.DMA((7,)), pltpu.SemaphoreType.DMA((7,)), pltpu.SemaphoreType.DMA]


def _allgather8(v, name):
    m_per, n = v.shape
    return _pcall(
        functools.partial(_allgather8_ops), name=name, out_shape=jax.ShapeDtypeStruct((8 * m_per, n), v.dtype),
        in_specs=[pl.BlockSpec(memory_space=pltpu.VMEM)], out_specs=pl.BlockSpec(memory_space=pltpu.VMEM),
        scratch=list(_AG8_SEMS),
    )(v)


def _prologue(pack, cc_row, w_ada, b_shard, shards):
    n = len(shards)
    ncol = w_ada.shape[1]

    def body(pack_ref, cc_ref, w_ref, b_ref, *refs):
        ins = refs[:n]
        gath_ref, araw_ref, mod_ref = refs[n:n + 3]
        outs = refs[n + 3:2 * n + 3]
        modsh, s1, r1, l1, s2, r2, l2, gs, gr = refs[2 * n + 3:]
        start, finish = _gather_ops(ins, outs, gs, gr, relay=True)
        start()
        _allgather8_ops(pack_ref, gath_ref, s1, r1, l1)
        a = jnp.concatenate([gath_ref[8 * i:8 * i + 1, :] for i in range(8)] + [cc_ref[...], jnp.zeros((7, D), F32)],
                            axis=0)
        araw_ref[...] = a
        modsh[...] = _nn(_silu(a), w_ref[...]) + b_ref[...]
        _allgather8_ops(modsh, mod_ref, s2, r2, l2)
        finish()

    vm = pl.BlockSpec(memory_space=pltpu.VMEM)
    anyspec = pl.BlockSpec(memory_space=pl.ANY)
    return _pcall(
        body, name="prologue",
        out_shape=(jax.ShapeDtypeStruct((64, D), F32), jax.ShapeDtypeStruct((16, D), F32),
                   jax.ShapeDtypeStruct((128, ncol), F32)) + _gather_out(shards),
        in_specs=[vm, vm, vm, vm] + [anyspec] * n, out_specs=(vm, vm, vm) + (anyspec,) * n,
        scratch=[pltpu.VMEM((16, ncol), F32)] + list(_AG8_SEMS) + list(_AG8_SEMS) + _gather_sems(n), vmem_mb=40,
    )(pack, cc_row, w_ada, b_shard, *shards)


def _gather_ops(ins, outs, send_sems, recv_sems, relay=False):
    n = len(ins)
    x, y, c = lax.axis_index("x"), lax.axis_index("y"), lax.axis_index("c")
    me, sibling = (x, y, c), (x, y, 1 - c)
    chips = [(1 - x, y), (x, 1 - y), (1 - x, 1 - y)]
    direct = 2 if relay else 3

    def part(a, px, py, pc, quarter=None):
        half = ins[a].shape[0] // 2
        if quarter is None:
            return outs[a].at[2 * px + py, pl.ds(pc * half, half), :]
        return outs[a].at[2 * px + py, pl.ds(pc * half + quarter * (half // 2), half // 2), :]

    def copy(a, k, block, to, src=None, quarter=None):
        return pltpu.make_async_remote_copy(
            src_ref=part(a, *block, quarter) if src is None else src, dst_ref=part(a, *block, quarter),
            send_sem=send_sems.at[8 * a + k], recv_sem=recv_sems.at[8 * a + k], device_id=to,
            device_id_type=MESH_ID)

    def first(a, j):
        half = ins[a].shape[0] // 2
        return copy(a, j, me, (*chips[j], c), src=ins[a].at[pl.ds(c * half, half), :])

    relayed = lambda a, q: copy(a, 6 + q, (*chips[q], c), (*chips[1 - q], c), quarter=q)

    def start():
        for a in range(n):
            for j in range(direct):
                first(a, j).start()

    def finish():
        for a in range(n):
            for j in range(direct):
                copy(a, j, (*chips[j], c), me).wait_recv()
                copy(a, 3 + j, (*chips[j], c), sibling).start()
                if relay:
                    relayed(a, j).start()
            if relay:
                for q in range(2):
                    copy(a, 6 + q, (*chips[2], c), me, quarter=q).wait_recv()
                copy(a, 5, (*chips[2], c), sibling).start()
        for a in range(n):
            for j, chip in enumerate(chips):
                copy(a, 3 + j, (*chip, 1 - c), me).wait_recv()
        for a in range(n):
            for j, chip in enumerate(chips):
                if j < direct:
                    first(a, j).wait_send()
                    if relay:
                        relayed(a, j).wait_send()
                copy(a, 3 + j, (*chip, c), sibling).wait_send()

    return start, finish


def _gather_out(shards):
    return tuple(jax.ShapeDtypeStruct((4,) + s_.shape, s_.dtype) for s_ in shards)


def _gather_sems(n):
    return [pltpu.SemaphoreType.DMA((8 * n,)), pltpu.SemaphoreType.DMA((8 * n,))]


def _pair_ops(ins, outs, send_sems, recv_sems):
    x, y, c = lax.axis_index("x"), lax.axis_index("y"), lax.axis_index("c")
    cps = []
    for a in range(len(ins)):
        half = ins[a].shape[1] // 2
        cps.append(pltpu.make_async_remote_copy(
            src_ref=ins[a].at[:, pl.ds((1 - c) * half, half), :], dst_ref=outs[a], send_sem=send_sems.at[a],
            recv_sem=recv_sems.at[a], device_id=(x, y, 1 - c), device_id_type=MESH_ID))

    def start():
        for cp in cps:
            cp.start()

    def finish():
        for cp in cps:
            cp.wait()

    return start, finish


def _comm_pair(gs):
    n = len(gs)
    return (list(gs), tuple(jax.ShapeDtypeStruct((g.shape[0], g.shape[1] // 2, g.shape[2]), g.dtype) for g in gs),
            [pltpu.SemaphoreType.DMA((n,)), pltpu.SemaphoreType.DMA((n,))], _pair_ops)


def _exchange_ops(ins, outs, send_sems, recv_sems, dests):
    x, y, c = lax.axis_index("x"), lax.axis_index("y"), lax.axis_index("c")
    mine = 2 * x + y
    chips = [(1 - x, y), (x, 1 - y), (1 - x, 1 - y)]

    def each(fn):
        for a in range(len(ins)):
            lo, hi = dests[a]
            for j, (px, py) in enumerate(chips):
                q = 2 * px + py
                cp = pltpu.make_async_remote_copy(
                    src_ref=ins[a].at[jnp.clip(q - lo, 0, hi - lo - 1)], dst_ref=outs[a].at[j],
                    send_sem=send_sems.at[3 * a + j], recv_sem=recv_sems.at[3 * a + j], device_id=(px, py, c),
                    device_id_type=MESH_ID)
                fn(cp, (q >= lo) & (q < hi), (mine >= lo) & (mine < hi), (lo, hi) == (0, 4))

    def start():
        def go(cp, send_ok, recv_ok, always):
            if always:
                cp.start()
            else:
                pl.when(send_ok)(cp.start)
        each(go)

    def finish():
        def go(cp, send_ok, recv_ok, always):
            if always:
                cp.wait()
            else:
                pl.when(send_ok)(cp.wait_send)
                pl.when(recv_ok)(cp.wait_recv)
        each(go)

    return start, finish


def _comm_exchange(hs, dests):
    n = len(hs)
    return (list(hs), tuple(jax.ShapeDtypeStruct((3,) + h.shape[1:], h.dtype) for h in hs),
            [pltpu.SemaphoreType.DMA((3 * n,)), pltpu.SemaphoreType.DMA((3 * n,))],
            lambda i, o, s, r: _exchange_ops(i, o, s, r, dests))


def _comm_gather(shards):
    return (list(shards), _gather_out(shards), _gather_sems(len(shards)), _gather_ops)


def _carry(call, comm, steps):
    if comm is None:
        return call
    if isinstance(comm, list):
        for one in comm:
            call = _carry(call, one, steps)
        return call
    arrays, out_shape, sems, make = comm
    n, n_in, n_out = len(arrays), len(call["args"]), len(call["out_shape"])
    body = call["body"]

    def wrapped(*refs):
        base_in, cin = refs[:n_in], refs[n_in:n_in + n]
        rest = refs[n_in + n:]
        base_out, cout, scr = rest[:n_out], rest[n_out:n_out + n], rest[n_out + n:]
        start, finish = make(cin, cout, scr[-2], scr[-1])
        first, last = steps()
        pl.when(first)(start)
        body(*base_in, *base_out, *scr[:-2])
        pl.when(last)(finish)

    anyspec = pl.BlockSpec(memory_space=pl.ANY)
    return dict(call, body=wrapped, args=list(call["args"]) + arrays,
                in_specs=list(call["in_specs"]) + [anyspec] * n,
                out_shape=tuple(call["out_shape"]) + tuple(out_shape),
                out_specs=tuple(call["out_specs"]) + (anyspec,) * n,
                scratch=list(call["scratch"]) + sems)


def _run(call):
    args = call.pop("args")
    body = call.pop("body")
    return _pcall(body, **call)(*args)


def _pair_swap(rs):
    n = len(rs)

    def body(*refs):
        ins, outs = refs[:n], refs[n:2 * n]
        send_sems, recv_sems = refs[2 * n:]
        x, y, c = lax.axis_index("x"), lax.axis_index("y"), lax.axis_index("c")
        cps = [pltpu.make_async_remote_copy(
            src_ref=ins[a], dst_ref=outs[a], send_sem=send_sems.at[a], recv_sem=recv_sems.at[a],
            device_id=(x, y, 1 - c), device_id_type=MESH_ID) for a in range(n)]
        for cp in cps:
            cp.start()
        for cp in cps:
            cp.wait()

    return _pcall(
        body, name="grads_pair_swap", out_shape=tuple(jax.ShapeDtypeStruct(r.shape, r.dtype) for r in rs),
        in_specs=[pl.BlockSpec(memory_space=pl.ANY)] * n, out_specs=(pl.BlockSpec(memory_space=pl.ANY),) * n,
        scratch=[pltpu.SemaphoreType.DMA((n,)), pltpu.SemaphoreType.DMA((n,))],
    )(*rs)


SUM_STEPS = 4


def _pair_sum(gs, recvs, core, name):
    n = len(gs)

    def body(c_ref, *refs):
        for a in range(n):
            refs[2 * n + a][...] = (refs[a][...].astype(F32) + refs[n + a][...].astype(F32)).astype(refs[2 * n + a].dtype)

    blk = lambda g: (g.shape[0], g.shape[1] // (2 * SUM_STEPS), g.shape[2])
    return pl.pallas_call(
        body, name=name,
        out_shape=tuple(jax.ShapeDtypeStruct((g.shape[0], g.shape[1] // 2, g.shape[2]), g.dtype) for g in gs),
        grid_spec=pltpu.PrefetchScalarGridSpec(
            num_scalar_prefetch=1, grid=(SUM_STEPS,),
            in_specs=[pl.BlockSpec(blk(g), lambda i, cr: (0, cr[0] * SUM_STEPS + i, 0)) for g in gs]
            + [pl.BlockSpec(blk(g), lambda i, cr: (0, i, 0)) for g in gs],
            out_specs=tuple(pl.BlockSpec(blk(g), lambda i, cr: (0, i, 0)) for g in gs)),
        compiler_params=pltpu.CompilerParams(vmem_limit_bytes=40 << 20), interpret=_INTERPRET,
    )(core, *gs, *recvs)


def _chip_sum(hs, recvs, chip, dests, slots):
    n = len(hs)
    nout = max(slots) + 1
    first = [slots.index(o) for o in range(nout)]
    every = lambda d_: d_ == (0, 4)

    def own(d_):
        if every(d_):
            return lambda i, kr: (kr[0], i, 0)
        return lambda i, kr: (0, jnp.where(kr[0] == d_[0], i, 0), 0)

    def got(d_):
        if every(d_):
            return lambda i, kr: (0, i, 0)
        return lambda i, kr: (0, jnp.where(kr[0] == d_[0], i, 0), 0)

    def body(k_ref, *refs):
        for a in range(n):
            def emit(a=a):
                acc = refs[a][0].astype(F32)
                for j in range(3):
                    acc = acc + refs[n + a][j].astype(F32)
                refs[2 * n + slots[a]][...] = acc
            if every(dests[a]):
                emit()
            else:
                pl.when(k_ref[0] == dests[a][0])(emit)

    rb = lambda h: h.shape[1] // SUM_STEPS
    return pl.pallas_call(
        body, name="grads_chip_sum",
        out_shape=tuple(jax.ShapeDtypeStruct(hs[a].shape[1:], F32) for a in first),
        grid_spec=pltpu.PrefetchScalarGridSpec(
            num_scalar_prefetch=1, grid=(SUM_STEPS,),
            in_specs=[pl.BlockSpec((1, rb(h), h.shape[2]), own(d_)) for h, d_ in zip(hs, dests)]
            + [pl.BlockSpec((3, rb(h), h.shape[2]), got(d_)) for h, d_ in zip(hs, dests)],
            out_specs=tuple(pl.BlockSpec((rb(hs[a]), hs[a].shape[2]), lambda i, kr: (i, 0)) for a in first)),
        compiler_params=pltpu.CompilerParams(vmem_limit_bytes=40 << 20), interpret=_INTERPRET,
    )(chip, *hs, *recvs)


def _ada_bwd(araw, dmod, w):
    nblk = w.shape[1] // 512

    def body(a_ref, d_ref, w_ref, gw_ref, da_ref):
        j = pl.program_id(0)
        gw_ref[...] = _tn(_silu(a_ref[...]), d_ref[...])
        part = _nt(d_ref[...], w_ref[...])

        @pl.when(j == 0)
        def _():
            da_ref[...] = part

        @pl.when(j > 0)
        def _():
            da_ref[...] += part

    return _pcall(
        body, name="ada_bwd",
        out_shape=(jax.ShapeDtypeStruct(w.shape, F32), jax.ShapeDtypeStruct((16, D), F32)), grid=(nblk,),
        in_specs=[_full((16, D)), pl.BlockSpec((16, 512), lambda j: (0, j)), pl.BlockSpec((D, 512), lambda j: (0, j))],
        out_specs=(pl.BlockSpec((D, 512), lambda j: (0, j)), _full((16, D))), sem=("arbitrary",),
    )(araw, dmod, w)


def _w_specs():
    return [pl.BlockSpec((None, D, D), lambda j, i: (j // 2, j % 2, 0)),
            pl.BlockSpec((None, WTAIL, D), lambda j, i: (jnp.maximum(j // 2 - 1, 0), 0, 0)),
            pl.BlockSpec((None, WTAIL, D), lambda j, i: (3, 0, 0))]


def _inproj(xin, mods, wi_main, wi_tail, t_total, tb, blk_off, prev, name, comm=None):
    n = xin.shape[0]
    nt = n // tb
    ncol = 8

    def body(x_ref, mod_ref, w_ref, wb_ref, wdt_ref, *rest):
        p_ref, pdt_ref, u_ref, uscr = rest[-4:]
        j, i = pl.program_id(0), pl.program_id(1)
        rows = pl.ds(pl.multiple_of(i * tb, tb), tb)

        @pl.when(j == 0)
        def _():
            xv = x_ref[...]
            r = lax.rsqrt(jnp.mean(xv * xv, axis=1, keepdims=True) + EPS)
            u = (xv * r * mod_ref[2:3, :]) * mod_ref[0:1, :] + mod_ref[1:2, :]
            ub = u.astype(MXU_DTYPE)
            uscr[rows, :] = ub
            u_ref[...] = ub
            pdt_ref[...] = _nt(ub, wdt_ref[...])

        ub = uscr[rows, :]
        pv = _nt(ub, w_ref[...])

        @pl.when((j % 2 == 1) | (j == 0))
        def _():
            p_ref[...] = pv.astype(p_ref.dtype)

        @pl.when((j % 2 == 0) & (j > 0))
        def _():
            head = pv[:, 0:WTAIL] + _nt(ub, wb_ref[...])
            p_ref[...] = jnp.concatenate([head, pv[:, WTAIL:]], axis=1).astype(p_ref.dtype)

    once = lambda j, i: (jnp.where(j == 0, i, nt - 1) + blk_off, 0)
    in_specs = [pl.BlockSpec((tb, D), lambda j, i: (jnp.where(j == 0, i, nt - 1), 0)), _full((8, D))] + _w_specs()
    args = [xin, mods, wi_main, wi_tail, wi_tail]
    aliases = None
    if prev is not None:
        in_specs += [pl.BlockSpec(memory_space=pl.ANY)] * 3
        args += list(prev)
        aliases = {5: 0, 6: 1, 7: 2}
    call = dict(
        body=body, args=args, name=name,
        out_shape=(jax.ShapeDtypeStruct((t_total, ncol * D), MXU_DTYPE), jax.ShapeDtypeStruct((t_total, 128), F32),
                   jax.ShapeDtypeStruct((t_total, D), MXU_DTYPE)),
        grid=(ncol, nt), in_specs=in_specs,
        out_specs=(pl.BlockSpec((tb, D), lambda j, i: (i + blk_off, j)), pl.BlockSpec((tb, 128), once),
                   pl.BlockSpec((tb, D), once)),
        scratch=[pltpu.VMEM((n, D), MXU_DTYPE)], sem=("arbitrary", "arbitrary"), vmem_mb=48, aliases=aliases)
    steps = lambda: ((pl.program_id(0) == 0) & (pl.program_id(1) == 0),
                     (pl.program_id(0) == ncol - 1) & (pl.program_id(1) == nt - 1))
    return _run(_carry(call, comm, steps))


def _blk(s, nb, rev):
    return jnp.where(s == 0, nb - 1, (nb - 1 - s) if rev else (s - 1))


def _hgrn_gate(fr, lbraw_ref, d):
    lb = _sig(lbraw_ref[d:d + 1, :] - lbraw_ref[2 + d:3 + d, :])
    sg = _sig(fr)
    return lb, sg, lb + (1.0 - lb) * sg


def _hgrn_fwd(p_main, lbraw, d, nb, comm=None):
    t_total = p_main.shape[0]
    rev = d == 1
    nch = TB // HC
    scale = HF ** -0.5

    def body(q_ref, f_ref, v_ref, lb_ref, o_ref, sp_ref, st):
        s = pl.program_id(0)

        @pl.when(s == 0)
        def _():
            st[...] = jnp.zeros_like(st)

        mb = _tri(HC, rev)
        m01 = _b01(mb)
        order = list(reversed(range(nch)) if rev else range(nch))
        hs_ = [slice(h * HF, (h + 1) * HF) for h in range(NH)]
        pre = {}
        for c in order:
            rows = slice(c * HC, (c + 1) * HC)
            _, _, f = _hgrn_gate(f_ref[rows, :].astype(F32), lb_ref, d)
            k = 1.0 - f
            cum = _dot01(m01, jnp.log(f))
            tot = cum[0:1, :] if rev else cum[HC - 1:HC, :]
            qd = _silu(q_ref[rows, :].astype(F32)) * scale * jnp.exp(cum)
            ki = k * jnp.exp(-cum)
            etot = jnp.exp(tot)
            pre[c] = (_mx(qd), _mx(ki), _mx(ki * etot), _mx(v_ref[rows, :]), etot)
        scs = {c: [_nt(pre[c][0][:, cs], pre[c][1][:, cs]) for cs in hs_] for c in order}
        upd = {c: [_tn(pre[c][3][:, cs], pre[c][2][:, cs]) for cs in hs_] for c in order}
        intra = {c: [_nn(jnp.where(mb, scs[c][h], 0.0), pre[c][3][:, cs]) for h, cs in enumerate(hs_)] for c in order}
        for c in order:
            rows = slice(c * HC, (c + 1) * HC)
            qdb, etot = pre[c][0], pre[c][4]
            for h, cs in enumerate(hs_):
                sth = st[h]
                stb = sth.astype(sp_ref.dtype)
                sp_ref[c, h] = stb
                o_ref[rows, cs] = (intra[c][h] + _nt(qdb[:, cs], stb)).astype(o_ref.dtype)
                st[h] = sth * etot[:, cs] + upd[c][h]

    col = lambda j: (lambda s: (_blk(s, nb, rev), j))
    call = dict(
        body=body, args=[p_main, p_main, p_main, lbraw], name=f"hgrn_fwd_{d}",
        out_shape=(jax.ShapeDtypeStruct((t_total, D), MXU_DTYPE),
                   jax.ShapeDtypeStruct((nch * nb, NH, HF, HF), MXU_DTYPE)),
        grid=(nb,),
        in_specs=[pl.BlockSpec((TB, D), col(0)), pl.BlockSpec((TB, D), col(1 + d)), pl.BlockSpec((TB, D), col(3)),
                  _full((8, D))],
        out_specs=(pl.BlockSpec((TB, D), col(0)),
                   pl.BlockSpec((nch, NH, HF, HF), lambda s: (_blk(s, nb, rev), 0, 0, 0))),
        scratch=[pltpu.VMEM((NH, HF, HF), F32)], sem=("arbitrary",), vmem_mb=40)
    return _run(_carry(call, comm, lambda: (pl.program_id(0) == 0, pl.program_id(0) == nb - 1)))


def _hgrn_bwd(p_main, lbraw, sprev, do, d, nb, prev, comm=None):
    t_total = p_main.shape[0]
    rev = d == 1
    nch = TB // HC
    scale = HF ** -0.5
    last = prev is not None
    odt = MXU_DTYPE if last else F32

    def body(q_ref, f_ref, v_ref, lb_ref, sp_ref, do_ref, *rest):
        if last:
            dqp_ref, dvp_ref = rest[:2]
            rest = rest[2:]
        dq_ref, df_ref, dv_ref, dlb_ref, dst = rest
        sp_id = pl.program_id(0)
        is_ctx = sp_id == nb - 1

        @pl.when(sp_id == 0)
        def _():
            dst[...] = jnp.zeros_like(dst)
            dlb_ref[...] = jnp.zeros_like(dlb_ref)

        mb = _tri(HC, rev)
        mbt = _tri(HC, not rev)
        m01 = _b01(mb)
        mt01 = _b01(mbt)
        order = list(range(nch) if rev else reversed(range(nch)))
        hs_ = [slice(h * HF, (h + 1) * HF) for h in range(NH)]
        pre = {}
        for c in order:
            rows = slice(c * HC, (c + 1) * HC)
            lb, sg, f = _hgrn_gate(f_ref[rows, :].astype(F32), lb_ref, d)
            k = 1.0 - f
            cum = _dot01(m01, jnp.log(f))
            tot = cum[0:1, :] if rev else cum[HC - 1:HC, :]
            e = jnp.exp(cum)
            ei = jnp.exp(-cum)
            etot = jnp.exp(tot)
            ee = ei * etot
            qraw = q_ref[rows, :].astype(F32)
            sq = _sig(qraw)
            qd = qraw * sq * scale * e
            ki = k * ei
            ke = k * ee
            dov = jnp.where(is_ctx, 0.0, do_ref[rows, :].astype(F32))
            pre[c] = dict(lb=lb, sg=sg, f=f, e=e, ei=ei, ee=ee, etot=etot, qd=qd, ki=ki, ke=ke,
                          dsq=sq * (1.0 + qraw * (1.0 - sq)),
                          qdb=_mx(qd), kib=_mx(ki), keb=_mx(ke), vb=_mx(v_ref[rows, :]), dob=_mx(dov))
        units = [(c, h) for c in order for h in range(NH)]
        col = lambda u, key: pre[u[0]][key][:, hs_[u[1]]]
        pt = {u: jnp.where(mbt, _nt(col(u, "kib"), col(u, "qdb")), 0.0) for u in units}
        dp = {u: jnp.where(mb, _nt(col(u, "dob"), col(u, "vb")), 0.0) for u in units}
        dpt = {u: jnp.where(mbt, _nt(col(u, "vb"), col(u, "dob")), 0.0) for u in units}
        dv_i = {u: _nn(pt[u], col(u, "dob")) for u in units}
        dqd_ = {u: _nn(dp[u], col(u, "kib")) + _nn(col(u, "dob"), sp_ref[u[0], u[1]]) for u in units}
        dki_ = {u: _nn(dpt[u], col(u, "qdb")) for u in units}
        dsl = {u: _tn(col(u, "dob"), col(u, "qdb")) for u in units}
        for c in order:
            rows = slice(c * HC, (c + 1) * HC)
            p = pre[c]
            dv_l, dke_l, dtot_l = [], [], []
            for h, cs in enumerate(hs_):
                dso = dst[h]
                dsob = _mx(dso)
                dv_l.append(dv_i[(c, h)] + _nt(p["keb"][:, cs], dsob))
                dke_l.append(_nn(p["vb"][:, cs], dsob))
                dtot_l.append(_colsum(dso * sp_ref[c, h].astype(F32)) * p["etot"][:, cs])
                dst[h] = dso * p["etot"][:, cs] + dsl[(c, h)]
            lb, sg, f, e, ei, ee, qd, ki, ke = (p[n_] for n_ in ("lb", "sg", "f", "e", "ei", "ee", "qd", "ki", "ke"))
            dqd = jnp.concatenate([dqd_[(c, h)] for h in range(NH)], axis=1)
            dki = jnp.concatenate([dki_[(c, h)] for h in range(NH)], axis=1)
            dke = jnp.concatenate(dke_l, axis=1)
            dcum = dqd * qd - dki * ki - dke * ke
            dtot = jnp.concatenate(dtot_l, axis=1) + _colsum(dke * ke)
            dk = dki * ei + dke * ee
            dlf = _dot01(mt01, dcum) + dtot
            df = dlf / f - dk
            dlb_ref[0:1, :] += _colsum(df * (1.0 - sg))
            dfr = df * (1.0 - lb) * sg * (1.0 - sg)
            dq = dqd * e * scale * p["dsq"]
            dv = jnp.concatenate(dv_l, axis=1)
            if last:
                dq = dq + dqp_ref[rows, :]
                dv = dv + dvp_ref[rows, :]
            dq_ref[rows, :] = dq.astype(odt)
            dv_ref[rows, :] = dv.astype(odt)
            df_ref[rows, :] = dfr.astype(MXU_DTYPE)

    blk = lambda s: _blk(nb - 1 - s, nb, rev)
    col = lambda j: (lambda s: (blk(s), j))
    in_specs = [pl.BlockSpec((TB, D), col(0)), pl.BlockSpec((TB, D), col(1 + d)), pl.BlockSpec((TB, D), col(3)),
                _full((8, D)), pl.BlockSpec((nch, NH, HF, HF), lambda s: (blk(s), 0, 0, 0)),
                pl.BlockSpec((TB, D), lambda s: (jnp.minimum(blk(s), nb - 2), 0))]
    args = [p_main, p_main, p_main, lbraw, sprev, do]
    if last:
        in_specs += [pl.BlockSpec((TB, D), col(0))] * 2
        args += list(prev)
    call = dict(
        body=body, args=args, name=f"hgrn_bwd_{d}",
        out_shape=(jax.ShapeDtypeStruct((t_total, D), odt), jax.ShapeDtypeStruct((t_total, D), MXU_DTYPE),
                   jax.ShapeDtypeStruct((t_total, D), odt), jax.ShapeDtypeStruct((8, D), F32)),
        grid=(nb,), in_specs=in_specs,
        out_specs=(pl.BlockSpec((TB, D), col(0)), pl.BlockSpec((TB, D), col(0)), pl.BlockSpec((TB, D), col(0)),
                   _full((8, D))),
        scratch=[pltpu.VMEM((NH, HF, HF), F32)], sem=("arbitrary",), vmem_mb=48)
    return _run(_carry(call, comm, lambda: (pl.program_id(0) == 0, pl.program_id(0) == nb - 1)))


def _conv_masks(tb, is_ctx):
    seg = jnp.where(is_ctx, tb, GRID_W)
    pos = lax.broadcasted_iota(jnp.int32, (tb, 1), 0) & (seg - 1)
    return pos, seg


def _shift_rows(x, dshift, pos, seg):
    if dshift == 0:
        return x
    n = x.shape[0]
    rolled = pltpu.roll(x, (-dshift) % n, 0)
    ok = (pos + dshift >= 0) & (pos + dshift < seg)
    return jnp.where(ok, rolled, 0.0)


def _ssd_prep(p_main, p_dt, convp, dtb, nb):
    t_total = p_main.shape[0]

    def body(x_ref, dt_ref, cw_ref, dtb_ref, xa_ref, ds_ref, dts_ref):
        is_ctx = pl.program_id(0) == nb - 1
        pos, seg = _conv_masks(TB, is_ctx)
        xv = x_ref[...].astype(F32)
        acc = cw_ref[5:6, :] + cw_ref[2:3, :] * xv
        for kk in (0, 1, 3, 4):
            acc = acc + cw_ref[kk:kk + 1, :] * _shift_rows(xv, kk - 2, pos, seg)
        sg = _sig(acc)
        xa_ref[...] = (acc * sg).astype(xa_ref.dtype)
        ds_ref[...] = (sg * (1.0 + acc * (1.0 - sg))).astype(ds_ref.dtype)
        dts_ref[...] = _softplus(dt_ref[...] + dtb_ref[0:1, :])

    wide = pl.BlockSpec((TB, 2048), lambda i: (i, 0))
    return _pcall(
        body, name="ssd_prep",
        out_shape=(jax.ShapeDtypeStruct((t_total, 2048), MXU_DTYPE), jax.ShapeDtypeStruct((t_total, 2048), MXU_DTYPE),
                   jax.ShapeDtypeStruct((t_total, 128), F32)),
        grid=(nb,),
        in_specs=[pl.BlockSpec((TB, 2048), lambda i: (i, 3)), pl.BlockSpec((TB, 128), lambda i: (i, 0)),
                  _full((8, 2048)), _full((8, 128))],
        out_specs=(wide, wide, pl.BlockSpec((TB, 128), lambda i: (i, 0))),
        sem=("parallel",), vmem_mb=32,
    )(p_main, p_dt, convp, dtb)


def _ssd_prep_bwd(p_main, p_dt, convp, dtb, dsl, dxa, dxs_skip, ddts, nb):
    t_total = p_main.shape[0]

    def body(x_ref, dt_ref, cw_ref, dtb_ref, ds_ref, dxa_ref, dsk_ref, ddts_ref, dx_ref, ddt_ref, dcw_ref, ddtb_ref):
        i = pl.program_id(0)
        is_ctx = i == nb - 1

        @pl.when(i == 0)
        def _():
            dcw_ref[...] = jnp.zeros_like(dcw_ref)
            ddtb_ref[...] = jnp.zeros_like(ddtb_ref)

        pos, seg = _conv_masks(TB, is_ctx)
        xv = x_ref[...].astype(F32)
        dact = dxa_ref[...]
        dact = jnp.concatenate([dact[:, :D] + jnp.where(is_ctx, 0.0, dsk_ref[...].astype(F32)), dact[:, D:]], axis=1)
        dpre = dact * ds_ref[...].astype(F32)
        dxv = cw_ref[2:3, :] * dpre
        dcw_ref[2:3, :] += _colsum(xv * dpre)
        for kk in (0, 1, 3, 4):
            sdp = _shift_rows(dpre, 2 - kk, pos, seg)
            dxv = dxv + cw_ref[kk:kk + 1, :] * sdp
            dcw_ref[kk:kk + 1, :] += _colsum(xv * sdp)
        dx_ref[...] = dxv.astype(dx_ref.dtype)
        dcw_ref[5:6, :] += _colsum(dpre)
        draw = ddts_ref[...] * _sig(dt_ref[...] + dtb_ref[0:1, :])
        ddt_ref[...] = draw.astype(ddt_ref.dtype)
        ddtb_ref[0:1, :] += _colsum(draw)

    return _pcall(
        body, name="ssd_prep_bwd",
        out_shape=(jax.ShapeDtypeStruct((t_total, 2048), MXU_DTYPE), jax.ShapeDtypeStruct((t_total, 128), MXU_DTYPE),
                   jax.ShapeDtypeStruct((8, 2048), F32), jax.ShapeDtypeStruct((8, 128), F32)),
        grid=(nb,),
        in_specs=[pl.BlockSpec((TB, 2048), lambda i: (i, 3)), pl.BlockSpec((TB, 128), lambda i: (i, 0)),
                  _full((8, 2048)), _full((8, 128)), pl.BlockSpec((TB, 2048), lambda i: (i, 0)),
                  pl.BlockSpec((TB, 2048), lambda i: (i, 0)),
                  pl.BlockSpec((TB, D), lambda i: (jnp.minimum(i, nb - 2), 0)),
                  pl.BlockSpec((TB, 128), lambda i: (i, 0))],
        out_specs=(pl.BlockSpec((TB, 2048), lambda i: (i, 0)), pl.BlockSpec((TB, 128), lambda i: (i, 0)),
                   _full((8, 2048)), _full((8, 128))),
        sem=("arbitrary",), vmem_mb=40,
    )(p_main, p_dt, convp, dtb, dsl, dxa, dxs_skip, ddts)


def _dot2(x, m01):
    hi = x.astype(BF16)
    lo = (x - hi.astype(F32)).astype(BF16)
    f = lambda t: lax.dot_general(t, m01, (((1,), (0,)), ((), ())), preferred_element_type=F32)
    return f(hi) + f(lo)


def _head_lanes(c0, c1):
    p = lax.broadcasted_iota(jnp.int32, (128, 128), 0)
    l = lax.broadcasted_iota(jnp.int32, (128, 128), 1)
    return _b01(((l == c0) & (p < SP)) | ((l == c1) & (p >= SP)))


def _one_lane(col):
    return _b01(lax.broadcasted_iota(jnp.int32, (128, 128), 1) == col)


def _lane_pick(x, lane, col):
    return _rowsum(jnp.where(lane == col, x, 0.0))


def _ssd_chunk_common(dts, alog_ref, m01, rev):
    lane = lax.broadcasted_iota(jnp.int32, (1, 128), 1)
    arow = -jnp.exp(alog_ref[0:1, :])
    cum = _dot01(m01, dts * arow)
    tot = cum[0:1, :] if rev else cum[SC - 1:SC, :]
    return lane, arow, cum, cum.T, tot


def _ssd_fwd(xa, dts, alog, d, nb):
    t_total = xa.shape[0]
    rev = d == 1
    nch = TB // SC
    npair = SHEADS // 2

    def body(xa_ref, dts_ref, alog_ref, y_ref, sp_ref, st):
        s = pl.program_id(0)

        @pl.when(s == 0)
        def _():
            st[...] = jnp.zeros_like(st)

        mb = _tri(SC, rev)
        m01 = _b01(mb)
        lo = lax.broadcasted_iota(jnp.int32, (1, 128), 1) < SP
        rlo = lax.broadcasted_iota(jnp.int32, (128, 1), 0) < SP
        order = list(reversed(range(nch)) if rev else range(nch))
        pre = {}
        for c in order:
            rows = slice(c * SC, (c + 1) * SC)
            dts_c = dts_ref[rows, :]
            lane, arow, cum, cumt, tot = _ssd_chunk_common(dts_c, alog_ref, m01, rev)
            bgs = [_mx(xa_ref[rows, D + g * SN:D + (g + 1) * SN]) for g in range(4)]
            cgs = [_mx(xa_ref[rows, D + 512 + g * SN:D + 512 + (g + 1) * SN]) for g in range(4)]
            pairs = []
            for pr in range(npair):
                xs = xa_ref[rows, pr * 128:(pr + 1) * 128].astype(F32)
                cols = [16 * d + 2 * pr, 16 * d + 2 * pr + 1]
                cum_c = [_lane_pick(cum, lane, q) for q in cols]
                dt_c = [_lane_pick(dts_c, lane, q) for q in cols]
                tot_c = [_lane_pick(tot, lane, q) for q in cols]
                dtx = xs * jnp.where(lo, dt_c[0], dt_c[1])
                e1_pair = jnp.where(lo, jnp.exp(cum_c[0]), jnp.exp(cum_c[1]))
                e2_pair = jnp.where(lo, jnp.exp(tot_c[0] - cum_c[0]), jnp.exp(tot_c[1] - cum_c[1]))
                etot_col = jnp.where(rlo, jnp.exp(tot_c[0]), jnp.exp(tot_c[1]))
                decs = [jnp.where(mb, jnp.exp(cum_c[q] - cumt[cols[q]:cols[q] + 1, :]), 0.0) for q in range(2)]
                dtxq = [_mx(jnp.where(lo if q == 0 else ~lo, dtx, 0.0)) for q in range(2)]
                pairs.append(dict(e1=e1_pair, etot=etot_col, decs=decs, dtxq=dtxq, xe=_mx(dtx * e2_pair)))
            pre[c] = (bgs, cgs, pairs)
        gm = {(c, g): _nt(pre[c][1][g], pre[c][0][g]) for c in order for g in range(4)}
        upd = {(c, pr): _tn(pre[c][2][pr]["xe"], pre[c][0][pr // 2]) for c in order for pr in range(npair)}
        intra = {(c, pr): sum(_nn(gm[(c, pr // 2)] * pre[c][2][pr]["decs"][q], pre[c][2][pr]["dtxq"][q]) for q in range(2))
                 for c in order for pr in range(npair)}
        for c in order:
            rows = slice(c * SC, (c + 1) * SC)
            bgs, cgs, pairs = pre[c]
            for pr in range(npair):
                stp = st[pr]
                stb = stp.astype(sp_ref.dtype)
                sp_ref[c, pr] = stb
                y_ref[rows, pr * 128:(pr + 1) * 128] = (
                    intra[(c, pr)] + pairs[pr]["e1"] * _nt(cgs[pr // 2], stb)).astype(y_ref.dtype)
                st[pr] = stp * pairs[pr]["etot"] + upd[(c, pr)]

    blk = lambda s: _blk(s, nb, rev)
    return _pcall(
        body, name=f"ssd_fwd_{d}",
        out_shape=(jax.ShapeDtypeStruct((t_total, D), MXU_DTYPE),
                   jax.ShapeDtypeStruct((nch * nb, npair, 128, SN), MXU_DTYPE)),
        grid=(nb,),
        in_specs=[pl.BlockSpec((TB, 2048), lambda s: (blk(s), 0)), pl.BlockSpec((TB, 128), lambda s: (blk(s), 0)),
                  _full((8, 128))],
        out_specs=(pl.BlockSpec((TB, D), lambda s: (blk(s), 0)),
                   pl.BlockSpec((nch, npair, 128, SN), lambda s: (blk(s), 0, 0, 0))),
        scratch=[pltpu.VMEM((npair, 128, SN), F32)], sem=("arbitrary",), vmem_mb=40,
    )(xa, dts, alog)


def _ssd_bwd(xa, dts, alog, sprev, dy, d, nb, prev, comm=None):
    t_total = xa.shape[0]
    rev = d == 1
    nch = TB // SC
    npair = SHEADS // 2
    last = prev is not None

    def body(xa_ref, dts_ref, alog_ref, sp_ref, dy_ref, *rest):
        if last:
            dxp_ref, ddp_ref = rest[:2]
            rest = rest[2:]
        dxa_ref, ddts_ref, da_ref, dst, zc_scr = rest
        sp_id = pl.program_id(0)
        is_ctx = sp_id == nb - 1

        @pl.when(sp_id == 0)
        def _():
            dst[...] = jnp.zeros_like(dst)
            da_ref[...] = jnp.zeros_like(da_ref)
            zc_scr[...] = jnp.zeros_like(zc_scr)

        mb = _tri(SC, rev)
        m01 = _b01(mb)
        mt01 = _b01(_tri(SC, not rev))
        lo = lax.broadcasted_iota(jnp.int32, (1, 128), 1) < SP
        rlo = lax.broadcasted_iota(jnp.int32, (128, 1), 0) < SP
        order = list(range(nch) if rev else reversed(range(nch)))
        pre = {}
        for c in order:
            rows = slice(c * SC, (c + 1) * SC)
            dts_c = dts_ref[rows, :]
            lane, arow, cum, cumt, tot = _ssd_chunk_common(dts_c, alog_ref, m01, rev)
            pairs = []
            for pr in range(npair):
                xs = xa_ref[rows, pr * 128:(pr + 1) * 128].astype(F32)
                dyp = jnp.where(is_ctx, 0.0, dy_ref[rows, pr * 128:(pr + 1) * 128].astype(F32))
                cols = [16 * d + 2 * pr, 16 * d + 2 * pr + 1]
                cum_c = [_lane_pick(cum, lane, q) for q in cols]
                dt_c = [_lane_pick(dts_c, lane, q) for q in cols]
                tot_c = [_lane_pick(tot, lane, q) for q in cols]
                e1_c = [jnp.exp(cum_c[q]) for q in range(2)]
                e2_c = [jnp.exp(tot_c[q] - cum_c[q]) for q in range(2)]
                etot_c = [jnp.exp(tot_c[q]) for q in range(2)]
                dt_pair = jnp.where(lo, dt_c[0], dt_c[1])
                e1_pair = jnp.where(lo, e1_c[0], e1_c[1])
                e2_pair = jnp.where(lo, e2_c[0], e2_c[1])
                dtx = xs * dt_pair
                decs = [jnp.where(mb, jnp.exp(cum_c[q] - cumt[cols[q]:cols[q] + 1, :]), 0.0) for q in range(2)]
                dyq = [_mx(jnp.where(lo if q == 0 else ~lo, dyp, 0.0)) for q in range(2)]
                pairs.append(dict(xs=xs, dyp=dyp, cols=cols, e1_c=e1_c, e2_c=e2_c, etot_c=etot_c, dt_pair=dt_pair,
                                  e2_pair=e2_pair, etot_col=jnp.where(rlo, etot_c[0], etot_c[1]), dtx=dtx,
                                  dtxb=_mx(dtx), xeb=_mx(dtx * e2_pair), dy0b=_mx(dyp * e1_pair), decs=decs, dyq=dyq))
            pre[c] = dict(lane=lane, arow=arow, dts=dts_c, pairs=pairs, cum=cum, tot=tot,
                          bgb=[_mx(xa_ref[rows, D + g * SN:D + (g + 1) * SN]) for g in range(4)],
                          cgb=[_mx(xa_ref[rows, D + 512 + g * SN:D + 512 + (g + 1) * SN]) for g in range(4)])
        units = [(c, pr) for c in order for pr in range(npair)]
        P = lambda u: pre[u[0]]["pairs"][u[1]]
        cgu = lambda u: pre[u[0]]["cgb"][u[1] // 2]
        gm = {(c, g): _nt(pre[c]["cgb"][g], pre[c]["bgb"][g]) for c in order for g in range(4)}
        y0 = {u: _nt(cgu(u), sp_ref[u[0], u[1]]) for u in units}
        dcg_i = {u: _nn(P(u)["dy0b"], sp_ref[u[0], u[1]]) for u in units}
        dsl = {u: _tn(P(u)["dy0b"], cgu(u)) for u in units}
        w_ = {(u, q): gm[(u[0], u[1] // 2)] * P(u)["decs"][q] for u in units for q in range(2)}
        dw_ = {(u, q): jnp.where(mb, _nt(P(u)["dyq"][q], P(u)["dtxb"]), 0.0) for u in units for q in range(2)}
        ddtx_i = {(u, q): _tn(w_[(u, q)], P(u)["dyq"][q]) for u in units for q in range(2)}
        for c in order:
            rows = slice(c * SC, (c + 1) * SC)
            pc = pre[c]
            lane, arow, dts_c = pc["lane"], pc["arow"], pc["dts"]
            d1 = jnp.zeros((SC, 128), F32)
            d2 = jnp.zeros((SC, 128), F32)
            dz = jnp.zeros((SC, 128), F32)
            ddt = jnp.zeros((SC, 128), F32)
            dtot = jnp.zeros((1, 128), F32)
            dgm = [jnp.zeros((SC, SC), F32) for _ in range(4)]
            dbg = [jnp.zeros((SC, SN), F32) for _ in range(4)]
            dcg = [jnp.zeros((SC, SN), F32) for _ in range(4)]
            for pr in range(npair):
                u, g, p = (c, pr), pr // 2, pc["pairs"][pr]
                hs = _head_lanes(*p["cols"])
                dso = dst[pr]
                dsob = _mx(dso)
                dxe = _nt(pc["bgb"][g], dsob)
                dbg[g] = dbg[g] + _nn(p["xeb"], dsob)
                ddtx = dxe * p["e2_pair"]
                d2 = d2 + _dot2(dxe * p["dtx"], hs)
                dcg[g] = dcg[g] + dcg_i[u]
                d1 = d1 + _dot2(p["dyp"] * y0[u], hs)
                sprod = dso * sp_ref[c, pr].astype(F32)
                dst[pr] = dso * p["etot_col"] + dsl[u]
                for q in range(2):
                    hm = lo if q == 0 else ~lo
                    col = p["cols"][q]
                    dw = dw_[(u, q)]
                    ddtx = ddtx + jnp.where(hm, ddtx_i[(u, q)], 0.0)
                    dgm[g] = dgm[g] + dw * p["decs"][q]
                    z = dw * w_[(u, q)]
                    dz = dz + _dot2(z, _one_lane(col))
                    zc_scr[col:col + 1, :] = _colsum(z)
                    tsum = _rowsum(_colsum(sprod[q * SP:(q + 1) * SP, :]))
                    dtot = jnp.where(lane == col, tsum * p["etot_c"][q], dtot)
                dxs = ddtx * p["dt_pair"]
                ddt = ddt + _dot2(ddtx * p["xs"], hs)
                if last:
                    dxs = dxs + dxp_ref[rows, pr * 128:(pr + 1) * 128]
                dxa_ref[rows, pr * 128:(pr + 1) * 128] = dxs
            e2_all = jnp.exp(pc["tot"] - pc["cum"])
            dcum = dz - zc_scr[...].T + d1 * jnp.exp(pc["cum"]) - d2 * e2_all
            dtot = dtot + _colsum(d2 * e2_all)
            for g in range(4):
                db = dbg[g] + _tn(dgm[g], pc["cgb"][g])
                dc = dcg[g] + _nn(dgm[g], pc["bgb"][g])
                if last:
                    db = db + dxp_ref[rows, D + g * SN:D + (g + 1) * SN]
                    dc = dc + dxp_ref[rows, D + 512 + g * SN:D + 512 + (g + 1) * SN]
                dxa_ref[rows, D + g * SN:D + (g + 1) * SN] = db
                dxa_ref[rows, D + 512 + g * SN:D + 512 + (g + 1) * SN] = dc
            dla = _dot01(mt01, dcum) + dtot
            ddt = ddt + dla * arow
            da_ref[0:1, :] += _colsum(dla * dts_c)
            if last:
                ddt = ddt + ddp_ref[rows, :]
            ddts_ref[rows, :] = ddt

    blk = lambda s: _blk(nb - 1 - s, nb, rev)
    in_specs = [pl.BlockSpec((TB, 2048), lambda s: (blk(s), 0)), pl.BlockSpec((TB, 128), lambda s: (blk(s), 0)),
                _full((8, 128)), pl.BlockSpec((nch, npair, 128, SN), lambda s: (blk(s), 0, 0, 0)),
                pl.BlockSpec((TB, D), lambda s: (jnp.minimum(blk(s), nb - 2), 0))]
    args = [xa, dts, alog, sprev, dy]
    if last:
        in_specs += [pl.BlockSpec((TB, 2048), lambda s: (blk(s), 0)), pl.BlockSpec((TB, 128), lambda s: (blk(s), 0))]
        args += list(prev)
    call = dict(
        body=body, args=args, name=f"ssd_bwd_{d}",
        out_shape=(jax.ShapeDtypeStruct((t_total, 2048), F32), jax.ShapeDtypeStruct((t_total, 128), F32),
                   jax.ShapeDtypeStruct((8, 128), F32)),
        grid=(nb,), in_specs=in_specs,
        out_specs=(pl.BlockSpec((TB, 2048), lambda s: (blk(s), 0)), pl.BlockSpec((TB, 128), lambda s: (blk(s), 0)),
                   _full((8, 128))),
        scratch=[pltpu.VMEM((npair, 128, SN), F32), pltpu.VMEM((128, 128), F32)], sem=("arbitrary",), vmem_mb=48)
    return _run(_carry(call, comm, lambda: (pl.program_id(0) == 0, pl.program_id(0) == nb - 1)))


def _readout(o, g, yy, z, vec_ref):
    hg, ss, keep = [], [], []
    for h in range(NH):
        cs = slice(h * HF, (h + 1) * HF)
        oh = o[:, cs]
        r = lax.rsqrt(jnp.mean(oh * oh, axis=1, keepdims=True) + EPS)
        hg.append(oh * r * vec_ref[0:1, cs] * _silu(g[:, cs]))
        keep.append(r)
    u = yy * _silu(z)
    for gi in range(4):
        cs = slice(gi * 256, (gi + 1) * 256)
        ug = u[:, cs]
        r = lax.rsqrt(jnp.mean(ug * ug, axis=1, keepdims=True) + EPS)
        ss.append(ug * r * vec_ref[2:3, cs])
        keep.append(r)
    return jnp.concatenate(hg, axis=1), jnp.concatenate(ss, axis=1), keep, u


def _mix_out(o_f, o_b, p_main, y_f, y_b, xa, x, vecs, w_out):
    n = x.shape[0]

    def body(of_ref, ob_ref, g_ref, z_ref, yf_ref, yb_ref, xs_ref, x_ref, vec_ref, w_ref,
             ymix_ref, ylat_ref, h1_ref, u2_ref):
        o = of_ref[...].astype(F32) + ob_ref[...].astype(F32)
        yy = yf_ref[...].astype(F32) + yb_ref[...].astype(F32) + vec_ref[1:2, :] * xs_ref[...].astype(F32)
        hg, ss, _, _ = _readout(o, g_ref[...].astype(F32), yy, z_ref[...].astype(F32), vec_ref)
        ymix = jnp.concatenate([hg, ss], axis=1).astype(MXU_DTYPE)
        ymix_ref[...] = ymix
        ylat = _nn(ymix, w_ref[...])
        ylat_ref[...] = ylat
        h1 = x_ref[...] + vec_ref[3:4, :] * ylat
        h1_ref[...] = h1
        r = lax.rsqrt(jnp.mean(h1 * h1, axis=1, keepdims=True) + EPS)
        u2_ref[...] = ((h1 * r * vec_ref[6:7, :]) * vec_ref[4:5, :] + vec_ref[5:6, :]).astype(MXU_DTYPE)

    row = lambda j: (lambda i: (i, j))
    return _pcall(
        body, name="mix_out",
        out_shape=(jax.ShapeDtypeStruct((n, 2 * D), MXU_DTYPE), jax.ShapeDtypeStruct((n, D), F32),
                   jax.ShapeDtypeStruct((n, D), F32), jax.ShapeDtypeStruct((n, D), MXU_DTYPE)),
        grid=(n // TB,),
        in_specs=[pl.BlockSpec((TB, D), row(0)), pl.BlockSpec((TB, D), row(0)), pl.BlockSpec((TB, D), row(4)),
                  pl.BlockSpec((TB, D), row(5)), pl.BlockSpec((TB, D), row(0)), pl.BlockSpec((TB, D), row(0)),
                  pl.BlockSpec((TB, D), row(0)), pl.BlockSpec((TB, D), row(0)), _full((8, D)), _full((2 * D, D))],
        out_specs=(pl.BlockSpec((TB, 2 * D), row(0)), pl.BlockSpec((TB, D), row(0)), pl.BlockSpec((TB, D), row(0)),
                   pl.BlockSpec((TB, D), row(0))),
        sem=("parallel",), vmem_mb=48,
    )(o_f, o_b, p_main, p_main, y_f, y_b, xa, x, vecs, w_out)


def _mix_bwd(dylat, o_f, o_b, p_main, y_f, y_b, xa, vecs, w_out, comm=None):
    n = dylat.shape[0]
    t_total = p_main.shape[0]
    nlat = n // TB

    def body(*refs):
        dg_ref, dz_ref, acc_ref = refs[11], refs[13], refs[15]
        i = pl.program_id(0)

        @pl.when(i == 0)
        def _():
            acc_ref[...] = jnp.zeros_like(acc_ref)

        @pl.when(i < nlat)
        def _():
            compute(*refs)

        @pl.when(i == nlat)
        def _():
            dg_ref[...] = jnp.zeros_like(dg_ref)
            dz_ref[...] = jnp.zeros_like(dz_ref)

    def compute(dyl_ref, of_ref, ob_ref, g_ref, z_ref, yf_ref, yb_ref, xs_ref, vec_ref, w_ref,
                do_ref, dg_ref, dys_ref, dz_ref, dxs_ref, acc_ref):
        dymix = _nt(dyl_ref[...], w_ref[...])
        o = of_ref[...].astype(F32) + ob_ref[...].astype(F32)
        g = g_ref[...].astype(F32)
        z = z_ref[...].astype(F32)
        xs = xs_ref[...].astype(F32)
        yy = yf_ref[...].astype(F32) + yb_ref[...].astype(F32) + vec_ref[1:2, :] * xs
        _, _, keep, u = _readout(o, g, yy, z, vec_ref)
        do_l, dg_l = [], []
        for h in range(NH):
            cs = slice(h * HF, (h + 1) * HF)
            oh, gh, r, wv = o[:, cs], g[:, cs], keep[h], vec_ref[0:1, cs]
            dhg = dymix[:, cs]
            xh = oh * r
            dn = dhg * _silu(gh)
            dg_l.append(dhg * xh * wv * _dsilu(gh))
            acc_ref[0:1, cs] += _colsum(dn * xh)
            dxh = dn * wv
            do_l.append(r * (dxh - xh * jnp.mean(dxh * xh, axis=1, keepdims=True)))
        du_l = []
        for gi in range(4):
            cs = slice(gi * 256, (gi + 1) * 256)
            ug, r, wv = u[:, cs], keep[NH + gi], vec_ref[2:3, cs]
            dss = dymix[:, D + gi * 256:D + (gi + 1) * 256]
            xh = ug * r
            acc_ref[2:3, cs] += _colsum(dss * xh)
            dxh = dss * wv
            du_l.append(r * (dxh - xh * jnp.mean(dxh * xh, axis=1, keepdims=True)))
        du = jnp.concatenate(du_l, axis=1)
        dyy = du * _silu(z)
        do_ref[...] = jnp.concatenate(do_l, axis=1).astype(do_ref.dtype)
        dg_ref[...] = jnp.concatenate(dg_l, axis=1).astype(dg_ref.dtype)
        dys_ref[...] = dyy.astype(dys_ref.dtype)
        dz_ref[...] = (du * yy * _dsilu(z)).astype(dz_ref.dtype)
        dxs_ref[...] = (dyy * vec_ref[1:2, :]).astype(dxs_ref.dtype)
        acc_ref[1:2, :] += _colsum(dyy * xs)

    row = lambda j: (lambda i: (jnp.minimum(i, nlat - 1), j))
    lat = pl.BlockSpec((TB, D), row(0))
    tok = pl.BlockSpec((TB, D), lambda i: (i, 0))
    call = dict(
        body=body, args=[dylat, o_f, o_b, p_main, p_main, y_f, y_b, xa, vecs, w_out], name="mix_bwd",
        out_shape=(jax.ShapeDtypeStruct((n, D), MXU_DTYPE), jax.ShapeDtypeStruct((t_total, D), MXU_DTYPE),
                   jax.ShapeDtypeStruct((n, D), MXU_DTYPE), jax.ShapeDtypeStruct((t_total, D), MXU_DTYPE),
                   jax.ShapeDtypeStruct((n, D), MXU_DTYPE), jax.ShapeDtypeStruct((8, D), F32)),
        grid=(t_total // TB,),
        in_specs=[lat, lat, lat, pl.BlockSpec((TB, D), row(4)), pl.BlockSpec((TB, D), row(5)), lat, lat, lat,
                  _full((8, D)), _full((2 * D, D))],
        out_specs=(lat, tok, lat, tok, lat, _full((8, D))), scratch=[],
        sem=("arbitrary",), vmem_mb=48)
    return _run(_carry(call, comm, lambda: (pl.program_id(0) == 0, pl.program_id(0) == t_total // TB - 1)))


def _ffn_up(u2, w_gate, w_up):
    n = u2.shape[0]
    tb = 1024

    def body(u_ref, wg_ref, wu_ref, g_ref, up_ref, a_ref):
        uv = u_ref[...]
        gt = _nt(uv, wg_ref[...])
        upv = _nt(uv, wu_ref[...])
        g_ref[...] = gt.astype(g_ref.dtype)
        up_ref[...] = upv.astype(up_ref.dtype)
        a_ref[...] = (_silu(gt) * upv).astype(a_ref.dtype)

    blk = pl.BlockSpec((tb, FSL), lambda j, i: (i, j))
    wblk = pl.BlockSpec((None, FSL, D), lambda j, i: (j, 0, 0))
    return _pcall(
        body, name="ffn_up",
        out_shape=(jax.ShapeDtypeStruct((n, DFFP), MXU_DTYPE),) * 3,
        grid=(4, n // tb), in_specs=[pl.BlockSpec((tb, D), lambda j, i: (i, 0)), wblk, wblk],
        out_specs=(blk, blk, blk), sem=("parallel", "parallel"), vmem_mb=48,
    )(u2, w_gate, w_up)


def _ffn_down_loss(act, w_down, h1, tgt, vecs):
    n = act.shape[0]
    tb = 512

    def body(a_ref, w_ref, h1_ref, t_ref, vec_ref, dh2_ref, dffn_ref, acc_ref):
        i = pl.program_id(0)

        @pl.when(i == 0)
        def _():
            acc_ref[...] = jnp.zeros_like(acc_ref)

        g2 = vec_ref[0:1, :]
        fw = vec_ref[1:2, :]
        nsub = 4
        sb = tb // nsub
        wv = w_ref[...]
        ffns = [_nn(a_ref[r_ * sb:(r_ + 1) * sb, :], wv) for r_ in range(nsub)]
        for r_ in range(nsub):
            rows = slice(r_ * sb, (r_ + 1) * sb)
            ffn = ffns[r_]
            h2 = h1_ref[rows, :] + g2 * ffn
            r = lax.rsqrt(jnp.mean(h2 * h2, axis=1, keepdims=True) + EPS)
            xh = h2 * r
            err = xh * fw - t_ref[rows, :]
            dy = err * (1.0 / D)
            acc_ref[2:3, :] += _colsum(err * err) * (0.5 / D)
            acc_ref[1:2, :] += _colsum(dy * xh)
            dxh = dy * fw
            dh2 = r * (dxh - xh * jnp.mean(dxh * xh, axis=1, keepdims=True))
            dh2_ref[rows, :] = dh2
            dffn_ref[rows, :] = (g2 * dh2).astype(dffn_ref.dtype)
            acc_ref[0:1, :] += _colsum(dh2 * ffn)

    return _pcall(
        body, name="ffn_down_loss",
        out_shape=(jax.ShapeDtypeStruct((n, D), F32), jax.ShapeDtypeStruct((n, D), MXU_DTYPE),
                   jax.ShapeDtypeStruct((8, D), F32)),
        grid=(n // tb,),
        in_specs=[pl.BlockSpec((tb, DFFP), lambda i: (i, 0)), _full((DFFP, D)), pl.BlockSpec((tb, D), lambda i: (i, 0)),
                  pl.BlockSpec((tb, D), lambda i: (i, 0)), _full((8, D))],
        out_specs=(pl.BlockSpec((tb, D), lambda i: (i, 0)), pl.BlockSpec((tb, D), lambda i: (i, 0)), _full((8, D))),
        sem=("arbitrary",), vmem_mb=48,
    )(act, w_down, h1, tgt, vecs)


def _ffn_bwd(dffn, w_down, gate, up, w_gate_t, w_up_t):
    n = dffn.shape[0]
    tb = 1024

    def body(df_ref, wd_ref, g_ref, up_ref, wg_ref, wu_ref, dg_ref, dup_ref, du_ref):
        j = pl.program_id(1)
        nsub = 4
        sb = tb // nsub
        wd, wg, wu = wd_ref[...], wg_ref[...], wu_ref[...]
        dacts = [_nt(df_ref[r * sb:(r + 1) * sb, :], wd) for r in range(nsub)]
        parts = []
        for r in range(nsub):
            rows = slice(r * sb, (r + 1) * sb)
            gt = g_ref[rows, :].astype(F32)
            upv = up_ref[rows, :].astype(F32)
            sg = _sig(gt)
            dgt = (dacts[r] * upv * (sg * (1.0 + gt * (1.0 - sg)))).astype(MXU_DTYPE)
            dupv = (dacts[r] * (gt * sg)).astype(MXU_DTYPE)
            dg_ref[rows, :] = dgt
            dup_ref[rows, :] = dupv
            parts.append(_nn(dgt, wg) + _nn(dupv, wu))
        part = jnp.concatenate(parts, axis=0)

        @pl.when(j == 0)
        def _():
            du_ref[...] = part

        @pl.when(j > 0)
        def _():
            du_ref[...] += part

    tok = pl.BlockSpec((tb, D), lambda i, j: (i, 0))
    ffb = pl.BlockSpec((tb, FSL), lambda i, j: (i, j))
    wsl = pl.BlockSpec((None, FSL, D), lambda i, j: (j, 0, 0))
    return _pcall(
        body, name="ffn_bwd",
        out_shape=(jax.ShapeDtypeStruct((n, DFFP), MXU_DTYPE), jax.ShapeDtypeStruct((n, DFFP), MXU_DTYPE),
                   jax.ShapeDtypeStruct((n, D), F32)),
        grid=(n // tb, 4),
        in_specs=[tok, pl.BlockSpec((FSL, D), lambda i, j: (j, 0)), ffb, ffb, wsl, wsl],
        out_specs=(ffb, ffb, tok), sem=("parallel", "arbitrary"), vmem_mb=48,
    )(dffn, w_down, gate, up, w_gate_t, w_up_t)


def _ffn_norm_bwd(du, h1, ylat, dh2, vecs):
    n = du.shape[0]
    tb = 512

    def body(du_ref, h1_ref, yl_ref, dh2_ref, vec_ref, dh1_ref, dyl_ref, acc_ref):
        @pl.when(pl.program_id(0) == 0)
        def _():
            acc_ref[...] = jnp.zeros_like(acc_ref)

        duv = du_ref[...]
        h1 = h1_ref[...]
        r = lax.rsqrt(jnp.mean(h1 * h1, axis=1, keepdims=True) + EPS)
        xh = h1 * r
        nw = vec_ref[2:3, :]
        acc_ref[0:1, :] += _colsum(duv)
        acc_ref[1:2, :] += _colsum(duv * xh * nw)
        dn = duv * vec_ref[1:2, :]
        acc_ref[2:3, :] += _colsum(dn * xh)
        dxh = dn * nw
        dh1 = dh2_ref[...] + r * (dxh - xh * jnp.mean(dxh * xh, axis=1, keepdims=True))
        dh1_ref[...] = dh1
        dyl_ref[...] = (vec_ref[0:1, :] * dh1).astype(dyl_ref.dtype)
        acc_ref[3:4, :] += _colsum(dh1 * yl_ref[...])

    tok = pl.BlockSpec((tb, D), lambda i: (i, 0))
    return _pcall(
        body, name="ffn_norm_bwd",
        out_shape=(jax.ShapeDtypeStruct((n, D), F32), jax.ShapeDtypeStruct((n, D), MXU_DTYPE),
                   jax.ShapeDtypeStruct((8, D), F32)),
        grid=(n // tb,), in_specs=[tok, tok, tok, tok, _full((8, D))], out_specs=(tok, tok, _full((8, D))),
        sem=("arbitrary",), vmem_mb=40,
    )(du, h1, ylat, dh2, vecs)


def _deep_rows(rows):
    return max(r for r in range(128, 2305, 128) if rows % r == 0)


def _dw(a, b, name):
    tn_rows = a.shape[0]
    bt = _deep_rows(tn_rows)
    kk, nn_ = a.shape[1], b.shape[1]
    bk = 1024 if kk % 1024 == 0 else kk
    bn = 1024 if nn_ % 1024 == 0 else nn_
    nt = tn_rows // bt

    def body(a_ref, b_ref, o_ref, acc):
        t = pl.program_id(2)
        part = _tn(a_ref[...], b_ref[...])

        @pl.when(t == 0)
        def _():
            acc[...] = part

        @pl.when(t > 0)
        def _():
            acc[...] += part

        @pl.when(t == nt - 1)
        def _():
            o_ref[...] = acc[...].astype(o_ref.dtype)

    return _pcall(
        body, name=name, out_shape=jax.ShapeDtypeStruct((kk, nn_), MXU_DTYPE), grid=(kk // bk, nn_ // bn, nt),
        in_specs=[pl.BlockSpec((bt, bk), lambda i, j, t: (t, i)), pl.BlockSpec((bt, bn), lambda i, j, t: (t, j))],
        out_specs=pl.BlockSpec((bk, bn), lambda i, j, t: (i, j)), scratch=[pltpu.VMEM((bk, bn), F32)],
        sem=("parallel", "parallel", "arbitrary"), vmem_mb=40,
    )(a, b)


def _dw_in(segs, u_all, name):
    tiles = []
    for m, s_ in enumerate(segs):
        tiles += [(m, h) for h in range(s_.shape[1] // D)]
    ntile = len(tiles)
    t_total = u_all.shape[0]
    bt = _deep_rows(t_total)
    nt = t_total // bt

    def body(u_ref, *refs):
        seg_refs, o_ref, acc = refs[:len(segs)], refs[len(segs)], refs[len(segs) + 1]
        n, t = pl.program_id(0), pl.program_id(1)
        for k, (m, _) in enumerate(tiles):
            @pl.when(n == k)
            def _(m=m):
                part = _tn(seg_refs[m][...], u_ref[...])

                @pl.when(t == 0)
                def _():
                    acc[...] = part

                @pl.when(t > 0)
                def _():
                    acc[...] += part

        @pl.when(t == nt - 1)
        def _():
            o_ref[...] = acc[...].astype(o_ref.dtype)

    def seg_spec(m):
        ks = [k for k, (mm, _) in enumerate(tiles) if mm == m]
        lo, hi = ks[0], ks[-1]
        on = lambda n: (n >= lo) & (n <= hi)
        return pl.BlockSpec((bt, D), lambda n, t: (jnp.where(on(n), t, 0), jnp.where(on(n), n - lo, 0)))

    return _pcall(
        body, name=name, out_shape=jax.ShapeDtypeStruct((1, ntile * D, D), MXU_DTYPE), grid=(ntile, nt),
        in_specs=[pl.BlockSpec((bt, D), lambda n, t: (t, 0))] + [seg_spec(m) for m in range(len(segs))],
        out_specs=pl.BlockSpec((None, D, D), lambda n, t: (0, n, 0)),
        scratch=[pltpu.VMEM((D, D), F32)], sem=("parallel", "arbitrary"), vmem_mb=56,
    )(u_all, *segs)


def _du_prenorm_bwd(segs, ddt, wi_main, wi_tail, xin, mods, dres, row_off, tb, name, comm=None):
    n = xin.shape[0]
    nt = n // tb
    off = row_off // tb
    has_dx = dres is not None

    def body(*refs):
        seg_refs = refs[:7]
        ddt_ref, w_ref, wb_ref, wdt_ref, x_ref, mod_ref = refs[7:13]
        rest = refs[13:]
        if has_dx:
            dres_ref, dx_ref, acc_ref, du_scr = rest
        else:
            acc_ref, du_scr = rest
        j, i = pl.program_id(0), pl.program_id(1)
        rows = pl.ds(pl.multiple_of(i * tb, tb), tb)

        @pl.when((i == 0) & (j == 0))
        def _():
            acc_ref[...] = jnp.zeros_like(acc_ref)

        @pl.when(j == 0)
        def _():
            du_scr[rows, :] = _nn(ddt_ref[...], wdt_ref[...])

        for k in range(8):
            if not has_dx and k in (4, 5):
                continue

            @pl.when(j == k)
            def _(k=k):
                sv = seg_refs[min(k, 6)][...]
                part = _nn(sv, w_ref[...])
                if k in (2, 4, 6):
                    part = part + _nn(sv[:, 0:WTAIL], wb_ref[...])
                du_scr[rows, :] += part

        @pl.when(j == 7)
        def _():
            du = du_scr[rows, :]
            xv = x_ref[...]
            r = lax.rsqrt(jnp.mean(xv * xv, axis=1, keepdims=True) + EPS)
            xh = xv * r
            nw = mod_ref[1:2, :]
            acc_ref[0:1, :] += _colsum(du)
            acc_ref[1:2, :] += _colsum(du * xh * nw)
            dn = du * mod_ref[0:1, :]
            acc_ref[2:3, :] += _colsum(dn * xh)
            if has_dx:
                dxh = dn * nw
                dx_ref[...] = dres_ref[...] + r * (dxh - xh * jnp.mean(dxh * xh, axis=1, keepdims=True))

    def seg_spec(k):
        if k < 6:
            return pl.BlockSpec((tb, D), lambda j, i: (jnp.where(j == k, i + off, 0), 0))
        return pl.BlockSpec((tb, D), lambda j, i: (jnp.where(j >= 6, i + off, 0), jnp.where(j >= 6, j - 6, 0)))

    last = pl.BlockSpec((tb, D), lambda j, i: (jnp.where(j == 7, i, 0), 0))
    in_specs = [seg_spec(k) for k in range(7)]
    in_specs += [pl.BlockSpec((tb, 128), lambda j, i: (jnp.where(j == 0, i + off, 0), 0))] + _w_specs()
    in_specs += [last, _full((8, D))]
    args = list(segs) + [ddt, wi_main, wi_tail, wi_tail, xin, mods]
    out_shape = [jax.ShapeDtypeStruct((8, D), F32)]
    out_specs = [_full((8, D))]
    if has_dx:
        in_specs.append(last)
        args.append(dres)
        out_shape.insert(0, jax.ShapeDtypeStruct((n, D), F32))
        out_specs.insert(0, last)
    call = dict(body=body, args=args, name=name, out_shape=tuple(out_shape), grid=(8, nt), in_specs=in_specs,
                out_specs=tuple(out_specs), scratch=[pltpu.VMEM((n, D), F32)], sem=("arbitrary", "arbitrary"),
                vmem_mb=56)
    steps = lambda: ((pl.program_id(0) == 0) & (pl.program_id(1) == 0),
                     (pl.program_id(0) == 7) & (pl.program_id(1) == nt - 1))
    return _run(_carry(call, comm, steps))


def _sum8(v):
    def body(v_ref, o_ref):
        acc = v_ref[0]
        for k in range(1, 8):
            acc = acc + v_ref[k]
        o_ref[...] = acc

    return _pcall(body, name="small_sum", out_shape=jax.ShapeDtypeStruct(v.shape[1:], F32),
                  in_specs=[pl.BlockSpec(memory_space=pltpu.VMEM)], out_specs=pl.BlockSpec(memory_space=pltpu.VMEM))(v)


def _adamw(w, m, v, g, name):
    lead = w.ndim == 3
    rows, cols = w.shape[-2:]
    rb = 256 if rows % 256 == 0 else (352 if rows % 352 == 0 else rows)
    c1 = 1.0 - B1 ** STEP
    c2 = 1.0 - B2 ** STEP

    def body(w_ref, m_ref, v_ref, g_ref, d_ref, nm_ref, nv_ref):
        gv = g_ref[...]
        mn = B1 * m_ref[...] + (1.0 - B1) * gv
        vn = B2 * v_ref[...] + (1.0 - B2) * (gv * gv)
        nm_ref[...] = mn
        nv_ref[...] = vn
        d_ref[...] = -LR * ((mn / c1) / (jnp.sqrt(vn / c2) + AEPS) + WD * w_ref[...])

    if rb == rows and rows > 1024:
        cb, steps = 256, cols // 256
        gspec = pl.BlockSpec((rows, cb), lambda i: (0, i))
        spec = pl.BlockSpec((None, rows, cb), lambda i: (0, 0, i)) if lead else gspec
    else:
        steps = rows // rb
        gspec = pl.BlockSpec((rb, cols), lambda i: (i, 0))
        spec = pl.BlockSpec((None, rb, cols), lambda i: (0, i, 0)) if lead else gspec
    return _pcall(
        body, name=name, out_shape=(jax.ShapeDtypeStruct(w.shape, F32),) * 3, grid=(steps,),
        in_specs=[spec] * 3 + [gspec], out_specs=(spec,) * 3, sem=("parallel",), vmem_mb=40,
    )(w, m, v, g)


def _rows(v, n):
    f = v.reshape(-1)
    return jnp.pad(f, (0, n * D - f.shape[0])).reshape(n, D)


def kernel(x, c, ctx, c_ctx, w_ada, b_ada, norm_mix, w_in, conv_w, conv_b, ssd_a_log, ssd_dt_bias, ssd_d, ssd_norm, hgrn_lb_raw, hgrn_norm, w_out, norm_ffn, w_gate, w_up, w_down, final_norm, loss_target, m_c_ctx, m_w_ada, m_b_ada, m_norm_mix, m_w_in, m_conv_w, m_conv_b, m_ssd_a_log, m_ssd_dt_bias, m_ssd_d, m_ssd_norm, m_hgrn_lb_raw, m_hgrn_norm, m_w_out, m_norm_ffn, m_w_gate, m_w_up, m_w_down, m_final_norm, v_c_ctx, v_w_ada, v_b_ada, v_norm_mix, v_w_in, v_conv_w, v_conv_b, v_ssd_a_log, v_ssd_dt_bias, v_ssd_d, v_ssd_norm, v_hgrn_lb_raw, v_hgrn_norm, v_w_out, v_norm_ffn, v_w_gate, v_w_up, v_w_down, v_final_norm):
    ix, iy, ic = lax.axis_index("x"), lax.axis_index("y"), lax.axis_index("c")
    chip = 2 * ix + iy
    me = 2 * chip + ic
    xl, xc, tgt = x[0], ctx[0], loss_target[0]
    n_lat, n_ctx = xl.shape[0], xc.shape[0]
    assert n_ctx == TB and n_lat % 1024 == 0
    t_total = n_lat + n_ctx
    nb = t_total // TB

    tr = lambda a: jnp.swapaxes(a, -1, -2)
    shift = [functools.partial(jnp.pad, pad_width=((8 * k, WSL + WTAIL - NSH - 8 * k), (0, 0))) for k in range(4)]
    slab = lax.switch(chip, shift, tr(w_in[0]).astype(MXU_DTYPE))
    padrows = lambda a: jnp.pad(a, ((0, FSL - DFF // 4), (0, 0))).astype(MXU_DTYPE)
    shards = [slab[:WSL], slab[WSL:], w_out[0].astype(MXU_DTYPE), padrows(tr(w_gate[0])), padrows(tr(w_up[0])),
              padrows(w_down[0])]
    own = lambda g_, s_: lax.dynamic_update_slice(g_, s_[None], (chip, 0, 0))
    pack = jnp.concatenate([c, hgrn_lb_raw.reshape(1, D), _rows(conv_w[0], 3), jnp.zeros((3, D), F32)], axis=0)
    ncol_ada = w_ada.shape[2]
    b_shard = lax.dynamic_slice(b_ada, (0, chip * ncol_ada), (1, ncol_ada))
    gath, araw, mod_all, wi_main, wi_tail = _prologue(pack, c_ctx.reshape(1, D), w_ada[0], b_shard, shards[:2])
    wi_main, wi_tail = own(wi_main, shards[0]), own(wi_tail, shards[1])
    gath = gath.reshape(8, 8, D)
    lbraw_full = gath[0::2, 1].reshape(4, 2, 2, 256).transpose(1, 2, 0, 3).reshape(4, D)
    convw_full = gath[0::2, 2:5].reshape(4, 3 * D)[:, :KCONV * 512].reshape(4, KCONV, 512).transpose(1, 0, 2)
    convw_full = convw_full.reshape(KCONV, 2048)
    lbraw8 = jnp.pad(lbraw_full, ((0, 4), (0, 0)))
    convp = jnp.concatenate([convw_full, conv_b, jnp.zeros((2, 2048), F32)], axis=0)
    dtb = jnp.pad(ssd_dt_bias.reshape(1, 32), ((0, 7), (0, 96)))
    alog = jnp.pad(ssd_a_log.reshape(1, 32), ((0, 7), (0, 96)))
    mod_all = mod_all.reshape(8, 16, ncol_ada)[0::2]
    mod_full = mod_all.transpose(1, 0, 2).reshape(16, 4 * ncol_ada)
    my_mod = lax.dynamic_slice(mod_full, (me, 0), (1, 6 * D)).reshape(6, D)
    sh1, sc1, g1, sh2, sc2, g2 = (my_mod[k:k + 1] for k in range(6))
    csh1, csc1 = mod_full[8:9, 0:D], mod_full[8:9, D:2 * D]

    zrow = jnp.zeros((1, D), F32)
    mods_lat = jnp.concatenate([1.0 + sc1, sh1, norm_mix, zrow, zrow, zrow, zrow, zrow], axis=0)
    mods_ctx = jnp.concatenate([1.0 + csc1, csh1, norm_mix, zrow, zrow, zrow, zrow, zrow], axis=0)
    outs = _inproj(xl, mods_lat, wi_main, wi_tail, t_total, 1024, 0, None, "inproj_lat",
                   comm=_comm_gather(shards[2:5]))
    wo_g, wg_g, wu_g = (own(g_, s_) for g_, s_ in zip(outs[3:], shards[2:5]))
    w_out_f = wo_g.reshape(2 * D, D)
    p_main, p_dt, u_all = _inproj(xc, mods_ctx, wi_main, wi_tail, t_total, TB, nb - 1, outs[:3], "inproj_ctx")

    o_f, hs_f, wd_g = _hgrn_fwd(p_main, lbraw8, 0, nb, comm=_comm_gather(shards[5:]))
    w_down_f = own(wd_g, shards[5]).reshape(DFFP, D)
    o_b, hs_b = _hgrn_fwd(p_main, lbraw8, 1, nb)
    xa, dsl, dts = _ssd_prep(p_main, p_dt, convp, dtb, nb)
    y_f, ss_f = _ssd_fwd(xa, dts, alog, 0, nb)
    y_b, ss_b = _ssd_fwd(xa, dts, alog, 1, nb)

    vec_mix = jnp.concatenate([jnp.tile(hgrn_norm, (1, NH)), jnp.repeat(ssd_d, SP, axis=1), ssd_norm, g1, 1.0 + sc2,
                               sh2, norm_ffn, zrow], axis=0)
    ymix, ylat, h1, u2 = _mix_out(o_f, o_b, p_main, y_f, y_b, xa, xl, vec_mix, w_out_f)
    gate, up, act = _ffn_up(u2, wg_g, wu_g)
    vec_loss = jnp.concatenate([g2, final_norm.reshape(1, D)] + [zrow] * 6, axis=0)
    dh2, dffn, acc_loss = _ffn_down_loss(act, w_down_f, h1, tgt, vec_loss)

    core_arr = jnp.reshape(ic, (1,)).astype(jnp.int32)
    chip_arr = jnp.reshape(chip, (1,)).astype(jnp.int32)
    every = (0, 4)

    def pair_sum(gs, got, tag):
        return list(_pair_sum(gs, list(got), core_arr, "grads_pair_sum_" + tag))

    vec_ffn = jnp.concatenate([g1, 1.0 + sc2, norm_ffn] + [zrow] * 5, axis=0)
    dgate, dup, du2 = _ffn_bwd(dffn, w_down_f, gate, up, wg_g, wu_g)
    dh1, dylat, acc_ffn = _ffn_norm_bwd(du2, h1, ylat, dh2, vec_ffn)
    gw_down = _dw(act, dffn, "dw_down").reshape(4, FSL, D)
    ga1 = [_dw(dgate, u2, "dw_gate").reshape(4, FSL, D), _dw(dup, u2, "dw_up").reshape(4, FSL, D)]
    res = _mix_bwd(dylat, o_f, o_b, p_main, y_f, y_b, xa, vec_mix, w_out_f, comm=_comm_pair(ga1))
    (do, dgr, dys, dzr, dxs_skip, acc_mix), pair_a1 = res[:6], pair_sum(ga1, res[6:], "a1")
    ga2 = [gw_down, _dw(ymix, dylat, "dw_out").reshape(4, D // 2, D)]

    res = _hgrn_bwd(p_main, lbraw8, hs_f, do, 0, nb, None,
                    comm=[_comm_exchange(pair_a1, [every] * 2), _comm_pair(ga2)])
    (dq0, dff, dv0, dlb_f), recv_a, pair_a2 = res[:4], list(res[4:6]), pair_sum(ga2, res[6:], "a2")
    res = _hgrn_bwd(p_main, lbraw8, hs_b, do, 1, nb, (dq0, dv0), comm=_comm_exchange(pair_a2, [every] * 2))
    (dq, dfb, dv, dlb_b), recv_a = res[:4], recv_a + list(res[4:])
    pair_a, dests_a = pair_a1 + pair_a2, [every] * 4
    gw_in = [_dw_in([dq, dff], u_all, "dw_in_0"), _dw_in([dfb, dv], u_all, "dw_in_1"),
             _dw_in([dgr, dzr], u_all, "dw_in_2")]

    res = _ssd_bwd(xa, dts, alog, ss_f, dys, 0, nb, None, comm=_comm_pair(gw_in))
    (dxa0, ddts0, da_f), pair_b, dests_b = res[:3], pair_sum(gw_in, res[3:], "b"), [(0, 1), (1, 2), (2, 3)]
    res = _ssd_bwd(xa, dts, alog, ss_b, dys, 1, nb, (dxa0, ddts0), comm=_comm_exchange(pair_b, dests_b))
    (dxa, ddts, da_b), recv_b = res[:3], list(res[3:])
    dxbc, ddt, acc_conv, acc_dtb = _ssd_prep_bwd(p_main, p_dt, convp, dtb, dsl, dxa, dxs_skip, ddts, nb)
    gw_in.append(_dw_in([dxbc], u_all, "dw_in_3"))
    gw_in_dt = _dw(ddt, u_all, "dw_in_dt")
    gc = [gw_in[3], jnp.concatenate([g_[:, 0:WTAIL, :] for g_ in gw_in[1:]] + [gw_in_dt[None]], axis=0)]

    segs = [dq, dff, dfb, dv, dgr, dzr, dxbc]
    bmods_lat = jnp.concatenate([1.0 + sc1, norm_mix] + [zrow] * 6, axis=0)
    bmods_ctx = jnp.concatenate([1.0 + csc1, norm_mix] + [zrow] * 6, axis=0)
    res = _du_prenorm_bwd(segs, ddt, wi_main, wi_tail, xc, bmods_ctx, None, n_lat, TB, "du_ctx", comm=_comm_pair(gc))
    acc_ctx, pair_c, dests_c = res[0], pair_sum(gc, res[1:], "c"), [(3, 4), every]
    res = _du_prenorm_bwd(segs, ddt, wi_main, wi_tail, xl, bmods_lat, dh1, 0, 512, "du_lat",
                          comm=_comm_exchange(pair_c, dests_c))
    (grad_x, acc_lat), recv_c = res[:2], list(res[2:])

    mine = _chip_sum(pair_b + pair_c + pair_a, recv_b + recv_c + recv_a, chip_arr, dests_b + dests_c + dests_a,
                     [0, 0, 0, 0, 1, 3, 4, 5, 2])
    theirs = _pair_swap(mine)
    whole = [jnp.concatenate([jnp.where(ic == 0, m_, t_), jnp.where(ic == 0, t_, m_)], axis=0)
             for m_, t_ in zip(mine, theirs)]
    g_w_in = lax.dynamic_slice(jnp.concatenate(whole[0:2], axis=0), (8 * chip, 0), (NSH, D))
    g_w_out = whole[2]
    g_w_gate = whole[3][:DFF // 4]
    g_w_up = whole[4][:DFF // 4]
    g_w_down = whole[5][:DFF // 4]

    dmod_lat = jnp.concatenate([acc_lat[0:2], acc_ffn[3:4], acc_ffn[0:2], acc_loss[0:1]], axis=0)
    misc = jnp.concatenate([(da_f + da_b)[0, :32], jnp.zeros((96,), F32), acc_dtb[0, :32], jnp.zeros((96,), F32),
                            jnp.sum(acc_loss[2]).reshape(1), jnp.zeros((D - 257,), F32)]).reshape(1, D)
    sv = jnp.concatenate([
        dmod_lat, acc_ctx[0:2], (acc_lat[2:3] + acc_ctx[2:3]), acc_ffn[2:3], acc_loss[1:2], acc_mix[2:3],
        acc_mix[0:1], acc_mix[1:2], dlb_f[0:1], dlb_b[0:1], acc_conv[0:6].reshape(12, D), misc,
        jnp.zeros((3, D), F32)], axis=0)
    sv_all = _allgather8(sv, "small_grads_gather").reshape(8, 32, D)
    ssum = _sum8(sv_all)
    dmod_rows = sv_all[:, 0:6].reshape(8, 6 * D)
    dmod_ctx_row = jnp.concatenate([ssum[6:8].reshape(1, 2 * D), jnp.zeros((1, 4 * D), F32)], axis=1)
    dmod_full = jnp.concatenate([dmod_rows, dmod_ctx_row, jnp.zeros((7, 6 * D), F32)], axis=0)
    grad_b_ada = jnp.sum(dmod_full, axis=0, keepdims=True)
    dmod_shard = lax.dynamic_slice(dmod_full, (0, chip * ncol_ada), (16, ncol_ada))
    g_w_ada, da_part = _ada_bwd(araw, dmod_shard, w_ada[0])
    da_all = _allgather8(da_part, "ada_ctx_gather").reshape(8, 16, D)[0::2, 8]
    cc = c_ctx.reshape(1, D)
    grad_c_ctx = (jnp.sum(da_all, axis=0, keepdims=True) * _dsilu(cc)).reshape(D)

    grad_norm_mix, grad_norm_ffn, grad_final_norm = ssum[8:9], ssum[9:10], ssum[10].reshape(D)
    grad_ssd_norm = ssum[11:12]
    grad_hgrn_norm = jnp.sum(ssum[12].reshape(NH, HF), axis=0, keepdims=True)
    grad_ssd_d = jnp.sum(ssum[13].reshape(SHEADS, SP), axis=1).reshape(1, SHEADS)
    lb_full = _sig(lbraw_full[0:2] - lbraw_full[2:4])
    dr0 = ssum[14:16] * lb_full * (1.0 - lb_full)
    grad_lb_full = jnp.stack([dr0, -dr0], axis=0)
    grad_lb = lax.dynamic_slice(grad_lb_full, (0, 0, chip * 256), (2, 2, 256))
    grad_conv_w = lax.dynamic_slice(ssum[16:26].reshape(KCONV, 2048), (0, chip * 512), (KCONV, 512)).reshape(1, KCONV, 512)
    grad_conv_b = ssum[26:28].reshape(1, 2048)
    a_val = -jnp.exp(ssd_a_log)
    grad_a_log = ssum[28, 0:32].reshape(1, 2, SHEADS) * a_val
    grad_dt_bias = ssum[28, 128:160].reshape(1, 2, SHEADS)
    loss = ssum[28, 256]

    small_w = [c_ctx, b_ada, norm_mix, conv_w, conv_b, ssd_a_log, ssd_dt_bias, ssd_d, ssd_norm, hgrn_lb_raw,
               hgrn_norm, norm_ffn, final_norm]
    small_m = [m_c_ctx, m_b_ada, m_norm_mix, m_conv_w, m_conv_b, m_ssd_a_log, m_ssd_dt_bias, m_ssd_d, m_ssd_norm,
               m_hgrn_lb_raw, m_hgrn_norm, m_norm_ffn, m_final_norm]
    small_v = [v_c_ctx, v_b_ada, v_norm_mix, v_conv_w, v_conv_b, v_ssd_a_log, v_ssd_dt_bias, v_ssd_d, v_ssd_norm,
               v_hgrn_lb_raw, v_hgrn_norm, v_norm_ffn, v_final_norm]
    small_g = [grad_c_ctx, grad_b_ada, grad_norm_mix, grad_conv_w, grad_conv_b, grad_a_log, grad_dt_bias, grad_ssd_d,
               grad_ssd_norm, grad_lb, grad_hgrn_norm, grad_norm_ffn, grad_final_norm]
    nrows = [-(-a.size // D) for a in small_w]
    packs = lambda lst: jnp.concatenate([_rows(a, r) for a, r in zip(lst, nrows)]
                                        + [jnp.zeros((24 - sum(nrows), D), F32)], axis=0)
    sd, sm, svv = _adamw(packs(small_w), packs(small_m), packs(small_v), packs(small_g), "adamw_small")

    def unpack(p):
        out, r0 = [], 0
        for a, r in zip(small_w, nrows):
            out.append(p[r0:r0 + r].reshape(-1)[:a.size].reshape(a.shape))
            r0 += r
        return out

    sd, sm, svv = unpack(sd), unpack(sm), unpack(svv)
    big = {}
    for nm, w_, m_, v_, g_ in (("w_ada", w_ada, m_w_ada, v_w_ada, g_w_ada), ("w_in", w_in, m_w_in, v_w_in, g_w_in),
                               ("w_out", w_out, m_w_out, v_w_out, g_w_out),
                               ("w_gate", w_gate, m_w_gate, v_w_gate, g_w_gate),
                               ("w_up", w_up, m_w_up, v_w_up, g_w_up),
                               ("w_down", w_down, m_w_down, v_w_down, g_w_down)):
        if nm in ("w_in", "w_gate", "w_up"):
            big[nm] = tuple(tr(t) for t in (g_[None],) + tuple(_adamw(tr(w_), tr(m_), tr(v_), g_, "adamw_" + nm)))
        else:
            big[nm] = (g_[None],) + tuple(_adamw(w_, m_, v_, g_, "adamw_" + nm))

    order = ["c_ctx", "w_ada", "b_ada", "norm_mix", "w_in", "conv_w", "conv_b", "ssd_a_log", "ssd_dt_bias", "ssd_d",
             "ssd_norm", "hgrn_lb_raw", "hgrn_norm", "w_out", "norm_ffn", "w_gate", "w_up", "w_down", "final_norm"]
    small_names = ["c_ctx", "b_ada", "norm_mix", "conv_w", "conv_b", "ssd_a_log", "ssd_dt_bias", "ssd_d", "ssd_norm",
                   "hgrn_lb_raw", "hgrn_norm", "norm_ffn", "final_norm"]
    table = dict(big)
    for k, nm in enumerate(small_names):
        table[nm] = (small_g[k].reshape(small_w[k].shape), sd[k], sm[k], svv[k])
    grads = [table[nm][0] for nm in order]
    deltas = [table[nm][1] for nm in order]
    new_m = [table[nm][2] for nm in order]
    new_v = [table[nm][3] for nm in order]
    return (loss, grad_x[None], *grads, *deltas, *new_m, *new_v)
```

```python
import functools
import math

import jax
import jax.numpy as jnp
from jax import lax
from jax.experimental import pallas as pl
from jax.experimental.pallas import tpu as pltpu

F32 = jnp.float32
BF16 = jnp.bfloat16
MXU_DTYPE = jnp.bfloat16
_INTERPRET = False

D = 1024
NH, HF = 8, 128
HC = 64
SC = 128
SN = 128
SHEADS, SP = 16, 64
GRID_W = 64
KCONV = 5
DFF = 2816
FSL = 768
DFFP = 4 * FSL
NIN = 8224
TB = 256
EPS = 1e-6
LR, B1, B2, AEPS, WD, STEP = 0.001, 0.9, 0.999, 1e-08, 0.01, 10
MESH_ID = pl.DeviceIdType.MESH
NSH = NIN // 4
WSL = 2048
WTAIL = 128


def _pcall(body, *, name, out_shape, grid=(), in_specs=None, out_specs=None, scratch=(), sem=None,
           vmem_mb=None, aliases=None):
    params = {}
    if sem is not None:
        params["dimension_semantics"] = sem
    if vmem_mb is not None:
        params["vmem_limit_bytes"] = vmem_mb << 20
    kw = dict(name=name, out_shape=out_shape, scratch_shapes=list(scratch),
              input_output_aliases=aliases or {}, compiler_params=pltpu.CompilerParams(**params),
              interpret=_INTERPRET)
    if grid:
        kw["grid"] = grid
    if in_specs is not None:
        kw["in_specs"] = in_specs
    if out_specs is not None:
        kw["out_specs"] = out_specs
    return pl.pallas_call(body, **kw)


def _mx(a):
    return a.astype(MXU_DTYPE)


def _dg(a, b, ca, cb):
    return lax.dot_general(_mx(a), _mx(b), (((ca,), (cb,)), ((), ())), preferred_element_type=F32)


def _nn(a, b):
    return _dg(a, b, 1, 0)


def _nt(a, b):
    return _dg(a, b, 1, 1)


def _tn(a, b):
    return _dg(a, b, 0, 0)


def _dot01(m, x):
    hi = x.astype(BF16)
    r1 = x - hi.astype(F32)
    mid = r1.astype(BF16)
    lo = (r1 - mid.astype(F32)).astype(BF16)
    f = lambda t: lax.dot_general(m, t, (((1,), (0,)), ((), ())), preferred_element_type=F32)
    return f(hi) + f(mid) + f(lo)


def _tri(n, upper):
    r = lax.broadcasted_iota(jnp.int32, (n, n), 0)
    c = lax.broadcasted_iota(jnp.int32, (n, n), 1)
    return (c >= r) if upper else (c <= r)


def _b01(mask):
    return jnp.where(mask, 1.0, 0.0).astype(BF16)


def _sig(x):
    return jax.nn.sigmoid(x)


def _silu(x):
    return x * _sig(x)


def _dsilu(x):
    s = _sig(x)
    return s * (1.0 + x * (1.0 - s))


def _softplus(x):
    return jnp.maximum(x, 0.0) + jnp.log(1.0 + jnp.exp(-jnp.abs(x)))


def _rowsum(x):
    return jnp.sum(x, axis=1, keepdims=True)


def _colsum(x):
    return jnp.sum(x, axis=0, keepdims=True)


def _full(shape):
    return pl.BlockSpec(shape, lambda *_: (0,) * len(shape))


def _allgather8_ops(x_ref, out_ref, send_sems, recv_sems, local_sem):
    m_per = x_ref.shape[0]
    x, y, c = lax.axis_index("x"), lax.axis_index("y"), lax.axis_index("c")
    me, sibling = (x, y, c), (x, y, 1 - c)
    chips = [(1 - x, y), (x, 1 - y), (1 - x, 1 - y)]

    def rows(px, py, pc):
        return out_ref.at[pl.ds((4 * px + 2 * py + pc) * m_per, m_per), :]

    def copy(k, block, to, src=None):
        return pltpu.make_async_remote_copy(
            src_ref=rows(*block) if src is None else src, dst_ref=rows(*block),
            send_sem=send_sems.at[k], recv_sem=recv_sems.at[k], device_id=to, device_id_type=MESH_ID)

    mine = pltpu.make_async_copy(x_ref, rows(*me), local_sem)
    mine.start()
    first = [copy(0, me, sibling, src=x_ref)]
    first += [copy(1 + j, me, (*chip, c), src=x_ref) for j, chip in enumerate(chips)]
    for cp in first:
        cp.start()
    passed = [copy(4 + j, (*chip, c), sibling) for j, chip in enumerate(chips)]
    for j, chip in enumerate(chips):
        copy(1 + j, (*chip, c), me).wait_recv()
        passed[j].start()
    copy(0, sibling, me).wait_recv()
    for j, chip in enumerate(chips):
        copy(4 + j, (*chip, 1 - c), me).wait_recv()
    for cp in first + passed:
        cp.wait_send()
    mine.wait()


_AG8_SEMS = [pltpu.SemaphoreType.DMA((7,)), pltpu.SemaphoreType.DMA((7,)), pltpu.SemaphoreType.DMA]


def _allgather8(v, name):
    m_per, n = v.shape
    return _pcall(
        functools.partial(_allgather8_ops), name=name, out_shape=jax.ShapeDtypeStruct((8 * m_per, n), v.dtype),
        in_specs=[pl.BlockSpec(memory_space=pltpu.VMEM)], out_specs=pl.BlockSpec(memory_space=pltpu.VMEM),
        scratch=list(_AG8_SEMS),
    )(v)


def _prologue(pack, cc_row, w_ada, b_shard, shards):
    n = len(shards)
    ncol = w_ada.shape[1]

    def body(pack_ref, cc_ref, w_ref, b_ref, *refs):
        ins = refs[:n]
        gath_ref, araw_ref, mod_ref = refs[n:n + 3]
        outs = refs[n + 3:2 * n + 3]
        modsh, s1, r1, l1, s2, r2, l2, gs, gr = refs[2 * n + 3:]
        start, finish = _gather_ops(ins, outs, gs, gr, relay=True)
        start()
        _allgather8_ops(pack_ref, gath_ref, s1, r1, l1)
        a = jnp.concatenate([gath_ref[8 * i:8 * i + 1, :] for i in range(8)] + [cc_ref[...], jnp.zeros((7, D), F32)],
                            axis=0)
        araw_ref[...] = a
        modsh[...] = _nn(_silu(a), w_ref[...]) + b_ref[...]
        _allgather8_ops(modsh, mod_ref, s2, r2, l2)
        finish()

    vm = pl.BlockSpec(memory_space=pltpu.VMEM)
    anyspec = pl.BlockSpec(memory_space=pl.ANY)
    return _pcall(
        body, name="prologue",
        out_shape=(jax.ShapeDtypeStruct((64, D), F32), jax.ShapeDtypeStruct((16, D), F32),
                   jax.ShapeDtypeStruct((128, ncol), F32)) + _gather_out(shards),
        in_specs=[vm, vm, vm, vm] + [anyspec] * n, out_specs=(vm, vm, vm) + (anyspec,) * n,
        scratch=[pltpu.VMEM((16, ncol), F32)] + list(_AG8_SEMS) + list(_AG8_SEMS) + _gather_sems(n), vmem_mb=40,
    )(pack, cc_row, w_ada, b_shard, *shards)


def _gather_ops(ins, outs, send_sems, recv_sems, relay=False):
    n = len(ins)
    x, y, c = lax.axis_index("x"), lax.axis_index("y"), lax.axis_index("c")
    me, sibling = (x, y, c), (x, y, 1 - c)
    chips = [(1 - x, y), (x, 1 - y), (1 - x, 1 - y)]
    direct = 2 if relay else 3

    def part(a, px, py, pc, quarter=None):
        half = ins[a].shape[0] // 2
        if quarter is None:
            return outs[a].at[2 * px + py, pl.ds(pc * half, half), :]
        return outs[a].at[2 * px + py, pl.ds(pc * half + quarter * (half // 2), half // 2), :]

    def copy(a, k, block, to, src=None, quarter=None):
        return pltpu.make_async_remote_copy(
            src_ref=part(a, *block, quarter) if src is None else src, dst_ref=part(a, *block, quarter),
            send_sem=send_sems.at[8 * a + k], recv_sem=recv_sems.at[8 * a + k], device_id=to,
            device_id_type=MESH_ID)

    def first(a, j):
        half = ins[a].shape[0] // 2
        return copy(a, j, me, (*chips[j], c), src=ins[a].at[pl.ds(c * half, half), :])

    relayed = lambda a, q: copy(a, 6 + q, (*chips[q], c), (*chips[1 - q], c), quarter=q)

    def start():
        for a in range(n):
            for j in range(direct):
                first(a, j).start()

    def finish():
        for a in range(n):
            for j in range(direct):
                copy(a, j, (*chips[j], c), me).wait_recv()
                copy(a, 3 + j, (*chips[j], c), sibling).start()
                if relay:
                    relayed(a, j).start()
            if relay:
                for q in range(2):
                    copy(a, 6 + q, (*chips[2], c), me, quarter=q).wait_recv()
                copy(a, 5, (*chips[2], c), sibling).start()
        for a in range(n):
            for j, chip in enumerate(chips):
                copy(a, 3 + j, (*chip, 1 - c), me).wait_recv()
        for a in range(n):
            for j, chip in enumerate(chips):
                if j < direct:
                    first(a, j).wait_send()
                    if relay:
                        relayed(a, j).wait_send()
                copy(a, 3 + j, (*chip, c), sibling).wait_send()

    return start, finish


def _gather_out(shards):
    return tuple(jax.ShapeDtypeStruct((4,) + s_.shape, s_.dtype) for s_ in shards)


def _gather_sems(n):
    return [pltpu.SemaphoreType.DMA((8 * n,)), pltpu.SemaphoreType.DMA((8 * n,))]


def _pair_ops(ins, outs, send_sems, recv_sems):
    x, y, c = lax.axis_index("x"), lax.axis_index("y"), lax.axis_index("c")
    cps = []
    for a in range(len(ins)):
        half = ins[a].shape[1] // 2
        cps.append(pltpu.make_async_remote_copy(
            src_ref=ins[a].at[:, pl.ds((1 - c) * half, half), :], dst_ref=outs[a], send_sem=send_sems.at[a],
            recv_sem=recv_sems.at[a], device_id=(x, y, 1 - c), device_id_type=MESH_ID))

    def start():
        for cp in cps:
            cp.start()

    def finish():
        for cp in cps:
            cp.wait()

    return start, finish


def _comm_pair(gs):
    n = len(gs)
    return (list(gs), tuple(jax.ShapeDtypeStruct((g.shape[0], g.shape[1] // 2, g.shape[2]), g.dtype) for g in gs),
            [pltpu.SemaphoreType.DMA((n,)), pltpu.SemaphoreType.DMA((n,))], _pair_ops)


def _exchange_ops(ins, outs, send_sems, recv_sems, dests):
    x, y, c = lax.axis_index("x"), lax.axis_index("y"), lax.axis_index("c")
    mine = 2 * x + y
    chips = [(1 - x, y), (x, 1 - y), (1 - x, 1 - y)]

    def each(fn):
        for a in range(len(ins)):
            lo, hi = dests[a]
            for j, (px, py) in enumerate(chips):
                q = 2 * px + py
                cp = pltpu.make_async_remote_copy(
                    src_ref=ins[a].at[jnp.clip(q - lo, 0, hi - lo - 1)], dst_ref=outs[a].at[j],
                    send_sem=send_sems.at[3 * a + j], recv_sem=recv_sems.at[3 * a + j], device_id=(px, py, c),
                    device_id_type=MESH_ID)
                fn(cp, (q >= lo) & (q < hi), (mine >= lo) & (mine < hi), (lo, hi) == (0, 4))

    def start():
        def go(cp, send_ok, recv_ok, always):
            if always:
                cp.start()
            else:
                pl.when(send_ok)(cp.start)
        each(go)

    def finish():
        def go(cp, send_ok, recv_ok, always):
            if always:
                cp.wait()
            else:
                pl.when(send_ok)(cp.wait_send)
                pl.when(recv_ok)(cp.wait_recv)
        each(go)

    return start, finish


def _comm_exchange(hs, dests):
    n = len(hs)
    return (list(hs), tuple(jax.ShapeDtypeStruct((3,) + h.shape[1:], h.dtype) for h in hs),
            [pltpu.SemaphoreType.DMA((3 * n,)), pltpu.SemaphoreType.DMA((3 * n,))],
            lambda i, o, s, r: _exchange_ops(i, o, s, r, dests))


def _comm_gather(shards, relay=False):
    return (list(shards), _gather_out(shards), _gather_sems(len(shards)),
            lambda i, o, s, r: _gather_ops(i, o, s, r, relay))


def _carry(call, comm, steps):
    if comm is None:
        return call
    if isinstance(comm, list):
        for one in comm:
            call = _carry(call, one, steps)
        return call
    arrays, out_shape, sems, make = comm
    n, n_in, n_out = len(arrays), len(call["args"]), len(call["out_shape"])
    body = call["body"]

    def wrapped(*refs):
        base_in, cin = refs[:n_in], refs[n_in:n_in + n]
        rest = refs[n_in + n:]
        base_out, cout, scr = rest[:n_out], rest[n_out:n_out + n], rest[n_out + n:]
        start, finish = make(cin, cout, scr[-2], scr[-1])
        first, last = steps()
        pl.when(first)(start)
        body(*base_in, *base_out, *scr[:-2])
        pl.when(last)(finish)

    anyspec = pl.BlockSpec(memory_space=pl.ANY)
    return dict(call, body=wrapped, args=list(call["args"]) + arrays,
                in_specs=list(call["in_specs"]) + [anyspec] * n,
                out_shape=tuple(call["out_shape"]) + tuple(out_shape),
                out_specs=tuple(call["out_specs"]) + (anyspec,) * n,
                scratch=list(call["scratch"]) + sems)


def _run(call):
    args = call.pop("args")
    body = call.pop("body")
    return _pcall(body, **call)(*args)


def _pair_swap(rs, sv):
    n = len(rs)

    def body(sv_ref, *refs):
        ins, outs, got_ref = refs[:n], refs[n:2 * n], refs[2 * n]
        send_sems, recv_sems, s1, r1, l1 = refs[2 * n + 1:]
        x, y, c = lax.axis_index("x"), lax.axis_index("y"), lax.axis_index("c")
        cps = [pltpu.make_async_remote_copy(
            src_ref=ins[a], dst_ref=outs[a], send_sem=send_sems.at[a], recv_sem=recv_sems.at[a],
            device_id=(x, y, 1 - c), device_id_type=MESH_ID) for a in range(n)]
        for cp in cps:
            cp.start()
        _allgather8_ops(sv_ref, got_ref, s1, r1, l1)
        for cp in cps:
            cp.wait()

    vm, anyspec = pl.BlockSpec(memory_space=pltpu.VMEM), pl.BlockSpec(memory_space=pl.ANY)
    return _pcall(
        body, name="grads_pair_swap",
        out_shape=tuple(jax.ShapeDtypeStruct(r.shape, r.dtype) for r in rs)
        + (jax.ShapeDtypeStruct((8 * sv.shape[0], sv.shape[1]), sv.dtype),),
        in_specs=[vm] + [anyspec] * n, out_specs=(anyspec,) * n + (vm,),
        scratch=[pltpu.SemaphoreType.DMA((n,)), pltpu.SemaphoreType.DMA((n,))] + list(_AG8_SEMS),
    )(sv, *rs)


SUM_STEPS = 4


def _pair_sum(gs, recvs, core, name):
    n = len(gs)

    def body(c_ref, *refs):
        for a in range(n):
            refs[2 * n + a][...] = (refs[a][...].astype(F32) + refs[n + a][...].astype(F32)).astype(refs[2 * n + a].dtype)

    blk = lambda g: (g.shape[0], g.shape[1] // (2 * SUM_STEPS), g.shape[2])
    return pl.pallas_call(
        body, name=name,
        out_shape=tuple(jax.ShapeDtypeStruct((g.shape[0], g.shape[1] // 2, g.shape[2]), g.dtype) for g in gs),
        grid_spec=pltpu.PrefetchScalarGridSpec(
            num_scalar_prefetch=1, grid=(SUM_STEPS,),
            in_specs=[pl.BlockSpec(blk(g), lambda i, cr: (0, cr[0] * SUM_STEPS + i, 0)) for g in gs]
            + [pl.BlockSpec(blk(g), lambda i, cr: (0, i, 0)) for g in gs],
            out_specs=tuple(pl.BlockSpec(blk(g), lambda i, cr: (0, i, 0)) for g in gs)),
        compiler_params=pltpu.CompilerParams(vmem_limit_bytes=40 << 20), interpret=_INTERPRET,
    )(core, *gs, *recvs)


def _chip_sum(hs, recvs, chip, dests, slots):
    n = len(hs)
    nout = max(slots) + 1
    first = [slots.index(o) for o in range(nout)]
    every = lambda d_: d_ == (0, 4)

    def own(d_):
        if every(d_):
            return lambda i, kr: (kr[0], i, 0)
        return lambda i, kr: (0, jnp.where(kr[0] == d_[0], i, 0), 0)

    def got(d_):
        if every(d_):
            return lambda i, kr: (0, i, 0)
        return lambda i, kr: (0, jnp.where(kr[0] == d_[0], i, 0), 0)

    def body(k_ref, *refs):
        for a in range(n):
            def emit(a=a):
                acc = refs[a][0].astype(F32)
                for j in range(3):
                    acc = acc + refs[n + a][j].astype(F32)
                refs[2 * n + slots[a]][...] = acc
            if every(dests[a]):
                emit()
            else:
                pl.when(k_ref[0] == dests[a][0])(emit)

    rb = lambda h: h.shape[1] // SUM_STEPS
    return pl.pallas_call(
        body, name="grads_chip_sum",
        out_shape=tuple(jax.ShapeDtypeStruct(hs[a].shape[1:], F32) for a in first),
        grid_spec=pltpu.PrefetchScalarGridSpec(
            num_scalar_prefetch=1, grid=(SUM_STEPS,),
            in_specs=[pl.BlockSpec((1, rb(h), h.shape[2]), own(d_)) for h, d_ in zip(hs, dests)]
            + [pl.BlockSpec((3, rb(h), h.shape[2]), got(d_)) for h, d_ in zip(hs, dests)],
            out_specs=tuple(pl.BlockSpec((rb(hs[a]), hs[a].shape[2]), lambda i, kr: (i, 0)) for a in first)),
        compiler_params=pltpu.CompilerParams(vmem_limit_bytes=40 << 20), interpret=_INTERPRET,
    )(chip, *hs, *recvs)


def _ada_bwd(araw, dmod, w):
    nblk = w.shape[1] // 512

    def body(a_ref, d_ref, w_ref, gw_ref, da_ref):
        j = pl.program_id(0)
        gw_ref[...] = _tn(_silu(a_ref[...]), d_ref[...])
        part = _nt(d_ref[...], w_ref[...])

        @pl.when(j == 0)
        def _():
            da_ref[...] = part

        @pl.when(j > 0)
        def _():
            da_ref[...] += part

    return _pcall(
        body, name="ada_bwd",
        out_shape=(jax.ShapeDtypeStruct(w.shape, F32), jax.ShapeDtypeStruct((16, D), F32)), grid=(nblk,),
        in_specs=[_full((16, D)), pl.BlockSpec((16, 512), lambda j: (0, j)), pl.BlockSpec((D, 512), lambda j: (0, j))],
        out_specs=(pl.BlockSpec((D, 512), lambda j: (0, j)), _full((16, D))), sem=("arbitrary",),
    )(araw, dmod, w)


def _w_specs():
    return [pl.BlockSpec((None, D, D), lambda j, i: (j // 2, j % 2, 0)),
            pl.BlockSpec((None, WTAIL, D), lambda j, i: (jnp.maximum(j // 2 - 1, 0), 0, 0)),
            pl.BlockSpec((None, WTAIL, D), lambda j, i: (3, 0, 0))]


def _inproj(xin, mods, wi_main, wi_tail, t_total, tb, blk_off, prev, name, comm=None):
    n = xin.shape[0]
    nt = n // tb
    ncol = 8

    def body(x_ref, mod_ref, w_ref, wb_ref, wdt_ref, *rest):
        p_ref, pdt_ref, u_ref, uscr = rest[-4:]
        j, i = pl.program_id(0), pl.program_id(1)
        rows = pl.ds(pl.multiple_of(i * tb, tb), tb)

        @pl.when(j == 0)
        def _():
            xv = x_ref[...]
            r = lax.rsqrt(jnp.mean(xv * xv, axis=1, keepdims=True) + EPS)
            u = (xv * r * mod_ref[2:3, :]) * mod_ref[0:1, :] + mod_ref[1:2, :]
            ub = u.astype(MXU_DTYPE)
            uscr[rows, :] = ub
            u_ref[...] = ub
            pdt_ref[...] = _nt(ub, wdt_ref[...])

        ub = uscr[rows, :]
        pv = _nt(ub, w_ref[...])

        @pl.when((j % 2 == 1) | (j == 0))
        def _():
            p_ref[...] = pv.astype(p_ref.dtype)

        @pl.when((j % 2 == 0) & (j > 0))
        def _():
            head = pv[:, 0:WTAIL] + _nt(ub, wb_ref[...])
            p_ref[...] = jnp.concatenate([head, pv[:, WTAIL:]], axis=1).astype(p_ref.dtype)

    once = lambda j, i: (jnp.where(j == 0, i, nt - 1) + blk_off, 0)
    in_specs = [pl.BlockSpec((tb, D), lambda j, i: (jnp.where(j == 0, i, nt - 1), 0)), _full((8, D))] + _w_specs()
    args = [xin, mods, wi_main, wi_tail, wi_tail]
    aliases = None
    if prev is not None:
        in_specs += [pl.BlockSpec(memory_space=pl.ANY)] * 3
        args += list(prev)
        aliases = {5: 0, 6: 1, 7: 2}
    call = dict(
        body=body, args=args, name=name,
        out_shape=(jax.ShapeDtypeStruct((t_total, ncol * D), MXU_DTYPE), jax.ShapeDtypeStruct((t_total, 128), F32),
                   jax.ShapeDtypeStruct((t_total, D), MXU_DTYPE)),
        grid=(ncol, nt), in_specs=in_specs,
        out_specs=(pl.BlockSpec((tb, D), lambda j, i: (i + blk_off, j)), pl.BlockSpec((tb, 128), once),
                   pl.BlockSpec((tb, D), once)),
        scratch=[pltpu.VMEM((n, D), MXU_DTYPE)], sem=("arbitrary", "arbitrary"), vmem_mb=48, aliases=aliases)
    steps = lambda: ((pl.program_id(0) == 0) & (pl.program_id(1) == 0),
                     (pl.program_id(0) == ncol - 1) & (pl.program_id(1) == nt - 1))
    return _run(_carry(call, comm, steps))


def _blk(s, nb, rev):
    return jnp.where(s == 0, nb - 1, (nb - 1 - s) if rev else (s - 1))


def _hgrn_gate(fr, lbraw_ref, d):
    lb = _sig(lbraw_ref[d:d + 1, :] - lbraw_ref[2 + d:3 + d, :])
    sg = _sig(fr)
    return lb, sg, lb + (1.0 - lb) * sg


def _hgrn_fwd(p_main, lbraw, d, nb, comm=None):
    t_total = p_main.shape[0]
    rev = d == 1
    nch = TB // HC
    scale = HF ** -0.5

    def body(q_ref, f_ref, v_ref, lb_ref, o_ref, sp_ref, st):
        s = pl.program_id(0)

        @pl.when(s == 0)
        def _():
            st[...] = jnp.zeros_like(st)

        mb = _tri(HC, rev)
        m01 = _b01(mb)
        order = list(reversed(range(nch)) if rev else range(nch))
        hs_ = [slice(h * HF, (h + 1) * HF) for h in range(NH)]
        pre = {}
        for c in order:
            rows = slice(c * HC, (c + 1) * HC)
            _, _, f = _hgrn_gate(f_ref[rows, :].astype(F32), lb_ref, d)
            k = 1.0 - f
            cum = _dot01(m01, jnp.log(f))
            tot = cum[0:1, :] if rev else cum[HC - 1:HC, :]
            qd = _silu(q_ref[rows, :].astype(F32)) * scale * jnp.exp(cum)
            ki = k * jnp.exp(-cum)
            etot = jnp.exp(tot)
            pre[c] = (_mx(qd), _mx(ki), _mx(ki * etot), _mx(v_ref[rows, :]), etot)
        scs = {c: [_nt(pre[c][0][:, cs], pre[c][1][:, cs]) for cs in hs_] for c in order}
        upd = {c: [_tn(pre[c][3][:, cs], pre[c][2][:, cs]) for cs in hs_] for c in order}
        intra = {c: [_nn(jnp.where(mb, scs[c][h], 0.0), pre[c][3][:, cs]) for h, cs in enumerate(hs_)] for c in order}
        for c in order:
            rows = slice(c * HC, (c + 1) * HC)
            qdb, etot = pre[c][0], pre[c][4]
            for h, cs in enumerate(hs_):
                sth = st[h]
                stb = sth.astype(sp_ref.dtype)
                sp_ref[c, h] = stb
                o_ref[rows, cs] = (intra[c][h] + _nt(qdb[:, cs], stb)).astype(o_ref.dtype)
                st[h] = sth * etot[:, cs] + upd[c][h]

    col = lambda j: (lambda s: (_blk(s, nb, rev), j))
    call = dict(
        body=body, args=[p_main, p_main, p_main, lbraw], name=f"hgrn_fwd_{d}",
        out_shape=(jax.ShapeDtypeStruct((t_total, D), MXU_DTYPE),
                   jax.ShapeDtypeStruct((nch * nb, NH, HF, HF), MXU_DTYPE)),
        grid=(nb,),
        in_specs=[pl.BlockSpec((TB, D), col(0)), pl.BlockSpec((TB, D), col(1 + d)), pl.BlockSpec((TB, D), col(3)),
                  _full((8, D))],
        out_specs=(pl.BlockSpec((TB, D), col(0)),
                   pl.BlockSpec((nch, NH, HF, HF), lambda s: (_blk(s, nb, rev), 0, 0, 0))),
        scratch=[pltpu.VMEM((NH, HF, HF), F32)], sem=("arbitrary",), vmem_mb=40)
    return _run(_carry(call, comm, lambda: (pl.program_id(0) == 0, pl.program_id(0) == nb - 1)))


def _hgrn_bwd(p_main, lbraw, sprev, do, d, nb, prev, comm=None):
    t_total = p_main.shape[0]
    rev = d == 1
    nch = TB // HC
    scale = HF ** -0.5
    last = prev is not None
    odt = MXU_DTYPE if last else F32

    def body(q_ref, f_ref, v_ref, lb_ref, sp_ref, do_ref, *rest):
        if last:
            dqp_ref, dvp_ref = rest[:2]
            rest = rest[2:]
        dq_ref, df_ref, dv_ref, dlb_ref, dst = rest
        sp_id = pl.program_id(0)
        is_ctx = sp_id == nb - 1

        @pl.when(sp_id == 0)
        def _():
            dst[...] = jnp.zeros_like(dst)
            dlb_ref[...] = jnp.zeros_like(dlb_ref)

        mb = _tri(HC, rev)
        mbt = _tri(HC, not rev)
        m01 = _b01(mb)
        mt01 = _b01(mbt)
        order = list(range(nch) if rev else reversed(range(nch)))
        hs_ = [slice(h * HF, (h + 1) * HF) for h in range(NH)]
        pre = {}
        for c in order:
            rows = slice(c * HC, (c + 1) * HC)
            lb, sg, f = _hgrn_gate(f_ref[rows, :].astype(F32), lb_ref, d)
            k = 1.0 - f
            cum = _dot01(m01, jnp.log(f))
            tot = cum[0:1, :] if rev else cum[HC - 1:HC, :]
            e = jnp.exp(cum)
            ei = jnp.exp(-cum)
            etot = jnp.exp(tot)
            ee = ei * etot
            qraw = q_ref[rows, :].astype(F32)
            sq = _sig(qraw)
            qd = qraw * sq * scale * e
            ki = k * ei
            ke = k * ee
            dov = jnp.where(is_ctx, 0.0, do_ref[rows, :].astype(F32))
            pre[c] = dict(lb=lb, sg=sg, f=f, e=e, ei=ei, ee=ee, etot=etot, qd=qd, ki=ki, ke=ke,
                          dsq=sq * (1.0 + qraw * (1.0 - sq)),
                          qdb=_mx(qd), kib=_mx(ki), keb=_mx(ke), vb=_mx(v_ref[rows, :]), dob=_mx(dov))
        units = [(c, h) for c in order for h in range(NH)]
        col = lambda u, key: pre[u[0]][key][:, hs_[u[1]]]
        pt = {u: jnp.where(mbt, _nt(col(u, "kib"), col(u, "qdb")), 0.0) for u in units}
        dp = {u: jnp.where(mb, _nt(col(u, "dob"), col(u, "vb")), 0.0) for u in units}
        dpt = {u: jnp.where(mbt, _nt(col(u, "vb"), col(u, "dob")), 0.0) for u in units}
        dv_i = {u: _nn(pt[u], col(u, "dob")) for u in units}
        dqd_ = {u: _nn(dp[u], col(u, "kib")) + _nn(col(u, "dob"), sp_ref[u[0], u[1]]) for u in units}
        dki_ = {u: _nn(dpt[u], col(u, "qdb")) for u in units}
        dsl = {u: _tn(col(u, "dob"), col(u, "qdb")) for u in units}
        for c in order:
            rows = slice(c * HC, (c + 1) * HC)
            p = pre[c]
            dv_l, dke_l, dtot_l = [], [], []
            for h, cs in enumerate(hs_):
                dso = dst[h]
                dsob = _mx(dso)
                dv_l.append(dv_i[(c, h)] + _nt(p["keb"][:, cs], dsob))
                dke_l.append(_nn(p["vb"][:, cs], dsob))
                dtot_l.append(_colsum(dso * sp_ref[c, h].astype(F32)) * p["etot"][:, cs])
                dst[h] = dso * p["etot"][:, cs] + dsl[(c, h)]
            lb, sg, f, e, ei, ee, qd, ki, ke = (p[n_] for n_ in ("lb", "sg", "f", "e", "ei", "ee", "qd", "ki", "ke"))
            dqd = jnp.concatenate([dqd_[(c, h)] for h in range(NH)], axis=1)
            dki = jnp.concatenate([dki_[(c, h)] for h in range(NH)], axis=1)
            dke = jnp.concatenate(dke_l, axis=1)
            dcum = dqd * qd - dki * ki - dke * ke
            dtot = jnp.concatenate(dtot_l, axis=1) + _colsum(dke * ke)
            dk = dki * ei + dke * ee
            dlf = _dot01(mt01, dcum) + dtot
            df = dlf / f - dk
            dlb_ref[0:1, :] += _colsum(df * (1.0 - sg))
            dfr = df * (1.0 - lb) * sg * (1.0 - sg)
            dq = dqd * e * scale * p["dsq"]
            dv = jnp.concatenate(dv_l, axis=1)
            if last:
                dq = dq + dqp_ref[rows, :]
                dv = dv + dvp_ref[rows, :]
            dq_ref[rows, :] = dq.astype(odt)
            dv_ref[rows, :] = dv.astype(odt)
            df_ref[rows, :] = dfr.astype(MXU_DTYPE)

    blk = lambda s: _blk(nb - 1 - s, nb, rev)
    col = lambda j: (lambda s: (blk(s), j))
    in_specs = [pl.BlockSpec((TB, D), col(0)), pl.BlockSpec((TB, D), col(1 + d)), pl.BlockSpec((TB, D), col(3)),
                _full((8, D)), pl.BlockSpec((nch, NH, HF, HF), lambda s: (blk(s), 0, 0, 0)),
                pl.BlockSpec((TB, D), lambda s: (jnp.minimum(blk(s), nb - 2), 0))]
    args = [p_main, p_main, p_main, lbraw, sprev, do]
    if last:
        in_specs += [pl.BlockSpec((TB, D), col(0))] * 2
        args += list(prev)
    call = dict(
        body=body, args=args, name=f"hgrn_bwd_{d}",
        out_shape=(jax.ShapeDtypeStruct((t_total, D), odt), jax.ShapeDtypeStruct((t_total, D), MXU_DTYPE),
                   jax.ShapeDtypeStruct((t_total, D), odt), jax.ShapeDtypeStruct((8, D), F32)),
        grid=(nb,), in_specs=in_specs,
        out_specs=(pl.BlockSpec((TB, D), col(0)), pl.BlockSpec((TB, D), col(0)), pl.BlockSpec((TB, D), col(0)),
                   _full((8, D))),
        scratch=[pltpu.VMEM((NH, HF, HF), F32)], sem=("arbitrary",), vmem_mb=48)
    return _run(_carry(call, comm, lambda: (pl.program_id(0) == 0, pl.program_id(0) == nb - 1)))


def _conv_masks(tb, is_ctx):
    seg = jnp.where(is_ctx, tb, GRID_W)
    pos = lax.broadcasted_iota(jnp.int32, (tb, 1), 0) & (seg - 1)
    return pos, seg


def _shift_rows(x, dshift, pos, seg):
    if dshift == 0:
        return x
    n = x.shape[0]
    rolled = pltpu.roll(x, (-dshift) % n, 0)
    ok = (pos + dshift >= 0) & (pos + dshift < seg)
    return jnp.where(ok, rolled, 0.0)


def _ssd_prep(p_main, p_dt, convp, dtb, nb):
    t_total = p_main.shape[0]

    def body(x_ref, dt_ref, cw_ref, dtb_ref, xa_ref, ds_ref, dts_ref):
        is_ctx = pl.program_id(0) == nb - 1
        pos, seg = _conv_masks(TB, is_ctx)
        xv = x_ref[...].astype(F32)
        acc = cw_ref[5:6, :] + cw_ref[2:3, :] * xv
        for kk in (0, 1, 3, 4):
            acc = acc + cw_ref[kk:kk + 1, :] * _shift_rows(xv, kk - 2, pos, seg)
        sg = _sig(acc)
        xa_ref[...] = (acc * sg).astype(xa_ref.dtype)
        ds_ref[...] = (sg * (1.0 + acc * (1.0 - sg))).astype(ds_ref.dtype)
        dts_ref[...] = _softplus(dt_ref[...] + dtb_ref[0:1, :])

    wide = pl.BlockSpec((TB, 2048), lambda i: (i, 0))
    return _pcall(
        body, name="ssd_prep",
        out_shape=(jax.ShapeDtypeStruct((t_total, 2048), MXU_DTYPE), jax.ShapeDtypeStruct((t_total, 2048), MXU_DTYPE),
                   jax.ShapeDtypeStruct((t_total, 128), F32)),
        grid=(nb,),
        in_specs=[pl.BlockSpec((TB, 2048), lambda i: (i, 3)), pl.BlockSpec((TB, 128), lambda i: (i, 0)),
                  _full((8, 2048)), _full((8, 128))],
        out_specs=(wide, wide, pl.BlockSpec((TB, 128), lambda i: (i, 0))),
        sem=("parallel",), vmem_mb=32,
    )(p_main, p_dt, convp, dtb)


def _ssd_prep_bwd(p_main, p_dt, convp, dtb, dsl, dxa, dxs_skip, ddts, nb):
    t_total = p_main.shape[0]

    def body(x_ref, dt_ref, cw_ref, dtb_ref, ds_ref, dxa_ref, dsk_ref, ddts_ref, dx_ref, ddt_ref, dcw_ref, ddtb_ref):
        i = pl.program_id(0)
        is_ctx = i == nb - 1

        @pl.when(i == 0)
        def _():
            dcw_ref[...] = jnp.zeros_like(dcw_ref)
            ddtb_ref[...] = jnp.zeros_like(ddtb_ref)

        pos, seg = _conv_masks(TB, is_ctx)
        xv = x_ref[...].astype(F32)
        dact = dxa_ref[...]
        dact = jnp.concatenate([dact[:, :D] + jnp.where(is_ctx, 0.0, dsk_ref[...].astype(F32)), dact[:, D:]], axis=1)
        dpre = dact * ds_ref[...].astype(F32)
        dxv = cw_ref[2:3, :] * dpre
        dcw_ref[2:3, :] += _colsum(xv * dpre)
        for kk in (0, 1, 3, 4):
            sdp = _shift_rows(dpre, 2 - kk, pos, seg)
            dxv = dxv + cw_ref[kk:kk + 1, :] * sdp
            dcw_ref[kk:kk + 1, :] += _colsum(xv * sdp)
        dx_ref[...] = dxv.astype(dx_ref.dtype)
        dcw_ref[5:6, :] += _colsum(dpre)
        draw = ddts_ref[...] * _sig(dt_ref[...] + dtb_ref[0:1, :])
        ddt_ref[...] = draw.astype(ddt_ref.dtype)
        ddtb_ref[0:1, :] += _colsum(draw)

    return _pcall(
        body, name="ssd_prep_bwd",
        out_shape=(jax.ShapeDtypeStruct((t_total, 2048), MXU_DTYPE), jax.ShapeDtypeStruct((t_total, 128), MXU_DTYPE),
                   jax.ShapeDtypeStruct((8, 2048), F32), jax.ShapeDtypeStruct((8, 128), F32)),
        grid=(nb,),
        in_specs=[pl.BlockSpec((TB, 2048), lambda i: (i, 3)), pl.BlockSpec((TB, 128), lambda i: (i, 0)),
                  _full((8, 2048)), _full((8, 128)), pl.BlockSpec((TB, 2048), lambda i: (i, 0)),
                  pl.BlockSpec((TB, 2048), lambda i: (i, 0)),
                  pl.BlockSpec((TB, D), lambda i: (jnp.minimum(i, nb - 2), 0)),
                  pl.BlockSpec((TB, 128), lambda i: (i, 0))],
        out_specs=(pl.BlockSpec((TB, 2048), lambda i: (i, 0)), pl.BlockSpec((TB, 128), lambda i: (i, 0)),
                   _full((8, 2048)), _full((8, 128))),
        sem=("arbitrary",), vmem_mb=40,
    )(p_main, p_dt, convp, dtb, dsl, dxa, dxs_skip, ddts)


def _dot2(x, m01):
    hi = x.astype(BF16)
    lo = (x - hi.astype(F32)).astype(BF16)
    f = lambda t: lax.dot_general(t, m01, (((1,), (0,)), ((), ())), preferred_element_type=F32)
    return f(hi) + f(lo)


def _head_lanes(c0, c1):
    p = lax.broadcasted_iota(jnp.int32, (128, 128), 0)
    l = lax.broadcasted_iota(jnp.int32, (128, 128), 1)
    return _b01(((l == c0) & (p < SP)) | ((l == c1) & (p >= SP)))


def _one_lane(col):
    return _b01(lax.broadcasted_iota(jnp.int32, (128, 128), 1) == col)


def _lane_pick(x, lane, col):
    return _rowsum(jnp.where(lane == col, x, 0.0))


def _ssd_chunk_common(dts, alog_ref, m01, rev):
    lane = lax.broadcasted_iota(jnp.int32, (1, 128), 1)
    arow = -jnp.exp(alog_ref[0:1, :])
    cum = _dot01(m01, dts * arow)
    tot = cum[0:1, :] if rev else cum[SC - 1:SC, :]
    return lane, arow, cum, cum.T, tot


def _ssd_fwd(xa, dts, alog, d, nb):
    t_total = xa.shape[0]
    rev = d == 1
    nch = TB // SC
    npair = SHEADS // 2

    def body(xa_ref, dts_ref, alog_ref, y_ref, sp_ref, st):
        s = pl.program_id(0)

        @pl.when(s == 0)
        def _():
            st[...] = jnp.zeros_like(st)

        mb = _tri(SC, rev)
        m01 = _b01(mb)
        lo = lax.broadcasted_iota(jnp.int32, (1, 128), 1) < SP
        rlo = lax.broadcasted_iota(jnp.int32, (128, 1), 0) < SP
        order = list(reversed(range(nch)) if rev else range(nch))
        pre = {}
        for c in order:
            rows = slice(c * SC, (c + 1) * SC)
            dts_c = dts_ref[rows, :]
            lane, arow, cum, cumt, tot = _ssd_chunk_common(dts_c, alog_ref, m01, rev)
            bgs = [_mx(xa_ref[rows, D + g * SN:D + (g + 1) * SN]) for g in range(4)]
            cgs = [_mx(xa_ref[rows, D + 512 + g * SN:D + 512 + (g + 1) * SN]) for g in range(4)]
            pairs = []
            for pr in range(npair):
                xs = xa_ref[rows, pr * 128:(pr + 1) * 128].astype(F32)
                cols = [16 * d + 2 * pr, 16 * d + 2 * pr + 1]
                cum_c = [_lane_pick(cum, lane, q) for q in cols]
                dt_c = [_lane_pick(dts_c, lane, q) for q in cols]
                tot_c = [_lane_pick(tot, lane, q) for q in cols]
                dtx = xs * jnp.where(lo, dt_c[0], dt_c[1])
                e1_pair = jnp.where(lo, jnp.exp(cum_c[0]), jnp.exp(cum_c[1]))
                e2_pair = jnp.where(lo, jnp.exp(tot_c[0] - cum_c[0]), jnp.exp(tot_c[1] - cum_c[1]))
                etot_col = jnp.where(rlo, jnp.exp(tot_c[0]), jnp.exp(tot_c[1]))
                decs = [jnp.where(mb, jnp.exp(cum_c[q] - cumt[cols[q]:cols[q] + 1, :]), 0.0) for q in range(2)]
                dtxq = [_mx(jnp.where(lo if q == 0 else ~lo, dtx, 0.0)) for q in range(2)]
                pairs.append(dict(e1=e1_pair, etot=etot_col, decs=decs, dtxq=dtxq, xe=_mx(dtx * e2_pair)))
            pre[c] = (bgs, cgs, pairs)
        gm = {(c, g): _nt(pre[c][1][g], pre[c][0][g]) for c in order for g in range(4)}
        upd = {(c, pr): _tn(pre[c][2][pr]["xe"], pre[c][0][pr // 2]) for c in order for pr in range(npair)}
        intra = {(c, pr): sum(_nn(gm[(c, pr // 2)] * pre[c][2][pr]["decs"][q], pre[c][2][pr]["dtxq"][q]) for q in range(2))
                 for c in order for pr in range(npair)}
        for c in order:
            rows = slice(c * SC, (c + 1) * SC)
            bgs, cgs, pairs = pre[c]
            for pr in range(npair):
                stp = st[pr]
                stb = stp.astype(sp_ref.dtype)
                sp_ref[c, pr] = stb
                y_ref[rows, pr * 128:(pr + 1) * 128] = (
                    intra[(c, pr)] + pairs[pr]["e1"] * _nt(cgs[pr // 2], stb)).astype(y_ref.dtype)
                st[pr] = stp * pairs[pr]["etot"] + upd[(c, pr)]

    blk = lambda s: _blk(s, nb, rev)
    return _pcall(
        body, name=f"ssd_fwd_{d}",
        out_shape=(jax.ShapeDtypeStruct((t_total, D), MXU_DTYPE),
                   jax.ShapeDtypeStruct((nch * nb, npair, 128, SN), MXU_DTYPE)),
        grid=(nb,),
        in_specs=[pl.BlockSpec((TB, 2048), lambda s: (blk(s), 0)), pl.BlockSpec((TB, 128), lambda s: (blk(s), 0)),
                  _full((8, 128))],
        out_specs=(pl.BlockSpec((TB, D), lambda s: (blk(s), 0)),
                   pl.BlockSpec((nch, npair, 128, SN), lambda s: (blk(s), 0, 0, 0))),
        scratch=[pltpu.VMEM((npair, 128, SN), F32)], sem=("arbitrary",), vmem_mb=40,
    )(xa, dts, alog)


def _ssd_bwd(xa, dts, alog, sprev, dy, d, nb, prev, comm=None):
    t_total = xa.shape[0]
    rev = d == 1
    nch = TB // SC
    npair = SHEADS // 2
    last = prev is not None

    def body(xa_ref, dts_ref, alog_ref, sp_ref, dy_ref, *rest):
        if last:
            dxp_ref, ddp_ref = rest[:2]
            rest = rest[2:]
        dxa_ref, ddts_ref, da_ref, dst, zc_scr = rest
        sp_id = pl.program_id(0)
        is_ctx = sp_id == nb - 1

        @pl.when(sp_id == 0)
        def _():
            dst[...] = jnp.zeros_like(dst)
            da_ref[...] = jnp.zeros_like(da_ref)
            zc_scr[...] = jnp.zeros_like(zc_scr)

        mb = _tri(SC, rev)
        m01 = _b01(mb)
        mt01 = _b01(_tri(SC, not rev))
        lo = lax.broadcasted_iota(jnp.int32, (1, 128), 1) < SP
        rlo = lax.broadcasted_iota(jnp.int32, (128, 1), 0) < SP
        order = list(range(nch) if rev else reversed(range(nch)))
        pre = {}
        for c in order:
            rows = slice(c * SC, (c + 1) * SC)
            dts_c = dts_ref[rows, :]
            lane, arow, cum, cumt, tot = _ssd_chunk_common(dts_c, alog_ref, m01, rev)
            pairs = []
            for pr in range(npair):
                xs = xa_ref[rows, pr * 128:(pr + 1) * 128].astype(F32)
                dyp = jnp.where(is_ctx, 0.0, dy_ref[rows, pr * 128:(pr + 1) * 128].astype(F32))
                cols = [16 * d + 2 * pr, 16 * d + 2 * pr + 1]
                cum_c = [_lane_pick(cum, lane, q) for q in cols]
                dt_c = [_lane_pick(dts_c, lane, q) for q in cols]
                tot_c = [_lane_pick(tot, lane, q) for q in cols]
                e1_c = [jnp.exp(cum_c[q]) for q in range(2)]
                e2_c = [jnp.exp(tot_c[q] - cum_c[q]) for q in range(2)]
                etot_c = [jnp.exp(tot_c[q]) for q in range(2)]
                dt_pair = jnp.where(lo, dt_c[0], dt_c[1])
                e1_pair = jnp.where(lo, e1_c[0], e1_c[1])
                e2_pair = jnp.where(lo, e2_c[0], e2_c[1])
                dtx = xs * dt_pair
                decs = [jnp.where(mb, jnp.exp(cum_c[q] - cumt[cols[q]:cols[q] + 1, :]), 0.0) for q in range(2)]
                dyq = [_mx(jnp.where(lo if q == 0 else ~lo, dyp, 0.0)) for q in range(2)]
                pairs.append(dict(xs=xs, dyp=dyp, cols=cols, e1_c=e1_c, e2_c=e2_c, etot_c=etot_c, dt_pair=dt_pair,
                                  e2_pair=e2_pair, etot_col=jnp.where(rlo, etot_c[0], etot_c[1]), dtx=dtx,
                                  dtxb=_mx(dtx), xeb=_mx(dtx * e2_pair), dy0b=_mx(dyp * e1_pair), decs=decs, dyq=dyq))
            pre[c] = dict(lane=lane, arow=arow, dts=dts_c, pairs=pairs, cum=cum, tot=tot,
                          bgb=[_mx(xa_ref[rows, D + g * SN:D + (g + 1) * SN]) for g in range(4)],
                          cgb=[_mx(xa_ref[rows, D + 512 + g * SN:D + 512 + (g + 1) * SN]) for g in range(4)])
        units = [(c, pr) for c in order for pr in range(npair)]
        P = lambda u: pre[u[0]]["pairs"][u[1]]
        cgu = lambda u: pre[u[0]]["cgb"][u[1] // 2]
        gm = {(c, g): _nt(pre[c]["cgb"][g], pre[c]["bgb"][g]) for c in order for g in range(4)}
        y0 = {u: _nt(cgu(u), sp_ref[u[0], u[1]]) for u in units}
        dcg_i = {u: _nn(P(u)["dy0b"], sp_ref[u[0], u[1]]) for u in units}
        dsl = {u: _tn(P(u)["dy0b"], cgu(u)) for u in units}
        w_ = {(u, q): gm[(u[0], u[1] // 2)] * P(u)["decs"][q] for u in units for q in range(2)}
        dw_ = {(u, q): jnp.where(mb, _nt(P(u)["dyq"][q], P(u)["dtxb"]), 0.0) for u in units for q in range(2)}
        ddtx_i = {(u, q): _tn(w_[(u, q)], P(u)["dyq"][q]) for u in units for q in range(2)}
        for c in order:
            rows = slice(c * SC, (c + 1) * SC)
            pc = pre[c]
            lane, arow, dts_c = pc["lane"], pc["arow"], pc["dts"]
            d1 = jnp.zeros((SC, 128), F32)
            d2 = jnp.zeros((SC, 128), F32)
            dz = jnp.zeros((SC, 128), F32)
            ddt = jnp.zeros((SC, 128), F32)
            dtot = jnp.zeros((1, 128), F32)
            dgm = [jnp.zeros((SC, SC), F32) for _ in range(4)]
            dbg = [jnp.zeros((SC, SN), F32) for _ in range(4)]
            dcg = [jnp.zeros((SC, SN), F32) for _ in range(4)]
            for pr in range(npair):
                u, g, p = (c, pr), pr // 2, pc["pairs"][pr]
                hs = _head_lanes(*p["cols"])
                dso = dst[pr]
                dsob = _mx(dso)
                dxe = _nt(pc["bgb"][g], dsob)
                dbg[g] = dbg[g] + _nn(p["xeb"], dsob)
                ddtx = dxe * p["e2_pair"]
                d2 = d2 + _dot2(dxe * p["dtx"], hs)
                dcg[g] = dcg[g] + dcg_i[u]
                d1 = d1 + _dot2(p["dyp"] * y0[u], hs)
                sprod = dso * sp_ref[c, pr].astype(F32)
                dst[pr] = dso * p["etot_col"] + dsl[u]
                for q in range(2):
                    hm = lo if q == 0 else ~lo
                    col = p["cols"][q]
                    dw = dw_[(u, q)]
                    ddtx = ddtx + jnp.where(hm, ddtx_i[(u, q)], 0.0)
                    dgm[g] = dgm[g] + dw * p["decs"][q]
                    z = dw * w_[(u, q)]
                    dz = dz + _dot2(z, _one_lane(col))
                    zc_scr[col:col + 1, :] = _colsum(z)
                    tsum = _rowsum(_colsum(sprod[q * SP:(q + 1) * SP, :]))
                    dtot = jnp.where(lane == col, tsum * p["etot_c"][q], dtot)
                dxs = ddtx * p["dt_pair"]
                ddt = ddt + _dot2(ddtx * p["xs"], hs)
                if last:
                    dxs = dxs + dxp_ref[rows, pr * 128:(pr + 1) * 128]
                dxa_ref[rows, pr * 128:(pr + 1) * 128] = dxs
            e2_all = jnp.exp(pc["tot"] - pc["cum"])
            dcum = dz - zc_scr[...].T + d1 * jnp.exp(pc["cum"]) - d2 * e2_all
            dtot = dtot + _colsum(d2 * e2_all)
            for g in range(4):
                db = dbg[g] + _tn(dgm[g], pc["cgb"][g])
                dc = dcg[g] + _nn(dgm[g], pc["bgb"][g])
                if last:
                    db = db + dxp_ref[rows, D + g * SN:D + (g + 1) * SN]
                    dc = dc + dxp_ref[rows, D + 512 + g * SN:D + 512 + (g + 1) * SN]
                dxa_ref[rows, D + g * SN:D + (g + 1) * SN] = db
                dxa_ref[rows, D + 512 + g * SN:D + 512 + (g + 1) * SN] = dc
            dla = _dot01(mt01, dcum) + dtot
            ddt = ddt + dla * arow
            da_ref[0:1, :] += _colsum(dla * dts_c)
            if last:
                ddt = ddt + ddp_ref[rows, :]
            ddts_ref[rows, :] = ddt

    blk = lambda s: _blk(nb - 1 - s, nb, rev)
    in_specs = [pl.BlockSpec((TB, 2048), lambda s: (blk(s), 0)), pl.BlockSpec((TB, 128), lambda s: (blk(s), 0)),
                _full((8, 128)), pl.BlockSpec((nch, npair, 128, SN), lambda s: (blk(s), 0, 0, 0)),
                pl.BlockSpec((TB, D), lambda s: (jnp.minimum(blk(s), nb - 2), 0))]
    args = [xa, dts, alog, sprev, dy]
    if last:
        in_specs += [pl.BlockSpec((TB, 2048), lambda s: (blk(s), 0)), pl.BlockSpec((TB, 128), lambda s: (blk(s), 0))]
        args += list(prev)
    call = dict(
        body=body, args=args, name=f"ssd_bwd_{d}",
        out_shape=(jax.ShapeDtypeStruct((t_total, 2048), F32), jax.ShapeDtypeStruct((t_total, 128), F32),
                   jax.ShapeDtypeStruct((8, 128), F32)),
        grid=(nb,), in_specs=in_specs,
        out_specs=(pl.BlockSpec((TB, 2048), lambda s: (blk(s), 0)), pl.BlockSpec((TB, 128), lambda s: (blk(s), 0)),
                   _full((8, 128))),
        scratch=[pltpu.VMEM((npair, 128, SN), F32), pltpu.VMEM((128, 128), F32)], sem=("arbitrary",), vmem_mb=48)
    return _run(_carry(call, comm, lambda: (pl.program_id(0) == 0, pl.program_id(0) == nb - 1)))


def _readout(o, g, yy, z, vec_ref):
    hg, ss, keep = [], [], []
    for h in range(NH):
        cs = slice(h * HF, (h + 1) * HF)
        oh = o[:, cs]
        r = lax.rsqrt(jnp.mean(oh * oh, axis=1, keepdims=True) + EPS)
        hg.append(oh * r * vec_ref[0:1, cs] * _silu(g[:, cs]))
        keep.append(r)
    u = yy * _silu(z)
    for gi in range(4):
        cs = slice(gi * 256, (gi + 1) * 256)
        ug = u[:, cs]
        r = lax.rsqrt(jnp.mean(ug * ug, axis=1, keepdims=True) + EPS)
        ss.append(ug * r * vec_ref[2:3, cs])
        keep.append(r)
    return jnp.concatenate(hg, axis=1), jnp.concatenate(ss, axis=1), keep, u


def _mix_out(o_f, o_b, p_main, y_f, y_b, xa, x, vecs, w_out):
    n = x.shape[0]

    def body(of_ref, ob_ref, g_ref, z_ref, yf_ref, yb_ref, xs_ref, x_ref, vec_ref, w_ref,
             ymix_ref, ylat_ref, h1_ref, u2_ref):
        o = of_ref[...].astype(F32) + ob_ref[...].astype(F32)
        yy = yf_ref[...].astype(F32) + yb_ref[...].astype(F32) + vec_ref[1:2, :] * xs_ref[...].astype(F32)
        hg, ss, _, _ = _readout(o, g_ref[...].astype(F32), yy, z_ref[...].astype(F32), vec_ref)
        ymix = jnp.concatenate([hg, ss], axis=1).astype(MXU_DTYPE)
        ymix_ref[...] = ymix
        ylat = _nn(ymix, w_ref[...])
        ylat_ref[...] = ylat
        h1 = x_ref[...] + vec_ref[3:4, :] * ylat
        h1_ref[...] = h1
        r = lax.rsqrt(jnp.mean(h1 * h1, axis=1, keepdims=True) + EPS)
        u2_ref[...] = ((h1 * r * vec_ref[6:7, :]) * vec_ref[4:5, :] + vec_ref[5:6, :]).astype(MXU_DTYPE)

    row = lambda j: (lambda i: (i, j))
    return _pcall(
        body, name="mix_out",
        out_shape=(jax.ShapeDtypeStruct((n, 2 * D), MXU_DTYPE), jax.ShapeDtypeStruct((n, D), F32),
                   jax.ShapeDtypeStruct((n, D), F32), jax.ShapeDtypeStruct((n, D), MXU_DTYPE)),
        grid=(n // TB,),
        in_specs=[pl.BlockSpec((TB, D), row(0)), pl.BlockSpec((TB, D), row(0)), pl.BlockSpec((TB, D), row(4)),
                  pl.BlockSpec((TB, D), row(5)), pl.BlockSpec((TB, D), row(0)), pl.BlockSpec((TB, D), row(0)),
                  pl.BlockSpec((TB, D), row(0)), pl.BlockSpec((TB, D), row(0)), _full((8, D)), _full((2 * D, D))],
        out_specs=(pl.BlockSpec((TB, 2 * D), row(0)), pl.BlockSpec((TB, D), row(0)), pl.BlockSpec((TB, D), row(0)),
                   pl.BlockSpec((TB, D), row(0))),
        sem=("parallel",), vmem_mb=48,
    )(o_f, o_b, p_main, p_main, y_f, y_b, xa, x, vecs, w_out)


def _mix_bwd(dylat, o_f, o_b, p_main, y_f, y_b, xa, vecs, w_out, comm=None):
    n = dylat.shape[0]
    t_total = p_main.shape[0]
    nlat = n // TB

    def body(*refs):
        dg_ref, dz_ref, acc_ref = refs[11], refs[13], refs[15]
        i = pl.program_id(0)

        @pl.when(i == 0)
        def _():
            acc_ref[...] = jnp.zeros_like(acc_ref)

        @pl.when(i < nlat)
        def _():
            compute(*refs)

        @pl.when(i == nlat)
        def _():
            dg_ref[...] = jnp.zeros_like(dg_ref)
            dz_ref[...] = jnp.zeros_like(dz_ref)

    def compute(dyl_ref, of_ref, ob_ref, g_ref, z_ref, yf_ref, yb_ref, xs_ref, vec_ref, w_ref,
                do_ref, dg_ref, dys_ref, dz_ref, dxs_ref, acc_ref):
        dymix = _nt(dyl_ref[...], w_ref[...])
        o = of_ref[...].astype(F32) + ob_ref[...].astype(F32)
        g = g_ref[...].astype(F32)
        z = z_ref[...].astype(F32)
        xs = xs_ref[...].astype(F32)
        yy = yf_ref[...].astype(F32) + yb_ref[...].astype(F32) + vec_ref[1:2, :] * xs
        _, _, keep, u = _readout(o, g, yy, z, vec_ref)
        do_l, dg_l = [], []
        for h in range(NH):
            cs = slice(h * HF, (h + 1) * HF)
            oh, gh, r, wv = o[:, cs], g[:, cs], keep[h], vec_ref[0:1, cs]
            dhg = dymix[:, cs]
            xh = oh * r
            dn = dhg * _silu(gh)
            dg_l.append(dhg * xh * wv * _dsilu(gh))
            acc_ref[0:1, cs] += _colsum(dn * xh)
            dxh = dn * wv
            do_l.append(r * (dxh - xh * jnp.mean(dxh * xh, axis=1, keepdims=True)))
        du_l = []
        for gi in range(4):
            cs = slice(gi * 256, (gi + 1) * 256)
            ug, r, wv = u[:, cs], keep[NH + gi], vec_ref[2:3, cs]
            dss = dymix[:, D + gi * 256:D + (gi + 1) * 256]
            xh = ug * r
            acc_ref[2:3, cs] += _colsum(dss * xh)
            dxh = dss * wv
            du_l.append(r * (dxh - xh * jnp.mean(dxh * xh, axis=1, keepdims=True)))
        du = jnp.concatenate(du_l, axis=1)
        dyy = du * _silu(z)
        do_ref[...] = jnp.concatenate(do_l, axis=1).astype(do_ref.dtype)
        dg_ref[...] = jnp.concatenate(dg_l, axis=1).astype(dg_ref.dtype)
        dys_ref[...] = dyy.astype(dys_ref.dtype)
        dz_ref[...] = (du * yy * _dsilu(z)).astype(dz_ref.dtype)
        dxs_ref[...] = (dyy * vec_ref[1:2, :]).astype(dxs_ref.dtype)
        acc_ref[1:2, :] += _colsum(dyy * xs)

    row = lambda j: (lambda i: (jnp.minimum(i, nlat - 1), j))
    lat = pl.BlockSpec((TB, D), row(0))
    tok = pl.BlockSpec((TB, D), lambda i: (i, 0))
    call = dict(
        body=body, args=[dylat, o_f, o_b, p_main, p_main, y_f, y_b, xa, vecs, w_out], name="mix_bwd",
        out_shape=(jax.ShapeDtypeStruct((n, D), MXU_DTYPE), jax.ShapeDtypeStruct((t_total, D), MXU_DTYPE),
                   jax.ShapeDtypeStruct((n, D), MXU_DTYPE), jax.ShapeDtypeStruct((t_total, D), MXU_DTYPE),
                   jax.ShapeDtypeStruct((n, D), MXU_DTYPE), jax.ShapeDtypeStruct((8, D), F32)),
        grid=(t_total // TB,),
        in_specs=[lat, lat, lat, pl.BlockSpec((TB, D), row(4)), pl.BlockSpec((TB, D), row(5)), lat, lat, lat,
                  _full((8, D)), _full((2 * D, D))],
        out_specs=(lat, tok, lat, tok, lat, _full((8, D))), scratch=[],
        sem=("arbitrary",), vmem_mb=48)
    return _run(_carry(call, comm, lambda: (pl.program_id(0) == 0, pl.program_id(0) == t_total // TB - 1)))


def _ffn_up(u2, w_gate, w_up):
    n = u2.shape[0]
    tb = 1024

    def body(u_ref, wg_ref, wu_ref, g_ref, up_ref, a_ref):
        uv = u_ref[...]
        gt = _nt(uv, wg_ref[...])
        upv = _nt(uv, wu_ref[...])
        g_ref[...] = gt.astype(g_ref.dtype)
        up_ref[...] = upv.astype(up_ref.dtype)
        a_ref[...] = (_silu(gt) * upv).astype(a_ref.dtype)

    blk = pl.BlockSpec((tb, FSL), lambda j, i: (i, j))
    wblk = pl.BlockSpec((None, FSL, D), lambda j, i: (j, 0, 0))
    return _pcall(
        body, name="ffn_up",
        out_shape=(jax.ShapeDtypeStruct((n, DFFP), MXU_DTYPE),) * 3,
        grid=(4, n // tb), in_specs=[pl.BlockSpec((tb, D), lambda j, i: (i, 0)), wblk, wblk],
        out_specs=(blk, blk, blk), sem=("parallel", "parallel"), vmem_mb=48,
    )(u2, w_gate, w_up)


def _ffn_down_loss(act, w_down, h1, tgt, vecs):
    n = act.shape[0]
    tb = 512

    def body(a_ref, w_ref, h1_ref, t_ref, vec_ref, dh2_ref, dffn_ref, acc_ref):
        i = pl.program_id(0)

        @pl.when(i == 0)
        def _():
            acc_ref[...] = jnp.zeros_like(acc_ref)

        g2 = vec_ref[0:1, :]
        fw = vec_ref[1:2, :]
        nsub = 4
        sb = tb // nsub
        wv = w_ref[...]
        ffns = [_nn(a_ref[r_ * sb:(r_ + 1) * sb, :], wv) for r_ in range(nsub)]
        for r_ in range(nsub):
            rows = slice(r_ * sb, (r_ + 1) * sb)
            ffn = ffns[r_]
            h2 = h1_ref[rows, :] + g2 * ffn
            r = lax.rsqrt(jnp.mean(h2 * h2, axis=1, keepdims=True) + EPS)
            xh = h2 * r
            err = xh * fw - t_ref[rows, :]
            dy = err * (1.0 / D)
            acc_ref[2:3, :] += _colsum(err * err) * (0.5 / D)
            acc_ref[1:2, :] += _colsum(dy * xh)
            dxh = dy * fw
            dh2 = r * (dxh - xh * jnp.mean(dxh * xh, axis=1, keepdims=True))
            dh2_ref[rows, :] = dh2
            dffn_ref[rows, :] = (g2 * dh2).astype(dffn_ref.dtype)
            acc_ref[0:1, :] += _colsum(dh2 * ffn)

    return _pcall(
        body, name="ffn_down_loss",
        out_shape=(jax.ShapeDtypeStruct((n, D), F32), jax.ShapeDtypeStruct((n, D), MXU_DTYPE),
                   jax.ShapeDtypeStruct((8, D), F32)),
        grid=(n // tb,),
        in_specs=[pl.BlockSpec((tb, DFFP), lambda i: (i, 0)), _full((DFFP, D)), pl.BlockSpec((tb, D), lambda i: (i, 0)),
                  pl.BlockSpec((tb, D), lambda i: (i, 0)), _full((8, D))],
        out_specs=(pl.BlockSpec((tb, D), lambda i: (i, 0)), pl.BlockSpec((tb, D), lambda i: (i, 0)), _full((8, D))),
        sem=("arbitrary",), vmem_mb=48,
    )(act, w_down, h1, tgt, vecs)


def _ffn_bwd(dffn, w_down, gate, up, w_gate_t, w_up_t):
    n = dffn.shape[0]
    tb = 1024

    def body(df_ref, wd_ref, g_ref, up_ref, wg_ref, wu_ref, dg_ref, dup_ref, du_ref):
        j = pl.program_id(1)
        nsub = 4
        sb = tb // nsub
        wd, wg, wu = wd_ref[...], wg_ref[...], wu_ref[...]
        dacts = [_nt(df_ref[r * sb:(r + 1) * sb, :], wd) for r in range(nsub)]
        parts = []
        for r in range(nsub):
            rows = slice(r * sb, (r + 1) * sb)
            gt = g_ref[rows, :].astype(F32)
            upv = up_ref[rows, :].astype(F32)
            sg = _sig(gt)
            dgt = (dacts[r] * upv * (sg * (1.0 + gt * (1.0 - sg)))).astype(MXU_DTYPE)
            dupv = (dacts[r] * (gt * sg)).astype(MXU_DTYPE)
            dg_ref[rows, :] = dgt
            dup_ref[rows, :] = dupv
            parts.append(_nn(dgt, wg) + _nn(dupv, wu))
        part = jnp.concatenate(parts, axis=0)

        @pl.when(j == 0)
        def _():
            du_ref[...] = part

        @pl.when(j > 0)
        def _():
            du_ref[...] += part

    tok = pl.BlockSpec((tb, D), lambda i, j: (i, 0))
    ffb = pl.BlockSpec((tb, FSL), lambda i, j: (i, j))
    wsl = pl.BlockSpec((None, FSL, D), lambda i, j: (j, 0, 0))
    return _pcall(
        body, name="ffn_bwd",
        out_shape=(jax.ShapeDtypeStruct((n, DFFP), MXU_DTYPE), jax.ShapeDtypeStruct((n, DFFP), MXU_DTYPE),
                   jax.ShapeDtypeStruct((n, D), F32)),
        grid=(n // tb, 4),
        in_specs=[tok, pl.BlockSpec((FSL, D), lambda i, j: (j, 0)), ffb, ffb, wsl, wsl],
        out_specs=(ffb, ffb, tok), sem=("parallel", "arbitrary"), vmem_mb=48,
    )(dffn, w_down, gate, up, w_gate_t, w_up_t)


def _ffn_norm_bwd(du, h1, ylat, dh2, vecs):
    n = du.shape[0]
    tb = 512

    def body(du_ref, h1_ref, yl_ref, dh2_ref, vec_ref, dh1_ref, dyl_ref, acc_ref):
        @pl.when(pl.program_id(0) == 0)
        def _():
            acc_ref[...] = jnp.zeros_like(acc_ref)

        duv = du_ref[...]
        h1 = h1_ref[...]
        r = lax.rsqrt(jnp.mean(h1 * h1, axis=1, keepdims=True) + EPS)
        xh = h1 * r
        nw = vec_ref[2:3, :]
        acc_ref[0:1, :] += _colsum(duv)
        acc_ref[1:2, :] += _colsum(duv * xh * nw)
        dn = duv * vec_ref[1:2, :]
        acc_ref[2:3, :] += _colsum(dn * xh)
        dxh = dn * nw
        dh1 = dh2_ref[...] + r * (dxh - xh * jnp.mean(dxh * xh, axis=1, keepdims=True))
        dh1_ref[...] = dh1
        dyl_ref[...] = (vec_ref[0:1, :] * dh1).astype(dyl_ref.dtype)
        acc_ref[3:4, :] += _colsum(dh1 * yl_ref[...])

    tok = pl.BlockSpec((tb, D), lambda i: (i, 0))
    return _pcall(
        body, name="ffn_norm_bwd",
        out_shape=(jax.ShapeDtypeStruct((n, D), F32), jax.ShapeDtypeStruct((n, D), MXU_DTYPE),
                   jax.ShapeDtypeStruct((8, D), F32)),
        grid=(n // tb,), in_specs=[tok, tok, tok, tok, _full((8, D))], out_specs=(tok, tok, _full((8, D))),
        sem=("arbitrary",), vmem_mb=40,
    )(du, h1, ylat, dh2, vecs)


def _deep_rows(rows):
    return max(r for r in range(128, 2305, 128) if rows % r == 0)


def _dw(a, b, name):
    tn_rows = a.shape[0]
    bt = _deep_rows(tn_rows)
    kk, nn_ = a.shape[1], b.shape[1]
    bk = 1024 if kk % 1024 == 0 else kk
    bn = 1024 if nn_ % 1024 == 0 else nn_
    nt = tn_rows // bt

    def body(a_ref, b_ref, o_ref, acc):
        t = pl.program_id(2)
        part = _tn(a_ref[...], b_ref[...])

        @pl.when(t == 0)
        def _():
            acc[...] = part

        @pl.when(t > 0)
        def _():
            acc[...] += part

        @pl.when(t == nt - 1)
        def _():
            o_ref[...] = acc[...].astype(o_ref.dtype)

    return _pcall(
        body, name=name, out_shape=jax.ShapeDtypeStruct((kk, nn_), MXU_DTYPE), grid=(kk // bk, nn_ // bn, nt),
        in_specs=[pl.BlockSpec((bt, bk), lambda i, j, t: (t, i)), pl.BlockSpec((bt, bn), lambda i, j, t: (t, j))],
        out_specs=pl.BlockSpec((bk, bn), lambda i, j, t: (i, j)), scratch=[pltpu.VMEM((bk, bn), F32)],
        sem=("parallel", "parallel", "arbitrary"), vmem_mb=40,
    )(a, b)


def _dw_in(segs, u_all, name):
    tiles = []
    for m, s_ in enumerate(segs):
        tiles += [(m, h) for h in range(s_.shape[1] // D)]
    ntile = len(tiles)
    t_total = u_all.shape[0]
    bt = _deep_rows(t_total)
    nt = t_total // bt

    def body(u_ref, *refs):
        seg_refs, o_ref, acc = refs[:len(segs)], refs[len(segs)], refs[len(segs) + 1]
        n, t = pl.program_id(0), pl.program_id(1)
        for k, (m, _) in enumerate(tiles):
            @pl.when(n == k)
            def _(m=m):
                part = _tn(seg_refs[m][...], u_ref[...])

                @pl.when(t == 0)
                def _():
                    acc[...] = part

                @pl.when(t > 0)
                def _():
                    acc[...] += part

        @pl.when(t == nt - 1)
        def _():
            o_ref[...] = acc[...].astype(o_ref.dtype)

    def seg_spec(m):
        ks = [k for k, (mm, _) in enumerate(tiles) if mm == m]
        lo, hi = ks[0], ks[-1]
        on = lambda n: (n >= lo) & (n <= hi)
        return pl.BlockSpec((bt, D), lambda n, t: (jnp.where(on(n), t, 0), jnp.where(on(n), n - lo, 0)))

    return _pcall(
        body, name=name, out_shape=jax.ShapeDtypeStruct((1, ntile * D, D), MXU_DTYPE), grid=(ntile, nt),
        in_specs=[pl.BlockSpec((bt, D), lambda n, t: (t, 0))] + [seg_spec(m) for m in range(len(segs))],
        out_specs=pl.BlockSpec((None, D, D), lambda n, t: (0, n, 0)),
        scratch=[pltpu.VMEM((D, D), F32)], sem=("parallel", "arbitrary"), vmem_mb=56,
    )(u_all, *segs)


def _du_prenorm_bwd(segs, ddt, wi_main, wi_tail, xin, mods, dres, row_off, tb, name, comm=None):
    n = xin.shape[0]
    nt = n // tb
    off = row_off // tb
    has_dx = dres is not None

    def body(*refs):
        seg_refs = refs[:7]
        ddt_ref, w_ref, wb_ref, wdt_ref, x_ref, mod_ref = refs[7:13]
        rest = refs[13:]
        if has_dx:
            dres_ref, dx_ref, acc_ref, du_scr = rest
        else:
            acc_ref, du_scr = rest
        j, i = pl.program_id(0), pl.program_id(1)
        rows = pl.ds(pl.multiple_of(i * tb, tb), tb)

        @pl.when((i == 0) & (j == 0))
        def _():
            acc_ref[...] = jnp.zeros_like(acc_ref)

        @pl.when(j == 0)
        def _():
            du_scr[rows, :] = _nn(ddt_ref[...], wdt_ref[...])

        for k in range(8):
            if not has_dx and k in (4, 5):
                continue

            @pl.when(j == k)
            def _(k=k):
                sv = seg_refs[min(k, 6)][...]
                part = _nn(sv, w_ref[...])
                if k in (2, 4, 6):
                    part = part + _nn(sv[:, 0:WTAIL], wb_ref[...])
                du_scr[rows, :] += part

        @pl.when(j == 7)
        def _():
            du = du_scr[rows, :]
            xv = x_ref[...]
            r = lax.rsqrt(jnp.mean(xv * xv, axis=1, keepdims=True) + EPS)
            xh = xv * r
            nw = mod_ref[1:2, :]
            acc_ref[0:1, :] += _colsum(du)
            acc_ref[1:2, :] += _colsum(du * xh * nw)
            dn = du * mod_ref[0:1, :]
            acc_ref[2:3, :] += _colsum(dn * xh)
            if has_dx:
                dxh = dn * nw
                dx_ref[...] = dres_ref[...] + r * (dxh - xh * jnp.mean(dxh * xh, axis=1, keepdims=True))

    def seg_spec(k):
        if k < 6:
            return pl.BlockSpec((tb, D), lambda j, i: (jnp.where(j == k, i + off, 0), 0))
        return pl.BlockSpec((tb, D), lambda j, i: (jnp.where(j >= 6, i + off, 0), jnp.where(j >= 6, j - 6, 0)))

    last = pl.BlockSpec((tb, D), lambda j, i: (jnp.where(j == 7, i, 0), 0))
    in_specs = [seg_spec(k) for k in range(7)]
    in_specs += [pl.BlockSpec((tb, 128), lambda j, i: (jnp.where(j == 0, i + off, 0), 0))] + _w_specs()
    in_specs += [last, _full((8, D))]
    args = list(segs) + [ddt, wi_main, wi_tail, wi_tail, xin, mods]
    out_shape = [jax.ShapeDtypeStruct((8, D), F32)]
    out_specs = [_full((8, D))]
    if has_dx:
        in_specs.append(last)
        args.append(dres)
        out_shape.insert(0, jax.ShapeDtypeStruct((n, D), F32))
        out_specs.insert(0, last)
    call = dict(body=body, args=args, name=name, out_shape=tuple(out_shape), grid=(8, nt), in_specs=in_specs,
                out_specs=tuple(out_specs), scratch=[pltpu.VMEM((n, D), F32)], sem=("arbitrary", "arbitrary"),
                vmem_mb=56)
    steps = lambda: ((pl.program_id(0) == 0) & (pl.program_id(1) == 0),
                     (pl.program_id(0) == 7) & (pl.program_id(1) == nt - 1))
    return _run(_carry(call, comm, steps))


def _sum8(v):
    def body(v_ref, o_ref):
        acc = v_ref[0]
        for k in range(1, 8):
            acc = acc + v_ref[k]
        o_ref[...] = acc

    return _pcall(body, name="small_sum", out_shape=jax.ShapeDtypeStruct(v.shape[1:], F32),
                  in_specs=[pl.BlockSpec(memory_space=pltpu.VMEM)], out_specs=pl.BlockSpec(memory_space=pltpu.VMEM))(v)


def _adamw(w, m, v, g, name):
    lead = w.ndim == 3
    rows, cols = w.shape[-2:]
    rb = 256 if rows % 256 == 0 else (352 if rows % 352 == 0 else rows)
    c1 = 1.0 - B1 ** STEP
    c2 = 1.0 - B2 ** STEP

    def body(w_ref, m_ref, v_ref, g_ref, d_ref, nm_ref, nv_ref):
        gv = g_ref[...]
        mn = B1 * m_ref[...] + (1.0 - B1) * gv
        vn = B2 * v_ref[...] + (1.0 - B2) * (gv * gv)
        nm_ref[...] = mn
        nv_ref[...] = vn
        d_ref[...] = -LR * ((mn / c1) / (jnp.sqrt(vn / c2) + AEPS) + WD * w_ref[...])

    if rb == rows and rows > 1024:
        cb, steps = 256, cols // 256
        gspec = pl.BlockSpec((rows, cb), lambda i: (0, i))
        spec = pl.BlockSpec((None, rows, cb), lambda i: (0, 0, i)) if lead else gspec
    else:
        steps = rows // rb
        gspec = pl.BlockSpec((rb, cols), lambda i: (i, 0))
        spec = pl.BlockSpec((None, rb, cols), lambda i: (0, i, 0)) if lead else gspec
    return _pcall(
        body, name=name, out_shape=(jax.ShapeDtypeStruct(w.shape, F32),) * 3, grid=(steps,),
        in_specs=[spec] * 3 + [gspec], out_specs=(spec,) * 3, sem=("parallel",), vmem_mb=40,
    )(w, m, v, g)


def _rows(v, n):
    f = v.reshape(-1)
    return jnp.pad(f, (0, n * D - f.shape[0])).reshape(n, D)


def kernel(x, c, ctx, c_ctx, w_ada, b_ada, norm_mix, w_in, conv_w, conv_b, ssd_a_log, ssd_dt_bias, ssd_d, ssd_norm, hgrn_lb_raw, hgrn_norm, w_out, norm_ffn, w_gate, w_up, w_down, final_norm, loss_target, m_c_ctx, m_w_ada, m_b_ada, m_norm_mix, m_w_in, m_conv_w, m_conv_b, m_ssd_a_log, m_ssd_dt_bias, m_ssd_d, m_ssd_norm, m_hgrn_lb_raw, m_hgrn_norm, m_w_out, m_norm_ffn, m_w_gate, m_w_up, m_w_down, m_final_norm, v_c_ctx, v_w_ada, v_b_ada, v_norm_mix, v_w_in, v_conv_w, v_conv_b, v_ssd_a_log, v_ssd_dt_bias, v_ssd_d, v_ssd_norm, v_hgrn_lb_raw, v_hgrn_norm, v_w_out, v_norm_ffn, v_w_gate, v_w_up, v_w_down, v_final_norm):
    ix, iy, ic = lax.axis_index("x"), lax.axis_index("y"), lax.axis_index("c")
    chip = 2 * ix + iy
    me = 2 * chip + ic
    xl, xc, tgt = x[0], ctx[0], loss_target[0]
    n_lat, n_ctx = xl.shape[0], xc.shape[0]
    assert n_ctx == TB and n_lat % 1024 == 0
    t_total = n_lat + n_ctx
    nb = t_total // TB

    tr = lambda a: jnp.swapaxes(a, -1, -2)
    shift = [functools.partial(jnp.pad, pad_width=((8 * k, WSL + WTAIL - NSH - 8 * k), (0, 0))) for k in range(4)]
    slab = lax.switch(chip, shift, tr(w_in[0]).astype(MXU_DTYPE))
    padrows = lambda a: jnp.pad(a, ((0, FSL - DFF // 4), (0, 0))).astype(MXU_DTYPE)
    shards = [slab[:WSL], slab[WSL:], w_out[0].astype(MXU_DTYPE), padrows(tr(w_gate[0])), padrows(tr(w_up[0])),
              padrows(w_down[0])]
    own = lambda g_, s_: lax.dynamic_update_slice(g_, s_[None], (chip, 0, 0))
    pack = jnp.concatenate([c, hgrn_lb_raw.reshape(1, D), _rows(conv_w[0], 3), jnp.zeros((3, D), F32)], axis=0)
    ncol_ada = w_ada.shape[2]
    b_shard = lax.dynamic_slice(b_ada, (0, chip * ncol_ada), (1, ncol_ada))
    gath, araw, mod_all, wi_main, wi_tail = _prologue(pack, c_ctx.reshape(1, D), w_ada[0], b_shard, shards[:2])
    wi_main, wi_tail = own(wi_main, shards[0]), own(wi_tail, shards[1])
    gath = gath.reshape(8, 8, D)
    lbraw_full = gath[0::2, 1].reshape(4, 2, 2, 256).transpose(1, 2, 0, 3).reshape(4, D)
    convw_full = gath[0::2, 2:5].reshape(4, 3 * D)[:, :KCONV * 512].reshape(4, KCONV, 512).transpose(1, 0, 2)
    convw_full = convw_full.reshape(KCONV, 2048)
    lbraw8 = jnp.pad(lbraw_full, ((0, 4), (0, 0)))
    convp = jnp.concatenate([convw_full, conv_b, jnp.zeros((2, 2048), F32)], axis=0)
    dtb = jnp.pad(ssd_dt_bias.reshape(1, 32), ((0, 7), (0, 96)))
    alog = jnp.pad(ssd_a_log.reshape(1, 32), ((0, 7), (0, 96)))
    mod_all = mod_all.reshape(8, 16, ncol_ada)[0::2]
    mod_full = mod_all.transpose(1, 0, 2).reshape(16, 4 * ncol_ada)
    my_mod = lax.dynamic_slice(mod_full, (me, 0), (1, 6 * D)).reshape(6, D)
    sh1, sc1, g1, sh2, sc2, g2 = (my_mod[k:k + 1] for k in range(6))
    csh1, csc1 = mod_full[8:9, 0:D], mod_full[8:9, D:2 * D]

    zrow = jnp.zeros((1, D), F32)
    mods_lat = jnp.concatenate([1.0 + sc1, sh1, norm_mix, zrow, zrow, zrow, zrow, zrow], axis=0)
    mods_ctx = jnp.concatenate([1.0 + csc1, csh1, norm_mix, zrow, zrow, zrow, zrow, zrow], axis=0)
    outs = _inproj(xl, mods_lat, wi_main, wi_tail, t_total, 1024, 0, None, "inproj_lat",
                   comm=_comm_gather(shards[2:], relay=True))
    wo_g, wg_g, wu_g, wd_g = (own(g_, s_) for g_, s_ in zip(outs[3:], shards[2:]))
    w_out_f = wo_g.reshape(2 * D, D)
    w_down_f = wd_g.reshape(DFFP, D)
    p_main, p_dt, u_all = _inproj(xc, mods_ctx, wi_main, wi_tail, t_total, TB, nb - 1, outs[:3], "inproj_ctx")

    o_f, hs_f = _hgrn_fwd(p_main, lbraw8, 0, nb)
    o_b, hs_b = _hgrn_fwd(p_main, lbraw8, 1, nb)
    xa, dsl, dts = _ssd_prep(p_main, p_dt, convp, dtb, nb)
    y_f, ss_f = _ssd_fwd(xa, dts, alog, 0, nb)
    y_b, ss_b = _ssd_fwd(xa, dts, alog, 1, nb)

    vec_mix = jnp.concatenate([jnp.tile(hgrn_norm, (1, NH)), jnp.repeat(ssd_d, SP, axis=1), ssd_norm, g1, 1.0 + sc2,
                               sh2, norm_ffn, zrow], axis=0)
    ymix, ylat, h1, u2 = _mix_out(o_f, o_b, p_main, y_f, y_b, xa, xl, vec_mix, w_out_f)
    gate, up, act = _ffn_up(u2, wg_g, wu_g)
    vec_loss = jnp.concatenate([g2, final_norm.reshape(1, D)] + [zrow] * 6, axis=0)
    dh2, dffn, acc_loss = _ffn_down_loss(act, w_down_f, h1, tgt, vec_loss)

    core_arr = jnp.reshape(ic, (1,)).astype(jnp.int32)
    chip_arr = jnp.reshape(chip, (1,)).astype(jnp.int32)
    every = (0, 4)

    def pair_sum(gs, got, tag):
        return list(_pair_sum(gs, list(got), core_arr, "grads_pair_sum_" + tag))

    vec_ffn = jnp.concatenate([g1, 1.0 + sc2, norm_ffn] + [zrow] * 5, axis=0)
    dgate, dup, du2 = _ffn_bwd(dffn, w_down_f, gate, up, wg_g, wu_g)
    dh1, dylat, acc_ffn = _ffn_norm_bwd(du2, h1, ylat, dh2, vec_ffn)
    gw_down = _dw(act, dffn, "dw_down").reshape(4, FSL, D)
    ga1 = [_dw(dgate, u2, "dw_gate").reshape(4, FSL, D), _dw(dup, u2, "dw_up").reshape(4, FSL, D)]
    res = _mix_bwd(dylat, o_f, o_b, p_main, y_f, y_b, xa, vec_mix, w_out_f, comm=_comm_pair(ga1))
    (do, dgr, dys, dzr, dxs_skip, acc_mix), pair_a1 = res[:6], pair_sum(ga1, res[6:], "a1")
    ga2 = [gw_down, _dw(ymix, dylat, "dw_out").reshape(4, D // 2, D)]

    res = _hgrn_bwd(p_main, lbraw8, hs_f, do, 0, nb, None,
                    comm=[_comm_exchange(pair_a1, [every] * 2), _comm_pair(ga2)])
    (dq0, dff, dv0, dlb_f), recv_a, pair_a2 = res[:4], list(res[4:6]), pair_sum(ga2, res[6:], "a2")
    res = _hgrn_bwd(p_main, lbraw8, hs_b, do, 1, nb, (dq0, dv0), comm=_comm_exchange(pair_a2, [every] * 2))
    (dq, dfb, dv, dlb_b), recv_a = res[:4], recv_a + list(res[4:])
    pair_a, dests_a = pair_a1 + pair_a2, [every] * 4
    gw_in = [_dw_in([dq, dff], u_all, "dw_in_0"), _dw_in([dfb, dv], u_all, "dw_in_1"),
             _dw_in([dgr, dzr], u_all, "dw_in_2")]

    res = _ssd_bwd(xa, dts, alog, ss_f, dys, 0, nb, None, comm=_comm_pair(gw_in))
    (dxa0, ddts0, da_f), pair_b, dests_b = res[:3], pair_sum(gw_in, res[3:], "b"), [(0, 1), (1, 2), (2, 3)]
    res = _ssd_bwd(xa, dts, alog, ss_b, dys, 1, nb, (dxa0, ddts0), comm=_comm_exchange(pair_b, dests_b))
    (dxa, ddts, da_b), recv_b = res[:3], list(res[3:])
    dxbc, ddt, acc_conv, acc_dtb = _ssd_prep_bwd(p_main, p_dt, convp, dtb, dsl, dxa, dxs_skip, ddts, nb)
    gw_in.append(_dw_in([dxbc], u_all, "dw_in_3"))
    gw_in_dt = _dw(ddt, u_all, "dw_in_dt")
    gc = [gw_in[3], jnp.concatenate([g_[:, 0:WTAIL, :] for g_ in gw_in[1:]] + [gw_in_dt[None]], axis=0)]

    segs = [dq, dff, dfb, dv, dgr, dzr, dxbc]
    bmods_lat = jnp.concatenate([1.0 + sc1, norm_mix] + [zrow] * 6, axis=0)
    bmods_ctx = jnp.concatenate([1.0 + csc1, norm_mix] + [zrow] * 6, axis=0)
    res = _du_prenorm_bwd(segs, ddt, wi_main, wi_tail, xc, bmods_ctx, None, n_lat, TB, "du_ctx", comm=_comm_pair(gc))
    acc_ctx, pair_c, dests_c = res[0], pair_sum(gc, res[1:], "c"), [(3, 4), every]
    res = _du_prenorm_bwd(segs, ddt, wi_main, wi_tail, xl, bmods_lat, dh1, 0, 512, "du_lat",
                          comm=_comm_exchange(pair_c, dests_c))
    (grad_x, acc_lat), recv_c = res[:2], list(res[2:])

    mine = _chip_sum(pair_b + pair_c + pair_a, recv_b + recv_c + recv_a, chip_arr, dests_b + dests_c + dests_a,
                     [0, 0, 0, 0, 1, 3, 4, 5, 2])
    dmod_lat = jnp.concatenate([acc_lat[0:2], acc_ffn[3:4], acc_ffn[0:2], acc_loss[0:1]], axis=0)
    misc = jnp.concatenate([(da_f + da_b)[0, :32], jnp.zeros((96,), F32), acc_dtb[0, :32], jnp.zeros((96,), F32),
                            jnp.sum(acc_loss[2]).reshape(1), jnp.zeros((D - 257,), F32)]).reshape(1, D)
    sv = jnp.concatenate([
        dmod_lat, acc_ctx[0:2], (acc_lat[2:3] + acc_ctx[2:3]), acc_ffn[2:3], acc_loss[1:2], acc_mix[2:3],
        acc_mix[0:1], acc_mix[1:2], dlb_f[0:1], dlb_b[0:1], acc_conv[0:6].reshape(12, D), misc,
        jnp.zeros((3, D), F32)], axis=0)
    res = _pair_swap(mine, sv)
    theirs, sv_all = res[:-1], res[-1].reshape(8, 32, D)
    whole = [jnp.concatenate([jnp.where(ic == 0, m_, t_), jnp.where(ic == 0, t_, m_)], axis=0)
             for m_, t_ in zip(mine, theirs)]
    g_w_in = lax.dynamic_slice(jnp.concatenate(whole[0:2], axis=0), (8 * chip, 0), (NSH, D))
    g_w_out = whole[2]
    g_w_gate = whole[3][:DFF // 4]
    g_w_up = whole[4][:DFF // 4]
    g_w_down = whole[5][:DFF // 4]
    ssum = _sum8(sv_all)
    dmod_rows = sv_all[:, 0:6].reshape(8, 6 * D)
    dmod_ctx_row = jnp.concatenate([ssum[6:8].reshape(1, 2 * D), jnp.zeros((1, 4 * D), F32)], axis=1)
    dmod_full = jnp.concatenate([dmod_rows, dmod_ctx_row, jnp.zeros((7, 6 * D), F32)], axis=0)
    grad_b_ada = jnp.sum(dmod_full, axis=0, keepdims=True)
    dmod_shard = lax.dynamic_slice(dmod_full, (0, chip * ncol_ada), (16, ncol_ada))
    g_w_ada, da_part = _ada_bwd(araw, dmod_shard, w_ada[0])
    da_all = _allgather8(da_part, "ada_ctx_gather").reshape(8, 16, D)[0::2, 8]
    cc = c_ctx.reshape(1, D)
    grad_c_ctx = (jnp.sum(da_all, axis=0, keepdims=True) * _dsilu(cc)).reshape(D)

    grad_norm_mix, grad_norm_ffn, grad_final_norm = ssum[8:9], ssum[9:10], ssum[10].reshape(D)
    grad_ssd_norm = ssum[11:12]
    grad_hgrn_norm = jnp.sum(ssum[12].reshape(NH, HF), axis=0, keepdims=True)
    grad_ssd_d = jnp.sum(ssum[13].reshape(SHEADS, SP), axis=1).reshape(1, SHEADS)
    lb_full = _sig(lbraw_full[0:2] - lbraw_full[2:4])
    dr0 = ssum[14:16] * lb_full * (1.0 - lb_full)
    grad_lb_full = jnp.stack([dr0, -dr0], axis=0)
    grad_lb = lax.dynamic_slice(grad_lb_full, (0, 0, chip * 256), (2, 2, 256))
    grad_conv_w = lax.dynamic_slice(ssum[16:26].reshape(KCONV, 2048), (0, chip * 512), (KCONV, 512)).reshape(1, KCONV, 512)
    grad_conv_b = ssum[26:28].reshape(1, 2048)
    a_val = -jnp.exp(ssd_a_log)
    grad_a_log = ssum[28, 0:32].reshape(1, 2, SHEADS) * a_val
    grad_dt_bias = ssum[28, 128:160].reshape(1, 2, SHEADS)
    loss = ssum[28, 256]

    small_w = [c_ctx, b_ada, norm_mix, conv_w, conv_b, ssd_a_log, ssd_dt_bias, ssd_d, ssd_norm, hgrn_lb_raw,
               hgrn_norm, norm_ffn, final_norm]
    small_m = [m_c_ctx, m_b_ada, m_norm_mix, m_conv_w, m_conv_b, m_ssd_a_log, m_ssd_dt_bias, m_ssd_d, m_ssd_norm,
               m_hgrn_lb_raw, m_hgrn_norm, m_norm_ffn, m_final_norm]
    small_v = [v_c_ctx, v_b_ada, v_norm_mix, v_conv_w, v_conv_b, v_ssd_a_log, v_ssd_dt_bias, v_ssd_d, v_ssd_norm,
               v_hgrn_lb_raw, v_hgrn_norm, v_norm_ffn, v_final_norm]
    small_g = [grad_c_ctx, grad_b_ada, grad_norm_mix, grad_conv_w, grad_conv_b, grad_a_log, grad_dt_bias, grad_ssd_d,
               grad_ssd_norm, grad_lb, grad_hgrn_norm, grad_norm_ffn, grad_final_norm]
    nrows = [-(-a.size // D) for a in small_w]
    packs = lambda lst: jnp.concatenate([_rows(a, r) for a, r in zip(lst, nrows)]
                                        + [jnp.zeros((24 - sum(nrows), D), F32)], axis=0)
    sd, sm, svv = _adamw(packs(small_w), packs(small_m), packs(small_v), packs(small_g), "adamw_small")

    def unpack(p):
        out, r0 = [], 0
        for a, r in zip(small_w, nrows):
            out.append(p[r0:r0 + r].reshape(-1)[:a.size].reshape(a.shape))
            r0 += r
        return out

    sd, sm, svv = unpack(sd), unpack(sm), unpack(svv)
    big = {}
    for nm, w_, m_, v_, g_ in (("w_ada", w_ada, m_w_ada, v_w_ada, g_w_ada), ("w_in", w_in, m_w_in, v_w_in, g_w_in),
                               ("w_out", w_out, m_w_out, v_w_out, g_w_out),
                               ("w_gate", w_gate, m_w_gate, v_w_gate, g_w_gate),
                               ("w_up", w_up, m_w_up, v_w_up, g_w_up),
                               ("w_down", w_down, m_w_down, v_w_down, g_w_down)):
        if nm in ("w_in", "w_gate", "w_up"):
            big[nm] = tuple(tr(t) for t in (g_[None],) + tuple(_adamw(tr(w_), tr(m_), tr(v_), g_, "adamw_" + nm)))
        else:
            big[nm] = (g_[None],) + tuple(_adamw(w_, m_, v_, g_, "adamw_" + nm))

    order = ["c_ctx", "w_ada", "b_ada", "norm_mix", "w_in", "conv_w", "conv_b", "ssd_a_log", "ssd_dt_bias", "ssd_d",
             "ssd_norm", "hgrn_lb_raw", "hgrn_norm", "w_out", "norm_ffn", "w_gate", "w_up", "w_down", "final_norm"]
    small_names = ["c_ctx", "b_ada", "norm_mix", "conv_w", "conv_b", "ssd_a_log", "ssd_dt_bias", "ssd_d", "ssd_norm",
                   "hgrn_lb_raw", "hgrn_norm", "norm_ffn", "final_norm"]
    table = dict(big)
    for k, nm in enumerate(small_names):
        table[nm] = (small_g[k].reshape(small_w[k].shape), sd[k], sm[k], svv[k])
    grads = [table[nm][0] for nm in order]
    deltas = [table[nm][1] for nm in order]
    new_m = [table[nm][2] for nm in order]
    new_v = [table[nm][3] for nm in order]
    return (loss, grad_x[None], *grads, *deltas, *new_m, *new_v)
```

```python
import functools
import math

import jax
import jax.numpy as jnp
from jax import lax
from jax.experimental import pallas as pl
from jax.experimental.pallas import tpu as pltpu

F32 = jnp.float32
BF16 = jnp.bfloat16
MXU_DTYPE = jnp.bfloat16
_INTERPRET = False

D = 1024
NH, HF = 8, 128
HC = 64
SC = 128
SN = 128
SHEADS, SP = 16, 64
GRID_W = 64
KCONV = 5
DFF = 2816
FSL = 768
DFFP = 4 * FSL
NIN = 8224
TB = 256
EPS = 1e-6
LR, B1, B2, AEPS, WD, STEP = 0.001, 0.9, 0.999, 1e-08, 0.01, 10
MESH_ID = pl.DeviceIdType.MESH
NSH = NIN // 4
WSL = 2048
WTAIL = 128


def _pcall(body, *, name, out_shape, grid=(), in_specs=None, out_specs=None, scratch=(), sem=None,
           vmem_mb=None, aliases=None):
    params = {}
    if sem is not None:
        params["dimension_semantics"] = sem
    if vmem_mb is not None:
        params["vmem_limit_bytes"] = vmem_mb << 20
    kw = dict(name=name, out_shape=out_shape, scratch_shapes=list(scratch),
              input_output_aliases=aliases or {}, compiler_params=pltpu.CompilerParams(**params),
              interpret=_INTERPRET)
    if grid:
        kw["grid"] = grid
    if in_specs is not None:
        kw["in_specs"] = in_specs
    if out_specs is not None:
        kw["out_specs"] = out_specs
    return pl.pallas_call(body, **kw)


def _mx(a):
    return a.astype(MXU_DTYPE)


def _dg(a, b, ca, cb):
    return lax.dot_general(_mx(a), _mx(b), (((ca,), (cb,)), ((), ())), preferred_element_type=F32)


def _nn(a, b):
    return _dg(a, b, 1, 0)


def _nt(a, b):
    return _dg(a, b, 1, 1)


def _tn(a, b):
    return _dg(a, b, 0, 0)


def _dot01(m, x):
    hi = x.astype(BF16)
    r1 = x - hi.astype(F32)
    mid = r1.astype(BF16)
    lo = (r1 - mid.astype(F32)).astype(BF16)
    f = lambda t: lax.dot_general(m, t, (((1,), (0,)), ((), ())), preferred_element_type=F32)
    return f(hi) + f(mid) + f(lo)


def _tri(n, upper):
    r = lax.broadcasted_iota(jnp.int32, (n, n), 0)
    c = lax.broadcasted_iota(jnp.int32, (n, n), 1)
    return (c >= r) if upper else (c <= r)


def _b01(mask):
    return jnp.where(mask, 1.0, 0.0).astype(BF16)


def _sig(x):
    return jax.nn.sigmoid(x)


def _silu(x):
    return x * _sig(x)


def _dsilu(x):
    s = _sig(x)
    return s * (1.0 + x * (1.0 - s))


def _softplus(x):
    return jnp.maximum(x, 0.0) + jnp.log(1.0 + jnp.exp(-jnp.abs(x)))


def _rowsum(x):
    return jnp.sum(x, axis=1, keepdims=True)


def _colsum(x):
    return jnp.sum(x, axis=0, keepdims=True)


def _full(shape):
    return pl.BlockSpec(shape, lambda *_: (0,) * len(shape))


def _allgather8_ops(x_ref, out_ref, send_sems, recv_sems, local_sem):
    m_per = x_ref.shape[0]
    x, y, c = lax.axis_index("x"), lax.axis_index("y"), lax.axis_index("c")
    me, sibling = (x, y, c), (x, y, 1 - c)
    chips = [(1 - x, y), (x, 1 - y), (1 - x, 1 - y)]

    def rows(px, py, pc):
        return out_ref.at[pl.ds((4 * px + 2 * py + pc) * m_per, m_per), :]

    def copy(k, block, to, src=None):
        return pltpu.make_async_remote_copy(
            src_ref=rows(*block) if src is None else src, dst_ref=rows(*block),
            send_sem=send_sems.at[k], recv_sem=recv_sems.at[k], device_id=to, device_id_type=MESH_ID)

    mine = pltpu.make_async_copy(x_ref, rows(*me), local_sem)
    mine.start()
    first = [copy(0, me, sibling, src=x_ref)]
    first += [copy(1 + j, me, (*chip, c), src=x_ref) for j, chip in enumerate(chips)]
    for cp in first:
        cp.start()
    passed = [copy(4 + j, (*chip, c), sibling) for j, chip in enumerate(chips)]
    for j, chip in enumerate(chips):
        copy(1 + j, (*chip, c), me).wait_recv()
        passed[j].start()
    copy(0, sibling, me).wait_recv()
    for j, chip in enumerate(chips):
        copy(4 + j, (*chip, 1 - c), me).wait_recv()
    for cp in first + passed:
        cp.wait_send()
    mine.wait()


_AG8_SEMS = [pltpu.SemaphoreType.DMA((7,)), pltpu.SemaphoreType.DMA((7,)), pltpu.SemaphoreType.DMA]


def _allgather8(v, name):
    m_per, n = v.shape
    return _pcall(
        functools.partial(_allgather8_ops), name=name, out_shape=jax.ShapeDtypeStruct((8 * m_per, n), v.dtype),
        in_specs=[pl.BlockSpec(memory_space=pltpu.VMEM)], out_specs=pl.BlockSpec(memory_space=pltpu.VMEM),
        scratch=list(_AG8_SEMS),
    )(v)


def _prologue(pack, cc_row, w_ada, b_shard, shards):
    n = len(shards)
    ncol = w_ada.shape[1]

    def body(pack_ref, cc_ref, w_ref, b_ref, *refs):
        ins = refs[:n]
        gath_ref, araw_ref, mod_ref = refs[n:n + 3]
        outs = refs[n + 3:2 * n + 3]
        modsh, s1, r1, l1, s2, r2, l2, gs, gr = refs[2 * n + 3:]
        start, forward, finish = _gather_ops(ins, outs, gs, gr, relay=True)
        start()
        _allgather8_ops(pack_ref, gath_ref, s1, r1, l1)
        a = jnp.concatenate([gath_ref[8 * i:8 * i + 1, :] for i in range(8)] + [cc_ref[...], jnp.zeros((7, D), F32)],
                            axis=0)
        araw_ref[...] = a
        modsh[...] = _nn(_silu(a), w_ref[...]) + b_ref[...]
        _allgather8_ops(modsh, mod_ref, s2, r2, l2)
        forward()
        finish()

    vm = pl.BlockSpec(memory_space=pltpu.VMEM)
    anyspec = pl.BlockSpec(memory_space=pl.ANY)
    return _pcall(
        body, name="prologue",
        out_shape=(jax.ShapeDtypeStruct((64, D), F32), jax.ShapeDtypeStruct((16, D), F32),
                   jax.ShapeDtypeStruct((128, ncol), F32)) + _gather_out(shards),
        in_specs=[vm, vm, vm, vm] + [anyspec] * n, out_specs=(vm, vm, vm) + (anyspec,) * n,
        scratch=[pltpu.VMEM((16, ncol), F32)] + list(_AG8_SEMS) + list(_AG8_SEMS) + _gather_sems(n), vmem_mb=40,
    )(pack, cc_row, w_ada, b_shard, *shards)


def _gather_ops(ins, outs, send_sems, recv_sems, relay=False):
    n = len(ins)
    x, y, c = lax.axis_index("x"), lax.axis_index("y"), lax.axis_index("c")
    me, sibling = (x, y, c), (x, y, 1 - c)
    chips = [(1 - x, y), (x, 1 - y), (1 - x, 1 - y)]
    direct = 2 if relay else 3

    def part(a, px, py, pc, quarter=None):
        half = ins[a].shape[0] // 2
        if quarter is None:
            return outs[a].at[2 * px + py, pl.ds(pc * half, half), :]
        return outs[a].at[2 * px + py, pl.ds(pc * half + quarter * (half // 2), half // 2), :]

    def copy(a, k, block, to, src=None, quarter=None):
        return pltpu.make_async_remote_copy(
            src_ref=part(a, *block, quarter) if src is None else src, dst_ref=part(a, *block, quarter),
            send_sem=send_sems.at[8 * a + k], recv_sem=recv_sems.at[8 * a + k], device_id=to,
            device_id_type=MESH_ID)

    def first(a, j):
        half = ins[a].shape[0] // 2
        return copy(a, j, me, (*chips[j], c), src=ins[a].at[pl.ds(c * half, half), :])

    relayed = lambda a, q: copy(a, 6 + q, (*chips[q], c), (*chips[1 - q], c), quarter=q)

    def start():
        for a in range(n):
            for j in range(direct):
                first(a, j).start()

    def forward():
        for a in range(n):
            for j in range(direct):
                copy(a, j, (*chips[j], c), me).wait_recv()
                copy(a, 3 + j, (*chips[j], c), sibling).start()
                if relay:
                    relayed(a, j).start()
            if relay:
                for q in range(2):
                    copy(a, 6 + q, (*chips[2], c), me, quarter=q).wait_recv()
                copy(a, 5, (*chips[2], c), sibling).start()

    def finish():
        for a in range(n):
            for j, chip in enumerate(chips):
                copy(a, 3 + j, (*chip, 1 - c), me).wait_recv()
        for a in range(n):
            for j, chip in enumerate(chips):
                if j < direct:
                    first(a, j).wait_send()
                    if relay:
                        relayed(a, j).wait_send()
                copy(a, 3 + j, (*chip, c), sibling).wait_send()

    return start, forward, finish


def _gather_out(shards):
    return tuple(jax.ShapeDtypeStruct((4,) + s_.shape, s_.dtype) for s_ in shards)


def _gather_sems(n):
    return [pltpu.SemaphoreType.DMA((8 * n,)), pltpu.SemaphoreType.DMA((8 * n,))]


def _pair_ops(ins, outs, send_sems, recv_sems):
    x, y, c = lax.axis_index("x"), lax.axis_index("y"), lax.axis_index("c")
    cps = []
    for a in range(len(ins)):
        half = ins[a].shape[1] // 2
        cps.append(pltpu.make_async_remote_copy(
            src_ref=ins[a].at[:, pl.ds((1 - c) * half, half), :], dst_ref=outs[a], send_sem=send_sems.at[a],
            recv_sem=recv_sems.at[a], device_id=(x, y, 1 - c), device_id_type=MESH_ID))

    def start():
        for cp in cps:
            cp.start()

    def finish():
        for cp in cps:
            cp.wait()

    return start, finish


def _comm_pair(gs):
    n = len(gs)
    return (list(gs), tuple(jax.ShapeDtypeStruct((g.shape[0], g.shape[1] // 2, g.shape[2]), g.dtype) for g in gs),
            [pltpu.SemaphoreType.DMA((n,)), pltpu.SemaphoreType.DMA((n,))], _pair_ops)


def _exchange_ops(ins, outs, send_sems, recv_sems, dests):
    x, y, c = lax.axis_index("x"), lax.axis_index("y"), lax.axis_index("c")
    mine = 2 * x + y
    chips = [(1 - x, y), (x, 1 - y), (1 - x, 1 - y)]

    def each(fn):
        for a in range(len(ins)):
            lo, hi = dests[a]
            for j, (px, py) in enumerate(chips):
                q = 2 * px + py
                cp = pltpu.make_async_remote_copy(
                    src_ref=ins[a].at[jnp.clip(q - lo, 0, hi - lo - 1)], dst_ref=outs[a].at[j],
                    send_sem=send_sems.at[3 * a + j], recv_sem=recv_sems.at[3 * a + j], device_id=(px, py, c),
                    device_id_type=MESH_ID)
                fn(cp, (q >= lo) & (q < hi), (mine >= lo) & (mine < hi), (lo, hi) == (0, 4))

    def start():
        def go(cp, send_ok, recv_ok, always):
            if always:
                cp.start()
            else:
                pl.when(send_ok)(cp.start)
        each(go)

    def finish():
        def go(cp, send_ok, recv_ok, always):
            if always:
                cp.wait()
            else:
                pl.when(send_ok)(cp.wait_send)
                pl.when(recv_ok)(cp.wait_recv)
        each(go)

    return start, finish


def _comm_exchange(hs, dests):
    n = len(hs)
    return (list(hs), tuple(jax.ShapeDtypeStruct((3,) + h.shape[1:], h.dtype) for h in hs),
            [pltpu.SemaphoreType.DMA((3 * n,)), pltpu.SemaphoreType.DMA((3 * n,))],
            lambda i, o, s, r: _exchange_ops(i, o, s, r, dests))


def _comm_gather(shards, relay=False):
    return (list(shards), _gather_out(shards), _gather_sems(len(shards)),
            lambda i, o, s, r: _gather_ops(i, o, s, r, relay))


def _carry(call, comm, steps):
    if comm is None:
        return call
    if isinstance(comm, list):
        for one in comm:
            call = _carry(call, one, steps)
        return call
    arrays, out_shape, sems, make = comm
    n, n_in, n_out = len(arrays), len(call["args"]), len(call["out_shape"])
    body = call["body"]

    def wrapped(*refs):
        base_in, cin = refs[:n_in], refs[n_in:n_in + n]
        rest = refs[n_in + n:]
        base_out, cout, scr = rest[:n_out], rest[n_out:n_out + n], rest[n_out + n:]
        ops = make(cin, cout, scr[-2], scr[-1])
        when = steps()
        pl.when(when[0])(ops[0])
        if len(ops) == 3 and len(when) == 3:
            pl.when(when[2])(ops[1])
        body(*base_in, *base_out, *scr[:-2])
        if len(ops) == 3 and len(when) == 2:
            pl.when(when[1])(ops[1])
        pl.when(when[1])(ops[-1])

    anyspec = pl.BlockSpec(memory_space=pl.ANY)
    return dict(call, body=wrapped, args=list(call["args"]) + arrays,
                in_specs=list(call["in_specs"]) + [anyspec] * n,
                out_shape=tuple(call["out_shape"]) + tuple(out_shape),
                out_specs=tuple(call["out_specs"]) + (anyspec,) * n,
                scratch=list(call["scratch"]) + sems)


def _run(call):
    args = call.pop("args")
    body = call.pop("body")
    return _pcall(body, **call)(*args)


def _pair_swap(rs, sv):
    n = len(rs)

    def body(sv_ref, *refs):
        ins, outs, got_ref = refs[:n], refs[n:2 * n], refs[2 * n]
        send_sems, recv_sems, s1, r1, l1 = refs[2 * n + 1:]
        x, y, c = lax.axis_index("x"), lax.axis_index("y"), lax.axis_index("c")
        cps = [pltpu.make_async_remote_copy(
            src_ref=ins[a], dst_ref=outs[a], send_sem=send_sems.at[a], recv_sem=recv_sems.at[a],
            device_id=(x, y, 1 - c), device_id_type=MESH_ID) for a in range(n)]
        for cp in cps:
            cp.start()
        _allgather8_ops(sv_ref, got_ref, s1, r1, l1)
        for cp in cps:
            cp.wait()

    vm, anyspec = pl.BlockSpec(memory_space=pltpu.VMEM), pl.BlockSpec(memory_space=pl.ANY)
    return _pcall(
        body, name="grads_pair_swap",
        out_shape=tuple(jax.ShapeDtypeStruct(r.shape, r.dtype) for r in rs)
        + (jax.ShapeDtypeStruct((8 * sv.shape[0], sv.shape[1]), sv.dtype),),
        in_specs=[vm] + [anyspec] * n, out_specs=(anyspec,) * n + (vm,),
        scratch=[pltpu.SemaphoreType.DMA((n,)), pltpu.SemaphoreType.DMA((n,))] + list(_AG8_SEMS),
    )(sv, *rs)


SUM_STEPS = 4


def _pair_sum(gs, recvs, core, name):
    n = len(gs)

    def body(c_ref, *refs):
        for a in range(n):
            refs[2 * n + a][...] = (refs[a][...].astype(F32) + refs[n + a][...].astype(F32)).astype(refs[2 * n + a].dtype)

    blk = lambda g: (g.shape[0], g.shape[1] // (2 * SUM_STEPS), g.shape[2])
    return pl.pallas_call(
        body, name=name,
        out_shape=tuple(jax.ShapeDtypeStruct((g.shape[0], g.shape[1] // 2, g.shape[2]), g.dtype) for g in gs),
        grid_spec=pltpu.PrefetchScalarGridSpec(
            num_scalar_prefetch=1, grid=(SUM_STEPS,),
            in_specs=[pl.BlockSpec(blk(g), lambda i, cr: (0, cr[0] * SUM_STEPS + i, 0)) for g in gs]
            + [pl.BlockSpec(blk(g), lambda i, cr: (0, i, 0)) for g in gs],
            out_specs=tuple(pl.BlockSpec(blk(g), lambda i, cr: (0, i, 0)) for g in gs)),
        compiler_params=pltpu.CompilerParams(vmem_limit_bytes=40 << 20), interpret=_INTERPRET,
    )(core, *gs, *recvs)


def _chip_sum(hs, recvs, chip, dests, slots):
    n = len(hs)
    nout = max(slots) + 1
    first = [slots.index(o) for o in range(nout)]
    every = lambda d_: d_ == (0, 4)

    def own(d_):
        if every(d_):
            return lambda i, kr: (kr[0], i, 0)
        return lambda i, kr: (0, jnp.where(kr[0] == d_[0], i, 0), 0)

    def got(d_):
        if every(d_):
            return lambda i, kr: (0, i, 0)
        return lambda i, kr: (0, jnp.where(kr[0] == d_[0], i, 0), 0)

    def body(k_ref, *refs):
        for a in range(n):
            def emit(a=a):
                acc = refs[a][0].astype(F32)
                for j in range(3):
                    acc = acc + refs[n + a][j].astype(F32)
                refs[2 * n + slots[a]][...] = acc
            if every(dests[a]):
                emit()
            else:
                pl.when(k_ref[0] == dests[a][0])(emit)

    rb = lambda h: h.shape[1] // SUM_STEPS
    return pl.pallas_call(
        body, name="grads_chip_sum",
        out_shape=tuple(jax.ShapeDtypeStruct(hs[a].shape[1:], F32) for a in first),
        grid_spec=pltpu.PrefetchScalarGridSpec(
            num_scalar_prefetch=1, grid=(SUM_STEPS,),
            in_specs=[pl.BlockSpec((1, rb(h), h.shape[2]), own(d_)) for h, d_ in zip(hs, dests)]
            + [pl.BlockSpec((3, rb(h), h.shape[2]), got(d_)) for h, d_ in zip(hs, dests)],
            out_specs=tuple(pl.BlockSpec((rb(hs[a]), hs[a].shape[2]), lambda i, kr: (i, 0)) for a in first)),
        compiler_params=pltpu.CompilerParams(vmem_limit_bytes=40 << 20), interpret=_INTERPRET,
    )(chip, *hs, *recvs)


def _ada_bwd(araw, dmod, w):
    nblk = w.shape[1] // 512

    def body(a_ref, d_ref, w_ref, gw_ref, da_ref):
        j = pl.program_id(0)
        gw_ref[...] = _tn(_silu(a_ref[...]), d_ref[...])
        part = _nt(d_ref[...], w_ref[...])

        @pl.when(j == 0)
        def _():
            da_ref[...] = part

        @pl.when(j > 0)
        def _():
            da_ref[...] += part

    return _pcall(
        body, name="ada_bwd",
        out_shape=(jax.ShapeDtypeStruct(w.shape, F32), jax.ShapeDtypeStruct((16, D), F32)), grid=(nblk,),
        in_specs=[_full((16, D)), pl.BlockSpec((16, 512), lambda j: (0, j)), pl.BlockSpec((D, 512), lambda j: (0, j))],
        out_specs=(pl.BlockSpec((D, 512), lambda j: (0, j)), _full((16, D))), sem=("arbitrary",),
    )(araw, dmod, w)


def _w_specs():
    return [pl.BlockSpec((None, D, D), lambda j, i: (j // 2, j % 2, 0)),
            pl.BlockSpec((None, WTAIL, D), lambda j, i: (jnp.maximum(j // 2 - 1, 0), 0, 0)),
            pl.BlockSpec((None, WTAIL, D), lambda j, i: (3, 0, 0))]


def _inproj(xin, mods, wi_main, wi_tail, t_total, tb, blk_off, prev, name, comm=None):
    n = xin.shape[0]
    nt = n // tb
    ncol = 8

    def body(x_ref, mod_ref, w_ref, wb_ref, wdt_ref, *rest):
        p_ref, pdt_ref, u_ref, uscr = rest[-4:]
        j, i = pl.program_id(0), pl.program_id(1)
        rows = pl.ds(pl.multiple_of(i * tb, tb), tb)

        @pl.when(j == 0)
        def _():
            xv = x_ref[...]
            r = lax.rsqrt(jnp.mean(xv * xv, axis=1, keepdims=True) + EPS)
            u = (xv * r * mod_ref[2:3, :]) * mod_ref[0:1, :] + mod_ref[1:2, :]
            ub = u.astype(MXU_DTYPE)
            uscr[rows, :] = ub
            u_ref[...] = ub
            pdt_ref[...] = _nt(ub, wdt_ref[...])

        ub = uscr[rows, :]
        pv = _nt(ub, w_ref[...])

        @pl.when((j % 2 == 1) | (j == 0))
        def _():
            p_ref[...] = pv.astype(p_ref.dtype)

        @pl.when((j % 2 == 0) & (j > 0))
        def _():
            head = pv[:, 0:WTAIL] + _nt(ub, wb_ref[...])
            p_ref[...] = jnp.concatenate([head, pv[:, WTAIL:]], axis=1).astype(p_ref.dtype)

    once = lambda j, i: (jnp.where(j == 0, i, nt - 1) + blk_off, 0)
    in_specs = [pl.BlockSpec((tb, D), lambda j, i: (jnp.where(j == 0, i, nt - 1), 0)), _full((8, D))] + _w_specs()
    args = [xin, mods, wi_main, wi_tail, wi_tail]
    aliases = None
    if prev is not None:
        in_specs += [pl.BlockSpec(memory_space=pl.ANY)] * 3
        args += list(prev)
        aliases = {5: 0, 6: 1, 7: 2}
    call = dict(
        body=body, args=args, name=name,
        out_shape=(jax.ShapeDtypeStruct((t_total, ncol * D), MXU_DTYPE), jax.ShapeDtypeStruct((t_total, 128), F32),
                   jax.ShapeDtypeStruct((t_total, D), MXU_DTYPE)),
        grid=(ncol, nt), in_specs=in_specs,
        out_specs=(pl.BlockSpec((tb, D), lambda j, i: (i + blk_off, j)), pl.BlockSpec((tb, 128), once),
                   pl.BlockSpec((tb, D), once)),
        scratch=[pltpu.VMEM((n, D), MXU_DTYPE)], sem=("arbitrary", "arbitrary"), vmem_mb=48, aliases=aliases)
    steps = lambda: ((pl.program_id(0) == 0) & (pl.program_id(1) == 0),
                     (pl.program_id(0) == ncol - 1) & (pl.program_id(1) == nt - 1),
                     (pl.program_id(0) == ncol - 2) & (pl.program_id(1) == 0))
    return _run(_carry(call, comm, steps))


def _blk(s, nb, rev):
    return jnp.where(s == 0, nb - 1, (nb - 1 - s) if rev else (s - 1))


def _hgrn_gate(fr, lbraw_ref, d):
    lb = _sig(lbraw_ref[d:d + 1, :] - lbraw_ref[2 + d:3 + d, :])
    sg = _sig(fr)
    return lb, sg, lb + (1.0 - lb) * sg


def _hgrn_fwd(p_main, lbraw, d, nb, comm=None):
    t_total = p_main.shape[0]
    rev = d == 1
    nch = TB // HC
    scale = HF ** -0.5

    def body(q_ref, f_ref, v_ref, lb_ref, o_ref, sp_ref, st):
        s = pl.program_id(0)

        @pl.when(s == 0)
        def _():
            st[...] = jnp.zeros_like(st)

        mb = _tri(HC, rev)
        m01 = _b01(mb)
        order = list(reversed(range(nch)) if rev else range(nch))
        hs_ = [slice(h * HF, (h + 1) * HF) for h in range(NH)]
        pre = {}
        for c in order:
            rows = slice(c * HC, (c + 1) * HC)
            _, _, f = _hgrn_gate(f_ref[rows, :].astype(F32), lb_ref, d)
            k = 1.0 - f
            cum = _dot01(m01, jnp.log(f))
            tot = cum[0:1, :] if rev else cum[HC - 1:HC, :]
            qd = _silu(q_ref[rows, :].astype(F32)) * scale * jnp.exp(cum)
            ki = k * jnp.exp(-cum)
            etot = jnp.exp(tot)
            pre[c] = (_mx(qd), _mx(ki), _mx(ki * etot), _mx(v_ref[rows, :]), etot)
        scs = {c: [_nt(pre[c][0][:, cs], pre[c][1][:, cs]) for cs in hs_] for c in order}
        upd = {c: [_tn(pre[c][3][:, cs], pre[c][2][:, cs]) for cs in hs_] for c in order}
        intra = {c: [_nn(jnp.where(mb, scs[c][h], 0.0), pre[c][3][:, cs]) for h, cs in enumerate(hs_)] for c in order}
        for c in order:
            rows = slice(c * HC, (c + 1) * HC)
            qdb, etot = pre[c][0], pre[c][4]
            for h, cs in enumerate(hs_):
                sth = st[h]
                stb = sth.astype(sp_ref.dtype)
                sp_ref[c, h] = stb
                o_ref[rows, cs] = (intra[c][h] + _nt(qdb[:, cs], stb)).astype(o_ref.dtype)
                st[h] = sth * etot[:, cs] + upd[c][h]

    col = lambda j: (lambda s: (_blk(s, nb, rev), j))
    call = dict(
        body=body, args=[p_main, p_main, p_main, lbraw], name=f"hgrn_fwd_{d}",
        out_shape=(jax.ShapeDtypeStruct((t_total, D), MXU_DTYPE),
                   jax.ShapeDtypeStruct((nch * nb, NH, HF, HF), MXU_DTYPE)),
        grid=(nb,),
        in_specs=[pl.BlockSpec((TB, D), col(0)), pl.BlockSpec((TB, D), col(1 + d)), pl.BlockSpec((TB, D), col(3)),
                  _full((8, D))],
        out_specs=(pl.BlockSpec((TB, D), col(0)),
                   pl.BlockSpec((nch, NH, HF, HF), lambda s: (_blk(s, nb, rev), 0, 0, 0))),
        scratch=[pltpu.VMEM((NH, HF, HF), F32)], sem=("arbitrary",), vmem_mb=40)
    return _run(_carry(call, comm, lambda: (pl.program_id(0) == 0, pl.program_id(0) == nb - 1,
                                            pl.program_id(0) == nb - 4)))


def _hgrn_bwd(p_main, lbraw, sprev, do, d, nb, prev, comm=None):
    t_total = p_main.shape[0]
    rev = d == 1
    nch = TB // HC
    scale = HF ** -0.5
    last = prev is not None
    odt = MXU_DTYPE if last else F32

    def body(q_ref, f_ref, v_ref, lb_ref, sp_ref, do_ref, *rest):
        if last:
            dqp_ref, dvp_ref = rest[:2]
            rest = rest[2:]
        dq_ref, df_ref, dv_ref, dlb_ref, dst = rest
        sp_id = pl.program_id(0)
        is_ctx = sp_id == nb - 1

        @pl.when(sp_id == 0)
        def _():
            dst[...] = jnp.zeros_like(dst)
            dlb_ref[...] = jnp.zeros_like(dlb_ref)

        mb = _tri(HC, rev)
        mbt = _tri(HC, not rev)
        m01 = _b01(mb)
        mt01 = _b01(mbt)
        order = list(range(nch) if rev else reversed(range(nch)))
        hs_ = [slice(h * HF, (h + 1) * HF) for h in range(NH)]
        pre = {}
        for c in order:
            rows = slice(c * HC, (c + 1) * HC)
            lb, sg, f = _hgrn_gate(f_ref[rows, :].astype(F32), lb_ref, d)
            k = 1.0 - f
            cum = _dot01(m01, jnp.log(f))
            tot = cum[0:1, :] if rev else cum[HC - 1:HC, :]
            e = jnp.exp(cum)
            ei = jnp.exp(-cum)
            etot = jnp.exp(tot)
            ee = ei * etot
            qraw = q_ref[rows, :].astype(F32)
            sq = _sig(qraw)
            qd = qraw * sq * scale * e
            ki = k * ei
            ke = k * ee
            dov = jnp.where(is_ctx, 0.0, do_ref[rows, :].astype(F32))
            pre[c] = dict(lb=lb, sg=sg, f=f, e=e, ei=ei, ee=ee, etot=etot, qd=qd, ki=ki, ke=ke,
                          dsq=sq * (1.0 + qraw * (1.0 - sq)),
                          qdb=_mx(qd), kib=_mx(ki), keb=_mx(ke), vb=_mx(v_ref[rows, :]), dob=_mx(dov))
        units = [(c, h) for c in order for h in range(NH)]
        col = lambda u, key: pre[u[0]][key][:, hs_[u[1]]]
        pt = {u: jnp.where(mbt, _nt(col(u, "kib"), col(u, "qdb")), 0.0) for u in units}
        dp = {u: jnp.where(mb, _nt(col(u, "dob"), col(u, "vb")), 0.0) for u in units}
        dpt = {u: jnp.where(mbt, _nt(col(u, "vb"), col(u, "dob")), 0.0) for u in units}
        dv_i = {u: _nn(pt[u], col(u, "dob")) for u in units}
        dqd_ = {u: _nn(dp[u], col(u, "kib")) + _nn(col(u, "dob"), sp_ref[u[0], u[1]]) for u in units}
        dki_ = {u: _nn(dpt[u], col(u, "qdb")) for u in units}
        dsl = {u: _tn(col(u, "dob"), col(u, "qdb")) for u in units}
        for c in order:
            rows = slice(c * HC, (c + 1) * HC)
            p = pre[c]
            dv_l, dke_l, dtot_l = [], [], []
            for h, cs in enumerate(hs_):
                dso = dst[h]
                dsob = _mx(dso)
                dv_l.append(dv_i[(c, h)] + _nt(p["keb"][:, cs], dsob))
                dke_l.append(_nn(p["vb"][:, cs], dsob))
                dtot_l.append(_colsum(dso * sp_ref[c, h].astype(F32)) * p["etot"][:, cs])
                dst[h] = dso * p["etot"][:, cs] + dsl[(c, h)]
            lb, sg, f, e, ei, ee, qd, ki, ke = (p[n_] for n_ in ("lb", "sg", "f", "e", "ei", "ee", "qd", "ki", "ke"))
            dqd = jnp.concatenate([dqd_[(c, h)] for h in range(NH)], axis=1)
            dki = jnp.concatenate([dki_[(c, h)] for h in range(NH)], axis=1)
            dke = jnp.concatenate(dke_l, axis=1)
            dcum = dqd * qd - dki * ki - dke * ke
            dtot = jnp.concatenate(dtot_l, axis=1) + _colsum(dke * ke)
            dk = dki * ei + dke * ee
            dlf = _dot01(mt01, dcum) + dtot
            df = dlf / f - dk
            dlb_ref[0:1, :] += _colsum(df * (1.0 - sg))
            dfr = df * (1.0 - lb) * sg * (1.0 - sg)
            dq = dqd * e * scale * p["dsq"]
            dv = jnp.concatenate(dv_l, axis=1)
            if last:
                dq = dq + dqp_ref[rows, :]
                dv = dv + dvp_ref[rows, :]
            dq_ref[rows, :] = dq.astype(odt)
            dv_ref[rows, :] = dv.astype(odt)
            df_ref[rows, :] = dfr.astype(MXU_DTYPE)

    blk = lambda s: _blk(nb - 1 - s, nb, rev)
    col = lambda j: (lambda s: (blk(s), j))
    in_specs = [pl.BlockSpec((TB, D), col(0)), pl.BlockSpec((TB, D), col(1 + d)), pl.BlockSpec((TB, D), col(3)),
                _full((8, D)), pl.BlockSpec((nch, NH, HF, HF), lambda s: (blk(s), 0, 0, 0)),
                pl.BlockSpec((TB, D), lambda s: (jnp.minimum(blk(s), nb - 2), 0))]
    args = [p_main, p_main, p_main, lbraw, sprev, do]
    if last:
        in_specs += [pl.BlockSpec((TB, D), col(0))] * 2
        args += list(prev)
    call = dict(
        body=body, args=args, name=f"hgrn_bwd_{d}",
        out_shape=(jax.ShapeDtypeStruct((t_total, D), odt), jax.ShapeDtypeStruct((t_total, D), MXU_DTYPE),
                   jax.ShapeDtypeStruct((t_total, D), odt), jax.ShapeDtypeStruct((8, D), F32)),
        grid=(nb,), in_specs=in_specs,
        out_specs=(pl.BlockSpec((TB, D), col(0)), pl.BlockSpec((TB, D), col(0)), pl.BlockSpec((TB, D), col(0)),
                   _full((8, D))),
        scratch=[pltpu.VMEM((NH, HF, HF), F32)], sem=("arbitrary",), vmem_mb=48)
    return _run(_carry(call, comm, lambda: (pl.program_id(0) == 0, pl.program_id(0) == nb - 1)))


def _conv_masks(tb, is_ctx):
    seg = jnp.where(is_ctx, tb, GRID_W)
    pos = lax.broadcasted_iota(jnp.int32, (tb, 1), 0) & (seg - 1)
    return pos, seg


def _shift_rows(x, dshift, pos, seg):
    if dshift == 0:
        return x
    n = x.shape[0]
    rolled = pltpu.roll(x, (-dshift) % n, 0)
    ok = (pos + dshift >= 0) & (pos + dshift < seg)
    return jnp.where(ok, rolled, 0.0)


def _ssd_prep(p_main, p_dt, convp, dtb, nb):
    t_total = p_main.shape[0]

    def body(x_ref, dt_ref, cw_ref, dtb_ref, xa_ref, ds_ref, dts_ref):
        is_ctx = pl.program_id(0) == nb - 1
        pos, seg = _conv_masks(TB, is_ctx)
        xv = x_ref[...].astype(F32)
        acc = cw_ref[5:6, :] + cw_ref[2:3, :] * xv
        for kk in (0, 1, 3, 4):
            acc = acc + cw_ref[kk:kk + 1, :] * _shift_rows(xv, kk - 2, pos, seg)
        sg = _sig(acc)
        xa_ref[...] = (acc * sg).astype(xa_ref.dtype)
        ds_ref[...] = (sg * (1.0 + acc * (1.0 - sg))).astype(ds_ref.dtype)
        dts_ref[...] = _softplus(dt_ref[...] + dtb_ref[0:1, :])

    wide = pl.BlockSpec((TB, 2048), lambda i: (i, 0))
    return _pcall(
        body, name="ssd_prep",
        out_shape=(jax.ShapeDtypeStruct((t_total, 2048), MXU_DTYPE), jax.ShapeDtypeStruct((t_total, 2048), MXU_DTYPE),
                   jax.ShapeDtypeStruct((t_total, 128), F32)),
        grid=(nb,),
        in_specs=[pl.BlockSpec((TB, 2048), lambda i: (i, 3)), pl.BlockSpec((TB, 128), lambda i: (i, 0)),
                  _full((8, 2048)), _full((8, 128))],
        out_specs=(wide, wide, pl.BlockSpec((TB, 128), lambda i: (i, 0))),
        sem=("parallel",), vmem_mb=32,
    )(p_main, p_dt, convp, dtb)


def _ssd_prep_bwd(p_main, p_dt, convp, dtb, dsl, dxa, dxs_skip, ddts, nb):
    t_total = p_main.shape[0]

    def body(x_ref, dt_ref, cw_ref, dtb_ref, ds_ref, dxa_ref, dsk_ref, ddts_ref, dx_ref, ddt_ref, dcw_ref, ddtb_ref):
        i = pl.program_id(0)
        is_ctx = i == nb - 1

        @pl.when(i == 0)
        def _():
            dcw_ref[...] = jnp.zeros_like(dcw_ref)
            ddtb_ref[...] = jnp.zeros_like(ddtb_ref)

        pos, seg = _conv_masks(TB, is_ctx)
        xv = x_ref[...].astype(F32)
        dact = dxa_ref[...]
        dact = jnp.concatenate([dact[:, :D] + jnp.where(is_ctx, 0.0, dsk_ref[...].astype(F32)), dact[:, D:]], axis=1)
        dpre = dact * ds_ref[...].astype(F32)
        dxv = cw_ref[2:3, :] * dpre
        dcw_ref[2:3, :] += _colsum(xv * dpre)
        for kk in (0, 1, 3, 4):
            sdp = _shift_rows(dpre, 2 - kk, pos, seg)
            dxv = dxv + cw_ref[kk:kk + 1, :] * sdp
            dcw_ref[kk:kk + 1, :] += _colsum(xv * sdp)
        dx_ref[...] = dxv.astype(dx_ref.dtype)
        dcw_ref[5:6, :] += _colsum(dpre)
        draw = ddts_ref[...] * _sig(dt_ref[...] + dtb_ref[0:1, :])
        ddt_ref[...] = draw.astype(ddt_ref.dtype)
        ddtb_ref[0:1, :] += _colsum(draw)

    return _pcall(
        body, name="ssd_prep_bwd",
        out_shape=(jax.ShapeDtypeStruct((t_total, 2048), MXU_DTYPE), jax.ShapeDtypeStruct((t_total, 128), MXU_DTYPE),
                   jax.ShapeDtypeStruct((8, 2048), F32), jax.ShapeDtypeStruct((8, 128), F32)),
        grid=(nb,),
        in_specs=[pl.BlockSpec((TB, 2048), lambda i: (i, 3)), pl.BlockSpec((TB, 128), lambda i: (i, 0)),
                  _full((8, 2048)), _full((8, 128)), pl.BlockSpec((TB, 2048), lambda i: (i, 0)),
                  pl.BlockSpec((TB, 2048), lambda i: (i, 0)),
                  pl.BlockSpec((TB, D), lambda i: (jnp.minimum(i, nb - 2), 0)),
                  pl.BlockSpec((TB, 128), lambda i: (i, 0))],
        out_specs=(pl.BlockSpec((TB, 2048), lambda i: (i, 0)), pl.BlockSpec((TB, 128), lambda i: (i, 0)),
                   _full((8, 2048)), _full((8, 128))),
        sem=("arbitrary",), vmem_mb=40,
    )(p_main, p_dt, convp, dtb, dsl, dxa, dxs_skip, ddts)


def _dot2(x, m01):
    hi = x.astype(BF16)
    lo = (x - hi.astype(F32)).astype(BF16)
    f = lambda t: lax.dot_general(t, m01, (((1,), (0,)), ((), ())), preferred_element_type=F32)
    return f(hi) + f(lo)


def _head_lanes(c0, c1):
    p = lax.broadcasted_iota(jnp.int32, (128, 128), 0)
    l = lax.broadcasted_iota(jnp.int32, (128, 128), 1)
    return _b01(((l == c0) & (p < SP)) | ((l == c1) & (p >= SP)))


def _one_lane(col):
    return _b01(lax.broadcasted_iota(jnp.int32, (128, 128), 1) == col)


def _lane_pick(x, lane, col):
    return _rowsum(jnp.where(lane == col, x, 0.0))


def _ssd_chunk_common(dts, alog_ref, m01, rev):
    lane = lax.broadcasted_iota(jnp.int32, (1, 128), 1)
    arow = -jnp.exp(alog_ref[0:1, :])
    cum = _dot01(m01, dts * arow)
    tot = cum[0:1, :] if rev else cum[SC - 1:SC, :]
    return lane, arow, cum, cum.T, tot


def _ssd_fwd(xa, dts, alog, d, nb):
    t_total = xa.shape[0]
    rev = d == 1
    nch = TB // SC
    npair = SHEADS // 2

    def body(xa_ref, dts_ref, alog_ref, y_ref, sp_ref, st):
        s = pl.program_id(0)

        @pl.when(s == 0)
        def _():
            st[...] = jnp.zeros_like(st)

        mb = _tri(SC, rev)
        m01 = _b01(mb)
        lo = lax.broadcasted_iota(jnp.int32, (1, 128), 1) < SP
        rlo = lax.broadcasted_iota(jnp.int32, (128, 1), 0) < SP
        order = list(reversed(range(nch)) if rev else range(nch))
        pre = {}
        for c in order:
            rows = slice(c * SC, (c + 1) * SC)
            dts_c = dts_ref[rows, :]
            lane, arow, cum, cumt, tot = _ssd_chunk_common(dts_c, alog_ref, m01, rev)
            bgs = [_mx(xa_ref[rows, D + g * SN:D + (g + 1) * SN]) for g in range(4)]
            cgs = [_mx(xa_ref[rows, D + 512 + g * SN:D + 512 + (g + 1) * SN]) for g in range(4)]
            pairs = []
            for pr in range(npair):
                xs = xa_ref[rows, pr * 128:(pr + 1) * 128].astype(F32)
                cols = [16 * d + 2 * pr, 16 * d + 2 * pr + 1]
                cum_c = [_lane_pick(cum, lane, q) for q in cols]
                dt_c = [_lane_pick(dts_c, lane, q) for q in cols]
                tot_c = [_lane_pick(tot, lane, q) for q in cols]
                dtx = xs * jnp.where(lo, dt_c[0], dt_c[1])
                e1_pair = jnp.where(lo, jnp.exp(cum_c[0]), jnp.exp(cum_c[1]))
                e2_pair = jnp.where(lo, jnp.exp(tot_c[0] - cum_c[0]), jnp.exp(tot_c[1] - cum_c[1]))
                etot_col = jnp.where(rlo, jnp.exp(tot_c[0]), jnp.exp(tot_c[1]))
                decs = [jnp.where(mb, jnp.exp(cum_c[q] - cumt[cols[q]:cols[q] + 1, :]), 0.0) for q in range(2)]
                dtxq = [_mx(jnp.where(lo if q == 0 else ~lo, dtx, 0.0)) for q in range(2)]
                pairs.append(dict(e1=e1_pair, etot=etot_col, decs=decs, dtxq=dtxq, xe=_mx(dtx * e2_pair)))
            pre[c] = (bgs, cgs, pairs)
        gm = {(c, g): _nt(pre[c][1][g], pre[c][0][g]) for c in order for g in range(4)}
        upd = {(c, pr): _tn(pre[c][2][pr]["xe"], pre[c][0][pr // 2]) for c in order for pr in range(npair)}
        intra = {(c, pr): sum(_nn(gm[(c, pr // 2)] * pre[c][2][pr]["decs"][q], pre[c][2][pr]["dtxq"][q]) for q in range(2))
                 for c in order for pr in range(npair)}
        for c in order:
            rows = slice(c * SC, (c + 1) * SC)
            bgs, cgs, pairs = pre[c]
            for pr in range(npair):
                stp = st[pr]
                stb = stp.astype(sp_ref.dtype)
                sp_ref[c, pr] = stb
                y_ref[rows, pr * 128:(pr + 1) * 128] = (
                    intra[(c, pr)] + pairs[pr]["e1"] * _nt(cgs[pr // 2], stb)).astype(y_ref.dtype)
                st[pr] = stp * pairs[pr]["etot"] + upd[(c, pr)]

    blk = lambda s: _blk(s, nb, rev)
    return _pcall(
        body, name=f"ssd_fwd_{d}",
        out_shape=(jax.ShapeDtypeStruct((t_total, D), MXU_DTYPE),
                   jax.ShapeDtypeStruct((nch * nb, npair, 128, SN), MXU_DTYPE)),
        grid=(nb,),
        in_specs=[pl.BlockSpec((TB, 2048), lambda s: (blk(s), 0)), pl.BlockSpec((TB, 128), lambda s: (blk(s), 0)),
                  _full((8, 128))],
        out_specs=(pl.BlockSpec((TB, D), lambda s: (blk(s), 0)),
                   pl.BlockSpec((nch, npair, 128, SN), lambda s: (blk(s), 0, 0, 0))),
        scratch=[pltpu.VMEM((npair, 128, SN), F32)], sem=("arbitrary",), vmem_mb=40,
    )(xa, dts, alog)


def _ssd_bwd(xa, dts, alog, sprev, dy, d, nb, prev, comm=None):
    t_total = xa.shape[0]
    rev = d == 1
    nch = TB // SC
    npair = SHEADS // 2
    last = prev is not None

    def body(xa_ref, dts_ref, alog_ref, sp_ref, dy_ref, *rest):
        if last:
            dxp_ref, ddp_ref = rest[:2]
            rest = rest[2:]
        dxa_ref, ddts_ref, da_ref, dst, zc_scr = rest
        sp_id = pl.program_id(0)
        is_ctx = sp_id == nb - 1

        @pl.when(sp_id == 0)
        def _():
            dst[...] = jnp.zeros_like(dst)
            da_ref[...] = jnp.zeros_like(da_ref)
            zc_scr[...] = jnp.zeros_like(zc_scr)

        mb = _tri(SC, rev)
        m01 = _b01(mb)
        mt01 = _b01(_tri(SC, not rev))
        lo = lax.broadcasted_iota(jnp.int32, (1, 128), 1) < SP
        rlo = lax.broadcasted_iota(jnp.int32, (128, 1), 0) < SP
        order = list(range(nch) if rev else reversed(range(nch)))
        pre = {}
        for c in order:
            rows = slice(c * SC, (c + 1) * SC)
            dts_c = dts_ref[rows, :]
            lane, arow, cum, cumt, tot = _ssd_chunk_common(dts_c, alog_ref, m01, rev)
            pairs = []
            for pr in range(npair):
                xs = xa_ref[rows, pr * 128:(pr + 1) * 128].astype(F32)
                dyp = jnp.where(is_ctx, 0.0, dy_ref[rows, pr * 128:(pr + 1) * 128].astype(F32))
                cols = [16 * d + 2 * pr, 16 * d + 2 * pr + 1]
                cum_c = [_lane_pick(cum, lane, q) for q in cols]
                dt_c = [_lane_pick(dts_c, lane, q) for q in cols]
                tot_c = [_lane_pick(tot, lane, q) for q in cols]
                e1_c = [jnp.exp(cum_c[q]) for q in range(2)]
                e2_c = [jnp.exp(tot_c[q] - cum_c[q]) for q in range(2)]
                etot_c = [jnp.exp(tot_c[q]) for q in range(2)]
                dt_pair = jnp.where(lo, dt_c[0], dt_c[1])
                e1_pair = jnp.where(lo, e1_c[0], e1_c[1])
                e2_pair = jnp.where(lo, e2_c[0], e2_c[1])
                dtx = xs * dt_pair
                decs = [jnp.where(mb, jnp.exp(cum_c[q] - cumt[cols[q]:cols[q] + 1, :]), 0.0) for q in range(2)]
                dyq = [_mx(jnp.where(lo if q == 0 else ~lo, dyp, 0.0)) for q in range(2)]
                pairs.append(dict(xs=xs, dyp=dyp, cols=cols, e1_c=e1_c, e2_c=e2_c, etot_c=etot_c, dt_pair=dt_pair,
                                  e2_pair=e2_pair, etot_col=jnp.where(rlo, etot_c[0], etot_c[1]), dtx=dtx,
                                  dtxb=_mx(dtx), xeb=_mx(dtx * e2_pair), dy0b=_mx(dyp * e1_pair), decs=decs, dyq=dyq))
            pre[c] = dict(lane=lane, arow=arow, dts=dts_c, pairs=pairs, cum=cum, tot=tot,
                          bgb=[_mx(xa_ref[rows, D + g * SN:D + (g + 1) * SN]) for g in range(4)],
                          cgb=[_mx(xa_ref[rows, D + 512 + g * SN:D + 512 + (g + 1) * SN]) for g in range(4)])
        units = [(c, pr) for c in order for pr in range(npair)]
        P = lambda u: pre[u[0]]["pairs"][u[1]]
        cgu = lambda u: pre[u[0]]["cgb"][u[1] // 2]
        gm = {(c, g): _nt(pre[c]["cgb"][g], pre[c]["bgb"][g]) for c in order for g in range(4)}
        y0 = {u: _nt(cgu(u), sp_ref[u[0], u[1]]) for u in units}
        dcg_i = {u: _nn(P(u)["dy0b"], sp_ref[u[0], u[1]]) for u in units}
        dsl = {u: _tn(P(u)["dy0b"], cgu(u)) for u in units}
        w_ = {(u, q): gm[(u[0], u[1] // 2)] * P(u)["decs"][q] for u in units for q in range(2)}
        dw_ = {(u, q): jnp.where(mb, _nt(P(u)["dyq"][q], P(u)["dtxb"]), 0.0) for u in units for q in range(2)}
        ddtx_i = {(u, q): _tn(w_[(u, q)], P(u)["dyq"][q]) for u in units for q in range(2)}
        for c in order:
            rows = slice(c * SC, (c + 1) * SC)
            pc = pre[c]
            lane, arow, dts_c = pc["lane"], pc["arow"], pc["dts"]
            d1 = jnp.zeros((SC, 128), F32)
            d2 = jnp.zeros((SC, 128), F32)
            dz = jnp.zeros((SC, 128), F32)
            ddt = jnp.zeros((SC, 128), F32)
            dtot = jnp.zeros((1, 128), F32)
            dgm = [jnp.zeros((SC, SC), F32) for _ in range(4)]
            dbg = [jnp.zeros((SC, SN), F32) for _ in range(4)]
            dcg = [jnp.zeros((SC, SN), F32) for _ in range(4)]
            for pr in range(npair):
                u, g, p = (c, pr), pr // 2, pc["pairs"][pr]
                hs = _head_lanes(*p["cols"])
                dso = dst[pr]
                dsob = _mx(dso)
                dxe = _nt(pc["bgb"][g], dsob)
                dbg[g] = dbg[g] + _nn(p["xeb"], dsob)
                ddtx = dxe * p["e2_pair"]
                d2 = d2 + _dot2(dxe * p["dtx"], hs)
                dcg[g] = dcg[g] + dcg_i[u]
                d1 = d1 + _dot2(p["dyp"] * y0[u], hs)
                sprod = dso * sp_ref[c, pr].astype(F32)
                dst[pr] = dso * p["etot_col"] + dsl[u]
                for q in range(2):
                    hm = lo if q == 0 else ~lo
                    col = p["cols"][q]
                    dw = dw_[(u, q)]
                    ddtx = ddtx + jnp.where(hm, ddtx_i[(u, q)], 0.0)
                    dgm[g] = dgm[g] + dw * p["decs"][q]
                    z = dw * w_[(u, q)]
                    dz = dz + _dot2(z, _one_lane(col))
                    zc_scr[col:col + 1, :] = _colsum(z)
                    tsum = _rowsum(_colsum(sprod[q * SP:(q + 1) * SP, :]))
                    dtot = jnp.where(lane == col, tsum * p["etot_c"][q], dtot)
                dxs = ddtx * p["dt_pair"]
                ddt = ddt + _dot2(ddtx * p["xs"], hs)
                if last:
                    dxs = dxs + dxp_ref[rows, pr * 128:(pr + 1) * 128]
                dxa_ref[rows, pr * 128:(pr + 1) * 128] = dxs
            e2_all = jnp.exp(pc["tot"] - pc["cum"])
            dcum = dz - zc_scr[...].T + d1 * jnp.exp(pc["cum"]) - d2 * e2_all
            dtot = dtot + _colsum(d2 * e2_all)
            for g in range(4):
                db = dbg[g] + _tn(dgm[g], pc["cgb"][g])
                dc = dcg[g] + _nn(dgm[g], pc["bgb"][g])
                if last:
                    db = db + dxp_ref[rows, D + g * SN:D + (g + 1) * SN]
                    dc = dc + dxp_ref[rows, D + 512 + g * SN:D + 512 + (g + 1) * SN]
                dxa_ref[rows, D + g * SN:D + (g + 1) * SN] = db
                dxa_ref[rows, D + 512 + g * SN:D + 512 + (g + 1) * SN] = dc
            dla = _dot01(mt01, dcum) + dtot
            ddt = ddt + dla * arow
            da_ref[0:1, :] += _colsum(dla * dts_c)
            if last:
                ddt = ddt + ddp_ref[rows, :]
            ddts_ref[rows, :] = ddt

    blk = lambda s: _blk(nb - 1 - s, nb, rev)
    in_specs = [pl.BlockSpec((TB, 2048), lambda s: (blk(s), 0)), pl.BlockSpec((TB, 128), lambda s: (blk(s), 0)),
                _full((8, 128)), pl.BlockSpec((nch, npair, 128, SN), lambda s: (blk(s), 0, 0, 0)),
                pl.BlockSpec((TB, D), lambda s: (jnp.minimum(blk(s), nb - 2), 0))]
    args = [xa, dts, alog, sprev, dy]
    if last:
        in_specs += [pl.BlockSpec((TB, 2048), lambda s: (blk(s), 0)), pl.BlockSpec((TB, 128), lambda s: (blk(s), 0))]
        args += list(prev)
    call = dict(
        body=body, args=args, name=f"ssd_bwd_{d}",
        out_shape=(jax.ShapeDtypeStruct((t_total, 2048), F32), jax.ShapeDtypeStruct((t_total, 128), F32),
                   jax.ShapeDtypeStruct((8, 128), F32)),
        grid=(nb,), in_specs=in_specs,
        out_specs=(pl.BlockSpec((TB, 2048), lambda s: (blk(s), 0)), pl.BlockSpec((TB, 128), lambda s: (blk(s), 0)),
                   _full((8, 128))),
        scratch=[pltpu.VMEM((npair, 128, SN), F32), pltpu.VMEM((128, 128), F32)], sem=("arbitrary",), vmem_mb=48)
    return _run(_carry(call, comm, lambda: (pl.program_id(0) == 0, pl.program_id(0) == nb - 1)))


def _readout(o, g, yy, z, vec_ref):
    hg, ss, keep = [], [], []
    for h in range(NH):
        cs = slice(h * HF, (h + 1) * HF)
        oh = o[:, cs]
        r = lax.rsqrt(jnp.mean(oh * oh, axis=1, keepdims=True) + EPS)
        hg.append(oh * r * vec_ref[0:1, cs] * _silu(g[:, cs]))
        keep.append(r)
    u = yy * _silu(z)
    for gi in range(4):
        cs = slice(gi * 256, (gi + 1) * 256)
        ug = u[:, cs]
        r = lax.rsqrt(jnp.mean(ug * ug, axis=1, keepdims=True) + EPS)
        ss.append(ug * r * vec_ref[2:3, cs])
        keep.append(r)
    return jnp.concatenate(hg, axis=1), jnp.concatenate(ss, axis=1), keep, u


def _mix_out(o_f, o_b, p_main, y_f, y_b, xa, x, vecs, w_out):
    n = x.shape[0]

    def body(of_ref, ob_ref, g_ref, z_ref, yf_ref, yb_ref, xs_ref, x_ref, vec_ref, w_ref,
             ymix_ref, ylat_ref, h1_ref, u2_ref):
        o = of_ref[...].astype(F32) + ob_ref[...].astype(F32)
        yy = yf_ref[...].astype(F32) + yb_ref[...].astype(F32) + vec_ref[1:2, :] * xs_ref[...].astype(F32)
        hg, ss, _, _ = _readout(o, g_ref[...].astype(F32), yy, z_ref[...].astype(F32), vec_ref)
        ymix = jnp.concatenate([hg, ss], axis=1).astype(MXU_DTYPE)
        ymix_ref[...] = ymix
        ylat = _nn(ymix, w_ref[...])
        ylat_ref[...] = ylat
        h1 = x_ref[...] + vec_ref[3:4, :] * ylat
        h1_ref[...] = h1
        r = lax.rsqrt(jnp.mean(h1 * h1, axis=1, keepdims=True) + EPS)
        u2_ref[...] = ((h1 * r * vec_ref[6:7, :]) * vec_ref[4:5, :] + vec_ref[5:6, :]).astype(MXU_DTYPE)

    row = lambda j: (lambda i: (i, j))
    return _pcall(
        body, name="mix_out",
        out_shape=(jax.ShapeDtypeStruct((n, 2 * D), MXU_DTYPE), jax.ShapeDtypeStruct((n, D), F32),
                   jax.ShapeDtypeStruct((n, D), F32), jax.ShapeDtypeStruct((n, D), MXU_DTYPE)),
        grid=(n // TB,),
        in_specs=[pl.BlockSpec((TB, D), row(0)), pl.BlockSpec((TB, D), row(0)), pl.BlockSpec((TB, D), row(4)),
                  pl.BlockSpec((TB, D), row(5)), pl.BlockSpec((TB, D), row(0)), pl.BlockSpec((TB, D), row(0)),
                  pl.BlockSpec((TB, D), row(0)), pl.BlockSpec((TB, D), row(0)), _full((8, D)), _full((2 * D, D))],
        out_specs=(pl.BlockSpec((TB, 2 * D), row(0)), pl.BlockSpec((TB, D), row(0)), pl.BlockSpec((TB, D), row(0)),
                   pl.BlockSpec((TB, D), row(0))),
        sem=("parallel",), vmem_mb=48,
    )(o_f, o_b, p_main, p_main, y_f, y_b, xa, x, vecs, w_out)


def _mix_bwd(dylat, o_f, o_b, p_main, y_f, y_b, xa, vecs, w_out, comm=None):
    n = dylat.shape[0]
    t_total = p_main.shape[0]
    nlat = n // TB

    def body(*refs):
        dg_ref, dz_ref, acc_ref = refs[11], refs[13], refs[15]
        i = pl.program_id(0)

        @pl.when(i == 0)
        def _():
            acc_ref[...] = jnp.zeros_like(acc_ref)

        @pl.when(i < nlat)
        def _():
            compute(*refs)

        @pl.when(i == nlat)
        def _():
            dg_ref[...] = jnp.zeros_like(dg_ref)
            dz_ref[...] = jnp.zeros_like(dz_ref)

    def compute(dyl_ref, of_ref, ob_ref, g_ref, z_ref, yf_ref, yb_ref, xs_ref, vec_ref, w_ref,
                do_ref, dg_ref, dys_ref, dz_ref, dxs_ref, acc_ref):
        dymix = _nt(dyl_ref[...], w_ref[...])
        o = of_ref[...].astype(F32) + ob_ref[...].astype(F32)
        g = g_ref[...].astype(F32)
        z = z_ref[...].astype(F32)
        xs = xs_ref[...].astype(F32)
        yy = yf_ref[...].astype(F32) + yb_ref[...].astype(F32) + vec_ref[1:2, :] * xs
        _, _, keep, u = _readout(o, g, yy, z, vec_ref)
        do_l, dg_l = [], []
        for h in range(NH):
            cs = slice(h * HF, (h + 1) * HF)
            oh, gh, r, wv = o[:, cs], g[:, cs], keep[h], vec_ref[0:1, cs]
            dhg = dymix[:, cs]
            xh = oh * r
            dn = dhg * _silu(gh)
            dg_l.append(dhg * xh * wv * _dsilu(gh))
            acc_ref[0:1, cs] += _colsum(dn * xh)
            dxh = dn * wv
            do_l.append(r * (dxh - xh * jnp.mean(dxh * xh, axis=1, keepdims=True)))
        du_l = []
        for gi in range(4):
            cs = slice(gi * 256, (gi + 1) * 256)
            ug, r, wv = u[:, cs], keep[NH + gi], vec_ref[2:3, cs]
            dss = dymix[:, D + gi * 256:D + (gi + 1) * 256]
            xh = ug * r
            acc_ref[2:3, cs] += _colsum(dss * xh)
            dxh = dss * wv
            du_l.append(r * (dxh - xh * jnp.mean(dxh * xh, axis=1, keepdims=True)))
        du = jnp.concatenate(du_l, axis=1)
        dyy = du * _silu(z)
        do_ref[...] = jnp.concatenate(do_l, axis=1).astype(do_ref.dtype)
        dg_ref[...] = jnp.concatenate(dg_l, axis=1).astype(dg_ref.dtype)
        dys_ref[...] = dyy.astype(dys_ref.dtype)
        dz_ref[...] = (du * yy * _dsilu(z)).astype(dz_ref.dtype)
        dxs_ref[...] = (dyy * vec_ref[1:2, :]).astype(dxs_ref.dtype)
        acc_ref[1:2, :] += _colsum(dyy * xs)

    row = lambda j: (lambda i: (jnp.minimum(i, nlat - 1), j))
    lat = pl.BlockSpec((TB, D), row(0))
    tok = pl.BlockSpec((TB, D), lambda i: (i, 0))
    call = dict(
        body=body, args=[dylat, o_f, o_b, p_main, p_main, y_f, y_b, xa, vecs, w_out], name="mix_bwd",
        out_shape=(jax.ShapeDtypeStruct((n, D), MXU_DTYPE), jax.ShapeDtypeStruct((t_total, D), MXU_DTYPE),
                   jax.ShapeDtypeStruct((n, D), MXU_DTYPE), jax.ShapeDtypeStruct((t_total, D), MXU_DTYPE),
                   jax.ShapeDtypeStruct((n, D), MXU_DTYPE), jax.ShapeDtypeStruct((8, D), F32)),
        grid=(t_total // TB,),
        in_specs=[lat, lat, lat, pl.BlockSpec((TB, D), row(4)), pl.BlockSpec((TB, D), row(5)), lat, lat, lat,
                  _full((8, D)), _full((2 * D, D))],
        out_specs=(lat, tok, lat, tok, lat, _full((8, D))), scratch=[],
        sem=("arbitrary",), vmem_mb=48)
    return _run(_carry(call, comm, lambda: (pl.program_id(0) == 0, pl.program_id(0) == t_total // TB - 1)))


def _ffn_up(u2, w_gate, w_up):
    n = u2.shape[0]
    tb = 1024

    def body(u_ref, wg_ref, wu_ref, g_ref, up_ref, a_ref):
        uv = u_ref[...]
        gt = _nt(uv, wg_ref[...])
        upv = _nt(uv, wu_ref[...])
        g_ref[...] = gt.astype(g_ref.dtype)
        up_ref[...] = upv.astype(up_ref.dtype)
        a_ref[...] = (_silu(gt) * upv).astype(a_ref.dtype)

    blk = pl.BlockSpec((tb, FSL), lambda j, i: (i, j))
    wblk = pl.BlockSpec((None, FSL, D), lambda j, i: (j, 0, 0))
    return _pcall(
        body, name="ffn_up",
        out_shape=(jax.ShapeDtypeStruct((n, DFFP), MXU_DTYPE),) * 3,
        grid=(4, n // tb), in_specs=[pl.BlockSpec((tb, D), lambda j, i: (i, 0)), wblk, wblk],
        out_specs=(blk, blk, blk), sem=("parallel", "parallel"), vmem_mb=48,
    )(u2, w_gate, w_up)


def _ffn_down_loss(act, w_down, h1, tgt, vecs):
    n = act.shape[0]
    tb = 512

    def body(a_ref, w_ref, h1_ref, t_ref, vec_ref, dh2_ref, dffn_ref, acc_ref):
        i = pl.program_id(0)

        @pl.when(i == 0)
        def _():
            acc_ref[...] = jnp.zeros_like(acc_ref)

        g2 = vec_ref[0:1, :]
        fw = vec_ref[1:2, :]
        nsub = 4
        sb = tb // nsub
        wv = w_ref[...]
        ffns = [_nn(a_ref[r_ * sb:(r_ + 1) * sb, :], wv) for r_ in range(nsub)]
        for r_ in range(nsub):
            rows = slice(r_ * sb, (r_ + 1) * sb)
            ffn = ffns[r_]
            h2 = h1_ref[rows, :] + g2 * ffn
            r = lax.rsqrt(jnp.mean(h2 * h2, axis=1, keepdims=True) + EPS)
            xh = h2 * r
            err = xh * fw - t_ref[rows, :]
            dy = err * (1.0 / D)
            acc_ref[2:3, :] += _colsum(err * err) * (0.5 / D)
            acc_ref[1:2, :] += _colsum(dy * xh)
            dxh = dy * fw
            dh2 = r * (dxh - xh * jnp.mean(dxh * xh, axis=1, keepdims=True))
            dh2_ref[rows, :] = dh2
            dffn_ref[rows, :] = (g2 * dh2).astype(dffn_ref.dtype)
            acc_ref[0:1, :] += _colsum(dh2 * ffn)

    return _pcall(
        body, name="ffn_down_loss",
        out_shape=(jax.ShapeDtypeStruct((n, D), F32), jax.ShapeDtypeStruct((n, D), MXU_DTYPE),
                   jax.ShapeDtypeStruct((8, D), F32)),
        grid=(n // tb,),
        in_specs=[pl.BlockSpec((tb, DFFP), lambda i: (i, 0)), _full((DFFP, D)), pl.BlockSpec((tb, D), lambda i: (i, 0)),
                  pl.BlockSpec((tb, D), lambda i: (i, 0)), _full((8, D))],
        out_specs=(pl.BlockSpec((tb, D), lambda i: (i, 0)), pl.BlockSpec((tb, D), lambda i: (i, 0)), _full((8, D))),
        sem=("arbitrary",), vmem_mb=48,
    )(act, w_down, h1, tgt, vecs)


def _ffn_bwd(dffn, w_down, gate, up, w_gate_t, w_up_t):
    n = dffn.shape[0]
    tb = 1024

    def body(df_ref, wd_ref, g_ref, up_ref, wg_ref, wu_ref, dg_ref, dup_ref, du_ref):
        j = pl.program_id(1)
        nsub = 4
        sb = tb // nsub
        wd, wg, wu = wd_ref[...], wg_ref[...], wu_ref[...]
        dacts = [_nt(df_ref[r * sb:(r + 1) * sb, :], wd) for r in range(nsub)]
        parts = []
        for r in range(nsub):
            rows = slice(r * sb, (r + 1) * sb)
            gt = g_ref[rows, :].astype(F32)
            upv = up_ref[rows, :].astype(F32)
            sg = _sig(gt)
            dgt = (dacts[r] * upv * (sg * (1.0 + gt * (1.0 - sg)))).astype(MXU_DTYPE)
            dupv = (dacts[r] * (gt * sg)).astype(MXU_DTYPE)
            dg_ref[rows, :] = dgt
            dup_ref[rows, :] = dupv
            parts.append(_nn(dgt, wg) + _nn(dupv, wu))
        part = jnp.concatenate(parts, axis=0)

        @pl.when(j == 0)
        def _():
            du_ref[...] = part

        @pl.when(j > 0)
        def _():
            du_ref[...] += part

    tok = pl.BlockSpec((tb, D), lambda i, j: (i, 0))
    ffb = pl.BlockSpec((tb, FSL), lambda i, j: (i, j))
    wsl = pl.BlockSpec((None, FSL, D), lambda i, j: (j, 0, 0))
    return _pcall(
        body, name="ffn_bwd",
        out_shape=(jax.ShapeDtypeStruct((n, DFFP), MXU_DTYPE), jax.ShapeDtypeStruct((n, DFFP), MXU_DTYPE),
                   jax.ShapeDtypeStruct((n, D), F32)),
        grid=(n // tb, 4),
        in_specs=[tok, pl.BlockSpec((FSL, D), lambda i, j: (j, 0)), ffb, ffb, wsl, wsl],
        out_specs=(ffb, ffb, tok), sem=("parallel", "arbitrary"), vmem_mb=48,
    )(dffn, w_down, gate, up, w_gate_t, w_up_t)


def _ffn_norm_bwd(du, h1, ylat, dh2, vecs):
    n = du.shape[0]
    tb = 512

    def body(du_ref, h1_ref, yl_ref, dh2_ref, vec_ref, dh1_ref, dyl_ref, acc_ref):
        @pl.when(pl.program_id(0) == 0)
        def _():
            acc_ref[...] = jnp.zeros_like(acc_ref)

        duv = du_ref[...]
        h1 = h1_ref[...]
        r = lax.rsqrt(jnp.mean(h1 * h1, axis=1, keepdims=True) + EPS)
        xh = h1 * r
        nw = vec_ref[2:3, :]
        acc_ref[0:1, :] += _colsum(duv)
        acc_ref[1:2, :] += _colsum(duv * xh * nw)
        dn = duv * vec_ref[1:2, :]
        acc_ref[2:3, :] += _colsum(dn * xh)
        dxh = dn * nw
        dh1 = dh2_ref[...] + r * (dxh - xh * jnp.mean(dxh * xh, axis=1, keepdims=True))
        dh1_ref[...] = dh1
        dyl_ref[...] = (vec_ref[0:1, :] * dh1).astype(dyl_ref.dtype)
        acc_ref[3:4, :] += _colsum(dh1 * yl_ref[...])

    tok = pl.BlockSpec((tb, D), lambda i: (i, 0))
    return _pcall(
        body, name="ffn_norm_bwd",
        out_shape=(jax.ShapeDtypeStruct((n, D), F32), jax.ShapeDtypeStruct((n, D), MXU_DTYPE),
                   jax.ShapeDtypeStruct((8, D), F32)),
        grid=(n // tb,), in_specs=[tok, tok, tok, tok, _full((8, D))], out_specs=(tok, tok, _full((8, D))),
        sem=("arbitrary",), vmem_mb=40,
    )(du, h1, ylat, dh2, vecs)


def _deep_rows(rows):
    return max(r for r in range(128, 2305, 128) if rows % r == 0)


def _dw(a, b, name):
    tn_rows = a.shape[0]
    bt = _deep_rows(tn_rows)
    kk, nn_ = a.shape[1], b.shape[1]
    bk = 1024 if kk % 1024 == 0 else kk
    bn = 1024 if nn_ % 1024 == 0 else nn_
    nt = tn_rows // bt

    def body(a_ref, b_ref, o_ref, acc):
        t = pl.program_id(2)
        part = _tn(a_ref[...], b_ref[...])

        @pl.when(t == 0)
        def _():
            acc[...] = part

        @pl.when(t > 0)
        def _():
            acc[...] += part

        @pl.when(t == nt - 1)
        def _():
            o_ref[...] = acc[...].astype(o_ref.dtype)

    return _pcall(
        body, name=name, out_shape=jax.ShapeDtypeStruct((kk, nn_), MXU_DTYPE), grid=(kk // bk, nn_ // bn, nt),
        in_specs=[pl.BlockSpec((bt, bk), lambda i, j, t: (t, i)), pl.BlockSpec((bt, bn), lambda i, j, t: (t, j))],
        out_specs=pl.BlockSpec((bk, bn), lambda i, j, t: (i, j)), scratch=[pltpu.VMEM((bk, bn), F32)],
        sem=("parallel", "parallel", "arbitrary"), vmem_mb=40,
    )(a, b)


def _dw_in(segs, u_all, name):
    tiles = []
    for m, s_ in enumerate(segs):
        tiles += [(m, h) for h in range(s_.shape[1] // D)]
    ntile = len(tiles)
    t_total = u_all.shape[0]
    bt = _deep_rows(t_total)
    nt = t_total // bt

    def body(u_ref, *refs):
        seg_refs, o_ref, acc = refs[:len(segs)], refs[len(segs)], refs[len(segs) + 1]
        n, t = pl.program_id(0), pl.program_id(1)
        for k, (m, _) in enumerate(tiles):
            @pl.when(n == k)
            def _(m=m):
                part = _tn(seg_refs[m][...], u_ref[...])

                @pl.when(t == 0)
                def _():
                    acc[...] = part

                @pl.when(t > 0)
                def _():
                    acc[...] += part

        @pl.when(t == nt - 1)
        def _():
            o_ref[...] = acc[...].astype(o_ref.dtype)

    def seg_spec(m):
        ks = [k for k, (mm, _) in enumerate(tiles) if mm == m]
        lo, hi = ks[0], ks[-1]
        on = lambda n: (n >= lo) & (n <= hi)
        return pl.BlockSpec((bt, D), lambda n, t: (jnp.where(on(n), t, 0), jnp.where(on(n), n - lo, 0)))

    return _pcall(
        body, name=name, out_shape=jax.ShapeDtypeStruct((1, ntile * D, D), MXU_DTYPE), grid=(ntile, nt),
        in_specs=[pl.BlockSpec((bt, D), lambda n, t: (t, 0))] + [seg_spec(m) for m in range(len(segs))],
        out_specs=pl.BlockSpec((None, D, D), lambda n, t: (0, n, 0)),
        scratch=[pltpu.VMEM((D, D), F32)], sem=("parallel", "arbitrary"), vmem_mb=56,
    )(u_all, *segs)


def _du_prenorm_bwd(segs, ddt, wi_main, wi_tail, xin, mods, dres, row_off, tb, name, comm=None):
    n = xin.shape[0]
    nt = n // tb
    off = row_off // tb
    has_dx = dres is not None

    def body(*refs):
        seg_refs = refs[:7]
        ddt_ref, w_ref, wb_ref, wdt_ref, x_ref, mod_ref = refs[7:13]
        rest = refs[13:]
        if has_dx:
            dres_ref, dx_ref, acc_ref, du_scr = rest
        else:
            acc_ref, du_scr = rest
        j, i = pl.program_id(0), pl.program_id(1)
        rows = pl.ds(pl.multiple_of(i * tb, tb), tb)

        @pl.when((i == 0) & (j == 0))
        def _():
            acc_ref[...] = jnp.zeros_like(acc_ref)

        @pl.when(j == 0)
        def _():
            du_scr[rows, :] = _nn(ddt_ref[...], wdt_ref[...])

        for k in range(8):
            if not has_dx and k in (4, 5):
                continue

            @pl.when(j == k)
            def _(k=k):
                sv = seg_refs[min(k, 6)][...]
                part = _nn(sv, w_ref[...])
                if k in (2, 4, 6):
                    part = part + _nn(sv[:, 0:WTAIL], wb_ref[...])
                du_scr[rows, :] += part

        @pl.when(j == 7)
        def _():
            du = du_scr[rows, :]
            xv = x_ref[...]
            r = lax.rsqrt(jnp.mean(xv * xv, axis=1, keepdims=True) + EPS)
            xh = xv * r
            nw = mod_ref[1:2, :]
            acc_ref[0:1, :] += _colsum(du)
            acc_ref[1:2, :] += _colsum(du * xh * nw)
            dn = du * mod_ref[0:1, :]
            acc_ref[2:3, :] += _colsum(dn * xh)
            if has_dx:
                dxh = dn * nw
                dx_ref[...] = dres_ref[...] + r * (dxh - xh * jnp.mean(dxh * xh, axis=1, keepdims=True))

    def seg_spec(k):
        if k < 6:
            return pl.BlockSpec((tb, D), lambda j, i: (jnp.where(j == k, i + off, 0), 0))
        return pl.BlockSpec((tb, D), lambda j, i: (jnp.where(j >= 6, i + off, 0), jnp.where(j >= 6, j - 6, 0)))

    last = pl.BlockSpec((tb, D), lambda j, i: (jnp.where(j == 7, i, 0), 0))
    in_specs = [seg_spec(k) for k in range(7)]
    in_specs += [pl.BlockSpec((tb, 128), lambda j, i: (jnp.where(j == 0, i + off, 0), 0))] + _w_specs()
    in_specs += [last, _full((8, D))]
    args = list(segs) + [ddt, wi_main, wi_tail, wi_tail, xin, mods]
    out_shape = [jax.ShapeDtypeStruct((8, D), F32)]
    out_specs = [_full((8, D))]
    if has_dx:
        in_specs.append(last)
        args.append(dres)
        out_shape.insert(0, jax.ShapeDtypeStruct((n, D), F32))
        out_specs.insert(0, last)
    call = dict(body=body, args=args, name=name, out_shape=tuple(out_shape), grid=(8, nt), in_specs=in_specs,
                out_specs=tuple(out_specs), scratch=[pltpu.VMEM((n, D), F32)], sem=("arbitrary", "arbitrary"),
                vmem_mb=56)
    steps = lambda: ((pl.program_id(0) == 0) & (pl.program_id(1) == 0),
                     (pl.program_id(0) == 7) & (pl.program_id(1) == nt - 1))
    return _run(_carry(call, comm, steps))


def _sum8(v):
    def body(v_ref, o_ref):
        acc = v_ref[0]
        for k in range(1, 8):
            acc = acc + v_ref[k]
        o_ref[...] = acc

    return _pcall(body, name="small_sum", out_shape=jax.ShapeDtypeStruct(v.shape[1:], F32),
                  in_specs=[pl.BlockSpec(memory_space=pltpu.VMEM)], out_specs=pl.BlockSpec(memory_space=pltpu.VMEM))(v)


def _adamw(w, m, v, g, name):
    lead = w.ndim == 3
    rows, cols = w.shape[-2:]
    rb = 256 if rows % 256 == 0 else (352 if rows % 352 == 0 else rows)
    c1 = 1.0 - B1 ** STEP
    c2 = 1.0 - B2 ** STEP

    def body(w_ref, m_ref, v_ref, g_ref, d_ref, nm_ref, nv_ref):
        gv = g_ref[...]
        mn = B1 * m_ref[...] + (1.0 - B1) * gv
        vn = B2 * v_ref[...] + (1.0 - B2) * (gv * gv)
        nm_ref[...] = mn
        nv_ref[...] = vn
        d_ref[...] = -LR * ((mn / c1) / (jnp.sqrt(vn / c2) + AEPS) + WD * w_ref[...])

    if rb == rows and rows > 1024:
        cb, steps = 256, cols // 256
        gspec = pl.BlockSpec((rows, cb), lambda i: (0, i))
        spec = pl.BlockSpec((None, rows, cb), lambda i: (0, 0, i)) if lead else gspec
    else:
        steps = rows // rb
        gspec = pl.BlockSpec((rb, cols), lambda i: (i, 0))
        spec = pl.BlockSpec((None, rb, cols), lambda i: (0, i, 0)) if lead else gspec
    return _pcall(
        body, name=name, out_shape=(jax.ShapeDtypeStruct(w.shape, F32),) * 3, grid=(steps,),
        in_specs=[spec] * 3 + [gspec], out_specs=(spec,) * 3, sem=("parallel",), vmem_mb=40,
    )(w, m, v, g)


def _rows(v, n):
    f = v.reshape(-1)
    return jnp.pad(f, (0, n * D - f.shape[0])).reshape(n, D)


def kernel(x, c, ctx, c_ctx, w_ada, b_ada, norm_mix, w_in, conv_w, conv_b, ssd_a_log, ssd_dt_bias, ssd_d, ssd_norm, hgrn_lb_raw, hgrn_norm, w_out, norm_ffn, w_gate, w_up, w_down, final_norm, loss_target, m_c_ctx, m_w_ada, m_b_ada, m_norm_mix, m_w_in, m_conv_w, m_conv_b, m_ssd_a_log, m_ssd_dt_bias, m_ssd_d, m_ssd_norm, m_hgrn_lb_raw, m_hgrn_norm, m_w_out, m_norm_ffn, m_w_gate, m_w_up, m_w_down, m_final_norm, v_c_ctx, v_w_ada, v_b_ada, v_norm_mix, v_w_in, v_conv_w, v_conv_b, v_ssd_a_log, v_ssd_dt_bias, v_ssd_d, v_ssd_norm, v_hgrn_lb_raw, v_hgrn_norm, v_w_out, v_norm_ffn, v_w_gate, v_w_up, v_w_down, v_final_norm):
    ix, iy, ic = lax.axis_index("x"), lax.axis_index("y"), lax.axis_index("c")
    chip = 2 * ix + iy
    me = 2 * chip + ic
    xl, xc, tgt = x[0], ctx[0], loss_target[0]
    n_lat, n_ctx = xl.shape[0], xc.shape[0]
    assert n_ctx == TB and n_lat % 1024 == 0
    t_total = n_lat + n_ctx
    nb = t_total // TB

    tr = lambda a: jnp.swapaxes(a, -1, -2)
    shift = [functools.partial(jnp.pad, pad_width=((8 * k, WSL + WTAIL - NSH - 8 * k), (0, 0))) for k in range(4)]
    slab = lax.switch(chip, shift, tr(w_in[0]).astype(MXU_DTYPE))
    padrows = lambda a: jnp.pad(a, ((0, FSL - DFF // 4), (0, 0))).astype(MXU_DTYPE)
    shards = [slab[:WSL], slab[WSL:], w_out[0].astype(MXU_DTYPE), padrows(tr(w_gate[0])), padrows(tr(w_up[0])),
              padrows(w_down[0])]
    own = lambda g_, s_: lax.dynamic_update_slice(g_, s_[None], (chip, 0, 0))
    pack = jnp.concatenate([c, hgrn_lb_raw.reshape(1, D), _rows(conv_w[0], 3), jnp.zeros((3, D), F32)], axis=0)
    ncol_ada = w_ada.shape[2]
    b_shard = lax.dynamic_slice(b_ada, (0, chip * ncol_ada), (1, ncol_ada))
    gath, araw, mod_all, wi_main, wi_tail = _prologue(pack, c_ctx.reshape(1, D), w_ada[0], b_shard, shards[:2])
    wi_main, wi_tail = own(wi_main, shards[0]), own(wi_tail, shards[1])
    gath = gath.reshape(8, 8, D)
    lbraw_full = gath[0::2, 1].reshape(4, 2, 2, 256).transpose(1, 2, 0, 3).reshape(4, D)
    convw_full = gath[0::2, 2:5].reshape(4, 3 * D)[:, :KCONV * 512].reshape(4, KCONV, 512).transpose(1, 0, 2)
    convw_full = convw_full.reshape(KCONV, 2048)
    lbraw8 = jnp.pad(lbraw_full, ((0, 4), (0, 0)))
    convp = jnp.concatenate([convw_full, conv_b, jnp.zeros((2, 2048), F32)], axis=0)
    dtb = jnp.pad(ssd_dt_bias.reshape(1, 32), ((0, 7), (0, 96)))
    alog = jnp.pad(ssd_a_log.reshape(1, 32), ((0, 7), (0, 96)))
    mod_all = mod_all.reshape(8, 16, ncol_ada)[0::2]
    mod_full = mod_all.transpose(1, 0, 2).reshape(16, 4 * ncol_ada)
    my_mod = lax.dynamic_slice(mod_full, (me, 0), (1, 6 * D)).reshape(6, D)
    sh1, sc1, g1, sh2, sc2, g2 = (my_mod[k:k + 1] for k in range(6))
    csh1, csc1 = mod_full[8:9, 0:D], mod_full[8:9, D:2 * D]

    zrow = jnp.zeros((1, D), F32)
    mods_lat = jnp.concatenate([1.0 + sc1, sh1, norm_mix, zrow, zrow, zrow, zrow, zrow], axis=0)
    mods_ctx = jnp.concatenate([1.0 + csc1, csh1, norm_mix, zrow, zrow, zrow, zrow, zrow], axis=0)
    outs = _inproj(xl, mods_lat, wi_main, wi_tail, t_total, 1024, 0, None, "inproj_lat",
                   comm=_comm_gather(shards[2:5]))
    wo_g, wg_g, wu_g = (own(g_, s_) for g_, s_ in zip(outs[3:], shards[2:5]))
    w_out_f = wo_g.reshape(2 * D, D)
    p_main, p_dt, u_all = _inproj(xc, mods_ctx, wi_main, wi_tail, t_total, TB, nb - 1, outs[:3], "inproj_ctx")

    o_f, hs_f, wd_g = _hgrn_fwd(p_main, lbraw8, 0, nb, comm=_comm_gather(shards[5:]))
    w_down_f = own(wd_g, shards[5]).reshape(DFFP, D)
    o_b, hs_b = _hgrn_fwd(p_main, lbraw8, 1, nb)
    xa, dsl, dts = _ssd_prep(p_main, p_dt, convp, dtb, nb)
    y_f, ss_f = _ssd_fwd(xa, dts, alog, 0, nb)
    y_b, ss_b = _ssd_fwd(xa, dts, alog, 1, nb)

    vec_mix = jnp.concatenate([jnp.tile(hgrn_norm, (1, NH)), jnp.repeat(ssd_d, SP, axis=1), ssd_norm, g1, 1.0 + sc2,
                               sh2, norm_ffn, zrow], axis=0)
    ymix, ylat, h1, u2 = _mix_out(o_f, o_b, p_main, y_f, y_b, xa, xl, vec_mix, w_out_f)
    gate, up, act = _ffn_up(u2, wg_g, wu_g)
    vec_loss = jnp.concatenate([g2, final_norm.reshape(1, D)] + [zrow] * 6, axis=0)
    dh2, dffn, acc_loss = _ffn_down_loss(act, w_down_f, h1, tgt, vec_loss)

    core_arr = jnp.reshape(ic, (1,)).astype(jnp.int32)
    chip_arr = jnp.reshape(chip, (1,)).astype(jnp.int32)
    every = (0, 4)

    def pair_sum(gs, got, tag):
        return list(_pair_sum(gs, list(got), core_arr, "grads_pair_sum_" + tag))

    vec_ffn = jnp.concatenate([g1, 1.0 + sc2, norm_ffn] + [zrow] * 5, axis=0)
    dgate, dup, du2 = _ffn_bwd(dffn, w_down_f, gate, up, wg_g, wu_g)
    dh1, dylat, acc_ffn = _ffn_norm_bwd(du2, h1, ylat, dh2, vec_ffn)
    gw_down = _dw(act, dffn, "dw_down").reshape(4, FSL, D)
    ga1 = [_dw(dgate, u2, "dw_gate").reshape(4, FSL, D), _dw(dup, u2, "dw_up").reshape(4, FSL, D)]
    res = _mix_bwd(dylat, o_f, o_b, p_main, y_f, y_b, xa, vec_mix, w_out_f, comm=_comm_pair(ga1))
    (do, dgr, dys, dzr, dxs_skip, acc_mix), pair_a1 = res[:6], pair_sum(ga1, res[6:], "a1")
    ga2 = [gw_down, _dw(ymix, dylat, "dw_out").reshape(4, D // 2, D)]

    res = _hgrn_bwd(p_main, lbraw8, hs_f, do, 0, nb, None,
                    comm=[_comm_exchange(pair_a1, [every] * 2), _comm_pair(ga2)])
    (dq0, dff, dv0, dlb_f), recv_a, pair_a2 = res[:4], list(res[4:6]), pair_sum(ga2, res[6:], "a2")
    res = _hgrn_bwd(p_main, lbraw8, hs_b, do, 1, nb, (dq0, dv0), comm=_comm_exchange(pair_a2, [every] * 2))
    (dq, dfb, dv, dlb_b), recv_a = res[:4], recv_a + list(res[4:])
    pair_a, dests_a = pair_a1 + pair_a2, [every] * 4
    gw_in = [_dw_in([dq, dff], u_all, "dw_in_0"), _dw_in([dfb, dv], u_all, "dw_in_1"),
             _dw_in([dgr, dzr], u_all, "dw_in_2")]

    res = _ssd_bwd(xa, dts, alog, ss_f, dys, 0, nb, None, comm=_comm_pair(gw_in))
    (dxa0, ddts0, da_f), pair_b, dests_b = res[:3], pair_sum(gw_in, res[3:], "b"), [(0, 1), (1, 2), (2, 3)]
    res = _ssd_bwd(xa, dts, alog, ss_b, dys, 1, nb, (dxa0, ddts0), comm=_comm_exchange(pair_b, dests_b))
    (dxa, ddts, da_b), recv_b = res[:3], list(res[3:])
    dxbc, ddt, acc_conv, acc_dtb = _ssd_prep_bwd(p_main, p_dt, convp, dtb, dsl, dxa, dxs_skip, ddts, nb)
    gw_in.append(_dw_in([dxbc], u_all, "dw_in_3"))
    gw_in_dt = _dw(ddt, u_all, "dw_in_dt")
    gc = [gw_in[3], jnp.concatenate([g_[:, 0:WTAIL, :] for g_ in gw_in[1:]] + [gw_in_dt[None]], axis=0)]

    segs = [dq, dff, dfb, dv, dgr, dzr, dxbc]
    bmods_lat = jnp.concatenate([1.0 + sc1, norm_mix] + [zrow] * 6, axis=0)
    bmods_ctx = jnp.concatenate([1.0 + csc1, norm_mix] + [zrow] * 6, axis=0)
    res = _du_prenorm_bwd(segs, ddt, wi_main, wi_tail, xc, bmods_ctx, None, n_lat, TB, "du_ctx", comm=_comm_pair(gc))
    acc_ctx, pair_c, dests_c = res[0], pair_sum(gc, res[1:], "c"), [(3, 4), every]
    res = _du_prenorm_bwd(segs, ddt, wi_main, wi_tail, xl, bmods_lat, dh1, 0, 512, "du_lat",
                          comm=_comm_exchange(pair_c, dests_c))
    (grad_x, acc_lat), recv_c = res[:2], list(res[2:])

    mine = _chip_sum(pair_b + pair_c + pair_a, recv_b + recv_c + recv_a, chip_arr, dests_b + dests_c + dests_a,
                     [0, 0, 0, 0, 1, 3, 4, 5, 2])
    dmod_lat = jnp.concatenate([acc_lat[0:2], acc_ffn[3:4], acc_ffn[0:2], acc_loss[0:1]], axis=0)
    misc = jnp.concatenate([(da_f + da_b)[0, :32], jnp.zeros((96,), F32), acc_dtb[0, :32], jnp.zeros((96,), F32),
                            jnp.sum(acc_loss[2]).reshape(1), jnp.zeros((D - 257,), F32)]).reshape(1, D)
    sv = jnp.concatenate([
        dmod_lat, acc_ctx[0:2], (acc_lat[2:3] + acc_ctx[2:3]), acc_ffn[2:3], acc_loss[1:2], acc_mix[2:3],
        acc_mix[0:1], acc_mix[1:2], dlb_f[0:1], dlb_b[0:1], acc_conv[0:6].reshape(12, D), misc,
        jnp.zeros((3, D), F32)], axis=0)
    res = _pair_swap(mine, sv)
    theirs, sv_all = res[:-1], res[-1].reshape(8, 32, D)
    whole = [jnp.concatenate([jnp.where(ic == 0, m_, t_), jnp.where(ic == 0, t_, m_)], axis=0)
             for m_, t_ in zip(mine, theirs)]
    g_w_in = lax.dynamic_slice(jnp.concatenate(whole[0:2], axis=0), (8 * chip, 0), (NSH, D))
    g_w_out = whole[2]
    g_w_gate = whole[3][:DFF // 4]
    g_w_up = whole[4][:DFF // 4]
    g_w_down = whole[5][:DFF // 4]
    ssum = _sum8(sv_all)
    dmod_rows = sv_all[:, 0:6].reshape(8, 6 * D)
    dmod_ctx_row = jnp.concatenate([ssum[6:8].reshape(1, 2 * D), jnp.zeros((1, 4 * D), F32)], axis=1)
    dmod_full = jnp.concatenate([dmod_rows, dmod_ctx_row, jnp.zeros((7, 6 * D), F32)], axis=0)
    grad_b_ada = jnp.sum(dmod_full, axis=0, keepdims=True)
    dmod_shard = lax.dynamic_slice(dmod_full, (0, chip * ncol_ada), (16, ncol_ada))
    g_w_ada, da_part = _ada_bwd(araw, dmod_shard, w_ada[0])
    da_all = _allgather8(da_part, "ada_ctx_gather").reshape(8, 16, D)[0::2, 8]
    cc = c_ctx.reshape(1, D)
    grad_c_ctx = (jnp.sum(da_all, axis=0, keepdims=True) * _dsilu(cc)).reshape(D)

    grad_norm_mix, grad_norm_ffn, grad_final_norm = ssum[8:9], ssum[9:10], ssum[10].reshape(D)
    grad_ssd_norm = ssum[11:12]
    grad_hgrn_norm = jnp.sum(ssum[12].reshape(NH, HF), axis=0, keepdims=True)
    grad_ssd_d = jnp.sum(ssum[13].reshape(SHEADS, SP), axis=1).reshape(1, SHEADS)
    lb_full = _sig(lbraw_full[0:2] - lbraw_full[2:4])
    dr0 = ssum[14:16] * lb_full * (1.0 - lb_full)
    grad_lb_full = jnp.stack([dr0, -dr0], axis=0)
    grad_lb = lax.dynamic_slice(grad_lb_full, (0, 0, chip * 256), (2, 2, 256))
    grad_conv_w = lax.dynamic_slice(ssum[16:26].reshape(KCONV, 2048), (0, chip * 512), (KCONV, 512)).reshape(1, KCONV, 512)
    grad_conv_b = ssum[26:28].reshape(1, 2048)
    a_val = -jnp.exp(ssd_a_log)
    grad_a_log = ssum[28, 0:32].reshape(1, 2, SHEADS) * a_val
    grad_dt_bias = ssum[28, 128:160].reshape(1, 2, SHEADS)
    loss = ssum[28, 256]

    small_w = [c_ctx, b_ada, norm_mix, conv_w, conv_b, ssd_a_log, ssd_dt_bias, ssd_d, ssd_norm, hgrn_lb_raw,
               hgrn_norm, norm_ffn, final_norm]
    small_m = [m_c_ctx, m_b_ada, m_norm_mix, m_conv_w, m_conv_b, m_ssd_a_log, m_ssd_dt_bias, m_ssd_d, m_ssd_norm,
               m_hgrn_lb_raw, m_hgrn_norm, m_norm_ffn, m_final_norm]
    small_v = [v_c_ctx, v_b_ada, v_norm_mix, v_conv_w, v_conv_b, v_ssd_a_log, v_ssd_dt_bias, v_ssd_d, v_ssd_norm,
               v_hgrn_lb_raw, v_hgrn_norm, v_norm_ffn, v_final_norm]
    small_g = [grad_c_ctx, grad_b_ada, grad_norm_mix, grad_conv_w, grad_conv_b, grad_a_log, grad_dt_bias, grad_ssd_d,
               grad_ssd_norm, grad_lb, grad_hgrn_norm, grad_norm_ffn, grad_final_norm]
    nrows = [-(-a.size // D) for a in small_w]
    packs = lambda lst: jnp.concatenate([_rows(a, r) for a, r in zip(lst, nrows)]
                                        + [jnp.zeros((24 - sum(nrows), D), F32)], axis=0)
    sd, sm, svv = _adamw(packs(small_w), packs(small_m), packs(small_v), packs(small_g), "adamw_small")

    def unpack(p):
        out, r0 = [], 0
        for a, r in zip(small_w, nrows):
            out.append(p[r0:r0 + r].reshape(-1)[:a.size].reshape(a.shape))
            r0 += r
        return out

    sd, sm, svv = unpack(sd), unpack(sm), unpack(svv)
    big = {}
    for nm, w_, m_, v_, g_ in (("w_ada", w_ada, m_w_ada, v_w_ada, g_w_ada), ("w_in", w_in, m_w_in, v_w_in, g_w_in),
                               ("w_out", w_out, m_w_out, v_w_out, g_w_out),
                               ("w_gate", w_gate, m_w_gate, v_w_gate, g_w_gate),
                               ("w_up", w_up, m_w_up, v_w_up, g_w_up),
                               ("w_down", w_down, m_w_down, v_w_down, g_w_down)):
        if nm in ("w_in", "w_gate", "w_up"):
            big[nm] = tuple(tr(t) for t in (g_[None],) + tuple(_adamw(tr(w_), tr(m_), tr(v_), g_, "adamw_" + nm)))
        else:
            big[nm] = (g_[None],) + tuple(_adamw(w_, m_, v_, g_, "adamw_" + nm))

    order = ["c_ctx", "w_ada", "b_ada", "norm_mix", "w_in", "conv_w", "conv_b", "ssd_a_log", "ssd_dt_bias", "ssd_d",
             "ssd_norm", "hgrn_lb_raw", "hgrn_norm", "w_out", "norm_ffn", "w_gate", "w_up", "w_down", "final_norm"]
    small_names = ["c_ctx", "b_ada", "norm_mix", "conv_w", "conv_b", "ssd_a_log", "ssd_dt_bias", "ssd_d", "ssd_norm",
                   "hgrn_lb_raw", "hgrn_norm", "norm_ffn", "final_norm"]
    table = dict(big)
    for k, nm in enumerate(small_names):
        table[nm] = (small_g[k].reshape(small_w[k].shape), sd[k], sm[k], svv[k])
    grads = [table[nm][0] for nm in order]
    deltas = [table[nm][1] for nm in order]
    new_m = [table[nm][2] for nm in order]
    new_v = [table[nm][3] for nm in order]
    return (loss, grad_x[None], *grads, *deltas, *new_m, *new_v)
```

```python
import functools
import math

import jax
import jax.numpy as jnp
from jax import lax
from jax.experimental import pallas as pl
from jax.experimental.pallas import tpu as pltpu

F32 = jnp.float32
BF16 = jnp.bfloat16
MXU_DTYPE = jnp.bfloat16
_INTERPRET = False

D = 1024
NH, HF = 8, 128
HC = 64
SC = 128
SN = 128
SHEADS, SP = 16, 64
GRID_W = 64
KCONV = 5
DFF = 2816
FSL = 768
DFFP = 4 * FSL
NIN = 8224
TB = 256
EPS = 1e-6
LR, B1, B2, AEPS, WD, STEP = 0.001, 0.9, 0.999, 1e-08, 0.01, 10
MESH_ID = pl.DeviceIdType.MESH
NSH = NIN // 4
WSL = 2048
WTAIL = 128


def _pcall(body, *, name, out_shape, grid=(), in_specs=None, out_specs=None, scratch=(), sem=None,
           vmem_mb=None, aliases=None):
    params = {}
    if sem is not None:
        params["dimension_semantics"] = sem
    if vmem_mb is not None:
        params["vmem_limit_bytes"] = vmem_mb << 20
    kw = dict(name=name, out_shape=out_shape, scratch_shapes=list(scratch),
              input_output_aliases=aliases or {}, compiler_params=pltpu.CompilerParams(**params),
              interpret=_INTERPRET)
    if grid:
        kw["grid"] = grid
    if in_specs is not None:
        kw["in_specs"] = in_specs
    if out_specs is not None:
        kw["out_specs"] = out_specs
    return pl.pallas_call(body, **kw)


def _mx(a):
    return a.astype(MXU_DTYPE)


def _dg(a, b, ca, cb):
    return lax.dot_general(_mx(a), _mx(b), (((ca,), (cb,)), ((), ())), preferred_element_type=F32)


def _nn(a, b):
    return _dg(a, b, 1, 0)


def _nt(a, b):
    return _dg(a, b, 1, 1)


def _tn(a, b):
    return _dg(a, b, 0, 0)


def _dot01(m, x, ways=3):
    f = lambda t: lax.dot_general(m, t, (((1,), (0,)), ((), ())), preferred_element_type=F32)
    hi = x.astype(BF16)
    r1 = x - hi.astype(F32)
    mid = r1.astype(BF16)
    if ways == 2:
        return f(hi) + f(mid)
    lo = (r1 - mid.astype(F32)).astype(BF16)
    return f(hi) + f(mid) + f(lo)


def _tri(n, upper):
    r = lax.broadcasted_iota(jnp.int32, (n, n), 0)
    c = lax.broadcasted_iota(jnp.int32, (n, n), 1)
    return (c >= r) if upper else (c <= r)


def _b01(mask):
    return jnp.where(mask, 1.0, 0.0).astype(BF16)


def _sig(x):
    return jax.nn.sigmoid(x)


def _silu(x):
    return x * _sig(x)


def _dsilu(x):
    s = _sig(x)
    return s * (1.0 + x * (1.0 - s))


def _softplus(x):
    return jnp.maximum(x, 0.0) + jnp.log(1.0 + jnp.exp(-jnp.abs(x)))


def _rowsum(x):
    return jnp.sum(x, axis=1, keepdims=True)


def _colsum(x):
    return jnp.sum(x, axis=0, keepdims=True)


def _full(shape):
    return pl.BlockSpec(shape, lambda *_: (0,) * len(shape))


def _allgather8_ops(x_ref, out_ref, send_sems, recv_sems, local_sem):
    m_per = x_ref.shape[0]
    x, y, c = lax.axis_index("x"), lax.axis_index("y"), lax.axis_index("c")
    me, sibling = (x, y, c), (x, y, 1 - c)
    chips = [(1 - x, y), (x, 1 - y), (1 - x, 1 - y)]

    def rows(px, py, pc):
        return out_ref.at[pl.ds((4 * px + 2 * py + pc) * m_per, m_per), :]

    def copy(k, block, to, src=None):
        return pltpu.make_async_remote_copy(
            src_ref=rows(*block) if src is None else src, dst_ref=rows(*block),
            send_sem=send_sems.at[k], recv_sem=recv_sems.at[k], device_id=to, device_id_type=MESH_ID)

    mine = pltpu.make_async_copy(x_ref, rows(*me), local_sem)
    mine.start()
    first = [copy(0, me, sibling, src=x_ref)]
    first += [copy(1 + j, me, (*chip, c), src=x_ref) for j, chip in enumerate(chips)]
    for cp in first:
        cp.start()
    passed = [copy(4 + j, (*chip, c), sibling) for j, chip in enumerate(chips)]
    for j, chip in enumerate(chips):
        copy(1 + j, (*chip, c), me).wait_recv()
        passed[j].start()
    copy(0, sibling, me).wait_recv()
    for j, chip in enumerate(chips):
        copy(4 + j, (*chip, 1 - c), me).wait_recv()
    for cp in first + passed:
        cp.wait_send()
    mine.wait()


_AG8_SEMS = [pltpu.SemaphoreType.DMA((7,)), pltpu.SemaphoreType.DMA((7,)), pltpu.SemaphoreType.DMA]


def _allgather8(v, name):
    m_per, n = v.shape
    return _pcall(
        functools.partial(_allgather8_ops), name=name, out_shape=jax.ShapeDtypeStruct((8 * m_per, n), v.dtype),
        in_specs=[pl.BlockSpec(memory_space=pltpu.VMEM)], out_specs=pl.BlockSpec(memory_space=pltpu.VMEM),
        scratch=list(_AG8_SEMS),
    )(v)


def _prologue(pack, cc_row, w_ada, b_shard, shards):
    n = len(shards)
    ncol = w_ada.shape[1]

    def body(pack_ref, cc_ref, w_ref, b_ref, *refs):
        ins = refs[:n]
        gath_ref, araw_ref, mod_ref = refs[n:n + 3]
        outs = refs[n + 3:2 * n + 3]
        modsh, s1, r1, l1, s2, r2, l2, gs, gr = refs[2 * n + 3:]
        start, forward, finish = _gather_ops(ins, outs, gs, gr, relay=True)
        start()
        _allgather8_ops(pack_ref, gath_ref, s1, r1, l1)
        a = jnp.concatenate([gath_ref[8 * i:8 * i + 1, :] for i in range(8)] + [cc_ref[...], jnp.zeros((7, D), F32)],
                            axis=0)
        araw_ref[...] = a
        modsh[...] = _nn(_silu(a), w_ref[...]) + b_ref[...]
        _allgather8_ops(modsh, mod_ref, s2, r2, l2)
        forward()
        finish()

    vm = pl.BlockSpec(memory_space=pltpu.VMEM)
    anyspec = pl.BlockSpec(memory_space=pl.ANY)
    return _pcall(
        body, name="prologue",
        out_shape=(jax.ShapeDtypeStruct((64, D), F32), jax.ShapeDtypeStruct((16, D), F32),
                   jax.ShapeDtypeStruct((128, ncol), F32)) + _gather_out(shards),
        in_specs=[vm, vm, vm, vm] + [anyspec] * n, out_specs=(vm, vm, vm) + (anyspec,) * n,
        scratch=[pltpu.VMEM((16, ncol), F32)] + list(_AG8_SEMS) + list(_AG8_SEMS) + _gather_sems(n), vmem_mb=40,
    )(pack, cc_row, w_ada, b_shard, *shards)


def _gather_ops(ins, outs, send_sems, recv_sems, relay=False):
    n = len(ins)
    x, y, c = lax.axis_index("x"), lax.axis_index("y"), lax.axis_index("c")
    me, sibling = (x, y, c), (x, y, 1 - c)
    chips = [(1 - x, y), (x, 1 - y), (1 - x, 1 - y)]
    direct = 2 if relay else 3

    def part(a, px, py, pc, quarter=None):
        half = ins[a].shape[0] // 2
        if quarter is None:
            return outs[a].at[2 * px + py, pl.ds(pc * half, half), :]
        return outs[a].at[2 * px + py, pl.ds(pc * half + quarter * (half // 2), half // 2), :]

    def copy(a, k, block, to, src=None, quarter=None):
        return pltpu.make_async_remote_copy(
            src_ref=part(a, *block, quarter) if src is None else src, dst_ref=part(a, *block, quarter),
            send_sem=send_sems.at[8 * a + k], recv_sem=recv_sems.at[8 * a + k], device_id=to,
            device_id_type=MESH_ID)

    def first(a, j):
        half = ins[a].shape[0] // 2
        return copy(a, j, me, (*chips[j], c), src=ins[a].at[pl.ds(c * half, half), :])

    relayed = lambda a, q: copy(a, 6 + q, (*chips[q], c), (*chips[1 - q], c), quarter=q)

    def start():
        for a in range(n):
            for j in range(direct):
                first(a, j).start()

    def forward():
        for a in range(n):
            for j in range(direct):
                copy(a, j, (*chips[j], c), me).wait_recv()
                copy(a, 3 + j, (*chips[j], c), sibling).start()
                if relay:
                    relayed(a, j).start()
            if relay:
                for q in range(2):
                    copy(a, 6 + q, (*chips[2], c), me, quarter=q).wait_recv()
                copy(a, 5, (*chips[2], c), sibling).start()

    def finish():
        for a in range(n):
            for j, chip in enumerate(chips):
                copy(a, 3 + j, (*chip, 1 - c), me).wait_recv()
        for a in range(n):
            for j, chip in enumerate(chips):
                if j < direct:
                    first(a, j).wait_send()
                    if relay:
                        relayed(a, j).wait_send()
                copy(a, 3 + j, (*chip, c), sibling).wait_send()

    return start, forward, finish


def _gather_out(shards):
    return tuple(jax.ShapeDtypeStruct((4,) + s_.shape, s_.dtype) for s_ in shards)


def _gather_sems(n):
    return [pltpu.SemaphoreType.DMA((8 * n,)), pltpu.SemaphoreType.DMA((8 * n,))]


def _pair_ops(ins, outs, send_sems, recv_sems):
    x, y, c = lax.axis_index("x"), lax.axis_index("y"), lax.axis_index("c")
    cps = []
    for a in range(len(ins)):
        half = ins[a].shape[1] // 2
        cps.append(pltpu.make_async_remote_copy(
            src_ref=ins[a].at[:, pl.ds((1 - c) * half, half), :], dst_ref=outs[a], send_sem=send_sems.at[a],
            recv_sem=recv_sems.at[a], device_id=(x, y, 1 - c), device_id_type=MESH_ID))

    def start():
        for cp in cps:
            cp.start()

    def finish():
        for cp in cps:
            cp.wait()

    return start, finish


def _comm_pair(gs):
    n = len(gs)
    return (list(gs), tuple(jax.ShapeDtypeStruct((g.shape[0], g.shape[1] // 2, g.shape[2]), g.dtype) for g in gs),
            [pltpu.SemaphoreType.DMA((n,)), pltpu.SemaphoreType.DMA((n,))], _pair_ops)


def _exchange_ops(ins, outs, send_sems, recv_sems, dests):
    x, y, c = lax.axis_index("x"), lax.axis_index("y"), lax.axis_index("c")
    mine = 2 * x + y
    chips = [(1 - x, y), (x, 1 - y), (1 - x, 1 - y)]

    def each(fn):
        for a in range(len(ins)):
            lo, hi = dests[a]
            for j, (px, py) in enumerate(chips):
                q = 2 * px + py
                cp = pltpu.make_async_remote_copy(
                    src_ref=ins[a].at[jnp.clip(q - lo, 0, hi - lo - 1)], dst_ref=outs[a].at[j],
                    send_sem=send_sems.at[3 * a + j], recv_sem=recv_sems.at[3 * a + j], device_id=(px, py, c),
                    device_id_type=MESH_ID)
                fn(cp, (q >= lo) & (q < hi), (mine >= lo) & (mine < hi), (lo, hi) == (0, 4))

    def start():
        def go(cp, send_ok, recv_ok, always):
            if always:
                cp.start()
            else:
                pl.when(send_ok)(cp.start)
        each(go)

    def finish():
        def go(cp, send_ok, recv_ok, always):
            if always:
                cp.wait()
            else:
                pl.when(send_ok)(cp.wait_send)
                pl.when(recv_ok)(cp.wait_recv)
        each(go)

    return start, finish


def _comm_exchange(hs, dests):
    n = len(hs)
    return (list(hs), tuple(jax.ShapeDtypeStruct((3,) + h.shape[1:], h.dtype) for h in hs),
            [pltpu.SemaphoreType.DMA((3 * n,)), pltpu.SemaphoreType.DMA((3 * n,))],
            lambda i, o, s, r: _exchange_ops(i, o, s, r, dests))


def _comm_gather(shards, relay=False):
    return (list(shards), _gather_out(shards), _gather_sems(len(shards)),
            lambda i, o, s, r: _gather_ops(i, o, s, r, relay))


def _carry(call, comm, steps):
    if comm is None:
        return call
    if isinstance(comm, list):
        for one in comm:
            call = _carry(call, one, steps)
        return call
    arrays, out_shape, sems, make = comm
    n, n_in, n_out = len(arrays), len(call["args"]), len(call["out_shape"])
    body = call["body"]

    def wrapped(*refs):
        base_in, cin = refs[:n_in], refs[n_in:n_in + n]
        rest = refs[n_in + n:]
        base_out, cout, scr = rest[:n_out], rest[n_out:n_out + n], rest[n_out + n:]
        ops = make(cin, cout, scr[-2], scr[-1])
        when = steps()
        pl.when(when[0])(ops[0])
        if len(ops) == 3 and len(when) == 3:
            pl.when(when[2])(ops[1])
        body(*base_in, *base_out, *scr[:-2])
        if len(ops) == 3 and len(when) == 2:
            pl.when(when[1])(ops[1])
        pl.when(when[1])(ops[-1])

    anyspec = pl.BlockSpec(memory_space=pl.ANY)
    return dict(call, body=wrapped, args=list(call["args"]) + arrays,
                in_specs=list(call["in_specs"]) + [anyspec] * n,
                out_shape=tuple(call["out_shape"]) + tuple(out_shape),
                out_specs=tuple(call["out_specs"]) + (anyspec,) * n,
                scratch=list(call["scratch"]) + sems)


def _run(call):
    args = call.pop("args")
    body = call.pop("body")
    return _pcall(body, **call)(*args)


def _pair_swap(rs, sv):
    n = len(rs)

    def body(sv_ref, *refs):
        ins, outs, got_ref = refs[:n], refs[n:2 * n], refs[2 * n]
        send_sems, recv_sems, s1, r1, l1 = refs[2 * n + 1:]
        x, y, c = lax.axis_index("x"), lax.axis_index("y"), lax.axis_index("c")
        cps = [pltpu.make_async_remote_copy(
            src_ref=ins[a], dst_ref=outs[a], send_sem=send_sems.at[a], recv_sem=recv_sems.at[a],
            device_id=(x, y, 1 - c), device_id_type=MESH_ID) for a in range(n)]
        for cp in cps:
            cp.start()
        _allgather8_ops(sv_ref, got_ref, s1, r1, l1)
        for cp in cps:
            cp.wait()

    vm, anyspec = pl.BlockSpec(memory_space=pltpu.VMEM), pl.BlockSpec(memory_space=pl.ANY)
    return _pcall(
        body, name="grads_pair_swap",
        out_shape=tuple(jax.ShapeDtypeStruct(r.shape, r.dtype) for r in rs)
        + (jax.ShapeDtypeStruct((8 * sv.shape[0], sv.shape[1]), sv.dtype),),
        in_specs=[vm] + [anyspec] * n, out_specs=(anyspec,) * n + (vm,),
        scratch=[pltpu.SemaphoreType.DMA((n,)), pltpu.SemaphoreType.DMA((n,))] + list(_AG8_SEMS),
    )(sv, *rs)


SUM_STEPS = 4


def _pair_sum(gs, recvs, core, name):
    n = len(gs)

    def body(c_ref, *refs):
        for a in range(n):
            refs[2 * n + a][...] = (refs[a][...].astype(F32) + refs[n + a][...].astype(F32)).astype(refs[2 * n + a].dtype)

    blk = lambda g: (g.shape[0], g.shape[1] // (2 * SUM_STEPS), g.shape[2])
    return pl.pallas_call(
        body, name=name,
        out_shape=tuple(jax.ShapeDtypeStruct((g.shape[0], g.shape[1] // 2, g.shape[2]), g.dtype) for g in gs),
        grid_spec=pltpu.PrefetchScalarGridSpec(
            num_scalar_prefetch=1, grid=(SUM_STEPS,),
            in_specs=[pl.BlockSpec(blk(g), lambda i, cr: (0, cr[0] * SUM_STEPS + i, 0)) for g in gs]
            + [pl.BlockSpec(blk(g), lambda i, cr: (0, i, 0)) for g in gs],
            out_specs=tuple(pl.BlockSpec(blk(g), lambda i, cr: (0, i, 0)) for g in gs)),
        compiler_params=pltpu.CompilerParams(vmem_limit_bytes=40 << 20), interpret=_INTERPRET,
    )(core, *gs, *recvs)


def _chip_sum(hs, recvs, chip, dests, slots):
    n = len(hs)
    nout = max(slots) + 1
    first = [slots.index(o) for o in range(nout)]
    every = lambda d_: d_ == (0, 4)

    def own(d_):
        if every(d_):
            return lambda i, kr: (kr[0], i, 0)
        return lambda i, kr: (0, jnp.where(kr[0] == d_[0], i, 0), 0)

    def got(d_):
        if every(d_):
            return lambda i, kr: (0, i, 0)
        return lambda i, kr: (0, jnp.where(kr[0] == d_[0], i, 0), 0)

    def body(k_ref, *refs):
        for a in range(n):
            def emit(a=a):
                acc = refs[a][0].astype(F32)
                for j in range(3):
                    acc = acc + refs[n + a][j].astype(F32)
                refs[2 * n + slots[a]][...] = acc
            if every(dests[a]):
                emit()
            else:
                pl.when(k_ref[0] == dests[a][0])(emit)

    rb = lambda h: h.shape[1] // SUM_STEPS
    return pl.pallas_call(
        body, name="grads_chip_sum",
        out_shape=tuple(jax.ShapeDtypeStruct(hs[a].shape[1:], F32) for a in first),
        grid_spec=pltpu.PrefetchScalarGridSpec(
            num_scalar_prefetch=1, grid=(SUM_STEPS,),
            in_specs=[pl.BlockSpec((1, rb(h), h.shape[2]), own(d_)) for h, d_ in zip(hs, dests)]
            + [pl.BlockSpec((3, rb(h), h.shape[2]), got(d_)) for h, d_ in zip(hs, dests)],
            out_specs=tuple(pl.BlockSpec((rb(hs[a]), hs[a].shape[2]), lambda i, kr: (i, 0)) for a in first)),
        compiler_params=pltpu.CompilerParams(vmem_limit_bytes=40 << 20), interpret=_INTERPRET,
    )(chip, *hs, *recvs)


def _ada_bwd(araw, dmod, w):
    nblk = w.shape[1] // 512

    def body(a_ref, d_ref, w_ref, gw_ref, da_ref):
        j = pl.program_id(0)
        gw_ref[...] = _tn(_silu(a_ref[...]), d_ref[...])
        part = _nt(d_ref[...], w_ref[...])

        @pl.when(j == 0)
        def _():
            da_ref[...] = part

        @pl.when(j > 0)
        def _():
            da_ref[...] += part

    return _pcall(
        body, name="ada_bwd",
        out_shape=(jax.ShapeDtypeStruct(w.shape, F32), jax.ShapeDtypeStruct((16, D), F32)), grid=(nblk,),
        in_specs=[_full((16, D)), pl.BlockSpec((16, 512), lambda j: (0, j)), pl.BlockSpec((D, 512), lambda j: (0, j))],
        out_specs=(pl.BlockSpec((D, 512), lambda j: (0, j)), _full((16, D))), sem=("arbitrary",),
    )(araw, dmod, w)


def _w_specs():
    return [pl.BlockSpec((None, D, D), lambda j, i: (j // 2, j % 2, 0)),
            pl.BlockSpec((None, WTAIL, D), lambda j, i: (jnp.maximum(j // 2 - 1, 0), 0, 0)),
            pl.BlockSpec((None, WTAIL, D), lambda j, i: (3, 0, 0))]


def _inproj(xin, mods, wi_main, wi_tail, t_total, tb, blk_off, prev, name, comm=None):
    n = xin.shape[0]
    nt = n // tb
    ncol = 8

    def body(x_ref, mod_ref, w_ref, wb_ref, wdt_ref, *rest):
        p_ref, pdt_ref, u_ref, uscr = rest[-4:]
        j, i = pl.program_id(0), pl.program_id(1)
        rows = pl.ds(pl.multiple_of(i * tb, tb), tb)

        @pl.when(j == 0)
        def _():
            xv = x_ref[...]
            r = lax.rsqrt(jnp.mean(xv * xv, axis=1, keepdims=True) + EPS)
            u = (xv * r * mod_ref[2:3, :]) * mod_ref[0:1, :] + mod_ref[1:2, :]
            ub = u.astype(MXU_DTYPE)
            uscr[rows, :] = ub
            u_ref[...] = ub
            pdt_ref[...] = _nt(ub, wdt_ref[...])

        ub = uscr[rows, :]
        pv = _nt(ub, w_ref[...])

        @pl.when((j % 2 == 1) | (j == 0))
        def _():
            p_ref[...] = pv.astype(p_ref.dtype)

        @pl.when((j % 2 == 0) & (j > 0))
        def _():
            head = pv[:, 0:WTAIL] + _nt(ub, wb_ref[...])
            p_ref[...] = jnp.concatenate([head, pv[:, WTAIL:]], axis=1).astype(p_ref.dtype)

    once = lambda j, i: (jnp.where(j == 0, i, nt - 1) + blk_off, 0)
    in_specs = [pl.BlockSpec((tb, D), lambda j, i: (jnp.where(j == 0, i, nt - 1), 0)), _full((8, D))] + _w_specs()
    args = [xin, mods, wi_main, wi_tail, wi_tail]
    aliases = None
    if prev is not None:
        in_specs += [pl.BlockSpec(memory_space=pl.ANY)] * 3
        args += list(prev)
        aliases = {5: 0, 6: 1, 7: 2}
    call = dict(
        body=body, args=args, name=name,
        out_shape=(jax.ShapeDtypeStruct((t_total, ncol * D), MXU_DTYPE), jax.ShapeDtypeStruct((t_total, 128), F32),
                   jax.ShapeDtypeStruct((t_total, D), MXU_DTYPE)),
        grid=(ncol, nt), in_specs=in_specs,
        out_specs=(pl.BlockSpec((tb, D), lambda j, i: (i + blk_off, j)), pl.BlockSpec((tb, 128), once),
                   pl.BlockSpec((tb, D), once)),
        scratch=[pltpu.VMEM((n, D), MXU_DTYPE)], sem=("arbitrary", "arbitrary"), vmem_mb=48, aliases=aliases)
    steps = lambda: ((pl.program_id(0) == 0) & (pl.program_id(1) == 0),
                     (pl.program_id(0) == ncol - 1) & (pl.program_id(1) == nt - 1),
                     (pl.program_id(0) == ncol - 1) & (pl.program_id(1) == 0))
    return _run(_carry(call, comm, steps))


def _blk(s, nb, rev):
    return jnp.where(s == 0, nb - 1, (nb - 1 - s) if rev else (s - 1))


def _hgrn_gate(fr, lbraw_ref, d):
    lb = _sig(lbraw_ref[d:d + 1, :] - lbraw_ref[2 + d:3 + d, :])
    sg = _sig(fr)
    return lb, sg, lb + (1.0 - lb) * sg


def _hgrn_fwd(p_main, lbraw, d, nb, comm=None):
    t_total = p_main.shape[0]
    rev = d == 1
    nch = TB // HC
    scale = HF ** -0.5

    def body(q_ref, f_ref, v_ref, lb_ref, o_ref, sp_ref, st):
        s = pl.program_id(0)

        @pl.when(s == 0)
        def _():
            st[...] = jnp.zeros_like(st)

        mb = _tri(HC, rev)
        m01 = _b01(mb)
        order = list(reversed(range(nch)) if rev else range(nch))
        hs_ = [slice(h * HF, (h + 1) * HF) for h in range(NH)]
        pre = {}
        for c in order:
            rows = slice(c * HC, (c + 1) * HC)
            _, _, f = _hgrn_gate(f_ref[rows, :].astype(F32), lb_ref, d)
            k = 1.0 - f
            cum = _dot01(m01, jnp.log(f))
            tot = cum[0:1, :] if rev else cum[HC - 1:HC, :]
            qd = _silu(q_ref[rows, :].astype(F32)) * scale * jnp.exp(cum)
            ki = k * jnp.exp(-cum)
            etot = jnp.exp(tot)
            pre[c] = (_mx(qd), _mx(ki), _mx(ki * etot), _mx(v_ref[rows, :]), etot)
        scs = {c: [_nt(pre[c][0][:, cs], pre[c][1][:, cs]) for cs in hs_] for c in order}
        upd = {c: [_tn(pre[c][3][:, cs], pre[c][2][:, cs]) for cs in hs_] for c in order}
        intra = {c: [_nn(jnp.where(mb, scs[c][h], 0.0), pre[c][3][:, cs]) for h, cs in enumerate(hs_)] for c in order}
        for c in order:
            rows = slice(c * HC, (c + 1) * HC)
            qdb, etot = pre[c][0], pre[c][4]
            for h, cs in enumerate(hs_):
                sth = st[h]
                stb = sth.astype(sp_ref.dtype)
                sp_ref[c, h] = stb
                o_ref[rows, cs] = (intra[c][h] + _nt(qdb[:, cs], stb)).astype(o_ref.dtype)
                st[h] = sth * etot[:, cs] + upd[c][h]

    col = lambda j: (lambda s: (_blk(s, nb, rev), j))
    call = dict(
        body=body, args=[p_main, p_main, p_main, lbraw], name=f"hgrn_fwd_{d}",
        out_shape=(jax.ShapeDtypeStruct((t_total, D), MXU_DTYPE),
                   jax.ShapeDtypeStruct((nch * nb, NH, HF, HF), MXU_DTYPE)),
        grid=(nb,),
        in_specs=[pl.BlockSpec((TB, D), col(0)), pl.BlockSpec((TB, D), col(1 + d)), pl.BlockSpec((TB, D), col(3)),
                  _full((8, D))],
        out_specs=(pl.BlockSpec((TB, D), col(0)),
                   pl.BlockSpec((nch, NH, HF, HF), lambda s: (_blk(s, nb, rev), 0, 0, 0))),
        scratch=[pltpu.VMEM((NH, HF, HF), F32)], sem=("arbitrary",), vmem_mb=40)
    return _run(_carry(call, comm, lambda: (pl.program_id(0) == 0, pl.program_id(0) == nb - 1,
                                            pl.program_id(0) == nb - 4)))


def _hgrn_bwd(p_main, lbraw, sprev, do, d, nb, prev, comm=None):
    t_total = p_main.shape[0]
    rev = d == 1
    nch = TB // HC
    scale = HF ** -0.5
    last = prev is not None
    odt = MXU_DTYPE if last else F32

    def body(q_ref, f_ref, v_ref, lb_ref, sp_ref, do_ref, *rest):
        if last:
            dqp_ref, dvp_ref = rest[:2]
            rest = rest[2:]
        dq_ref, df_ref, dv_ref, dlb_ref, dst = rest
        sp_id = pl.program_id(0)
        is_ctx = sp_id == nb - 1

        @pl.when(sp_id == 0)
        def _():
            dst[...] = jnp.zeros_like(dst)
            dlb_ref[...] = jnp.zeros_like(dlb_ref)

        mb = _tri(HC, rev)
        mbt = _tri(HC, not rev)
        m01 = _b01(mb)
        mt01 = _b01(mbt)
        order = list(range(nch) if rev else reversed(range(nch)))
        hs_ = [slice(h * HF, (h + 1) * HF) for h in range(NH)]
        pre = {}
        for c in order:
            rows = slice(c * HC, (c + 1) * HC)
            lb, sg, f = _hgrn_gate(f_ref[rows, :].astype(F32), lb_ref, d)
            k = 1.0 - f
            cum = _dot01(m01, jnp.log(f))
            tot = cum[0:1, :] if rev else cum[HC - 1:HC, :]
            e = jnp.exp(cum)
            ei = jnp.exp(-cum)
            etot = jnp.exp(tot)
            ee = ei * etot
            qraw = q_ref[rows, :].astype(F32)
            sq = _sig(qraw)
            qd = qraw * sq * scale * e
            ki = k * ei
            ke = k * ee
            dov = jnp.where(is_ctx, 0.0, do_ref[rows, :].astype(F32))
            pre[c] = dict(lb=lb, sg=sg, f=f, e=e, ei=ei, ee=ee, etot=etot, qd=qd, ki=ki, ke=ke,
                          dsq=sq * (1.0 + qraw * (1.0 - sq)),
                          qdb=_mx(qd), kib=_mx(ki), keb=_mx(ke), vb=_mx(v_ref[rows, :]), dob=_mx(dov))
        units = [(c, h) for c in order for h in range(NH)]
        col = lambda u, key: pre[u[0]][key][:, hs_[u[1]]]
        pt = {u: jnp.where(mbt, _nt(col(u, "kib"), col(u, "qdb")), 0.0) for u in units}
        dp = {u: jnp.where(mb, _nt(col(u, "dob"), col(u, "vb")), 0.0) for u in units}
        dpt = {u: jnp.where(mbt, _nt(col(u, "vb"), col(u, "dob")), 0.0) for u in units}
        dv_i = {u: _nn(pt[u], col(u, "dob")) for u in units}
        dqd_ = {u: _nn(dp[u], col(u, "kib")) + _nn(col(u, "dob"), sp_ref[u[0], u[1]]) for u in units}
        dki_ = {u: _nn(dpt[u], col(u, "qdb")) for u in units}
        dsl = {u: _tn(col(u, "dob"), col(u, "qdb")) for u in units}
        for c in order:
            rows = slice(c * HC, (c + 1) * HC)
            p = pre[c]
            dv_l, dke_l, dtot_l = [], [], []
            for h, cs in enumerate(hs_):
                dso = dst[h]
                dsob = _mx(dso)
                dv_l.append(dv_i[(c, h)] + _nt(p["keb"][:, cs], dsob))
                dke_l.append(_nn(p["vb"][:, cs], dsob))
                dtot_l.append(_colsum(dso * sp_ref[c, h].astype(F32)) * p["etot"][:, cs])
                dst[h] = dso * p["etot"][:, cs] + dsl[(c, h)]
            lb, sg, f, e, ei, ee, qd, ki, ke = (p[n_] for n_ in ("lb", "sg", "f", "e", "ei", "ee", "qd", "ki", "ke"))
            dqd = jnp.concatenate([dqd_[(c, h)] for h in range(NH)], axis=1)
            dki = jnp.concatenate([dki_[(c, h)] for h in range(NH)], axis=1)
            dke = jnp.concatenate(dke_l, axis=1)
            dcum = dqd * qd - dki * ki - dke * ke
            dtot = jnp.concatenate(dtot_l, axis=1) + _colsum(dke * ke)
            dk = dki * ei + dke * ee
            dlf = _dot01(mt01, dcum, ways=2) + dtot
            df = dlf / f - dk
            dlb_ref[0:1, :] += _colsum(df * (1.0 - sg))
            dfr = df * (1.0 - lb) * sg * (1.0 - sg)
            dq = dqd * e * scale * p["dsq"]
            dv = jnp.concatenate(dv_l, axis=1)
            if last:
                dq = dq + dqp_ref[rows, :]
                dv = dv + dvp_ref[rows, :]
            dq_ref[rows, :] = dq.astype(odt)
            dv_ref[rows, :] = dv.astype(odt)
            df_ref[rows, :] = dfr.astype(MXU_DTYPE)

    blk = lambda s: _blk(nb - 1 - s, nb, rev)
    col = lambda j: (lambda s: (blk(s), j))
    in_specs = [pl.BlockSpec((TB, D), col(0)), pl.BlockSpec((TB, D), col(1 + d)), pl.BlockSpec((TB, D), col(3)),
                _full((8, D)), pl.BlockSpec((nch, NH, HF, HF), lambda s: (blk(s), 0, 0, 0)),
                pl.BlockSpec((TB, D), lambda s: (jnp.minimum(blk(s), nb - 2), 0))]
    args = [p_main, p_main, p_main, lbraw, sprev, do]
    if last:
        in_specs += [pl.BlockSpec((TB, D), col(0))] * 2
        args += list(prev)
    call = dict(
        body=body, args=args, name=f"hgrn_bwd_{d}",
        out_shape=(jax.ShapeDtypeStruct((t_total, D), odt), jax.ShapeDtypeStruct((t_total, D), MXU_DTYPE),
                   jax.ShapeDtypeStruct((t_total, D), odt), jax.ShapeDtypeStruct((8, D), F32)),
        grid=(nb,), in_specs=in_specs,
        out_specs=(pl.BlockSpec((TB, D), col(0)), pl.BlockSpec((TB, D), col(0)), pl.BlockSpec((TB, D), col(0)),
                   _full((8, D))),
        scratch=[pltpu.VMEM((NH, HF, HF), F32)], sem=("arbitrary",), vmem_mb=48)
    return _run(_carry(call, comm, lambda: (pl.program_id(0) == 0, pl.program_id(0) == nb - 1)))


def _conv_masks(tb, is_ctx):
    seg = jnp.where(is_ctx, tb, GRID_W)
    pos = lax.broadcasted_iota(jnp.int32, (tb, 1), 0) & (seg - 1)
    return pos, seg


def _shift_rows(x, dshift, pos, seg):
    if dshift == 0:
        return x
    n = x.shape[0]
    rolled = pltpu.roll(x, (-dshift) % n, 0)
    ok = (pos + dshift >= 0) & (pos + dshift < seg)
    return jnp.where(ok, rolled, 0.0)


def _ssd_prep(p_main, p_dt, convp, dtb, nb):
    t_total = p_main.shape[0]

    def body(x_ref, dt_ref, cw_ref, dtb_ref, xa_ref, ds_ref, dts_ref):
        is_ctx = pl.program_id(0) == nb - 1
        pos, seg = _conv_masks(TB, is_ctx)
        xv = x_ref[...].astype(F32)
        acc = cw_ref[5:6, :] + cw_ref[2:3, :] * xv
        for kk in (0, 1, 3, 4):
            acc = acc + cw_ref[kk:kk + 1, :] * _shift_rows(xv, kk - 2, pos, seg)
        sg = _sig(acc)
        xa_ref[...] = (acc * sg).astype(xa_ref.dtype)
        ds_ref[...] = (sg * (1.0 + acc * (1.0 - sg))).astype(ds_ref.dtype)
        dts_ref[...] = _softplus(dt_ref[...] + dtb_ref[0:1, :])

    wide = pl.BlockSpec((TB, 2048), lambda i: (i, 0))
    return _pcall(
        body, name="ssd_prep",
        out_shape=(jax.ShapeDtypeStruct((t_total, 2048), MXU_DTYPE), jax.ShapeDtypeStruct((t_total, 2048), MXU_DTYPE),
                   jax.ShapeDtypeStruct((t_total, 128), F32)),
        grid=(nb,),
        in_specs=[pl.BlockSpec((TB, 2048), lambda i: (i, 3)), pl.BlockSpec((TB, 128), lambda i: (i, 0)),
                  _full((8, 2048)), _full((8, 128))],
        out_specs=(wide, wide, pl.BlockSpec((TB, 128), lambda i: (i, 0))),
        sem=("parallel",), vmem_mb=32,
    )(p_main, p_dt, convp, dtb)


def _ssd_prep_bwd(p_main, p_dt, convp, dtb, dsl, dxa, dxs_skip, ddts, nb):
    t_total = p_main.shape[0]

    def body(x_ref, dt_ref, cw_ref, dtb_ref, ds_ref, dxa_ref, dsk_ref, ddts_ref, dx_ref, ddt_ref, dcw_ref, ddtb_ref):
        i = pl.program_id(0)
        is_ctx = i == nb - 1

        @pl.when(i == 0)
        def _():
            dcw_ref[...] = jnp.zeros_like(dcw_ref)
            ddtb_ref[...] = jnp.zeros_like(ddtb_ref)

        pos, seg = _conv_masks(TB, is_ctx)
        xv = x_ref[...].astype(F32)
        dact = dxa_ref[...]
        dact = jnp.concatenate([dact[:, :D] + jnp.where(is_ctx, 0.0, dsk_ref[...].astype(F32)), dact[:, D:]], axis=1)
        dpre = dact * ds_ref[...].astype(F32)
        dxv = cw_ref[2:3, :] * dpre
        dcw_ref[2:3, :] += _colsum(xv * dpre)
        for kk in (0, 1, 3, 4):
            sdp = _shift_rows(dpre, 2 - kk, pos, seg)
            dxv = dxv + cw_ref[kk:kk + 1, :] * sdp
            dcw_ref[kk:kk + 1, :] += _colsum(xv * sdp)
        dx_ref[...] = dxv.astype(dx_ref.dtype)
        dcw_ref[5:6, :] += _colsum(dpre)
        draw = ddts_ref[...] * _sig(dt_ref[...] + dtb_ref[0:1, :])
        ddt_ref[...] = draw.astype(ddt_ref.dtype)
        ddtb_ref[0:1, :] += _colsum(draw)

    return _pcall(
        body, name="ssd_prep_bwd",
        out_shape=(jax.ShapeDtypeStruct((t_total, 2048), MXU_DTYPE), jax.ShapeDtypeStruct((t_total, 128), MXU_DTYPE),
                   jax.ShapeDtypeStruct((8, 2048), F32), jax.ShapeDtypeStruct((8, 128), F32)),
        grid=(nb,),
        in_specs=[pl.BlockSpec((TB, 2048), lambda i: (i, 3)), pl.BlockSpec((TB, 128), lambda i: (i, 0)),
                  _full((8, 2048)), _full((8, 128)), pl.BlockSpec((TB, 2048), lambda i: (i, 0)),
                  pl.BlockSpec((TB, 2048), lambda i: (i, 0)),
                  pl.BlockSpec((TB, D), lambda i: (jnp.minimum(i, nb - 2), 0)),
                  pl.BlockSpec((TB, 128), lambda i: (i, 0))],
        out_specs=(pl.BlockSpec((TB, 2048), lambda i: (i, 0)), pl.BlockSpec((TB, 128), lambda i: (i, 0)),
                   _full((8, 2048)), _full((8, 128))),
        sem=("arbitrary",), vmem_mb=40,
    )(p_main, p_dt, convp, dtb, dsl, dxa, dxs_skip, ddts)


def _dot2(x, m01):
    hi = x.astype(BF16)
    lo = (x - hi.astype(F32)).astype(BF16)
    f = lambda t: lax.dot_general(t, m01, (((1,), (0,)), ((), ())), preferred_element_type=F32)
    return f(hi) + f(lo)


def _head_lanes(c0, c1):
    p = lax.broadcasted_iota(jnp.int32, (128, 128), 0)
    l = lax.broadcasted_iota(jnp.int32, (128, 128), 1)
    return _b01(((l == c0) & (p < SP)) | ((l == c1) & (p >= SP)))


def _one_lane(col):
    return _b01(lax.broadcasted_iota(jnp.int32, (128, 128), 1) == col)


def _lane_pick(x, lane, col):
    return _rowsum(jnp.where(lane == col, x, 0.0))


def _ssd_chunk_common(dts, alog_ref, m01, rev):
    lane = lax.broadcasted_iota(jnp.int32, (1, 128), 1)
    arow = -jnp.exp(alog_ref[0:1, :])
    cum = _dot01(m01, dts * arow)
    tot = cum[0:1, :] if rev else cum[SC - 1:SC, :]
    return lane, arow, cum, cum.T, tot


def _ssd_fwd(xa, dts, alog, d, nb):
    t_total = xa.shape[0]
    rev = d == 1
    nch = TB // SC
    npair = SHEADS // 2

    def body(xa_ref, dts_ref, alog_ref, y_ref, sp_ref, st):
        s = pl.program_id(0)

        @pl.when(s == 0)
        def _():
            st[...] = jnp.zeros_like(st)

        mb = _tri(SC, rev)
        m01 = _b01(mb)
        lo = lax.broadcasted_iota(jnp.int32, (1, 128), 1) < SP
        rlo = lax.broadcasted_iota(jnp.int32, (128, 1), 0) < SP
        order = list(reversed(range(nch)) if rev else range(nch))
        pre = {}
        for c in order:
            rows = slice(c * SC, (c + 1) * SC)
            dts_c = dts_ref[rows, :]
            lane, arow, cum, cumt, tot = _ssd_chunk_common(dts_c, alog_ref, m01, rev)
            bgs = [_mx(xa_ref[rows, D + g * SN:D + (g + 1) * SN]) for g in range(4)]
            cgs = [_mx(xa_ref[rows, D + 512 + g * SN:D + 512 + (g + 1) * SN]) for g in range(4)]
            pairs = []
            for pr in range(npair):
                xs = xa_ref[rows, pr * 128:(pr + 1) * 128].astype(F32)
                cols = [16 * d + 2 * pr, 16 * d + 2 * pr + 1]
                cum_c = [_lane_pick(cum, lane, q) for q in cols]
                dt_c = [_lane_pick(dts_c, lane, q) for q in cols]
                tot_c = [_lane_pick(tot, lane, q) for q in cols]
                dtx = xs * jnp.where(lo, dt_c[0], dt_c[1])
                e1_pair = jnp.where(lo, jnp.exp(cum_c[0]), jnp.exp(cum_c[1]))
                e2_pair = jnp.where(lo, jnp.exp(tot_c[0] - cum_c[0]), jnp.exp(tot_c[1] - cum_c[1]))
                etot_col = jnp.where(rlo, jnp.exp(tot_c[0]), jnp.exp(tot_c[1]))
                decs = [jnp.where(mb, jnp.exp(cum_c[q] - cumt[cols[q]:cols[q] + 1, :]), 0.0) for q in range(2)]
                dtxq = [_mx(jnp.where(lo if q == 0 else ~lo, dtx, 0.0)) for q in range(2)]
                pairs.append(dict(e1=e1_pair, etot=etot_col, decs=decs, dtxq=dtxq, xe=_mx(dtx * e2_pair)))
            pre[c] = (bgs, cgs, pairs)
        gm = {(c, g): _nt(pre[c][1][g], pre[c][0][g]) for c in order for g in range(4)}
        upd = {(c, pr): _tn(pre[c][2][pr]["xe"], pre[c][0][pr // 2]) for c in order for pr in range(npair)}
        intra = {(c, pr): sum(_nn(gm[(c, pr // 2)] * pre[c][2][pr]["decs"][q], pre[c][2][pr]["dtxq"][q]) for q in range(2))
                 for c in order for pr in range(npair)}
        for c in order:
            rows = slice(c * SC, (c + 1) * SC)
            bgs, cgs, pairs = pre[c]
            for pr in range(npair):
                stp = st[pr]
                stb = stp.astype(sp_ref.dtype)
                sp_ref[c, pr] = stb
                y_ref[rows, pr * 128:(pr + 1) * 128] = (
                    intra[(c, pr)] + pairs[pr]["e1"] * _nt(cgs[pr // 2], stb)).astype(y_ref.dtype)
                st[pr] = stp * pairs[pr]["etot"] + upd[(c, pr)]

    blk = lambda s: _blk(s, nb, rev)
    return _pcall(
        body, name=f"ssd_fwd_{d}",
        out_shape=(jax.ShapeDtypeStruct((t_total, D), MXU_DTYPE),
                   jax.ShapeDtypeStruct((nch * nb, npair, 128, SN), MXU_DTYPE)),
        grid=(nb,),
        in_specs=[pl.BlockSpec((TB, 2048), lambda s: (blk(s), 0)), pl.BlockSpec((TB, 128), lambda s: (blk(s), 0)),
                  _full((8, 128))],
        out_specs=(pl.BlockSpec((TB, D), lambda s: (blk(s), 0)),
                   pl.BlockSpec((nch, npair, 128, SN), lambda s: (blk(s), 0, 0, 0))),
        scratch=[pltpu.VMEM((npair, 128, SN), F32)], sem=("arbitrary",), vmem_mb=40,
    )(xa, dts, alog)


def _ssd_bwd(xa, dts, alog, sprev, dy, d, nb, prev, comm=None):
    t_total = xa.shape[0]
    rev = d == 1
    nch = TB // SC
    npair = SHEADS // 2
    last = prev is not None

    def body(xa_ref, dts_ref, alog_ref, sp_ref, dy_ref, *rest):
        if last:
            dxp_ref, ddp_ref = rest[:2]
            rest = rest[2:]
        dxa_ref, ddts_ref, da_ref, dst, zc_scr = rest
        sp_id = pl.program_id(0)
        is_ctx = sp_id == nb - 1

        @pl.when(sp_id == 0)
        def _():
            dst[...] = jnp.zeros_like(dst)
            da_ref[...] = jnp.zeros_like(da_ref)
            zc_scr[...] = jnp.zeros_like(zc_scr)

        mb = _tri(SC, rev)
        m01 = _b01(mb)
        mt01 = _b01(_tri(SC, not rev))
        lo = lax.broadcasted_iota(jnp.int32, (1, 128), 1) < SP
        rlo = lax.broadcasted_iota(jnp.int32, (128, 1), 0) < SP
        order = list(range(nch) if rev else reversed(range(nch)))
        pre = {}
        for c in order:
            rows = slice(c * SC, (c + 1) * SC)
            dts_c = dts_ref[rows, :]
            lane, arow, cum, cumt, tot = _ssd_chunk_common(dts_c, alog_ref, m01, rev)
            pairs = []
            for pr in range(npair):
                xs = xa_ref[rows, pr * 128:(pr + 1) * 128].astype(F32)
                dyp = jnp.where(is_ctx, 0.0, dy_ref[rows, pr * 128:(pr + 1) * 128].astype(F32))
                cols = [16 * d + 2 * pr, 16 * d + 2 * pr + 1]
                cum_c = [_lane_pick(cum, lane, q) for q in cols]
                dt_c = [_lane_pick(dts_c, lane, q) for q in cols]
                tot_c = [_lane_pick(tot, lane, q) for q in cols]
                e1_c = [jnp.exp(cum_c[q]) for q in range(2)]
                e2_c = [jnp.exp(tot_c[q] - cum_c[q]) for q in range(2)]
                etot_c = [jnp.exp(tot_c[q]) for q in range(2)]
                dt_pair = jnp.where(lo, dt_c[0], dt_c[1])
                e1_pair = jnp.where(lo, e1_c[0], e1_c[1])
                e2_pair = jnp.where(lo, e2_c[0], e2_c[1])
                dtx = xs * dt_pair
                decs = [jnp.where(mb, jnp.exp(cum_c[q] - cumt[cols[q]:cols[q] + 1, :]), 0.0) for q in range(2)]
                dyq = [_mx(jnp.where(lo if q == 0 else ~lo, dyp, 0.0)) for q in range(2)]
                pairs.append(dict(xs=xs, dyp=dyp, cols=cols, e1_c=e1_c, e2_c=e2_c, etot_c=etot_c, dt_pair=dt_pair,
                                  e2_pair=e2_pair, etot_col=jnp.where(rlo, etot_c[0], etot_c[1]), dtx=dtx,
                                  dtxb=_mx(dtx), xeb=_mx(dtx * e2_pair), dy0b=_mx(dyp * e1_pair), decs=decs, dyq=dyq))
            pre[c] = dict(lane=lane, arow=arow, dts=dts_c, pairs=pairs, cum=cum, tot=tot,
                          bgb=[_mx(xa_ref[rows, D + g * SN:D + (g + 1) * SN]) for g in range(4)],
                          cgb=[_mx(xa_ref[rows, D + 512 + g * SN:D + 512 + (g + 1) * SN]) for g in range(4)])
        units = [(c, pr) for c in order for pr in range(npair)]
        P = lambda u: pre[u[0]]["pairs"][u[1]]
        cgu = lambda u: pre[u[0]]["cgb"][u[1] // 2]
        gm = {(c, g): _nt(pre[c]["cgb"][g], pre[c]["bgb"][g]) for c in order for g in range(4)}
        y0 = {u: _nt(cgu(u), sp_ref[u[0], u[1]]) for u in units}
        dcg_i = {u: _nn(P(u)["dy0b"], sp_ref[u[0], u[1]]) for u in units}
        dsl = {u: _tn(P(u)["dy0b"], cgu(u)) for u in units}
        w_ = {(u, q): gm[(u[0], u[1] // 2)] * P(u)["decs"][q] for u in units for q in range(2)}
        dw_ = {(u, q): jnp.where(mb, _nt(P(u)["dyq"][q], P(u)["dtxb"]), 0.0) for u in units for q in range(2)}
        ddtx_i = {(u, q): _tn(w_[(u, q)], P(u)["dyq"][q]) for u in units for q in range(2)}
        for c in order:
            rows = slice(c * SC, (c + 1) * SC)
            pc = pre[c]
            lane, arow, dts_c = pc["lane"], pc["arow"], pc["dts"]
            d1 = jnp.zeros((SC, 128), F32)
            d2 = jnp.zeros((SC, 128), F32)
            dz = jnp.zeros((SC, 128), F32)
            ddt = jnp.zeros((SC, 128), F32)
            dtot = jnp.zeros((1, 128), F32)
            dgm = [jnp.zeros((SC, SC), F32) for _ in range(4)]
            dbg = [jnp.zeros((SC, SN), F32) for _ in range(4)]
            dcg = [jnp.zeros((SC, SN), F32) for _ in range(4)]
            for pr in range(npair):
                u, g, p = (c, pr), pr // 2, pc["pairs"][pr]
                hs = _head_lanes(*p["cols"])
                dso = dst[pr]
                dsob = _mx(dso)
                dxe = _nt(pc["bgb"][g], dsob)
                dbg[g] = dbg[g] + _nn(p["xeb"], dsob)
                ddtx = dxe * p["e2_pair"]
                d2 = d2 + _dot2(dxe * p["dtx"], hs)
                dcg[g] = dcg[g] + dcg_i[u]
                d1 = d1 + _dot2(p["dyp"] * y0[u], hs)
                sprod = dso * sp_ref[c, pr].astype(F32)
                dst[pr] = dso * p["etot_col"] + dsl[u]
                for q in range(2):
                    hm = lo if q == 0 else ~lo
                    col = p["cols"][q]
                    dw = dw_[(u, q)]
                    ddtx = ddtx + jnp.where(hm, ddtx_i[(u, q)], 0.0)
                    dgm[g] = dgm[g] + dw * p["decs"][q]
                    z = dw * w_[(u, q)]
                    dz = dz + _dot2(z, _one_lane(col))
                    zc_scr[col:col + 1, :] = _colsum(z)
                    tsum = _rowsum(_colsum(sprod[q * SP:(q + 1) * SP, :]))
                    dtot = jnp.where(lane == col, tsum * p["etot_c"][q], dtot)
                dxs = ddtx * p["dt_pair"]
                ddt = ddt + _dot2(ddtx * p["xs"], hs)
                if last:
                    dxs = dxs + dxp_ref[rows, pr * 128:(pr + 1) * 128]
                dxa_ref[rows, pr * 128:(pr + 1) * 128] = dxs
            e2_all = jnp.exp(pc["tot"] - pc["cum"])
            dcum = dz - zc_scr[...].T + d1 * jnp.exp(pc["cum"]) - d2 * e2_all
            dtot = dtot + _colsum(d2 * e2_all)
            for g in range(4):
                db = dbg[g] + _tn(dgm[g], pc["cgb"][g])
                dc = dcg[g] + _nn(dgm[g], pc["bgb"][g])
                if last:
                    db = db + dxp_ref[rows, D + g * SN:D + (g + 1) * SN]
                    dc = dc + dxp_ref[rows, D + 512 + g * SN:D + 512 + (g + 1) * SN]
                dxa_ref[rows, D + g * SN:D + (g + 1) * SN] = db
                dxa_ref[rows, D + 512 + g * SN:D + 512 + (g + 1) * SN] = dc
            dla = _dot01(mt01, dcum, ways=2) + dtot
            ddt = ddt + dla * arow
            da_ref[0:1, :] += _colsum(dla * dts_c)
            if last:
                ddt = ddt + ddp_ref[rows, :]
            ddts_ref[rows, :] = ddt

    blk = lambda s: _blk(nb - 1 - s, nb, rev)
    in_specs = [pl.BlockSpec((TB, 2048), lambda s: (blk(s), 0)), pl.BlockSpec((TB, 128), lambda s: (blk(s), 0)),
                _full((8, 128)), pl.BlockSpec((nch, npair, 128, SN), lambda s: (blk(s), 0, 0, 0)),
                pl.BlockSpec((TB, D), lambda s: (jnp.minimum(blk(s), nb - 2), 0))]
    args = [xa, dts, alog, sprev, dy]
    if last:
        in_specs += [pl.BlockSpec((TB, 2048), lambda s: (blk(s), 0)), pl.BlockSpec((TB, 128), lambda s: (blk(s), 0))]
        args += list(prev)
    call = dict(
        body=body, args=args, name=f"ssd_bwd_{d}",
        out_shape=(jax.ShapeDtypeStruct((t_total, 2048), F32), jax.ShapeDtypeStruct((t_total, 128), F32),
                   jax.ShapeDtypeStruct((8, 128), F32)),
        grid=(nb,), in_specs=in_specs,
        out_specs=(pl.BlockSpec((TB, 2048), lambda s: (blk(s), 0)), pl.BlockSpec((TB, 128), lambda s: (blk(s), 0)),
                   _full((8, 128))),
        scratch=[pltpu.VMEM((npair, 128, SN), F32), pltpu.VMEM((128, 128), F32)], sem=("arbitrary",), vmem_mb=48)
    return _run(_carry(call, comm, lambda: (pl.program_id(0) == 0, pl.program_id(0) == nb - 1)))


def _readout(o, g, yy, z, vec_ref):
    hg, ss, keep = [], [], []
    for h in range(NH):
        cs = slice(h * HF, (h + 1) * HF)
        oh = o[:, cs]
        r = lax.rsqrt(jnp.mean(oh * oh, axis=1, keepdims=True) + EPS)
        hg.append(oh * r * vec_ref[0:1, cs] * _silu(g[:, cs]))
        keep.append(r)
    u = yy * _silu(z)
    for gi in range(4):
        cs = slice(gi * 256, (gi + 1) * 256)
        ug = u[:, cs]
        r = lax.rsqrt(jnp.mean(ug * ug, axis=1, keepdims=True) + EPS)
        ss.append(ug * r * vec_ref[2:3, cs])
        keep.append(r)
    return jnp.concatenate(hg, axis=1), jnp.concatenate(ss, axis=1), keep, u


def _mix_out(o_f, o_b, p_main, y_f, y_b, xa, x, vecs, w_out):
    n = x.shape[0]

    def body(of_ref, ob_ref, g_ref, z_ref, yf_ref, yb_ref, xs_ref, x_ref, vec_ref, w_ref,
             ymix_ref, ylat_ref, h1_ref, u2_ref):
        o = of_ref[...].astype(F32) + ob_ref[...].astype(F32)
        yy = yf_ref[...].astype(F32) + yb_ref[...].astype(F32) + vec_ref[1:2, :] * xs_ref[...].astype(F32)
        hg, ss, _, _ = _readout(o, g_ref[...].astype(F32), yy, z_ref[...].astype(F32), vec_ref)
        ymix = jnp.concatenate([hg, ss], axis=1).astype(MXU_DTYPE)
        ymix_ref[...] = ymix
        ylat = _nn(ymix, w_ref[...])
        ylat_ref[...] = ylat
        h1 = x_ref[...] + vec_ref[3:4, :] * ylat
        h1_ref[...] = h1
        r = lax.rsqrt(jnp.mean(h1 * h1, axis=1, keepdims=True) + EPS)
        u2_ref[...] = ((h1 * r * vec_ref[6:7, :]) * vec_ref[4:5, :] + vec_ref[5:6, :]).astype(MXU_DTYPE)

    row = lambda j: (lambda i: (i, j))
    return _pcall(
        body, name="mix_out",
        out_shape=(jax.ShapeDtypeStruct((n, 2 * D), MXU_DTYPE), jax.ShapeDtypeStruct((n, D), F32),
                   jax.ShapeDtypeStruct((n, D), F32), jax.ShapeDtypeStruct((n, D), MXU_DTYPE)),
        grid=(n // TB,),
        in_specs=[pl.BlockSpec((TB, D), row(0)), pl.BlockSpec((TB, D), row(0)), pl.BlockSpec((TB, D), row(4)),
                  pl.BlockSpec((TB, D), row(5)), pl.BlockSpec((TB, D), row(0)), pl.BlockSpec((TB, D), row(0)),
                  pl.BlockSpec((TB, D), row(0)), pl.BlockSpec((TB, D), row(0)), _full((8, D)), _full((2 * D, D))],
        out_specs=(pl.BlockSpec((TB, 2 * D), row(0)), pl.BlockSpec((TB, D), row(0)), pl.BlockSpec((TB, D), row(0)),
                   pl.BlockSpec((TB, D), row(0))),
        sem=("parallel",), vmem_mb=48,
    )(o_f, o_b, p_main, p_main, y_f, y_b, xa, x, vecs, w_out)


def _mix_bwd(dylat, o_f, o_b, p_main, y_f, y_b, xa, vecs, w_out, comm=None):
    n = dylat.shape[0]
    t_total = p_main.shape[0]
    nlat = n // TB

    def body(*refs):
        dg_ref, dz_ref, acc_ref = refs[11], refs[13], refs[15]
        i = pl.program_id(0)

        @pl.when(i == 0)
        def _():
            acc_ref[...] = jnp.zeros_like(acc_ref)

        @pl.when(i < nlat)
        def _():
            compute(*refs)

        @pl.when(i == nlat)
        def _():
            dg_ref[...] = jnp.zeros_like(dg_ref)
            dz_ref[...] = jnp.zeros_like(dz_ref)

    def compute(dyl_ref, of_ref, ob_ref, g_ref, z_ref, yf_ref, yb_ref, xs_ref, vec_ref, w_ref,
                do_ref, dg_ref, dys_ref, dz_ref, dxs_ref, acc_ref):
        dymix = _nt(dyl_ref[...], w_ref[...])
        o = of_ref[...].astype(F32) + ob_ref[...].astype(F32)
        g = g_ref[...].astype(F32)
        z = z_ref[...].astype(F32)
        xs = xs_ref[...].astype(F32)
        yy = yf_ref[...].astype(F32) + yb_ref[...].astype(F32) + vec_ref[1:2, :] * xs
        _, _, keep, u = _readout(o, g, yy, z, vec_ref)
        do_l, dg_l = [], []
        for h in range(NH):
            cs = slice(h * HF, (h + 1) * HF)
            oh, gh, r, wv = o[:, cs], g[:, cs], keep[h], vec_ref[0:1, cs]
            dhg = dymix[:, cs]
            xh = oh * r
            dn = dhg * _silu(gh)
            dg_l.append(dhg * xh * wv * _dsilu(gh))
            acc_ref[0:1, cs] += _colsum(dn * xh)
            dxh = dn * wv
            do_l.append(r * (dxh - xh * jnp.mean(dxh * xh, axis=1, keepdims=True)))
        du_l = []
        for gi in range(4):
            cs = slice(gi * 256, (gi + 1) * 256)
            ug, r, wv = u[:, cs], keep[NH + gi], vec_ref[2:3, cs]
            dss = dymix[:, D + gi * 256:D + (gi + 1) * 256]
            xh = ug * r
            acc_ref[2:3, cs] += _colsum(dss * xh)
            dxh = dss * wv
            du_l.append(r * (dxh - xh * jnp.mean(dxh * xh, axis=1, keepdims=True)))
        du = jnp.concatenate(du_l, axis=1)
        dyy = du * _silu(z)
        do_ref[...] = jnp.concatenate(do_l, axis=1).astype(do_ref.dtype)
        dg_ref[...] = jnp.concatenate(dg_l, axis=1).astype(dg_ref.dtype)
        dys_ref[...] = dyy.astype(dys_ref.dtype)
        dz_ref[...] = (du * yy * _dsilu(z)).astype(dz_ref.dtype)
        dxs_ref[...] = (dyy * vec_ref[1:2, :]).astype(dxs_ref.dtype)
        acc_ref[1:2, :] += _colsum(dyy * xs)

    row = lambda j: (lambda i: (jnp.minimum(i, nlat - 1), j))
    lat = pl.BlockSpec((TB, D), row(0))
    tok = pl.BlockSpec((TB, D), lambda i: (i, 0))
    call = dict(
        body=body, args=[dylat, o_f, o_b, p_main, p_main, y_f, y_b, xa, vecs, w_out], name="mix_bwd",
        out_shape=(jax.ShapeDtypeStruct((n, D), MXU_DTYPE), jax.ShapeDtypeStruct((t_total, D), MXU_DTYPE),
                   jax.ShapeDtypeStruct((n, D), MXU_DTYPE), jax.ShapeDtypeStruct((t_total, D), MXU_DTYPE),
                   jax.ShapeDtypeStruct((n, D), MXU_DTYPE), jax.ShapeDtypeStruct((8, D), F32)),
        grid=(t_total // TB,),
        in_specs=[lat, lat, lat, pl.BlockSpec((TB, D), row(4)), pl.BlockSpec((TB, D), row(5)), lat, lat, lat,
                  _full((8, D)), _full((2 * D, D))],
        out_specs=(lat, tok, lat, tok, lat, _full((8, D))), scratch=[],
        sem=("arbitrary",), vmem_mb=48)
    return _run(_carry(call, comm, lambda: (pl.program_id(0) == 0, pl.program_id(0) == t_total // TB - 1)))


def _ffn_up(u2, w_gate, w_up):
    n = u2.shape[0]
    tb = 1024

    def body(u_ref, wg_ref, wu_ref, g_ref, up_ref, a_ref):
        uv = u_ref[...]
        gt = _nt(uv, wg_ref[...])
        upv = _nt(uv, wu_ref[...])
        g_ref[...] = gt.astype(g_ref.dtype)
        up_ref[...] = upv.astype(up_ref.dtype)
        a_ref[...] = (_silu(gt) * upv).astype(a_ref.dtype)

    blk = pl.BlockSpec((tb, FSL), lambda j, i: (i, j))
    wblk = pl.BlockSpec((None, FSL, D), lambda j, i: (j, 0, 0))
    return _pcall(
        body, name="ffn_up",
        out_shape=(jax.ShapeDtypeStruct((n, DFFP), MXU_DTYPE),) * 3,
        grid=(4, n // tb), in_specs=[pl.BlockSpec((tb, D), lambda j, i: (i, 0)), wblk, wblk],
        out_specs=(blk, blk, blk), sem=("parallel", "parallel"), vmem_mb=48,
    )(u2, w_gate, w_up)


def _ffn_down_loss(act, w_down, h1, tgt, vecs):
    n = act.shape[0]
    tb = 512

    def body(a_ref, w_ref, h1_ref, t_ref, vec_ref, dh2_ref, dffn_ref, acc_ref):
        i = pl.program_id(0)

        @pl.when(i == 0)
        def _():
            acc_ref[...] = jnp.zeros_like(acc_ref)

        g2 = vec_ref[0:1, :]
        fw = vec_ref[1:2, :]
        nsub = 4
        sb = tb // nsub
        wv = w_ref[...]
        ffns = [_nn(a_ref[r_ * sb:(r_ + 1) * sb, :], wv) for r_ in range(nsub)]
        for r_ in range(nsub):
            rows = slice(r_ * sb, (r_ + 1) * sb)
            ffn = ffns[r_]
            h2 = h1_ref[rows, :] + g2 * ffn
            r = lax.rsqrt(jnp.mean(h2 * h2, axis=1, keepdims=True) + EPS)
            xh = h2 * r
            err = xh * fw - t_ref[rows, :]
            dy = err * (1.0 / D)
            acc_ref[2:3, :] += _colsum(err * err) * (0.5 / D)
            acc_ref[1:2, :] += _colsum(dy * xh)
            dxh = dy * fw
            dh2 = r * (dxh - xh * jnp.mean(dxh * xh, axis=1, keepdims=True))
            dh2_ref[rows, :] = dh2
            dffn_ref[rows, :] = (g2 * dh2).astype(dffn_ref.dtype)
            acc_ref[0:1, :] += _colsum(dh2 * ffn)

    return _pcall(
        body, name="ffn_down_loss",
        out_shape=(jax.ShapeDtypeStruct((n, D), F32), jax.ShapeDtypeStruct((n, D), MXU_DTYPE),
                   jax.ShapeDtypeStruct((8, D), F32)),
        grid=(n // tb,),
        in_specs=[pl.BlockSpec((tb, DFFP), lambda i: (i, 0)), _full((DFFP, D)), pl.BlockSpec((tb, D), lambda i: (i, 0)),
                  pl.BlockSpec((tb, D), lambda i: (i, 0)), _full((8, D))],
        out_specs=(pl.BlockSpec((tb, D), lambda i: (i, 0)), pl.BlockSpec((tb, D), lambda i: (i, 0)), _full((8, D))),
        sem=("arbitrary",), vmem_mb=48,
    )(act, w_down, h1, tgt, vecs)


def _ffn_bwd(dffn, w_down, gate, up, w_gate_t, w_up_t):
    n = dffn.shape[0]
    tb = 1024

    def body(df_ref, wd_ref, g_ref, up_ref, wg_ref, wu_ref, dg_ref, dup_ref, du_ref):
        j = pl.program_id(1)
        nsub = 4
        sb = tb // nsub
        wd, wg, wu = wd_ref[...], wg_ref[...], wu_ref[...]
        dacts = [_nt(df_ref[r * sb:(r + 1) * sb, :], wd) for r in range(nsub)]
        parts = []
        for r in range(nsub):
            rows = slice(r * sb, (r + 1) * sb)
            gt = g_ref[rows, :].astype(F32)
            upv = up_ref[rows, :].astype(F32)
            sg = _sig(gt)
            dgt = (dacts[r] * upv * (sg * (1.0 + gt * (1.0 - sg)))).astype(MXU_DTYPE)
            dupv = (dacts[r] * (gt * sg)).astype(MXU_DTYPE)
            dg_ref[rows, :] = dgt
            dup_ref[rows, :] = dupv
            parts.append(_nn(dgt, wg) + _nn(dupv, wu))
        part = jnp.concatenate(parts, axis=0)

        @pl.when(j == 0)
        def _():
            du_ref[...] = part

        @pl.when(j > 0)
        def _():
            du_ref[...] += part

    tok = pl.BlockSpec((tb, D), lambda i, j: (i, 0))
    ffb = pl.BlockSpec((tb, FSL), lambda i, j: (i, j))
    wsl = pl.BlockSpec((None, FSL, D), lambda i, j: (j, 0, 0))
    return _pcall(
        body, name="ffn_bwd",
        out_shape=(jax.ShapeDtypeStruct((n, DFFP), MXU_DTYPE), jax.ShapeDtypeStruct((n, DFFP), MXU_DTYPE),
                   jax.ShapeDtypeStruct((n, D), F32)),
        grid=(n // tb, 4),
        in_specs=[tok, pl.BlockSpec((FSL, D), lambda i, j: (j, 0)), ffb, ffb, wsl, wsl],
        out_specs=(ffb, ffb, tok), sem=("parallel", "arbitrary"), vmem_mb=48,
    )(dffn, w_down, gate, up, w_gate_t, w_up_t)


def _ffn_norm_bwd(du, h1, ylat, dh2, vecs):
    n = du.shape[0]
    tb = 512

    def body(du_ref, h1_ref, yl_ref, dh2_ref, vec_ref, dh1_ref, dyl_ref, acc_ref):
        @pl.when(pl.program_id(0) == 0)
        def _():
            acc_ref[...] = jnp.zeros_like(acc_ref)

        duv = du_ref[...]
        h1 = h1_ref[...]
        r = lax.rsqrt(jnp.mean(h1 * h1, axis=1, keepdims=True) + EPS)
        xh = h1 * r
        nw = vec_ref[2:3, :]
        acc_ref[0:1, :] += _colsum(duv)
        acc_ref[1:2, :] += _colsum(duv * xh * nw)
        dn = duv * vec_ref[1:2, :]
        acc_ref[2:3, :] += _colsum(dn * xh)
        dxh = dn * nw
        dh1 = dh2_ref[...] + r * (dxh - xh * jnp.mean(dxh * xh, axis=1, keepdims=True))
        dh1_ref[...] = dh1
        dyl_ref[...] = (vec_ref[0:1, :] * dh1).astype(dyl_ref.dtype)
        acc_ref[3:4, :] += _colsum(dh1 * yl_ref[...])

    tok = pl.BlockSpec((tb, D), lambda i: (i, 0))
    return _pcall(
        body, name="ffn_norm_bwd",
        out_shape=(jax.ShapeDtypeStruct((n, D), F32), jax.ShapeDtypeStruct((n, D), MXU_DTYPE),
                   jax.ShapeDtypeStruct((8, D), F32)),
        grid=(n // tb,), in_specs=[tok, tok, tok, tok, _full((8, D))], out_specs=(tok, tok, _full((8, D))),
        sem=("arbitrary",), vmem_mb=40,
    )(du, h1, ylat, dh2, vecs)


def _deep_rows(rows):
    return max(r for r in range(128, 2305, 128) if rows % r == 0)


def _dw(a, b, name):
    tn_rows = a.shape[0]
    bt = _deep_rows(tn_rows)
    kk, nn_ = a.shape[1], b.shape[1]
    bk = 1024 if kk % 1024 == 0 else kk
    bn = 1024 if nn_ % 1024 == 0 else nn_
    nt = tn_rows // bt

    def body(a_ref, b_ref, o_ref, acc):
        t = pl.program_id(2)
        part = _tn(a_ref[...], b_ref[...])

        @pl.when(t == 0)
        def _():
            acc[...] = part

        @pl.when(t > 0)
        def _():
            acc[...] += part

        @pl.when(t == nt - 1)
        def _():
            o_ref[...] = acc[...].astype(o_ref.dtype)

    return _pcall(
        body, name=name, out_shape=jax.ShapeDtypeStruct((kk, nn_), MXU_DTYPE), grid=(kk // bk, nn_ // bn, nt),
        in_specs=[pl.BlockSpec((bt, bk), lambda i, j, t: (t, i)), pl.BlockSpec((bt, bn), lambda i, j, t: (t, j))],
        out_specs=pl.BlockSpec((bk, bn), lambda i, j, t: (i, j)), scratch=[pltpu.VMEM((bk, bn), F32)],
        sem=("parallel", "parallel", "arbitrary"), vmem_mb=40,
    )(a, b)


def _dw_in(segs, u_all, name):
    tiles = []
    for m, s_ in enumerate(segs):
        tiles += [(m, h) for h in range(s_.shape[1] // D)]
    ntile = len(tiles)
    t_total = u_all.shape[0]
    bt = _deep_rows(t_total)
    nt = t_total // bt

    def body(u_ref, *refs):
        seg_refs, o_ref, acc = refs[:len(segs)], refs[len(segs)], refs[len(segs) + 1]
        n, t = pl.program_id(0), pl.program_id(1)
        for k, (m, _) in enumerate(tiles):
            @pl.when(n == k)
            def _(m=m):
                part = _tn(seg_refs[m][...], u_ref[...])

                @pl.when(t == 0)
                def _():
                    acc[...] = part

                @pl.when(t > 0)
                def _():
                    acc[...] += part

        @pl.when(t == nt - 1)
        def _():
            o_ref[...] = acc[...].astype(o_ref.dtype)

    def seg_spec(m):
        ks = [k for k, (mm, _) in enumerate(tiles) if mm == m]
        lo, hi = ks[0], ks[-1]
        on = lambda n: (n >= lo) & (n <= hi)
        return pl.BlockSpec((bt, D), lambda n, t: (jnp.where(on(n), t, 0), jnp.where(on(n), n - lo, 0)))

    return _pcall(
        body, name=name, out_shape=jax.ShapeDtypeStruct((1, ntile * D, D), MXU_DTYPE), grid=(ntile, nt),
        in_specs=[pl.BlockSpec((bt, D), lambda n, t: (t, 0))] + [seg_spec(m) for m in range(len(segs))],
        out_specs=pl.BlockSpec((None, D, D), lambda n, t: (0, n, 0)),
        scratch=[pltpu.VMEM((D, D), F32)], sem=("parallel", "arbitrary"), vmem_mb=56,
    )(u_all, *segs)


def _du_prenorm_bwd(segs, ddt, wi_main, wi_tail, xin, mods, dres, row_off, tb, name, comm=None):
    n = xin.shape[0]
    nt = n // tb
    off = row_off // tb
    has_dx = dres is not None

    def body(*refs):
        seg_refs = refs[:7]
        ddt_ref, w_ref, wb_ref, wdt_ref, x_ref, mod_ref = refs[7:13]
        rest = refs[13:]
        if has_dx:
            dres_ref, dx_ref, acc_ref, du_scr = rest
        else:
            acc_ref, du_scr = rest
        j, i = pl.program_id(0), pl.program_id(1)
        rows = pl.ds(pl.multiple_of(i * tb, tb), tb)

        @pl.when((i == 0) & (j == 0))
        def _():
            acc_ref[...] = jnp.zeros_like(acc_ref)

        @pl.when(j == 0)
        def _():
            du_scr[rows, :] = _nn(ddt_ref[...], wdt_ref[...])

        for k in range(8):
            if not has_dx and k in (4, 5):
                continue

            @pl.when(j == k)
            def _(k=k):
                sv = seg_refs[min(k, 6)][...]
                part = _nn(sv, w_ref[...])
                if k in (2, 4, 6):
                    part = part + _nn(sv[:, 0:WTAIL], wb_ref[...])
                du_scr[rows, :] += part

        @pl.when(j == 7)
        def _():
            du = du_scr[rows, :]
            xv = x_ref[...]
            r = lax.rsqrt(jnp.mean(xv * xv, axis=1, keepdims=True) + EPS)
            xh = xv * r
            nw = mod_ref[1:2, :]
            acc_ref[0:1, :] += _colsum(du)
            acc_ref[1:2, :] += _colsum(du * xh * nw)
            dn = du * mod_ref[0:1, :]
            acc_ref[2:3, :] += _colsum(dn * xh)
            if has_dx:
                dxh = dn * nw
                dx_ref[...] = dres_ref[...] + r * (dxh - xh * jnp.mean(dxh * xh, axis=1, keepdims=True))

    def seg_spec(k):
        if k < 6:
            return pl.BlockSpec((tb, D), lambda j, i: (jnp.where(j == k, i + off, 0), 0))
        return pl.BlockSpec((tb, D), lambda j, i: (jnp.where(j >= 6, i + off, 0), jnp.where(j >= 6, j - 6, 0)))

    last = pl.BlockSpec((tb, D), lambda j, i: (jnp.where(j == 7, i, 0), 0))
    in_specs = [seg_spec(k) for k in range(7)]
    in_specs += [pl.BlockSpec((tb, 128), lambda j, i: (jnp.where(j == 0, i + off, 0), 0))] + _w_specs()
    in_specs += [last, _full((8, D))]
    args = list(segs) + [ddt, wi_main, wi_tail, wi_tail, xin, mods]
    out_shape = [jax.ShapeDtypeStruct((8, D), F32)]
    out_specs = [_full((8, D))]
    if has_dx:
        in_specs.append(last)
        args.append(dres)
        out_shape.insert(0, jax.ShapeDtypeStruct((n, D), F32))
        out_specs.insert(0, last)
    call = dict(body=body, args=args, name=name, out_shape=tuple(out_shape), grid=(8, nt), in_specs=in_specs,
                out_specs=tuple(out_specs), scratch=[pltpu.VMEM((n, D), F32)], sem=("arbitrary", "arbitrary"),
                vmem_mb=56)
    steps = lambda: ((pl.program_id(0) == 0) & (pl.program_id(1) == 0),
                     (pl.program_id(0) == 7) & (pl.program_id(1) == nt - 1))
    return _run(_carry(call, comm, steps))


def _sum8(v):
    def body(v_ref, o_ref):
        acc = v_ref[0]
        for k in range(1, 8):
            acc = acc + v_ref[k]
        o_ref[...] = acc

    return _pcall(body, name="small_sum", out_shape=jax.ShapeDtypeStruct(v.shape[1:], F32),
                  in_specs=[pl.BlockSpec(memory_space=pltpu.VMEM)], out_specs=pl.BlockSpec(memory_space=pltpu.VMEM))(v)


def _adamw(w, m, v, g, name):
    lead = w.ndim == 3
    rows, cols = w.shape[-2:]
    rb = 256 if rows % 256 == 0 else (352 if rows % 352 == 0 else rows)
    c1 = 1.0 - B1 ** STEP
    c2 = 1.0 - B2 ** STEP

    def body(w_ref, m_ref, v_ref, g_ref, d_ref, nm_ref, nv_ref):
        gv = g_ref[...]
        mn = B1 * m_ref[...] + (1.0 - B1) * gv
        vn = B2 * v_ref[...] + (1.0 - B2) * (gv * gv)
        nm_ref[...] = mn
        nv_ref[...] = vn
        d_ref[...] = -LR * ((mn / c1) / (jnp.sqrt(vn / c2) + AEPS) + WD * w_ref[...])

    if rb == rows and rows > 1024:
        cb, steps = 256, cols // 256
        gspec = pl.BlockSpec((rows, cb), lambda i: (0, i))
        spec = pl.BlockSpec((None, rows, cb), lambda i: (0, 0, i)) if lead else gspec
    else:
        steps = rows // rb
        gspec = pl.BlockSpec((rb, cols), lambda i: (i, 0))
        spec = pl.BlockSpec((None, rb, cols), lambda i: (0, i, 0)) if lead else gspec
    return _pcall(
        body, name=name, out_shape=(jax.ShapeDtypeStruct(w.shape, F32),) * 3, grid=(steps,),
        in_specs=[spec] * 3 + [gspec], out_specs=(spec,) * 3, sem=("parallel",), vmem_mb=40,
    )(w, m, v, g)


def _rows(v, n):
    f = v.reshape(-1)
    return jnp.pad(f, (0, n * D - f.shape[0])).reshape(n, D)


def kernel(x, c, ctx, c_ctx, w_ada, b_ada, norm_mix, w_in, conv_w, conv_b, ssd_a_log, ssd_dt_bias, ssd_d, ssd_norm, hgrn_lb_raw, hgrn_norm, w_out, norm_ffn, w_gate, w_up, w_down, final_norm, loss_target, m_c_ctx, m_w_ada, m_b_ada, m_norm_mix, m_w_in, m_conv_w, m_conv_b, m_ssd_a_log, m_ssd_dt_bias, m_ssd_d, m_ssd_norm, m_hgrn_lb_raw, m_hgrn_norm, m_w_out, m_norm_ffn, m_w_gate, m_w_up, m_w_down, m_final_norm, v_c_ctx, v_w_ada, v_b_ada, v_norm_mix, v_w_in, v_conv_w, v_conv_b, v_ssd_a_log, v_ssd_dt_bias, v_ssd_d, v_ssd_norm, v_hgrn_lb_raw, v_hgrn_norm, v_w_out, v_norm_ffn, v_w_gate, v_w_up, v_w_down, v_final_norm):
    ix, iy, ic = lax.axis_index("x"), lax.axis_index("y"), lax.axis_index("c")
    chip = 2 * ix + iy
    me = 2 * chip + ic
    xl, xc, tgt = x[0], ctx[0], loss_target[0]
    n_lat, n_ctx = xl.shape[0], xc.shape[0]
    assert n_ctx == TB and n_lat % 1024 == 0
    t_total = n_lat + n_ctx
    nb = t_total // TB

    tr = lambda a: jnp.swapaxes(a, -1, -2)
    shift = [functools.partial(jnp.pad, pad_width=((8 * k, WSL + WTAIL - NSH - 8 * k), (0, 0))) for k in range(4)]
    slab = lax.switch(chip, shift, tr(w_in[0]).astype(MXU_DTYPE))
    padrows = lambda a: jnp.pad(a, ((0, FSL - DFF // 4), (0, 0))).astype(MXU_DTYPE)
    shards = [slab[:WSL], slab[WSL:], w_out[0].astype(MXU_DTYPE), padrows(tr(w_gate[0])), padrows(tr(w_up[0])),
              padrows(w_down[0])]
    own = lambda g_, s_: lax.dynamic_update_slice(g_, s_[None], (chip, 0, 0))
    pack = jnp.concatenate([c, hgrn_lb_raw.reshape(1, D), _rows(conv_w[0], 3), jnp.zeros((3, D), F32)], axis=0)
    ncol_ada = w_ada.shape[2]
    b_shard = lax.dynamic_slice(b_ada, (0, chip * ncol_ada), (1, ncol_ada))
    gath, araw, mod_all, wi_main, wi_tail = _prologue(pack, c_ctx.reshape(1, D), w_ada[0], b_shard, shards[:2])
    wi_main, wi_tail = own(wi_main, shards[0]), own(wi_tail, shards[1])
    gath = gath.reshape(8, 8, D)
    lbraw_full = gath[0::2, 1].reshape(4, 2, 2, 256).transpose(1, 2, 0, 3).reshape(4, D)
    convw_full = gath[0::2, 2:5].reshape(4, 3 * D)[:, :KCONV * 512].reshape(4, KCONV, 512).transpose(1, 0, 2)
    convw_full = convw_full.reshape(KCONV, 2048)
    lbraw8 = jnp.pad(lbraw_full, ((0, 4), (0, 0)))
    convp = jnp.concatenate([convw_full, conv_b, jnp.zeros((2, 2048), F32)], axis=0)
    dtb = jnp.pad(ssd_dt_bias.reshape(1, 32), ((0, 7), (0, 96)))
    alog = jnp.pad(ssd_a_log.reshape(1, 32), ((0, 7), (0, 96)))
    mod_all = mod_all.reshape(8, 16, ncol_ada)[0::2]
    mod_full = mod_all.transpose(1, 0, 2).reshape(16, 4 * ncol_ada)
    my_mod = lax.dynamic_slice(mod_full, (me, 0), (1, 6 * D)).reshape(6, D)
    sh1, sc1, g1, sh2, sc2, g2 = (my_mod[k:k + 1] for k in range(6))
    csh1, csc1 = mod_full[8:9, 0:D], mod_full[8:9, D:2 * D]

    zrow = jnp.zeros((1, D), F32)
    mods_lat = jnp.concatenate([1.0 + sc1, sh1, norm_mix, zrow, zrow, zrow, zrow, zrow], axis=0)
    mods_ctx = jnp.concatenate([1.0 + csc1, csh1, norm_mix, zrow, zrow, zrow, zrow, zrow], axis=0)
    outs = _inproj(xl, mods_lat, wi_main, wi_tail, t_total, 1024, 0, None, "inproj_lat",
                   comm=_comm_gather(shards[2:5]))
    wo_g, wg_g, wu_g = (own(g_, s_) for g_, s_ in zip(outs[3:], shards[2:5]))
    w_out_f = wo_g.reshape(2 * D, D)
    p_main, p_dt, u_all = _inproj(xc, mods_ctx, wi_main, wi_tail, t_total, TB, nb - 1, outs[:3], "inproj_ctx")

    o_f, hs_f, wd_g = _hgrn_fwd(p_main, lbraw8, 0, nb, comm=_comm_gather(shards[5:]))
    w_down_f = own(wd_g, shards[5]).reshape(DFFP, D)
    o_b, hs_b = _hgrn_fwd(p_main, lbraw8, 1, nb)
    xa, dsl, dts = _ssd_prep(p_main, p_dt, convp, dtb, nb)
    y_f, ss_f = _ssd_fwd(xa, dts, alog, 0, nb)
    y_b, ss_b = _ssd_fwd(xa, dts, alog, 1, nb)

    vec_mix = jnp.concatenate([jnp.tile(hgrn_norm, (1, NH)), jnp.repeat(ssd_d, SP, axis=1), ssd_norm, g1, 1.0 + sc2,
                               sh2, norm_ffn, zrow], axis=0)
    ymix, ylat, h1, u2 = _mix_out(o_f, o_b, p_main, y_f, y_b, xa, xl, vec_mix, w_out_f)
    gate, up, act = _ffn_up(u2, wg_g, wu_g)
    vec_loss = jnp.concatenate([g2, final_norm.reshape(1, D)] + [zrow] * 6, axis=0)
    dh2, dffn, acc_loss = _ffn_down_loss(act, w_down_f, h1, tgt, vec_loss)

    core_arr = jnp.reshape(ic, (1,)).astype(jnp.int32)
    chip_arr = jnp.reshape(chip, (1,)).astype(jnp.int32)
    every = (0, 4)

    def pair_sum(gs, got, tag):
        return list(_pair_sum(gs, list(got), core_arr, "grads_pair_sum_" + tag))

    vec_ffn = jnp.concatenate([g1, 1.0 + sc2, norm_ffn] + [zrow] * 5, axis=0)
    dgate, dup, du2 = _ffn_bwd(dffn, w_down_f, gate, up, wg_g, wu_g)
    dh1, dylat, acc_ffn = _ffn_norm_bwd(du2, h1, ylat, dh2, vec_ffn)
    gw_down = _dw(act, dffn, "dw_down").reshape(4, FSL, D)
    ga1 = [_dw(dgate, u2, "dw_gate").reshape(4, FSL, D), _dw(dup, u2, "dw_up").reshape(4, FSL, D)]
    res = _mix_bwd(dylat, o_f, o_b, p_main, y_f, y_b, xa, vec_mix, w_out_f, comm=_comm_pair(ga1))
    (do, dgr, dys, dzr, dxs_skip, acc_mix), pair_a1 = res[:6], pair_sum(ga1, res[6:], "a1")
    ga2 = [gw_down, _dw(ymix, dylat, "dw_out").reshape(4, D // 2, D)]

    res = _hgrn_bwd(p_main, lbraw8, hs_f, do, 0, nb, None,
                    comm=[_comm_exchange(pair_a1, [every] * 2), _comm_pair(ga2)])
    (dq0, dff, dv0, dlb_f), recv_a, pair_a2 = res[:4], list(res[4:6]), pair_sum(ga2, res[6:], "a2")
    res = _hgrn_bwd(p_main, lbraw8, hs_b, do, 1, nb, (dq0, dv0), comm=_comm_exchange(pair_a2, [every] * 2))
    (dq, dfb, dv, dlb_b), recv_a = res[:4], recv_a + list(res[4:])
    pair_a, dests_a = pair_a1 + pair_a2, [every] * 4
    gw_in = [_dw_in([dq, dff], u_all, "dw_in_0"), _dw_in([dfb, dv], u_all, "dw_in_1"),
             _dw_in([dgr, dzr], u_all, "dw_in_2")]

    res = _ssd_bwd(xa, dts, alog, ss_f, dys, 0, nb, None, comm=_comm_pair(gw_in))
    (dxa0, ddts0, da_f), pair_b, dests_b = res[:3], pair_sum(gw_in, res[3:], "b"), [(0, 1), (1, 2), (2, 3)]
    res = _ssd_bwd(xa, dts, alog, ss_b, dys, 1, nb, (dxa0, ddts0), comm=_comm_exchange(pair_b, dests_b))
    (dxa, ddts, da_b), recv_b = res[:3], list(res[3:])
    dxbc, ddt, acc_conv, acc_dtb = _ssd_prep_bwd(p_main, p_dt, convp, dtb, dsl, dxa, dxs_skip, ddts, nb)
    gw_in.append(_dw_in([dxbc], u_all, "dw_in_3"))
    gw_in_dt = _dw(ddt, u_all, "dw_in_dt")
    gc = [gw_in[3], jnp.concatenate([g_[:, 0:WTAIL, :] for g_ in gw_in[1:]] + [gw_in_dt[None]], axis=0)]

    segs = [dq, dff, dfb, dv, dgr, dzr, dxbc]
    bmods_lat = jnp.concatenate([1.0 + sc1, norm_mix] + [zrow] * 6, axis=0)
    bmods_ctx = jnp.concatenate([1.0 + csc1, norm_mix] + [zrow] * 6, axis=0)
    res = _du_prenorm_bwd(segs, ddt, wi_main, wi_tail, xc, bmods_ctx, None, n_lat, TB, "du_ctx", comm=_comm_pair(gc))
    acc_ctx, pair_c, dests_c = res[0], pair_sum(gc, res[1:], "c"), [(3, 4), every]
    res = _du_prenorm_bwd(segs, ddt, wi_main, wi_tail, xl, bmods_lat, dh1, 0, 512, "du_lat",
                          comm=_comm_exchange(pair_c, dests_c))
    (grad_x, acc_lat), recv_c = res[:2], list(res[2:])

    mine = _chip_sum(pair_b + pair_c + pair_a, recv_b + recv_c + recv_a, chip_arr, dests_b + dests_c + dests_a,
                     [0, 0, 0, 0, 1, 3, 4, 5, 2])
    dmod_lat = jnp.concatenate([acc_lat[0:2], acc_ffn[3:4], acc_ffn[0:2], acc_loss[0:1]], axis=0)
    misc = jnp.concatenate([(da_f + da_b)[0, :32], jnp.zeros((96,), F32), acc_dtb[0, :32], jnp.zeros((96,), F32),
                            jnp.sum(acc_loss[2]).reshape(1), jnp.zeros((D - 257,), F32)]).reshape(1, D)
    sv = jnp.concatenate([
        dmod_lat, acc_ctx[0:2], (acc_lat[2:3] + acc_ctx[2:3]), acc_ffn[2:3], acc_loss[1:2], acc_mix[2:3],
        acc_mix[0:1], acc_mix[1:2], dlb_f[0:1], dlb_b[0:1], acc_conv[0:6].reshape(12, D), misc,
        jnp.zeros((3, D), F32)], axis=0)
    res = _pair_swap(mine, sv)
    theirs, sv_all = res[:-1], res[-1].reshape(8, 32, D)
    whole = [jnp.concatenate([jnp.where(ic == 0, m_, t_), jnp.where(ic == 0, t_, m_)], axis=0)
             for m_, t_ in zip(mine, theirs)]
    g_w_in = lax.dynamic_slice(jnp.concatenate(whole[0:2], axis=0), (8 * chip, 0), (NSH, D))
    g_w_out = whole[2]
    g_w_gate = whole[3][:DFF // 4]
    g_w_up = whole[4][:DFF // 4]
    g_w_down = whole[5][:DFF // 4]
    ssum = _sum8(sv_all)
    dmod_rows = sv_all[:, 0:6].reshape(8, 6 * D)
    dmod_ctx_row = jnp.concatenate([ssum[6:8].reshape(1, 2 * D), jnp.zeros((1, 4 * D), F32)], axis=1)
    dmod_full = jnp.concatenate([dmod_rows, dmod_ctx_row, jnp.zeros((7, 6 * D), F32)], axis=0)
    grad_b_ada = jnp.sum(dmod_full, axis=0, keepdims=True)
    dmod_shard = lax.dynamic_slice(dmod_full, (0, chip * ncol_ada), (16, ncol_ada))
    g_w_ada, da_part = _ada_bwd(araw, dmod_shard, w_ada[0])
    da_all = _allgather8(da_part, "ada_ctx_gather").reshape(8, 16, D)[0::2, 8]
    cc = c_ctx.reshape(1, D)
    grad_c_ctx = (jnp.sum(da_all, axis=0, keepdims=True) * _dsilu(cc)).reshape(D)

    grad_norm_mix, grad_norm_ffn, grad_final_norm = ssum[8:9], ssum[9:10], ssum[10].reshape(D)
    grad_ssd_norm = ssum[11:12]
    grad_hgrn_norm = jnp.sum(ssum[12].reshape(NH, HF), axis=0, keepdims=True)
    grad_ssd_d = jnp.sum(ssum[13].reshape(SHEADS, SP), axis=1).reshape(1, SHEADS)
    lb_full = _sig(lbraw_full[0:2] - lbraw_full[2:4])
    dr0 = ssum[14:16] * lb_full * (1.0 - lb_full)
    grad_lb_full = jnp.stack([dr0, -dr0], axis=0)
    grad_lb = lax.dynamic_slice(grad_lb_full, (0, 0, chip * 256), (2, 2, 256))
    grad_conv_w = lax.dynamic_slice(ssum[16:26].reshape(KCONV, 2048), (0, chip * 512), (KCONV, 512)).reshape(1, KCONV, 512)
    grad_conv_b = ssum[26:28].reshape(1, 2048)
    a_val = -jnp.exp(ssd_a_log)
    grad_a_log = ssum[28, 0:32].reshape(1, 2, SHEADS) * a_val
    grad_dt_bias = ssum[28, 128:160].reshape(1, 2, SHEADS)
    loss = ssum[28, 256]

    small_w = [c_ctx, b_ada, norm_mix, conv_w, conv_b, ssd_a_log, ssd_dt_bias, ssd_d, ssd_norm, hgrn_lb_raw,
               hgrn_norm, norm_ffn, final_norm]
    small_m = [m_c_ctx, m_b_ada, m_norm_mix, m_conv_w, m_conv_b, m_ssd_a_log, m_ssd_dt_bias, m_ssd_d, m_ssd_norm,
               m_hgrn_lb_raw, m_hgrn_norm, m_norm_ffn, m_final_norm]
    small_v = [v_c_ctx, v_b_ada, v_norm_mix, v_conv_w, v_conv_b, v_ssd_a_log, v_ssd_dt_bias, v_ssd_d, v_ssd_norm,
               v_hgrn_lb_raw, v_hgrn_norm, v_norm_ffn, v_final_norm]
    small_g = [grad_c_ctx, grad_b_ada, grad_norm_mix, grad_conv_w, grad_conv_b, grad_a_log, grad_dt_bias, grad_ssd_d,
               grad_ssd_norm, grad_lb, grad_hgrn_norm, grad_norm_ffn, grad_final_norm]
    nrows = [-(-a.size // D) for a in small_w]
    packs = lambda lst: jnp.concatenate([_rows(a, r) for a, r in zip(lst, nrows)]
                                        + [jnp.zeros((24 - sum(nrows), D), F32)], axis=0)
    sd, sm, svv = _adamw(packs(small_w), packs(small_m), packs(small_v), packs(small_g), "adamw_small")

    def unpack(p):
        out, r0 = [], 0
        for a, r in zip(small_w, nrows):
            out.append(p[r0:r0 + r].reshape(-1)[:a.size].reshape(a.shape))
            r0 += r
        return out

    sd, sm, svv = unpack(sd), unpack(sm), unpack(svv)
    big = {}
    for nm, w_, m_, v_, g_ in (("w_ada", w_ada, m_w_ada, v_w_ada, g_w_ada), ("w_in", w_in, m_w_in, v_w_in, g_w_in),
                               ("w_out", w_out, m_w_out, v_w_out, g_w_out),
                               ("w_gate", w_gate, m_w_gate, v_w_gate, g_w_gate),
                               ("w_up", w_up, m_w_up, v_w_up, g_w_up),
                               ("w_down", w_down, m_w_down, v_w_down, g_w_down)):
        if nm in ("w_in", "w_gate", "w_up"):
            big[nm] = tuple(tr(t) for t in (g_[None],) + tuple(_adamw(tr(w_), tr(m_), tr(v_), g_, "adamw_" + nm)))
        else:
            big[nm] = (g_[None],) + tuple(_adamw(w_, m_, v_, g_, "adamw_" + nm))

    order = ["c_ctx", "w_ada", "b_ada", "norm_mix", "w_in", "conv_w", "conv_b", "ssd_a_log", "ssd_dt_bias", "ssd_d",
             "ssd_norm", "hgrn_lb_raw", "hgrn_norm", "w_out", "norm_ffn", "w_gate", "w_up", "w_down", "final_norm"]
    small_names = ["c_ctx", "b_ada", "norm_mix", "conv_w", "conv_b", "ssd_a_log", "ssd_dt_bias", "ssd_d", "ssd_norm",
                   "hgrn_lb_raw", "hgrn_norm", "norm_ffn", "final_norm"]
    table = dict(big)
    for k, nm in enumerate(small_names):
        table[nm] = (small_g[k].reshape(small_w[k].shape), sd[k], sm[k], svv[k])
    grads = [table[nm][0] for nm in order]
    deltas = [table[nm][1] for nm in order]
    new_m = [table[nm][2] for nm in order]
    new_v = [table[nm][3] for nm in order]
    return (loss, grad_x[None], *grads, *deltas, *new_m, *new_v)
```

```python
import functools
import math

import jax
import jax.numpy as jnp
from jax import lax
from jax.experimental import pallas as pl
from jax.experimental.pallas import tpu as pltpu

F32 = jnp.float32
BF16 = jnp.bfloat16
MXU_DTYPE = jnp.bfloat16
_INTERPRET = False

D = 1024
NH, HF = 8, 128
HC = 64
SC = 128
SN = 128
SHEADS, SP = 16, 64
GRID_W = 64
KCONV = 5
DFF = 2816
FSL = 768
DFFP = 4 * FSL
NIN = 8224
TB = 256
EPS = 1e-6
LR, B1, B2, AEPS, WD, STEP = 0.001, 0.9, 0.999, 1e-08, 0.01, 10
MESH_ID = pl.DeviceIdType.MESH
NSH = NIN // 4
WSL = 2048
WTAIL = 128


def _pcall(body, *, name, out_shape, grid=(), in_specs=None, out_specs=None, scratch=(), sem=None,
           vmem_mb=None, aliases=None):
    params = {}
    if sem is not None:
        params["dimension_semantics"] = sem
    if vmem_mb is not None:
        params["vmem_limit_bytes"] = vmem_mb << 20
    kw = dict(name=name, out_shape=out_shape, scratch_shapes=list(scratch),
              input_output_aliases=aliases or {}, compiler_params=pltpu.CompilerParams(**params),
              interpret=_INTERPRET)
    if grid:
        kw["grid"] = grid
    if in_specs is not None:
        kw["in_specs"] = in_specs
    if out_specs is not None:
        kw["out_specs"] = out_specs
    return pl.pallas_call(body, **kw)


def _mx(a):
    return a.astype(MXU_DTYPE)


def _dg(a, b, ca, cb):
    return lax.dot_general(_mx(a), _mx(b), (((ca,), (cb,)), ((), ())), preferred_element_type=F32)


def _nn(a, b):
    return _dg(a, b, 1, 0)


def _nt(a, b):
    return _dg(a, b, 1, 1)


def _tn(a, b):
    return _dg(a, b, 0, 0)


def _dot01(m, x, ways=3):
    f = lambda t: lax.dot_general(m, t, (((1,), (0,)), ((), ())), preferred_element_type=F32)
    hi = x.astype(BF16)
    r1 = x - hi.astype(F32)
    mid = r1.astype(BF16)
    if ways == 2:
        return f(hi) + f(mid)
    lo = (r1 - mid.astype(F32)).astype(BF16)
    return f(hi) + f(mid) + f(lo)


def _tri(n, upper):
    r = lax.broadcasted_iota(jnp.int32, (n, n), 0)
    c = lax.broadcasted_iota(jnp.int32, (n, n), 1)
    return (c >= r) if upper else (c <= r)


def _b01(mask):
    return jnp.where(mask, 1.0, 0.0).astype(BF16)


def _sig(x):
    return jax.nn.sigmoid(x)


def _silu(x):
    return x * _sig(x)


def _dsilu(x):
    s = _sig(x)
    return s * (1.0 + x * (1.0 - s))


def _softplus(x):
    return jnp.maximum(x, 0.0) + jnp.log(1.0 + jnp.exp(-jnp.abs(x)))


def _rowsum(x):
    return jnp.sum(x, axis=1, keepdims=True)


def _colsum(x):
    return jnp.sum(x, axis=0, keepdims=True)


def _full(shape):
    return pl.BlockSpec(shape, lambda *_: (0,) * len(shape))


def _allgather8_phases(x_ref, out_ref, send_sems, recv_sems, local_sem):
    m_per = x_ref.shape[0]
    x, y, c = lax.axis_index("x"), lax.axis_index("y"), lax.axis_index("c")
    me, sibling = (x, y, c), (x, y, 1 - c)
    chips = [(1 - x, y), (x, 1 - y), (1 - x, 1 - y)]

    def rows(px, py, pc):
        return out_ref.at[pl.ds((4 * px + 2 * py + pc) * m_per, m_per), :]

    def copy(k, block, to, src=None):
        return pltpu.make_async_remote_copy(
            src_ref=rows(*block) if src is None else src, dst_ref=rows(*block),
            send_sem=send_sems.at[k], recv_sem=recv_sems.at[k], device_id=to, device_id_type=MESH_ID)

    mine = pltpu.make_async_copy(x_ref, rows(*me), local_sem)
    first = [copy(0, me, sibling, src=x_ref)]
    first += [copy(1 + j, me, (*chip, c), src=x_ref) for j, chip in enumerate(chips)]
    passed = [copy(4 + j, (*chip, c), sibling) for j, chip in enumerate(chips)]

    def start():
        mine.start()
        for cp in first:
            cp.start()

    def forward():
        for j, chip in enumerate(chips):
            copy(1 + j, (*chip, c), me).wait_recv()
            passed[j].start()

    def finish():
        copy(0, sibling, me).wait_recv()
        for j, chip in enumerate(chips):
            copy(4 + j, (*chip, 1 - c), me).wait_recv()
        for cp in first + passed:
            cp.wait_send()
        mine.wait()

    return start, forward, finish


def _allgather8_ops(x_ref, out_ref, send_sems, recv_sems, local_sem):
    for phase in _allgather8_phases(x_ref, out_ref, send_sems, recv_sems, local_sem):
        phase()


def _comm_allgather8(v):
    return ([v], (jax.ShapeDtypeStruct((8 * v.shape[0], v.shape[1]), v.dtype),),
            [pltpu.SemaphoreType.DMA((8,)), pltpu.SemaphoreType.DMA((8,))],
            lambda i, o, s, r: _allgather8_phases(i[0], o[0], s, r, s.at[7]))


_AG8_SEMS = [pltpu.SemaphoreType.DMA((7,)), pltpu.SemaphoreType.DMA((7,)), pltpu.SemaphoreType.DMA]


def _prologue(pack, cc_row, w_ada, b_shard, shards):
    n = len(shards)
    ncol = w_ada.shape[1]

    def body(pack_ref, cc_ref, w_ref, b_ref, *refs):
        ins = refs[:n]
        gath_ref, araw_ref, mod_ref = refs[n:n + 3]
        outs = refs[n + 3:2 * n + 3]
        modsh, s1, r1, l1, s2, r2, l2, gs, gr = refs[2 * n + 3:]
        start, forward, finish = _gather_ops(ins, outs, gs, gr, relay=True)
        start()
        _allgather8_ops(pack_ref, gath_ref, s1, r1, l1)
        a = jnp.concatenate([gath_ref[8 * i:8 * i + 1, :] for i in range(8)] + [cc_ref[...], jnp.zeros((7, D), F32)],
                            axis=0)
        araw_ref[...] = a
        modsh[...] = _nn(_silu(a), w_ref[...]) + b_ref[...]
        _allgather8_ops(modsh, mod_ref, s2, r2, l2)
        forward()
        finish()

    vm = pl.BlockSpec(memory_space=pltpu.VMEM)
    anyspec = pl.BlockSpec(memory_space=pl.ANY)
    return _pcall(
        body, name="prologue",
        out_shape=(jax.ShapeDtypeStruct((64, D), F32), jax.ShapeDtypeStruct((16, D), F32),
                   jax.ShapeDtypeStruct((128, ncol), F32)) + _gather_out(shards),
        in_specs=[vm, vm, vm, vm] + [anyspec] * n, out_specs=(vm, vm, vm) + (anyspec,) * n,
        scratch=[pltpu.VMEM((16, ncol), F32)] + list(_AG8_SEMS) + list(_AG8_SEMS) + _gather_sems(n), vmem_mb=40,
    )(pack, cc_row, w_ada, b_shard, *shards)


def _gather_ops(ins, outs, send_sems, recv_sems, relay=False):
    n = len(ins)
    x, y, c = lax.axis_index("x"), lax.axis_index("y"), lax.axis_index("c")
    me, sibling = (x, y, c), (x, y, 1 - c)
    chips = [(1 - x, y), (x, 1 - y), (1 - x, 1 - y)]
    direct = 2 if relay else 3

    def part(a, px, py, pc, quarter=None):
        half = ins[a].shape[0] // 2
        if quarter is None:
            return outs[a].at[2 * px + py, pl.ds(pc * half, half), :]
        return outs[a].at[2 * px + py, pl.ds(pc * half + quarter * (half // 2), half // 2), :]

    def copy(a, k, block, to, src=None, quarter=None):
        return pltpu.make_async_remote_copy(
            src_ref=part(a, *block, quarter) if src is None else src, dst_ref=part(a, *block, quarter),
            send_sem=send_sems.at[8 * a + k], recv_sem=recv_sems.at[8 * a + k], device_id=to,
            device_id_type=MESH_ID)

    def first(a, j):
        half = ins[a].shape[0] // 2
        return copy(a, j, me, (*chips[j], c), src=ins[a].at[pl.ds(c * half, half), :])

    relayed = lambda a, q: copy(a, 6 + q, (*chips[q], c), (*chips[1 - q], c), quarter=q)

    def start():
        for a in range(n):
            for j in range(direct):
                first(a, j).start()

    def forward():
        for a in range(n):
            for j in range(direct):
                copy(a, j, (*chips[j], c), me).wait_recv()
                copy(a, 3 + j, (*chips[j], c), sibling).start()
                if relay:
                    relayed(a, j).start()
            if relay:
                for q in range(2):
                    copy(a, 6 + q, (*chips[2], c), me, quarter=q).wait_recv()
                copy(a, 5, (*chips[2], c), sibling).start()

    def finish():
        for a in range(n):
            for j, chip in enumerate(chips):
                copy(a, 3 + j, (*chip, 1 - c), me).wait_recv()
        for a in range(n):
            for j, chip in enumerate(chips):
                if j < direct:
                    first(a, j).wait_send()
                    if relay:
                        relayed(a, j).wait_send()
                copy(a, 3 + j, (*chip, c), sibling).wait_send()

    return start, forward, finish


def _gather_out(shards):
    return tuple(jax.ShapeDtypeStruct((4,) + s_.shape, s_.dtype) for s_ in shards)


def _gather_sems(n):
    return [pltpu.SemaphoreType.DMA((8 * n,)), pltpu.SemaphoreType.DMA((8 * n,))]


def _pair_ops(ins, outs, send_sems, recv_sems):
    x, y, c = lax.axis_index("x"), lax.axis_index("y"), lax.axis_index("c")
    cps = []
    for a in range(len(ins)):
        half = ins[a].shape[1] // 2
        cps.append(pltpu.make_async_remote_copy(
            src_ref=ins[a].at[:, pl.ds((1 - c) * half, half), :], dst_ref=outs[a], send_sem=send_sems.at[a],
            recv_sem=recv_sems.at[a], device_id=(x, y, 1 - c), device_id_type=MESH_ID))

    def start():
        for cp in cps:
            cp.start()

    def finish():
        for cp in cps:
            cp.wait()

    return start, finish


def _comm_pair(gs):
    n = len(gs)
    return (list(gs), tuple(jax.ShapeDtypeStruct((g.shape[0], g.shape[1] // 2, g.shape[2]), g.dtype) for g in gs),
            [pltpu.SemaphoreType.DMA((n,)), pltpu.SemaphoreType.DMA((n,))], _pair_ops)


def _exchange_ops(ins, outs, send_sems, recv_sems, dests):
    x, y, c = lax.axis_index("x"), lax.axis_index("y"), lax.axis_index("c")
    mine = 2 * x + y
    chips = [(1 - x, y), (x, 1 - y), (1 - x, 1 - y)]

    def each(fn):
        for a in range(len(ins)):
            lo, hi = dests[a]
            for j, (px, py) in enumerate(chips):
                q = 2 * px + py
                cp = pltpu.make_async_remote_copy(
                    src_ref=ins[a].at[jnp.clip(q - lo, 0, hi - lo - 1)], dst_ref=outs[a].at[j],
                    send_sem=send_sems.at[3 * a + j], recv_sem=recv_sems.at[3 * a + j], device_id=(px, py, c),
                    device_id_type=MESH_ID)
                fn(cp, (q >= lo) & (q < hi), (mine >= lo) & (mine < hi), (lo, hi) == (0, 4))

    def start():
        def go(cp, send_ok, recv_ok, always):
            if always:
                cp.start()
            else:
                pl.when(send_ok)(cp.start)
        each(go)

    def finish():
        def go(cp, send_ok, recv_ok, always):
            if always:
                cp.wait()
            else:
                pl.when(send_ok)(cp.wait_send)
                pl.when(recv_ok)(cp.wait_recv)
        each(go)

    return start, finish


def _comm_exchange(hs, dests):
    n = len(hs)
    return (list(hs), tuple(jax.ShapeDtypeStruct((3,) + h.shape[1:], h.dtype) for h in hs),
            [pltpu.SemaphoreType.DMA((3 * n,)), pltpu.SemaphoreType.DMA((3 * n,))],
            lambda i, o, s, r: _exchange_ops(i, o, s, r, dests))


def _comm_gather(shards, relay=False):
    return (list(shards), _gather_out(shards), _gather_sems(len(shards)),
            lambda i, o, s, r: _gather_ops(i, o, s, r, relay))


def _carry(call, comm, steps):
    if comm is None:
        return call
    if isinstance(comm, list):
        for one in comm:
            call = _carry(call, one, steps)
        return call
    arrays, out_shape, sems, make = comm
    n, n_in, n_out = len(arrays), len(call["args"]), len(call["out_shape"])
    body = call["body"]

    def wrapped(*refs):
        base_in, cin = refs[:n_in], refs[n_in:n_in + n]
        rest = refs[n_in + n:]
        base_out, cout, scr = rest[:n_out], rest[n_out:n_out + n], rest[n_out + n:]
        ops = make(cin, cout, scr[-2], scr[-1])
        when = steps()
        pl.when(when[0])(ops[0])
        if len(ops) == 3 and len(when) == 3:
            pl.when(when[2])(ops[1])
        body(*base_in, *base_out, *scr[:-2])
        if len(ops) == 3 and len(when) == 2:
            pl.when(when[1])(ops[1])
        pl.when(when[1])(ops[-1])

    anyspec = pl.BlockSpec(memory_space=pl.ANY)
    return dict(call, body=wrapped, args=list(call["args"]) + arrays,
                in_specs=list(call["in_specs"]) + [anyspec] * n,
                out_shape=tuple(call["out_shape"]) + tuple(out_shape),
                out_specs=tuple(call["out_specs"]) + (anyspec,) * n,
                scratch=list(call["scratch"]) + sems)


def _run(call):
    args = call.pop("args")
    body = call.pop("body")
    return _pcall(body, **call)(*args)


def _pair_swap(rs, sv):
    n = len(rs)

    def body(sv_ref, *refs):
        ins, outs, got_ref = refs[:n], refs[n:2 * n], refs[2 * n]
        send_sems, recv_sems, s1, r1, l1 = refs[2 * n + 1:]
        x, y, c = lax.axis_index("x"), lax.axis_index("y"), lax.axis_index("c")
        cps = [pltpu.make_async_remote_copy(
            src_ref=ins[a], dst_ref=outs[a], send_sem=send_sems.at[a], recv_sem=recv_sems.at[a],
            device_id=(x, y, 1 - c), device_id_type=MESH_ID) for a in range(n)]
        for cp in cps:
            cp.start()
        _allgather8_ops(sv_ref, got_ref, s1, r1, l1)
        for cp in cps:
            cp.wait()

    vm, anyspec = pl.BlockSpec(memory_space=pltpu.VMEM), pl.BlockSpec(memory_space=pl.ANY)
    return _pcall(
        body, name="grads_pair_swap",
        out_shape=tuple(jax.ShapeDtypeStruct(r.shape, r.dtype) for r in rs)
        + (jax.ShapeDtypeStruct((8 * sv.shape[0], sv.shape[1]), sv.dtype),),
        in_specs=[vm] + [anyspec] * n, out_specs=(anyspec,) * n + (vm,),
        scratch=[pltpu.SemaphoreType.DMA((n,)), pltpu.SemaphoreType.DMA((n,))] + list(_AG8_SEMS),
    )(sv, *rs)


SUM_STEPS = 4


def _pair_sum(gs, recvs, core, name):
    n = len(gs)

    def body(c_ref, *refs):
        for a in range(n):
            refs[2 * n + a][...] = (refs[a][...].astype(F32) + refs[n + a][...].astype(F32)).astype(refs[2 * n + a].dtype)

    blk = lambda g: (g.shape[0], g.shape[1] // (2 * SUM_STEPS), g.shape[2])
    return pl.pallas_call(
        body, name=name,
        out_shape=tuple(jax.ShapeDtypeStruct((g.shape[0], g.shape[1] // 2, g.shape[2]), g.dtype) for g in gs),
        grid_spec=pltpu.PrefetchScalarGridSpec(
            num_scalar_prefetch=1, grid=(SUM_STEPS,),
            in_specs=[pl.BlockSpec(blk(g), lambda i, cr: (0, cr[0] * SUM_STEPS + i, 0)) for g in gs]
            + [pl.BlockSpec(blk(g), lambda i, cr: (0, i, 0)) for g in gs],
            out_specs=tuple(pl.BlockSpec(blk(g), lambda i, cr: (0, i, 0)) for g in gs)),
        compiler_params=pltpu.CompilerParams(vmem_limit_bytes=40 << 20), interpret=_INTERPRET,
    )(core, *gs, *recvs)


def _chip_sum(hs, recvs, chip, dests, slots):
    n = len(hs)
    nout = max(slots) + 1
    first = [slots.index(o) for o in range(nout)]
    every = lambda d_: d_ == (0, 4)

    def own(d_):
        if every(d_):
            return lambda i, kr: (kr[0], i, 0)
        return lambda i, kr: (0, jnp.where(kr[0] == d_[0], i, 0), 0)

    def got(d_):
        if every(d_):
            return lambda i, kr: (0, i, 0)
        return lambda i, kr: (0, jnp.where(kr[0] == d_[0], i, 0), 0)

    def body(k_ref, *refs):
        for a in range(n):
            def emit(a=a):
                acc = refs[a][0].astype(F32)
                for j in range(3):
                    acc = acc + refs[n + a][j].astype(F32)
                refs[2 * n + slots[a]][...] = acc
            if every(dests[a]):
                emit()
            else:
                pl.when(k_ref[0] == dests[a][0])(emit)

    rb = lambda h: h.shape[1] // SUM_STEPS
    return pl.pallas_call(
        body, name="grads_chip_sum",
        out_shape=tuple(jax.ShapeDtypeStruct(hs[a].shape[1:], F32) for a in first),
        grid_spec=pltpu.PrefetchScalarGridSpec(
            num_scalar_prefetch=1, grid=(SUM_STEPS,),
            in_specs=[pl.BlockSpec((1, rb(h), h.shape[2]), own(d_)) for h, d_ in zip(hs, dests)]
            + [pl.BlockSpec((3, rb(h), h.shape[2]), got(d_)) for h, d_ in zip(hs, dests)],
            out_specs=tuple(pl.BlockSpec((rb(hs[a]), hs[a].shape[2]), lambda i, kr: (i, 0)) for a in first)),
        compiler_params=pltpu.CompilerParams(vmem_limit_bytes=40 << 20), interpret=_INTERPRET,
    )(chip, *hs, *recvs)


def _ada_bwd(araw, dmod, w):
    nblk = w.shape[1] // 512

    def body(a_ref, d_ref, w_ref, gw_ref, da_ref):
        j = pl.program_id(0)
        gw_ref[...] = _tn(_silu(a_ref[...]), d_ref[...])
        part = _nt(d_ref[...], w_ref[...])

        @pl.when(j == 0)
        def _():
            da_ref[...] = part

        @pl.when(j > 0)
        def _():
            da_ref[...] += part

    return _pcall(
        body, name="ada_bwd",
        out_shape=(jax.ShapeDtypeStruct(w.shape, F32), jax.ShapeDtypeStruct((16, D), F32)), grid=(nblk,),
        in_specs=[_full((16, D)), pl.BlockSpec((16, 512), lambda j: (0, j)), pl.BlockSpec((D, 512), lambda j: (0, j))],
        out_specs=(pl.BlockSpec((D, 512), lambda j: (0, j)), _full((16, D))), sem=("arbitrary",),
    )(araw, dmod, w)


def _w_specs():
    return [pl.BlockSpec((None, D, D), lambda j, i: (j // 2, j % 2, 0)),
            pl.BlockSpec((None, WTAIL, D), lambda j, i: (jnp.maximum(j // 2 - 1, 0), 0, 0)),
            pl.BlockSpec((None, WTAIL, D), lambda j, i: (3, 0, 0))]


def _inproj(xin, mods, wi_main, wi_tail, t_total, tb, blk_off, prev, name, comm=None):
    n = xin.shape[0]
    nt = n // tb
    ncol = 8

    def body(x_ref, mod_ref, w_ref, wb_ref, wdt_ref, *rest):
        p_ref, pdt_ref, u_ref, uscr = rest[-4:]
        j, i = pl.program_id(0), pl.program_id(1)
        rows = pl.ds(pl.multiple_of(i * tb, tb), tb)

        @pl.when(j == 0)
        def _():
            xv = x_ref[...]
            r = lax.rsqrt(jnp.mean(xv * xv, axis=1, keepdims=True) + EPS)
            u = (xv * r * mod_ref[2:3, :]) * mod_ref[0:1, :] + mod_ref[1:2, :]
            ub = u.astype(MXU_DTYPE)
            uscr[rows, :] = ub
            u_ref[...] = ub
            pdt_ref[...] = _nt(ub, wdt_ref[...])

        ub = uscr[rows, :]
        pv = _nt(ub, w_ref[...])

        @pl.when((j % 2 == 1) | (j == 0))
        def _():
            p_ref[...] = pv.astype(p_ref.dtype)

        @pl.when((j % 2 == 0) & (j > 0))
        def _():
            head = pv[:, 0:WTAIL] + _nt(ub, wb_ref[...])
            p_ref[...] = jnp.concatenate([head, pv[:, WTAIL:]], axis=1).astype(p_ref.dtype)

    once = lambda j, i: (jnp.where(j == 0, i, nt - 1) + blk_off, 0)
    in_specs = [pl.BlockSpec((tb, D), lambda j, i: (jnp.where(j == 0, i, nt - 1), 0)), _full((8, D))] + _w_specs()
    args = [xin, mods, wi_main, wi_tail, wi_tail]
    aliases = None
    if prev is not None:
        in_specs += [pl.BlockSpec(memory_space=pl.ANY)] * 3
        args += list(prev)
        aliases = {5: 0, 6: 1, 7: 2}
    call = dict(
        body=body, args=args, name=name,
        out_shape=(jax.ShapeDtypeStruct((t_total, ncol * D), MXU_DTYPE), jax.ShapeDtypeStruct((t_total, 128), F32),
                   jax.ShapeDtypeStruct((t_total, D), MXU_DTYPE)),
        grid=(ncol, nt), in_specs=in_specs,
        out_specs=(pl.BlockSpec((tb, D), lambda j, i: (i + blk_off, j)), pl.BlockSpec((tb, 128), once),
                   pl.BlockSpec((tb, D), once)),
        scratch=[pltpu.VMEM((n, D), MXU_DTYPE)], sem=("arbitrary", "arbitrary"), vmem_mb=48, aliases=aliases)
    steps = lambda: ((pl.program_id(0) == 0) & (pl.program_id(1) == 0),
                     (pl.program_id(0) == ncol - 1) & (pl.program_id(1) == nt - 1),
                     (pl.program_id(0) == ncol - 1) & (pl.program_id(1) == 0))
    return _run(_carry(call, comm, steps))


def _blk(s, nb, rev):
    return jnp.where(s == 0, nb - 1, (nb - 1 - s) if rev else (s - 1))


def _hgrn_gate(fr, lbraw_ref, d):
    lb = _sig(lbraw_ref[d:d + 1, :] - lbraw_ref[2 + d:3 + d, :])
    sg = _sig(fr)
    return lb, sg, lb + (1.0 - lb) * sg


def _hgrn_fwd(p_main, lbraw, d, nb, comm=None):
    t_total = p_main.shape[0]
    rev = d == 1
    nch = TB // HC
    scale = HF ** -0.5

    def body(q_ref, f_ref, v_ref, lb_ref, o_ref, sp_ref, st):
        s = pl.program_id(0)

        @pl.when(s == 0)
        def _():
            st[...] = jnp.zeros_like(st)

        mb = _tri(HC, rev)
        m01 = _b01(mb)
        order = list(reversed(range(nch)) if rev else range(nch))
        hs_ = [slice(h * HF, (h + 1) * HF) for h in range(NH)]
        pre = {}
        for c in order:
            rows = slice(c * HC, (c + 1) * HC)
            _, _, f = _hgrn_gate(f_ref[rows, :].astype(F32), lb_ref, d)
            k = 1.0 - f
            cum = _dot01(m01, jnp.log(f))
            tot = cum[0:1, :] if rev else cum[HC - 1:HC, :]
            qd = _silu(q_ref[rows, :].astype(F32)) * scale * jnp.exp(cum)
            ki = k * jnp.exp(-cum)
            etot = jnp.exp(tot)
            pre[c] = (_mx(qd), _mx(ki), _mx(ki * etot), _mx(v_ref[rows, :]), etot)
        scs = {c: [_nt(pre[c][0][:, cs], pre[c][1][:, cs]) for cs in hs_] for c in order}
        upd = {c: [_tn(pre[c][3][:, cs], pre[c][2][:, cs]) for cs in hs_] for c in order}
        intra = {c: [_nn(jnp.where(mb, scs[c][h], 0.0), pre[c][3][:, cs]) for h, cs in enumerate(hs_)] for c in order}
        for c in order:
            rows = slice(c * HC, (c + 1) * HC)
            qdb, etot = pre[c][0], pre[c][4]
            for h, cs in enumerate(hs_):
                sth = st[h]
                stb = sth.astype(sp_ref.dtype)
                sp_ref[c, h] = stb
                o_ref[rows, cs] = (intra[c][h] + _nt(qdb[:, cs], stb)).astype(o_ref.dtype)
                st[h] = sth * etot[:, cs] + upd[c][h]

    col = lambda j: (lambda s: (_blk(s, nb, rev), j))
    call = dict(
        body=body, args=[p_main, p_main, p_main, lbraw], name=f"hgrn_fwd_{d}",
        out_shape=(jax.ShapeDtypeStruct((t_total, D), MXU_DTYPE),
                   jax.ShapeDtypeStruct((nch * nb, NH, HF, HF), MXU_DTYPE)),
        grid=(nb,),
        in_specs=[pl.BlockSpec((TB, D), col(0)), pl.BlockSpec((TB, D), col(1 + d)), pl.BlockSpec((TB, D), col(3)),
                  _full((8, D))],
        out_specs=(pl.BlockSpec((TB, D), col(0)),
                   pl.BlockSpec((nch, NH, HF, HF), lambda s: (_blk(s, nb, rev), 0, 0, 0))),
        scratch=[pltpu.VMEM((NH, HF, HF), F32)], sem=("arbitrary",), vmem_mb=40)
    return _run(_carry(call, comm, lambda: (pl.program_id(0) == 0, pl.program_id(0) == nb - 1,
                                            pl.program_id(0) == nb - 4)))


def _hgrn_bwd(p_main, lbraw, sprev, do, d, nb, prev, comm=None):
    t_total = p_main.shape[0]
    rev = d == 1
    nch = TB // HC
    scale = HF ** -0.5
    last = prev is not None
    odt = MXU_DTYPE if last else F32

    def body(q_ref, f_ref, v_ref, lb_ref, sp_ref, do_ref, *rest):
        if last:
            dqp_ref, dvp_ref = rest[:2]
            rest = rest[2:]
        dq_ref, df_ref, dv_ref, dlb_ref, dst = rest
        sp_id = pl.program_id(0)
        is_ctx = sp_id == nb - 1

        @pl.when(sp_id == 0)
        def _():
            dst[...] = jnp.zeros_like(dst)
            dlb_ref[...] = jnp.zeros_like(dlb_ref)

        mb = _tri(HC, rev)
        mbt = _tri(HC, not rev)
        m01 = _b01(mb)
        mt01 = _b01(mbt)
        order = list(range(nch) if rev else reversed(range(nch)))
        hs_ = [slice(h * HF, (h + 1) * HF) for h in range(NH)]
        pre = {}
        for c in order:
            rows = slice(c * HC, (c + 1) * HC)
            lb, sg, f = _hgrn_gate(f_ref[rows, :].astype(F32), lb_ref, d)
            k = 1.0 - f
            cum = _dot01(m01, jnp.log(f))
            tot = cum[0:1, :] if rev else cum[HC - 1:HC, :]
            e = jnp.exp(cum)
            ei = jnp.exp(-cum)
            etot = jnp.exp(tot)
            ee = ei * etot
            qraw = q_ref[rows, :].astype(F32)
            sq = _sig(qraw)
            qd = qraw * sq * scale * e
            ki = k * ei
            ke = k * ee
            dov = jnp.where(is_ctx, 0.0, do_ref[rows, :].astype(F32))
            pre[c] = dict(lb=lb, sg=sg, f=f, e=e, ei=ei, ee=ee, etot=etot, qd=qd, ki=ki, ke=ke,
                          dsq=sq * (1.0 + qraw * (1.0 - sq)),
                          qdb=_mx(qd), kib=_mx(ki), keb=_mx(ke), vb=_mx(v_ref[rows, :]), dob=_mx(dov))
        units = [(c, h) for c in order for h in range(NH)]
        col = lambda u, key: pre[u[0]][key][:, hs_[u[1]]]
        pt = {u: jnp.where(mbt, _nt(col(u, "kib"), col(u, "qdb")), 0.0) for u in units}
        dp = {u: jnp.where(mb, _nt(col(u, "dob"), col(u, "vb")), 0.0) for u in units}
        dpt = {u: jnp.where(mbt, _nt(col(u, "vb"), col(u, "dob")), 0.0) for u in units}
        dv_i = {u: _nn(pt[u], col(u, "dob")) for u in units}
        dqd_ = {u: _nn(dp[u], col(u, "kib")) + _nn(col(u, "dob"), sp_ref[u[0], u[1]]) for u in units}
        dki_ = {u: _nn(dpt[u], col(u, "qdb")) for u in units}
        dsl = {u: _tn(col(u, "dob"), col(u, "qdb")) for u in units}
        for c in order:
            rows = slice(c * HC, (c + 1) * HC)
            p = pre[c]
            dv_l, dke_l, dtot_l = [], [], []
            for h, cs in enumerate(hs_):
                dso = dst[h]
                dsob = _mx(dso)
                dv_l.append(dv_i[(c, h)] + _nt(p["keb"][:, cs], dsob))
                dke_l.append(_nn(p["vb"][:, cs], dsob))
                dtot_l.append(_colsum(dso * sp_ref[c, h].astype(F32)) * p["etot"][:, cs])
                dst[h] = dso * p["etot"][:, cs] + dsl[(c, h)]
            lb, sg, f, e, ei, ee, qd, ki, ke = (p[n_] for n_ in ("lb", "sg", "f", "e", "ei", "ee", "qd", "ki", "ke"))
            dqd = jnp.concatenate([dqd_[(c, h)] for h in range(NH)], axis=1)
            dki = jnp.concatenate([dki_[(c, h)] for h in range(NH)], axis=1)
            dke = jnp.concatenate(dke_l, axis=1)
            dcum = dqd * qd - dki * ki - dke * ke
            dtot = jnp.concatenate(dtot_l, axis=1) + _colsum(dke * ke)
            dk = dki * ei + dke * ee
            dlf = _dot01(mt01, dcum, ways=2) + dtot
            df = dlf / f - dk
            dlb_ref[0:1, :] += _colsum(df * (1.0 - sg))
            dfr = df * (1.0 - lb) * sg * (1.0 - sg)
            dq = dqd * e * scale * p["dsq"]
            dv = jnp.concatenate(dv_l, axis=1)
            if last:
                dq = dq + dqp_ref[rows, :]
                dv = dv + dvp_ref[rows, :]
            dq_ref[rows, :] = dq.astype(odt)
            dv_ref[rows, :] = dv.astype(odt)
            df_ref[rows, :] = dfr.astype(MXU_DTYPE)

    blk = lambda s: _blk(nb - 1 - s, nb, rev)
    col = lambda j: (lambda s: (blk(s), j))
    in_specs = [pl.BlockSpec((TB, D), col(0)), pl.BlockSpec((TB, D), col(1 + d)), pl.BlockSpec((TB, D), col(3)),
                _full((8, D)), pl.BlockSpec((nch, NH, HF, HF), lambda s: (blk(s), 0, 0, 0)),
                pl.BlockSpec((TB, D), lambda s: (jnp.minimum(blk(s), nb - 2), 0))]
    args = [p_main, p_main, p_main, lbraw, sprev, do]
    if last:
        in_specs += [pl.BlockSpec((TB, D), col(0))] * 2
        args += list(prev)
    call = dict(
        body=body, args=args, name=f"hgrn_bwd_{d}",
        out_shape=(jax.ShapeDtypeStruct((t_total, D), odt), jax.ShapeDtypeStruct((t_total, D), MXU_DTYPE),
                   jax.ShapeDtypeStruct((t_total, D), odt), jax.ShapeDtypeStruct((8, D), F32)),
        grid=(nb,), in_specs=in_specs,
        out_specs=(pl.BlockSpec((TB, D), col(0)), pl.BlockSpec((TB, D), col(0)), pl.BlockSpec((TB, D), col(0)),
                   _full((8, D))),
        scratch=[pltpu.VMEM((NH, HF, HF), F32)], sem=("arbitrary",), vmem_mb=48)
    return _run(_carry(call, comm, lambda: (pl.program_id(0) == 0, pl.program_id(0) == nb - 1)))


def _conv_masks(tb, is_ctx):
    seg = jnp.where(is_ctx, tb, GRID_W)
    pos = lax.broadcasted_iota(jnp.int32, (tb, 1), 0) & (seg - 1)
    return pos, seg


def _shift_rows(x, dshift, pos, seg):
    if dshift == 0:
        return x
    n = x.shape[0]
    rolled = pltpu.roll(x, (-dshift) % n, 0)
    ok = (pos + dshift >= 0) & (pos + dshift < seg)
    return jnp.where(ok, rolled, 0.0)


def _ssd_prep(p_main, p_dt, convp, dtb, nb):
    t_total = p_main.shape[0]

    def body(x_ref, dt_ref, cw_ref, dtb_ref, xa_ref, ds_ref, dts_ref):
        is_ctx = pl.program_id(0) == nb - 1
        pos, seg = _conv_masks(TB, is_ctx)
        xv = x_ref[...].astype(F32)
        acc = cw_ref[5:6, :] + cw_ref[2:3, :] * xv
        for kk in (0, 1, 3, 4):
            acc = acc + cw_ref[kk:kk + 1, :] * _shift_rows(xv, kk - 2, pos, seg)
        sg = _sig(acc)
        xa_ref[...] = (acc * sg).astype(xa_ref.dtype)
        ds_ref[...] = (sg * (1.0 + acc * (1.0 - sg))).astype(ds_ref.dtype)
        dts_ref[...] = _softplus(dt_ref[...] + dtb_ref[0:1, :])

    wide = pl.BlockSpec((TB, 2048), lambda i: (i, 0))
    return _pcall(
        body, name="ssd_prep",
        out_shape=(jax.ShapeDtypeStruct((t_total, 2048), MXU_DTYPE), jax.ShapeDtypeStruct((t_total, 2048), MXU_DTYPE),
                   jax.ShapeDtypeStruct((t_total, 128), F32)),
        grid=(nb,),
        in_specs=[pl.BlockSpec((TB, 2048), lambda i: (i, 3)), pl.BlockSpec((TB, 128), lambda i: (i, 0)),
                  _full((8, 2048)), _full((8, 128))],
        out_specs=(wide, wide, pl.BlockSpec((TB, 128), lambda i: (i, 0))),
        sem=("parallel",), vmem_mb=32,
    )(p_main, p_dt, convp, dtb)


def _ssd_prep_bwd(p_main, p_dt, convp, dtb, dsl, dxa, dxs_skip, ddts, nb):
    t_total = p_main.shape[0]

    def body(x_ref, dt_ref, cw_ref, dtb_ref, ds_ref, dxa_ref, dsk_ref, ddts_ref, dx_ref, ddt_ref, dcw_ref, ddtb_ref):
        i = pl.program_id(0)
        is_ctx = i == nb - 1

        @pl.when(i == 0)
        def _():
            dcw_ref[...] = jnp.zeros_like(dcw_ref)
            ddtb_ref[...] = jnp.zeros_like(ddtb_ref)

        pos, seg = _conv_masks(TB, is_ctx)
        xv = x_ref[...].astype(F32)
        dact = dxa_ref[...]
        dact = jnp.concatenate([dact[:, :D] + jnp.where(is_ctx, 0.0, dsk_ref[...].astype(F32)), dact[:, D:]], axis=1)
        dpre = dact * ds_ref[...].astype(F32)
        dxv = cw_ref[2:3, :] * dpre
        dcw_ref[2:3, :] += _colsum(xv * dpre)
        for kk in (0, 1, 3, 4):
            sdp = _shift_rows(dpre, 2 - kk, pos, seg)
            dxv = dxv + cw_ref[kk:kk + 1, :] * sdp
            dcw_ref[kk:kk + 1, :] += _colsum(xv * sdp)
        dx_ref[...] = dxv.astype(dx_ref.dtype)
        dcw_ref[5:6, :] += _colsum(dpre)
        draw = ddts_ref[...] * _sig(dt_ref[...] + dtb_ref[0:1, :])
        ddt_ref[...] = draw.astype(ddt_ref.dtype)
        ddtb_ref[0:1, :] += _colsum(draw)

    return _pcall(
        body, name="ssd_prep_bwd",
        out_shape=(jax.ShapeDtypeStruct((t_total, 2048), MXU_DTYPE), jax.ShapeDtypeStruct((t_total, 128), MXU_DTYPE),
                   jax.ShapeDtypeStruct((8, 2048), F32), jax.ShapeDtypeStruct((8, 128), F32)),
        grid=(nb,),
        in_specs=[pl.BlockSpec((TB, 2048), lambda i: (i, 3)), pl.BlockSpec((TB, 128), lambda i: (i, 0)),
                  _full((8, 2048)), _full((8, 128)), pl.BlockSpec((TB, 2048), lambda i: (i, 0)),
                  pl.BlockSpec((TB, 2048), lambda i: (i, 0)),
                  pl.BlockSpec((TB, D), lambda i: (jnp.minimum(i, nb - 2), 0)),
                  pl.BlockSpec((TB, 128), lambda i: (i, 0))],
        out_specs=(pl.BlockSpec((TB, 2048), lambda i: (i, 0)), pl.BlockSpec((TB, 128), lambda i: (i, 0)),
                   _full((8, 2048)), _full((8, 128))),
        sem=("arbitrary",), vmem_mb=40,
    )(p_main, p_dt, convp, dtb, dsl, dxa, dxs_skip, ddts)


def _dot2(x, m01):
    hi = x.astype(BF16)
    lo = (x - hi.astype(F32)).astype(BF16)
    f = lambda t: lax.dot_general(t, m01, (((1,), (0,)), ((), ())), preferred_element_type=F32)
    return f(hi) + f(lo)


def _head_lanes(c0, c1):
    p = lax.broadcasted_iota(jnp.int32, (128, 128), 0)
    l = lax.broadcasted_iota(jnp.int32, (128, 128), 1)
    return _b01(((l == c0) & (p < SP)) | ((l == c1) & (p >= SP)))


def _one_lane(col):
    return _b01(lax.broadcasted_iota(jnp.int32, (128, 128), 1) == col)


def _lane_pick(x, lane, col):
    return _rowsum(jnp.where(lane == col, x, 0.0))


def _ssd_chunk_common(dts, alog_ref, m01, rev):
    lane = lax.broadcasted_iota(jnp.int32, (1, 128), 1)
    arow = -jnp.exp(alog_ref[0:1, :])
    cum = _dot01(m01, dts * arow)
    tot = cum[0:1, :] if rev else cum[SC - 1:SC, :]
    return lane, arow, cum, cum.T, tot


def _ssd_fwd(xa, dts, alog, d, nb):
    t_total = xa.shape[0]
    rev = d == 1
    nch = TB // SC
    npair = SHEADS // 2

    def body(xa_ref, dts_ref, alog_ref, y_ref, sp_ref, st):
        s = pl.program_id(0)

        @pl.when(s == 0)
        def _():
            st[...] = jnp.zeros_like(st)

        mb = _tri(SC, rev)
        m01 = _b01(mb)
        lo = lax.broadcasted_iota(jnp.int32, (1, 128), 1) < SP
        rlo = lax.broadcasted_iota(jnp.int32, (128, 1), 0) < SP
        order = list(reversed(range(nch)) if rev else range(nch))
        pre = {}
        for c in order:
            rows = slice(c * SC, (c + 1) * SC)
            dts_c = dts_ref[rows, :]
            lane, arow, cum, cumt, tot = _ssd_chunk_common(dts_c, alog_ref, m01, rev)
            bgs = [_mx(xa_ref[rows, D + g * SN:D + (g + 1) * SN]) for g in range(4)]
            cgs = [_mx(xa_ref[rows, D + 512 + g * SN:D + 512 + (g + 1) * SN]) for g in range(4)]
            pairs = []
            for pr in range(npair):
                xs = xa_ref[rows, pr * 128:(pr + 1) * 128].astype(F32)
                cols = [16 * d + 2 * pr, 16 * d + 2 * pr + 1]
                cum_c = [_lane_pick(cum, lane, q) for q in cols]
                dt_c = [_lane_pick(dts_c, lane, q) for q in cols]
                tot_c = [_lane_pick(tot, lane, q) for q in cols]
                dtx = xs * jnp.where(lo, dt_c[0], dt_c[1])
                e1_pair = jnp.where(lo, jnp.exp(cum_c[0]), jnp.exp(cum_c[1]))
                e2_pair = jnp.where(lo, jnp.exp(tot_c[0] - cum_c[0]), jnp.exp(tot_c[1] - cum_c[1]))
                etot_col = jnp.where(rlo, jnp.exp(tot_c[0]), jnp.exp(tot_c[1]))
                decs = [jnp.where(mb, jnp.exp(cum_c[q] - cumt[cols[q]:cols[q] + 1, :]), 0.0) for q in range(2)]
                dtxq = [_mx(jnp.where(lo if q == 0 else ~lo, dtx, 0.0)) for q in range(2)]
                pairs.append(dict(e1=e1_pair, etot=etot_col, decs=decs, dtxq=dtxq, xe=_mx(dtx * e2_pair)))
            pre[c] = (bgs, cgs, pairs)
        gm = {(c, g): _nt(pre[c][1][g], pre[c][0][g]) for c in order for g in range(4)}
        upd = {(c, pr): _tn(pre[c][2][pr]["xe"], pre[c][0][pr // 2]) for c in order for pr in range(npair)}
        intra = {(c, pr): sum(_nn(gm[(c, pr // 2)] * pre[c][2][pr]["decs"][q], pre[c][2][pr]["dtxq"][q]) for q in range(2))
                 for c in order for pr in range(npair)}
        for c in order:
            rows = slice(c * SC, (c + 1) * SC)
            bgs, cgs, pairs = pre[c]
            for pr in range(npair):
                stp = st[pr]
                stb = stp.astype(sp_ref.dtype)
                sp_ref[c, pr] = stb
                y_ref[rows, pr * 128:(pr + 1) * 128] = (
                    intra[(c, pr)] + pairs[pr]["e1"] * _nt(cgs[pr // 2], stb)).astype(y_ref.dtype)
                st[pr] = stp * pairs[pr]["etot"] + upd[(c, pr)]

    blk = lambda s: _blk(s, nb, rev)
    return _pcall(
        body, name=f"ssd_fwd_{d}",
        out_shape=(jax.ShapeDtypeStruct((t_total, D), MXU_DTYPE),
                   jax.ShapeDtypeStruct((nch * nb, npair, 128, SN), MXU_DTYPE)),
        grid=(nb,),
        in_specs=[pl.BlockSpec((TB, 2048), lambda s: (blk(s), 0)), pl.BlockSpec((TB, 128), lambda s: (blk(s), 0)),
                  _full((8, 128))],
        out_specs=(pl.BlockSpec((TB, D), lambda s: (blk(s), 0)),
                   pl.BlockSpec((nch, npair, 128, SN), lambda s: (blk(s), 0, 0, 0))),
        scratch=[pltpu.VMEM((npair, 128, SN), F32)], sem=("arbitrary",), vmem_mb=40,
    )(xa, dts, alog)


def _ssd_bwd(xa, dts, alog, sprev, dy, d, nb, prev, comm=None):
    t_total = xa.shape[0]
    rev = d == 1
    nch = TB // SC
    npair = SHEADS // 2
    last = prev is not None

    def body(xa_ref, dts_ref, alog_ref, sp_ref, dy_ref, *rest):
        if last:
            dxp_ref, ddp_ref = rest[:2]
            rest = rest[2:]
        dxa_ref, ddts_ref, da_ref, dst, zc_scr = rest
        sp_id = pl.program_id(0)
        is_ctx = sp_id == nb - 1

        @pl.when(sp_id == 0)
        def _():
            dst[...] = jnp.zeros_like(dst)
            da_ref[...] = jnp.zeros_like(da_ref)
            zc_scr[...] = jnp.zeros_like(zc_scr)

        mb = _tri(SC, rev)
        m01 = _b01(mb)
        mt01 = _b01(_tri(SC, not rev))
        lo = lax.broadcasted_iota(jnp.int32, (1, 128), 1) < SP
        rlo = lax.broadcasted_iota(jnp.int32, (128, 1), 0) < SP
        order = list(range(nch) if rev else reversed(range(nch)))
        pre = {}
        for c in order:
            rows = slice(c * SC, (c + 1) * SC)
            dts_c = dts_ref[rows, :]
            lane, arow, cum, cumt, tot = _ssd_chunk_common(dts_c, alog_ref, m01, rev)
            pairs = []
            for pr in range(npair):
                xs = xa_ref[rows, pr * 128:(pr + 1) * 128].astype(F32)
                dyp = jnp.where(is_ctx, 0.0, dy_ref[rows, pr * 128:(pr + 1) * 128].astype(F32))
                cols = [16 * d + 2 * pr, 16 * d + 2 * pr + 1]
                cum_c = [_lane_pick(cum, lane, q) for q in cols]
                dt_c = [_lane_pick(dts_c, lane, q) for q in cols]
                tot_c = [_lane_pick(tot, lane, q) for q in cols]
                e1_c = [jnp.exp(cum_c[q]) for q in range(2)]
                e2_c = [jnp.exp(tot_c[q] - cum_c[q]) for q in range(2)]
                etot_c = [jnp.exp(tot_c[q]) for q in range(2)]
                dt_pair = jnp.where(lo, dt_c[0], dt_c[1])
                e1_pair = jnp.where(lo, e1_c[0], e1_c[1])
                e2_pair = jnp.where(lo, e2_c[0], e2_c[1])
                dtx = xs * dt_pair
                decs = [jnp.where(mb, jnp.exp(cum_c[q] - cumt[cols[q]:cols[q] + 1, :]), 0.0) for q in range(2)]
                dyq = [_mx(jnp.where(lo if q == 0 else ~lo, dyp, 0.0)) for q in range(2)]
                pairs.append(dict(xs=xs, dyp=dyp, cols=cols, e1_c=e1_c, e2_c=e2_c, etot_c=etot_c, dt_pair=dt_pair,
                                  e2_pair=e2_pair, etot_col=jnp.where(rlo, etot_c[0], etot_c[1]), dtx=dtx,
                                  dtxb=_mx(dtx), xeb=_mx(dtx * e2_pair), dy0b=_mx(dyp * e1_pair), decs=decs, dyq=dyq))
            pre[c] = dict(lane=lane, arow=arow, dts=dts_c, pairs=pairs, cum=cum, tot=tot,
                          bgb=[_mx(xa_ref[rows, D + g * SN:D + (g + 1) * SN]) for g in range(4)],
                          cgb=[_mx(xa_ref[rows, D + 512 + g * SN:D + 512 + (g + 1) * SN]) for g in range(4)])
        units = [(c, pr) for c in order for pr in range(npair)]
        P = lambda u: pre[u[0]]["pairs"][u[1]]
        cgu = lambda u: pre[u[0]]["cgb"][u[1] // 2]
        gm = {(c, g): _nt(pre[c]["cgb"][g], pre[c]["bgb"][g]) for c in order for g in range(4)}
        y0 = {u: _nt(cgu(u), sp_ref[u[0], u[1]]) for u in units}
        dcg_i = {u: _nn(P(u)["dy0b"], sp_ref[u[0], u[1]]) for u in units}
        dsl = {u: _tn(P(u)["dy0b"], cgu(u)) for u in units}
        w_ = {(u, q): gm[(u[0], u[1] // 2)] * P(u)["decs"][q] for u in units for q in range(2)}
        dw_ = {(u, q): jnp.where(mb, _nt(P(u)["dyq"][q], P(u)["dtxb"]), 0.0) for u in units for q in range(2)}
        ddtx_i = {(u, q): _tn(w_[(u, q)], P(u)["dyq"][q]) for u in units for q in range(2)}
        for c in order:
            rows = slice(c * SC, (c + 1) * SC)
            pc = pre[c]
            lane, arow, dts_c = pc["lane"], pc["arow"], pc["dts"]
            d1 = jnp.zeros((SC, 128), F32)
            d2 = jnp.zeros((SC, 128), F32)
            dz = jnp.zeros((SC, 128), F32)
            ddt = jnp.zeros((SC, 128), F32)
            dtot = jnp.zeros((1, 128), F32)
            dgm = [jnp.zeros((SC, SC), F32) for _ in range(4)]
            dbg = [jnp.zeros((SC, SN), F32) for _ in range(4)]
            dcg = [jnp.zeros((SC, SN), F32) for _ in range(4)]
            for pr in range(npair):
                u, g, p = (c, pr), pr // 2, pc["pairs"][pr]
                hs = _head_lanes(*p["cols"])
                dso = dst[pr]
                dsob = _mx(dso)
                dxe = _nt(pc["bgb"][g], dsob)
                dbg[g] = dbg[g] + _nn(p["xeb"], dsob)
                ddtx = dxe * p["e2_pair"]
                d2 = d2 + _dot2(dxe * p["dtx"], hs)
                dcg[g] = dcg[g] + dcg_i[u]
                d1 = d1 + _dot2(p["dyp"] * y0[u], hs)
                sprod = dso * sp_ref[c, pr].astype(F32)
                dst[pr] = dso * p["etot_col"] + dsl[u]
                for q in range(2):
                    hm = lo if q == 0 else ~lo
                    col = p["cols"][q]
                    dw = dw_[(u, q)]
                    ddtx = ddtx + jnp.where(hm, ddtx_i[(u, q)], 0.0)
                    dgm[g] = dgm[g] + dw * p["decs"][q]
                    z = dw * w_[(u, q)]
                    dz = dz + _dot2(z, _one_lane(col))
                    zc_scr[col:col + 1, :] = _colsum(z)
                    tsum = _rowsum(_colsum(sprod[q * SP:(q + 1) * SP, :]))
                    dtot = jnp.where(lane == col, tsum * p["etot_c"][q], dtot)
                dxs = ddtx * p["dt_pair"]
                ddt = ddt + _dot2(ddtx * p["xs"], hs)
                if last:
                    dxs = dxs + dxp_ref[rows, pr * 128:(pr + 1) * 128]
                dxa_ref[rows, pr * 128:(pr + 1) * 128] = dxs
            e2_all = jnp.exp(pc["tot"] - pc["cum"])
            dcum = dz - zc_scr[...].T + d1 * jnp.exp(pc["cum"]) - d2 * e2_all
            dtot = dtot + _colsum(d2 * e2_all)
            for g in range(4):
                db = dbg[g] + _tn(dgm[g], pc["cgb"][g])
                dc = dcg[g] + _nn(dgm[g], pc["bgb"][g])
                if last:
                    db = db + dxp_ref[rows, D + g * SN:D + (g + 1) * SN]
                    dc = dc + dxp_ref[rows, D + 512 + g * SN:D + 512 + (g + 1) * SN]
                dxa_ref[rows, D + g * SN:D + (g + 1) * SN] = db
                dxa_ref[rows, D + 512 + g * SN:D + 512 + (g + 1) * SN] = dc
            dla = _dot01(mt01, dcum, ways=2) + dtot
            ddt = ddt + dla * arow
            da_ref[0:1, :] += _colsum(dla * dts_c)
            if last:
                ddt = ddt + ddp_ref[rows, :]
            ddts_ref[rows, :] = ddt

    blk = lambda s: _blk(nb - 1 - s, nb, rev)
    in_specs = [pl.BlockSpec((TB, 2048), lambda s: (blk(s), 0)), pl.BlockSpec((TB, 128), lambda s: (blk(s), 0)),
                _full((8, 128)), pl.BlockSpec((nch, npair, 128, SN), lambda s: (blk(s), 0, 0, 0)),
                pl.BlockSpec((TB, D), lambda s: (jnp.minimum(blk(s), nb - 2), 0))]
    args = [xa, dts, alog, sprev, dy]
    if last:
        in_specs += [pl.BlockSpec((TB, 2048), lambda s: (blk(s), 0)), pl.BlockSpec((TB, 128), lambda s: (blk(s), 0))]
        args += list(prev)
    call = dict(
        body=body, args=args, name=f"ssd_bwd_{d}",
        out_shape=(jax.ShapeDtypeStruct((t_total, 2048), F32), jax.ShapeDtypeStruct((t_total, 128), F32),
                   jax.ShapeDtypeStruct((8, 128), F32)),
        grid=(nb,), in_specs=in_specs,
        out_specs=(pl.BlockSpec((TB, 2048), lambda s: (blk(s), 0)), pl.BlockSpec((TB, 128), lambda s: (blk(s), 0)),
                   _full((8, 128))),
        scratch=[pltpu.VMEM((npair, 128, SN), F32), pltpu.VMEM((128, 128), F32)], sem=("arbitrary",), vmem_mb=48)
    return _run(_carry(call, comm, lambda: (pl.program_id(0) == 0, pl.program_id(0) == nb - 1)))


def _readout(o, g, yy, z, vec_ref):
    hg, ss, keep = [], [], []
    for h in range(NH):
        cs = slice(h * HF, (h + 1) * HF)
        oh = o[:, cs]
        r = lax.rsqrt(jnp.mean(oh * oh, axis=1, keepdims=True) + EPS)
        hg.append(oh * r * vec_ref[0:1, cs] * _silu(g[:, cs]))
        keep.append(r)
    u = yy * _silu(z)
    for gi in range(4):
        cs = slice(gi * 256, (gi + 1) * 256)
        ug = u[:, cs]
        r = lax.rsqrt(jnp.mean(ug * ug, axis=1, keepdims=True) + EPS)
        ss.append(ug * r * vec_ref[2:3, cs])
        keep.append(r)
    return jnp.concatenate(hg, axis=1), jnp.concatenate(ss, axis=1), keep, u


def _mix_out(o_f, o_b, p_main, y_f, y_b, xa, x, vecs, w_out):
    n = x.shape[0]

    def body(of_ref, ob_ref, g_ref, z_ref, yf_ref, yb_ref, xs_ref, x_ref, vec_ref, w_ref,
             ymix_ref, ylat_ref, h1_ref, u2_ref):
        o = of_ref[...].astype(F32) + ob_ref[...].astype(F32)
        yy = yf_ref[...].astype(F32) + yb_ref[...].astype(F32) + vec_ref[1:2, :] * xs_ref[...].astype(F32)
        hg, ss, _, _ = _readout(o, g_ref[...].astype(F32), yy, z_ref[...].astype(F32), vec_ref)
        ymix = jnp.concatenate([hg, ss], axis=1).astype(MXU_DTYPE)
        ymix_ref[...] = ymix
        ylat = _nn(ymix, w_ref[...])
        ylat_ref[...] = ylat
        h1 = x_ref[...] + vec_ref[3:4, :] * ylat
        h1_ref[...] = h1
        r = lax.rsqrt(jnp.mean(h1 * h1, axis=1, keepdims=True) + EPS)
        u2_ref[...] = ((h1 * r * vec_ref[6:7, :]) * vec_ref[4:5, :] + vec_ref[5:6, :]).astype(MXU_DTYPE)

    row = lambda j: (lambda i: (i, j))
    return _pcall(
        body, name="mix_out",
        out_shape=(jax.ShapeDtypeStruct((n, 2 * D), MXU_DTYPE), jax.ShapeDtypeStruct((n, D), F32),
                   jax.ShapeDtypeStruct((n, D), F32), jax.ShapeDtypeStruct((n, D), MXU_DTYPE)),
        grid=(n // TB,),
        in_specs=[pl.BlockSpec((TB, D), row(0)), pl.BlockSpec((TB, D), row(0)), pl.BlockSpec((TB, D), row(4)),
                  pl.BlockSpec((TB, D), row(5)), pl.BlockSpec((TB, D), row(0)), pl.BlockSpec((TB, D), row(0)),
                  pl.BlockSpec((TB, D), row(0)), pl.BlockSpec((TB, D), row(0)), _full((8, D)), _full((2 * D, D))],
        out_specs=(pl.BlockSpec((TB, 2 * D), row(0)), pl.BlockSpec((TB, D), row(0)), pl.BlockSpec((TB, D), row(0)),
                   pl.BlockSpec((TB, D), row(0))),
        sem=("parallel",), vmem_mb=48,
    )(o_f, o_b, p_main, p_main, y_f, y_b, xa, x, vecs, w_out)


def _mix_bwd(dylat, o_f, o_b, p_main, y_f, y_b, xa, vecs, w_out, comm=None):
    n = dylat.shape[0]
    t_total = p_main.shape[0]
    nlat = n // TB

    def body(*refs):
        dg_ref, dz_ref, acc_ref = refs[11], refs[13], refs[15]
        i = pl.program_id(0)

        @pl.when(i == 0)
        def _():
            acc_ref[...] = jnp.zeros_like(acc_ref)

        @pl.when(i < nlat)
        def _():
            compute(*refs)

        @pl.when(i == nlat)
        def _():
            dg_ref[...] = jnp.zeros_like(dg_ref)
            dz_ref[...] = jnp.zeros_like(dz_ref)

    def compute(dyl_ref, of_ref, ob_ref, g_ref, z_ref, yf_ref, yb_ref, xs_ref, vec_ref, w_ref,
                do_ref, dg_ref, dys_ref, dz_ref, dxs_ref, acc_ref):
        dymix = _nt(dyl_ref[...], w_ref[...])
        o = of_ref[...].astype(F32) + ob_ref[...].astype(F32)
        g = g_ref[...].astype(F32)
        z = z_ref[...].astype(F32)
        xs = xs_ref[...].astype(F32)
        yy = yf_ref[...].astype(F32) + yb_ref[...].astype(F32) + vec_ref[1:2, :] * xs
        _, _, keep, u = _readout(o, g, yy, z, vec_ref)
        do_l, dg_l = [], []
        for h in range(NH):
            cs = slice(h * HF, (h + 1) * HF)
            oh, gh, r, wv = o[:, cs], g[:, cs], keep[h], vec_ref[0:1, cs]
            dhg = dymix[:, cs]
            xh = oh * r
            dn = dhg * _silu(gh)
            dg_l.append(dhg * xh * wv * _dsilu(gh))
            acc_ref[0:1, cs] += _colsum(dn * xh)
            dxh = dn * wv
            do_l.append(r * (dxh - xh * jnp.mean(dxh * xh, axis=1, keepdims=True)))
        du_l = []
        for gi in range(4):
            cs = slice(gi * 256, (gi + 1) * 256)
            ug, r, wv = u[:, cs], keep[NH + gi], vec_ref[2:3, cs]
            dss = dymix[:, D + gi * 256:D + (gi + 1) * 256]
            xh = ug * r
            acc_ref[2:3, cs] += _colsum(dss * xh)
            dxh = dss * wv
            du_l.append(r * (dxh - xh * jnp.mean(dxh * xh, axis=1, keepdims=True)))
        du = jnp.concatenate(du_l, axis=1)
        dyy = du * _silu(z)
        do_ref[...] = jnp.concatenate(do_l, axis=1).astype(do_ref.dtype)
        dg_ref[...] = jnp.concatenate(dg_l, axis=1).astype(dg_ref.dtype)
        dys_ref[...] = dyy.astype(dys_ref.dtype)
        dz_ref[...] = (du * yy * _dsilu(z)).astype(dz_ref.dtype)
        dxs_ref[...] = (dyy * vec_ref[1:2, :]).astype(dxs_ref.dtype)
        acc_ref[1:2, :] += _colsum(dyy * xs)

    row = lambda j: (lambda i: (jnp.minimum(i, nlat - 1), j))
    lat = pl.BlockSpec((TB, D), row(0))
    tok = pl.BlockSpec((TB, D), lambda i: (i, 0))
    call = dict(
        body=body, args=[dylat, o_f, o_b, p_main, p_main, y_f, y_b, xa, vecs, w_out], name="mix_bwd",
        out_shape=(jax.ShapeDtypeStruct((n, D), MXU_DTYPE), jax.ShapeDtypeStruct((t_total, D), MXU_DTYPE),
                   jax.ShapeDtypeStruct((n, D), MXU_DTYPE), jax.ShapeDtypeStruct((t_total, D), MXU_DTYPE),
                   jax.ShapeDtypeStruct((n, D), MXU_DTYPE), jax.ShapeDtypeStruct((8, D), F32)),
        grid=(t_total // TB,),
        in_specs=[lat, lat, lat, pl.BlockSpec((TB, D), row(4)), pl.BlockSpec((TB, D), row(5)), lat, lat, lat,
                  _full((8, D)), _full((2 * D, D))],
        out_specs=(lat, tok, lat, tok, lat, _full((8, D))), scratch=[],
        sem=("arbitrary",), vmem_mb=48)
    return _run(_carry(call, comm, lambda: (pl.program_id(0) == 0, pl.program_id(0) == t_total // TB - 1)))


def _ffn_up(u2, w_gate, w_up):
    n = u2.shape[0]
    tb = 1024

    def body(u_ref, wg_ref, wu_ref, g_ref, up_ref, a_ref):
        uv = u_ref[...]
        gt = _nt(uv, wg_ref[...])
        upv = _nt(uv, wu_ref[...])
        g_ref[...] = gt.astype(g_ref.dtype)
        up_ref[...] = upv.astype(up_ref.dtype)
        a_ref[...] = (_silu(gt) * upv).astype(a_ref.dtype)

    blk = pl.BlockSpec((tb, FSL), lambda j, i: (i, j))
    wblk = pl.BlockSpec((None, FSL, D), lambda j, i: (j, 0, 0))
    return _pcall(
        body, name="ffn_up",
        out_shape=(jax.ShapeDtypeStruct((n, DFFP), MXU_DTYPE),) * 3,
        grid=(4, n // tb), in_specs=[pl.BlockSpec((tb, D), lambda j, i: (i, 0)), wblk, wblk],
        out_specs=(blk, blk, blk), sem=("parallel", "parallel"), vmem_mb=48,
    )(u2, w_gate, w_up)


def _ffn_down_loss(act, w_down, h1, tgt, vecs):
    n = act.shape[0]
    tb = 512

    def body(a_ref, w_ref, h1_ref, t_ref, vec_ref, dh2_ref, dffn_ref, acc_ref):
        i = pl.program_id(0)

        @pl.when(i == 0)
        def _():
            acc_ref[...] = jnp.zeros_like(acc_ref)

        g2 = vec_ref[0:1, :]
        fw = vec_ref[1:2, :]
        nsub = 4
        sb = tb // nsub
        wv = w_ref[...]
        ffns = [_nn(a_ref[r_ * sb:(r_ + 1) * sb, :], wv) for r_ in range(nsub)]
        for r_ in range(nsub):
            rows = slice(r_ * sb, (r_ + 1) * sb)
            ffn = ffns[r_]
            h2 = h1_ref[rows, :] + g2 * ffn
            r = lax.rsqrt(jnp.mean(h2 * h2, axis=1, keepdims=True) + EPS)
            xh = h2 * r
            err = xh * fw - t_ref[rows, :]
            dy = err * (1.0 / D)
            acc_ref[2:3, :] += _colsum(err * err) * (0.5 / D)
            acc_ref[1:2, :] += _colsum(dy * xh)
            dxh = dy * fw
            dh2 = r * (dxh - xh * jnp.mean(dxh * xh, axis=1, keepdims=True))
            dh2_ref[rows, :] = dh2
            dffn_ref[rows, :] = (g2 * dh2).astype(dffn_ref.dtype)
            acc_ref[0:1, :] += _colsum(dh2 * ffn)

    return _pcall(
        body, name="ffn_down_loss",
        out_shape=(jax.ShapeDtypeStruct((n, D), F32), jax.ShapeDtypeStruct((n, D), MXU_DTYPE),
                   jax.ShapeDtypeStruct((8, D), F32)),
        grid=(n // tb,),
        in_specs=[pl.BlockSpec((tb, DFFP), lambda i: (i, 0)), _full((DFFP, D)), pl.BlockSpec((tb, D), lambda i: (i, 0)),
                  pl.BlockSpec((tb, D), lambda i: (i, 0)), _full((8, D))],
        out_specs=(pl.BlockSpec((tb, D), lambda i: (i, 0)), pl.BlockSpec((tb, D), lambda i: (i, 0)), _full((8, D))),
        sem=("arbitrary",), vmem_mb=48,
    )(act, w_down, h1, tgt, vecs)


def _ffn_bwd(dffn, w_down, gate, up, w_gate_t, w_up_t):
    n = dffn.shape[0]
    tb = 1024

    def body(df_ref, wd_ref, g_ref, up_ref, wg_ref, wu_ref, dg_ref, dup_ref, du_ref):
        j = pl.program_id(1)
        nsub = 4
        sb = tb // nsub
        wd, wg, wu = wd_ref[...], wg_ref[...], wu_ref[...]
        dacts = [_nt(df_ref[r * sb:(r + 1) * sb, :], wd) for r in range(nsub)]
        parts = []
        for r in range(nsub):
            rows = slice(r * sb, (r + 1) * sb)
            gt = g_ref[rows, :].astype(F32)
            upv = up_ref[rows, :].astype(F32)
            sg = _sig(gt)
            dgt = (dacts[r] * upv * (sg * (1.0 + gt * (1.0 - sg)))).astype(MXU_DTYPE)
            dupv = (dacts[r] * (gt * sg)).astype(MXU_DTYPE)
            dg_ref[rows, :] = dgt
            dup_ref[rows, :] = dupv
            parts.append(_nn(dgt, wg) + _nn(dupv, wu))
        part = jnp.concatenate(parts, axis=0)

        @pl.when(j == 0)
        def _():
            du_ref[...] = part

        @pl.when(j > 0)
        def _():
            du_ref[...] += part

    tok = pl.BlockSpec((tb, D), lambda i, j: (i, 0))
    ffb = pl.BlockSpec((tb, FSL), lambda i, j: (i, j))
    wsl = pl.BlockSpec((None, FSL, D), lambda i, j: (j, 0, 0))
    return _pcall(
        body, name="ffn_bwd",
        out_shape=(jax.ShapeDtypeStruct((n, DFFP), MXU_DTYPE), jax.ShapeDtypeStruct((n, DFFP), MXU_DTYPE),
                   jax.ShapeDtypeStruct((n, D), F32)),
        grid=(n // tb, 4),
        in_specs=[tok, pl.BlockSpec((FSL, D), lambda i, j: (j, 0)), ffb, ffb, wsl, wsl],
        out_specs=(ffb, ffb, tok), sem=("parallel", "arbitrary"), vmem_mb=48,
    )(dffn, w_down, gate, up, w_gate_t, w_up_t)


def _ffn_norm_bwd(du, h1, ylat, dh2, vecs):
    n = du.shape[0]
    tb = 512

    def body(du_ref, h1_ref, yl_ref, dh2_ref, vec_ref, dh1_ref, dyl_ref, acc_ref):
        @pl.when(pl.program_id(0) == 0)
        def _():
            acc_ref[...] = jnp.zeros_like(acc_ref)

        duv = du_ref[...]
        h1 = h1_ref[...]
        r = lax.rsqrt(jnp.mean(h1 * h1, axis=1, keepdims=True) + EPS)
        xh = h1 * r
        nw = vec_ref[2:3, :]
        acc_ref[0:1, :] += _colsum(duv)
        acc_ref[1:2, :] += _colsum(duv * xh * nw)
        dn = duv * vec_ref[1:2, :]
        acc_ref[2:3, :] += _colsum(dn * xh)
        dxh = dn * nw
        dh1 = dh2_ref[...] + r * (dxh - xh * jnp.mean(dxh * xh, axis=1, keepdims=True))
        dh1_ref[...] = dh1
        dyl_ref[...] = (vec_ref[0:1, :] * dh1).astype(dyl_ref.dtype)
        acc_ref[3:4, :] += _colsum(dh1 * yl_ref[...])

    tok = pl.BlockSpec((tb, D), lambda i: (i, 0))
    return _pcall(
        body, name="ffn_norm_bwd",
        out_shape=(jax.ShapeDtypeStruct((n, D), F32), jax.ShapeDtypeStruct((n, D), MXU_DTYPE),
                   jax.ShapeDtypeStruct((8, D), F32)),
        grid=(n // tb,), in_specs=[tok, tok, tok, tok, _full((8, D))], out_specs=(tok, tok, _full((8, D))),
        sem=("arbitrary",), vmem_mb=40,
    )(du, h1, ylat, dh2, vecs)


def _deep_rows(rows):
    return max(r for r in range(128, 2305, 128) if rows % r == 0)


def _dw(a, b, name):
    tn_rows = a.shape[0]
    bt = _deep_rows(tn_rows)
    kk, nn_ = a.shape[1], b.shape[1]
    bk = 1024 if kk % 1024 == 0 else kk
    bn = 1024 if nn_ % 1024 == 0 else nn_
    nt = tn_rows // bt

    def body(a_ref, b_ref, o_ref, acc):
        t = pl.program_id(2)
        part = _tn(a_ref[...], b_ref[...])

        @pl.when(t == 0)
        def _():
            acc[...] = part

        @pl.when(t > 0)
        def _():
            acc[...] += part

        @pl.when(t == nt - 1)
        def _():
            o_ref[...] = acc[...].astype(o_ref.dtype)

    return _pcall(
        body, name=name, out_shape=jax.ShapeDtypeStruct((kk, nn_), MXU_DTYPE), grid=(kk // bk, nn_ // bn, nt),
        in_specs=[pl.BlockSpec((bt, bk), lambda i, j, t: (t, i)), pl.BlockSpec((bt, bn), lambda i, j, t: (t, j))],
        out_specs=pl.BlockSpec((bk, bn), lambda i, j, t: (i, j)), scratch=[pltpu.VMEM((bk, bn), F32)],
        sem=("parallel", "parallel", "arbitrary"), vmem_mb=40,
    )(a, b)


def _dw_in(segs, u_all, name):
    tiles = []
    for m, s_ in enumerate(segs):
        tiles += [(m, h) for h in range(s_.shape[1] // D)]
    ntile = len(tiles)
    t_total = u_all.shape[0]
    bt = _deep_rows(t_total)
    nt = t_total // bt

    def body(u_ref, *refs):
        seg_refs, o_ref, acc = refs[:len(segs)], refs[len(segs)], refs[len(segs) + 1]
        n, t = pl.program_id(0), pl.program_id(1)
        for k, (m, _) in enumerate(tiles):
            @pl.when(n == k)
            def _(m=m):
                part = _tn(seg_refs[m][...], u_ref[...])

                @pl.when(t == 0)
                def _():
                    acc[...] = part

                @pl.when(t > 0)
                def _():
                    acc[...] += part

        @pl.when(t == nt - 1)
        def _():
            o_ref[...] = acc[...].astype(o_ref.dtype)

    def seg_spec(m):
        ks = [k for k, (mm, _) in enumerate(tiles) if mm == m]
        lo, hi = ks[0], ks[-1]
        on = lambda n: (n >= lo) & (n <= hi)
        return pl.BlockSpec((bt, D), lambda n, t: (jnp.where(on(n), t, 0), jnp.where(on(n), n - lo, 0)))

    return _pcall(
        body, name=name, out_shape=jax.ShapeDtypeStruct((1, ntile * D, D), MXU_DTYPE), grid=(ntile, nt),
        in_specs=[pl.BlockSpec((bt, D), lambda n, t: (t, 0))] + [seg_spec(m) for m in range(len(segs))],
        out_specs=pl.BlockSpec((None, D, D), lambda n, t: (0, n, 0)),
        scratch=[pltpu.VMEM((D, D), F32)], sem=("parallel", "arbitrary"), vmem_mb=56,
    )(u_all, *segs)


def _du_prenorm_bwd(segs, ddt, wi_main, wi_tail, xin, mods, dres, row_off, tb, name, comm=None):
    n = xin.shape[0]
    nt = n // tb
    off = row_off // tb
    has_dx = dres is not None

    def body(*refs):
        seg_refs = refs[:7]
        ddt_ref, w_ref, wb_ref, wdt_ref, x_ref, mod_ref = refs[7:13]
        rest = refs[13:]
        if has_dx:
            dres_ref, dx_ref, acc_ref, du_scr = rest
        else:
            acc_ref, du_scr = rest
        j, i = pl.program_id(0), pl.program_id(1)
        rows = pl.ds(pl.multiple_of(i * tb, tb), tb)

        @pl.when((i == 0) & (j == 0))
        def _():
            acc_ref[...] = jnp.zeros_like(acc_ref)

        @pl.when(j == 0)
        def _():
            du_scr[rows, :] = _nn(ddt_ref[...], wdt_ref[...])

        for k in range(8):
            if not has_dx and k in (4, 5):
                continue

            @pl.when(j == k)
            def _(k=k):
                sv = seg_refs[min(k, 6)][...]
                part = _nn(sv, w_ref[...])
                if k in (2, 4, 6):
                    part = part + _nn(sv[:, 0:WTAIL], wb_ref[...])
                du_scr[rows, :] += part

        @pl.when(j == 7)
        def _():
            du = du_scr[rows, :]
            xv = x_ref[...]
            r = lax.rsqrt(jnp.mean(xv * xv, axis=1, keepdims=True) + EPS)
            xh = xv * r
            nw = mod_ref[1:2, :]
            acc_ref[0:1, :] += _colsum(du)
            acc_ref[1:2, :] += _colsum(du * xh * nw)
            dn = du * mod_ref[0:1, :]
            acc_ref[2:3, :] += _colsum(dn * xh)
            if has_dx:
                dxh = dn * nw
                dx_ref[...] = dres_ref[...] + r * (dxh - xh * jnp.mean(dxh * xh, axis=1, keepdims=True))

    def seg_spec(k):
        if k < 6:
            return pl.BlockSpec((tb, D), lambda j, i: (jnp.where(j == k, i + off, 0), 0))
        return pl.BlockSpec((tb, D), lambda j, i: (jnp.where(j >= 6, i + off, 0), jnp.where(j >= 6, j - 6, 0)))

    last = pl.BlockSpec((tb, D), lambda j, i: (jnp.where(j == 7, i, 0), 0))
    in_specs = [seg_spec(k) for k in range(7)]
    in_specs += [pl.BlockSpec((tb, 128), lambda j, i: (jnp.where(j == 0, i + off, 0), 0))] + _w_specs()
    in_specs += [last, _full((8, D))]
    args = list(segs) + [ddt, wi_main, wi_tail, wi_tail, xin, mods]
    out_shape = [jax.ShapeDtypeStruct((8, D), F32)]
    out_specs = [_full((8, D))]
    if has_dx:
        in_specs.append(last)
        args.append(dres)
        out_shape.insert(0, jax.ShapeDtypeStruct((n, D), F32))
        out_specs.insert(0, last)
    call = dict(body=body, args=args, name=name, out_shape=tuple(out_shape), grid=(8, nt), in_specs=in_specs,
                out_specs=tuple(out_specs), scratch=[pltpu.VMEM((n, D), F32)], sem=("arbitrary", "arbitrary"),
                vmem_mb=56)
    steps = lambda: ((pl.program_id(0) == 0) & (pl.program_id(1) == 0),
                     (pl.program_id(0) == 7) & (pl.program_id(1) == nt - 1))
    return _run(_carry(call, comm, steps))


def _sum8(v):
    def body(v_ref, o_ref):
        acc = v_ref[0]
        for k in range(1, 8):
            acc = acc + v_ref[k]
        o_ref[...] = acc

    return _pcall(body, name="small_sum", out_shape=jax.ShapeDtypeStruct(v.shape[1:], F32),
                  in_specs=[pl.BlockSpec(memory_space=pltpu.VMEM)], out_specs=pl.BlockSpec(memory_space=pltpu.VMEM))(v)


def _adamw(w, m, v, g, name, comm=None):
    lead = w.ndim == 3
    rows, cols = w.shape[-2:]
    rb = 256 if rows % 256 == 0 else (352 if rows % 352 == 0 else rows)
    c1 = 1.0 - B1 ** STEP
    c2 = 1.0 - B2 ** STEP

    def body(w_ref, m_ref, v_ref, g_ref, d_ref, nm_ref, nv_ref):
        gv = g_ref[...]
        mn = B1 * m_ref[...] + (1.0 - B1) * gv
        vn = B2 * v_ref[...] + (1.0 - B2) * (gv * gv)
        nm_ref[...] = mn
        nv_ref[...] = vn
        d_ref[...] = -LR * ((mn / c1) / (jnp.sqrt(vn / c2) + AEPS) + WD * w_ref[...])

    if rb == rows and rows > 1024:
        cb, steps = 256, cols // 256
        gspec = pl.BlockSpec((rows, cb), lambda i: (0, i))
        spec = pl.BlockSpec((None, rows, cb), lambda i: (0, 0, i)) if lead else gspec
    else:
        steps = rows // rb
        gspec = pl.BlockSpec((rb, cols), lambda i: (i, 0))
        spec = pl.BlockSpec((None, rb, cols), lambda i: (0, i, 0)) if lead else gspec
    call = dict(body=body, args=[w, m, v, g], name=name, out_shape=(jax.ShapeDtypeStruct(w.shape, F32),) * 3,
                grid=(steps,), in_specs=[spec] * 3 + [gspec], out_specs=(spec,) * 3, scratch=[],
                sem=("arbitrary",) if comm is not None else ("parallel",), vmem_mb=40)
    return _run(_carry(call, comm, lambda: (pl.program_id(0) == 0, pl.program_id(0) == steps - 1,
                                            pl.program_id(0) == steps - 1)))


def _rows(v, n):
    f = v.reshape(-1)
    return jnp.pad(f, (0, n * D - f.shape[0])).reshape(n, D)


def kernel(x, c, ctx, c_ctx, w_ada, b_ada, norm_mix, w_in, conv_w, conv_b, ssd_a_log, ssd_dt_bias, ssd_d, ssd_norm, hgrn_lb_raw, hgrn_norm, w_out, norm_ffn, w_gate, w_up, w_down, final_norm, loss_target, m_c_ctx, m_w_ada, m_b_ada, m_norm_mix, m_w_in, m_conv_w, m_conv_b, m_ssd_a_log, m_ssd_dt_bias, m_ssd_d, m_ssd_norm, m_hgrn_lb_raw, m_hgrn_norm, m_w_out, m_norm_ffn, m_w_gate, m_w_up, m_w_down, m_final_norm, v_c_ctx, v_w_ada, v_b_ada, v_norm_mix, v_w_in, v_conv_w, v_conv_b, v_ssd_a_log, v_ssd_dt_bias, v_ssd_d, v_ssd_norm, v_hgrn_lb_raw, v_hgrn_norm, v_w_out, v_norm_ffn, v_w_gate, v_w_up, v_w_down, v_final_norm):
    ix, iy, ic = lax.axis_index("x"), lax.axis_index("y"), lax.axis_index("c")
    chip = 2 * ix + iy
    me = 2 * chip + ic
    xl, xc, tgt = x[0], ctx[0], loss_target[0]
    n_lat, n_ctx = xl.shape[0], xc.shape[0]
    assert n_ctx == TB and n_lat % 1024 == 0
    t_total = n_lat + n_ctx
    nb = t_total // TB

    tr = lambda a: jnp.swapaxes(a, -1, -2)
    shift = [functools.partial(jnp.pad, pad_width=((8 * k, WSL + WTAIL - NSH - 8 * k), (0, 0))) for k in range(4)]
    slab = lax.switch(chip, shift, tr(w_in[0]).astype(MXU_DTYPE))
    padrows = lambda a: jnp.pad(a, ((0, FSL - DFF // 4), (0, 0))).astype(MXU_DTYPE)
    shards = [slab[:WSL], slab[WSL:], w_out[0].astype(MXU_DTYPE), padrows(tr(w_gate[0])), padrows(tr(w_up[0])),
              padrows(w_down[0])]
    own = lambda g_, s_: lax.dynamic_update_slice(g_, s_[None], (chip, 0, 0))
    pack = jnp.concatenate([c, hgrn_lb_raw.reshape(1, D), _rows(conv_w[0], 3), jnp.zeros((3, D), F32)], axis=0)
    ncol_ada = w_ada.shape[2]
    b_shard = lax.dynamic_slice(b_ada, (0, chip * ncol_ada), (1, ncol_ada))
    gath, araw, mod_all, wi_main, wi_tail = _prologue(pack, c_ctx.reshape(1, D), w_ada[0], b_shard, shards[:2])
    wi_main, wi_tail = own(wi_main, shards[0]), own(wi_tail, shards[1])
    gath = gath.reshape(8, 8, D)
    lbraw_full = gath[0::2, 1].reshape(4, 2, 2, 256).transpose(1, 2, 0, 3).reshape(4, D)
    convw_full = gath[0::2, 2:5].reshape(4, 3 * D)[:, :KCONV * 512].reshape(4, KCONV, 512).transpose(1, 0, 2)
    convw_full = convw_full.reshape(KCONV, 2048)
    lbraw8 = jnp.pad(lbraw_full, ((0, 4), (0, 0)))
    convp = jnp.concatenate([convw_full, conv_b, jnp.zeros((2, 2048), F32)], axis=0)
    dtb = jnp.pad(ssd_dt_bias.reshape(1, 32), ((0, 7), (0, 96)))
    alog = jnp.pad(ssd_a_log.reshape(1, 32), ((0, 7), (0, 96)))
    mod_all = mod_all.reshape(8, 16, ncol_ada)[0::2]
    mod_full = mod_all.transpose(1, 0, 2).reshape(16, 4 * ncol_ada)
    my_mod = lax.dynamic_slice(mod_full, (me, 0), (1, 6 * D)).reshape(6, D)
    sh1, sc1, g1, sh2, sc2, g2 = (my_mod[k:k + 1] for k in range(6))
    csh1, csc1 = mod_full[8:9, 0:D], mod_full[8:9, D:2 * D]

    zrow = jnp.zeros((1, D), F32)
    mods_lat = jnp.concatenate([1.0 + sc1, sh1, norm_mix, zrow, zrow, zrow, zrow, zrow], axis=0)
    mods_ctx = jnp.concatenate([1.0 + csc1, csh1, norm_mix, zrow, zrow, zrow, zrow, zrow], axis=0)
    outs = _inproj(xl, mods_lat, wi_main, wi_tail, t_total, 1024, 0, None, "inproj_lat",
                   comm=_comm_gather(shards[2:5]))
    wo_g, wg_g, wu_g = (own(g_, s_) for g_, s_ in zip(outs[3:], shards[2:5]))
    w_out_f = wo_g.reshape(2 * D, D)
    p_main, p_dt, u_all = _inproj(xc, mods_ctx, wi_main, wi_tail, t_total, TB, nb - 1, outs[:3], "inproj_ctx")

    o_f, hs_f, wd_g = _hgrn_fwd(p_main, lbraw8, 0, nb, comm=_comm_gather(shards[5:]))
    w_down_f = own(wd_g, shards[5]).reshape(DFFP, D)
    o_b, hs_b = _hgrn_fwd(p_main, lbraw8, 1, nb)
    xa, dsl, dts = _ssd_prep(p_main, p_dt, convp, dtb, nb)
    y_f, ss_f = _ssd_fwd(xa, dts, alog, 0, nb)
    y_b, ss_b = _ssd_fwd(xa, dts, alog, 1, nb)

    vec_mix = jnp.concatenate([jnp.tile(hgrn_norm, (1, NH)), jnp.repeat(ssd_d, SP, axis=1), ssd_norm, g1, 1.0 + sc2,
                               sh2, norm_ffn, zrow], axis=0)
    ymix, ylat, h1, u2 = _mix_out(o_f, o_b, p_main, y_f, y_b, xa, xl, vec_mix, w_out_f)
    gate, up, act = _ffn_up(u2, wg_g, wu_g)
    vec_loss = jnp.concatenate([g2, final_norm.reshape(1, D)] + [zrow] * 6, axis=0)
    dh2, dffn, acc_loss = _ffn_down_loss(act, w_down_f, h1, tgt, vec_loss)

    core_arr = jnp.reshape(ic, (1,)).astype(jnp.int32)
    chip_arr = jnp.reshape(chip, (1,)).astype(jnp.int32)
    every = (0, 4)

    def pair_sum(gs, got, tag):
        return list(_pair_sum(gs, list(got), core_arr, "grads_pair_sum_" + tag))

    vec_ffn = jnp.concatenate([g1, 1.0 + sc2, norm_ffn] + [zrow] * 5, axis=0)
    dgate, dup, du2 = _ffn_bwd(dffn, w_down_f, gate, up, wg_g, wu_g)
    dh1, dylat, acc_ffn = _ffn_norm_bwd(du2, h1, ylat, dh2, vec_ffn)
    gw_down = _dw(act, dffn, "dw_down").reshape(4, FSL, D)
    ga1 = [_dw(dgate, u2, "dw_gate").reshape(4, FSL, D), _dw(dup, u2, "dw_up").reshape(4, FSL, D)]
    res = _mix_bwd(dylat, o_f, o_b, p_main, y_f, y_b, xa, vec_mix, w_out_f, comm=_comm_pair(ga1))
    (do, dgr, dys, dzr, dxs_skip, acc_mix), pair_a1 = res[:6], pair_sum(ga1, res[6:], "a1")
    ga2 = [gw_down, _dw(ymix, dylat, "dw_out").reshape(4, D // 2, D)]

    res = _hgrn_bwd(p_main, lbraw8, hs_f, do, 0, nb, None,
                    comm=[_comm_exchange(pair_a1, [every] * 2), _comm_pair(ga2)])
    (dq0, dff, dv0, dlb_f), recv_a, pair_a2 = res[:4], list(res[4:6]), pair_sum(ga2, res[6:], "a2")
    res = _hgrn_bwd(p_main, lbraw8, hs_b, do, 1, nb, (dq0, dv0), comm=_comm_exchange(pair_a2, [every] * 2))
    (dq, dfb, dv, dlb_b), recv_a = res[:4], recv_a + list(res[4:])
    pair_a, dests_a = pair_a1 + pair_a2, [every] * 4
    gw_in = [_dw_in([dq, dff], u_all, "dw_in_0"), _dw_in([dfb, dv], u_all, "dw_in_1"),
             _dw_in([dgr, dzr], u_all, "dw_in_2")]

    res = _ssd_bwd(xa, dts, alog, ss_f, dys, 0, nb, None, comm=_comm_pair(gw_in))
    (dxa0, ddts0, da_f), pair_b, dests_b = res[:3], pair_sum(gw_in, res[3:], "b"), [(0, 1), (1, 2), (2, 3)]
    res = _ssd_bwd(xa, dts, alog, ss_b, dys, 1, nb, (dxa0, ddts0), comm=_comm_exchange(pair_b, dests_b))
    (dxa, ddts, da_b), recv_b = res[:3], list(res[3:])
    dxbc, ddt, acc_conv, acc_dtb = _ssd_prep_bwd(p_main, p_dt, convp, dtb, dsl, dxa, dxs_skip, ddts, nb)
    gw_in.append(_dw_in([dxbc], u_all, "dw_in_3"))
    gw_in_dt = _dw(ddt, u_all, "dw_in_dt")
    gc = [gw_in[3], jnp.concatenate([g_[:, 0:WTAIL, :] for g_ in gw_in[1:]] + [gw_in_dt[None]], axis=0)]

    segs = [dq, dff, dfb, dv, dgr, dzr, dxbc]
    bmods_lat = jnp.concatenate([1.0 + sc1, norm_mix] + [zrow] * 6, axis=0)
    bmods_ctx = jnp.concatenate([1.0 + csc1, norm_mix] + [zrow] * 6, axis=0)
    res = _du_prenorm_bwd(segs, ddt, wi_main, wi_tail, xc, bmods_ctx, None, n_lat, TB, "du_ctx", comm=_comm_pair(gc))
    acc_ctx, pair_c, dests_c = res[0], pair_sum(gc, res[1:], "c"), [(3, 4), every]
    res = _du_prenorm_bwd(segs, ddt, wi_main, wi_tail, xl, bmods_lat, dh1, 0, 512, "du_lat",
                          comm=_comm_exchange(pair_c, dests_c))
    (grad_x, acc_lat), recv_c = res[:2], list(res[2:])

    mine = _chip_sum(pair_b + pair_c + pair_a, recv_b + recv_c + recv_a, chip_arr, dests_b + dests_c + dests_a,
                     [0, 0, 0, 0, 1, 3, 4, 5, 2])
    dmod_lat = jnp.concatenate([acc_lat[0:2], acc_ffn[3:4], acc_ffn[0:2], acc_loss[0:1]], axis=0)
    misc = jnp.concatenate([(da_f + da_b)[0, :32], jnp.zeros((96,), F32), acc_dtb[0, :32], jnp.zeros((96,), F32),
                            jnp.sum(acc_loss[2]).reshape(1), jnp.zeros((D - 257,), F32)]).reshape(1, D)
    sv = jnp.concatenate([
        dmod_lat, acc_ctx[0:2], (acc_lat[2:3] + acc_ctx[2:3]), acc_ffn[2:3], acc_loss[1:2], acc_mix[2:3],
        acc_mix[0:1], acc_mix[1:2], dlb_f[0:1], dlb_b[0:1], acc_conv[0:6].reshape(12, D), misc,
        jnp.zeros((3, D), F32)], axis=0)
    res = _pair_swap(mine, sv)
    theirs, sv_all = res[:-1], res[-1].reshape(8, 32, D)
    whole = [jnp.concatenate([jnp.where(ic == 0, m_, t_), jnp.where(ic == 0, t_, m_)], axis=0)
             for m_, t_ in zip(mine, theirs)]
    g_w_in = lax.dynamic_slice(jnp.concatenate(whole[0:2], axis=0), (8 * chip, 0), (NSH, D))
    g_w_out = whole[2]
    g_w_gate = whole[3][:DFF // 4]
    g_w_up = whole[4][:DFF // 4]
    g_w_down = whole[5][:DFF // 4]
    ssum = _sum8(sv_all)
    dmod_rows = sv_all[:, 0:6].reshape(8, 6 * D)
    dmod_ctx_row = jnp.concatenate([ssum[6:8].reshape(1, 2 * D), jnp.zeros((1, 4 * D), F32)], axis=1)
    dmod_full = jnp.concatenate([dmod_rows, dmod_ctx_row, jnp.zeros((7, 6 * D), F32)], axis=0)
    grad_b_ada = jnp.sum(dmod_full, axis=0, keepdims=True)
    dmod_shard = lax.dynamic_slice(dmod_full, (0, chip * ncol_ada), (16, ncol_ada))
    g_w_ada, da_part = _ada_bwd(araw, dmod_shard, w_ada[0])

    big = {}
    for nm, w_, m_, v_, g_ in (("w_ada", w_ada, m_w_ada, v_w_ada, g_w_ada), ("w_in", w_in, m_w_in, v_w_in, g_w_in),
                               ("w_out", w_out, m_w_out, v_w_out, g_w_out),
                               ("w_gate", w_gate, m_w_gate, v_w_gate, g_w_gate),
                               ("w_up", w_up, m_w_up, v_w_up, g_w_up),
                               ("w_down", w_down, m_w_down, v_w_down, g_w_down)):
        if nm == "w_in":
            res = _adamw(tr(w_), tr(m_), tr(v_), g_, "adamw_" + nm, comm=_comm_allgather8(da_part))
            big[nm], da_all = tuple(tr(t) for t in (g_[None],) + tuple(res[:3])), res[3].reshape(8, 16, D)[0::2, 8]
        elif nm in ("w_gate", "w_up"):
            big[nm] = tuple(tr(t) for t in (g_[None],) + tuple(_adamw(tr(w_), tr(m_), tr(v_), g_, "adamw_" + nm)))
        else:
            big[nm] = (g_[None],) + tuple(_adamw(w_, m_, v_, g_, "adamw_" + nm))
    cc = c_ctx.reshape(1, D)
    grad_c_ctx = (jnp.sum(da_all, axis=0, keepdims=True) * _dsilu(cc)).reshape(D)

    grad_norm_mix, grad_norm_ffn, grad_final_norm = ssum[8:9], ssum[9:10], ssum[10].reshape(D)
    grad_ssd_norm = ssum[11:12]
    grad_hgrn_norm = jnp.sum(ssum[12].reshape(NH, HF), axis=0, keepdims=True)
    grad_ssd_d = jnp.sum(ssum[13].reshape(SHEADS, SP), axis=1).reshape(1, SHEADS)
    lb_full = _sig(lbraw_full[0:2] - lbraw_full[2:4])
    dr0 = ssum[14:16] * lb_full * (1.0 - lb_full)
    grad_lb_full = jnp.stack([dr0, -dr0], axis=0)
    grad_lb = lax.dynamic_slice(grad_lb_full, (0, 0, chip * 256), (2, 2, 256))
    grad_conv_w = lax.dynamic_slice(ssum[16:26].reshape(KCONV, 2048), (0, chip * 512), (KCONV, 512)).reshape(1, KCONV, 512)
    grad_conv_b = ssum[26:28].reshape(1, 2048)
    a_val = -jnp.exp(ssd_a_log)
    grad_a_log = ssum[28, 0:32].reshape(1, 2, SHEADS) * a_val
    grad_dt_bias = ssum[28, 128:160].reshape(1, 2, SHEADS)
    loss = ssum[28, 256]

    small_w = [c_ctx, b_ada, norm_mix, conv_w, conv_b, ssd_a_log, ssd_dt_bias, ssd_d, ssd_norm, hgrn_lb_raw,
               hgrn_norm, norm_ffn, final_norm]
    small_m = [m_c_ctx, m_b_ada, m_norm_mix, m_conv_w, m_conv_b, m_ssd_a_log, m_ssd_dt_bias, m_ssd_d, m_ssd_norm,
               m_hgrn_lb_raw, m_hgrn_norm, m_norm_ffn, m_final_norm]
    small_v = [v_c_ctx, v_b_ada, v_norm_mix, v_conv_w, v_conv_b, v_ssd_a_log, v_ssd_dt_bias, v_ssd_d, v_ssd_norm,
               v_hgrn_lb_raw, v_hgrn_norm, v_norm_ffn, v_final_norm]
    small_g = [grad_c_ctx, grad_b_ada, grad_norm_mix, grad_conv_w, grad_conv_b, grad_a_log, grad_dt_bias, grad_ssd_d,
               grad_ssd_norm, grad_lb, grad_hgrn_norm, grad_norm_ffn, grad_final_norm]
    nrows = [-(-a.size // D) for a in small_w]
    packs = lambda lst: jnp.concatenate([_rows(a, r) for a, r in zip(lst, nrows)]
                                        + [jnp.zeros((24 - sum(nrows), D), F32)], axis=0)
    sd, sm, svv = _adamw(packs(small_w), packs(small_m), packs(small_v), packs(small_g), "adamw_small")

    def unpack(p):
        out, r0 = [], 0
        for a, r in zip(small_w, nrows):
            out.append(p[r0:r0 + r].reshape(-1)[:a.size].reshape(a.shape))
            r0 += r
        return out

    sd, sm, svv = unpack(sd), unpack(sm), unpack(svv)

    order = ["c_ctx", "w_ada", "b_ada", "norm_mix", "w_in", "conv_w", "conv_b", "ssd_a_log", "ssd_dt_bias", "ssd_d",
             "ssd_norm", "hgrn_lb_raw", "hgrn_norm", "w_out", "norm_ffn", "w_gate", "w_up", "w_down", "final_norm"]
    small_names = ["c_ctx", "b_ada", "norm_mix", "conv_w", "conv_b", "ssd_a_log", "ssd_dt_bias", "ssd_d", "ssd_norm",
                   "hgrn_lb_raw", "hgrn_norm", "norm_ffn", "final_norm"]
    table = dict(big)
    for k, nm in enumerate(small_names):
        table[nm] = (small_g[k].reshape(small_w[k].shape), sd[k], sm[k], svv[k])
    grads = [table[nm][0] for nm in order]
    deltas = [table[nm][1] for nm in order]
    new_m = [table[nm][2] for nm in order]
    new_v = [table[nm][3] for nm in order]
    return (loss, grad_x[None], *grads, *deltas, *new_m, *new_v)
```

```python
import functools
import math

import jax
import jax.numpy as jnp
from jax import lax
from jax.experimental import pallas as pl
from jax.experimental.pallas import tpu as pltpu

F32 = jnp.float32
BF16 = jnp.bfloat16
MXU_DTYPE = jnp.bfloat16
_INTERPRET = False

D = 1024
NH, HF = 8, 128
HC = 64
SC = 128
SN = 128
SHEADS, SP = 16, 64
GRID_W = 64
KCONV = 5
DFF = 2816
FSL = 768
DFFP = 4 * FSL
NIN = 8224
TB = 256
EPS = 1e-6
LR, B1, B2, AEPS, WD, STEP = 0.001, 0.9, 0.999, 1e-08, 0.01, 10
MESH_ID = pl.DeviceIdType.MESH
NSH = NIN // 4
WSL = 2048
WTAIL = 128


def _pcall(body, *, name, out_shape, grid=(), in_specs=None, out_specs=None, scratch=(), sem=None,
           vmem_mb=None, aliases=None):
    params = {}
    if sem is not None:
        params["dimension_semantics"] = sem
    if vmem_mb is not None:
        params["vmem_limit_bytes"] = vmem_mb << 20
    kw = dict(name=name, out_shape=out_shape, scratch_shapes=list(scratch),
              input_output_aliases=aliases or {}, compiler_params=pltpu.CompilerParams(**params),
              interpret=_INTERPRET)
    if grid:
        kw["grid"] = grid
    if in_specs is not None:
        kw["in_specs"] = in_specs
    if out_specs is not None:
        kw["out_specs"] = out_specs
    return pl.pallas_call(body, **kw)


def _mx(a):
    return a.astype(MXU_DTYPE)


def _dg(a, b, ca, cb):
    return lax.dot_general(_mx(a), _mx(b), (((ca,), (cb,)), ((), ())), preferred_element_type=F32)


def _nn(a, b):
    return _dg(a, b, 1, 0)


def _nt(a, b):
    return _dg(a, b, 1, 1)


def _tn(a, b):
    return _dg(a, b, 0, 0)


def _dot01(m, x, ways=3):
    f = lambda t: lax.dot_general(m, t, (((1,), (0,)), ((), ())), preferred_element_type=F32)
    hi = x.astype(BF16)
    r1 = x - hi.astype(F32)
    mid = r1.astype(BF16)
    if ways == 2:
        return f(hi) + f(mid)
    lo = (r1 - mid.astype(F32)).astype(BF16)
    return f(hi) + f(mid) + f(lo)


def _tri(n, upper):
    r = lax.broadcasted_iota(jnp.int32, (n, n), 0)
    c = lax.broadcasted_iota(jnp.int32, (n, n), 1)
    return (c >= r) if upper else (c <= r)


def _b01(mask):
    return jnp.where(mask, 1.0, 0.0).astype(BF16)


def _sig(x):
    return jax.nn.sigmoid(x)


def _silu(x):
    return x * _sig(x)


def _dsilu(x):
    s = _sig(x)
    return s * (1.0 + x * (1.0 - s))


def _softplus(x):
    return jnp.maximum(x, 0.0) + jnp.log(1.0 + jnp.exp(-jnp.abs(x)))


def _rowsum(x):
    return jnp.sum(x, axis=1, keepdims=True)


def _colsum(x):
    return jnp.sum(x, axis=0, keepdims=True)


def _full(shape):
    return pl.BlockSpec(shape, lambda *_: (0,) * len(shape))


def _allgather8_phases(x_ref, out_ref, send_sems, recv_sems, local_sem):
    m_per = x_ref.shape[0]
    x, y, c = lax.axis_index("x"), lax.axis_index("y"), lax.axis_index("c")
    me, sibling = (x, y, c), (x, y, 1 - c)
    chips = [(1 - x, y), (x, 1 - y), (1 - x, 1 - y)]

    def rows(px, py, pc):
        return out_ref.at[pl.ds((4 * px + 2 * py + pc) * m_per, m_per), :]

    def copy(k, block, to, src=None):
        return pltpu.make_async_remote_copy(
            src_ref=rows(*block) if src is None else src, dst_ref=rows(*block),
            send_sem=send_sems.at[k], recv_sem=recv_sems.at[k], device_id=to, device_id_type=MESH_ID)

    mine = pltpu.make_async_copy(x_ref, rows(*me), local_sem)
    first = [copy(0, me, sibling, src=x_ref)]
    first += [copy(1 + j, me, (*chip, c), src=x_ref) for j, chip in enumerate(chips)]
    passed = [copy(4 + j, (*chip, c), sibling) for j, chip in enumerate(chips)]

    def start():
        mine.start()
        for cp in first:
            cp.start()

    def forward():
        for j, chip in enumerate(chips):
            copy(1 + j, (*chip, c), me).wait_recv()
            passed[j].start()

    def finish():
        copy(0, sibling, me).wait_recv()
        for j, chip in enumerate(chips):
            copy(4 + j, (*chip, 1 - c), me).wait_recv()
        for cp in first + passed:
            cp.wait_send()
        mine.wait()

    return start, forward, finish


def _allgather8_ops(x_ref, out_ref, send_sems, recv_sems, local_sem):
    for phase in _allgather8_phases(x_ref, out_ref, send_sems, recv_sems, local_sem):
        phase()


def _allgather8(v, name):
    m_per, n = v.shape
    return _pcall(
        functools.partial(_allgather8_ops), name=name, out_shape=jax.ShapeDtypeStruct((8 * m_per, n), v.dtype),
        in_specs=[pl.BlockSpec(memory_space=pltpu.VMEM)], out_specs=pl.BlockSpec(memory_space=pltpu.VMEM),
        scratch=list(_AG8_SEMS),
    )(v)


_AG8_SEMS = [pltpu.SemaphoreType.DMA((7,)), pltpu.SemaphoreType.DMA((7,)), pltpu.SemaphoreType.DMA]


def _prologue(pack, cc_row, w_ada, b_shard, shards):
    n = len(shards)
    ncol = w_ada.shape[1]

    def body(pack_ref, cc_ref, w_ref, b_ref, *refs):
        ins = refs[:n]
        gath_ref, araw_ref, mod_ref = refs[n:n + 3]
        outs = refs[n + 3:2 * n + 3]
        modsh, s1, r1, l1, s2, r2, l2, gs, gr = refs[2 * n + 3:]
        start, forward, finish = _gather_ops(ins, outs, gs, gr, relay=True)
        start()
        _allgather8_ops(pack_ref, gath_ref, s1, r1, l1)
        a = jnp.concatenate([gath_ref[8 * i:8 * i + 1, :] for i in range(8)] + [cc_ref[...], jnp.zeros((7, D), F32)],
                            axis=0)
        araw_ref[...] = a
        modsh[...] = _nn(_silu(a), w_ref[...]) + b_ref[...]
        _allgather8_ops(modsh, mod_ref, s2, r2, l2)
        forward()
        finish()

    vm = pl.BlockSpec(memory_space=pltpu.VMEM)
    anyspec = pl.BlockSpec(memory_space=pl.ANY)
    return _pcall(
        body, name="prologue",
        out_shape=(jax.ShapeDtypeStruct((64, D), F32), jax.ShapeDtypeStruct((16, D), F32),
                   jax.ShapeDtypeStruct((128, ncol), F32)) + _gather_out(shards),
        in_specs=[vm, vm, vm, vm] + [anyspec] * n, out_specs=(vm, vm, vm) + (anyspec,) * n,
        scratch=[pltpu.VMEM((16, ncol), F32)] + list(_AG8_SEMS) + list(_AG8_SEMS) + _gather_sems(n), vmem_mb=40,
    )(pack, cc_row, w_ada, b_shard, *shards)


def _gather_ops(ins, outs, send_sems, recv_sems, relay=False):
    n = len(ins)
    x, y, c = lax.axis_index("x"), lax.axis_index("y"), lax.axis_index("c")
    me, sibling = (x, y, c), (x, y, 1 - c)
    chips = [(1 - x, y), (x, 1 - y), (1 - x, 1 - y)]
    direct = 2 if relay else 3

    def part(a, px, py, pc, quarter=None):
        half = ins[a].shape[0] // 2
        if quarter is None:
            return outs[a].at[2 * px + py, pl.ds(pc * half, half), :]
        return outs[a].at[2 * px + py, pl.ds(pc * half + quarter * (half // 2), half // 2), :]

    def copy(a, k, block, to, src=None, quarter=None):
        return pltpu.make_async_remote_copy(
            src_ref=part(a, *block, quarter) if src is None else src, dst_ref=part(a, *block, quarter),
            send_sem=send_sems.at[8 * a + k], recv_sem=recv_sems.at[8 * a + k], device_id=to,
            device_id_type=MESH_ID)

    def first(a, j):
        half = ins[a].shape[0] // 2
        return copy(a, j, me, (*chips[j], c), src=ins[a].at[pl.ds(c * half, half), :])

    relayed = lambda a, q: copy(a, 6 + q, (*chips[q], c), (*chips[1 - q], c), quarter=q)

    def start():
        for a in range(n):
            for j in range(direct):
                first(a, j).start()

    def forward():
        for a in range(n):
            for j in range(direct):
                copy(a, j, (*chips[j], c), me).wait_recv()
                copy(a, 3 + j, (*chips[j], c), sibling).start()
                if relay:
                    relayed(a, j).start()
            if relay:
                for q in range(2):
                    copy(a, 6 + q, (*chips[2], c), me, quarter=q).wait_recv()
                copy(a, 5, (*chips[2], c), sibling).start()

    def finish():
        for a in range(n):
            for j, chip in enumerate(chips):
                copy(a, 3 + j, (*chip, 1 - c), me).wait_recv()
        for a in range(n):
            for j, chip in enumerate(chips):
                if j < direct:
                    first(a, j).wait_send()
                    if relay:
                        relayed(a, j).wait_send()
                copy(a, 3 + j, (*chip, c), sibling).wait_send()

    return start, forward, finish


def _gather_out(shards):
    return tuple(jax.ShapeDtypeStruct((4,) + s_.shape, s_.dtype) for s_ in shards)


def _gather_sems(n):
    return [pltpu.SemaphoreType.DMA((8 * n,)), pltpu.SemaphoreType.DMA((8 * n,))]


def _pair_ops(ins, outs, send_sems, recv_sems):
    x, y, c = lax.axis_index("x"), lax.axis_index("y"), lax.axis_index("c")
    cps = []
    for a in range(len(ins)):
        half = ins[a].shape[1] // 2
        cps.append(pltpu.make_async_remote_copy(
            src_ref=ins[a].at[:, pl.ds((1 - c) * half, half), :], dst_ref=outs[a], send_sem=send_sems.at[a],
            recv_sem=recv_sems.at[a], device_id=(x, y, 1 - c), device_id_type=MESH_ID))

    def start():
        for cp in cps:
            cp.start()

    def finish():
        for cp in cps:
            cp.wait()

    return start, finish


def _comm_pair(gs):
    n = len(gs)
    return (list(gs), tuple(jax.ShapeDtypeStruct((g.shape[0], g.shape[1] // 2, g.shape[2]), g.dtype) for g in gs),
            [pltpu.SemaphoreType.DMA((n,)), pltpu.SemaphoreType.DMA((n,))], _pair_ops)


def _exchange_ops(ins, outs, send_sems, recv_sems, dests):
    x, y, c = lax.axis_index("x"), lax.axis_index("y"), lax.axis_index("c")
    mine = 2 * x + y
    chips = [(1 - x, y), (x, 1 - y), (1 - x, 1 - y)]

    def each(fn):
        for a in range(len(ins)):
            lo, hi = dests[a]
            for j, (px, py) in enumerate(chips):
                q = 2 * px + py
                cp = pltpu.make_async_remote_copy(
                    src_ref=ins[a].at[jnp.clip(q - lo, 0, hi - lo - 1)], dst_ref=outs[a].at[j],
                    send_sem=send_sems.at[3 * a + j], recv_sem=recv_sems.at[3 * a + j], device_id=(px, py, c),
                    device_id_type=MESH_ID)
                fn(cp, (q >= lo) & (q < hi), (mine >= lo) & (mine < hi), (lo, hi) == (0, 4))

    def start():
        def go(cp, send_ok, recv_ok, always):
            if always:
                cp.start()
            else:
                pl.when(send_ok)(cp.start)
        each(go)

    def finish():
        def go(cp, send_ok, recv_ok, always):
            if always:
                cp.wait()
            else:
                pl.when(send_ok)(cp.wait_send)
                pl.when(recv_ok)(cp.wait_recv)
        each(go)

    return start, finish


def _comm_exchange(hs, dests):
    n = len(hs)
    return (list(hs), tuple(jax.ShapeDtypeStruct((3,) + h.shape[1:], h.dtype) for h in hs),
            [pltpu.SemaphoreType.DMA((3 * n,)), pltpu.SemaphoreType.DMA((3 * n,))],
            lambda i, o, s, r: _exchange_ops(i, o, s, r, dests))


def _comm_gather(shards, relay=False):
    return (list(shards), _gather_out(shards), _gather_sems(len(shards)),
            lambda i, o, s, r: _gather_ops(i, o, s, r, relay))


def _carry(call, comm, steps):
    if comm is None:
        return call
    if isinstance(comm, list):
        for one in comm:
            call = _carry(call, one, steps)
        return call
    arrays, out_shape, sems, make = comm
    n, n_in, n_out = len(arrays), len(call["args"]), len(call["out_shape"])
    body = call["body"]

    def wrapped(*refs):
        base_in, cin = refs[:n_in], refs[n_in:n_in + n]
        rest = refs[n_in + n:]
        base_out, cout, scr = rest[:n_out], rest[n_out:n_out + n], rest[n_out + n:]
        ops = make(cin, cout, scr[-2], scr[-1])
        when = steps()
        pl.when(when[0])(ops[0])
        if len(ops) == 3 and len(when) == 3:
            pl.when(when[2])(ops[1])
        body(*base_in, *base_out, *scr[:-2])
        if len(ops) == 3 and len(when) == 2:
            pl.when(when[1])(ops[1])
        pl.when(when[1])(ops[-1])

    anyspec = pl.BlockSpec(memory_space=pl.ANY)
    return dict(call, body=wrapped, args=list(call["args"]) + arrays,
                in_specs=list(call["in_specs"]) + [anyspec] * n,
                out_shape=tuple(call["out_shape"]) + tuple(out_shape),
                out_specs=tuple(call["out_specs"]) + (anyspec,) * n,
                scratch=list(call["scratch"]) + sems)


def _run(call):
    args = call.pop("args")
    body = call.pop("body")
    return _pcall(body, **call)(*args)


def _pair_swap(rs, sv):
    n = len(rs)

    def body(sv_ref, *refs):
        ins, outs, got_ref = refs[:n], refs[n:2 * n], refs[2 * n]
        send_sems, recv_sems, s1, r1, l1 = refs[2 * n + 1:]
        x, y, c = lax.axis_index("x"), lax.axis_index("y"), lax.axis_index("c")
        cps = [pltpu.make_async_remote_copy(
            src_ref=ins[a], dst_ref=outs[a], send_sem=send_sems.at[a], recv_sem=recv_sems.at[a],
            device_id=(x, y, 1 - c), device_id_type=MESH_ID) for a in range(n)]
        for cp in cps:
            cp.start()
        _allgather8_ops(sv_ref, got_ref, s1, r1, l1)
        for cp in cps:
            cp.wait()

    vm, anyspec = pl.BlockSpec(memory_space=pltpu.VMEM), pl.BlockSpec(memory_space=pl.ANY)
    return _pcall(
        body, name="grads_pair_swap",
        out_shape=tuple(jax.ShapeDtypeStruct(r.shape, r.dtype) for r in rs)
        + (jax.ShapeDtypeStruct((8 * sv.shape[0], sv.shape[1]), sv.dtype),),
        in_specs=[vm] + [anyspec] * n, out_specs=(anyspec,) * n + (vm,),
        scratch=[pltpu.SemaphoreType.DMA((n,)), pltpu.SemaphoreType.DMA((n,))] + list(_AG8_SEMS),
    )(sv, *rs)


SUM_STEPS = 4


def _pair_sum(gs, recvs, core, name):
    n = len(gs)

    def body(c_ref, *refs):
        for a in range(n):
            refs[2 * n + a][...] = (refs[a][...].astype(F32) + refs[n + a][...].astype(F32)).astype(refs[2 * n + a].dtype)

    blk = lambda g: (g.shape[0], g.shape[1] // (2 * SUM_STEPS), g.shape[2])
    return pl.pallas_call(
        body, name=name,
        out_shape=tuple(jax.ShapeDtypeStruct((g.shape[0], g.shape[1] // 2, g.shape[2]), g.dtype) for g in gs),
        grid_spec=pltpu.PrefetchScalarGridSpec(
            num_scalar_prefetch=1, grid=(SUM_STEPS,),
            in_specs=[pl.BlockSpec(blk(g), lambda i, cr: (0, cr[0] * SUM_STEPS + i, 0)) for g in gs]
            + [pl.BlockSpec(blk(g), lambda i, cr: (0, i, 0)) for g in gs],
            out_specs=tuple(pl.BlockSpec(blk(g), lambda i, cr: (0, i, 0)) for g in gs)),
        compiler_params=pltpu.CompilerParams(vmem_limit_bytes=40 << 20), interpret=_INTERPRET,
    )(core, *gs, *recvs)


def _chip_sum(hs, recvs, chip, dests, slots):
    n = len(hs)
    nout = max(slots) + 1
    first = [slots.index(o) for o in range(nout)]
    every = lambda d_: d_ == (0, 4)

    def own(d_):
        if every(d_):
            return lambda i, kr: (kr[0], i, 0)
        return lambda i, kr: (0, jnp.where(kr[0] == d_[0], i, 0), 0)

    def got(d_):
        if every(d_):
            return lambda i, kr: (0, i, 0)
        return lambda i, kr: (0, jnp.where(kr[0] == d_[0], i, 0), 0)

    def body(k_ref, *refs):
        for a in range(n):
            def emit(a=a):
                acc = refs[a][0].astype(F32)
                for j in range(3):
                    acc = acc + refs[n + a][j].astype(F32)
                refs[2 * n + slots[a]][...] = acc
            if every(dests[a]):
                emit()
            else:
                pl.when(k_ref[0] == dests[a][0])(emit)

    rb = lambda h: h.shape[1] // SUM_STEPS
    return pl.pallas_call(
        body, name="grads_chip_sum",
        out_shape=tuple(jax.ShapeDtypeStruct(hs[a].shape[1:], F32) for a in first),
        grid_spec=pltpu.PrefetchScalarGridSpec(
            num_scalar_prefetch=1, grid=(SUM_STEPS,),
            in_specs=[pl.BlockSpec((1, rb(h), h.shape[2]), own(d_)) for h, d_ in zip(hs, dests)]
            + [pl.BlockSpec((3, rb(h), h.shape[2]), got(d_)) for h, d_ in zip(hs, dests)],
            out_specs=tuple(pl.BlockSpec((rb(hs[a]), hs[a].shape[2]), lambda i, kr: (i, 0)) for a in first)),
        compiler_params=pltpu.CompilerParams(vmem_limit_bytes=40 << 20), interpret=_INTERPRET,
    )(chip, *hs, *recvs)


def _ada_bwd(araw, dmod, w):
    nblk = w.shape[1] // 512

    def body(a_ref, d_ref, w_ref, gw_ref, da_ref):
        j = pl.program_id(0)
        gw_ref[...] = _tn(_silu(a_ref[...]), d_ref[...])
        part = _nt(d_ref[...], w_ref[...])

        @pl.when(j == 0)
        def _():
            da_ref[...] = part

        @pl.when(j > 0)
        def _():
            da_ref[...] += part

    return _pcall(
        body, name="ada_bwd",
        out_shape=(jax.ShapeDtypeStruct(w.shape, F32), jax.ShapeDtypeStruct((16, D), F32)), grid=(nblk,),
        in_specs=[_full((16, D)), pl.BlockSpec((16, 512), lambda j: (0, j)), pl.BlockSpec((D, 512), lambda j: (0, j))],
        out_specs=(pl.BlockSpec((D, 512), lambda j: (0, j)), _full((16, D))), sem=("arbitrary",),
    )(araw, dmod, w)


def _w_specs():
    return [pl.BlockSpec((None, D, D), lambda j, i: (j // 2, j % 2, 0)),
            pl.BlockSpec((None, WTAIL, D), lambda j, i: (jnp.maximum(j // 2 - 1, 0), 0, 0)),
            pl.BlockSpec((None, WTAIL, D), lambda j, i: (3, 0, 0))]


def _inproj(xin, mods, wi_main, wi_tail, t_total, tb, blk_off, prev, name, comm=None):
    n = xin.shape[0]
    nt = n // tb
    ncol = 8

    def body(x_ref, mod_ref, w_ref, wb_ref, wdt_ref, *rest):
        p_ref, pdt_ref, u_ref, uscr = rest[-4:]
        j, i = pl.program_id(0), pl.program_id(1)
        rows = pl.ds(pl.multiple_of(i * tb, tb), tb)

        @pl.when(j == 0)
        def _():
            xv = x_ref[...]
            r = lax.rsqrt(jnp.mean(xv * xv, axis=1, keepdims=True) + EPS)
            u = (xv * r * mod_ref[2:3, :]) * mod_ref[0:1, :] + mod_ref[1:2, :]
            ub = u.astype(MXU_DTYPE)
            uscr[rows, :] = ub
            u_ref[...] = ub
            pdt_ref[...] = _nt(ub, wdt_ref[...])

        ub = uscr[rows, :]
        pv = _nt(ub, w_ref[...])

        @pl.when((j % 2 == 1) | (j == 0))
        def _():
            p_ref[...] = pv.astype(p_ref.dtype)

        @pl.when((j % 2 == 0) & (j > 0))
        def _():
            head = pv[:, 0:WTAIL] + _nt(ub, wb_ref[...])
            p_ref[...] = jnp.concatenate([head, pv[:, WTAIL:]], axis=1).astype(p_ref.dtype)

    once = lambda j, i: (jnp.where(j == 0, i, nt - 1) + blk_off, 0)
    in_specs = [pl.BlockSpec((tb, D), lambda j, i: (jnp.where(j == 0, i, nt - 1), 0)), _full((8, D))] + _w_specs()
    args = [xin, mods, wi_main, wi_tail, wi_tail]
    aliases = None
    if prev is not None:
        in_specs += [pl.BlockSpec(memory_space=pl.ANY)] * 3
        args += list(prev)
        aliases = {5: 0, 6: 1, 7: 2}
    call = dict(
        body=body, args=args, name=name,
        out_shape=(jax.ShapeDtypeStruct((t_total, ncol * D), MXU_DTYPE), jax.ShapeDtypeStruct((t_total, 128), F32),
                   jax.ShapeDtypeStruct((t_total, D), MXU_DTYPE)),
        grid=(ncol, nt), in_specs=in_specs,
        out_specs=(pl.BlockSpec((tb, D), lambda j, i: (i + blk_off, j)), pl.BlockSpec((tb, 128), once),
                   pl.BlockSpec((tb, D), once)),
        scratch=[pltpu.VMEM((n, D), MXU_DTYPE)], sem=("arbitrary", "arbitrary"), vmem_mb=48, aliases=aliases)
    steps = lambda: ((pl.program_id(0) == 0) & (pl.program_id(1) == 0),
                     (pl.program_id(0) == ncol - 1) & (pl.program_id(1) == nt - 1),
                     (pl.program_id(0) == ncol - 1) & (pl.program_id(1) == 0))
    return _run(_carry(call, comm, steps))


def _blk(s, nb, rev):
    return jnp.where(s == 0, nb - 1, (nb - 1 - s) if rev else (s - 1))


def _hgrn_gate(fr, lbraw_ref, d):
    lb = _sig(lbraw_ref[d:d + 1, :] - lbraw_ref[2 + d:3 + d, :])
    sg = _sig(fr)
    return lb, sg, lb + (1.0 - lb) * sg


def _hgrn_fwd(p_main, lbraw, d, nb, comm=None):
    t_total = p_main.shape[0]
    rev = d == 1
    nch = TB // HC
    scale = HF ** -0.5

    def body(q_ref, f_ref, v_ref, lb_ref, o_ref, sp_ref, st):
        s = pl.program_id(0)

        @pl.when(s == 0)
        def _():
            st[...] = jnp.zeros_like(st)

        mb = _tri(HC, rev)
        m01 = _b01(mb)
        order = list(reversed(range(nch)) if rev else range(nch))
        hs_ = [slice(h * HF, (h + 1) * HF) for h in range(NH)]
        pre = {}
        for c in order:
            rows = slice(c * HC, (c + 1) * HC)
            _, _, f = _hgrn_gate(f_ref[rows, :].astype(F32), lb_ref, d)
            k = 1.0 - f
            cum = _dot01(m01, jnp.log(f))
            tot = cum[0:1, :] if rev else cum[HC - 1:HC, :]
            qd = _silu(q_ref[rows, :].astype(F32)) * scale * jnp.exp(cum)
            ki = k * jnp.exp(-cum)
            etot = jnp.exp(tot)
            pre[c] = (_mx(qd), _mx(ki), _mx(ki * etot), _mx(v_ref[rows, :]), etot)
        scs = {c: [_nt(pre[c][0][:, cs], pre[c][1][:, cs]) for cs in hs_] for c in order}
        upd = {c: [_tn(pre[c][3][:, cs], pre[c][2][:, cs]) for cs in hs_] for c in order}
        intra = {c: [_nn(jnp.where(mb, scs[c][h], 0.0), pre[c][3][:, cs]) for h, cs in enumerate(hs_)] for c in order}
        for c in order:
            rows = slice(c * HC, (c + 1) * HC)
            qdb, etot = pre[c][0], pre[c][4]
            for h, cs in enumerate(hs_):
                sth = st[h]
                stb = sth.astype(sp_ref.dtype)
                sp_ref[c, h] = stb
                o_ref[rows, cs] = (intra[c][h] + _nt(qdb[:, cs], stb)).astype(o_ref.dtype)
                st[h] = sth * etot[:, cs] + upd[c][h]

    col = lambda j: (lambda s: (_blk(s, nb, rev), j))
    call = dict(
        body=body, args=[p_main, p_main, p_main, lbraw], name=f"hgrn_fwd_{d}",
        out_shape=(jax.ShapeDtypeStruct((t_total, D), MXU_DTYPE),
                   jax.ShapeDtypeStruct((nch * nb, NH, HF, HF), MXU_DTYPE)),
        grid=(nb,),
        in_specs=[pl.BlockSpec((TB, D), col(0)), pl.BlockSpec((TB, D), col(1 + d)), pl.BlockSpec((TB, D), col(3)),
                  _full((8, D))],
        out_specs=(pl.BlockSpec((TB, D), col(0)),
                   pl.BlockSpec((nch, NH, HF, HF), lambda s: (_blk(s, nb, rev), 0, 0, 0))),
        scratch=[pltpu.VMEM((NH, HF, HF), F32)], sem=("arbitrary",), vmem_mb=40)
    return _run(_carry(call, comm, lambda: (pl.program_id(0) == 0, pl.program_id(0) == nb - 1,
                                            pl.program_id(0) == nb - 4)))


def _hgrn_bwd(p_main, lbraw, sprev, do, d, nb, prev, comm=None):
    t_total = p_main.shape[0]
    rev = d == 1
    nch = TB // HC
    scale = HF ** -0.5
    last = prev is not None
    odt = MXU_DTYPE if last else F32

    def body(q_ref, f_ref, v_ref, lb_ref, sp_ref, do_ref, *rest):
        if last:
            dqp_ref, dvp_ref = rest[:2]
            rest = rest[2:]
        dq_ref, df_ref, dv_ref, dlb_ref, dst = rest
        sp_id = pl.program_id(0)
        is_ctx = sp_id == nb - 1

        @pl.when(sp_id == 0)
        def _():
            dst[...] = jnp.zeros_like(dst)
            dlb_ref[...] = jnp.zeros_like(dlb_ref)

        mb = _tri(HC, rev)
        mbt = _tri(HC, not rev)
        m01 = _b01(mb)
        mt01 = _b01(mbt)
        order = list(range(nch) if rev else reversed(range(nch)))
        hs_ = [slice(h * HF, (h + 1) * HF) for h in range(NH)]
        pre = {}
        for c in order:
            rows = slice(c * HC, (c + 1) * HC)
            lb, sg, f = _hgrn_gate(f_ref[rows, :].astype(F32), lb_ref, d)
            k = 1.0 - f
            cum = _dot01(m01, jnp.log(f))
            tot = cum[0:1, :] if rev else cum[HC - 1:HC, :]
            e = jnp.exp(cum)
            ei = jnp.exp(-cum)
            etot = jnp.exp(tot)
            ee = ei * etot
            qraw = q_ref[rows, :].astype(F32)
            sq = _sig(qraw)
            qd = qraw * sq * scale * e
            ki = k * ei
            ke = k * ee
            dov = jnp.where(is_ctx, 0.0, do_ref[rows, :].astype(F32))
            pre[c] = dict(lb=lb, sg=sg, f=f, e=e, ei=ei, ee=ee, etot=etot, qd=qd, ki=ki, ke=ke,
                          dsq=sq * (1.0 + qraw * (1.0 - sq)),
                          qdb=_mx(qd), kib=_mx(ki), keb=_mx(ke), vb=_mx(v_ref[rows, :]), dob=_mx(dov))
        units = [(c, h) for c in order for h in range(NH)]
        col = lambda u, key: pre[u[0]][key][:, hs_[u[1]]]
        pt = {u: jnp.where(mbt, _nt(col(u, "kib"), col(u, "qdb")), 0.0) for u in units}
        dp = {u: jnp.where(mb, _nt(col(u, "dob"), col(u, "vb")), 0.0) for u in units}
        dpt = {u: jnp.where(mbt, _nt(col(u, "vb"), col(u, "dob")), 0.0) for u in units}
        dv_i = {u: _nn(pt[u], col(u, "dob")) for u in units}
        dqd_ = {u: _nn(dp[u], col(u, "kib")) + _nn(col(u, "dob"), sp_ref[u[0], u[1]]) for u in units}
        dki_ = {u: _nn(dpt[u], col(u, "qdb")) for u in units}
        dsl = {u: _tn(col(u, "dob"), col(u, "qdb")) for u in units}
        for c in order:
            rows = slice(c * HC, (c + 1) * HC)
            p = pre[c]
            dv_l, dke_l, dtot_l = [], [], []
            for h, cs in enumerate(hs_):
                dso = dst[h]
                dsob = _mx(dso)
                dv_l.append(dv_i[(c, h)] + _nt(p["keb"][:, cs], dsob))
                dke_l.append(_nn(p["vb"][:, cs], dsob))
                dtot_l.append(_colsum(dso * sp_ref[c, h].astype(F32)) * p["etot"][:, cs])
                dst[h] = dso * p["etot"][:, cs] + dsl[(c, h)]
            lb, sg, f, e, ei, ee, qd, ki, ke = (p[n_] for n_ in ("lb", "sg", "f", "e", "ei", "ee", "qd", "ki", "ke"))
            dqd = jnp.concatenate([dqd_[(c, h)] for h in range(NH)], axis=1)
            dki = jnp.concatenate([dki_[(c, h)] for h in range(NH)], axis=1)
            dke = jnp.concatenate(dke_l, axis=1)
            dcum = dqd * qd - dki * ki - dke * ke
            dtot = jnp.concatenate(dtot_l, axis=1) + _colsum(dke * ke)
            dk = dki * ei + dke * ee
            dlf = _dot01(mt01, dcum, ways=2) + dtot
            df = dlf / f - dk
            dlb_ref[0:1, :] += _colsum(df * (1.0 - sg))
            dfr = df * (1.0 - lb) * sg * (1.0 - sg)
            dq = dqd * e * scale * p["dsq"]
            dv = jnp.concatenate(dv_l, axis=1)
            if last:
                dq = dq + dqp_ref[rows, :]
                dv = dv + dvp_ref[rows, :]
            dq_ref[rows, :] = dq.astype(odt)
            dv_ref[rows, :] = dv.astype(odt)
            df_ref[rows, :] = dfr.astype(MXU_DTYPE)

    blk = lambda s: _blk(nb - 1 - s, nb, rev)
    col = lambda j: (lambda s: (blk(s), j))
    in_specs = [pl.BlockSpec((TB, D), col(0)), pl.BlockSpec((TB, D), col(1 + d)), pl.BlockSpec((TB, D), col(3)),
                _full((8, D)), pl.BlockSpec((nch, NH, HF, HF), lambda s: (blk(s), 0, 0, 0)),
                pl.BlockSpec((TB, D), lambda s: (jnp.minimum(blk(s), nb - 2), 0))]
    args = [p_main, p_main, p_main, lbraw, sprev, do]
    if last:
        in_specs += [pl.BlockSpec((TB, D), col(0))] * 2
        args += list(prev)
    call = dict(
        body=body, args=args, name=f"hgrn_bwd_{d}",
        out_shape=(jax.ShapeDtypeStruct((t_total, D), odt), jax.ShapeDtypeStruct((t_total, D), MXU_DTYPE),
                   jax.ShapeDtypeStruct((t_total, D), odt), jax.ShapeDtypeStruct((8, D), F32)),
        grid=(nb,), in_specs=in_specs,
        out_specs=(pl.BlockSpec((TB, D), col(0)), pl.BlockSpec((TB, D), col(0)), pl.BlockSpec((TB, D), col(0)),
                   _full((8, D))),
        scratch=[pltpu.VMEM((NH, HF, HF), F32)], sem=("arbitrary",), vmem_mb=48)
    return _run(_carry(call, comm, lambda: (pl.program_id(0) == 0, pl.program_id(0) == nb - 1)))


def _conv_masks(tb, is_ctx):
    seg = jnp.where(is_ctx, tb, GRID_W)
    pos = lax.broadcasted_iota(jnp.int32, (tb, 1), 0) & (seg - 1)
    return pos, seg


def _shift_rows(x, dshift, pos, seg):
    if dshift == 0:
        return x
    n = x.shape[0]
    rolled = pltpu.roll(x, (-dshift) % n, 0)
    ok = (pos + dshift >= 0) & (pos + dshift < seg)
    return jnp.where(ok, rolled, 0.0)


def _ssd_prep(p_main, p_dt, convp, dtb, nb):
    t_total = p_main.shape[0]

    def body(x_ref, dt_ref, cw_ref, dtb_ref, xa_ref, ds_ref, dts_ref):
        is_ctx = pl.program_id(0) == nb - 1
        pos, seg = _conv_masks(TB, is_ctx)
        xv = x_ref[...].astype(F32)
        acc = cw_ref[5:6, :] + cw_ref[2:3, :] * xv
        for kk in (0, 1, 3, 4):
            acc = acc + cw_ref[kk:kk + 1, :] * _shift_rows(xv, kk - 2, pos, seg)
        sg = _sig(acc)
        xa_ref[...] = (acc * sg).astype(xa_ref.dtype)
        ds_ref[...] = (sg * (1.0 + acc * (1.0 - sg))).astype(ds_ref.dtype)
        dts_ref[...] = _softplus(dt_ref[...] + dtb_ref[0:1, :])

    wide = pl.BlockSpec((TB, 2048), lambda i: (i, 0))
    return _pcall(
        body, name="ssd_prep",
        out_shape=(jax.ShapeDtypeStruct((t_total, 2048), MXU_DTYPE), jax.ShapeDtypeStruct((t_total, 2048), MXU_DTYPE),
                   jax.ShapeDtypeStruct((t_total, 128), F32)),
        grid=(nb,),
        in_specs=[pl.BlockSpec((TB, 2048), lambda i: (i, 3)), pl.BlockSpec((TB, 128), lambda i: (i, 0)),
                  _full((8, 2048)), _full((8, 128))],
        out_specs=(wide, wide, pl.BlockSpec((TB, 128), lambda i: (i, 0))),
        sem=("parallel",), vmem_mb=32,
    )(p_main, p_dt, convp, dtb)


def _ssd_prep_bwd(p_main, p_dt, convp, dtb, dsl, dxa, dxs_skip, ddts, nb):
    t_total = p_main.shape[0]

    def body(x_ref, dt_ref, cw_ref, dtb_ref, ds_ref, dxa_ref, dsk_ref, ddts_ref, dx_ref, ddt_ref, dcw_ref, ddtb_ref):
        i = pl.program_id(0)
        is_ctx = i == nb - 1

        @pl.when(i == 0)
        def _():
            dcw_ref[...] = jnp.zeros_like(dcw_ref)
            ddtb_ref[...] = jnp.zeros_like(ddtb_ref)

        pos, seg = _conv_masks(TB, is_ctx)
        xv = x_ref[...].astype(F32)
        dact = dxa_ref[...]
        dact = jnp.concatenate([dact[:, :D] + jnp.where(is_ctx, 0.0, dsk_ref[...].astype(F32)), dact[:, D:]], axis=1)
        dpre = dact * ds_ref[...].astype(F32)
        dxv = cw_ref[2:3, :] * dpre
        dcw_ref[2:3, :] += _colsum(xv * dpre)
        for kk in (0, 1, 3, 4):
            sdp = _shift_rows(dpre, 2 - kk, pos, seg)
            dxv = dxv + cw_ref[kk:kk + 1, :] * sdp
            dcw_ref[kk:kk + 1, :] += _colsum(xv * sdp)
        dx_ref[...] = dxv.astype(dx_ref.dtype)
        dcw_ref[5:6, :] += _colsum(dpre)
        draw = ddts_ref[...] * _sig(dt_ref[...] + dtb_ref[0:1, :])
        ddt_ref[...] = draw.astype(ddt_ref.dtype)
        ddtb_ref[0:1, :] += _colsum(draw)

    return _pcall(
        body, name="ssd_prep_bwd",
        out_shape=(jax.ShapeDtypeStruct((t_total, 2048), MXU_DTYPE), jax.ShapeDtypeStruct((t_total, 128), MXU_DTYPE),
                   jax.ShapeDtypeStruct((8, 2048), F32), jax.ShapeDtypeStruct((8, 128), F32)),
        grid=(nb,),
        in_specs=[pl.BlockSpec((TB, 2048), lambda i: (i, 3)), pl.BlockSpec((TB, 128), lambda i: (i, 0)),
                  _full((8, 2048)), _full((8, 128)), pl.BlockSpec((TB, 2048), lambda i: (i, 0)),
                  pl.BlockSpec((TB, 2048), lambda i: (i, 0)),
                  pl.BlockSpec((TB, D), lambda i: (jnp.minimum(i, nb - 2), 0)),
                  pl.BlockSpec((TB, 128), lambda i: (i, 0))],
        out_specs=(pl.BlockSpec((TB, 2048), lambda i: (i, 0)), pl.BlockSpec((TB, 128), lambda i: (i, 0)),
                   _full((8, 2048)), _full((8, 128))),
        sem=("arbitrary",), vmem_mb=40,
    )(p_main, p_dt, convp, dtb, dsl, dxa, dxs_skip, ddts)


def _dot2(x, m01):
    hi = x.astype(BF16)
    lo = (x - hi.astype(F32)).astype(BF16)
    f = lambda t: lax.dot_general(t, m01, (((1,), (0,)), ((), ())), preferred_element_type=F32)
    return f(hi) + f(lo)


def _head_lanes(c0, c1):
    p = lax.broadcasted_iota(jnp.int32, (128, 128), 0)
    l = lax.broadcasted_iota(jnp.int32, (128, 128), 1)
    return _b01(((l == c0) & (p < SP)) | ((l == c1) & (p >= SP)))


def _one_lane(col):
    return _b01(lax.broadcasted_iota(jnp.int32, (128, 128), 1) == col)


def _lane_pick(x, lane, col):
    return _rowsum(jnp.where(lane == col, x, 0.0))


def _ssd_chunk_common(dts, alog_ref, m01, rev):
    lane = lax.broadcasted_iota(jnp.int32, (1, 128), 1)
    arow = -jnp.exp(alog_ref[0:1, :])
    cum = _dot01(m01, dts * arow)
    tot = cum[0:1, :] if rev else cum[SC - 1:SC, :]
    return lane, arow, cum, cum.T, tot


def _ssd_fwd(xa, dts, alog, d, nb):
    t_total = xa.shape[0]
    rev = d == 1
    nch = TB // SC
    npair = SHEADS // 2

    def body(xa_ref, dts_ref, alog_ref, y_ref, sp_ref, st):
        s = pl.program_id(0)

        @pl.when(s == 0)
        def _():
            st[...] = jnp.zeros_like(st)

        mb = _tri(SC, rev)
        m01 = _b01(mb)
        lo = lax.broadcasted_iota(jnp.int32, (1, 128), 1) < SP
        rlo = lax.broadcasted_iota(jnp.int32, (128, 1), 0) < SP
        order = list(reversed(range(nch)) if rev else range(nch))
        pre = {}
        for c in order:
            rows = slice(c * SC, (c + 1) * SC)
            dts_c = dts_ref[rows, :]
            lane, arow, cum, cumt, tot = _ssd_chunk_common(dts_c, alog_ref, m01, rev)
            bgs = [_mx(xa_ref[rows, D + g * SN:D + (g + 1) * SN]) for g in range(4)]
            cgs = [_mx(xa_ref[rows, D + 512 + g * SN:D + 512 + (g + 1) * SN]) for g in range(4)]
            pairs = []
            for pr in range(npair):
                xs = xa_ref[rows, pr * 128:(pr + 1) * 128].astype(F32)
                cols = [16 * d + 2 * pr, 16 * d + 2 * pr + 1]
                cum_c = [_lane_pick(cum, lane, q) for q in cols]
                dt_c = [_lane_pick(dts_c, lane, q) for q in cols]
                tot_c = [_lane_pick(tot, lane, q) for q in cols]
                dtx = xs * jnp.where(lo, dt_c[0], dt_c[1])
                e1_pair = jnp.where(lo, jnp.exp(cum_c[0]), jnp.exp(cum_c[1]))
                e2_pair = jnp.where(lo, jnp.exp(tot_c[0] - cum_c[0]), jnp.exp(tot_c[1] - cum_c[1]))
                etot_col = jnp.where(rlo, jnp.exp(tot_c[0]), jnp.exp(tot_c[1]))
                decs = [jnp.where(mb, jnp.exp(cum_c[q] - cumt[cols[q]:cols[q] + 1, :]), 0.0) for q in range(2)]
                dtxq = [_mx(jnp.where(lo if q == 0 else ~lo, dtx, 0.0)) for q in range(2)]
                pairs.append(dict(e1=e1_pair, etot=etot_col, decs=decs, dtxq=dtxq, xe=_mx(dtx * e2_pair)))
            pre[c] = (bgs, cgs, pairs)
        gm = {(c, g): _nt(pre[c][1][g], pre[c][0][g]) for c in order for g in range(4)}
        upd = {(c, pr): _tn(pre[c][2][pr]["xe"], pre[c][0][pr // 2]) for c in order for pr in range(npair)}
        intra = {(c, pr): sum(_nn(gm[(c, pr // 2)] * pre[c][2][pr]["decs"][q], pre[c][2][pr]["dtxq"][q]) for q in range(2))
                 for c in order for pr in range(npair)}
        for c in order:
            rows = slice(c * SC, (c + 1) * SC)
            bgs, cgs, pairs = pre[c]
            for pr in range(npair):
                stp = st[pr]
                stb = stp.astype(sp_ref.dtype)
                sp_ref[c, pr] = stb
                y_ref[rows, pr * 128:(pr + 1) * 128] = (
                    intra[(c, pr)] + pairs[pr]["e1"] * _nt(cgs[pr // 2], stb)).astype(y_ref.dtype)
                st[pr] = stp * pairs[pr]["etot"] + upd[(c, pr)]

    blk = lambda s: _blk(s, nb, rev)
    return _pcall(
        body, name=f"ssd_fwd_{d}",
        out_shape=(jax.ShapeDtypeStruct((t_total, D), MXU_DTYPE),
                   jax.ShapeDtypeStruct((nch * nb, npair, 128, SN), MXU_DTYPE)),
        grid=(nb,),
        in_specs=[pl.BlockSpec((TB, 2048), lambda s: (blk(s), 0)), pl.BlockSpec((TB, 128), lambda s: (blk(s), 0)),
                  _full((8, 128))],
        out_specs=(pl.BlockSpec((TB, D), lambda s: (blk(s), 0)),
                   pl.BlockSpec((nch, npair, 128, SN), lambda s: (blk(s), 0, 0, 0))),
        scratch=[pltpu.VMEM((npair, 128, SN), F32)], sem=("arbitrary",), vmem_mb=40,
    )(xa, dts, alog)


def _ssd_bwd(xa, dts, alog, sprev, dy, d, nb, prev, comm=None):
    t_total = xa.shape[0]
    rev = d == 1
    nch = TB // SC
    npair = SHEADS // 2
    last = prev is not None

    def body(xa_ref, dts_ref, alog_ref, sp_ref, dy_ref, *rest):
        if last:
            dxp_ref, ddp_ref = rest[:2]
            rest = rest[2:]
        dxa_ref, ddts_ref, da_ref, dst, zc_scr = rest
        sp_id = pl.program_id(0)
        is_ctx = sp_id == nb - 1

        @pl.when(sp_id == 0)
        def _():
            dst[...] = jnp.zeros_like(dst)
            da_ref[...] = jnp.zeros_like(da_ref)
            zc_scr[...] = jnp.zeros_like(zc_scr)

        mb = _tri(SC, rev)
        m01 = _b01(mb)
        mt01 = _b01(_tri(SC, not rev))
        lo = lax.broadcasted_iota(jnp.int32, (1, 128), 1) < SP
        rlo = lax.broadcasted_iota(jnp.int32, (128, 1), 0) < SP
        order = list(range(nch) if rev else reversed(range(nch)))
        pre = {}
        for c in order:
            rows = slice(c * SC, (c + 1) * SC)
            dts_c = dts_ref[rows, :]
            lane, arow, cum, cumt, tot = _ssd_chunk_common(dts_c, alog_ref, m01, rev)
            pairs = []
            for pr in range(npair):
                xs = xa_ref[rows, pr * 128:(pr + 1) * 128].astype(F32)
                dyp = jnp.where(is_ctx, 0.0, dy_ref[rows, pr * 128:(pr + 1) * 128].astype(F32))
                cols = [16 * d + 2 * pr, 16 * d + 2 * pr + 1]
                cum_c = [_lane_pick(cum, lane, q) for q in cols]
                dt_c = [_lane_pick(dts_c, lane, q) for q in cols]
                tot_c = [_lane_pick(tot, lane, q) for q in cols]
                e1_c = [jnp.exp(cum_c[q]) for q in range(2)]
                e2_c = [jnp.exp(tot_c[q] - cum_c[q]) for q in range(2)]
                etot_c = [jnp.exp(tot_c[q]) for q in range(2)]
                dt_pair = jnp.where(lo, dt_c[0], dt_c[1])
                e1_pair = jnp.where(lo, e1_c[0], e1_c[1])
                e2_pair = jnp.where(lo, e2_c[0], e2_c[1])
                dtx = xs * dt_pair
                decs = [jnp.where(mb, jnp.exp(cum_c[q] - cumt[cols[q]:cols[q] + 1, :]), 0.0) for q in range(2)]
                dyq = [_mx(jnp.where(lo if q == 0 else ~lo, dyp, 0.0)) for q in range(2)]
                pairs.append(dict(xs=xs, dyp=dyp, cols=cols, e1_c=e1_c, e2_c=e2_c, etot_c=etot_c, dt_pair=dt_pair,
                                  e2_pair=e2_pair, etot_col=jnp.where(rlo, etot_c[0], etot_c[1]), dtx=dtx,
                                  dtxb=_mx(dtx), xeb=_mx(dtx * e2_pair), dy0b=_mx(dyp * e1_pair), decs=decs, dyq=dyq))
            pre[c] = dict(lane=lane, arow=arow, dts=dts_c, pairs=pairs, cum=cum, tot=tot,
                          bgb=[_mx(xa_ref[rows, D + g * SN:D + (g + 1) * SN]) for g in range(4)],
                          cgb=[_mx(xa_ref[rows, D + 512 + g * SN:D + 512 + (g + 1) * SN]) for g in range(4)])
        units = [(c, pr) for c in order for pr in range(npair)]
        head_lanes = [_head_lanes(16 * d + 2 * pr, 16 * d + 2 * pr + 1) for pr in range(npair)]
        one_lane = {16 * d + h: _one_lane(16 * d + h) for h in range(SHEADS)}
        P = lambda u: pre[u[0]]["pairs"][u[1]]
        cgu = lambda u: pre[u[0]]["cgb"][u[1] // 2]
        gm = {(c, g): _nt(pre[c]["cgb"][g], pre[c]["bgb"][g]) for c in order for g in range(4)}
        y0 = {u: _nt(cgu(u), sp_ref[u[0], u[1]]) for u in units}
        dcg_i = {u: _nn(P(u)["dy0b"], sp_ref[u[0], u[1]]) for u in units}
        dsl = {u: _tn(P(u)["dy0b"], cgu(u)) for u in units}
        w_ = {(u, q): gm[(u[0], u[1] // 2)] * P(u)["decs"][q] for u in units for q in range(2)}
        dw_ = {(u, q): jnp.where(mb, _nt(P(u)["dyq"][q], P(u)["dtxb"]), 0.0) for u in units for q in range(2)}
        ddtx_i = {(u, q): _tn(w_[(u, q)], P(u)["dyq"][q]) for u in units for q in range(2)}
        for c in order:
            rows = slice(c * SC, (c + 1) * SC)
            pc = pre[c]
            lane, arow, dts_c = pc["lane"], pc["arow"], pc["dts"]
            d1 = jnp.zeros((SC, 128), F32)
            d2 = jnp.zeros((SC, 128), F32)
            dz = jnp.zeros((SC, 128), F32)
            ddt = jnp.zeros((SC, 128), F32)
            dtot = jnp.zeros((1, 128), F32)
            dgm = [jnp.zeros((SC, SC), F32) for _ in range(4)]
            dbg = [jnp.zeros((SC, SN), F32) for _ in range(4)]
            dcg = [jnp.zeros((SC, SN), F32) for _ in range(4)]
            for pr in range(npair):
                u, g, p = (c, pr), pr // 2, pc["pairs"][pr]
                hs = head_lanes[pr]
                dso = dst[pr]
                dsob = _mx(dso)
                dxe = _nt(pc["bgb"][g], dsob)
                dbg[g] = dbg[g] + _nn(p["xeb"], dsob)
                ddtx = dxe * p["e2_pair"]
                d2 = d2 + _dot2(dxe * p["dtx"], hs)
                dcg[g] = dcg[g] + dcg_i[u]
                d1 = d1 + _dot2(p["dyp"] * y0[u], hs)
                sprod = dso * sp_ref[c, pr].astype(F32)
                dst[pr] = dso * p["etot_col"] + dsl[u]
                for q in range(2):
                    hm = lo if q == 0 else ~lo
                    col = p["cols"][q]
                    dw = dw_[(u, q)]
                    ddtx = ddtx + jnp.where(hm, ddtx_i[(u, q)], 0.0)
                    dgm[g] = dgm[g] + dw * p["decs"][q]
                    z = dw * w_[(u, q)]
                    dz = dz + _dot2(z, one_lane[col])
                    zc_scr[col:col + 1, :] = _colsum(z)
                    tsum = _rowsum(_colsum(sprod[q * SP:(q + 1) * SP, :]))
                    dtot = jnp.where(lane == col, tsum * p["etot_c"][q], dtot)
                dxs = ddtx * p["dt_pair"]
                ddt = ddt + _dot2(ddtx * p["xs"], hs)
                if last:
                    dxs = dxs + dxp_ref[rows, pr * 128:(pr + 1) * 128]
                dxa_ref[rows, pr * 128:(pr + 1) * 128] = dxs
            e2_all = jnp.exp(pc["tot"] - pc["cum"])
            dcum = dz - zc_scr[...].T + d1 * jnp.exp(pc["cum"]) - d2 * e2_all
            dtot = dtot + _colsum(d2 * e2_all)
            for g in range(4):
                db = dbg[g] + _tn(dgm[g], pc["cgb"][g])
                dc = dcg[g] + _nn(dgm[g], pc["bgb"][g])
                if last:
                    db = db + dxp_ref[rows, D + g * SN:D + (g + 1) * SN]
                    dc = dc + dxp_ref[rows, D + 512 + g * SN:D + 512 + (g + 1) * SN]
                dxa_ref[rows, D + g * SN:D + (g + 1) * SN] = db
                dxa_ref[rows, D + 512 + g * SN:D + 512 + (g + 1) * SN] = dc
            dla = _dot01(mt01, dcum, ways=2) + dtot
            ddt = ddt + dla * arow
            da_ref[0:1, :] += _colsum(dla * dts_c)
            if last:
                ddt = ddt + ddp_ref[rows, :]
            ddts_ref[rows, :] = ddt

    blk = lambda s: _blk(nb - 1 - s, nb, rev)
    in_specs = [pl.BlockSpec((TB, 2048), lambda s: (blk(s), 0)), pl.BlockSpec((TB, 128), lambda s: (blk(s), 0)),
                _full((8, 128)), pl.BlockSpec((nch, npair, 128, SN), lambda s: (blk(s), 0, 0, 0)),
                pl.BlockSpec((TB, D), lambda s: (jnp.minimum(blk(s), nb - 2), 0))]
    args = [xa, dts, alog, sprev, dy]
    if last:
        in_specs += [pl.BlockSpec((TB, 2048), lambda s: (blk(s), 0)), pl.BlockSpec((TB, 128), lambda s: (blk(s), 0))]
        args += list(prev)
    call = dict(
        body=body, args=args, name=f"ssd_bwd_{d}",
        out_shape=(jax.ShapeDtypeStruct((t_total, 2048), F32), jax.ShapeDtypeStruct((t_total, 128), F32),
                   jax.ShapeDtypeStruct((8, 128), F32)),
        grid=(nb,), in_specs=in_specs,
        out_specs=(pl.BlockSpec((TB, 2048), lambda s: (blk(s), 0)), pl.BlockSpec((TB, 128), lambda s: (blk(s), 0)),
                   _full((8, 128))),
        scratch=[pltpu.VMEM((npair, 128, SN), F32), pltpu.VMEM((128, 128), F32)], sem=("arbitrary",), vmem_mb=48)
    return _run(_carry(call, comm, lambda: (pl.program_id(0) == 0, pl.program_id(0) == nb - 1)))


def _readout(o, g, yy, z, vec_ref):
    hg, ss, keep = [], [], []
    for h in range(NH):
        cs = slice(h * HF, (h + 1) * HF)
        oh = o[:, cs]
        r = lax.rsqrt(jnp.mean(oh * oh, axis=1, keepdims=True) + EPS)
        hg.append(oh * r * vec_ref[0:1, cs] * _silu(g[:, cs]))
        keep.append(r)
    u = yy * _silu(z)
    for gi in range(4):
        cs = slice(gi * 256, (gi + 1) * 256)
        ug = u[:, cs]
        r = lax.rsqrt(jnp.mean(ug * ug, axis=1, keepdims=True) + EPS)
        ss.append(ug * r * vec_ref[2:3, cs])
        keep.append(r)
    return jnp.concatenate(hg, axis=1), jnp.concatenate(ss, axis=1), keep, u


def _mix_out(o_f, o_b, p_main, y_f, y_b, xa, x, vecs, w_out):
    n = x.shape[0]

    def body(of_ref, ob_ref, g_ref, z_ref, yf_ref, yb_ref, xs_ref, x_ref, vec_ref, w_ref,
             ymix_ref, ylat_ref, h1_ref, u2_ref):
        o = of_ref[...].astype(F32) + ob_ref[...].astype(F32)
        yy = yf_ref[...].astype(F32) + yb_ref[...].astype(F32) + vec_ref[1:2, :] * xs_ref[...].astype(F32)
        hg, ss, _, _ = _readout(o, g_ref[...].astype(F32), yy, z_ref[...].astype(F32), vec_ref)
        ymix = jnp.concatenate([hg, ss], axis=1).astype(MXU_DTYPE)
        ymix_ref[...] = ymix
        ylat = _nn(ymix, w_ref[...])
        ylat_ref[...] = ylat
        h1 = x_ref[...] + vec_ref[3:4, :] * ylat
        h1_ref[...] = h1
        r = lax.rsqrt(jnp.mean(h1 * h1, axis=1, keepdims=True) + EPS)
        u2_ref[...] = ((h1 * r * vec_ref[6:7, :]) * vec_ref[4:5, :] + vec_ref[5:6, :]).astype(MXU_DTYPE)

    row = lambda j: (lambda i: (i, j))
    return _pcall(
        body, name="mix_out",
        out_shape=(jax.ShapeDtypeStruct((n, 2 * D), MXU_DTYPE), jax.ShapeDtypeStruct((n, D), F32),
                   jax.ShapeDtypeStruct((n, D), F32), jax.ShapeDtypeStruct((n, D), MXU_DTYPE)),
        grid=(n // TB,),
        in_specs=[pl.BlockSpec((TB, D), row(0)), pl.BlockSpec((TB, D), row(0)), pl.BlockSpec((TB, D), row(4)),
                  pl.BlockSpec((TB, D), row(5)), pl.BlockSpec((TB, D), row(0)), pl.BlockSpec((TB, D), row(0)),
                  pl.BlockSpec((TB, D), row(0)), pl.BlockSpec((TB, D), row(0)), _full((8, D)), _full((2 * D, D))],
        out_specs=(pl.BlockSpec((TB, 2 * D), row(0)), pl.BlockSpec((TB, D), row(0)), pl.BlockSpec((TB, D), row(0)),
                   pl.BlockSpec((TB, D), row(0))),
        sem=("parallel",), vmem_mb=48,
    )(o_f, o_b, p_main, p_main, y_f, y_b, xa, x, vecs, w_out)


def _mix_bwd(dylat, o_f, o_b, p_main, y_f, y_b, xa, vecs, w_out, comm=None):
    n = dylat.shape[0]
    t_total = p_main.shape[0]
    nlat = n // TB

    def body(*refs):
        dg_ref, dz_ref, acc_ref = refs[11], refs[13], refs[15]
        i = pl.program_id(0)

        @pl.when(i == 0)
        def _():
            acc_ref[...] = jnp.zeros_like(acc_ref)

        @pl.when(i < nlat)
        def _():
            compute(*refs)

        @pl.when(i == nlat)
        def _():
            dg_ref[...] = jnp.zeros_like(dg_ref)
            dz_ref[...] = jnp.zeros_like(dz_ref)

    def compute(dyl_ref, of_ref, ob_ref, g_ref, z_ref, yf_ref, yb_ref, xs_ref, vec_ref, w_ref,
                do_ref, dg_ref, dys_ref, dz_ref, dxs_ref, acc_ref):
        dymix = _nt(dyl_ref[...], w_ref[...])
        o = of_ref[...].astype(F32) + ob_ref[...].astype(F32)
        g = g_ref[...].astype(F32)
        z = z_ref[...].astype(F32)
        xs = xs_ref[...].astype(F32)
        yy = yf_ref[...].astype(F32) + yb_ref[...].astype(F32) + vec_ref[1:2, :] * xs
        _, _, keep, u = _readout(o, g, yy, z, vec_ref)
        do_l, dg_l = [], []
        for h in range(NH):
            cs = slice(h * HF, (h + 1) * HF)
            oh, gh, r, wv = o[:, cs], g[:, cs], keep[h], vec_ref[0:1, cs]
            dhg = dymix[:, cs]
            xh = oh * r
            dn = dhg * _silu(gh)
            dg_l.append(dhg * xh * wv * _dsilu(gh))
            acc_ref[0:1, cs] += _colsum(dn * xh)
            dxh = dn * wv
            do_l.append(r * (dxh - xh * jnp.mean(dxh * xh, axis=1, keepdims=True)))
        du_l = []
        for gi in range(4):
            cs = slice(gi * 256, (gi + 1) * 256)
            ug, r, wv = u[:, cs], keep[NH + gi], vec_ref[2:3, cs]
            dss = dymix[:, D + gi * 256:D + (gi + 1) * 256]
            xh = ug * r
            acc_ref[2:3, cs] += _colsum(dss * xh)
            dxh = dss * wv
            du_l.append(r * (dxh - xh * jnp.mean(dxh * xh, axis=1, keepdims=True)))
        du = jnp.concatenate(du_l, axis=1)
        dyy = du * _silu(z)
        do_ref[...] = jnp.concatenate(do_l, axis=1).astype(do_ref.dtype)
        dg_ref[...] = jnp.concatenate(dg_l, axis=1).astype(dg_ref.dtype)
        dys_ref[...] = dyy.astype(dys_ref.dtype)
        dz_ref[...] = (du * yy * _dsilu(z)).astype(dz_ref.dtype)
        dxs_ref[...] = (dyy * vec_ref[1:2, :]).astype(dxs_ref.dtype)
        acc_ref[1:2, :] += _colsum(dyy * xs)

    row = lambda j: (lambda i: (jnp.minimum(i, nlat - 1), j))
    lat = pl.BlockSpec((TB, D), row(0))
    tok = pl.BlockSpec((TB, D), lambda i: (i, 0))
    call = dict(
        body=body, args=[dylat, o_f, o_b, p_main, p_main, y_f, y_b, xa, vecs, w_out], name="mix_bwd",
        out_shape=(jax.ShapeDtypeStruct((n, D), MXU_DTYPE), jax.ShapeDtypeStruct((t_total, D), MXU_DTYPE),
                   jax.ShapeDtypeStruct((n, D), MXU_DTYPE), jax.ShapeDtypeStruct((t_total, D), MXU_DTYPE),
                   jax.ShapeDtypeStruct((n, D), MXU_DTYPE), jax.ShapeDtypeStruct((8, D), F32)),
        grid=(t_total // TB,),
        in_specs=[lat, lat, lat, pl.BlockSpec((TB, D), row(4)), pl.BlockSpec((TB, D), row(5)), lat, lat, lat,
                  _full((8, D)), _full((2 * D, D))],
        out_specs=(lat, tok, lat, tok, lat, _full((8, D))), scratch=[],
        sem=("arbitrary",), vmem_mb=48)
    return _run(_carry(call, comm, lambda: (pl.program_id(0) == 0, pl.program_id(0) == t_total // TB - 1)))


def _ffn_up(u2, w_gate, w_up):
    n = u2.shape[0]
    tb = 1024

    def body(u_ref, wg_ref, wu_ref, g_ref, up_ref, a_ref):
        uv = u_ref[...]
        gt = _nt(uv, wg_ref[...])
        upv = _nt(uv, wu_ref[...])
        g_ref[...] = gt.astype(g_ref.dtype)
        up_ref[...] = upv.astype(up_ref.dtype)
        a_ref[...] = (_silu(gt) * upv).astype(a_ref.dtype)

    blk = pl.BlockSpec((tb, FSL), lambda j, i: (i, j))
    wblk = pl.BlockSpec((None, FSL, D), lambda j, i: (j, 0, 0))
    return _pcall(
        body, name="ffn_up",
        out_shape=(jax.ShapeDtypeStruct((n, DFFP), MXU_DTYPE),) * 3,
        grid=(4, n // tb), in_specs=[pl.BlockSpec((tb, D), lambda j, i: (i, 0)), wblk, wblk],
        out_specs=(blk, blk, blk), sem=("parallel", "parallel"), vmem_mb=48,
    )(u2, w_gate, w_up)


def _ffn_down_loss(act, w_down, h1, tgt, vecs):
    n = act.shape[0]
    tb = 512

    def body(a_ref, w_ref, h1_ref, t_ref, vec_ref, dh2_ref, dffn_ref, acc_ref):
        i = pl.program_id(0)

        @pl.when(i == 0)
        def _():
            acc_ref[...] = jnp.zeros_like(acc_ref)

        g2 = vec_ref[0:1, :]
        fw = vec_ref[1:2, :]
        nsub = 4
        sb = tb // nsub
        wv = w_ref[...]
        ffns = [_nn(a_ref[r_ * sb:(r_ + 1) * sb, :], wv) for r_ in range(nsub)]
        for r_ in range(nsub):
            rows = slice(r_ * sb, (r_ + 1) * sb)
            ffn = ffns[r_]
            h2 = h1_ref[rows, :] + g2 * ffn
            r = lax.rsqrt(jnp.mean(h2 * h2, axis=1, keepdims=True) + EPS)
            xh = h2 * r
            err = xh * fw - t_ref[rows, :]
            dy = err * (1.0 / D)
            acc_ref[2:3, :] += _colsum(err * err) * (0.5 / D)
            acc_ref[1:2, :] += _colsum(dy * xh)
            dxh = dy * fw
            dh2 = r * (dxh - xh * jnp.mean(dxh * xh, axis=1, keepdims=True))
            dh2_ref[rows, :] = dh2
            dffn_ref[rows, :] = (g2 * dh2).astype(dffn_ref.dtype)
            acc_ref[0:1, :] += _colsum(dh2 * ffn)

    return _pcall(
        body, name="ffn_down_loss",
        out_shape=(jax.ShapeDtypeStruct((n, D), F32), jax.ShapeDtypeStruct((n, D), MXU_DTYPE),
                   jax.ShapeDtypeStruct((8, D), F32)),
        grid=(n // tb,),
        in_specs=[pl.BlockSpec((tb, DFFP), lambda i: (i, 0)), _full((DFFP, D)), pl.BlockSpec((tb, D), lambda i: (i, 0)),
                  pl.BlockSpec((tb, D), lambda i: (i, 0)), _full((8, D))],
        out_specs=(pl.BlockSpec((tb, D), lambda i: (i, 0)), pl.BlockSpec((tb, D), lambda i: (i, 0)), _full((8, D))),
        sem=("arbitrary",), vmem_mb=48,
    )(act, w_down, h1, tgt, vecs)


def _ffn_bwd(dffn, w_down, gate, up, w_gate_t, w_up_t):
    n = dffn.shape[0]
    tb = 1024

    def body(df_ref, wd_ref, g_ref, up_ref, wg_ref, wu_ref, dg_ref, dup_ref, du_ref):
        j = pl.program_id(1)
        nsub = 4
        sb = tb // nsub
        wd, wg, wu = wd_ref[...], wg_ref[...], wu_ref[...]
        dacts = [_nt(df_ref[r * sb:(r + 1) * sb, :], wd) for r in range(nsub)]
        parts = []
        for r in range(nsub):
            rows = slice(r * sb, (r + 1) * sb)
            gt = g_ref[rows, :].astype(F32)
            upv = up_ref[rows, :].astype(F32)
            sg = _sig(gt)
            dgt = (dacts[r] * upv * (sg * (1.0 + gt * (1.0 - sg)))).astype(MXU_DTYPE)
            dupv = (dacts[r] * (gt * sg)).astype(MXU_DTYPE)
            dg_ref[rows, :] = dgt
            dup_ref[rows, :] = dupv
            parts.append(_nn(dgt, wg) + _nn(dupv, wu))
        part = jnp.concatenate(parts, axis=0)

        @pl.when(j == 0)
        def _():
            du_ref[...] = part

        @pl.when(j > 0)
        def _():
            du_ref[...] += part

    tok = pl.BlockSpec((tb, D), lambda i, j: (i, 0))
    ffb = pl.BlockSpec((tb, FSL), lambda i, j: (i, j))
    wsl = pl.BlockSpec((None, FSL, D), lambda i, j: (j, 0, 0))
    return _pcall(
        body, name="ffn_bwd",
        out_shape=(jax.ShapeDtypeStruct((n, DFFP), MXU_DTYPE), jax.ShapeDtypeStruct((n, DFFP), MXU_DTYPE),
                   jax.ShapeDtypeStruct((n, D), F32)),
        grid=(n // tb, 4),
        in_specs=[tok, pl.BlockSpec((FSL, D), lambda i, j: (j, 0)), ffb, ffb, wsl, wsl],
        out_specs=(ffb, ffb, tok), sem=("parallel", "arbitrary"), vmem_mb=48,
    )(dffn, w_down, gate, up, w_gate_t, w_up_t)


def _ffn_norm_bwd(du, h1, ylat, dh2, vecs):
    n = du.shape[0]
    tb = 512

    def body(du_ref, h1_ref, yl_ref, dh2_ref, vec_ref, dh1_ref, dyl_ref, acc_ref):
        @pl.when(pl.program_id(0) == 0)
        def _():
            acc_ref[...] = jnp.zeros_like(acc_ref)

        duv = du_ref[...]
        h1 = h1_ref[...]
        r = lax.rsqrt(jnp.mean(h1 * h1, axis=1, keepdims=True) + EPS)
        xh = h1 * r
        nw = vec_ref[2:3, :]
        acc_ref[0:1, :] += _colsum(duv)
        acc_ref[1:2, :] += _colsum(duv * xh * nw)
        dn = duv * vec_ref[1:2, :]
        acc_ref[2:3, :] += _colsum(dn * xh)
        dxh = dn * nw
        dh1 = dh2_ref[...] + r * (dxh - xh * jnp.mean(dxh * xh, axis=1, keepdims=True))
        dh1_ref[...] = dh1
        dyl_ref[...] = (vec_ref[0:1, :] * dh1).astype(dyl_ref.dtype)
        acc_ref[3:4, :] += _colsum(dh1 * yl_ref[...])

    tok = pl.BlockSpec((tb, D), lambda i: (i, 0))
    return _pcall(
        body, name="ffn_norm_bwd",
        out_shape=(jax.ShapeDtypeStruct((n, D), F32), jax.ShapeDtypeStruct((n, D), MXU_DTYPE),
                   jax.ShapeDtypeStruct((8, D), F32)),
        grid=(n // tb,), in_specs=[tok, tok, tok, tok, _full((8, D))], out_specs=(tok, tok, _full((8, D))),
        sem=("arbitrary",), vmem_mb=40,
    )(du, h1, ylat, dh2, vecs)


def _deep_rows(rows):
    return max(r for r in range(128, 2305, 128) if rows % r == 0)


def _dw(a, b, name):
    tn_rows = a.shape[0]
    bt = _deep_rows(tn_rows)
    kk, nn_ = a.shape[1], b.shape[1]
    bk = 1024 if kk % 1024 == 0 else kk
    bn = 1024 if nn_ % 1024 == 0 else nn_
    nt = tn_rows // bt

    def body(a_ref, b_ref, o_ref, acc):
        t = pl.program_id(2)
        part = _tn(a_ref[...], b_ref[...])

        @pl.when(t == 0)
        def _():
            acc[...] = part

        @pl.when(t > 0)
        def _():
            acc[...] += part

        @pl.when(t == nt - 1)
        def _():
            o_ref[...] = acc[...].astype(o_ref.dtype)

    return _pcall(
        body, name=name, out_shape=jax.ShapeDtypeStruct((kk, nn_), MXU_DTYPE), grid=(kk // bk, nn_ // bn, nt),
        in_specs=[pl.BlockSpec((bt, bk), lambda i, j, t: (t, i)), pl.BlockSpec((bt, bn), lambda i, j, t: (t, j))],
        out_specs=pl.BlockSpec((bk, bn), lambda i, j, t: (i, j)), scratch=[pltpu.VMEM((bk, bn), F32)],
        sem=("parallel", "parallel", "arbitrary"), vmem_mb=40,
    )(a, b)


def _dw_in(segs, u_all, name):
    tiles = []
    for m, s_ in enumerate(segs):
        tiles += [(m, h) for h in range(s_.shape[1] // D)]
    ntile = len(tiles)
    t_total = u_all.shape[0]
    bt = _deep_rows(t_total)
    nt = t_total // bt

    def body(u_ref, *refs):
        seg_refs, o_ref, acc = refs[:len(segs)], refs[len(segs)], refs[len(segs) + 1]
        n, t = pl.program_id(0), pl.program_id(1)
        for k, (m, _) in enumerate(tiles):
            @pl.when(n == k)
            def _(m=m):
                part = _tn(seg_refs[m][...], u_ref[...])

                @pl.when(t == 0)
                def _():
                    acc[...] = part

                @pl.when(t > 0)
                def _():
                    acc[...] += part

        @pl.when(t == nt - 1)
        def _():
            o_ref[...] = acc[...].astype(o_ref.dtype)

    def seg_spec(m):
        ks = [k for k, (mm, _) in enumerate(tiles) if mm == m]
        lo, hi = ks[0], ks[-1]
        on = lambda n: (n >= lo) & (n <= hi)
        return pl.BlockSpec((bt, D), lambda n, t: (jnp.where(on(n), t, 0), jnp.where(on(n), n - lo, 0)))

    return _pcall(
        body, name=name, out_shape=jax.ShapeDtypeStruct((1, ntile * D, D), MXU_DTYPE), grid=(ntile, nt),
        in_specs=[pl.BlockSpec((bt, D), lambda n, t: (t, 0))] + [seg_spec(m) for m in range(len(segs))],
        out_specs=pl.BlockSpec((None, D, D), lambda n, t: (0, n, 0)),
        scratch=[pltpu.VMEM((D, D), F32)], sem=("parallel", "arbitrary"), vmem_mb=56,
    )(u_all, *segs)


def _du_prenorm_bwd(segs, ddt, wi_main, wi_tail, xin, mods, dres, row_off, tb, name, comm=None):
    n = xin.shape[0]
    nt = n // tb
    off = row_off // tb
    has_dx = dres is not None

    def body(*refs):
        seg_refs = refs[:7]
        ddt_ref, w_ref, wb_ref, wdt_ref, x_ref, mod_ref = refs[7:13]
        rest = refs[13:]
        if has_dx:
            dres_ref, dx_ref, acc_ref, du_scr = rest
        else:
            acc_ref, du_scr = rest
        j, i = pl.program_id(0), pl.program_id(1)
        rows = pl.ds(pl.multiple_of(i * tb, tb), tb)

        @pl.when((i == 0) & (j == 0))
        def _():
            acc_ref[...] = jnp.zeros_like(acc_ref)

        @pl.when(j == 0)
        def _():
            du_scr[rows, :] = _nn(ddt_ref[...], wdt_ref[...])

        for k in range(8):
            if not has_dx and k in (4, 5):
                continue

            @pl.when(j == k)
            def _(k=k):
                sv = seg_refs[min(k, 6)][...]
                part = _nn(sv, w_ref[...])
                if k in (2, 4, 6):
                    part = part + _nn(sv[:, 0:WTAIL], wb_ref[...])
                du_scr[rows, :] += part

        @pl.when(j == 7)
        def _():
            du = du_scr[rows, :]
            xv = x_ref[...]
            r = lax.rsqrt(jnp.mean(xv * xv, axis=1, keepdims=True) + EPS)
            xh = xv * r
            nw = mod_ref[1:2, :]
            acc_ref[0:1, :] += _colsum(du)
            acc_ref[1:2, :] += _colsum(du * xh * nw)
            dn = du * mod_ref[0:1, :]
            acc_ref[2:3, :] += _colsum(dn * xh)
            if has_dx:
                dxh = dn * nw
                dx_ref[...] = dres_ref[...] + r * (dxh - xh * jnp.mean(dxh * xh, axis=1, keepdims=True))

    def seg_spec(k):
        if k < 6:
            return pl.BlockSpec((tb, D), lambda j, i: (jnp.where(j == k, i + off, 0), 0))
        return pl.BlockSpec((tb, D), lambda j, i: (jnp.where(j >= 6, i + off, 0), jnp.where(j >= 6, j - 6, 0)))

    last = pl.BlockSpec((tb, D), lambda j, i: (jnp.where(j == 7, i, 0), 0))
    in_specs = [seg_spec(k) for k in range(7)]
    in_specs += [pl.BlockSpec((tb, 128), lambda j, i: (jnp.where(j == 0, i + off, 0), 0))] + _w_specs()
    in_specs += [last, _full((8, D))]
    args = list(segs) + [ddt, wi_main, wi_tail, wi_tail, xin, mods]
    out_shape = [jax.ShapeDtypeStruct((8, D), F32)]
    out_specs = [_full((8, D))]
    if has_dx:
        in_specs.append(last)
        args.append(dres)
        out_shape.insert(0, jax.ShapeDtypeStruct((n, D), F32))
        out_specs.insert(0, last)
    call = dict(body=body, args=args, name=name, out_shape=tuple(out_shape), grid=(8, nt), in_specs=in_specs,
                out_specs=tuple(out_specs), scratch=[pltpu.VMEM((n, D), F32)], sem=("arbitrary", "arbitrary"),
                vmem_mb=56)
    steps = lambda: ((pl.program_id(0) == 0) & (pl.program_id(1) == 0),
                     (pl.program_id(0) == 7) & (pl.program_id(1) == nt - 1))
    return _run(_carry(call, comm, steps))


def _sum8(v):
    def body(v_ref, o_ref):
        acc = v_ref[0]
        for k in range(1, 8):
            acc = acc + v_ref[k]
        o_ref[...] = acc

    return _pcall(body, name="small_sum", out_shape=jax.ShapeDtypeStruct(v.shape[1:], F32),
                  in_specs=[pl.BlockSpec(memory_space=pltpu.VMEM)], out_specs=pl.BlockSpec(memory_space=pltpu.VMEM))(v)


def _adamw(w, m, v, g, name, comm=None):
    lead = w.ndim == 3
    rows, cols = w.shape[-2:]
    rb = 256 if rows % 256 == 0 else (352 if rows % 352 == 0 else rows)
    c1 = 1.0 - B1 ** STEP
    c2 = 1.0 - B2 ** STEP

    def body(w_ref, m_ref, v_ref, g_ref, d_ref, nm_ref, nv_ref):
        gv = g_ref[...]
        mn = B1 * m_ref[...] + (1.0 - B1) * gv
        vn = B2 * v_ref[...] + (1.0 - B2) * (gv * gv)
        nm_ref[...] = mn
        nv_ref[...] = vn
        d_ref[...] = -LR * ((mn / c1) / (jnp.sqrt(vn / c2) + AEPS) + WD * w_ref[...])

    if rb == rows and rows > 1024:
        cb, steps = 256, cols // 256
        gspec = pl.BlockSpec((rows, cb), lambda i: (0, i))
        spec = pl.BlockSpec((None, rows, cb), lambda i: (0, 0, i)) if lead else gspec
    else:
        steps = rows // rb
        gspec = pl.BlockSpec((rb, cols), lambda i: (i, 0))
        spec = pl.BlockSpec((None, rb, cols), lambda i: (0, i, 0)) if lead else gspec
    call = dict(body=body, args=[w, m, v, g], name=name, out_shape=(jax.ShapeDtypeStruct(w.shape, F32),) * 3,
                grid=(steps,), in_specs=[spec] * 3 + [gspec], out_specs=(spec,) * 3, scratch=[],
                sem=("arbitrary",) if comm is not None else ("parallel",), vmem_mb=40)
    return _run(_carry(call, comm, lambda: (pl.program_id(0) == 0, pl.program_id(0) == steps - 1,
                                            pl.program_id(0) == steps - 1)))


def _rows(v, n):
    f = v.reshape(-1)
    return jnp.pad(f, (0, n * D - f.shape[0])).reshape(n, D)


def kernel(x, c, ctx, c_ctx, w_ada, b_ada, norm_mix, w_in, conv_w, conv_b, ssd_a_log, ssd_dt_bias, ssd_d, ssd_norm, hgrn_lb_raw, hgrn_norm, w_out, norm_ffn, w_gate, w_up, w_down, final_norm, loss_target, m_c_ctx, m_w_ada, m_b_ada, m_norm_mix, m_w_in, m_conv_w, m_conv_b, m_ssd_a_log, m_ssd_dt_bias, m_ssd_d, m_ssd_norm, m_hgrn_lb_raw, m_hgrn_norm, m_w_out, m_norm_ffn, m_w_gate, m_w_up, m_w_down, m_final_norm, v_c_ctx, v_w_ada, v_b_ada, v_norm_mix, v_w_in, v_conv_w, v_conv_b, v_ssd_a_log, v_ssd_dt_bias, v_ssd_d, v_ssd_norm, v_hgrn_lb_raw, v_hgrn_norm, v_w_out, v_norm_ffn, v_w_gate, v_w_up, v_w_down, v_final_norm):
    ix, iy, ic = lax.axis_index("x"), lax.axis_index("y"), lax.axis_index("c")
    chip = 2 * ix + iy
    me = 2 * chip + ic
    xl, xc, tgt = x[0], ctx[0], loss_target[0]
    n_lat, n_ctx = xl.shape[0], xc.shape[0]
    assert n_ctx == TB and n_lat % 1024 == 0
    t_total = n_lat + n_ctx
    nb = t_total // TB

    tr = lambda a: jnp.swapaxes(a, -1, -2)
    shift = [functools.partial(jnp.pad, pad_width=((8 * k, WSL + WTAIL - NSH - 8 * k), (0, 0))) for k in range(4)]
    slab = lax.switch(chip, shift, tr(w_in[0]).astype(MXU_DTYPE))
    padrows = lambda a: jnp.pad(a, ((0, FSL - DFF // 4), (0, 0))).astype(MXU_DTYPE)
    shards = [slab[:WSL], slab[WSL:], w_out[0].astype(MXU_DTYPE), padrows(tr(w_gate[0])), padrows(tr(w_up[0])),
              padrows(w_down[0])]
    own = lambda g_, s_: lax.dynamic_update_slice(g_, s_[None], (chip, 0, 0))
    pack = jnp.concatenate([c, hgrn_lb_raw.reshape(1, D), _rows(conv_w[0], 3), jnp.zeros((3, D), F32)], axis=0)
    ncol_ada = w_ada.shape[2]
    b_shard = lax.dynamic_slice(b_ada, (0, chip * ncol_ada), (1, ncol_ada))
    gath, araw, mod_all, wi_main, wi_tail = _prologue(pack, c_ctx.reshape(1, D), w_ada[0], b_shard, shards[:2])
    wi_main, wi_tail = own(wi_main, shards[0]), own(wi_tail, shards[1])
    gath = gath.reshape(8, 8, D)
    lbraw_full = gath[0::2, 1].reshape(4, 2, 2, 256).transpose(1, 2, 0, 3).reshape(4, D)
    convw_full = gath[0::2, 2:5].reshape(4, 3 * D)[:, :KCONV * 512].reshape(4, KCONV, 512).transpose(1, 0, 2)
    convw_full = convw_full.reshape(KCONV, 2048)
    lbraw8 = jnp.pad(lbraw_full, ((0, 4), (0, 0)))
    convp = jnp.concatenate([convw_full, conv_b, jnp.zeros((2, 2048), F32)], axis=0)
    dtb = jnp.pad(ssd_dt_bias.reshape(1, 32), ((0, 7), (0, 96)))
    alog = jnp.pad(ssd_a_log.reshape(1, 32), ((0, 7), (0, 96)))
    mod_all = mod_all.reshape(8, 16, ncol_ada)[0::2]
    mod_full = mod_all.transpose(1, 0, 2).reshape(16, 4 * ncol_ada)
    my_mod = lax.dynamic_slice(mod_full, (me, 0), (1, 6 * D)).reshape(6, D)
    sh1, sc1, g1, sh2, sc2, g2 = (my_mod[k:k + 1] for k in range(6))
    csh1, csc1 = mod_full[8:9, 0:D], mod_full[8:9, D:2 * D]

    zrow = jnp.zeros((1, D), F32)
    mods_lat = jnp.concatenate([1.0 + sc1, sh1, norm_mix, zrow, zrow, zrow, zrow, zrow], axis=0)
    mods_ctx = jnp.concatenate([1.0 + csc1, csh1, norm_mix, zrow, zrow, zrow, zrow, zrow], axis=0)
    outs = _inproj(xl, mods_lat, wi_main, wi_tail, t_total, 1024, 0, None, "inproj_lat",
                   comm=_comm_gather(shards[2:5]))
    wo_g, wg_g, wu_g = (own(g_, s_) for g_, s_ in zip(outs[3:], shards[2:5]))
    w_out_f = wo_g.reshape(2 * D, D)
    p_main, p_dt, u_all = _inproj(xc, mods_ctx, wi_main, wi_tail, t_total, TB, nb - 1, outs[:3], "inproj_ctx")

    o_f, hs_f, wd_g = _hgrn_fwd(p_main, lbraw8, 0, nb, comm=_comm_gather(shards[5:]))
    w_down_f = own(wd_g, shards[5]).reshape(DFFP, D)
    o_b, hs_b = _hgrn_fwd(p_main, lbraw8, 1, nb)
    xa, dsl, dts = _ssd_prep(p_main, p_dt, convp, dtb, nb)
    y_f, ss_f = _ssd_fwd(xa, dts, alog, 0, nb)
    y_b, ss_b = _ssd_fwd(xa, dts, alog, 1, nb)

    vec_mix = jnp.concatenate([jnp.tile(hgrn_norm, (1, NH)), jnp.repeat(ssd_d, SP, axis=1), ssd_norm, g1, 1.0 + sc2,
                               sh2, norm_ffn, zrow], axis=0)
    ymix, ylat, h1, u2 = _mix_out(o_f, o_b, p_main, y_f, y_b, xa, xl, vec_mix, w_out_f)
    gate, up, act = _ffn_up(u2, wg_g, wu_g)
    vec_loss = jnp.concatenate([g2, final_norm.reshape(1, D)] + [zrow] * 6, axis=0)
    dh2, dffn, acc_loss = _ffn_down_loss(act, w_down_f, h1, tgt, vec_loss)

    core_arr = jnp.reshape(ic, (1,)).astype(jnp.int32)
    chip_arr = jnp.reshape(chip, (1,)).astype(jnp.int32)
    every = (0, 4)

    def pair_sum(gs, got, tag):
        return list(_pair_sum(gs, list(got), core_arr, "grads_pair_sum_" + tag))

    vec_ffn = jnp.concatenate([g1, 1.0 + sc2, norm_ffn] + [zrow] * 5, axis=0)
    dgate, dup, du2 = _ffn_bwd(dffn, w_down_f, gate, up, wg_g, wu_g)
    dh1, dylat, acc_ffn = _ffn_norm_bwd(du2, h1, ylat, dh2, vec_ffn)
    gw_down = _dw(act, dffn, "dw_down").reshape(4, FSL, D)
    ga1 = [_dw(dgate, u2, "dw_gate").reshape(4, FSL, D), _dw(dup, u2, "dw_up").reshape(4, FSL, D)]
    res = _mix_bwd(dylat, o_f, o_b, p_main, y_f, y_b, xa, vec_mix, w_out_f, comm=_comm_pair(ga1))
    (do, dgr, dys, dzr, dxs_skip, acc_mix), pair_a1 = res[:6], pair_sum(ga1, res[6:], "a1")
    ga2 = [gw_down, _dw(ymix, dylat, "dw_out").reshape(4, D // 2, D)]

    res = _hgrn_bwd(p_main, lbraw8, hs_f, do, 0, nb, None,
                    comm=[_comm_exchange(pair_a1, [every] * 2), _comm_pair(ga2)])
    (dq0, dff, dv0, dlb_f), recv_a, pair_a2 = res[:4], list(res[4:6]), pair_sum(ga2, res[6:], "a2")
    res = _hgrn_bwd(p_main, lbraw8, hs_b, do, 1, nb, (dq0, dv0), comm=_comm_exchange(pair_a2, [every] * 2))
    (dq, dfb, dv, dlb_b), recv_a = res[:4], recv_a + list(res[4:])
    pair_a, dests_a = pair_a1 + pair_a2, [every] * 4
    gw_in = [_dw_in([dq, dff], u_all, "dw_in_0"), _dw_in([dfb, dv], u_all, "dw_in_1"),
             _dw_in([dgr, dzr], u_all, "dw_in_2")]

    res = _ssd_bwd(xa, dts, alog, ss_f, dys, 0, nb, None, comm=_comm_pair(gw_in))
    (dxa0, ddts0, da_f), pair_b, dests_b = res[:3], pair_sum(gw_in, res[3:], "b"), [(0, 1), (1, 2), (2, 3)]
    res = _ssd_bwd(xa, dts, alog, ss_b, dys, 1, nb, (dxa0, ddts0), comm=_comm_exchange(pair_b, dests_b))
    (dxa, ddts, da_b), recv_b = res[:3], list(res[3:])
    dxbc, ddt, acc_conv, acc_dtb = _ssd_prep_bwd(p_main, p_dt, convp, dtb, dsl, dxa, dxs_skip, ddts, nb)
    gw_in.append(_dw_in([dxbc], u_all, "dw_in_3"))
    gw_in_dt = _dw(ddt, u_all, "dw_in_dt")
    gc = [gw_in[3], jnp.concatenate([g_[:, 0:WTAIL, :] for g_ in gw_in[1:]] + [gw_in_dt[None]], axis=0)]

    segs = [dq, dff, dfb, dv, dgr, dzr, dxbc]
    bmods_lat = jnp.concatenate([1.0 + sc1, norm_mix] + [zrow] * 6, axis=0)
    bmods_ctx = jnp.concatenate([1.0 + csc1, norm_mix] + [zrow] * 6, axis=0)
    res = _du_prenorm_bwd(segs, ddt, wi_main, wi_tail, xc, bmods_ctx, None, n_lat, TB, "du_ctx", comm=_comm_pair(gc))
    acc_ctx, pair_c, dests_c = res[0], pair_sum(gc, res[1:], "c"), [(3, 4), every]
    res = _du_prenorm_bwd(segs, ddt, wi_main, wi_tail, xl, bmods_lat, dh1, 0, 512, "du_lat",
                          comm=_comm_exchange(pair_c, dests_c))
    (grad_x, acc_lat), recv_c = res[:2], list(res[2:])

    mine = _chip_sum(pair_b + pair_c + pair_a, recv_b + recv_c + recv_a, chip_arr, dests_b + dests_c + dests_a,
                     [0, 0, 0, 0, 1, 3, 4, 5, 2])
    dmod_lat = jnp.concatenate([acc_lat[0:2], acc_ffn[3:4], acc_ffn[0:2], acc_loss[0:1]], axis=0)
    misc = jnp.concatenate([(da_f + da_b)[0, :32], jnp.zeros((96,), F32), acc_dtb[0, :32], jnp.zeros((96,), F32),
                            jnp.sum(acc_loss[2]).reshape(1), jnp.zeros((D - 257,), F32)]).reshape(1, D)
    sv = jnp.concatenate([
        dmod_lat, acc_ctx[0:2], (acc_lat[2:3] + acc_ctx[2:3]), acc_ffn[2:3], acc_loss[1:2], acc_mix[2:3],
        acc_mix[0:1], acc_mix[1:2], dlb_f[0:1], dlb_b[0:1], acc_conv[0:6].reshape(12, D), misc,
        jnp.zeros((3, D), F32)], axis=0)
    res = _pair_swap(mine, sv)
    theirs, sv_all = res[:-1], res[-1].reshape(8, 32, D)
    whole = [jnp.concatenate([jnp.where(ic == 0, m_, t_), jnp.where(ic == 0, t_, m_)], axis=0)
             for m_, t_ in zip(mine, theirs)]
    g_w_in = lax.dynamic_slice(jnp.concatenate(whole[0:2], axis=0), (8 * chip, 0), (NSH, D))
    g_w_out = whole[2]
    g_w_gate = whole[3][:DFF // 4]
    g_w_up = whole[4][:DFF // 4]
    g_w_down = whole[5][:DFF // 4]
    ssum = _sum8(sv_all)
    dmod_rows = sv_all[:, 0:6].reshape(8, 6 * D)
    dmod_ctx_row = jnp.concatenate([ssum[6:8].reshape(1, 2 * D), jnp.zeros((1, 4 * D), F32)], axis=1)
    dmod_full = jnp.concatenate([dmod_rows, dmod_ctx_row, jnp.zeros((7, 6 * D), F32)], axis=0)
    grad_b_ada = jnp.sum(dmod_full, axis=0, keepdims=True)
    dmod_shard = lax.dynamic_slice(dmod_full, (0, chip * ncol_ada), (16, ncol_ada))
    g_w_ada, da_part = _ada_bwd(araw, dmod_shard, w_ada[0])
    da_all = _allgather8(da_part, "ada_ctx_gather").reshape(8, 16, D)[0::2, 8]

    big = {}
    for nm, w_, m_, v_, g_ in (("w_ada", w_ada, m_w_ada, v_w_ada, g_w_ada), ("w_in", w_in, m_w_in, v_w_in, g_w_in),
                               ("w_out", w_out, m_w_out, v_w_out, g_w_out),
                               ("w_gate", w_gate, m_w_gate, v_w_gate, g_w_gate),
                               ("w_up", w_up, m_w_up, v_w_up, g_w_up),
                               ("w_down", w_down, m_w_down, v_w_down, g_w_down)):
        if nm in ("w_in", "w_gate", "w_up"):
            big[nm] = tuple(tr(t) for t in (g_[None],) + tuple(_adamw(tr(w_), tr(m_), tr(v_), g_, "adamw_" + nm)))
        else:
            big[nm] = (g_[None],) + tuple(_adamw(w_, m_, v_, g_, "adamw_" + nm))
    cc = c_ctx.reshape(1, D)
    grad_c_ctx = (jnp.sum(da_all, axis=0, keepdims=True) * _dsilu(cc)).reshape(D)

    grad_norm_mix, grad_norm_ffn, grad_final_norm = ssum[8:9], ssum[9:10], ssum[10].reshape(D)
    grad_ssd_norm = ssum[11:12]
    grad_hgrn_norm = jnp.sum(ssum[12].reshape(NH, HF), axis=0, keepdims=True)
    grad_ssd_d = jnp.sum(ssum[13].reshape(SHEADS, SP), axis=1).reshape(1, SHEADS)
    lb_full = _sig(lbraw_full[0:2] - lbraw_full[2:4])
    dr0 = ssum[14:16] * lb_full * (1.0 - lb_full)
    grad_lb_full = jnp.stack([dr0, -dr0], axis=0)
    grad_lb = lax.dynamic_slice(grad_lb_full, (0, 0, chip * 256), (2, 2, 256))
    grad_conv_w = lax.dynamic_slice(ssum[16:26].reshape(KCONV, 2048), (0, chip * 512), (KCONV, 512)).reshape(1, KCONV, 512)
    grad_conv_b = ssum[26:28].reshape(1, 2048)
    a_val = -jnp.exp(ssd_a_log)
    grad_a_log = ssum[28, 0:32].reshape(1, 2, SHEADS) * a_val
    grad_dt_bias = ssum[28, 128:160].reshape(1, 2, SHEADS)
    loss = ssum[28, 256]

    small_w = [c_ctx, b_ada, norm_mix, conv_w, conv_b, ssd_a_log, ssd_dt_bias, ssd_d, ssd_norm, hgrn_lb_raw,
               hgrn_norm, norm_ffn, final_norm]
    small_m = [m_c_ctx, m_b_ada, m_norm_mix, m_conv_w, m_conv_b, m_ssd_a_log, m_ssd_dt_bias, m_ssd_d, m_ssd_norm,
               m_hgrn_lb_raw, m_hgrn_norm, m_norm_ffn, m_final_norm]
    small_v = [v_c_ctx, v_b_ada, v_norm_mix, v_conv_w, v_conv_b, v_ssd_a_log, v_ssd_dt_bias, v_ssd_d, v_ssd_norm,
               v_hgrn_lb_raw, v_hgrn_norm, v_norm_ffn, v_final_norm]
    small_g = [grad_c_ctx, grad_b_ada, grad_norm_mix, grad_conv_w, grad_conv_b, grad_a_log, grad_dt_bias, grad_ssd_d,
               grad_ssd_norm, grad_lb, grad_hgrn_norm, grad_norm_ffn, grad_final_norm]
    nrows = [-(-a.size // D) for a in small_w]
    packs = lambda lst: jnp.concatenate([_rows(a, r) for a, r in zip(lst, nrows)]
                                        + [jnp.zeros((24 - sum(nrows), D), F32)], axis=0)
    sd, sm, svv = _adamw(packs(small_w), packs(small_m), packs(small_v), packs(small_g), "adamw_small")

    def unpack(p):
        out, r0 = [], 0
        for a, r in zip(small_w, nrows):
            out.append(p[r0:r0 + r].reshape(-1)[:a.size].reshape(a.shape))
            r0 += r
        return out

    sd, sm, svv = unpack(sd), unpack(sm), unpack(svv)

    order = ["c_ctx", "w_ada", "b_ada", "norm_mix", "w_in", "conv_w", "conv_b", "ssd_a_log", "ssd_dt_bias", "ssd_d",
             "ssd_norm", "hgrn_lb_raw", "hgrn_norm", "w_out", "norm_ffn", "w_gate", "w_up", "w_down", "final_norm"]
    small_names = ["c_ctx", "b_ada", "norm_mix", "conv_w", "conv_b", "ssd_a_log", "ssd_dt_bias", "ssd_d", "ssd_norm",
                   "hgrn_lb_raw", "hgrn_norm", "norm_ffn", "final_norm"]
    table = dict(big)
    for k, nm in enumerate(small_names):
        table[nm] = (small_g[k].reshape(small_w[k].shape), sd[k], sm[k], svv[k])
    grads = [table[nm][0] for nm in order]
    deltas = [table[nm][1] for nm in order]
    new_m = [table[nm][2] for nm in order]
    new_v = [table[nm][3] for nm in order]
    return (loss, grad_x[None], *grads, *deltas, *new_m, *new_v)
```

```python
import functools
import math

import jax
import jax.numpy as jnp
from jax import lax
from jax.experimental import pallas as pl
from jax.experimental.pallas import tpu as pltpu

F32 = jnp.float32
BF16 = jnp.bfloat16
MXU_DTYPE = jnp.bfloat16
_INTERPRET = False

D = 1024
NH, HF = 8, 128
HC = 64
SC = 128
SN = 128
SHEADS, SP = 16, 64
GRID_W = 64
KCONV = 5
DFF = 2816
FSL = 768
DFFP = 4 * FSL
NIN = 8224
TB = 256
EPS = 1e-6
LR, B1, B2, AEPS, WD, STEP = 0.001, 0.9, 0.999, 1e-08, 0.01, 10
MESH_ID = pl.DeviceIdType.MESH
NSH = NIN // 4
WSL = 2048
WTAIL = 128


def _pcall(body, *, name, out_shape, grid=(), in_specs=None, out_specs=None, scratch=(), sem=None,
           vmem_mb=None, aliases=None):
    params = {}
    if sem is not None:
        params["dimension_semantics"] = sem
    if vmem_mb is not None:
        params["vmem_limit_bytes"] = vmem_mb << 20
    kw = dict(name=name, out_shape=out_shape, scratch_shapes=list(scratch),
              input_output_aliases=aliases or {}, compiler_params=pltpu.CompilerParams(**params),
              interpret=_INTERPRET)
    if grid:
        kw["grid"] = grid
    if in_specs is not None:
        kw["in_specs"] = in_specs
    if out_specs is not None:
        kw["out_specs"] = out_specs
    return pl.pallas_call(body, **kw)


def _mx(a):
    return a.astype(MXU_DTYPE)


def _dg(a, b, ca, cb):
    return lax.dot_general(_mx(a), _mx(b), (((ca,), (cb,)), ((), ())), preferred_element_type=F32)


def _nn(a, b):
    return _dg(a, b, 1, 0)


def _nt(a, b):
    return _dg(a, b, 1, 1)


def _tn(a, b):
    return _dg(a, b, 0, 0)


def _dot01(m, x, ways=3):
    f = lambda t: lax.dot_general(m, t, (((1,), (0,)), ((), ())), preferred_element_type=F32)
    hi = x.astype(BF16)
    r1 = x - hi.astype(F32)
    mid = r1.astype(BF16)
    if ways == 2:
        return f(hi) + f(mid)
    lo = (r1 - mid.astype(F32)).astype(BF16)
    return f(hi) + f(mid) + f(lo)


def _tri(n, upper):
    r = lax.broadcasted_iota(jnp.int32, (n, n), 0)
    c = lax.broadcasted_iota(jnp.int32, (n, n), 1)
    return (c >= r) if upper else (c <= r)


def _b01(mask):
    return jnp.where(mask, 1.0, 0.0).astype(BF16)


def _sig(x):
    return jax.nn.sigmoid(x)


def _silu(x):
    return x * _sig(x)


def _dsilu(x):
    s = _sig(x)
    return s * (1.0 + x * (1.0 - s))


def _softplus(x):
    return jnp.maximum(x, 0.0) + jnp.log(1.0 + jnp.exp(-jnp.abs(x)))


def _rowsum(x):
    return jnp.sum(x, axis=1, keepdims=True)


def _colsum(x):
    return jnp.sum(x, axis=0, keepdims=True)


def _full(shape):
    return pl.BlockSpec(shape, lambda *_: (0,) * len(shape))


def _allgather8_phases(x_ref, out_ref, send_sems, recv_sems, local_sem):
    m_per = x_ref.shape[0]
    x, y, c = lax.axis_index("x"), lax.axis_index("y"), lax.axis_index("c")
    me, sibling = (x, y, c), (x, y, 1 - c)
    chips = [(1 - x, y), (x, 1 - y), (1 - x, 1 - y)]

    def rows(px, py, pc):
        return out_ref.at[pl.ds((4 * px + 2 * py + pc) * m_per, m_per), :]

    def copy(k, block, to, src=None):
        return pltpu.make_async_remote_copy(
            src_ref=rows(*block) if src is None else src, dst_ref=rows(*block),
            send_sem=send_sems.at[k], recv_sem=recv_sems.at[k], device_id=to, device_id_type=MESH_ID)

    mine = pltpu.make_async_copy(x_ref, rows(*me), local_sem)
    first = [copy(0, me, sibling, src=x_ref)]
    first += [copy(1 + j, me, (*chip, c), src=x_ref) for j, chip in enumerate(chips)]
    passed = [copy(4 + j, (*chip, c), sibling) for j, chip in enumerate(chips)]

    def start():
        mine.start()
        for cp in first:
            cp.start()

    def forward():
        for j, chip in enumerate(chips):
            copy(1 + j, (*chip, c), me).wait_recv()
            passed[j].start()

    def finish():
        copy(0, sibling, me).wait_recv()
        for j, chip in enumerate(chips):
            copy(4 + j, (*chip, 1 - c), me).wait_recv()
        for cp in first + passed:
            cp.wait_send()
        mine.wait()

    return start, forward, finish


def _allgather8_ops(x_ref, out_ref, send_sems, recv_sems, local_sem):
    for phase in _allgather8_phases(x_ref, out_ref, send_sems, recv_sems, local_sem):
        phase()


def _allgather8(v, name):
    m_per, n = v.shape
    return _pcall(
        functools.partial(_allgather8_ops), name=name, out_shape=jax.ShapeDtypeStruct((8 * m_per, n), v.dtype),
        in_specs=[pl.BlockSpec(memory_space=pltpu.VMEM)], out_specs=pl.BlockSpec(memory_space=pltpu.VMEM),
        scratch=list(_AG8_SEMS),
    )(v)


_AG8_SEMS = [pltpu.SemaphoreType.DMA((7,)), pltpu.SemaphoreType.DMA((7,)), pltpu.SemaphoreType.DMA]


def _prologue(pack, cc_row, w_ada, b_shard, shards):
    n = len(shards)
    ncol = w_ada.shape[1]

    def body(pack_ref, cc_ref, w_ref, b_ref, *refs):
        ins = refs[:n]
        gath_ref, araw_ref, mod_ref = refs[n:n + 3]
        outs = refs[n + 3:2 * n + 3]
        modsh, s1, r1, l1, s2, r2, l2, gs, gr = refs[2 * n + 3:]
        start, forward, finish = _gather_ops(ins, outs, gs, gr, relay=True)
        start()
        _allgather8_ops(pack_ref, gath_ref, s1, r1, l1)
        a = jnp.concatenate([gath_ref[8 * i:8 * i + 1, :] for i in range(8)] + [cc_ref[...], jnp.zeros((7, D), F32)],
                            axis=0)
        araw_ref[...] = a
        modsh[...] = _nn(_silu(a), w_ref[...]) + b_ref[...]
        _allgather8_ops(modsh, mod_ref, s2, r2, l2)
        forward()
        finish()

    vm = pl.BlockSpec(memory_space=pltpu.VMEM)
    anyspec = pl.BlockSpec(memory_space=pl.ANY)
    return _pcall(
        body, name="prologue",
        out_shape=(jax.ShapeDtypeStruct((64, D), F32), jax.ShapeDtypeStruct((16, D), F32),
                   jax.ShapeDtypeStruct((128, ncol), F32)) + _gather_out(shards),
        in_specs=[vm, vm, vm, vm] + [anyspec] * n, out_specs=(vm, vm, vm) + (anyspec,) * n,
        scratch=[pltpu.VMEM((16, ncol), F32)] + list(_AG8_SEMS) + list(_AG8_SEMS) + _gather_sems(n), vmem_mb=40,
    )(pack, cc_row, w_ada, b_shard, *shards)


def _gather_ops(ins, outs, send_sems, recv_sems, relay=False):
    n = len(ins)
    x, y, c = lax.axis_index("x"), lax.axis_index("y"), lax.axis_index("c")
    me, sibling = (x, y, c), (x, y, 1 - c)
    chips = [(1 - x, y), (x, 1 - y), (1 - x, 1 - y)]
    direct = 2 if relay else 3

    def part(a, px, py, pc, quarter=None):
        half = ins[a].shape[0] // 2
        if quarter is None:
            return outs[a].at[2 * px + py, pl.ds(pc * half, half), :]
        return outs[a].at[2 * px + py, pl.ds(pc * half + quarter * (half // 2), half // 2), :]

    def copy(a, k, block, to, src=None, quarter=None):
        return pltpu.make_async_remote_copy(
            src_ref=part(a, *block, quarter) if src is None else src, dst_ref=part(a, *block, quarter),
            send_sem=send_sems.at[8 * a + k], recv_sem=recv_sems.at[8 * a + k], device_id=to,
            device_id_type=MESH_ID)

    def first(a, j):
        half = ins[a].shape[0] // 2
        return copy(a, j, me, (*chips[j], c), src=ins[a].at[pl.ds(c * half, half), :])

    relayed = lambda a, q: copy(a, 6 + q, (*chips[q], c), (*chips[1 - q], c), quarter=q)

    def start():
        for a in range(n):
            for j in range(direct):
                first(a, j).start()

    def forward():
        for a in range(n):
            for j in range(direct):
                copy(a, j, (*chips[j], c), me).wait_recv()
                copy(a, 3 + j, (*chips[j], c), sibling).start()
                if relay:
                    relayed(a, j).start()
            if relay:
                for q in range(2):
                    copy(a, 6 + q, (*chips[2], c), me, quarter=q).wait_recv()
                copy(a, 5, (*chips[2], c), sibling).start()

    def finish():
        for a in range(n):
            for j, chip in enumerate(chips):
                copy(a, 3 + j, (*chip, 1 - c), me).wait_recv()
        for a in range(n):
            for j, chip in enumerate(chips):
                if j < direct:
                    first(a, j).wait_send()
                    if relay:
                        relayed(a, j).wait_send()
                copy(a, 3 + j, (*chip, c), sibling).wait_send()

    return start, forward, finish


def _gather_out(shards):
    return tuple(jax.ShapeDtypeStruct((4,) + s_.shape, s_.dtype) for s_ in shards)


def _gather_sems(n):
    return [pltpu.SemaphoreType.DMA((8 * n,)), pltpu.SemaphoreType.DMA((8 * n,))]


def _pair_ops(ins, outs, send_sems, recv_sems):
    x, y, c = lax.axis_index("x"), lax.axis_index("y"), lax.axis_index("c")
    cps = []
    for a in range(len(ins)):
        half = ins[a].shape[1] // 2
        cps.append(pltpu.make_async_remote_copy(
            src_ref=ins[a].at[:, pl.ds((1 - c) * half, half), :], dst_ref=outs[a], send_sem=send_sems.at[a],
            recv_sem=recv_sems.at[a], device_id=(x, y, 1 - c), device_id_type=MESH_ID))

    def start():
        for cp in cps:
            cp.start()

    def finish():
        for cp in cps:
            cp.wait()

    return start, finish


def _comm_pair(gs):
    n = len(gs)
    return (list(gs), tuple(jax.ShapeDtypeStruct((g.shape[0], g.shape[1] // 2, g.shape[2]), g.dtype) for g in gs),
            [pltpu.SemaphoreType.DMA((n,)), pltpu.SemaphoreType.DMA((n,))], _pair_ops)


def _exchange_ops(ins, outs, send_sems, recv_sems, dests):
    x, y, c = lax.axis_index("x"), lax.axis_index("y"), lax.axis_index("c")
    mine = 2 * x + y
    chips = [(1 - x, y), (x, 1 - y), (1 - x, 1 - y)]

    def each(fn):
        for a in range(len(ins)):
            lo, hi = dests[a]
            for j, (px, py) in enumerate(chips):
                q = 2 * px + py
                cp = pltpu.make_async_remote_copy(
                    src_ref=ins[a].at[jnp.clip(q - lo, 0, hi - lo - 1)], dst_ref=outs[a].at[j],
                    send_sem=send_sems.at[3 * a + j], recv_sem=recv_sems.at[3 * a + j], device_id=(px, py, c),
                    device_id_type=MESH_ID)
                fn(cp, (q >= lo) & (q < hi), (mine >= lo) & (mine < hi), (lo, hi) == (0, 4))

    def start():
        def go(cp, send_ok, recv_ok, always):
            if always:
                cp.start()
            else:
                pl.when(send_ok)(cp.start)
        each(go)

    def finish():
        def go(cp, send_ok, recv_ok, always):
            if always:
                cp.wait()
            else:
                pl.when(send_ok)(cp.wait_send)
                pl.when(recv_ok)(cp.wait_recv)
        each(go)

    return start, finish


def _comm_exchange(hs, dests):
    n = len(hs)
    return (list(hs), tuple(jax.ShapeDtypeStruct((3,) + h.shape[1:], h.dtype) for h in hs),
            [pltpu.SemaphoreType.DMA((3 * n,)), pltpu.SemaphoreType.DMA((3 * n,))],
            lambda i, o, s, r: _exchange_ops(i, o, s, r, dests))


def _comm_gather(shards, relay=False):
    return (list(shards), _gather_out(shards), _gather_sems(len(shards)),
            lambda i, o, s, r: _gather_ops(i, o, s, r, relay))


def _carry(call, comm, steps):
    if comm is None:
        return call
    if isinstance(comm, list):
        for one in comm:
            call = _carry(call, one, steps)
        return call
    arrays, out_shape, sems, make = comm
    n, n_in, n_out = len(arrays), len(call["args"]), len(call["out_shape"])
    body = call["body"]

    def wrapped(*refs):
        base_in, cin = refs[:n_in], refs[n_in:n_in + n]
        rest = refs[n_in + n:]
        base_out, cout, scr = rest[:n_out], rest[n_out:n_out + n], rest[n_out + n:]
        ops = make(cin, cout, scr[-2], scr[-1])
        when = steps()
        pl.when(when[0])(ops[0])
        if len(ops) == 3 and len(when) == 3:
            pl.when(when[2])(ops[1])
        body(*base_in, *base_out, *scr[:-2])
        if len(ops) == 3 and len(when) == 2:
            pl.when(when[1])(ops[1])
        pl.when(when[1])(ops[-1])

    anyspec = pl.BlockSpec(memory_space=pl.ANY)
    return dict(call, body=wrapped, args=list(call["args"]) + arrays,
                in_specs=list(call["in_specs"]) + [anyspec] * n,
                out_shape=tuple(call["out_shape"]) + tuple(out_shape),
                out_specs=tuple(call["out_specs"]) + (anyspec,) * n,
                scratch=list(call["scratch"]) + sems)


def _run(call):
    args = call.pop("args")
    body = call.pop("body")
    return _pcall(body, **call)(*args)


def _pair_swap(rs, sv):
    n = len(rs)

    def body(sv_ref, *refs):
        ins, outs, got_ref = refs[:n], refs[n:2 * n], refs[2 * n]
        send_sems, recv_sems, s1, r1, l1 = refs[2 * n + 1:]
        x, y, c = lax.axis_index("x"), lax.axis_index("y"), lax.axis_index("c")
        cps = [pltpu.make_async_remote_copy(
            src_ref=ins[a], dst_ref=outs[a], send_sem=send_sems.at[a], recv_sem=recv_sems.at[a],
            device_id=(x, y, 1 - c), device_id_type=MESH_ID) for a in range(n)]
        for cp in cps:
            cp.start()
        _allgather8_ops(sv_ref, got_ref, s1, r1, l1)
        for cp in cps:
            cp.wait()

    vm, anyspec = pl.BlockSpec(memory_space=pltpu.VMEM), pl.BlockSpec(memory_space=pl.ANY)
    return _pcall(
        body, name="grads_pair_swap",
        out_shape=tuple(jax.ShapeDtypeStruct(r.shape, r.dtype) for r in rs)
        + (jax.ShapeDtypeStruct((8 * sv.shape[0], sv.shape[1]), sv.dtype),),
        in_specs=[vm] + [anyspec] * n, out_specs=(anyspec,) * n + (vm,),
        scratch=[pltpu.SemaphoreType.DMA((n,)), pltpu.SemaphoreType.DMA((n,))] + list(_AG8_SEMS),
    )(sv, *rs)


SUM_STEPS = 4


def _pair_sum(gs, recvs, core, name):
    n = len(gs)

    def body(c_ref, *refs):
        for a in range(n):
            refs[2 * n + a][...] = (refs[a][...].astype(F32) + refs[n + a][...].astype(F32)).astype(refs[2 * n + a].dtype)

    blk = lambda g: (g.shape[0], g.shape[1] // (2 * SUM_STEPS), g.shape[2])
    return pl.pallas_call(
        body, name=name,
        out_shape=tuple(jax.ShapeDtypeStruct((g.shape[0], g.shape[1] // 2, g.shape[2]), g.dtype) for g in gs),
        grid_spec=pltpu.PrefetchScalarGridSpec(
            num_scalar_prefetch=1, grid=(SUM_STEPS,),
            in_specs=[pl.BlockSpec(blk(g), lambda i, cr: (0, cr[0] * SUM_STEPS + i, 0)) for g in gs]
            + [pl.BlockSpec(blk(g), lambda i, cr: (0, i, 0)) for g in gs],
            out_specs=tuple(pl.BlockSpec(blk(g), lambda i, cr: (0, i, 0)) for g in gs)),
        compiler_params=pltpu.CompilerParams(vmem_limit_bytes=40 << 20), interpret=_INTERPRET,
    )(core, *gs, *recvs)


def _chip_sum(hs, recvs, chip, dests, slots):
    n = len(hs)
    nout = max(slots) + 1
    first = [slots.index(o) for o in range(nout)]
    every = lambda d_: d_ == (0, 4)

    def own(d_):
        if every(d_):
            return lambda i, kr: (kr[0], i, 0)
        return lambda i, kr: (0, jnp.where(kr[0] == d_[0], i, 0), 0)

    def got(d_):
        if every(d_):
            return lambda i, kr: (0, i, 0)
        return lambda i, kr: (0, jnp.where(kr[0] == d_[0], i, 0), 0)

    def body(k_ref, *refs):
        for a in range(n):
            def emit(a=a):
                acc = refs[a][0].astype(F32)
                for j in range(3):
                    acc = acc + refs[n + a][j].astype(F32)
                refs[2 * n + slots[a]][...] = acc
            if every(dests[a]):
                emit()
            else:
                pl.when(k_ref[0] == dests[a][0])(emit)

    rb = lambda h: h.shape[1] // SUM_STEPS
    return pl.pallas_call(
        body, name="grads_chip_sum",
        out_shape=tuple(jax.ShapeDtypeStruct(hs[a].shape[1:], F32) for a in first),
        grid_spec=pltpu.PrefetchScalarGridSpec(
            num_scalar_prefetch=1, grid=(SUM_STEPS,),
            in_specs=[pl.BlockSpec((1, rb(h), h.shape[2]), own(d_)) for h, d_ in zip(hs, dests)]
            + [pl.BlockSpec((3, rb(h), h.shape[2]), got(d_)) for h, d_ in zip(hs, dests)],
            out_specs=tuple(pl.BlockSpec((rb(hs[a]), hs[a].shape[2]), lambda i, kr: (i, 0)) for a in first)),
        compiler_params=pltpu.CompilerParams(vmem_limit_bytes=40 << 20), interpret=_INTERPRET,
    )(chip, *hs, *recvs)


def _ada_bwd(araw, dmod, w):
    nblk = w.shape[1] // 512

    def body(a_ref, d_ref, w_ref, gw_ref, da_ref):
        j = pl.program_id(0)
        gw_ref[...] = _tn(_silu(a_ref[...]), d_ref[...])
        part = _nt(d_ref[...], w_ref[...])

        @pl.when(j == 0)
        def _():
            da_ref[...] = part

        @pl.when(j > 0)
        def _():
            da_ref[...] += part

    return _pcall(
        body, name="ada_bwd",
        out_shape=(jax.ShapeDtypeStruct(w.shape, F32), jax.ShapeDtypeStruct((16, D), F32)), grid=(nblk,),
        in_specs=[_full((16, D)), pl.BlockSpec((16, 512), lambda j: (0, j)), pl.BlockSpec((D, 512), lambda j: (0, j))],
        out_specs=(pl.BlockSpec((D, 512), lambda j: (0, j)), _full((16, D))), sem=("arbitrary",),
    )(araw, dmod, w)


def _w_specs():
    return [pl.BlockSpec((None, D, D), lambda j, i: (j // 2, j % 2, 0)),
            pl.BlockSpec((None, WTAIL, D), lambda j, i: (jnp.maximum(j // 2 - 1, 0), 0, 0)),
            pl.BlockSpec((None, WTAIL, D), lambda j, i: (3, 0, 0))]


def _inproj(xin, mods, wi_main, wi_tail, t_total, tb, blk_off, prev, name, comm=None):
    n = xin.shape[0]
    nt = n // tb
    ncol = 8

    def body(x_ref, mod_ref, w_ref, wb_ref, wdt_ref, *rest):
        p_ref, pdt_ref, u_ref, uscr = rest[-4:]
        j, i = pl.program_id(0), pl.program_id(1)
        rows = pl.ds(pl.multiple_of(i * tb, tb), tb)

        @pl.when(j == 0)
        def _():
            xv = x_ref[...]
            r = lax.rsqrt(jnp.mean(xv * xv, axis=1, keepdims=True) + EPS)
            u = (xv * r * mod_ref[2:3, :]) * mod_ref[0:1, :] + mod_ref[1:2, :]
            ub = u.astype(MXU_DTYPE)
            uscr[rows, :] = ub
            u_ref[...] = ub
            pdt_ref[...] = _nt(ub, wdt_ref[...])

        ub = uscr[rows, :]
        pv = _nt(ub, w_ref[...])

        @pl.when((j % 2 == 1) | (j == 0))
        def _():
            p_ref[...] = pv.astype(p_ref.dtype)

        @pl.when((j % 2 == 0) & (j > 0))
        def _():
            head = pv[:, 0:WTAIL] + _nt(ub, wb_ref[...])
            p_ref[...] = jnp.concatenate([head, pv[:, WTAIL:]], axis=1).astype(p_ref.dtype)

    once = lambda j, i: (jnp.where(j == 0, i, nt - 1) + blk_off, 0)
    in_specs = [pl.BlockSpec((tb, D), lambda j, i: (jnp.where(j == 0, i, nt - 1), 0)), _full((8, D))] + _w_specs()
    args = [xin, mods, wi_main, wi_tail, wi_tail]
    aliases = None
    if prev is not None:
        in_specs += [pl.BlockSpec(memory_space=pl.ANY)] * 3
        args += list(prev)
        aliases = {5: 0, 6: 1, 7: 2}
    call = dict(
        body=body, args=args, name=name,
        out_shape=(jax.ShapeDtypeStruct((t_total, ncol * D), MXU_DTYPE), jax.ShapeDtypeStruct((t_total, 128), F32),
                   jax.ShapeDtypeStruct((t_total, D), MXU_DTYPE)),
        grid=(ncol, nt), in_specs=in_specs,
        out_specs=(pl.BlockSpec((tb, D), lambda j, i: (i + blk_off, j)), pl.BlockSpec((tb, 128), once),
                   pl.BlockSpec((tb, D), once)),
        scratch=[pltpu.VMEM((n, D), MXU_DTYPE)], sem=("arbitrary", "arbitrary"), vmem_mb=48, aliases=aliases)
    steps = lambda: ((pl.program_id(0) == 0) & (pl.program_id(1) == 0),
                     (pl.program_id(0) == ncol - 1) & (pl.program_id(1) == nt - 1),
                     (pl.program_id(0) == ncol - 1) & (pl.program_id(1) == 0))
    return _run(_carry(call, comm, steps))


def _blk(s, nb, rev):
    return jnp.where(s == 0, nb - 1, (nb - 1 - s) if rev else (s - 1))


def _hgrn_gate(fr, lbraw_ref, d):
    lb = _sig(lbraw_ref[d:d + 1, :] - lbraw_ref[2 + d:3 + d, :])
    sg = _sig(fr)
    return lb, sg, lb + (1.0 - lb) * sg


def _hgrn_fwd(p_main, lbraw, d, nb, comm=None):
    t_total = p_main.shape[0]
    rev = d == 1
    nch = TB // HC
    scale = HF ** -0.5

    def body(q_ref, f_ref, v_ref, lb_ref, o_ref, sp_ref, st):
        s = pl.program_id(0)

        @pl.when(s == 0)
        def _():
            st[...] = jnp.zeros_like(st)

        mb = _tri(HC, rev)
        m01 = _b01(mb)
        order = list(reversed(range(nch)) if rev else range(nch))
        hs_ = [slice(h * HF, (h + 1) * HF) for h in range(NH)]
        pre = {}
        for c in order:
            rows = slice(c * HC, (c + 1) * HC)
            _, _, f = _hgrn_gate(f_ref[rows, :].astype(F32), lb_ref, d)
            k = 1.0 - f
            cum = _dot01(m01, jnp.log(f))
            tot = cum[0:1, :] if rev else cum[HC - 1:HC, :]
            qd = _silu(q_ref[rows, :].astype(F32)) * scale * jnp.exp(cum)
            ki = k * jnp.exp(-cum)
            etot = jnp.exp(tot)
            pre[c] = (_mx(qd), _mx(ki), _mx(ki * etot), _mx(v_ref[rows, :]), etot)
        scs = {c: [_nt(pre[c][0][:, cs], pre[c][1][:, cs]) for cs in hs_] for c in order}
        upd = {c: [_tn(pre[c][3][:, cs], pre[c][2][:, cs]) for cs in hs_] for c in order}
        intra = {c: [_nn(jnp.where(mb, scs[c][h], 0.0), pre[c][3][:, cs]) for h, cs in enumerate(hs_)] for c in order}
        for c in order:
            rows = slice(c * HC, (c + 1) * HC)
            qdb, etot = pre[c][0], pre[c][4]
            for h, cs in enumerate(hs_):
                sth = st[h]
                stb = sth.astype(sp_ref.dtype)
                sp_ref[c, h] = stb
                o_ref[rows, cs] = (intra[c][h] + _nt(qdb[:, cs], stb)).astype(o_ref.dtype)
                st[h] = sth * etot[:, cs] + upd[c][h]

    col = lambda j: (lambda s: (_blk(s, nb, rev), j))
    call = dict(
        body=body, args=[p_main, p_main, p_main, lbraw], name=f"hgrn_fwd_{d}",
        out_shape=(jax.ShapeDtypeStruct((t_total, D), MXU_DTYPE),
                   jax.ShapeDtypeStruct((nch * nb, NH, HF, HF), MXU_DTYPE)),
        grid=(nb,),
        in_specs=[pl.BlockSpec((TB, D), col(0)), pl.BlockSpec((TB, D), col(1 + d)), pl.BlockSpec((TB, D), col(3)),
                  _full((8, D))],
        out_specs=(pl.BlockSpec((TB, D), col(0)),
                   pl.BlockSpec((nch, NH, HF, HF), lambda s: (_blk(s, nb, rev), 0, 0, 0))),
        scratch=[pltpu.VMEM((NH, HF, HF), F32)], sem=("arbitrary",), vmem_mb=40)
    return _run(_carry(call, comm, lambda: (pl.program_id(0) == 0, pl.program_id(0) == nb - 1,
                                            pl.program_id(0) == nb - 4)))


def _hgrn_bwd(p_main, lbraw, sprev, do, d, nb, prev, comm=None):
    t_total = p_main.shape[0]
    rev = d == 1
    nch = TB // HC
    scale = HF ** -0.5
    last = prev is not None
    odt = MXU_DTYPE if last else F32

    def body(q_ref, f_ref, v_ref, lb_ref, sp_ref, do_ref, *rest):
        if last:
            dqp_ref, dvp_ref = rest[:2]
            rest = rest[2:]
        dq_ref, df_ref, dv_ref, dlb_ref, dst = rest
        sp_id = pl.program_id(0)
        is_ctx = sp_id == nb - 1

        @pl.when(sp_id == 0)
        def _():
            dst[...] = jnp.zeros_like(dst)
            dlb_ref[...] = jnp.zeros_like(dlb_ref)

        mb = _tri(HC, rev)
        mbt = _tri(HC, not rev)
        m01 = _b01(mb)
        mt01 = _b01(mbt)
        order = list(range(nch) if rev else reversed(range(nch)))
        hs_ = [slice(h * HF, (h + 1) * HF) for h in range(NH)]
        pre = {}
        for c in order:
            rows = slice(c * HC, (c + 1) * HC)
            lb, sg, f = _hgrn_gate(f_ref[rows, :].astype(F32), lb_ref, d)
            k = 1.0 - f
            cum = _dot01(m01, jnp.log(f))
            tot = cum[0:1, :] if rev else cum[HC - 1:HC, :]
            e = jnp.exp(cum)
            ei = jnp.exp(-cum)
            etot = jnp.exp(tot)
            ee = ei * etot
            qraw = q_ref[rows, :].astype(F32)
            sq = _sig(qraw)
            qd = qraw * sq * scale * e
            ki = k * ei
            ke = k * ee
            dov = jnp.where(is_ctx, 0.0, do_ref[rows, :].astype(F32))
            pre[c] = dict(lb=lb, sg=sg, f=f, e=e, ei=ei, ee=ee, etot=etot, qd=qd, ki=ki, ke=ke,
                          dsq=sq * (1.0 + qraw * (1.0 - sq)),
                          qdb=_mx(qd), kib=_mx(ki), keb=_mx(ke), vb=_mx(v_ref[rows, :]), dob=_mx(dov))
        units = [(c, h) for c in order for h in range(NH)]
        col = lambda u, key: pre[u[0]][key][:, hs_[u[1]]]
        pt = {u: jnp.where(mbt, _nt(col(u, "kib"), col(u, "qdb")), 0.0) for u in units}
        dp = {u: jnp.where(mb, _nt(col(u, "dob"), col(u, "vb")), 0.0) for u in units}
        dpt = {u: jnp.where(mbt, _nt(col(u, "vb"), col(u, "dob")), 0.0) for u in units}
        dv_i = {u: _nn(pt[u], col(u, "dob")) for u in units}
        dqd_ = {u: _nn(dp[u], col(u, "kib")) + _nn(col(u, "dob"), sp_ref[u[0], u[1]]) for u in units}
        dki_ = {u: _nn(dpt[u], col(u, "qdb")) for u in units}
        dsl = {u: _tn(col(u, "dob"), col(u, "qdb")) for u in units}
        for c in order:
            rows = slice(c * HC, (c + 1) * HC)
            p = pre[c]
            dv_l, dke_l, dtot_l = [], [], []
            for h, cs in enumerate(hs_):
                dso = dst[h]
                dsob = _mx(dso)
                dv_l.append(dv_i[(c, h)] + _nt(p["keb"][:, cs], dsob))
                dke_l.append(_nn(p["vb"][:, cs], dsob))
                dtot_l.append(_colsum(dso * sp_ref[c, h].astype(F32)) * p["etot"][:, cs])
                dst[h] = dso * p["etot"][:, cs] + dsl[(c, h)]
            lb, sg, f, e, ei, ee, qd, ki, ke = (p[n_] for n_ in ("lb", "sg", "f", "e", "ei", "ee", "qd", "ki", "ke"))
            dqd = jnp.concatenate([dqd_[(c, h)] for h in range(NH)], axis=1)
            dki = jnp.concatenate([dki_[(c, h)] for h in range(NH)], axis=1)
            dke = jnp.concatenate(dke_l, axis=1)
            dcum = dqd * qd - dki * ki - dke * ke
            dtot = jnp.concatenate(dtot_l, axis=1) + _colsum(dke * ke)
            dk = dki * ei + dke * ee
            dlf = _dot01(mt01, dcum, ways=2) + dtot
            df = dlf / f - dk
            dlb_ref[0:1, :] += _colsum(df * (1.0 - sg))
            dfr = df * (1.0 - lb) * sg * (1.0 - sg)
            dq = dqd * e * scale * p["dsq"]
            dv = jnp.concatenate(dv_l, axis=1)
            if last:
                dq = dq + dqp_ref[rows, :]
                dv = dv + dvp_ref[rows, :]
            dq_ref[rows, :] = dq.astype(odt)
            dv_ref[rows, :] = dv.astype(odt)
            df_ref[rows, :] = dfr.astype(MXU_DTYPE)

    blk = lambda s: _blk(nb - 1 - s, nb, rev)
    col = lambda j: (lambda s: (blk(s), j))
    in_specs = [pl.BlockSpec((TB, D), col(0)), pl.BlockSpec((TB, D), col(1 + d)), pl.BlockSpec((TB, D), col(3)),
                _full((8, D)), pl.BlockSpec((nch, NH, HF, HF), lambda s: (blk(s), 0, 0, 0)),
                pl.BlockSpec((TB, D), lambda s: (jnp.minimum(blk(s), nb - 2), 0))]
    args = [p_main, p_main, p_main, lbraw, sprev, do]
    if last:
        in_specs += [pl.BlockSpec((TB, D), col(0))] * 2
        args += list(prev)
    call = dict(
        body=body, args=args, name=f"hgrn_bwd_{d}",
        out_shape=(jax.ShapeDtypeStruct((t_total, D), odt), jax.ShapeDtypeStruct((t_total, D), MXU_DTYPE),
                   jax.ShapeDtypeStruct((t_total, D), odt), jax.ShapeDtypeStruct((8, D), F32)),
        grid=(nb,), in_specs=in_specs,
        out_specs=(pl.BlockSpec((TB, D), col(0)), pl.BlockSpec((TB, D), col(0)), pl.BlockSpec((TB, D), col(0)),
                   _full((8, D))),
        scratch=[pltpu.VMEM((NH, HF, HF), F32)], sem=("arbitrary",), vmem_mb=48)
    return _run(_carry(call, comm, lambda: (pl.program_id(0) == 0, pl.program_id(0) == nb - 1)))


def _conv_masks(tb, is_ctx):
    seg = jnp.where(is_ctx, tb, GRID_W)
    pos = lax.broadcasted_iota(jnp.int32, (tb, 1), 0) & (seg - 1)
    return pos, seg


def _shift_rows(x, dshift, pos, seg):
    if dshift == 0:
        return x
    n = x.shape[0]
    rolled = pltpu.roll(x, (-dshift) % n, 0)
    ok = (pos + dshift >= 0) & (pos + dshift < seg)
    return jnp.where(ok, rolled, 0.0)


def _ssd_prep(p_main, p_dt, convp, dtb, nb):
    t_total = p_main.shape[0]

    def body(x_ref, dt_ref, cw_ref, dtb_ref, xa_ref, ds_ref, dts_ref):
        is_ctx = pl.program_id(0) == nb - 1
        pos, seg = _conv_masks(TB, is_ctx)
        xv = x_ref[...].astype(F32)
        acc = cw_ref[5:6, :] + cw_ref[2:3, :] * xv
        for kk in (0, 1, 3, 4):
            acc = acc + cw_ref[kk:kk + 1, :] * _shift_rows(xv, kk - 2, pos, seg)
        sg = _sig(acc)
        xa_ref[...] = (acc * sg).astype(xa_ref.dtype)
        ds_ref[...] = (sg * (1.0 + acc * (1.0 - sg))).astype(ds_ref.dtype)
        dts_ref[...] = _softplus(dt_ref[...] + dtb_ref[0:1, :])

    wide = pl.BlockSpec((TB, 2048), lambda i: (i, 0))
    return _pcall(
        body, name="ssd_prep",
        out_shape=(jax.ShapeDtypeStruct((t_total, 2048), MXU_DTYPE), jax.ShapeDtypeStruct((t_total, 2048), MXU_DTYPE),
                   jax.ShapeDtypeStruct((t_total, 128), F32)),
        grid=(nb,),
        in_specs=[pl.BlockSpec((TB, 2048), lambda i: (i, 3)), pl.BlockSpec((TB, 128), lambda i: (i, 0)),
                  _full((8, 2048)), _full((8, 128))],
        out_specs=(wide, wide, pl.BlockSpec((TB, 128), lambda i: (i, 0))),
        sem=("parallel",), vmem_mb=32,
    )(p_main, p_dt, convp, dtb)


def _ssd_prep_bwd(p_main, p_dt, convp, dtb, dsl, dxa, dxs_skip, ddts, nb):
    t_total = p_main.shape[0]

    def body(x_ref, dt_ref, cw_ref, dtb_ref, ds_ref, dxa_ref, dsk_ref, ddts_ref, dx_ref, ddt_ref, dcw_ref, ddtb_ref):
        i = pl.program_id(0)
        is_ctx = i == nb - 1

        @pl.when(i == 0)
        def _():
            dcw_ref[...] = jnp.zeros_like(dcw_ref)
            ddtb_ref[...] = jnp.zeros_like(ddtb_ref)

        pos, seg = _conv_masks(TB, is_ctx)
        xv = x_ref[...].astype(F32)
        dact = dxa_ref[...]
        dact = jnp.concatenate([dact[:, :D] + jnp.where(is_ctx, 0.0, dsk_ref[...].astype(F32)), dact[:, D:]], axis=1)
        dpre = dact * ds_ref[...].astype(F32)
        dxv = cw_ref[2:3, :] * dpre
        dcw_ref[2:3, :] += _colsum(xv * dpre)
        for kk in (0, 1, 3, 4):
            sdp = _shift_rows(dpre, 2 - kk, pos, seg)
            dxv = dxv + cw_ref[kk:kk + 1, :] * sdp
            dcw_ref[kk:kk + 1, :] += _colsum(xv * sdp)
        dx_ref[...] = dxv.astype(dx_ref.dtype)
        dcw_ref[5:6, :] += _colsum(dpre)
        draw = ddts_ref[...] * _sig(dt_ref[...] + dtb_ref[0:1, :])
        ddt_ref[...] = draw.astype(ddt_ref.dtype)
        ddtb_ref[0:1, :] += _colsum(draw)

    return _pcall(
        body, name="ssd_prep_bwd",
        out_shape=(jax.ShapeDtypeStruct((t_total, 2048), MXU_DTYPE), jax.ShapeDtypeStruct((t_total, 128), MXU_DTYPE),
                   jax.ShapeDtypeStruct((8, 2048), F32), jax.ShapeDtypeStruct((8, 128), F32)),
        grid=(nb,),
        in_specs=[pl.BlockSpec((TB, 2048), lambda i: (i, 3)), pl.BlockSpec((TB, 128), lambda i: (i, 0)),
                  _full((8, 2048)), _full((8, 128)), pl.BlockSpec((TB, 2048), lambda i: (i, 0)),
                  pl.BlockSpec((TB, 2048), lambda i: (i, 0)),
                  pl.BlockSpec((TB, D), lambda i: (jnp.minimum(i, nb - 2), 0)),
                  pl.BlockSpec((TB, 128), lambda i: (i, 0))],
        out_specs=(pl.BlockSpec((TB, 2048), lambda i: (i, 0)), pl.BlockSpec((TB, 128), lambda i: (i, 0)),
                   _full((8, 2048)), _full((8, 128))),
        sem=("arbitrary",), vmem_mb=40,
    )(p_main, p_dt, convp, dtb, dsl, dxa, dxs_skip, ddts)


def _dot2(x, m01):
    hi = x.astype(BF16)
    lo = (x - hi.astype(F32)).astype(BF16)
    f = lambda t: lax.dot_general(t, m01, (((1,), (0,)), ((), ())), preferred_element_type=F32)
    return f(hi) + f(lo)


def _head_lanes(c0, c1):
    p = lax.broadcasted_iota(jnp.int32, (128, 128), 0)
    l = lax.broadcasted_iota(jnp.int32, (128, 128), 1)
    return _b01(((l == c0) & (p < SP)) | ((l == c1) & (p >= SP)))


def _one_lane(col):
    return _b01(lax.broadcasted_iota(jnp.int32, (128, 128), 1) == col)


def _lane_pick(x, lane, col):
    return _rowsum(jnp.where(lane == col, x, 0.0))


def _ssd_chunk_common(dts, alog_ref, m01, rev):
    lane = lax.broadcasted_iota(jnp.int32, (1, 128), 1)
    arow = -jnp.exp(alog_ref[0:1, :])
    cum = _dot01(m01, dts * arow)
    tot = cum[0:1, :] if rev else cum[SC - 1:SC, :]
    return lane, arow, cum, cum.T, tot


def _ssd_fwd(xa, dts, alog, d, nb):
    t_total = xa.shape[0]
    rev = d == 1
    nch = TB // SC
    npair = SHEADS // 2

    def body(xa_ref, dts_ref, alog_ref, y_ref, sp_ref, st):
        s = pl.program_id(0)

        @pl.when(s == 0)
        def _():
            st[...] = jnp.zeros_like(st)

        mb = _tri(SC, rev)
        m01 = _b01(mb)
        lo = lax.broadcasted_iota(jnp.int32, (1, 128), 1) < SP
        rlo = lax.broadcasted_iota(jnp.int32, (128, 1), 0) < SP
        order = list(reversed(range(nch)) if rev else range(nch))
        pre = {}
        for c in order:
            rows = slice(c * SC, (c + 1) * SC)
            dts_c = dts_ref[rows, :]
            lane, arow, cum, cumt, tot = _ssd_chunk_common(dts_c, alog_ref, m01, rev)
            bgs = [_mx(xa_ref[rows, D + g * SN:D + (g + 1) * SN]) for g in range(4)]
            cgs = [_mx(xa_ref[rows, D + 512 + g * SN:D + 512 + (g + 1) * SN]) for g in range(4)]
            pairs = []
            for pr in range(npair):
                xs = xa_ref[rows, pr * 128:(pr + 1) * 128].astype(F32)
                cols = [16 * d + 2 * pr, 16 * d + 2 * pr + 1]
                cum_c = [_lane_pick(cum, lane, q) for q in cols]
                dt_c = [_lane_pick(dts_c, lane, q) for q in cols]
                tot_c = [_lane_pick(tot, lane, q) for q in cols]
                dtx = xs * jnp.where(lo, dt_c[0], dt_c[1])
                e1_pair = jnp.where(lo, jnp.exp(cum_c[0]), jnp.exp(cum_c[1]))
                e2_pair = jnp.where(lo, jnp.exp(tot_c[0] - cum_c[0]), jnp.exp(tot_c[1] - cum_c[1]))
                etot_col = jnp.where(rlo, jnp.exp(tot_c[0]), jnp.exp(tot_c[1]))
                decs = [jnp.where(mb, jnp.exp(cum_c[q] - cumt[cols[q]:cols[q] + 1, :]), 0.0) for q in range(2)]
                dtxq = [_mx(jnp.where(lo if q == 0 else ~lo, dtx, 0.0)) for q in range(2)]
                pairs.append(dict(e1=e1_pair, etot=etot_col, decs=decs, dtxq=dtxq, xe=_mx(dtx * e2_pair)))
            pre[c] = (bgs, cgs, pairs)
        gm = {(c, g): _nt(pre[c][1][g], pre[c][0][g]) for c in order for g in range(4)}
        upd = {(c, pr): _tn(pre[c][2][pr]["xe"], pre[c][0][pr // 2]) for c in order for pr in range(npair)}
        intra = {(c, pr): sum(_nn(gm[(c, pr // 2)] * pre[c][2][pr]["decs"][q], pre[c][2][pr]["dtxq"][q]) for q in range(2))
                 for c in order for pr in range(npair)}
        for c in order:
            rows = slice(c * SC, (c + 1) * SC)
            bgs, cgs, pairs = pre[c]
            for pr in range(npair):
                stp = st[pr]
                stb = stp.astype(sp_ref.dtype)
                sp_ref[c, pr] = stb
                y_ref[rows, pr * 128:(pr + 1) * 128] = (
                    intra[(c, pr)] + pairs[pr]["e1"] * _nt(cgs[pr // 2], stb)).astype(y_ref.dtype)
                st[pr] = stp * pairs[pr]["etot"] + upd[(c, pr)]

    blk = lambda s: _blk(s, nb, rev)
    return _pcall(
        body, name=f"ssd_fwd_{d}",
        out_shape=(jax.ShapeDtypeStruct((t_total, D), MXU_DTYPE),
                   jax.ShapeDtypeStruct((nch * nb, npair, 128, SN), MXU_DTYPE)),
        grid=(nb,),
        in_specs=[pl.BlockSpec((TB, 2048), lambda s: (blk(s), 0)), pl.BlockSpec((TB, 128), lambda s: (blk(s), 0)),
                  _full((8, 128))],
        out_specs=(pl.BlockSpec((TB, D), lambda s: (blk(s), 0)),
                   pl.BlockSpec((nch, npair, 128, SN), lambda s: (blk(s), 0, 0, 0))),
        scratch=[pltpu.VMEM((npair, 128, SN), F32)], sem=("arbitrary",), vmem_mb=40,
    )(xa, dts, alog)


def _ssd_bwd(xa, dts, alog, sprev, dy, d, nb, prev, comm=None):
    t_total = xa.shape[0]
    rev = d == 1
    nch = TB // SC
    npair = SHEADS // 2
    last = prev is not None

    def body(xa_ref, dts_ref, alog_ref, sp_ref, dy_ref, *rest):
        if last:
            dxp_ref, ddp_ref = rest[:2]
            rest = rest[2:]
        dxa_ref, ddts_ref, da_ref, dst, zc_scr = rest
        sp_id = pl.program_id(0)
        is_ctx = sp_id == nb - 1

        @pl.when(sp_id == 0)
        def _():
            dst[...] = jnp.zeros_like(dst)
            da_ref[...] = jnp.zeros_like(da_ref)
            zc_scr[...] = jnp.zeros_like(zc_scr)

        mb = _tri(SC, rev)
        m01 = _b01(mb)
        mt01 = _b01(_tri(SC, not rev))
        lo = lax.broadcasted_iota(jnp.int32, (1, 128), 1) < SP
        rlo = lax.broadcasted_iota(jnp.int32, (128, 1), 0) < SP
        order = list(range(nch) if rev else reversed(range(nch)))
        pre = {}
        for c in order:
            rows = slice(c * SC, (c + 1) * SC)
            dts_c = dts_ref[rows, :]
            lane, arow, cum, cumt, tot = _ssd_chunk_common(dts_c, alog_ref, m01, rev)
            pairs = []
            for pr in range(npair):
                xs = xa_ref[rows, pr * 128:(pr + 1) * 128].astype(F32)
                dyp = jnp.where(is_ctx, 0.0, dy_ref[rows, pr * 128:(pr + 1) * 128].astype(F32))
                cols = [16 * d + 2 * pr, 16 * d + 2 * pr + 1]
                cum_c = [_lane_pick(cum, lane, q) for q in cols]
                dt_c = [_lane_pick(dts_c, lane, q) for q in cols]
                tot_c = [_lane_pick(tot, lane, q) for q in cols]
                e1_c = [jnp.exp(cum_c[q]) for q in range(2)]
                e2_c = [jnp.exp(tot_c[q] - cum_c[q]) for q in range(2)]
                etot_c = [jnp.exp(tot_c[q]) for q in range(2)]
                dt_pair = jnp.where(lo, dt_c[0], dt_c[1])
                e1_pair = jnp.where(lo, e1_c[0], e1_c[1])
                e2_pair = jnp.where(lo, e2_c[0], e2_c[1])
                dtx = xs * dt_pair
                decs = [jnp.where(mb, jnp.exp(cum_c[q] - cumt[cols[q]:cols[q] + 1, :]), 0.0) for q in range(2)]
                dyq = [_mx(jnp.where(lo if q == 0 else ~lo, dyp, 0.0)) for q in range(2)]
                pairs.append(dict(xs=xs, dyp=dyp, cols=cols, e1_c=e1_c, e2_c=e2_c, etot_c=etot_c, dt_pair=dt_pair,
                                  e2_pair=e2_pair, etot_col=jnp.where(rlo, etot_c[0], etot_c[1]), dtx=dtx,
                                  dtxb=_mx(dtx), xeb=_mx(dtx * e2_pair), dy0b=_mx(dyp * e1_pair), decs=decs, dyq=dyq))
            pre[c] = dict(lane=lane, arow=arow, dts=dts_c, pairs=pairs, cum=cum, tot=tot,
                          bgb=[_mx(xa_ref[rows, D + g * SN:D + (g + 1) * SN]) for g in range(4)],
                          cgb=[_mx(xa_ref[rows, D + 512 + g * SN:D + 512 + (g + 1) * SN]) for g in range(4)])
        units = [(c, pr) for c in order for pr in range(npair)]
        head_lanes = [_head_lanes(16 * d + 2 * pr, 16 * d + 2 * pr + 1) for pr in range(npair)]
        one_lane = {16 * d + h: _one_lane(16 * d + h) for h in range(SHEADS)}
        P = lambda u: pre[u[0]]["pairs"][u[1]]
        cgu = lambda u: pre[u[0]]["cgb"][u[1] // 2]
        gm = {(c, g): _nt(pre[c]["cgb"][g], pre[c]["bgb"][g]) for c in order for g in range(4)}
        y0 = {u: _nt(cgu(u), sp_ref[u[0], u[1]]) for u in units}
        dcg_i = {u: _nn(P(u)["dy0b"], sp_ref[u[0], u[1]]) for u in units}
        dsl = {u: _tn(P(u)["dy0b"], cgu(u)) for u in units}
        w_ = {(u, q): gm[(u[0], u[1] // 2)] * P(u)["decs"][q] for u in units for q in range(2)}
        dw_ = {(u, q): jnp.where(mb, _nt(P(u)["dyq"][q], P(u)["dtxb"]), 0.0) for u in units for q in range(2)}
        ddtx_i = {(u, q): _tn(w_[(u, q)], P(u)["dyq"][q]) for u in units for q in range(2)}
        for c in order:
            rows = slice(c * SC, (c + 1) * SC)
            pc = pre[c]
            lane, arow, dts_c = pc["lane"], pc["arow"], pc["dts"]
            d1 = jnp.zeros((SC, 128), F32)
            d2 = jnp.zeros((SC, 128), F32)
            dz = jnp.zeros((SC, 128), F32)
            ddt = jnp.zeros((SC, 128), F32)
            dtot = jnp.zeros((1, 128), F32)
            dgm = [jnp.zeros((SC, SC), F32) for _ in range(4)]
            dbg = [jnp.zeros((SC, SN), F32) for _ in range(4)]
            dcg = [jnp.zeros((SC, SN), F32) for _ in range(4)]
            for pr in range(npair):
                u, g, p = (c, pr), pr // 2, pc["pairs"][pr]
                hs = head_lanes[pr]
                dso = dst[pr]
                dsob = _mx(dso)
                dxe = _nt(pc["bgb"][g], dsob)
                dbg[g] = dbg[g] + _nn(p["xeb"], dsob)
                ddtx = dxe * p["e2_pair"]
                d2 = d2 + _dot2(dxe * p["dtx"], hs)
                dcg[g] = dcg[g] + dcg_i[u]
                d1 = d1 + _dot2(p["dyp"] * y0[u], hs)
                sprod = dso * sp_ref[c, pr].astype(F32)
                dst[pr] = dso * p["etot_col"] + dsl[u]
                for q in range(2):
                    hm = lo if q == 0 else ~lo
                    col = p["cols"][q]
                    dw = dw_[(u, q)]
                    ddtx = ddtx + jnp.where(hm, ddtx_i[(u, q)], 0.0)
                    dgm[g] = dgm[g] + dw * p["decs"][q]
                    z = dw * w_[(u, q)]
                    dz = dz + _dot2(z, one_lane[col])
                    zc_scr[col:col + 1, :] = _colsum(z)
                    tsum = _rowsum(_colsum(sprod[q * SP:(q + 1) * SP, :]))
                    dtot = jnp.where(lane == col, tsum * p["etot_c"][q], dtot)
                dxs = ddtx * p["dt_pair"]
                ddt = ddt + _dot2(ddtx * p["xs"], hs)
                if last:
                    dxs = dxs + dxp_ref[rows, pr * 128:(pr + 1) * 128]
                dxa_ref[rows, pr * 128:(pr + 1) * 128] = dxs
            e2_all = jnp.exp(pc["tot"] - pc["cum"])
            dcum = dz - zc_scr[...].T + d1 * jnp.exp(pc["cum"]) - d2 * e2_all
            dtot = dtot + _colsum(d2 * e2_all)
            for g in range(4):
                db = dbg[g] + _tn(dgm[g], pc["cgb"][g])
                dc = dcg[g] + _nn(dgm[g], pc["bgb"][g])
                if last:
                    db = db + dxp_ref[rows, D + g * SN:D + (g + 1) * SN]
                    dc = dc + dxp_ref[rows, D + 512 + g * SN:D + 512 + (g + 1) * SN]
                dxa_ref[rows, D + g * SN:D + (g + 1) * SN] = db
                dxa_ref[rows, D + 512 + g * SN:D + 512 + (g + 1) * SN] = dc
            dla = _dot01(mt01, dcum, ways=2) + dtot
            ddt = ddt + dla * arow
            da_ref[0:1, :] += _colsum(dla * dts_c)
            if last:
                ddt = ddt + ddp_ref[rows, :]
            ddts_ref[rows, :] = ddt

    blk = lambda s: _blk(nb - 1 - s, nb, rev)
    in_specs = [pl.BlockSpec((TB, 2048), lambda s: (blk(s), 0)), pl.BlockSpec((TB, 128), lambda s: (blk(s), 0)),
                _full((8, 128)), pl.BlockSpec((nch, npair, 128, SN), lambda s: (blk(s), 0, 0, 0)),
                pl.BlockSpec((TB, D), lambda s: (jnp.minimum(blk(s), nb - 2), 0))]
    args = [xa, dts, alog, sprev, dy]
    if last:
        in_specs += [pl.BlockSpec((TB, 2048), lambda s: (blk(s), 0)), pl.BlockSpec((TB, 128), lambda s: (blk(s), 0))]
        args += list(prev)
    call = dict(
        body=body, args=args, name=f"ssd_bwd_{d}",
        out_shape=(jax.ShapeDtypeStruct((t_total, 2048), F32), jax.ShapeDtypeStruct((t_total, 128), F32),
                   jax.ShapeDtypeStruct((8, 128), F32)),
        grid=(nb,), in_specs=in_specs,
        out_specs=(pl.BlockSpec((TB, 2048), lambda s: (blk(s), 0)), pl.BlockSpec((TB, 128), lambda s: (blk(s), 0)),
                   _full((8, 128))),
        scratch=[pltpu.VMEM((npair, 128, SN), F32), pltpu.VMEM((128, 128), F32)], sem=("arbitrary",), vmem_mb=48)
    return _run(_carry(call, comm, lambda: (pl.program_id(0) == 0, pl.program_id(0) == nb - 1)))


def _readout(o, g, yy, z, vec_ref):
    hg, ss, keep = [], [], []
    for h in range(NH):
        cs = slice(h * HF, (h + 1) * HF)
        oh = o[:, cs]
        r = lax.rsqrt(jnp.mean(oh * oh, axis=1, keepdims=True) + EPS)
        hg.append(oh * r * vec_ref[0:1, cs] * _silu(g[:, cs]))
        keep.append(r)
    u = yy * _silu(z)
    for gi in range(4):
        cs = slice(gi * 256, (gi + 1) * 256)
        ug = u[:, cs]
        r = lax.rsqrt(jnp.mean(ug * ug, axis=1, keepdims=True) + EPS)
        ss.append(ug * r * vec_ref[2:3, cs])
        keep.append(r)
    return jnp.concatenate(hg, axis=1), jnp.concatenate(ss, axis=1), keep, u


def _mix_out(o_f, o_b, p_main, y_f, y_b, xa, x, vecs, w_out):
    n = x.shape[0]

    def body(of_ref, ob_ref, g_ref, z_ref, yf_ref, yb_ref, xs_ref, x_ref, vec_ref, w_ref,
             ymix_ref, ylat_ref, h1_ref, u2_ref):
        o = of_ref[...].astype(F32) + ob_ref[...].astype(F32)
        yy = yf_ref[...].astype(F32) + yb_ref[...].astype(F32) + vec_ref[1:2, :] * xs_ref[...].astype(F32)
        hg, ss, _, _ = _readout(o, g_ref[...].astype(F32), yy, z_ref[...].astype(F32), vec_ref)
        ymix = jnp.concatenate([hg, ss], axis=1).astype(MXU_DTYPE)
        ymix_ref[...] = ymix
        ylat = _nn(ymix, w_ref[...])
        ylat_ref[...] = ylat
        h1 = x_ref[...] + vec_ref[3:4, :] * ylat
        h1_ref[...] = h1
        r = lax.rsqrt(jnp.mean(h1 * h1, axis=1, keepdims=True) + EPS)
        u2_ref[...] = ((h1 * r * vec_ref[6:7, :]) * vec_ref[4:5, :] + vec_ref[5:6, :]).astype(MXU_DTYPE)

    row = lambda j: (lambda i: (i, j))
    return _pcall(
        body, name="mix_out",
        out_shape=(jax.ShapeDtypeStruct((n, 2 * D), MXU_DTYPE), jax.ShapeDtypeStruct((n, D), F32),
                   jax.ShapeDtypeStruct((n, D), F32), jax.ShapeDtypeStruct((n, D), MXU_DTYPE)),
        grid=(n // TB,),
        in_specs=[pl.BlockSpec((TB, D), row(0)), pl.BlockSpec((TB, D), row(0)), pl.BlockSpec((TB, D), row(4)),
                  pl.BlockSpec((TB, D), row(5)), pl.BlockSpec((TB, D), row(0)), pl.BlockSpec((TB, D), row(0)),
                  pl.BlockSpec((TB, D), row(0)), pl.BlockSpec((TB, D), row(0)), _full((8, D)), _full((2 * D, D))],
        out_specs=(pl.BlockSpec((TB, 2 * D), row(0)), pl.BlockSpec((TB, D), row(0)), pl.BlockSpec((TB, D), row(0)),
                   pl.BlockSpec((TB, D), row(0))),
        sem=("parallel",), vmem_mb=48,
    )(o_f, o_b, p_main, p_main, y_f, y_b, xa, x, vecs, w_out)


def _mix_bwd(dylat, o_f, o_b, p_main, y_f, y_b, xa, vecs, w_out, comm=None):
    n = dylat.shape[0]
    t_total = p_main.shape[0]
    nlat = n // TB

    def body(*refs):
        dg_ref, dz_ref, acc_ref = refs[11], refs[13], refs[15]
        i = pl.program_id(0)

        @pl.when(i == 0)
        def _():
            acc_ref[...] = jnp.zeros_like(acc_ref)

        @pl.when(i < nlat)
        def _():
            compute(*refs)

        @pl.when(i == nlat)
        def _():
            dg_ref[...] = jnp.zeros_like(dg_ref)
            dz_ref[...] = jnp.zeros_like(dz_ref)

    def compute(dyl_ref, of_ref, ob_ref, g_ref, z_ref, yf_ref, yb_ref, xs_ref, vec_ref, w_ref,
                do_ref, dg_ref, dys_ref, dz_ref, dxs_ref, acc_ref):
        dymix = _nt(dyl_ref[...], w_ref[...])
        o = of_ref[...].astype(F32) + ob_ref[...].astype(F32)
        g = g_ref[...].astype(F32)
        z = z_ref[...].astype(F32)
        xs = xs_ref[...].astype(F32)
        yy = yf_ref[...].astype(F32) + yb_ref[...].astype(F32) + vec_ref[1:2, :] * xs
        _, _, keep, u = _readout(o, g, yy, z, vec_ref)
        do_l, dg_l = [], []
        for h in range(NH):
            cs = slice(h * HF, (h + 1) * HF)
            oh, gh, r, wv = o[:, cs], g[:, cs], keep[h], vec_ref[0:1, cs]
            dhg = dymix[:, cs]
            xh = oh * r
            dn = dhg * _silu(gh)
            dg_l.append(dhg * xh * wv * _dsilu(gh))
            acc_ref[0:1, cs] += _colsum(dn * xh)
            dxh = dn * wv
            do_l.append(r * (dxh - xh * jnp.mean(dxh * xh, axis=1, keepdims=True)))
        du_l = []
        for gi in range(4):
            cs = slice(gi * 256, (gi + 1) * 256)
            ug, r, wv = u[:, cs], keep[NH + gi], vec_ref[2:3, cs]
            dss = dymix[:, D + gi * 256:D + (gi + 1) * 256]
            xh = ug * r
            acc_ref[2:3, cs] += _colsum(dss * xh)
            dxh = dss * wv
            du_l.append(r * (dxh - xh * jnp.mean(dxh * xh, axis=1, keepdims=True)))
        du = jnp.concatenate(du_l, axis=1)
        dyy = du * _silu(z)
        do_ref[...] = jnp.concatenate(do_l, axis=1).astype(do_ref.dtype)
        dg_ref[...] = jnp.concatenate(dg_l, axis=1).astype(dg_ref.dtype)
        dys_ref[...] = dyy.astype(dys_ref.dtype)
        dz_ref[...] = (du * yy * _dsilu(z)).astype(dz_ref.dtype)
        dxs_ref[...] = (dyy * vec_ref[1:2, :]).astype(dxs_ref.dtype)
        acc_ref[1:2, :] += _colsum(dyy * xs)

    row = lambda j: (lambda i: (jnp.minimum(i, nlat - 1), j))
    lat = pl.BlockSpec((TB, D), row(0))
    tok = pl.BlockSpec((TB, D), lambda i: (i, 0))
    call = dict(
        body=body, args=[dylat, o_f, o_b, p_main, p_main, y_f, y_b, xa, vecs, w_out], name="mix_bwd",
        out_shape=(jax.ShapeDtypeStruct((n, D), MXU_DTYPE), jax.ShapeDtypeStruct((t_total, D), MXU_DTYPE),
                   jax.ShapeDtypeStruct((n, D), MXU_DTYPE), jax.ShapeDtypeStruct((t_total, D), MXU_DTYPE),
                   jax.ShapeDtypeStruct((n, D), MXU_DTYPE), jax.ShapeDtypeStruct((8, D), F32)),
        grid=(t_total // TB,),
        in_specs=[lat, lat, lat, pl.BlockSpec((TB, D), row(4)), pl.BlockSpec((TB, D), row(5)), lat, lat, lat,
                  _full((8, D)), _full((2 * D, D))],
        out_specs=(lat, tok, lat, tok, lat, _full((8, D))), scratch=[],
        sem=("arbitrary",), vmem_mb=48)
    return _run(_carry(call, comm, lambda: (pl.program_id(0) == 0, pl.program_id(0) == t_total // TB - 1)))


def _ffn_up(u2, w_gate, w_up):
    n = u2.shape[0]
    tb = 1024

    def body(u_ref, wg_ref, wu_ref, g_ref, up_ref, a_ref):
        uv = u_ref[...]
        gt = _nt(uv, wg_ref[...])
        upv = _nt(uv, wu_ref[...])
        g_ref[...] = gt.astype(g_ref.dtype)
        up_ref[...] = upv.astype(up_ref.dtype)
        a_ref[...] = (_silu(gt) * upv).astype(a_ref.dtype)

    blk = pl.BlockSpec((tb, FSL), lambda j, i: (i, j))
    wblk = pl.BlockSpec((None, FSL, D), lambda j, i: (j, 0, 0))
    return _pcall(
        body, name="ffn_up",
        out_shape=(jax.ShapeDtypeStruct((n, DFFP), MXU_DTYPE),) * 3,
        grid=(4, n // tb), in_specs=[pl.BlockSpec((tb, D), lambda j, i: (i, 0)), wblk, wblk],
        out_specs=(blk, blk, blk), sem=("parallel", "parallel"), vmem_mb=48,
    )(u2, w_gate, w_up)


def _ffn_down_loss(act, w_down, h1, tgt, vecs):
    n = act.shape[0]
    tb = 512

    def body(a_ref, w_ref, h1_ref, t_ref, vec_ref, dh2_ref, dffn_ref, acc_ref):
        i = pl.program_id(0)

        @pl.when(i == 0)
        def _():
            acc_ref[...] = jnp.zeros_like(acc_ref)

        g2 = vec_ref[0:1, :]
        fw = vec_ref[1:2, :]
        nsub = 4
        sb = tb // nsub
        wv = w_ref[...]
        ffns = [_nn(a_ref[r_ * sb:(r_ + 1) * sb, :], wv) for r_ in range(nsub)]
        for r_ in range(nsub):
            rows = slice(r_ * sb, (r_ + 1) * sb)
            ffn = ffns[r_]
            h2 = h1_ref[rows, :] + g2 * ffn
            r = lax.rsqrt(jnp.mean(h2 * h2, axis=1, keepdims=True) + EPS)
            xh = h2 * r
            err = xh * fw - t_ref[rows, :]
            dy = err * (1.0 / D)
            acc_ref[2:3, :] += _colsum(err * err) * (0.5 / D)
            acc_ref[1:2, :] += _colsum(dy * xh)
            dxh = dy * fw
            dh2 = r * (dxh - xh * jnp.mean(dxh * xh, axis=1, keepdims=True))
            dh2_ref[rows, :] = dh2
            dffn_ref[rows, :] = (g2 * dh2).astype(dffn_ref.dtype)
            acc_ref[0:1, :] += _colsum(dh2 * ffn)

    return _pcall(
        body, name="ffn_down_loss",
        out_shape=(jax.ShapeDtypeStruct((n, D), F32), jax.ShapeDtypeStruct((n, D), MXU_DTYPE),
                   jax.ShapeDtypeStruct((8, D), F32)),
        grid=(n // tb,),
        in_specs=[pl.BlockSpec((tb, DFFP), lambda i: (i, 0)), _full((DFFP, D)), pl.BlockSpec((tb, D), lambda i: (i, 0)),
                  pl.BlockSpec((tb, D), lambda i: (i, 0)), _full((8, D))],
        out_specs=(pl.BlockSpec((tb, D), lambda i: (i, 0)), pl.BlockSpec((tb, D), lambda i: (i, 0)), _full((8, D))),
        sem=("arbitrary",), vmem_mb=48,
    )(act, w_down, h1, tgt, vecs)


def _ffn_bwd(dffn, w_down, gate, up, w_gate_t, w_up_t):
    n = dffn.shape[0]
    tb = 1024

    def body(df_ref, wd_ref, g_ref, up_ref, wg_ref, wu_ref, dg_ref, dup_ref, du_ref):
        j = pl.program_id(1)
        nsub = 4
        sb = tb // nsub
        wd, wg, wu = wd_ref[...], wg_ref[...], wu_ref[...]
        dacts = [_nt(df_ref[r * sb:(r + 1) * sb, :], wd) for r in range(nsub)]
        parts = []
        for r in range(nsub):
            rows = slice(r * sb, (r + 1) * sb)
            gt = g_ref[rows, :].astype(F32)
            upv = up_ref[rows, :].astype(F32)
            sg = _sig(gt)
            dgt = (dacts[r] * upv * (sg * (1.0 + gt * (1.0 - sg)))).astype(MXU_DTYPE)
            dupv = (dacts[r] * (gt * sg)).astype(MXU_DTYPE)
            dg_ref[rows, :] = dgt
            dup_ref[rows, :] = dupv
            parts.append(_nn(dgt, wg) + _nn(dupv, wu))
        part = jnp.concatenate(parts, axis=0)

        @pl.when(j == 0)
        def _():
            du_ref[...] = part

        @pl.when(j > 0)
        def _():
            du_ref[...] += part

    tok = pl.BlockSpec((tb, D), lambda i, j: (i, 0))
    ffb = pl.BlockSpec((tb, FSL), lambda i, j: (i, j))
    wsl = pl.BlockSpec((None, FSL, D), lambda i, j: (j, 0, 0))
    return _pcall(
        body, name="ffn_bwd",
        out_shape=(jax.ShapeDtypeStruct((n, DFFP), MXU_DTYPE), jax.ShapeDtypeStruct((n, DFFP), MXU_DTYPE),
                   jax.ShapeDtypeStruct((n, D), F32)),
        grid=(n // tb, 4),
        in_specs=[tok, pl.BlockSpec((FSL, D), lambda i, j: (j, 0)), ffb, ffb, wsl, wsl],
        out_specs=(ffb, ffb, tok), sem=("parallel", "arbitrary"), vmem_mb=48,
    )(dffn, w_down, gate, up, w_gate_t, w_up_t)


def _ffn_norm_bwd(du, h1, ylat, dh2, vecs):
    n = du.shape[0]
    tb = 512

    def body(du_ref, h1_ref, yl_ref, dh2_ref, vec_ref, dh1_ref, dyl_ref, acc_ref):
        @pl.when(pl.program_id(0) == 0)
        def _():
            acc_ref[...] = jnp.zeros_like(acc_ref)

        duv = du_ref[...]
        h1 = h1_ref[...]
        r = lax.rsqrt(jnp.mean(h1 * h1, axis=1, keepdims=True) + EPS)
        xh = h1 * r
        nw = vec_ref[2:3, :]
        acc_ref[0:1, :] += _colsum(duv)
        acc_ref[1:2, :] += _colsum(duv * xh * nw)
        dn = duv * vec_ref[1:2, :]
        acc_ref[2:3, :] += _colsum(dn * xh)
        dxh = dn * nw
        dh1 = dh2_ref[...] + r * (dxh - xh * jnp.mean(dxh * xh, axis=1, keepdims=True))
        dh1_ref[...] = dh1
        dyl_ref[...] = (vec_ref[0:1, :] * dh1).astype(dyl_ref.dtype)
        acc_ref[3:4, :] += _colsum(dh1 * yl_ref[...])

    tok = pl.BlockSpec((tb, D), lambda i: (i, 0))
    return _pcall(
        body, name="ffn_norm_bwd",
        out_shape=(jax.ShapeDtypeStruct((n, D), F32), jax.ShapeDtypeStruct((n, D), MXU_DTYPE),
                   jax.ShapeDtypeStruct((8, D), F32)),
        grid=(n // tb,), in_specs=[tok, tok, tok, tok, _full((8, D))], out_specs=(tok, tok, _full((8, D))),
        sem=("arbitrary",), vmem_mb=40,
    )(du, h1, ylat, dh2, vecs)


def _deep_rows(rows):
    return max(r for r in range(128, 2305, 128) if rows % r == 0)


def _dw(a, b, name):
    tn_rows = a.shape[0]
    bt = _deep_rows(tn_rows)
    kk, nn_ = a.shape[1], b.shape[1]
    bk = 1024 if kk % 1024 == 0 else kk
    bn = 1024 if nn_ % 1024 == 0 else nn_
    nt = tn_rows // bt

    def body(a_ref, b_ref, o_ref, acc):
        t = pl.program_id(2)
        part = _tn(a_ref[...], b_ref[...])

        @pl.when(t == 0)
        def _():
            acc[...] = part

        @pl.when(t > 0)
        def _():
            acc[...] += part

        @pl.when(t == nt - 1)
        def _():
            o_ref[...] = acc[...].astype(o_ref.dtype)

    return _pcall(
        body, name=name, out_shape=jax.ShapeDtypeStruct((kk, nn_), MXU_DTYPE), grid=(kk // bk, nn_ // bn, nt),
        in_specs=[pl.BlockSpec((bt, bk), lambda i, j, t: (t, i)), pl.BlockSpec((bt, bn), lambda i, j, t: (t, j))],
        out_specs=pl.BlockSpec((bk, bn), lambda i, j, t: (i, j)), scratch=[pltpu.VMEM((bk, bn), F32)],
        sem=("parallel", "parallel", "arbitrary"), vmem_mb=40,
    )(a, b)


def _dw_in(segs, u_all, name):
    tiles = []
    for m, s_ in enumerate(segs):
        tiles += [(m, h) for h in range(s_.shape[1] // D)]
    ntile = len(tiles)
    t_total = u_all.shape[0]
    bt = _deep_rows(t_total)
    nt = t_total // bt

    def body(u_ref, *refs):
        seg_refs, o_ref, acc = refs[:len(segs)], refs[len(segs)], refs[len(segs) + 1]
        n, t = pl.program_id(0), pl.program_id(1)
        for k, (m, _) in enumerate(tiles):
            @pl.when(n == k)
            def _(m=m):
                part = _tn(seg_refs[m][...], u_ref[...])

                @pl.when(t == 0)
                def _():
                    acc[...] = part

                @pl.when(t > 0)
                def _():
                    acc[...] += part

        @pl.when(t == nt - 1)
        def _():
            o_ref[...] = acc[...].astype(o_ref.dtype)

    def seg_spec(m):
        ks = [k for k, (mm, _) in enumerate(tiles) if mm == m]
        lo, hi = ks[0], ks[-1]
        on = lambda n: (n >= lo) & (n <= hi)
        return pl.BlockSpec((bt, D), lambda n, t: (jnp.where(on(n), t, 0), jnp.where(on(n), n - lo, 0)))

    return _pcall(
        body, name=name, out_shape=jax.ShapeDtypeStruct((1, ntile * D, D), MXU_DTYPE), grid=(ntile, nt),
        in_specs=[pl.BlockSpec((bt, D), lambda n, t: (t, 0))] + [seg_spec(m) for m in range(len(segs))],
        out_specs=pl.BlockSpec((None, D, D), lambda n, t: (0, n, 0)),
        scratch=[pltpu.VMEM((D, D), F32)], sem=("parallel", "arbitrary"), vmem_mb=56,
    )(u_all, *segs)


def _du_prenorm_bwd(segs, ddt, wi_main, wi_tail, xin, mods, dres, row_off, tb, name, comm=None):
    n = xin.shape[0]
    nt = n // tb
    off = row_off // tb
    has_dx = dres is not None

    def body(*refs):
        seg_refs = refs[:7]
        ddt_ref, w_ref, wb_ref, wdt_ref, x_ref, mod_ref = refs[7:13]
        rest = refs[13:]
        if has_dx:
            dres_ref, dx_ref, acc_ref, du_scr = rest
        else:
            acc_ref, du_scr = rest
        j, i = pl.program_id(0), pl.program_id(1)
        rows = pl.ds(pl.multiple_of(i * tb, tb), tb)

        @pl.when((i == 0) & (j == 0))
        def _():
            acc_ref[...] = jnp.zeros_like(acc_ref)

        @pl.when(j == 0)
        def _():
            du_scr[rows, :] = _nn(ddt_ref[...], wdt_ref[...])

        for k in range(4):
            if not has_dx and k == 2:
                continue

            @pl.when(j == k)
            def _(k=k):
                if k < 3:
                    sa, sb = seg_refs[2 * k][...], seg_refs[2 * k + 1][...]
                else:
                    sa, sb = seg_refs[6][:, 0:D], seg_refs[6][:, D:2 * D]
                part = _nn(sa, w_ref[0:D, :]) + _nn(sb, w_ref[D:2 * D, :])
                if k > 0:
                    part = part + _nn(sa[:, 0:WTAIL], wb_ref[...])
                du_scr[rows, :] += part

        @pl.when(j == 3)
        def _():
            du = du_scr[rows, :]
            xv = x_ref[...]
            r = lax.rsqrt(jnp.mean(xv * xv, axis=1, keepdims=True) + EPS)
            xh = xv * r
            nw = mod_ref[1:2, :]
            acc_ref[0:1, :] += _colsum(du)
            acc_ref[1:2, :] += _colsum(du * xh * nw)
            dn = du * mod_ref[0:1, :]
            acc_ref[2:3, :] += _colsum(dn * xh)
            if has_dx:
                dxh = dn * nw
                dx_ref[...] = dres_ref[...] + r * (dxh - xh * jnp.mean(dxh * xh, axis=1, keepdims=True))

    def seg_spec(k):
        width = D if k < 6 else 2 * D
        return pl.BlockSpec((tb, width), lambda j, i: (jnp.where(j == min(k // 2, 3), i + off, 0), 0))

    last = pl.BlockSpec((tb, D), lambda j, i: (jnp.where(j == 3, i, 0), 0))
    in_specs = [seg_spec(k) for k in range(7)]
    in_specs += [pl.BlockSpec((tb, 128), lambda j, i: (jnp.where(j == 0, i + off, 0), 0)),
                 pl.BlockSpec((None, 2 * D, D), lambda j, i: (j, 0, 0)),
                 pl.BlockSpec((None, WTAIL, D), lambda j, i: (jnp.maximum(j - 1, 0), 0, 0)),
                 pl.BlockSpec((None, WTAIL, D), lambda j, i: (3, 0, 0))]
    in_specs += [last, _full((8, D))]
    args = list(segs) + [ddt, wi_main, wi_tail, wi_tail, xin, mods]
    out_shape = [jax.ShapeDtypeStruct((8, D), F32)]
    out_specs = [_full((8, D))]
    if has_dx:
        in_specs.append(last)
        args.append(dres)
        out_shape.insert(0, jax.ShapeDtypeStruct((n, D), F32))
        out_specs.insert(0, last)
    call = dict(body=body, args=args, name=name, out_shape=tuple(out_shape), grid=(4, nt), in_specs=in_specs,
                out_specs=tuple(out_specs), scratch=[pltpu.VMEM((n, D), F32)], sem=("arbitrary", "arbitrary"),
                vmem_mb=58)
    steps = lambda: ((pl.program_id(0) == 0) & (pl.program_id(1) == 0),
                     (pl.program_id(0) == 3) & (pl.program_id(1) == nt - 1))
    return _run(_carry(call, comm, steps))


def _sum8(v):
    def body(v_ref, o_ref):
        acc = v_ref[0]
        for k in range(1, 8):
            acc = acc + v_ref[k]
        o_ref[...] = acc

    return _pcall(body, name="small_sum", out_shape=jax.ShapeDtypeStruct(v.shape[1:], F32),
                  in_specs=[pl.BlockSpec(memory_space=pltpu.VMEM)], out_specs=pl.BlockSpec(memory_space=pltpu.VMEM))(v)


def _adamw(w, m, v, g, name, comm=None):
    lead = w.ndim == 3
    rows, cols = w.shape[-2:]
    rb = 256 if rows % 256 == 0 else (352 if rows % 352 == 0 else rows)
    c1 = 1.0 - B1 ** STEP
    c2 = 1.0 - B2 ** STEP

    def body(w_ref, m_ref, v_ref, g_ref, d_ref, nm_ref, nv_ref):
        gv = g_ref[...]
        mn = B1 * m_ref[...] + (1.0 - B1) * gv
        vn = B2 * v_ref[...] + (1.0 - B2) * (gv * gv)
        nm_ref[...] = mn
        nv_ref[...] = vn
        d_ref[...] = -LR * ((mn / c1) / (jnp.sqrt(vn / c2) + AEPS) + WD * w_ref[...])

    if rb == rows and rows > 1024:
        cb, steps = 256, cols // 256
        gspec = pl.BlockSpec((rows, cb), lambda i: (0, i))
        spec = pl.BlockSpec((None, rows, cb), lambda i: (0, 0, i)) if lead else gspec
    else:
        steps = rows // rb
        gspec = pl.BlockSpec((rb, cols), lambda i: (i, 0))
        spec = pl.BlockSpec((None, rb, cols), lambda i: (0, i, 0)) if lead else gspec
    call = dict(body=body, args=[w, m, v, g], name=name, out_shape=(jax.ShapeDtypeStruct(w.shape, F32),) * 3,
                grid=(steps,), in_specs=[spec] * 3 + [gspec], out_specs=(spec,) * 3, scratch=[],
                sem=("arbitrary",) if comm is not None else ("parallel",), vmem_mb=40)
    return _run(_carry(call, comm, lambda: (pl.program_id(0) == 0, pl.program_id(0) == steps - 1,
                                            pl.program_id(0) == steps - 1)))


def _rows(v, n):
    f = v.reshape(-1)
    return jnp.pad(f, (0, n * D - f.shape[0])).reshape(n, D)


def kernel(x, c, ctx, c_ctx, w_ada, b_ada, norm_mix, w_in, conv_w, conv_b, ssd_a_log, ssd_dt_bias, ssd_d, ssd_norm, hgrn_lb_raw, hgrn_norm, w_out, norm_ffn, w_gate, w_up, w_down, final_norm, loss_target, m_c_ctx, m_w_ada, m_b_ada, m_norm_mix, m_w_in, m_conv_w, m_conv_b, m_ssd_a_log, m_ssd_dt_bias, m_ssd_d, m_ssd_norm, m_hgrn_lb_raw, m_hgrn_norm, m_w_out, m_norm_ffn, m_w_gate, m_w_up, m_w_down, m_final_norm, v_c_ctx, v_w_ada, v_b_ada, v_norm_mix, v_w_in, v_conv_w, v_conv_b, v_ssd_a_log, v_ssd_dt_bias, v_ssd_d, v_ssd_norm, v_hgrn_lb_raw, v_hgrn_norm, v_w_out, v_norm_ffn, v_w_gate, v_w_up, v_w_down, v_final_norm):
    ix, iy, ic = lax.axis_index("x"), lax.axis_index("y"), lax.axis_index("c")
    chip = 2 * ix + iy
    me = 2 * chip + ic
    xl, xc, tgt = x[0], ctx[0], loss_target[0]
    n_lat, n_ctx = xl.shape[0], xc.shape[0]
    assert n_ctx == TB and n_lat % 1024 == 0
    t_total = n_lat + n_ctx
    nb = t_total // TB

    tr = lambda a: jnp.swapaxes(a, -1, -2)
    shift = [functools.partial(jnp.pad, pad_width=((8 * k, WSL + WTAIL - NSH - 8 * k), (0, 0))) for k in range(4)]
    slab = lax.switch(chip, shift, tr(w_in[0]).astype(MXU_DTYPE))
    padrows = lambda a: jnp.pad(a, ((0, FSL - DFF // 4), (0, 0))).astype(MXU_DTYPE)
    shards = [slab[:WSL], slab[WSL:], w_out[0].astype(MXU_DTYPE), padrows(tr(w_gate[0])), padrows(tr(w_up[0])),
              padrows(w_down[0])]
    own = lambda g_, s_: lax.dynamic_update_slice(g_, s_[None], (chip, 0, 0))
    pack = jnp.concatenate([c, hgrn_lb_raw.reshape(1, D), _rows(conv_w[0], 3), jnp.zeros((3, D), F32)], axis=0)
    ncol_ada = w_ada.shape[2]
    b_shard = lax.dynamic_slice(b_ada, (0, chip * ncol_ada), (1, ncol_ada))
    gath, araw, mod_all, wi_main, wi_tail = _prologue(pack, c_ctx.reshape(1, D), w_ada[0], b_shard, shards[:2])
    wi_main, wi_tail = own(wi_main, shards[0]), own(wi_tail, shards[1])
    gath = gath.reshape(8, 8, D)
    lbraw_full = gath[0::2, 1].reshape(4, 2, 2, 256).transpose(1, 2, 0, 3).reshape(4, D)
    convw_full = gath[0::2, 2:5].reshape(4, 3 * D)[:, :KCONV * 512].reshape(4, KCONV, 512).transpose(1, 0, 2)
    convw_full = convw_full.reshape(KCONV, 2048)
    lbraw8 = jnp.pad(lbraw_full, ((0, 4), (0, 0)))
    convp = jnp.concatenate([convw_full, conv_b, jnp.zeros((2, 2048), F32)], axis=0)
    dtb = jnp.pad(ssd_dt_bias.reshape(1, 32), ((0, 7), (0, 96)))
    alog = jnp.pad(ssd_a_log.reshape(1, 32), ((0, 7), (0, 96)))
    mod_all = mod_all.reshape(8, 16, ncol_ada)[0::2]
    mod_full = mod_all.transpose(1, 0, 2).reshape(16, 4 * ncol_ada)
    my_mod = lax.dynamic_slice(mod_full, (me, 0), (1, 6 * D)).reshape(6, D)
    sh1, sc1, g1, sh2, sc2, g2 = (my_mod[k:k + 1] for k in range(6))
    csh1, csc1 = mod_full[8:9, 0:D], mod_full[8:9, D:2 * D]

    zrow = jnp.zeros((1, D), F32)
    mods_lat = jnp.concatenate([1.0 + sc1, sh1, norm_mix, zrow, zrow, zrow, zrow, zrow], axis=0)
    mods_ctx = jnp.concatenate([1.0 + csc1, csh1, norm_mix, zrow, zrow, zrow, zrow, zrow], axis=0)
    outs = _inproj(xl, mods_lat, wi_main, wi_tail, t_total, 1024, 0, None, "inproj_lat",
                   comm=_comm_gather(shards[2:5]))
    wo_g, wg_g, wu_g = (own(g_, s_) for g_, s_ in zip(outs[3:], shards[2:5]))
    w_out_f = wo_g.reshape(2 * D, D)
    p_main, p_dt, u_all = _inproj(xc, mods_ctx, wi_main, wi_tail, t_total, TB, nb - 1, outs[:3], "inproj_ctx")

    o_f, hs_f, wd_g = _hgrn_fwd(p_main, lbraw8, 0, nb, comm=_comm_gather(shards[5:]))
    w_down_f = own(wd_g, shards[5]).reshape(DFFP, D)
    o_b, hs_b = _hgrn_fwd(p_main, lbraw8, 1, nb)
    xa, dsl, dts = _ssd_prep(p_main, p_dt, convp, dtb, nb)
    y_f, ss_f = _ssd_fwd(xa, dts, alog, 0, nb)
    y_b, ss_b = _ssd_fwd(xa, dts, alog, 1, nb)

    vec_mix = jnp.concatenate([jnp.tile(hgrn_norm, (1, NH)), jnp.repeat(ssd_d, SP, axis=1), ssd_norm, g1, 1.0 + sc2,
                               sh2, norm_ffn, zrow], axis=0)
    ymix, ylat, h1, u2 = _mix_out(o_f, o_b, p_main, y_f, y_b, xa, xl, vec_mix, w_out_f)
    gate, up, act = _ffn_up(u2, wg_g, wu_g)
    vec_loss = jnp.concatenate([g2, final_norm.reshape(1, D)] + [zrow] * 6, axis=0)
    dh2, dffn, acc_loss = _ffn_down_loss(act, w_down_f, h1, tgt, vec_loss)

    core_arr = jnp.reshape(ic, (1,)).astype(jnp.int32)
    chip_arr = jnp.reshape(chip, (1,)).astype(jnp.int32)
    every = (0, 4)

    def pair_sum(gs, got, tag):
        return list(_pair_sum(gs, list(got), core_arr, "grads_pair_sum_" + tag))

    vec_ffn = jnp.concatenate([g1, 1.0 + sc2, norm_ffn] + [zrow] * 5, axis=0)
    dgate, dup, du2 = _ffn_bwd(dffn, w_down_f, gate, up, wg_g, wu_g)
    dh1, dylat, acc_ffn = _ffn_norm_bwd(du2, h1, ylat, dh2, vec_ffn)
    gw_down = _dw(act, dffn, "dw_down").reshape(4, FSL, D)
    ga1 = [_dw(dgate, u2, "dw_gate").reshape(4, FSL, D), _dw(dup, u2, "dw_up").reshape(4, FSL, D)]
    res = _mix_bwd(dylat, o_f, o_b, p_main, y_f, y_b, xa, vec_mix, w_out_f, comm=_comm_pair(ga1))
    (do, dgr, dys, dzr, dxs_skip, acc_mix), pair_a1 = res[:6], pair_sum(ga1, res[6:], "a1")
    ga2 = [gw_down, _dw(ymix, dylat, "dw_out").reshape(4, D // 2, D)]

    res = _hgrn_bwd(p_main, lbraw8, hs_f, do, 0, nb, None,
                    comm=[_comm_exchange(pair_a1, [every] * 2), _comm_pair(ga2)])
    (dq0, dff, dv0, dlb_f), recv_a, pair_a2 = res[:4], list(res[4:6]), pair_sum(ga2, res[6:], "a2")
    res = _hgrn_bwd(p_main, lbraw8, hs_b, do, 1, nb, (dq0, dv0), comm=_comm_exchange(pair_a2, [every] * 2))
    (dq, dfb, dv, dlb_b), recv_a = res[:4], recv_a + list(res[4:])
    pair_a, dests_a = pair_a1 + pair_a2, [every] * 4
    gw_in = [_dw_in([dq, dff], u_all, "dw_in_0"), _dw_in([dfb, dv], u_all, "dw_in_1"),
             _dw_in([dgr, dzr], u_all, "dw_in_2")]

    res = _ssd_bwd(xa, dts, alog, ss_f, dys, 0, nb, None, comm=_comm_pair(gw_in))
    (dxa0, ddts0, da_f), pair_b, dests_b = res[:3], pair_sum(gw_in, res[3:], "b"), [(0, 1), (1, 2), (2, 3)]
    res = _ssd_bwd(xa, dts, alog, ss_b, dys, 1, nb, (dxa0, ddts0), comm=_comm_exchange(pair_b, dests_b))
    (dxa, ddts, da_b), recv_b = res[:3], list(res[3:])
    dxbc, ddt, acc_conv, acc_dtb = _ssd_prep_bwd(p_main, p_dt, convp, dtb, dsl, dxa, dxs_skip, ddts, nb)
    gw_in.append(_dw_in([dxbc], u_all, "dw_in_3"))
    gw_in_dt = _dw(ddt, u_all, "dw_in_dt")
    gc = [gw_in[3], jnp.concatenate([g_[:, 0:WTAIL, :] for g_ in gw_in[1:]] + [gw_in_dt[None]], axis=0)]

    segs = [dq, dff, dfb, dv, dgr, dzr, dxbc]
    bmods_lat = jnp.concatenate([1.0 + sc1, norm_mix] + [zrow] * 6, axis=0)
    bmods_ctx = jnp.concatenate([1.0 + csc1, norm_mix] + [zrow] * 6, axis=0)
    res = _du_prenorm_bwd(segs, ddt, wi_main, wi_tail, xc, bmods_ctx, None, n_lat, TB, "du_ctx", comm=_comm_pair(gc))
    acc_ctx, pair_c, dests_c = res[0], pair_sum(gc, res[1:], "c"), [(3, 4), every]
    res = _du_prenorm_bwd(segs, ddt, wi_main, wi_tail, xl, bmods_lat, dh1, 0, 512, "du_lat",
                          comm=_comm_exchange(pair_c, dests_c))
    (grad_x, acc_lat), recv_c = res[:2], list(res[2:])

    mine = _chip_sum(pair_b + pair_c + pair_a, recv_b + recv_c + recv_a, chip_arr, dests_b + dests_c + dests_a,
                     [0, 0, 0, 0, 1, 3, 4, 5, 2])
    dmod_lat = jnp.concatenate([acc_lat[0:2], acc_ffn[3:4], acc_ffn[0:2], acc_loss[0:1]], axis=0)
    misc = jnp.concatenate([(da_f + da_b)[0, :32], jnp.zeros((96,), F32), acc_dtb[0, :32], jnp.zeros((96,), F32),
                            jnp.sum(acc_loss[2]).reshape(1), jnp.zeros((D - 257,), F32)]).reshape(1, D)
    sv = jnp.concatenate([
        dmod_lat, acc_ctx[0:2], (acc_lat[2:3] + acc_ctx[2:3]), acc_ffn[2:3], acc_loss[1:2], acc_mix[2:3],
        acc_mix[0:1], acc_mix[1:2], dlb_f[0:1], dlb_b[0:1], acc_conv[0:6].reshape(12, D), misc,
        jnp.zeros((3, D), F32)], axis=0)
    res = _pair_swap(mine, sv)
    theirs, sv_all = res[:-1], res[-1].reshape(8, 32, D)
    whole = [jnp.concatenate([jnp.where(ic == 0, m_, t_), jnp.where(ic == 0, t_, m_)], axis=0)
             for m_, t_ in zip(mine, theirs)]
    g_w_in = lax.dynamic_slice(jnp.concatenate(whole[0:2], axis=0), (8 * chip, 0), (NSH, D))
    g_w_out = whole[2]
    g_w_gate = whole[3][:DFF // 4]
    g_w_up = whole[4][:DFF // 4]
    g_w_down = whole[5][:DFF // 4]
    ssum = _sum8(sv_all)
    dmod_rows = sv_all[:, 0:6].reshape(8, 6 * D)
    dmod_ctx_row = jnp.concatenate([ssum[6:8].reshape(1, 2 * D), jnp.zeros((1, 4 * D), F32)], axis=1)
    dmod_full = jnp.concatenate([dmod_rows, dmod_ctx_row, jnp.zeros((7, 6 * D), F32)], axis=0)
    grad_b_ada = jnp.sum(dmod_full, axis=0, keepdims=True)
    dmod_shard = lax.dynamic_slice(dmod_full, (0, chip * ncol_ada), (16, ncol_ada))
    g_w_ada, da_part = _ada_bwd(araw, dmod_shard, w_ada[0])
    da_all = _allgather8(da_part, "ada_ctx_gather").reshape(8, 16, D)[0::2, 8]

    big = {}
    for nm, w_, m_, v_, g_ in (("w_ada", w_ada, m_w_ada, v_w_ada, g_w_ada), ("w_in", w_in, m_w_in, v_w_in, g_w_in),
                               ("w_out", w_out, m_w_out, v_w_out, g_w_out),
                               ("w_gate", w_gate, m_w_gate, v_w_gate, g_w_gate),
                               ("w_up", w_up, m_w_up, v_w_up, g_w_up),
                               ("w_down", w_down, m_w_down, v_w_down, g_w_down)):
        if nm in ("w_in", "w_gate", "w_up"):
            big[nm] = tuple(tr(t) for t in (g_[None],) + tuple(_adamw(tr(w_), tr(m_), tr(v_), g_, "adamw_" + nm)))
        else:
            big[nm] = (g_[None],) + tuple(_adamw(w_, m_, v_, g_, "adamw_" + nm))
    cc = c_ctx.reshape(1, D)
    grad_c_ctx = (jnp.sum(da_all, axis=0, keepdims=True) * _dsilu(cc)).reshape(D)

    grad_norm_mix, grad_norm_ffn, grad_final_norm = ssum[8:9], ssum[9:10], ssum[10].reshape(D)
    grad_ssd_norm = ssum[11:12]
    grad_hgrn_norm = jnp.sum(ssum[12].reshape(NH, HF), axis=0, keepdims=True)
    grad_ssd_d = jnp.sum(ssum[13].reshape(SHEADS, SP), axis=1).reshape(1, SHEADS)
    lb_full = _sig(lbraw_full[0:2] - lbraw_full[2:4])
    dr0 = ssum[14:16] * lb_full * (1.0 - lb_full)
    grad_lb_full = jnp.stack([dr0, -dr0], axis=0)
    grad_lb = lax.dynamic_slice(grad_lb_full, (0, 0, chip * 256), (2, 2, 256))
    grad_conv_w = lax.dynamic_slice(ssum[16:26].reshape(KCONV, 2048), (0, chip * 512), (KCONV, 512)).reshape(1, KCONV, 512)
    grad_conv_b = ssum[26:28].reshape(1, 2048)
    a_val = -jnp.exp(ssd_a_log)
    grad_a_log = ssum[28, 0:32].reshape(1, 2, SHEADS) * a_val
    grad_dt_bias = ssum[28, 128:160].reshape(1, 2, SHEADS)
    loss = ssum[28, 256]

    small_w = [c_ctx, b_ada, norm_mix, conv_w, conv_b, ssd_a_log, ssd_dt_bias, ssd_d, ssd_norm, hgrn_lb_raw,
               hgrn_norm, norm_ffn, final_norm]
    small_m = [m_c_ctx, m_b_ada, m_norm_mix, m_conv_w, m_conv_b, m_ssd_a_log, m_ssd_dt_bias, m_ssd_d, m_ssd_norm,
               m_hgrn_lb_raw, m_hgrn_norm, m_norm_ffn, m_final_norm]
    small_v = [v_c_ctx, v_b_ada, v_norm_mix, v_conv_w, v_conv_b, v_ssd_a_log, v_ssd_dt_bias, v_ssd_d, v_ssd_norm,
               v_hgrn_lb_raw, v_hgrn_norm, v_norm_ffn, v_final_norm]
    small_g = [grad_c_ctx, grad_b_ada, grad_norm_mix, grad_conv_w, grad_conv_b, grad_a_log, grad_dt_bias, grad_ssd_d,
               grad_ssd_norm, grad_lb, grad_hgrn_norm, grad_norm_ffn, grad_final_norm]
    nrows = [-(-a.size // D) for a in small_w]
    packs = lambda lst: jnp.concatenate([_rows(a, r) for a, r in zip(lst, nrows)]
                                        + [jnp.zeros((24 - sum(nrows), D), F32)], axis=0)
    sd, sm, svv = _adamw(packs(small_w), packs(small_m), packs(small_v), packs(small_g), "adamw_small")

    def unpack(p):
        out, r0 = [], 0
        for a, r in zip(small_w, nrows):
            out.append(p[r0:r0 + r].reshape(-1)[:a.size].reshape(a.shape))
            r0 += r
        return out

    sd, sm, svv = unpack(sd), unpack(sm), unpack(svv)

    order = ["c_ctx", "w_ada", "b_ada", "norm_mix", "w_in", "conv_w", "conv_b", "ssd_a_log", "ssd_dt_bias", "ssd_d",
             "ssd_norm", "hgrn_lb_raw", "hgrn_norm", "w_out", "norm_ffn", "w_gate", "w_up", "w_down", "final_norm"]
    small_names = ["c_ctx", "b_ada", "norm_mix", "conv_w", "conv_b", "ssd_a_log", "ssd_dt_bias", "ssd_d", "ssd_norm",
                   "hgrn_lb_raw", "hgrn_norm", "norm_ffn", "final_norm"]
    table = dict(big)
    for k, nm in enumerate(small_names):
        table[nm] = (small_g[k].reshape(small_w[k].shape), sd[k], sm[k], svv[k])
    grads = [table[nm][0] for nm in order]
    deltas = [table[nm][1] for nm in order]
    new_m = [table[nm][2] for nm in order]
    new_v = [table[nm][3] for nm in order]
    return (loss, grad_x[None], *grads, *deltas, *new_m, *new_v)
```

```python
import functools
import math

import jax
import jax.numpy as jnp
from jax import lax
from jax.experimental import pallas as pl
from jax.experimental.pallas import tpu as pltpu

F32 = jnp.float32
BF16 = jnp.bfloat16
MXU_DTYPE = jnp.bfloat16
_INTERPRET = False

D = 1024
NH, HF = 8, 128
HC = 64
SC = 128
SN = 128
SHEADS, SP = 16, 64
GRID_W = 64
KCONV = 5
DFF = 2816
FSL = 768
DFFP = 4 * FSL
NIN = 8224
TB = 256
EPS = 1e-6
LR, B1, B2, AEPS, WD, STEP = 0.001, 0.9, 0.999, 1e-08, 0.01, 10
MESH_ID = pl.DeviceIdType.MESH
NSH = NIN // 4
WSL = 2048
WTAIL = 128


def _pcall(body, *, name, out_shape, grid=(), in_specs=None, out_specs=None, scratch=(), sem=None,
           vmem_mb=None, aliases=None):
    params = {}
    if sem is not None:
        params["dimension_semantics"] = sem
    if vmem_mb is not None:
        params["vmem_limit_bytes"] = vmem_mb << 20
    kw = dict(name=name, out_shape=out_shape, scratch_shapes=list(scratch),
              input_output_aliases=aliases or {}, compiler_params=pltpu.CompilerParams(**params),
              interpret=_INTERPRET)
    if grid:
        kw["grid"] = grid
    if in_specs is not None:
        kw["in_specs"] = in_specs
    if out_specs is not None:
        kw["out_specs"] = out_specs
    return pl.pallas_call(body, **kw)


def _mx(a):
    return a.astype(MXU_DTYPE)


def _dg(a, b, ca, cb):
    return lax.dot_general(_mx(a), _mx(b), (((ca,), (cb,)), ((), ())), preferred_element_type=F32)


def _nn(a, b):
    return _dg(a, b, 1, 0)


def _nt(a, b):
    return _dg(a, b, 1, 1)


def _tn(a, b):
    return _dg(a, b, 0, 0)


def _dot01(m, x, ways=3):
    f = lambda t: lax.dot_general(m, t, (((1,), (0,)), ((), ())), preferred_element_type=F32)
    hi = x.astype(BF16)
    r1 = x - hi.astype(F32)
    mid = r1.astype(BF16)
    if ways == 2:
        return f(hi) + f(mid)
    lo = (r1 - mid.astype(F32)).astype(BF16)
    return f(hi) + f(mid) + f(lo)


def _tri(n, upper):
    r = lax.broadcasted_iota(jnp.int32, (n, n), 0)
    c = lax.broadcasted_iota(jnp.int32, (n, n), 1)
    return (c >= r) if upper else (c <= r)


def _b01(mask):
    return jnp.where(mask, 1.0, 0.0).astype(BF16)


def _sig(x):
    return jax.nn.sigmoid(x)


def _silu(x):
    return x * _sig(x)


def _dsilu(x):
    s = _sig(x)
    return s * (1.0 + x * (1.0 - s))


def _softplus(x):
    return jnp.maximum(x, 0.0) + jnp.log(1.0 + jnp.exp(-jnp.abs(x)))


def _rowsum(x):
    return jnp.sum(x, axis=1, keepdims=True)


def _colsum(x):
    return jnp.sum(x, axis=0, keepdims=True)


def _full(shape):
    return pl.BlockSpec(shape, lambda *_: (0,) * len(shape))


def _allgather8_phases(x_ref, out_ref, send_sems, recv_sems, local_sem):
    m_per = x_ref.shape[0]
    x, y, c = lax.axis_index("x"), lax.axis_index("y"), lax.axis_index("c")
    me, sibling = (x, y, c), (x, y, 1 - c)
    chips = [(1 - x, y), (x, 1 - y), (1 - x, 1 - y)]

    def rows(px, py, pc):
        return out_ref.at[pl.ds((4 * px + 2 * py + pc) * m_per, m_per), :]

    def copy(k, block, to, src=None):
        return pltpu.make_async_remote_copy(
            src_ref=rows(*block) if src is None else src, dst_ref=rows(*block),
            send_sem=send_sems.at[k], recv_sem=recv_sems.at[k], device_id=to, device_id_type=MESH_ID)

    mine = pltpu.make_async_copy(x_ref, rows(*me), local_sem)
    first = [copy(0, me, sibling, src=x_ref)]
    first += [copy(1 + j, me, (*chip, c), src=x_ref) for j, chip in enumerate(chips)]
    passed = [copy(4 + j, (*chip, c), sibling) for j, chip in enumerate(chips)]

    def start():
        mine.start()
        for cp in first:
            cp.start()

    def forward():
        for j, chip in enumerate(chips):
            copy(1 + j, (*chip, c), me).wait_recv()
            passed[j].start()

    def finish():
        copy(0, sibling, me).wait_recv()
        for j, chip in enumerate(chips):
            copy(4 + j, (*chip, 1 - c), me).wait_recv()
        for cp in first + passed:
            cp.wait_send()
        mine.wait()

    return start, forward, finish


def _allgather8_ops(x_ref, out_ref, send_sems, recv_sems, local_sem):
    for phase in _allgather8_phases(x_ref, out_ref, send_sems, recv_sems, local_sem):
        phase()


def _allgather8(v, name):
    m_per, n = v.shape
    return _pcall(
        functools.partial(_allgather8_ops), name=name, out_shape=jax.ShapeDtypeStruct((8 * m_per, n), v.dtype),
        in_specs=[pl.BlockSpec(memory_space=pltpu.VMEM)], out_specs=pl.BlockSpec(memory_space=pltpu.VMEM),
        scratch=list(_AG8_SEMS),
    )(v)


_AG8_SEMS = [pltpu.SemaphoreType.DMA((7,)), pltpu.SemaphoreType.DMA((7,)), pltpu.SemaphoreType.DMA]


def _prologue(pack, cc_row, w_ada, b_shard, shards):
    n = len(shards)
    ncol = w_ada.shape[1]

    def body(pack_ref, cc_ref, w_ref, b_ref, *refs):
        ins = refs[:n]
        gath_ref, araw_ref, mod_ref = refs[n:n + 3]
        outs = refs[n + 3:2 * n + 3]
        modsh, s1, r1, l1, s2, r2, l2, gs, gr = refs[2 * n + 3:]
        start, forward, finish = _gather_ops(ins, outs, gs, gr, relay=True)
        start()
        _allgather8_ops(pack_ref, gath_ref, s1, r1, l1)
        a = jnp.concatenate([gath_ref[8 * i:8 * i + 1, :] for i in range(8)] + [cc_ref[...], jnp.zeros((7, D), F32)],
                            axis=0)
        araw_ref[...] = a
        modsh[...] = _nn(_silu(a), w_ref[...]) + b_ref[...]
        _allgather8_ops(modsh, mod_ref, s2, r2, l2)
        forward()
        finish()

    vm = pl.BlockSpec(memory_space=pltpu.VMEM)
    anyspec = pl.BlockSpec(memory_space=pl.ANY)
    return _pcall(
        body, name="prologue",
        out_shape=(jax.ShapeDtypeStruct((64, D), F32), jax.ShapeDtypeStruct((16, D), F32),
                   jax.ShapeDtypeStruct((128, ncol), F32)) + _gather_out(shards),
        in_specs=[vm, vm, vm, vm] + [anyspec] * n, out_specs=(vm, vm, vm) + (anyspec,) * n,
        scratch=[pltpu.VMEM((16, ncol), F32)] + list(_AG8_SEMS) + list(_AG8_SEMS) + _gather_sems(n), vmem_mb=40,
    )(pack, cc_row, w_ada, b_shard, *shards)


def _gather_ops(ins, outs, send_sems, recv_sems, relay=False):
    n = len(ins)
    x, y, c = lax.axis_index("x"), lax.axis_index("y"), lax.axis_index("c")
    me, sibling = (x, y, c), (x, y, 1 - c)
    chips = [(1 - x, y), (x, 1 - y), (1 - x, 1 - y)]
    direct = 2 if relay else 3

    def part(a, px, py, pc, quarter=None):
        half = ins[a].shape[0] // 2
        if quarter is None:
            return outs[a].at[2 * px + py, pl.ds(pc * half, half), :]
        return outs[a].at[2 * px + py, pl.ds(pc * half + quarter * (half // 2), half // 2), :]

    def copy(a, k, block, to, src=None, quarter=None):
        return pltpu.make_async_remote_copy(
            src_ref=part(a, *block, quarter) if src is None else src, dst_ref=part(a, *block, quarter),
            send_sem=send_sems.at[8 * a + k], recv_sem=recv_sems.at[8 * a + k], device_id=to,
            device_id_type=MESH_ID)

    def first(a, j):
        half = ins[a].shape[0] // 2
        return copy(a, j, me, (*chips[j], c), src=ins[a].at[pl.ds(c * half, half), :])

    relayed = lambda a, q: copy(a, 6 + q, (*chips[q], c), (*chips[1 - q], c), quarter=q)

    def start():
        for a in range(n):
            for j in range(direct):
                first(a, j).start()

    def forward():
        for a in range(n):
            for j in range(direct):
                copy(a, j, (*chips[j], c), me).wait_recv()
                copy(a, 3 + j, (*chips[j], c), sibling).start()
                if relay:
                    relayed(a, j).start()
            if relay:
                for q in range(2):
                    copy(a, 6 + q, (*chips[2], c), me, quarter=q).wait_recv()
                copy(a, 5, (*chips[2], c), sibling).start()

    def finish():
        for a in range(n):
            for j, chip in enumerate(chips):
                copy(a, 3 + j, (*chip, 1 - c), me).wait_recv()
        for a in range(n):
            for j, chip in enumerate(chips):
                if j < direct:
                    first(a, j).wait_send()
                    if relay:
                        relayed(a, j).wait_send()
                copy(a, 3 + j, (*chip, c), sibling).wait_send()

    return start, forward, finish


def _gather_out(shards):
    return tuple(jax.ShapeDtypeStruct((4,) + s_.shape, s_.dtype) for s_ in shards)


def _gather_sems(n):
    return [pltpu.SemaphoreType.DMA((8 * n,)), pltpu.SemaphoreType.DMA((8 * n,))]


def _pair_ops(ins, outs, send_sems, recv_sems):
    x, y, c = lax.axis_index("x"), lax.axis_index("y"), lax.axis_index("c")
    cps = []
    for a in range(len(ins)):
        half = ins[a].shape[1] // 2
        cps.append(pltpu.make_async_remote_copy(
            src_ref=ins[a].at[:, pl.ds((1 - c) * half, half), :], dst_ref=outs[a], send_sem=send_sems.at[a],
            recv_sem=recv_sems.at[a], device_id=(x, y, 1 - c), device_id_type=MESH_ID))

    def start():
        for cp in cps:
            cp.start()

    def finish():
        for cp in cps:
            cp.wait()

    return start, finish


def _comm_pair(gs):
    n = len(gs)
    return (list(gs), tuple(jax.ShapeDtypeStruct((g.shape[0], g.shape[1] // 2, g.shape[2]), g.dtype) for g in gs),
            [pltpu.SemaphoreType.DMA((n,)), pltpu.SemaphoreType.DMA((n,))], _pair_ops)


def _exchange_ops(ins, outs, send_sems, recv_sems, dests):
    x, y, c = lax.axis_index("x"), lax.axis_index("y"), lax.axis_index("c")
    mine = 2 * x + y
    chips = [(1 - x, y), (x, 1 - y), (1 - x, 1 - y)]

    def each(fn):
        for a in range(len(ins)):
            lo, hi = dests[a]
            for j, (px, py) in enumerate(chips):
                q = 2 * px + py
                cp = pltpu.make_async_remote_copy(
                    src_ref=ins[a].at[jnp.clip(q - lo, 0, hi - lo - 1)], dst_ref=outs[a].at[j],
                    send_sem=send_sems.at[3 * a + j], recv_sem=recv_sems.at[3 * a + j], device_id=(px, py, c),
                    device_id_type=MESH_ID)
                fn(cp, (q >= lo) & (q < hi), (mine >= lo) & (mine < hi), (lo, hi) == (0, 4))

    def start():
        def go(cp, send_ok, recv_ok, always):
            if always:
                cp.start()
            else:
                pl.when(send_ok)(cp.start)
        each(go)

    def finish():
        def go(cp, send_ok, recv_ok, always):
            if always:
                cp.wait()
            else:
                pl.when(send_ok)(cp.wait_send)
                pl.when(recv_ok)(cp.wait_recv)
        each(go)

    return start, finish


def _comm_exchange(hs, dests):
    n = len(hs)
    return (list(hs), tuple(jax.ShapeDtypeStruct((3,) + h.shape[1:], h.dtype) for h in hs),
            [pltpu.SemaphoreType.DMA((3 * n,)), pltpu.SemaphoreType.DMA((3 * n,))],
            lambda i, o, s, r: _exchange_ops(i, o, s, r, dests))


def _comm_gather(shards, relay=False):
    return (list(shards), _gather_out(shards), _gather_sems(len(shards)),
            lambda i, o, s, r: _gather_ops(i, o, s, r, relay))


def _carry(call, comm, steps):
    if comm is None:
        return call
    if isinstance(comm, list):
        for one in comm:
            call = _carry(call, one, steps)
        return call
    arrays, out_shape, sems, make = comm
    n, n_in, n_out = len(arrays), len(call["args"]), len(call["out_shape"])
    body = call["body"]

    def wrapped(*refs):
        base_in, cin = refs[:n_in], refs[n_in:n_in + n]
        rest = refs[n_in + n:]
        base_out, cout, scr = rest[:n_out], rest[n_out:n_out + n], rest[n_out + n:]
        ops = make(cin, cout, scr[-2], scr[-1])
        when = steps()
        pl.when(when[0])(ops[0])
        if len(ops) == 3 and len(when) == 3:
            pl.when(when[2])(ops[1])
        body(*base_in, *base_out, *scr[:-2])
        if len(ops) == 3 and len(when) == 2:
            pl.when(when[1])(ops[1])
        pl.when(when[1])(ops[-1])

    anyspec = pl.BlockSpec(memory_space=pl.ANY)
    return dict(call, body=wrapped, args=list(call["args"]) + arrays,
                in_specs=list(call["in_specs"]) + [anyspec] * n,
                out_shape=tuple(call["out_shape"]) + tuple(out_shape),
                out_specs=tuple(call["out_specs"]) + (anyspec,) * n,
                scratch=list(call["scratch"]) + sems)


def _run(call):
    args = call.pop("args")
    body = call.pop("body")
    return _pcall(body, **call)(*args)


def _pair_swap(rs, sv):
    n = len(rs)

    def body(sv_ref, *refs):
        ins, outs, got_ref = refs[:n], refs[n:2 * n], refs[2 * n]
        send_sems, recv_sems, s1, r1, l1 = refs[2 * n + 1:]
        x, y, c = lax.axis_index("x"), lax.axis_index("y"), lax.axis_index("c")
        cps = [pltpu.make_async_remote_copy(
            src_ref=ins[a], dst_ref=outs[a], send_sem=send_sems.at[a], recv_sem=recv_sems.at[a],
            device_id=(x, y, 1 - c), device_id_type=MESH_ID) for a in range(n)]
        for cp in cps:
            cp.start()
        _allgather8_ops(sv_ref, got_ref, s1, r1, l1)
        for cp in cps:
            cp.wait()

    vm, anyspec = pl.BlockSpec(memory_space=pltpu.VMEM), pl.BlockSpec(memory_space=pl.ANY)
    return _pcall(
        body, name="grads_pair_swap",
        out_shape=tuple(jax.ShapeDtypeStruct(r.shape, r.dtype) for r in rs)
        + (jax.ShapeDtypeStruct((8 * sv.shape[0], sv.shape[1]), sv.dtype),),
        in_specs=[vm] + [anyspec] * n, out_specs=(anyspec,) * n + (vm,),
        scratch=[pltpu.SemaphoreType.DMA((n,)), pltpu.SemaphoreType.DMA((n,))] + list(_AG8_SEMS),
    )(sv, *rs)


SUM_STEPS = 4


def _pair_sum(gs, recvs, core, name):
    n = len(gs)

    def body(c_ref, *refs):
        for a in range(n):
            refs[2 * n + a][...] = (refs[a][...].astype(F32) + refs[n + a][...].astype(F32)).astype(refs[2 * n + a].dtype)

    blk = lambda g: (g.shape[0], g.shape[1] // (2 * SUM_STEPS), g.shape[2])
    return pl.pallas_call(
        body, name=name,
        out_shape=tuple(jax.ShapeDtypeStruct((g.shape[0], g.shape[1] // 2, g.shape[2]), g.dtype) for g in gs),
        grid_spec=pltpu.PrefetchScalarGridSpec(
            num_scalar_prefetch=1, grid=(SUM_STEPS,),
            in_specs=[pl.BlockSpec(blk(g), lambda i, cr: (0, cr[0] * SUM_STEPS + i, 0)) for g in gs]
            + [pl.BlockSpec(blk(g), lambda i, cr: (0, i, 0)) for g in gs],
            out_specs=tuple(pl.BlockSpec(blk(g), lambda i, cr: (0, i, 0)) for g in gs)),
        compiler_params=pltpu.CompilerParams(vmem_limit_bytes=40 << 20), interpret=_INTERPRET,
    )(core, *gs, *recvs)


def _chip_sum(hs, recvs, chip, dests, slots):
    n = len(hs)
    nout = max(slots) + 1
    first = [slots.index(o) for o in range(nout)]
    every = lambda d_: d_ == (0, 4)

    def own(d_):
        if every(d_):
            return lambda i, kr: (kr[0], i, 0)
        return lambda i, kr: (0, jnp.where(kr[0] == d_[0], i, 0), 0)

    def got(d_):
        if every(d_):
            return lambda i, kr: (0, i, 0)
        return lambda i, kr: (0, jnp.where(kr[0] == d_[0], i, 0), 0)

    def body(k_ref, *refs):
        for a in range(n):
            def emit(a=a):
                acc = refs[a][0].astype(F32)
                for j in range(3):
                    acc = acc + refs[n + a][j].astype(F32)
                refs[2 * n + slots[a]][...] = acc
            if every(dests[a]):
                emit()
            else:
                pl.when(k_ref[0] == dests[a][0])(emit)

    rb = lambda h: h.shape[1] // SUM_STEPS
    return pl.pallas_call(
        body, name="grads_chip_sum",
        out_shape=tuple(jax.ShapeDtypeStruct(hs[a].shape[1:], F32) for a in first),
        grid_spec=pltpu.PrefetchScalarGridSpec(
            num_scalar_prefetch=1, grid=(SUM_STEPS,),
            in_specs=[pl.BlockSpec((1, rb(h), h.shape[2]), own(d_)) for h, d_ in zip(hs, dests)]
            + [pl.BlockSpec((3, rb(h), h.shape[2]), got(d_)) for h, d_ in zip(hs, dests)],
            out_specs=tuple(pl.BlockSpec((rb(hs[a]), hs[a].shape[2]), lambda i, kr: (i, 0)) for a in first)),
        compiler_params=pltpu.CompilerParams(vmem_limit_bytes=40 << 20), interpret=_INTERPRET,
    )(chip, *hs, *recvs)


def _ada_bwd(araw, dmod, w):
    nblk = w.shape[1] // 512

    def body(a_ref, d_ref, w_ref, gw_ref, da_ref):
        j = pl.program_id(0)
        gw_ref[...] = _tn(_silu(a_ref[...]), d_ref[...])
        part = _nt(d_ref[...], w_ref[...])

        @pl.when(j == 0)
        def _():
            da_ref[...] = part

        @pl.when(j > 0)
        def _():
            da_ref[...] += part

    return _pcall(
        body, name="ada_bwd",
        out_shape=(jax.ShapeDtypeStruct(w.shape, F32), jax.ShapeDtypeStruct((16, D), F32)), grid=(nblk,),
        in_specs=[_full((16, D)), pl.BlockSpec((16, 512), lambda j: (0, j)), pl.BlockSpec((D, 512), lambda j: (0, j))],
        out_specs=(pl.BlockSpec((D, 512), lambda j: (0, j)), _full((16, D))), sem=("arbitrary",),
    )(araw, dmod, w)


def _w_specs():
    return [pl.BlockSpec((None, 2 * D, D), lambda j, i: (j, 0, 0)),
            pl.BlockSpec((None, WTAIL, D), lambda j, i: (jnp.maximum(j - 1, 0), 0, 0)),
            pl.BlockSpec((None, WTAIL, D), lambda j, i: (3, 0, 0))]


def _inproj(xin, mods, wi_main, wi_tail, t_total, tb, blk_off, prev, name, comm=None):
    n = xin.shape[0]
    nt = n // tb
    nslab = 4

    def body(x_ref, mod_ref, w_ref, wb_ref, wdt_ref, *rest):
        p_ref, pdt_ref, u_ref, uscr = rest[-4:]
        j, i = pl.program_id(0), pl.program_id(1)
        rows = pl.ds(pl.multiple_of(i * tb, tb), tb)

        @pl.when(j == 0)
        def _():
            xv = x_ref[...]
            r = lax.rsqrt(jnp.mean(xv * xv, axis=1, keepdims=True) + EPS)
            u = (xv * r * mod_ref[2:3, :]) * mod_ref[0:1, :] + mod_ref[1:2, :]
            ub = u.astype(MXU_DTYPE)
            uscr[rows, :] = ub
            u_ref[...] = ub
            pdt_ref[...] = _nt(ub, wdt_ref[...])

        ub = uscr[rows, :]
        pv = _nt(ub, w_ref[0:D, :])
        p_ref[:, D:2 * D] = _nt(ub, w_ref[D:2 * D, :]).astype(p_ref.dtype)

        @pl.when(j == 0)
        def _():
            p_ref[:, 0:D] = pv.astype(p_ref.dtype)

        @pl.when(j > 0)
        def _():
            head = pv[:, 0:WTAIL] + _nt(ub, wb_ref[...])
            p_ref[:, 0:D] = jnp.concatenate([head, pv[:, WTAIL:]], axis=1).astype(p_ref.dtype)

    once = lambda j, i: (jnp.where(j == 0, i, nt - 1) + blk_off, 0)
    in_specs = [pl.BlockSpec((tb, D), lambda j, i: (jnp.where(j == 0, i, nt - 1), 0)), _full((8, D))] + _w_specs()
    args = [xin, mods, wi_main, wi_tail, wi_tail]
    aliases = None
    if prev is not None:
        in_specs += [pl.BlockSpec(memory_space=pl.ANY)] * 3
        args += list(prev)
        aliases = {5: 0, 6: 1, 7: 2}
    call = dict(
        body=body, args=args, name=name,
        out_shape=(jax.ShapeDtypeStruct((t_total, nslab * 2 * D), MXU_DTYPE),
                   jax.ShapeDtypeStruct((t_total, 128), F32), jax.ShapeDtypeStruct((t_total, D), MXU_DTYPE)),
        grid=(nslab, nt), in_specs=in_specs,
        out_specs=(pl.BlockSpec((tb, 2 * D), lambda j, i: (i + blk_off, j)), pl.BlockSpec((tb, 128), once),
                   pl.BlockSpec((tb, D), once)),
        scratch=[pltpu.VMEM((n, D), MXU_DTYPE)], sem=("arbitrary", "arbitrary"), vmem_mb=56, aliases=aliases)
    steps = lambda: ((pl.program_id(0) == 0) & (pl.program_id(1) == 0),
                     (pl.program_id(0) == nslab - 1) & (pl.program_id(1) == nt - 1),
                     (pl.program_id(0) == nslab - 1) & (pl.program_id(1) == nt // 2))
    return _run(_carry(call, comm, steps))


def _blk(s, nb, rev):
    return jnp.where(s == 0, nb - 1, (nb - 1 - s) if rev else (s - 1))


def _hgrn_gate(fr, lbraw_ref, d):
    lb = _sig(lbraw_ref[d:d + 1, :] - lbraw_ref[2 + d:3 + d, :])
    sg = _sig(fr)
    return lb, sg, lb + (1.0 - lb) * sg


def _hgrn_fwd(p_main, lbraw, d, nb, comm=None):
    t_total = p_main.shape[0]
    rev = d == 1
    nch = TB // HC
    scale = HF ** -0.5

    def body(q_ref, f_ref, v_ref, lb_ref, o_ref, sp_ref, st):
        s = pl.program_id(0)

        @pl.when(s == 0)
        def _():
            st[...] = jnp.zeros_like(st)

        mb = _tri(HC, rev)
        m01 = _b01(mb)
        order = list(reversed(range(nch)) if rev else range(nch))
        hs_ = [slice(h * HF, (h + 1) * HF) for h in range(NH)]
        pre = {}
        for c in order:
            rows = slice(c * HC, (c + 1) * HC)
            _, _, f = _hgrn_gate(f_ref[rows, :].astype(F32), lb_ref, d)
            k = 1.0 - f
            cum = _dot01(m01, jnp.log(f))
            tot = cum[0:1, :] if rev else cum[HC - 1:HC, :]
            qd = _silu(q_ref[rows, :].astype(F32)) * scale * jnp.exp(cum)
            ki = k * jnp.exp(-cum)
            etot = jnp.exp(tot)
            pre[c] = (_mx(qd), _mx(ki), _mx(ki * etot), _mx(v_ref[rows, :]), etot)
        scs = {c: [_nt(pre[c][0][:, cs], pre[c][1][:, cs]) for cs in hs_] for c in order}
        upd = {c: [_tn(pre[c][3][:, cs], pre[c][2][:, cs]) for cs in hs_] for c in order}
        intra = {c: [_nn(jnp.where(mb, scs[c][h], 0.0), pre[c][3][:, cs]) for h, cs in enumerate(hs_)] for c in order}
        for c in order:
            rows = slice(c * HC, (c + 1) * HC)
            qdb, etot = pre[c][0], pre[c][4]
            for h, cs in enumerate(hs_):
                sth = st[h]
                stb = sth.astype(sp_ref.dtype)
                sp_ref[c, h] = stb
                o_ref[rows, cs] = (intra[c][h] + _nt(qdb[:, cs], stb)).astype(o_ref.dtype)
                st[h] = sth * etot[:, cs] + upd[c][h]

    col = lambda j: (lambda s: (_blk(s, nb, rev), j))
    call = dict(
        body=body, args=[p_main, p_main, p_main, lbraw], name=f"hgrn_fwd_{d}",
        out_shape=(jax.ShapeDtypeStruct((t_total, D), MXU_DTYPE),
                   jax.ShapeDtypeStruct((nch * nb, NH, HF, HF), MXU_DTYPE)),
        grid=(nb,),
        in_specs=[pl.BlockSpec((TB, D), col(0)), pl.BlockSpec((TB, D), col(1 + d)), pl.BlockSpec((TB, D), col(3)),
                  _full((8, D))],
        out_specs=(pl.BlockSpec((TB, D), col(0)),
                   pl.BlockSpec((nch, NH, HF, HF), lambda s: (_blk(s, nb, rev), 0, 0, 0))),
        scratch=[pltpu.VMEM((NH, HF, HF), F32)], sem=("arbitrary",), vmem_mb=40)
    return _run(_carry(call, comm, lambda: (pl.program_id(0) == 0, pl.program_id(0) == nb - 1,
                                            pl.program_id(0) == nb - 4)))


def _hgrn_bwd(p_main, lbraw, sprev, do, d, nb, prev, comm=None):
    t_total = p_main.shape[0]
    rev = d == 1
    nch = TB // HC
    scale = HF ** -0.5
    last = prev is not None
    odt = MXU_DTYPE if last else F32

    def body(q_ref, f_ref, v_ref, lb_ref, sp_ref, do_ref, *rest):
        if last:
            dqp_ref, dvp_ref = rest[:2]
            rest = rest[2:]
        dq_ref, df_ref, dv_ref, dlb_ref, dst = rest
        sp_id = pl.program_id(0)
        is_ctx = sp_id == nb - 1

        @pl.when(sp_id == 0)
        def _():
            dst[...] = jnp.zeros_like(dst)
            dlb_ref[...] = jnp.zeros_like(dlb_ref)

        mb = _tri(HC, rev)
        mbt = _tri(HC, not rev)
        m01 = _b01(mb)
        mt01 = _b01(mbt)
        order = list(range(nch) if rev else reversed(range(nch)))
        hs_ = [slice(h * HF, (h + 1) * HF) for h in range(NH)]
        pre = {}
        for c in order:
            rows = slice(c * HC, (c + 1) * HC)
            lb, sg, f = _hgrn_gate(f_ref[rows, :].astype(F32), lb_ref, d)
            k = 1.0 - f
            cum = _dot01(m01, jnp.log(f))
            tot = cum[0:1, :] if rev else cum[HC - 1:HC, :]
            e = jnp.exp(cum)
            ei = jnp.exp(-cum)
            etot = jnp.exp(tot)
            ee = ei * etot
            qraw = q_ref[rows, :].astype(F32)
            sq = _sig(qraw)
            qd = qraw * sq * scale * e
            ki = k * ei
            ke = k * ee
            dov = jnp.where(is_ctx, 0.0, do_ref[rows, :].astype(F32))
            pre[c] = dict(lb=lb, sg=sg, f=f, e=e, ei=ei, ee=ee, etot=etot, qd=qd, ki=ki, ke=ke,
                          dsq=sq * (1.0 + qraw * (1.0 - sq)),
                          qdb=_mx(qd), kib=_mx(ki), keb=_mx(ke), vb=_mx(v_ref[rows, :]), dob=_mx(dov))
        units = [(c, h) for c in order for h in range(NH)]
        col = lambda u, key: pre[u[0]][key][:, hs_[u[1]]]
        pt = {u: jnp.where(mbt, _nt(col(u, "kib"), col(u, "qdb")), 0.0) for u in units}
        dp = {u: jnp.where(mb, _nt(col(u, "dob"), col(u, "vb")), 0.0) for u in units}
        dpt = {u: jnp.where(mbt, _nt(col(u, "vb"), col(u, "dob")), 0.0) for u in units}
        dv_i = {u: _nn(pt[u], col(u, "dob")) for u in units}
        dqd_ = {u: _nn(dp[u], col(u, "kib")) + _nn(col(u, "dob"), sp_ref[u[0], u[1]]) for u in units}
        dki_ = {u: _nn(dpt[u], col(u, "qdb")) for u in units}
        dsl = {u: _tn(col(u, "dob"), col(u, "qdb")) for u in units}
        for c in order:
            rows = slice(c * HC, (c + 1) * HC)
            p = pre[c]
            dv_l, dke_l, dtot_l = [], [], []
            for h, cs in enumerate(hs_):
                dso = dst[h]
                dsob = _mx(dso)
                dv_l.append(dv_i[(c, h)] + _nt(p["keb"][:, cs], dsob))
                dke_l.append(_nn(p["vb"][:, cs], dsob))
                dtot_l.append(_colsum(dso * sp_ref[c, h].astype(F32)) * p["etot"][:, cs])
                dst[h] = dso * p["etot"][:, cs] + dsl[(c, h)]
            lb, sg, f, e, ei, ee, qd, ki, ke = (p[n_] for n_ in ("lb", "sg", "f", "e", "ei", "ee", "qd", "ki", "ke"))
            dqd = jnp.concatenate([dqd_[(c, h)] for h in range(NH)], axis=1)
            dki = jnp.concatenate([dki_[(c, h)] for h in range(NH)], axis=1)
            dke = jnp.concatenate(dke_l, axis=1)
            dcum = dqd * qd - dki * ki - dke * ke
            dtot = jnp.concatenate(dtot_l, axis=1) + _colsum(dke * ke)
            dk = dki * ei + dke * ee
            dlf = _dot01(mt01, dcum, ways=2) + dtot
            df = dlf / f - dk
            dlb_ref[0:1, :] += _colsum(df * (1.0 - sg))
            dfr = df * (1.0 - lb) * sg * (1.0 - sg)
            dq = dqd * e * scale * p["dsq"]
            dv = jnp.concatenate(dv_l, axis=1)
            if last:
                dq = dq + dqp_ref[rows, :]
                dv = dv + dvp_ref[rows, :]
            dq_ref[rows, :] = dq.astype(odt)
            dv_ref[rows, :] = dv.astype(odt)
            df_ref[rows, :] = dfr.astype(MXU_DTYPE)

    blk = lambda s: _blk(nb - 1 - s, nb, rev)
    col = lambda j: (lambda s: (blk(s), j))
    in_specs = [pl.BlockSpec((TB, D), col(0)), pl.BlockSpec((TB, D), col(1 + d)), pl.BlockSpec((TB, D), col(3)),
                _full((8, D)), pl.BlockSpec((nch, NH, HF, HF), lambda s: (blk(s), 0, 0, 0)),
                pl.BlockSpec((TB, D), lambda s: (jnp.minimum(blk(s), nb - 2), 0))]
    args = [p_main, p_main, p_main, lbraw, sprev, do]
    if last:
        in_specs += [pl.BlockSpec((TB, D), col(0))] * 2
        args += list(prev)
    call = dict(
        body=body, args=args, name=f"hgrn_bwd_{d}",
        out_shape=(jax.ShapeDtypeStruct((t_total, D), odt), jax.ShapeDtypeStruct((t_total, D), MXU_DTYPE),
                   jax.ShapeDtypeStruct((t_total, D), odt), jax.ShapeDtypeStruct((8, D), F32)),
        grid=(nb,), in_specs=in_specs,
        out_specs=(pl.BlockSpec((TB, D), col(0)), pl.BlockSpec((TB, D), col(0)), pl.BlockSpec((TB, D), col(0)),
                   _full((8, D))),
        scratch=[pltpu.VMEM((NH, HF, HF), F32)], sem=("arbitrary",), vmem_mb=48)
    return _run(_carry(call, comm, lambda: (pl.program_id(0) == 0, pl.program_id(0) == nb - 1)))


def _conv_masks(tb, is_ctx):
    seg = jnp.where(is_ctx, tb, GRID_W)
    pos = lax.broadcasted_iota(jnp.int32, (tb, 1), 0) & (seg - 1)
    return pos, seg


def _shift_rows(x, dshift, pos, seg):
    if dshift == 0:
        return x
    n = x.shape[0]
    rolled = pltpu.roll(x, (-dshift) % n, 0)
    ok = (pos + dshift >= 0) & (pos + dshift < seg)
    return jnp.where(ok, rolled, 0.0)


def _ssd_prep(p_main, p_dt, convp, dtb, nb):
    t_total = p_main.shape[0]

    def body(x_ref, dt_ref, cw_ref, dtb_ref, xa_ref, ds_ref, dts_ref):
        is_ctx = pl.program_id(0) == nb - 1
        pos, seg = _conv_masks(TB, is_ctx)
        xv = x_ref[...].astype(F32)
        acc = cw_ref[5:6, :] + cw_ref[2:3, :] * xv
        for kk in (0, 1, 3, 4):
            acc = acc + cw_ref[kk:kk + 1, :] * _shift_rows(xv, kk - 2, pos, seg)
        sg = _sig(acc)
        xa_ref[...] = (acc * sg).astype(xa_ref.dtype)
        ds_ref[...] = (sg * (1.0 + acc * (1.0 - sg))).astype(ds_ref.dtype)
        dts_ref[...] = _softplus(dt_ref[...] + dtb_ref[0:1, :])

    wide = pl.BlockSpec((TB, 2048), lambda i: (i, 0))
    return _pcall(
        body, name="ssd_prep",
        out_shape=(jax.ShapeDtypeStruct((t_total, 2048), MXU_DTYPE), jax.ShapeDtypeStruct((t_total, 2048), MXU_DTYPE),
                   jax.ShapeDtypeStruct((t_total, 128), F32)),
        grid=(nb,),
        in_specs=[pl.BlockSpec((TB, 2048), lambda i: (i, 3)), pl.BlockSpec((TB, 128), lambda i: (i, 0)),
                  _full((8, 2048)), _full((8, 128))],
        out_specs=(wide, wide, pl.BlockSpec((TB, 128), lambda i: (i, 0))),
        sem=("parallel",), vmem_mb=32,
    )(p_main, p_dt, convp, dtb)


def _ssd_prep_bwd(p_main, p_dt, convp, dtb, dsl, dxa, dxs_skip, ddts, nb):
    t_total = p_main.shape[0]

    def body(x_ref, dt_ref, cw_ref, dtb_ref, ds_ref, dxa_ref, dsk_ref, ddts_ref, dx_ref, ddt_ref, dcw_ref, ddtb_ref):
        i = pl.program_id(0)
        is_ctx = i == nb - 1

        @pl.when(i == 0)
        def _():
            dcw_ref[...] = jnp.zeros_like(dcw_ref)
            ddtb_ref[...] = jnp.zeros_like(ddtb_ref)

        pos, seg = _conv_masks(TB, is_ctx)
        xv = x_ref[...].astype(F32)
        dact = dxa_ref[...]
        dact = jnp.concatenate([dact[:, :D] + jnp.where(is_ctx, 0.0, dsk_ref[...].astype(F32)), dact[:, D:]], axis=1)
        dpre = dact * ds_ref[...].astype(F32)
        dxv = cw_ref[2:3, :] * dpre
        dcw_ref[2:3, :] += _colsum(xv * dpre)
        for kk in (0, 1, 3, 4):
            sdp = _shift_rows(dpre, 2 - kk, pos, seg)
            dxv = dxv + cw_ref[kk:kk + 1, :] * sdp
            dcw_ref[kk:kk + 1, :] += _colsum(xv * sdp)
        dx_ref[...] = dxv.astype(dx_ref.dtype)
        dcw_ref[5:6, :] += _colsum(dpre)
        draw = ddts_ref[...] * _sig(dt_ref[...] + dtb_ref[0:1, :])
        ddt_ref[...] = draw.astype(ddt_ref.dtype)
        ddtb_ref[0:1, :] += _colsum(draw)

    return _pcall(
        body, name="ssd_prep_bwd",
        out_shape=(jax.ShapeDtypeStruct((t_total, 2048), MXU_DTYPE), jax.ShapeDtypeStruct((t_total, 128), MXU_DTYPE),
                   jax.ShapeDtypeStruct((8, 2048), F32), jax.ShapeDtypeStruct((8, 128), F32)),
        grid=(nb,),
        in_specs=[pl.BlockSpec((TB, 2048), lambda i: (i, 3)), pl.BlockSpec((TB, 128), lambda i: (i, 0)),
                  _full((8, 2048)), _full((8, 128)), pl.BlockSpec((TB, 2048), lambda i: (i, 0)),
                  pl.BlockSpec((TB, 2048), lambda i: (i, 0)),
                  pl.BlockSpec((TB, D), lambda i: (jnp.minimum(i, nb - 2), 0)),
                  pl.BlockSpec((TB, 128), lambda i: (i, 0))],
        out_specs=(pl.BlockSpec((TB, 2048), lambda i: (i, 0)), pl.BlockSpec((TB, 128), lambda i: (i, 0)),
                   _full((8, 2048)), _full((8, 128))),
        sem=("arbitrary",), vmem_mb=40,
    )(p_main, p_dt, convp, dtb, dsl, dxa, dxs_skip, ddts)


def _dot2(x, m01):
    hi = x.astype(BF16)
    lo = (x - hi.astype(F32)).astype(BF16)
    f = lambda t: lax.dot_general(t, m01, (((1,), (0,)), ((), ())), preferred_element_type=F32)
    return f(hi) + f(lo)


def _head_lanes(c0, c1):
    p = lax.broadcasted_iota(jnp.int32, (128, 128), 0)
    l = lax.broadcasted_iota(jnp.int32, (128, 128), 1)
    return _b01(((l == c0) & (p < SP)) | ((l == c1) & (p >= SP)))


def _one_lane(col):
    return _b01(lax.broadcasted_iota(jnp.int32, (128, 128), 1) == col)


def _lane_pick(x, lane, col):
    return _rowsum(jnp.where(lane == col, x, 0.0))


def _ssd_chunk_common(dts, alog_ref, m01, rev):
    lane = lax.broadcasted_iota(jnp.int32, (1, 128), 1)
    arow = -jnp.exp(alog_ref[0:1, :])
    cum = _dot01(m01, dts * arow)
    tot = cum[0:1, :] if rev else cum[SC - 1:SC, :]
    return lane, arow, cum, cum.T, tot


def _ssd_fwd(xa, dts, alog, d, nb):
    t_total = xa.shape[0]
    rev = d == 1
    nch = TB // SC
    npair = SHEADS // 2

    def body(xa_ref, dts_ref, alog_ref, y_ref, sp_ref, st):
        s = pl.program_id(0)

        @pl.when(s == 0)
        def _():
            st[...] = jnp.zeros_like(st)

        mb = _tri(SC, rev)
        m01 = _b01(mb)
        lo = lax.broadcasted_iota(jnp.int32, (1, 128), 1) < SP
        rlo = lax.broadcasted_iota(jnp.int32, (128, 1), 0) < SP
        order = list(reversed(range(nch)) if rev else range(nch))
        pre = {}
        for c in order:
            rows = slice(c * SC, (c + 1) * SC)
            dts_c = dts_ref[rows, :]
            lane, arow, cum, cumt, tot = _ssd_chunk_common(dts_c, alog_ref, m01, rev)
            bgs = [_mx(xa_ref[rows, D + g * SN:D + (g + 1) * SN]) for g in range(4)]
            cgs = [_mx(xa_ref[rows, D + 512 + g * SN:D + 512 + (g + 1) * SN]) for g in range(4)]
            pairs = []
            for pr in range(npair):
                xs = xa_ref[rows, pr * 128:(pr + 1) * 128].astype(F32)
                cols = [16 * d + 2 * pr, 16 * d + 2 * pr + 1]
                cum_c = [_lane_pick(cum, lane, q) for q in cols]
                dt_c = [_lane_pick(dts_c, lane, q) for q in cols]
                tot_c = [_lane_pick(tot, lane, q) for q in cols]
                dtx = xs * jnp.where(lo, dt_c[0], dt_c[1])
                e1_pair = jnp.where(lo, jnp.exp(cum_c[0]), jnp.exp(cum_c[1]))
                e2_pair = jnp.where(lo, jnp.exp(tot_c[0] - cum_c[0]), jnp.exp(tot_c[1] - cum_c[1]))
                etot_col = jnp.where(rlo, jnp.exp(tot_c[0]), jnp.exp(tot_c[1]))
                decs = [jnp.where(mb, jnp.exp(cum_c[q] - cumt[cols[q]:cols[q] + 1, :]), 0.0) for q in range(2)]
                dtxq = [_mx(jnp.where(lo if q == 0 else ~lo, dtx, 0.0)) for q in range(2)]
                pairs.append(dict(e1=e1_pair, etot=etot_col, decs=decs, dtxq=dtxq, xe=_mx(dtx * e2_pair)))
            pre[c] = (bgs, cgs, pairs)
        gm = {(c, g): _nt(pre[c][1][g], pre[c][0][g]) for c in order for g in range(4)}
        upd = {(c, pr): _tn(pre[c][2][pr]["xe"], pre[c][0][pr // 2]) for c in order for pr in range(npair)}
        intra = {(c, pr): sum(_nn(gm[(c, pr // 2)] * pre[c][2][pr]["decs"][q], pre[c][2][pr]["dtxq"][q]) for q in range(2))
                 for c in order for pr in range(npair)}
        for c in order:
            rows = slice(c * SC, (c + 1) * SC)
            bgs, cgs, pairs = pre[c]
            for pr in range(npair):
                stp = st[pr]
                stb = stp.astype(sp_ref.dtype)
                sp_ref[c, pr] = stb
                y_ref[rows, pr * 128:(pr + 1) * 128] = (
                    intra[(c, pr)] + pairs[pr]["e1"] * _nt(cgs[pr // 2], stb)).astype(y_ref.dtype)
                st[pr] = stp * pairs[pr]["etot"] + upd[(c, pr)]

    blk = lambda s: _blk(s, nb, rev)
    return _pcall(
        body, name=f"ssd_fwd_{d}",
        out_shape=(jax.ShapeDtypeStruct((t_total, D), MXU_DTYPE),
                   jax.ShapeDtypeStruct((nch * nb, npair, 128, SN), MXU_DTYPE)),
        grid=(nb,),
        in_specs=[pl.BlockSpec((TB, 2048), lambda s: (blk(s), 0)), pl.BlockSpec((TB, 128), lambda s: (blk(s), 0)),
                  _full((8, 128))],
        out_specs=(pl.BlockSpec((TB, D), lambda s: (blk(s), 0)),
                   pl.BlockSpec((nch, npair, 128, SN), lambda s: (blk(s), 0, 0, 0))),
        scratch=[pltpu.VMEM((npair, 128, SN), F32)], sem=("arbitrary",), vmem_mb=40,
    )(xa, dts, alog)


def _ssd_bwd(xa, dts, alog, sprev, dy, d, nb, prev, comm=None):
    t_total = xa.shape[0]
    rev = d == 1
    nch = TB // SC
    npair = SHEADS // 2
    last = prev is not None

    def body(xa_ref, dts_ref, alog_ref, sp_ref, dy_ref, *rest):
        if last:
            dxp_ref, ddp_ref = rest[:2]
            rest = rest[2:]
        dxa_ref, ddts_ref, da_ref, dst, zc_scr = rest
        sp_id = pl.program_id(0)
        is_ctx = sp_id == nb - 1

        @pl.when(sp_id == 0)
        def _():
            dst[...] = jnp.zeros_like(dst)
            da_ref[...] = jnp.zeros_like(da_ref)
            zc_scr[...] = jnp.zeros_like(zc_scr)

        mb = _tri(SC, rev)
        m01 = _b01(mb)
        mt01 = _b01(_tri(SC, not rev))
        lo = lax.broadcasted_iota(jnp.int32, (1, 128), 1) < SP
        rlo = lax.broadcasted_iota(jnp.int32, (128, 1), 0) < SP
        order = list(range(nch) if rev else reversed(range(nch)))
        pre = {}
        for c in order:
            rows = slice(c * SC, (c + 1) * SC)
            dts_c = dts_ref[rows, :]
            lane, arow, cum, cumt, tot = _ssd_chunk_common(dts_c, alog_ref, m01, rev)
            pairs = []
            for pr in range(npair):
                xs = xa_ref[rows, pr * 128:(pr + 1) * 128].astype(F32)
                dyp = jnp.where(is_ctx, 0.0, dy_ref[rows, pr * 128:(pr + 1) * 128].astype(F32))
                cols = [16 * d + 2 * pr, 16 * d + 2 * pr + 1]
                cum_c = [_lane_pick(cum, lane, q) for q in cols]
                dt_c = [_lane_pick(dts_c, lane, q) for q in cols]
                tot_c = [_lane_pick(tot, lane, q) for q in cols]
                e1_c = [jnp.exp(cum_c[q]) for q in range(2)]
                e2_c = [jnp.exp(tot_c[q] - cum_c[q]) for q in range(2)]
                etot_c = [jnp.exp(tot_c[q]) for q in range(2)]
                dt_pair = jnp.where(lo, dt_c[0], dt_c[1])
                e1_pair = jnp.where(lo, e1_c[0], e1_c[1])
                e2_pair = jnp.where(lo, e2_c[0], e2_c[1])
                dtx = xs * dt_pair
                decs = [jnp.where(mb, jnp.exp(cum_c[q] - cumt[cols[q]:cols[q] + 1, :]), 0.0) for q in range(2)]
                dyq = [_mx(jnp.where(lo if q == 0 else ~lo, dyp, 0.0)) for q in range(2)]
                pairs.append(dict(xs=xs, dyp=dyp, cols=cols, e1_c=e1_c, e2_c=e2_c, etot_c=etot_c, dt_pair=dt_pair,
                                  e2_pair=e2_pair, etot_col=jnp.where(rlo, etot_c[0], etot_c[1]), dtx=dtx,
                                  dtxb=_mx(dtx), xeb=_mx(dtx * e2_pair), dy0b=_mx(dyp * e1_pair), decs=decs, dyq=dyq))
            pre[c] = dict(lane=lane, arow=arow, dts=dts_c, pairs=pairs, cum=cum, tot=tot,
                          bgb=[_mx(xa_ref[rows, D + g * SN:D + (g + 1) * SN]) for g in range(4)],
                          cgb=[_mx(xa_ref[rows, D + 512 + g * SN:D + 512 + (g + 1) * SN]) for g in range(4)])
        units = [(c, pr) for c in order for pr in range(npair)]
        head_lanes = [_head_lanes(16 * d + 2 * pr, 16 * d + 2 * pr + 1) for pr in range(npair)]
        one_lane = {16 * d + h: _one_lane(16 * d + h) for h in range(SHEADS)}
        P = lambda u: pre[u[0]]["pairs"][u[1]]
        cgu = lambda u: pre[u[0]]["cgb"][u[1] // 2]
        gm = {(c, g): _nt(pre[c]["cgb"][g], pre[c]["bgb"][g]) for c in order for g in range(4)}
        y0 = {u: _nt(cgu(u), sp_ref[u[0], u[1]]) for u in units}
        dcg_i = {u: _nn(P(u)["dy0b"], sp_ref[u[0], u[1]]) for u in units}
        dsl = {u: _tn(P(u)["dy0b"], cgu(u)) for u in units}
        w_ = {(u, q): gm[(u[0], u[1] // 2)] * P(u)["decs"][q] for u in units for q in range(2)}
        dw_ = {(u, q): jnp.where(mb, _nt(P(u)["dyq"][q], P(u)["dtxb"]), 0.0) for u in units for q in range(2)}
        ddtx_i = {(u, q): _tn(w_[(u, q)], P(u)["dyq"][q]) for u in units for q in range(2)}
        for c in order:
            rows = slice(c * SC, (c + 1) * SC)
            pc = pre[c]
            lane, arow, dts_c = pc["lane"], pc["arow"], pc["dts"]
            d1 = jnp.zeros((SC, 128), F32)
            d2 = jnp.zeros((SC, 128), F32)
            dz = jnp.zeros((SC, 128), F32)
            ddt = jnp.zeros((SC, 128), F32)
            dtot = jnp.zeros((1, 128), F32)
            dgm = [jnp.zeros((SC, SC), F32) for _ in range(4)]
            dbg = [jnp.zeros((SC, SN), F32) for _ in range(4)]
            dcg = [jnp.zeros((SC, SN), F32) for _ in range(4)]
            for pr in range(npair):
                u, g, p = (c, pr), pr // 2, pc["pairs"][pr]
                hs = head_lanes[pr]
                dso = dst[pr]
                dsob = _mx(dso)
                dxe = _nt(pc["bgb"][g], dsob)
                dbg[g] = dbg[g] + _nn(p["xeb"], dsob)
                ddtx = dxe * p["e2_pair"]
                d2 = d2 + _dot2(dxe * p["dtx"], hs)
                dcg[g] = dcg[g] + dcg_i[u]
                d1 = d1 + _dot2(p["dyp"] * y0[u], hs)
                sprod = dso * sp_ref[c, pr].astype(F32)
                dst[pr] = dso * p["etot_col"] + dsl[u]
                for q in range(2):
                    hm = lo if q == 0 else ~lo
                    col = p["cols"][q]
                    dw = dw_[(u, q)]
                    ddtx = ddtx + jnp.where(hm, ddtx_i[(u, q)], 0.0)
                    dgm[g] = dgm[g] + dw * p["decs"][q]
                    z = dw * w_[(u, q)]
                    dz = dz + _dot2(z, one_lane[col])
                    zc_scr[col:col + 1, :] = _colsum(z)
                    tsum = _rowsum(_colsum(sprod[q * SP:(q + 1) * SP, :]))
                    dtot = jnp.where(lane == col, tsum * p["etot_c"][q], dtot)
                dxs = ddtx * p["dt_pair"]
                ddt = ddt + _dot2(ddtx * p["xs"], hs)
                if last:
                    dxs = dxs + dxp_ref[rows, pr * 128:(pr + 1) * 128]
                dxa_ref[rows, pr * 128:(pr + 1) * 128] = dxs
            e2_all = jnp.exp(pc["tot"] - pc["cum"])
            dcum = dz - zc_scr[...].T + d1 * jnp.exp(pc["cum"]) - d2 * e2_all
            dtot = dtot + _colsum(d2 * e2_all)
            for g in range(4):
                db = dbg[g] + _tn(dgm[g], pc["cgb"][g])
                dc = dcg[g] + _nn(dgm[g], pc["bgb"][g])
                if last:
                    db = db + dxp_ref[rows, D + g * SN:D + (g + 1) * SN]
                    dc = dc + dxp_ref[rows, D + 512 + g * SN:D + 512 + (g + 1) * SN]
                dxa_ref[rows, D + g * SN:D + (g + 1) * SN] = db
                dxa_ref[rows, D + 512 + g * SN:D + 512 + (g + 1) * SN] = dc
            dla = _dot01(mt01, dcum, ways=2) + dtot
            ddt = ddt + dla * arow
            da_ref[0:1, :] += _colsum(dla * dts_c)
            if last:
                ddt = ddt + ddp_ref[rows, :]
            ddts_ref[rows, :] = ddt

    blk = lambda s: _blk(nb - 1 - s, nb, rev)
    in_specs = [pl.BlockSpec((TB, 2048), lambda s: (blk(s), 0)), pl.BlockSpec((TB, 128), lambda s: (blk(s), 0)),
                _full((8, 128)), pl.BlockSpec((nch, npair, 128, SN), lambda s: (blk(s), 0, 0, 0)),
                pl.BlockSpec((TB, D), lambda s: (jnp.minimum(blk(s), nb - 2), 0))]
    args = [xa, dts, alog, sprev, dy]
    if last:
        in_specs += [pl.BlockSpec((TB, 2048), lambda s: (blk(s), 0)), pl.BlockSpec((TB, 128), lambda s: (blk(s), 0))]
        args += list(prev)
    call = dict(
        body=body, args=args, name=f"ssd_bwd_{d}",
        out_shape=(jax.ShapeDtypeStruct((t_total, 2048), F32), jax.ShapeDtypeStruct((t_total, 128), F32),
                   jax.ShapeDtypeStruct((8, 128), F32)),
        grid=(nb,), in_specs=in_specs,
        out_specs=(pl.BlockSpec((TB, 2048), lambda s: (blk(s), 0)), pl.BlockSpec((TB, 128), lambda s: (blk(s), 0)),
                   _full((8, 128))),
        scratch=[pltpu.VMEM((npair, 128, SN), F32), pltpu.VMEM((128, 128), F32)], sem=("arbitrary",), vmem_mb=48)
    return _run(_carry(call, comm, lambda: (pl.program_id(0) == 0, pl.program_id(0) == nb - 1)))


def _readout(o, g, yy, z, vec_ref):
    hg, ss, keep = [], [], []
    for h in range(NH):
        cs = slice(h * HF, (h + 1) * HF)
        oh = o[:, cs]
        r = lax.rsqrt(jnp.mean(oh * oh, axis=1, keepdims=True) + EPS)
        hg.append(oh * r * vec_ref[0:1, cs] * _silu(g[:, cs]))
        keep.append(r)
    u = yy * _silu(z)
    for gi in range(4):
        cs = slice(gi * 256, (gi + 1) * 256)
        ug = u[:, cs]
        r = lax.rsqrt(jnp.mean(ug * ug, axis=1, keepdims=True) + EPS)
        ss.append(ug * r * vec_ref[2:3, cs])
        keep.append(r)
    return jnp.concatenate(hg, axis=1), jnp.concatenate(ss, axis=1), keep, u


def _mix_out(o_f, o_b, p_main, y_f, y_b, xa, x, vecs, w_out):
    n = x.shape[0]

    def body(of_ref, ob_ref, g_ref, z_ref, yf_ref, yb_ref, xs_ref, x_ref, vec_ref, w_ref,
             ymix_ref, ylat_ref, h1_ref, u2_ref):
        o = of_ref[...].astype(F32) + ob_ref[...].astype(F32)
        yy = yf_ref[...].astype(F32) + yb_ref[...].astype(F32) + vec_ref[1:2, :] * xs_ref[...].astype(F32)
        hg, ss, _, _ = _readout(o, g_ref[...].astype(F32), yy, z_ref[...].astype(F32), vec_ref)
        ymix = jnp.concatenate([hg, ss], axis=1).astype(MXU_DTYPE)
        ymix_ref[...] = ymix
        ylat = _nn(ymix, w_ref[...])
        ylat_ref[...] = ylat
        h1 = x_ref[...] + vec_ref[3:4, :] * ylat
        h1_ref[...] = h1
        r = lax.rsqrt(jnp.mean(h1 * h1, axis=1, keepdims=True) + EPS)
        u2_ref[...] = ((h1 * r * vec_ref[6:7, :]) * vec_ref[4:5, :] + vec_ref[5:6, :]).astype(MXU_DTYPE)

    row = lambda j: (lambda i: (i, j))
    return _pcall(
        body, name="mix_out",
        out_shape=(jax.ShapeDtypeStruct((n, 2 * D), MXU_DTYPE), jax.ShapeDtypeStruct((n, D), F32),
                   jax.ShapeDtypeStruct((n, D), F32), jax.ShapeDtypeStruct((n, D), MXU_DTYPE)),
        grid=(n // TB,),
        in_specs=[pl.BlockSpec((TB, D), row(0)), pl.BlockSpec((TB, D), row(0)), pl.BlockSpec((TB, D), row(4)),
                  pl.BlockSpec((TB, D), row(5)), pl.BlockSpec((TB, D), row(0)), pl.BlockSpec((TB, D), row(0)),
                  pl.BlockSpec((TB, D), row(0)), pl.BlockSpec((TB, D), row(0)), _full((8, D)), _full((2 * D, D))],
        out_specs=(pl.BlockSpec((TB, 2 * D), row(0)), pl.BlockSpec((TB, D), row(0)), pl.BlockSpec((TB, D), row(0)),
                   pl.BlockSpec((TB, D), row(0))),
        sem=("parallel",), vmem_mb=48,
    )(o_f, o_b, p_main, p_main, y_f, y_b, xa, x, vecs, w_out)


def _mix_bwd(dylat, o_f, o_b, p_main, y_f, y_b, xa, vecs, w_out, comm=None):
    n = dylat.shape[0]
    t_total = p_main.shape[0]
    nlat = n // TB

    def body(*refs):
        dg_ref, dz_ref, acc_ref = refs[11], refs[13], refs[15]
        i = pl.program_id(0)

        @pl.when(i == 0)
        def _():
            acc_ref[...] = jnp.zeros_like(acc_ref)

        @pl.when(i < nlat)
        def _():
            compute(*refs)

        @pl.when(i == nlat)
        def _():
            dg_ref[...] = jnp.zeros_like(dg_ref)
            dz_ref[...] = jnp.zeros_like(dz_ref)

    def compute(dyl_ref, of_ref, ob_ref, g_ref, z_ref, yf_ref, yb_ref, xs_ref, vec_ref, w_ref,
                do_ref, dg_ref, dys_ref, dz_ref, dxs_ref, acc_ref):
        dymix = _nt(dyl_ref[...], w_ref[...])
        o = of_ref[...].astype(F32) + ob_ref[...].astype(F32)
        g = g_ref[...].astype(F32)
        z = z_ref[...].astype(F32)
        xs = xs_ref[...].astype(F32)
        yy = yf_ref[...].astype(F32) + yb_ref[...].astype(F32) + vec_ref[1:2, :] * xs
        _, _, keep, u = _readout(o, g, yy, z, vec_ref)
        do_l, dg_l = [], []
        for h in range(NH):
            cs = slice(h * HF, (h + 1) * HF)
            oh, gh, r, wv = o[:, cs], g[:, cs], keep[h], vec_ref[0:1, cs]
            dhg = dymix[:, cs]
            xh = oh * r
            dn = dhg * _silu(gh)
            dg_l.append(dhg * xh * wv * _dsilu(gh))
            acc_ref[0:1, cs] += _colsum(dn * xh)
            dxh = dn * wv
            do_l.append(r * (dxh - xh * jnp.mean(dxh * xh, axis=1, keepdims=True)))
        du_l = []
        for gi in range(4):
            cs = slice(gi * 256, (gi + 1) * 256)
            ug, r, wv = u[:, cs], keep[NH + gi], vec_ref[2:3, cs]
            dss = dymix[:, D + gi * 256:D + (gi + 1) * 256]
            xh = ug * r
            acc_ref[2:3, cs] += _colsum(dss * xh)
            dxh = dss * wv
            du_l.append(r * (dxh - xh * jnp.mean(dxh * xh, axis=1, keepdims=True)))
        du = jnp.concatenate(du_l, axis=1)
        dyy = du * _silu(z)
        do_ref[...] = jnp.concatenate(do_l, axis=1).astype(do_ref.dtype)
        dg_ref[...] = jnp.concatenate(dg_l, axis=1).astype(dg_ref.dtype)
        dys_ref[...] = dyy.astype(dys_ref.dtype)
        dz_ref[...] = (du * yy * _dsilu(z)).astype(dz_ref.dtype)
        dxs_ref[...] = (dyy * vec_ref[1:2, :]).astype(dxs_ref.dtype)
        acc_ref[1:2, :] += _colsum(dyy * xs)

    row = lambda j: (lambda i: (jnp.minimum(i, nlat - 1), j))
    lat = pl.BlockSpec((TB, D), row(0))
    tok = pl.BlockSpec((TB, D), lambda i: (i, 0))
    call = dict(
        body=body, args=[dylat, o_f, o_b, p_main, p_main, y_f, y_b, xa, vecs, w_out], name="mix_bwd",
        out_shape=(jax.ShapeDtypeStruct((n, D), MXU_DTYPE), jax.ShapeDtypeStruct((t_total, D), MXU_DTYPE),
                   jax.ShapeDtypeStruct((n, D), MXU_DTYPE), jax.ShapeDtypeStruct((t_total, D), MXU_DTYPE),
                   jax.ShapeDtypeStruct((n, D), MXU_DTYPE), jax.ShapeDtypeStruct((8, D), F32)),
        grid=(t_total // TB,),
        in_specs=[lat, lat, lat, pl.BlockSpec((TB, D), row(4)), pl.BlockSpec((TB, D), row(5)), lat, lat, lat,
                  _full((8, D)), _full((2 * D, D))],
        out_specs=(lat, tok, lat, tok, lat, _full((8, D))), scratch=[],
        sem=("arbitrary",), vmem_mb=48)
    return _run(_carry(call, comm, lambda: (pl.program_id(0) == 0, pl.program_id(0) == t_total // TB - 1)))


def _ffn_up(u2, w_gate, w_up):
    n = u2.shape[0]
    tb = 1024

    def body(u_ref, wg_ref, wu_ref, g_ref, up_ref, a_ref):
        uv = u_ref[...]
        gt = _nt(uv, wg_ref[...])
        upv = _nt(uv, wu_ref[...])
        g_ref[...] = gt.astype(g_ref.dtype)
        up_ref[...] = upv.astype(up_ref.dtype)
        a_ref[...] = (_silu(gt) * upv).astype(a_ref.dtype)

    blk = pl.BlockSpec((tb, FSL), lambda j, i: (i, j))
    wblk = pl.BlockSpec((None, FSL, D), lambda j, i: (j, 0, 0))
    return _pcall(
        body, name="ffn_up",
        out_shape=(jax.ShapeDtypeStruct((n, DFFP), MXU_DTYPE),) * 3,
        grid=(4, n // tb), in_specs=[pl.BlockSpec((tb, D), lambda j, i: (i, 0)), wblk, wblk],
        out_specs=(blk, blk, blk), sem=("parallel", "parallel"), vmem_mb=48,
    )(u2, w_gate, w_up)


def _ffn_down_loss(act, w_down, h1, tgt, vecs):
    n = act.shape[0]
    tb = 512

    def body(a_ref, w_ref, h1_ref, t_ref, vec_ref, dh2_ref, dffn_ref, acc_ref):
        i = pl.program_id(0)

        @pl.when(i == 0)
        def _():
            acc_ref[...] = jnp.zeros_like(acc_ref)

        g2 = vec_ref[0:1, :]
        fw = vec_ref[1:2, :]
        nsub = 4
        sb = tb // nsub
        wv = w_ref[...]
        ffns = [_nn(a_ref[r_ * sb:(r_ + 1) * sb, :], wv) for r_ in range(nsub)]
        for r_ in range(nsub):
            rows = slice(r_ * sb, (r_ + 1) * sb)
            ffn = ffns[r_]
            h2 = h1_ref[rows, :] + g2 * ffn
            r = lax.rsqrt(jnp.mean(h2 * h2, axis=1, keepdims=True) + EPS)
            xh = h2 * r
            err = xh * fw - t_ref[rows, :]
            dy = err * (1.0 / D)
            acc_ref[2:3, :] += _colsum(err * err) * (0.5 / D)
            acc_ref[1:2, :] += _colsum(dy * xh)
            dxh = dy * fw
            dh2 = r * (dxh - xh * jnp.mean(dxh * xh, axis=1, keepdims=True))
            dh2_ref[rows, :] = dh2
            dffn_ref[rows, :] = (g2 * dh2).astype(dffn_ref.dtype)
            acc_ref[0:1, :] += _colsum(dh2 * ffn)

    return _pcall(
        body, name="ffn_down_loss",
        out_shape=(jax.ShapeDtypeStruct((n, D), F32), jax.ShapeDtypeStruct((n, D), MXU_DTYPE),
                   jax.ShapeDtypeStruct((8, D), F32)),
        grid=(n // tb,),
        in_specs=[pl.BlockSpec((tb, DFFP), lambda i: (i, 0)), _full((DFFP, D)), pl.BlockSpec((tb, D), lambda i: (i, 0)),
                  pl.BlockSpec((tb, D), lambda i: (i, 0)), _full((8, D))],
        out_specs=(pl.BlockSpec((tb, D), lambda i: (i, 0)), pl.BlockSpec((tb, D), lambda i: (i, 0)), _full((8, D))),
        sem=("arbitrary",), vmem_mb=48,
    )(act, w_down, h1, tgt, vecs)


def _ffn_bwd(dffn, w_down, gate, up, w_gate_t, w_up_t):
    n = dffn.shape[0]
    tb = 1024

    def body(df_ref, wd_ref, g_ref, up_ref, wg_ref, wu_ref, dg_ref, dup_ref, du_ref):
        j = pl.program_id(1)
        nsub = 4
        sb = tb // nsub
        wd, wg, wu = wd_ref[...], wg_ref[...], wu_ref[...]
        dacts = [_nt(df_ref[r * sb:(r + 1) * sb, :], wd) for r in range(nsub)]
        parts = []
        for r in range(nsub):
            rows = slice(r * sb, (r + 1) * sb)
            gt = g_ref[rows, :].astype(F32)
            upv = up_ref[rows, :].astype(F32)
            sg = _sig(gt)
            dgt = (dacts[r] * upv * (sg * (1.0 + gt * (1.0 - sg)))).astype(MXU_DTYPE)
            dupv = (dacts[r] * (gt * sg)).astype(MXU_DTYPE)
            dg_ref[rows, :] = dgt
            dup_ref[rows, :] = dupv
            parts.append(_nn(dgt, wg) + _nn(dupv, wu))
        part = jnp.concatenate(parts, axis=0)

        @pl.when(j == 0)
        def _():
            du_ref[...] = part

        @pl.when(j > 0)
        def _():
            du_ref[...] += part

    tok = pl.BlockSpec((tb, D), lambda i, j: (i, 0))
    ffb = pl.BlockSpec((tb, FSL), lambda i, j: (i, j))
    wsl = pl.BlockSpec((None, FSL, D), lambda i, j: (j, 0, 0))
    return _pcall(
        body, name="ffn_bwd",
        out_shape=(jax.ShapeDtypeStruct((n, DFFP), MXU_DTYPE), jax.ShapeDtypeStruct((n, DFFP), MXU_DTYPE),
                   jax.ShapeDtypeStruct((n, D), F32)),
        grid=(n // tb, 4),
        in_specs=[tok, pl.BlockSpec((FSL, D), lambda i, j: (j, 0)), ffb, ffb, wsl, wsl],
        out_specs=(ffb, ffb, tok), sem=("parallel", "arbitrary"), vmem_mb=48,
    )(dffn, w_down, gate, up, w_gate_t, w_up_t)


def _ffn_norm_bwd(du, h1, ylat, dh2, vecs):
    n = du.shape[0]
    tb = 512

    def body(du_ref, h1_ref, yl_ref, dh2_ref, vec_ref, dh1_ref, dyl_ref, acc_ref):
        @pl.when(pl.program_id(0) == 0)
        def _():
            acc_ref[...] = jnp.zeros_like(acc_ref)

        duv = du_ref[...]
        h1 = h1_ref[...]
        r = lax.rsqrt(jnp.mean(h1 * h1, axis=1, keepdims=True) + EPS)
        xh = h1 * r
        nw = vec_ref[2:3, :]
        acc_ref[0:1, :] += _colsum(duv)
        acc_ref[1:2, :] += _colsum(duv * xh * nw)
        dn = duv * vec_ref[1:2, :]
        acc_ref[2:3, :] += _colsum(dn * xh)
        dxh = dn * nw
        dh1 = dh2_ref[...] + r * (dxh - xh * jnp.mean(dxh * xh, axis=1, keepdims=True))
        dh1_ref[...] = dh1
        dyl_ref[...] = (vec_ref[0:1, :] * dh1).astype(dyl_ref.dtype)
        acc_ref[3:4, :] += _colsum(dh1 * yl_ref[...])

    tok = pl.BlockSpec((tb, D), lambda i: (i, 0))
    return _pcall(
        body, name="ffn_norm_bwd",
        out_shape=(jax.ShapeDtypeStruct((n, D), F32), jax.ShapeDtypeStruct((n, D), MXU_DTYPE),
                   jax.ShapeDtypeStruct((8, D), F32)),
        grid=(n // tb,), in_specs=[tok, tok, tok, tok, _full((8, D))], out_specs=(tok, tok, _full((8, D))),
        sem=("arbitrary",), vmem_mb=40,
    )(du, h1, ylat, dh2, vecs)


def _deep_rows(rows):
    return max(r for r in range(128, 2305, 128) if rows % r == 0)


def _dw(a, b, name):
    tn_rows = a.shape[0]
    bt = _deep_rows(tn_rows)
    kk, nn_ = a.shape[1], b.shape[1]
    bk = 1024 if kk % 1024 == 0 else kk
    bn = 1024 if nn_ % 1024 == 0 else nn_
    nt = tn_rows // bt

    def body(a_ref, b_ref, o_ref, acc):
        t = pl.program_id(2)
        part = _tn(a_ref[...], b_ref[...])

        @pl.when(t == 0)
        def _():
            acc[...] = part

        @pl.when(t > 0)
        def _():
            acc[...] += part

        @pl.when(t == nt - 1)
        def _():
            o_ref[...] = acc[...].astype(o_ref.dtype)

    return _pcall(
        body, name=name, out_shape=jax.ShapeDtypeStruct((kk, nn_), MXU_DTYPE), grid=(kk // bk, nn_ // bn, nt),
        in_specs=[pl.BlockSpec((bt, bk), lambda i, j, t: (t, i)), pl.BlockSpec((bt, bn), lambda i, j, t: (t, j))],
        out_specs=pl.BlockSpec((bk, bn), lambda i, j, t: (i, j)), scratch=[pltpu.VMEM((bk, bn), F32)],
        sem=("parallel", "parallel", "arbitrary"), vmem_mb=40,
    )(a, b)


def _dw_in(segs, u_all, name):
    tiles = []
    for m, s_ in enumerate(segs):
        tiles += [(m, h) for h in range(s_.shape[1] // D)]
    ntile = len(tiles)
    t_total = u_all.shape[0]
    bt = _deep_rows(t_total)
    nt = t_total // bt

    def body(u_ref, *refs):
        seg_refs, o_ref, acc = refs[:len(segs)], refs[len(segs)], refs[len(segs) + 1]
        n, t = pl.program_id(0), pl.program_id(1)
        for k, (m, _) in enumerate(tiles):
            @pl.when(n == k)
            def _(m=m):
                part = _tn(seg_refs[m][...], u_ref[...])

                @pl.when(t == 0)
                def _():
                    acc[...] = part

                @pl.when(t > 0)
                def _():
                    acc[...] += part

        @pl.when(t == nt - 1)
        def _():
            o_ref[...] = acc[...].astype(o_ref.dtype)

    def seg_spec(m):
        ks = [k for k, (mm, _) in enumerate(tiles) if mm == m]
        lo, hi = ks[0], ks[-1]
        on = lambda n: (n >= lo) & (n <= hi)
        return pl.BlockSpec((bt, D), lambda n, t: (jnp.where(on(n), t, 0), jnp.where(on(n), n - lo, 0)))

    return _pcall(
        body, name=name, out_shape=jax.ShapeDtypeStruct((1, ntile * D, D), MXU_DTYPE), grid=(ntile, nt),
        in_specs=[pl.BlockSpec((bt, D), lambda n, t: (t, 0))] + [seg_spec(m) for m in range(len(segs))],
        out_specs=pl.BlockSpec((None, D, D), lambda n, t: (0, n, 0)),
        scratch=[pltpu.VMEM((D, D), F32)], sem=("parallel", "arbitrary"), vmem_mb=56,
    )(u_all, *segs)


def _du_prenorm_bwd(segs, ddt, wi_main, wi_tail, xin, mods, dres, row_off, tb, name, comm=None):
    n = xin.shape[0]
    nt = n // tb
    off = row_off // tb
    has_dx = dres is not None

    def body(*refs):
        seg_refs = refs[:7]
        ddt_ref, w_ref, wb_ref, wdt_ref, x_ref, mod_ref = refs[7:13]
        rest = refs[13:]
        if has_dx:
            dres_ref, dx_ref, acc_ref, du_scr = rest
        else:
            acc_ref, du_scr = rest
        j, i = pl.program_id(0), pl.program_id(1)
        rows = pl.ds(pl.multiple_of(i * tb, tb), tb)

        @pl.when((i == 0) & (j == 0))
        def _():
            acc_ref[...] = jnp.zeros_like(acc_ref)

        @pl.when(j == 0)
        def _():
            du_scr[rows, :] = _nn(ddt_ref[...], wdt_ref[...])

        for k in range(4):
            if not has_dx and k == 2:
                continue

            @pl.when(j == k)
            def _(k=k):
                if k < 3:
                    sa, sb = seg_refs[2 * k][...], seg_refs[2 * k + 1][...]
                else:
                    sa, sb = seg_refs[6][:, 0:D], seg_refs[6][:, D:2 * D]
                part = _nn(sa, w_ref[0:D, :]) + _nn(sb, w_ref[D:2 * D, :])
                if k > 0:
                    part = part + _nn(sa[:, 0:WTAIL], wb_ref[...])
                du_scr[rows, :] += part

        @pl.when(j == 3)
        def _():
            du = du_scr[rows, :]
            xv = x_ref[...]
            r = lax.rsqrt(jnp.mean(xv * xv, axis=1, keepdims=True) + EPS)
            xh = xv * r
            nw = mod_ref[1:2, :]
            acc_ref[0:1, :] += _colsum(du)
            acc_ref[1:2, :] += _colsum(du * xh * nw)
            dn = du * mod_ref[0:1, :]
            acc_ref[2:3, :] += _colsum(dn * xh)
            if has_dx:
                dxh = dn * nw
                dx_ref[...] = dres_ref[...] + r * (dxh - xh * jnp.mean(dxh * xh, axis=1, keepdims=True))

    def seg_spec(k):
        width = D if k < 6 else 2 * D
        return pl.BlockSpec((tb, width), lambda j, i: (jnp.where(j == min(k // 2, 3), i + off, 0), 0))

    last = pl.BlockSpec((tb, D), lambda j, i: (jnp.where(j == 3, i, 0), 0))
    in_specs = [seg_spec(k) for k in range(7)]
    in_specs += [pl.BlockSpec((tb, 128), lambda j, i: (jnp.where(j == 0, i + off, 0), 0))] + _w_specs()
    in_specs += [last, _full((8, D))]
    args = list(segs) + [ddt, wi_main, wi_tail, wi_tail, xin, mods]
    out_shape = [jax.ShapeDtypeStruct((8, D), F32)]
    out_specs = [_full((8, D))]
    if has_dx:
        in_specs.append(last)
        args.append(dres)
        out_shape.insert(0, jax.ShapeDtypeStruct((n, D), F32))
        out_specs.insert(0, last)
    call = dict(body=body, args=args, name=name, out_shape=tuple(out_shape), grid=(4, nt), in_specs=in_specs,
                out_specs=tuple(out_specs), scratch=[pltpu.VMEM((n, D), F32)], sem=("arbitrary", "arbitrary"),
                vmem_mb=58)
    steps = lambda: ((pl.program_id(0) == 0) & (pl.program_id(1) == 0),
                     (pl.program_id(0) == 3) & (pl.program_id(1) == nt - 1))
    return _run(_carry(call, comm, steps))


def _sum8(v):
    def body(v_ref, o_ref):
        acc = v_ref[0]
        for k in range(1, 8):
            acc = acc + v_ref[k]
        o_ref[...] = acc

    return _pcall(body, name="small_sum", out_shape=jax.ShapeDtypeStruct(v.shape[1:], F32),
                  in_specs=[pl.BlockSpec(memory_space=pltpu.VMEM)], out_specs=pl.BlockSpec(memory_space=pltpu.VMEM))(v)


def _adamw(w, m, v, g, name, comm=None):
    lead = w.ndim == 3
    rows, cols = w.shape[-2:]
    rb = 256 if rows % 256 == 0 else (352 if rows % 352 == 0 else rows)
    c1 = 1.0 - B1 ** STEP
    c2 = 1.0 - B2 ** STEP

    def body(w_ref, m_ref, v_ref, g_ref, d_ref, nm_ref, nv_ref):
        gv = g_ref[...]
        mn = B1 * m_ref[...] + (1.0 - B1) * gv
        vn = B2 * v_ref[...] + (1.0 - B2) * (gv * gv)
        nm_ref[...] = mn
        nv_ref[...] = vn
        d_ref[...] = -LR * ((mn / c1) / (jnp.sqrt(vn / c2) + AEPS) + WD * w_ref[...])

    if rb == rows and rows > 1024:
        cb, steps = 256, cols // 256
        gspec = pl.BlockSpec((rows, cb), lambda i: (0, i))
        spec = pl.BlockSpec((None, rows, cb), lambda i: (0, 0, i)) if lead else gspec
    else:
        steps = rows // rb
        gspec = pl.BlockSpec((rb, cols), lambda i: (i, 0))
        spec = pl.BlockSpec((None, rb, cols), lambda i: (0, i, 0)) if lead else gspec
    call = dict(body=body, args=[w, m, v, g], name=name, out_shape=(jax.ShapeDtypeStruct(w.shape, F32),) * 3,
                grid=(steps,), in_specs=[spec] * 3 + [gspec], out_specs=(spec,) * 3, scratch=[],
                sem=("arbitrary",) if comm is not None else ("parallel",), vmem_mb=40)
    return _run(_carry(call, comm, lambda: (pl.program_id(0) == 0, pl.program_id(0) == steps - 1,
                                            pl.program_id(0) == steps - 1)))


def _rows(v, n):
    f = v.reshape(-1)
    return jnp.pad(f, (0, n * D - f.shape[0])).reshape(n, D)


def kernel(x, c, ctx, c_ctx, w_ada, b_ada, norm_mix, w_in, conv_w, conv_b, ssd_a_log, ssd_dt_bias, ssd_d, ssd_norm, hgrn_lb_raw, hgrn_norm, w_out, norm_ffn, w_gate, w_up, w_down, final_norm, loss_target, m_c_ctx, m_w_ada, m_b_ada, m_norm_mix, m_w_in, m_conv_w, m_conv_b, m_ssd_a_log, m_ssd_dt_bias, m_ssd_d, m_ssd_norm, m_hgrn_lb_raw, m_hgrn_norm, m_w_out, m_norm_ffn, m_w_gate, m_w_up, m_w_down, m_final_norm, v_c_ctx, v_w_ada, v_b_ada, v_norm_mix, v_w_in, v_conv_w, v_conv_b, v_ssd_a_log, v_ssd_dt_bias, v_ssd_d, v_ssd_norm, v_hgrn_lb_raw, v_hgrn_norm, v_w_out, v_norm_ffn, v_w_gate, v_w_up, v_w_down, v_final_norm):
    ix, iy, ic = lax.axis_index("x"), lax.axis_index("y"), lax.axis_index("c")
    chip = 2 * ix + iy
    me = 2 * chip + ic
    xl, xc, tgt = x[0], ctx[0], loss_target[0]
    n_lat, n_ctx = xl.shape[0], xc.shape[0]
    assert n_ctx == TB and n_lat % 1024 == 0
    t_total = n_lat + n_ctx
    nb = t_total // TB

    tr = lambda a: jnp.swapaxes(a, -1, -2)
    shift = [functools.partial(jnp.pad, pad_width=((8 * k, WSL + WTAIL - NSH - 8 * k), (0, 0))) for k in range(4)]
    slab = lax.switch(chip, shift, tr(w_in[0]).astype(MXU_DTYPE))
    padrows = lambda a: jnp.pad(a, ((0, FSL - DFF // 4), (0, 0))).astype(MXU_DTYPE)
    shards = [slab[:WSL], slab[WSL:], w_out[0].astype(MXU_DTYPE), padrows(tr(w_gate[0])), padrows(tr(w_up[0])),
              padrows(w_down[0])]
    own = lambda g_, s_: lax.dynamic_update_slice(g_, s_[None], (chip, 0, 0))
    pack = jnp.concatenate([c, hgrn_lb_raw.reshape(1, D), _rows(conv_w[0], 3), jnp.zeros((3, D), F32)], axis=0)
    ncol_ada = w_ada.shape[2]
    b_shard = lax.dynamic_slice(b_ada, (0, chip * ncol_ada), (1, ncol_ada))
    gath, araw, mod_all, wi_main, wi_tail = _prologue(pack, c_ctx.reshape(1, D), w_ada[0], b_shard, shards[:2])
    wi_main, wi_tail = own(wi_main, shards[0]), own(wi_tail, shards[1])
    gath = gath.reshape(8, 8, D)
    lbraw_full = gath[0::2, 1].reshape(4, 2, 2, 256).transpose(1, 2, 0, 3).reshape(4, D)
    convw_full = gath[0::2, 2:5].reshape(4, 3 * D)[:, :KCONV * 512].reshape(4, KCONV, 512).transpose(1, 0, 2)
    convw_full = convw_full.reshape(KCONV, 2048)
    lbraw8 = jnp.pad(lbraw_full, ((0, 4), (0, 0)))
    convp = jnp.concatenate([convw_full, conv_b, jnp.zeros((2, 2048), F32)], axis=0)
    dtb = jnp.pad(ssd_dt_bias.reshape(1, 32), ((0, 7), (0, 96)))
    alog = jnp.pad(ssd_a_log.reshape(1, 32), ((0, 7), (0, 96)))
    mod_all = mod_all.reshape(8, 16, ncol_ada)[0::2]
    mod_full = mod_all.transpose(1, 0, 2).reshape(16, 4 * ncol_ada)
    my_mod = lax.dynamic_slice(mod_full, (me, 0), (1, 6 * D)).reshape(6, D)
    sh1, sc1, g1, sh2, sc2, g2 = (my_mod[k:k + 1] for k in range(6))
    csh1, csc1 = mod_full[8:9, 0:D], mod_full[8:9, D:2 * D]

    zrow = jnp.zeros((1, D), F32)
    mods_lat = jnp.concatenate([1.0 + sc1, sh1, norm_mix, zrow, zrow, zrow, zrow, zrow], axis=0)
    mods_ctx = jnp.concatenate([1.0 + csc1, csh1, norm_mix, zrow, zrow, zrow, zrow, zrow], axis=0)
    outs = _inproj(xl, mods_lat, wi_main, wi_tail, t_total, 1024, 0, None, "inproj_lat",
                   comm=_comm_gather(shards[2:5]))
    wo_g, wg_g, wu_g = (own(g_, s_) for g_, s_ in zip(outs[3:], shards[2:5]))
    w_out_f = wo_g.reshape(2 * D, D)
    p_main, p_dt, u_all = _inproj(xc, mods_ctx, wi_main, wi_tail, t_total, TB, nb - 1, outs[:3], "inproj_ctx")

    o_f, hs_f, wd_g = _hgrn_fwd(p_main, lbraw8, 0, nb, comm=_comm_gather(shards[5:]))
    w_down_f = own(wd_g, shards[5]).reshape(DFFP, D)
    o_b, hs_b = _hgrn_fwd(p_main, lbraw8, 1, nb)
    xa, dsl, dts = _ssd_prep(p_main, p_dt, convp, dtb, nb)
    y_f, ss_f = _ssd_fwd(xa, dts, alog, 0, nb)
    y_b, ss_b = _ssd_fwd(xa, dts, alog, 1, nb)

    vec_mix = jnp.concatenate([jnp.tile(hgrn_norm, (1, NH)), jnp.repeat(ssd_d, SP, axis=1), ssd_norm, g1, 1.0 + sc2,
                               sh2, norm_ffn, zrow], axis=0)
    ymix, ylat, h1, u2 = _mix_out(o_f, o_b, p_main, y_f, y_b, xa, xl, vec_mix, w_out_f)
    gate, up, act = _ffn_up(u2, wg_g, wu_g)
    vec_loss = jnp.concatenate([g2, final_norm.reshape(1, D)] + [zrow] * 6, axis=0)
    dh2, dffn, acc_loss = _ffn_down_loss(act, w_down_f, h1, tgt, vec_loss)

    core_arr = jnp.reshape(ic, (1,)).astype(jnp.int32)
    chip_arr = jnp.reshape(chip, (1,)).astype(jnp.int32)
    every = (0, 4)

    def pair_sum(gs, got, tag):
        return list(_pair_sum(gs, list(got), core_arr, "grads_pair_sum_" + tag))

    vec_ffn = jnp.concatenate([g1, 1.0 + sc2, norm_ffn] + [zrow] * 5, axis=0)
    dgate, dup, du2 = _ffn_bwd(dffn, w_down_f, gate, up, wg_g, wu_g)
    dh1, dylat, acc_ffn = _ffn_norm_bwd(du2, h1, ylat, dh2, vec_ffn)
    gw_down = _dw(act, dffn, "dw_down").reshape(4, FSL, D)
    ga1 = [_dw(dgate, u2, "dw_gate").reshape(4, FSL, D), _dw(dup, u2, "dw_up").reshape(4, FSL, D)]
    res = _mix_bwd(dylat, o_f, o_b, p_main, y_f, y_b, xa, vec_mix, w_out_f, comm=_comm_pair(ga1))
    (do, dgr, dys, dzr, dxs_skip, acc_mix), pair_a1 = res[:6], pair_sum(ga1, res[6:], "a1")
    ga2 = [gw_down, _dw(ymix, dylat, "dw_out").reshape(4, D // 2, D)]

    res = _hgrn_bwd(p_main, lbraw8, hs_f, do, 0, nb, None,
                    comm=[_comm_exchange(pair_a1, [every] * 2), _comm_pair(ga2)])
    (dq0, dff, dv0, dlb_f), recv_a, pair_a2 = res[:4], list(res[4:6]), pair_sum(ga2, res[6:], "a2")
    res = _hgrn_bwd(p_main, lbraw8, hs_b, do, 1, nb, (dq0, dv0), comm=_comm_exchange(pair_a2, [every] * 2))
    (dq, dfb, dv, dlb_b), recv_a = res[:4], recv_a + list(res[4:])
    pair_a, dests_a = pair_a1 + pair_a2, [every] * 4
    gw_in = [_dw_in([dq, dff], u_all, "dw_in_0"), _dw_in([dfb, dv], u_all, "dw_in_1"),
             _dw_in([dgr, dzr], u_all, "dw_in_2")]

    res = _ssd_bwd(xa, dts, alog, ss_f, dys, 0, nb, None, comm=_comm_pair(gw_in))
    (dxa0, ddts0, da_f), pair_b, dests_b = res[:3], pair_sum(gw_in, res[3:], "b"), [(0, 1), (1, 2), (2, 3)]
    res = _ssd_bwd(xa, dts, alog, ss_b, dys, 1, nb, (dxa0, ddts0), comm=_comm_exchange(pair_b, dests_b))
    (dxa, ddts, da_b), recv_b = res[:3], list(res[3:])
    dxbc, ddt, acc_conv, acc_dtb = _ssd_prep_bwd(p_main, p_dt, convp, dtb, dsl, dxa, dxs_skip, ddts, nb)
    gw_in.append(_dw_in([dxbc], u_all, "dw_in_3"))
    gw_in_dt = _dw(ddt, u_all, "dw_in_dt")
    gc = [gw_in[3], jnp.concatenate([g_[:, 0:WTAIL, :] for g_ in gw_in[1:]] + [gw_in_dt[None]], axis=0)]

    segs = [dq, dff, dfb, dv, dgr, dzr, dxbc]
    bmods_lat = jnp.concatenate([1.0 + sc1, norm_mix] + [zrow] * 6, axis=0)
    bmods_ctx = jnp.concatenate([1.0 + csc1, norm_mix] + [zrow] * 6, axis=0)
    res = _du_prenorm_bwd(segs, ddt, wi_main, wi_tail, xc, bmods_ctx, None, n_lat, TB, "du_ctx", comm=_comm_pair(gc))
    acc_ctx, pair_c, dests_c = res[0], pair_sum(gc, res[1:], "c"), [(3, 4), every]
    res = _du_prenorm_bwd(segs, ddt, wi_main, wi_tail, xl, bmods_lat, dh1, 0, 512, "du_lat",
                          comm=_comm_exchange(pair_c, dests_c))
    (grad_x, acc_lat), recv_c = res[:2], list(res[2:])

    mine = _chip_sum(pair_b + pair_c + pair_a, recv_b + recv_c + recv_a, chip_arr, dests_b + dests_c + dests_a,
                     [0, 0, 0, 0, 1, 3, 4, 5, 2])
    dmod_lat = jnp.concatenate([acc_lat[0:2], acc_ffn[3:4], acc_ffn[0:2], acc_loss[0:1]], axis=0)
    misc = jnp.concatenate([(da_f + da_b)[0, :32], jnp.zeros((96,), F32), acc_dtb[0, :32], jnp.zeros((96,), F32),
                            jnp.sum(acc_loss[2]).reshape(1), jnp.zeros((D - 257,), F32)]).reshape(1, D)
    sv = jnp.concatenate([
        dmod_lat, acc_ctx[0:2], (acc_lat[2:3] + acc_ctx[2:3]), acc_ffn[2:3], acc_loss[1:2], acc_mix[2:3],
        acc_mix[0:1], acc_mix[1:2], dlb_f[0:1], dlb_b[0:1], acc_conv[0:6].reshape(12, D), misc,
        jnp.zeros((3, D), F32)], axis=0)
    res = _pair_swap(mine, sv)
    theirs, sv_all = res[:-1], res[-1].reshape(8, 32, D)
    whole = [jnp.concatenate([jnp.where(ic == 0, m_, t_), jnp.where(ic == 0, t_, m_)], axis=0)
             for m_, t_ in zip(mine, theirs)]
    g_w_in = lax.dynamic_slice(jnp.concatenate(whole[0:2], axis=0), (8 * chip, 0), (NSH, D))
    g_w_out = whole[2]
    g_w_gate = whole[3][:DFF // 4]
    g_w_up = whole[4][:DFF // 4]
    g_w_down = whole[5][:DFF // 4]
    ssum = _sum8(sv_all)
    dmod_rows = sv_all[:, 0:6].reshape(8, 6 * D)
    dmod_ctx_row = jnp.concatenate([ssum[6:8].reshape(1, 2 * D), jnp.zeros((1, 4 * D), F32)], axis=1)
    dmod_full = jnp.concatenate([dmod_rows, dmod_ctx_row, jnp.zeros((7, 6 * D), F32)], axis=0)
    grad_b_ada = jnp.sum(dmod_full, axis=0, keepdims=True)
    dmod_shard = lax.dynamic_slice(dmod_full, (0, chip * ncol_ada), (16, ncol_ada))
    g_w_ada, da_part = _ada_bwd(araw, dmod_shard, w_ada[0])
    da_all = _allgather8(da_part, "ada_ctx_gather").reshape(8, 16, D)[0::2, 8]

    big = {}
    for nm, w_, m_, v_, g_ in (("w_ada", w_ada, m_w_ada, v_w_ada, g_w_ada), ("w_in", w_in, m_w_in, v_w_in, g_w_in),
                               ("w_out", w_out, m_w_out, v_w_out, g_w_out),
                               ("w_gate", w_gate, m_w_gate, v_w_gate, g_w_gate),
                               ("w_up", w_up, m_w_up, v_w_up, g_w_up),
                               ("w_down", w_down, m_w_down, v_w_down, g_w_down)):
        if nm in ("w_in", "w_gate", "w_up"):
            big[nm] = tuple(tr(t) for t in (g_[None],) + tuple(_adamw(tr(w_), tr(m_), tr(v_), g_, "adamw_" + nm)))
        else:
            big[nm] = (g_[None],) + tuple(_adamw(w_, m_, v_, g_, "adamw_" + nm))
    cc = c_ctx.reshape(1, D)
    grad_c_ctx = (jnp.sum(da_all, axis=0, keepdims=True) * _dsilu(cc)).reshape(D)

    grad_norm_mix, grad_norm_ffn, grad_final_norm = ssum[8:9], ssum[9:10], ssum[10].reshape(D)
    grad_ssd_norm = ssum[11:12]
    grad_hgrn_norm = jnp.sum(ssum[12].reshape(NH, HF), axis=0, keepdims=True)
    grad_ssd_d = jnp.sum(ssum[13].reshape(SHEADS, SP), axis=1).reshape(1, SHEADS)
    lb_full = _sig(lbraw_full[0:2] - lbraw_full[2:4])
    dr0 = ssum[14:16] * lb_full * (1.0 - lb_full)
    grad_lb_full = jnp.stack([dr0, -dr0], axis=0)
    grad_lb = lax.dynamic_slice(grad_lb_full, (0, 0, chip * 256), (2, 2, 256))
    grad_conv_w = lax.dynamic_slice(ssum[16:26].reshape(KCONV, 2048), (0, chip * 512), (KCONV, 512)).reshape(1, KCONV, 512)
    grad_conv_b = ssum[26:28].reshape(1, 2048)
    a_val = -jnp.exp(ssd_a_log)
    grad_a_log = ssum[28, 0:32].reshape(1, 2, SHEADS) * a_val
    grad_dt_bias = ssum[28, 128:160].reshape(1, 2, SHEADS)
    loss = ssum[28, 256]

    small_w = [c_ctx, b_ada, norm_mix, conv_w, conv_b, ssd_a_log, ssd_dt_bias, ssd_d, ssd_norm, hgrn_lb_raw,
               hgrn_norm, norm_ffn, final_norm]
    small_m = [m_c_ctx, m_b_ada, m_norm_mix, m_conv_w, m_conv_b, m_ssd_a_log, m_ssd_dt_bias, m_ssd_d, m_ssd_norm,
               m_hgrn_lb_raw, m_hgrn_norm, m_norm_ffn, m_final_norm]
    small_v = [v_c_ctx, v_b_ada, v_norm_mix, v_conv_w, v_conv_b, v_ssd_a_log, v_ssd_dt_bias, v_ssd_d, v_ssd_norm,
               v_hgrn_lb_raw, v_hgrn_norm, v_norm_ffn, v_final_norm]
    small_g = [grad_c_ctx, grad_b_ada, grad_norm_mix, grad_conv_w, grad_conv_b, grad_a_log, grad_dt_bias, grad_ssd_d,
               grad_ssd_norm, grad_lb, grad_hgrn_norm, grad_norm_ffn, grad_final_norm]
    nrows = [-(-a.size // D) for a in small_w]
    packs = lambda lst: jnp.concatenate([_rows(a, r) for a, r in zip(lst, nrows)]
                                        + [jnp.zeros((24 - sum(nrows), D), F32)], axis=0)
    sd, sm, svv = _adamw(packs(small_w), packs(small_m), packs(small_v), packs(small_g), "adamw_small")

    def unpack(p):
        out, r0 = [], 0
        for a, r in zip(small_w, nrows):
            out.append(p[r0:r0 + r].reshape(-1)[:a.size].reshape(a.shape))
            r0 += r
        return out

    sd, sm, svv = unpack(sd), unpack(sm), unpack(svv)

    order = ["c_ctx", "w_ada", "b_ada", "norm_mix", "w_in", "conv_w", "conv_b", "ssd_a_log", "ssd_dt_bias", "ssd_d",
             "ssd_norm", "hgrn_lb_raw", "hgrn_norm", "w_out", "norm_ffn", "w_gate", "w_up", "w_down", "final_norm"]
    small_names = ["c_ctx", "b_ada", "norm_mix", "conv_w", "conv_b", "ssd_a_log", "ssd_dt_bias", "ssd_d", "ssd_norm",
                   "hgrn_lb_raw", "hgrn_norm", "norm_ffn", "final_norm"]
    table = dict(big)
    for k, nm in enumerate(small_names):
        table[nm] = (small_g[k].reshape(small_w[k].shape), sd[k], sm[k], svv[k])
    grads = [table[nm][0] for nm in order]
    deltas = [table[nm][1] for nm in order]
    new_m = [table[nm][2] for nm in order]
    new_v = [table[nm][3] for nm in order]
    return (loss, grad_x[None], *grads, *deltas, *new_m, *new_v)
```

```python
import functools
import math

import jax
import jax.numpy as jnp
from jax import lax
from jax.experimental import pallas as pl
from jax.experimental.pallas import tpu as pltpu

F32 = jnp.float32
BF16 = jnp.bfloat16
MXU_DTYPE = jnp.bfloat16
_INTERPRET = False

D = 1024
NH, HF = 8, 128
HC = 64
SC = 128
SN = 128
SHEADS, SP = 16, 64
GRID_W = 64
KCONV = 5
DFF = 2816
FSL = 768
DFFP = 4 * FSL
NIN = 8224
TB = 256
EPS = 1e-6
LR, B1, B2, AEPS, WD, STEP = 0.001, 0.9, 0.999, 1e-08, 0.01, 10
MESH_ID = pl.DeviceIdType.MESH
NSH = NIN // 4
WSL = 2048
WTAIL = 128


def _pcall(body, *, name, out_shape, grid=(), in_specs=None, out_specs=None, scratch=(), sem=None,
           vmem_mb=None, aliases=None):
    params = {}
    if sem is not None:
        params["dimension_semantics"] = sem
    if vmem_mb is not None:
        params["vmem_limit_bytes"] = vmem_mb << 20
    kw = dict(name=name, out_shape=out_shape, scratch_shapes=list(scratch),
              input_output_aliases=aliases or {}, compiler_params=pltpu.CompilerParams(**params),
              interpret=_INTERPRET)
    if grid:
        kw["grid"] = grid
    if in_specs is not None:
        kw["in_specs"] = in_specs
    if out_specs is not None:
        kw["out_specs"] = out_specs
    return pl.pallas_call(body, **kw)


def _mx(a):
    return a.astype(MXU_DTYPE)


def _dg(a, b, ca, cb):
    return lax.dot_general(_mx(a), _mx(b), (((ca,), (cb,)), ((), ())), preferred_element_type=F32)


def _nn(a, b):
    return _dg(a, b, 1, 0)


def _nt(a, b):
    return _dg(a, b, 1, 1)


def _tn(a, b):
    return _dg(a, b, 0, 0)


def _dot01(m, x, ways=3):
    f = lambda t: lax.dot_general(m, t, (((1,), (0,)), ((), ())), preferred_element_type=F32)
    hi = x.astype(BF16)
    r1 = x - hi.astype(F32)
    mid = r1.astype(BF16)
    if ways == 2:
        return f(hi) + f(mid)
    lo = (r1 - mid.astype(F32)).astype(BF16)
    return f(hi) + f(mid) + f(lo)


def _tri(n, upper):
    r = lax.broadcasted_iota(jnp.int32, (n, n), 0)
    c = lax.broadcasted_iota(jnp.int32, (n, n), 1)
    return (c >= r) if upper else (c <= r)


def _b01(mask):
    return jnp.where(mask, 1.0, 0.0).astype(BF16)


def _sig(x):
    return jax.nn.sigmoid(x)


def _silu(x):
    return x * _sig(x)


def _dsilu(x):
    s = _sig(x)
    return s * (1.0 + x * (1.0 - s))


def _softplus(x):
    return jnp.maximum(x, 0.0) + jnp.log(1.0 + jnp.exp(-jnp.abs(x)))


def _rowsum(x):
    return jnp.sum(x, axis=1, keepdims=True)


def _colsum(x):
    return jnp.sum(x, axis=0, keepdims=True)


def _full(shape):
    return pl.BlockSpec(shape, lambda *_: (0,) * len(shape))


def _allgather8_phases(x_ref, out_ref, send_sems, recv_sems, local_sem):
    m_per = x_ref.shape[0]
    x, y, c = lax.axis_index("x"), lax.axis_index("y"), lax.axis_index("c")
    me, sibling = (x, y, c), (x, y, 1 - c)
    chips = [(1 - x, y), (x, 1 - y), (1 - x, 1 - y)]

    def rows(px, py, pc):
        return out_ref.at[pl.ds((4 * px + 2 * py + pc) * m_per, m_per), :]

    def copy(k, block, to, src=None):
        return pltpu.make_async_remote_copy(
            src_ref=rows(*block) if src is None else src, dst_ref=rows(*block),
            send_sem=send_sems.at[k], recv_sem=recv_sems.at[k], device_id=to, device_id_type=MESH_ID)

    mine = pltpu.make_async_copy(x_ref, rows(*me), local_sem)
    first = [copy(0, me, sibling, src=x_ref)]
    first += [copy(1 + j, me, (*chip, c), src=x_ref) for j, chip in enumerate(chips)]
    passed = [copy(4 + j, (*chip, c), sibling) for j, chip in enumerate(chips)]

    def start():
        mine.start()
        for cp in first:
            cp.start()

    def forward():
        for j, chip in enumerate(chips):
            copy(1 + j, (*chip, c), me).wait_recv()
            passed[j].start()

    def finish():
        copy(0, sibling, me).wait_recv()
        for j, chip in enumerate(chips):
            copy(4 + j, (*chip, 1 - c), me).wait_recv()
        for cp in first + passed:
            cp.wait_send()
        mine.wait()

    return start, forward, finish


def _allgather8_ops(x_ref, out_ref, send_sems, recv_sems, local_sem):
    for phase in _allgather8_phases(x_ref, out_ref, send_sems, recv_sems, local_sem):
        phase()


def _allgather8(v, name):
    m_per, n = v.shape
    return _pcall(
        functools.partial(_allgather8_ops), name=name, out_shape=jax.ShapeDtypeStruct((8 * m_per, n), v.dtype),
        in_specs=[pl.BlockSpec(memory_space=pltpu.VMEM)], out_specs=pl.BlockSpec(memory_space=pltpu.VMEM),
        scratch=list(_AG8_SEMS),
    )(v)


_AG8_SEMS = [pltpu.SemaphoreType.DMA((7,)), pltpu.SemaphoreType.DMA((7,)), pltpu.SemaphoreType.DMA]


def _prologue(pack, cc_row, w_ada, b_shard, shards):
    n = len(shards)
    ncol = w_ada.shape[1]

    def body(pack_ref, cc_ref, w_ref, b_ref, *refs):
        ins = refs[:n]
        gath_ref, araw_ref, mod_ref = refs[n:n + 3]
        outs = refs[n + 3:2 * n + 3]
        modsh, s1, r1, l1, s2, r2, l2, gs, gr = refs[2 * n + 3:]
        start, forward, finish = _gather_ops(ins, outs, gs, gr, relay=True)
        start()
        _allgather8_ops(pack_ref, gath_ref, s1, r1, l1)
        a = jnp.concatenate([gath_ref[8 * i:8 * i + 1, :] for i in range(8)] + [cc_ref[...], jnp.zeros((7, D), F32)],
                            axis=0)
        araw_ref[...] = a
        modsh[...] = _nn(_silu(a), w_ref[...]) + b_ref[...]
        _allgather8_ops(modsh, mod_ref, s2, r2, l2)
        forward()
        finish()

    vm = pl.BlockSpec(memory_space=pltpu.VMEM)
    anyspec = pl.BlockSpec(memory_space=pl.ANY)
    return _pcall(
        body, name="prologue",
        out_shape=(jax.ShapeDtypeStruct((64, D), F32), jax.ShapeDtypeStruct((16, D), F32),
                   jax.ShapeDtypeStruct((128, ncol), F32)) + _gather_out(shards),
        in_specs=[vm, vm, vm, vm] + [anyspec] * n, out_specs=(vm, vm, vm) + (anyspec,) * n,
        scratch=[pltpu.VMEM((16, ncol), F32)] + list(_AG8_SEMS) + list(_AG8_SEMS) + _gather_sems(n), vmem_mb=40,
    )(pack, cc_row, w_ada, b_shard, *shards)


def _gather_ops(ins, outs, send_sems, recv_sems, relay=False):
    n = len(ins)
    x, y, c = lax.axis_index("x"), lax.axis_index("y"), lax.axis_index("c")
    me, sibling = (x, y, c), (x, y, 1 - c)
    chips = [(1 - x, y), (x, 1 - y), (1 - x, 1 - y)]
    direct = 2 if relay else 3

    def part(a, px, py, pc, quarter=None):
        half = ins[a].shape[0] // 2
        if quarter is None:
            return outs[a].at[2 * px + py, pl.ds(pc * half, half), :]
        return outs[a].at[2 * px + py, pl.ds(pc * half + quarter * (half // 2), half // 2), :]

    def copy(a, k, block, to, src=None, quarter=None):
        return pltpu.make_async_remote_copy(
            src_ref=part(a, *block, quarter) if src is None else src, dst_ref=part(a, *block, quarter),
            send_sem=send_sems.at[8 * a + k], recv_sem=recv_sems.at[8 * a + k], device_id=to,
            device_id_type=MESH_ID)

    def first(a, j):
        half = ins[a].shape[0] // 2
        return copy(a, j, me, (*chips[j], c), src=ins[a].at[pl.ds(c * half, half), :])

    relayed = lambda a, q: copy(a, 6 + q, (*chips[q], c), (*chips[1 - q], c), quarter=q)

    def start():
        for a in range(n):
            for j in range(direct):
                first(a, j).start()

    def forward():
        for a in range(n):
            for j in range(direct):
                copy(a, j, (*chips[j], c), me).wait_recv()
                copy(a, 3 + j, (*chips[j], c), sibling).start()
                if relay:
                    relayed(a, j).start()
            if relay:
                for q in range(2):
                    copy(a, 6 + q, (*chips[2], c), me, quarter=q).wait_recv()
                copy(a, 5, (*chips[2], c), sibling).start()

    def finish():
        for a in range(n):
            for j, chip in enumerate(chips):
                copy(a, 3 + j, (*chip, 1 - c), me).wait_recv()
        for a in range(n):
            for j, chip in enumerate(chips):
                if j < direct:
                    first(a, j).wait_send()
                    if relay:
                        relayed(a, j).wait_send()
                copy(a, 3 + j, (*chip, c), sibling).wait_send()

    return start, forward, finish


def _gather_out(shards):
    return tuple(jax.ShapeDtypeStruct((4,) + s_.shape, s_.dtype) for s_ in shards)


def _gather_sems(n):
    return [pltpu.SemaphoreType.DMA((8 * n,)), pltpu.SemaphoreType.DMA((8 * n,))]


def _pair_ops(ins, outs, send_sems, recv_sems):
    x, y, c = lax.axis_index("x"), lax.axis_index("y"), lax.axis_index("c")
    cps = []
    for a in range(len(ins)):
        half = ins[a].shape[1] // 2
        cps.append(pltpu.make_async_remote_copy(
            src_ref=ins[a].at[:, pl.ds((1 - c) * half, half), :], dst_ref=outs[a], send_sem=send_sems.at[a],
            recv_sem=recv_sems.at[a], device_id=(x, y, 1 - c), device_id_type=MESH_ID))

    def start():
        for cp in cps:
            cp.start()

    def finish():
        for cp in cps:
            cp.wait()

    return start, finish


def _comm_pair(gs):
    n = len(gs)
    return (list(gs), tuple(jax.ShapeDtypeStruct((g.shape[0], g.shape[1] // 2, g.shape[2]), g.dtype) for g in gs),
            [pltpu.SemaphoreType.DMA((n,)), pltpu.SemaphoreType.DMA((n,))], _pair_ops)


def _exchange_ops(ins, outs, send_sems, recv_sems, dests):
    x, y, c = lax.axis_index("x"), lax.axis_index("y"), lax.axis_index("c")
    mine = 2 * x + y
    chips = [(1 - x, y), (x, 1 - y), (1 - x, 1 - y)]

    def each(fn):
        for a in range(len(ins)):
            lo, hi = dests[a]
            for j, (px, py) in enumerate(chips):
                q = 2 * px + py
                cp = pltpu.make_async_remote_copy(
                    src_ref=ins[a].at[jnp.clip(q - lo, 0, hi - lo - 1)], dst_ref=outs[a].at[j],
                    send_sem=send_sems.at[3 * a + j], recv_sem=recv_sems.at[3 * a + j], device_id=(px, py, c),
                    device_id_type=MESH_ID)
                fn(cp, (q >= lo) & (q < hi), (mine >= lo) & (mine < hi), (lo, hi) == (0, 4))

    def start():
        def go(cp, send_ok, recv_ok, always):
            if always:
                cp.start()
            else:
                pl.when(send_ok)(cp.start)
        each(go)

    def finish():
        def go(cp, send_ok, recv_ok, always):
            if always:
                cp.wait()
            else:
                pl.when(send_ok)(cp.wait_send)
                pl.when(recv_ok)(cp.wait_recv)
        each(go)

    return start, finish


def _comm_exchange(hs, dests):
    n = len(hs)
    return (list(hs), tuple(jax.ShapeDtypeStruct((3,) + h.shape[1:], h.dtype) for h in hs),
            [pltpu.SemaphoreType.DMA((3 * n,)), pltpu.SemaphoreType.DMA((3 * n,))],
            lambda i, o, s, r: _exchange_ops(i, o, s, r, dests))


def _comm_gather(shards, relay=False):
    return (list(shards), _gather_out(shards), _gather_sems(len(shards)),
            lambda i, o, s, r: _gather_ops(i, o, s, r, relay))


def _carry(call, comm, steps):
    if comm is None:
        return call
    if isinstance(comm, list):
        for one in comm:
            call = _carry(call, one, steps)
        return call
    arrays, out_shape, sems, make = comm
    n, n_in, n_out = len(arrays), len(call["args"]), len(call["out_shape"])
    body = call["body"]

    def wrapped(*refs):
        base_in, cin = refs[:n_in], refs[n_in:n_in + n]
        rest = refs[n_in + n:]
        base_out, cout, scr = rest[:n_out], rest[n_out:n_out + n], rest[n_out + n:]
        ops = make(cin, cout, scr[-2], scr[-1])
        when = steps()
        pl.when(when[0])(ops[0])
        if len(ops) == 3 and len(when) == 3:
            pl.when(when[2])(ops[1])
        body(*base_in, *base_out, *scr[:-2])
        if len(ops) == 3 and len(when) == 2:
            pl.when(when[1])(ops[1])
        pl.when(when[1])(ops[-1])

    anyspec = pl.BlockSpec(memory_space=pl.ANY)
    return dict(call, body=wrapped, args=list(call["args"]) + arrays,
                in_specs=list(call["in_specs"]) + [anyspec] * n,
                out_shape=tuple(call["out_shape"]) + tuple(out_shape),
                out_specs=tuple(call["out_specs"]) + (anyspec,) * n,
                scratch=list(call["scratch"]) + sems)


def _run(call):
    args = call.pop("args")
    body = call.pop("body")
    return _pcall(body, **call)(*args)


def _pair_swap(rs, sv):
    n = len(rs)

    def body(sv_ref, *refs):
        ins, outs, got_ref = refs[:n], refs[n:2 * n], refs[2 * n]
        send_sems, recv_sems, s1, r1, l1 = refs[2 * n + 1:]
        x, y, c = lax.axis_index("x"), lax.axis_index("y"), lax.axis_index("c")
        cps = [pltpu.make_async_remote_copy(
            src_ref=ins[a], dst_ref=outs[a], send_sem=send_sems.at[a], recv_sem=recv_sems.at[a],
            device_id=(x, y, 1 - c), device_id_type=MESH_ID) for a in range(n)]
        for cp in cps:
            cp.start()
        _allgather8_ops(sv_ref, got_ref, s1, r1, l1)
        for cp in cps:
            cp.wait()

    vm, anyspec = pl.BlockSpec(memory_space=pltpu.VMEM), pl.BlockSpec(memory_space=pl.ANY)
    return _pcall(
        body, name="grads_pair_swap",
        out_shape=tuple(jax.ShapeDtypeStruct(r.shape, r.dtype) for r in rs)
        + (jax.ShapeDtypeStruct((8 * sv.shape[0], sv.shape[1]), sv.dtype),),
        in_specs=[vm] + [anyspec] * n, out_specs=(anyspec,) * n + (vm,),
        scratch=[pltpu.SemaphoreType.DMA((n,)), pltpu.SemaphoreType.DMA((n,))] + list(_AG8_SEMS),
    )(sv, *rs)


SUM_STEPS = 4


def _pair_sum(gs, recvs, core, name):
    n = len(gs)

    def body(c_ref, *refs):
        for a in range(n):
            refs[2 * n + a][...] = (refs[a][...].astype(F32) + refs[n + a][...].astype(F32)).astype(refs[2 * n + a].dtype)

    blk = lambda g: (g.shape[0], g.shape[1] // (2 * SUM_STEPS), g.shape[2])
    return pl.pallas_call(
        body, name=name,
        out_shape=tuple(jax.ShapeDtypeStruct((g.shape[0], g.shape[1] // 2, g.shape[2]), g.dtype) for g in gs),
        grid_spec=pltpu.PrefetchScalarGridSpec(
            num_scalar_prefetch=1, grid=(SUM_STEPS,),
            in_specs=[pl.BlockSpec(blk(g), lambda i, cr: (0, cr[0] * SUM_STEPS + i, 0)) for g in gs]
            + [pl.BlockSpec(blk(g), lambda i, cr: (0, i, 0)) for g in gs],
            out_specs=tuple(pl.BlockSpec(blk(g), lambda i, cr: (0, i, 0)) for g in gs)),
        compiler_params=pltpu.CompilerParams(vmem_limit_bytes=40 << 20), interpret=_INTERPRET,
    )(core, *gs, *recvs)


def _chip_sum(hs, recvs, chip, dests, slots):
    n = len(hs)
    nout = max(slots) + 1
    first = [slots.index(o) for o in range(nout)]
    every = lambda d_: d_ == (0, 4)

    def own(d_):
        if every(d_):
            return lambda i, kr: (kr[0], i, 0)
        return lambda i, kr: (0, jnp.where(kr[0] == d_[0], i, 0), 0)

    def got(d_):
        if every(d_):
            return lambda i, kr: (0, i, 0)
        return lambda i, kr: (0, jnp.where(kr[0] == d_[0], i, 0), 0)

    def body(k_ref, *refs):
        for a in range(n):
            def emit(a=a):
                acc = refs[a][0].astype(F32)
                for j in range(3):
                    acc = acc + refs[n + a][j].astype(F32)
                refs[2 * n + slots[a]][...] = acc
            if every(dests[a]):
                emit()
            else:
                pl.when(k_ref[0] == dests[a][0])(emit)

    rb = lambda h: h.shape[1] // SUM_STEPS
    return pl.pallas_call(
        body, name="grads_chip_sum",
        out_shape=tuple(jax.ShapeDtypeStruct(hs[a].shape[1:], F32) for a in first),
        grid_spec=pltpu.PrefetchScalarGridSpec(
            num_scalar_prefetch=1, grid=(SUM_STEPS,),
            in_specs=[pl.BlockSpec((1, rb(h), h.shape[2]), own(d_)) for h, d_ in zip(hs, dests)]
            + [pl.BlockSpec((3, rb(h), h.shape[2]), got(d_)) for h, d_ in zip(hs, dests)],
            out_specs=tuple(pl.BlockSpec((rb(hs[a]), hs[a].shape[2]), lambda i, kr: (i, 0)) for a in first)),
        compiler_params=pltpu.CompilerParams(vmem_limit_bytes=40 << 20), interpret=_INTERPRET,
    )(chip, *hs, *recvs)


def _ada_bwd(araw, dmod, w):
    nblk = w.shape[1] // 512

    def body(a_ref, d_ref, w_ref, gw_ref, da_ref):
        j = pl.program_id(0)
        gw_ref[...] = _tn(_silu(a_ref[...]), d_ref[...])
        part = _nt(d_ref[...], w_ref[...])

        @pl.when(j == 0)
        def _():
            da_ref[...] = part

        @pl.when(j > 0)
        def _():
            da_ref[...] += part

    return _pcall(
        body, name="ada_bwd",
        out_shape=(jax.ShapeDtypeStruct(w.shape, F32), jax.ShapeDtypeStruct((16, D), F32)), grid=(nblk,),
        in_specs=[_full((16, D)), pl.BlockSpec((16, 512), lambda j: (0, j)), pl.BlockSpec((D, 512), lambda j: (0, j))],
        out_specs=(pl.BlockSpec((D, 512), lambda j: (0, j)), _full((16, D))), sem=("arbitrary",),
    )(araw, dmod, w)


def _w_specs():
    return [pl.BlockSpec((None, 2 * D, D), lambda j, i: (j, 0, 0)),
            pl.BlockSpec((None, WTAIL, D), lambda j, i: (jnp.maximum(j - 1, 0), 0, 0)),
            pl.BlockSpec((None, WTAIL, D), lambda j, i: (3, 0, 0))]


def _inproj(xin, mods, wi_main, wi_tail, t_total, tb, blk_off, prev, name, comm=None):
    n = xin.shape[0]
    nt = n // tb
    nslab = 4

    def body(x_ref, mod_ref, w_ref, wb_ref, wdt_ref, *rest):
        p_ref, pdt_ref, u_ref, uscr = rest[-4:]
        j, i = pl.program_id(0), pl.program_id(1)
        rows = pl.ds(pl.multiple_of(i * tb, tb), tb)

        @pl.when(j == 0)
        def _():
            xv = x_ref[...]
            r = lax.rsqrt(jnp.mean(xv * xv, axis=1, keepdims=True) + EPS)
            u = (xv * r * mod_ref[2:3, :]) * mod_ref[0:1, :] + mod_ref[1:2, :]
            ub = u.astype(MXU_DTYPE)
            uscr[rows, :] = ub
            u_ref[...] = ub
            pdt_ref[...] = _nt(ub, wdt_ref[...])

        ub = uscr[rows, :]
        pv = _nt(ub, w_ref[0:D, :])
        p_ref[:, D:2 * D] = _nt(ub, w_ref[D:2 * D, :]).astype(p_ref.dtype)

        @pl.when(j == 0)
        def _():
            p_ref[:, 0:D] = pv.astype(p_ref.dtype)

        @pl.when(j > 0)
        def _():
            head = pv[:, 0:WTAIL] + _nt(ub, wb_ref[...])
            p_ref[:, 0:D] = jnp.concatenate([head, pv[:, WTAIL:]], axis=1).astype(p_ref.dtype)

    once = lambda j, i: (jnp.where(j == 0, i, nt - 1) + blk_off, 0)
    in_specs = [pl.BlockSpec((tb, D), lambda j, i: (jnp.where(j == 0, i, nt - 1), 0)), _full((8, D))] + _w_specs()
    args = [xin, mods, wi_main, wi_tail, wi_tail]
    aliases = None
    if prev is not None:
        in_specs += [pl.BlockSpec(memory_space=pl.ANY)] * 3
        args += list(prev)
        aliases = {5: 0, 6: 1, 7: 2}
    call = dict(
        body=body, args=args, name=name,
        out_shape=(jax.ShapeDtypeStruct((t_total, nslab * 2 * D), MXU_DTYPE),
                   jax.ShapeDtypeStruct((t_total, 128), F32), jax.ShapeDtypeStruct((t_total, D), MXU_DTYPE)),
        grid=(nslab, nt), in_specs=in_specs,
        out_specs=(pl.BlockSpec((tb, 2 * D), lambda j, i: (i + blk_off, j)), pl.BlockSpec((tb, 128), once),
                   pl.BlockSpec((tb, D), once)),
        scratch=[pltpu.VMEM((n, D), MXU_DTYPE)], sem=("arbitrary", "arbitrary"), vmem_mb=56, aliases=aliases)
    steps = lambda: ((pl.program_id(0) == 0) & (pl.program_id(1) == 0),
                     (pl.program_id(0) == nslab - 1) & (pl.program_id(1) == nt - 1),
                     (pl.program_id(0) == nslab - 1) & (pl.program_id(1) == nt // 2))
    return _run(_carry(call, comm, steps))


def _blk(s, nb, rev):
    return jnp.where(s == 0, nb - 1, (nb - 1 - s) if rev else (s - 1))


def _hgrn_gate(fr, lbraw_ref, d):
    lb = _sig(lbraw_ref[d:d + 1, :] - lbraw_ref[2 + d:3 + d, :])
    sg = _sig(fr)
    return lb, sg, lb + (1.0 - lb) * sg


def _hgrn_fwd(p_main, lbraw, d, nb, comm=None):
    t_total = p_main.shape[0]
    rev = d == 1
    nch = TB // HC
    scale = HF ** -0.5

    def body(q_ref, f_ref, v_ref, lb_ref, o_ref, sp_ref, st):
        s = pl.program_id(0)

        @pl.when(s == 0)
        def _():
            st[...] = jnp.zeros_like(st)

        mb = _tri(HC, rev)
        m01 = _b01(mb)
        order = list(reversed(range(nch)) if rev else range(nch))
        hs_ = [slice(h * HF, (h + 1) * HF) for h in range(NH)]
        pre = {}
        for c in order:
            rows = slice(c * HC, (c + 1) * HC)
            _, _, f = _hgrn_gate(f_ref[rows, :].astype(F32), lb_ref, d)
            k = 1.0 - f
            cum = _dot01(m01, jnp.log(f))
            tot = cum[0:1, :] if rev else cum[HC - 1:HC, :]
            qd = _silu(q_ref[rows, :].astype(F32)) * scale * jnp.exp(cum)
            ki = k * jnp.exp(-cum)
            etot = jnp.exp(tot)
            pre[c] = (_mx(qd), _mx(ki), _mx(ki * etot), _mx(v_ref[rows, :]), etot)
        scs = {c: [_nt(pre[c][0][:, cs], pre[c][1][:, cs]) for cs in hs_] for c in order}
        upd = {c: [_tn(pre[c][3][:, cs], pre[c][2][:, cs]) for cs in hs_] for c in order}
        intra = {c: [_nn(jnp.where(mb, scs[c][h], 0.0), pre[c][3][:, cs]) for h, cs in enumerate(hs_)] for c in order}
        for c in order:
            rows = slice(c * HC, (c + 1) * HC)
            qdb, etot = pre[c][0], pre[c][4]
            for h, cs in enumerate(hs_):
                sth = st[h]
                stb = sth.astype(sp_ref.dtype)
                sp_ref[c, h] = stb
                o_ref[rows, cs] = (intra[c][h] + _nt(qdb[:, cs], stb)).astype(o_ref.dtype)
                st[h] = sth * etot[:, cs] + upd[c][h]

    col = lambda j: (lambda s: (_blk(s, nb, rev), j))
    call = dict(
        body=body, args=[p_main, p_main, p_main, lbraw], name=f"hgrn_fwd_{d}",
        out_shape=(jax.ShapeDtypeStruct((t_total, D), MXU_DTYPE),
                   jax.ShapeDtypeStruct((nch * nb, NH, HF, HF), MXU_DTYPE)),
        grid=(nb,),
        in_specs=[pl.BlockSpec((TB, D), col(0)), pl.BlockSpec((TB, D), col(1 + d)), pl.BlockSpec((TB, D), col(3)),
                  _full((8, D))],
        out_specs=(pl.BlockSpec((TB, D), col(0)),
                   pl.BlockSpec((nch, NH, HF, HF), lambda s: (_blk(s, nb, rev), 0, 0, 0))),
        scratch=[pltpu.VMEM((NH, HF, HF), F32)], sem=("arbitrary",), vmem_mb=40)
    return _run(_carry(call, comm, lambda: (pl.program_id(0) == 0, pl.program_id(0) == nb - 1,
                                            pl.program_id(0) == nb - 4)))


def _hgrn_bwd(p_main, lbraw, sprev, do, d, nb, prev, comm=None):
    t_total = p_main.shape[0]
    rev = d == 1
    nch = TB // HC
    scale = HF ** -0.5
    last = prev is not None
    odt = MXU_DTYPE if last else F32

    def body(q_ref, f_ref, v_ref, lb_ref, sp_ref, do_ref, *rest):
        if last:
            dqp_ref, dvp_ref = rest[:2]
            rest = rest[2:]
        dq_ref, df_ref, dv_ref, dlb_ref, dst = rest
        sp_id = pl.program_id(0)
        is_ctx = sp_id == nb - 1

        @pl.when(sp_id == 0)
        def _():
            dst[...] = jnp.zeros_like(dst)
            dlb_ref[...] = jnp.zeros_like(dlb_ref)

        mb = _tri(HC, rev)
        mbt = _tri(HC, not rev)
        m01 = _b01(mb)
        mt01 = _b01(mbt)
        order = list(range(nch) if rev else reversed(range(nch)))
        hs_ = [slice(h * HF, (h + 1) * HF) for h in range(NH)]
        pre = {}
        for c in order:
            rows = slice(c * HC, (c + 1) * HC)
            lb, sg, f = _hgrn_gate(f_ref[rows, :].astype(F32), lb_ref, d)
            k = 1.0 - f
            cum = _dot01(m01, jnp.log(f))
            tot = cum[0:1, :] if rev else cum[HC - 1:HC, :]
            e = jnp.exp(cum)
            ei = jnp.exp(-cum)
            etot = jnp.exp(tot)
            ee = ei * etot
            qraw = q_ref[rows, :].astype(F32)
            sq = _sig(qraw)
            qd = qraw * sq * scale * e
            ki = k * ei
            ke = k * ee
            dov = jnp.where(is_ctx, 0.0, do_ref[rows, :].astype(F32))
            pre[c] = dict(lb=lb, sg=sg, f=f, e=e, ei=ei, ee=ee, etot=etot, qd=qd, ki=ki, ke=ke,
                          dsq=sq * (1.0 + qraw * (1.0 - sq)),
                          qdb=_mx(qd), kib=_mx(ki), keb=_mx(ke), vb=_mx(v_ref[rows, :]), dob=_mx(dov))
        units = [(c, h) for c in order for h in range(NH)]
        col = lambda u, key: pre[u[0]][key][:, hs_[u[1]]]
        pt = {u: jnp.where(mbt, _nt(col(u, "kib"), col(u, "qdb")), 0.0) for u in units}
        dp = {u: jnp.where(mb, _nt(col(u, "dob"), col(u, "vb")), 0.0) for u in units}
        dpt = {u: jnp.where(mbt, _nt(col(u, "vb"), col(u, "dob")), 0.0) for u in units}
        dv_i = {u: _nn(pt[u], col(u, "dob")) for u in units}
        dqd_ = {u: _nn(dp[u], col(u, "kib")) + _nn(col(u, "dob"), sp_ref[u[0], u[1]]) for u in units}
        dki_ = {u: _nn(dpt[u], col(u, "qdb")) for u in units}
        dsl = {u: _tn(col(u, "dob"), col(u, "qdb")) for u in units}
        for c in order:
            rows = slice(c * HC, (c + 1) * HC)
            p = pre[c]
            dv_l, dke_l, dtot_l = [], [], []
            for h, cs in enumerate(hs_):
                dso = dst[h]
                dsob = _mx(dso)
                dv_l.append(dv_i[(c, h)] + _nt(p["keb"][:, cs], dsob))
                dke_l.append(_nn(p["vb"][:, cs], dsob))
                dtot_l.append(_colsum(dso * sp_ref[c, h].astype(F32)) * p["etot"][:, cs])
                dst[h] = dso * p["etot"][:, cs] + dsl[(c, h)]
            lb, sg, f, e, ei, ee, qd, ki, ke = (p[n_] for n_ in ("lb", "sg", "f", "e", "ei", "ee", "qd", "ki", "ke"))
            dqd = jnp.concatenate([dqd_[(c, h)] for h in range(NH)], axis=1)
            dki = jnp.concatenate([dki_[(c, h)] for h in range(NH)], axis=1)
            dke = jnp.concatenate(dke_l, axis=1)
            dcum = dqd * qd - dki * ki - dke * ke
            dtot = jnp.concatenate(dtot_l, axis=1) + _colsum(dke * ke)
            dk = dki * ei + dke * ee
            dlf = _dot01(mt01, dcum, ways=2) + dtot
            df = dlf / f - dk
            dlb_ref[0:1, :] += _colsum(df * (1.0 - sg))
            dfr = df * (1.0 - lb) * sg * (1.0 - sg)
            dq = dqd * e * scale * p["dsq"]
            dv = jnp.concatenate(dv_l, axis=1)
            if last:
                dq = dq + dqp_ref[rows, :]
                dv = dv + dvp_ref[rows, :]
            dq_ref[rows, :] = dq.astype(odt)
            dv_ref[rows, :] = dv.astype(odt)
            df_ref[rows, :] = dfr.astype(MXU_DTYPE)

    blk = lambda s: _blk(nb - 1 - s, nb, rev)
    col = lambda j: (lambda s: (blk(s), j))
    in_specs = [pl.BlockSpec((TB, D), col(0)), pl.BlockSpec((TB, D), col(1 + d)), pl.BlockSpec((TB, D), col(3)),
                _full((8, D)), pl.BlockSpec((nch, NH, HF, HF), lambda s: (blk(s), 0, 0, 0)),
                pl.BlockSpec((TB, D), lambda s: (jnp.minimum(blk(s), nb - 2), 0))]
    args = [p_main, p_main, p_main, lbraw, sprev, do]
    if last:
        in_specs += [pl.BlockSpec((TB, D), col(0))] * 2
        args += list(prev)
    call = dict(
        body=body, args=args, name=f"hgrn_bwd_{d}",
        out_shape=(jax.ShapeDtypeStruct((t_total, D), odt), jax.ShapeDtypeStruct((t_total, D), MXU_DTYPE),
                   jax.ShapeDtypeStruct((t_total, D), odt), jax.ShapeDtypeStruct((8, D), F32)),
        grid=(nb,), in_specs=in_specs,
        out_specs=(pl.BlockSpec((TB, D), col(0)), pl.BlockSpec((TB, D), col(0)), pl.BlockSpec((TB, D), col(0)),
                   _full((8, D))),
        scratch=[pltpu.VMEM((NH, HF, HF), F32)], sem=("arbitrary",), vmem_mb=48)
    return _run(_carry(call, comm, lambda: (pl.program_id(0) == 0, pl.program_id(0) == nb - 1)))


def _conv_masks(tb, is_ctx):
    seg = jnp.where(is_ctx, tb, GRID_W)
    pos = lax.broadcasted_iota(jnp.int32, (tb, 1), 0) & (seg - 1)
    return pos, seg


def _shift_rows(x, dshift, pos, seg):
    if dshift == 0:
        return x
    n = x.shape[0]
    rolled = pltpu.roll(x, (-dshift) % n, 0)
    ok = (pos + dshift >= 0) & (pos + dshift < seg)
    return jnp.where(ok, rolled, 0.0)


def _ssd_prep(p_main, p_dt, convp, dtb, nb):
    t_total = p_main.shape[0]

    def body(x_ref, dt_ref, cw_ref, dtb_ref, xa_ref, ds_ref, dts_ref):
        is_ctx = pl.program_id(0) == nb - 1
        pos, seg = _conv_masks(TB, is_ctx)
        xv = x_ref[...].astype(F32)
        acc = cw_ref[5:6, :] + cw_ref[2:3, :] * xv
        for kk in (0, 1, 3, 4):
            acc = acc + cw_ref[kk:kk + 1, :] * _shift_rows(xv, kk - 2, pos, seg)
        sg = _sig(acc)
        xa_ref[...] = (acc * sg).astype(xa_ref.dtype)
        ds_ref[...] = (sg * (1.0 + acc * (1.0 - sg))).astype(ds_ref.dtype)
        dts_ref[...] = _softplus(dt_ref[...] + dtb_ref[0:1, :])

    wide = pl.BlockSpec((TB, 2048), lambda i: (i, 0))
    return _pcall(
        body, name="ssd_prep",
        out_shape=(jax.ShapeDtypeStruct((t_total, 2048), MXU_DTYPE), jax.ShapeDtypeStruct((t_total, 2048), MXU_DTYPE),
                   jax.ShapeDtypeStruct((t_total, 128), F32)),
        grid=(nb,),
        in_specs=[pl.BlockSpec((TB, 2048), lambda i: (i, 3)), pl.BlockSpec((TB, 128), lambda i: (i, 0)),
                  _full((8, 2048)), _full((8, 128))],
        out_specs=(wide, wide, pl.BlockSpec((TB, 128), lambda i: (i, 0))),
        sem=("parallel",), vmem_mb=32,
    )(p_main, p_dt, convp, dtb)


def _ssd_prep_bwd(p_main, p_dt, convp, dtb, dsl, dxa, dxs_skip, ddts, nb):
    t_total = p_main.shape[0]

    def body(x_ref, dt_ref, cw_ref, dtb_ref, ds_ref, dxa_ref, dsk_ref, ddts_ref, dx_ref, ddt_ref, dcw_ref, ddtb_ref):
        i = pl.program_id(0)
        is_ctx = i == nb - 1

        @pl.when(i == 0)
        def _():
            dcw_ref[...] = jnp.zeros_like(dcw_ref)
            ddtb_ref[...] = jnp.zeros_like(ddtb_ref)

        pos, seg = _conv_masks(TB, is_ctx)
        xv = x_ref[...].astype(F32)
        dact = dxa_ref[...]
        dact = jnp.concatenate([dact[:, :D] + jnp.where(is_ctx, 0.0, dsk_ref[...].astype(F32)), dact[:, D:]], axis=1)
        dpre = dact * ds_ref[...].astype(F32)
        dxv = cw_ref[2:3, :] * dpre
        dcw_ref[2:3, :] += _colsum(xv * dpre)
        for kk in (0, 1, 3, 4):
            sdp = _shift_rows(dpre, 2 - kk, pos, seg)
            dxv = dxv + cw_ref[kk:kk + 1, :] * sdp
            dcw_ref[kk:kk + 1, :] += _colsum(xv * sdp)
        dx_ref[...] = dxv.astype(dx_ref.dtype)
        dcw_ref[5:6, :] += _colsum(dpre)
        draw = ddts_ref[...] * _sig(dt_ref[...] + dtb_ref[0:1, :])
        ddt_ref[...] = draw.astype(ddt_ref.dtype)
        ddtb_ref[0:1, :] += _colsum(draw)

    return _pcall(
        body, name="ssd_prep_bwd",
        out_shape=(jax.ShapeDtypeStruct((t_total, 2048), MXU_DTYPE), jax.ShapeDtypeStruct((t_total, 128), MXU_DTYPE),
                   jax.ShapeDtypeStruct((8, 2048), F32), jax.ShapeDtypeStruct((8, 128), F32)),
        grid=(nb,),
        in_specs=[pl.BlockSpec((TB, 2048), lambda i: (i, 3)), pl.BlockSpec((TB, 128), lambda i: (i, 0)),
                  _full((8, 2048)), _full((8, 128)), pl.BlockSpec((TB, 2048), lambda i: (i, 0)),
                  pl.BlockSpec((TB, 2048), lambda i: (i, 0)),
                  pl.BlockSpec((TB, D), lambda i: (jnp.minimum(i, nb - 2), 0)),
                  pl.BlockSpec((TB, 128), lambda i: (i, 0))],
        out_specs=(pl.BlockSpec((TB, 2048), lambda i: (i, 0)), pl.BlockSpec((TB, 128), lambda i: (i, 0)),
                   _full((8, 2048)), _full((8, 128))),
        sem=("arbitrary",), vmem_mb=40,
    )(p_main, p_dt, convp, dtb, dsl, dxa, dxs_skip, ddts)


def _dot2(x, m01):
    hi = x.astype(BF16)
    lo = (x - hi.astype(F32)).astype(BF16)
    f = lambda t: lax.dot_general(t, m01, (((1,), (0,)), ((), ())), preferred_element_type=F32)
    return f(hi) + f(lo)


def _head_lanes(c0, c1):
    p = lax.broadcasted_iota(jnp.int32, (128, 128), 0)
    l = lax.broadcasted_iota(jnp.int32, (128, 128), 1)
    return _b01(((l == c0) & (p < SP)) | ((l == c1) & (p >= SP)))


def _one_lane(col):
    return _b01(lax.broadcasted_iota(jnp.int32, (128, 128), 1) == col)


def _lane_pick(x, lane, col):
    return _rowsum(jnp.where(lane == col, x, 0.0))


def _ssd_chunk_common(dts, alog_ref, m01, rev):
    lane = lax.broadcasted_iota(jnp.int32, (1, 128), 1)
    arow = -jnp.exp(alog_ref[0:1, :])
    cum = _dot01(m01, dts * arow)
    tot = cum[0:1, :] if rev else cum[SC - 1:SC, :]
    return lane, arow, cum, cum.T, tot


def _ssd_fwd(xa, dts, alog, d, nb):
    t_total = xa.shape[0]
    rev = d == 1
    nch = TB // SC
    npair = SHEADS // 2

    def body(xa_ref, dts_ref, alog_ref, y_ref, sp_ref, st):
        s = pl.program_id(0)

        @pl.when(s == 0)
        def _():
            st[...] = jnp.zeros_like(st)

        mb = _tri(SC, rev)
        m01 = _b01(mb)
        lo = lax.broadcasted_iota(jnp.int32, (1, 128), 1) < SP
        rlo = lax.broadcasted_iota(jnp.int32, (128, 1), 0) < SP
        order = list(reversed(range(nch)) if rev else range(nch))
        pre = {}
        for c in order:
            rows = slice(c * SC, (c + 1) * SC)
            dts_c = dts_ref[rows, :]
            lane, arow, cum, cumt, tot = _ssd_chunk_common(dts_c, alog_ref, m01, rev)
            bgs = [_mx(xa_ref[rows, D + g * SN:D + (g + 1) * SN]) for g in range(4)]
            cgs = [_mx(xa_ref[rows, D + 512 + g * SN:D + 512 + (g + 1) * SN]) for g in range(4)]
            pairs = []
            for pr in range(npair):
                xs = xa_ref[rows, pr * 128:(pr + 1) * 128].astype(F32)
                cols = [16 * d + 2 * pr, 16 * d + 2 * pr + 1]
                cum_c = [_lane_pick(cum, lane, q) for q in cols]
                dt_c = [_lane_pick(dts_c, lane, q) for q in cols]
                tot_c = [_lane_pick(tot, lane, q) for q in cols]
                dtx = xs * jnp.where(lo, dt_c[0], dt_c[1])
                e1_pair = jnp.where(lo, jnp.exp(cum_c[0]), jnp.exp(cum_c[1]))
                e2_pair = jnp.where(lo, jnp.exp(tot_c[0] - cum_c[0]), jnp.exp(tot_c[1] - cum_c[1]))
                etot_col = jnp.where(rlo, jnp.exp(tot_c[0]), jnp.exp(tot_c[1]))
                decs = [jnp.where(mb, jnp.exp(cum_c[q] - cumt[cols[q]:cols[q] + 1, :]), 0.0) for q in range(2)]
                dtxq = [_mx(jnp.where(lo if q == 0 else ~lo, dtx, 0.0)) for q in range(2)]
                pairs.append(dict(e1=e1_pair, etot=etot_col, decs=decs, dtxq=dtxq, xe=_mx(dtx * e2_pair)))
            pre[c] = (bgs, cgs, pairs)
        gm = {(c, g): _nt(pre[c][1][g], pre[c][0][g]) for c in order for g in range(4)}
        upd = {(c, pr): _tn(pre[c][2][pr]["xe"], pre[c][0][pr // 2]) for c in order for pr in range(npair)}
        intra = {(c, pr): sum(_nn(gm[(c, pr // 2)] * pre[c][2][pr]["decs"][q], pre[c][2][pr]["dtxq"][q]) for q in range(2))
                 for c in order for pr in range(npair)}
        for c in order:
            rows = slice(c * SC, (c + 1) * SC)
            bgs, cgs, pairs = pre[c]
            for pr in range(npair):
                stp = st[pr]
                stb = stp.astype(sp_ref.dtype)
                sp_ref[c, pr] = stb
                y_ref[rows, pr * 128:(pr + 1) * 128] = (
                    intra[(c, pr)] + pairs[pr]["e1"] * _nt(cgs[pr // 2], stb)).astype(y_ref.dtype)
                st[pr] = stp * pairs[pr]["etot"] + upd[(c, pr)]

    blk = lambda s: _blk(s, nb, rev)
    return _pcall(
        body, name=f"ssd_fwd_{d}",
        out_shape=(jax.ShapeDtypeStruct((t_total, D), MXU_DTYPE),
                   jax.ShapeDtypeStruct((nch * nb, npair, 128, SN), MXU_DTYPE)),
        grid=(nb,),
        in_specs=[pl.BlockSpec((TB, 2048), lambda s: (blk(s), 0)), pl.BlockSpec((TB, 128), lambda s: (blk(s), 0)),
                  _full((8, 128))],
        out_specs=(pl.BlockSpec((TB, D), lambda s: (blk(s), 0)),
                   pl.BlockSpec((nch, npair, 128, SN), lambda s: (blk(s), 0, 0, 0))),
        scratch=[pltpu.VMEM((npair, 128, SN), F32)], sem=("arbitrary",), vmem_mb=40,
    )(xa, dts, alog)


def _ssd_bwd(xa, dts, alog, sprev, dy, d, nb, prev, comm=None):
    t_total = xa.shape[0]
    rev = d == 1
    nch = TB // SC
    npair = SHEADS // 2
    last = prev is not None

    def body(xa_ref, dts_ref, alog_ref, sp_ref, dy_ref, *rest):
        if last:
            dxp_ref, ddp_ref = rest[:2]
            rest = rest[2:]
        dxa_ref, ddts_ref, da_ref, dst, zc_scr = rest
        sp_id = pl.program_id(0)
        is_ctx = sp_id == nb - 1

        @pl.when(sp_id == 0)
        def _():
            dst[...] = jnp.zeros_like(dst)
            da_ref[...] = jnp.zeros_like(da_ref)
            zc_scr[...] = jnp.zeros_like(zc_scr)

        mb = _tri(SC, rev)
        m01 = _b01(mb)
        mt01 = _b01(_tri(SC, not rev))
        lo = lax.broadcasted_iota(jnp.int32, (1, 128), 1) < SP
        rlo = lax.broadcasted_iota(jnp.int32, (128, 1), 0) < SP
        order = list(range(nch) if rev else reversed(range(nch)))
        pre = {}
        for c in order:
            rows = slice(c * SC, (c + 1) * SC)
            dts_c = dts_ref[rows, :]
            lane, arow, cum, cumt, tot = _ssd_chunk_common(dts_c, alog_ref, m01, rev)
            pairs = []
            for pr in range(npair):
                xs = xa_ref[rows, pr * 128:(pr + 1) * 128].astype(F32)
                dyp = jnp.where(is_ctx, 0.0, dy_ref[rows, pr * 128:(pr + 1) * 128].astype(F32))
                cols = [16 * d + 2 * pr, 16 * d + 2 * pr + 1]
                cum_c = [_lane_pick(cum, lane, q) for q in cols]
                dt_c = [_lane_pick(dts_c, lane, q) for q in cols]
                tot_c = [_lane_pick(tot, lane, q) for q in cols]
                e1_c = [jnp.exp(cum_c[q]) for q in range(2)]
                e2_c = [jnp.exp(tot_c[q] - cum_c[q]) for q in range(2)]
                etot_c = [jnp.exp(tot_c[q]) for q in range(2)]
                dt_pair = jnp.where(lo, dt_c[0], dt_c[1])
                e1_pair = jnp.where(lo, e1_c[0], e1_c[1])
                e2_pair = jnp.where(lo, e2_c[0], e2_c[1])
                dtx = xs * dt_pair
                decs = [jnp.where(mb, jnp.exp(cum_c[q] - cumt[cols[q]:cols[q] + 1, :]), 0.0) for q in range(2)]
                dyq = [_mx(jnp.where(lo if q == 0 else ~lo, dyp, 0.0)) for q in range(2)]
                pairs.append(dict(xs=xs, dyp=dyp, cols=cols, e1_c=e1_c, e2_c=e2_c, etot_c=etot_c, dt_pair=dt_pair,
                                  e2_pair=e2_pair, etot_col=jnp.where(rlo, etot_c[0], etot_c[1]), dtx=dtx,
                                  dtxb=_mx(dtx), xeb=_mx(dtx * e2_pair), dy0b=_mx(dyp * e1_pair), decs=decs, dyq=dyq))
            pre[c] = dict(lane=lane, arow=arow, dts=dts_c, pairs=pairs, cum=cum, tot=tot,
                          bgb=[_mx(xa_ref[rows, D + g * SN:D + (g + 1) * SN]) for g in range(4)],
                          cgb=[_mx(xa_ref[rows, D + 512 + g * SN:D + 512 + (g + 1) * SN]) for g in range(4)])
        units = [(c, pr) for c in order for pr in range(npair)]
        head_lanes = [_head_lanes(16 * d + 2 * pr, 16 * d + 2 * pr + 1) for pr in range(npair)]
        one_lane = {16 * d + h: _one_lane(16 * d + h) for h in range(SHEADS)}
        P = lambda u: pre[u[0]]["pairs"][u[1]]
        cgu = lambda u: pre[u[0]]["cgb"][u[1] // 2]
        gm = {(c, g): _nt(pre[c]["cgb"][g], pre[c]["bgb"][g]) for c in order for g in range(4)}
        y0 = {u: _nt(cgu(u), sp_ref[u[0], u[1]]) for u in units}
        dcg_i = {u: _nn(P(u)["dy0b"], sp_ref[u[0], u[1]]) for u in units}
        dsl = {u: _tn(P(u)["dy0b"], cgu(u)) for u in units}
        w_ = {(u, q): gm[(u[0], u[1] // 2)] * P(u)["decs"][q] for u in units for q in range(2)}
        dw_ = {(u, q): jnp.where(mb, _nt(P(u)["dyq"][q], P(u)["dtxb"]), 0.0) for u in units for q in range(2)}
        ddtx_i = {(u, q): _tn(w_[(u, q)], P(u)["dyq"][q]) for u in units for q in range(2)}
        for c in order:
            rows = slice(c * SC, (c + 1) * SC)
            pc = pre[c]
            lane, arow, dts_c = pc["lane"], pc["arow"], pc["dts"]
            d1 = jnp.zeros((SC, 128), F32)
            d2 = jnp.zeros((SC, 128), F32)
            dz = jnp.zeros((SC, 128), F32)
            ddt = jnp.zeros((SC, 128), F32)
            dtot = jnp.zeros((1, 128), F32)
            dgm = [jnp.zeros((SC, SC), F32) for _ in range(4)]
            dbg = [jnp.zeros((SC, SN), F32) for _ in range(4)]
            dcg = [jnp.zeros((SC, SN), F32) for _ in range(4)]
            for pr in range(npair):
                u, g, p = (c, pr), pr // 2, pc["pairs"][pr]
                hs = head_lanes[pr]
                dso = dst[pr]
                dsob = _mx(dso)
                dxe = _nt(pc["bgb"][g], dsob)
                dbg[g] = dbg[g] + _nn(p["xeb"], dsob)
                ddtx = dxe * p["e2_pair"]
                d2 = d2 + _dot2(dxe * p["dtx"], hs)
                dcg[g] = dcg[g] + dcg_i[u]
                d1 = d1 + _dot2(p["dyp"] * y0[u], hs)
                sprod = dso * sp_ref[c, pr].astype(F32)
                dst[pr] = dso * p["etot_col"] + dsl[u]
                for q in range(2):
                    hm = lo if q == 0 else ~lo
                    col = p["cols"][q]
                    dw = dw_[(u, q)]
                    ddtx = ddtx + jnp.where(hm, ddtx_i[(u, q)], 0.0)
                    dgm[g] = dgm[g] + dw * p["decs"][q]
                    z = dw * w_[(u, q)]
                    dz = dz + _dot2(z, one_lane[col])
                    zc_scr[col:col + 1, :] = _colsum(z)
                    tsum = _rowsum(_colsum(sprod[q * SP:(q + 1) * SP, :]))
                    dtot = jnp.where(lane == col, tsum * p["etot_c"][q], dtot)
                dxs = ddtx * p["dt_pair"]
                ddt = ddt + _dot2(ddtx * p["xs"], hs)
                if last:
                    dxs = dxs + dxp_ref[rows, pr * 128:(pr + 1) * 128]
                dxa_ref[rows, pr * 128:(pr + 1) * 128] = dxs
            e2_all = jnp.exp(pc["tot"] - pc["cum"])
            dcum = dz - zc_scr[...].T + d1 * jnp.exp(pc["cum"]) - d2 * e2_all
            dtot = dtot + _colsum(d2 * e2_all)
            for g in range(4):
                db = dbg[g] + _tn(dgm[g], pc["cgb"][g])
                dc = dcg[g] + _nn(dgm[g], pc["bgb"][g])
                if last:
                    db = db + dxp_ref[rows, D + g * SN:D + (g + 1) * SN]
                    dc = dc + dxp_ref[rows, D + 512 + g * SN:D + 512 + (g + 1) * SN]
                dxa_ref[rows, D + g * SN:D + (g + 1) * SN] = db
                dxa_ref[rows, D + 512 + g * SN:D + 512 + (g + 1) * SN] = dc
            dla = _dot01(mt01, dcum, ways=2) + dtot
            ddt = ddt + dla * arow
            da_ref[0:1, :] += _colsum(dla * dts_c)
            if last:
                ddt = ddt + ddp_ref[rows, :]
            ddts_ref[rows, :] = ddt

    blk = lambda s: _blk(nb - 1 - s, nb, rev)
    in_specs = [pl.BlockSpec((TB, 2048), lambda s: (blk(s), 0)), pl.BlockSpec((TB, 128), lambda s: (blk(s), 0)),
                _full((8, 128)), pl.BlockSpec((nch, npair, 128, SN), lambda s: (blk(s), 0, 0, 0)),
                pl.BlockSpec((TB, D), lambda s: (jnp.minimum(blk(s), nb - 2), 0))]
    args = [xa, dts, alog, sprev, dy]
    if last:
        in_specs += [pl.BlockSpec((TB, 2048), lambda s: (blk(s), 0)), pl.BlockSpec((TB, 128), lambda s: (blk(s), 0))]
        args += list(prev)
    call = dict(
        body=body, args=args, name=f"ssd_bwd_{d}",
        out_shape=(jax.ShapeDtypeStruct((t_total, 2048), F32), jax.ShapeDtypeStruct((t_total, 128), F32),
                   jax.ShapeDtypeStruct((8, 128), F32)),
        grid=(nb,), in_specs=in_specs,
        out_specs=(pl.BlockSpec((TB, 2048), lambda s: (blk(s), 0)), pl.BlockSpec((TB, 128), lambda s: (blk(s), 0)),
                   _full((8, 128))),
        scratch=[pltpu.VMEM((npair, 128, SN), F32), pltpu.VMEM((128, 128), F32)], sem=("arbitrary",), vmem_mb=48)
    return _run(_carry(call, comm, lambda: (pl.program_id(0) == 0, pl.program_id(0) == nb - 1)))


def _readout(o, g, yy, z, vec_ref):
    hg, ss, keep = [], [], []
    for h in range(NH):
        cs = slice(h * HF, (h + 1) * HF)
        oh = o[:, cs]
        r = lax.rsqrt(jnp.mean(oh * oh, axis=1, keepdims=True) + EPS)
        hg.append(oh * r * vec_ref[0:1, cs] * _silu(g[:, cs]))
        keep.append(r)
    u = yy * _silu(z)
    for gi in range(4):
        cs = slice(gi * 256, (gi + 1) * 256)
        ug = u[:, cs]
        r = lax.rsqrt(jnp.mean(ug * ug, axis=1, keepdims=True) + EPS)
        ss.append(ug * r * vec_ref[2:3, cs])
        keep.append(r)
    return jnp.concatenate(hg, axis=1), jnp.concatenate(ss, axis=1), keep, u


def _mix_out(o_f, o_b, p_main, y_f, y_b, xa, x, vecs, w_out):
    n = x.shape[0]

    def body(of_ref, ob_ref, g_ref, z_ref, yf_ref, yb_ref, xs_ref, x_ref, vec_ref, w_ref,
             ymix_ref, ylat_ref, h1_ref, u2_ref):
        o = of_ref[...].astype(F32) + ob_ref[...].astype(F32)
        yy = yf_ref[...].astype(F32) + yb_ref[...].astype(F32) + vec_ref[1:2, :] * xs_ref[...].astype(F32)
        hg, ss, _, _ = _readout(o, g_ref[...].astype(F32), yy, z_ref[...].astype(F32), vec_ref)
        ymix = jnp.concatenate([hg, ss], axis=1).astype(MXU_DTYPE)
        ymix_ref[...] = ymix
        ylat = _nn(ymix, w_ref[...])
        ylat_ref[...] = ylat
        h1 = x_ref[...] + vec_ref[3:4, :] * ylat
        h1_ref[...] = h1
        r = lax.rsqrt(jnp.mean(h1 * h1, axis=1, keepdims=True) + EPS)
        u2_ref[...] = ((h1 * r * vec_ref[6:7, :]) * vec_ref[4:5, :] + vec_ref[5:6, :]).astype(MXU_DTYPE)

    row = lambda j: (lambda i: (i, j))
    return _pcall(
        body, name="mix_out",
        out_shape=(jax.ShapeDtypeStruct((n, 2 * D), MXU_DTYPE), jax.ShapeDtypeStruct((n, D), F32),
                   jax.ShapeDtypeStruct((n, D), F32), jax.ShapeDtypeStruct((n, D), MXU_DTYPE)),
        grid=(n // TB,),
        in_specs=[pl.BlockSpec((TB, D), row(0)), pl.BlockSpec((TB, D), row(0)), pl.BlockSpec((TB, D), row(4)),
                  pl.BlockSpec((TB, D), row(5)), pl.BlockSpec((TB, D), row(0)), pl.BlockSpec((TB, D), row(0)),
                  pl.BlockSpec((TB, D), row(0)), pl.BlockSpec((TB, D), row(0)), _full((8, D)), _full((2 * D, D))],
        out_specs=(pl.BlockSpec((TB, 2 * D), row(0)), pl.BlockSpec((TB, D), row(0)), pl.BlockSpec((TB, D), row(0)),
                   pl.BlockSpec((TB, D), row(0))),
        sem=("parallel",), vmem_mb=48,
    )(o_f, o_b, p_main, p_main, y_f, y_b, xa, x, vecs, w_out)


def _mix_bwd(dylat, o_f, o_b, p_main, y_f, y_b, xa, vecs, w_out, comm=None):
    n = dylat.shape[0]
    t_total = p_main.shape[0]
    nlat = n // TB

    def body(*refs):
        dg_ref, dz_ref, acc_ref = refs[11], refs[13], refs[15]
        i = pl.program_id(0)

        @pl.when(i == 0)
        def _():
            acc_ref[...] = jnp.zeros_like(acc_ref)

        @pl.when(i < nlat)
        def _():
            compute(*refs)

        @pl.when(i == nlat)
        def _():
            dg_ref[...] = jnp.zeros_like(dg_ref)
            dz_ref[...] = jnp.zeros_like(dz_ref)

    def compute(dyl_ref, of_ref, ob_ref, g_ref, z_ref, yf_ref, yb_ref, xs_ref, vec_ref, w_ref,
                do_ref, dg_ref, dys_ref, dz_ref, dxs_ref, acc_ref):
        dymix = _nt(dyl_ref[...], w_ref[...])
        o = of_ref[...].astype(F32) + ob_ref[...].astype(F32)
        g = g_ref[...].astype(F32)
        z = z_ref[...].astype(F32)
        xs = xs_ref[...].astype(F32)
        yy = yf_ref[...].astype(F32) + yb_ref[...].astype(F32) + vec_ref[1:2, :] * xs
        _, _, keep, u = _readout(o, g, yy, z, vec_ref)
        do_l, dg_l = [], []
        for h in range(NH):
            cs = slice(h * HF, (h + 1) * HF)
            oh, gh, r, wv = o[:, cs], g[:, cs], keep[h], vec_ref[0:1, cs]
            dhg = dymix[:, cs]
            xh = oh * r
            dn = dhg * _silu(gh)
            dg_l.append(dhg * xh * wv * _dsilu(gh))
            acc_ref[0:1, cs] += _colsum(dn * xh)
            dxh = dn * wv
            do_l.append(r * (dxh - xh * jnp.mean(dxh * xh, axis=1, keepdims=True)))
        du_l = []
        for gi in range(4):
            cs = slice(gi * 256, (gi + 1) * 256)
            ug, r, wv = u[:, cs], keep[NH + gi], vec_ref[2:3, cs]
            dss = dymix[:, D + gi * 256:D + (gi + 1) * 256]
            xh = ug * r
            acc_ref[2:3, cs] += _colsum(dss * xh)
            dxh = dss * wv
            du_l.append(r * (dxh - xh * jnp.mean(dxh * xh, axis=1, keepdims=True)))
        du = jnp.concatenate(du_l, axis=1)
        dyy = du * _silu(z)
        do_ref[...] = jnp.concatenate(do_l, axis=1).astype(do_ref.dtype)
        dg_ref[...] = jnp.concatenate(dg_l, axis=1).astype(dg_ref.dtype)
        dys_ref[...] = dyy.astype(dys_ref.dtype)
        dz_ref[...] = (du * yy * _dsilu(z)).astype(dz_ref.dtype)
        dxs_ref[...] = (dyy * vec_ref[1:2, :]).astype(dxs_ref.dtype)
        acc_ref[1:2, :] += _colsum(dyy * xs)

    row = lambda j: (lambda i: (jnp.minimum(i, nlat - 1), j))
    lat = pl.BlockSpec((TB, D), row(0))
    tok = pl.BlockSpec((TB, D), lambda i: (i, 0))
    call = dict(
        body=body, args=[dylat, o_f, o_b, p_main, p_main, y_f, y_b, xa, vecs, w_out], name="mix_bwd",
        out_shape=(jax.ShapeDtypeStruct((n, D), MXU_DTYPE), jax.ShapeDtypeStruct((t_total, D), MXU_DTYPE),
                   jax.ShapeDtypeStruct((n, D), MXU_DTYPE), jax.ShapeDtypeStruct((t_total, D), MXU_DTYPE),
                   jax.ShapeDtypeStruct((n, D), MXU_DTYPE), jax.ShapeDtypeStruct((8, D), F32)),
        grid=(t_total // TB,),
        in_specs=[lat, lat, lat, pl.BlockSpec((TB, D), row(4)), pl.BlockSpec((TB, D), row(5)), lat, lat, lat,
                  _full((8, D)), _full((2 * D, D))],
        out_specs=(lat, tok, lat, tok, lat, _full((8, D))), scratch=[],
        sem=("arbitrary",), vmem_mb=48)
    return _run(_carry(call, comm, lambda: (pl.program_id(0) == 0, pl.program_id(0) == t_total // TB - 1)))


def _ffn_up(u2, w_gate, w_up):
    n = u2.shape[0]
    tb = 1024

    def body(u_ref, wg_ref, wu_ref, g_ref, up_ref, a_ref):
        uv = u_ref[...]
        gt = _nt(uv, wg_ref[...])
        upv = _nt(uv, wu_ref[...])
        g_ref[...] = gt.astype(g_ref.dtype)
        up_ref[...] = upv.astype(up_ref.dtype)
        a_ref[...] = (_silu(gt) * upv).astype(a_ref.dtype)

    blk = pl.BlockSpec((tb, FSL), lambda j, i: (i, j))
    wblk = pl.BlockSpec((None, FSL, D), lambda j, i: (j, 0, 0))
    return _pcall(
        body, name="ffn_up",
        out_shape=(jax.ShapeDtypeStruct((n, DFFP), MXU_DTYPE),) * 3,
        grid=(4, n // tb), in_specs=[pl.BlockSpec((tb, D), lambda j, i: (i, 0)), wblk, wblk],
        out_specs=(blk, blk, blk), sem=("parallel", "parallel"), vmem_mb=48,
    )(u2, w_gate, w_up)


def _ffn_down_loss(act, w_down, h1, tgt, vecs):
    n = act.shape[0]
    tb = 512

    def body(a_ref, w_ref, h1_ref, t_ref, vec_ref, dh2_ref, dffn_ref, acc_ref):
        i = pl.program_id(0)

        @pl.when(i == 0)
        def _():
            acc_ref[...] = jnp.zeros_like(acc_ref)

        g2 = vec_ref[0:1, :]
        fw = vec_ref[1:2, :]
        nsub = 4
        sb = tb // nsub
        wv = w_ref[...]
        ffns = [_nn(a_ref[r_ * sb:(r_ + 1) * sb, :], wv) for r_ in range(nsub)]
        for r_ in range(nsub):
            rows = slice(r_ * sb, (r_ + 1) * sb)
            ffn = ffns[r_]
            h2 = h1_ref[rows, :] + g2 * ffn
            r = lax.rsqrt(jnp.mean(h2 * h2, axis=1, keepdims=True) + EPS)
            xh = h2 * r
            err = xh * fw - t_ref[rows, :]
            dy = err * (1.0 / D)
            acc_ref[2:3, :] += _colsum(err * err) * (0.5 / D)
            acc_ref[1:2, :] += _colsum(dy * xh)
            dxh = dy * fw
            dh2 = r * (dxh - xh * jnp.mean(dxh * xh, axis=1, keepdims=True))
            dh2_ref[rows, :] = dh2
            dffn_ref[rows, :] = (g2 * dh2).astype(dffn_ref.dtype)
            acc_ref[0:1, :] += _colsum(dh2 * ffn)

    return _pcall(
        body, name="ffn_down_loss",
        out_shape=(jax.ShapeDtypeStruct((n, D), F32), jax.ShapeDtypeStruct((n, D), MXU_DTYPE),
                   jax.ShapeDtypeStruct((8, D), F32)),
        grid=(n // tb,),
        in_specs=[pl.BlockSpec((tb, DFFP), lambda i: (i, 0)), _full((DFFP, D)), pl.BlockSpec((tb, D), lambda i: (i, 0)),
                  pl.BlockSpec((tb, D), lambda i: (i, 0)), _full((8, D))],
        out_specs=(pl.BlockSpec((tb, D), lambda i: (i, 0)), pl.BlockSpec((tb, D), lambda i: (i, 0)), _full((8, D))),
        sem=("arbitrary",), vmem_mb=48,
    )(act, w_down, h1, tgt, vecs)


def _ffn_bwd(dffn, w_down, gate, up, w_gate_t, w_up_t):
    n = dffn.shape[0]
    tb = 1024

    def body(df_ref, wd_ref, g_ref, up_ref, wg_ref, wu_ref, dg_ref, dup_ref, du_ref):
        j = pl.program_id(1)
        nsub = 4
        sb = tb // nsub
        wd, wg, wu = wd_ref[...], wg_ref[...], wu_ref[...]
        dacts = [_nt(df_ref[r * sb:(r + 1) * sb, :], wd) for r in range(nsub)]
        parts = []
        for r in range(nsub):
            rows = slice(r * sb, (r + 1) * sb)
            gt = g_ref[rows, :].astype(F32)
            upv = up_ref[rows, :].astype(F32)
            sg = _sig(gt)
            dgt = (dacts[r] * upv * (sg * (1.0 + gt * (1.0 - sg)))).astype(MXU_DTYPE)
            dupv = (dacts[r] * (gt * sg)).astype(MXU_DTYPE)
            dg_ref[rows, :] = dgt
            dup_ref[rows, :] = dupv
            parts.append(_nn(dgt, wg) + _nn(dupv, wu))
        part = jnp.concatenate(parts, axis=0)

        @pl.when(j == 0)
        def _():
            du_ref[...] = part

        @pl.when(j > 0)
        def _():
            du_ref[...] += part

    tok = pl.BlockSpec((tb, D), lambda i, j: (i, 0))
    ffb = pl.BlockSpec((tb, FSL), lambda i, j: (i, j))
    wsl = pl.BlockSpec((None, FSL, D), lambda i, j: (j, 0, 0))
    return _pcall(
        body, name="ffn_bwd",
        out_shape=(jax.ShapeDtypeStruct((n, DFFP), MXU_DTYPE), jax.ShapeDtypeStruct((n, DFFP), MXU_DTYPE),
                   jax.ShapeDtypeStruct((n, D), F32)),
        grid=(n // tb, 4),
        in_specs=[tok, pl.BlockSpec((FSL, D), lambda i, j: (j, 0)), ffb, ffb, wsl, wsl],
        out_specs=(ffb, ffb, tok), sem=("parallel", "arbitrary"), vmem_mb=48,
    )(dffn, w_down, gate, up, w_gate_t, w_up_t)


def _ffn_norm_bwd(du, h1, ylat, dh2, vecs):
    n = du.shape[0]
    tb = 512

    def body(du_ref, h1_ref, yl_ref, dh2_ref, vec_ref, dh1_ref, dyl_ref, acc_ref):
        @pl.when(pl.program_id(0) == 0)
        def _():
            acc_ref[...] = jnp.zeros_like(acc_ref)

        duv = du_ref[...]
        h1 = h1_ref[...]
        r = lax.rsqrt(jnp.mean(h1 * h1, axis=1, keepdims=True) + EPS)
        xh = h1 * r
        nw = vec_ref[2:3, :]
        acc_ref[0:1, :] += _colsum(duv)
        acc_ref[1:2, :] += _colsum(duv * xh * nw)
        dn = duv * vec_ref[1:2, :]
        acc_ref[2:3, :] += _colsum(dn * xh)
        dxh = dn * nw
        dh1 = dh2_ref[...] + r * (dxh - xh * jnp.mean(dxh * xh, axis=1, keepdims=True))
        dh1_ref[...] = dh1
        dyl_ref[...] = (vec_ref[0:1, :] * dh1).astype(dyl_ref.dtype)
        acc_ref[3:4, :] += _colsum(dh1 * yl_ref[...])

    tok = pl.BlockSpec((tb, D), lambda i: (i, 0))
    return _pcall(
        body, name="ffn_norm_bwd",
        out_shape=(jax.ShapeDtypeStruct((n, D), F32), jax.ShapeDtypeStruct((n, D), MXU_DTYPE),
                   jax.ShapeDtypeStruct((8, D), F32)),
        grid=(n // tb,), in_specs=[tok, tok, tok, tok, _full((8, D))], out_specs=(tok, tok, _full((8, D))),
        sem=("arbitrary",), vmem_mb=40,
    )(du, h1, ylat, dh2, vecs)


def _deep_rows(rows):
    return max(r for r in range(128, 2305, 128) if rows % r == 0)


def _dw(a, b, name):
    tn_rows = a.shape[0]
    bt = _deep_rows(tn_rows)
    kk, nn_ = a.shape[1], b.shape[1]
    bk = 1024 if kk % 1024 == 0 else kk
    bn = 1024 if nn_ % 1024 == 0 else nn_
    nt = tn_rows // bt

    def body(a_ref, b_ref, o_ref, acc):
        t = pl.program_id(2)
        part = _tn(a_ref[...], b_ref[...])

        @pl.when(t == 0)
        def _():
            acc[...] = part

        @pl.when(t > 0)
        def _():
            acc[...] += part

        @pl.when(t == nt - 1)
        def _():
            o_ref[...] = acc[...].astype(o_ref.dtype)

    return _pcall(
        body, name=name, out_shape=jax.ShapeDtypeStruct((kk, nn_), MXU_DTYPE), grid=(kk // bk, nn_ // bn, nt),
        in_specs=[pl.BlockSpec((bt, bk), lambda i, j, t: (t, i)), pl.BlockSpec((bt, bn), lambda i, j, t: (t, j))],
        out_specs=pl.BlockSpec((bk, bn), lambda i, j, t: (i, j)), scratch=[pltpu.VMEM((bk, bn), F32)],
        sem=("parallel", "parallel", "arbitrary"), vmem_mb=40,
    )(a, b)


def _dw_in(segs, u_all, name):
    tiles = []
    for m, s_ in enumerate(segs):
        tiles += [(m, h) for h in range(s_.shape[1] // D)]
    ntile = len(tiles)
    t_total = u_all.shape[0]
    bt = _deep_rows(t_total)
    nt = t_total // bt

    def body(u_ref, *refs):
        seg_refs, o_ref, acc = refs[:len(segs)], refs[len(segs)], refs[len(segs) + 1]
        n, t = pl.program_id(0), pl.program_id(1)
        for k, (m, _) in enumerate(tiles):
            @pl.when(n == k)
            def _(m=m):
                part = _tn(seg_refs[m][...], u_ref[...])

                @pl.when(t == 0)
                def _():
                    acc[...] = part

                @pl.when(t > 0)
                def _():
                    acc[...] += part

        @pl.when(t == nt - 1)
        def _():
            o_ref[...] = acc[...].astype(o_ref.dtype)

    def seg_spec(m):
        ks = [k for k, (mm, _) in enumerate(tiles) if mm == m]
        lo, hi = ks[0], ks[-1]
        on = lambda n: (n >= lo) & (n <= hi)
        return pl.BlockSpec((bt, D), lambda n, t: (jnp.where(on(n), t, 0), jnp.where(on(n), n - lo, 0)))

    return _pcall(
        body, name=name, out_shape=jax.ShapeDtypeStruct((1, ntile * D, D), MXU_DTYPE), grid=(ntile, nt),
        in_specs=[pl.BlockSpec((bt, D), lambda n, t: (t, 0))] + [seg_spec(m) for m in range(len(segs))],
        out_specs=pl.BlockSpec((None, D, D), lambda n, t: (0, n, 0)),
        scratch=[pltpu.VMEM((D, D), F32)], sem=("parallel", "arbitrary"), vmem_mb=56,
    )(u_all, *segs)


def _du_prenorm_bwd(segs, ddt, wi_main, wi_tail, xin, mods, dres, row_off, tb, name, comm=None):
    n = xin.shape[0]
    nt = n // tb
    off = row_off // tb
    has_dx = dres is not None

    def body(*refs):
        seg_refs = refs[:7]
        ddt_ref, w_ref, wb_ref, wdt_ref, x_ref, mod_ref = refs[7:13]
        rest = refs[13:]
        if has_dx:
            dres_ref, dx_ref, acc_ref, du_scr = rest
        else:
            acc_ref, du_scr = rest
        j, i = pl.program_id(0), pl.program_id(1)
        rows = pl.ds(pl.multiple_of(i * tb, tb), tb)

        @pl.when((i == 0) & (j == 0))
        def _():
            acc_ref[...] = jnp.zeros_like(acc_ref)

        @pl.when(j == 0)
        def _():
            du_scr[rows, :] = _nn(ddt_ref[...], wdt_ref[...])

        for k in range(4):
            if not has_dx and k == 2:
                continue

            @pl.when(j == k)
            def _(k=k):
                if k < 3:
                    sa, sb = seg_refs[2 * k][...], seg_refs[2 * k + 1][...]
                else:
                    sa, sb = seg_refs[6][:, 0:D], seg_refs[6][:, D:2 * D]
                part = _nn(sa, w_ref[0:D, :]) + _nn(sb, w_ref[D:2 * D, :])
                if k > 0:
                    part = part + _nn(sa[:, 0:WTAIL], wb_ref[...])
                du_scr[rows, :] += part

        @pl.when(j == 3)
        def _():
            du = du_scr[rows, :]
            xv = x_ref[...]
            r = lax.rsqrt(jnp.mean(xv * xv, axis=1, keepdims=True) + EPS)
            xh = xv * r
            nw = mod_ref[1:2, :]
            acc_ref[0:1, :] += _colsum(du)
            acc_ref[1:2, :] += _colsum(du * xh * nw)
            dn = du * mod_ref[0:1, :]
            acc_ref[2:3, :] += _colsum(dn * xh)
            if has_dx:
                dxh = dn * nw
                dx_ref[...] = dres_ref[...] + r * (dxh - xh * jnp.mean(dxh * xh, axis=1, keepdims=True))

    def seg_spec(k):
        width = D if k < 6 else 2 * D
        return pl.BlockSpec((tb, width), lambda j, i: (jnp.where(j == min(k // 2, 3), i + off, 0), 0))

    last = pl.BlockSpec((tb, D), lambda j, i: (jnp.where(j == 3, i, 0), 0))
    in_specs = [seg_spec(k) for k in range(7)]
    in_specs += [pl.BlockSpec((tb, 128), lambda j, i: (jnp.where(j == 0, i + off, 0), 0))] + _w_specs()
    in_specs += [last, _full((8, D))]
    args = list(segs) + [ddt, wi_main, wi_tail, wi_tail, xin, mods]
    out_shape = [jax.ShapeDtypeStruct((8, D), F32)]
    out_specs = [_full((8, D))]
    if has_dx:
        in_specs.append(last)
        args.append(dres)
        out_shape.insert(0, jax.ShapeDtypeStruct((n, D), F32))
        out_specs.insert(0, last)
    call = dict(body=body, args=args, name=name, out_shape=tuple(out_shape), grid=(4, nt), in_specs=in_specs,
                out_specs=tuple(out_specs), scratch=[pltpu.VMEM((n, D), F32)], sem=("arbitrary", "arbitrary"),
                vmem_mb=58)
    steps = lambda: ((pl.program_id(0) == 0) & (pl.program_id(1) == 0),
                     (pl.program_id(0) == 3) & (pl.program_id(1) == nt - 1))
    return _run(_carry(call, comm, steps))


def _sum8(v):
    def body(v_ref, o_ref):
        acc = v_ref[0]
        for k in range(1, 8):
            acc = acc + v_ref[k]
        o_ref[...] = acc

    return _pcall(body, name="small_sum", out_shape=jax.ShapeDtypeStruct(v.shape[1:], F32),
                  in_specs=[pl.BlockSpec(memory_space=pltpu.VMEM)], out_specs=pl.BlockSpec(memory_space=pltpu.VMEM))(v)


def _adamw(w, m, v, g, name, comm=None):
    lead = w.ndim == 3
    rows, cols = w.shape[-2:]
    rb = 256 if rows % 256 == 0 else (352 if rows % 352 == 0 else rows)
    c1 = 1.0 - B1 ** STEP
    c2 = 1.0 - B2 ** STEP

    def body(w_ref, m_ref, v_ref, g_ref, d_ref, nm_ref, nv_ref):
        gv = g_ref[...]
        mn = B1 * m_ref[...] + (1.0 - B1) * gv
        vn = B2 * v_ref[...] + (1.0 - B2) * (gv * gv)
        nm_ref[...] = mn
        nv_ref[...] = vn
        d_ref[...] = -LR * ((mn / c1) / (jnp.sqrt(vn / c2) + AEPS) + WD * w_ref[...])

    if rb == rows and rows > 1024:
        cb, steps = 256, cols // 256
        gspec = pl.BlockSpec((rows, cb), lambda i: (0, i))
        spec = pl.BlockSpec((None, rows, cb), lambda i: (0, 0, i)) if lead else gspec
    else:
        steps = rows // rb
        gspec = pl.BlockSpec((rb, cols), lambda i: (i, 0))
        spec = pl.BlockSpec((None, rb, cols), lambda i: (0, i, 0)) if lead else gspec
    call = dict(body=body, args=[w, m, v, g], name=name, out_shape=(jax.ShapeDtypeStruct(w.shape, F32),) * 3,
                grid=(steps,), in_specs=[spec] * 3 + [gspec], out_specs=(spec,) * 3, scratch=[],
                sem=("arbitrary",) if comm is not None else ("parallel",), vmem_mb=40)
    return _run(_carry(call, comm, lambda: (pl.program_id(0) == 0, pl.program_id(0) == steps - 1,
                                            pl.program_id(0) == steps - 1)))


def _rows(v, n):
    f = v.reshape(-1)
    return jnp.pad(f, (0, n * D - f.shape[0])).reshape(n, D)


def kernel(x, c, ctx, c_ctx, w_ada, b_ada, norm_mix, w_in, conv_w, conv_b, ssd_a_log, ssd_dt_bias, ssd_d, ssd_norm, hgrn_lb_raw, hgrn_norm, w_out, norm_ffn, w_gate, w_up, w_down, final_norm, loss_target, m_c_ctx, m_w_ada, m_b_ada, m_norm_mix, m_w_in, m_conv_w, m_conv_b, m_ssd_a_log, m_ssd_dt_bias, m_ssd_d, m_ssd_norm, m_hgrn_lb_raw, m_hgrn_norm, m_w_out, m_norm_ffn, m_w_gate, m_w_up, m_w_down, m_final_norm, v_c_ctx, v_w_ada, v_b_ada, v_norm_mix, v_w_in, v_conv_w, v_conv_b, v_ssd_a_log, v_ssd_dt_bias, v_ssd_d, v_ssd_norm, v_hgrn_lb_raw, v_hgrn_norm, v_w_out, v_norm_ffn, v_w_gate, v_w_up, v_w_down, v_final_norm):
    ix, iy, ic = lax.axis_index("x"), lax.axis_index("y"), lax.axis_index("c")
    chip = 2 * ix + iy
    me = 2 * chip + ic
    xl, xc, tgt = x[0], ctx[0], loss_target[0]
    n_lat, n_ctx = xl.shape[0], xc.shape[0]
    assert n_ctx == TB and n_lat % 1024 == 0
    t_total = n_lat + n_ctx
    nb = t_total // TB

    tr = lambda a: jnp.swapaxes(a, -1, -2)
    shift = [functools.partial(jnp.pad, pad_width=((8 * k, WSL + WTAIL - NSH - 8 * k), (0, 0))) for k in range(4)]
    slab = lax.switch(chip, shift, tr(w_in[0]).astype(MXU_DTYPE))
    padrows = lambda a: jnp.pad(a, ((0, FSL - DFF // 4), (0, 0))).astype(MXU_DTYPE)
    shards = [slab[:WSL], slab[WSL:], w_out[0].astype(MXU_DTYPE), padrows(tr(w_gate[0])), padrows(tr(w_up[0])),
              padrows(w_down[0])]
    own = lambda g_, s_: lax.dynamic_update_slice(g_, s_[None], (chip, 0, 0))
    pack = jnp.concatenate([c, hgrn_lb_raw.reshape(1, D), _rows(conv_w[0], 3), jnp.zeros((3, D), F32)], axis=0)
    ncol_ada = w_ada.shape[2]
    b_shard = lax.dynamic_slice(b_ada, (0, chip * ncol_ada), (1, ncol_ada))
    gath, araw, mod_all, wi_main, wi_tail = _prologue(pack, c_ctx.reshape(1, D), w_ada[0], b_shard, shards[:2])
    wi_main, wi_tail = own(wi_main, shards[0]), own(wi_tail, shards[1])
    gath = gath.reshape(8, 8, D)
    lbraw_full = gath[0::2, 1].reshape(4, 2, 2, 256).transpose(1, 2, 0, 3).reshape(4, D)
    convw_full = gath[0::2, 2:5].reshape(4, 3 * D)[:, :KCONV * 512].reshape(4, KCONV, 512).transpose(1, 0, 2)
    convw_full = convw_full.reshape(KCONV, 2048)
    lbraw8 = jnp.pad(lbraw_full, ((0, 4), (0, 0)))
    convp = jnp.concatenate([convw_full, conv_b, jnp.zeros((2, 2048), F32)], axis=0)
    dtb = jnp.pad(ssd_dt_bias.reshape(1, 32), ((0, 7), (0, 96)))
    alog = jnp.pad(ssd_a_log.reshape(1, 32), ((0, 7), (0, 96)))
    mod_all = mod_all.reshape(8, 16, ncol_ada)[0::2]
    mod_full = mod_all.transpose(1, 0, 2).reshape(16, 4 * ncol_ada)
    my_mod = lax.dynamic_slice(mod_full, (me, 0), (1, 6 * D)).reshape(6, D)
    sh1, sc1, g1, sh2, sc2, g2 = (my_mod[k:k + 1] for k in range(6))
    csh1, csc1 = mod_full[8:9, 0:D], mod_full[8:9, D:2 * D]

    zrow = jnp.zeros((1, D), F32)
    mods_lat = jnp.concatenate([1.0 + sc1, sh1, norm_mix, zrow, zrow, zrow, zrow, zrow], axis=0)
    mods_ctx = jnp.concatenate([1.0 + csc1, csh1, norm_mix, zrow, zrow, zrow, zrow, zrow], axis=0)
    outs = _inproj(xl, mods_lat, wi_main, wi_tail, t_total, 1024, 0, None, "inproj_lat",
                   comm=_comm_gather(shards[2:4]))
    wo_g, wg_g = (own(g_, s_) for g_, s_ in zip(outs[3:], shards[2:4]))
    w_out_f = wo_g.reshape(2 * D, D)
    p_main, p_dt, u_all = _inproj(xc, mods_ctx, wi_main, wi_tail, t_total, TB, nb - 1, outs[:3], "inproj_ctx")

    o_f, hs_f, wd_g = _hgrn_fwd(p_main, lbraw8, 0, nb, comm=_comm_gather(shards[5:]))
    w_down_f = own(wd_g, shards[5]).reshape(DFFP, D)
    o_b, hs_b, wu_g = _hgrn_fwd(p_main, lbraw8, 1, nb, comm=_comm_gather(shards[4:5]))
    wu_g = own(wu_g, shards[4])
    xa, dsl, dts = _ssd_prep(p_main, p_dt, convp, dtb, nb)
    y_f, ss_f = _ssd_fwd(xa, dts, alog, 0, nb)
    y_b, ss_b = _ssd_fwd(xa, dts, alog, 1, nb)

    vec_mix = jnp.concatenate([jnp.tile(hgrn_norm, (1, NH)), jnp.repeat(ssd_d, SP, axis=1), ssd_norm, g1, 1.0 + sc2,
                               sh2, norm_ffn, zrow], axis=0)
    ymix, ylat, h1, u2 = _mix_out(o_f, o_b, p_main, y_f, y_b, xa, xl, vec_mix, w_out_f)
    gate, up, act = _ffn_up(u2, wg_g, wu_g)
    vec_loss = jnp.concatenate([g2, final_norm.reshape(1, D)] + [zrow] * 6, axis=0)
    dh2, dffn, acc_loss = _ffn_down_loss(act, w_down_f, h1, tgt, vec_loss)

    core_arr = jnp.reshape(ic, (1,)).astype(jnp.int32)
    chip_arr = jnp.reshape(chip, (1,)).astype(jnp.int32)
    every = (0, 4)

    def pair_sum(gs, got, tag):
        return list(_pair_sum(gs, list(got), core_arr, "grads_pair_sum_" + tag))

    vec_ffn = jnp.concatenate([g1, 1.0 + sc2, norm_ffn] + [zrow] * 5, axis=0)
    dgate, dup, du2 = _ffn_bwd(dffn, w_down_f, gate, up, wg_g, wu_g)
    dh1, dylat, acc_ffn = _ffn_norm_bwd(du2, h1, ylat, dh2, vec_ffn)
    gw_down = _dw(act, dffn, "dw_down").reshape(4, FSL, D)
    ga1 = [_dw(dgate, u2, "dw_gate").reshape(4, FSL, D), _dw(dup, u2, "dw_up").reshape(4, FSL, D)]
    res = _mix_bwd(dylat, o_f, o_b, p_main, y_f, y_b, xa, vec_mix, w_out_f, comm=_comm_pair(ga1))
    (do, dgr, dys, dzr, dxs_skip, acc_mix), pair_a1 = res[:6], pair_sum(ga1, res[6:], "a1")
    ga2 = [gw_down, _dw(ymix, dylat, "dw_out").reshape(4, D // 2, D)]

    res = _hgrn_bwd(p_main, lbraw8, hs_f, do, 0, nb, None,
                    comm=[_comm_exchange(pair_a1, [every] * 2), _comm_pair(ga2)])
    (dq0, dff, dv0, dlb_f), recv_a, pair_a2 = res[:4], list(res[4:6]), pair_sum(ga2, res[6:], "a2")
    res = _hgrn_bwd(p_main, lbraw8, hs_b, do, 1, nb, (dq0, dv0), comm=_comm_exchange(pair_a2, [every] * 2))
    (dq, dfb, dv, dlb_b), recv_a = res[:4], recv_a + list(res[4:])
    pair_a, dests_a = pair_a1 + pair_a2, [every] * 4
    gw_in = [_dw_in([dq, dff], u_all, "dw_in_0"), _dw_in([dfb, dv], u_all, "dw_in_1"),
             _dw_in([dgr, dzr], u_all, "dw_in_2")]

    res = _ssd_bwd(xa, dts, alog, ss_f, dys, 0, nb, None, comm=_comm_pair(gw_in))
    (dxa0, ddts0, da_f), pair_b, dests_b = res[:3], pair_sum(gw_in, res[3:], "b"), [(0, 1), (1, 2), (2, 3)]
    res = _ssd_bwd(xa, dts, alog, ss_b, dys, 1, nb, (dxa0, ddts0), comm=_comm_exchange(pair_b, dests_b))
    (dxa, ddts, da_b), recv_b = res[:3], list(res[3:])
    dxbc, ddt, acc_conv, acc_dtb = _ssd_prep_bwd(p_main, p_dt, convp, dtb, dsl, dxa, dxs_skip, ddts, nb)
    gw_in.append(_dw_in([dxbc], u_all, "dw_in_3"))
    gw_in_dt = _dw(ddt, u_all, "dw_in_dt")
    gc = [gw_in[3], jnp.concatenate([g_[:, 0:WTAIL, :] for g_ in gw_in[1:]] + [gw_in_dt[None]], axis=0)]

    segs = [dq, dff, dfb, dv, dgr, dzr, dxbc]
    bmods_lat = jnp.concatenate([1.0 + sc1, norm_mix] + [zrow] * 6, axis=0)
    bmods_ctx = jnp.concatenate([1.0 + csc1, norm_mix] + [zrow] * 6, axis=0)
    res = _du_prenorm_bwd(segs, ddt, wi_main, wi_tail, xc, bmods_ctx, None, n_lat, TB, "du_ctx", comm=_comm_pair(gc))
    acc_ctx, pair_c, dests_c = res[0], pair_sum(gc, res[1:], "c"), [(3, 4), every]
    res = _du_prenorm_bwd(segs, ddt, wi_main, wi_tail, xl, bmods_lat, dh1, 0, 512, "du_lat",
                          comm=_comm_exchange(pair_c, dests_c))
    (grad_x, acc_lat), recv_c = res[:2], list(res[2:])

    mine = _chip_sum(pair_b + pair_c + pair_a, recv_b + recv_c + recv_a, chip_arr, dests_b + dests_c + dests_a,
                     [0, 0, 0, 0, 1, 3, 4, 5, 2])
    dmod_lat = jnp.concatenate([acc_lat[0:2], acc_ffn[3:4], acc_ffn[0:2], acc_loss[0:1]], axis=0)
    misc = jnp.concatenate([(da_f + da_b)[0, :32], jnp.zeros((96,), F32), acc_dtb[0, :32], jnp.zeros((96,), F32),
                            jnp.sum(acc_loss[2]).reshape(1), jnp.zeros((D - 257,), F32)]).reshape(1, D)
    sv = jnp.concatenate([
        dmod_lat, acc_ctx[0:2], (acc_lat[2:3] + acc_ctx[2:3]), acc_ffn[2:3], acc_loss[1:2], acc_mix[2:3],
        acc_mix[0:1], acc_mix[1:2], dlb_f[0:1], dlb_b[0:1], acc_conv[0:6].reshape(12, D), misc,
        jnp.zeros((3, D), F32)], axis=0)
    res = _pair_swap(mine, sv)
    theirs, sv_all = res[:-1], res[-1].reshape(8, 32, D)
    whole = [jnp.concatenate([jnp.where(ic == 0, m_, t_), jnp.where(ic == 0, t_, m_)], axis=0)
             for m_, t_ in zip(mine, theirs)]
    g_w_in = lax.dynamic_slice(jnp.concatenate(whole[0:2], axis=0), (8 * chip, 0), (NSH, D))
    g_w_out = whole[2]
    g_w_gate = whole[3][:DFF // 4]
    g_w_up = whole[4][:DFF // 4]
    g_w_down = whole[5][:DFF // 4]
    ssum = _sum8(sv_all)
    dmod_rows = sv_all[:, 0:6].reshape(8, 6 * D)
    dmod_ctx_row = jnp.concatenate([ssum[6:8].reshape(1, 2 * D), jnp.zeros((1, 4 * D), F32)], axis=1)
    dmod_full = jnp.concatenate([dmod_rows, dmod_ctx_row, jnp.zeros((7, 6 * D), F32)], axis=0)
    grad_b_ada = jnp.sum(dmod_full, axis=0, keepdims=True)
    dmod_shard = lax.dynamic_slice(dmod_full, (0, chip * ncol_ada), (16, ncol_ada))
    g_w_ada, da_part = _ada_bwd(araw, dmod_shard, w_ada[0])
    da_all = _allgather8(da_part, "ada_ctx_gather").reshape(8, 16, D)[0::2, 8]

    big = {}
    for nm, w_, m_, v_, g_ in (("w_ada", w_ada, m_w_ada, v_w_ada, g_w_ada), ("w_in", w_in, m_w_in, v_w_in, g_w_in),
                               ("w_out", w_out, m_w_out, v_w_out, g_w_out),
                               ("w_gate", w_gate, m_w_gate, v_w_gate, g_w_gate),
                               ("w_up", w_up, m_w_up, v_w_up, g_w_up),
                               ("w_down", w_down, m_w_down, v_w_down, g_w_down)):
        if nm in ("w_in", "w_gate", "w_up"):
            big[nm] = tuple(tr(t) for t in (g_[None],) + tuple(_adamw(tr(w_), tr(m_), tr(v_), g_, "adamw_" + nm)))
        else:
            big[nm] = (g_[None],) + tuple(_adamw(w_, m_, v_, g_, "adamw_" + nm))
    cc = c_ctx.reshape(1, D)
    grad_c_ctx = (jnp.sum(da_all, axis=0, keepdims=True) * _dsilu(cc)).reshape(D)

    grad_norm_mix, grad_norm_ffn, grad_final_norm = ssum[8:9], ssum[9:10], ssum[10].reshape(D)
    grad_ssd_norm = ssum[11:12]
    grad_hgrn_norm = jnp.sum(ssum[12].reshape(NH, HF), axis=0, keepdims=True)
    grad_ssd_d = jnp.sum(ssum[13].reshape(SHEADS, SP), axis=1).reshape(1, SHEADS)
    lb_full = _sig(lbraw_full[0:2] - lbraw_full[2:4])
    dr0 = ssum[14:16] * lb_full * (1.0 - lb_full)
    grad_lb_full = jnp.stack([dr0, -dr0], axis=0)
    grad_lb = lax.dynamic_slice(grad_lb_full, (0, 0, chip * 256), (2, 2, 256))
    grad_conv_w = lax.dynamic_slice(ssum[16:26].reshape(KCONV, 2048), (0, chip * 512), (KCONV, 512)).reshape(1, KCONV, 512)
    grad_conv_b = ssum[26:28].reshape(1, 2048)
    a_val = -jnp.exp(ssd_a_log)
    grad_a_log = ssum[28, 0:32].reshape(1, 2, SHEADS) * a_val
    grad_dt_bias = ssum[28, 128:160].reshape(1, 2, SHEADS)
    loss = ssum[28, 256]

    small_w = [c_ctx, b_ada, norm_mix, conv_w, conv_b, ssd_a_log, ssd_dt_bias, ssd_d, ssd_norm, hgrn_lb_raw,
               hgrn_norm, norm_ffn, final_norm]
    small_m = [m_c_ctx, m_b_ada, m_norm_mix, m_conv_w, m_conv_b, m_ssd_a_log, m_ssd_dt_bias, m_ssd_d, m_ssd_norm,
               m_hgrn_lb_raw, m_hgrn_norm, m_norm_ffn, m_final_norm]
    small_v = [v_c_ctx, v_b_ada, v_norm_mix, v_conv_w, v_conv_b, v_ssd_a_log, v_ssd_dt_bias, v_ssd_d, v_ssd_norm,
               v_hgrn_lb_raw, v_hgrn_norm, v_norm_ffn, v_final_norm]
    small_g = [grad_c_ctx, grad_b_ada, grad_norm_mix, grad_conv_w, grad_conv_b, grad_a_log, grad_dt_bias, grad_ssd_d,
               grad_ssd_norm, grad_lb, grad_hgrn_norm, grad_norm_ffn, grad_final_norm]
    nrows = [-(-a.size // D) for a in small_w]
    packs = lambda lst: jnp.concatenate([_rows(a, r) for a, r in zip(lst, nrows)]
                                        + [jnp.zeros((24 - sum(nrows), D), F32)], axis=0)
    sd, sm, svv = _adamw(packs(small_w), packs(small_m), packs(small_v), packs(small_g), "adamw_small")

    def unpack(p):
        out, r0 = [], 0
        for a, r in zip(small_w, nrows):
            out.append(p[r0:r0 + r].reshape(-1)[:a.size].reshape(a.shape))
            r0 += r
        return out

    sd, sm, svv = unpack(sd), unpack(sm), unpack(svv)

    order = ["c_ctx", "w_ada", "b_ada", "norm_mix", "w_in", "conv_w", "conv_b", "ssd_a_log", "ssd_dt_bias", "ssd_d",
             "ssd_norm", "hgrn_lb_raw", "hgrn_norm", "w_out", "norm_ffn", "w_gate", "w_up", "w_down", "final_norm"]
    small_names = ["c_ctx", "b_ada", "norm_mix", "conv_w", "conv_b", "ssd_a_log", "ssd_dt_bias", "ssd_d", "ssd_norm",
                   "hgrn_lb_raw", "hgrn_norm", "norm_ffn", "final_norm"]
    table = dict(big)
    for k, nm in enumerate(small_names):
        table[nm] = (small_g[k].reshape(small_w[k].shape), sd[k], sm[k], svv[k])
    grads = [table[nm][0] for nm in order]
    deltas = [table[nm][1] for nm in order]
    new_m = [table[nm][2] for nm in order]
    new_v = [table[nm][3] for nm in order]
    return (loss, grad_x[None], *grads, *deltas, *new_m, *new_v)
```

```python
import functools
import math

import jax
import jax.numpy as jnp
from jax import lax
from jax.experimental import pallas as pl
from jax.experimental.pallas import tpu as pltpu

F32 = jnp.float32
BF16 = jnp.bfloat16
MXU_DTYPE = jnp.bfloat16
_INTERPRET = False

D = 1024
NH, HF = 8, 128
HC = 64
SC = 128
SN = 128
SHEADS, SP = 16, 64
GRID_W = 64
KCONV = 5
DFF = 2816
FSL = 768
DFFP = 4 * FSL
NIN = 8224
TB = 256
EPS = 1e-6
LR, B1, B2, AEPS, WD, STEP = 0.001, 0.9, 0.999, 1e-08, 0.01, 10
MESH_ID = pl.DeviceIdType.MESH
NSH = NIN // 4
WSL = 2048
WTAIL = 128


def _pcall(body, *, name, out_shape, grid=(), in_specs=None, out_specs=None, scratch=(), sem=None,
           vmem_mb=None, aliases=None):
    params = {}
    if sem is not None:
        params["dimension_semantics"] = sem
    if vmem_mb is not None:
        params["vmem_limit_bytes"] = vmem_mb << 20
    kw = dict(name=name, out_shape=out_shape, scratch_shapes=list(scratch),
              input_output_aliases=aliases or {}, compiler_params=pltpu.CompilerParams(**params),
              interpret=_INTERPRET)
    if grid:
        kw["grid"] = grid
    if in_specs is not None:
        kw["in_specs"] = in_specs
    if out_specs is not None:
        kw["out_specs"] = out_specs
    return pl.pallas_call(body, **kw)


def _mx(a):
    return a.astype(MXU_DTYPE)


def _dg(a, b, ca, cb):
    return lax.dot_general(_mx(a), _mx(b), (((ca,), (cb,)), ((), ())), preferred_element_type=F32)


def _nn(a, b):
    return _dg(a, b, 1, 0)


def _nt(a, b):
    return _dg(a, b, 1, 1)


def _tn(a, b):
    return _dg(a, b, 0, 0)


def _dot01(m, x, ways=3):
    f = lambda t: lax.dot_general(m, t, (((1,), (0,)), ((), ())), preferred_element_type=F32)
    hi = x.astype(BF16)
    r1 = x - hi.astype(F32)
    mid = r1.astype(BF16)
    if ways == 2:
        return f(hi) + f(mid)
    lo = (r1 - mid.astype(F32)).astype(BF16)
    return f(hi) + f(mid) + f(lo)


def _tri(n, upper):
    r = lax.broadcasted_iota(jnp.int32, (n, n), 0)
    c = lax.broadcasted_iota(jnp.int32, (n, n), 1)
    return (c >= r) if upper else (c <= r)


def _b01(mask):
    return jnp.where(mask, 1.0, 0.0).astype(BF16)


def _sig(x):
    return jax.nn.sigmoid(x)


def _silu(x):
    return x * _sig(x)


def _dsilu(x):
    s = _sig(x)
    return s * (1.0 + x * (1.0 - s))


def _softplus(x):
    return jnp.maximum(x, 0.0) + jnp.log(1.0 + jnp.exp(-jnp.abs(x)))


def _rowsum(x):
    return jnp.sum(x, axis=1, keepdims=True)


def _colsum(x):
    return jnp.sum(x, axis=0, keepdims=True)


def _full(shape):
    return pl.BlockSpec(shape, lambda *_: (0,) * len(shape))


def _allgather8_phases(x_ref, out_ref, send_sems, recv_sems, local_sem):
    m_per = x_ref.shape[0]
    x, y, c = lax.axis_index("x"), lax.axis_index("y"), lax.axis_index("c")
    me, sibling = (x, y, c), (x, y, 1 - c)
    chips = [(1 - x, y), (x, 1 - y), (1 - x, 1 - y)]

    def rows(px, py, pc):
        return out_ref.at[pl.ds((4 * px + 2 * py + pc) * m_per, m_per), :]

    def copy(k, block, to, src=None):
        return pltpu.make_async_remote_copy(
            src_ref=rows(*block) if src is None else src, dst_ref=rows(*block),
            send_sem=send_sems.at[k], recv_sem=recv_sems.at[k], device_id=to, device_id_type=MESH_ID)

    mine = pltpu.make_async_copy(x_ref, rows(*me), local_sem)
    first = [copy(0, me, sibling, src=x_ref)]
    first += [copy(1 + j, me, (*chip, c), src=x_ref) for j, chip in enumerate(chips)]
    passed = [copy(4 + j, (*chip, c), sibling) for j, chip in enumerate(chips)]

    def start():
        mine.start()
        for cp in first:
            cp.start()

    def forward():
        for j, chip in enumerate(chips):
            copy(1 + j, (*chip, c), me).wait_recv()
            passed[j].start()

    def finish():
        copy(0, sibling, me).wait_recv()
        for j, chip in enumerate(chips):
            copy(4 + j, (*chip, 1 - c), me).wait_recv()
        for cp in first + passed:
            cp.wait_send()
        mine.wait()

    return start, forward, finish


def _allgather8_ops(x_ref, out_ref, send_sems, recv_sems, local_sem):
    for phase in _allgather8_phases(x_ref, out_ref, send_sems, recv_sems, local_sem):
        phase()


def _allgather8(v, name):
    m_per, n = v.shape
    return _pcall(
        functools.partial(_allgather8_ops), name=name, out_shape=jax.ShapeDtypeStruct((8 * m_per, n), v.dtype),
        in_specs=[pl.BlockSpec(memory_space=pltpu.VMEM)], out_specs=pl.BlockSpec(memory_space=pltpu.VMEM),
        scratch=list(_AG8_SEMS),
    )(v)


_AG8_SEMS = [pltpu.SemaphoreType.DMA((7,)), pltpu.SemaphoreType.DMA((7,)), pltpu.SemaphoreType.DMA]


def _prologue(pack, cc_row, w_ada, b_shard, shards):
    n = len(shards)
    ncol = w_ada.shape[1]

    def body(pack_ref, cc_ref, w_ref, b_ref, *refs):
        ins = refs[:n]
        gath_ref, araw_ref, mod_ref = refs[n:n + 3]
        outs = refs[n + 3:2 * n + 3]
        modsh, s1, r1, l1, s2, r2, l2, gs, gr = refs[2 * n + 3:]
        start, forward, finish = _gather_ops(ins, outs, gs, gr, relay=True)
        start()
        _allgather8_ops(pack_ref, gath_ref, s1, r1, l1)
        a = jnp.concatenate([gath_ref[8 * i:8 * i + 1, :] for i in range(8)] + [cc_ref[...], jnp.zeros((7, D), F32)],
                            axis=0)
        araw_ref[...] = a
        modsh[...] = _nn(_silu(a), w_ref[...]) + b_ref[...]
        _allgather8_ops(modsh, mod_ref, s2, r2, l2)
        forward()
        finish()

    vm = pl.BlockSpec(memory_space=pltpu.VMEM)
    anyspec = pl.BlockSpec(memory_space=pl.ANY)
    return _pcall(
        body, name="prologue",
        out_shape=(jax.ShapeDtypeStruct((64, D), F32), jax.ShapeDtypeStruct((16, D), F32),
                   jax.ShapeDtypeStruct((128, ncol), F32)) + _gather_out(shards),
        in_specs=[vm, vm, vm, vm] + [anyspec] * n, out_specs=(vm, vm, vm) + (anyspec,) * n,
        scratch=[pltpu.VMEM((16, ncol), F32)] + list(_AG8_SEMS) + list(_AG8_SEMS) + _gather_sems(n), vmem_mb=40,
    )(pack, cc_row, w_ada, b_shard, *shards)


def _gather_ops(ins, outs, send_sems, recv_sems, relay=False):
    n = len(ins)
    x, y, c = lax.axis_index("x"), lax.axis_index("y"), lax.axis_index("c")
    me, sibling = (x, y, c), (x, y, 1 - c)
    chips = [(1 - x, y), (x, 1 - y), (1 - x, 1 - y)]
    direct = 2 if relay else 3

    def part(a, px, py, pc, quarter=None):
        half = ins[a].shape[0] // 2
        if quarter is None:
            return outs[a].at[2 * px + py, pl.ds(pc * half, half), :]
        return outs[a].at[2 * px + py, pl.ds(pc * half + quarter * (half // 2), half // 2), :]

    def copy(a, k, block, to, src=None, quarter=None):
        return pltpu.make_async_remote_copy(
            src_ref=part(a, *block, quarter) if src is None else src, dst_ref=part(a, *block, quarter),
            send_sem=send_sems.at[8 * a + k], recv_sem=recv_sems.at[8 * a + k], device_id=to,
            device_id_type=MESH_ID)

    def first(a, j):
        half = ins[a].shape[0] // 2
        return copy(a, j, me, (*chips[j], c), src=ins[a].at[pl.ds(c * half, half), :])

    relayed = lambda a, q: copy(a, 6 + q, (*chips[q], c), (*chips[1 - q], c), quarter=q)

    def start():
        for a in range(n):
            for j in range(direct):
                first(a, j).start()

    def forward():
        for a in range(n):
            for j in range(direct):
                copy(a, j, (*chips[j], c), me).wait_recv()
                copy(a, 3 + j, (*chips[j], c), sibling).start()
                if relay:
                    relayed(a, j).start()
            if relay:
                for q in range(2):
                    copy(a, 6 + q, (*chips[2], c), me, quarter=q).wait_recv()
                copy(a, 5, (*chips[2], c), sibling).start()

    def finish():
        for a in range(n):
            for j, chip in enumerate(chips):
                copy(a, 3 + j, (*chip, 1 - c), me).wait_recv()
        for a in range(n):
            for j, chip in enumerate(chips):
                if j < direct:
                    first(a, j).wait_send()
                    if relay:
                        relayed(a, j).wait_send()
                copy(a, 3 + j, (*chip, c), sibling).wait_send()

    return start, forward, finish


def _gather_out(shards):
    return tuple(jax.ShapeDtypeStruct((4,) + s_.shape, s_.dtype) for s_ in shards)


def _gather_sems(n):
    return [pltpu.SemaphoreType.DMA((8 * n,)), pltpu.SemaphoreType.DMA((8 * n,))]


def _pair_ops(ins, outs, send_sems, recv_sems):
    x, y, c = lax.axis_index("x"), lax.axis_index("y"), lax.axis_index("c")
    cps = []
    for a in range(len(ins)):
        half = ins[a].shape[1] // 2
        cps.append(pltpu.make_async_remote_copy(
            src_ref=ins[a].at[:, pl.ds((1 - c) * half, half), :], dst_ref=outs[a], send_sem=send_sems.at[a],
            recv_sem=recv_sems.at[a], device_id=(x, y, 1 - c), device_id_type=MESH_ID))

    def start():
        for cp in cps:
            cp.start()

    def finish():
        for cp in cps:
            cp.wait()

    return start, finish


def _comm_pair(gs):
    n = len(gs)
    return (list(gs), tuple(jax.ShapeDtypeStruct((g.shape[0], g.shape[1] // 2, g.shape[2]), g.dtype) for g in gs),
            [pltpu.SemaphoreType.DMA((n,)), pltpu.SemaphoreType.DMA((n,))], _pair_ops)


def _exchange_ops(ins, outs, send_sems, recv_sems, dests):
    x, y, c = lax.axis_index("x"), lax.axis_index("y"), lax.axis_index("c")
    mine = 2 * x + y
    chips = [(1 - x, y), (x, 1 - y), (1 - x, 1 - y)]

    def each(fn):
        for a in range(len(ins)):
            lo, hi = dests[a]
            for j, (px, py) in enumerate(chips):
                q = 2 * px + py
                cp = pltpu.make_async_remote_copy(
                    src_ref=ins[a].at[jnp.clip(q - lo, 0, hi - lo - 1)], dst_ref=outs[a].at[j],
                    send_sem=send_sems.at[3 * a + j], recv_sem=recv_sems.at[3 * a + j], device_id=(px, py, c),
                    device_id_type=MESH_ID)
                fn(cp, (q >= lo) & (q < hi), (mine >= lo) & (mine < hi), (lo, hi) == (0, 4))

    def start():
        def go(cp, send_ok, recv_ok, always):
            if always:
                cp.start()
            else:
                pl.when(send_ok)(cp.start)
        each(go)

    def finish():
        def go(cp, send_ok, recv_ok, always):
            if always:
                cp.wait()
            else:
                pl.when(send_ok)(cp.wait_send)
                pl.when(recv_ok)(cp.wait_recv)
        each(go)

    return start, finish


def _comm_exchange(hs, dests):
    n = len(hs)
    return (list(hs), tuple(jax.ShapeDtypeStruct((3,) + h.shape[1:], h.dtype) for h in hs),
            [pltpu.SemaphoreType.DMA((3 * n,)), pltpu.SemaphoreType.DMA((3 * n,))],
            lambda i, o, s, r: _exchange_ops(i, o, s, r, dests))


def _comm_gather(shards, relay=False):
    return (list(shards), _gather_out(shards), _gather_sems(len(shards)),
            lambda i, o, s, r: _gather_ops(i, o, s, r, relay))


def _carry(call, comm, steps):
    if comm is None:
        return call
    if isinstance(comm, list):
        for one in comm:
            call = _carry(call, one, steps)
        return call
    arrays, out_shape, sems, make = comm
    n, n_in, n_out = len(arrays), len(call["args"]), len(call["out_shape"])
    body = call["body"]

    def wrapped(*refs):
        base_in, cin = refs[:n_in], refs[n_in:n_in + n]
        rest = refs[n_in + n:]
        base_out, cout, scr = rest[:n_out], rest[n_out:n_out + n], rest[n_out + n:]
        ops = make(cin, cout, scr[-2], scr[-1])
        when = steps()
        pl.when(when[0])(ops[0])
        if len(ops) == 3 and len(when) == 3:
            pl.when(when[2])(ops[1])
        body(*base_in, *base_out, *scr[:-2])
        if len(ops) == 3 and len(when) == 2:
            pl.when(when[1])(ops[1])
        pl.when(when[1])(ops[-1])

    anyspec = pl.BlockSpec(memory_space=pl.ANY)
    return dict(call, body=wrapped, args=list(call["args"]) + arrays,
                in_specs=list(call["in_specs"]) + [anyspec] * n,
                out_shape=tuple(call["out_shape"]) + tuple(out_shape),
                out_specs=tuple(call["out_specs"]) + (anyspec,) * n,
                scratch=list(call["scratch"]) + sems)


def _run(call):
    args = call.pop("args")
    body = call.pop("body")
    return _pcall(body, **call)(*args)


def _pair_swap(rs, sv):
    n = len(rs)

    def body(sv_ref, *refs):
        ins, outs, got_ref = refs[:n], refs[n:2 * n], refs[2 * n]
        send_sems, recv_sems, s1, r1, l1 = refs[2 * n + 1:]
        x, y, c = lax.axis_index("x"), lax.axis_index("y"), lax.axis_index("c")
        cps = [pltpu.make_async_remote_copy(
            src_ref=ins[a], dst_ref=outs[a], send_sem=send_sems.at[a], recv_sem=recv_sems.at[a],
            device_id=(x, y, 1 - c), device_id_type=MESH_ID) for a in range(n)]
        for cp in cps:
            cp.start()
        _allgather8_ops(sv_ref, got_ref, s1, r1, l1)
        for cp in cps:
            cp.wait()

    vm, anyspec = pl.BlockSpec(memory_space=pltpu.VMEM), pl.BlockSpec(memory_space=pl.ANY)
    return _pcall(
        body, name="grads_pair_swap",
        out_shape=tuple(jax.ShapeDtypeStruct(r.shape, r.dtype) for r in rs)
        + (jax.ShapeDtypeStruct((8 * sv.shape[0], sv.shape[1]), sv.dtype),),
        in_specs=[vm] + [anyspec] * n, out_specs=(anyspec,) * n + (vm,),
        scratch=[pltpu.SemaphoreType.DMA((n,)), pltpu.SemaphoreType.DMA((n,))] + list(_AG8_SEMS),
    )(sv, *rs)


SUM_STEPS = 4


def _pair_sum(gs, recvs, core, name):
    n = len(gs)

    def body(c_ref, *refs):
        for a in range(n):
            refs[2 * n + a][...] = (refs[a][...].astype(F32) + refs[n + a][...].astype(F32)).astype(refs[2 * n + a].dtype)

    blk = lambda g: (g.shape[0], g.shape[1] // (2 * SUM_STEPS), g.shape[2])
    return pl.pallas_call(
        body, name=name,
        out_shape=tuple(jax.ShapeDtypeStruct((g.shape[0], g.shape[1] // 2, g.shape[2]), g.dtype) for g in gs),
        grid_spec=pltpu.PrefetchScalarGridSpec(
            num_scalar_prefetch=1, grid=(SUM_STEPS,),
            in_specs=[pl.BlockSpec(blk(g), lambda i, cr: (0, cr[0] * SUM_STEPS + i, 0)) for g in gs]
            + [pl.BlockSpec(blk(g), lambda i, cr: (0, i, 0)) for g in gs],
            out_specs=tuple(pl.BlockSpec(blk(g), lambda i, cr: (0, i, 0)) for g in gs)),
        compiler_params=pltpu.CompilerParams(vmem_limit_bytes=40 << 20), interpret=_INTERPRET,
    )(core, *gs, *recvs)


def _chip_sum(hs, recvs, chip, dests, slots):
    n = len(hs)
    nout = max(slots) + 1
    first = [slots.index(o) for o in range(nout)]
    every = lambda d_: d_ == (0, 4)

    def own(d_):
        if every(d_):
            return lambda i, kr: (kr[0], i, 0)
        return lambda i, kr: (0, jnp.where(kr[0] == d_[0], i, 0), 0)

    def got(d_):
        if every(d_):
            return lambda i, kr: (0, i, 0)
        return lambda i, kr: (0, jnp.where(kr[0] == d_[0], i, 0), 0)

    def body(k_ref, *refs):
        for a in range(n):
            def emit(a=a):
                acc = refs[a][0].astype(F32)
                for j in range(3):
                    acc = acc + refs[n + a][j].astype(F32)
                refs[2 * n + slots[a]][...] = acc
            if every(dests[a]):
                emit()
            else:
                pl.when(k_ref[0] == dests[a][0])(emit)

    rb = lambda h: h.shape[1] // SUM_STEPS
    return pl.pallas_call(
        body, name="grads_chip_sum",
        out_shape=tuple(jax.ShapeDtypeStruct(hs[a].shape[1:], F32) for a in first),
        grid_spec=pltpu.PrefetchScalarGridSpec(
            num_scalar_prefetch=1, grid=(SUM_STEPS,),
            in_specs=[pl.BlockSpec((1, rb(h), h.shape[2]), own(d_)) for h, d_ in zip(hs, dests)]
            + [pl.BlockSpec((3, rb(h), h.shape[2]), got(d_)) for h, d_ in zip(hs, dests)],
            out_specs=tuple(pl.BlockSpec((rb(hs[a]), hs[a].shape[2]), lambda i, kr: (i, 0)) for a in first)),
        compiler_params=pltpu.CompilerParams(vmem_limit_bytes=40 << 20), interpret=_INTERPRET,
    )(chip, *hs, *recvs)


def _ada_bwd(araw, dmod, w):
    nblk = w.shape[1] // 512

    def body(a_ref, d_ref, w_ref, gw_ref, da_ref):
        j = pl.program_id(0)
        gw_ref[...] = _tn(_silu(a_ref[...]), d_ref[...])
        part = _nt(d_ref[...], w_ref[...])

        @pl.when(j == 0)
        def _():
            da_ref[...] = part

        @pl.when(j > 0)
        def _():
            da_ref[...] += part

    return _pcall(
        body, name="ada_bwd",
        out_shape=(jax.ShapeDtypeStruct(w.shape, F32), jax.ShapeDtypeStruct((16, D), F32)), grid=(nblk,),
        in_specs=[_full((16, D)), pl.BlockSpec((16, 512), lambda j: (0, j)), pl.BlockSpec((D, 512), lambda j: (0, j))],
        out_specs=(pl.BlockSpec((D, 512), lambda j: (0, j)), _full((16, D))), sem=("arbitrary",),
    )(araw, dmod, w)


def _w_specs():
    return [pl.BlockSpec((None, 2 * D, D), lambda j, i: (j, 0, 0)),
            pl.BlockSpec((None, WTAIL, D), lambda j, i: (jnp.maximum(j - 1, 0), 0, 0)),
            pl.BlockSpec((None, WTAIL, D), lambda j, i: (3, 0, 0))]


def _inproj(xin, mods, wi_main, wi_tail, t_total, tb, blk_off, prev, name, comm=None):
    n = xin.shape[0]
    nt = n // tb
    nslab = 4

    def body(x_ref, mod_ref, w_ref, wb_ref, wdt_ref, *rest):
        p_ref, pdt_ref, u_ref, uscr = rest[-4:]
        j, i = pl.program_id(0), pl.program_id(1)
        rows = pl.ds(pl.multiple_of(i * tb, tb), tb)

        @pl.when(j == 0)
        def _():
            xv = x_ref[...]
            r = lax.rsqrt(jnp.mean(xv * xv, axis=1, keepdims=True) + EPS)
            u = (xv * r * mod_ref[2:3, :]) * mod_ref[0:1, :] + mod_ref[1:2, :]
            ub = u.astype(MXU_DTYPE)
            uscr[rows, :] = ub
            u_ref[...] = ub
            pdt_ref[...] = _nt(ub, wdt_ref[...])

        ub = uscr[rows, :]
        pv = _nt(ub, w_ref[0:D, :])
        p_ref[:, D:2 * D] = _nt(ub, w_ref[D:2 * D, :]).astype(p_ref.dtype)

        @pl.when(j == 0)
        def _():
            p_ref[:, 0:D] = pv.astype(p_ref.dtype)

        @pl.when(j > 0)
        def _():
            head = pv[:, 0:WTAIL] + _nt(ub, wb_ref[...])
            p_ref[:, 0:D] = jnp.concatenate([head, pv[:, WTAIL:]], axis=1).astype(p_ref.dtype)

    once = lambda j, i: (jnp.where(j == 0, i, nt - 1) + blk_off, 0)
    in_specs = [pl.BlockSpec((tb, D), lambda j, i: (jnp.where(j == 0, i, nt - 1), 0)), _full((8, D))] + _w_specs()
    args = [xin, mods, wi_main, wi_tail, wi_tail]
    aliases = None
    if prev is not None:
        in_specs += [pl.BlockSpec(memory_space=pl.ANY)] * 3
        args += list(prev)
        aliases = {5: 0, 6: 1, 7: 2}
    call = dict(
        body=body, args=args, name=name,
        out_shape=(jax.ShapeDtypeStruct((t_total, nslab * 2 * D), MXU_DTYPE),
                   jax.ShapeDtypeStruct((t_total, 128), F32), jax.ShapeDtypeStruct((t_total, D), MXU_DTYPE)),
        grid=(nslab, nt), in_specs=in_specs,
        out_specs=(pl.BlockSpec((tb, 2 * D), lambda j, i: (i + blk_off, j)), pl.BlockSpec((tb, 128), once),
                   pl.BlockSpec((tb, D), once)),
        scratch=[pltpu.VMEM((n, D), MXU_DTYPE)], sem=("arbitrary", "arbitrary"), vmem_mb=56, aliases=aliases)
    steps = lambda: ((pl.program_id(0) == 0) & (pl.program_id(1) == 0),
                     (pl.program_id(0) == nslab - 1) & (pl.program_id(1) == nt - 1),
                     (pl.program_id(0) == nslab - 1) & (pl.program_id(1) == nt // 2))
    return _run(_carry(call, comm, steps))


def _blk(s, nb, rev):
    return jnp.where(s == 0, nb - 1, (nb - 1 - s) if rev else (s - 1))


def _hgrn_gate(fr, lbraw_ref, d):
    lb = _sig(lbraw_ref[d:d + 1, :] - lbraw_ref[2 + d:3 + d, :])
    sg = _sig(fr)
    return lb, sg, lb + (1.0 - lb) * sg


def _hgrn_fwd(p_main, lbraw, d, nb, comm=None):
    t_total = p_main.shape[0]
    rev = d == 1
    nch = TB // HC
    scale = HF ** -0.5

    def body(q_ref, f_ref, v_ref, lb_ref, o_ref, sp_ref, st):
        s = pl.program_id(0)

        @pl.when(s == 0)
        def _():
            st[...] = jnp.zeros_like(st)

        mb = _tri(HC, rev)
        m01 = _b01(mb)
        order = list(reversed(range(nch)) if rev else range(nch))
        hs_ = [slice(h * HF, (h + 1) * HF) for h in range(NH)]
        pre = {}
        for c in order:
            rows = slice(c * HC, (c + 1) * HC)
            _, _, f = _hgrn_gate(f_ref[rows, :].astype(F32), lb_ref, d)
            k = 1.0 - f
            cum = _dot01(m01, jnp.log(f))
            tot = cum[0:1, :] if rev else cum[HC - 1:HC, :]
            qd = _silu(q_ref[rows, :].astype(F32)) * scale * jnp.exp(cum)
            ki = k * jnp.exp(-cum)
            etot = jnp.exp(tot)
            pre[c] = (_mx(qd), _mx(ki), _mx(ki * etot), _mx(v_ref[rows, :]), etot)
        scs = {c: [_nt(pre[c][0][:, cs], pre[c][1][:, cs]) for cs in hs_] for c in order}
        upd = {c: [_tn(pre[c][3][:, cs], pre[c][2][:, cs]) for cs in hs_] for c in order}
        intra = {c: [_nn(jnp.where(mb, scs[c][h], 0.0), pre[c][3][:, cs]) for h, cs in enumerate(hs_)] for c in order}
        for c in order:
            rows = slice(c * HC, (c + 1) * HC)
            qdb, etot = pre[c][0], pre[c][4]
            for h, cs in enumerate(hs_):
                sth = st[h]
                stb = sth.astype(sp_ref.dtype)
                sp_ref[c, h] = stb
                o_ref[rows, cs] = (intra[c][h] + _nt(qdb[:, cs], stb)).astype(o_ref.dtype)
                st[h] = sth * etot[:, cs] + upd[c][h]

    col = lambda j: (lambda s: (_blk(s, nb, rev), j))
    call = dict(
        body=body, args=[p_main, p_main, p_main, lbraw], name=f"hgrn_fwd_{d}",
        out_shape=(jax.ShapeDtypeStruct((t_total, D), MXU_DTYPE),
                   jax.ShapeDtypeStruct((nch * nb, NH, HF, HF), MXU_DTYPE)),
        grid=(nb,),
        in_specs=[pl.BlockSpec((TB, D), col(0)), pl.BlockSpec((TB, D), col(1 + d)), pl.BlockSpec((TB, D), col(3)),
                  _full((8, D))],
        out_specs=(pl.BlockSpec((TB, D), col(0)),
                   pl.BlockSpec((nch, NH, HF, HF), lambda s: (_blk(s, nb, rev), 0, 0, 0))),
        scratch=[pltpu.VMEM((NH, HF, HF), F32)], sem=("arbitrary",), vmem_mb=40)
    return _run(_carry(call, comm, lambda: (pl.program_id(0) == 0, pl.program_id(0) == nb - 1,
                                            pl.program_id(0) == nb - 4)))


def _hgrn_bwd(p_main, lbraw, sprev, do, d, nb, prev, comm=None):
    t_total = p_main.shape[0]
    rev = d == 1
    nch = TB // HC
    scale = HF ** -0.5
    last = prev is not None
    odt = MXU_DTYPE if last else F32

    def body(q_ref, f_ref, v_ref, lb_ref, sp_ref, do_ref, *rest):
        if last:
            dqp_ref, dvp_ref = rest[:2]
            rest = rest[2:]
        dq_ref, df_ref, dv_ref, dlb_ref, dst = rest
        sp_id = pl.program_id(0)
        is_ctx = sp_id == nb - 1

        @pl.when(sp_id == 0)
        def _():
            dst[...] = jnp.zeros_like(dst)
            dlb_ref[...] = jnp.zeros_like(dlb_ref)

        mb = _tri(HC, rev)
        mbt = _tri(HC, not rev)
        m01 = _b01(mb)
        mt01 = _b01(mbt)
        order = list(range(nch) if rev else reversed(range(nch)))
        hs_ = [slice(h * HF, (h + 1) * HF) for h in range(NH)]
        pre = {}
        for c in order:
            rows = slice(c * HC, (c + 1) * HC)
            lb, sg, f = _hgrn_gate(f_ref[rows, :].astype(F32), lb_ref, d)
            k = 1.0 - f
            cum = _dot01(m01, jnp.log(f))
            tot = cum[0:1, :] if rev else cum[HC - 1:HC, :]
            e = jnp.exp(cum)
            ei = jnp.exp(-cum)
            etot = jnp.exp(tot)
            ee = ei * etot
            qraw = q_ref[rows, :].astype(F32)
            sq = _sig(qraw)
            qd = qraw * sq * scale * e
            ki = k * ei
            ke = k * ee
            dov = jnp.where(is_ctx, 0.0, do_ref[rows, :].astype(F32))
            pre[c] = dict(lb=lb, sg=sg, f=f, e=e, ei=ei, ee=ee, etot=etot, qd=qd, ki=ki, ke=ke,
                          dsq=sq * (1.0 + qraw * (1.0 - sq)),
                          qdb=_mx(qd), kib=_mx(ki), keb=_mx(ke), vb=_mx(v_ref[rows, :]), dob=_mx(dov))
        units = [(c, h) for c in order for h in range(NH)]
        col = lambda u, key: pre[u[0]][key][:, hs_[u[1]]]
        pt = {u: jnp.where(mbt, _nt(col(u, "kib"), col(u, "qdb")), 0.0) for u in units}
        dp = {u: jnp.where(mb, _nt(col(u, "dob"), col(u, "vb")), 0.0) for u in units}
        dpt = {u: jnp.where(mbt, _nt(col(u, "vb"), col(u, "dob")), 0.0) for u in units}
        dv_i = {u: _nn(pt[u], col(u, "dob")) for u in units}
        dqd_ = {u: _nn(dp[u], col(u, "kib")) + _nn(col(u, "dob"), sp_ref[u[0], u[1]]) for u in units}
        dki_ = {u: _nn(dpt[u], col(u, "qdb")) for u in units}
        dsl = {u: _tn(col(u, "dob"), col(u, "qdb")) for u in units}
        for c in order:
            rows = slice(c * HC, (c + 1) * HC)
            p = pre[c]
            dv_l, dke_l, dtot_l = [], [], []
            for h, cs in enumerate(hs_):
                dso = dst[h]
                dsob = _mx(dso)
                dv_l.append(dv_i[(c, h)] + _nt(p["keb"][:, cs], dsob))
                dke_l.append(_nn(p["vb"][:, cs], dsob))
                dtot_l.append(_colsum(dso * sp_ref[c, h].astype(F32)) * p["etot"][:, cs])
                dst[h] = dso * p["etot"][:, cs] + dsl[(c, h)]
            lb, sg, f, e, ei, ee, qd, ki, ke = (p[n_] for n_ in ("lb", "sg", "f", "e", "ei", "ee", "qd", "ki", "ke"))
            dqd = jnp.concatenate([dqd_[(c, h)] for h in range(NH)], axis=1)
            dki = jnp.concatenate([dki_[(c, h)] for h in range(NH)], axis=1)
            dke = jnp.concatenate(dke_l, axis=1)
            dcum = dqd * qd - dki * ki - dke * ke
            dtot = jnp.concatenate(dtot_l, axis=1) + _colsum(dke * ke)
            dk = dki * ei + dke * ee
            dlf = _dot01(mt01, dcum, ways=2) + dtot
            df = dlf / f - dk
            dlb_ref[0:1, :] += _colsum(df * (1.0 - sg))
            dfr = df * (1.0 - lb) * sg * (1.0 - sg)
            dq = dqd * e * scale * p["dsq"]
            dv = jnp.concatenate(dv_l, axis=1)
            if last:
                dq = dq + dqp_ref[rows, :]
                dv = dv + dvp_ref[rows, :]
            dq_ref[rows, :] = dq.astype(odt)
            dv_ref[rows, :] = dv.astype(odt)
            df_ref[rows, :] = dfr.astype(MXU_DTYPE)

    blk = lambda s: _blk(nb - 1 - s, nb, rev)
    col = lambda j: (lambda s: (blk(s), j))
    in_specs = [pl.BlockSpec((TB, D), col(0)), pl.BlockSpec((TB, D), col(1 + d)), pl.BlockSpec((TB, D), col(3)),
                _full((8, D)), pl.BlockSpec((nch, NH, HF, HF), lambda s: (blk(s), 0, 0, 0)),
                pl.BlockSpec((TB, D), lambda s: (jnp.minimum(blk(s), nb - 2), 0))]
    args = [p_main, p_main, p_main, lbraw, sprev, do]
    if last:
        in_specs += [pl.BlockSpec((TB, D), col(0))] * 2
        args += list(prev)
    call = dict(
        body=body, args=args, name=f"hgrn_bwd_{d}",
        out_shape=(jax.ShapeDtypeStruct((t_total, D), odt), jax.ShapeDtypeStruct((t_total, D), MXU_DTYPE),
                   jax.ShapeDtypeStruct((t_total, D), odt), jax.ShapeDtypeStruct((8, D), F32)),
        grid=(nb,), in_specs=in_specs,
        out_specs=(pl.BlockSpec((TB, D), col(0)), pl.BlockSpec((TB, D), col(0)), pl.BlockSpec((TB, D), col(0)),
                   _full((8, D))),
        scratch=[pltpu.VMEM((NH, HF, HF), F32)], sem=("arbitrary",), vmem_mb=48)
    return _run(_carry(call, comm, lambda: (pl.program_id(0) == 0, pl.program_id(0) == nb - 1)))


def _conv_masks(tb, is_ctx):
    seg = jnp.where(is_ctx, tb, GRID_W)
    pos = lax.broadcasted_iota(jnp.int32, (tb, 1), 0) & (seg - 1)
    return pos, seg


def _shift_rows(x, dshift, pos, seg):
    if dshift == 0:
        return x
    n = x.shape[0]
    rolled = pltpu.roll(x, (-dshift) % n, 0)
    ok = (pos + dshift >= 0) & (pos + dshift < seg)
    return jnp.where(ok, rolled, 0.0)


def _ssd_prep(p_main, p_dt, convp, dtb, nb):
    t_total = p_main.shape[0]

    def body(x_ref, dt_ref, cw_ref, dtb_ref, xa_ref, ds_ref, dts_ref):
        is_ctx = pl.program_id(0) == nb - 1
        pos, seg = _conv_masks(TB, is_ctx)
        xv = x_ref[...].astype(F32)
        acc = cw_ref[5:6, :] + cw_ref[2:3, :] * xv
        for kk in (0, 1, 3, 4):
            acc = acc + cw_ref[kk:kk + 1, :] * _shift_rows(xv, kk - 2, pos, seg)
        sg = _sig(acc)
        xa_ref[...] = (acc * sg).astype(xa_ref.dtype)
        ds_ref[...] = (sg * (1.0 + acc * (1.0 - sg))).astype(ds_ref.dtype)
        dts_ref[...] = _softplus(dt_ref[...] + dtb_ref[0:1, :])

    wide = pl.BlockSpec((TB, 2048), lambda i: (i, 0))
    return _pcall(
        body, name="ssd_prep",
        out_shape=(jax.ShapeDtypeStruct((t_total, 2048), MXU_DTYPE), jax.ShapeDtypeStruct((t_total, 2048), MXU_DTYPE),
                   jax.ShapeDtypeStruct((t_total, 128), F32)),
        grid=(nb,),
        in_specs=[pl.BlockSpec((TB, 2048), lambda i: (i, 3)), pl.BlockSpec((TB, 128), lambda i: (i, 0)),
                  _full((8, 2048)), _full((8, 128))],
        out_specs=(wide, wide, pl.BlockSpec((TB, 128), lambda i: (i, 0))),
        sem=("parallel",), vmem_mb=32,
    )(p_main, p_dt, convp, dtb)


def _ssd_prep_bwd(p_main, p_dt, convp, dtb, dsl, dxa, dxs_skip, ddts, nb):
    t_total = p_main.shape[0]

    def body(x_ref, dt_ref, cw_ref, dtb_ref, ds_ref, dxa_ref, dsk_ref, ddts_ref, dx_ref, ddt_ref, dcw_ref, ddtb_ref):
        i = pl.program_id(0)
        is_ctx = i == nb - 1

        @pl.when(i == 0)
        def _():
            dcw_ref[...] = jnp.zeros_like(dcw_ref)
            ddtb_ref[...] = jnp.zeros_like(ddtb_ref)

        pos, seg = _conv_masks(TB, is_ctx)
        xv = x_ref[...].astype(F32)
        dact = dxa_ref[...]
        dact = jnp.concatenate([dact[:, :D] + jnp.where(is_ctx, 0.0, dsk_ref[...].astype(F32)), dact[:, D:]], axis=1)
        dpre = dact * ds_ref[...].astype(F32)
        dxv = cw_ref[2:3, :] * dpre
        dcw_ref[2:3, :] += _colsum(xv * dpre)
        for kk in (0, 1, 3, 4):
            sdp = _shift_rows(dpre, 2 - kk, pos, seg)
            dxv = dxv + cw_ref[kk:kk + 1, :] * sdp
            dcw_ref[kk:kk + 1, :] += _colsum(xv * sdp)
        dx_ref[...] = dxv.astype(dx_ref.dtype)
        dcw_ref[5:6, :] += _colsum(dpre)
        draw = ddts_ref[...] * _sig(dt_ref[...] + dtb_ref[0:1, :])
        ddt_ref[...] = draw.astype(ddt_ref.dtype)
        ddtb_ref[0:1, :] += _colsum(draw)

    return _pcall(
        body, name="ssd_prep_bwd",
        out_shape=(jax.ShapeDtypeStruct((t_total, 2048), MXU_DTYPE), jax.ShapeDtypeStruct((t_total, 128), MXU_DTYPE),
                   jax.ShapeDtypeStruct((8, 2048), F32), jax.ShapeDtypeStruct((8, 128), F32)),
        grid=(nb,),
        in_specs=[pl.BlockSpec((TB, 2048), lambda i: (i, 3)), pl.BlockSpec((TB, 128), lambda i: (i, 0)),
                  _full((8, 2048)), _full((8, 128)), pl.BlockSpec((TB, 2048), lambda i: (i, 0)),
                  pl.BlockSpec((TB, 2048), lambda i: (i, 0)),
                  pl.BlockSpec((TB, D), lambda i: (jnp.minimum(i, nb - 2), 0)),
                  pl.BlockSpec((TB, 128), lambda i: (i, 0))],
        out_specs=(pl.BlockSpec((TB, 2048), lambda i: (i, 0)), pl.BlockSpec((TB, 128), lambda i: (i, 0)),
                   _full((8, 2048)), _full((8, 128))),
        sem=("arbitrary",), vmem_mb=40,
    )(p_main, p_dt, convp, dtb, dsl, dxa, dxs_skip, ddts)


def _dot2(x, m01):
    hi = x.astype(BF16)
    lo = (x - hi.astype(F32)).astype(BF16)
    f = lambda t: lax.dot_general(t, m01, (((1,), (0,)), ((), ())), preferred_element_type=F32)
    return f(hi) + f(lo)


def _head_lanes(c0, c1):
    p = lax.broadcasted_iota(jnp.int32, (128, 128), 0)
    l = lax.broadcasted_iota(jnp.int32, (128, 128), 1)
    return _b01(((l == c0) & (p < SP)) | ((l == c1) & (p >= SP)))


def _one_lane(col):
    return _b01(lax.broadcasted_iota(jnp.int32, (128, 128), 1) == col)


def _lane_pick(x, lane, col):
    return _rowsum(jnp.where(lane == col, x, 0.0))


def _ssd_chunk_common(dts, alog_ref, m01, rev):
    lane = lax.broadcasted_iota(jnp.int32, (1, 128), 1)
    arow = -jnp.exp(alog_ref[0:1, :])
    cum = _dot01(m01, dts * arow)
    tot = cum[0:1, :] if rev else cum[SC - 1:SC, :]
    return lane, arow, cum, cum.T, tot


def _ssd_fwd(xa, dts, alog, d, nb, comm=None):
    t_total = xa.shape[0]
    rev = d == 1
    nch = TB // SC
    npair = SHEADS // 2

    def body(xa_ref, dts_ref, alog_ref, y_ref, sp_ref, st):
        s = pl.program_id(0)

        @pl.when(s == 0)
        def _():
            st[...] = jnp.zeros_like(st)

        mb = _tri(SC, rev)
        m01 = _b01(mb)
        lo = lax.broadcasted_iota(jnp.int32, (1, 128), 1) < SP
        rlo = lax.broadcasted_iota(jnp.int32, (128, 1), 0) < SP
        order = list(reversed(range(nch)) if rev else range(nch))
        pre = {}
        for c in order:
            rows = slice(c * SC, (c + 1) * SC)
            dts_c = dts_ref[rows, :]
            lane, arow, cum, cumt, tot = _ssd_chunk_common(dts_c, alog_ref, m01, rev)
            bgs = [_mx(xa_ref[rows, D + g * SN:D + (g + 1) * SN]) for g in range(4)]
            cgs = [_mx(xa_ref[rows, D + 512 + g * SN:D + 512 + (g + 1) * SN]) for g in range(4)]
            pairs = []
            for pr in range(npair):
                xs = xa_ref[rows, pr * 128:(pr + 1) * 128].astype(F32)
                cols = [16 * d + 2 * pr, 16 * d + 2 * pr + 1]
                cum_c = [_lane_pick(cum, lane, q) for q in cols]
                dt_c = [_lane_pick(dts_c, lane, q) for q in cols]
                tot_c = [_lane_pick(tot, lane, q) for q in cols]
                dtx = xs * jnp.where(lo, dt_c[0], dt_c[1])
                e1_pair = jnp.where(lo, jnp.exp(cum_c[0]), jnp.exp(cum_c[1]))
                e2_pair = jnp.where(lo, jnp.exp(tot_c[0] - cum_c[0]), jnp.exp(tot_c[1] - cum_c[1]))
                etot_col = jnp.where(rlo, jnp.exp(tot_c[0]), jnp.exp(tot_c[1]))
                decs = [jnp.where(mb, jnp.exp(cum_c[q] - cumt[cols[q]:cols[q] + 1, :]), 0.0) for q in range(2)]
                dtxq = [_mx(jnp.where(lo if q == 0 else ~lo, dtx, 0.0)) for q in range(2)]
                pairs.append(dict(e1=e1_pair, etot=etot_col, decs=decs, dtxq=dtxq, xe=_mx(dtx * e2_pair)))
            pre[c] = (bgs, cgs, pairs)
        gm = {(c, g): _nt(pre[c][1][g], pre[c][0][g]) for c in order for g in range(4)}
        upd = {(c, pr): _tn(pre[c][2][pr]["xe"], pre[c][0][pr // 2]) for c in order for pr in range(npair)}
        intra = {(c, pr): sum(_nn(gm[(c, pr // 2)] * pre[c][2][pr]["decs"][q], pre[c][2][pr]["dtxq"][q]) for q in range(2))
                 for c in order for pr in range(npair)}
        for c in order:
            rows = slice(c * SC, (c + 1) * SC)
            bgs, cgs, pairs = pre[c]
            for pr in range(npair):
                stp = st[pr]
                stb = stp.astype(sp_ref.dtype)
                sp_ref[c, pr] = stb
                y_ref[rows, pr * 128:(pr + 1) * 128] = (
                    intra[(c, pr)] + pairs[pr]["e1"] * _nt(cgs[pr // 2], stb)).astype(y_ref.dtype)
                st[pr] = stp * pairs[pr]["etot"] + upd[(c, pr)]

    blk = lambda s: _blk(s, nb, rev)
    call = dict(
        body=body, args=[xa, dts, alog], name=f"ssd_fwd_{d}",
        out_shape=(jax.ShapeDtypeStruct((t_total, D), MXU_DTYPE),
                   jax.ShapeDtypeStruct((nch * nb, npair, 128, SN), MXU_DTYPE)),
        grid=(nb,),
        in_specs=[pl.BlockSpec((TB, 2048), lambda s: (blk(s), 0)), pl.BlockSpec((TB, 128), lambda s: (blk(s), 0)),
                  _full((8, 128))],
        out_specs=(pl.BlockSpec((TB, D), lambda s: (blk(s), 0)),
                   pl.BlockSpec((nch, npair, 128, SN), lambda s: (blk(s), 0, 0, 0))),
        scratch=[pltpu.VMEM((npair, 128, SN), F32)], sem=("arbitrary",), vmem_mb=40)
    return _run(_carry(call, comm, lambda: (pl.program_id(0) == 0, pl.program_id(0) == nb - 1,
                                            pl.program_id(0) == nb - 4)))


def _ssd_bwd(xa, dts, alog, sprev, dy, d, nb, prev, comm=None):
    t_total = xa.shape[0]
    rev = d == 1
    nch = TB // SC
    npair = SHEADS // 2
    last = prev is not None

    def body(xa_ref, dts_ref, alog_ref, sp_ref, dy_ref, *rest):
        if last:
            dxp_ref, ddp_ref = rest[:2]
            rest = rest[2:]
        dxa_ref, ddts_ref, da_ref, dst, zc_scr = rest
        sp_id = pl.program_id(0)
        is_ctx = sp_id == nb - 1

        @pl.when(sp_id == 0)
        def _():
            dst[...] = jnp.zeros_like(dst)
            da_ref[...] = jnp.zeros_like(da_ref)
            zc_scr[...] = jnp.zeros_like(zc_scr)

        mb = _tri(SC, rev)
        m01 = _b01(mb)
        mt01 = _b01(_tri(SC, not rev))
        lo = lax.broadcasted_iota(jnp.int32, (1, 128), 1) < SP
        rlo = lax.broadcasted_iota(jnp.int32, (128, 1), 0) < SP
        order = list(range(nch) if rev else reversed(range(nch)))
        pre = {}
        for c in order:
            rows = slice(c * SC, (c + 1) * SC)
            dts_c = dts_ref[rows, :]
            lane, arow, cum, cumt, tot = _ssd_chunk_common(dts_c, alog_ref, m01, rev)
            pairs = []
            for pr in range(npair):
                xs = xa_ref[rows, pr * 128:(pr + 1) * 128].astype(F32)
                dyp = jnp.where(is_ctx, 0.0, dy_ref[rows, pr * 128:(pr + 1) * 128].astype(F32))
                cols = [16 * d + 2 * pr, 16 * d + 2 * pr + 1]
                cum_c = [_lane_pick(cum, lane, q) for q in cols]
                dt_c = [_lane_pick(dts_c, lane, q) for q in cols]
                tot_c = [_lane_pick(tot, lane, q) for q in cols]
                e1_c = [jnp.exp(cum_c[q]) for q in range(2)]
                e2_c = [jnp.exp(tot_c[q] - cum_c[q]) for q in range(2)]
                etot_c = [jnp.exp(tot_c[q]) for q in range(2)]
                dt_pair = jnp.where(lo, dt_c[0], dt_c[1])
                e1_pair = jnp.where(lo, e1_c[0], e1_c[1])
                e2_pair = jnp.where(lo, e2_c[0], e2_c[1])
                dtx = xs * dt_pair
                decs = [jnp.where(mb, jnp.exp(cum_c[q] - cumt[cols[q]:cols[q] + 1, :]), 0.0) for q in range(2)]
                dyq = [_mx(jnp.where(lo if q == 0 else ~lo, dyp, 0.0)) for q in range(2)]
                pairs.append(dict(xs=xs, dyp=dyp, cols=cols, e1_c=e1_c, e2_c=e2_c, etot_c=etot_c, dt_pair=dt_pair,
                                  e2_pair=e2_pair, etot_col=jnp.where(rlo, etot_c[0], etot_c[1]), dtx=dtx,
                                  dtxb=_mx(dtx), xeb=_mx(dtx * e2_pair), dy0b=_mx(dyp * e1_pair), decs=decs, dyq=dyq))
            pre[c] = dict(lane=lane, arow=arow, dts=dts_c, pairs=pairs, cum=cum, tot=tot,
                          bgb=[_mx(xa_ref[rows, D + g * SN:D + (g + 1) * SN]) for g in range(4)],
                          cgb=[_mx(xa_ref[rows, D + 512 + g * SN:D + 512 + (g + 1) * SN]) for g in range(4)])
        units = [(c, pr) for c in order for pr in range(npair)]
        head_lanes = [_head_lanes(16 * d + 2 * pr, 16 * d + 2 * pr + 1) for pr in range(npair)]
        one_lane = {16 * d + h: _one_lane(16 * d + h) for h in range(SHEADS)}
        P = lambda u: pre[u[0]]["pairs"][u[1]]
        cgu = lambda u: pre[u[0]]["cgb"][u[1] // 2]
        gm = {(c, g): _nt(pre[c]["cgb"][g], pre[c]["bgb"][g]) for c in order for g in range(4)}
        y0 = {u: _nt(cgu(u), sp_ref[u[0], u[1]]) for u in units}
        dcg_i = {u: _nn(P(u)["dy0b"], sp_ref[u[0], u[1]]) for u in units}
        dsl = {u: _tn(P(u)["dy0b"], cgu(u)) for u in units}
        w_ = {(u, q): gm[(u[0], u[1] // 2)] * P(u)["decs"][q] for u in units for q in range(2)}
        dw_ = {(u, q): jnp.where(mb, _nt(P(u)["dyq"][q], P(u)["dtxb"]), 0.0) for u in units for q in range(2)}
        ddtx_i = {(u, q): _tn(w_[(u, q)], P(u)["dyq"][q]) for u in units for q in range(2)}
        for c in order:
            rows = slice(c * SC, (c + 1) * SC)
            pc = pre[c]
            lane, arow, dts_c = pc["lane"], pc["arow"], pc["dts"]
            d1 = jnp.zeros((SC, 128), F32)
            d2 = jnp.zeros((SC, 128), F32)
            dz = jnp.zeros((SC, 128), F32)
            ddt = jnp.zeros((SC, 128), F32)
            dtot = jnp.zeros((1, 128), F32)
            dgm = [jnp.zeros((SC, SC), F32) for _ in range(4)]
            dbg = [jnp.zeros((SC, SN), F32) for _ in range(4)]
            dcg = [jnp.zeros((SC, SN), F32) for _ in range(4)]
            for pr in range(npair):
                u, g, p = (c, pr), pr // 2, pc["pairs"][pr]
                hs = head_lanes[pr]
                dso = dst[pr]
                dsob = _mx(dso)
                dxe = _nt(pc["bgb"][g], dsob)
                dbg[g] = dbg[g] + _nn(p["xeb"], dsob)
                ddtx = dxe * p["e2_pair"]
                d2 = d2 + _dot2(dxe * p["dtx"], hs)
                dcg[g] = dcg[g] + dcg_i[u]
                d1 = d1 + _dot2(p["dyp"] * y0[u], hs)
                sprod = dso * sp_ref[c, pr].astype(F32)
                dst[pr] = dso * p["etot_col"] + dsl[u]
                for q in range(2):
                    hm = lo if q == 0 else ~lo
                    col = p["cols"][q]
                    dw = dw_[(u, q)]
                    ddtx = ddtx + jnp.where(hm, ddtx_i[(u, q)], 0.0)
                    dgm[g] = dgm[g] + dw * p["decs"][q]
                    z = dw * w_[(u, q)]
                    dz = dz + _dot2(z, one_lane[col])
                    zc_scr[col:col + 1, :] = _colsum(z)
                    tsum = _rowsum(_colsum(sprod[q * SP:(q + 1) * SP, :]))
                    dtot = jnp.where(lane == col, tsum * p["etot_c"][q], dtot)
                dxs = ddtx * p["dt_pair"]
                ddt = ddt + _dot2(ddtx * p["xs"], hs)
                if last:
                    dxs = dxs + dxp_ref[rows, pr * 128:(pr + 1) * 128]
                dxa_ref[rows, pr * 128:(pr + 1) * 128] = dxs
            e2_all = jnp.exp(pc["tot"] - pc["cum"])
            dcum = dz - zc_scr[...].T + d1 * jnp.exp(pc["cum"]) - d2 * e2_all
            dtot = dtot + _colsum(d2 * e2_all)
            for g in range(4):
                db = dbg[g] + _tn(dgm[g], pc["cgb"][g])
                dc = dcg[g] + _nn(dgm[g], pc["bgb"][g])
                if last:
                    db = db + dxp_ref[rows, D + g * SN:D + (g + 1) * SN]
                    dc = dc + dxp_ref[rows, D + 512 + g * SN:D + 512 + (g + 1) * SN]
                dxa_ref[rows, D + g * SN:D + (g + 1) * SN] = db
                dxa_ref[rows, D + 512 + g * SN:D + 512 + (g + 1) * SN] = dc
            dla = _dot01(mt01, dcum, ways=2) + dtot
            ddt = ddt + dla * arow
            da_ref[0:1, :] += _colsum(dla * dts_c)
            if last:
                ddt = ddt + ddp_ref[rows, :]
            ddts_ref[rows, :] = ddt

    blk = lambda s: _blk(nb - 1 - s, nb, rev)
    in_specs = [pl.BlockSpec((TB, 2048), lambda s: (blk(s), 0)), pl.BlockSpec((TB, 128), lambda s: (blk(s), 0)),
                _full((8, 128)), pl.BlockSpec((nch, npair, 128, SN), lambda s: (blk(s), 0, 0, 0)),
                pl.BlockSpec((TB, D), lambda s: (jnp.minimum(blk(s), nb - 2), 0))]
    args = [xa, dts, alog, sprev, dy]
    if last:
        in_specs += [pl.BlockSpec((TB, 2048), lambda s: (blk(s), 0)), pl.BlockSpec((TB, 128), lambda s: (blk(s), 0))]
        args += list(prev)
    call = dict(
        body=body, args=args, name=f"ssd_bwd_{d}",
        out_shape=(jax.ShapeDtypeStruct((t_total, 2048), F32), jax.ShapeDtypeStruct((t_total, 128), F32),
                   jax.ShapeDtypeStruct((8, 128), F32)),
        grid=(nb,), in_specs=in_specs,
        out_specs=(pl.BlockSpec((TB, 2048), lambda s: (blk(s), 0)), pl.BlockSpec((TB, 128), lambda s: (blk(s), 0)),
                   _full((8, 128))),
        scratch=[pltpu.VMEM((npair, 128, SN), F32), pltpu.VMEM((128, 128), F32)], sem=("arbitrary",), vmem_mb=48)
    return _run(_carry(call, comm, lambda: (pl.program_id(0) == 0, pl.program_id(0) == nb - 1)))


def _readout(o, g, yy, z, vec_ref):
    hg, ss, keep = [], [], []
    for h in range(NH):
        cs = slice(h * HF, (h + 1) * HF)
        oh = o[:, cs]
        r = lax.rsqrt(jnp.mean(oh * oh, axis=1, keepdims=True) + EPS)
        hg.append(oh * r * vec_ref[0:1, cs] * _silu(g[:, cs]))
        keep.append(r)
    u = yy * _silu(z)
    for gi in range(4):
        cs = slice(gi * 256, (gi + 1) * 256)
        ug = u[:, cs]
        r = lax.rsqrt(jnp.mean(ug * ug, axis=1, keepdims=True) + EPS)
        ss.append(ug * r * vec_ref[2:3, cs])
        keep.append(r)
    return jnp.concatenate(hg, axis=1), jnp.concatenate(ss, axis=1), keep, u


def _mix_out(o_f, o_b, p_main, y_f, y_b, xa, x, vecs, w_out):
    n = x.shape[0]

    def body(of_ref, ob_ref, g_ref, z_ref, yf_ref, yb_ref, xs_ref, x_ref, vec_ref, w_ref,
             ymix_ref, ylat_ref, h1_ref, u2_ref):
        o = of_ref[...].astype(F32) + ob_ref[...].astype(F32)
        yy = yf_ref[...].astype(F32) + yb_ref[...].astype(F32) + vec_ref[1:2, :] * xs_ref[...].astype(F32)
        hg, ss, _, _ = _readout(o, g_ref[...].astype(F32), yy, z_ref[...].astype(F32), vec_ref)
        ymix = jnp.concatenate([hg, ss], axis=1).astype(MXU_DTYPE)
        ymix_ref[...] = ymix
        ylat = _nn(ymix, w_ref[...])
        ylat_ref[...] = ylat
        h1 = x_ref[...] + vec_ref[3:4, :] * ylat
        h1_ref[...] = h1
        r = lax.rsqrt(jnp.mean(h1 * h1, axis=1, keepdims=True) + EPS)
        u2_ref[...] = ((h1 * r * vec_ref[6:7, :]) * vec_ref[4:5, :] + vec_ref[5:6, :]).astype(MXU_DTYPE)

    row = lambda j: (lambda i: (i, j))
    return _pcall(
        body, name="mix_out",
        out_shape=(jax.ShapeDtypeStruct((n, 2 * D), MXU_DTYPE), jax.ShapeDtypeStruct((n, D), F32),
                   jax.ShapeDtypeStruct((n, D), F32), jax.ShapeDtypeStruct((n, D), MXU_DTYPE)),
        grid=(n // TB,),
        in_specs=[pl.BlockSpec((TB, D), row(0)), pl.BlockSpec((TB, D), row(0)), pl.BlockSpec((TB, D), row(4)),
                  pl.BlockSpec((TB, D), row(5)), pl.BlockSpec((TB, D), row(0)), pl.BlockSpec((TB, D), row(0)),
                  pl.BlockSpec((TB, D), row(0)), pl.BlockSpec((TB, D), row(0)), _full((8, D)), _full((2 * D, D))],
        out_specs=(pl.BlockSpec((TB, 2 * D), row(0)), pl.BlockSpec((TB, D), row(0)), pl.BlockSpec((TB, D), row(0)),
                   pl.BlockSpec((TB, D), row(0))),
        sem=("parallel",), vmem_mb=48,
    )(o_f, o_b, p_main, p_main, y_f, y_b, xa, x, vecs, w_out)


def _mix_bwd(dylat, o_f, o_b, p_main, y_f, y_b, xa, vecs, w_out, comm=None):
    n = dylat.shape[0]
    t_total = p_main.shape[0]
    nlat = n // TB

    def body(*refs):
        dg_ref, dz_ref, acc_ref = refs[11], refs[13], refs[15]
        i = pl.program_id(0)

        @pl.when(i == 0)
        def _():
            acc_ref[...] = jnp.zeros_like(acc_ref)

        @pl.when(i < nlat)
        def _():
            compute(*refs)

        @pl.when(i == nlat)
        def _():
            dg_ref[...] = jnp.zeros_like(dg_ref)
            dz_ref[...] = jnp.zeros_like(dz_ref)

    def compute(dyl_ref, of_ref, ob_ref, g_ref, z_ref, yf_ref, yb_ref, xs_ref, vec_ref, w_ref,
                do_ref, dg_ref, dys_ref, dz_ref, dxs_ref, acc_ref):
        dymix = _nt(dyl_ref[...], w_ref[...])
        o = of_ref[...].astype(F32) + ob_ref[...].astype(F32)
        g = g_ref[...].astype(F32)
        z = z_ref[...].astype(F32)
        xs = xs_ref[...].astype(F32)
        yy = yf_ref[...].astype(F32) + yb_ref[...].astype(F32) + vec_ref[1:2, :] * xs
        _, _, keep, u = _readout(o, g, yy, z, vec_ref)
        do_l, dg_l = [], []
        for h in range(NH):
            cs = slice(h * HF, (h + 1) * HF)
            oh, gh, r, wv = o[:, cs], g[:, cs], keep[h], vec_ref[0:1, cs]
            dhg = dymix[:, cs]
            xh = oh * r
            dn = dhg * _silu(gh)
            dg_l.append(dhg * xh * wv * _dsilu(gh))
            acc_ref[0:1, cs] += _colsum(dn * xh)
            dxh = dn * wv
            do_l.append(r * (dxh - xh * jnp.mean(dxh * xh, axis=1, keepdims=True)))
        du_l = []
        for gi in range(4):
            cs = slice(gi * 256, (gi + 1) * 256)
            ug, r, wv = u[:, cs], keep[NH + gi], vec_ref[2:3, cs]
            dss = dymix[:, D + gi * 256:D + (gi + 1) * 256]
            xh = ug * r
            acc_ref[2:3, cs] += _colsum(dss * xh)
            dxh = dss * wv
            du_l.append(r * (dxh - xh * jnp.mean(dxh * xh, axis=1, keepdims=True)))
        du = jnp.concatenate(du_l, axis=1)
        dyy = du * _silu(z)
        do_ref[...] = jnp.concatenate(do_l, axis=1).astype(do_ref.dtype)
        dg_ref[...] = jnp.concatenate(dg_l, axis=1).astype(dg_ref.dtype)
        dys_ref[...] = dyy.astype(dys_ref.dtype)
        dz_ref[...] = (du * yy * _dsilu(z)).astype(dz_ref.dtype)
        dxs_ref[...] = (dyy * vec_ref[1:2, :]).astype(dxs_ref.dtype)
        acc_ref[1:2, :] += _colsum(dyy * xs)

    row = lambda j: (lambda i: (jnp.minimum(i, nlat - 1), j))
    lat = pl.BlockSpec((TB, D), row(0))
    tok = pl.BlockSpec((TB, D), lambda i: (i, 0))
    call = dict(
        body=body, args=[dylat, o_f, o_b, p_main, p_main, y_f, y_b, xa, vecs, w_out], name="mix_bwd",
        out_shape=(jax.ShapeDtypeStruct((n, D), MXU_DTYPE), jax.ShapeDtypeStruct((t_total, D), MXU_DTYPE),
                   jax.ShapeDtypeStruct((n, D), MXU_DTYPE), jax.ShapeDtypeStruct((t_total, D), MXU_DTYPE),
                   jax.ShapeDtypeStruct((n, D), MXU_DTYPE), jax.ShapeDtypeStruct((8, D), F32)),
        grid=(t_total // TB,),
        in_specs=[lat, lat, lat, pl.BlockSpec((TB, D), row(4)), pl.BlockSpec((TB, D), row(5)), lat, lat, lat,
                  _full((8, D)), _full((2 * D, D))],
        out_specs=(lat, tok, lat, tok, lat, _full((8, D))), scratch=[],
        sem=("arbitrary",), vmem_mb=48)
    return _run(_carry(call, comm, lambda: (pl.program_id(0) == 0, pl.program_id(0) == t_total // TB - 1)))


def _ffn_up(u2, w_gate, w_up):
    n = u2.shape[0]
    tb = 1024

    def body(u_ref, wg_ref, wu_ref, g_ref, up_ref, a_ref):
        uv = u_ref[...]
        gt = _nt(uv, wg_ref[...])
        upv = _nt(uv, wu_ref[...])
        g_ref[...] = gt.astype(g_ref.dtype)
        up_ref[...] = upv.astype(up_ref.dtype)
        a_ref[...] = (_silu(gt) * upv).astype(a_ref.dtype)

    blk = pl.BlockSpec((tb, FSL), lambda j, i: (i, j))
    wblk = pl.BlockSpec((None, FSL, D), lambda j, i: (j, 0, 0))
    return _pcall(
        body, name="ffn_up",
        out_shape=(jax.ShapeDtypeStruct((n, DFFP), MXU_DTYPE),) * 3,
        grid=(4, n // tb), in_specs=[pl.BlockSpec((tb, D), lambda j, i: (i, 0)), wblk, wblk],
        out_specs=(blk, blk, blk), sem=("parallel", "parallel"), vmem_mb=48,
    )(u2, w_gate, w_up)


def _ffn_down_loss(act, w_down, h1, tgt, vecs):
    n = act.shape[0]
    tb = 512

    def body(a_ref, w_ref, h1_ref, t_ref, vec_ref, dh2_ref, dffn_ref, acc_ref):
        i = pl.program_id(0)

        @pl.when(i == 0)
        def _():
            acc_ref[...] = jnp.zeros_like(acc_ref)

        g2 = vec_ref[0:1, :]
        fw = vec_ref[1:2, :]
        nsub = 4
        sb = tb // nsub
        wv = w_ref[...]
        ffns = [_nn(a_ref[r_ * sb:(r_ + 1) * sb, :], wv) for r_ in range(nsub)]
        for r_ in range(nsub):
            rows = slice(r_ * sb, (r_ + 1) * sb)
            ffn = ffns[r_]
            h2 = h1_ref[rows, :] + g2 * ffn
            r = lax.rsqrt(jnp.mean(h2 * h2, axis=1, keepdims=True) + EPS)
            xh = h2 * r
            err = xh * fw - t_ref[rows, :]
            dy = err * (1.0 / D)
            acc_ref[2:3, :] += _colsum(err * err) * (0.5 / D)
            acc_ref[1:2, :] += _colsum(dy * xh)
            dxh = dy * fw
            dh2 = r * (dxh - xh * jnp.mean(dxh * xh, axis=1, keepdims=True))
            dh2_ref[rows, :] = dh2
            dffn_ref[rows, :] = (g2 * dh2).astype(dffn_ref.dtype)
            acc_ref[0:1, :] += _colsum(dh2 * ffn)

    return _pcall(
        body, name="ffn_down_loss",
        out_shape=(jax.ShapeDtypeStruct((n, D), F32), jax.ShapeDtypeStruct((n, D), MXU_DTYPE),
                   jax.ShapeDtypeStruct((8, D), F32)),
        grid=(n // tb,),
        in_specs=[pl.BlockSpec((tb, DFFP), lambda i: (i, 0)), _full((DFFP, D)), pl.BlockSpec((tb, D), lambda i: (i, 0)),
                  pl.BlockSpec((tb, D), lambda i: (i, 0)), _full((8, D))],
        out_specs=(pl.BlockSpec((tb, D), lambda i: (i, 0)), pl.BlockSpec((tb, D), lambda i: (i, 0)), _full((8, D))),
        sem=("arbitrary",), vmem_mb=48,
    )(act, w_down, h1, tgt, vecs)


def _ffn_bwd(dffn, w_down, gate, up, w_gate_t, w_up_t):
    n = dffn.shape[0]
    tb = 1024

    def body(df_ref, wd_ref, g_ref, up_ref, wg_ref, wu_ref, dg_ref, dup_ref, du_ref):
        j = pl.program_id(1)
        nsub = 4
        sb = tb // nsub
        wd, wg, wu = wd_ref[...], wg_ref[...], wu_ref[...]
        dacts = [_nt(df_ref[r * sb:(r + 1) * sb, :], wd) for r in range(nsub)]
        parts = []
        for r in range(nsub):
            rows = slice(r * sb, (r + 1) * sb)
            gt = g_ref[rows, :].astype(F32)
            upv = up_ref[rows, :].astype(F32)
            sg = _sig(gt)
            dgt = (dacts[r] * upv * (sg * (1.0 + gt * (1.0 - sg)))).astype(MXU_DTYPE)
            dupv = (dacts[r] * (gt * sg)).astype(MXU_DTYPE)
            dg_ref[rows, :] = dgt
            dup_ref[rows, :] = dupv
            parts.append(_nn(dgt, wg) + _nn(dupv, wu))
        part = jnp.concatenate(parts, axis=0)

        @pl.when(j == 0)
        def _():
            du_ref[...] = part

        @pl.when(j > 0)
        def _():
            du_ref[...] += part

    tok = pl.BlockSpec((tb, D), lambda i, j: (i, 0))
    ffb = pl.BlockSpec((tb, FSL), lambda i, j: (i, j))
    wsl = pl.BlockSpec((None, FSL, D), lambda i, j: (j, 0, 0))
    return _pcall(
        body, name="ffn_bwd",
        out_shape=(jax.ShapeDtypeStruct((n, DFFP), MXU_DTYPE), jax.ShapeDtypeStruct((n, DFFP), MXU_DTYPE),
                   jax.ShapeDtypeStruct((n, D), F32)),
        grid=(n // tb, 4),
        in_specs=[tok, pl.BlockSpec((FSL, D), lambda i, j: (j, 0)), ffb, ffb, wsl, wsl],
        out_specs=(ffb, ffb, tok), sem=("parallel", "arbitrary"), vmem_mb=48,
    )(dffn, w_down, gate, up, w_gate_t, w_up_t)


def _ffn_norm_bwd(du, h1, ylat, dh2, vecs):
    n = du.shape[0]
    tb = 512

    def body(du_ref, h1_ref, yl_ref, dh2_ref, vec_ref, dh1_ref, dyl_ref, acc_ref):
        @pl.when(pl.program_id(0) == 0)
        def _():
            acc_ref[...] = jnp.zeros_like(acc_ref)

        duv = du_ref[...]
        h1 = h1_ref[...]
        r = lax.rsqrt(jnp.mean(h1 * h1, axis=1, keepdims=True) + EPS)
        xh = h1 * r
        nw = vec_ref[2:3, :]
        acc_ref[0:1, :] += _colsum(duv)
        acc_ref[1:2, :] += _colsum(duv * xh * nw)
        dn = duv * vec_ref[1:2, :]
        acc_ref[2:3, :] += _colsum(dn * xh)
        dxh = dn * nw
        dh1 = dh2_ref[...] + r * (dxh - xh * jnp.mean(dxh * xh, axis=1, keepdims=True))
        dh1_ref[...] = dh1
        dyl_ref[...] = (vec_ref[0:1, :] * dh1).astype(dyl_ref.dtype)
        acc_ref[3:4, :] += _colsum(dh1 * yl_ref[...])

    tok = pl.BlockSpec((tb, D), lambda i: (i, 0))
    return _pcall(
        body, name="ffn_norm_bwd",
        out_shape=(jax.ShapeDtypeStruct((n, D), F32), jax.ShapeDtypeStruct((n, D), MXU_DTYPE),
                   jax.ShapeDtypeStruct((8, D), F32)),
        grid=(n // tb,), in_specs=[tok, tok, tok, tok, _full((8, D))], out_specs=(tok, tok, _full((8, D))),
        sem=("arbitrary",), vmem_mb=40,
    )(du, h1, ylat, dh2, vecs)


def _deep_rows(rows):
    return max(r for r in range(128, 2305, 128) if rows % r == 0)


def _dw(a, b, name):
    tn_rows = a.shape[0]
    bt = _deep_rows(tn_rows)
    kk, nn_ = a.shape[1], b.shape[1]
    bk = 1024 if kk % 1024 == 0 else kk
    bn = 1024 if nn_ % 1024 == 0 else nn_
    nt = tn_rows // bt

    def body(a_ref, b_ref, o_ref, acc):
        t = pl.program_id(2)
        part = _tn(a_ref[...], b_ref[...])

        @pl.when(t == 0)
        def _():
            acc[...] = part

        @pl.when(t > 0)
        def _():
            acc[...] += part

        @pl.when(t == nt - 1)
        def _():
            o_ref[...] = acc[...].astype(o_ref.dtype)

    return _pcall(
        body, name=name, out_shape=jax.ShapeDtypeStruct((kk, nn_), MXU_DTYPE), grid=(kk // bk, nn_ // bn, nt),
        in_specs=[pl.BlockSpec((bt, bk), lambda i, j, t: (t, i)), pl.BlockSpec((bt, bn), lambda i, j, t: (t, j))],
        out_specs=pl.BlockSpec((bk, bn), lambda i, j, t: (i, j)), scratch=[pltpu.VMEM((bk, bn), F32)],
        sem=("parallel", "parallel", "arbitrary"), vmem_mb=40,
    )(a, b)


def _dw_in(segs, u_all, name):
    tiles = []
    for m, s_ in enumerate(segs):
        tiles += [(m, h) for h in range(s_.shape[1] // D)]
    ntile = len(tiles)
    t_total = u_all.shape[0]
    bt = _deep_rows(t_total)
    nt = t_total // bt

    def body(u_ref, *refs):
        seg_refs, o_ref, acc = refs[:len(segs)], refs[len(segs)], refs[len(segs) + 1]
        n, t = pl.program_id(0), pl.program_id(1)
        for k, (m, _) in enumerate(tiles):
            @pl.when(n == k)
            def _(m=m):
                part = _tn(seg_refs[m][...], u_ref[...])

                @pl.when(t == 0)
                def _():
                    acc[...] = part

                @pl.when(t > 0)
                def _():
                    acc[...] += part

        @pl.when(t == nt - 1)
        def _():
            o_ref[...] = acc[...].astype(o_ref.dtype)

    def seg_spec(m):
        ks = [k for k, (mm, _) in enumerate(tiles) if mm == m]
        lo, hi = ks[0], ks[-1]
        on = lambda n: (n >= lo) & (n <= hi)
        return pl.BlockSpec((bt, D), lambda n, t: (jnp.where(on(n), t, 0), jnp.where(on(n), n - lo, 0)))

    return _pcall(
        body, name=name, out_shape=jax.ShapeDtypeStruct((1, ntile * D, D), MXU_DTYPE), grid=(ntile, nt),
        in_specs=[pl.BlockSpec((bt, D), lambda n, t: (t, 0))] + [seg_spec(m) for m in range(len(segs))],
        out_specs=pl.BlockSpec((None, D, D), lambda n, t: (0, n, 0)),
        scratch=[pltpu.VMEM((D, D), F32)], sem=("parallel", "arbitrary"), vmem_mb=56,
    )(u_all, *segs)


def _du_prenorm_bwd(segs, ddt, wi_main, wi_tail, xin, mods, dres, row_off, tb, name, comm=None):
    n = xin.shape[0]
    nt = n // tb
    off = row_off // tb
    has_dx = dres is not None

    def body(*refs):
        seg_refs = refs[:7]
        ddt_ref, w_ref, wb_ref, wdt_ref, x_ref, mod_ref = refs[7:13]
        rest = refs[13:]
        if has_dx:
            dres_ref, dx_ref, acc_ref, du_scr = rest
        else:
            acc_ref, du_scr = rest
        j, i = pl.program_id(0), pl.program_id(1)
        rows = pl.ds(pl.multiple_of(i * tb, tb), tb)

        @pl.when((i == 0) & (j == 0))
        def _():
            acc_ref[...] = jnp.zeros_like(acc_ref)

        @pl.when(j == 0)
        def _():
            du_scr[rows, :] = _nn(ddt_ref[...], wdt_ref[...])

        for k in range(4):
            if not has_dx and k == 2:
                continue

            @pl.when(j == k)
            def _(k=k):
                if k < 3:
                    sa, sb = seg_refs[2 * k][...], seg_refs[2 * k + 1][...]
                else:
                    sa, sb = seg_refs[6][:, 0:D], seg_refs[6][:, D:2 * D]
                part = _nn(sa, w_ref[0:D, :]) + _nn(sb, w_ref[D:2 * D, :])
                if k > 0:
                    part = part + _nn(sa[:, 0:WTAIL], wb_ref[...])
                du_scr[rows, :] += part

        @pl.when(j == 3)
        def _():
            du = du_scr[rows, :]
            xv = x_ref[...]
            r = lax.rsqrt(jnp.mean(xv * xv, axis=1, keepdims=True) + EPS)
            xh = xv * r
            nw = mod_ref[1:2, :]
            acc_ref[0:1, :] += _colsum(du)
            acc_ref[1:2, :] += _colsum(du * xh * nw)
            dn = du * mod_ref[0:1, :]
            acc_ref[2:3, :] += _colsum(dn * xh)
            if has_dx:
                dxh = dn * nw
                dx_ref[...] = dres_ref[...] + r * (dxh - xh * jnp.mean(dxh * xh, axis=1, keepdims=True))

    def seg_spec(k):
        width = D if k < 6 else 2 * D
        return pl.BlockSpec((tb, width), lambda j, i: (jnp.where(j == min(k // 2, 3), i + off, 0), 0))

    last = pl.BlockSpec((tb, D), lambda j, i: (jnp.where(j == 3, i, 0), 0))
    in_specs = [seg_spec(k) for k in range(7)]
    in_specs += [pl.BlockSpec((tb, 128), lambda j, i: (jnp.where(j == 0, i + off, 0), 0))] + _w_specs()
    in_specs += [last, _full((8, D))]
    args = list(segs) + [ddt, wi_main, wi_tail, wi_tail, xin, mods]
    out_shape = [jax.ShapeDtypeStruct((8, D), F32)]
    out_specs = [_full((8, D))]
    if has_dx:
        in_specs.append(last)
        args.append(dres)
        out_shape.insert(0, jax.ShapeDtypeStruct((n, D), F32))
        out_specs.insert(0, last)
    call = dict(body=body, args=args, name=name, out_shape=tuple(out_shape), grid=(4, nt), in_specs=in_specs,
                out_specs=tuple(out_specs), scratch=[pltpu.VMEM((n, D), F32)], sem=("arbitrary", "arbitrary"),
                vmem_mb=58)
    steps = lambda: ((pl.program_id(0) == 0) & (pl.program_id(1) == 0),
                     (pl.program_id(0) == 3) & (pl.program_id(1) == nt - 1))
    return _run(_carry(call, comm, steps))


def _sum8(v):
    def body(v_ref, o_ref):
        acc = v_ref[0]
        for k in range(1, 8):
            acc = acc + v_ref[k]
        o_ref[...] = acc

    return _pcall(body, name="small_sum", out_shape=jax.ShapeDtypeStruct(v.shape[1:], F32),
                  in_specs=[pl.BlockSpec(memory_space=pltpu.VMEM)], out_specs=pl.BlockSpec(memory_space=pltpu.VMEM))(v)


def _adamw(w, m, v, g, name, comm=None):
    lead = w.ndim == 3
    rows, cols = w.shape[-2:]
    rb = 256 if rows % 256 == 0 else (352 if rows % 352 == 0 else rows)
    c1 = 1.0 - B1 ** STEP
    c2 = 1.0 - B2 ** STEP

    def body(w_ref, m_ref, v_ref, g_ref, d_ref, nm_ref, nv_ref):
        gv = g_ref[...]
        mn = B1 * m_ref[...] + (1.0 - B1) * gv
        vn = B2 * v_ref[...] + (1.0 - B2) * (gv * gv)
        nm_ref[...] = mn
        nv_ref[...] = vn
        d_ref[...] = -LR * ((mn / c1) / (jnp.sqrt(vn / c2) + AEPS) + WD * w_ref[...])

    if rb == rows and rows > 1024:
        cb, steps = 256, cols // 256
        gspec = pl.BlockSpec((rows, cb), lambda i: (0, i))
        spec = pl.BlockSpec((None, rows, cb), lambda i: (0, 0, i)) if lead else gspec
    else:
        steps = rows // rb
        gspec = pl.BlockSpec((rb, cols), lambda i: (i, 0))
        spec = pl.BlockSpec((None, rb, cols), lambda i: (0, i, 0)) if lead else gspec
    call = dict(body=body, args=[w, m, v, g], name=name, out_shape=(jax.ShapeDtypeStruct(w.shape, F32),) * 3,
                grid=(steps,), in_specs=[spec] * 3 + [gspec], out_specs=(spec,) * 3, scratch=[],
                sem=("arbitrary",) if comm is not None else ("parallel",), vmem_mb=40)
    return _run(_carry(call, comm, lambda: (pl.program_id(0) == 0, pl.program_id(0) == steps - 1,
                                            pl.program_id(0) == steps - 1)))


def _rows(v, n):
    f = v.reshape(-1)
    return jnp.pad(f, (0, n * D - f.shape[0])).reshape(n, D)


def kernel(x, c, ctx, c_ctx, w_ada, b_ada, norm_mix, w_in, conv_w, conv_b, ssd_a_log, ssd_dt_bias, ssd_d, ssd_norm, hgrn_lb_raw, hgrn_norm, w_out, norm_ffn, w_gate, w_up, w_down, final_norm, loss_target, m_c_ctx, m_w_ada, m_b_ada, m_norm_mix, m_w_in, m_conv_w, m_conv_b, m_ssd_a_log, m_ssd_dt_bias, m_ssd_d, m_ssd_norm, m_hgrn_lb_raw, m_hgrn_norm, m_w_out, m_norm_ffn, m_w_gate, m_w_up, m_w_down, m_final_norm, v_c_ctx, v_w_ada, v_b_ada, v_norm_mix, v_w_in, v_conv_w, v_conv_b, v_ssd_a_log, v_ssd_dt_bias, v_ssd_d, v_ssd_norm, v_hgrn_lb_raw, v_hgrn_norm, v_w_out, v_norm_ffn, v_w_gate, v_w_up, v_w_down, v_final_norm):
    ix, iy, ic = lax.axis_index("x"), lax.axis_index("y"), lax.axis_index("c")
    chip = 2 * ix + iy
    me = 2 * chip + ic
    xl, xc, tgt = x[0], ctx[0], loss_target[0]
    n_lat, n_ctx = xl.shape[0], xc.shape[0]
    assert n_ctx == TB and n_lat % 1024 == 0
    t_total = n_lat + n_ctx
    nb = t_total // TB

    tr = lambda a: jnp.swapaxes(a, -1, -2)
    shift = [functools.partial(jnp.pad, pad_width=((8 * k, WSL + WTAIL - NSH - 8 * k), (0, 0))) for k in range(4)]
    slab = lax.switch(chip, shift, tr(w_in[0]).astype(MXU_DTYPE))
    padrows = lambda a: jnp.pad(a, ((0, FSL - DFF // 4), (0, 0))).astype(MXU_DTYPE)
    shards = [slab[:WSL], slab[WSL:], w_out[0].astype(MXU_DTYPE), padrows(tr(w_gate[0])), padrows(tr(w_up[0])),
              padrows(w_down[0])]
    own = lambda g_, s_: lax.dynamic_update_slice(g_, s_[None], (chip, 0, 0))
    pack = jnp.concatenate([c, hgrn_lb_raw.reshape(1, D), _rows(conv_w[0], 3), jnp.zeros((3, D), F32)], axis=0)
    ncol_ada = w_ada.shape[2]
    b_shard = lax.dynamic_slice(b_ada, (0, chip * ncol_ada), (1, ncol_ada))
    gath, araw, mod_all, wi_main, wi_tail = _prologue(pack, c_ctx.reshape(1, D), w_ada[0], b_shard, shards[:2])
    wi_main, wi_tail = own(wi_main, shards[0]), own(wi_tail, shards[1])
    gath = gath.reshape(8, 8, D)
    lbraw_full = gath[0::2, 1].reshape(4, 2, 2, 256).transpose(1, 2, 0, 3).reshape(4, D)
    convw_full = gath[0::2, 2:5].reshape(4, 3 * D)[:, :KCONV * 512].reshape(4, KCONV, 512).transpose(1, 0, 2)
    convw_full = convw_full.reshape(KCONV, 2048)
    lbraw8 = jnp.pad(lbraw_full, ((0, 4), (0, 0)))
    convp = jnp.concatenate([convw_full, conv_b, jnp.zeros((2, 2048), F32)], axis=0)
    dtb = jnp.pad(ssd_dt_bias.reshape(1, 32), ((0, 7), (0, 96)))
    alog = jnp.pad(ssd_a_log.reshape(1, 32), ((0, 7), (0, 96)))
    mod_all = mod_all.reshape(8, 16, ncol_ada)[0::2]
    mod_full = mod_all.transpose(1, 0, 2).reshape(16, 4 * ncol_ada)
    my_mod = lax.dynamic_slice(mod_full, (me, 0), (1, 6 * D)).reshape(6, D)
    sh1, sc1, g1, sh2, sc2, g2 = (my_mod[k:k + 1] for k in range(6))
    csh1, csc1 = mod_full[8:9, 0:D], mod_full[8:9, D:2 * D]

    zrow = jnp.zeros((1, D), F32)
    mods_lat = jnp.concatenate([1.0 + sc1, sh1, norm_mix, zrow, zrow, zrow, zrow, zrow], axis=0)
    mods_ctx = jnp.concatenate([1.0 + csc1, csh1, norm_mix, zrow, zrow, zrow, zrow, zrow], axis=0)
    outs = _inproj(xl, mods_lat, wi_main, wi_tail, t_total, 1024, 0, None, "inproj_lat",
                   comm=_comm_gather(shards[2:3]))
    w_out_f = own(outs[3], shards[2]).reshape(2 * D, D)
    p_main, p_dt, u_all = _inproj(xc, mods_ctx, wi_main, wi_tail, t_total, TB, nb - 1, outs[:3], "inproj_ctx")

    o_f, hs_f, wd_g = _hgrn_fwd(p_main, lbraw8, 0, nb, comm=_comm_gather(shards[5:]))
    w_down_f = own(wd_g, shards[5]).reshape(DFFP, D)
    o_b, hs_b, wu_g = _hgrn_fwd(p_main, lbraw8, 1, nb, comm=_comm_gather(shards[4:5]))
    wu_g = own(wu_g, shards[4])
    xa, dsl, dts = _ssd_prep(p_main, p_dt, convp, dtb, nb)
    y_f, ss_f, wg_g = _ssd_fwd(xa, dts, alog, 0, nb, comm=_comm_gather(shards[3:4]))
    wg_g = own(wg_g, shards[3])
    y_b, ss_b = _ssd_fwd(xa, dts, alog, 1, nb)

    vec_mix = jnp.concatenate([jnp.tile(hgrn_norm, (1, NH)), jnp.repeat(ssd_d, SP, axis=1), ssd_norm, g1, 1.0 + sc2,
                               sh2, norm_ffn, zrow], axis=0)
    ymix, ylat, h1, u2 = _mix_out(o_f, o_b, p_main, y_f, y_b, xa, xl, vec_mix, w_out_f)
    gate, up, act = _ffn_up(u2, wg_g, wu_g)
    vec_loss = jnp.concatenate([g2, final_norm.reshape(1, D)] + [zrow] * 6, axis=0)
    dh2, dffn, acc_loss = _ffn_down_loss(act, w_down_f, h1, tgt, vec_loss)

    core_arr = jnp.reshape(ic, (1,)).astype(jnp.int32)
    chip_arr = jnp.reshape(chip, (1,)).astype(jnp.int32)
    every = (0, 4)

    def pair_sum(gs, got, tag):
        return list(_pair_sum(gs, list(got), core_arr, "grads_pair_sum_" + tag))

    vec_ffn = jnp.concatenate([g1, 1.0 + sc2, norm_ffn] + [zrow] * 5, axis=0)
    dgate, dup, du2 = _ffn_bwd(dffn, w_down_f, gate, up, wg_g, wu_g)
    dh1, dylat, acc_ffn = _ffn_norm_bwd(du2, h1, ylat, dh2, vec_ffn)
    gw_down = _dw(act, dffn, "dw_down").reshape(4, FSL, D)
    ga1 = [_dw(dgate, u2, "dw_gate").reshape(4, FSL, D), _dw(dup, u2, "dw_up").reshape(4, FSL, D)]
    res = _mix_bwd(dylat, o_f, o_b, p_main, y_f, y_b, xa, vec_mix, w_out_f, comm=_comm_pair(ga1))
    (do, dgr, dys, dzr, dxs_skip, acc_mix), pair_a1 = res[:6], pair_sum(ga1, res[6:], "a1")
    ga2 = [gw_down, _dw(ymix, dylat, "dw_out").reshape(4, D // 2, D)]

    res = _hgrn_bwd(p_main, lbraw8, hs_f, do, 0, nb, None,
                    comm=[_comm_exchange(pair_a1, [every] * 2), _comm_pair(ga2)])
    (dq0, dff, dv0, dlb_f), recv_a, pair_a2 = res[:4], list(res[4:6]), pair_sum(ga2, res[6:], "a2")
    res = _hgrn_bwd(p_main, lbraw8, hs_b, do, 1, nb, (dq0, dv0), comm=_comm_exchange(pair_a2, [every] * 2))
    (dq, dfb, dv, dlb_b), recv_a = res[:4], recv_a + list(res[4:])
    pair_a, dests_a = pair_a1 + pair_a2, [every] * 4
    gw_in = [_dw_in([dq, dff], u_all, "dw_in_0"), _dw_in([dfb, dv], u_all, "dw_in_1"),
             _dw_in([dgr, dzr], u_all, "dw_in_2")]

    res = _ssd_bwd(xa, dts, alog, ss_f, dys, 0, nb, None, comm=_comm_pair(gw_in))
    (dxa0, ddts0, da_f), pair_b, dests_b = res[:3], pair_sum(gw_in, res[3:], "b"), [(0, 1), (1, 2), (2, 3)]
    res = _ssd_bwd(xa, dts, alog, ss_b, dys, 1, nb, (dxa0, ddts0), comm=_comm_exchange(pair_b, dests_b))
    (dxa, ddts, da_b), recv_b = res[:3], list(res[3:])
    dxbc, ddt, acc_conv, acc_dtb = _ssd_prep_bwd(p_main, p_dt, convp, dtb, dsl, dxa, dxs_skip, ddts, nb)
    gw_in.append(_dw_in([dxbc], u_all, "dw_in_3"))
    gw_in_dt = _dw(ddt, u_all, "dw_in_dt")
    gc = [gw_in[3], jnp.concatenate([g_[:, 0:WTAIL, :] for g_ in gw_in[1:]] + [gw_in_dt[None]], axis=0)]

    segs = [dq, dff, dfb, dv, dgr, dzr, dxbc]
    bmods_lat = jnp.concatenate([1.0 + sc1, norm_mix] + [zrow] * 6, axis=0)
    bmods_ctx = jnp.concatenate([1.0 + csc1, norm_mix] + [zrow] * 6, axis=0)
    res = _du_prenorm_bwd(segs, ddt, wi_main, wi_tail, xc, bmods_ctx, None, n_lat, TB, "du_ctx", comm=_comm_pair(gc))
    acc_ctx, pair_c, dests_c = res[0], pair_sum(gc, res[1:], "c"), [(3, 4), every]
    res = _du_prenorm_bwd(segs, ddt, wi_main, wi_tail, xl, bmods_lat, dh1, 0, 512, "du_lat",
                          comm=_comm_exchange(pair_c, dests_c))
    (grad_x, acc_lat), recv_c = res[:2], list(res[2:])

    mine = _chip_sum(pair_b + pair_c + pair_a, recv_b + recv_c + recv_a, chip_arr, dests_b + dests_c + dests_a,
                     [0, 0, 0, 0, 1, 3, 4, 5, 2])
    dmod_lat = jnp.concatenate([acc_lat[0:2], acc_ffn[3:4], acc_ffn[0:2], acc_loss[0:1]], axis=0)
    misc = jnp.concatenate([(da_f + da_b)[0, :32], jnp.zeros((96,), F32), acc_dtb[0, :32], jnp.zeros((96,), F32),
                            jnp.sum(acc_loss[2]).reshape(1), jnp.zeros((D - 257,), F32)]).reshape(1, D)
    sv = jnp.concatenate([
        dmod_lat, acc_ctx[0:2], (acc_lat[2:3] + acc_ctx[2:3]), acc_ffn[2:3], acc_loss[1:2], acc_mix[2:3],
        acc_mix[0:1], acc_mix[1:2], dlb_f[0:1], dlb_b[0:1], acc_conv[0:6].reshape(12, D), misc,
        jnp.zeros((3, D), F32)], axis=0)
    res = _pair_swap(mine, sv)
    theirs, sv_all = res[:-1], res[-1].reshape(8, 32, D)
    whole = [jnp.concatenate([jnp.where(ic == 0, m_, t_), jnp.where(ic == 0, t_, m_)], axis=0)
             for m_, t_ in zip(mine, theirs)]
    g_w_in = lax.dynamic_slice(jnp.concatenate(whole[0:2], axis=0), (8 * chip, 0), (NSH, D))
    g_w_out = whole[2]
    g_w_gate = whole[3][:DFF // 4]
    g_w_up = whole[4][:DFF // 4]
    g_w_down = whole[5][:DFF // 4]
    ssum = _sum8(sv_all)
    dmod_rows = sv_all[:, 0:6].reshape(8, 6 * D)
    dmod_ctx_row = jnp.concatenate([ssum[6:8].reshape(1, 2 * D), jnp.zeros((1, 4 * D), F32)], axis=1)
    dmod_full = jnp.concatenate([dmod_rows, dmod_ctx_row, jnp.zeros((7, 6 * D), F32)], axis=0)
    grad_b_ada = jnp.sum(dmod_full, axis=0, keepdims=True)
    dmod_shard = lax.dynamic_slice(dmod_full, (0, chip * ncol_ada), (16, ncol_ada))
    g_w_ada, da_part = _ada_bwd(araw, dmod_shard, w_ada[0])
    da_all = _allgather8(da_part, "ada_ctx_gather").reshape(8, 16, D)[0::2, 8]

    big = {}
    for nm, w_, m_, v_, g_ in (("w_ada", w_ada, m_w_ada, v_w_ada, g_w_ada), ("w_in", w_in, m_w_in, v_w_in, g_w_in),
                               ("w_out", w_out, m_w_out, v_w_out, g_w_out),
                               ("w_gate", w_gate, m_w_gate, v_w_gate, g_w_gate),
                               ("w_up", w_up, m_w_up, v_w_up, g_w_up),
                               ("w_down", w_down, m_w_down, v_w_down, g_w_down)):
        if nm in ("w_in", "w_gate", "w_up"):
            big[nm] = tuple(tr(t) for t in (g_[None],) + tuple(_adamw(tr(w_), tr(m_), tr(v_), g_, "adamw_" + nm)))
        else:
            big[nm] = (g_[None],) + tuple(_adamw(w_, m_, v_, g_, "adamw_" + nm))
    cc = c_ctx.reshape(1, D)
    grad_c_ctx = (jnp.sum(da_all, axis=0, keepdims=True) * _dsilu(cc)).reshape(D)

    grad_norm_mix, grad_norm_ffn, grad_final_norm = ssum[8:9], ssum[9:10], ssum[10].reshape(D)
    grad_ssd_norm = ssum[11:12]
    grad_hgrn_norm = jnp.sum(ssum[12].reshape(NH, HF), axis=0, keepdims=True)
    grad_ssd_d = jnp.sum(ssum[13].reshape(SHEADS, SP), axis=1).reshape(1, SHEADS)
    lb_full = _sig(lbraw_full[0:2] - lbraw_full[2:4])
    dr0 = ssum[14:16] * lb_full * (1.0 - lb_full)
    grad_lb_full = jnp.stack([dr0, -dr0], axis=0)
    grad_lb = lax.dynamic_slice(grad_lb_full, (0, 0, chip * 256), (2, 2, 256))
    grad_conv_w = lax.dynamic_slice(ssum[16:26].reshape(KCONV, 2048), (0, chip * 512), (KCONV, 512)).reshape(1, KCONV, 512)
    grad_conv_b = ssum[26:28].reshape(1, 2048)
    a_val = -jnp.exp(ssd_a_log)
    grad_a_log = ssum[28, 0:32].reshape(1, 2, SHEADS) * a_val
    grad_dt_bias = ssum[28, 128:160].reshape(1, 2, SHEADS)
    loss = ssum[28, 256]

    small_w = [c_ctx, b_ada, norm_mix, conv_w, conv_b, ssd_a_log, ssd_dt_bias, ssd_d, ssd_norm, hgrn_lb_raw,
               hgrn_norm, norm_ffn, final_norm]
    small_m = [m_c_ctx, m_b_ada, m_norm_mix, m_conv_w, m_conv_b, m_ssd_a_log, m_ssd_dt_bias, m_ssd_d, m_ssd_norm,
               m_hgrn_lb_raw, m_hgrn_norm, m_norm_ffn, m_final_norm]
    small_v = [v_c_ctx, v_b_ada, v_norm_mix, v_conv_w, v_conv_b, v_ssd_a_log, v_ssd_dt_bias, v_ssd_d, v_ssd_norm,
               v_hgrn_lb_raw, v_hgrn_norm, v_norm_ffn, v_final_norm]
    small_g = [grad_c_ctx, grad_b_ada, grad_norm_mix, grad_conv_w, grad_conv_b, grad_a_log, grad_dt_bias, grad_ssd_d,
               grad_ssd_norm, grad_lb, grad_hgrn_norm, grad_norm_ffn, grad_final_norm]
    nrows = [-(-a.size // D) for a in small_w]
    packs = lambda lst: jnp.concatenate([_rows(a, r) for a, r in zip(lst, nrows)]
                                        + [jnp.zeros((24 - sum(nrows), D), F32)], axis=0)
    sd, sm, svv = _adamw(packs(small_w), packs(small_m), packs(small_v), packs(small_g), "adamw_small")

    def unpack(p):
        out, r0 = [], 0
        for a, r in zip(small_w, nrows):
            out.append(p[r0:r0 + r].reshape(-1)[:a.size].reshape(a.shape))
            r0 += r
        return out

    sd, sm, svv = unpack(sd), unpack(sm), unpack(svv)

    order = ["c_ctx", "w_ada", "b_ada", "norm_mix", "w_in", "conv_w", "conv_b", "ssd_a_log", "ssd_dt_bias", "ssd_d",
             "ssd_norm", "hgrn_lb_raw", "hgrn_norm", "w_out", "norm_ffn", "w_gate", "w_up", "w_down", "final_norm"]
    small_names = ["c_ctx", "b_ada", "norm_mix", "conv_w", "conv_b", "ssd_a_log", "ssd_dt_bias", "ssd_d", "ssd_norm",
                   "hgrn_lb_raw", "hgrn_norm", "norm_ffn", "final_norm"]
    table = dict(big)
    for k, nm in enumerate(small_names):
        table[nm] = (small_g[k].reshape(small_w[k].shape), sd[k], sm[k], svv[k])
    grads = [table[nm][0] for nm in order]
    deltas = [table[nm][1] for nm in order]
    new_m = [table[nm][2] for nm in order]
    new_v = [table[nm][3] for nm in order]
    return (loss, grad_x[None], *grads, *deltas, *new_m, *new_v)
```

```python
import functools
import math

import jax
import jax.numpy as jnp
from jax import lax
from jax.experimental import pallas as pl
from jax.experimental.pallas import tpu as pltpu

F32 = jnp.float32
BF16 = jnp.bfloat16
MXU_DTYPE = jnp.bfloat16
_INTERPRET = False

D = 1024
NH, HF = 8, 128
HC = 64
SC = 128
SN = 128
SHEADS, SP = 16, 64
GRID_W = 64
KCONV = 5
DFF = 2816
FSL = 768
DFFP = 4 * FSL
NIN = 8224
TB = 256
EPS = 1e-6
LR, B1, B2, AEPS, WD, STEP = 0.001, 0.9, 0.999, 1e-08, 0.01, 10
MESH_ID = pl.DeviceIdType.MESH
NSH = NIN // 4
WSL = 2048
WTAIL = 128


def _pcall(body, *, name, out_shape, grid=(), in_specs=None, out_specs=None, scratch=(), sem=None,
           vmem_mb=None, aliases=None):
    params = {}
    if sem is not None:
        params["dimension_semantics"] = sem
    if vmem_mb is not None:
        params["vmem_limit_bytes"] = vmem_mb << 20
    kw = dict(name=name, out_shape=out_shape, scratch_shapes=list(scratch),
              input_output_aliases=aliases or {}, compiler_params=pltpu.CompilerParams(**params),
              interpret=_INTERPRET)
    if grid:
        kw["grid"] = grid
    if in_specs is not None:
        kw["in_specs"] = in_specs
    if out_specs is not None:
        kw["out_specs"] = out_specs
    return pl.pallas_call(body, **kw)


def _mx(a):
    return a.astype(MXU_DTYPE)


def _dg(a, b, ca, cb):
    return lax.dot_general(_mx(a), _mx(b), (((ca,), (cb,)), ((), ())), preferred_element_type=F32)


def _nn(a, b):
    return _dg(a, b, 1, 0)


def _nt(a, b):
    return _dg(a, b, 1, 1)


def _tn(a, b):
    return _dg(a, b, 0, 0)


def _dot01(m, x, ways=3):
    f = lambda t: lax.dot_general(m, t, (((1,), (0,)), ((), ())), preferred_element_type=F32)
    hi = x.astype(BF16)
    r1 = x - hi.astype(F32)
    mid = r1.astype(BF16)
    if ways == 2:
        return f(hi) + f(mid)
    lo = (r1 - mid.astype(F32)).astype(BF16)
    return f(hi) + f(mid) + f(lo)


def _tri(n, upper):
    r = lax.broadcasted_iota(jnp.int32, (n, n), 0)
    c = lax.broadcasted_iota(jnp.int32, (n, n), 1)
    return (c >= r) if upper else (c <= r)


def _b01(mask):
    return jnp.where(mask, 1.0, 0.0).astype(BF16)


def _sig(x):
    return jax.nn.sigmoid(x)


def _silu(x):
    return x * _sig(x)


def _dsilu(x):
    s = _sig(x)
    return s * (1.0 + x * (1.0 - s))


def _softplus(x):
    return jnp.maximum(x, 0.0) + jnp.log(1.0 + jnp.exp(-jnp.abs(x)))


def _rowsum(x):
    return jnp.sum(x, axis=1, keepdims=True)


def _colsum(x):
    return jnp.sum(x, axis=0, keepdims=True)


def _full(shape):
    return pl.BlockSpec(shape, lambda *_: (0,) * len(shape))


def _allgather8_phases(x_ref, out_ref, send_sems, recv_sems, local_sem):
    m_per = x_ref.shape[0]
    x, y, c = lax.axis_index("x"), lax.axis_index("y"), lax.axis_index("c")
    me, sibling = (x, y, c), (x, y, 1 - c)
    chips = [(1 - x, y), (x, 1 - y), (1 - x, 1 - y)]

    def rows(px, py, pc):
        return out_ref.at[pl.ds((4 * px + 2 * py + pc) * m_per, m_per), :]

    def copy(k, block, to, src=None):
        return pltpu.make_async_remote_copy(
            src_ref=rows(*block) if src is None else src, dst_ref=rows(*block),
            send_sem=send_sems.at[k], recv_sem=recv_sems.at[k], device_id=to, device_id_type=MESH_ID)

    mine = pltpu.make_async_copy(x_ref, rows(*me), local_sem)
    first = [copy(0, me, sibling, src=x_ref)]
    first += [copy(1 + j, me, (*chip, c), src=x_ref) for j, chip in enumerate(chips)]
    passed = [copy(4 + j, (*chip, c), sibling) for j, chip in enumerate(chips)]

    def start():
        mine.start()
        for cp in first:
            cp.start()

    def forward():
        for j, chip in enumerate(chips):
            copy(1 + j, (*chip, c), me).wait_recv()
            passed[j].start()

    def finish():
        copy(0, sibling, me).wait_recv()
        for j, chip in enumerate(chips):
            copy(4 + j, (*chip, 1 - c), me).wait_recv()
        for cp in first + passed:
            cp.wait_send()
        mine.wait()

    return start, forward, finish


def _allgather8_ops(x_ref, out_ref, send_sems, recv_sems, local_sem):
    for phase in _allgather8_phases(x_ref, out_ref, send_sems, recv_sems, local_sem):
        phase()


def _allgather8(v, name):
    m_per, n = v.shape
    return _pcall(
        functools.partial(_allgather8_ops), name=name, out_shape=jax.ShapeDtypeStruct((8 * m_per, n), v.dtype),
        in_specs=[pl.BlockSpec(memory_space=pltpu.VMEM)], out_specs=pl.BlockSpec(memory_space=pltpu.VMEM),
        scratch=list(_AG8_SEMS),
    )(v)


_AG8_SEMS = [pltpu.SemaphoreType.DMA((7,)), pltpu.SemaphoreType.DMA((7,)), pltpu.SemaphoreType.DMA]


def _prologue(pack, cc_row, w_ada, b_shard, shards):
    n = len(shards)
    ncol = w_ada.shape[1]

    def body(pack_ref, cc_ref, w_ref, b_ref, *refs):
        ins = refs[:n]
        gath_ref, araw_ref, mod_ref = refs[n:n + 3]
        outs = refs[n + 3:2 * n + 3]
        modsh, s1, r1, l1, s2, r2, l2, gs, gr = refs[2 * n + 3:]
        start, forward, finish = _gather_ops(ins, outs, gs, gr, relay=True)
        start()
        _allgather8_ops(pack_ref, gath_ref, s1, r1, l1)
        a = jnp.concatenate([gath_ref[8 * i:8 * i + 1, :] for i in range(8)] + [cc_ref[...], jnp.zeros((7, D), F32)],
                            axis=0)
        araw_ref[...] = a
        modsh[...] = _nn(_silu(a), w_ref[...]) + b_ref[...]
        _allgather8_ops(modsh, mod_ref, s2, r2, l2)
        forward()
        finish()

    vm = pl.BlockSpec(memory_space=pltpu.VMEM)
    anyspec = pl.BlockSpec(memory_space=pl.ANY)
    return _pcall(
        body, name="prologue",
        out_shape=(jax.ShapeDtypeStruct((64, D), F32), jax.ShapeDtypeStruct((16, D), F32),
                   jax.ShapeDtypeStruct((128, ncol), F32)) + _gather_out(shards),
        in_specs=[vm, vm, vm, vm] + [anyspec] * n, out_specs=(vm, vm, vm) + (anyspec,) * n,
        scratch=[pltpu.VMEM((16, ncol), F32)] + list(_AG8_SEMS) + list(_AG8_SEMS) + _gather_sems(n), vmem_mb=40,
    )(pack, cc_row, w_ada, b_shard, *shards)


def _gather_ops(ins, outs, send_sems, recv_sems, relay=False):
    n = len(ins)
    x, y, c = lax.axis_index("x"), lax.axis_index("y"), lax.axis_index("c")
    me, sibling = (x, y, c), (x, y, 1 - c)
    chips = [(1 - x, y), (x, 1 - y), (1 - x, 1 - y)]
    direct = 2 if relay else 3

    def part(a, px, py, pc, quarter=None):
        half = ins[a].shape[0] // 2
        if quarter is None:
            return outs[a].at[2 * px + py, pl.ds(pc * half, half), :]
        return outs[a].at[2 * px + py, pl.ds(pc * half + quarter * (half // 2), half // 2), :]

    def copy(a, k, block, to, src=None, quarter=None):
        return pltpu.make_async_remote_copy(
            src_ref=part(a, *block, quarter) if src is None else src, dst_ref=part(a, *block, quarter),
            send_sem=send_sems.at[8 * a + k], recv_sem=recv_sems.at[8 * a + k], device_id=to,
            device_id_type=MESH_ID)

    def first(a, j):
        half = ins[a].shape[0] // 2
        return copy(a, j, me, (*chips[j], c), src=ins[a].at[pl.ds(c * half, half), :])

    relayed = lambda a, q: copy(a, 6 + q, (*chips[q], c), (*chips[1 - q], c), quarter=q)

    def start():
        for a in range(n):
            for j in range(direct):
                first(a, j).start()

    def forward():
        for a in range(n):
            for j in range(direct):
                copy(a, j, (*chips[j], c), me).wait_recv()
                copy(a, 3 + j, (*chips[j], c), sibling).start()
                if relay:
                    relayed(a, j).start()
            if relay:
                for q in range(2):
                    copy(a, 6 + q, (*chips[2], c), me, quarter=q).wait_recv()
                copy(a, 5, (*chips[2], c), sibling).start()

    def finish():
        for a in range(n):
            for j, chip in enumerate(chips):
                copy(a, 3 + j, (*chip, 1 - c), me).wait_recv()
        for a in range(n):
            for j, chip in enumerate(chips):
                if j < direct:
                    first(a, j).wait_send()
                    if relay:
                        relayed(a, j).wait_send()
                copy(a, 3 + j, (*chip, c), sibling).wait_send()

    return start, forward, finish


def _gather_out(shards):
    return tuple(jax.ShapeDtypeStruct((4,) + s_.shape, s_.dtype) for s_ in shards)


def _gather_sems(n):
    return [pltpu.SemaphoreType.DMA((8 * n,)), pltpu.SemaphoreType.DMA((8 * n,))]


def _pair_ops(ins, outs, send_sems, recv_sems):
    x, y, c = lax.axis_index("x"), lax.axis_index("y"), lax.axis_index("c")
    cps = []
    for a in range(len(ins)):
        half = ins[a].shape[1] // 2
        cps.append(pltpu.make_async_remote_copy(
            src_ref=ins[a].at[:, pl.ds((1 - c) * half, half), :], dst_ref=outs[a], send_sem=send_sems.at[a],
            recv_sem=recv_sems.at[a], device_id=(x, y, 1 - c), device_id_type=MESH_ID))

    def start():
        for cp in cps:
            cp.start()

    def finish():
        for cp in cps:
            cp.wait()

    return start, finish


def _comm_pair(gs):
    n = len(gs)
    return (list(gs), tuple(jax.ShapeDtypeStruct((g.shape[0], g.shape[1] // 2, g.shape[2]), g.dtype) for g in gs),
            [pltpu.SemaphoreType.DMA((n,)), pltpu.SemaphoreType.DMA((n,))], _pair_ops)


def _exchange_ops(ins, outs, send_sems, recv_sems, dests):
    x, y, c = lax.axis_index("x"), lax.axis_index("y"), lax.axis_index("c")
    mine = 2 * x + y
    chips = [(1 - x, y), (x, 1 - y), (1 - x, 1 - y)]

    def each(fn):
        for a in range(len(ins)):
            lo, hi = dests[a]
            for j, (px, py) in enumerate(chips):
                q = 2 * px + py
                cp = pltpu.make_async_remote_copy(
                    src_ref=ins[a].at[jnp.clip(q - lo, 0, hi - lo - 1)], dst_ref=outs[a].at[j],
                    send_sem=send_sems.at[3 * a + j], recv_sem=recv_sems.at[3 * a + j], device_id=(px, py, c),
                    device_id_type=MESH_ID)
                fn(cp, (q >= lo) & (q < hi), (mine >= lo) & (mine < hi), (lo, hi) == (0, 4))

    def start():
        def go(cp, send_ok, recv_ok, always):
            if always:
                cp.start()
            else:
                pl.when(send_ok)(cp.start)
        each(go)

    def finish():
        def go(cp, send_ok, recv_ok, always):
            if always:
                cp.wait()
            else:
                pl.when(send_ok)(cp.wait_send)
                pl.when(recv_ok)(cp.wait_recv)
        each(go)

    return start, finish


def _comm_exchange(hs, dests):
    n = len(hs)
    return (list(hs), tuple(jax.ShapeDtypeStruct((3,) + h.shape[1:], h.dtype) for h in hs),
            [pltpu.SemaphoreType.DMA((3 * n,)), pltpu.SemaphoreType.DMA((3 * n,))],
            lambda i, o, s, r: _exchange_ops(i, o, s, r, dests))


def _comm_gather(shards, relay=False):
    return (list(shards), _gather_out(shards), _gather_sems(len(shards)),
            lambda i, o, s, r: _gather_ops(i, o, s, r, relay))


def _carry(call, comm, steps):
    if comm is None:
        return call
    if isinstance(comm, list):
        for one in comm:
            call = _carry(call, one, steps)
        return call
    arrays, out_shape, sems, make = comm
    n, n_in, n_out = len(arrays), len(call["args"]), len(call["out_shape"])
    body = call["body"]

    def wrapped(*refs):
        base_in, cin = refs[:n_in], refs[n_in:n_in + n]
        rest = refs[n_in + n:]
        base_out, cout, scr = rest[:n_out], rest[n_out:n_out + n], rest[n_out + n:]
        ops = make(cin, cout, scr[-2], scr[-1])
        when = steps()
        pl.when(when[0])(ops[0])
        if len(ops) == 3 and len(when) == 3:
            pl.when(when[2])(ops[1])
        body(*base_in, *base_out, *scr[:-2])
        if len(ops) == 3 and len(when) == 2:
            pl.when(when[1])(ops[1])
        pl.when(when[1])(ops[-1])

    anyspec = pl.BlockSpec(memory_space=pl.ANY)
    return dict(call, body=wrapped, args=list(call["args"]) + arrays,
                in_specs=list(call["in_specs"]) + [anyspec] * n,
                out_shape=tuple(call["out_shape"]) + tuple(out_shape),
                out_specs=tuple(call["out_specs"]) + (anyspec,) * n,
                scratch=list(call["scratch"]) + sems)


def _run(call):
    args = call.pop("args")
    body = call.pop("body")
    return _pcall(body, **call)(*args)


def _pair_swap(rs, sv):
    n = len(rs)

    def body(sv_ref, *refs):
        ins, outs, got_ref = refs[:n], refs[n:2 * n], refs[2 * n]
        send_sems, recv_sems, s1, r1, l1 = refs[2 * n + 1:]
        x, y, c = lax.axis_index("x"), lax.axis_index("y"), lax.axis_index("c")
        cps = [pltpu.make_async_remote_copy(
            src_ref=ins[a], dst_ref=outs[a], send_sem=send_sems.at[a], recv_sem=recv_sems.at[a],
            device_id=(x, y, 1 - c), device_id_type=MESH_ID) for a in range(n)]
        for cp in cps:
            cp.start()
        _allgather8_ops(sv_ref, got_ref, s1, r1, l1)
        for cp in cps:
            cp.wait()

    vm, anyspec = pl.BlockSpec(memory_space=pltpu.VMEM), pl.BlockSpec(memory_space=pl.ANY)
    return _pcall(
        body, name="grads_pair_swap",
        out_shape=tuple(jax.ShapeDtypeStruct(r.shape, r.dtype) for r in rs)
        + (jax.ShapeDtypeStruct((8 * sv.shape[0], sv.shape[1]), sv.dtype),),
        in_specs=[vm] + [anyspec] * n, out_specs=(anyspec,) * n + (vm,),
        scratch=[pltpu.SemaphoreType.DMA((n,)), pltpu.SemaphoreType.DMA((n,))] + list(_AG8_SEMS),
    )(sv, *rs)


SUM_STEPS = 4


def _pair_sum(gs, recvs, core, name):
    n = len(gs)

    def body(c_ref, *refs):
        for a in range(n):
            refs[2 * n + a][...] = (refs[a][...].astype(F32) + refs[n + a][...].astype(F32)).astype(refs[2 * n + a].dtype)

    blk = lambda g: (g.shape[0], g.shape[1] // (2 * SUM_STEPS), g.shape[2])
    return pl.pallas_call(
        body, name=name,
        out_shape=tuple(jax.ShapeDtypeStruct((g.shape[0], g.shape[1] // 2, g.shape[2]), g.dtype) for g in gs),
        grid_spec=pltpu.PrefetchScalarGridSpec(
            num_scalar_prefetch=1, grid=(SUM_STEPS,),
            in_specs=[pl.BlockSpec(blk(g), lambda i, cr: (0, cr[0] * SUM_STEPS + i, 0)) for g in gs]
            + [pl.BlockSpec(blk(g), lambda i, cr: (0, i, 0)) for g in gs],
            out_specs=tuple(pl.BlockSpec(blk(g), lambda i, cr: (0, i, 0)) for g in gs)),
        compiler_params=pltpu.CompilerParams(vmem_limit_bytes=40 << 20), interpret=_INTERPRET,
    )(core, *gs, *recvs)


def _chip_sum(hs, recvs, chip, dests, slots):
    n = len(hs)
    nout = max(slots) + 1
    first = [slots.index(o) for o in range(nout)]
    every = lambda d_: d_ == (0, 4)

    def own(d_):
        if every(d_):
            return lambda i, kr: (kr[0], i, 0)
        return lambda i, kr: (0, jnp.where(kr[0] == d_[0], i, 0), 0)

    def got(d_):
        if every(d_):
            return lambda i, kr: (0, i, 0)
        return lambda i, kr: (0, jnp.where(kr[0] == d_[0], i, 0), 0)

    def body(k_ref, *refs):
        for a in range(n):
            def emit(a=a):
                acc = refs[a][0].astype(F32)
                for j in range(3):
                    acc = acc + refs[n + a][j].astype(F32)
                refs[2 * n + slots[a]][...] = acc
            if every(dests[a]):
                emit()
            else:
                pl.when(k_ref[0] == dests[a][0])(emit)

    rb = lambda h: h.shape[1] // SUM_STEPS
    return pl.pallas_call(
        body, name="grads_chip_sum",
        out_shape=tuple(jax.ShapeDtypeStruct(hs[a].shape[1:], F32) for a in first),
        grid_spec=pltpu.PrefetchScalarGridSpec(
            num_scalar_prefetch=1, grid=(SUM_STEPS,),
            in_specs=[pl.BlockSpec((1, rb(h), h.shape[2]), own(d_)) for h, d_ in zip(hs, dests)]
            + [pl.BlockSpec((3, rb(h), h.shape[2]), got(d_)) for h, d_ in zip(hs, dests)],
            out_specs=tuple(pl.BlockSpec((rb(hs[a]), hs[a].shape[2]), lambda i, kr: (i, 0)) for a in first)),
        compiler_params=pltpu.CompilerParams(vmem_limit_bytes=40 << 20), interpret=_INTERPRET,
    )(chip, *hs, *recvs)


def _ada_bwd(araw, dmod, w):
    nblk = w.shape[1] // 512

    def body(a_ref, d_ref, w_ref, gw_ref, da_ref):
        j = pl.program_id(0)
        gw_ref[...] = _tn(_silu(a_ref[...]), d_ref[...])
        part = _nt(d_ref[...], w_ref[...])

        @pl.when(j == 0)
        def _():
            da_ref[...] = part

        @pl.when(j > 0)
        def _():
            da_ref[...] += part

    return _pcall(
        body, name="ada_bwd",
        out_shape=(jax.ShapeDtypeStruct(w.shape, F32), jax.ShapeDtypeStruct((16, D), F32)), grid=(nblk,),
        in_specs=[_full((16, D)), pl.BlockSpec((16, 512), lambda j: (0, j)), pl.BlockSpec((D, 512), lambda j: (0, j))],
        out_specs=(pl.BlockSpec((D, 512), lambda j: (0, j)), _full((16, D))), sem=("arbitrary",),
    )(araw, dmod, w)


def _w_specs():
    return [pl.BlockSpec((None, 2 * D, D), lambda j, i: (j, 0, 0)),
            pl.BlockSpec((None, WTAIL, D), lambda j, i: (jnp.maximum(j - 1, 0), 0, 0)),
            pl.BlockSpec((None, WTAIL, D), lambda j, i: (3, 0, 0))]


def _inproj(xin, mods, wi_main, wi_tail, t_total, tb, blk_off, prev, name, comm=None):
    n = xin.shape[0]
    nt = n // tb
    nslab = 4

    def body(x_ref, mod_ref, w_ref, wb_ref, wdt_ref, *rest):
        p_ref, pdt_ref, u_ref, uscr = rest[-4:]
        j, i = pl.program_id(0), pl.program_id(1)
        rows = pl.ds(pl.multiple_of(i * tb, tb), tb)

        @pl.when(j == 0)
        def _():
            xv = x_ref[...]
            r = lax.rsqrt(jnp.mean(xv * xv, axis=1, keepdims=True) + EPS)
            u = (xv * r * mod_ref[2:3, :]) * mod_ref[0:1, :] + mod_ref[1:2, :]
            ub = u.astype(MXU_DTYPE)
            uscr[rows, :] = ub
            u_ref[...] = ub
            pdt_ref[...] = _nt(ub, wdt_ref[...])

        ub = uscr[rows, :]
        pv = _nt(ub, w_ref[0:D, :])
        p_ref[:, D:2 * D] = _nt(ub, w_ref[D:2 * D, :]).astype(p_ref.dtype)

        @pl.when(j == 0)
        def _():
            p_ref[:, 0:D] = pv.astype(p_ref.dtype)

        @pl.when(j > 0)
        def _():
            head = pv[:, 0:WTAIL] + _nt(ub, wb_ref[...])
            p_ref[:, 0:D] = jnp.concatenate([head, pv[:, WTAIL:]], axis=1).astype(p_ref.dtype)

    once = lambda j, i: (jnp.where(j == 0, i, nt - 1) + blk_off, 0)
    in_specs = [pl.BlockSpec((tb, D), lambda j, i: (jnp.where(j == 0, i, nt - 1), 0)), _full((8, D))] + _w_specs()
    args = [xin, mods, wi_main, wi_tail, wi_tail]
    aliases = None
    if prev is not None:
        in_specs += [pl.BlockSpec(memory_space=pl.ANY)] * 3
        args += list(prev)
        aliases = {5: 0, 6: 1, 7: 2}
    call = dict(
        body=body, args=args, name=name,
        out_shape=(jax.ShapeDtypeStruct((t_total, nslab * 2 * D), MXU_DTYPE),
                   jax.ShapeDtypeStruct((t_total, 128), F32), jax.ShapeDtypeStruct((t_total, D), MXU_DTYPE)),
        grid=(nslab, nt), in_specs=in_specs,
        out_specs=(pl.BlockSpec((tb, 2 * D), lambda j, i: (i + blk_off, j)), pl.BlockSpec((tb, 128), once),
                   pl.BlockSpec((tb, D), once)),
        scratch=[pltpu.VMEM((n, D), MXU_DTYPE)], sem=("arbitrary", "arbitrary"), vmem_mb=56, aliases=aliases)
    steps = lambda: ((pl.program_id(0) == 0) & (pl.program_id(1) == 0),
                     (pl.program_id(0) == nslab - 1) & (pl.program_id(1) == nt - 1),
                     (pl.program_id(0) == nslab - 1) & (pl.program_id(1) == nt // 2))
    return _run(_carry(call, comm, steps))


def _blk(s, nb, rev):
    return jnp.where(s == 0, nb - 1, (nb - 1 - s) if rev else (s - 1))


def _hgrn_gate(fr, lbraw_ref, d):
    lb = _sig(lbraw_ref[d:d + 1, :] - lbraw_ref[2 + d:3 + d, :])
    sg = _sig(fr)
    return lb, sg, lb + (1.0 - lb) * sg


def _hgrn_fwd(p_main, lbraw, nb, comm=None):
    t_total = p_main.shape[0]
    nch = TB // HC
    scale = HF ** -0.5
    hs_ = [slice(h * HF, (h + 1) * HF) for h in range(NH)]

    def prepare(d, q_ref, f_ref, v_ref, lb_ref):
        rev = d == 1
        mb = _tri(HC, rev)
        m01 = _b01(mb)
        order = list(reversed(range(nch)) if rev else range(nch))
        pre = {}
        for c in order:
            rows = slice(c * HC, (c + 1) * HC)
            _, _, f = _hgrn_gate(f_ref[rows, :].astype(F32), lb_ref, d)
            k = 1.0 - f
            cum = _dot01(m01, jnp.log(f))
            tot = cum[0:1, :] if rev else cum[HC - 1:HC, :]
            qd = _silu(q_ref[rows, :].astype(F32)) * scale * jnp.exp(cum)
            ki = k * jnp.exp(-cum)
            etot = jnp.exp(tot)
            pre[c] = (_mx(qd), _mx(ki), _mx(ki * etot), _mx(v_ref[rows, :]), etot)
        scs = {c: [_nt(pre[c][0][:, cs], pre[c][1][:, cs]) for cs in hs_] for c in order}
        upd = {c: [_tn(pre[c][3][:, cs], pre[c][2][:, cs]) for cs in hs_] for c in order}
        intra = {c: [_nn(jnp.where(mb, scs[c][h], 0.0), pre[c][3][:, cs]) for h, cs in enumerate(hs_)] for c in order}
        return order, pre, upd, intra

    def chain_step(prep, n, o_ref, sp_ref, st):
        order, pre, upd, intra = prep
        c = order[n]
        rows = slice(c * HC, (c + 1) * HC)
        qdb, etot = pre[c][0], pre[c][4]
        for h, cs in enumerate(hs_):
            sth = st[h]
            stb = sth.astype(sp_ref.dtype)
            sp_ref[c, h] = stb
            o_ref[rows, cs] = (intra[c][h] + _nt(qdb[:, cs], stb)).astype(o_ref.dtype)
            st[h] = sth * etot[:, cs] + upd[c][h]

    def body(q0, f0, v0, q1, f1, v1, lb_ref, o0, sp0, o1, sp1, st0, st1):
        @pl.when(pl.program_id(0) == 0)
        def _():
            st0[...] = jnp.zeros_like(st0)
            st1[...] = jnp.zeros_like(st1)

        prep0 = prepare(0, q0, f0, v0, lb_ref)
        prep1 = prepare(1, q1, f1, v1, lb_ref)
        for n in range(nch):
            chain_step(prep0, n, o0, sp0, st0)
            chain_step(prep1, n, o1, sp1, st1)

    col = lambda rev, j: (lambda s: (_blk(s, nb, rev), j))
    state = lambda rev: pl.BlockSpec((nch, NH, HF, HF), lambda s: (_blk(s, nb, rev), 0, 0, 0))
    o_shape = jax.ShapeDtypeStruct((t_total, D), MXU_DTYPE)
    s_shape = jax.ShapeDtypeStruct((nch * nb, NH, HF, HF), MXU_DTYPE)
    call = dict(
        body=body, args=[p_main] * 6 + [lbraw], name="hgrn_fwd",
        out_shape=(o_shape, s_shape, o_shape, s_shape), grid=(nb,),
        in_specs=[pl.BlockSpec((TB, D), col(False, 0)), pl.BlockSpec((TB, D), col(False, 1)),
                  pl.BlockSpec((TB, D), col(False, 3)), pl.BlockSpec((TB, D), col(True, 0)),
                  pl.BlockSpec((TB, D), col(True, 2)), pl.BlockSpec((TB, D), col(True, 3)), _full((8, D))],
        out_specs=(pl.BlockSpec((TB, D), col(False, 0)), state(False), pl.BlockSpec((TB, D), col(True, 0)),
                   state(True)),
        scratch=[pltpu.VMEM((NH, HF, HF), F32), pltpu.VMEM((NH, HF, HF), F32)], sem=("arbitrary",), vmem_mb=40)
    return _run(_carry(call, comm, lambda: (pl.program_id(0) == 0, pl.program_id(0) == nb - 1,
                                            pl.program_id(0) == nb - 3)))


def _hgrn_bwd(p_main, lbraw, sprev, do, d, nb, prev, comm=None):
    t_total = p_main.shape[0]
    rev = d == 1
    nch = TB // HC
    scale = HF ** -0.5
    last = prev is not None
    odt = MXU_DTYPE if last else F32

    def body(q_ref, f_ref, v_ref, lb_ref, sp_ref, do_ref, *rest):
        if last:
            dqp_ref, dvp_ref = rest[:2]
            rest = rest[2:]
        dq_ref, df_ref, dv_ref, dlb_ref, dst = rest
        sp_id = pl.program_id(0)
        is_ctx = sp_id == nb - 1

        @pl.when(sp_id == 0)
        def _():
            dst[...] = jnp.zeros_like(dst)
            dlb_ref[...] = jnp.zeros_like(dlb_ref)

        mb = _tri(HC, rev)
        mbt = _tri(HC, not rev)
        m01 = _b01(mb)
        mt01 = _b01(mbt)
        order = list(range(nch) if rev else reversed(range(nch)))
        hs_ = [slice(h * HF, (h + 1) * HF) for h in range(NH)]
        pre = {}
        for c in order:
            rows = slice(c * HC, (c + 1) * HC)
            lb, sg, f = _hgrn_gate(f_ref[rows, :].astype(F32), lb_ref, d)
            k = 1.0 - f
            cum = _dot01(m01, jnp.log(f))
            tot = cum[0:1, :] if rev else cum[HC - 1:HC, :]
            e = jnp.exp(cum)
            ei = jnp.exp(-cum)
            etot = jnp.exp(tot)
            ee = ei * etot
            qraw = q_ref[rows, :].astype(F32)
            sq = _sig(qraw)
            qd = qraw * sq * scale * e
            ki = k * ei
            ke = k * ee
            dov = jnp.where(is_ctx, 0.0, do_ref[rows, :].astype(F32))
            pre[c] = dict(lb=lb, sg=sg, f=f, e=e, ei=ei, ee=ee, etot=etot, qd=qd, ki=ki, ke=ke,
                          dsq=sq * (1.0 + qraw * (1.0 - sq)),
                          qdb=_mx(qd), kib=_mx(ki), keb=_mx(ke), vb=_mx(v_ref[rows, :]), dob=_mx(dov))
        units = [(c, h) for c in order for h in range(NH)]
        col = lambda u, key: pre[u[0]][key][:, hs_[u[1]]]
        pt = {u: jnp.where(mbt, _nt(col(u, "kib"), col(u, "qdb")), 0.0) for u in units}
        dp = {u: jnp.where(mb, _nt(col(u, "dob"), col(u, "vb")), 0.0) for u in units}
        dpt = {u: jnp.where(mbt, _nt(col(u, "vb"), col(u, "dob")), 0.0) for u in units}
        dv_i = {u: _nn(pt[u], col(u, "dob")) for u in units}
        dqd_ = {u: _nn(dp[u], col(u, "kib")) + _nn(col(u, "dob"), sp_ref[u[0], u[1]]) for u in units}
        dki_ = {u: _nn(dpt[u], col(u, "qdb")) for u in units}
        dsl = {u: _tn(col(u, "dob"), col(u, "qdb")) for u in units}
        for c in order:
            rows = slice(c * HC, (c + 1) * HC)
            p = pre[c]
            dv_l, dke_l, dtot_l = [], [], []
            for h, cs in enumerate(hs_):
                dso = dst[h]
                dsob = _mx(dso)
                dv_l.append(dv_i[(c, h)] + _nt(p["keb"][:, cs], dsob))
                dke_l.append(_nn(p["vb"][:, cs], dsob))
                dtot_l.append(_colsum(dso * sp_ref[c, h].astype(F32)) * p["etot"][:, cs])
                dst[h] = dso * p["etot"][:, cs] + dsl[(c, h)]
            lb, sg, f, e, ei, ee, qd, ki, ke = (p[n_] for n_ in ("lb", "sg", "f", "e", "ei", "ee", "qd", "ki", "ke"))
            dqd = jnp.concatenate([dqd_[(c, h)] for h in range(NH)], axis=1)
            dki = jnp.concatenate([dki_[(c, h)] for h in range(NH)], axis=1)
            dke = jnp.concatenate(dke_l, axis=1)
            dcum = dqd * qd - dki * ki - dke * ke
            dtot = jnp.concatenate(dtot_l, axis=1) + _colsum(dke * ke)
            dk = dki * ei + dke * ee
            dlf = _dot01(mt01, dcum, ways=2) + dtot
            df = dlf / f - dk
            dlb_ref[0:1, :] += _colsum(df * (1.0 - sg))
            dfr = df * (1.0 - lb) * sg * (1.0 - sg)
            dq = dqd * e * scale * p["dsq"]
            dv = jnp.concatenate(dv_l, axis=1)
            if last:
                dq = dq + dqp_ref[rows, :]
                dv = dv + dvp_ref[rows, :]
            dq_ref[rows, :] = dq.astype(odt)
            dv_ref[rows, :] = dv.astype(odt)
            df_ref[rows, :] = dfr.astype(MXU_DTYPE)

    blk = lambda s: _blk(nb - 1 - s, nb, rev)
    col = lambda j: (lambda s: (blk(s), j))
    in_specs = [pl.BlockSpec((TB, D), col(0)), pl.BlockSpec((TB, D), col(1 + d)), pl.BlockSpec((TB, D), col(3)),
                _full((8, D)), pl.BlockSpec((nch, NH, HF, HF), lambda s: (blk(s), 0, 0, 0)),
                pl.BlockSpec((TB, D), lambda s: (jnp.minimum(blk(s), nb - 2), 0))]
    args = [p_main, p_main, p_main, lbraw, sprev, do]
    if last:
        in_specs += [pl.BlockSpec((TB, D), col(0))] * 2
        args += list(prev)
    call = dict(
        body=body, args=args, name=f"hgrn_bwd_{d}",
        out_shape=(jax.ShapeDtypeStruct((t_total, D), odt), jax.ShapeDtypeStruct((t_total, D), MXU_DTYPE),
                   jax.ShapeDtypeStruct((t_total, D), odt), jax.ShapeDtypeStruct((8, D), F32)),
        grid=(nb,), in_specs=in_specs,
        out_specs=(pl.BlockSpec((TB, D), col(0)), pl.BlockSpec((TB, D), col(0)), pl.BlockSpec((TB, D), col(0)),
                   _full((8, D))),
        scratch=[pltpu.VMEM((NH, HF, HF), F32)], sem=("arbitrary",), vmem_mb=48)
    return _run(_carry(call, comm, lambda: (pl.program_id(0) == 0, pl.program_id(0) == nb - 1)))


def _conv_masks(tb, is_ctx):
    seg = jnp.where(is_ctx, tb, GRID_W)
    pos = lax.broadcasted_iota(jnp.int32, (tb, 1), 0) & (seg - 1)
    return pos, seg


def _shift_rows(x, dshift, pos, seg):
    if dshift == 0:
        return x
    n = x.shape[0]
    rolled = pltpu.roll(x, (-dshift) % n, 0)
    ok = (pos + dshift >= 0) & (pos + dshift < seg)
    return jnp.where(ok, rolled, 0.0)


def _ssd_prep(p_main, p_dt, convp, dtb, nb):
    t_total = p_main.shape[0]

    def body(x_ref, dt_ref, cw_ref, dtb_ref, xa_ref, ds_ref, dts_ref):
        is_ctx = pl.program_id(0) == nb - 1
        pos, seg = _conv_masks(TB, is_ctx)
        xv = x_ref[...].astype(F32)
        acc = cw_ref[5:6, :] + cw_ref[2:3, :] * xv
        for kk in (0, 1, 3, 4):
            acc = acc + cw_ref[kk:kk + 1, :] * _shift_rows(xv, kk - 2, pos, seg)
        sg = _sig(acc)
        xa_ref[...] = (acc * sg).astype(xa_ref.dtype)
        ds_ref[...] = (sg * (1.0 + acc * (1.0 - sg))).astype(ds_ref.dtype)
        dts_ref[...] = _softplus(dt_ref[...] + dtb_ref[0:1, :])

    wide = pl.BlockSpec((TB, 2048), lambda i: (i, 0))
    return _pcall(
        body, name="ssd_prep",
        out_shape=(jax.ShapeDtypeStruct((t_total, 2048), MXU_DTYPE), jax.ShapeDtypeStruct((t_total, 2048), MXU_DTYPE),
                   jax.ShapeDtypeStruct((t_total, 128), F32)),
        grid=(nb,),
        in_specs=[pl.BlockSpec((TB, 2048), lambda i: (i, 3)), pl.BlockSpec((TB, 128), lambda i: (i, 0)),
                  _full((8, 2048)), _full((8, 128))],
        out_specs=(wide, wide, pl.BlockSpec((TB, 128), lambda i: (i, 0))),
        sem=("parallel",), vmem_mb=32,
    )(p_main, p_dt, convp, dtb)


def _ssd_prep_bwd(p_main, p_dt, convp, dtb, dsl, dxa, dxs_skip, ddts, nb):
    t_total = p_main.shape[0]

    def body(x_ref, dt_ref, cw_ref, dtb_ref, ds_ref, dxa_ref, dsk_ref, ddts_ref, dx_ref, ddt_ref, dcw_ref, ddtb_ref):
        i = pl.program_id(0)
        is_ctx = i == nb - 1

        @pl.when(i == 0)
        def _():
            dcw_ref[...] = jnp.zeros_like(dcw_ref)
            ddtb_ref[...] = jnp.zeros_like(ddtb_ref)

        pos, seg = _conv_masks(TB, is_ctx)
        xv = x_ref[...].astype(F32)
        dact = dxa_ref[...]
        dact = jnp.concatenate([dact[:, :D] + jnp.where(is_ctx, 0.0, dsk_ref[...].astype(F32)), dact[:, D:]], axis=1)
        dpre = dact * ds_ref[...].astype(F32)
        dxv = cw_ref[2:3, :] * dpre
        dcw_ref[2:3, :] += _colsum(xv * dpre)
        for kk in (0, 1, 3, 4):
            sdp = _shift_rows(dpre, 2 - kk, pos, seg)
            dxv = dxv + cw_ref[kk:kk + 1, :] * sdp
            dcw_ref[kk:kk + 1, :] += _colsum(xv * sdp)
        dx_ref[...] = dxv.astype(dx_ref.dtype)
        dcw_ref[5:6, :] += _colsum(dpre)
        draw = ddts_ref[...] * _sig(dt_ref[...] + dtb_ref[0:1, :])
        ddt_ref[...] = draw.astype(ddt_ref.dtype)
        ddtb_ref[0:1, :] += _colsum(draw)

    return _pcall(
        body, name="ssd_prep_bwd",
        out_shape=(jax.ShapeDtypeStruct((t_total, 2048), MXU_DTYPE), jax.ShapeDtypeStruct((t_total, 128), MXU_DTYPE),
                   jax.ShapeDtypeStruct((8, 2048), F32), jax.ShapeDtypeStruct((8, 128), F32)),
        grid=(nb,),
        in_specs=[pl.BlockSpec((TB, 2048), lambda i: (i, 3)), pl.BlockSpec((TB, 128), lambda i: (i, 0)),
                  _full((8, 2048)), _full((8, 128)), pl.BlockSpec((TB, 2048), lambda i: (i, 0)),
                  pl.BlockSpec((TB, 2048), lambda i: (i, 0)),
                  pl.BlockSpec((TB, D), lambda i: (jnp.minimum(i, nb - 2), 0)),
                  pl.BlockSpec((TB, 128), lambda i: (i, 0))],
        out_specs=(pl.BlockSpec((TB, 2048), lambda i: (i, 0)), pl.BlockSpec((TB, 128), lambda i: (i, 0)),
                   _full((8, 2048)), _full((8, 128))),
        sem=("arbitrary",), vmem_mb=40,
    )(p_main, p_dt, convp, dtb, dsl, dxa, dxs_skip, ddts)


def _dot2(x, m01):
    hi = x.astype(BF16)
    lo = (x - hi.astype(F32)).astype(BF16)
    f = lambda t: lax.dot_general(t, m01, (((1,), (0,)), ((), ())), preferred_element_type=F32)
    return f(hi) + f(lo)


def _head_lanes(c0, c1):
    p = lax.broadcasted_iota(jnp.int32, (128, 128), 0)
    l = lax.broadcasted_iota(jnp.int32, (128, 128), 1)
    return _b01(((l == c0) & (p < SP)) | ((l == c1) & (p >= SP)))


def _one_lane(col):
    return _b01(lax.broadcasted_iota(jnp.int32, (128, 128), 1) == col)


def _lane_pick(x, lane, col):
    return _rowsum(jnp.where(lane == col, x, 0.0))


def _ssd_chunk_common(dts, alog_ref, m01, rev):
    lane = lax.broadcasted_iota(jnp.int32, (1, 128), 1)
    arow = -jnp.exp(alog_ref[0:1, :])
    cum = _dot01(m01, dts * arow)
    tot = cum[0:1, :] if rev else cum[SC - 1:SC, :]
    return lane, arow, cum, cum.T, tot


def _ssd_fwd(xa, dts, alog, d, nb, comm=None):
    t_total = xa.shape[0]
    rev = d == 1
    nch = TB // SC
    npair = SHEADS // 2

    def body(xa_ref, dts_ref, alog_ref, y_ref, sp_ref, st):
        s = pl.program_id(0)

        @pl.when(s == 0)
        def _():
            st[...] = jnp.zeros_like(st)

        mb = _tri(SC, rev)
        m01 = _b01(mb)
        lo = lax.broadcasted_iota(jnp.int32, (1, 128), 1) < SP
        rlo = lax.broadcasted_iota(jnp.int32, (128, 1), 0) < SP
        order = list(reversed(range(nch)) if rev else range(nch))
        pre = {}
        for c in order:
            rows = slice(c * SC, (c + 1) * SC)
            dts_c = dts_ref[rows, :]
            lane, arow, cum, cumt, tot = _ssd_chunk_common(dts_c, alog_ref, m01, rev)
            bgs = [_mx(xa_ref[rows, D + g * SN:D + (g + 1) * SN]) for g in range(4)]
            cgs = [_mx(xa_ref[rows, D + 512 + g * SN:D + 512 + (g + 1) * SN]) for g in range(4)]
            pairs = []
            for pr in range(npair):
                xs = xa_ref[rows, pr * 128:(pr + 1) * 128].astype(F32)
                cols = [16 * d + 2 * pr, 16 * d + 2 * pr + 1]
                cum_c = [_lane_pick(cum, lane, q) for q in cols]
                dt_c = [_lane_pick(dts_c, lane, q) for q in cols]
                tot_c = [_lane_pick(tot, lane, q) for q in cols]
                dtx = xs * jnp.where(lo, dt_c[0], dt_c[1])
                e1_pair = jnp.where(lo, jnp.exp(cum_c[0]), jnp.exp(cum_c[1]))
                e2_pair = jnp.where(lo, jnp.exp(tot_c[0] - cum_c[0]), jnp.exp(tot_c[1] - cum_c[1]))
                etot_col = jnp.where(rlo, jnp.exp(tot_c[0]), jnp.exp(tot_c[1]))
                decs = [jnp.where(mb, jnp.exp(cum_c[q] - cumt[cols[q]:cols[q] + 1, :]), 0.0) for q in range(2)]
                dtxq = [_mx(jnp.where(lo if q == 0 else ~lo, dtx, 0.0)) for q in range(2)]
                pairs.append(dict(e1=e1_pair, etot=etot_col, decs=decs, dtxq=dtxq, xe=_mx(dtx * e2_pair)))
            pre[c] = (bgs, cgs, pairs)
        gm = {(c, g): _nt(pre[c][1][g], pre[c][0][g]) for c in order for g in range(4)}
        upd = {(c, pr): _tn(pre[c][2][pr]["xe"], pre[c][0][pr // 2]) for c in order for pr in range(npair)}
        intra = {(c, pr): sum(_nn(gm[(c, pr // 2)] * pre[c][2][pr]["decs"][q], pre[c][2][pr]["dtxq"][q]) for q in range(2))
                 for c in order for pr in range(npair)}
        for c in order:
            rows = slice(c * SC, (c + 1) * SC)
            bgs, cgs, pairs = pre[c]
            for pr in range(npair):
                stp = st[pr]
                stb = stp.astype(sp_ref.dtype)
                sp_ref[c, pr] = stb
                y_ref[rows, pr * 128:(pr + 1) * 128] = (
                    intra[(c, pr)] + pairs[pr]["e1"] * _nt(cgs[pr // 2], stb)).astype(y_ref.dtype)
                st[pr] = stp * pairs[pr]["etot"] + upd[(c, pr)]

    blk = lambda s: _blk(s, nb, rev)
    call = dict(
        body=body, args=[xa, dts, alog], name=f"ssd_fwd_{d}",
        out_shape=(jax.ShapeDtypeStruct((t_total, D), MXU_DTYPE),
                   jax.ShapeDtypeStruct((nch * nb, npair, 128, SN), MXU_DTYPE)),
        grid=(nb,),
        in_specs=[pl.BlockSpec((TB, 2048), lambda s: (blk(s), 0)), pl.BlockSpec((TB, 128), lambda s: (blk(s), 0)),
                  _full((8, 128))],
        out_specs=(pl.BlockSpec((TB, D), lambda s: (blk(s), 0)),
                   pl.BlockSpec((nch, npair, 128, SN), lambda s: (blk(s), 0, 0, 0))),
        scratch=[pltpu.VMEM((npair, 128, SN), F32)], sem=("arbitrary",), vmem_mb=40)
    return _run(_carry(call, comm, lambda: (pl.program_id(0) == 0, pl.program_id(0) == nb - 1,
                                            pl.program_id(0) == nb - 4)))


def _ssd_bwd(xa, dts, alog, sprev, dy, d, nb, prev, comm=None):
    t_total = xa.shape[0]
    rev = d == 1
    nch = TB // SC
    npair = SHEADS // 2
    last = prev is not None

    def body(xa_ref, dts_ref, alog_ref, sp_ref, dy_ref, *rest):
        if last:
            dxp_ref, ddp_ref = rest[:2]
            rest = rest[2:]
        dxa_ref, ddts_ref, da_ref, dst, zc_scr = rest
        sp_id = pl.program_id(0)
        is_ctx = sp_id == nb - 1

        @pl.when(sp_id == 0)
        def _():
            dst[...] = jnp.zeros_like(dst)
            da_ref[...] = jnp.zeros_like(da_ref)
            zc_scr[...] = jnp.zeros_like(zc_scr)

        mb = _tri(SC, rev)
        m01 = _b01(mb)
        mt01 = _b01(_tri(SC, not rev))
        lo = lax.broadcasted_iota(jnp.int32, (1, 128), 1) < SP
        rlo = lax.broadcasted_iota(jnp.int32, (128, 1), 0) < SP
        order = list(range(nch) if rev else reversed(range(nch)))
        pre = {}
        for c in order:
            rows = slice(c * SC, (c + 1) * SC)
            dts_c = dts_ref[rows, :]
            lane, arow, cum, cumt, tot = _ssd_chunk_common(dts_c, alog_ref, m01, rev)
            pairs = []
            for pr in range(npair):
                xs = xa_ref[rows, pr * 128:(pr + 1) * 128].astype(F32)
                dyp = jnp.where(is_ctx, 0.0, dy_ref[rows, pr * 128:(pr + 1) * 128].astype(F32))
                cols = [16 * d + 2 * pr, 16 * d + 2 * pr + 1]
                cum_c = [_lane_pick(cum, lane, q) for q in cols]
                dt_c = [_lane_pick(dts_c, lane, q) for q in cols]
                tot_c = [_lane_pick(tot, lane, q) for q in cols]
                e1_c = [jnp.exp(cum_c[q]) for q in range(2)]
                e2_c = [jnp.exp(tot_c[q] - cum_c[q]) for q in range(2)]
                etot_c = [jnp.exp(tot_c[q]) for q in range(2)]
                dt_pair = jnp.where(lo, dt_c[0], dt_c[1])
                e1_pair = jnp.where(lo, e1_c[0], e1_c[1])
                e2_pair = jnp.where(lo, e2_c[0], e2_c[1])
                dtx = xs * dt_pair
                decs = [jnp.where(mb, jnp.exp(cum_c[q] - cumt[cols[q]:cols[q] + 1, :]), 0.0) for q in range(2)]
                dyq = [_mx(jnp.where(lo if q == 0 else ~lo, dyp, 0.0)) for q in range(2)]
                pairs.append(dict(xs=xs, dyp=dyp, cols=cols, e1_c=e1_c, e2_c=e2_c, etot_c=etot_c, dt_pair=dt_pair,
                                  e2_pair=e2_pair, etot_col=jnp.where(rlo, etot_c[0], etot_c[1]), dtx=dtx,
                                  dtxb=_mx(dtx), xeb=_mx(dtx * e2_pair), dy0b=_mx(dyp * e1_pair), decs=decs, dyq=dyq))
            pre[c] = dict(lane=lane, arow=arow, dts=dts_c, pairs=pairs, cum=cum, tot=tot,
                          bgb=[_mx(xa_ref[rows, D + g * SN:D + (g + 1) * SN]) for g in range(4)],
                          cgb=[_mx(xa_ref[rows, D + 512 + g * SN:D + 512 + (g + 1) * SN]) for g in range(4)])
        units = [(c, pr) for c in order for pr in range(npair)]
        head_lanes = [_head_lanes(16 * d + 2 * pr, 16 * d + 2 * pr + 1) for pr in range(npair)]
        one_lane = {16 * d + h: _one_lane(16 * d + h) for h in range(SHEADS)}
        P = lambda u: pre[u[0]]["pairs"][u[1]]
        cgu = lambda u: pre[u[0]]["cgb"][u[1] // 2]
        gm = {(c, g): _nt(pre[c]["cgb"][g], pre[c]["bgb"][g]) for c in order for g in range(4)}
        y0 = {u: _nt(cgu(u), sp_ref[u[0], u[1]]) for u in units}
        dcg_i = {u: _nn(P(u)["dy0b"], sp_ref[u[0], u[1]]) for u in units}
        dsl = {u: _tn(P(u)["dy0b"], cgu(u)) for u in units}
        w_ = {(u, q): gm[(u[0], u[1] // 2)] * P(u)["decs"][q] for u in units for q in range(2)}
        dw_ = {(u, q): jnp.where(mb, _nt(P(u)["dyq"][q], P(u)["dtxb"]), 0.0) for u in units for q in range(2)}
        ddtx_i = {(u, q): _tn(w_[(u, q)], P(u)["dyq"][q]) for u in units for q in range(2)}
        for c in order:
            rows = slice(c * SC, (c + 1) * SC)
            pc = pre[c]
            lane, arow, dts_c = pc["lane"], pc["arow"], pc["dts"]
            d1 = jnp.zeros((SC, 128), F32)
            d2 = jnp.zeros((SC, 128), F32)
            dz = jnp.zeros((SC, 128), F32)
            ddt = jnp.zeros((SC, 128), F32)
            dtot = jnp.zeros((1, 128), F32)
            dgm = [jnp.zeros((SC, SC), F32) for _ in range(4)]
            dbg = [jnp.zeros((SC, SN), F32) for _ in range(4)]
            dcg = [jnp.zeros((SC, SN), F32) for _ in range(4)]
            for pr in range(npair):
                u, g, p = (c, pr), pr // 2, pc["pairs"][pr]
                hs = head_lanes[pr]
                dso = dst[pr]
                dsob = _mx(dso)
                dxe = _nt(pc["bgb"][g], dsob)
                dbg[g] = dbg[g] + _nn(p["xeb"], dsob)
                ddtx = dxe * p["e2_pair"]
                d2 = d2 + _dot2(dxe * p["dtx"], hs)
                dcg[g] = dcg[g] + dcg_i[u]
                d1 = d1 + _dot2(p["dyp"] * y0[u], hs)
                sprod = dso * sp_ref[c, pr].astype(F32)
                dst[pr] = dso * p["etot_col"] + dsl[u]
                for q in range(2):
                    hm = lo if q == 0 else ~lo
                    col = p["cols"][q]
                    dw = dw_[(u, q)]
                    ddtx = ddtx + jnp.where(hm, ddtx_i[(u, q)], 0.0)
                    dgm[g] = dgm[g] + dw * p["decs"][q]
                    z = dw * w_[(u, q)]
                    dz = dz + _dot2(z, one_lane[col])
                    zc_scr[col:col + 1, :] = _colsum(z)
                    tsum = _rowsum(_colsum(sprod[q * SP:(q + 1) * SP, :]))
                    dtot = jnp.where(lane == col, tsum * p["etot_c"][q], dtot)
                dxs = ddtx * p["dt_pair"]
                ddt = ddt + _dot2(ddtx * p["xs"], hs)
                if last:
                    dxs = dxs + dxp_ref[rows, pr * 128:(pr + 1) * 128]
                dxa_ref[rows, pr * 128:(pr + 1) * 128] = dxs
            e2_all = jnp.exp(pc["tot"] - pc["cum"])
            dcum = dz - zc_scr[...].T + d1 * jnp.exp(pc["cum"]) - d2 * e2_all
            dtot = dtot + _colsum(d2 * e2_all)
            for g in range(4):
                db = dbg[g] + _tn(dgm[g], pc["cgb"][g])
                dc = dcg[g] + _nn(dgm[g], pc["bgb"][g])
                if last:
                    db = db + dxp_ref[rows, D + g * SN:D + (g + 1) * SN]
                    dc = dc + dxp_ref[rows, D + 512 + g * SN:D + 512 + (g + 1) * SN]
                dxa_ref[rows, D + g * SN:D + (g + 1) * SN] = db
                dxa_ref[rows, D + 512 + g * SN:D + 512 + (g + 1) * SN] = dc
            dla = _dot01(mt01, dcum, ways=2) + dtot
            ddt = ddt + dla * arow
            da_ref[0:1, :] += _colsum(dla * dts_c)
            if last:
                ddt = ddt + ddp_ref[rows, :]
            ddts_ref[rows, :] = ddt

    blk = lambda s: _blk(nb - 1 - s, nb, rev)
    in_specs = [pl.BlockSpec((TB, 2048), lambda s: (blk(s), 0)), pl.BlockSpec((TB, 128), lambda s: (blk(s), 0)),
                _full((8, 128)), pl.BlockSpec((nch, npair, 128, SN), lambda s: (blk(s), 0, 0, 0)),
                pl.BlockSpec((TB, D), lambda s: (jnp.minimum(blk(s), nb - 2), 0))]
    args = [xa, dts, alog, sprev, dy]
    if last:
        in_specs += [pl.BlockSpec((TB, 2048), lambda s: (blk(s), 0)), pl.BlockSpec((TB, 128), lambda s: (blk(s), 0))]
        args += list(prev)
    call = dict(
        body=body, args=args, name=f"ssd_bwd_{d}",
        out_shape=(jax.ShapeDtypeStruct((t_total, 2048), F32), jax.ShapeDtypeStruct((t_total, 128), F32),
                   jax.ShapeDtypeStruct((8, 128), F32)),
        grid=(nb,), in_specs=in_specs,
        out_specs=(pl.BlockSpec((TB, 2048), lambda s: (blk(s), 0)), pl.BlockSpec((TB, 128), lambda s: (blk(s), 0)),
                   _full((8, 128))),
        scratch=[pltpu.VMEM((npair, 128, SN), F32), pltpu.VMEM((128, 128), F32)], sem=("arbitrary",), vmem_mb=48)
    return _run(_carry(call, comm, lambda: (pl.program_id(0) == 0, pl.program_id(0) == nb - 1)))


def _readout(o, g, yy, z, vec_ref):
    hg, ss, keep = [], [], []
    for h in range(NH):
        cs = slice(h * HF, (h + 1) * HF)
        oh = o[:, cs]
        r = lax.rsqrt(jnp.mean(oh * oh, axis=1, keepdims=True) + EPS)
        hg.append(oh * r * vec_ref[0:1, cs] * _silu(g[:, cs]))
        keep.append(r)
    u = yy * _silu(z)
    for gi in range(4):
        cs = slice(gi * 256, (gi + 1) * 256)
        ug = u[:, cs]
        r = lax.rsqrt(jnp.mean(ug * ug, axis=1, keepdims=True) + EPS)
        ss.append(ug * r * vec_ref[2:3, cs])
        keep.append(r)
    return jnp.concatenate(hg, axis=1), jnp.concatenate(ss, axis=1), keep, u


def _mix_out(o_f, o_b, p_main, y_f, y_b, xa, x, vecs, w_out):
    n = x.shape[0]

    def body(of_ref, ob_ref, g_ref, z_ref, yf_ref, yb_ref, xs_ref, x_ref, vec_ref, w_ref,
             ymix_ref, ylat_ref, h1_ref, u2_ref):
        o = of_ref[...].astype(F32) + ob_ref[...].astype(F32)
        yy = yf_ref[...].astype(F32) + yb_ref[...].astype(F32) + vec_ref[1:2, :] * xs_ref[...].astype(F32)
        hg, ss, _, _ = _readout(o, g_ref[...].astype(F32), yy, z_ref[...].astype(F32), vec_ref)
        ymix = jnp.concatenate([hg, ss], axis=1).astype(MXU_DTYPE)
        ymix_ref[...] = ymix
        ylat = _nn(ymix, w_ref[...])
        ylat_ref[...] = ylat
        h1 = x_ref[...] + vec_ref[3:4, :] * ylat
        h1_ref[...] = h1
        r = lax.rsqrt(jnp.mean(h1 * h1, axis=1, keepdims=True) + EPS)
        u2_ref[...] = ((h1 * r * vec_ref[6:7, :]) * vec_ref[4:5, :] + vec_ref[5:6, :]).astype(MXU_DTYPE)

    row = lambda j: (lambda i: (i, j))
    return _pcall(
        body, name="mix_out",
        out_shape=(jax.ShapeDtypeStruct((n, 2 * D), MXU_DTYPE), jax.ShapeDtypeStruct((n, D), F32),
                   jax.ShapeDtypeStruct((n, D), F32), jax.ShapeDtypeStruct((n, D), MXU_DTYPE)),
        grid=(n // TB,),
        in_specs=[pl.BlockSpec((TB, D), row(0)), pl.BlockSpec((TB, D), row(0)), pl.BlockSpec((TB, D), row(4)),
                  pl.BlockSpec((TB, D), row(5)), pl.BlockSpec((TB, D), row(0)), pl.BlockSpec((TB, D), row(0)),
                  pl.BlockSpec((TB, D), row(0)), pl.BlockSpec((TB, D), row(0)), _full((8, D)), _full((2 * D, D))],
        out_specs=(pl.BlockSpec((TB, 2 * D), row(0)), pl.BlockSpec((TB, D), row(0)), pl.BlockSpec((TB, D), row(0)),
                   pl.BlockSpec((TB, D), row(0))),
        sem=("parallel",), vmem_mb=48,
    )(o_f, o_b, p_main, p_main, y_f, y_b, xa, x, vecs, w_out)


def _mix_bwd(dylat, o_f, o_b, p_main, y_f, y_b, xa, vecs, w_out, comm=None):
    n = dylat.shape[0]
    t_total = p_main.shape[0]
    nlat = n // TB

    def body(*refs):
        dg_ref, dz_ref, acc_ref = refs[11], refs[13], refs[15]
        i = pl.program_id(0)

        @pl.when(i == 0)
        def _():
            acc_ref[...] = jnp.zeros_like(acc_ref)

        @pl.when(i < nlat)
        def _():
            compute(*refs)

        @pl.when(i == nlat)
        def _():
            dg_ref[...] = jnp.zeros_like(dg_ref)
            dz_ref[...] = jnp.zeros_like(dz_ref)

    def compute(dyl_ref, of_ref, ob_ref, g_ref, z_ref, yf_ref, yb_ref, xs_ref, vec_ref, w_ref,
                do_ref, dg_ref, dys_ref, dz_ref, dxs_ref, acc_ref):
        dymix = _nt(dyl_ref[...], w_ref[...])
        o = of_ref[...].astype(F32) + ob_ref[...].astype(F32)
        g = g_ref[...].astype(F32)
        z = z_ref[...].astype(F32)
        xs = xs_ref[...].astype(F32)
        yy = yf_ref[...].astype(F32) + yb_ref[...].astype(F32) + vec_ref[1:2, :] * xs
        _, _, keep, u = _readout(o, g, yy, z, vec_ref)
        do_l, dg_l = [], []
        for h in range(NH):
            cs = slice(h * HF, (h + 1) * HF)
            oh, gh, r, wv = o[:, cs], g[:, cs], keep[h], vec_ref[0:1, cs]
            dhg = dymix[:, cs]
            xh = oh * r
            dn = dhg * _silu(gh)
            dg_l.append(dhg * xh * wv * _dsilu(gh))
            acc_ref[0:1, cs] += _colsum(dn * xh)
            dxh = dn * wv
            do_l.append(r * (dxh - xh * jnp.mean(dxh * xh, axis=1, keepdims=True)))
        du_l = []
        for gi in range(4):
            cs = slice(gi * 256, (gi + 1) * 256)
            ug, r, wv = u[:, cs], keep[NH + gi], vec_ref[2:3, cs]
            dss = dymix[:, D + gi * 256:D + (gi + 1) * 256]
            xh = ug * r
            acc_ref[2:3, cs] += _colsum(dss * xh)
            dxh = dss * wv
            du_l.append(r * (dxh - xh * jnp.mean(dxh * xh, axis=1, keepdims=True)))
        du = jnp.concatenate(du_l, axis=1)
        dyy = du * _silu(z)
        do_ref[...] = jnp.concatenate(do_l, axis=1).astype(do_ref.dtype)
        dg_ref[...] = jnp.concatenate(dg_l, axis=1).astype(dg_ref.dtype)
        dys_ref[...] = dyy.astype(dys_ref.dtype)
        dz_ref[...] = (du * yy * _dsilu(z)).astype(dz_ref.dtype)
        dxs_ref[...] = (dyy * vec_ref[1:2, :]).astype(dxs_ref.dtype)
        acc_ref[1:2, :] += _colsum(dyy * xs)

    row = lambda j: (lambda i: (jnp.minimum(i, nlat - 1), j))
    lat = pl.BlockSpec((TB, D), row(0))
    tok = pl.BlockSpec((TB, D), lambda i: (i, 0))
    call = dict(
        body=body, args=[dylat, o_f, o_b, p_main, p_main, y_f, y_b, xa, vecs, w_out], name="mix_bwd",
        out_shape=(jax.ShapeDtypeStruct((n, D), MXU_DTYPE), jax.ShapeDtypeStruct((t_total, D), MXU_DTYPE),
                   jax.ShapeDtypeStruct((n, D), MXU_DTYPE), jax.ShapeDtypeStruct((t_total, D), MXU_DTYPE),
                   jax.ShapeDtypeStruct((n, D), MXU_DTYPE), jax.ShapeDtypeStruct((8, D), F32)),
        grid=(t_total // TB,),
        in_specs=[lat, lat, lat, pl.BlockSpec((TB, D), row(4)), pl.BlockSpec((TB, D), row(5)), lat, lat, lat,
                  _full((8, D)), _full((2 * D, D))],
        out_specs=(lat, tok, lat, tok, lat, _full((8, D))), scratch=[],
        sem=("arbitrary",), vmem_mb=48)
    return _run(_carry(call, comm, lambda: (pl.program_id(0) == 0, pl.program_id(0) == t_total // TB - 1)))


def _ffn_up(u2, w_gate, w_up):
    n = u2.shape[0]
    tb = 1024

    def body(u_ref, wg_ref, wu_ref, g_ref, up_ref, a_ref):
        uv = u_ref[...]
        gt = _nt(uv, wg_ref[...])
        upv = _nt(uv, wu_ref[...])
        g_ref[...] = gt.astype(g_ref.dtype)
        up_ref[...] = upv.astype(up_ref.dtype)
        a_ref[...] = (_silu(gt) * upv).astype(a_ref.dtype)

    blk = pl.BlockSpec((tb, FSL), lambda j, i: (i, j))
    wblk = pl.BlockSpec((None, FSL, D), lambda j, i: (j, 0, 0))
    return _pcall(
        body, name="ffn_up",
        out_shape=(jax.ShapeDtypeStruct((n, DFFP), MXU_DTYPE),) * 3,
        grid=(4, n // tb), in_specs=[pl.BlockSpec((tb, D), lambda j, i: (i, 0)), wblk, wblk],
        out_specs=(blk, blk, blk), sem=("parallel", "parallel"), vmem_mb=48,
    )(u2, w_gate, w_up)


def _ffn_down_loss(act, w_down, h1, tgt, vecs):
    n = act.shape[0]
    tb = 512

    def body(a_ref, w_ref, h1_ref, t_ref, vec_ref, dh2_ref, dffn_ref, acc_ref):
        i = pl.program_id(0)

        @pl.when(i == 0)
        def _():
            acc_ref[...] = jnp.zeros_like(acc_ref)

        g2 = vec_ref[0:1, :]
        fw = vec_ref[1:2, :]
        nsub = 4
        sb = tb // nsub
        wv = w_ref[...]
        ffns = [_nn(a_ref[r_ * sb:(r_ + 1) * sb, :], wv) for r_ in range(nsub)]
        for r_ in range(nsub):
            rows = slice(r_ * sb, (r_ + 1) * sb)
            ffn = ffns[r_]
            h2 = h1_ref[rows, :] + g2 * ffn
            r = lax.rsqrt(jnp.mean(h2 * h2, axis=1, keepdims=True) + EPS)
            xh = h2 * r
            err = xh * fw - t_ref[rows, :]
            dy = err * (1.0 / D)
            acc_ref[2:3, :] += _colsum(err * err) * (0.5 / D)
            acc_ref[1:2, :] += _colsum(dy * xh)
            dxh = dy * fw
            dh2 = r * (dxh - xh * jnp.mean(dxh * xh, axis=1, keepdims=True))
            dh2_ref[rows, :] = dh2
            dffn_ref[rows, :] = (g2 * dh2).astype(dffn_ref.dtype)
            acc_ref[0:1, :] += _colsum(dh2 * ffn)

    return _pcall(
        body, name="ffn_down_loss",
        out_shape=(jax.ShapeDtypeStruct((n, D), F32), jax.ShapeDtypeStruct((n, D), MXU_DTYPE),
                   jax.ShapeDtypeStruct((8, D), F32)),
        grid=(n // tb,),
        in_specs=[pl.BlockSpec((tb, DFFP), lambda i: (i, 0)), _full((DFFP, D)), pl.BlockSpec((tb, D), lambda i: (i, 0)),
                  pl.BlockSpec((tb, D), lambda i: (i, 0)), _full((8, D))],
        out_specs=(pl.BlockSpec((tb, D), lambda i: (i, 0)), pl.BlockSpec((tb, D), lambda i: (i, 0)), _full((8, D))),
        sem=("arbitrary",), vmem_mb=48,
    )(act, w_down, h1, tgt, vecs)


def _ffn_bwd(dffn, w_down, gate, up, w_gate_t, w_up_t):
    n = dffn.shape[0]
    tb = 1024

    def body(df_ref, wd_ref, g_ref, up_ref, wg_ref, wu_ref, dg_ref, dup_ref, du_ref):
        j = pl.program_id(1)
        nsub = 4
        sb = tb // nsub
        wd, wg, wu = wd_ref[...], wg_ref[...], wu_ref[...]
        dacts = [_nt(df_ref[r * sb:(r + 1) * sb, :], wd) for r in range(nsub)]
        parts = []
        for r in range(nsub):
            rows = slice(r * sb, (r + 1) * sb)
            gt = g_ref[rows, :].astype(F32)
            upv = up_ref[rows, :].astype(F32)
            sg = _sig(gt)
            dgt = (dacts[r] * upv * (sg * (1.0 + gt * (1.0 - sg)))).astype(MXU_DTYPE)
            dupv = (dacts[r] * (gt * sg)).astype(MXU_DTYPE)
            dg_ref[rows, :] = dgt
            dup_ref[rows, :] = dupv
            parts.append(_nn(dgt, wg) + _nn(dupv, wu))
        part = jnp.concatenate(parts, axis=0)

        @pl.when(j == 0)
        def _():
            du_ref[...] = part

        @pl.when(j > 0)
        def _():
            du_ref[...] += part

    tok = pl.BlockSpec((tb, D), lambda i, j: (i, 0))
    ffb = pl.BlockSpec((tb, FSL), lambda i, j: (i, j))
    wsl = pl.BlockSpec((None, FSL, D), lambda i, j: (j, 0, 0))
    return _pcall(
        body, name="ffn_bwd",
        out_shape=(jax.ShapeDtypeStruct((n, DFFP), MXU_DTYPE), jax.ShapeDtypeStruct((n, DFFP), MXU_DTYPE),
                   jax.ShapeDtypeStruct((n, D), F32)),
        grid=(n // tb, 4),
        in_specs=[tok, pl.BlockSpec((FSL, D), lambda i, j: (j, 0)), ffb, ffb, wsl, wsl],
        out_specs=(ffb, ffb, tok), sem=("parallel", "arbitrary"), vmem_mb=48,
    )(dffn, w_down, gate, up, w_gate_t, w_up_t)


def _ffn_norm_bwd(du, h1, ylat, dh2, vecs):
    n = du.shape[0]
    tb = 512

    def body(du_ref, h1_ref, yl_ref, dh2_ref, vec_ref, dh1_ref, dyl_ref, acc_ref):
        @pl.when(pl.program_id(0) == 0)
        def _():
            acc_ref[...] = jnp.zeros_like(acc_ref)

        duv = du_ref[...]
        h1 = h1_ref[...]
        r = lax.rsqrt(jnp.mean(h1 * h1, axis=1, keepdims=True) + EPS)
        xh = h1 * r
        nw = vec_ref[2:3, :]
        acc_ref[0:1, :] += _colsum(duv)
        acc_ref[1:2, :] += _colsum(duv * xh * nw)
        dn = duv * vec_ref[1:2, :]
        acc_ref[2:3, :] += _colsum(dn * xh)
        dxh = dn * nw
        dh1 = dh2_ref[...] + r * (dxh - xh * jnp.mean(dxh * xh, axis=1, keepdims=True))
        dh1_ref[...] = dh1
        dyl_ref[...] = (vec_ref[0:1, :] * dh1).astype(dyl_ref.dtype)
        acc_ref[3:4, :] += _colsum(dh1 * yl_ref[...])

    tok = pl.BlockSpec((tb, D), lambda i: (i, 0))
    return _pcall(
        body, name="ffn_norm_bwd",
        out_shape=(jax.ShapeDtypeStruct((n, D), F32), jax.ShapeDtypeStruct((n, D), MXU_DTYPE),
                   jax.ShapeDtypeStruct((8, D), F32)),
        grid=(n // tb,), in_specs=[tok, tok, tok, tok, _full((8, D))], out_specs=(tok, tok, _full((8, D))),
        sem=("arbitrary",), vmem_mb=40,
    )(du, h1, ylat, dh2, vecs)


def _deep_rows(rows):
    return max(r for r in range(128, 2305, 128) if rows % r == 0)


def _dw(a, b, name):
    tn_rows = a.shape[0]
    bt = _deep_rows(tn_rows)
    kk, nn_ = a.shape[1], b.shape[1]
    bk = 1024 if kk % 1024 == 0 else kk
    bn = 1024 if nn_ % 1024 == 0 else nn_
    nt = tn_rows // bt

    def body(a_ref, b_ref, o_ref, acc):
        t = pl.program_id(2)
        part = _tn(a_ref[...], b_ref[...])

        @pl.when(t == 0)
        def _():
            acc[...] = part

        @pl.when(t > 0)
        def _():
            acc[...] += part

        @pl.when(t == nt - 1)
        def _():
            o_ref[...] = acc[...].astype(o_ref.dtype)

    return _pcall(
        body, name=name, out_shape=jax.ShapeDtypeStruct((kk, nn_), MXU_DTYPE), grid=(kk // bk, nn_ // bn, nt),
        in_specs=[pl.BlockSpec((bt, bk), lambda i, j, t: (t, i)), pl.BlockSpec((bt, bn), lambda i, j, t: (t, j))],
        out_specs=pl.BlockSpec((bk, bn), lambda i, j, t: (i, j)), scratch=[pltpu.VMEM((bk, bn), F32)],
        sem=("parallel", "parallel", "arbitrary"), vmem_mb=40,
    )(a, b)


def _dw_in(segs, u_all, name):
    tiles = []
    for m, s_ in enumerate(segs):
        tiles += [(m, h) for h in range(s_.shape[1] // D)]
    ntile = len(tiles)
    t_total = u_all.shape[0]
    bt = _deep_rows(t_total)
    nt = t_total // bt

    def body(u_ref, *refs):
        seg_refs, o_ref, acc = refs[:len(segs)], refs[len(segs)], refs[len(segs) + 1]
        n, t = pl.program_id(0), pl.program_id(1)
        for k, (m, _) in enumerate(tiles):
            @pl.when(n == k)
            def _(m=m):
                part = _tn(seg_refs[m][...], u_ref[...])

                @pl.when(t == 0)
                def _():
                    acc[...] = part

                @pl.when(t > 0)
                def _():
                    acc[...] += part

        @pl.when(t == nt - 1)
        def _():
            o_ref[...] = acc[...].astype(o_ref.dtype)

    def seg_spec(m):
        ks = [k for k, (mm, _) in enumerate(tiles) if mm == m]
        lo, hi = ks[0], ks[-1]
        on = lambda n: (n >= lo) & (n <= hi)
        return pl.BlockSpec((bt, D), lambda n, t: (jnp.where(on(n), t, 0), jnp.where(on(n), n - lo, 0)))

    return _pcall(
        body, name=name, out_shape=jax.ShapeDtypeStruct((1, ntile * D, D), MXU_DTYPE), grid=(ntile, nt),
        in_specs=[pl.BlockSpec((bt, D), lambda n, t: (t, 0))] + [seg_spec(m) for m in range(len(segs))],
        out_specs=pl.BlockSpec((None, D, D), lambda n, t: (0, n, 0)),
        scratch=[pltpu.VMEM((D, D), F32)], sem=("parallel", "arbitrary"), vmem_mb=56,
    )(u_all, *segs)


def _du_prenorm_bwd(segs, ddt, wi_main, wi_tail, xin, mods, dres, row_off, tb, name, comm=None):
    n = xin.shape[0]
    nt = n // tb
    off = row_off // tb
    has_dx = dres is not None

    def body(*refs):
        seg_refs = refs[:7]
        ddt_ref, w_ref, wb_ref, wdt_ref, x_ref, mod_ref = refs[7:13]
        rest = refs[13:]
        if has_dx:
            dres_ref, dx_ref, acc_ref, du_scr = rest
        else:
            acc_ref, du_scr = rest
        j, i = pl.program_id(0), pl.program_id(1)
        rows = pl.ds(pl.multiple_of(i * tb, tb), tb)

        @pl.when((i == 0) & (j == 0))
        def _():
            acc_ref[...] = jnp.zeros_like(acc_ref)

        @pl.when(j == 0)
        def _():
            du_scr[rows, :] = _nn(ddt_ref[...], wdt_ref[...])

        for k in range(4):
            if not has_dx and k == 2:
                continue

            @pl.when(j == k)
            def _(k=k):
                if k < 3:
                    sa, sb = seg_refs[2 * k][...], seg_refs[2 * k + 1][...]
                else:
                    sa, sb = seg_refs[6][:, 0:D], seg_refs[6][:, D:2 * D]
                part = _nn(sa, w_ref[0:D, :]) + _nn(sb, w_ref[D:2 * D, :])
                if k > 0:
                    part = part + _nn(sa[:, 0:WTAIL], wb_ref[...])
                du_scr[rows, :] += part

        @pl.when(j == 3)
        def _():
            du = du_scr[rows, :]
            xv = x_ref[...]
            r = lax.rsqrt(jnp.mean(xv * xv, axis=1, keepdims=True) + EPS)
            xh = xv * r
            nw = mod_ref[1:2, :]
            acc_ref[0:1, :] += _colsum(du)
            acc_ref[1:2, :] += _colsum(du * xh * nw)
            dn = du * mod_ref[0:1, :]
            acc_ref[2:3, :] += _colsum(dn * xh)
            if has_dx:
                dxh = dn * nw
                dx_ref[...] = dres_ref[...] + r * (dxh - xh * jnp.mean(dxh * xh, axis=1, keepdims=True))

    def seg_spec(k):
        width = D if k < 6 else 2 * D
        return pl.BlockSpec((tb, width), lambda j, i: (jnp.where(j == min(k // 2, 3), i + off, 0), 0))

    last = pl.BlockSpec((tb, D), lambda j, i: (jnp.where(j == 3, i, 0), 0))
    in_specs = [seg_spec(k) for k in range(7)]
    in_specs += [pl.BlockSpec((tb, 128), lambda j, i: (jnp.where(j == 0, i + off, 0), 0))] + _w_specs()
    in_specs += [last, _full((8, D))]
    args = list(segs) + [ddt, wi_main, wi_tail, wi_tail, xin, mods]
    out_shape = [jax.ShapeDtypeStruct((8, D), F32)]
    out_specs = [_full((8, D))]
    if has_dx:
        in_specs.append(last)
        args.append(dres)
        out_shape.insert(0, jax.ShapeDtypeStruct((n, D), F32))
        out_specs.insert(0, last)
    call = dict(body=body, args=args, name=name, out_shape=tuple(out_shape), grid=(4, nt), in_specs=in_specs,
                out_specs=tuple(out_specs), scratch=[pltpu.VMEM((n, D), F32)], sem=("arbitrary", "arbitrary"),
                vmem_mb=58)
    steps = lambda: ((pl.program_id(0) == 0) & (pl.program_id(1) == 0),
                     (pl.program_id(0) == 3) & (pl.program_id(1) == nt - 1))
    return _run(_carry(call, comm, steps))


def _sum8(v):
    def body(v_ref, o_ref):
        acc = v_ref[0]
        for k in range(1, 8):
            acc = acc + v_ref[k]
        o_ref[...] = acc

    return _pcall(body, name="small_sum", out_shape=jax.ShapeDtypeStruct(v.shape[1:], F32),
                  in_specs=[pl.BlockSpec(memory_space=pltpu.VMEM)], out_specs=pl.BlockSpec(memory_space=pltpu.VMEM))(v)


def _adamw(w, m, v, g, name, comm=None):
    lead = w.ndim == 3
    rows, cols = w.shape[-2:]
    rb = 256 if rows % 256 == 0 else (352 if rows % 352 == 0 else rows)
    c1 = 1.0 - B1 ** STEP
    c2 = 1.0 - B2 ** STEP

    def body(w_ref, m_ref, v_ref, g_ref, d_ref, nm_ref, nv_ref):
        gv = g_ref[...]
        mn = B1 * m_ref[...] + (1.0 - B1) * gv
        vn = B2 * v_ref[...] + (1.0 - B2) * (gv * gv)
        nm_ref[...] = mn
        nv_ref[...] = vn
        d_ref[...] = -LR * ((mn / c1) / (jnp.sqrt(vn / c2) + AEPS) + WD * w_ref[...])

    if rb == rows and rows > 1024:
        cb, steps = 256, cols // 256
        gspec = pl.BlockSpec((rows, cb), lambda i: (0, i))
        spec = pl.BlockSpec((None, rows, cb), lambda i: (0, 0, i)) if lead else gspec
    else:
        steps = rows // rb
        gspec = pl.BlockSpec((rb, cols), lambda i: (i, 0))
        spec = pl.BlockSpec((None, rb, cols), lambda i: (0, i, 0)) if lead else gspec
    call = dict(body=body, args=[w, m, v, g], name=name, out_shape=(jax.ShapeDtypeStruct(w.shape, F32),) * 3,
                grid=(steps,), in_specs=[spec] * 3 + [gspec], out_specs=(spec,) * 3, scratch=[],
                sem=("arbitrary",) if comm is not None else ("parallel",), vmem_mb=40)
    return _run(_carry(call, comm, lambda: (pl.program_id(0) == 0, pl.program_id(0) == steps - 1,
                                            pl.program_id(0) == steps - 1)))


def _rows(v, n):
    f = v.reshape(-1)
    return jnp.pad(f, (0, n * D - f.shape[0])).reshape(n, D)


def kernel(x, c, ctx, c_ctx, w_ada, b_ada, norm_mix, w_in, conv_w, conv_b, ssd_a_log, ssd_dt_bias, ssd_d, ssd_norm, hgrn_lb_raw, hgrn_norm, w_out, norm_ffn, w_gate, w_up, w_down, final_norm, loss_target, m_c_ctx, m_w_ada, m_b_ada, m_norm_mix, m_w_in, m_conv_w, m_conv_b, m_ssd_a_log, m_ssd_dt_bias, m_ssd_d, m_ssd_norm, m_hgrn_lb_raw, m_hgrn_norm, m_w_out, m_norm_ffn, m_w_gate, m_w_up, m_w_down, m_final_norm, v_c_ctx, v_w_ada, v_b_ada, v_norm_mix, v_w_in, v_conv_w, v_conv_b, v_ssd_a_log, v_ssd_dt_bias, v_ssd_d, v_ssd_norm, v_hgrn_lb_raw, v_hgrn_norm, v_w_out, v_norm_ffn, v_w_gate, v_w_up, v_w_down, v_final_norm):
    ix, iy, ic = lax.axis_index("x"), lax.axis_index("y"), lax.axis_index("c")
    chip = 2 * ix + iy
    me = 2 * chip + ic
    xl, xc, tgt = x[0], ctx[0], loss_target[0]
    n_lat, n_ctx = xl.shape[0], xc.shape[0]
    assert n_ctx == TB and n_lat % 1024 == 0
    t_total = n_lat + n_ctx
    nb = t_total // TB

    tr = lambda a: jnp.swapaxes(a, -1, -2)
    shift = [functools.partial(jnp.pad, pad_width=((8 * k, WSL + WTAIL - NSH - 8 * k), (0, 0))) for k in range(4)]
    slab = lax.switch(chip, shift, tr(w_in[0]).astype(MXU_DTYPE))
    padrows = lambda a: jnp.pad(a, ((0, FSL - DFF // 4), (0, 0))).astype(MXU_DTYPE)
    shards = [slab[:WSL], slab[WSL:], w_out[0].astype(MXU_DTYPE), padrows(tr(w_gate[0])), padrows(tr(w_up[0])),
              padrows(w_down[0])]
    own = lambda g_, s_: lax.dynamic_update_slice(g_, s_[None], (chip, 0, 0))
    pack = jnp.concatenate([c, hgrn_lb_raw.reshape(1, D), _rows(conv_w[0], 3), jnp.zeros((3, D), F32)], axis=0)
    ncol_ada = w_ada.shape[2]
    b_shard = lax.dynamic_slice(b_ada, (0, chip * ncol_ada), (1, ncol_ada))
    gath, araw, mod_all, wi_main, wi_tail = _prologue(pack, c_ctx.reshape(1, D), w_ada[0], b_shard, shards[:2])
    wi_main, wi_tail = own(wi_main, shards[0]), own(wi_tail, shards[1])
    gath = gath.reshape(8, 8, D)
    lbraw_full = gath[0::2, 1].reshape(4, 2, 2, 256).transpose(1, 2, 0, 3).reshape(4, D)
    convw_full = gath[0::2, 2:5].reshape(4, 3 * D)[:, :KCONV * 512].reshape(4, KCONV, 512).transpose(1, 0, 2)
    convw_full = convw_full.reshape(KCONV, 2048)
    lbraw8 = jnp.pad(lbraw_full, ((0, 4), (0, 0)))
    convp = jnp.concatenate([convw_full, conv_b, jnp.zeros((2, 2048), F32)], axis=0)
    dtb = jnp.pad(ssd_dt_bias.reshape(1, 32), ((0, 7), (0, 96)))
    alog = jnp.pad(ssd_a_log.reshape(1, 32), ((0, 7), (0, 96)))
    mod_all = mod_all.reshape(8, 16, ncol_ada)[0::2]
    mod_full = mod_all.transpose(1, 0, 2).reshape(16, 4 * ncol_ada)
    my_mod = lax.dynamic_slice(mod_full, (me, 0), (1, 6 * D)).reshape(6, D)
    sh1, sc1, g1, sh2, sc2, g2 = (my_mod[k:k + 1] for k in range(6))
    csh1, csc1 = mod_full[8:9, 0:D], mod_full[8:9, D:2 * D]

    zrow = jnp.zeros((1, D), F32)
    mods_lat = jnp.concatenate([1.0 + sc1, sh1, norm_mix, zrow, zrow, zrow, zrow, zrow], axis=0)
    mods_ctx = jnp.concatenate([1.0 + csc1, csh1, norm_mix, zrow, zrow, zrow, zrow, zrow], axis=0)
    outs = _inproj(xl, mods_lat, wi_main, wi_tail, t_total, 1024, 0, None, "inproj_lat",
                   comm=_comm_gather(shards[2:3]))
    w_out_f = own(outs[3], shards[2]).reshape(2 * D, D)
    p_main, p_dt, u_all = _inproj(xc, mods_ctx, wi_main, wi_tail, t_total, TB, nb - 1, outs[:3], "inproj_ctx")

    o_f, hs_f, o_b, hs_b, wu_g, wd_g = _hgrn_fwd(p_main, lbraw8, nb, comm=_comm_gather(shards[4:]))
    w_down_f = own(wd_g, shards[5]).reshape(DFFP, D)
    wu_g = own(wu_g, shards[4])
    xa, dsl, dts = _ssd_prep(p_main, p_dt, convp, dtb, nb)
    y_f, ss_f, wg_g = _ssd_fwd(xa, dts, alog, 0, nb, comm=_comm_gather(shards[3:4]))
    wg_g = own(wg_g, shards[3])
    y_b, ss_b = _ssd_fwd(xa, dts, alog, 1, nb)

    vec_mix = jnp.concatenate([jnp.tile(hgrn_norm, (1, NH)), jnp.repeat(ssd_d, SP, axis=1), ssd_norm, g1, 1.0 + sc2,
                               sh2, norm_ffn, zrow], axis=0)
    ymix, ylat, h1, u2 = _mix_out(o_f, o_b, p_main, y_f, y_b, xa, xl, vec_mix, w_out_f)
    gate, up, act = _ffn_up(u2, wg_g, wu_g)
    vec_loss = jnp.concatenate([g2, final_norm.reshape(1, D)] + [zrow] * 6, axis=0)
    dh2, dffn, acc_loss = _ffn_down_loss(act, w_down_f, h1, tgt, vec_loss)

    core_arr = jnp.reshape(ic, (1,)).astype(jnp.int32)
    chip_arr = jnp.reshape(chip, (1,)).astype(jnp.int32)
    every = (0, 4)

    def pair_sum(gs, got, tag):
        return list(_pair_sum(gs, list(got), core_arr, "grads_pair_sum_" + tag))

    vec_ffn = jnp.concatenate([g1, 1.0 + sc2, norm_ffn] + [zrow] * 5, axis=0)
    dgate, dup, du2 = _ffn_bwd(dffn, w_down_f, gate, up, wg_g, wu_g)
    dh1, dylat, acc_ffn = _ffn_norm_bwd(du2, h1, ylat, dh2, vec_ffn)
    gw_down = _dw(act, dffn, "dw_down").reshape(4, FSL, D)
    ga1 = [_dw(dgate, u2, "dw_gate").reshape(4, FSL, D), _dw(dup, u2, "dw_up").reshape(4, FSL, D)]
    res = _mix_bwd(dylat, o_f, o_b, p_main, y_f, y_b, xa, vec_mix, w_out_f, comm=_comm_pair(ga1))
    (do, dgr, dys, dzr, dxs_skip, acc_mix), pair_a1 = res[:6], pair_sum(ga1, res[6:], "a1")
    ga2 = [gw_down, _dw(ymix, dylat, "dw_out").reshape(4, D // 2, D)]

    res = _hgrn_bwd(p_main, lbraw8, hs_f, do, 0, nb, None,
                    comm=[_comm_exchange(pair_a1, [every] * 2), _comm_pair(ga2)])
    (dq0, dff, dv0, dlb_f), recv_a, pair_a2 = res[:4], list(res[4:6]), pair_sum(ga2, res[6:], "a2")
    res = _hgrn_bwd(p_main, lbraw8, hs_b, do, 1, nb, (dq0, dv0), comm=_comm_exchange(pair_a2, [every] * 2))
    (dq, dfb, dv, dlb_b), recv_a = res[:4], recv_a + list(res[4:])
    pair_a, dests_a = pair_a1 + pair_a2, [every] * 4
    gw_in = [_dw_in([dq, dff], u_all, "dw_in_0"), _dw_in([dfb, dv], u_all, "dw_in_1"),
             _dw_in([dgr, dzr], u_all, "dw_in_2")]

    res = _ssd_bwd(xa, dts, alog, ss_f, dys, 0, nb, None, comm=_comm_pair(gw_in))
    (dxa0, ddts0, da_f), pair_b, dests_b = res[:3], pair_sum(gw_in, res[3:], "b"), [(0, 1), (1, 2), (2, 3)]
    res = _ssd_bwd(xa, dts, alog, ss_b, dys, 1, nb, (dxa0, ddts0), comm=_comm_exchange(pair_b, dests_b))
    (dxa, ddts, da_b), recv_b = res[:3], list(res[3:])
    dxbc, ddt, acc_conv, acc_dtb = _ssd_prep_bwd(p_main, p_dt, convp, dtb, dsl, dxa, dxs_skip, ddts, nb)
    gw_in.append(_dw_in([dxbc], u_all, "dw_in_3"))
    gw_in_dt = _dw(ddt, u_all, "dw_in_dt")
    gc = [gw_in[3], jnp.concatenate([g_[:, 0:WTAIL, :] for g_ in gw_in[1:]] + [gw_in_dt[None]], axis=0)]

    segs = [dq, dff, dfb, dv, dgr, dzr, dxbc]
    bmods_lat = jnp.concatenate([1.0 + sc1, norm_mix] + [zrow] * 6, axis=0)
    bmods_ctx = jnp.concatenate([1.0 + csc1, norm_mix] + [zrow] * 6, axis=0)
    res = _du_prenorm_bwd(segs, ddt, wi_main, wi_tail, xc, bmods_ctx, None, n_lat, TB, "du_ctx", comm=_comm_pair(gc))
    acc_ctx, pair_c, dests_c = res[0], pair_sum(gc, res[1:], "c"), [(3, 4), every]
    res = _du_prenorm_bwd(segs, ddt, wi_main, wi_tail, xl, bmods_lat, dh1, 0, 512, "du_lat",
                          comm=_comm_exchange(pair_c, dests_c))
    (grad_x, acc_lat), recv_c = res[:2], list(res[2:])

    mine = _chip_sum(pair_b + pair_c + pair_a, recv_b + recv_c + recv_a, chip_arr, dests_b + dests_c + dests_a,
                     [0, 0, 0, 0, 1, 3, 4, 5, 2])
    dmod_lat = jnp.concatenate([acc_lat[0:2], acc_ffn[3:4], acc_ffn[0:2], acc_loss[0:1]], axis=0)
    misc = jnp.concatenate([(da_f + da_b)[0, :32], jnp.zeros((96,), F32), acc_dtb[0, :32], jnp.zeros((96,), F32),
                            jnp.sum(acc_loss[2]).reshape(1), jnp.zeros((D - 257,), F32)]).reshape(1, D)
    sv = jnp.concatenate([
        dmod_lat, acc_ctx[0:2], (acc_lat[2:3] + acc_ctx[2:3]), acc_ffn[2:3], acc_loss[1:2], acc_mix[2:3],
        acc_mix[0:1], acc_mix[1:2], dlb_f[0:1], dlb_b[0:1], acc_conv[0:6].reshape(12, D), misc,
        jnp.zeros((3, D), F32)], axis=0)
    res = _pair_swap(mine, sv)
    theirs, sv_all = res[:-1], res[-1].reshape(8, 32, D)
    whole = [jnp.concatenate([jnp.where(ic == 0, m_, t_), jnp.where(ic == 0, t_, m_)], axis=0)
             for m_, t_ in zip(mine, theirs)]
    g_w_in = lax.dynamic_slice(jnp.concatenate(whole[0:2], axis=0), (8 * chip, 0), (NSH, D))
    g_w_out = whole[2]
    g_w_gate = whole[3][:DFF // 4]
    g_w_up = whole[4][:DFF // 4]
    g_w_down = whole[5][:DFF // 4]
    ssum = _sum8(sv_all)
    dmod_rows = sv_all[:, 0:6].reshape(8, 6 * D)
    dmod_ctx_row = jnp.concatenate([ssum[6:8].reshape(1, 2 * D), jnp.zeros((1, 4 * D), F32)], axis=1)
    dmod_full = jnp.concatenate([dmod_rows, dmod_ctx_row, jnp.zeros((7, 6 * D), F32)], axis=0)
    grad_b_ada = jnp.sum(dmod_full, axis=0, keepdims=True)
    dmod_shard = lax.dynamic_slice(dmod_full, (0, chip * ncol_ada), (16, ncol_ada))
    g_w_ada, da_part = _ada_bwd(araw, dmod_shard, w_ada[0])
    da_all = _allgather8(da_part, "ada_ctx_gather").reshape(8, 16, D)[0::2, 8]

    big = {}
    for nm, w_, m_, v_, g_ in (("w_ada", w_ada, m_w_ada, v_w_ada, g_w_ada), ("w_in", w_in, m_w_in, v_w_in, g_w_in),
                               ("w_out", w_out, m_w_out, v_w_out, g_w_out),
                               ("w_gate", w_gate, m_w_gate, v_w_gate, g_w_gate),
                               ("w_up", w_up, m_w_up, v_w_up, g_w_up),
                               ("w_down", w_down, m_w_down, v_w_down, g_w_down)):
        if nm in ("w_in", "w_gate", "w_up"):
            big[nm] = tuple(tr(t) for t in (g_[None],) + tuple(_adamw(tr(w_), tr(m_), tr(v_), g_, "adamw_" + nm)))
        else:
            big[nm] = (g_[None],) + tuple(_adamw(w_, m_, v_, g_, "adamw_" + nm))
    cc = c_ctx.reshape(1, D)
    grad_c_ctx = (jnp.sum(da_all, axis=0, keepdims=True) * _dsilu(cc)).reshape(D)

    grad_norm_mix, grad_norm_ffn, grad_final_norm = ssum[8:9], ssum[9:10], ssum[10].reshape(D)
    grad_ssd_norm = ssum[11:12]
    grad_hgrn_norm = jnp.sum(ssum[12].reshape(NH, HF), axis=0, keepdims=True)
    grad_ssd_d = jnp.sum(ssum[13].reshape(SHEADS, SP), axis=1).reshape(1, SHEADS)
    lb_full = _sig(lbraw_full[0:2] - lbraw_full[2:4])
    dr0 = ssum[14:16] * lb_full * (1.0 - lb_full)
    grad_lb_full = jnp.stack([dr0, -dr0], axis=0)
    grad_lb = lax.dynamic_slice(grad_lb_full, (0, 0, chip * 256), (2, 2, 256))
    grad_conv_w = lax.dynamic_slice(ssum[16:26].reshape(KCONV, 2048), (0, chip * 512), (KCONV, 512)).reshape(1, KCONV, 512)
    grad_conv_b = ssum[26:28].reshape(1, 2048)
    a_val = -jnp.exp(ssd_a_log)
    grad_a_log = ssum[28, 0:32].reshape(1, 2, SHEADS) * a_val
    grad_dt_bias = ssum[28, 128:160].reshape(1, 2, SHEADS)
    loss = ssum[28, 256]

    small_w = [c_ctx, b_ada, norm_mix, conv_w, conv_b, ssd_a_log, ssd_dt_bias, ssd_d, ssd_norm, hgrn_lb_raw,
               hgrn_norm, norm_ffn, final_norm]
    small_m = [m_c_ctx, m_b_ada, m_norm_mix, m_conv_w, m_conv_b, m_ssd_a_log, m_ssd_dt_bias, m_ssd_d, m_ssd_norm,
               m_hgrn_lb_raw, m_hgrn_norm, m_norm_ffn, m_final_norm]
    small_v = [v_c_ctx, v_b_ada, v_norm_mix, v_conv_w, v_conv_b, v_ssd_a_log, v_ssd_dt_bias, v_ssd_d, v_ssd_norm,
               v_hgrn_lb_raw, v_hgrn_norm, v_norm_ffn, v_final_norm]
    small_g = [grad_c_ctx, grad_b_ada, grad_norm_mix, grad_conv_w, grad_conv_b, grad_a_log, grad_dt_bias, grad_ssd_d,
               grad_ssd_norm, grad_lb, grad_hgrn_norm, grad_norm_ffn, grad_final_norm]
    nrows = [-(-a.size // D) for a in small_w]
    packs = lambda lst: jnp.concatenate([_rows(a, r) for a, r in zip(lst, nrows)]
                                        + [jnp.zeros((24 - sum(nrows), D), F32)], axis=0)
    sd, sm, svv = _adamw(packs(small_w), packs(small_m), packs(small_v), packs(small_g), "adamw_small")

    def unpack(p):
        out, r0 = [], 0
        for a, r in zip(small_w, nrows):
            out.append(p[r0:r0 + r].reshape(-1)[:a.size].reshape(a.shape))
            r0 += r
        return out

    sd, sm, svv = unpack(sd), unpack(sm), unpack(svv)

    order = ["c_ctx", "w_ada", "b_ada", "norm_mix", "w_in", "conv_w", "conv_b", "ssd_a_log", "ssd_dt_bias", "ssd_d",
             "ssd_norm", "hgrn_lb_raw", "hgrn_norm", "w_out", "norm_ffn", "w_gate", "w_up", "w_down", "final_norm"]
    small_names = ["c_ctx", "b_ada", "norm_mix", "conv_w", "conv_b", "ssd_a_log", "ssd_dt_bias", "ssd_d", "ssd_norm",
                   "hgrn_lb_raw", "hgrn_norm", "norm_ffn", "final_norm"]
    table = dict(big)
    for k, nm in enumerate(small_names):
        table[nm] = (small_g[k].reshape(small_w[k].shape), sd[k], sm[k], svv[k])
    grads = [table[nm][0] for nm in order]
    deltas = [table[nm][1] for nm in order]
    new_m = [table[nm][2] for nm in order]
    new_v = [table[nm][3] for nm in order]
    return (loss, grad_x[None], *grads, *deltas, *new_m, *new_v)
```

```python
import functools
import math

import jax
import jax.numpy as jnp
from jax import lax
from jax.experimental import pallas as pl
from jax.experimental.pallas import tpu as pltpu

F32 = jnp.float32
BF16 = jnp.bfloat16
MXU_DTYPE = jnp.bfloat16
_INTERPRET = False

D = 1024
NH, HF = 8, 128
HC = 64
SC = 128
SN = 128
SHEADS, SP = 16, 64
GRID_W = 64
KCONV = 5
DFF = 2816
FSL = 768
DFFP = 4 * FSL
NIN = 8224
TB = 256
EPS = 1e-6
LR, B1, B2, AEPS, WD, STEP = 0.001, 0.9, 0.999, 1e-08, 0.01, 10
MESH_ID = pl.DeviceIdType.MESH
NSH = NIN // 4
WSL = 2048
WTAIL = 128


def _pcall(body, *, name, out_shape, grid=(), in_specs=None, out_specs=None, scratch=(), sem=None,
           vmem_mb=None, aliases=None):
    params = {}
    if sem is not None:
        params["dimension_semantics"] = sem
    if vmem_mb is not None:
        params["vmem_limit_bytes"] = vmem_mb << 20
    kw = dict(name=name, out_shape=out_shape, scratch_shapes=list(scratch),
              input_output_aliases=aliases or {}, compiler_params=pltpu.CompilerParams(**params),
              interpret=_INTERPRET)
    if grid:
        kw["grid"] = grid
    if in_specs is not None:
        kw["in_specs"] = in_specs
    if out_specs is not None:
        kw["out_specs"] = out_specs
    return pl.pallas_call(body, **kw)


def _mx(a):
    return a.astype(MXU_DTYPE)


def _dg(a, b, ca, cb):
    return lax.dot_general(_mx(a), _mx(b), (((ca,), (cb,)), ((), ())), preferred_element_type=F32)


def _nn(a, b):
    return _dg(a, b, 1, 0)


def _nt(a, b):
    return _dg(a, b, 1, 1)


def _tn(a, b):
    return _dg(a, b, 0, 0)


def _dot01(m, x, ways=3):
    f = lambda t: lax.dot_general(m, t, (((1,), (0,)), ((), ())), preferred_element_type=F32)
    hi = x.astype(BF16)
    r1 = x - hi.astype(F32)
    mid = r1.astype(BF16)
    if ways == 2:
        return f(hi) + f(mid)
    lo = (r1 - mid.astype(F32)).astype(BF16)
    return f(hi) + f(mid) + f(lo)


def _tri(n, upper):
    r = lax.broadcasted_iota(jnp.int32, (n, n), 0)
    c = lax.broadcasted_iota(jnp.int32, (n, n), 1)
    return (c >= r) if upper else (c <= r)


def _b01(mask):
    return jnp.where(mask, 1.0, 0.0).astype(BF16)


def _sig(x):
    return jax.nn.sigmoid(x)


def _silu(x):
    return x * _sig(x)


def _dsilu(x):
    s = _sig(x)
    return s * (1.0 + x * (1.0 - s))


def _softplus(x):
    return jnp.maximum(x, 0.0) + jnp.log(1.0 + jnp.exp(-jnp.abs(x)))


def _rowsum(x):
    return jnp.sum(x, axis=1, keepdims=True)


def _colsum(x):
    return jnp.sum(x, axis=0, keepdims=True)


def _full(shape):
    return pl.BlockSpec(shape, lambda *_: (0,) * len(shape))


def _allgather8_phases(x_ref, out_ref, send_sems, recv_sems, local_sem):
    m_per = x_ref.shape[0]
    x, y, c = lax.axis_index("x"), lax.axis_index("y"), lax.axis_index("c")
    me, sibling = (x, y, c), (x, y, 1 - c)
    chips = [(1 - x, y), (x, 1 - y), (1 - x, 1 - y)]

    def rows(px, py, pc):
        return out_ref.at[pl.ds((4 * px + 2 * py + pc) * m_per, m_per), :]

    def copy(k, block, to, src=None):
        return pltpu.make_async_remote_copy(
            src_ref=rows(*block) if src is None else src, dst_ref=rows(*block),
            send_sem=send_sems.at[k], recv_sem=recv_sems.at[k], device_id=to, device_id_type=MESH_ID)

    mine = pltpu.make_async_copy(x_ref, rows(*me), local_sem)
    first = [copy(0, me, sibling, src=x_ref)]
    first += [copy(1 + j, me, (*chip, c), src=x_ref) for j, chip in enumerate(chips)]
    passed = [copy(4 + j, (*chip, c), sibling) for j, chip in enumerate(chips)]

    def start():
        mine.start()
        for cp in first:
            cp.start()

    def forward():
        for j, chip in enumerate(chips):
            copy(1 + j, (*chip, c), me).wait_recv()
            passed[j].start()

    def finish():
        copy(0, sibling, me).wait_recv()
        for j, chip in enumerate(chips):
            copy(4 + j, (*chip, 1 - c), me).wait_recv()
        for cp in first + passed:
            cp.wait_send()
        mine.wait()

    return start, forward, finish


def _allgather8_ops(x_ref, out_ref, send_sems, recv_sems, local_sem):
    for phase in _allgather8_phases(x_ref, out_ref, send_sems, recv_sems, local_sem):
        phase()


def _allgather8(v, name):
    m_per, n = v.shape
    return _pcall(
        functools.partial(_allgather8_ops), name=name, out_shape=jax.ShapeDtypeStruct((8 * m_per, n), v.dtype),
        in_specs=[pl.BlockSpec(memory_space=pltpu.VMEM)], out_specs=pl.BlockSpec(memory_space=pltpu.VMEM),
        scratch=list(_AG8_SEMS),
    )(v)


_AG8_SEMS = [pltpu.SemaphoreType.DMA((7,)), pltpu.SemaphoreType.DMA((7,)), pltpu.SemaphoreType.DMA]


def _prologue(pack, cc_row, w_ada, b_shard, shards):
    n = len(shards)
    ncol = w_ada.shape[1]

    def body(pack_ref, cc_ref, w_ref, b_ref, *refs):
        ins = refs[:n]
        gath_ref, araw_ref, mod_ref = refs[n:n + 3]
        outs = refs[n + 3:2 * n + 3]
        modsh, s1, r1, l1, s2, r2, l2, gs, gr = refs[2 * n + 3:]
        start, forward, finish = _gather_ops(ins, outs, gs, gr, relay=True)
        start()
        _allgather8_ops(pack_ref, gath_ref, s1, r1, l1)
        a = jnp.concatenate([gath_ref[8 * i:8 * i + 1, :] for i in range(8)] + [cc_ref[...], jnp.zeros((7, D), F32)],
                            axis=0)
        araw_ref[...] = a
        modsh[...] = _nn(_silu(a), w_ref[...]) + b_ref[...]
        _allgather8_ops(modsh, mod_ref, s2, r2, l2)
        forward()
        finish()

    vm = pl.BlockSpec(memory_space=pltpu.VMEM)
    anyspec = pl.BlockSpec(memory_space=pl.ANY)
    return _pcall(
        body, name="prologue",
        out_shape=(jax.ShapeDtypeStruct((64, D), F32), jax.ShapeDtypeStruct((16, D), F32),
                   jax.ShapeDtypeStruct((128, ncol), F32)) + _gather_out(shards),
        in_specs=[vm, vm, vm, vm] + [anyspec] * n, out_specs=(vm, vm, vm) + (anyspec,) * n,
        scratch=[pltpu.VMEM((16, ncol), F32)] + list(_AG8_SEMS) + list(_AG8_SEMS) + _gather_sems(n), vmem_mb=40,
    )(pack, cc_row, w_ada, b_shard, *shards)


def _gather_ops(ins, outs, send_sems, recv_sems, relay=False):
    n = len(ins)
    x, y, c = lax.axis_index("x"), lax.axis_index("y"), lax.axis_index("c")
    me, sibling = (x, y, c), (x, y, 1 - c)
    chips = [(1 - x, y), (x, 1 - y), (1 - x, 1 - y)]
    direct = 2 if relay else 3

    def part(a, px, py, pc, quarter=None):
        half = ins[a].shape[0] // 2
        if quarter is None:
            return outs[a].at[2 * px + py, pl.ds(pc * half, half), :]
        return outs[a].at[2 * px + py, pl.ds(pc * half + quarter * (half // 2), half // 2), :]

    def copy(a, k, block, to, src=None, quarter=None):
        return pltpu.make_async_remote_copy(
            src_ref=part(a, *block, quarter) if src is None else src, dst_ref=part(a, *block, quarter),
            send_sem=send_sems.at[8 * a + k], recv_sem=recv_sems.at[8 * a + k], device_id=to,
            device_id_type=MESH_ID)

    def first(a, j):
        half = ins[a].shape[0] // 2
        return copy(a, j, me, (*chips[j], c), src=ins[a].at[pl.ds(c * half, half), :])

    relayed = lambda a, q: copy(a, 6 + q, (*chips[q], c), (*chips[1 - q], c), quarter=q)

    def start():
        for a in range(n):
            for j in range(direct):
                first(a, j).start()

    def forward():
        for a in range(n):
            for j in range(direct):
                copy(a, j, (*chips[j], c), me).wait_recv()
                copy(a, 3 + j, (*chips[j], c), sibling).start()
                if relay:
                    relayed(a, j).start()
            if relay:
                for q in range(2):
                    copy(a, 6 + q, (*chips[2], c), me, quarter=q).wait_recv()
                copy(a, 5, (*chips[2], c), sibling).start()

    def finish():
        for a in range(n):
            for j, chip in enumerate(chips):
                copy(a, 3 + j, (*chip, 1 - c), me).wait_recv()
        for a in range(n):
            for j, chip in enumerate(chips):
                if j < direct:
                    first(a, j).wait_send()
                    if relay:
                        relayed(a, j).wait_send()
                copy(a, 3 + j, (*chip, c), sibling).wait_send()

    return start, forward, finish


def _gather_out(shards):
    return tuple(jax.ShapeDtypeStruct((4,) + s_.shape, s_.dtype) for s_ in shards)


def _gather_sems(n):
    return [pltpu.SemaphoreType.DMA((8 * n,)), pltpu.SemaphoreType.DMA((8 * n,))]


def _pair_ops(ins, outs, send_sems, recv_sems):
    x, y, c = lax.axis_index("x"), lax.axis_index("y"), lax.axis_index("c")
    cps = []
    for a in range(len(ins)):
        half = ins[a].shape[1] // 2
        cps.append(pltpu.make_async_remote_copy(
            src_ref=ins[a].at[:, pl.ds((1 - c) * half, half), :], dst_ref=outs[a], send_sem=send_sems.at[a],
            recv_sem=recv_sems.at[a], device_id=(x, y, 1 - c), device_id_type=MESH_ID))

    def start():
        for cp in cps:
            cp.start()

    def finish():
        for cp in cps:
            cp.wait()

    return start, finish


def _comm_pair(gs):
    n = len(gs)
    return (list(gs), tuple(jax.ShapeDtypeStruct((g.shape[0], g.shape[1] // 2, g.shape[2]), g.dtype) for g in gs),
            [pltpu.SemaphoreType.DMA((n,)), pltpu.SemaphoreType.DMA((n,))], _pair_ops)


def _exchange_ops(ins, outs, send_sems, recv_sems, dests):
    x, y, c = lax.axis_index("x"), lax.axis_index("y"), lax.axis_index("c")
    mine = 2 * x + y
    chips = [(1 - x, y), (x, 1 - y), (1 - x, 1 - y)]

    def each(fn):
        for a in range(len(ins)):
            lo, hi = dests[a]
            for j, (px, py) in enumerate(chips):
                q = 2 * px + py
                cp = pltpu.make_async_remote_copy(
                    src_ref=ins[a].at[jnp.clip(q - lo, 0, hi - lo - 1)], dst_ref=outs[a].at[j],
                    send_sem=send_sems.at[3 * a + j], recv_sem=recv_sems.at[3 * a + j], device_id=(px, py, c),
                    device_id_type=MESH_ID)
                fn(cp, (q >= lo) & (q < hi), (mine >= lo) & (mine < hi), (lo, hi) == (0, 4))

    def start():
        def go(cp, send_ok, recv_ok, always):
            if always:
                cp.start()
            else:
                pl.when(send_ok)(cp.start)
        each(go)

    def finish():
        def go(cp, send_ok, recv_ok, always):
            if always:
                cp.wait()
            else:
                pl.when(send_ok)(cp.wait_send)
                pl.when(recv_ok)(cp.wait_recv)
        each(go)

    return start, finish


def _comm_exchange(hs, dests):
    n = len(hs)
    return (list(hs), tuple(jax.ShapeDtypeStruct((3,) + h.shape[1:], h.dtype) for h in hs),
            [pltpu.SemaphoreType.DMA((3 * n,)), pltpu.SemaphoreType.DMA((3 * n,))],
            lambda i, o, s, r: _exchange_ops(i, o, s, r, dests))


def _comm_gather(shards, relay=False):
    return (list(shards), _gather_out(shards), _gather_sems(len(shards)),
            lambda i, o, s, r: _gather_ops(i, o, s, r, relay))


def _carry(call, comm, steps):
    if comm is None:
        return call
    if isinstance(comm, list):
        for one in comm:
            call = _carry(call, one, steps)
        return call
    arrays, out_shape, sems, make = comm
    n, n_in, n_out = len(arrays), len(call["args"]), len(call["out_shape"])
    body = call["body"]

    def wrapped(*refs):
        base_in, cin = refs[:n_in], refs[n_in:n_in + n]
        rest = refs[n_in + n:]
        base_out, cout, scr = rest[:n_out], rest[n_out:n_out + n], rest[n_out + n:]
        ops = make(cin, cout, scr[-2], scr[-1])
        when = steps()
        pl.when(when[0])(ops[0])
        if len(ops) == 3 and len(when) == 3:
            pl.when(when[2])(ops[1])
        body(*base_in, *base_out, *scr[:-2])
        if len(ops) == 3 and len(when) == 2:
            pl.when(when[1])(ops[1])
        pl.when(when[1])(ops[-1])

    anyspec = pl.BlockSpec(memory_space=pl.ANY)
    return dict(call, body=wrapped, args=list(call["args"]) + arrays,
                in_specs=list(call["in_specs"]) + [anyspec] * n,
                out_shape=tuple(call["out_shape"]) + tuple(out_shape),
                out_specs=tuple(call["out_specs"]) + (anyspec,) * n,
                scratch=list(call["scratch"]) + sems)


def _run(call):
    args = call.pop("args")
    body = call.pop("body")
    return _pcall(body, **call)(*args)


def _pair_swap(rs, sv):
    n = len(rs)

    def body(sv_ref, *refs):
        ins, outs, got_ref = refs[:n], refs[n:2 * n], refs[2 * n]
        send_sems, recv_sems, s1, r1, l1 = refs[2 * n + 1:]
        x, y, c = lax.axis_index("x"), lax.axis_index("y"), lax.axis_index("c")
        cps = [pltpu.make_async_remote_copy(
            src_ref=ins[a], dst_ref=outs[a], send_sem=send_sems.at[a], recv_sem=recv_sems.at[a],
            device_id=(x, y, 1 - c), device_id_type=MESH_ID) for a in range(n)]
        for cp in cps:
            cp.start()
        _allgather8_ops(sv_ref, got_ref, s1, r1, l1)
        for cp in cps:
            cp.wait()

    vm, anyspec = pl.BlockSpec(memory_space=pltpu.VMEM), pl.BlockSpec(memory_space=pl.ANY)
    return _pcall(
        body, name="grads_pair_swap",
        out_shape=tuple(jax.ShapeDtypeStruct(r.shape, r.dtype) for r in rs)
        + (jax.ShapeDtypeStruct((8 * sv.shape[0], sv.shape[1]), sv.dtype),),
        in_specs=[vm] + [anyspec] * n, out_specs=(anyspec,) * n + (vm,),
        scratch=[pltpu.SemaphoreType.DMA((n,)), pltpu.SemaphoreType.DMA((n,))] + list(_AG8_SEMS),
    )(sv, *rs)


SUM_STEPS = 4


def _pair_sum(gs, recvs, core, name):
    n = len(gs)

    def body(c_ref, *refs):
        for a in range(n):
            refs[2 * n + a][...] = (refs[a][...].astype(F32) + refs[n + a][...].astype(F32)).astype(refs[2 * n + a].dtype)

    blk = lambda g: (g.shape[0], g.shape[1] // (2 * SUM_STEPS), g.shape[2])
    return pl.pallas_call(
        body, name=name,
        out_shape=tuple(jax.ShapeDtypeStruct((g.shape[0], g.shape[1] // 2, g.shape[2]), g.dtype) for g in gs),
        grid_spec=pltpu.PrefetchScalarGridSpec(
            num_scalar_prefetch=1, grid=(SUM_STEPS,),
            in_specs=[pl.BlockSpec(blk(g), lambda i, cr: (0, cr[0] * SUM_STEPS + i, 0)) for g in gs]
            + [pl.BlockSpec(blk(g), lambda i, cr: (0, i, 0)) for g in gs],
            out_specs=tuple(pl.BlockSpec(blk(g), lambda i, cr: (0, i, 0)) for g in gs)),
        compiler_params=pltpu.CompilerParams(vmem_limit_bytes=40 << 20), interpret=_INTERPRET,
    )(core, *gs, *recvs)


def _chip_sum(hs, recvs, chip, dests, slots):
    n = len(hs)
    nout = max(slots) + 1
    first = [slots.index(o) for o in range(nout)]
    every = lambda d_: d_ == (0, 4)

    def own(d_):
        if every(d_):
            return lambda i, kr: (kr[0], i, 0)
        return lambda i, kr: (0, jnp.where(kr[0] == d_[0], i, 0), 0)

    def got(d_):
        if every(d_):
            return lambda i, kr: (0, i, 0)
        return lambda i, kr: (0, jnp.where(kr[0] == d_[0], i, 0), 0)

    def body(k_ref, *refs):
        for a in range(n):
            def emit(a=a):
                acc = refs[a][0].astype(F32)
                for j in range(3):
                    acc = acc + refs[n + a][j].astype(F32)
                refs[2 * n + slots[a]][...] = acc
            if every(dests[a]):
                emit()
            else:
                pl.when(k_ref[0] == dests[a][0])(emit)

    rb = lambda h: h.shape[1] // SUM_STEPS
    return pl.pallas_call(
        body, name="grads_chip_sum",
        out_shape=tuple(jax.ShapeDtypeStruct(hs[a].shape[1:], F32) for a in first),
        grid_spec=pltpu.PrefetchScalarGridSpec(
            num_scalar_prefetch=1, grid=(SUM_STEPS,),
            in_specs=[pl.BlockSpec((1, rb(h), h.shape[2]), own(d_)) for h, d_ in zip(hs, dests)]
            + [pl.BlockSpec((3, rb(h), h.shape[2]), got(d_)) for h, d_ in zip(hs, dests)],
            out_specs=tuple(pl.BlockSpec((rb(hs[a]), hs[a].shape[2]), lambda i, kr: (i, 0)) for a in first)),
        compiler_params=pltpu.CompilerParams(vmem_limit_bytes=40 << 20), interpret=_INTERPRET,
    )(chip, *hs, *recvs)


def _ada_bwd(araw, dmod, w):
    nblk = w.shape[1] // 512

    def body(a_ref, d_ref, w_ref, gw_ref, da_ref):
        j = pl.program_id(0)
        gw_ref[...] = _tn(_silu(a_ref[...]), d_ref[...])
        part = _nt(d_ref[...], w_ref[...])

        @pl.when(j == 0)
        def _():
            da_ref[...] = part

        @pl.when(j > 0)
        def _():
            da_ref[...] += part

    return _pcall(
        body, name="ada_bwd",
        out_shape=(jax.ShapeDtypeStruct(w.shape, F32), jax.ShapeDtypeStruct((16, D), F32)), grid=(nblk,),
        in_specs=[_full((16, D)), pl.BlockSpec((16, 512), lambda j: (0, j)), pl.BlockSpec((D, 512), lambda j: (0, j))],
        out_specs=(pl.BlockSpec((D, 512), lambda j: (0, j)), _full((16, D))), sem=("arbitrary",),
    )(araw, dmod, w)


def _w_specs():
    return [pl.BlockSpec((None, 2 * D, D), lambda j, i: (j, 0, 0)),
            pl.BlockSpec((None, WTAIL, D), lambda j, i: (jnp.maximum(j - 1, 0), 0, 0)),
            pl.BlockSpec((None, WTAIL, D), lambda j, i: (3, 0, 0))]


def _inproj(xin, mods, wi_main, wi_tail, t_total, tb, blk_off, prev, name, comm=None):
    n = xin.shape[0]
    nt = n // tb
    nslab = 4

    def body(x_ref, mod_ref, w_ref, wb_ref, wdt_ref, *rest):
        p_ref, pdt_ref, u_ref, uscr = rest[-4:]
        j, i = pl.program_id(0), pl.program_id(1)
        rows = pl.ds(pl.multiple_of(i * tb, tb), tb)

        @pl.when(j == 0)
        def _():
            xv = x_ref[...]
            r = lax.rsqrt(jnp.mean(xv * xv, axis=1, keepdims=True) + EPS)
            u = (xv * r * mod_ref[2:3, :]) * mod_ref[0:1, :] + mod_ref[1:2, :]
            ub = u.astype(MXU_DTYPE)
            uscr[rows, :] = ub
            u_ref[...] = ub
            pdt_ref[...] = _nt(ub, wdt_ref[...])

        ub = uscr[rows, :]
        pv = _nt(ub, w_ref[0:D, :])
        p_ref[:, D:2 * D] = _nt(ub, w_ref[D:2 * D, :]).astype(p_ref.dtype)

        @pl.when(j == 0)
        def _():
            p_ref[:, 0:D] = pv.astype(p_ref.dtype)

        @pl.when(j > 0)
        def _():
            head = pv[:, 0:WTAIL] + _nt(ub, wb_ref[...])
            p_ref[:, 0:D] = jnp.concatenate([head, pv[:, WTAIL:]], axis=1).astype(p_ref.dtype)

    once = lambda j, i: (jnp.where(j == 0, i, nt - 1) + blk_off, 0)
    in_specs = [pl.BlockSpec((tb, D), lambda j, i: (jnp.where(j == 0, i, nt - 1), 0)), _full((8, D))] + _w_specs()
    args = [xin, mods, wi_main, wi_tail, wi_tail]
    aliases = None
    if prev is not None:
        in_specs += [pl.BlockSpec(memory_space=pl.ANY)] * 3
        args += list(prev)
        aliases = {5: 0, 6: 1, 7: 2}
    call = dict(
        body=body, args=args, name=name,
        out_shape=(jax.ShapeDtypeStruct((t_total, nslab * 2 * D), MXU_DTYPE),
                   jax.ShapeDtypeStruct((t_total, 128), F32), jax.ShapeDtypeStruct((t_total, D), MXU_DTYPE)),
        grid=(nslab, nt), in_specs=in_specs,
        out_specs=(pl.BlockSpec((tb, 2 * D), lambda j, i: (i + blk_off, j)), pl.BlockSpec((tb, 128), once),
                   pl.BlockSpec((tb, D), once)),
        scratch=[pltpu.VMEM((n, D), MXU_DTYPE)], sem=("arbitrary", "arbitrary"), vmem_mb=56, aliases=aliases)
    steps = lambda: ((pl.program_id(0) == 0) & (pl.program_id(1) == 0),
                     (pl.program_id(0) == nslab - 1) & (pl.program_id(1) == nt - 1),
                     (pl.program_id(0) == nslab - 1) & (pl.program_id(1) == nt // 2))
    return _run(_carry(call, comm, steps))


def _blk(s, nb, rev):
    return jnp.where(s == 0, nb - 1, (nb - 1 - s) if rev else (s - 1))


def _hgrn_gate(fr, lbraw_ref, d):
    lb = _sig(lbraw_ref[d:d + 1, :] - lbraw_ref[2 + d:3 + d, :])
    sg = _sig(fr)
    return lb, sg, lb + (1.0 - lb) * sg


def _hgrn_fwd(p_main, lbraw, nb, comm=None):
    t_total = p_main.shape[0]
    nch = TB // HC
    scale = HF ** -0.5
    hs_ = [slice(h * HF, (h + 1) * HF) for h in range(NH)]

    def prepare(d, q_ref, f_ref, v_ref, lb_ref):
        rev = d == 1
        mb = _tri(HC, rev)
        m01 = _b01(mb)
        order = list(reversed(range(nch)) if rev else range(nch))
        pre = {}
        for c in order:
            rows = slice(c * HC, (c + 1) * HC)
            _, _, f = _hgrn_gate(f_ref[rows, :].astype(F32), lb_ref, d)
            k = 1.0 - f
            cum = _dot01(m01, jnp.log(f))
            tot = cum[0:1, :] if rev else cum[HC - 1:HC, :]
            qd = _silu(q_ref[rows, :].astype(F32)) * scale * jnp.exp(cum)
            ki = k * jnp.exp(-cum)
            etot = jnp.exp(tot)
            pre[c] = (_mx(qd), _mx(ki), _mx(ki * etot), _mx(v_ref[rows, :]), etot)
        scs = {c: [_nt(pre[c][0][:, cs], pre[c][1][:, cs]) for cs in hs_] for c in order}
        upd = {c: [_tn(pre[c][3][:, cs], pre[c][2][:, cs]) for cs in hs_] for c in order}
        intra = {c: [_nn(jnp.where(mb, scs[c][h], 0.0), pre[c][3][:, cs]) for h, cs in enumerate(hs_)] for c in order}
        return order, pre, upd, intra

    def chain_step(prep, n, o_ref, sp_ref, st):
        order, pre, upd, intra = prep
        c = order[n]
        rows = slice(c * HC, (c + 1) * HC)
        qdb, etot = pre[c][0], pre[c][4]
        for h, cs in enumerate(hs_):
            sth = st[h]
            stb = sth.astype(sp_ref.dtype)
            sp_ref[c, h] = stb
            o_ref[rows, cs] = (intra[c][h] + _nt(qdb[:, cs], stb)).astype(o_ref.dtype)
            st[h] = sth * etot[:, cs] + upd[c][h]

    def body(q0, f0, v0, q1, f1, v1, lb_ref, o0, sp0, o1, sp1, st0, st1):
        @pl.when(pl.program_id(0) == 0)
        def _():
            st0[...] = jnp.zeros_like(st0)
            st1[...] = jnp.zeros_like(st1)

        prep0 = prepare(0, q0, f0, v0, lb_ref)
        prep1 = prepare(1, q1, f1, v1, lb_ref)
        for n in range(nch):
            chain_step(prep0, n, o0, sp0, st0)
            chain_step(prep1, n, o1, sp1, st1)

    col = lambda rev, j: (lambda s: (_blk(s, nb, rev), j))
    state = lambda rev: pl.BlockSpec((nch, NH, HF, HF), lambda s: (_blk(s, nb, rev), 0, 0, 0))
    o_shape = jax.ShapeDtypeStruct((t_total, D), MXU_DTYPE)
    s_shape = jax.ShapeDtypeStruct((nch * nb, NH, HF, HF), MXU_DTYPE)
    call = dict(
        body=body, args=[p_main] * 6 + [lbraw], name="hgrn_fwd",
        out_shape=(o_shape, s_shape, o_shape, s_shape), grid=(nb,),
        in_specs=[pl.BlockSpec((TB, D), col(False, 0)), pl.BlockSpec((TB, D), col(False, 1)),
                  pl.BlockSpec((TB, D), col(False, 3)), pl.BlockSpec((TB, D), col(True, 0)),
                  pl.BlockSpec((TB, D), col(True, 2)), pl.BlockSpec((TB, D), col(True, 3)), _full((8, D))],
        out_specs=(pl.BlockSpec((TB, D), col(False, 0)), state(False), pl.BlockSpec((TB, D), col(True, 0)),
                   state(True)),
        scratch=[pltpu.VMEM((NH, HF, HF), F32), pltpu.VMEM((NH, HF, HF), F32)], sem=("arbitrary",), vmem_mb=40)
    return _run(_carry(call, comm, lambda: (pl.program_id(0) == 0, pl.program_id(0) == nb - 1,
                                            pl.program_id(0) == nb - 3)))


def _hgrn_bwd(p_main, lbraw, sprev, do, d, nb, prev, comm=None):
    t_total = p_main.shape[0]
    rev = d == 1
    nch = TB // HC
    scale = HF ** -0.5
    last = prev is not None
    odt = MXU_DTYPE if last else F32

    def body(q_ref, f_ref, v_ref, lb_ref, sp_ref, do_ref, *rest):
        if last:
            dqp_ref, dvp_ref = rest[:2]
            rest = rest[2:]
        dq_ref, df_ref, dv_ref, dlb_ref, dst = rest
        sp_id = pl.program_id(0)
        is_ctx = sp_id == nb - 1

        @pl.when(sp_id == 0)
        def _():
            dst[...] = jnp.zeros_like(dst)
            dlb_ref[...] = jnp.zeros_like(dlb_ref)

        mb = _tri(HC, rev)
        mbt = _tri(HC, not rev)
        m01 = _b01(mb)
        mt01 = _b01(mbt)
        order = list(range(nch) if rev else reversed(range(nch)))
        hs_ = [slice(h * HF, (h + 1) * HF) for h in range(NH)]
        pre = {}
        for c in order:
            rows = slice(c * HC, (c + 1) * HC)
            lb, sg, f = _hgrn_gate(f_ref[rows, :].astype(F32), lb_ref, d)
            k = 1.0 - f
            cum = _dot01(m01, jnp.log(f))
            tot = cum[0:1, :] if rev else cum[HC - 1:HC, :]
            e = jnp.exp(cum)
            ei = jnp.exp(-cum)
            etot = jnp.exp(tot)
            ee = ei * etot
            qraw = q_ref[rows, :].astype(F32)
            sq = _sig(qraw)
            qd = qraw * sq * scale * e
            ki = k * ei
            ke = k * ee
            dov = jnp.where(is_ctx, 0.0, do_ref[rows, :].astype(F32))
            pre[c] = dict(lb=lb, sg=sg, f=f, e=e, ei=ei, ee=ee, etot=etot, qd=qd, ki=ki, ke=ke,
                          dsq=sq * (1.0 + qraw * (1.0 - sq)),
                          qdb=_mx(qd), kib=_mx(ki), keb=_mx(ke), vb=_mx(v_ref[rows, :]), dob=_mx(dov))
        units = [(c, h) for c in order for h in range(NH)]
        col = lambda u, key: pre[u[0]][key][:, hs_[u[1]]]
        pt = {u: jnp.where(mbt, _nt(col(u, "kib"), col(u, "qdb")), 0.0) for u in units}
        dp = {u: jnp.where(mb, _nt(col(u, "dob"), col(u, "vb")), 0.0) for u in units}
        dpt = {u: jnp.where(mbt, _nt(col(u, "vb"), col(u, "dob")), 0.0) for u in units}
        dv_i = {u: _nn(pt[u], col(u, "dob")) for u in units}
        dqd_ = {u: _nn(dp[u], col(u, "kib")) + _nn(col(u, "dob"), sp_ref[u[0], u[1]]) for u in units}
        dki_ = {u: _nn(dpt[u], col(u, "qdb")) for u in units}
        dsl = {u: _tn(col(u, "dob"), col(u, "qdb")) for u in units}
        for c in order:
            rows = slice(c * HC, (c + 1) * HC)
            p = pre[c]
            dv_l, dke_l, dtot_l = [], [], []
            for h, cs in enumerate(hs_):
                dso = dst[h]
                dsob = _mx(dso)
                dv_l.append(dv_i[(c, h)] + _nt(p["keb"][:, cs], dsob))
                dke_l.append(_nn(p["vb"][:, cs], dsob))
                dtot_l.append(_colsum(dso * sp_ref[c, h].astype(F32)) * p["etot"][:, cs])
                dst[h] = dso * p["etot"][:, cs] + dsl[(c, h)]
            lb, sg, f, e, ei, ee, qd, ki, ke = (p[n_] for n_ in ("lb", "sg", "f", "e", "ei", "ee", "qd", "ki", "ke"))
            dqd = jnp.concatenate([dqd_[(c, h)] for h in range(NH)], axis=1)
            dki = jnp.concatenate([dki_[(c, h)] for h in range(NH)], axis=1)
            dke = jnp.concatenate(dke_l, axis=1)
            dcum = dqd * qd - dki * ki - dke * ke
            dtot = jnp.concatenate(dtot_l, axis=1) + _colsum(dke * ke)
            dk = dki * ei + dke * ee
            dlf = _dot01(mt01, dcum, ways=2) + dtot
            df = dlf / f - dk
            dlb_ref[0:1, :] += _colsum(df * (1.0 - sg))
            dfr = df * (1.0 - lb) * sg * (1.0 - sg)
            dq = dqd * e * scale * p["dsq"]
            dv = jnp.concatenate(dv_l, axis=1)
            if last:
                dq = dq + dqp_ref[rows, :]
                dv = dv + dvp_ref[rows, :]
            dq_ref[rows, :] = dq.astype(odt)
            dv_ref[rows, :] = dv.astype(odt)
            df_ref[rows, :] = dfr.astype(MXU_DTYPE)

    blk = lambda s: _blk(nb - 1 - s, nb, rev)
    col = lambda j: (lambda s: (blk(s), j))
    in_specs = [pl.BlockSpec((TB, D), col(0)), pl.BlockSpec((TB, D), col(1 + d)), pl.BlockSpec((TB, D), col(3)),
                _full((8, D)), pl.BlockSpec((nch, NH, HF, HF), lambda s: (blk(s), 0, 0, 0)),
                pl.BlockSpec((TB, D), lambda s: (jnp.minimum(blk(s), nb - 2), 0))]
    args = [p_main, p_main, p_main, lbraw, sprev, do]
    if last:
        in_specs += [pl.BlockSpec((TB, D), col(0))] * 2
        args += list(prev)
    call = dict(
        body=body, args=args, name=f"hgrn_bwd_{d}",
        out_shape=(jax.ShapeDtypeStruct((t_total, D), odt), jax.ShapeDtypeStruct((t_total, D), MXU_DTYPE),
                   jax.ShapeDtypeStruct((t_total, D), odt), jax.ShapeDtypeStruct((8, D), F32)),
        grid=(nb,), in_specs=in_specs,
        out_specs=(pl.BlockSpec((TB, D), col(0)), pl.BlockSpec((TB, D), col(0)), pl.BlockSpec((TB, D), col(0)),
                   _full((8, D))),
        scratch=[pltpu.VMEM((NH, HF, HF), F32)], sem=("arbitrary",), vmem_mb=48)
    return _run(_carry(call, comm, lambda: (pl.program_id(0) == 0, pl.program_id(0) == nb - 1)))


def _conv_masks(tb, is_ctx):
    seg = jnp.where(is_ctx, tb, GRID_W)
    pos = lax.broadcasted_iota(jnp.int32, (tb, 1), 0) & (seg - 1)
    return pos, seg


def _shift_rows(x, dshift, pos, seg):
    if dshift == 0:
        return x
    n = x.shape[0]
    rolled = pltpu.roll(x, (-dshift) % n, 0)
    ok = (pos + dshift >= 0) & (pos + dshift < seg)
    return jnp.where(ok, rolled, 0.0)


def _ssd_prep(p_main, p_dt, convp, dtb, nb):
    t_total = p_main.shape[0]

    def body(x_ref, dt_ref, cw_ref, dtb_ref, xa_ref, ds_ref, dts_ref):
        is_ctx = pl.program_id(0) == nb - 1
        pos, seg = _conv_masks(TB, is_ctx)
        xv = x_ref[...].astype(F32)
        acc = cw_ref[5:6, :] + cw_ref[2:3, :] * xv
        for kk in (0, 1, 3, 4):
            acc = acc + cw_ref[kk:kk + 1, :] * _shift_rows(xv, kk - 2, pos, seg)
        sg = _sig(acc)
        xa_ref[...] = (acc * sg).astype(xa_ref.dtype)
        ds_ref[...] = (sg * (1.0 + acc * (1.0 - sg))).astype(ds_ref.dtype)
        dts_ref[...] = _softplus(dt_ref[...] + dtb_ref[0:1, :])

    wide = pl.BlockSpec((TB, 2048), lambda i: (i, 0))
    return _pcall(
        body, name="ssd_prep",
        out_shape=(jax.ShapeDtypeStruct((t_total, 2048), MXU_DTYPE), jax.ShapeDtypeStruct((t_total, 2048), MXU_DTYPE),
                   jax.ShapeDtypeStruct((t_total, 128), F32)),
        grid=(nb,),
        in_specs=[pl.BlockSpec((TB, 2048), lambda i: (i, 3)), pl.BlockSpec((TB, 128), lambda i: (i, 0)),
                  _full((8, 2048)), _full((8, 128))],
        out_specs=(wide, wide, pl.BlockSpec((TB, 128), lambda i: (i, 0))),
        sem=("parallel",), vmem_mb=32,
    )(p_main, p_dt, convp, dtb)


def _ssd_prep_bwd(p_main, p_dt, convp, dtb, dsl, dxa, dxs_skip, ddts, nb):
    t_total = p_main.shape[0]

    def body(x_ref, dt_ref, cw_ref, dtb_ref, ds_ref, dxa_ref, dsk_ref, ddts_ref, dx_ref, ddt_ref, dcw_ref, ddtb_ref):
        i = pl.program_id(0)
        is_ctx = i == nb - 1

        @pl.when(i == 0)
        def _():
            dcw_ref[...] = jnp.zeros_like(dcw_ref)
            ddtb_ref[...] = jnp.zeros_like(ddtb_ref)

        pos, seg = _conv_masks(TB, is_ctx)
        xv = x_ref[...].astype(F32)
        dact = dxa_ref[...]
        dact = jnp.concatenate([dact[:, :D] + jnp.where(is_ctx, 0.0, dsk_ref[...].astype(F32)), dact[:, D:]], axis=1)
        dpre = dact * ds_ref[...].astype(F32)
        dxv = cw_ref[2:3, :] * dpre
        dcw_ref[2:3, :] += _colsum(xv * dpre)
        for kk in (0, 1, 3, 4):
            sdp = _shift_rows(dpre, 2 - kk, pos, seg)
            dxv = dxv + cw_ref[kk:kk + 1, :] * sdp
            dcw_ref[kk:kk + 1, :] += _colsum(xv * sdp)
        dx_ref[...] = dxv.astype(dx_ref.dtype)
        dcw_ref[5:6, :] += _colsum(dpre)
        draw = ddts_ref[...] * _sig(dt_ref[...] + dtb_ref[0:1, :])
        ddt_ref[...] = draw.astype(ddt_ref.dtype)
        ddtb_ref[0:1, :] += _colsum(draw)

    return _pcall(
        body, name="ssd_prep_bwd",
        out_shape=(jax.ShapeDtypeStruct((t_total, 2048), MXU_DTYPE), jax.ShapeDtypeStruct((t_total, 128), MXU_DTYPE),
                   jax.ShapeDtypeStruct((8, 2048), F32), jax.ShapeDtypeStruct((8, 128), F32)),
        grid=(nb,),
        in_specs=[pl.BlockSpec((TB, 2048), lambda i: (i, 3)), pl.BlockSpec((TB, 128), lambda i: (i, 0)),
                  _full((8, 2048)), _full((8, 128)), pl.BlockSpec((TB, 2048), lambda i: (i, 0)),
                  pl.BlockSpec((TB, 2048), lambda i: (i, 0)),
                  pl.BlockSpec((TB, D), lambda i: (jnp.minimum(i, nb - 2), 0)),
                  pl.BlockSpec((TB, 128), lambda i: (i, 0))],
        out_specs=(pl.BlockSpec((TB, 2048), lambda i: (i, 0)), pl.BlockSpec((TB, 128), lambda i: (i, 0)),
                   _full((8, 2048)), _full((8, 128))),
        sem=("arbitrary",), vmem_mb=40,
    )(p_main, p_dt, convp, dtb, dsl, dxa, dxs_skip, ddts)


def _dot2(x, m01):
    hi = x.astype(BF16)
    lo = (x - hi.astype(F32)).astype(BF16)
    f = lambda t: lax.dot_general(t, m01, (((1,), (0,)), ((), ())), preferred_element_type=F32)
    return f(hi) + f(lo)


def _head_lanes(c0, c1):
    p = lax.broadcasted_iota(jnp.int32, (128, 128), 0)
    l = lax.broadcasted_iota(jnp.int32, (128, 128), 1)
    return _b01(((l == c0) & (p < SP)) | ((l == c1) & (p >= SP)))


def _one_lane(col):
    return _b01(lax.broadcasted_iota(jnp.int32, (128, 128), 1) == col)


def _lane_pick(x, lane, col):
    return _rowsum(jnp.where(lane == col, x, 0.0))


def _ssd_chunk_common(dts, alog_ref, m01, rev):
    lane = lax.broadcasted_iota(jnp.int32, (1, 128), 1)
    arow = -jnp.exp(alog_ref[0:1, :])
    cum = _dot01(m01, dts * arow)
    tot = cum[0:1, :] if rev else cum[SC - 1:SC, :]
    return lane, arow, cum, cum.T, tot


def _ssd_fwd(xa, dts, alog, nb, comm=None):
    t_total = xa.shape[0]
    nch = TB // SC
    npair = SHEADS // 2

    def prepare(d, xa_ref, dts_ref, alog_ref):
        rev = d == 1
        mb = _tri(SC, rev)
        m01 = _b01(mb)
        lo = lax.broadcasted_iota(jnp.int32, (1, 128), 1) < SP
        rlo = lax.broadcasted_iota(jnp.int32, (128, 1), 0) < SP
        order = list(reversed(range(nch)) if rev else range(nch))
        pre = {}
        for c in order:
            rows = slice(c * SC, (c + 1) * SC)
            dts_c = dts_ref[rows, :]
            lane, arow, cum, cumt, tot = _ssd_chunk_common(dts_c, alog_ref, m01, rev)
            bgs = [_mx(xa_ref[rows, D + g * SN:D + (g + 1) * SN]) for g in range(4)]
            cgs = [_mx(xa_ref[rows, D + 512 + g * SN:D + 512 + (g + 1) * SN]) for g in range(4)]
            pairs = []
            for pr in range(npair):
                xs = xa_ref[rows, pr * 128:(pr + 1) * 128].astype(F32)
                cols = [16 * d + 2 * pr, 16 * d + 2 * pr + 1]
                cum_c = [_lane_pick(cum, lane, q) for q in cols]
                dt_c = [_lane_pick(dts_c, lane, q) for q in cols]
                tot_c = [_lane_pick(tot, lane, q) for q in cols]
                dtx = xs * jnp.where(lo, dt_c[0], dt_c[1])
                e1_pair = jnp.where(lo, jnp.exp(cum_c[0]), jnp.exp(cum_c[1]))
                e2_pair = jnp.where(lo, jnp.exp(tot_c[0] - cum_c[0]), jnp.exp(tot_c[1] - cum_c[1]))
                etot_col = jnp.where(rlo, jnp.exp(tot_c[0]), jnp.exp(tot_c[1]))
                decs = [jnp.where(mb, jnp.exp(cum_c[q] - cumt[cols[q]:cols[q] + 1, :]), 0.0) for q in range(2)]
                dtxq = [_mx(jnp.where(lo if q == 0 else ~lo, dtx, 0.0)) for q in range(2)]
                pairs.append(dict(e1=e1_pair, etot=etot_col, decs=decs, dtxq=dtxq, xe=_mx(dtx * e2_pair)))
            pre[c] = (bgs, cgs, pairs)
        gm = {(c, g): _nt(pre[c][1][g], pre[c][0][g]) for c in order for g in range(4)}
        upd = {(c, pr): _tn(pre[c][2][pr]["xe"], pre[c][0][pr // 2]) for c in order for pr in range(npair)}
        intra = {(c, pr): sum(_nn(gm[(c, pr // 2)] * pre[c][2][pr]["decs"][q], pre[c][2][pr]["dtxq"][q]) for q in range(2))
                 for c in order for pr in range(npair)}
        return order, pre, upd, intra

    def chain_step(prep, n, y_ref, sp_ref, st):
        order, pre, upd, intra = prep
        c = order[n]
        rows = slice(c * SC, (c + 1) * SC)
        bgs, cgs, pairs = pre[c]
        for pr in range(npair):
            stp = st[pr]
            stb = stp.astype(sp_ref.dtype)
            sp_ref[c, pr] = stb
            y_ref[rows, pr * 128:(pr + 1) * 128] = (
                intra[(c, pr)] + pairs[pr]["e1"] * _nt(cgs[pr // 2], stb)).astype(y_ref.dtype)
            st[pr] = stp * pairs[pr]["etot"] + upd[(c, pr)]

    def body(xa0, dts0, xa1, dts1, alog_ref, y0, sp0, y1, sp1, st0, st1):
        @pl.when(pl.program_id(0) == 0)
        def _():
            st0[...] = jnp.zeros_like(st0)
            st1[...] = jnp.zeros_like(st1)

        prep0 = prepare(0, xa0, dts0, alog_ref)
        prep1 = prepare(1, xa1, dts1, alog_ref)
        for n in range(nch):
            chain_step(prep0, n, y0, sp0, st0)
            chain_step(prep1, n, y1, sp1, st1)

    blk = lambda rev: (lambda s: (_blk(s, nb, rev), 0))
    state = lambda rev: pl.BlockSpec((nch, npair, 128, SN), lambda s: (_blk(s, nb, rev), 0, 0, 0))
    y_shape = jax.ShapeDtypeStruct((t_total, D), MXU_DTYPE)
    s_shape = jax.ShapeDtypeStruct((nch * nb, npair, 128, SN), MXU_DTYPE)
    call = dict(
        body=body, args=[xa, dts, xa, dts, alog], name="ssd_fwd",
        out_shape=(y_shape, s_shape, y_shape, s_shape), grid=(nb,),
        in_specs=[pl.BlockSpec((TB, 2048), blk(False)), pl.BlockSpec((TB, 128), blk(False)),
                  pl.BlockSpec((TB, 2048), blk(True)), pl.BlockSpec((TB, 128), blk(True)), _full((8, 128))],
        out_specs=(pl.BlockSpec((TB, D), blk(False)), state(False), pl.BlockSpec((TB, D), blk(True)), state(True)),
        scratch=[pltpu.VMEM((npair, 128, SN), F32), pltpu.VMEM((npair, 128, SN), F32)], sem=("arbitrary",),
        vmem_mb=40)
    return _run(_carry(call, comm, lambda: (pl.program_id(0) == 0, pl.program_id(0) == nb - 1,
                                            pl.program_id(0) == nb - 3)))


def _ssd_bwd(xa, dts, alog, sprev, dy, d, nb, prev, comm=None):
    t_total = xa.shape[0]
    rev = d == 1
    nch = TB // SC
    npair = SHEADS // 2
    last = prev is not None

    def body(xa_ref, dts_ref, alog_ref, sp_ref, dy_ref, *rest):
        if last:
            dxp_ref, ddp_ref = rest[:2]
            rest = rest[2:]
        dxa_ref, ddts_ref, da_ref, dst, zc_scr = rest
        sp_id = pl.program_id(0)
        is_ctx = sp_id == nb - 1

        @pl.when(sp_id == 0)
        def _():
            dst[...] = jnp.zeros_like(dst)
            da_ref[...] = jnp.zeros_like(da_ref)
            zc_scr[...] = jnp.zeros_like(zc_scr)

        mb = _tri(SC, rev)
        m01 = _b01(mb)
        mt01 = _b01(_tri(SC, not rev))
        lo = lax.broadcasted_iota(jnp.int32, (1, 128), 1) < SP
        rlo = lax.broadcasted_iota(jnp.int32, (128, 1), 0) < SP
        order = list(range(nch) if rev else reversed(range(nch)))
        pre = {}
        for c in order:
            rows = slice(c * SC, (c + 1) * SC)
            dts_c = dts_ref[rows, :]
            lane, arow, cum, cumt, tot = _ssd_chunk_common(dts_c, alog_ref, m01, rev)
            pairs = []
            for pr in range(npair):
                xs = xa_ref[rows, pr * 128:(pr + 1) * 128].astype(F32)
                dyp = jnp.where(is_ctx, 0.0, dy_ref[rows, pr * 128:(pr + 1) * 128].astype(F32))
                cols = [16 * d + 2 * pr, 16 * d + 2 * pr + 1]
                cum_c = [_lane_pick(cum, lane, q) for q in cols]
                dt_c = [_lane_pick(dts_c, lane, q) for q in cols]
                tot_c = [_lane_pick(tot, lane, q) for q in cols]
                e1_c = [jnp.exp(cum_c[q]) for q in range(2)]
                e2_c = [jnp.exp(tot_c[q] - cum_c[q]) for q in range(2)]
                etot_c = [jnp.exp(tot_c[q]) for q in range(2)]
                dt_pair = jnp.where(lo, dt_c[0], dt_c[1])
                e1_pair = jnp.where(lo, e1_c[0], e1_c[1])
                e2_pair = jnp.where(lo, e2_c[0], e2_c[1])
                dtx = xs * dt_pair
                decs = [jnp.where(mb, jnp.exp(cum_c[q] - cumt[cols[q]:cols[q] + 1, :]), 0.0) for q in range(2)]
                dyq = [_mx(jnp.where(lo if q == 0 else ~lo, dyp, 0.0)) for q in range(2)]
                pairs.append(dict(xs=xs, dyp=dyp, cols=cols, e1_c=e1_c, e2_c=e2_c, etot_c=etot_c, dt_pair=dt_pair,
                                  e2_pair=e2_pair, etot_col=jnp.where(rlo, etot_c[0], etot_c[1]), dtx=dtx,
                                  dtxb=_mx(dtx), xeb=_mx(dtx * e2_pair), dy0b=_mx(dyp * e1_pair), decs=decs, dyq=dyq))
            pre[c] = dict(lane=lane, arow=arow, dts=dts_c, pairs=pairs, cum=cum, tot=tot,
                          bgb=[_mx(xa_ref[rows, D + g * SN:D + (g + 1) * SN]) for g in range(4)],
                          cgb=[_mx(xa_ref[rows, D + 512 + g * SN:D + 512 + (g + 1) * SN]) for g in range(4)])
        units = [(c, pr) for c in order for pr in range(npair)]
        head_lanes = [_head_lanes(16 * d + 2 * pr, 16 * d + 2 * pr + 1) for pr in range(npair)]
        one_lane = {16 * d + h: _one_lane(16 * d + h) for h in range(SHEADS)}
        P = lambda u: pre[u[0]]["pairs"][u[1]]
        cgu = lambda u: pre[u[0]]["cgb"][u[1] // 2]
        gm = {(c, g): _nt(pre[c]["cgb"][g], pre[c]["bgb"][g]) for c in order for g in range(4)}
        y0 = {u: _nt(cgu(u), sp_ref[u[0], u[1]]) for u in units}
        dcg_i = {u: _nn(P(u)["dy0b"], sp_ref[u[0], u[1]]) for u in units}
        dsl = {u: _tn(P(u)["dy0b"], cgu(u)) for u in units}
        w_ = {(u, q): gm[(u[0], u[1] // 2)] * P(u)["decs"][q] for u in units for q in range(2)}
        dw_ = {(u, q): jnp.where(mb, _nt(P(u)["dyq"][q], P(u)["dtxb"]), 0.0) for u in units for q in range(2)}
        ddtx_i = {(u, q): _tn(w_[(u, q)], P(u)["dyq"][q]) for u in units for q in range(2)}
        for c in order:
            rows = slice(c * SC, (c + 1) * SC)
            pc = pre[c]
            lane, arow, dts_c = pc["lane"], pc["arow"], pc["dts"]
            d1 = jnp.zeros((SC, 128), F32)
            d2 = jnp.zeros((SC, 128), F32)
            dz = jnp.zeros((SC, 128), F32)
            ddt = jnp.zeros((SC, 128), F32)
            dtot = jnp.zeros((1, 128), F32)
            dgm = [jnp.zeros((SC, SC), F32) for _ in range(4)]
            dbg = [jnp.zeros((SC, SN), F32) for _ in range(4)]
            dcg = [jnp.zeros((SC, SN), F32) for _ in range(4)]
            for pr in range(npair):
                u, g, p = (c, pr), pr // 2, pc["pairs"][pr]
                hs = head_lanes[pr]
                dso = dst[pr]
                dsob = _mx(dso)
                dxe = _nt(pc["bgb"][g], dsob)
                dbg[g] = dbg[g] + _nn(p["xeb"], dsob)
                ddtx = dxe * p["e2_pair"]
                d2 = d2 + _dot2(dxe * p["dtx"], hs)
                dcg[g] = dcg[g] + dcg_i[u]
                d1 = d1 + _dot2(p["dyp"] * y0[u], hs)
                sprod = dso * sp_ref[c, pr].astype(F32)
                dst[pr] = dso * p["etot_col"] + dsl[u]
                for q in range(2):
                    hm = lo if q == 0 else ~lo
                    col = p["cols"][q]
                    dw = dw_[(u, q)]
                    ddtx = ddtx + jnp.where(hm, ddtx_i[(u, q)], 0.0)
                    dgm[g] = dgm[g] + dw * p["decs"][q]
                    z = dw * w_[(u, q)]
                    dz = dz + _dot2(z, one_lane[col])
                    zc_scr[col:col + 1, :] = _colsum(z)
                    tsum = _rowsum(_colsum(sprod[q * SP:(q + 1) * SP, :]))
                    dtot = jnp.where(lane == col, tsum * p["etot_c"][q], dtot)
                dxs = ddtx * p["dt_pair"]
                ddt = ddt + _dot2(ddtx * p["xs"], hs)
                if last:
                    dxs = dxs + dxp_ref[rows, pr * 128:(pr + 1) * 128]
                dxa_ref[rows, pr * 128:(pr + 1) * 128] = dxs
            e2_all = jnp.exp(pc["tot"] - pc["cum"])
            dcum = dz - zc_scr[...].T + d1 * jnp.exp(pc["cum"]) - d2 * e2_all
            dtot = dtot + _colsum(d2 * e2_all)
            for g in range(4):
                db = dbg[g] + _tn(dgm[g], pc["cgb"][g])
                dc = dcg[g] + _nn(dgm[g], pc["bgb"][g])
                if last:
                    db = db + dxp_ref[rows, D + g * SN:D + (g + 1) * SN]
                    dc = dc + dxp_ref[rows, D + 512 + g * SN:D + 512 + (g + 1) * SN]
                dxa_ref[rows, D + g * SN:D + (g + 1) * SN] = db
                dxa_ref[rows, D + 512 + g * SN:D + 512 + (g + 1) * SN] = dc
            dla = _dot01(mt01, dcum, ways=2) + dtot
            ddt = ddt + dla * arow
            da_ref[0:1, :] += _colsum(dla * dts_c)
            if last:
                ddt = ddt + ddp_ref[rows, :]
            ddts_ref[rows, :] = ddt

    blk = lambda s: _blk(nb - 1 - s, nb, rev)
    in_specs = [pl.BlockSpec((TB, 2048), lambda s: (blk(s), 0)), pl.BlockSpec((TB, 128), lambda s: (blk(s), 0)),
                _full((8, 128)), pl.BlockSpec((nch, npair, 128, SN), lambda s: (blk(s), 0, 0, 0)),
                pl.BlockSpec((TB, D), lambda s: (jnp.minimum(blk(s), nb - 2), 0))]
    args = [xa, dts, alog, sprev, dy]
    if last:
        in_specs += [pl.BlockSpec((TB, 2048), lambda s: (blk(s), 0)), pl.BlockSpec((TB, 128), lambda s: (blk(s), 0))]
        args += list(prev)
    call = dict(
        body=body, args=args, name=f"ssd_bwd_{d}",
        out_shape=(jax.ShapeDtypeStruct((t_total, 2048), F32), jax.ShapeDtypeStruct((t_total, 128), F32),
                   jax.ShapeDtypeStruct((8, 128), F32)),
        grid=(nb,), in_specs=in_specs,
        out_specs=(pl.BlockSpec((TB, 2048), lambda s: (blk(s), 0)), pl.BlockSpec((TB, 128), lambda s: (blk(s), 0)),
                   _full((8, 128))),
        scratch=[pltpu.VMEM((npair, 128, SN), F32), pltpu.VMEM((128, 128), F32)], sem=("arbitrary",), vmem_mb=48)
    return _run(_carry(call, comm, lambda: (pl.program_id(0) == 0, pl.program_id(0) == nb - 1)))


def _readout(o, g, yy, z, vec_ref):
    hg, ss, keep = [], [], []
    for h in range(NH):
        cs = slice(h * HF, (h + 1) * HF)
        oh = o[:, cs]
        r = lax.rsqrt(jnp.mean(oh * oh, axis=1, keepdims=True) + EPS)
        hg.append(oh * r * vec_ref[0:1, cs] * _silu(g[:, cs]))
        keep.append(r)
    u = yy * _silu(z)
    for gi in range(4):
        cs = slice(gi * 256, (gi + 1) * 256)
        ug = u[:, cs]
        r = lax.rsqrt(jnp.mean(ug * ug, axis=1, keepdims=True) + EPS)
        ss.append(ug * r * vec_ref[2:3, cs])
        keep.append(r)
    return jnp.concatenate(hg, axis=1), jnp.concatenate(ss, axis=1), keep, u


def _mix_out(o_f, o_b, p_main, y_f, y_b, xa, x, vecs, w_out):
    n = x.shape[0]

    def body(of_ref, ob_ref, g_ref, z_ref, yf_ref, yb_ref, xs_ref, x_ref, vec_ref, w_ref,
             ymix_ref, ylat_ref, h1_ref, u2_ref):
        o = of_ref[...].astype(F32) + ob_ref[...].astype(F32)
        yy = yf_ref[...].astype(F32) + yb_ref[...].astype(F32) + vec_ref[1:2, :] * xs_ref[...].astype(F32)
        hg, ss, _, _ = _readout(o, g_ref[...].astype(F32), yy, z_ref[...].astype(F32), vec_ref)
        ymix = jnp.concatenate([hg, ss], axis=1).astype(MXU_DTYPE)
        ymix_ref[...] = ymix
        ylat = _nn(ymix, w_ref[...])
        ylat_ref[...] = ylat
        h1 = x_ref[...] + vec_ref[3:4, :] * ylat
        h1_ref[...] = h1
        r = lax.rsqrt(jnp.mean(h1 * h1, axis=1, keepdims=True) + EPS)
        u2_ref[...] = ((h1 * r * vec_ref[6:7, :]) * vec_ref[4:5, :] + vec_ref[5:6, :]).astype(MXU_DTYPE)

    row = lambda j: (lambda i: (i, j))
    return _pcall(
        body, name="mix_out",
        out_shape=(jax.ShapeDtypeStruct((n, 2 * D), MXU_DTYPE), jax.ShapeDtypeStruct((n, D), F32),
                   jax.ShapeDtypeStruct((n, D), F32), jax.ShapeDtypeStruct((n, D), MXU_DTYPE)),
        grid=(n // TB,),
        in_specs=[pl.BlockSpec((TB, D), row(0)), pl.BlockSpec((TB, D), row(0)), pl.BlockSpec((TB, D), row(4)),
                  pl.BlockSpec((TB, D), row(5)), pl.BlockSpec((TB, D), row(0)), pl.BlockSpec((TB, D), row(0)),
                  pl.BlockSpec((TB, D), row(0)), pl.BlockSpec((TB, D), row(0)), _full((8, D)), _full((2 * D, D))],
        out_specs=(pl.BlockSpec((TB, 2 * D), row(0)), pl.BlockSpec((TB, D), row(0)), pl.BlockSpec((TB, D), row(0)),
                   pl.BlockSpec((TB, D), row(0))),
        sem=("parallel",), vmem_mb=48,
    )(o_f, o_b, p_main, p_main, y_f, y_b, xa, x, vecs, w_out)


def _mix_bwd(dylat, o_f, o_b, p_main, y_f, y_b, xa, vecs, w_out, comm=None):
    n = dylat.shape[0]
    t_total = p_main.shape[0]
    nlat = n // TB

    def body(*refs):
        dg_ref, dz_ref, acc_ref = refs[11], refs[13], refs[15]
        i = pl.program_id(0)

        @pl.when(i == 0)
        def _():
            acc_ref[...] = jnp.zeros_like(acc_ref)

        @pl.when(i < nlat)
        def _():
            compute(*refs)

        @pl.when(i == nlat)
        def _():
            dg_ref[...] = jnp.zeros_like(dg_ref)
            dz_ref[...] = jnp.zeros_like(dz_ref)

    def compute(dyl_ref, of_ref, ob_ref, g_ref, z_ref, yf_ref, yb_ref, xs_ref, vec_ref, w_ref,
                do_ref, dg_ref, dys_ref, dz_ref, dxs_ref, acc_ref):
        dymix = _nt(dyl_ref[...], w_ref[...])
        o = of_ref[...].astype(F32) + ob_ref[...].astype(F32)
        g = g_ref[...].astype(F32)
        z = z_ref[...].astype(F32)
        xs = xs_ref[...].astype(F32)
        yy = yf_ref[...].astype(F32) + yb_ref[...].astype(F32) + vec_ref[1:2, :] * xs
        _, _, keep, u = _readout(o, g, yy, z, vec_ref)
        do_l, dg_l = [], []
        for h in range(NH):
            cs = slice(h * HF, (h + 1) * HF)
            oh, gh, r, wv = o[:, cs], g[:, cs], keep[h], vec_ref[0:1, cs]
            dhg = dymix[:, cs]
            xh = oh * r
            dn = dhg * _silu(gh)
            dg_l.append(dhg * xh * wv * _dsilu(gh))
            acc_ref[0:1, cs] += _colsum(dn * xh)
            dxh = dn * wv
            do_l.append(r * (dxh - xh * jnp.mean(dxh * xh, axis=1, keepdims=True)))
        du_l = []
        for gi in range(4):
            cs = slice(gi * 256, (gi + 1) * 256)
            ug, r, wv = u[:, cs], keep[NH + gi], vec_ref[2:3, cs]
            dss = dymix[:, D + gi * 256:D + (gi + 1) * 256]
            xh = ug * r
            acc_ref[2:3, cs] += _colsum(dss * xh)
            dxh = dss * wv
            du_l.append(r * (dxh - xh * jnp.mean(dxh * xh, axis=1, keepdims=True)))
        du = jnp.concatenate(du_l, axis=1)
        dyy = du * _silu(z)
        do_ref[...] = jnp.concatenate(do_l, axis=1).astype(do_ref.dtype)
        dg_ref[...] = jnp.concatenate(dg_l, axis=1).astype(dg_ref.dtype)
        dys_ref[...] = dyy.astype(dys_ref.dtype)
        dz_ref[...] = (du * yy * _dsilu(z)).astype(dz_ref.dtype)
        dxs_ref[...] = (dyy * vec_ref[1:2, :]).astype(dxs_ref.dtype)
        acc_ref[1:2, :] += _colsum(dyy * xs)

    row = lambda j: (lambda i: (jnp.minimum(i, nlat - 1), j))
    lat = pl.BlockSpec((TB, D), row(0))
    tok = pl.BlockSpec((TB, D), lambda i: (i, 0))
    call = dict(
        body=body, args=[dylat, o_f, o_b, p_main, p_main, y_f, y_b, xa, vecs, w_out], name="mix_bwd",
        out_shape=(jax.ShapeDtypeStruct((n, D), MXU_DTYPE), jax.ShapeDtypeStruct((t_total, D), MXU_DTYPE),
                   jax.ShapeDtypeStruct((n, D), MXU_DTYPE), jax.ShapeDtypeStruct((t_total, D), MXU_DTYPE),
                   jax.ShapeDtypeStruct((n, D), MXU_DTYPE), jax.ShapeDtypeStruct((8, D), F32)),
        grid=(t_total // TB,),
        in_specs=[lat, lat, lat, pl.BlockSpec((TB, D), row(4)), pl.BlockSpec((TB, D), row(5)), lat, lat, lat,
                  _full((8, D)), _full((2 * D, D))],
        out_specs=(lat, tok, lat, tok, lat, _full((8, D))), scratch=[],
        sem=("arbitrary",), vmem_mb=48)
    return _run(_carry(call, comm, lambda: (pl.program_id(0) == 0, pl.program_id(0) == t_total // TB - 1)))


def _ffn_up(u2, w_gate, w_up):
    n = u2.shape[0]
    tb = 1024

    def body(u_ref, wg_ref, wu_ref, g_ref, up_ref, a_ref):
        uv = u_ref[...]
        gt = _nt(uv, wg_ref[...])
        upv = _nt(uv, wu_ref[...])
        g_ref[...] = gt.astype(g_ref.dtype)
        up_ref[...] = upv.astype(up_ref.dtype)
        a_ref[...] = (_silu(gt) * upv).astype(a_ref.dtype)

    blk = pl.BlockSpec((tb, FSL), lambda j, i: (i, j))
    wblk = pl.BlockSpec((None, FSL, D), lambda j, i: (j, 0, 0))
    return _pcall(
        body, name="ffn_up",
        out_shape=(jax.ShapeDtypeStruct((n, DFFP), MXU_DTYPE),) * 3,
        grid=(4, n // tb), in_specs=[pl.BlockSpec((tb, D), lambda j, i: (i, 0)), wblk, wblk],
        out_specs=(blk, blk, blk), sem=("parallel", "parallel"), vmem_mb=48,
    )(u2, w_gate, w_up)


def _ffn_down_loss(act, w_down, h1, tgt, vecs):
    n = act.shape[0]
    tb = 512

    def body(a_ref, w_ref, h1_ref, t_ref, vec_ref, dh2_ref, dffn_ref, acc_ref):
        i = pl.program_id(0)

        @pl.when(i == 0)
        def _():
            acc_ref[...] = jnp.zeros_like(acc_ref)

        g2 = vec_ref[0:1, :]
        fw = vec_ref[1:2, :]
        nsub = 4
        sb = tb // nsub
        wv = w_ref[...]
        ffns = [_nn(a_ref[r_ * sb:(r_ + 1) * sb, :], wv) for r_ in range(nsub)]
        for r_ in range(nsub):
            rows = slice(r_ * sb, (r_ + 1) * sb)
            ffn = ffns[r_]
            h2 = h1_ref[rows, :] + g2 * ffn
            r = lax.rsqrt(jnp.mean(h2 * h2, axis=1, keepdims=True) + EPS)
            xh = h2 * r
            err = xh * fw - t_ref[rows, :]
            dy = err * (1.0 / D)
            acc_ref[2:3, :] += _colsum(err * err) * (0.5 / D)
            acc_ref[1:2, :] += _colsum(dy * xh)
            dxh = dy * fw
            dh2 = r * (dxh - xh * jnp.mean(dxh * xh, axis=1, keepdims=True))
            dh2_ref[rows, :] = dh2
            dffn_ref[rows, :] = (g2 * dh2).astype(dffn_ref.dtype)
            acc_ref[0:1, :] += _colsum(dh2 * ffn)

    return _pcall(
        body, name="ffn_down_loss",
        out_shape=(jax.ShapeDtypeStruct((n, D), F32), jax.ShapeDtypeStruct((n, D), MXU_DTYPE),
                   jax.ShapeDtypeStruct((8, D), F32)),
        grid=(n // tb,),
        in_specs=[pl.BlockSpec((tb, DFFP), lambda i: (i, 0)), _full((DFFP, D)), pl.BlockSpec((tb, D), lambda i: (i, 0)),
                  pl.BlockSpec((tb, D), lambda i: (i, 0)), _full((8, D))],
        out_specs=(pl.BlockSpec((tb, D), lambda i: (i, 0)), pl.BlockSpec((tb, D), lambda i: (i, 0)), _full((8, D))),
        sem=("arbitrary",), vmem_mb=48,
    )(act, w_down, h1, tgt, vecs)


def _ffn_bwd(dffn, w_down, gate, up, w_gate_t, w_up_t):
    n = dffn.shape[0]
    tb = 1024

    def body(df_ref, wd_ref, g_ref, up_ref, wg_ref, wu_ref, dg_ref, dup_ref, du_ref):
        j = pl.program_id(1)
        nsub = 4
        sb = tb // nsub
        wd, wg, wu = wd_ref[...], wg_ref[...], wu_ref[...]
        dacts = [_nt(df_ref[r * sb:(r + 1) * sb, :], wd) for r in range(nsub)]
        parts = []
        for r in range(nsub):
            rows = slice(r * sb, (r + 1) * sb)
            gt = g_ref[rows, :].astype(F32)
            upv = up_ref[rows, :].astype(F32)
            sg = _sig(gt)
            dgt = (dacts[r] * upv * (sg * (1.0 + gt * (1.0 - sg)))).astype(MXU_DTYPE)
            dupv = (dacts[r] * (gt * sg)).astype(MXU_DTYPE)
            dg_ref[rows, :] = dgt
            dup_ref[rows, :] = dupv
            parts.append(_nn(dgt, wg) + _nn(dupv, wu))
        part = jnp.concatenate(parts, axis=0)

        @pl.when(j == 0)
        def _():
            du_ref[...] = part

        @pl.when(j > 0)
        def _():
            du_ref[...] += part

    tok = pl.BlockSpec((tb, D), lambda i, j: (i, 0))
    ffb = pl.BlockSpec((tb, FSL), lambda i, j: (i, j))
    wsl = pl.BlockSpec((None, FSL, D), lambda i, j: (j, 0, 0))
    return _pcall(
        body, name="ffn_bwd",
        out_shape=(jax.ShapeDtypeStruct((n, DFFP), MXU_DTYPE), jax.ShapeDtypeStruct((n, DFFP), MXU_DTYPE),
                   jax.ShapeDtypeStruct((n, D), F32)),
        grid=(n // tb, 4),
        in_specs=[tok, pl.BlockSpec((FSL, D), lambda i, j: (j, 0)), ffb, ffb, wsl, wsl],
        out_specs=(ffb, ffb, tok), sem=("parallel", "arbitrary"), vmem_mb=48,
    )(dffn, w_down, gate, up, w_gate_t, w_up_t)


def _ffn_norm_bwd(du, h1, ylat, dh2, vecs):
    n = du.shape[0]
    tb = 512

    def body(du_ref, h1_ref, yl_ref, dh2_ref, vec_ref, dh1_ref, dyl_ref, acc_ref):
        @pl.when(pl.program_id(0) == 0)
        def _():
            acc_ref[...] = jnp.zeros_like(acc_ref)

        duv = du_ref[...]
        h1 = h1_ref[...]
        r = lax.rsqrt(jnp.mean(h1 * h1, axis=1, keepdims=True) + EPS)
        xh = h1 * r
        nw = vec_ref[2:3, :]
        acc_ref[0:1, :] += _colsum(duv)
        acc_ref[1:2, :] += _colsum(duv * xh * nw)
        dn = duv * vec_ref[1:2, :]
        acc_ref[2:3, :] += _colsum(dn * xh)
        dxh = dn * nw
        dh1 = dh2_ref[...] + r * (dxh - xh * jnp.mean(dxh * xh, axis=1, keepdims=True))
        dh1_ref[...] = dh1
        dyl_ref[...] = (vec_ref[0:1, :] * dh1).astype(dyl_ref.dtype)
        acc_ref[3:4, :] += _colsum(dh1 * yl_ref[...])

    tok = pl.BlockSpec((tb, D), lambda i: (i, 0))
    return _pcall(
        body, name="ffn_norm_bwd",
        out_shape=(jax.ShapeDtypeStruct((n, D), F32), jax.ShapeDtypeStruct((n, D), MXU_DTYPE),
                   jax.ShapeDtypeStruct((8, D), F32)),
        grid=(n // tb,), in_specs=[tok, tok, tok, tok, _full((8, D))], out_specs=(tok, tok, _full((8, D))),
        sem=("arbitrary",), vmem_mb=40,
    )(du, h1, ylat, dh2, vecs)


def _deep_rows(rows):
    return max(r for r in range(128, 2305, 128) if rows % r == 0)


def _dw(a, b, name):
    tn_rows = a.shape[0]
    bt = _deep_rows(tn_rows)
    kk, nn_ = a.shape[1], b.shape[1]
    bk = 1024 if kk % 1024 == 0 else kk
    bn = 1024 if nn_ % 1024 == 0 else nn_
    nt = tn_rows // bt

    def body(a_ref, b_ref, o_ref, acc):
        t = pl.program_id(2)
        part = _tn(a_ref[...], b_ref[...])

        @pl.when(t == 0)
        def _():
            acc[...] = part

        @pl.when(t > 0)
        def _():
            acc[...] += part

        @pl.when(t == nt - 1)
        def _():
            o_ref[...] = acc[...].astype(o_ref.dtype)

    return _pcall(
        body, name=name, out_shape=jax.ShapeDtypeStruct((kk, nn_), MXU_DTYPE), grid=(kk // bk, nn_ // bn, nt),
        in_specs=[pl.BlockSpec((bt, bk), lambda i, j, t: (t, i)), pl.BlockSpec((bt, bn), lambda i, j, t: (t, j))],
        out_specs=pl.BlockSpec((bk, bn), lambda i, j, t: (i, j)), scratch=[pltpu.VMEM((bk, bn), F32)],
        sem=("parallel", "parallel", "arbitrary"), vmem_mb=40,
    )(a, b)


def _dw_in(segs, u_all, name):
    tiles = []
    for m, s_ in enumerate(segs):
        tiles += [(m, h) for h in range(s_.shape[1] // D)]
    ntile = len(tiles)
    t_total = u_all.shape[0]
    bt = _deep_rows(t_total)
    nt = t_total // bt

    def body(u_ref, *refs):
        seg_refs, o_ref, acc = refs[:len(segs)], refs[len(segs)], refs[len(segs) + 1]
        n, t = pl.program_id(0), pl.program_id(1)
        for k, (m, _) in enumerate(tiles):
            @pl.when(n == k)
            def _(m=m):
                part = _tn(seg_refs[m][...], u_ref[...])

                @pl.when(t == 0)
                def _():
                    acc[...] = part

                @pl.when(t > 0)
                def _():
                    acc[...] += part

        @pl.when(t == nt - 1)
        def _():
            o_ref[...] = acc[...].astype(o_ref.dtype)

    def seg_spec(m):
        ks = [k for k, (mm, _) in enumerate(tiles) if mm == m]
        lo, hi = ks[0], ks[-1]
        on = lambda n: (n >= lo) & (n <= hi)
        return pl.BlockSpec((bt, D), lambda n, t: (jnp.where(on(n), t, 0), jnp.where(on(n), n - lo, 0)))

    return _pcall(
        body, name=name, out_shape=jax.ShapeDtypeStruct((1, ntile * D, D), MXU_DTYPE), grid=(ntile, nt),
        in_specs=[pl.BlockSpec((bt, D), lambda n, t: (t, 0))] + [seg_spec(m) for m in range(len(segs))],
        out_specs=pl.BlockSpec((None, D, D), lambda n, t: (0, n, 0)),
        scratch=[pltpu.VMEM((D, D), F32)], sem=("parallel", "arbitrary"), vmem_mb=56,
    )(u_all, *segs)


def _du_prenorm_bwd(segs, ddt, wi_main, wi_tail, xin, mods, dres, row_off, tb, name, comm=None):
    n = xin.shape[0]
    nt = n // tb
    off = row_off // tb
    has_dx = dres is not None

    def body(*refs):
        seg_refs = refs[:7]
        ddt_ref, w_ref, wb_ref, wdt_ref, x_ref, mod_ref = refs[7:13]
        rest = refs[13:]
        if has_dx:
            dres_ref, dx_ref, acc_ref, du_scr = rest
        else:
            acc_ref, du_scr = rest
        j, i = pl.program_id(0), pl.program_id(1)
        rows = pl.ds(pl.multiple_of(i * tb, tb), tb)

        @pl.when((i == 0) & (j == 0))
        def _():
            acc_ref[...] = jnp.zeros_like(acc_ref)

        @pl.when(j == 0)
        def _():
            du_scr[rows, :] = _nn(ddt_ref[...], wdt_ref[...])

        for k in range(4):
            if not has_dx and k == 2:
                continue

            @pl.when(j == k)
            def _(k=k):
                if k < 3:
                    sa, sb = seg_refs[2 * k][...], seg_refs[2 * k + 1][...]
                else:
                    sa, sb = seg_refs[6][:, 0:D], seg_refs[6][:, D:2 * D]
                part = _nn(sa, w_ref[0:D, :]) + _nn(sb, w_ref[D:2 * D, :])
                if k > 0:
                    part = part + _nn(sa[:, 0:WTAIL], wb_ref[...])
                du_scr[rows, :] += part

        @pl.when(j == 3)
        def _():
            du = du_scr[rows, :]
            xv = x_ref[...]
            r = lax.rsqrt(jnp.mean(xv * xv, axis=1, keepdims=True) + EPS)
            xh = xv * r
            nw = mod_ref[1:2, :]
            acc_ref[0:1, :] += _colsum(du)
            acc_ref[1:2, :] += _colsum(du * xh * nw)
            dn = du * mod_ref[0:1, :]
            acc_ref[2:3, :] += _colsum(dn * xh)
            if has_dx:
                dxh = dn * nw
                dx_ref[...] = dres_ref[...] + r * (dxh - xh * jnp.mean(dxh * xh, axis=1, keepdims=True))

    def seg_spec(k):
        width = D if k < 6 else 2 * D
        return pl.BlockSpec((tb, width), lambda j, i: (jnp.where(j == min(k // 2, 3), i + off, 0), 0))

    last = pl.BlockSpec((tb, D), lambda j, i: (jnp.where(j == 3, i, 0), 0))
    in_specs = [seg_spec(k) for k in range(7)]
    in_specs += [pl.BlockSpec((tb, 128), lambda j, i: (jnp.where(j == 0, i + off, 0), 0))] + _w_specs()
    in_specs += [last, _full((8, D))]
    args = list(segs) + [ddt, wi_main, wi_tail, wi_tail, xin, mods]
    out_shape = [jax.ShapeDtypeStruct((8, D), F32)]
    out_specs = [_full((8, D))]
    if has_dx:
        in_specs.append(last)
        args.append(dres)
        out_shape.insert(0, jax.ShapeDtypeStruct((n, D), F32))
        out_specs.insert(0, last)
    call = dict(body=body, args=args, name=name, out_shape=tuple(out_shape), grid=(4, nt), in_specs=in_specs,
                out_specs=tuple(out_specs), scratch=[pltpu.VMEM((n, D), F32)], sem=("arbitrary", "arbitrary"),
                vmem_mb=58)
    steps = lambda: ((pl.program_id(0) == 0) & (pl.program_id(1) == 0),
                     (pl.program_id(0) == 3) & (pl.program_id(1) == nt - 1))
    return _run(_carry(call, comm, steps))


def _sum8(v):
    def body(v_ref, o_ref):
        acc = v_ref[0]
        for k in range(1, 8):
            acc = acc + v_ref[k]
        o_ref[...] = acc

    return _pcall(body, name="small_sum", out_shape=jax.ShapeDtypeStruct(v.shape[1:], F32),
                  in_specs=[pl.BlockSpec(memory_space=pltpu.VMEM)], out_specs=pl.BlockSpec(memory_space=pltpu.VMEM))(v)


def _adamw(w, m, v, g, name, comm=None):
    lead = w.ndim == 3
    rows, cols = w.shape[-2:]
    rb = 256 if rows % 256 == 0 else (352 if rows % 352 == 0 else rows)
    c1 = 1.0 - B1 ** STEP
    c2 = 1.0 - B2 ** STEP

    def body(w_ref, m_ref, v_ref, g_ref, d_ref, nm_ref, nv_ref):
        gv = g_ref[...]
        mn = B1 * m_ref[...] + (1.0 - B1) * gv
        vn = B2 * v_ref[...] + (1.0 - B2) * (gv * gv)
        nm_ref[...] = mn
        nv_ref[...] = vn
        d_ref[...] = -LR * ((mn / c1) / (jnp.sqrt(vn / c2) + AEPS) + WD * w_ref[...])

    if rb == rows and rows > 1024:
        cb, steps = 256, cols // 256
        gspec = pl.BlockSpec((rows, cb), lambda i: (0, i))
        spec = pl.BlockSpec((None, rows, cb), lambda i: (0, 0, i)) if lead else gspec
    else:
        steps = rows // rb
        gspec = pl.BlockSpec((rb, cols), lambda i: (i, 0))
        spec = pl.BlockSpec((None, rb, cols), lambda i: (0, i, 0)) if lead else gspec
    call = dict(body=body, args=[w, m, v, g], name=name, out_shape=(jax.ShapeDtypeStruct(w.shape, F32),) * 3,
                grid=(steps,), in_specs=[spec] * 3 + [gspec], out_specs=(spec,) * 3, scratch=[],
                sem=("arbitrary",) if comm is not None else ("parallel",), vmem_mb=40)
    return _run(_carry(call, comm, lambda: (pl.program_id(0) == 0, pl.program_id(0) == steps - 1,
                                            pl.program_id(0) == steps - 1)))


def _rows(v, n):
    f = v.reshape(-1)
    return jnp.pad(f, (0, n * D - f.shape[0])).reshape(n, D)


def kernel(x, c, ctx, c_ctx, w_ada, b_ada, norm_mix, w_in, conv_w, conv_b, ssd_a_log, ssd_dt_bias, ssd_d, ssd_norm, hgrn_lb_raw, hgrn_norm, w_out, norm_ffn, w_gate, w_up, w_down, final_norm, loss_target, m_c_ctx, m_w_ada, m_b_ada, m_norm_mix, m_w_in, m_conv_w, m_conv_b, m_ssd_a_log, m_ssd_dt_bias, m_ssd_d, m_ssd_norm, m_hgrn_lb_raw, m_hgrn_norm, m_w_out, m_norm_ffn, m_w_gate, m_w_up, m_w_down, m_final_norm, v_c_ctx, v_w_ada, v_b_ada, v_norm_mix, v_w_in, v_conv_w, v_conv_b, v_ssd_a_log, v_ssd_dt_bias, v_ssd_d, v_ssd_norm, v_hgrn_lb_raw, v_hgrn_norm, v_w_out, v_norm_ffn, v_w_gate, v_w_up, v_w_down, v_final_norm):
    ix, iy, ic = lax.axis_index("x"), lax.axis_index("y"), lax.axis_index("c")
    chip = 2 * ix + iy
    me = 2 * chip + ic
    xl, xc, tgt = x[0], ctx[0], loss_target[0]
    n_lat, n_ctx = xl.shape[0], xc.shape[0]
    assert n_ctx == TB and n_lat % 1024 == 0
    t_total = n_lat + n_ctx
    nb = t_total // TB

    tr = lambda a: jnp.swapaxes(a, -1, -2)
    shift = [functools.partial(jnp.pad, pad_width=((8 * k, WSL + WTAIL - NSH - 8 * k), (0, 0))) for k in range(4)]
    slab = lax.switch(chip, shift, tr(w_in[0]).astype(MXU_DTYPE))
    padrows = lambda a: jnp.pad(a, ((0, FSL - DFF // 4), (0, 0))).astype(MXU_DTYPE)
    shards = [slab[:WSL], slab[WSL:], w_out[0].astype(MXU_DTYPE), padrows(tr(w_gate[0])), padrows(tr(w_up[0])),
              padrows(w_down[0])]
    own = lambda g_, s_: lax.dynamic_update_slice(g_, s_[None], (chip, 0, 0))
    pack = jnp.concatenate([c, hgrn_lb_raw.reshape(1, D), _rows(conv_w[0], 3), jnp.zeros((3, D), F32)], axis=0)
    ncol_ada = w_ada.shape[2]
    b_shard = lax.dynamic_slice(b_ada, (0, chip * ncol_ada), (1, ncol_ada))
    gath, araw, mod_all, wi_main, wi_tail = _prologue(pack, c_ctx.reshape(1, D), w_ada[0], b_shard, shards[:2])
    wi_main, wi_tail = own(wi_main, shards[0]), own(wi_tail, shards[1])
    gath = gath.reshape(8, 8, D)
    lbraw_full = gath[0::2, 1].reshape(4, 2, 2, 256).transpose(1, 2, 0, 3).reshape(4, D)
    convw_full = gath[0::2, 2:5].reshape(4, 3 * D)[:, :KCONV * 512].reshape(4, KCONV, 512).transpose(1, 0, 2)
    convw_full = convw_full.reshape(KCONV, 2048)
    lbraw8 = jnp.pad(lbraw_full, ((0, 4), (0, 0)))
    convp = jnp.concatenate([convw_full, conv_b, jnp.zeros((2, 2048), F32)], axis=0)
    dtb = jnp.pad(ssd_dt_bias.reshape(1, 32), ((0, 7), (0, 96)))
    alog = jnp.pad(ssd_a_log.reshape(1, 32), ((0, 7), (0, 96)))
    mod_all = mod_all.reshape(8, 16, ncol_ada)[0::2]
    mod_full = mod_all.transpose(1, 0, 2).reshape(16, 4 * ncol_ada)
    my_mod = lax.dynamic_slice(mod_full, (me, 0), (1, 6 * D)).reshape(6, D)
    sh1, sc1, g1, sh2, sc2, g2 = (my_mod[k:k + 1] for k in range(6))
    csh1, csc1 = mod_full[8:9, 0:D], mod_full[8:9, D:2 * D]

    zrow = jnp.zeros((1, D), F32)
    mods_lat = jnp.concatenate([1.0 + sc1, sh1, norm_mix, zrow, zrow, zrow, zrow, zrow], axis=0)
    mods_ctx = jnp.concatenate([1.0 + csc1, csh1, norm_mix, zrow, zrow, zrow, zrow, zrow], axis=0)
    outs = _inproj(xl, mods_lat, wi_main, wi_tail, t_total, 1024, 0, None, "inproj_lat",
                   comm=_comm_gather(shards[2:3]))
    w_out_f = own(outs[3], shards[2]).reshape(2 * D, D)
    p_main, p_dt, u_all = _inproj(xc, mods_ctx, wi_main, wi_tail, t_total, TB, nb - 1, outs[:3], "inproj_ctx")

    o_f, hs_f, o_b, hs_b, wu_g, wd_g = _hgrn_fwd(p_main, lbraw8, nb, comm=_comm_gather(shards[4:]))
    w_down_f = own(wd_g, shards[5]).reshape(DFFP, D)
    wu_g = own(wu_g, shards[4])
    xa, dsl, dts = _ssd_prep(p_main, p_dt, convp, dtb, nb)
    y_f, ss_f, y_b, ss_b, wg_g = _ssd_fwd(xa, dts, alog, nb, comm=_comm_gather(shards[3:4]))
    wg_g = own(wg_g, shards[3])

    vec_mix = jnp.concatenate([jnp.tile(hgrn_norm, (1, NH)), jnp.repeat(ssd_d, SP, axis=1), ssd_norm, g1, 1.0 + sc2,
                               sh2, norm_ffn, zrow], axis=0)
    ymix, ylat, h1, u2 = _mix_out(o_f, o_b, p_main, y_f, y_b, xa, xl, vec_mix, w_out_f)
    gate, up, act = _ffn_up(u2, wg_g, wu_g)
    vec_loss = jnp.concatenate([g2, final_norm.reshape(1, D)] + [zrow] * 6, axis=0)
    dh2, dffn, acc_loss = _ffn_down_loss(act, w_down_f, h1, tgt, vec_loss)

    core_arr = jnp.reshape(ic, (1,)).astype(jnp.int32)
    chip_arr = jnp.reshape(chip, (1,)).astype(jnp.int32)
    every = (0, 4)

    def pair_sum(gs, got, tag):
        return list(_pair_sum(gs, list(got), core_arr, "grads_pair_sum_" + tag))

    vec_ffn = jnp.concatenate([g1, 1.0 + sc2, norm_ffn] + [zrow] * 5, axis=0)
    dgate, dup, du2 = _ffn_bwd(dffn, w_down_f, gate, up, wg_g, wu_g)
    dh1, dylat, acc_ffn = _ffn_norm_bwd(du2, h1, ylat, dh2, vec_ffn)
    gw_down = _dw(act, dffn, "dw_down").reshape(4, FSL, D)
    ga1 = [_dw(dgate, u2, "dw_gate").reshape(4, FSL, D), _dw(dup, u2, "dw_up").reshape(4, FSL, D)]
    res = _mix_bwd(dylat, o_f, o_b, p_main, y_f, y_b, xa, vec_mix, w_out_f, comm=_comm_pair(ga1))
    (do, dgr, dys, dzr, dxs_skip, acc_mix), pair_a1 = res[:6], pair_sum(ga1, res[6:], "a1")
    ga2 = [gw_down, _dw(ymix, dylat, "dw_out").reshape(4, D // 2, D)]

    res = _hgrn_bwd(p_main, lbraw8, hs_f, do, 0, nb, None,
                    comm=[_comm_exchange(pair_a1, [every] * 2), _comm_pair(ga2)])
    (dq0, dff, dv0, dlb_f), recv_a, pair_a2 = res[:4], list(res[4:6]), pair_sum(ga2, res[6:], "a2")
    res = _hgrn_bwd(p_main, lbraw8, hs_b, do, 1, nb, (dq0, dv0), comm=_comm_exchange(pair_a2, [every] * 2))
    (dq, dfb, dv, dlb_b), recv_a = res[:4], recv_a + list(res[4:])
    pair_a, dests_a = pair_a1 + pair_a2, [every] * 4
    gw_in = [_dw_in([dq, dff], u_all, "dw_in_0"), _dw_in([dfb, dv], u_all, "dw_in_1"),
             _dw_in([dgr, dzr], u_all, "dw_in_2")]

    res = _ssd_bwd(xa, dts, alog, ss_f, dys, 0, nb, None, comm=_comm_pair(gw_in))
    (dxa0, ddts0, da_f), pair_b, dests_b = res[:3], pair_sum(gw_in, res[3:], "b"), [(0, 1), (1, 2), (2, 3)]
    res = _ssd_bwd(xa, dts, alog, ss_b, dys, 1, nb, (dxa0, ddts0), comm=_comm_exchange(pair_b, dests_b))
    (dxa, ddts, da_b), recv_b = res[:3], list(res[3:])
    dxbc, ddt, acc_conv, acc_dtb = _ssd_prep_bwd(p_main, p_dt, convp, dtb, dsl, dxa, dxs_skip, ddts, nb)
    gw_in.append(_dw_in([dxbc], u_all, "dw_in_3"))
    gw_in_dt = _dw(ddt, u_all, "dw_in_dt")
    gc = [gw_in[3], jnp.concatenate([g_[:, 0:WTAIL, :] for g_ in gw_in[1:]] + [gw_in_dt[None]], axis=0)]

    segs = [dq, dff, dfb, dv, dgr, dzr, dxbc]
    bmods_lat = jnp.concatenate([1.0 + sc1, norm_mix] + [zrow] * 6, axis=0)
    bmods_ctx = jnp.concatenate([1.0 + csc1, norm_mix] + [zrow] * 6, axis=0)
    res = _du_prenorm_bwd(segs, ddt, wi_main, wi_tail, xc, bmods_ctx, None, n_lat, TB, "du_ctx", comm=_comm_pair(gc))
    acc_ctx, pair_c, dests_c = res[0], pair_sum(gc, res[1:], "c"), [(3, 4), every]
    res = _du_prenorm_bwd(segs, ddt, wi_main, wi_tail, xl, bmods_lat, dh1, 0, 512, "du_lat",
                          comm=_comm_exchange(pair_c, dests_c))
    (grad_x, acc_lat), recv_c = res[:2], list(res[2:])

    mine = _chip_sum(pair_b + pair_c + pair_a, recv_b + recv_c + recv_a, chip_arr, dests_b + dests_c + dests_a,
                     [0, 0, 0, 0, 1, 3, 4, 5, 2])
    dmod_lat = jnp.concatenate([acc_lat[0:2], acc_ffn[3:4], acc_ffn[0:2], acc_loss[0:1]], axis=0)
    misc = jnp.concatenate([(da_f + da_b)[0, :32], jnp.zeros((96,), F32), acc_dtb[0, :32], jnp.zeros((96,), F32),
                            jnp.sum(acc_loss[2]).reshape(1), jnp.zeros((D - 257,), F32)]).reshape(1, D)
    sv = jnp.concatenate([
        dmod_lat, acc_ctx[0:2], (acc_lat[2:3] + acc_ctx[2:3]), acc_ffn[2:3], acc_loss[1:2], acc_mix[2:3],
        acc_mix[0:1], acc_mix[1:2], dlb_f[0:1], dlb_b[0:1], acc_conv[0:6].reshape(12, D), misc,
        jnp.zeros((3, D), F32)], axis=0)
    res = _pair_swap(mine, sv)
    theirs, sv_all = res[:-1], res[-1].reshape(8, 32, D)
    whole = [jnp.concatenate([jnp.where(ic == 0, m_, t_), jnp.where(ic == 0, t_, m_)], axis=0)
             for m_, t_ in zip(mine, theirs)]
    g_w_in = lax.dynamic_slice(jnp.concatenate(whole[0:2], axis=0), (8 * chip, 0), (NSH, D))
    g_w_out = whole[2]
    g_w_gate = whole[3][:DFF // 4]
    g_w_up = whole[4][:DFF // 4]
    g_w_down = whole[5][:DFF // 4]
    ssum = _sum8(sv_all)
    dmod_rows = sv_all[:, 0:6].reshape(8, 6 * D)
    dmod_ctx_row = jnp.concatenate([ssum[6:8].reshape(1, 2 * D), jnp.zeros((1, 4 * D), F32)], axis=1)
    dmod_full = jnp.concatenate([dmod_rows, dmod_ctx_row, jnp.zeros((7, 6 * D), F32)], axis=0)
    grad_b_ada = jnp.sum(dmod_full, axis=0, keepdims=True)
    dmod_shard = lax.dynamic_slice(dmod_full, (0, chip * ncol_ada), (16, ncol_ada))
    g_w_ada, da_part = _ada_bwd(araw, dmod_shard, w_ada[0])
    da_all = _allgather8(da_part, "ada_ctx_gather").reshape(8, 16, D)[0::2, 8]

    big = {}
    for nm, w_, m_, v_, g_ in (("w_ada", w_ada, m_w_ada, v_w_ada, g_w_ada), ("w_in", w_in, m_w_in, v_w_in, g_w_in),
                               ("w_out", w_out, m_w_out, v_w_out, g_w_out),
                               ("w_gate", w_gate, m_w_gate, v_w_gate, g_w_gate),
                               ("w_up", w_up, m_w_up, v_w_up, g_w_up),
                               ("w_down", w_down, m_w_down, v_w_down, g_w_down)):
        if nm in ("w_in", "w_gate", "w_up"):
            big[nm] = tuple(tr(t) for t in (g_[None],) + tuple(_adamw(tr(w_), tr(m_), tr(v_), g_, "adamw_" + nm)))
        else:
            big[nm] = (g_[None],) + tuple(_adamw(w_, m_, v_, g_, "adamw_" + nm))
    cc = c_ctx.reshape(1, D)
    grad_c_ctx = (jnp.sum(da_all, axis=0, keepdims=True) * _dsilu(cc)).reshape(D)

    grad_norm_mix, grad_norm_ffn, grad_final_norm = ssum[8:9], ssum[9:10], ssum[10].reshape(D)
    grad_ssd_norm = ssum[11:12]
    grad_hgrn_norm = jnp.sum(ssum[12].reshape(NH, HF), axis=0, keepdims=True)
    grad_ssd_d = jnp.sum(ssum[13].reshape(SHEADS, SP), axis=1).reshape(1, SHEADS)
    lb_full = _sig(lbraw_full[0:2] - lbraw_full[2:4])
    dr0 = ssum[14:16] * lb_full * (1.0 - lb_full)
    grad_lb_full = jnp.stack([dr0, -dr0], axis=0)
    grad_lb = lax.dynamic_slice(grad_lb_full, (0, 0, chip * 256), (2, 2, 256))
    grad_conv_w = lax.dynamic_slice(ssum[16:26].reshape(KCONV, 2048), (0, chip * 512), (KCONV, 512)).reshape(1, KCONV, 512)
    grad_conv_b = ssum[26:28].reshape(1, 2048)
    a_val = -jnp.exp(ssd_a_log)
    grad_a_log = ssum[28, 0:32].reshape(1, 2, SHEADS) * a_val
    grad_dt_bias = ssum[28, 128:160].reshape(1, 2, SHEADS)
    loss = ssum[28, 256]

    small_w = [c_ctx, b_ada, norm_mix, conv_w, conv_b, ssd_a_log, ssd_dt_bias, ssd_d, ssd_norm, hgrn_lb_raw,
               hgrn_norm, norm_ffn, final_norm]
    small_m = [m_c_ctx, m_b_ada, m_norm_mix, m_conv_w, m_conv_b, m_ssd_a_log, m_ssd_dt_bias, m_ssd_d, m_ssd_norm,
               m_hgrn_lb_raw, m_hgrn_norm, m_norm_ffn, m_final_norm]
    small_v = [v_c_ctx, v_b_ada, v_norm_mix, v_conv_w, v_conv_b, v_ssd_a_log, v_ssd_dt_bias, v_ssd_d, v_ssd_norm,
               v_hgrn_lb_raw, v_hgrn_norm, v_norm_ffn, v_final_norm]
    small_g = [grad_c_ctx, grad_b_ada, grad_norm_mix, grad_conv_w, grad_conv_b, grad_a_log, grad_dt_bias, grad_ssd_d,
               grad_ssd_norm, grad_lb, grad_hgrn_norm, grad_norm_ffn, grad_final_norm]
    nrows = [-(-a.size // D) for a in small_w]
    packs = lambda lst: jnp.concatenate([_rows(a, r) for a, r in zip(lst, nrows)]
                                        + [jnp.zeros((24 - sum(nrows), D), F32)], axis=0)
    sd, sm, svv = _adamw(packs(small_w), packs(small_m), packs(small_v), packs(small_g), "adamw_small")

    def unpack(p):
        out, r0 = [], 0
        for a, r in zip(small_w, nrows):
            out.append(p[r0:r0 + r].reshape(-1)[:a.size].reshape(a.shape))
            r0 += r
        return out

    sd, sm, svv = unpack(sd), unpack(sm), unpack(svv)

    order = ["c_ctx", "w_ada", "b_ada", "norm_mix", "w_in", "conv_w", "conv_b", "ssd_a_log", "ssd_dt_bias", "ssd_d",
             "ssd_norm", "hgrn_lb_raw", "hgrn_norm", "w_out", "norm_ffn", "w_gate", "w_up", "w_down", "final_norm"]
    small_names = ["c_ctx", "b_ada", "norm_mix", "conv_w", "conv_b", "ssd_a_log", "ssd_dt_bias", "ssd_d", "ssd_norm",
                   "hgrn_lb_raw", "hgrn_norm", "norm_ffn", "final_norm"]
    table = dict(big)
    for k, nm in enumerate(small_names):
        table[nm] = (small_g[k].reshape(small_w[k].shape), sd[k], sm[k], svv[k])
    grads = [table[nm][0] for nm in order]
    deltas = [table[nm][1] for nm in order]
    new_m = [table[nm][2] for nm in order]
    new_v = [table[nm][3] for nm in order]
    return (loss, grad_x[None], *grads, *deltas, *new_m, *new_v)
```
